```python
import math
import jax, jax.numpy as jnp
from jax import lax
import numpy as np

D_MODEL = 1024
BATCH = 8
SEQ = 4096
DEPTH = 1

MEM_LEN = 256
ATT_WIDTH = D_MODEL // 2
HEAD_DIM = 64
N_ATT_HEADS = ATT_WIDTH // HEAD_DIM
CONV_WIDTH = D_MODEL - ATT_WIDTH
CONV_K = 3
DILATED_PATTERNS = ((128, 1), (512, 4), (2048, 16))
N_MEM_HEADS = 4
MEM_HEAD_DIM = D_MODEL // N_MEM_HEADS
D_FF = 4 * D_MODEL
NORM_EPS = 1e-6
NEG_INF = -1e30
IN_COLS = 3 * ATT_WIDTH + 3 * CONV_WIDTH

kernel_name = "hybrid_dilated_attn_shortconv_block"


def rms_norm(x, g):
    xf = x.astype(jnp.float32)
    y = xf * lax.rsqrt(jnp.mean(xf * xf, axis=-1, keepdims=True) + NORM_EPS)
    return (y * g.astype(jnp.float32)).astype(x.dtype)


def dilated_window_attention(q, k, v, window, dilation):
    b, s, h, e = q.shape
    steps = window // dilation
    span = steps * dilation
    s_pad = -(-s // span) * span
    nb = s_pad // span
    pad = ((0, 0), (0, s_pad - s), (0, 0), (0, 0))

    def blocks(t):
        return jnp.pad(t, pad).reshape(b, nb, steps, dilation, h, e)

    def with_prev(t):
        prev = jnp.pad(t, ((0, 0), (1, 0), (0, 0), (0, 0), (0, 0), (0, 0)))[:, :-1]
        return jnp.concatenate([prev, t], axis=2)

    qb = blocks(q)
    kk = with_prev(blocks(k))
    vv = with_prev(blocks(v))
    scale = 1.0 / math.sqrt(e)
    scores = jnp.einsum('bnqrhe,bnkrhe->bnrhqk', qb, kk).astype(jnp.float32) * scale
    i = jnp.arange(steps)[:, None]
    j = jnp.arange(2 * steps)[None, :]
    band = (j >= i) & (j <= i + steps)
    has_prev = (jnp.arange(nb) > 0)[:, None, None]
    valid = band[None] & (has_prev | (j[None] >= steps))
    scores = jnp.where(valid[None, :, None, None], scores, NEG_INF)
    lse = jax.nn.logsumexp(scores, axis=-1)
    p = jnp.exp(scores - lse[..., None]).astype(v.dtype)
    o = jnp.einsum('bnrhqk,bnkrhe->bnqrhe', p, vv)
    o = o.reshape(b, s_pad, h, e)[:, :s]
    lse = jnp.transpose(lse, (0, 1, 4, 2, 3)).reshape(b, s_pad, h)[:, :s]
    return o, lse


def short_gated_conv(bg, cg, xc, conv_w):
    u = cg * xc
    up = jnp.pad(u, ((0, 0), (CONV_K - 1, 0), (0, 0)))
    s = u.shape[1]
    conv = sum(up[:, tap:tap + s] * conv_w[tap] for tap in range(CONV_K))
    return bg * conv


def hybrid_mixer(h, w_in, conv_w, g_attn_out, g_conv_out, w_out):
    b, s, _ = h.shape
    proj = h @ w_in
    q, k, v, bg, cg, xc = jnp.split(proj, 6, axis=-1)
    q = q.reshape(b, s, N_ATT_HEADS, HEAD_DIM)
    k = k.reshape(b, s, N_ATT_HEADS, HEAD_DIM)
    v = v.reshape(b, s, N_ATT_HEADS, HEAD_DIM)
    outs, lses = [], []
    for window, dilation in DILATED_PATTERNS:
        o, lse = dilated_window_attention(q, k, v, window, dilation)
        outs.append(o)
        lses.append(lse)
    mix_w = jax.nn.softmax(jnp.stack(lses, axis=0), axis=0)
    attn = jnp.einsum('pbsh,pbshe->bshe', mix_w, jnp.stack(outs, axis=0).astype(jnp.float32))
    attn = attn.astype(h.dtype).reshape(b, s, ATT_WIDTH)
    conv = short_gated_conv(bg, cg, xc, conv_w)
    merged = jnp.concatenate([rms_norm(attn, g_attn_out), rms_norm(conv, g_conv_out)], axis=-1)
    return merged @ w_out


def memory_cross_attention(h, mem_n, w_q_mem, w_kv_mem, w_o_mem):
    b, s, _ = h.shape
    q = (h @ w_q_mem).reshape(b, s, N_MEM_HEADS, MEM_HEAD_DIM)
    kv = mem_n @ w_kv_mem
    k, v = jnp.split(kv, 2, axis=-1)
    k = k.reshape(b, MEM_LEN, N_MEM_HEADS, MEM_HEAD_DIM)
    v = v.reshape(b, MEM_LEN, N_MEM_HEADS, MEM_HEAD_DIM)
    scores = jnp.einsum('bshe,bmhe->bhsm', q, k).astype(jnp.float32) / math.sqrt(MEM_HEAD_DIM)
    p = jax.nn.softmax(scores, axis=-1).astype(v.dtype)
    o = jnp.einsum('bhsm,bmhe->bshe', p, v).reshape(b, s, D_MODEL)
    return o @ w_o_mem


def squared_relu_mlp(h, w_up, w_down):
    a = jax.nn.relu(h @ w_up)
    return (a * a) @ w_down


def _fwd_setup_inputs(seed: int = 0) -> dict:
    key = jax.random.key(seed)
    ks = jax.random.split(key, 17)
    f32 = jnp.float32

    def dense(k, fan_in, shape, gain=1.0):
        return jax.random.normal(k, shape, f32) * (gain * fan_in ** -0.5)

    def gain(k, n):
        return 1.0 + 0.02 * jax.random.normal(k, (n,), f32)

    return {
        "x": jax.random.normal(ks[0], (BATCH, SEQ, D_MODEL), f32),
        "mem": jax.random.normal(ks[1], (BATCH, MEM_LEN, D_MODEL), f32),
        "g_mix": gain(ks[2], D_MODEL),
        "w_in": dense(ks[3], D_MODEL, (D_MODEL, IN_COLS)),
        "conv_w": dense(ks[4], CONV_K, (CONV_K, CONV_WIDTH)),
        "g_attn_out": gain(ks[5], ATT_WIDTH),
        "g_conv_out": gain(ks[6], CONV_WIDTH),
        "w_out": dense(ks[7], D_MODEL, (D_MODEL, D_MODEL), 0.5),
        "g_xattn": gain(ks[8], D_MODEL),
        "g_mem": gain(ks[9], D_MODEL),
        "w_q_mem": dense(ks[10], D_MODEL, (D_MODEL, D_MODEL)),
        "w_kv_mem": dense(ks[11], D_MODEL, (D_MODEL, 2 * D_MODEL)),
        "w_o_mem": dense(ks[12], D_MODEL, (D_MODEL, D_MODEL), 0.5),
        "g_mlp": gain(ks[13], D_MODEL),
        "w_up": dense(ks[14], D_MODEL, (D_MODEL, D_FF)),
        "w_down": dense(ks[15], D_FF, (D_FF, D_MODEL), 0.5),
        "g_final": gain(ks[16], D_MODEL),
    }


def _fwd_reference(x, mem, g_mix, w_in, conv_w, g_attn_out, g_conv_out, w_out,
              g_xattn, g_mem, w_q_mem, w_kv_mem, w_o_mem,
              g_mlp, w_up, w_down, g_final):
    for _ in range(DEPTH):
        x = x + hybrid_mixer(rms_norm(x, g_mix), w_in, conv_w, g_attn_out, g_conv_out, w_out)
        x = x + memory_cross_attention(rms_norm(x, g_xattn), rms_norm(mem, g_mem),
                                       w_q_mem, w_kv_mem, w_o_mem)
        x = x + squared_relu_mlp(rms_norm(x, g_mlp), w_up, w_down)
    return rms_norm(x, g_final)


import jax as _jax
import jax.numpy as _jnp

TWIN_FORMAT = 'train_step'
FWD_PARAMS = ['x', 'mem', 'g_mix', 'w_in', 'conv_w', 'g_attn_out', 'g_conv_out', 'w_out', 'g_xattn', 'g_mem', 'w_q_mem', 'w_kv_mem', 'w_o_mem', 'g_mlp', 'w_up', 'w_down', 'g_final']
TWIN_WEIGHTS = ['g_mix', 'w_in', 'conv_w', 'g_attn_out', 'g_conv_out', 'w_out', 'g_xattn', 'g_mem', 'w_q_mem', 'w_kv_mem', 'w_o_mem', 'g_mlp', 'w_up', 'w_down', 'g_final']
TWIN_DIFF_INPUT = 'x'
TWIN_INPUTS = ['x', 'mem', 'g_mix', 'w_in', 'conv_w', 'g_attn_out', 'g_conv_out', 'w_out', 'g_xattn', 'g_mem', 'w_q_mem', 'w_kv_mem', 'w_o_mem', 'g_mlp', 'w_up', 'w_down', 'g_final', 'loss_target', 'm_g_mix', 'm_w_in', 'm_conv_w', 'm_g_attn_out', 'm_g_conv_out', 'm_w_out', 'm_g_xattn', 'm_g_mem', 'm_w_q_mem', 'm_w_kv_mem', 'm_w_o_mem', 'm_g_mlp', 'm_w_up', 'm_w_down', 'm_g_final', 'v_g_mix', 'v_w_in', 'v_conv_w', 'v_g_attn_out', 'v_g_conv_out', 'v_w_out', 'v_g_xattn', 'v_g_mem', 'v_w_q_mem', 'v_w_kv_mem', 'v_w_o_mem', 'v_g_mlp', 'v_w_up', 'v_w_down', 'v_g_final']
TWIN_OUTPUTS = ['loss', 'grad_x', 'grad_g_mix', 'grad_w_in', 'grad_conv_w', 'grad_g_attn_out', 'grad_g_conv_out', 'grad_w_out', 'grad_g_xattn', 'grad_g_mem', 'grad_w_q_mem', 'grad_w_kv_mem', 'grad_w_o_mem', 'grad_g_mlp', 'grad_w_up', 'grad_w_down', 'grad_g_final', 'delta_g_mix', 'delta_w_in', 'delta_conv_w', 'delta_g_attn_out', 'delta_g_conv_out', 'delta_w_out', 'delta_g_xattn', 'delta_g_mem', 'delta_w_q_mem', 'delta_w_kv_mem', 'delta_w_o_mem', 'delta_g_mlp', 'delta_w_up', 'delta_w_down', 'delta_g_final', 'new_m_g_mix', 'new_m_w_in', 'new_m_conv_w', 'new_m_g_attn_out', 'new_m_g_conv_out', 'new_m_w_out', 'new_m_g_xattn', 'new_m_g_mem', 'new_m_w_q_mem', 'new_m_w_kv_mem', 'new_m_w_o_mem', 'new_m_g_mlp', 'new_m_w_up', 'new_m_w_down', 'new_m_g_final', 'new_v_g_mix', 'new_v_w_in', 'new_v_conv_w', 'new_v_g_attn_out', 'new_v_g_conv_out', 'new_v_w_out', 'new_v_g_xattn', 'new_v_g_mem', 'new_v_w_q_mem', 'new_v_w_kv_mem', 'new_v_w_o_mem', 'new_v_g_mlp', 'new_v_w_up', 'new_v_w_down', 'new_v_g_final']
TWIN_LEAF_KINDS = {'loss': 'loss', 'grad_x': 'grad_x', 'grad_g_mix': 'grad_w', 'grad_w_in': 'grad_w', 'grad_conv_w': 'grad_w', 'grad_g_attn_out': 'grad_w', 'grad_g_conv_out': 'grad_w', 'grad_w_out': 'grad_w', 'grad_g_xattn': 'grad_w', 'grad_g_mem': 'grad_w', 'grad_w_q_mem': 'grad_w', 'grad_w_kv_mem': 'grad_w', 'grad_w_o_mem': 'grad_w', 'grad_g_mlp': 'grad_w', 'grad_w_up': 'grad_w', 'grad_w_down': 'grad_w', 'grad_g_final': 'grad_w', 'delta_g_mix': 'delta_w', 'delta_w_in': 'delta_w', 'delta_conv_w': 'delta_w', 'delta_g_attn_out': 'delta_w', 'delta_g_conv_out': 'delta_w', 'delta_w_out': 'delta_w', 'delta_g_xattn': 'delta_w', 'delta_g_mem': 'delta_w', 'delta_w_q_mem': 'delta_w', 'delta_w_kv_mem': 'delta_w', 'delta_w_o_mem': 'delta_w', 'delta_g_mlp': 'delta_w', 'delta_w_up': 'delta_w', 'delta_w_down': 'delta_w', 'delta_g_final': 'delta_w', 'new_m_g_mix': 'new_m', 'new_m_w_in': 'new_m', 'new_m_conv_w': 'new_m', 'new_m_g_attn_out': 'new_m', 'new_m_g_conv_out': 'new_m', 'new_m_w_out': 'new_m', 'new_m_g_xattn': 'new_m', 'new_m_g_mem': 'new_m', 'new_m_w_q_mem': 'new_m', 'new_m_w_kv_mem': 'new_m', 'new_m_w_o_mem': 'new_m', 'new_m_g_mlp': 'new_m', 'new_m_w_up': 'new_m', 'new_m_w_down': 'new_m', 'new_m_g_final': 'new_m', 'new_v_g_mix': 'new_v', 'new_v_w_in': 'new_v', 'new_v_conv_w': 'new_v', 'new_v_g_attn_out': 'new_v', 'new_v_g_conv_out': 'new_v', 'new_v_w_out': 'new_v', 'new_v_g_xattn': 'new_v', 'new_v_g_mem': 'new_v', 'new_v_w_q_mem': 'new_v', 'new_v_w_kv_mem': 'new_v', 'new_v_w_o_mem': 'new_v', 'new_v_g_mlp': 'new_v', 'new_v_w_up': 'new_v', 'new_v_w_down': 'new_v', 'new_v_g_final': 'new_v'}


def _forward(args):
    return _fwd_reference(*[args[k] for k in FWD_PARAMS])


def _output_shape():
    out = _jax.eval_shape(lambda: _forward(_fwd_setup_inputs(0)))
    return out.shape, out.dtype

N_MICROBATCH = 1
ADAM_LR = 0.001
ADAM_B1 = 0.9
ADAM_B2 = 0.999
ADAM_EPS = 1e-08
ADAM_WD = 0.01
ADAM_STEP = 10
PER_EXAMPLE_BATCH_AXIS = {'x': 0, 'mem': 0, 'loss_target': 0}
SHARED_INPUTS = []
_WEIGHT_DTYPES = {'g_mix': _jnp.float32, 'w_in': _jnp.float32, 'conv_w': _jnp.float32, 'g_attn_out': _jnp.float32, 'g_conv_out': _jnp.float32, 'w_out': _jnp.float32, 'g_xattn': _jnp.float32, 'g_mem': _jnp.float32, 'w_q_mem': _jnp.float32, 'w_kv_mem': _jnp.float32, 'w_o_mem': _jnp.float32, 'g_mlp': _jnp.float32, 'w_up': _jnp.float32, 'w_down': _jnp.float32, 'g_final': _jnp.float32}
MOMENT_SCALE = {'g_mix': 1.491991e-01, 'w_in': 8.336537e-02, 'conv_w': 8.965550e-02, 'g_attn_out': 8.866195e-02, 'g_conv_out': 8.715062e-02, 'w_out': 1.659016e-01, 'g_xattn': 9.692541e-03, 'g_mem': 1.345116e-02, 'w_q_mem': 8.622834e-03, 'w_kv_mem': 8.742893e-03, 'w_o_mem': 1.781127e-02, 'g_mlp': 1.004116e-01, 'w_up': 4.939165e-02, 'w_down': 1.779133e-01, 'g_final': 3.208182e+01}


def _to_microbatches(a, axis):
    t = _jnp.moveaxis(a, axis, 0)
    t = t.reshape((N_MICROBATCH, t.shape[0] // N_MICROBATCH) + t.shape[1:])
    return _jnp.moveaxis(t, 1, axis + 1)


def setup_inputs(seed: int = 0) -> dict:
    inp = _fwd_setup_inputs(seed)
    key = _jax.random.fold_in(_jax.random.key(seed), 7919)
    shape, _ = _output_shape()
    out = dict(inp)
    out["loss_target"] = _jax.random.normal(_jax.random.fold_in(key, 0), shape, _jnp.float32)
    for i, name in enumerate(TWIN_WEIGHTS):
        w = inp[name].astype(_jnp.float32)
        if MOMENT_SCALE is None:
            s = _jnp.sqrt(_jnp.mean(_jnp.square(w)) + 1e-30)
        else:
            s = MOMENT_SCALE[name]
        km, kv = _jax.random.split(_jax.random.fold_in(key, i + 1))
        out[name] = w
        out["m_" + name] = s * _jax.random.normal(km, w.shape, _jnp.float32)
        out["v_" + name] = (s * s) * _jax.random.uniform(kv, w.shape, _jnp.float32, 0.5, 1.5)
    if N_MICROBATCH > 1:
        for name, axis in PER_EXAMPLE_BATCH_AXIS.items():
            out[name] = _to_microbatches(out[name], axis)
    return {'x': out['x'], 'mem': out['mem'], 'g_mix': out['g_mix'], 'w_in': out['w_in'], 'conv_w': out['conv_w'], 'g_attn_out': out['g_attn_out'], 'g_conv_out': out['g_conv_out'], 'w_out': out['w_out'], 'g_xattn': out['g_xattn'], 'g_mem': out['g_mem'], 'w_q_mem': out['w_q_mem'], 'w_kv_mem': out['w_kv_mem'], 'w_o_mem': out['w_o_mem'], 'g_mlp': out['g_mlp'], 'w_up': out['w_up'], 'w_down': out['w_down'], 'g_final': out['g_final'], 'loss_target': out['loss_target'], 'm_g_mix': out['m_g_mix'], 'm_w_in': out['m_w_in'], 'm_conv_w': out['m_conv_w'], 'm_g_attn_out': out['m_g_attn_out'], 'm_g_conv_out': out['m_g_conv_out'], 'm_w_out': out['m_w_out'], 'm_g_xattn': out['m_g_xattn'], 'm_g_mem': out['m_g_mem'], 'm_w_q_mem': out['m_w_q_mem'], 'm_w_kv_mem': out['m_w_kv_mem'], 'm_w_o_mem': out['m_w_o_mem'], 'm_g_mlp': out['m_g_mlp'], 'm_w_up': out['m_w_up'], 'm_w_down': out['m_w_down'], 'm_g_final': out['m_g_final'], 'v_g_mix': out['v_g_mix'], 'v_w_in': out['v_w_in'], 'v_conv_w': out['v_conv_w'], 'v_g_attn_out': out['v_g_attn_out'], 'v_g_conv_out': out['v_g_conv_out'], 'v_w_out': out['v_w_out'], 'v_g_xattn': out['v_g_xattn'], 'v_g_mem': out['v_g_mem'], 'v_w_q_mem': out['v_w_q_mem'], 'v_w_kv_mem': out['v_w_kv_mem'], 'v_w_o_mem': out['v_w_o_mem'], 'v_g_mlp': out['v_g_mlp'], 'v_w_up': out['v_w_up'], 'v_w_down': out['v_w_down'], 'v_g_final': out['v_g_final']}


def _loss(weights, diff, rest, loss_target):
    with _jax.named_scope("forward"):
        args = {**rest, TWIN_DIFF_INPUT: diff, **{k: w.astype(_WEIGHT_DTYPES[k]) for k, w in weights.items()}}
        y = _forward(args)
    with _jax.named_scope("loss_head"):
        err = _jnp.square(y.astype(_jnp.float32) - loss_target)
        return 0.5 * _jnp.sum(_jnp.mean(err, axis=-1)) if err.ndim else 0.5 * err


def _adamw(w, g, m, v):
    m = ADAM_B1 * m + (1.0 - ADAM_B1) * g
    v = ADAM_B2 * v + (1.0 - ADAM_B2) * _jnp.square(g)
    m_hat = m / (1.0 - ADAM_B1 ** ADAM_STEP)
    v_hat = v / (1.0 - ADAM_B2 ** ADAM_STEP)
    delta = -ADAM_LR * (m_hat / (_jnp.sqrt(v_hat) + ADAM_EPS) + ADAM_WD * w)
    return delta, m, v


def reference(x, mem, g_mix, w_in, conv_w, g_attn_out, g_conv_out, w_out, g_xattn, g_mem, w_q_mem, w_kv_mem, w_o_mem, g_mlp, w_up, w_down, g_final, loss_target, m_g_mix, m_w_in, m_conv_w, m_g_attn_out, m_g_conv_out, m_w_out, m_g_xattn, m_g_mem, m_w_q_mem, m_w_kv_mem, m_w_o_mem, m_g_mlp, m_w_up, m_w_down, m_g_final, v_g_mix, v_w_in, v_conv_w, v_g_attn_out, v_g_conv_out, v_w_out, v_g_xattn, v_g_mem, v_w_q_mem, v_w_kv_mem, v_w_o_mem, v_g_mlp, v_w_up, v_w_down, v_g_final):
    given = dict(x=x, mem=mem, g_mix=g_mix, w_in=w_in, conv_w=conv_w, g_attn_out=g_attn_out, g_conv_out=g_conv_out, w_out=w_out, g_xattn=g_xattn, g_mem=g_mem, w_q_mem=w_q_mem, w_kv_mem=w_kv_mem, w_o_mem=w_o_mem, g_mlp=g_mlp, w_up=w_up, w_down=w_down, g_final=g_final, loss_target=loss_target, m_g_mix=m_g_mix, m_w_in=m_w_in, m_conv_w=m_conv_w, m_g_attn_out=m_g_attn_out, m_g_conv_out=m_g_conv_out, m_w_out=m_w_out, m_g_xattn=m_g_xattn, m_g_mem=m_g_mem, m_w_q_mem=m_w_q_mem, m_w_kv_mem=m_w_kv_mem, m_w_o_mem=m_w_o_mem, m_g_mlp=m_g_mlp, m_w_up=m_w_up, m_w_down=m_w_down, m_g_final=m_g_final, v_g_mix=v_g_mix, v_w_in=v_w_in, v_conv_w=v_conv_w, v_g_attn_out=v_g_attn_out, v_g_conv_out=v_g_conv_out, v_w_out=v_w_out, v_g_xattn=v_g_xattn, v_g_mem=v_g_mem, v_w_q_mem=v_w_q_mem, v_w_kv_mem=v_w_kv_mem, v_w_o_mem=v_w_o_mem, v_g_mlp=v_g_mlp, v_w_up=v_w_up, v_w_down=v_w_down, v_g_final=v_g_final)
    weights = {n: given[n] for n in TWIN_WEIGHTS}
    shared = {n: given[n] for n in SHARED_INPUTS}
    per_example = {n: given[n] for n in ['x', 'mem']}
    grad_fn = _jax.value_and_grad(_loss, argnums=(0, 1))

    def one_microbatch(ex, loss_target):
        ex = dict(ex)
        diff = ex.pop(TWIN_DIFF_INPUT)
        return grad_fn(weights, diff, {**shared, **ex}, loss_target)

    if N_MICROBATCH == 1:
        loss, (grad_w, grad_x) = one_microbatch(per_example, given["loss_target"])
    else:
        def body(carry, xs):
            loss_sum, grad_sum = carry
            l_k, (gw_k, gx_k) = one_microbatch(xs[0], xs[1])
            with _jax.named_scope("update"):
                return (loss_sum + l_k, _jax.tree.map(_jnp.add, grad_sum, gw_k)), gx_k

        init = (_jnp.zeros((), _jnp.float32), _jax.tree.map(_jnp.zeros_like, weights))
        (loss, grad_w), grad_x = _jax.lax.scan(body, init, (per_example, given["loss_target"]))
    with _jax.named_scope("update"):
        delta_w, new_m, new_v = {}, {}, {}
        for n in TWIN_WEIGHTS:
            delta_w[n], new_m[n], new_v[n] = _adamw(weights[n], grad_w[n], given["m_" + n], given["v_" + n])
    return (loss, grad_x, *[grad_w[n] for n in TWIN_WEIGHTS], *[delta_w[n] for n in TWIN_WEIGHTS],
            *[new_m[n] for n in TWIN_WEIGHTS], *[new_v[n] for n in TWIN_WEIGHTS])
```

```python
import jax
import jax.numpy as jnp
from jax import lax
from jax.experimental import pallas as pl
from jax.experimental.pallas import tpu as pltpu

F32 = jnp.float32
BF16 = jnp.bfloat16
NORM_EPS = 1e-6
NEG_INF = -1e30
N_DEV = 8
BLK = 128
DILATIONS = (1, 4, 16)
HEAD_DIM = 64
N_MEM_HEADS = 4
ADAM_LR = 0.001
ADAM_B1 = 0.9
ADAM_B2 = 0.999
ADAM_EPS = 1e-08
ADAM_WD = 0.01
ADAM_STEP = 10
MESH = pl.DeviceIdType.MESH
ANY = pl.BlockSpec(memory_space=pl.ANY)


def _dot(a, b):
    return jnp.dot(a, b, preferred_element_type=F32)


def _dot_nt(a, b):
    return lax.dot_general(a, b, (((1,), (1,)), ((), ())), preferred_element_type=F32)


def _dot_tn(a, b):
    return lax.dot_general(a, b, (((0,), (0,)), ((), ())), preferred_element_type=F32)


def _params(semantics, vmem_mb):
    return pltpu.CompilerParams(dimension_semantics=semantics, vmem_limit_bytes=vmem_mb << 20)


def _rms_fwd(x, g):
    r = lax.rsqrt(jnp.mean(x * x, axis=-1, keepdims=True) + NORM_EPS)
    xh = x * r
    return xh * g, xh, r


def _rms_bwd(dy, xh, r, g):
    gy = dy * g
    return r * (gy - xh * jnp.mean(xh * gy, axis=-1, keepdims=True))


def _position():
    x, y, c = lax.axis_index("x"), lax.axis_index("y"), lax.axis_index("c")
    return x, y, c


def _all_gather(shards, name):
    n = len(shards)

    def body(*refs):
        ins, outs = refs[:n], refs[n:2 * n]
        send_sems, recv_sems, local_sems = refs[2 * n:]
        x, y, c = _position()
        me, sibling = (x, y, c), (x, y, 1 - c)
        chips = [(1 - x, y), (x, 1 - y), (1 - x, 1 - y)]

        def lin(px, py, pc):
            return 4 * px + 2 * py + pc

        def copy(a, k, block, to, src=None):
            dst = outs[a].at[lin(*block)]
            return pltpu.make_async_remote_copy(
                src_ref=dst if src is None else src, dst_ref=dst,
                send_sem=send_sems.at[a, k], recv_sem=recv_sems.at[a, k],
                device_id=to, device_id_type=MESH)

        mine = [pltpu.make_async_copy(ins[a], outs[a].at[lin(*me)], local_sems.at[a]) for a in range(n)]
        for cp in mine:
            cp.start()
        first = []
        for a in range(n):
            first.append(copy(a, 0, me, sibling, src=ins[a]))
            first += [copy(a, 1 + j, me, (*chip, c), src=ins[a]) for j, chip in enumerate(chips)]
        for cp in first:
            cp.start()
        passed = []
        for j, chip in enumerate(chips):
            for a in range(n):
                copy(a, 1 + j, (*chip, c), me).wait_recv()
                fwd = copy(a, 4 + j, (*chip, c), sibling)
                fwd.start()
                passed.append(fwd)
        for a in range(n):
            copy(a, 0, sibling, me).wait_recv()
            for j, chip in enumerate(chips):
                copy(a, 4 + j, (*chip, 1 - c), me).wait_recv()
        for cp in first + passed:
            cp.wait_send()
        for cp in mine:
            cp.wait()

    return pl.pallas_call(
        body, name=name,
        out_shape=[jax.ShapeDtypeStruct((N_DEV,) + s.shape, s.dtype) for s in shards],
        in_specs=[ANY] * n, out_specs=[ANY] * n,
        scratch_shapes=[pltpu.SemaphoreType.DMA((n, 7)), pltpu.SemaphoreType.DMA((n, 7)),
                        pltpu.SemaphoreType.DMA((n,))],
    )(*shards)


def _exchange(parts, bcast, name):
    n = len(parts)

    def body(*refs):
        ins, outs = refs[:n], refs[n:2 * n]
        send_sems, recv_sems, local_sems = refs[2 * n:]
        x, y, c = _position()
        me = 4 * x + 2 * y + c
        peers = []
        for k in range(1, N_DEV):
            px = 1 - x if k & 4 else x
            py = 1 - y if k & 2 else y
            pc = 1 - c if k & 1 else c
            peers.append((px, py, pc))

        def src(a, j):
            return ins[a] if bcast[a] else ins[a].at[j]

        local = [pltpu.make_async_copy(src(a, me), outs[a].at[me], local_sems.at[a]) for a in range(n)]
        for cp in local:
            cp.start()
        sends = []
        for a in range(n):
            for k, peer in enumerate(peers):
                plin = 4 * peer[0] + 2 * peer[1] + peer[2]
                cp = pltpu.make_async_remote_copy(
                    src_ref=src(a, plin), dst_ref=outs[a].at[me],
                    send_sem=send_sems.at[a, k], recv_sem=recv_sems.at[a, k],
                    device_id=peer, device_id_type=MESH)
                cp.start()
                sends.append(cp)
        for a in range(n):
            for k, peer in enumerate(peers):
                plin = 4 * peer[0] + 2 * peer[1] + peer[2]
                pltpu.make_async_remote_copy(
                    src_ref=src(a, plin), dst_ref=outs[a].at[plin],
                    send_sem=send_sems.at[a, k], recv_sem=recv_sems.at[a, k],
                    device_id=peer, device_id_type=MESH).wait_recv()
        for cp in sends:
            cp.wait_send()
        for cp in local:
            cp.wait()

    def out_shape(p, b):
        shape = ((N_DEV,) + p.shape) if b else p.shape
        return jax.ShapeDtypeStruct(shape, p.dtype)

    return pl.pallas_call(
        body, name=name,
        out_shape=[out_shape(p, b) for p, b in zip(parts, bcast)],
        in_specs=[ANY] * n, out_specs=[ANY] * n,
        scratch_shapes=[pltpu.SemaphoreType.DMA((n, 7)), pltpu.SemaphoreType.DMA((n, 7)),
                        pltpu.SemaphoreType.DMA((n,))],
    )(*parts)


def _norm_matmul(x, g, w, *, name, out_dtype, tb, blk0=0, nblk=None, relu=False, save_h=False):
    t, d = x.shape
    bn = w.shape[2]
    nblk = w.shape[0] if nblk is None else nblk

    def body(x_ref, g_ref, w_ref, o_ref, *rest):
        h_scr = rest[-1]

        @pl.when(pl.program_id(1) == 0)
        def _():
            h = _rms_fwd(x_ref[...], g_ref[...])[0].astype(BF16)
            h_scr[...] = h
            if save_h:
                rest[0][...] = h

        acc = _dot(h_scr[...], w_ref[0])
        if relu:
            acc = jnp.maximum(acc, 0.0)
        o_ref[...] = acc.astype(out_dtype)

    out_shape = [jax.ShapeDtypeStruct((t, nblk * bn), out_dtype)]
    out_specs = [pl.BlockSpec((tb, bn), lambda i, j: (i, j))]
    if save_h:
        out_shape.append(jax.ShapeDtypeStruct((t, d), BF16))
        out_specs.append(pl.BlockSpec((tb, d), lambda i, j: (i, 0)))
    res = pl.pallas_call(
        body, name=name, grid=(t // tb, nblk),
        in_specs=[pl.BlockSpec((tb, d), lambda i, j: (i, 0)),
                  pl.BlockSpec((1, d), lambda i, j: (0, 0)),
                  pl.BlockSpec((1, d, bn), lambda i, j: (j + blk0, 0, 0))],
        out_specs=out_specs, out_shape=out_shape,
        scratch_shapes=[pltpu.VMEM((tb, d), BF16)],
        compiler_params=_params(("parallel", "arbitrary"), 40),
    )(x, g, w)
    return res if save_h else res[0]


def _matmul_nt_normbwd(dy, w, x, g, dres, *, name, tb):
    t, d = x.shape
    nblk, _, bn = w.shape
    n_i = t // tb
    has_res = dres is not None

    def body(dy_ref, w_ref, x_ref, g_ref, *rest):
        if has_res:
            dres_ref, dx_ref, gg_ref, acc = rest
        else:
            dx_ref, gg_ref, acc = rest
        i, j = pl.program_id(0), pl.program_id(1)

        @pl.when(j == 0)
        def _():
            acc[...] = jnp.zeros_like(acc)

        acc[...] += _dot_nt(dy_ref[...], w_ref[0])

        @pl.when(j == nblk - 1)
        def _():
            g_v = g_ref[...]
            _, xh, r = _rms_fwd(x_ref[...], g_v)
            dh = acc[...]
            dx = _rms_bwd(dh, xh, r, g_v)
            if has_res:
                dx = dx + dres_ref[...]
            dx_ref[...] = dx
            part = jnp.sum(dh * xh, axis=0, keepdims=True)

            @pl.when(i == 0)
            def _():
                gg_ref[...] = part

            @pl.when(i != 0)
            def _():
                gg_ref[...] += part

    in_specs = [pl.BlockSpec((tb, bn), lambda i, j: (i, j)),
                pl.BlockSpec((1, d, bn), lambda i, j: (j, 0, 0)),
                pl.BlockSpec((tb, d), lambda i, j: (i, 0)),
                pl.BlockSpec((1, d), lambda i, j: (0, 0))]
    args = [dy, w, x, g]
    if has_res:
        in_specs.append(pl.BlockSpec((tb, d), lambda i, j: (i, 0)))
        args.append(dres)
    return pl.pallas_call(
        body, name=name, grid=(n_i, nblk),
        in_specs=in_specs,
        out_specs=[pl.BlockSpec((tb, d), lambda i, j: (i, 0)), pl.BlockSpec((1, d), lambda i, j: (0, 0))],
        out_shape=[jax.ShapeDtypeStruct((t, d), F32), jax.ShapeDtypeStruct((1, d), F32)],
        scratch_shapes=[pltpu.VMEM((tb, d), F32)],
        compiler_params=_params(("arbitrary", "arbitrary"), 40),
    )(*args)


def _matmul_tn(a, b, *, name, bm, bn, bt, col_sharded=False, square_a=False):
    t, m = a.shape
    n = b.shape[1]
    n_t = t // bt

    def body(a_ref, b_ref, o_ref, acc):
        k = pl.program_id(2)

        @pl.when(k == 0)
        def _():
            acc[...] = jnp.zeros_like(acc)

        av = a_ref[...]
        if square_a:
            av = av.astype(F32)
            av = av * av
        acc[...] += _dot_tn(av.astype(BF16), b_ref[...].astype(BF16))

        @pl.when(k == n_t - 1)
        def _():
            if col_sharded:
                o_ref[0] = acc[...].astype(BF16)
            else:
                o_ref[...] = acc[...].astype(BF16)

    if col_sharded:
        per = (n // N_DEV) // bn
        out_shape = jax.ShapeDtypeStruct((N_DEV, m, n // N_DEV), BF16)
        out_spec = pl.BlockSpec((1, bm, bn), lambda i, j, k: (j // per, i, j % per))
    else:
        out_shape = jax.ShapeDtypeStruct((m, n), BF16)
        out_spec = pl.BlockSpec((bm, bn), lambda i, j, k: (i, j))
    return pl.pallas_call(
        body, name=name, grid=(m // bm, n // bn, n_t),
        in_specs=[pl.BlockSpec((bt, bm), lambda i, j, k: (k, i)),
                  pl.BlockSpec((bt, bn), lambda i, j, k: (k, j))],
        out_specs=out_spec, out_shape=out_shape,
        scratch_shapes=[pltpu.VMEM((bm, bn), F32)],
        compiler_params=_params(("parallel", "parallel", "arbitrary"), 40),
    )(a, b)


def _attn_masks():
    row = lax.broadcasted_iota(jnp.int32, (BLK, BLK), 0)
    col = lax.broadcasted_iota(jnp.int32, (BLK, BLK), 1)
    lane_lo = col < HEAD_DIM
    head_mask = [jnp.where(lane_lo, 1.0, 0.0).astype(BF16), jnp.where(lane_lo, 0.0, 1.0).astype(BF16)]
    return col <= row, col >= row, lane_lo, head_mask


def _head_bcast(v, h, lane_lo):
    rolled = pltpu.roll(v, HEAD_DIM, axis=1)
    return jnp.where(lane_lo, v, rolled) if h == 0 else jnp.where(lane_lo, rolled, v)


def _attn_fwd(qkv, d, name):
    t = qkv.shape[0]
    nb = t // (BLK * d)
    q3 = qkv.reshape(nb, BLK, d * 1536)

    def body(q_ref, k_ref, v_ref, o_ref, l_ref):
        cur_ok, prev_ok, lane_lo, head_mask = _attn_masks()

        def block(n, has_prev):
            q, kc, vc = q_ref[n], k_ref[n], v_ref[n]
            if has_prev:
                kp, vp = k_ref[n - 1], v_ref[n - 1]
            outs, lses = [], []
            for h in range(2):
                qh = q * head_mask[h]
                sc = jnp.where(cur_ok, _dot_nt(qh, kc) * 0.125, NEG_INF)
                mx = jnp.max(sc, axis=1, keepdims=True)
                if has_prev:
                    sp = jnp.where(prev_ok, _dot_nt(qh, kp) * 0.125, NEG_INF)
                    mx = jnp.maximum(mx, jnp.max(sp, axis=1, keepdims=True))
                ec = jnp.exp(sc - mx)
                den = jnp.sum(ec, axis=1, keepdims=True)
                if has_prev:
                    ep = jnp.exp(sp - mx)
                    den = den + jnp.sum(ep, axis=1, keepdims=True)
                inv = 1.0 / den
                o = _dot((ec * inv).astype(BF16), vc)
                if has_prev:
                    o = o + _dot((ep * inv).astype(BF16), vp)
                outs.append(o)
                lses.append(jnp.broadcast_to(mx + jnp.log(den), (BLK, BLK)))
            o_ref[n] = jnp.where(lane_lo, outs[0], outs[1])
            l_ref[n] = jnp.where(lane_lo, lses[0], lses[1])

        block(0, False)
        if nb > 1:
            def loop(n, carry):
                block(n, True)
                return carry
            lax.fori_loop(1, nb, loop, 0)

    blk = (nb, BLK, BLK)
    o3, l3 = pl.pallas_call(
        body, name=name, grid=(d, 4),
        in_specs=[pl.BlockSpec(blk, lambda r, hp: (0, 0, r * 12 + hp)),
                  pl.BlockSpec(blk, lambda r, hp: (0, 0, r * 12 + 4 + hp)),
                  pl.BlockSpec(blk, lambda r, hp: (0, 0, r * 12 + 8 + hp))],
        out_specs=[pl.BlockSpec(blk, lambda r, hp: (0, 0, r * 4 + hp))] * 2,
        out_shape=[jax.ShapeDtypeStruct((nb, BLK, d * 512), F32)] * 2,
        compiler_params=_params(("parallel", "parallel"), 40),
    )(q3, q3, q3)
    return o3.reshape(t, 512), l3.reshape(t, 512)


def _attn_bwd(qkv, do, lse, dl, d, name):
    t = qkv.shape[0]
    nb = t // (BLK * d)
    q3 = qkv.reshape(nb, BLK, d * 1536)
    do3 = do.reshape(nb, BLK, d * 512)
    l3 = lse.reshape(nb, BLK, d * 512)
    dl3 = dl.reshape(nb, BLK, d * 512)

    def body(q_ref, k_ref, v_ref, do_ref, l_ref, dl_ref, dq_ref, dk_ref, dv_ref):
        cur_ok, prev_ok, lane_lo, head_mask = _attn_masks()

        def block(n, has_prev):
            q, kc, vc, dov = q_ref[n], k_ref[n], v_ref[n], do_ref[n]
            lse_v, dl_v = l_ref[n], dl_ref[n]
            if has_prev:
                kp, vp = k_ref[n - 1], v_ref[n - 1]
            dqs = []
            dkc = dvc = dkp = dvp = None
            for h in range(2):
                qh = q * head_mask[h]
                doh = dov * head_mask[h]
                lse_h = _head_bcast(lse_v, h, lane_lo)
                dl_h = _head_bcast(dl_v, h, lane_lo)

                def grads(k_blk, v_blk, ok):
                    p = jnp.exp(jnp.where(ok, _dot_nt(qh, k_blk) * 0.125, NEG_INF) - lse_h)
                    ds = (p * (_dot_nt(doh, v_blk) - dl_h) * 0.125).astype(BF16)
                    return _dot(ds, k_blk), _dot_tn(ds, qh), _dot_tn(p.astype(BF16), doh)

                dq, dk1, dv1 = grads(kc, vc, cur_ok)
                dkc = dk1 if dkc is None else dkc + dk1
                dvc = dv1 if dvc is None else dvc + dv1
                if has_prev:
                    dq2, dk2, dv2 = grads(kp, vp, prev_ok)
                    dq = dq + dq2
                    dkp = dk2 if dkp is None else dkp + dk2
                    dvp = dv2 if dvp is None else dvp + dv2
                dqs.append(dq)
            dq_ref[n] = jnp.where(lane_lo, dqs[0], dqs[1])
            dk_ref[n] = dkc
            dv_ref[n] = dvc
            if has_prev:
                dk_ref[n - 1] += dkp
                dv_ref[n - 1] += dvp

        block(0, False)
        if nb > 1:
            def loop(n, carry):
                block(n, True)
                return carry
            lax.fori_loop(1, nb, loop, 0)

    blk = (nb, BLK, BLK)
    small = pl.BlockSpec(blk, lambda r, hp: (0, 0, r * 4 + hp))
    outs = pl.pallas_call(
        body, name=name, grid=(d, 4),
        in_specs=[pl.BlockSpec(blk, lambda r, hp: (0, 0, r * 12 + hp)),
                  pl.BlockSpec(blk, lambda r, hp: (0, 0, r * 12 + 4 + hp)),
                  pl.BlockSpec(blk, lambda r, hp: (0, 0, r * 12 + 8 + hp)),
                  small, small, small],
        out_specs=[small] * 3,
        out_shape=[jax.ShapeDtypeStruct((nb, BLK, d * 512), F32)] * 3,
        compiler_params=_params(("parallel", "parallel"), 48),
    )(q3, q3, q3, do3, l3, dl3)
    return [o.reshape(t, 512) for o in outs]


def _mix_weights(l1, l2, l3):
    mx = jnp.maximum(jnp.maximum(l1, l2), l3)
    e1, e2, e3 = jnp.exp(l1 - mx), jnp.exp(l2 - mx), jnp.exp(l3 - mx)
    inv = 1.0 / (e1 + e2 + e3)
    return e1 * inv, e2 * inv, e3 * inv


def _shift_down(u, halo, tb):
    row = lax.broadcasted_iota(jnp.int32, u.shape, 0)
    h6 = jnp.broadcast_to(halo[6:7, :], u.shape)
    h7 = jnp.broadcast_to(halo[7:8, :], u.shape)
    m1 = jnp.where(row == 0, h7, pltpu.roll(u, 1, axis=0))
    m2 = jnp.where(row == 0, h6, jnp.where(row == 1, h7, pltpu.roll(u, 2, axis=0)))
    return m1, m2


def _shift_up(u, halo, tb):
    row = lax.broadcasted_iota(jnp.int32, u.shape, 0)
    h0 = jnp.broadcast_to(halo[0:1, :], u.shape)
    h1 = jnp.broadcast_to(halo[1:2, :], u.shape)
    p1 = jnp.where(row == tb - 1, h0, pltpu.roll(u, tb - 1, axis=0))
    p2 = jnp.where(row == tb - 2, h0, jnp.where(row == tb - 1, h1, pltpu.roll(u, tb - 2, axis=0)))
    return p1, p2


def _conv_fwd(gates, halo, first, cw, tb):
    bg, cg, xc = gates[:, 0:512], gates[:, 512:1024], gates[:, 1024:1536]
    u = cg * xc
    uh = halo[:, 512:1024] * halo[:, 1024:1536]
    uh = jnp.where(first, jnp.zeros_like(uh), uh)
    m1, m2 = _shift_down(u, uh, tb)
    conv = m2 * cw[0:1, :] + m1 * cw[1:2, :] + u * cw[2:3, :]
    return bg, u, m1, m2, conv


def _mixer_fwd(x, o_l, gates, cw, g_a, g_c, w_out, *, tb):
    t, d = x.shape
    hb = tb // 8

    def body(x_ref, o1, o2, o3, l1, l2, l3, gt_ref, gh_ref, cw_ref, ga_ref, gc_ref, wa_ref, wb_ref, x1_ref, mg_ref):
        w1, w2, w3 = _mix_weights(l1[...], l2[...], l3[...])
        attn = w1 * o1[...] + w2 * o2[...] + w3 * o3[...]
        an = _rms_fwd(attn, ga_ref[...])[0].astype(BF16)
        bg, _, _, _, conv = _conv_fwd(gt_ref[...], gh_ref[...], pl.program_id(0) == 0, cw_ref[...], tb)
        cn = _rms_fwd(bg * conv, gc_ref[...])[0].astype(BF16)
        mg_ref[:, 0:512] = an
        mg_ref[:, 512:1024] = cn
        x1_ref[...] = x_ref[...] + _dot(an, wa_ref[...]) + _dot(cn, wb_ref[...])

    tok = lambda c: pl.BlockSpec((tb, c), lambda i: (i, 0))
    const = lambda r, c, i0=0: pl.BlockSpec((r, c), lambda i: (i0, 0))
    return pl.pallas_call(
        body, name="mixer_fwd", grid=(t // tb,),
        in_specs=[tok(d)] + [tok(512)] * 6
                 + [tok(1536), pl.BlockSpec((8, 1536), lambda i: (jnp.maximum(i * hb - 1, 0), 0)),
                    const(3, 512), const(1, 512), const(1, 512), const(512, d), const(512, d, 1)],
        out_specs=[tok(d), tok(d)],
        out_shape=[jax.ShapeDtypeStruct((t, d), F32), jax.ShapeDtypeStruct((t, d), BF16)],
        compiler_params=_params(("parallel",), 48),
    )(x, *o_l, gates, gates, cw, g_a, g_c, w_out, w_out)


def _mixer_bwd(dx1, o_l, gates, cw, g_a, g_c, w_out, head_sum, *, tb):
    t, d = dx1.shape
    hb = tb // 8

    def body(dx_ref, o1, o2, o3, l1, l2, l3, gt_ref, gh_ref, cw_ref, ga_ref, gc_ref, wa_ref, wb_ref, hs_ref,
             do1, do2, do3, dl1, dl2, dl3, dy_ref, gga_ref, ggc_ref):
        i = pl.program_id(0)
        dxb = dx_ref[...].astype(BF16)
        dma = _dot_nt(dxb, wa_ref[...])
        dmc = _dot_nt(dxb, wb_ref[...])
        ws = _mix_weights(l1[...], l2[...], l3[...])
        attn = ws[0] * o1[...] + ws[1] * o2[...] + ws[2] * o3[...]
        g_av = ga_ref[...]
        _, ah, ra = _rms_fwd(attn, g_av)
        dattn = _rms_bwd(dma, ah, ra, g_av)
        z = dattn * attn
        hs = hs_ref[...]
        z1 = z.astype(BF16)
        z2 = (z - z1.astype(F32)).astype(BF16)
        z3 = (z - z1.astype(F32) - z2.astype(F32)).astype(BF16)
        zsum = _dot(z1, hs) + _dot(z2, hs) + _dot(z3, hs)
        for w, do_ref, dl_ref in zip(ws, (do1, do2, do3), (dl1, dl2, dl3)):
            do_ref[...] = (w * dattn).astype(BF16)
            dl_ref[...] = w * zsum
        bg, _, _, _, conv = _conv_fwd(gt_ref[...], gh_ref[...], i == 0, cw_ref[...], tb)
        g_cv = gc_ref[...]
        _, yh, rc = _rms_fwd(bg * conv, g_cv)
        dy_ref[...] = _rms_bwd(dmc, yh, rc, g_cv)
        pa = jnp.sum(dma * ah, axis=0, keepdims=True)
        pc = jnp.sum(dmc * yh, axis=0, keepdims=True)

        @pl.when(i == 0)
        def _():
            gga_ref[...] = pa
            ggc_ref[...] = pc

        @pl.when(i != 0)
        def _():
            gga_ref[...] += pa
            ggc_ref[...] += pc

    tok = lambda c: pl.BlockSpec((tb, c), lambda i: (i, 0))
    const = lambda r, c, i0=0: pl.BlockSpec((r, c), lambda i: (i0, 0))
    return pl.pallas_call(
        body, name="mixer_bwd", grid=(t // tb,),
        in_specs=[tok(d)] + [tok(512)] * 6
                 + [tok(1536), pl.BlockSpec((8, 1536), lambda i: (jnp.maximum(i * hb - 1, 0), 0)),
                    const(3, 512), const(1, 512), const(1, 512), const(512, d), const(512, d, 1), const(512, 512)],
        out_specs=[tok(512)] * 7 + [const(1, 512), const(1, 512)],
        out_shape=[jax.ShapeDtypeStruct((t, 512), BF16)] * 3 + [jax.ShapeDtypeStruct((t, 512), F32)] * 4
                  + [jax.ShapeDtypeStruct((1, 512), F32)] * 2,
        compiler_params=_params(("arbitrary",), 48),
    )(dx1, *o_l, gates, gates, cw, g_a, g_c, w_out, w_out, head_sum)


def _conv_bwd(dy, gates, cw, *, tb):
    t = dy.shape[0]
    hb = tb // 8
    n_i = t // tb

    def body(dy_ref, dyn_ref, gt_ref, gp_ref, gn_ref, cw_ref, dg_ref, gcw_ref):
        i = pl.program_id(0)
        cw_v = cw_ref[...]
        gates_v = gt_ref[...]
        bg, u, m1, m2, conv = _conv_fwd(gates_v, gp_ref[...], i == 0, cw_v, tb)
        dy_v = dy_ref[...]
        dconv = dy_v * bg
        dch = dyn_ref[...] * gn_ref[:, 0:512]
        dch = jnp.where(i == n_i - 1, jnp.zeros_like(dch), dch)
        p1, p2 = _shift_up(dconv, dch, tb)
        du = dconv * cw_v[2:3, :] + p1 * cw_v[1:2, :] + p2 * cw_v[0:1, :]
        dg_ref[:, 0:512] = (dy_v * conv).astype(BF16)
        dg_ref[:, 512:1024] = (du * gates_v[:, 1024:1536]).astype(BF16)
        dg_ref[:, 1024:1536] = (du * gates_v[:, 512:1024]).astype(BF16)
        parts = [jnp.sum(dconv * m2, axis=0, keepdims=True), jnp.sum(dconv * m1, axis=0, keepdims=True),
                 jnp.sum(dconv * u, axis=0, keepdims=True)]

        @pl.when(i == 0)
        def _():
            gcw_ref[...] = jnp.zeros_like(gcw_ref)

        for tap in range(3):
            gcw_ref[tap:tap + 1, :] += parts[tap]

    tok = lambda c: pl.BlockSpec((tb, c), lambda i: (i, 0))
    nxt = lambda c: pl.BlockSpec((8, c), lambda i: (jnp.minimum((i + 1) * hb, t // 8 - 1), 0))
    return pl.pallas_call(
        body, name="conv_bwd", grid=(n_i,),
        in_specs=[tok(512), nxt(512), tok(1536),
                  pl.BlockSpec((8, 1536), lambda i: (jnp.maximum(i * hb - 1, 0), 0)), nxt(1536),
                  pl.BlockSpec((3, 512), lambda i: (0, 0))],
        out_specs=[tok(1536), pl.BlockSpec((8, 512), lambda i: (0, 0))],
        out_shape=[jax.ShapeDtypeStruct((t, 1536), BF16), jax.ShapeDtypeStruct((8, 512), F32)],
        compiler_params=_params(("arbitrary",), 40),
    )(dy, dy, gates, gates, gates, cw)


def _pack_dproj(dqkv, dgates, *, tb):
    t = dgates.shape[0]

    def body(*refs):
        ins, dg_ref, o_ref = refs[:9], refs[9], refs[10]
        for s in range(3):
            o_ref[:, 512 * s:512 * (s + 1)] = (ins[s][...] + ins[3 + s][...] + ins[6 + s][...]).astype(BF16)
        o_ref[:, 1536:3072] = dg_ref[...]

    tok = lambda c: pl.BlockSpec((tb, c), lambda i: (i, 0))
    return pl.pallas_call(
        body, name="pack_dproj", grid=(t // tb,),
        in_specs=[tok(512)] * 9 + [tok(1536)],
        out_specs=tok(3072), out_shape=jax.ShapeDtypeStruct((t, 3072), BF16),
        compiler_params=_params(("parallel",), 40),
    )(*dqkv, dgates)


def _xattn_fwd(x1, g, w_q, kv, w_o, *, tb):
    t, d = x1.shape
    hd = d // N_MEM_HEADS
    m = kv.shape[0]

    def body(x_ref, g_ref, wq_ref, k_ref, v_ref, wo_ref, x2_ref, h_ref, q_ref, o_ref):
        xv = x_ref[...]
        h = _rms_fwd(xv, g_ref[...])[0].astype(BF16)
        h_ref[...] = h
        q = _dot(h, wq_ref[...]).astype(BF16)
        q_ref[...] = q
        for hh in range(N_MEM_HEADS):
            sl = slice(hh * hd, (hh + 1) * hd)
            s = _dot_nt(q[:, sl], k_ref[:, sl]) * (1.0 / 16.0)
            e = jnp.exp(s - jnp.max(s, axis=1, keepdims=True))
            p = e / jnp.sum(e, axis=1, keepdims=True)
            o_ref[:, sl] = _dot(p.astype(BF16), v_ref[:, sl]).astype(BF16)
        x2_ref[...] = xv + _dot(o_ref[...], wo_ref[...])

    tok = pl.BlockSpec((tb, d), lambda i: (i, 0))
    full = pl.BlockSpec((d, d), lambda i: (0, 0))
    return pl.pallas_call(
        body, name="xattn_fwd", grid=(t // tb,),
        in_specs=[tok, pl.BlockSpec((1, d), lambda i: (0, 0)), full,
                  pl.BlockSpec((m, d), lambda i: (0, 0)), pl.BlockSpec((m, d), lambda i: (0, 1)), full],
        out_specs=[tok] * 4,
        out_shape=[jax.ShapeDtypeStruct((t, d), F32)] + [jax.ShapeDtypeStruct((t, d), BF16)] * 3,
        compiler_params=_params(("parallel",), 48),
    )(x1, g, w_q, kv, kv, w_o)


def _xattn_bwd(dx2, x1, g, q, w_q, kv, w_o, *, tb):
    t, d = x1.shape
    hd = d // N_MEM_HEADS
    m = kv.shape[0]

    def body(dx2_ref, x_ref, g_ref, q_ref, wq_ref, k_ref, v_ref, wo_ref, dx1_ref, dq_ref, dk_ref, dv_ref, gg_ref):
        i = pl.program_id(0)

        @pl.when(i == 0)
        def _():
            dk_ref[...] = jnp.zeros_like(dk_ref)
            dv_ref[...] = jnp.zeros_like(dv_ref)

        dx2 = dx2_ref[...]
        do = _dot_nt(dx2.astype(BF16), wo_ref[...]).astype(BF16)
        for hh in range(N_MEM_HEADS):
            sl = slice(hh * hd, (hh + 1) * hd)
            qh, kh, vh, doh = q_ref[:, sl], k_ref[:, sl], v_ref[:, sl], do[:, sl]
            s = _dot_nt(qh, kh) * (1.0 / 16.0)
            e = jnp.exp(s - jnp.max(s, axis=1, keepdims=True))
            p = e / jnp.sum(e, axis=1, keepdims=True)
            dp = _dot_nt(doh, vh)
            ds = (p * (dp - jnp.sum(dp * p, axis=1, keepdims=True)) * (1.0 / 16.0)).astype(BF16)
            dq_ref[:, sl] = _dot(ds, kh).astype(BF16)
            dk_ref[:, sl] += _dot_tn(ds, qh)
            dv_ref[:, sl] += _dot_tn(p.astype(BF16), doh)
        dh = _dot_nt(dq_ref[...], wq_ref[...])
        g_v = g_ref[...]
        _, xh, r = _rms_fwd(x_ref[...], g_v)
        dx1_ref[...] = dx2 + _rms_bwd(dh, xh, r, g_v)
        part = jnp.sum(dh * xh, axis=0, keepdims=True)

        @pl.when(i == 0)
        def _():
            gg_ref[...] = part

        @pl.when(i != 0)
        def _():
            gg_ref[...] += part

    tok = pl.BlockSpec((tb, d), lambda i: (i, 0))
    full = pl.BlockSpec((d, d), lambda i: (0, 0))
    acc = pl.BlockSpec((m, d), lambda i: (0, 0))
    return pl.pallas_call(
        body, name="xattn_bwd", grid=(t // tb,),
        in_specs=[tok, tok, pl.BlockSpec((1, d), lambda i: (0, 0)), tok, full,
                  pl.BlockSpec((m, d), lambda i: (0, 0)), pl.BlockSpec((m, d), lambda i: (0, 1)), full],
        out_specs=[tok, tok, acc, acc, pl.BlockSpec((1, d), lambda i: (0, 0))],
        out_shape=[jax.ShapeDtypeStruct((t, d), F32), jax.ShapeDtypeStruct((t, d), BF16),
                   jax.ShapeDtypeStruct((m, d), F32), jax.ShapeDtypeStruct((m, d), F32),
                   jax.ShapeDtypeStruct((1, d), F32)],
        compiler_params=_params(("arbitrary",), 48),
    )(dx2, x1, g, q, w_q, kv, kv, w_o)


def _mlp_down_loss(a, w_down, x2, tgt, g, *, tb):
    t, d = x2.shape
    f = a.shape[1]

    def body(a_ref, w_ref, x_ref, t_ref, g_ref, dx_ref, loss_ref, gg_ref):
        i = pl.program_id(0)
        av = a_ref[...].astype(F32)
        x3 = x_ref[...] + _dot((av * av).astype(BF16), w_ref[...])
        g_v = g_ref[...]
        out, xh, r = _rms_fwd(x3, g_v)
        err = out - t_ref[...]
        dout = err * (1.0 / d)
        dx_ref[...] = _rms_bwd(dout, xh, r, g_v)
        part = jnp.sum(dout * xh, axis=0, keepdims=True)
        lpart = 0.5 * jnp.sum(jnp.mean(err * err, axis=-1, keepdims=True), axis=0, keepdims=True)
        lpart = jnp.broadcast_to(lpart, loss_ref.shape)

        @pl.when(i == 0)
        def _():
            gg_ref[...] = part
            loss_ref[...] = lpart

        @pl.when(i != 0)
        def _():
            gg_ref[...] += part
            loss_ref[...] += lpart

    tok = pl.BlockSpec((tb, d), lambda i: (i, 0))
    return pl.pallas_call(
        body, name="mlp_down_loss", grid=(t // tb,),
        in_specs=[pl.BlockSpec((tb, f), lambda i: (i, 0)), pl.BlockSpec((f, d), lambda i: (0, 0)), tok, tok,
                  pl.BlockSpec((1, d), lambda i: (0, 0))],
        out_specs=[tok, pl.BlockSpec((8, 128), lambda i: (0, 0)), pl.BlockSpec((1, d), lambda i: (0, 0))],
        out_shape=[jax.ShapeDtypeStruct((t, d), F32), jax.ShapeDtypeStruct((8, 128), F32),
                   jax.ShapeDtypeStruct((1, d), F32)],
        compiler_params=_params(("arbitrary",), 56),
    )(a, w_down, x2, tgt, g)


def _mlp_dpre(dx3, w_down, a, *, tb):
    t, d = dx3.shape
    nblk, bn, _ = w_down.shape

    def body(dx_ref, w_ref, a_ref, o_ref, dxb):
        @pl.when(pl.program_id(1) == 0)
        def _():
            dxb[...] = dx_ref[...].astype(BF16)

        o_ref[...] = (2.0 * a_ref[...].astype(F32) * _dot_nt(dxb[...], w_ref[0])).astype(BF16)

    return pl.pallas_call(
        body, name="mlp_dpre", grid=(t // tb, nblk),
        in_specs=[pl.BlockSpec((tb, d), lambda i, j: (i, 0)), pl.BlockSpec((1, bn, d), lambda i, j: (j, 0, 0)),
                  pl.BlockSpec((tb, bn), lambda i, j: (i, j))],
        out_specs=pl.BlockSpec((tb, bn), lambda i, j: (i, j)),
        out_shape=jax.ShapeDtypeStruct((t, nblk * bn), BF16),
        scratch_shapes=[pltpu.VMEM((tb, d), BF16)],
        compiler_params=_params(("parallel", "arbitrary"), 40),
    )(dx3, w_down, a)


def _adamw(gsum, w, m, v):
    m_new = ADAM_B1 * m + (1.0 - ADAM_B1) * gsum
    v_new = ADAM_B2 * v + (1.0 - ADAM_B2) * (gsum * gsum)
    m_hat = m_new / (1.0 - ADAM_B1 ** ADAM_STEP)
    v_hat = v_new / (1.0 - ADAM_B2 ** ADAM_STEP)
    delta = -ADAM_LR * (m_hat / (jnp.sqrt(v_hat) + ADAM_EPS) + ADAM_WD * w)
    return delta, m_new, v_new


def _sum_adamw(parts, w, m, v, *, name, tr):
    r, c = w.shape

    def body(p_ref, w_ref, m_ref, v_ref, g_ref, d_ref, mo_ref, vo_ref):
        g = p_ref[0].astype(F32)
        for k in range(1, N_DEV):
            g = g + p_ref[k].astype(F32)
        g_ref[...] = g
        d_ref[...], mo_ref[...], vo_ref[...] = _adamw(g, w_ref[...], m_ref[...], v_ref[...])

    blk = pl.BlockSpec((tr, c), lambda i: (i, 0))
    return pl.pallas_call(
        body, name=name, grid=(r // tr,),
        in_specs=[pl.BlockSpec((N_DEV, tr, c), lambda i: (0, i, 0)), blk, blk, blk],
        out_specs=[blk] * 4, out_shape=[jax.ShapeDtypeStruct((r, c), F32)] * 4,
        compiler_params=_params(("parallel",), 40),
    )(parts, w, m, v)


def _sum_small(parts):
    _, r, c = parts.shape

    def body(p_ref, o_ref):
        s = p_ref[0]
        for k in range(1, N_DEV):
            s = s + p_ref[k]
        o_ref[...] = s

    return pl.pallas_call(body, name="sum_small", out_shape=jax.ShapeDtypeStruct((r, c), F32))(parts)


def _adamw_small(g, w, m, v):
    def body(g_ref, w_ref, m_ref, v_ref, d_ref, mo_ref, vo_ref):
        d_ref[...], mo_ref[...], vo_ref[...] = _adamw(g_ref[...], w_ref[...], m_ref[...], v_ref[...])

    return pl.pallas_call(body, name="adamw_small", out_shape=[jax.ShapeDtypeStruct(g.shape, F32)] * 3)(g, w, m, v)


def _head_sum_matrix():
    r = lax.broadcasted_iota(jnp.int32, (512, 512), 0) // HEAD_DIM
    c = lax.broadcasted_iota(jnp.int32, (512, 512), 1) // HEAD_DIM
    return (r == c).astype(BF16)


def _local_step(x, mem, tgt, gains, wg, cw):
    t, d = x.shape
    w_in, w_up = wg["w_in"], wg["w_up"]

    qkv, h1 = _norm_matmul(x, gains["g_mix"], w_in, name="proj_qkv", out_dtype=BF16, tb=512, blk0=0, nblk=4,
                           save_h=True)
    gates = _norm_matmul(x, gains["g_mix"], w_in, name="proj_gates", out_dtype=F32, tb=512, blk0=4, nblk=4)
    o_l = []
    for dil in DILATIONS:
        o_l.append(_attn_fwd(qkv, dil, name=f"attn_fwd_d{dil}"))
    o_l = [p[0] for p in o_l] + [p[1] for p in o_l]
    x1, merged = _mixer_fwd(x, o_l, gates, cw, gains["g_attn_out"], gains["g_conv_out"], wg["w_out"], tb=256)
    kv, mem_n = _norm_matmul(mem, gains["g_mem"], wg["w_kv"], name="mem_kv", out_dtype=BF16, tb=mem.shape[0],
                             save_h=True)
    x2, h2, qm, om = _xattn_fwd(x1, gains["g_xattn"], wg["w_q"], kv, wg["w_o"], tb=256)
    a, h3 = _norm_matmul(x2, gains["g_mlp"], w_up, name="mlp_up", out_dtype=BF16, tb=512, relu=True, save_h=True)
    dx3, loss_blk, gg_final = _mlp_down_loss(a, wg["w_down"], x2, tgt, gains["g_final"], tb=256)

    w_down_blocks = wg["w_down"].reshape(N_DEV, -1, d)
    dpre = _mlp_dpre(dx3, w_down_blocks, a, tb=512)
    gw_down = _matmul_tn(a, dx3, name="grad_w_down", bm=512, bn=512, bt=1024, square_a=True)
    gw_up = _matmul_tn(h3, dpre, name="grad_w_up", bm=512, bn=512, bt=1024, col_sharded=True)
    dx2, gg_mlp = _matmul_nt_normbwd(dpre, w_up, x2, gains["g_mlp"], dx3, name="mlp_dx", tb=512)

    dx1, dqm, dk, dv, gg_xattn = _xattn_bwd(dx2, x1, gains["g_xattn"], qm, wg["w_q"], kv, wg["w_o"], tb=256)
    gw_o = _matmul_tn(om, dx2, name="grad_w_o", bm=512, bn=512, bt=1024)
    gw_q = _matmul_tn(h2, dqm, name="grad_w_q", bm=512, bn=512, bt=1024)
    dkv = jnp.concatenate([dk, dv], axis=1).astype(BF16)
    gw_kv = _matmul_tn(mem_n, dkv, name="grad_w_kv", bm=512, bn=256, bt=mem.shape[0], col_sharded=True)
    _, gg_mem = _matmul_nt_normbwd(dkv, wg["w_kv"], mem, gains["g_mem"], None, name="mem_dx", tb=mem.shape[0])

    gw_out = _matmul_tn(merged, dx1, name="grad_w_out", bm=512, bn=512, bt=1024)
    mb = _mixer_bwd(dx1, o_l, gates, cw, gains["g_attn_out"], gains["g_conv_out"], wg["w_out"], _head_sum_matrix(),
                    tb=256)
    do_l, dl_l, dy, gg_attn, gg_conv = mb[0:3], mb[3:6], mb[6], mb[7], mb[8]
    dgates, gcw = _conv_bwd(dy, gates, cw, tb=256)
    dqkv = []
    for p, dil in enumerate(DILATIONS):
        dqkv += _attn_bwd(qkv, do_l[p], o_l[3 + p], dl_l[p], dil, name=f"attn_bwd_d{dil}")
    dproj = _pack_dproj(dqkv, dgates, tb=256)
    gw_in = _matmul_tn(h1, dproj, name="grad_w_in", bm=512, bn=384, bt=1024, col_sharded=True)
    grad_x, gg_mix = _matmul_nt_normbwd(dproj, w_in, x, gains["g_mix"], dx1, name="mixer_dx", tb=512)

    grads = dict(w_in=gw_in, w_out=gw_out, w_q=gw_q, w_kv=gw_kv, w_o=gw_o, w_up=gw_up, w_down=gw_down)
    small = dict(g_mix=gg_mix, g_attn_out=gg_attn, g_conv_out=gg_conv, g_xattn=gg_xattn, g_mem=gg_mem,
                 g_mlp=gg_mlp, g_final=gg_final, conv_w=gcw[0:3], loss=loss_blk[0:1, 0:1])
    return grad_x, grads, small


_BIG = ("w_in", "w_out", "w_q", "w_kv", "w_o", "w_up", "w_down")
_GAIN_ROWS = ("g_mix", "g_xattn", "g_mem", "g_mlp", "g_final")


def _pack_small(vals, conv):
    rows = [vals[k].reshape(1, -1) for k in _GAIN_ROWS]
    rows.append(jnp.concatenate([vals["g_attn_out"].reshape(1, -1), vals["g_conv_out"].reshape(1, -1)], axis=1))
    flat = conv.reshape(1, -1)
    rows.append(jnp.pad(flat, ((0, 0), (0, 1024 - flat.shape[1]))))
    rows.append(jnp.zeros((1, 1024), F32))
    return jnp.concatenate(rows, axis=0)


def kernel(x, mem, g_mix, w_in, conv_w, g_attn_out, g_conv_out, w_out, g_xattn, g_mem, w_q_mem, w_kv_mem, w_o_mem, g_mlp, w_up, w_down, g_final, loss_target, m_g_mix, m_w_in, m_conv_w, m_g_attn_out, m_g_conv_out, m_w_out, m_g_xattn, m_g_mem, m_w_q_mem, m_w_kv_mem, m_w_o_mem, m_g_mlp, m_w_up, m_w_down, m_g_final, v_g_mix, v_w_in, v_conv_w, v_g_attn_out, v_g_conv_out, v_w_out, v_g_xattn, v_g_mem, v_w_q_mem, v_w_kv_mem, v_w_o_mem, v_g_mlp, v_w_up, v_w_down, v_g_final):
    d = x.shape[-1]
    me = 4 * lax.axis_index("x") + 2 * lax.axis_index("y") + lax.axis_index("c")
    w_shards = dict(w_in=w_in, w_out=w_out, w_q=w_q_mem, w_kv=w_kv_mem, w_o=w_o_mem, w_up=w_up, w_down=w_down)
    m_shards = dict(w_in=m_w_in, w_out=m_w_out, w_q=m_w_q_mem, w_kv=m_w_kv_mem, w_o=m_w_o_mem, w_up=m_w_up,
                    w_down=m_w_down)
    v_shards = dict(w_in=v_w_in, w_out=v_w_out, w_q=v_w_q_mem, w_kv=v_w_kv_mem, w_o=v_w_o_mem, w_up=v_w_up,
                    w_down=v_w_down)
    gains = dict(g_mix=g_mix, g_attn_out=g_attn_out, g_conv_out=g_conv_out, g_xattn=g_xattn, g_mem=g_mem,
                 g_mlp=g_mlp, g_final=g_final)
    gains2 = {k: v.reshape(1, -1) for k, v in gains.items()}

    gathered = _all_gather([w_shards[k].astype(BF16) for k in _BIG] + [conv_w], name="all_gather_weights")
    wg = dict(zip(_BIG, gathered[:-1]))
    for k in ("w_out", "w_q", "w_o", "w_down"):
        wg[k] = wg[k].reshape(-1, d)
    cw = gathered[-1].transpose(1, 0, 2).reshape(3, -1)

    grad_x, grads, small = _local_step(x[0], mem[0], loss_target[0], gains2, wg, cw)

    for k in ("w_out", "w_q", "w_o", "w_down"):
        grads[k] = grads[k].reshape(N_DEV, -1, d)
    small_rows = [small[k] for k in _GAIN_ROWS]
    small_rows.append(jnp.concatenate([small["g_attn_out"], small["g_conv_out"]], axis=1))
    small_rows.append(jnp.pad(small["conv_w"], ((0, 0), (0, 512))))
    small_rows.append(jnp.pad(small["loss"], ((0, 6), (0, 1023))))
    small_part = jnp.concatenate(small_rows, axis=0)
    received = _exchange([grads[k] for k in _BIG] + [small_part], [False] * len(_BIG) + [True],
                         name="exchange_grads")

    outs = {}
    tiles = dict(w_in=256, w_out=128, w_q=128, w_kv=256, w_o=128, w_up=256, w_down=256)
    for k, rec in zip(_BIG, received[:-1]):
        outs[k] = _sum_adamw(rec, w_shards[k], m_shards[k], v_shards[k], name=f"adamw_{k}", tr=tiles[k])

    ssum = _sum_small(received[-1])
    loss = ssum[9, 0]
    g_small = {k: ssum[i] for i, k in enumerate(_GAIN_ROWS)}
    g_small["g_attn_out"] = ssum[5, 0:512]
    g_small["g_conv_out"] = ssum[5, 512:1024]
    g_conv = lax.dynamic_slice_in_dim(ssum[6:9, 0:512], me * 64, 64, axis=1)
    m_small = dict(g_mix=m_g_mix, g_attn_out=m_g_attn_out, g_conv_out=m_g_conv_out, g_xattn=m_g_xattn,
                   g_mem=m_g_mem, g_mlp=m_g_mlp, g_final=m_g_final)
    v_small = dict(g_mix=v_g_mix, g_attn_out=v_g_attn_out, g_conv_out=v_g_conv_out, g_xattn=v_g_xattn,
                   g_mem=v_g_mem, g_mlp=v_g_mlp, g_final=v_g_final)
    packed = [_pack_small(g_small, g_conv), _pack_small(gains, conv_w), _pack_small(m_small, m_conv_w),
              _pack_small(v_small, v_conv_w)]
    upd = _adamw_small(*packed)

    def unpack(p):
        res = {k: p[i] for i, k in enumerate(_GAIN_ROWS)}
        res["g_attn_out"] = p[5, 0:512]
        res["g_conv_out"] = p[5, 512:1024]
        res["conv_w"] = p[6, 0:192].reshape(3, 64)
        return res

    g_small["conv_w"] = g_conv
    small_out = [g_small] + [unpack(p) for p in upd]
    names = {"g_mix": "g_mix", "w_in": "w_in", "conv_w": "conv_w", "g_attn_out": "g_attn_out",
             "g_conv_out": "g_conv_out", "w_out": "w_out", "g_xattn": "g_xattn", "g_mem": "g_mem",
             "w_q_mem": "w_q", "w_kv_mem": "w_kv", "w_o_mem": "w_o", "g_mlp": "g_mlp", "w_up": "w_up",
             "w_down": "w_down", "g_final": "g_final"}
    result = [loss, grad_x[None]]
    for which in range(4):
        for key in names.values():
            result.append(outs[key][which] if key in outs else small_out[which][key])
    return tuple(result)
```

```python
import math

import jax
import jax.numpy as jnp
from jax import lax
from jax.experimental import pallas as pl
from jax.experimental.pallas import tpu as pltpu

F32 = jnp.float32
BF16 = jnp.bfloat16
NORM_EPS = 1e-6
NEG_INF = -1e30
N_DEV = 8
BLK = 128
DILATIONS = (1, 4, 16)
HEAD_DIM = 64
N_MEM_HEADS = 4
ADAM_LR = 0.001
ADAM_B1 = 0.9
ADAM_B2 = 0.999
ADAM_EPS = 1e-08
ADAM_WD = 0.01
ADAM_STEP = 10
MESH = pl.DeviceIdType.MESH
ANY = pl.BlockSpec(memory_space=pl.ANY)


def _dot(a, b):
    return jnp.dot(a, b, preferred_element_type=F32)


def _dot_nt(a, b):
    return lax.dot_general(a, b, (((1,), (1,)), ((), ())), preferred_element_type=F32)


def _dot_tn(a, b):
    return lax.dot_general(a, b, (((0,), (0,)), ((), ())), preferred_element_type=F32)


def _params(semantics, vmem_mb):
    return pltpu.CompilerParams(dimension_semantics=semantics, vmem_limit_bytes=vmem_mb << 20)


def _rms_fwd(x, g):
    r = lax.rsqrt(jnp.mean(x * x, axis=-1, keepdims=True) + NORM_EPS)
    xh = x * r
    return xh * g, xh, r


def _rms_bwd(dy, xh, r, g):
    gy = dy * g
    return r * (gy - xh * jnp.mean(xh * gy, axis=-1, keepdims=True))


def _position():
    x, y, c = lax.axis_index("x"), lax.axis_index("y"), lax.axis_index("c")
    return x, y, c


class _Gather:
    has_mid = True

    def __init__(self, shards):
        self.arrays = list(shards)
        self.n = len(self.arrays)

    def out_shape(self):
        return [jax.ShapeDtypeStruct((N_DEV,) + s.shape, s.dtype) for s in self.arrays]

    def scratch(self):
        return [pltpu.SemaphoreType.DMA((self.n, 7)), pltpu.SemaphoreType.DMA((self.n, 7)),
                pltpu.SemaphoreType.DMA((self.n,))]

    def _ctx(self, ins, outs, sems):
        send_sems, recv_sems, local_sems = sems
        x, y, c = _position()
        me, sibling = (x, y, c), (x, y, 1 - c)
        chips = [(1 - x, y), (x, 1 - y), (1 - x, 1 - y)]

        def lin(px, py, pc):
            return 4 * px + 2 * py + pc

        def copy(a, k, block, to, src=None):
            dst = outs[a].at[lin(*block)]
            return pltpu.make_async_remote_copy(
                src_ref=dst if src is None else src, dst_ref=dst,
                send_sem=send_sems.at[a, k], recv_sem=recv_sems.at[a, k],
                device_id=to, device_id_type=MESH)

        mine = [pltpu.make_async_copy(ins[a], outs[a].at[lin(*me)], local_sems.at[a]) for a in range(self.n)]
        first = []
        for a in range(self.n):
            first.append(copy(a, 0, me, sibling, src=ins[a]))
            first += [copy(a, 1 + j, me, (*chip, c), src=ins[a]) for j, chip in enumerate(chips)]
        return c, me, sibling, chips, copy, mine, first

    def start(self, ins, outs, sems):
        _, _, _, _, _, mine, first = self._ctx(ins, outs, sems)
        for cp in mine + first:
            cp.start()

    def mid(self, ins, outs, sems):
        c, me, sibling, chips, copy, _, _ = self._ctx(ins, outs, sems)
        for j, chip in enumerate(chips):
            for a in range(self.n):
                copy(a, 1 + j, (*chip, c), me).wait_recv()
                copy(a, 4 + j, (*chip, c), sibling).start()

    def finish(self, ins, outs, sems):
        c, me, sibling, chips, copy, mine, first = self._ctx(ins, outs, sems)
        for a in range(self.n):
            copy(a, 0, sibling, me).wait_recv()
            for j, chip in enumerate(chips):
                copy(a, 4 + j, (*chip, 1 - c), me).wait_recv()
        for cp in first:
            cp.wait_send()
        for j, chip in enumerate(chips):
            for a in range(self.n):
                copy(a, 4 + j, (*chip, c), sibling).wait_send()
        for cp in mine:
            cp.wait()


class _Exchange:
    has_mid = False

    def __init__(self, parts, bcast=None):
        self.arrays = list(parts)
        self.n = len(self.arrays)
        self.bcast = [False] * self.n if bcast is None else list(bcast)

    def out_shape(self):
        return [jax.ShapeDtypeStruct(((N_DEV,) + p.shape) if b else p.shape, p.dtype)
                for p, b in zip(self.arrays, self.bcast)]

    def scratch(self):
        return [pltpu.SemaphoreType.DMA((self.n, 7)), pltpu.SemaphoreType.DMA((self.n, 7)),
                pltpu.SemaphoreType.DMA((self.n,))]

    def _ctx(self, ins, outs, sems):
        send_sems, recv_sems, local_sems = sems
        x, y, c = _position()
        me = 4 * x + 2 * y + c

        def src(a, j):
            return ins[a] if self.bcast[a] else ins[a].at[j]

        local = [pltpu.make_async_copy(src(a, me), outs[a].at[me], local_sems.at[a]) for a in range(self.n)]
        sends, recvs = [], []
        for a in range(self.n):
            for k in range(1, N_DEV):
                peer = (1 - x if k & 4 else x, 1 - y if k & 2 else y, 1 - c if k & 1 else c)
                plin = 4 * peer[0] + 2 * peer[1] + peer[2]
                sends.append(pltpu.make_async_remote_copy(
                    src_ref=src(a, plin), dst_ref=outs[a].at[me],
                    send_sem=send_sems.at[a, k - 1], recv_sem=recv_sems.at[a, k - 1],
                    device_id=peer, device_id_type=MESH))
                recvs.append(pltpu.make_async_remote_copy(
                    src_ref=src(a, plin), dst_ref=outs[a].at[plin],
                    send_sem=send_sems.at[a, k - 1], recv_sem=recv_sems.at[a, k - 1],
                    device_id=peer, device_id_type=MESH))
        return local, sends, recvs

    def start(self, ins, outs, sems):
        local, sends, _ = self._ctx(ins, outs, sems)
        for cp in local + sends:
            cp.start()

    def finish(self, ins, outs, sems):
        local, sends, recvs = self._ctx(ins, outs, sems)
        for cp in recvs:
            cp.wait_recv()
        for cp in sends:
            cp.wait_send()
        for cp in local:
            cp.wait()


def _comm_call(rider, name):
    n_in, n_out = len(rider.arrays), len(rider.out_shape())

    def body(*refs):
        ins, outs, sems = refs[:n_in], refs[n_in:n_in + n_out], refs[n_in + n_out:]
        rider.start(ins, outs, sems)
        if rider.has_mid:
            rider.mid(ins, outs, sems)
        rider.finish(ins, outs, sems)

    return pl.pallas_call(
        body, name=name, out_shape=rider.out_shape(),
        in_specs=[ANY] * n_in, out_specs=[ANY] * n_out, scratch_shapes=rider.scratch(),
    )(*rider.arrays)


def _pcall(body, *, name, grid, in_specs, out_specs, out_shape, scratch_shapes=(), semantics, vmem_mb, rider=None):
    in_specs, out_specs, out_shape = list(in_specs), list(out_specs), list(out_shape)
    scratch_shapes = list(scratch_shapes)
    if rider is None:
        call = pl.pallas_call(body, name=name, grid=grid, in_specs=in_specs, out_specs=out_specs,
                              out_shape=out_shape, scratch_shapes=scratch_shapes,
                              compiler_params=_params(semantics, vmem_mb))
        return lambda *args: (list(call(*args)), None)
    n_in, n_out, n_scr = len(in_specs), len(out_specs), len(scratch_shapes)
    r_in, r_shapes = len(rider.arrays), rider.out_shape()
    r_out = len(r_shapes)
    total = math.prod(grid)
    mid_step = (3 * total) // 4

    def wrapped(*refs):
        bounds = [0, n_in, r_in, n_out, r_out, n_scr]
        for i in range(1, len(bounds)):
            bounds[i] += bounds[i - 1]
        a, ra, o, ro, s = (refs[bounds[i]:bounds[i + 1]] for i in range(5))
        rs = refs[bounds[5]:]
        step = pl.program_id(0)
        for k in range(1, len(grid)):
            step = step * grid[k] + pl.program_id(k)
        pl.when(step == 0)(lambda: rider.start(ra, ro, rs))
        body(*a, *o, *s)
        if rider.has_mid:
            pl.when(step == mid_step)(lambda: rider.mid(ra, ro, rs))
        pl.when(step == total - 1)(lambda: rider.finish(ra, ro, rs))

    call = pl.pallas_call(
        wrapped, name=name, grid=grid, in_specs=in_specs + [ANY] * r_in, out_specs=out_specs + [ANY] * r_out,
        out_shape=out_shape + r_shapes, scratch_shapes=scratch_shapes + rider.scratch(),
        compiler_params=_params(("arbitrary",) * len(grid), vmem_mb))

    def run(*args):
        res = call(*args, *rider.arrays)
        return list(res[:n_out]), list(res[n_out:])

    return run


def _norm_matmul(x, g, w, *, name, out_dtype, tb, blk0=0, nblk=None, relu=False, save_h=False, rider=None):
    t, d = x.shape
    bn = w.shape[2]
    nblk = w.shape[0] if nblk is None else nblk

    def body(x_ref, g_ref, w_ref, o_ref, *rest):
        h_scr = rest[-1]

        @pl.when(pl.program_id(1) == 0)
        def _():
            h = _rms_fwd(x_ref[...], g_ref[...])[0].astype(BF16)
            h_scr[...] = h
            if save_h:
                rest[0][...] = h

        acc = _dot(h_scr[...], w_ref[0])
        if relu:
            acc = jnp.maximum(acc, 0.0)
        o_ref[...] = acc.astype(out_dtype)

    out_shape = [jax.ShapeDtypeStruct((t, nblk * bn), out_dtype)]
    out_specs = [pl.BlockSpec((tb, bn), lambda i, j: (i, j))]
    if save_h:
        out_shape.append(jax.ShapeDtypeStruct((t, d), BF16))
        out_specs.append(pl.BlockSpec((tb, d), lambda i, j: (i, 0)))
    res, extra = _pcall(
        body, name=name, grid=(t // tb, nblk),
        in_specs=[pl.BlockSpec((tb, d), lambda i, j: (i, 0)),
                  pl.BlockSpec((1, d), lambda i, j: (0, 0)),
                  pl.BlockSpec((1, d, bn), lambda i, j: (j + blk0, 0, 0))],
        out_specs=out_specs, out_shape=out_shape,
        scratch_shapes=[pltpu.VMEM((tb, d), BF16)],
        semantics=("parallel", "arbitrary"), vmem_mb=40, rider=rider,
    )(x, g, w)
    res = res if save_h else res[0]
    return res if rider is None else (res, extra)


def _matmul_nt_normbwd(dy, w, x, g, dres, *, name, tb, rider=None):
    t, d = x.shape
    nblk, _, bn = w.shape
    n_i = t // tb
    has_res = dres is not None

    def body(dy_ref, w_ref, x_ref, g_ref, *rest):
        if has_res:
            dres_ref, dx_ref, gg_ref, acc = rest
        else:
            dx_ref, gg_ref, acc = rest
        i, j = pl.program_id(0), pl.program_id(1)

        @pl.when(j == 0)
        def _():
            acc[...] = jnp.zeros_like(acc)

        acc[...] += _dot_nt(dy_ref[...], w_ref[0])

        @pl.when(j == nblk - 1)
        def _():
            g_v = g_ref[...]
            _, xh, r = _rms_fwd(x_ref[...], g_v)
            dh = acc[...]
            dx = _rms_bwd(dh, xh, r, g_v)
            if has_res:
                dx = dx + dres_ref[...]
            dx_ref[...] = dx
            part = jnp.sum(dh * xh, axis=0, keepdims=True)

            @pl.when(i == 0)
            def _():
                gg_ref[...] = part

            @pl.when(i != 0)
            def _():
                gg_ref[...] += part

    in_specs = [pl.BlockSpec((tb, bn), lambda i, j: (i, j)),
                pl.BlockSpec((1, d, bn), lambda i, j: (j, 0, 0)),
                pl.BlockSpec((tb, d), lambda i, j: (i, 0)),
                pl.BlockSpec((1, d), lambda i, j: (0, 0))]
    args = [dy, w, x, g]
    if has_res:
        in_specs.append(pl.BlockSpec((tb, d), lambda i, j: (i, 0)))
        args.append(dres)
    res, extra = _pcall(
        body, name=name, grid=(n_i, nblk),
        in_specs=in_specs,
        out_specs=[pl.BlockSpec((tb, d), lambda i, j: (i, 0)), pl.BlockSpec((1, d), lambda i, j: (0, 0))],
        out_shape=[jax.ShapeDtypeStruct((t, d), F32), jax.ShapeDtypeStruct((1, d), F32)],
        scratch_shapes=[pltpu.VMEM((tb, d), F32)],
        semantics=("arbitrary", "arbitrary"), vmem_mb=40, rider=rider,
    )(*args)
    return res if rider is None else (res, extra)


def _matmul_tn(a, b, *, name, bm, bn, bt, col_sharded=False, square_a=False, rider=None):
    t, m = a.shape
    n = b.shape[1]
    n_t = t // bt

    def body(a_ref, b_ref, o_ref, acc):
        k = pl.program_id(2)

        @pl.when(k == 0)
        def _():
            acc[...] = jnp.zeros_like(acc)

        av = a_ref[...]
        if square_a:
            av = av.astype(F32)
            av = av * av
        acc[...] += _dot_tn(av.astype(BF16), b_ref[...].astype(BF16))

        @pl.when(k == n_t - 1)
        def _():
            if col_sharded:
                o_ref[0] = acc[...].astype(BF16)
            else:
                o_ref[...] = acc[...].astype(BF16)

    if col_sharded:
        per = (n // N_DEV) // bn
        out_shape = jax.ShapeDtypeStruct((N_DEV, m, n // N_DEV), BF16)
        out_spec = pl.BlockSpec((1, bm, bn), lambda i, j, k: (j // per, i, j % per))
    else:
        out_shape = jax.ShapeDtypeStruct((m, n), BF16)
        out_spec = pl.BlockSpec((bm, bn), lambda i, j, k: (i, j))
    res, extra = _pcall(
        body, name=name, grid=(m // bm, n // bn, n_t),
        in_specs=[pl.BlockSpec((bt, bm), lambda i, j, k: (k, i)),
                  pl.BlockSpec((bt, bn), lambda i, j, k: (k, j))],
        out_specs=[out_spec], out_shape=[out_shape],
        scratch_shapes=[pltpu.VMEM((bm, bn), F32)],
        semantics=("parallel", "parallel", "arbitrary"), vmem_mb=40, rider=rider,
    )(a, b)
    return res[0] if rider is None else (res[0], extra)


def _attn_masks():
    row = lax.broadcasted_iota(jnp.int32, (BLK, BLK), 0)
    col = lax.broadcasted_iota(jnp.int32, (BLK, BLK), 1)
    lane_lo = col < HEAD_DIM
    head_mask = [jnp.where(lane_lo, 1.0, 0.0).astype(BF16), jnp.where(lane_lo, 0.0, 1.0).astype(BF16)]
    return col <= row, col >= row, lane_lo, head_mask


def _head_bcast(v, h, lane_lo):
    rolled = pltpu.roll(v, HEAD_DIM, axis=1)
    return jnp.where(lane_lo, v, rolled) if h == 0 else jnp.where(lane_lo, rolled, v)


def _attn_fwd(qkv, d, name, rider=None):
    t = qkv.shape[0]
    nb = t // (BLK * d)
    q3 = qkv.reshape(nb, BLK, d * 1536)

    def body(q_ref, k_ref, v_ref, o_ref, l_ref):
        cur_ok, prev_ok, lane_lo, head_mask = _attn_masks()

        def block(n, has_prev):
            q, kc, vc = q_ref[n], k_ref[n], v_ref[n]
            if has_prev:
                kp, vp = k_ref[n - 1], v_ref[n - 1]
            outs, lses = [], []
            for h in range(2):
                qh = q * head_mask[h]
                sc = jnp.where(cur_ok, _dot_nt(qh, kc) * 0.125, NEG_INF)
                mx = jnp.max(sc, axis=1, keepdims=True)
                if has_prev:
                    sp = jnp.where(prev_ok, _dot_nt(qh, kp) * 0.125, NEG_INF)
                    mx = jnp.maximum(mx, jnp.max(sp, axis=1, keepdims=True))
                ec = jnp.exp(sc - mx)
                den = jnp.sum(ec, axis=1, keepdims=True)
                if has_prev:
                    ep = jnp.exp(sp - mx)
                    den = den + jnp.sum(ep, axis=1, keepdims=True)
                inv = 1.0 / den
                o = _dot((ec * inv).astype(BF16), vc)
                if has_prev:
                    o = o + _dot((ep * inv).astype(BF16), vp)
                outs.append(o)
                lses.append(jnp.broadcast_to(mx + jnp.log(den), (BLK, BLK)))
            o_ref[n] = jnp.where(lane_lo, outs[0], outs[1])
            l_ref[n] = jnp.where(lane_lo, lses[0], lses[1])

        block(0, False)
        if nb > 1:
            def loop(n, carry):
                block(n, True)
                return carry
            lax.fori_loop(1, nb, loop, 0)

    blk = (nb, BLK, BLK)
    (o3, l3), extra = _pcall(
        body, name=name, grid=(d, 4),
        in_specs=[pl.BlockSpec(blk, lambda r, hp: (0, 0, r * 12 + hp)),
                  pl.BlockSpec(blk, lambda r, hp: (0, 0, r * 12 + 4 + hp)),
                  pl.BlockSpec(blk, lambda r, hp: (0, 0, r * 12 + 8 + hp))],
        out_specs=[pl.BlockSpec(blk, lambda r, hp: (0, 0, r * 4 + hp))] * 2,
        out_shape=[jax.ShapeDtypeStruct((nb, BLK, d * 512), F32)] * 2,
        semantics=("parallel", "parallel"), vmem_mb=40, rider=rider,
    )(q3, q3, q3)
    res = (o3.reshape(t, 512), l3.reshape(t, 512))
    return res if rider is None else (res, extra)


def _attn_bwd(qkv, do, lse, dl, d, name, rider=None):
    t = qkv.shape[0]
    nb = t // (BLK * d)
    q3 = qkv.reshape(nb, BLK, d * 1536)
    do3 = do.reshape(nb, BLK, d * 512)
    l3 = lse.reshape(nb, BLK, d * 512)
    dl3 = dl.reshape(nb, BLK, d * 512)

    def body(q_ref, k_ref, v_ref, do_ref, l_ref, dl_ref, dq_ref, dk_ref, dv_ref):
        cur_ok, prev_ok, lane_lo, head_mask = _attn_masks()

        def block(n, has_prev):
            q, kc, vc, dov = q_ref[n], k_ref[n], v_ref[n], do_ref[n]
            lse_v, dl_v = l_ref[n], dl_ref[n]
            if has_prev:
                kp, vp = k_ref[n - 1], v_ref[n - 1]
            dqs = []
            dkc = dvc = dkp = dvp = None
            for h in range(2):
                qh = q * head_mask[h]
                doh = dov * head_mask[h]
                lse_h = _head_bcast(lse_v, h, lane_lo)
                dl_h = _head_bcast(dl_v, h, lane_lo)

                def grads(k_blk, v_blk, ok):
                    p = jnp.exp(jnp.where(ok, _dot_nt(qh, k_blk) * 0.125, NEG_INF) - lse_h)
                    ds = (p * (_dot_nt(doh, v_blk) - dl_h) * 0.125).astype(BF16)
                    return _dot(ds, k_blk), _dot_tn(ds, qh), _dot_tn(p.astype(BF16), doh)

                dq, dk1, dv1 = grads(kc, vc, cur_ok)
                dkc = dk1 if dkc is None else dkc + dk1
                dvc = dv1 if dvc is None else dvc + dv1
                if has_prev:
                    dq2, dk2, dv2 = grads(kp, vp, prev_ok)
                    dq = dq + dq2
                    dkp = dk2 if dkp is None else dkp + dk2
                    dvp = dv2 if dvp is None else dvp + dv2
                dqs.append(dq)
            dq_ref[n] = jnp.where(lane_lo, dqs[0], dqs[1])
            dk_ref[n] = dkc
            dv_ref[n] = dvc
            if has_prev:
                dk_ref[n - 1] += dkp
                dv_ref[n - 1] += dvp

        block(0, False)
        if nb > 1:
            def loop(n, carry):
                block(n, True)
                return carry
            lax.fori_loop(1, nb, loop, 0)

    blk = (nb, BLK, BLK)
    small = pl.BlockSpec(blk, lambda r, hp: (0, 0, r * 4 + hp))
    outs, extra = _pcall(
        body, name=name, grid=(d, 4),
        in_specs=[pl.BlockSpec(blk, lambda r, hp: (0, 0, r * 12 + hp)),
                  pl.BlockSpec(blk, lambda r, hp: (0, 0, r * 12 + 4 + hp)),
                  pl.BlockSpec(blk, lambda r, hp: (0, 0, r * 12 + 8 + hp)),
                  small, small, small],
        out_specs=[small] * 3,
        out_shape=[jax.ShapeDtypeStruct((nb, BLK, d * 512), F32)] * 3,
        semantics=("parallel", "parallel"), vmem_mb=48, rider=rider,
    )(q3, q3, q3, do3, l3, dl3)
    res = [o.reshape(t, 512) for o in outs]
    return res if rider is None else (res, extra)


def _mix_weights(l1, l2, l3):
    mx = jnp.maximum(jnp.maximum(l1, l2), l3)
    e1, e2, e3 = jnp.exp(l1 - mx), jnp.exp(l2 - mx), jnp.exp(l3 - mx)
    inv = 1.0 / (e1 + e2 + e3)
    return e1 * inv, e2 * inv, e3 * inv


def _shift_down(u, halo, tb):
    row = lax.broadcasted_iota(jnp.int32, u.shape, 0)
    h6 = jnp.broadcast_to(halo[6:7, :], u.shape)
    h7 = jnp.broadcast_to(halo[7:8, :], u.shape)
    m1 = jnp.where(row == 0, h7, pltpu.roll(u, 1, axis=0))
    m2 = jnp.where(row == 0, h6, jnp.where(row == 1, h7, pltpu.roll(u, 2, axis=0)))
    return m1, m2


def _shift_up(u, halo, tb):
    row = lax.broadcasted_iota(jnp.int32, u.shape, 0)
    h0 = jnp.broadcast_to(halo[0:1, :], u.shape)
    h1 = jnp.broadcast_to(halo[1:2, :], u.shape)
    p1 = jnp.where(row == tb - 1, h0, pltpu.roll(u, tb - 1, axis=0))
    p2 = jnp.where(row == tb - 2, h0, jnp.where(row == tb - 1, h1, pltpu.roll(u, tb - 2, axis=0)))
    return p1, p2


def _conv_fwd(gates, halo, first, cw, tb):
    bg, cg, xc = gates[:, 0:512], gates[:, 512:1024], gates[:, 1024:1536]
    u = cg * xc
    uh = halo[:, 512:1024] * halo[:, 1024:1536]
    uh = jnp.where(first, jnp.zeros_like(uh), uh)
    m1, m2 = _shift_down(u, uh, tb)
    conv = m2 * cw[0:1, :] + m1 * cw[1:2, :] + u * cw[2:3, :]
    return bg, u, m1, m2, conv


def _mixer_fwd(x, o_l, gates, cw, g_a, g_c, w_out, *, tb):
    t, d = x.shape
    hb = tb // 8

    def body(x_ref, o1, o2, o3, l1, l2, l3, gt_ref, gh_ref, cw_ref, ga_ref, gc_ref, wa_ref, wb_ref, x1_ref, mg_ref):
        w1, w2, w3 = _mix_weights(l1[...], l2[...], l3[...])
        attn = w1 * o1[...] + w2 * o2[...] + w3 * o3[...]
        an = _rms_fwd(attn, ga_ref[...])[0].astype(BF16)
        bg, _, _, _, conv = _conv_fwd(gt_ref[...], gh_ref[...], pl.program_id(0) == 0, cw_ref[...], tb)
        cn = _rms_fwd(bg * conv, gc_ref[...])[0].astype(BF16)
        mg_ref[:, 0:512] = an
        mg_ref[:, 512:1024] = cn
        x1_ref[...] = x_ref[...] + _dot(an, wa_ref[...]) + _dot(cn, wb_ref[...])

    tok = lambda c: pl.BlockSpec((tb, c), lambda i: (i, 0))
    const = lambda r, c, i0=0: pl.BlockSpec((r, c), lambda i: (i0, 0))
    return pl.pallas_call(
        body, name="mixer_fwd", grid=(t // tb,),
        in_specs=[tok(d)] + [tok(512)] * 6
                 + [tok(1536), pl.BlockSpec((8, 1536), lambda i: (jnp.maximum(i * hb - 1, 0), 0)),
                    const(3, 512), const(1, 512), const(1, 512), const(512, d), const(512, d, 1)],
        out_specs=[tok(d), tok(d)],
        out_shape=[jax.ShapeDtypeStruct((t, d), F32), jax.ShapeDtypeStruct((t, d), BF16)],
        compiler_params=_params(("parallel",), 48),
    )(x, *o_l, gates, gates, cw, g_a, g_c, w_out, w_out)


def _mixer_bwd(dx1, o_l, gates, cw, g_a, g_c, w_out, head_sum, *, tb, rider=None):
    t, d = dx1.shape
    hb = tb // 8

    def body(dx_ref, o1, o2, o3, l1, l2, l3, gt_ref, gh_ref, cw_ref, ga_ref, gc_ref, wa_ref, wb_ref, hs_ref,
             do1, do2, do3, dl1, dl2, dl3, dy_ref, gga_ref, ggc_ref):
        i = pl.program_id(0)
        dxb = dx_ref[...].astype(BF16)
        dma = _dot_nt(dxb, wa_ref[...])
        dmc = _dot_nt(dxb, wb_ref[...])
        ws = _mix_weights(l1[...], l2[...], l3[...])
        attn = ws[0] * o1[...] + ws[1] * o2[...] + ws[2] * o3[...]
        g_av = ga_ref[...]
        _, ah, ra = _rms_fwd(attn, g_av)
        dattn = _rms_bwd(dma, ah, ra, g_av)
        z = dattn * attn
        hs = hs_ref[...]
        z1 = z.astype(BF16)
        z2 = (z - z1.astype(F32)).astype(BF16)
        z3 = (z - z1.astype(F32) - z2.astype(F32)).astype(BF16)
        zsum = _dot(z1, hs) + _dot(z2, hs) + _dot(z3, hs)
        for w, do_ref, dl_ref in zip(ws, (do1, do2, do3), (dl1, dl2, dl3)):
            do_ref[...] = (w * dattn).astype(BF16)
            dl_ref[...] = w * zsum
        bg, _, _, _, conv = _conv_fwd(gt_ref[...], gh_ref[...], i == 0, cw_ref[...], tb)
        g_cv = gc_ref[...]
        _, yh, rc = _rms_fwd(bg * conv, g_cv)
        dy_ref[...] = _rms_bwd(dmc, yh, rc, g_cv)
        pa = jnp.sum(dma * ah, axis=0, keepdims=True)
        pc = jnp.sum(dmc * yh, axis=0, keepdims=True)

        @pl.when(i == 0)
        def _():
            gga_ref[...] = pa
            ggc_ref[...] = pc

        @pl.when(i != 0)
        def _():
            gga_ref[...] += pa
            ggc_ref[...] += pc

    tok = lambda c: pl.BlockSpec((tb, c), lambda i: (i, 0))
    const = lambda r, c, i0=0: pl.BlockSpec((r, c), lambda i: (i0, 0))
    res, extra = _pcall(
        body, name="mixer_bwd", grid=(t // tb,),
        in_specs=[tok(d)] + [tok(512)] * 6
                 + [tok(1536), pl.BlockSpec((8, 1536), lambda i: (jnp.maximum(i * hb - 1, 0), 0)),
                    const(3, 512), const(1, 512), const(1, 512), const(512, d), const(512, d, 1), const(512, 512)],
        out_specs=[tok(512)] * 7 + [const(1, 512), const(1, 512)],
        out_shape=[jax.ShapeDtypeStruct((t, 512), BF16)] * 3 + [jax.ShapeDtypeStruct((t, 512), F32)] * 4
                  + [jax.ShapeDtypeStruct((1, 512), F32)] * 2,
        semantics=("arbitrary",), vmem_mb=48, rider=rider,
    )(dx1, *o_l, gates, gates, cw, g_a, g_c, w_out, w_out, head_sum)
    return res if rider is None else (res, extra)


def _conv_bwd(dy, gates, cw, *, tb):
    t = dy.shape[0]
    hb = tb // 8
    n_i = t // tb

    def body(dy_ref, dyn_ref, gt_ref, gp_ref, gn_ref, cw_ref, dg_ref, gcw_ref):
        i = pl.program_id(0)
        cw_v = cw_ref[...]
        gates_v = gt_ref[...]
        bg, u, m1, m2, conv = _conv_fwd(gates_v, gp_ref[...], i == 0, cw_v, tb)
        dy_v = dy_ref[...]
        dconv = dy_v * bg
        dch = dyn_ref[...] * gn_ref[:, 0:512]
        dch = jnp.where(i == n_i - 1, jnp.zeros_like(dch), dch)
        p1, p2 = _shift_up(dconv, dch, tb)
        du = dconv * cw_v[2:3, :] + p1 * cw_v[1:2, :] + p2 * cw_v[0:1, :]
        dg_ref[:, 0:512] = (dy_v * conv).astype(BF16)
        dg_ref[:, 512:1024] = (du * gates_v[:, 1024:1536]).astype(BF16)
        dg_ref[:, 1024:1536] = (du * gates_v[:, 512:1024]).astype(BF16)
        parts = [jnp.sum(dconv * m2, axis=0, keepdims=True), jnp.sum(dconv * m1, axis=0, keepdims=True),
                 jnp.sum(dconv * u, axis=0, keepdims=True)]

        @pl.when(i == 0)
        def _():
            gcw_ref[...] = jnp.zeros_like(gcw_ref)

        for tap in range(3):
            gcw_ref[tap:tap + 1, :] += parts[tap]

    tok = lambda c: pl.BlockSpec((tb, c), lambda i: (i, 0))
    nxt = lambda c: pl.BlockSpec((8, c), lambda i: (jnp.minimum((i + 1) * hb, t // 8 - 1), 0))
    return pl.pallas_call(
        body, name="conv_bwd", grid=(n_i,),
        in_specs=[tok(512), nxt(512), tok(1536),
                  pl.BlockSpec((8, 1536), lambda i: (jnp.maximum(i * hb - 1, 0), 0)), nxt(1536),
                  pl.BlockSpec((3, 512), lambda i: (0, 0))],
        out_specs=[tok(1536), pl.BlockSpec((8, 512), lambda i: (0, 0))],
        out_shape=[jax.ShapeDtypeStruct((t, 1536), BF16), jax.ShapeDtypeStruct((8, 512), F32)],
        compiler_params=_params(("arbitrary",), 40),
    )(dy, dy, gates, gates, gates, cw)


def _pack_dproj(dqkv, dgates, *, tb):
    t = dgates.shape[0]

    def body(*refs):
        ins, dg_ref, o_ref = refs[:9], refs[9], refs[10]
        for s in range(3):
            o_ref[:, 512 * s:512 * (s + 1)] = (ins[s][...] + ins[3 + s][...] + ins[6 + s][...]).astype(BF16)
        o_ref[:, 1536:3072] = dg_ref[...]

    tok = lambda c: pl.BlockSpec((tb, c), lambda i: (i, 0))
    return pl.pallas_call(
        body, name="pack_dproj", grid=(t // tb,),
        in_specs=[tok(512)] * 9 + [tok(1536)],
        out_specs=tok(3072), out_shape=jax.ShapeDtypeStruct((t, 3072), BF16),
        compiler_params=_params(("parallel",), 40),
    )(*dqkv, dgates)


def _xattn_fwd(x1, g, w_q, kv, w_o, *, tb):
    t, d = x1.shape
    hd = d // N_MEM_HEADS
    m = kv.shape[0]

    def body(x_ref, g_ref, wq_ref, k_ref, v_ref, wo_ref, x2_ref, h_ref, q_ref, o_ref):
        xv = x_ref[...]
        h = _rms_fwd(xv, g_ref[...])[0].astype(BF16)
        h_ref[...] = h
        q = _dot(h, wq_ref[...]).astype(BF16)
        q_ref[...] = q
        for hh in range(N_MEM_HEADS):
            sl = slice(hh * hd, (hh + 1) * hd)
            s = _dot_nt(q[:, sl], k_ref[:, sl]) * (1.0 / 16.0)
            e = jnp.exp(s - jnp.max(s, axis=1, keepdims=True))
            p = e / jnp.sum(e, axis=1, keepdims=True)
            o_ref[:, sl] = _dot(p.astype(BF16), v_ref[:, sl]).astype(BF16)
        x2_ref[...] = xv + _dot(o_ref[...], wo_ref[...])

    tok = pl.BlockSpec((tb, d), lambda i: (i, 0))
    full = pl.BlockSpec((d, d), lambda i: (0, 0))
    return pl.pallas_call(
        body, name="xattn_fwd", grid=(t // tb,),
        in_specs=[tok, pl.BlockSpec((1, d), lambda i: (0, 0)), full,
                  pl.BlockSpec((m, d), lambda i: (0, 0)), pl.BlockSpec((m, d), lambda i: (0, 1)), full],
        out_specs=[tok] * 4,
        out_shape=[jax.ShapeDtypeStruct((t, d), F32)] + [jax.ShapeDtypeStruct((t, d), BF16)] * 3,
        compiler_params=_params(("parallel",), 48),
    )(x1, g, w_q, kv, kv, w_o)


def _xattn_bwd(dx2, x1, g, q, w_q, kv, w_o, *, tb, rider=None):
    t, d = x1.shape
    hd = d // N_MEM_HEADS
    m = kv.shape[0]

    def body(dx2_ref, x_ref, g_ref, q_ref, wq_ref, k_ref, v_ref, wo_ref, dx1_ref, dq_ref, dk_ref, dv_ref, gg_ref):
        i = pl.program_id(0)

        @pl.when(i == 0)
        def _():
            dk_ref[...] = jnp.zeros_like(dk_ref)
            dv_ref[...] = jnp.zeros_like(dv_ref)

        dx2 = dx2_ref[...]
        do = _dot_nt(dx2.astype(BF16), wo_ref[...]).astype(BF16)
        for hh in range(N_MEM_HEADS):
            sl = slice(hh * hd, (hh + 1) * hd)
            qh, kh, vh, doh = q_ref[:, sl], k_ref[:, sl], v_ref[:, sl], do[:, sl]
            s = _dot_nt(qh, kh) * (1.0 / 16.0)
            e = jnp.exp(s - jnp.max(s, axis=1, keepdims=True))
            p = e / jnp.sum(e, axis=1, keepdims=True)
            dp = _dot_nt(doh, vh)
            ds = (p * (dp - jnp.sum(dp * p, axis=1, keepdims=True)) * (1.0 / 16.0)).astype(BF16)
            dq_ref[:, sl] = _dot(ds, kh).astype(BF16)
            dk_ref[:, sl] += _dot_tn(ds, qh)
            dv_ref[:, sl] += _dot_tn(p.astype(BF16), doh)
        dh = _dot_nt(dq_ref[...], wq_ref[...])
        g_v = g_ref[...]
        _, xh, r = _rms_fwd(x_ref[...], g_v)
        dx1_ref[...] = dx2 + _rms_bwd(dh, xh, r, g_v)
        part = jnp.sum(dh * xh, axis=0, keepdims=True)

        @pl.when(i == 0)
        def _():
            gg_ref[...] = part

        @pl.when(i != 0)
        def _():
            gg_ref[...] += part

    tok = pl.BlockSpec((tb, d), lambda i: (i, 0))
    full = pl.BlockSpec((d, d), lambda i: (0, 0))
    acc = pl.BlockSpec((m, d), lambda i: (0, 0))
    res, extra = _pcall(
        body, name="xattn_bwd", grid=(t // tb,),
        in_specs=[tok, tok, pl.BlockSpec((1, d), lambda i: (0, 0)), tok, full,
                  pl.BlockSpec((m, d), lambda i: (0, 0)), pl.BlockSpec((m, d), lambda i: (0, 1)), full],
        out_specs=[tok, tok, acc, acc, pl.BlockSpec((1, d), lambda i: (0, 0))],
        out_shape=[jax.ShapeDtypeStruct((t, d), F32), jax.ShapeDtypeStruct((t, d), BF16),
                   jax.ShapeDtypeStruct((m, d), F32), jax.ShapeDtypeStruct((m, d), F32),
                   jax.ShapeDtypeStruct((1, d), F32)],
        semantics=("arbitrary",), vmem_mb=48, rider=rider,
    )(dx2, x1, g, q, w_q, kv, kv, w_o)
    return res if rider is None else (res, extra)


def _mlp_down_loss(a, w_down, x2, tgt, g, *, tb):
    t, d = x2.shape
    f = a.shape[1]

    def body(a_ref, w_ref, x_ref, t_ref, g_ref, dx_ref, loss_ref, gg_ref):
        i = pl.program_id(0)
        av = a_ref[...].astype(F32)
        x3 = x_ref[...] + _dot((av * av).astype(BF16), w_ref[...])
        g_v = g_ref[...]
        out, xh, r = _rms_fwd(x3, g_v)
        err = out - t_ref[...]
        dout = err * (1.0 / d)
        dx_ref[...] = _rms_bwd(dout, xh, r, g_v)
        part = jnp.sum(dout * xh, axis=0, keepdims=True)
        lpart = 0.5 * jnp.sum(jnp.mean(err * err, axis=-1, keepdims=True), axis=0, keepdims=True)
        lpart = jnp.broadcast_to(lpart, loss_ref.shape)

        @pl.when(i == 0)
        def _():
            gg_ref[...] = part
            loss_ref[...] = lpart

        @pl.when(i != 0)
        def _():
            gg_ref[...] += part
            loss_ref[...] += lpart

    tok = pl.BlockSpec((tb, d), lambda i: (i, 0))
    return pl.pallas_call(
        body, name="mlp_down_loss", grid=(t // tb,),
        in_specs=[pl.BlockSpec((tb, f), lambda i: (i, 0)), pl.BlockSpec((f, d), lambda i: (0, 0)), tok, tok,
                  pl.BlockSpec((1, d), lambda i: (0, 0))],
        out_specs=[tok, pl.BlockSpec((8, 128), lambda i: (0, 0)), pl.BlockSpec((1, d), lambda i: (0, 0))],
        out_shape=[jax.ShapeDtypeStruct((t, d), F32), jax.ShapeDtypeStruct((8, 128), F32),
                   jax.ShapeDtypeStruct((1, d), F32)],
        compiler_params=_params(("arbitrary",), 56),
    )(a, w_down, x2, tgt, g)


def _mlp_dpre(dx3, w_down, a, *, tb):
    t, d = dx3.shape
    nblk, bn, _ = w_down.shape

    def body(dx_ref, w_ref, a_ref, o_ref, dxb):
        @pl.when(pl.program_id(1) == 0)
        def _():
            dxb[...] = dx_ref[...].astype(BF16)

        o_ref[...] = (2.0 * a_ref[...].astype(F32) * _dot_nt(dxb[...], w_ref[0])).astype(BF16)

    return pl.pallas_call(
        body, name="mlp_dpre", grid=(t // tb, nblk),
        in_specs=[pl.BlockSpec((tb, d), lambda i, j: (i, 0)), pl.BlockSpec((1, bn, d), lambda i, j: (j, 0, 0)),
                  pl.BlockSpec((tb, bn), lambda i, j: (i, j))],
        out_specs=pl.BlockSpec((tb, bn), lambda i, j: (i, j)),
        out_shape=jax.ShapeDtypeStruct((t, nblk * bn), BF16),
        scratch_shapes=[pltpu.VMEM((tb, d), BF16)],
        compiler_params=_params(("parallel", "arbitrary"), 40),
    )(dx3, w_down, a)


def _adamw(gsum, w, m, v):
    m_new = ADAM_B1 * m + (1.0 - ADAM_B1) * gsum
    v_new = ADAM_B2 * v + (1.0 - ADAM_B2) * (gsum * gsum)
    m_hat = m_new / (1.0 - ADAM_B1 ** ADAM_STEP)
    v_hat = v_new / (1.0 - ADAM_B2 ** ADAM_STEP)
    delta = -ADAM_LR * (m_hat / (jnp.sqrt(v_hat) + ADAM_EPS) + ADAM_WD * w)
    return delta, m_new, v_new


def _sum_adamw(parts, w, m, v, *, name, tr):
    r, c = w.shape

    def body(p_ref, w_ref, m_ref, v_ref, g_ref, d_ref, mo_ref, vo_ref):
        g = p_ref[0].astype(F32)
        for k in range(1, N_DEV):
            g = g + p_ref[k].astype(F32)
        g_ref[...] = g
        d_ref[...], mo_ref[...], vo_ref[...] = _adamw(g, w_ref[...], m_ref[...], v_ref[...])

    blk = pl.BlockSpec((tr, c), lambda i: (i, 0))
    return pl.pallas_call(
        body, name=name, grid=(r // tr,),
        in_specs=[pl.BlockSpec((N_DEV, tr, c), lambda i: (0, i, 0)), blk, blk, blk],
        out_specs=[blk] * 4, out_shape=[jax.ShapeDtypeStruct((r, c), F32)] * 4,
        compiler_params=_params(("parallel",), 40),
    )(parts, w, m, v)


def _sum_small(parts):
    _, r, c = parts.shape

    def body(p_ref, o_ref):
        s = p_ref[0]
        for k in range(1, N_DEV):
            s = s + p_ref[k]
        o_ref[...] = s

    return pl.pallas_call(body, name="sum_small", out_shape=jax.ShapeDtypeStruct((r, c), F32))(parts)


def _adamw_small(g, w, m, v):
    def body(g_ref, w_ref, m_ref, v_ref, d_ref, mo_ref, vo_ref):
        d_ref[...], mo_ref[...], vo_ref[...] = _adamw(g_ref[...], w_ref[...], m_ref[...], v_ref[...])

    return pl.pallas_call(body, name="adamw_small", out_shape=[jax.ShapeDtypeStruct(g.shape, F32)] * 3)(g, w, m, v)


def _head_sum_matrix():
    r = lax.broadcasted_iota(jnp.int32, (512, 512), 0) // HEAD_DIM
    c = lax.broadcasted_iota(jnp.int32, (512, 512), 1) // HEAD_DIM
    return (r == c).astype(BF16)


_ROW_SHARDED = ("w_out", "w_q", "w_o", "w_down")


class _Weights:
    def __init__(self, full, shards=None):
        self.full = dict(full)
        self.shards = shards

    def rider(self, names):
        return None if self.shards is None else _Gather([self.shards[n] for n in names])

    def arrived(self, names, gathered):
        if gathered is not None:
            for n, g in zip(names, gathered):
                self.full[n] = g.reshape(-1, g.shape[-1]) if n in _ROW_SHARDED else g

    def __getitem__(self, name):
        return self.full[name]


class _Grads:
    def __init__(self, distributed):
        self.distributed = distributed
        self.local = {}
        self.received = {}

    def add(self, name, g):
        self.local[name] = g.reshape(N_DEV, -1, g.shape[-1]) if name in _ROW_SHARDED else g

    def rider(self, names):
        return _Exchange([self.local[n] for n in names]) if self.distributed else None

    def arrived(self, names, received):
        if received is not None:
            for n, r in zip(names, received):
                self.received[n] = r


def _ride(fn, *args, rider=None, **kw):
    if rider is None:
        return fn(*args, **kw), None
    return fn(*args, rider=rider, **kw)


def _local_step(x, mem, tgt, gains, weights, cw, grads):
    t, d = x.shape
    w_in = weights["w_in"]

    names = ["w_q", "w_kv", "w_o"]
    (qkv, h1), got = _ride(_norm_matmul, x, gains["g_mix"], w_in, name="proj_qkv", out_dtype=BF16, tb=512, blk0=0,
                           nblk=4, save_h=True, rider=weights.rider(names))
    weights.arrived(names, got)
    gates, got = _ride(_norm_matmul, x, gains["g_mix"], w_in, name="proj_gates", out_dtype=F32, tb=512, blk0=4,
                       nblk=4, rider=weights.rider(["w_up"]))
    weights.arrived(["w_up"], got)
    o_l = []
    for dil in DILATIONS:
        rider = weights.rider(["w_down"]) if dil == 1 else None
        res, got = _ride(_attn_fwd, qkv, dil, name=f"attn_fwd_d{dil}", rider=rider)
        weights.arrived(["w_down"], got)
        o_l.append(res)
    o_l = [p[0] for p in o_l] + [p[1] for p in o_l]
    x1, merged = _mixer_fwd(x, o_l, gates, cw, gains["g_attn_out"], gains["g_conv_out"], weights["w_out"], tb=256)
    kv, mem_n = _norm_matmul(mem, gains["g_mem"], weights["w_kv"], name="mem_kv", out_dtype=BF16, tb=mem.shape[0],
                             save_h=True)
    x2, h2, qm, om = _xattn_fwd(x1, gains["g_xattn"], weights["w_q"], kv, weights["w_o"], tb=256)
    w_up = weights["w_up"]
    a, h3 = _norm_matmul(x2, gains["g_mlp"], w_up, name="mlp_up", out_dtype=BF16, tb=512, relu=True, save_h=True)
    dx3, loss_blk, gg_final = _mlp_down_loss(a, weights["w_down"], x2, tgt, gains["g_final"], tb=256)

    w_down_blocks = weights["w_down"].reshape(N_DEV, -1, d)
    dpre = _mlp_dpre(dx3, w_down_blocks, a, tb=512)
    grads.add("w_down", _matmul_tn(a, dx3, name="grad_w_down", bm=512, bn=512, bt=1024, square_a=True))
    gw_up, got = _ride(_matmul_tn, h3, dpre, name="grad_w_up", bm=512, bn=512, bt=1024, col_sharded=True,
                       rider=grads.rider(["w_down"]))
    grads.arrived(["w_down"], got)
    grads.add("w_up", gw_up)
    (dx2, gg_mlp), got = _ride(_matmul_nt_normbwd, dpre, w_up, x2, gains["g_mlp"], dx3, name="mlp_dx", tb=512,
                               rider=grads.rider(["w_up"]))
    grads.arrived(["w_up"], got)

    grads.add("w_o", _matmul_tn(om, dx2, name="grad_w_o", bm=512, bn=512, bt=1024))
    (dx1, dqm, dk, dv, gg_xattn), got = _ride(_xattn_bwd, dx2, x1, gains["g_xattn"], qm, weights["w_q"], kv,
                                              weights["w_o"], tb=256, rider=grads.rider(["w_o"]))
    grads.arrived(["w_o"], got)
    grads.add("w_q", _matmul_tn(h2, dqm, name="grad_w_q", bm=512, bn=512, bt=1024))
    dkv = jnp.concatenate([dk, dv], axis=1).astype(BF16)
    grads.add("w_kv", _matmul_tn(mem_n, dkv, name="grad_w_kv", bm=512, bn=256, bt=mem.shape[0], col_sharded=True))
    _, gg_mem = _matmul_nt_normbwd(dkv, weights["w_kv"], mem, gains["g_mem"], None, name="mem_dx", tb=mem.shape[0])

    grads.add("w_out", _matmul_tn(merged, dx1, name="grad_w_out", bm=512, bn=512, bt=1024))
    names = ["w_q", "w_out"]
    mb, got = _ride(_mixer_bwd, dx1, o_l, gates, cw, gains["g_attn_out"], gains["g_conv_out"], weights["w_out"],
                    _head_sum_matrix(), tb=256, rider=grads.rider(names))
    grads.arrived(names, got)
    do_l, dl_l, dy, gg_attn, gg_conv = mb[0:3], mb[3:6], mb[6], mb[7], mb[8]
    dgates, gcw = _conv_bwd(dy, gates, cw, tb=256)
    dqkv = []
    for p, dil in enumerate(DILATIONS):
        rider = grads.rider(["w_kv"]) if dil == 1 else None
        res, got = _ride(_attn_bwd, qkv, do_l[p], o_l[3 + p], dl_l[p], dil, name=f"attn_bwd_d{dil}", rider=rider)
        grads.arrived(["w_kv"], got)
        dqkv += res
    dproj = _pack_dproj(dqkv, dgates, tb=256)
    grads.add("w_in", _matmul_tn(h1, dproj, name="grad_w_in", bm=512, bn=384, bt=1024, col_sharded=True))
    (grad_x, gg_mix), got = _ride(_matmul_nt_normbwd, dproj, w_in, x, gains["g_mix"], dx1, name="mixer_dx", tb=512,
                                  rider=grads.rider(["w_in"]))
    grads.arrived(["w_in"], got)

    small = dict(g_mix=gg_mix, g_attn_out=gg_attn, g_conv_out=gg_conv, g_xattn=gg_xattn, g_mem=gg_mem,
                 g_mlp=gg_mlp, g_final=gg_final, conv_w=gcw[0:3], loss=loss_blk[0:1, 0:1])
    return grad_x, small


_BIG = ("w_in", "w_out", "w_q", "w_kv", "w_o", "w_up", "w_down")
_GAIN_ROWS = ("g_mix", "g_xattn", "g_mem", "g_mlp", "g_final")


def _pack_small(vals, conv):
    rows = [vals[k].reshape(1, -1) for k in _GAIN_ROWS]
    rows.append(jnp.concatenate([vals["g_attn_out"].reshape(1, -1), vals["g_conv_out"].reshape(1, -1)], axis=1))
    flat = conv.reshape(1, -1)
    rows.append(jnp.pad(flat, ((0, 0), (0, 1024 - flat.shape[1]))))
    rows.append(jnp.zeros((1, 1024), F32))
    return jnp.concatenate(rows, axis=0)


def kernel(x, mem, g_mix, w_in, conv_w, g_attn_out, g_conv_out, w_out, g_xattn, g_mem, w_q_mem, w_kv_mem, w_o_mem, g_mlp, w_up, w_down, g_final, loss_target, m_g_mix, m_w_in, m_conv_w, m_g_attn_out, m_g_conv_out, m_w_out, m_g_xattn, m_g_mem, m_w_q_mem, m_w_kv_mem, m_w_o_mem, m_g_mlp, m_w_up, m_w_down, m_g_final, v_g_mix, v_w_in, v_conv_w, v_g_attn_out, v_g_conv_out, v_w_out, v_g_xattn, v_g_mem, v_w_q_mem, v_w_kv_mem, v_w_o_mem, v_g_mlp, v_w_up, v_w_down, v_g_final):
    d = x.shape[-1]
    me = 4 * lax.axis_index("x") + 2 * lax.axis_index("y") + lax.axis_index("c")
    w_shards = dict(w_in=w_in, w_out=w_out, w_q=w_q_mem, w_kv=w_kv_mem, w_o=w_o_mem, w_up=w_up, w_down=w_down)
    m_shards = dict(w_in=m_w_in, w_out=m_w_out, w_q=m_w_q_mem, w_kv=m_w_kv_mem, w_o=m_w_o_mem, w_up=m_w_up,
                    w_down=m_w_down)
    v_shards = dict(w_in=v_w_in, w_out=v_w_out, w_q=v_w_q_mem, w_kv=v_w_kv_mem, w_o=v_w_o_mem, w_up=v_w_up,
                    w_down=v_w_down)
    gains = dict(g_mix=g_mix, g_attn_out=g_attn_out, g_conv_out=g_conv_out, g_xattn=g_xattn, g_mem=g_mem,
                 g_mlp=g_mlp, g_final=g_final)
    gains2 = {k: v.reshape(1, -1) for k, v in gains.items()}

    shards = {k: w_shards[k].astype(BF16) for k in _BIG}
    first = _comm_call(_Gather([shards["w_in"], shards["w_out"], conv_w]), name="gather_first")
    weights = _Weights({}, shards)
    weights.arrived(["w_in", "w_out"], first[:2])
    cw = first[2].transpose(1, 0, 2).reshape(3, -1)

    grads = _Grads(distributed=True)
    grad_x, small = _local_step(x[0], mem[0], loss_target[0], gains2, weights, cw, grads)

    small_rows = [small[k] for k in _GAIN_ROWS]
    small_rows.append(jnp.concatenate([small["g_attn_out"], small["g_conv_out"]], axis=1))
    small_rows.append(jnp.pad(small["conv_w"], ((0, 0), (0, 512))))
    small_rows.append(jnp.pad(small["loss"], ((0, 6), (0, 1023))))
    small_part = jnp.concatenate(small_rows, axis=0)
    small_received = _comm_call(_Exchange([small_part], [True]), name="exchange_small")[0]

    outs = {}
    tiles = dict(w_in=256, w_out=128, w_q=128, w_kv=256, w_o=128, w_up=256, w_down=256)
    for k in _BIG:
        outs[k] = _sum_adamw(grads.received[k], w_shards[k], m_shards[k], v_shards[k], name=f"adamw_{k}",
                             tr=tiles[k])

    ssum = _sum_small(small_received)
    loss = ssum[9, 0]
    g_small = {k: ssum[i] for i, k in enumerate(_GAIN_ROWS)}
    g_small["g_attn_out"] = ssum[5, 0:512]
    g_small["g_conv_out"] = ssum[5, 512:1024]
    g_conv = lax.dynamic_slice_in_dim(ssum[6:9, 0:512], me * 64, 64, axis=1)
    m_small = dict(g_mix=m_g_mix, g_attn_out=m_g_attn_out, g_conv_out=m_g_conv_out, g_xattn=m_g_xattn,
                   g_mem=m_g_mem, g_mlp=m_g_mlp, g_final=m_g_final)
    v_small = dict(g_mix=v_g_mix, g_attn_out=v_g_attn_out, g_conv_out=v_g_conv_out, g_xattn=v_g_xattn,
                   g_mem=v_g_mem, g_mlp=v_g_mlp, g_final=v_g_final)
    packed = [_pack_small(g_small, g_conv), _pack_small(gains, conv_w), _pack_small(m_small, m_conv_w),
              _pack_small(v_small, v_conv_w)]
    upd = _adamw_small(*packed)

    def unpack(p):
        res = {k: p[i] for i, k in enumerate(_GAIN_ROWS)}
        res["g_attn_out"] = p[5, 0:512]
        res["g_conv_out"] = p[5, 512:1024]
        res["conv_w"] = p[6, 0:192].reshape(3, 64)
        return res

    g_small["conv_w"] = g_conv
    small_out = [g_small] + [unpack(p) for p in upd]
    names = {"g_mix": "g_mix", "w_in": "w_in", "conv_w": "conv_w", "g_attn_out": "g_attn_out",
             "g_conv_out": "g_conv_out", "w_out": "w_out", "g_xattn": "g_xattn", "g_mem": "g_mem",
             "w_q_mem": "w_q", "w_kv_mem": "w_kv", "w_o_mem": "w_o", "g_mlp": "g_mlp", "w_up": "w_up",
             "w_down": "w_down", "g_final": "g_final"}
    result = [loss, grad_x[None]]
    for which in range(4):
        for key in names.values():
            result.append(outs[key][which] if key in outs else small_out[which][key])
    return tuple(result)
```

```python
import math

import jax
import jax.numpy as jnp
from jax import lax
from jax.experimental import pallas as pl
from jax.experimental.pallas import tpu as pltpu

F32 = jnp.float32
BF16 = jnp.bfloat16
NORM_EPS = 1e-6
NEG_INF = -1e30
N_DEV = 8
BLK = 128
HEAD_DIM = 64
N_MEM_HEADS = 4
ADAM_LR = 0.001
ADAM_B1 = 0.9
ADAM_B2 = 0.999
ADAM_EPS = 1e-08
ADAM_WD = 0.01
ADAM_STEP = 10
MESH = pl.DeviceIdType.MESH
ANY = pl.BlockSpec(memory_space=pl.ANY)


def _dot(a, b):
    return jnp.dot(a, b, preferred_element_type=F32)


def _dot_nt(a, b):
    return lax.dot_general(a, b, (((1,), (1,)), ((), ())), preferred_element_type=F32)


def _dot_tn(a, b):
    return lax.dot_general(a, b, (((0,), (0,)), ((), ())), preferred_element_type=F32)


def _params(semantics, vmem_mb):
    return pltpu.CompilerParams(dimension_semantics=semantics, vmem_limit_bytes=vmem_mb << 20)


def _rms_fwd(x, g):
    r = lax.rsqrt(jnp.mean(x * x, axis=-1, keepdims=True) + NORM_EPS)
    xh = x * r
    return xh * g, xh, r


def _rms_bwd(dy, xh, r, g):
    gy = dy * g
    return r * (gy - xh * jnp.mean(xh * gy, axis=-1, keepdims=True))


def _position():
    x, y, c = lax.axis_index("x"), lax.axis_index("y"), lax.axis_index("c")
    return x, y, c


class _Gather:
    has_mid = True

    def __init__(self, shards):
        self.arrays = list(shards)
        self.n = len(self.arrays)

    def out_shape(self):
        return [jax.ShapeDtypeStruct((N_DEV,) + s.shape, s.dtype) for s in self.arrays]

    def scratch(self):
        return [pltpu.SemaphoreType.DMA((self.n, 7)), pltpu.SemaphoreType.DMA((self.n, 7)),
                pltpu.SemaphoreType.DMA((self.n,))]

    def _ctx(self, ins, outs, sems):
        send_sems, recv_sems, local_sems = sems
        x, y, c = _position()
        me, sibling = (x, y, c), (x, y, 1 - c)
        chips = [(1 - x, y), (x, 1 - y), (1 - x, 1 - y)]

        def lin(px, py, pc):
            return 4 * px + 2 * py + pc

        def copy(a, k, block, to, src=None):
            dst = outs[a].at[lin(*block)]
            return pltpu.make_async_remote_copy(
                src_ref=dst if src is None else src, dst_ref=dst,
                send_sem=send_sems.at[a, k], recv_sem=recv_sems.at[a, k],
                device_id=to, device_id_type=MESH)

        def mine():
            return [pltpu.make_async_copy(ins[a], outs[a].at[lin(*me)], local_sems.at[a]) for a in range(self.n)]

        def first():
            res = []
            for a in range(self.n):
                res.append(copy(a, 0, me, sibling, src=ins[a]))
                res += [copy(a, 1 + j, me, (*chip, c), src=ins[a]) for j, chip in enumerate(chips)]
            return res

        return c, me, sibling, chips, copy, mine, first

    def start(self, ins, outs, sems):
        _, _, _, _, _, mine, first = self._ctx(ins, outs, sems)
        for cp in mine() + first():
            cp.start()

    def mid(self, ins, outs, sems):
        c, me, sibling, chips, copy, _, _ = self._ctx(ins, outs, sems)
        for j, chip in enumerate(chips):
            for a in range(self.n):
                copy(a, 1 + j, (*chip, c), me).wait_recv()
                copy(a, 4 + j, (*chip, c), sibling).start()

    def finish(self, ins, outs, sems):
        c, me, sibling, chips, copy, mine, first = self._ctx(ins, outs, sems)
        for a in range(self.n):
            copy(a, 0, sibling, me).wait_recv()
            for j, chip in enumerate(chips):
                copy(a, 4 + j, (*chip, 1 - c), me).wait_recv()
        for cp in first():
            cp.wait_send()
        for j, chip in enumerate(chips):
            for a in range(self.n):
                copy(a, 4 + j, (*chip, c), sibling).wait_send()
        for cp in mine():
            cp.wait()


class _Exchange:
    has_mid = False

    def __init__(self, parts, bcast=None):
        self.arrays = list(parts)
        self.n = len(self.arrays)
        self.bcast = [False] * self.n if bcast is None else list(bcast)

    def out_shape(self):
        return [jax.ShapeDtypeStruct(((N_DEV,) + p.shape) if b else p.shape, p.dtype)
                for p, b in zip(self.arrays, self.bcast)]

    def scratch(self):
        return [pltpu.SemaphoreType.DMA((self.n, 7)), pltpu.SemaphoreType.DMA((self.n, 7)),
                pltpu.SemaphoreType.DMA((self.n,))]

    def _ctx(self, ins, outs, sems):
        send_sems, recv_sems, local_sems = sems
        x, y, c = _position()
        me = 4 * x + 2 * y + c

        def src(a, j):
            return ins[a] if self.bcast[a] else ins[a].at[j]

        def local():
            return [pltpu.make_async_copy(src(a, me), outs[a].at[me], local_sems.at[a]) for a in range(self.n)]

        def remote(inbound):
            res = []
            for a in range(self.n):
                for k in range(1, N_DEV):
                    peer = (1 - x if k & 4 else x, 1 - y if k & 2 else y, 1 - c if k & 1 else c)
                    plin = 4 * peer[0] + 2 * peer[1] + peer[2]
                    res.append(pltpu.make_async_remote_copy(
                        src_ref=src(a, plin), dst_ref=outs[a].at[plin if inbound else me],
                        send_sem=send_sems.at[a, k - 1], recv_sem=recv_sems.at[a, k - 1],
                        device_id=peer, device_id_type=MESH))
            return res

        return local, remote

    def start(self, ins, outs, sems):
        local, remote = self._ctx(ins, outs, sems)
        for cp in local() + remote(False):
            cp.start()

    def finish(self, ins, outs, sems):
        local, remote = self._ctx(ins, outs, sems)
        for cp in remote(True):
            cp.wait_recv()
        for cp in remote(False):
            cp.wait_send()
        for cp in local():
            cp.wait()


def _comm_call(rider, name):
    n_in, n_out = len(rider.arrays), len(rider.out_shape())

    def body(*refs):
        ins, outs, sems = refs[:n_in], refs[n_in:n_in + n_out], refs[n_in + n_out:]
        rider.start(ins, outs, sems)
        if rider.has_mid:
            rider.mid(ins, outs, sems)
        rider.finish(ins, outs, sems)

    return pl.pallas_call(
        body, name=name, out_shape=rider.out_shape(),
        in_specs=[ANY] * n_in, out_specs=[ANY] * n_out, scratch_shapes=rider.scratch(),
    )(*rider.arrays)


def _pcall(body, *, name, grid, in_specs, out_specs, out_shape, scratch_shapes=(), semantics, vmem_mb, rider=None,
           aliases=None):
    in_specs, out_specs, out_shape = list(in_specs), list(out_specs), list(out_shape)
    scratch_shapes = list(scratch_shapes)
    aliases = dict(aliases or {})
    if rider is None:
        call = pl.pallas_call(body, name=name, grid=grid, in_specs=in_specs, out_specs=out_specs,
                              out_shape=out_shape, scratch_shapes=scratch_shapes, input_output_aliases=aliases,
                              compiler_params=_params(semantics, vmem_mb))
        return lambda *args: (list(call(*args)), None)
    n_in, n_out, n_scr = len(in_specs), len(out_specs), len(scratch_shapes)
    r_in, r_shapes = len(rider.arrays), rider.out_shape()
    r_out = len(r_shapes)
    total = math.prod(grid)
    mid_step = (3 * total) // 4

    def wrapped(*refs):
        bounds = [0, n_in, r_in, n_out, r_out, n_scr]
        for i in range(1, len(bounds)):
            bounds[i] += bounds[i - 1]
        a, ra, o, ro, s = (refs[bounds[i]:bounds[i + 1]] for i in range(5))
        rs = refs[bounds[5]:]
        step = pl.program_id(0)
        for k in range(1, len(grid)):
            step = step * grid[k] + pl.program_id(k)
        pl.when(step == 0)(lambda: rider.start(ra, ro, rs))
        body(*a, *o, *s)
        if rider.has_mid:
            pl.when(step == mid_step)(lambda: rider.mid(ra, ro, rs))
        pl.when(step == total - 1)(lambda: rider.finish(ra, ro, rs))

    call = pl.pallas_call(
        wrapped, name=name, grid=grid, in_specs=in_specs + [ANY] * r_in, out_specs=out_specs + [ANY] * r_out,
        out_shape=out_shape + r_shapes, scratch_shapes=scratch_shapes + rider.scratch(),
        input_output_aliases=aliases, compiler_params=_params(("arbitrary",) * len(grid), vmem_mb))

    def run(*args):
        res = call(*args, *rider.arrays)
        return list(res[:n_out]), list(res[n_out:])

    return run


def _norm_matmul(x, g, w, *, name, out_dtype, tb, blk0=0, nblk=None, relu=False, save_h=False, rider=None):
    t, d = x.shape
    bn = w.shape[2]
    nblk = w.shape[0] if nblk is None else nblk

    def body(x_ref, g_ref, w_ref, o_ref, *rest):
        h_scr = rest[-1]

        @pl.when(pl.program_id(1) == 0)
        def _():
            h = _rms_fwd(x_ref[...], g_ref[...])[0].astype(BF16)
            h_scr[...] = h
            if save_h:
                rest[0][...] = h

        acc = _dot(h_scr[...], w_ref[0])
        if relu:
            acc = jnp.maximum(acc, 0.0)
        o_ref[...] = acc.astype(out_dtype)

    out_shape = [jax.ShapeDtypeStruct((t, nblk * bn), out_dtype)]
    out_specs = [pl.BlockSpec((tb, bn), lambda i, j: (i, j))]
    if save_h:
        out_shape.append(jax.ShapeDtypeStruct((t, d), BF16))
        out_specs.append(pl.BlockSpec((tb, d), lambda i, j: (i, 0)))
    res, extra = _pcall(
        body, name=name, grid=(t // tb, nblk),
        in_specs=[pl.BlockSpec((tb, d), lambda i, j: (i, 0)),
                  pl.BlockSpec((1, d), lambda i, j: (0, 0)),
                  pl.BlockSpec((1, d, bn), lambda i, j: (j + blk0, 0, 0))],
        out_specs=out_specs, out_shape=out_shape,
        scratch_shapes=[pltpu.VMEM((tb, d), BF16)],
        semantics=("parallel", "arbitrary"), vmem_mb=40, rider=rider,
    )(x, g, w)
    res = res if save_h else res[0]
    return res if rider is None else (res, extra)


def _matmul_nt_normbwd(dy, w, x, g, dres, *, name, tb, stacked=False, rider=None):
    t, d = x.shape
    if stacked:
        nblk, _, bn = dy.shape
    else:
        nblk, _, bn = w.shape
    n_i = t // tb
    has_res = dres is not None

    def body(dy_ref, w_ref, x_ref, g_ref, *rest):
        if has_res:
            dres_ref, dx_ref, gg_ref, acc = rest
        else:
            dx_ref, gg_ref, acc = rest
        i, j = pl.program_id(0), pl.program_id(1)

        @pl.when(j == 0)
        def _():
            acc[...] = jnp.zeros_like(acc)

        acc[...] += _dot_nt(dy_ref[...], w_ref[...] if stacked else w_ref[0])

        @pl.when(j == nblk - 1)
        def _():
            g_v = g_ref[...]
            _, xh, r = _rms_fwd(x_ref[...], g_v)
            dh = acc[...]
            dx = _rms_bwd(dh, xh, r, g_v)
            if has_res:
                dx = dx + dres_ref[...]
            dx_ref[...] = dx
            part = jnp.sum(dh * xh, axis=0, keepdims=True)

            @pl.when(i == 0)
            def _():
                gg_ref[...] = part

            @pl.when(i != 0)
            def _():
                gg_ref[...] += part

    if stacked:
        in_specs = [pl.BlockSpec((None, tb, bn), lambda i, j: (j, i, 0)), pl.BlockSpec((d, bn), lambda i, j: (0, j))]
    else:
        in_specs = [pl.BlockSpec((tb, bn), lambda i, j: (i, j)), pl.BlockSpec((1, d, bn), lambda i, j: (j, 0, 0))]
    in_specs += [pl.BlockSpec((tb, d), lambda i, j: (i, 0)),
                 pl.BlockSpec((1, d), lambda i, j: (0, 0))]
    args = [dy, w, x, g]
    if has_res:
        in_specs.append(pl.BlockSpec((tb, d), lambda i, j: (i, 0)))
        args.append(dres)
    res, extra = _pcall(
        body, name=name, grid=(n_i, nblk),
        in_specs=in_specs,
        out_specs=[pl.BlockSpec((tb, d), lambda i, j: (i, 0)), pl.BlockSpec((1, d), lambda i, j: (0, 0))],
        out_shape=[jax.ShapeDtypeStruct((t, d), F32), jax.ShapeDtypeStruct((1, d), F32)],
        scratch_shapes=[pltpu.VMEM((tb, d), F32)],
        semantics=("arbitrary", "arbitrary"), vmem_mb=40, rider=rider,
    )(*args)
    return res if rider is None else (res, extra)


def _matmul_tn(a, b, *, name, bm, bn, bt, col_sharded=False, square_a=False, rider=None):
    t, m = a.shape
    stacked = b.ndim == 3
    n = b.shape[0] * bn if stacked else b.shape[1]
    n_t = t // bt

    def body(a_ref, b_ref, o_ref, acc):
        k = pl.program_id(2)

        @pl.when(k == 0)
        def _():
            acc[...] = jnp.zeros_like(acc)

        av = a_ref[...]
        if square_a:
            av = av.astype(F32)
            av = av * av
        acc[...] += _dot_tn(av.astype(BF16), b_ref[...].astype(BF16))

        @pl.when(k == n_t - 1)
        def _():
            if col_sharded:
                o_ref[0] = acc[...].astype(BF16)
            else:
                o_ref[...] = acc[...].astype(BF16)

    if col_sharded:
        per = (n // N_DEV) // bn
        out_shape = jax.ShapeDtypeStruct((N_DEV, m, n // N_DEV), BF16)
        out_spec = pl.BlockSpec((1, bm, bn), lambda i, j, k: (j // per, i, j % per))
    else:
        out_shape = jax.ShapeDtypeStruct((m, n), BF16)
        out_spec = pl.BlockSpec((bm, bn), lambda i, j, k: (i, j))
    res, extra = _pcall(
        body, name=name, grid=(m // bm, n // bn, n_t),
        in_specs=[pl.BlockSpec((bt, bm), lambda i, j, k: (k, i)),
                  pl.BlockSpec((None, bt, bn), lambda i, j, k: (j, k, 0)) if stacked
                  else pl.BlockSpec((bt, bn), lambda i, j, k: (k, j))],
        out_specs=[out_spec], out_shape=[out_shape],
        scratch_shapes=[pltpu.VMEM((bm, bn), F32)],
        semantics=("parallel", "parallel", "arbitrary"), vmem_mb=40, rider=rider,
    )(a, b)
    return res[0] if rider is None else (res[0], extra)


N_RES = 16
SEG = 128
HALF = N_RES * SEG
TI = 16


def _x4(a):
    return a.reshape(a.shape[0] // HALF, N_RES, SEG, a.shape[1])


def _reorder(a, inverse, name):
    t, c = a.shape
    n_i = SEG // TI
    natural = pl.BlockSpec((TI * N_RES, c), lambda s: (s, 0))
    major = pl.BlockSpec((1, N_RES, TI, c), lambda s: (s // n_i, 0, s % n_i, 0))

    def body(i_ref, o_ref, scr):
        for cb in range(c // BLK):
            cols = slice(cb * BLK, (cb + 1) * BLK)
            slab = scr.at[cb]
            if inverse:
                for r in range(N_RES):
                    slab[pl.ds(r, TI, stride=N_RES), :] = i_ref[0, r, :, cols]
                o_ref[:, cols] = slab[...]
            else:
                slab[...] = i_ref[:, cols]
                for r in range(N_RES):
                    o_ref[0, r, :, cols] = slab[pl.ds(r, TI, stride=N_RES), :]

    scratch = [pltpu.VMEM((c // BLK, TI * N_RES, BLK), a.dtype)]
    if inverse:
        return pl.pallas_call(
            body, name=name, grid=(t // (TI * N_RES),), in_specs=[major], out_specs=natural,
            out_shape=jax.ShapeDtypeStruct((t, c), a.dtype), scratch_shapes=scratch,
            compiler_params=_params(("parallel",), 32))(_x4(a))
    return pl.pallas_call(
        body, name=name, grid=(t // (TI * N_RES),), in_specs=[natural], out_specs=major,
        out_shape=jax.ShapeDtypeStruct((t // HALF, N_RES, SEG, c), a.dtype), scratch_shapes=scratch,
        compiler_params=_params(("parallel",), 32))(a).reshape(t, c)


_PATTERNS = ((1, 16, 8, SEG), (4, 4, 32, 4 * SEG), (16, 1, SEG, 0))
_FIRST = {1: 1, 4: 4, 16: 16}


def _group_rows(d, g):
    a = g >> 4
    if d == 16:
        base = a * HALF + (g & 15) * SEG
        prev = base - HALF
    elif d == 4:
        c = (g >> 2) & 3
        base = a * HALF + (g & 3) * SEG + c * 32
        prev = jnp.where(c > 0, base - 32, base - HALF + 96)
    else:
        c = g & 15
        base = a * HALF + c * 8
        prev = jnp.where(c > 0, base - 8, base - HALF + 120)
    return base, prev


def _load_rows(ref, base, n, rows, stride):
    parts = [ref[pl.ds(pl.multiple_of(base + j * stride, 8), rows), :] for j in range(n)]
    return parts[0] if n == 1 else jnp.concatenate(parts, axis=0)


def _store_rows(ref, base, val, n, rows, stride, add=False):
    for j in range(n):
        sl = pl.ds(pl.multiple_of(base + j * stride, 8), rows)
        piece = val[j * rows:(j + 1) * rows, :]
        if add:
            ref[sl, :] += piece
        else:
            ref[sl, :] = piece


def _band_bias(n, rows):
    shift = rows.bit_length() - 1
    lq = lax.broadcasted_iota(jnp.int32, (BLK, BLK), 0)
    lk = lax.broadcasted_iota(jnp.int32, (BLK, BLK), 1)
    iq = (lq & (rows - 1)) * n + (lq >> shift)
    ik = (lk & (rows - 1)) * n + (lk >> shift)
    zero = jnp.zeros((BLK, BLK), F32)
    return jnp.where(ik >= iq, zero, NEG_INF), jnp.where(ik <= iq, zero, NEG_INF)


def _set_bias(bias_scr, n, rows):
    prev_b, cur_b = _band_bias(n, rows)
    for half in range(2):
        bias_scr[half * BLK:(half + 1) * BLK, 0:BLK] = prev_b
        bias_scr[half * BLK:(half + 1) * BLK, BLK:2 * BLK] = cur_b


def _head_consts():
    lane_lo = lax.broadcasted_iota(jnp.int32, (BLK, BLK), 1) < HEAD_DIM
    return lane_lo, [jnp.where(lane_lo, 1.0, 0.0).astype(BF16), jnp.where(lane_lo, 0.0, 1.0).astype(BF16)]


def _stack_heads(v, head_mask):
    return jnp.concatenate([v * head_mask[0], v * head_mask[1]], axis=0)


def _unstack_heads(v2, lane_lo):
    return jnp.where(lane_lo, v2[:BLK], v2[BLK:])


def _rows_per_head(v, lane_lo):
    rolled = pltpu.roll(v, HEAD_DIM, axis=1)
    return jnp.concatenate([jnp.where(lane_lo, v, rolled), jnp.where(lane_lo, rolled, v)], axis=0)


def _loop(lo, hi, fn, pair=False):
    if pair:
        n2 = (hi - lo) // 2

        def body2(i, carry):
            fn(lo + 2 * i)
            fn(lo + 2 * i + 1)
            return carry

        if n2:
            lax.fori_loop(0, n2, body2, 0)
        lo = lo + 2 * n2

    def body(g, carry):
        fn(g)
        return carry

    if hi > lo:
        lax.fori_loop(lo, hi, body, 0)


def _mix_weights(l1, l2, l3):
    mx = jnp.maximum(jnp.maximum(l1, l2), l3)
    e1, e2, e3 = jnp.exp(l1 - mx), jnp.exp(l2 - mx), jnp.exp(l3 - mx)
    inv = 1.0 / (e1 + e2 + e3)
    return e1 * inv, e2 * inv, e3 * inv


def _attention_fwd(qkv, rider=None):
    t = qkv.shape[0]
    groups = 16 * (t // HALF)

    def body(q_ref, k_ref, v_ref, attn_ref, l1_ref, l2_ref, l3_ref, o_scr, bias_scr):
        lane_lo, head_mask = _head_consts()
        l_refs = (l1_ref, l2_ref, l3_ref)
        for p, (d, n, rows, stride) in enumerate(_PATTERNS):
            _set_bias(bias_scr, n, rows)
            o_p, l_p = o_scr.at[p], l_refs[p]

            def block(g, has_prev):
                base, prev = _group_rows(d, g)

                def load(ref, b):
                    return _load_rows(ref, b, n, rows, stride).astype(BF16)

                q2 = _stack_heads(load(q_ref, base), head_mask)
                k2, v2 = load(k_ref, base), load(v_ref, base)
                if has_prev:
                    k2 = jnp.concatenate([load(k_ref, prev), k2], axis=0)
                    v2 = jnp.concatenate([load(v_ref, prev), v2], axis=0)
                    bias = bias_scr[...]
                else:
                    bias = bias_scr[:, BLK:2 * BLK]
                s = _dot_nt(q2, k2) * 0.125 + bias
                mx = jnp.max(s, axis=1, keepdims=True)
                e = jnp.exp(s - mx)
                den = jnp.sum(e, axis=1, keepdims=True)
                o2 = _dot((e * (1.0 / den)).astype(BF16), v2)
                lse2 = jnp.broadcast_to(mx + jnp.log(den), (2 * BLK, BLK))
                _store_rows(o_p, base, _unstack_heads(o2, lane_lo), n, rows, stride)
                _store_rows(l_p, base, _unstack_heads(lse2, lane_lo), n, rows, stride)

            _loop(0, _FIRST[d], lambda g: block(g, False))
            _loop(_FIRST[d], groups, lambda g: block(g, True), pair=True)

        def mix(i):
            sl = pl.ds(pl.multiple_of(i * 256, 256), 256)
            w = _mix_weights(l1_ref[sl, :], l2_ref[sl, :], l3_ref[sl, :])
            attn_ref[sl, :] = w[0] * o_scr[0, sl, :] + w[1] * o_scr[1, sl, :] + w[2] * o_scr[2, sl, :]

        _loop(0, t // 256, mix)

    def col(c0):
        return pl.BlockSpec((t, BLK), lambda hp: (0, c0 + hp))

    res, extra = _pcall(
        body, name="attention_fwd", grid=(4,), in_specs=[col(0), col(4), col(8)], out_specs=[col(0)] * 4,
        out_shape=[jax.ShapeDtypeStruct((t, 512), F32)] * 4,
        scratch_shapes=[pltpu.VMEM((3, t, BLK), F32), pltpu.VMEM((2 * BLK, 2 * BLK), F32)],
        semantics=("parallel",), vmem_mb=48, rider=rider,
    )(qkv, qkv, qkv)
    return res if rider is None else (res, extra)


def _attention_bwd(qkv, dattn, dsum, lses, dproj, rider=None):
    t = qkv.shape[0]
    groups = 16 * (t // HALF)

    def body(q_ref, k_ref, v_ref, da_ref, ds_ref, l1_ref, l2_ref, l3_ref, kept_ref, out_ref, acc, bias_scr):
        del kept_ref
        lane_lo, head_mask = _head_consts()
        l_refs = (l1_ref, l2_ref, l3_ref)

        def clear(i):
            sl = pl.ds(pl.multiple_of(i * 512, 512), 512)
            for s in range(3):
                acc[s, sl, :] = jnp.zeros((512, BLK), F32)

        _loop(0, t // 512, clear)
        dq_acc, dk_acc, dv_acc = acc.at[0], acc.at[1], acc.at[2]
        for p, (d, n, rows, stride) in enumerate(_PATTERNS):
            _set_bias(bias_scr, n, rows)

            def block(g, has_prev):
                base, prev = _group_rows(d, g)

                def load(ref, b):
                    return _load_rows(ref, b, n, rows, stride)

                def put(ref, b, val):
                    _store_rows(ref, b, val, n, rows, stride, add=True)

                lse = [load(ref, base) for ref in l_refs]
                w = _mix_weights(*lse)[p]
                do2 = _stack_heads((w * load(da_ref, base)).astype(BF16), head_mask)
                dl2 = _rows_per_head(w * load(ds_ref, base), lane_lo)
                lse2 = _rows_per_head(lse[p], lane_lo)
                q2 = _stack_heads(load(q_ref, base).astype(BF16), head_mask)
                k2, v2 = load(k_ref, base).astype(BF16), load(v_ref, base).astype(BF16)
                if has_prev:
                    k2 = jnp.concatenate([load(k_ref, prev).astype(BF16), k2], axis=0)
                    v2 = jnp.concatenate([load(v_ref, prev).astype(BF16), v2], axis=0)
                    bias = bias_scr[...]
                    lse2 = jnp.concatenate([lse2, lse2], axis=1)
                    dl2 = jnp.concatenate([dl2, dl2], axis=1)
                else:
                    bias = bias_scr[:, BLK:2 * BLK]
                pr = jnp.exp(_dot_nt(q2, k2) * 0.125 + bias - lse2)
                ds = (pr * (_dot_nt(do2, v2) - dl2) * 0.125).astype(BF16)
                put(dq_acc, base, _unstack_heads(_dot(ds, k2), lane_lo))
                dk2 = _dot_tn(ds, q2)
                dv2 = _dot_tn(pr.astype(BF16), do2)
                if has_prev:
                    put(dk_acc, prev, dk2[:BLK])
                    put(dv_acc, prev, dv2[:BLK])
                    put(dk_acc, base, dk2[BLK:])
                    put(dv_acc, base, dv2[BLK:])
                else:
                    put(dk_acc, base, dk2)
                    put(dv_acc, base, dv2)

            _loop(0, _FIRST[d], lambda g: block(g, False))
            _loop(_FIRST[d], groups, lambda g: block(g, True), pair=True)

        def emit(i):
            sl = pl.ds(pl.multiple_of(i * 512, 512), 512)
            for s in range(3):
                out_ref[s, sl, :] = acc[s, sl, :].astype(BF16)

        _loop(0, t // 512, emit)

    def col(c0):
        return pl.BlockSpec((t, BLK), lambda hp: (0, c0 + hp))

    res, extra = _pcall(
        body, name="attention_bwd", grid=(4,),
        in_specs=[col(0), col(4), col(8)] + [col(0)] * 5 + [ANY],
        out_specs=[pl.BlockSpec((3, t, BLK), lambda hp: (0, 0, hp))],
        out_shape=[jax.ShapeDtypeStruct(dproj.shape, BF16)],
        scratch_shapes=[pltpu.VMEM((3, t, BLK), F32), pltpu.VMEM((2 * BLK, 2 * BLK), F32)],
        semantics=("parallel",), vmem_mb=56, rider=rider, aliases={8: 0},
    )(qkv, qkv, qkv, dattn, dsum, *lses, dproj)
    return res[0] if rider is None else (res[0], extra)


def _order_specs(t):
    n_i = SEG // TI
    nblk = (t // HALF) * n_i
    per = TI // 8

    def main(c):
        return pl.BlockSpec((1, N_RES, TI, c), lambda s: (s // n_i, 0, s % n_i, 0))

    def before(c):
        return pl.BlockSpec((1, 2, 8, c), lambda s: (jnp.maximum(s - 1, 0) // n_i, N_RES // 2 - 1,
                                                     (jnp.maximum(s - 1, 0) % n_i) * per + per - 1, 0))

    def after(c):
        return pl.BlockSpec((1, 2, 8, c), lambda s: (jnp.minimum(s + 1, nblk - 1) // n_i, 0,
                                                     (jnp.minimum(s + 1, nblk - 1) % n_i) * per, 0))

    return nblk, main, before, after


def _shift_in(v, row_in, up):
    rows = v.shape[0]
    idx = lax.broadcasted_iota(jnp.int32, v.shape, 0)
    fill = jnp.broadcast_to(row_in, v.shape)
    if up:
        return jnp.where(idx == rows - 1, fill, pltpu.roll(v, rows - 1, axis=0))
    return jnp.where(idx == 0, fill, pltpu.roll(v, 1, axis=0))


def _taps_behind(u, before):
    s15 = _shift_in(u[N_RES - 1], before[1, 7:8, :], up=False)
    s14 = _shift_in(u[N_RES - 2], before[0, 7:8, :], up=False)
    m1 = jnp.concatenate([s15[None], u[:N_RES - 1]], axis=0)
    m2 = jnp.concatenate([s14[None], s15[None], u[:N_RES - 2]], axis=0)
    return m1, m2


def _taps_ahead(u, after):
    t0 = _shift_in(u[0], after[0, 0:1, :], up=True)
    t1 = _shift_in(u[1], after[1, 0:1, :], up=True)
    p1 = jnp.concatenate([u[1:], t0[None]], axis=0)
    p2 = jnp.concatenate([u[2:], t0[None], t1[None]], axis=0)
    return p1, p2


def _conv_fwd(gates, before, first, cw):
    bg, cg, xc = gates[..., 0:512], gates[..., 512:1024], gates[..., 1024:1536]
    u = cg * xc
    ub = before[..., 512:1024] * before[..., 1024:1536]
    ub = jnp.where(first, jnp.zeros_like(ub), ub)
    m1, m2 = _taps_behind(u, ub)
    conv = m2 * cw[0:1, :] + m1 * cw[1:2, :] + u * cw[2:3, :]
    return bg, u, m1, m2, conv


def _sum_tokens(v):
    return jnp.sum(jnp.sum(v, axis=0), axis=0, keepdims=True)


def _mixer_fwd(x, attn, gates, cw, g_a, g_c, w_out):
    t, d = x.shape
    nblk, main, before, _ = _order_specs(t)
    rows = N_RES * TI

    def body(x_ref, at_ref, gt_ref, gb_ref, cw_ref, ga_ref, gc_ref, wa_ref, wb_ref, x1_ref, mg_ref):
        an = _rms_fwd(at_ref[0], ga_ref[...])[0].astype(BF16)
        bg, _, _, _, conv = _conv_fwd(gt_ref[0], gb_ref[0], pl.program_id(0) == 0, cw_ref[...])
        cn = _rms_fwd(bg * conv, gc_ref[...])[0].astype(BF16)
        mg_ref[0, :, :, 0:512] = an
        mg_ref[0, :, :, 512:1024] = cn
        y = _dot(an.reshape(rows, 512), wa_ref[...]) + _dot(cn.reshape(rows, 512), wb_ref[...])
        x1_ref[0] = x_ref[0] + y.reshape(N_RES, TI, d)

    const = lambda r, c, i0=0: pl.BlockSpec((r, c), lambda s: (i0, 0))
    x1, merged = pl.pallas_call(
        body, name="mixer_fwd", grid=(nblk,),
        in_specs=[main(d), main(512), main(1536), before(1536), const(3, 512), const(1, 512), const(1, 512),
                  const(512, d), const(512, d, 1)],
        out_specs=[main(d), main(d)],
        out_shape=[jax.ShapeDtypeStruct(_x4(x).shape, F32), jax.ShapeDtypeStruct(_x4(x).shape, BF16)],
        compiler_params=_params(("parallel",), 48),
    )(_x4(x), _x4(attn), _x4(gates), _x4(gates), cw, g_a, g_c, w_out, w_out)
    return x1.reshape(t, d), merged.reshape(t, d)


def _mixer_bwd(dx1, attn, gates, cw, g_a, g_c, w_out, head_sum, rider=None):
    t, d = dx1.shape
    nblk, main, before, _ = _order_specs(t)
    rows = N_RES * TI

    def body(dx_ref, at_ref, gt_ref, gb_ref, cw_ref, ga_ref, gc_ref, wa_ref, wb_ref, hs_ref,
             da_ref, dsum_ref, dy_ref, gga_ref, ggc_ref):
        s = pl.program_id(0)
        dxb = dx_ref[0].reshape(rows, d).astype(BF16)
        dma = _dot_nt(dxb, wa_ref[...]).reshape(N_RES, TI, 512)
        dmc = _dot_nt(dxb, wb_ref[...]).reshape(N_RES, TI, 512)
        attn_v, g_av = at_ref[0], ga_ref[...]
        _, ah, ra = _rms_fwd(attn_v, g_av)
        dattn = _rms_bwd(dma, ah, ra, g_av)
        da_ref[0] = dattn
        z = (dattn * attn_v).reshape(rows, 512)
        hs = hs_ref[...]
        z1 = z.astype(BF16)
        z2 = (z - z1.astype(F32)).astype(BF16)
        z3 = (z - z1.astype(F32) - z2.astype(F32)).astype(BF16)
        dsum_ref[0] = (_dot(z1, hs) + _dot(z2, hs) + _dot(z3, hs)).reshape(N_RES, TI, 512)
        bg, _, _, _, conv = _conv_fwd(gt_ref[0], gb_ref[0], s == 0, cw_ref[...])
        g_cv = gc_ref[...]
        _, yh, rc = _rms_fwd(bg * conv, g_cv)
        dy_ref[0] = _rms_bwd(dmc, yh, rc, g_cv)
        pa, pc = _sum_tokens(dma * ah), _sum_tokens(dmc * yh)

        @pl.when(s == 0)
        def _():
            gga_ref[...] = pa
            ggc_ref[...] = pc

        @pl.when(s != 0)
        def _():
            gga_ref[...] += pa
            ggc_ref[...] += pc

    const = lambda r, c, i0=0: pl.BlockSpec((r, c), lambda s: (i0, 0))
    shape4 = _x4(attn).shape
    res, extra = _pcall(
        body, name="mixer_bwd", grid=(nblk,),
        in_specs=[main(d), main(512), main(1536), before(1536), const(3, 512), const(1, 512), const(1, 512),
                  const(512, d), const(512, d, 1), const(512, 512)],
        out_specs=[main(512)] * 3 + [const(1, 512), const(1, 512)],
        out_shape=[jax.ShapeDtypeStruct(shape4, F32)] * 3 + [jax.ShapeDtypeStruct((1, 512), F32)] * 2,
        semantics=("arbitrary",), vmem_mb=48, rider=rider,
    )(_x4(dx1), _x4(attn), _x4(gates), _x4(gates), cw, g_a, g_c, w_out, w_out, head_sum)
    res = [r.reshape(t, 512) for r in res[:3]] + res[3:]
    return res if rider is None else (res, extra)


def _conv_bwd(dy, gates, cw):
    t = dy.shape[0]
    nblk, main, before, after = _order_specs(t)
    n_i = SEG // TI

    def body(dy_ref, dya_ref, gt_ref, gb_ref, ga_ref, cw_ref, dp_ref, gcw_ref):
        s = pl.program_id(0)
        cw_v, gates_v = cw_ref[...], gt_ref[0]
        bg, u, m1, m2, conv = _conv_fwd(gates_v, gb_ref[0], s == 0, cw_v)
        dy_v = dy_ref[0]
        dconv = dy_v * bg
        dca = dya_ref[0] * ga_ref[0][..., 0:512]
        dca = jnp.where(s == nblk - 1, jnp.zeros_like(dca), dca)
        p1, p2 = _taps_ahead(dconv, dca)
        du = dconv * cw_v[2:3, :] + p1 * cw_v[1:2, :] + p2 * cw_v[0:1, :]
        dp_ref[0, 0] = (dy_v * conv).astype(BF16)
        dp_ref[1, 0] = (du * gates_v[..., 1024:1536]).astype(BF16)
        dp_ref[2, 0] = (du * gates_v[..., 512:1024]).astype(BF16)
        parts = [_sum_tokens(dconv * m2), _sum_tokens(dconv * m1), _sum_tokens(dconv * u)]

        @pl.when(s == 0)
        def _():
            gcw_ref[...] = jnp.zeros_like(gcw_ref)

        for tap in range(3):
            gcw_ref[tap:tap + 1, :] += parts[tap]

    dproj, gcw = pl.pallas_call(
        body, name="conv_bwd", grid=(nblk,),
        in_specs=[main(512), after(512), main(1536), before(1536), after(1536),
                  pl.BlockSpec((3, 512), lambda s: (0, 0))],
        out_specs=[pl.BlockSpec((3, 1, N_RES, TI, 512), lambda s: (1, s // n_i, 0, s % n_i, 0)),
                   pl.BlockSpec((8, 512), lambda s: (0, 0))],
        out_shape=[jax.ShapeDtypeStruct((6, t // HALF, N_RES, SEG, 512), BF16), jax.ShapeDtypeStruct((8, 512), F32)],
        compiler_params=_params(("arbitrary",), 40),
    )(_x4(dy), _x4(dy), _x4(gates), _x4(gates), _x4(gates), cw)
    return dproj.reshape(6, t, 512), gcw


def _xattn_fwd(x1, g, w_q, kv, w_o, *, tb):
    t, d = x1.shape
    hd = d // N_MEM_HEADS
    m = kv.shape[0]

    def body(x_ref, g_ref, wq_ref, k_ref, v_ref, wo_ref, x2_ref, h_ref, q_ref, o_ref):
        xv = x_ref[...]
        h = _rms_fwd(xv, g_ref[...])[0].astype(BF16)
        h_ref[...] = h
        q = _dot(h, wq_ref[...]).astype(BF16)
        q_ref[...] = q
        for hh in range(N_MEM_HEADS):
            sl = slice(hh * hd, (hh + 1) * hd)
            s = _dot_nt(q[:, sl], k_ref[:, sl]) * (1.0 / 16.0)
            e = jnp.exp(s - jnp.max(s, axis=1, keepdims=True))
            p = e / jnp.sum(e, axis=1, keepdims=True)
            o_ref[:, sl] = _dot(p.astype(BF16), v_ref[:, sl]).astype(BF16)
        x2_ref[...] = xv + _dot(o_ref[...], wo_ref[...])

    tok = pl.BlockSpec((tb, d), lambda i: (i, 0))
    full = pl.BlockSpec((d, d), lambda i: (0, 0))
    return pl.pallas_call(
        body, name="xattn_fwd", grid=(t // tb,),
        in_specs=[tok, pl.BlockSpec((1, d), lambda i: (0, 0)), full,
                  pl.BlockSpec((m, d), lambda i: (0, 0)), pl.BlockSpec((m, d), lambda i: (0, 1)), full],
        out_specs=[tok] * 4,
        out_shape=[jax.ShapeDtypeStruct((t, d), F32)] + [jax.ShapeDtypeStruct((t, d), BF16)] * 3,
        compiler_params=_params(("parallel",), 48),
    )(x1, g, w_q, kv, kv, w_o)


def _xattn_bwd(dx2, x1, g, q, w_q, kv, w_o, *, tb, rider=None):
    t, d = x1.shape
    hd = d // N_MEM_HEADS
    m = kv.shape[0]

    def body(dx2_ref, x_ref, g_ref, q_ref, wq_ref, k_ref, v_ref, wo_ref, dx1_ref, dq_ref, dk_ref, dv_ref, gg_ref):
        i = pl.program_id(0)

        @pl.when(i == 0)
        def _():
            dk_ref[...] = jnp.zeros_like(dk_ref)
            dv_ref[...] = jnp.zeros_like(dv_ref)

        dx2 = dx2_ref[...]
        do = _dot_nt(dx2.astype(BF16), wo_ref[...]).astype(BF16)
        for hh in range(N_MEM_HEADS):
            sl = slice(hh * hd, (hh + 1) * hd)
            qh, kh, vh, doh = q_ref[:, sl], k_ref[:, sl], v_ref[:, sl], do[:, sl]
            s = _dot_nt(qh, kh) * (1.0 / 16.0)
            e = jnp.exp(s - jnp.max(s, axis=1, keepdims=True))
            p = e / jnp.sum(e, axis=1, keepdims=True)
            dp = _dot_nt(doh, vh)
            ds = (p * (dp - jnp.sum(dp * p, axis=1, keepdims=True)) * (1.0 / 16.0)).astype(BF16)
            dq_ref[:, sl] = _dot(ds, kh).astype(BF16)
            dk_ref[:, sl] += _dot_tn(ds, qh)
            dv_ref[:, sl] += _dot_tn(p.astype(BF16), doh)
        dh = _dot_nt(dq_ref[...], wq_ref[...])
        g_v = g_ref[...]
        _, xh, r = _rms_fwd(x_ref[...], g_v)
        dx1_ref[...] = dx2 + _rms_bwd(dh, xh, r, g_v)
        part = jnp.sum(dh * xh, axis=0, keepdims=True)

        @pl.when(i == 0)
        def _():
            gg_ref[...] = part

        @pl.when(i != 0)
        def _():
            gg_ref[...] += part

    tok = pl.BlockSpec((tb, d), lambda i: (i, 0))
    full = pl.BlockSpec((d, d), lambda i: (0, 0))
    acc = pl.BlockSpec((m, d), lambda i: (0, 0))
    res, extra = _pcall(
        body, name="xattn_bwd", grid=(t // tb,),
        in_specs=[tok, tok, pl.BlockSpec((1, d), lambda i: (0, 0)), tok, full,
                  pl.BlockSpec((m, d), lambda i: (0, 0)), pl.BlockSpec((m, d), lambda i: (0, 1)), full],
        out_specs=[tok, tok, acc, acc, pl.BlockSpec((1, d), lambda i: (0, 0))],
        out_shape=[jax.ShapeDtypeStruct((t, d), F32), jax.ShapeDtypeStruct((t, d), BF16),
                   jax.ShapeDtypeStruct((m, d), F32), jax.ShapeDtypeStruct((m, d), F32),
                   jax.ShapeDtypeStruct((1, d), F32)],
        semantics=("arbitrary",), vmem_mb=48, rider=rider,
    )(dx2, x1, g, q, w_q, kv, kv, w_o)
    return res if rider is None else (res, extra)


def _mlp_down_loss(a, w_down, x2, tgt, g, *, tb):
    t, d = x2.shape
    f = a.shape[1]

    def body(a_ref, w_ref, x_ref, t_ref, g_ref, dx_ref, loss_ref, gg_ref):
        i = pl.program_id(0)
        av = a_ref[...].astype(F32)
        x3 = x_ref[...] + _dot((av * av).astype(BF16), w_ref[...])
        g_v = g_ref[...]
        out, xh, r = _rms_fwd(x3, g_v)
        err = out - t_ref[...]
        dout = err * (1.0 / d)
        dx_ref[...] = _rms_bwd(dout, xh, r, g_v)
        part = jnp.sum(dout * xh, axis=0, keepdims=True)
        lpart = 0.5 * jnp.sum(jnp.mean(err * err, axis=-1, keepdims=True), axis=0, keepdims=True)
        lpart = jnp.broadcast_to(lpart, loss_ref.shape)

        @pl.when(i == 0)
        def _():
            gg_ref[...] = part
            loss_ref[...] = lpart

        @pl.when(i != 0)
        def _():
            gg_ref[...] += part
            loss_ref[...] += lpart

    tok = pl.BlockSpec((tb, d), lambda i: (i, 0))
    return pl.pallas_call(
        body, name="mlp_down_loss", grid=(t // tb,),
        in_specs=[pl.BlockSpec((tb, f), lambda i: (i, 0)), pl.BlockSpec((f, d), lambda i: (0, 0)), tok, tok,
                  pl.BlockSpec((1, d), lambda i: (0, 0))],
        out_specs=[tok, pl.BlockSpec((8, 128), lambda i: (0, 0)), pl.BlockSpec((1, d), lambda i: (0, 0))],
        out_shape=[jax.ShapeDtypeStruct((t, d), F32), jax.ShapeDtypeStruct((8, 128), F32),
                   jax.ShapeDtypeStruct((1, d), F32)],
        compiler_params=_params(("arbitrary",), 56),
    )(a, w_down, x2, tgt, g)


def _mlp_dpre(dx3, w_down, a, *, tb):
    t, d = dx3.shape
    nblk, bn, _ = w_down.shape

    def body(dx_ref, w_ref, a_ref, o_ref, dxb):
        @pl.when(pl.program_id(1) == 0)
        def _():
            dxb[...] = dx_ref[...].astype(BF16)

        o_ref[...] = (2.0 * a_ref[...].astype(F32) * _dot_nt(dxb[...], w_ref[0])).astype(BF16)

    return pl.pallas_call(
        body, name="mlp_dpre", grid=(t // tb, nblk),
        in_specs=[pl.BlockSpec((tb, d), lambda i, j: (i, 0)), pl.BlockSpec((1, bn, d), lambda i, j: (j, 0, 0)),
                  pl.BlockSpec((tb, bn), lambda i, j: (i, j))],
        out_specs=pl.BlockSpec((tb, bn), lambda i, j: (i, j)),
        out_shape=jax.ShapeDtypeStruct((t, nblk * bn), BF16),
        scratch_shapes=[pltpu.VMEM((tb, d), BF16)],
        compiler_params=_params(("parallel", "arbitrary"), 40),
    )(dx3, w_down, a)


def _adamw(gsum, w, m, v):
    m_new = ADAM_B1 * m + (1.0 - ADAM_B1) * gsum
    v_new = ADAM_B2 * v + (1.0 - ADAM_B2) * (gsum * gsum)
    m_hat = m_new / (1.0 - ADAM_B1 ** ADAM_STEP)
    v_hat = v_new / (1.0 - ADAM_B2 ** ADAM_STEP)
    delta = -ADAM_LR * (m_hat / (jnp.sqrt(v_hat) + ADAM_EPS) + ADAM_WD * w)
    return delta, m_new, v_new


def _sum_adamw(parts, w, m, v, *, name, tr):
    r, c = w.shape

    def body(p_ref, w_ref, m_ref, v_ref, g_ref, d_ref, mo_ref, vo_ref):
        g = p_ref[0].astype(F32)
        for k in range(1, N_DEV):
            g = g + p_ref[k].astype(F32)
        g_ref[...] = g
        d_ref[...], mo_ref[...], vo_ref[...] = _adamw(g, w_ref[...], m_ref[...], v_ref[...])

    blk = pl.BlockSpec((tr, c), lambda i: (i, 0))
    return pl.pallas_call(
        body, name=name, grid=(r // tr,),
        in_specs=[pl.BlockSpec((N_DEV, tr, c), lambda i: (0, i, 0)), blk, blk, blk],
        out_specs=[blk] * 4, out_shape=[jax.ShapeDtypeStruct((r, c), F32)] * 4,
        compiler_params=_params(("parallel",), 40),
    )(parts, w, m, v)


def _sum_small(parts):
    _, r, c = parts.shape

    def body(p_ref, o_ref):
        s = p_ref[0]
        for k in range(1, N_DEV):
            s = s + p_ref[k]
        o_ref[...] = s

    return pl.pallas_call(body, name="sum_small", out_shape=jax.ShapeDtypeStruct((r, c), F32))(parts)


def _adamw_small(g, w, m, v):
    def body(g_ref, w_ref, m_ref, v_ref, d_ref, mo_ref, vo_ref):
        d_ref[...], mo_ref[...], vo_ref[...] = _adamw(g_ref[...], w_ref[...], m_ref[...], v_ref[...])

    return pl.pallas_call(body, name="adamw_small", out_shape=[jax.ShapeDtypeStruct(g.shape, F32)] * 3)(g, w, m, v)


def _head_sum_matrix():
    r = lax.broadcasted_iota(jnp.int32, (512, 512), 0) // HEAD_DIM
    c = lax.broadcasted_iota(jnp.int32, (512, 512), 1) // HEAD_DIM
    return (r == c).astype(BF16)


_ROW_SHARDED = ("w_out", "w_q", "w_o", "w_down")


class _Weights:
    def __init__(self, full, shards=None):
        self.full = dict(full)
        self.shards = shards

    def rider(self, names):
        return None if self.shards is None else _Gather([self.shards[n] for n in names])

    def arrived(self, names, gathered):
        if gathered is not None:
            for n, g in zip(names, gathered):
                self.full[n] = g.reshape(-1, g.shape[-1]) if n in _ROW_SHARDED else g

    def __getitem__(self, name):
        return self.full[name]


class _Grads:
    def __init__(self, distributed):
        self.distributed = distributed
        self.local = {}
        self.received = {}

    def add(self, name, g):
        self.local[name] = g.reshape(N_DEV, -1, g.shape[-1]) if name in _ROW_SHARDED else g

    def rider(self, names):
        return _Exchange([self.local[n] for n in names]) if self.distributed else None

    def arrived(self, names, received):
        if received is not None:
            for n, r in zip(names, received):
                self.received[n] = r


def _ride(fn, *args, rider=None, **kw):
    if rider is None:
        return fn(*args, **kw), None
    return fn(*args, rider=rider, **kw)


def _local_step(x, mem, tgt, gains, weights, cw, grads):
    t, d = x.shape
    w_in = weights["w_in"]
    x = _reorder(x, False, "reorder_x")
    tgt = _reorder(tgt, False, "reorder_target")

    names = ["w_q", "w_kv", "w_o"]
    (qkv, h1), got = _ride(_norm_matmul, x, gains["g_mix"], w_in, name="proj_qkv", out_dtype=F32, tb=512, blk0=0,
                           nblk=4, save_h=True, rider=weights.rider(names))
    weights.arrived(names, got)
    gates, got = _ride(_norm_matmul, x, gains["g_mix"], w_in, name="proj_gates", out_dtype=F32, tb=512, blk0=4,
                       nblk=4, rider=weights.rider(["w_up"]))
    weights.arrived(["w_up"], got)
    (attn, *lses), got = _ride(_attention_fwd, qkv, rider=weights.rider(["w_down"]))
    weights.arrived(["w_down"], got)
    x1, merged = _mixer_fwd(x, attn, gates, cw, gains["g_attn_out"], gains["g_conv_out"], weights["w_out"])
    kv, mem_n = _norm_matmul(mem, gains["g_mem"], weights["w_kv"], name="mem_kv", out_dtype=BF16, tb=mem.shape[0],
                             save_h=True)
    x2, h2, qm, om = _xattn_fwd(x1, gains["g_xattn"], weights["w_q"], kv, weights["w_o"], tb=256)
    w_up = weights["w_up"]
    a, h3 = _norm_matmul(x2, gains["g_mlp"], w_up, name="mlp_up", out_dtype=BF16, tb=512, relu=True, save_h=True)
    dx3, loss_blk, gg_final = _mlp_down_loss(a, weights["w_down"], x2, tgt, gains["g_final"], tb=256)

    w_down_blocks = weights["w_down"].reshape(N_DEV, -1, d)
    dpre = _mlp_dpre(dx3, w_down_blocks, a, tb=512)
    grads.add("w_down", _matmul_tn(a, dx3, name="grad_w_down", bm=512, bn=512, bt=1024, square_a=True))
    gw_up, got = _ride(_matmul_tn, h3, dpre, name="grad_w_up", bm=512, bn=512, bt=1024, col_sharded=True,
                       rider=grads.rider(["w_down"]))
    grads.arrived(["w_down"], got)
    grads.add("w_up", gw_up)
    (dx2, gg_mlp), got = _ride(_matmul_nt_normbwd, dpre, w_up, x2, gains["g_mlp"], dx3, name="mlp_dx", tb=512,
                               rider=grads.rider(["w_up"]))
    grads.arrived(["w_up"], got)

    grads.add("w_o", _matmul_tn(om, dx2, name="grad_w_o", bm=512, bn=512, bt=1024))
    (dx1, dqm, dk, dv, gg_xattn), got = _ride(_xattn_bwd, dx2, x1, gains["g_xattn"], qm, weights["w_q"], kv,
                                              weights["w_o"], tb=256, rider=grads.rider(["w_o"]))
    grads.arrived(["w_o"], got)
    grads.add("w_q", _matmul_tn(h2, dqm, name="grad_w_q", bm=512, bn=512, bt=1024))
    dkv = jnp.concatenate([dk, dv], axis=1).astype(BF16)
    grads.add("w_kv", _matmul_tn(mem_n, dkv, name="grad_w_kv", bm=512, bn=256, bt=mem.shape[0], col_sharded=True))
    _, gg_mem = _matmul_nt_normbwd(dkv, weights["w_kv"], mem, gains["g_mem"], None, name="mem_dx", tb=mem.shape[0])

    grads.add("w_out", _matmul_tn(merged, dx1, name="grad_w_out", bm=512, bn=512, bt=1024))
    names = ["w_q", "w_out"]
    (dattn, dsum, dy, gg_attn, gg_conv), got = _ride(
        _mixer_bwd, dx1, attn, gates, cw, gains["g_attn_out"], gains["g_conv_out"], weights["w_out"],
        _head_sum_matrix(), rider=grads.rider(names))
    grads.arrived(names, got)
    dproj, gcw = _conv_bwd(dy, gates, cw)
    dproj, got = _ride(_attention_bwd, qkv, dattn, dsum, lses, dproj, rider=grads.rider(["w_kv"]))
    grads.arrived(["w_kv"], got)
    gw_in = _matmul_tn(h1, dproj, name="grad_w_in", bm=512, bn=512, bt=1024)
    grads.add("w_in", gw_in.reshape(d, N_DEV, -1).transpose(1, 0, 2))
    w_in_cols = w_in.transpose(1, 0, 2).reshape(d, -1)
    (grad_x, gg_mix), got = _ride(_matmul_nt_normbwd, dproj, w_in_cols, x, gains["g_mix"], dx1, name="mixer_dx",
                                  tb=512, stacked=True, rider=grads.rider(["w_in"]))
    grads.arrived(["w_in"], got)
    grad_x = _reorder(grad_x, True, "reorder_grad_x")

    small = dict(g_mix=gg_mix, g_attn_out=gg_attn, g_conv_out=gg_conv, g_xattn=gg_xattn, g_mem=gg_mem,
                 g_mlp=gg_mlp, g_final=gg_final, conv_w=gcw[0:3], loss=loss_blk[0:1, 0:1])
    return grad_x, small


_BIG = ("w_in", "w_out", "w_q", "w_kv", "w_o", "w_up", "w_down")
_GAIN_ROWS = ("g_mix", "g_xattn", "g_mem", "g_mlp", "g_final")


def _pack_small(vals, conv):
    rows = [vals[k].reshape(1, -1) for k in _GAIN_ROWS]
    rows.append(jnp.concatenate([vals["g_attn_out"].reshape(1, -1), vals["g_conv_out"].reshape(1, -1)], axis=1))
    flat = conv.reshape(1, -1)
    rows.append(jnp.pad(flat, ((0, 0), (0, 1024 - flat.shape[1]))))
    rows.append(jnp.zeros((1, 1024), F32))
    return jnp.concatenate(rows, axis=0)


def kernel(x, mem, g_mix, w_in, conv_w, g_attn_out, g_conv_out, w_out, g_xattn, g_mem, w_q_mem, w_kv_mem, w_o_mem, g_mlp, w_up, w_down, g_final, loss_target, m_g_mix, m_w_in, m_conv_w, m_g_attn_out, m_g_conv_out, m_w_out, m_g_xattn, m_g_mem, m_w_q_mem, m_w_kv_mem, m_w_o_mem, m_g_mlp, m_w_up, m_w_down, m_g_final, v_g_mix, v_w_in, v_conv_w, v_g_attn_out, v_g_conv_out, v_w_out, v_g_xattn, v_g_mem, v_w_q_mem, v_w_kv_mem, v_w_o_mem, v_g_mlp, v_w_up, v_w_down, v_g_final):
    d = x.shape[-1]
    me = 4 * lax.axis_index("x") + 2 * lax.axis_index("y") + lax.axis_index("c")
    w_shards = dict(w_in=w_in, w_out=w_out, w_q=w_q_mem, w_kv=w_kv_mem, w_o=w_o_mem, w_up=w_up, w_down=w_down)
    m_shards = dict(w_in=m_w_in, w_out=m_w_out, w_q=m_w_q_mem, w_kv=m_w_kv_mem, w_o=m_w_o_mem, w_up=m_w_up,
                    w_down=m_w_down)
    v_shards = dict(w_in=v_w_in, w_out=v_w_out, w_q=v_w_q_mem, w_kv=v_w_kv_mem, w_o=v_w_o_mem, w_up=v_w_up,
                    w_down=v_w_down)
    gains = dict(g_mix=g_mix, g_attn_out=g_attn_out, g_conv_out=g_conv_out, g_xattn=g_xattn, g_mem=g_mem,
                 g_mlp=g_mlp, g_final=g_final)
    gains2 = {k: v.reshape(1, -1) for k, v in gains.items()}

    shards = {k: w_shards[k].astype(BF16) for k in _BIG}
    first = _comm_call(_Gather([shards["w_in"], shards["w_out"], conv_w]), name="gather_first")
    weights = _Weights({}, shards)
    weights.arrived(["w_in", "w_out"], first[:2])
    cw = first[2].transpose(1, 0, 2).reshape(3, -1)

    grads = _Grads(distributed=True)
    grad_x, small = _local_step(x[0], mem[0], loss_target[0], gains2, weights, cw, grads)

    small_rows = [small[k] for k in _GAIN_ROWS]
    small_rows.append(jnp.concatenate([small["g_attn_out"], small["g_conv_out"]], axis=1))
    small_rows.append(jnp.pad(small["conv_w"], ((0, 0), (0, 512))))
    small_rows.append(jnp.pad(small["loss"], ((0, 6), (0, 1023))))
    small_part = jnp.concatenate(small_rows, axis=0)
    small_received = _comm_call(_Exchange([small_part], [True]), name="exchange_small")[0]

    outs = {}
    tiles = dict(w_in=256, w_out=128, w_q=128, w_kv=256, w_o=128, w_up=256, w_down=256)
    for k in _BIG:
        outs[k] = _sum_adamw(grads.received[k], w_shards[k], m_shards[k], v_shards[k], name=f"adamw_{k}",
                             tr=tiles[k])

    ssum = _sum_small(small_received)
    loss = ssum[9, 0]
    g_small = {k: ssum[i] for i, k in enumerate(_GAIN_ROWS)}
    g_small["g_attn_out"] = ssum[5, 0:512]
    g_small["g_conv_out"] = ssum[5, 512:1024]
    g_conv = lax.dynamic_slice_in_dim(ssum[6:9, 0:512], me * 64, 64, axis=1)
    m_small = dict(g_mix=m_g_mix, g_attn_out=m_g_attn_out, g_conv_out=m_g_conv_out, g_xattn=m_g_xattn,
                   g_mem=m_g_mem, g_mlp=m_g_mlp, g_final=m_g_final)
    v_small = dict(g_mix=v_g_mix, g_attn_out=v_g_attn_out, g_conv_out=v_g_conv_out, g_xattn=v_g_xattn,
                   g_mem=v_g_mem, g_mlp=v_g_mlp, g_final=v_g_final)
    packed = [_pack_small(g_small, g_conv), _pack_small(gains, conv_w), _pack_small(m_small, m_conv_w),
              _pack_small(v_small, v_conv_w)]
    upd = _adamw_small(*packed)

    def unpack(p):
        res = {k: p[i] for i, k in enumerate(_GAIN_ROWS)}
        res["g_attn_out"] = p[5, 0:512]
        res["g_conv_out"] = p[5, 512:1024]
        res["conv_w"] = p[6, 0:192].reshape(3, 64)
        return res

    g_small["conv_w"] = g_conv
    small_out = [g_small] + [unpack(p) for p in upd]
    names = {"g_mix": "g_mix", "w_in": "w_in", "conv_w": "conv_w", "g_attn_out": "g_attn_out",
             "g_conv_out": "g_conv_out", "w_out": "w_out", "g_xattn": "g_xattn", "g_mem": "g_mem",
             "w_q_mem": "w_q", "w_kv_mem": "w_kv", "w_o_mem": "w_o", "g_mlp": "g_mlp", "w_up": "w_up",
             "w_down": "w_down", "g_final": "g_final"}
    result = [loss, grad_x[None]]
    for which in range(4):
        for key in names.values():
            result.append(outs[key][which] if key in outs else small_out[which][key])
    return tuple(result)
```

```python
import math

import jax
import jax.numpy as jnp
from jax import lax
from jax.experimental import pallas as pl
from jax.experimental.pallas import tpu as pltpu

F32 = jnp.float32
BF16 = jnp.bfloat16
NORM_EPS = 1e-6
NEG_INF = -1e30
N_DEV = 8
BLK = 128
HEAD_DIM = 64
N_MEM_HEADS = 4
ADAM_LR = 0.001
ADAM_B1 = 0.9
ADAM_B2 = 0.999
ADAM_EPS = 1e-08
ADAM_WD = 0.01
ADAM_STEP = 10
MESH = pl.DeviceIdType.MESH
ANY = pl.BlockSpec(memory_space=pl.ANY)


def _dot(a, b):
    return jnp.dot(a, b, preferred_element_type=F32)


def _dot_nt(a, b):
    return lax.dot_general(a, b, (((1,), (1,)), ((), ())), preferred_element_type=F32)


def _dot_tn(a, b):
    return lax.dot_general(a, b, (((0,), (0,)), ((), ())), preferred_element_type=F32)


def _params(semantics, vmem_mb):
    return pltpu.CompilerParams(dimension_semantics=semantics, vmem_limit_bytes=vmem_mb << 20)


def _rms_fwd(x, g):
    r = lax.rsqrt(jnp.mean(x * x, axis=-1, keepdims=True) + NORM_EPS)
    xh = x * r
    return xh * g, xh, r


def _rms_bwd(dy, xh, r, g):
    gy = dy * g
    return r * (gy - xh * jnp.mean(xh * gy, axis=-1, keepdims=True))


def _position():
    x, y, c = lax.axis_index("x"), lax.axis_index("y"), lax.axis_index("c")
    return x, y, c


class _Gather:
    has_mid = True

    def __init__(self, shards):
        self.arrays = list(shards)
        self.n = len(self.arrays)

    def out_shape(self):
        return [jax.ShapeDtypeStruct((N_DEV,) + s.shape, s.dtype) for s in self.arrays]

    def scratch(self):
        return [pltpu.SemaphoreType.DMA((self.n, 7)), pltpu.SemaphoreType.DMA((self.n, 7)),
                pltpu.SemaphoreType.DMA((self.n,))]

    def _ctx(self, ins, outs, sems):
        send_sems, recv_sems, local_sems = sems
        x, y, c = _position()
        me, sibling = (x, y, c), (x, y, 1 - c)
        chips = [(1 - x, y), (x, 1 - y), (1 - x, 1 - y)]

        def lin(px, py, pc):
            return 4 * px + 2 * py + pc

        def copy(a, k, block, to, src=None):
            dst = outs[a].at[lin(*block)]
            return pltpu.make_async_remote_copy(
                src_ref=dst if src is None else src, dst_ref=dst,
                send_sem=send_sems.at[a, k], recv_sem=recv_sems.at[a, k],
                device_id=to, device_id_type=MESH)

        def mine():
            return [pltpu.make_async_copy(ins[a], outs[a].at[lin(*me)], local_sems.at[a]) for a in range(self.n)]

        def first():
            res = []
            for a in range(self.n):
                res.append(copy(a, 0, me, sibling, src=ins[a]))
                res += [copy(a, 1 + j, me, (*chip, c), src=ins[a]) for j, chip in enumerate(chips)]
            return res

        return c, me, sibling, chips, copy, mine, first

    def start(self, ins, outs, sems):
        _, _, _, _, _, mine, first = self._ctx(ins, outs, sems)
        for cp in mine() + first():
            cp.start()

    def mid(self, ins, outs, sems):
        c, me, sibling, chips, copy, _, _ = self._ctx(ins, outs, sems)
        for j, chip in enumerate(chips):
            for a in range(self.n):
                copy(a, 1 + j, (*chip, c), me).wait_recv()
                copy(a, 4 + j, (*chip, c), sibling).start()

    def finish(self, ins, outs, sems):
        c, me, sibling, chips, copy, mine, first = self._ctx(ins, outs, sems)
        for a in range(self.n):
            copy(a, 0, sibling, me).wait_recv()
            for j, chip in enumerate(chips):
                copy(a, 4 + j, (*chip, 1 - c), me).wait_recv()
        for cp in first():
            cp.wait_send()
        for j, chip in enumerate(chips):
            for a in range(self.n):
                copy(a, 4 + j, (*chip, c), sibling).wait_send()
        for cp in mine():
            cp.wait()


class _Exchange:
    has_mid = False

    def __init__(self, parts, bcast=None):
        self.arrays = list(parts)
        self.n = len(self.arrays)
        self.bcast = [False] * self.n if bcast is None else list(bcast)

    def out_shape(self):
        return [jax.ShapeDtypeStruct(((N_DEV,) + p.shape) if b else p.shape, p.dtype)
                for p, b in zip(self.arrays, self.bcast)]

    def scratch(self):
        return [pltpu.SemaphoreType.DMA((self.n, 7)), pltpu.SemaphoreType.DMA((self.n, 7)),
                pltpu.SemaphoreType.DMA((self.n,))]

    def _ctx(self, ins, outs, sems):
        send_sems, recv_sems, local_sems = sems
        x, y, c = _position()
        me = 4 * x + 2 * y + c

        def src(a, j):
            return ins[a] if self.bcast[a] else ins[a].at[j]

        def local():
            return [pltpu.make_async_copy(src(a, me), outs[a].at[me], local_sems.at[a]) for a in range(self.n)]

        def remote(inbound):
            res = []
            for a in range(self.n):
                for k in range(1, N_DEV):
                    peer = (1 - x if k & 4 else x, 1 - y if k & 2 else y, 1 - c if k & 1 else c)
                    plin = 4 * peer[0] + 2 * peer[1] + peer[2]
                    res.append(pltpu.make_async_remote_copy(
                        src_ref=src(a, plin), dst_ref=outs[a].at[plin if inbound else me],
                        send_sem=send_sems.at[a, k - 1], recv_sem=recv_sems.at[a, k - 1],
                        device_id=peer, device_id_type=MESH))
            return res

        return local, remote

    def start(self, ins, outs, sems):
        local, remote = self._ctx(ins, outs, sems)
        for cp in local() + remote(False):
            cp.start()

    def finish(self, ins, outs, sems):
        local, remote = self._ctx(ins, outs, sems)
        for cp in remote(True):
            cp.wait_recv()
        for cp in remote(False):
            cp.wait_send()
        for cp in local():
            cp.wait()


def _comm_call(rider, name):
    n_in, n_out = len(rider.arrays), len(rider.out_shape())

    def body(*refs):
        ins, outs, sems = refs[:n_in], refs[n_in:n_in + n_out], refs[n_in + n_out:]
        rider.start(ins, outs, sems)
        if rider.has_mid:
            rider.mid(ins, outs, sems)
        rider.finish(ins, outs, sems)

    return pl.pallas_call(
        body, name=name, out_shape=rider.out_shape(),
        in_specs=[ANY] * n_in, out_specs=[ANY] * n_out, scratch_shapes=rider.scratch(),
    )(*rider.arrays)


def _pcall(body, *, name, grid, in_specs, out_specs, out_shape, scratch_shapes=(), semantics, vmem_mb, rider=None,
           aliases=None):
    in_specs, out_specs, out_shape = list(in_specs), list(out_specs), list(out_shape)
    scratch_shapes = list(scratch_shapes)
    aliases = dict(aliases or {})
    if rider is None:
        call = pl.pallas_call(body, name=name, grid=grid, in_specs=in_specs, out_specs=out_specs,
                              out_shape=out_shape, scratch_shapes=scratch_shapes, input_output_aliases=aliases,
                              compiler_params=_params(semantics, vmem_mb))
        return lambda *args: (list(call(*args)), None)
    n_in, n_out, n_scr = len(in_specs), len(out_specs), len(scratch_shapes)
    r_in, r_shapes = len(rider.arrays), rider.out_shape()
    r_out = len(r_shapes)
    total = math.prod(grid)
    mid_step = (3 * total) // 4

    def wrapped(*refs):
        bounds = [0, n_in, r_in, n_out, r_out, n_scr]
        for i in range(1, len(bounds)):
            bounds[i] += bounds[i - 1]
        a, ra, o, ro, s = (refs[bounds[i]:bounds[i + 1]] for i in range(5))
        rs = refs[bounds[5]:]
        step = pl.program_id(0)
        for k in range(1, len(grid)):
            step = step * grid[k] + pl.program_id(k)
        pl.when(step == 0)(lambda: rider.start(ra, ro, rs))
        body(*a, *o, *s)
        if rider.has_mid:
            pl.when(step == mid_step)(lambda: rider.mid(ra, ro, rs))
        pl.when(step == total - 1)(lambda: rider.finish(ra, ro, rs))

    call = pl.pallas_call(
        wrapped, name=name, grid=grid, in_specs=in_specs + [ANY] * r_in, out_specs=out_specs + [ANY] * r_out,
        out_shape=out_shape + r_shapes, scratch_shapes=scratch_shapes + rider.scratch(),
        input_output_aliases=aliases, compiler_params=_params(("arbitrary",) * len(grid), vmem_mb))

    def run(*args):
        res = call(*args, *rider.arrays)
        return list(res[:n_out]), list(res[n_out:])

    return run


def _norm_matmul(x, g, w, *, name, out_dtype, tb, blk0=0, nblk=None, relu=False, save_h=False, rider=None):
    t, d = x.shape
    bn = w.shape[2]
    nblk = w.shape[0] if nblk is None else nblk

    def body(x_ref, g_ref, w_ref, o_ref, *rest):
        h_scr = rest[-1]

        @pl.when(pl.program_id(1) == 0)
        def _():
            h = _rms_fwd(x_ref[...], g_ref[...])[0].astype(BF16)
            h_scr[...] = h
            if save_h:
                rest[0][...] = h

        acc = _dot(h_scr[...], w_ref[0])
        if relu:
            acc = jnp.maximum(acc, 0.0)
        o_ref[...] = acc.astype(out_dtype)

    out_shape = [jax.ShapeDtypeStruct((t, nblk * bn), out_dtype)]
    out_specs = [pl.BlockSpec((tb, bn), lambda i, j: (i, j))]
    if save_h:
        out_shape.append(jax.ShapeDtypeStruct((t, d), BF16))
        out_specs.append(pl.BlockSpec((tb, d), lambda i, j: (i, 0)))
    res, extra = _pcall(
        body, name=name, grid=(t // tb, nblk),
        in_specs=[pl.BlockSpec((tb, d), lambda i, j: (i, 0)),
                  pl.BlockSpec((1, d), lambda i, j: (0, 0)),
                  pl.BlockSpec((1, d, bn), lambda i, j: (j + blk0, 0, 0))],
        out_specs=out_specs, out_shape=out_shape,
        scratch_shapes=[pltpu.VMEM((tb, d), BF16)],
        semantics=("parallel", "arbitrary"), vmem_mb=40, rider=rider,
    )(x, g, w)
    res = res if save_h else res[0]
    return res if rider is None else (res, extra)


def _matmul_nt_normbwd(dy, w, x, g, dres, *, name, tb, stacked=False, rider=None):
    t, d = x.shape
    if stacked:
        nblk, _, bn = dy.shape
    else:
        nblk, _, bn = w.shape
    n_i = t // tb
    has_res = dres is not None

    def body(dy_ref, w_ref, x_ref, g_ref, *rest):
        if has_res:
            dres_ref, dx_ref, gg_ref, acc = rest
        else:
            dx_ref, gg_ref, acc = rest
        i, j = pl.program_id(0), pl.program_id(1)

        @pl.when(j == 0)
        def _():
            acc[...] = jnp.zeros_like(acc)

        acc[...] += _dot_nt(dy_ref[...], w_ref[...] if stacked else w_ref[0])

        @pl.when(j == nblk - 1)
        def _():
            g_v = g_ref[...]
            _, xh, r = _rms_fwd(x_ref[...], g_v)
            dh = acc[...]
            dx = _rms_bwd(dh, xh, r, g_v)
            if has_res:
                dx = dx + dres_ref[...]
            dx_ref[...] = dx
            part = jnp.sum(dh * xh, axis=0, keepdims=True)

            @pl.when(i == 0)
            def _():
                gg_ref[...] = part

            @pl.when(i != 0)
            def _():
                gg_ref[...] += part

    if stacked:
        in_specs = [pl.BlockSpec((None, tb, bn), lambda i, j: (j, i, 0)), pl.BlockSpec((d, bn), lambda i, j: (0, j))]
    else:
        in_specs = [pl.BlockSpec((tb, bn), lambda i, j: (i, j)), pl.BlockSpec((1, d, bn), lambda i, j: (j, 0, 0))]
    in_specs += [pl.BlockSpec((tb, d), lambda i, j: (i, 0)),
                 pl.BlockSpec((1, d), lambda i, j: (0, 0))]
    args = [dy, w, x, g]
    if has_res:
        in_specs.append(pl.BlockSpec((tb, d), lambda i, j: (i, 0)))
        args.append(dres)
    res, extra = _pcall(
        body, name=name, grid=(n_i, nblk),
        in_specs=in_specs,
        out_specs=[pl.BlockSpec((tb, d), lambda i, j: (i, 0)), pl.BlockSpec((1, d), lambda i, j: (0, 0))],
        out_shape=[jax.ShapeDtypeStruct((t, d), F32), jax.ShapeDtypeStruct((1, d), F32)],
        scratch_shapes=[pltpu.VMEM((tb, d), F32)],
        semantics=("arbitrary", "arbitrary"), vmem_mb=40, rider=rider,
    )(*args)
    return res if rider is None else (res, extra)


def _matmul_tn(a, b, *, name, bm, bn, bt, col_sharded=False, square_a=False, rider=None):
    t, m = a.shape
    stacked = b.ndim == 3
    n = b.shape[0] * bn if stacked else b.shape[1]
    n_t = t // bt

    def body(a_ref, b_ref, o_ref, acc):
        k = pl.program_id(2)

        @pl.when(k == 0)
        def _():
            acc[...] = jnp.zeros_like(acc)

        av = a_ref[...]
        if square_a:
            av = av.astype(F32)
            av = av * av
        acc[...] += _dot_tn(av.astype(BF16), b_ref[...].astype(BF16))

        @pl.when(k == n_t - 1)
        def _():
            if col_sharded:
                o_ref[0] = acc[...].astype(BF16)
            else:
                o_ref[...] = acc[...].astype(BF16)

    if col_sharded:
        per = (n // N_DEV) // bn
        out_shape = jax.ShapeDtypeStruct((N_DEV, m, n // N_DEV), BF16)
        out_spec = pl.BlockSpec((1, bm, bn), lambda i, j, k: (j // per, i, j % per))
    else:
        out_shape = jax.ShapeDtypeStruct((m, n), BF16)
        out_spec = pl.BlockSpec((bm, bn), lambda i, j, k: (i, j))
    res, extra = _pcall(
        body, name=name, grid=(m // bm, n // bn, n_t),
        in_specs=[pl.BlockSpec((bt, bm), lambda i, j, k: (k, i)),
                  pl.BlockSpec((None, bt, bn), lambda i, j, k: (j, k, 0)) if stacked
                  else pl.BlockSpec((bt, bn), lambda i, j, k: (k, j))],
        out_specs=[out_spec], out_shape=[out_shape],
        scratch_shapes=[pltpu.VMEM((bm, bn), F32)],
        semantics=("parallel", "parallel", "arbitrary"), vmem_mb=40, rider=rider,
    )(a, b)
    return res[0] if rider is None else (res[0], extra)


N_RES = 16
SEG = 128
HALF = N_RES * SEG
TI = 16


def _x4(a):
    return a.reshape(a.shape[0] // HALF, N_RES, SEG, a.shape[1])


def _reorder(a, inverse, name):
    t, c = a.shape
    n_i = SEG // TI
    natural = pl.BlockSpec((TI * N_RES, c), lambda s: (s, 0))
    major = pl.BlockSpec((1, N_RES, TI, c), lambda s: (s // n_i, 0, s % n_i, 0))

    def body(i_ref, o_ref, scr):
        for cb in range(c // BLK):
            cols = slice(cb * BLK, (cb + 1) * BLK)
            slab = scr.at[cb]
            if inverse:
                for r in range(N_RES):
                    slab[pl.ds(r, TI, stride=N_RES), :] = i_ref[0, r, :, cols]
                o_ref[:, cols] = slab[...]
            else:
                slab[...] = i_ref[:, cols]
                for r in range(N_RES):
                    o_ref[0, r, :, cols] = slab[pl.ds(r, TI, stride=N_RES), :]

    scratch = [pltpu.VMEM((c // BLK, TI * N_RES, BLK), a.dtype)]
    if inverse:
        return pl.pallas_call(
            body, name=name, grid=(t // (TI * N_RES),), in_specs=[major], out_specs=natural,
            out_shape=jax.ShapeDtypeStruct((t, c), a.dtype), scratch_shapes=scratch,
            compiler_params=_params(("parallel",), 32))(_x4(a))
    return pl.pallas_call(
        body, name=name, grid=(t // (TI * N_RES),), in_specs=[natural], out_specs=major,
        out_shape=jax.ShapeDtypeStruct((t // HALF, N_RES, SEG, c), a.dtype), scratch_shapes=scratch,
        compiler_params=_params(("parallel",), 32))(a).reshape(t, c)


_PATTERNS = ((1, 16, 8, SEG), (4, 4, 32, 4 * SEG), (16, 1, SEG, 0))
_FIRST = {1: 1, 4: 4, 16: 16}


def _group_rows(d, g):
    a = g >> 4
    if d == 16:
        base = a * HALF + (g & 15) * SEG
        prev = base - HALF
    elif d == 4:
        c = (g >> 2) & 3
        base = a * HALF + (g & 3) * SEG + c * 32
        prev = jnp.where(c > 0, base - 32, base - HALF + 96)
    else:
        c = g & 15
        base = a * HALF + c * 8
        prev = jnp.where(c > 0, base - 8, base - HALF + 120)
    return base, prev


def _load_rows(ref, base, n, rows, stride):
    parts = [ref[pl.ds(pl.multiple_of(base + j * stride, 8), rows), :] for j in range(n)]
    return parts[0] if n == 1 else jnp.concatenate(parts, axis=0)


def _store_rows(ref, base, val, n, rows, stride, add=False):
    for j in range(n):
        sl = pl.ds(pl.multiple_of(base + j * stride, 8), rows)
        piece = val[j * rows:(j + 1) * rows, :]
        if add:
            ref[sl, :] += piece
        else:
            ref[sl, :] = piece


def _band_bias(n, rows):
    shift = rows.bit_length() - 1
    lq = lax.broadcasted_iota(jnp.int32, (BLK, BLK), 0)
    lk = lax.broadcasted_iota(jnp.int32, (BLK, BLK), 1)
    iq = (lq & (rows - 1)) * n + (lq >> shift)
    ik = (lk & (rows - 1)) * n + (lk >> shift)
    zero = jnp.zeros((BLK, BLK), F32)
    return jnp.where(ik >= iq, zero, NEG_INF), jnp.where(ik <= iq, zero, NEG_INF)


def _set_bias(bias_scr, n, rows):
    prev_b, cur_b = _band_bias(n, rows)
    for half in range(2):
        bias_scr[half * BLK:(half + 1) * BLK, 0:BLK] = prev_b
        bias_scr[half * BLK:(half + 1) * BLK, BLK:2 * BLK] = cur_b


def _head_consts():
    lane_lo = lax.broadcasted_iota(jnp.int32, (BLK, BLK), 1) < HEAD_DIM
    return lane_lo, [jnp.where(lane_lo, 1.0, 0.0).astype(BF16), jnp.where(lane_lo, 0.0, 1.0).astype(BF16)]


def _stack_heads(v, head_mask):
    return jnp.concatenate([v * head_mask[0], v * head_mask[1]], axis=0)


def _unstack_heads(v2, lane_lo):
    return jnp.where(lane_lo, v2[:BLK], v2[BLK:])


def _rows_per_head(v, lane_lo):
    rolled = pltpu.roll(v, HEAD_DIM, axis=1)
    return jnp.concatenate([jnp.where(lane_lo, v, rolled), jnp.where(lane_lo, rolled, v)], axis=0)


WIDTH = 4


def _loop(lo, hi, fn, width=None):
    if width is None:
        def body(g, carry):
            fn(g)
            return carry

        if hi > lo:
            lax.fori_loop(lo, hi, body, 0)
        return
    while hi > lo:
        trips = (hi - lo) // width
        if trips:
            def body(i, carry, lo=lo, width=width):
                fn([lo + width * i + j for j in range(width)])
                return carry

            lax.fori_loop(0, trips, body, 0)
            lo += trips * width
        width = max(1, width // 2)


def _mix_weights(l1, l2, l3):
    mx = jnp.maximum(jnp.maximum(l1, l2), l3)
    e1, e2, e3 = jnp.exp(l1 - mx), jnp.exp(l2 - mx), jnp.exp(l3 - mx)
    inv = 1.0 / (e1 + e2 + e3)
    return e1 * inv, e2 * inv, e3 * inv


def _attention_fwd(qkv, rider=None):
    t = qkv.shape[0]
    groups = 16 * (t // HALF)

    def body(q_ref, k_ref, v_ref, attn_ref, l1_ref, l2_ref, l3_ref, o_scr, bias_scr):
        lane_lo, head_mask = _head_consts()
        l_refs = (l1_ref, l2_ref, l3_ref)
        for p, (d, n, rows, stride) in enumerate(_PATTERNS):
            _set_bias(bias_scr, n, rows)
            o_p, l_p = o_scr.at[p], l_refs[p]

            def block(gs, has_prev):
                at = [_group_rows(d, g) for g in gs]

                def load(ref, b):
                    return _load_rows(ref, b, n, rows, stride).astype(BF16)

                q2 = [_stack_heads(load(q_ref, b), head_mask) for b, _ in at]
                k2 = [load(k_ref, b) for b, _ in at]
                v2 = [load(v_ref, b) for b, _ in at]
                if has_prev:
                    k2 = [jnp.concatenate([load(k_ref, pv), k], axis=0) for (_, pv), k in zip(at, k2)]
                    v2 = [jnp.concatenate([load(v_ref, pv), v], axis=0) for (_, pv), v in zip(at, v2)]
                s = [_dot_nt(q, k) for q, k in zip(q2, k2)]
                s = [x * 0.125 + (bias_scr[...] if has_prev else bias_scr[:, BLK:2 * BLK]) for x in s]
                mx = [jnp.max(x, axis=1, keepdims=True) for x in s]
                e = [jnp.exp(x - m) for x, m in zip(s, mx)]
                den = [jnp.sum(x, axis=1, keepdims=True) for x in e]
                pb = [(x * (1.0 / dn)).astype(BF16) for x, dn in zip(e, den)]
                o2 = [_dot(x, v) for x, v in zip(pb, v2)]
                lse2 = [jnp.broadcast_to(m + jnp.log(dn), (2 * BLK, BLK)) for m, dn in zip(mx, den)]
                for (b, _), o, l in zip(at, o2, lse2):
                    _store_rows(o_p, b, _unstack_heads(o, lane_lo), n, rows, stride)
                    _store_rows(l_p, b, _unstack_heads(l, lane_lo), n, rows, stride)

            _loop(0, _FIRST[d], lambda gs: block(gs, False), width=WIDTH)
            _loop(_FIRST[d], groups, lambda gs: block(gs, True), width=WIDTH)

        def mix(i):
            sl = pl.ds(pl.multiple_of(i * 256, 256), 256)
            w = _mix_weights(l1_ref[sl, :], l2_ref[sl, :], l3_ref[sl, :])
            attn_ref[sl, :] = w[0] * o_scr[0, sl, :] + w[1] * o_scr[1, sl, :] + w[2] * o_scr[2, sl, :]

        _loop(0, t // 256, mix)

    def col(c0):
        return pl.BlockSpec((t, BLK), lambda hp: (0, c0 + hp))

    res, extra = _pcall(
        body, name="attention_fwd", grid=(4,), in_specs=[col(0), col(4), col(8)], out_specs=[col(0)] * 4,
        out_shape=[jax.ShapeDtypeStruct((t, 512), F32)] * 4,
        scratch_shapes=[pltpu.VMEM((3, t, BLK), F32), pltpu.VMEM((2 * BLK, 2 * BLK), F32)],
        semantics=("parallel",), vmem_mb=48, rider=rider,
    )(qkv, qkv, qkv)
    return res if rider is None else (res, extra)


def _attention_bwd(qkv, dattn, dsum, lses, dproj, rider=None):
    t = qkv.shape[0]
    groups = 16 * (t // HALF)

    def body(q_ref, k_ref, v_ref, da_ref, ds_ref, l1_ref, l2_ref, l3_ref, kept_ref, out_ref, acc, bias_scr):
        del kept_ref
        lane_lo, head_mask = _head_consts()
        l_refs = (l1_ref, l2_ref, l3_ref)

        def clear(i):
            sl = pl.ds(pl.multiple_of(i * 512, 512), 512)
            for s in range(3):
                acc[s, sl, :] = jnp.zeros((512, BLK), F32)

        _loop(0, t // 512, clear)
        dq_acc, dk_acc, dv_acc = acc.at[0], acc.at[1], acc.at[2]
        for p, (d, n, rows, stride) in enumerate(_PATTERNS):
            _set_bias(bias_scr, n, rows)

            def block(gs, has_prev):
                at = [_group_rows(d, g) for g in gs]

                def load(ref, b):
                    return _load_rows(ref, b, n, rows, stride)

                def put(ref, b, val):
                    _store_rows(ref, b, val, n, rows, stride, add=True)

                def wide(x):
                    return jnp.concatenate([x, x], axis=1) if has_prev else x

                lse = [[load(ref, b) for ref in l_refs] for b, _ in at]
                w = [_mix_weights(*ls)[p] for ls in lse]
                do2 = [_stack_heads((wg * load(da_ref, b)).astype(BF16), head_mask) for wg, (b, _) in zip(w, at)]
                dl2 = [wide(_rows_per_head(wg * load(ds_ref, b), lane_lo)) for wg, (b, _) in zip(w, at)]
                lse2 = [wide(_rows_per_head(ls[p], lane_lo)) for ls in lse]
                q2 = [_stack_heads(load(q_ref, b).astype(BF16), head_mask) for b, _ in at]
                k2 = [load(k_ref, b).astype(BF16) for b, _ in at]
                v2 = [load(v_ref, b).astype(BF16) for b, _ in at]
                if has_prev:
                    k2 = [jnp.concatenate([load(k_ref, pv).astype(BF16), k], axis=0) for (_, pv), k in zip(at, k2)]
                    v2 = [jnp.concatenate([load(v_ref, pv).astype(BF16), v], axis=0) for (_, pv), v in zip(at, v2)]
                s = [_dot_nt(q, k) for q, k in zip(q2, k2)]
                dp = [_dot_nt(do, v) for do, v in zip(do2, v2)]
                pr = [jnp.exp(x * 0.125 + (bias_scr[...] if has_prev else bias_scr[:, BLK:2 * BLK]) - l)
                      for x, l in zip(s, lse2)]
                ds = [(pg * (x - dl) * 0.125).astype(BF16) for pg, x, dl in zip(pr, dp, dl2)]
                dq2 = [_dot(x, k) for x, k in zip(ds, k2)]
                dk2 = [_dot_tn(x, q) for x, q in zip(ds, q2)]
                dv2 = [_dot_tn(pg.astype(BF16), do) for pg, do in zip(pr, do2)]
                for (b, pv), dq, dk, dv in zip(at, dq2, dk2, dv2):
                    put(dq_acc, b, _unstack_heads(dq, lane_lo))
                    if has_prev:
                        put(dk_acc, pv, dk[:BLK])
                        put(dv_acc, pv, dv[:BLK])
                        put(dk_acc, b, dk[BLK:])
                        put(dv_acc, b, dv[BLK:])
                    else:
                        put(dk_acc, b, dk)
                        put(dv_acc, b, dv)

            _loop(0, _FIRST[d], lambda gs: block(gs, False), width=WIDTH)
            _loop(_FIRST[d], groups, lambda gs: block(gs, True), width=WIDTH)

        def emit(i):
            sl = pl.ds(pl.multiple_of(i * 512, 512), 512)
            for s in range(3):
                out_ref[s, sl, :] = acc[s, sl, :].astype(BF16)

        _loop(0, t // 512, emit)

    def col(c0):
        return pl.BlockSpec((t, BLK), lambda hp: (0, c0 + hp))

    res, extra = _pcall(
        body, name="attention_bwd", grid=(4,),
        in_specs=[col(0), col(4), col(8)] + [col(0)] * 5 + [ANY],
        out_specs=[pl.BlockSpec((3, t, BLK), lambda hp: (0, 0, hp))],
        out_shape=[jax.ShapeDtypeStruct(dproj.shape, BF16)],
        scratch_shapes=[pltpu.VMEM((3, t, BLK), F32), pltpu.VMEM((2 * BLK, 2 * BLK), F32)],
        semantics=("parallel",), vmem_mb=56, rider=rider, aliases={8: 0},
    )(qkv, qkv, qkv, dattn, dsum, *lses, dproj)
    return res[0] if rider is None else (res[0], extra)


def _order_specs(t):
    n_i = SEG // TI
    nblk = (t // HALF) * n_i
    per = TI // 8

    def main(c):
        return pl.BlockSpec((1, N_RES, TI, c), lambda s: (s // n_i, 0, s % n_i, 0))

    def before(c):
        return pl.BlockSpec((1, 2, 8, c), lambda s: (jnp.maximum(s - 1, 0) // n_i, N_RES // 2 - 1,
                                                     (jnp.maximum(s - 1, 0) % n_i) * per + per - 1, 0))

    def after(c):
        return pl.BlockSpec((1, 2, 8, c), lambda s: (jnp.minimum(s + 1, nblk - 1) // n_i, 0,
                                                     (jnp.minimum(s + 1, nblk - 1) % n_i) * per, 0))

    return nblk, main, before, after


def _shift_in(v, row_in, up):
    rows = v.shape[0]
    idx = lax.broadcasted_iota(jnp.int32, v.shape, 0)
    fill = jnp.broadcast_to(row_in, v.shape)
    if up:
        return jnp.where(idx == rows - 1, fill, pltpu.roll(v, rows - 1, axis=0))
    return jnp.where(idx == 0, fill, pltpu.roll(v, 1, axis=0))


def _taps_behind(u, before):
    s15 = _shift_in(u[N_RES - 1], before[1, 7:8, :], up=False)
    s14 = _shift_in(u[N_RES - 2], before[0, 7:8, :], up=False)
    m1 = jnp.concatenate([s15[None], u[:N_RES - 1]], axis=0)
    m2 = jnp.concatenate([s14[None], s15[None], u[:N_RES - 2]], axis=0)
    return m1, m2


def _taps_ahead(u, after):
    t0 = _shift_in(u[0], after[0, 0:1, :], up=True)
    t1 = _shift_in(u[1], after[1, 0:1, :], up=True)
    p1 = jnp.concatenate([u[1:], t0[None]], axis=0)
    p2 = jnp.concatenate([u[2:], t0[None], t1[None]], axis=0)
    return p1, p2


def _conv_fwd(gates, before, first, cw):
    bg, cg, xc = gates[..., 0:512], gates[..., 512:1024], gates[..., 1024:1536]
    u = cg * xc
    ub = before[..., 512:1024] * before[..., 1024:1536]
    ub = jnp.where(first, jnp.zeros_like(ub), ub)
    m1, m2 = _taps_behind(u, ub)
    conv = m2 * cw[0:1, :] + m1 * cw[1:2, :] + u * cw[2:3, :]
    return bg, u, m1, m2, conv


def _sum_tokens(v):
    return jnp.sum(jnp.sum(v, axis=0), axis=0, keepdims=True)


def _mixer_fwd(x, attn, gates, cw, g_a, g_c, w_out):
    t, d = x.shape
    nblk, main, before, _ = _order_specs(t)
    rows = N_RES * TI

    def body(x_ref, at_ref, gt_ref, gb_ref, cw_ref, ga_ref, gc_ref, wa_ref, wb_ref, x1_ref, mg_ref):
        an = _rms_fwd(at_ref[0], ga_ref[...])[0].astype(BF16)
        bg, _, _, _, conv = _conv_fwd(gt_ref[0], gb_ref[0], pl.program_id(0) == 0, cw_ref[...])
        cn = _rms_fwd(bg * conv, gc_ref[...])[0].astype(BF16)
        mg_ref[0, :, :, 0:512] = an
        mg_ref[0, :, :, 512:1024] = cn
        y = _dot(an.reshape(rows, 512), wa_ref[...]) + _dot(cn.reshape(rows, 512), wb_ref[...])
        x1_ref[0] = x_ref[0] + y.reshape(N_RES, TI, d)

    const = lambda r, c, i0=0: pl.BlockSpec((r, c), lambda s: (i0, 0))
    x1, merged = pl.pallas_call(
        body, name="mixer_fwd", grid=(nblk,),
        in_specs=[main(d), main(512), main(1536), before(1536), const(3, 512), const(1, 512), const(1, 512),
                  const(512, d), const(512, d, 1)],
        out_specs=[main(d), main(d)],
        out_shape=[jax.ShapeDtypeStruct(_x4(x).shape, F32), jax.ShapeDtypeStruct(_x4(x).shape, BF16)],
        compiler_params=_params(("parallel",), 48),
    )(_x4(x), _x4(attn), _x4(gates), _x4(gates), cw, g_a, g_c, w_out, w_out)
    return x1.reshape(t, d), merged.reshape(t, d)


def _mixer_bwd(dx1, attn, gates, cw, g_a, g_c, w_out, head_sum, rider=None):
    t, d = dx1.shape
    nblk, main, before, _ = _order_specs(t)
    rows = N_RES * TI

    def body(dx_ref, at_ref, gt_ref, gb_ref, cw_ref, ga_ref, gc_ref, wa_ref, wb_ref, hs_ref,
             da_ref, dsum_ref, dy_ref, gga_ref, ggc_ref):
        s = pl.program_id(0)
        dxb = dx_ref[0].reshape(rows, d).astype(BF16)
        dma = _dot_nt(dxb, wa_ref[...]).reshape(N_RES, TI, 512)
        dmc = _dot_nt(dxb, wb_ref[...]).reshape(N_RES, TI, 512)
        attn_v, g_av = at_ref[0], ga_ref[...]
        _, ah, ra = _rms_fwd(attn_v, g_av)
        dattn = _rms_bwd(dma, ah, ra, g_av)
        da_ref[0] = dattn
        z = (dattn * attn_v).reshape(rows, 512)
        hs = hs_ref[...]
        z1 = z.astype(BF16)
        z2 = (z - z1.astype(F32)).astype(BF16)
        z3 = (z - z1.astype(F32) - z2.astype(F32)).astype(BF16)
        dsum_ref[0] = (_dot(z1, hs) + _dot(z2, hs) + _dot(z3, hs)).reshape(N_RES, TI, 512)
        bg, _, _, _, conv = _conv_fwd(gt_ref[0], gb_ref[0], s == 0, cw_ref[...])
        g_cv = gc_ref[...]
        _, yh, rc = _rms_fwd(bg * conv, g_cv)
        dy_ref[0] = _rms_bwd(dmc, yh, rc, g_cv)
        pa, pc = _sum_tokens(dma * ah), _sum_tokens(dmc * yh)

        @pl.when(s == 0)
        def _():
            gga_ref[...] = pa
            ggc_ref[...] = pc

        @pl.when(s != 0)
        def _():
            gga_ref[...] += pa
            ggc_ref[...] += pc

    const = lambda r, c, i0=0: pl.BlockSpec((r, c), lambda s: (i0, 0))
    shape4 = _x4(attn).shape
    res, extra = _pcall(
        body, name="mixer_bwd", grid=(nblk,),
        in_specs=[main(d), main(512), main(1536), before(1536), const(3, 512), const(1, 512), const(1, 512),
                  const(512, d), const(512, d, 1), const(512, 512)],
        out_specs=[main(512)] * 3 + [const(1, 512), const(1, 512)],
        out_shape=[jax.ShapeDtypeStruct(shape4, F32)] * 3 + [jax.ShapeDtypeStruct((1, 512), F32)] * 2,
        semantics=("arbitrary",), vmem_mb=48, rider=rider,
    )(_x4(dx1), _x4(attn), _x4(gates), _x4(gates), cw, g_a, g_c, w_out, w_out, head_sum)
    res = [r.reshape(t, 512) for r in res[:3]] + res[3:]
    return res if rider is None else (res, extra)


def _conv_bwd(dy, gates, cw):
    t = dy.shape[0]
    nblk, main, before, after = _order_specs(t)
    n_i = SEG // TI

    def body(dy_ref, dya_ref, gt_ref, gb_ref, ga_ref, cw_ref, dp_ref, gcw_ref):
        s = pl.program_id(0)
        cw_v, gates_v = cw_ref[...], gt_ref[0]
        bg, u, m1, m2, conv = _conv_fwd(gates_v, gb_ref[0], s == 0, cw_v)
        dy_v = dy_ref[0]
        dconv = dy_v * bg
        dca = dya_ref[0] * ga_ref[0][..., 0:512]
        dca = jnp.where(s == nblk - 1, jnp.zeros_like(dca), dca)
        p1, p2 = _taps_ahead(dconv, dca)
        du = dconv * cw_v[2:3, :] + p1 * cw_v[1:2, :] + p2 * cw_v[0:1, :]
        dp_ref[0, 0] = (dy_v * conv).astype(BF16)
        dp_ref[1, 0] = (du * gates_v[..., 1024:1536]).astype(BF16)
        dp_ref[2, 0] = (du * gates_v[..., 512:1024]).astype(BF16)
        parts = [_sum_tokens(dconv * m2), _sum_tokens(dconv * m1), _sum_tokens(dconv * u)]

        @pl.when(s == 0)
        def _():
            gcw_ref[...] = jnp.zeros_like(gcw_ref)

        for tap in range(3):
            gcw_ref[tap:tap + 1, :] += parts[tap]

    dproj, gcw = pl.pallas_call(
        body, name="conv_bwd", grid=(nblk,),
        in_specs=[main(512), after(512), main(1536), before(1536), after(1536),
                  pl.BlockSpec((3, 512), lambda s: (0, 0))],
        out_specs=[pl.BlockSpec((3, 1, N_RES, TI, 512), lambda s: (1, s // n_i, 0, s % n_i, 0)),
                   pl.BlockSpec((8, 512), lambda s: (0, 0))],
        out_shape=[jax.ShapeDtypeStruct((6, t // HALF, N_RES, SEG, 512), BF16), jax.ShapeDtypeStruct((8, 512), F32)],
        compiler_params=_params(("arbitrary",), 40),
    )(_x4(dy), _x4(dy), _x4(gates), _x4(gates), _x4(gates), cw)
    return dproj.reshape(6, t, 512), gcw


def _xattn_fwd(x1, g, w_q, kv, w_o, *, tb):
    t, d = x1.shape
    hd = d // N_MEM_HEADS
    m = kv.shape[0]

    def body(x_ref, g_ref, wq_ref, k_ref, v_ref, wo_ref, x2_ref, h_ref, q_ref, o_ref):
        xv = x_ref[...]
        h = _rms_fwd(xv, g_ref[...])[0].astype(BF16)
        h_ref[...] = h
        q = _dot(h, wq_ref[...]).astype(BF16)
        q_ref[...] = q
        for hh in range(N_MEM_HEADS):
            sl = slice(hh * hd, (hh + 1) * hd)
            s = _dot_nt(q[:, sl], k_ref[:, sl]) * (1.0 / 16.0)
            e = jnp.exp(s - jnp.max(s, axis=1, keepdims=True))
            p = e / jnp.sum(e, axis=1, keepdims=True)
            o_ref[:, sl] = _dot(p.astype(BF16), v_ref[:, sl]).astype(BF16)
        x2_ref[...] = xv + _dot(o_ref[...], wo_ref[...])

    tok = pl.BlockSpec((tb, d), lambda i: (i, 0))
    full = pl.BlockSpec((d, d), lambda i: (0, 0))
    return pl.pallas_call(
        body, name="xattn_fwd", grid=(t // tb,),
        in_specs=[tok, pl.BlockSpec((1, d), lambda i: (0, 0)), full,
                  pl.BlockSpec((m, d), lambda i: (0, 0)), pl.BlockSpec((m, d), lambda i: (0, 1)), full],
        out_specs=[tok] * 4,
        out_shape=[jax.ShapeDtypeStruct((t, d), F32)] + [jax.ShapeDtypeStruct((t, d), BF16)] * 3,
        compiler_params=_params(("parallel",), 48),
    )(x1, g, w_q, kv, kv, w_o)


def _xattn_bwd(dx2, x1, g, q, w_q, kv, w_o, *, tb, rider=None):
    t, d = x1.shape
    hd = d // N_MEM_HEADS
    m = kv.shape[0]

    def body(dx2_ref, x_ref, g_ref, q_ref, wq_ref, k_ref, v_ref, wo_ref, dx1_ref, dq_ref, dk_ref, dv_ref, gg_ref):
        i = pl.program_id(0)

        @pl.when(i == 0)
        def _():
            dk_ref[...] = jnp.zeros_like(dk_ref)
            dv_ref[...] = jnp.zeros_like(dv_ref)

        dx2 = dx2_ref[...]
        do = _dot_nt(dx2.astype(BF16), wo_ref[...]).astype(BF16)
        for hh in range(N_MEM_HEADS):
            sl = slice(hh * hd, (hh + 1) * hd)
            qh, kh, vh, doh = q_ref[:, sl], k_ref[:, sl], v_ref[:, sl], do[:, sl]
            s = _dot_nt(qh, kh) * (1.0 / 16.0)
            e = jnp.exp(s - jnp.max(s, axis=1, keepdims=True))
            p = e / jnp.sum(e, axis=1, keepdims=True)
            dp = _dot_nt(doh, vh)
            ds = (p * (dp - jnp.sum(dp * p, axis=1, keepdims=True)) * (1.0 / 16.0)).astype(BF16)
            dq_ref[:, sl] = _dot(ds, kh).astype(BF16)
            dk_ref[:, sl] += _dot_tn(ds, qh)
            dv_ref[:, sl] += _dot_tn(p.astype(BF16), doh)
        dh = _dot_nt(dq_ref[...], wq_ref[...])
        g_v = g_ref[...]
        _, xh, r = _rms_fwd(x_ref[...], g_v)
        dx1_ref[...] = dx2 + _rms_bwd(dh, xh, r, g_v)
        part = jnp.sum(dh * xh, axis=0, keepdims=True)

        @pl.when(i == 0)
        def _():
            gg_ref[...] = part

        @pl.when(i != 0)
        def _():
            gg_ref[...] += part

    tok = pl.BlockSpec((tb, d), lambda i: (i, 0))
    full = pl.BlockSpec((d, d), lambda i: (0, 0))
    acc = pl.BlockSpec((m, d), lambda i: (0, 0))
    res, extra = _pcall(
        body, name="xattn_bwd", grid=(t // tb,),
        in_specs=[tok, tok, pl.BlockSpec((1, d), lambda i: (0, 0)), tok, full,
                  pl.BlockSpec((m, d), lambda i: (0, 0)), pl.BlockSpec((m, d), lambda i: (0, 1)), full],
        out_specs=[tok, tok, acc, acc, pl.BlockSpec((1, d), lambda i: (0, 0))],
        out_shape=[jax.ShapeDtypeStruct((t, d), F32), jax.ShapeDtypeStruct((t, d), BF16),
                   jax.ShapeDtypeStruct((m, d), F32), jax.ShapeDtypeStruct((m, d), F32),
                   jax.ShapeDtypeStruct((1, d), F32)],
        semantics=("arbitrary",), vmem_mb=48, rider=rider,
    )(dx2, x1, g, q, w_q, kv, kv, w_o)
    return res if rider is None else (res, extra)


def _mlp_down_loss(a, w_down, x2, tgt, g, *, tb):
    t, d = x2.shape
    f = a.shape[1]

    def body(a_ref, w_ref, x_ref, t_ref, g_ref, dx_ref, loss_ref, gg_ref):
        i = pl.program_id(0)
        av = a_ref[...].astype(F32)
        x3 = x_ref[...] + _dot((av * av).astype(BF16), w_ref[...])
        g_v = g_ref[...]
        out, xh, r = _rms_fwd(x3, g_v)
        err = out - t_ref[...]
        dout = err * (1.0 / d)
        dx_ref[...] = _rms_bwd(dout, xh, r, g_v)
        part = jnp.sum(dout * xh, axis=0, keepdims=True)
        lpart = 0.5 * jnp.sum(jnp.mean(err * err, axis=-1, keepdims=True), axis=0, keepdims=True)
        lpart = jnp.broadcast_to(lpart, loss_ref.shape)

        @pl.when(i == 0)
        def _():
            gg_ref[...] = part
            loss_ref[...] = lpart

        @pl.when(i != 0)
        def _():
            gg_ref[...] += part
            loss_ref[...] += lpart

    tok = pl.BlockSpec((tb, d), lambda i: (i, 0))
    return pl.pallas_call(
        body, name="mlp_down_loss", grid=(t // tb,),
        in_specs=[pl.BlockSpec((tb, f), lambda i: (i, 0)), pl.BlockSpec((f, d), lambda i: (0, 0)), tok, tok,
                  pl.BlockSpec((1, d), lambda i: (0, 0))],
        out_specs=[tok, pl.BlockSpec((8, 128), lambda i: (0, 0)), pl.BlockSpec((1, d), lambda i: (0, 0))],
        out_shape=[jax.ShapeDtypeStruct((t, d), F32), jax.ShapeDtypeStruct((8, 128), F32),
                   jax.ShapeDtypeStruct((1, d), F32)],
        compiler_params=_params(("arbitrary",), 56),
    )(a, w_down, x2, tgt, g)


def _mlp_dpre(dx3, w_down, a, *, tb):
    t, d = dx3.shape
    nblk, bn, _ = w_down.shape

    def body(dx_ref, w_ref, a_ref, o_ref, dxb):
        @pl.when(pl.program_id(1) == 0)
        def _():
            dxb[...] = dx_ref[...].astype(BF16)

        o_ref[...] = (2.0 * a_ref[...].astype(F32) * _dot_nt(dxb[...], w_ref[0])).astype(BF16)

    return pl.pallas_call(
        body, name="mlp_dpre", grid=(t // tb, nblk),
        in_specs=[pl.BlockSpec((tb, d), lambda i, j: (i, 0)), pl.BlockSpec((1, bn, d), lambda i, j: (j, 0, 0)),
                  pl.BlockSpec((tb, bn), lambda i, j: (i, j))],
        out_specs=pl.BlockSpec((tb, bn), lambda i, j: (i, j)),
        out_shape=jax.ShapeDtypeStruct((t, nblk * bn), BF16),
        scratch_shapes=[pltpu.VMEM((tb, d), BF16)],
        compiler_params=_params(("parallel", "arbitrary"), 40),
    )(dx3, w_down, a)


def _adamw(gsum, w, m, v):
    m_new = ADAM_B1 * m + (1.0 - ADAM_B1) * gsum
    v_new = ADAM_B2 * v + (1.0 - ADAM_B2) * (gsum * gsum)
    m_hat = m_new / (1.0 - ADAM_B1 ** ADAM_STEP)
    v_hat = v_new / (1.0 - ADAM_B2 ** ADAM_STEP)
    delta = -ADAM_LR * (m_hat / (jnp.sqrt(v_hat) + ADAM_EPS) + ADAM_WD * w)
    return delta, m_new, v_new


def _sum_adamw(parts, w, m, v, *, name, tr):
    r, c = w.shape

    def body(p_ref, w_ref, m_ref, v_ref, g_ref, d_ref, mo_ref, vo_ref):
        g = p_ref[0].astype(F32)
        for k in range(1, N_DEV):
            g = g + p_ref[k].astype(F32)
        g_ref[...] = g
        d_ref[...], mo_ref[...], vo_ref[...] = _adamw(g, w_ref[...], m_ref[...], v_ref[...])

    blk = pl.BlockSpec((tr, c), lambda i: (i, 0))
    return pl.pallas_call(
        body, name=name, grid=(r // tr,),
        in_specs=[pl.BlockSpec((N_DEV, tr, c), lambda i: (0, i, 0)), blk, blk, blk],
        out_specs=[blk] * 4, out_shape=[jax.ShapeDtypeStruct((r, c), F32)] * 4,
        compiler_params=_params(("parallel",), 40),
    )(parts, w, m, v)


def _sum_small(parts):
    _, r, c = parts.shape

    def body(p_ref, o_ref):
        s = p_ref[0]
        for k in range(1, N_DEV):
            s = s + p_ref[k]
        o_ref[...] = s

    return pl.pallas_call(body, name="sum_small", out_shape=jax.ShapeDtypeStruct((r, c), F32))(parts)


def _adamw_small(g, w, m, v):
    def body(g_ref, w_ref, m_ref, v_ref, d_ref, mo_ref, vo_ref):
        d_ref[...], mo_ref[...], vo_ref[...] = _adamw(g_ref[...], w_ref[...], m_ref[...], v_ref[...])

    return pl.pallas_call(body, name="adamw_small", out_shape=[jax.ShapeDtypeStruct(g.shape, F32)] * 3)(g, w, m, v)


def _head_sum_matrix():
    r = lax.broadcasted_iota(jnp.int32, (512, 512), 0) // HEAD_DIM
    c = lax.broadcasted_iota(jnp.int32, (512, 512), 1) // HEAD_DIM
    return (r == c).astype(BF16)


_ROW_SHARDED = ("w_out", "w_q", "w_o", "w_down")


class _Weights:
    def __init__(self, full, shards=None):
        self.full = dict(full)
        self.shards = shards

    def rider(self, names):
        return None if self.shards is None else _Gather([self.shards[n] for n in names])

    def arrived(self, names, gathered):
        if gathered is not None:
            for n, g in zip(names, gathered):
                self.full[n] = g.reshape(-1, g.shape[-1]) if n in _ROW_SHARDED else g

    def __getitem__(self, name):
        return self.full[name]


class _Grads:
    def __init__(self, distributed):
        self.distributed = distributed
        self.local = {}
        self.received = {}

    def add(self, name, g):
        self.local[name] = g.reshape(N_DEV, -1, g.shape[-1]) if name in _ROW_SHARDED else g

    def rider(self, names):
        return _Exchange([self.local[n] for n in names]) if self.distributed else None

    def arrived(self, names, received):
        if received is not None:
            for n, r in zip(names, received):
                self.received[n] = r


def _ride(fn, *args, rider=None, **kw):
    if rider is None:
        return fn(*args, **kw), None
    return fn(*args, rider=rider, **kw)


def _local_step(x, mem, tgt, gains, weights, cw, grads):
    t, d = x.shape
    w_in = weights["w_in"]
    x = _reorder(x, False, "reorder_x")
    tgt = _reorder(tgt, False, "reorder_target")

    names = ["w_q", "w_kv", "w_o"]
    (qkv, h1), got = _ride(_norm_matmul, x, gains["g_mix"], w_in, name="proj_qkv", out_dtype=F32, tb=512, blk0=0,
                           nblk=4, save_h=True, rider=weights.rider(names))
    weights.arrived(names, got)
    gates, got = _ride(_norm_matmul, x, gains["g_mix"], w_in, name="proj_gates", out_dtype=F32, tb=512, blk0=4,
                       nblk=4, rider=weights.rider(["w_up"]))
    weights.arrived(["w_up"], got)
    (attn, *lses), got = _ride(_attention_fwd, qkv, rider=weights.rider(["w_down"]))
    weights.arrived(["w_down"], got)
    x1, merged = _mixer_fwd(x, attn, gates, cw, gains["g_attn_out"], gains["g_conv_out"], weights["w_out"])
    kv, mem_n = _norm_matmul(mem, gains["g_mem"], weights["w_kv"], name="mem_kv", out_dtype=BF16, tb=mem.shape[0],
                             save_h=True)
    x2, h2, qm, om = _xattn_fwd(x1, gains["g_xattn"], weights["w_q"], kv, weights["w_o"], tb=256)
    w_up = weights["w_up"]
    a, h3 = _norm_matmul(x2, gains["g_mlp"], w_up, name="mlp_up", out_dtype=BF16, tb=512, relu=True, save_h=True)
    dx3, loss_blk, gg_final = _mlp_down_loss(a, weights["w_down"], x2, tgt, gains["g_final"], tb=256)

    w_down_blocks = weights["w_down"].reshape(N_DEV, -1, d)
    dpre = _mlp_dpre(dx3, w_down_blocks, a, tb=512)
    grads.add("w_down", _matmul_tn(a, dx3, name="grad_w_down", bm=512, bn=512, bt=1024, square_a=True))
    gw_up, got = _ride(_matmul_tn, h3, dpre, name="grad_w_up", bm=512, bn=512, bt=1024, col_sharded=True,
                       rider=grads.rider(["w_down"]))
    grads.arrived(["w_down"], got)
    grads.add("w_up", gw_up)
    (dx2, gg_mlp), got = _ride(_matmul_nt_normbwd, dpre, w_up, x2, gains["g_mlp"], dx3, name="mlp_dx", tb=512,
                               rider=grads.rider(["w_up"]))
    grads.arrived(["w_up"], got)

    grads.add("w_o", _matmul_tn(om, dx2, name="grad_w_o", bm=512, bn=512, bt=1024))
    (dx1, dqm, dk, dv, gg_xattn), got = _ride(_xattn_bwd, dx2, x1, gains["g_xattn"], qm, weights["w_q"], kv,
                                              weights["w_o"], tb=256, rider=grads.rider(["w_o"]))
    grads.arrived(["w_o"], got)
    grads.add("w_q", _matmul_tn(h2, dqm, name="grad_w_q", bm=512, bn=512, bt=1024))
    dkv = jnp.concatenate([dk, dv], axis=1).astype(BF16)
    grads.add("w_kv", _matmul_tn(mem_n, dkv, name="grad_w_kv", bm=512, bn=256, bt=mem.shape[0], col_sharded=True))
    _, gg_mem = _matmul_nt_normbwd(dkv, weights["w_kv"], mem, gains["g_mem"], None, name="mem_dx", tb=mem.shape[0])

    grads.add("w_out", _matmul_tn(merged, dx1, name="grad_w_out", bm=512, bn=512, bt=1024))
    names = ["w_q", "w_out"]
    (dattn, dsum, dy, gg_attn, gg_conv), got = _ride(
        _mixer_bwd, dx1, attn, gates, cw, gains["g_attn_out"], gains["g_conv_out"], weights["w_out"],
        _head_sum_matrix(), rider=grads.rider(names))
    grads.arrived(names, got)
    dproj, gcw = _conv_bwd(dy, gates, cw)
    dproj, got = _ride(_attention_bwd, qkv, dattn, dsum, lses, dproj, rider=grads.rider(["w_kv"]))
    grads.arrived(["w_kv"], got)
    gw_in = _matmul_tn(h1, dproj, name="grad_w_in", bm=512, bn=512, bt=1024)
    grads.add("w_in", gw_in.reshape(d, N_DEV, -1).transpose(1, 0, 2))
    w_in_cols = w_in.transpose(1, 0, 2).reshape(d, -1)
    (grad_x, gg_mix), got = _ride(_matmul_nt_normbwd, dproj, w_in_cols, x, gains["g_mix"], dx1, name="mixer_dx",
                                  tb=512, stacked=True, rider=grads.rider(["w_in"]))
    grads.arrived(["w_in"], got)
    grad_x = _reorder(grad_x, True, "reorder_grad_x")

    small = dict(g_mix=gg_mix, g_attn_out=gg_attn, g_conv_out=gg_conv, g_xattn=gg_xattn, g_mem=gg_mem,
                 g_mlp=gg_mlp, g_final=gg_final, conv_w=gcw[0:3], loss=loss_blk[0:1, 0:1])
    return grad_x, small


_BIG = ("w_in", "w_out", "w_q", "w_kv", "w_o", "w_up", "w_down")
_GAIN_ROWS = ("g_mix", "g_xattn", "g_mem", "g_mlp", "g_final")


def _pack_small(vals, conv):
    rows = [vals[k].reshape(1, -1) for k in _GAIN_ROWS]
    rows.append(jnp.concatenate([vals["g_attn_out"].reshape(1, -1), vals["g_conv_out"].reshape(1, -1)], axis=1))
    flat = conv.reshape(1, -1)
    rows.append(jnp.pad(flat, ((0, 0), (0, 1024 - flat.shape[1]))))
    rows.append(jnp.zeros((1, 1024), F32))
    return jnp.concatenate(rows, axis=0)


def kernel(x, mem, g_mix, w_in, conv_w, g_attn_out, g_conv_out, w_out, g_xattn, g_mem, w_q_mem, w_kv_mem, w_o_mem, g_mlp, w_up, w_down, g_final, loss_target, m_g_mix, m_w_in, m_conv_w, m_g_attn_out, m_g_conv_out, m_w_out, m_g_xattn, m_g_mem, m_w_q_mem, m_w_kv_mem, m_w_o_mem, m_g_mlp, m_w_up, m_w_down, m_g_final, v_g_mix, v_w_in, v_conv_w, v_g_attn_out, v_g_conv_out, v_w_out, v_g_xattn, v_g_mem, v_w_q_mem, v_w_kv_mem, v_w_o_mem, v_g_mlp, v_w_up, v_w_down, v_g_final):
    d = x.shape[-1]
    me = 4 * lax.axis_index("x") + 2 * lax.axis_index("y") + lax.axis_index("c")
    w_shards = dict(w_in=w_in, w_out=w_out, w_q=w_q_mem, w_kv=w_kv_mem, w_o=w_o_mem, w_up=w_up, w_down=w_down)
    m_shards = dict(w_in=m_w_in, w_out=m_w_out, w_q=m_w_q_mem, w_kv=m_w_kv_mem, w_o=m_w_o_mem, w_up=m_w_up,
                    w_down=m_w_down)
    v_shards = dict(w_in=v_w_in, w_out=v_w_out, w_q=v_w_q_mem, w_kv=v_w_kv_mem, w_o=v_w_o_mem, w_up=v_w_up,
                    w_down=v_w_down)
    gains = dict(g_mix=g_mix, g_attn_out=g_attn_out, g_conv_out=g_conv_out, g_xattn=g_xattn, g_mem=g_mem,
                 g_mlp=g_mlp, g_final=g_final)
    gains2 = {k: v.reshape(1, -1) for k, v in gains.items()}

    shards = {k: w_shards[k].astype(BF16) for k in _BIG}
    first = _comm_call(_Gather([shards["w_in"], shards["w_out"], conv_w]), name="gather_first")
    weights = _Weights({}, shards)
    weights.arrived(["w_in", "w_out"], first[:2])
    cw = first[2].transpose(1, 0, 2).reshape(3, -1)

    grads = _Grads(distributed=True)
    grad_x, small = _local_step(x[0], mem[0], loss_target[0], gains2, weights, cw, grads)

    small_rows = [small[k] for k in _GAIN_ROWS]
    small_rows.append(jnp.concatenate([small["g_attn_out"], small["g_conv_out"]], axis=1))
    small_rows.append(jnp.pad(small["conv_w"], ((0, 0), (0, 512))))
    small_rows.append(jnp.pad(small["loss"], ((0, 6), (0, 1023))))
    small_part = jnp.concatenate(small_rows, axis=0)
    small_received = _comm_call(_Exchange([small_part], [True]), name="exchange_small")[0]

    outs = {}
    tiles = dict(w_in=256, w_out=128, w_q=128, w_kv=256, w_o=128, w_up=256, w_down=256)
    for k in _BIG:
        outs[k] = _sum_adamw(grads.received[k], w_shards[k], m_shards[k], v_shards[k], name=f"adamw_{k}",
                             tr=tiles[k])

    ssum = _sum_small(small_received)
    loss = ssum[9, 0]
    g_small = {k: ssum[i] for i, k in enumerate(_GAIN_ROWS)}
    g_small["g_attn_out"] = ssum[5, 0:512]
    g_small["g_conv_out"] = ssum[5, 512:1024]
    g_conv = lax.dynamic_slice_in_dim(ssum[6:9, 0:512], me * 64, 64, axis=1)
    m_small = dict(g_mix=m_g_mix, g_attn_out=m_g_attn_out, g_conv_out=m_g_conv_out, g_xattn=m_g_xattn,
                   g_mem=m_g_mem, g_mlp=m_g_mlp, g_final=m_g_final)
    v_small = dict(g_mix=v_g_mix, g_attn_out=v_g_attn_out, g_conv_out=v_g_conv_out, g_xattn=v_g_xattn,
                   g_mem=v_g_mem, g_mlp=v_g_mlp, g_final=v_g_final)
    packed = [_pack_small(g_small, g_conv), _pack_small(gains, conv_w), _pack_small(m_small, m_conv_w),
              _pack_small(v_small, v_conv_w)]
    upd = _adamw_small(*packed)

    def unpack(p):
        res = {k: p[i] for i, k in enumerate(_GAIN_ROWS)}
        res["g_attn_out"] = p[5, 0:512]
        res["g_conv_out"] = p[5, 512:1024]
        res["conv_w"] = p[6, 0:192].reshape(3, 64)
        return res

    g_small["conv_w"] = g_conv
    small_out = [g_small] + [unpack(p) for p in upd]
    names = {"g_mix": "g_mix", "w_in": "w_in", "conv_w": "conv_w", "g_attn_out": "g_attn_out",
             "g_conv_out": "g_conv_out", "w_out": "w_out", "g_xattn": "g_xattn", "g_mem": "g_mem",
             "w_q_mem": "w_q", "w_kv_mem": "w_kv", "w_o_mem": "w_o", "g_mlp": "g_mlp", "w_up": "w_up",
             "w_down": "w_down", "g_final": "g_final"}
    result = [loss, grad_x[None]]
    for which in range(4):
        for key in names.values():
            result.append(outs[key][which] if key in outs else small_out[which][key])
    return tuple(result)
```

```python
import math

import jax
import jax.numpy as jnp
from jax import lax
from jax.experimental import pallas as pl
from jax.experimental.pallas import tpu as pltpu

F32 = jnp.float32
BF16 = jnp.bfloat16
NORM_EPS = 1e-6
NEG_INF = -1e30
N_DEV = 8
BLK = 128
HEAD_DIM = 64
N_MEM_HEADS = 4
ADAM_LR = 0.001
ADAM_B1 = 0.9
ADAM_B2 = 0.999
ADAM_EPS = 1e-08
ADAM_WD = 0.01
ADAM_STEP = 10
MESH = pl.DeviceIdType.MESH
ANY = pl.BlockSpec(memory_space=pl.ANY)


def _dot(a, b):
    return jnp.dot(a, b, preferred_element_type=F32)


def _dot_nt(a, b):
    return lax.dot_general(a, b, (((1,), (1,)), ((), ())), preferred_element_type=F32)


def _dot_tn(a, b):
    return lax.dot_general(a, b, (((0,), (0,)), ((), ())), preferred_element_type=F32)


def _params(semantics, vmem_mb):
    return pltpu.CompilerParams(dimension_semantics=semantics, vmem_limit_bytes=vmem_mb << 20)


def _rms_fwd(x, g):
    r = lax.rsqrt(jnp.mean(x * x, axis=-1, keepdims=True) + NORM_EPS)
    xh = x * r
    return xh * g, xh, r


def _rms_bwd(dy, xh, r, g):
    gy = dy * g
    return r * (gy - xh * jnp.mean(xh * gy, axis=-1, keepdims=True))


def _position():
    x, y, c = lax.axis_index("x"), lax.axis_index("y"), lax.axis_index("c")
    return x, y, c


def _block_of(ref, j, axis, shard_shape):
    r, c = shard_shape
    if axis is None:
        return ref.at[j]
    if axis == 0:
        return ref.at[pl.ds(j * r, r), :]
    return ref.at[:, pl.ds(j * c, c)]


class _Gather:
    has_mid = True

    def __init__(self, shards, axes):
        self.arrays = list(shards)
        self.axes = list(axes)
        self.n = len(self.arrays)

    def out_shape(self):
        res = []
        for s, axis in zip(self.arrays, self.axes):
            r, c = s.shape
            shape = (N_DEV, r, c) if axis is None else (N_DEV * r, c) if axis == 0 else (r, N_DEV * c)
            res.append(jax.ShapeDtypeStruct(shape, s.dtype))
        return res

    def scratch(self):
        return [pltpu.SemaphoreType.DMA((self.n, 7)), pltpu.SemaphoreType.DMA((self.n, 7)),
                pltpu.SemaphoreType.DMA((self.n,))]

    def _ctx(self, ins, outs, sems):
        send_sems, recv_sems, local_sems = sems
        x, y, c = _position()
        me, sibling = (x, y, c), (x, y, 1 - c)
        chips = [(1 - x, y), (x, 1 - y), (1 - x, 1 - y)]

        def lin(px, py, pc):
            return 4 * px + 2 * py + pc

        def place(a, block):
            return _block_of(outs[a], lin(*block), self.axes[a], self.arrays[a].shape)

        def copy(a, k, block, to, src=None):
            dst = place(a, block)
            return pltpu.make_async_remote_copy(
                src_ref=dst if src is None else src, dst_ref=dst,
                send_sem=send_sems.at[a, k], recv_sem=recv_sems.at[a, k],
                device_id=to, device_id_type=MESH)

        def mine():
            return [pltpu.make_async_copy(ins[a], place(a, me), local_sems.at[a]) for a in range(self.n)]

        def first():
            res = []
            for a in range(self.n):
                res.append(copy(a, 0, me, sibling, src=ins[a]))
                res += [copy(a, 1 + j, me, (*chip, c), src=ins[a]) for j, chip in enumerate(chips)]
            return res

        return c, me, sibling, chips, copy, mine, first

    def start(self, ins, outs, sems):
        _, _, _, _, _, mine, first = self._ctx(ins, outs, sems)
        for cp in mine() + first():
            cp.start()

    def mid(self, ins, outs, sems):
        c, me, sibling, chips, copy, _, _ = self._ctx(ins, outs, sems)
        for j, chip in enumerate(chips):
            for a in range(self.n):
                copy(a, 1 + j, (*chip, c), me).wait_recv()
                copy(a, 4 + j, (*chip, c), sibling).start()

    def finish(self, ins, outs, sems):
        c, me, sibling, chips, copy, mine, first = self._ctx(ins, outs, sems)
        for a in range(self.n):
            copy(a, 0, sibling, me).wait_recv()
            for j, chip in enumerate(chips):
                copy(a, 4 + j, (*chip, 1 - c), me).wait_recv()
        for cp in first():
            cp.wait_send()
        for j, chip in enumerate(chips):
            for a in range(self.n):
                copy(a, 4 + j, (*chip, c), sibling).wait_send()
        for cp in mine():
            cp.wait()


class _Exchange:
    has_mid = False

    def __init__(self, parts, axes):
        self.arrays = list(parts)
        self.axes = list(axes)
        self.n = len(self.arrays)

    def _piece(self, a):
        r, c = self.arrays[a].shape
        axis = self.axes[a]
        return (r, c) if axis is None else (r // N_DEV, c) if axis == 0 else (r, c // N_DEV)

    def out_shape(self):
        return [jax.ShapeDtypeStruct((N_DEV,) + self._piece(a), self.arrays[a].dtype) for a in range(self.n)]

    def scratch(self):
        return [pltpu.SemaphoreType.DMA((self.n, 7)), pltpu.SemaphoreType.DMA((self.n, 7)),
                pltpu.SemaphoreType.DMA((self.n,))]

    def _ctx(self, ins, outs, sems):
        send_sems, recv_sems, local_sems = sems
        x, y, c = _position()
        me = 4 * x + 2 * y + c

        def src(a, j):
            if self.axes[a] is None:
                return ins[a]
            return _block_of(ins[a], j, self.axes[a], self._piece(a))

        def local():
            return [pltpu.make_async_copy(src(a, me), outs[a].at[me], local_sems.at[a]) for a in range(self.n)]

        def remote(inbound):
            res = []
            for a in range(self.n):
                for k in range(1, N_DEV):
                    peer = (1 - x if k & 4 else x, 1 - y if k & 2 else y, 1 - c if k & 1 else c)
                    plin = 4 * peer[0] + 2 * peer[1] + peer[2]
                    res.append(pltpu.make_async_remote_copy(
                        src_ref=src(a, plin), dst_ref=outs[a].at[plin if inbound else me],
                        send_sem=send_sems.at[a, k - 1], recv_sem=recv_sems.at[a, k - 1],
                        device_id=peer, device_id_type=MESH))
            return res

        return local, remote

    def start(self, ins, outs, sems):
        local, remote = self._ctx(ins, outs, sems)
        for cp in local() + remote(False):
            cp.start()

    def finish(self, ins, outs, sems):
        local, remote = self._ctx(ins, outs, sems)
        for cp in remote(True):
            cp.wait_recv()
        for cp in remote(False):
            cp.wait_send()
        for cp in local():
            cp.wait()


def _comm_call(rider, name):
    n_in, n_out = len(rider.arrays), len(rider.out_shape())

    def body(*refs):
        ins, outs, sems = refs[:n_in], refs[n_in:n_in + n_out], refs[n_in + n_out:]
        rider.start(ins, outs, sems)
        if rider.has_mid:
            rider.mid(ins, outs, sems)
        rider.finish(ins, outs, sems)

    return pl.pallas_call(
        body, name=name, out_shape=rider.out_shape(),
        in_specs=[ANY] * n_in, out_specs=[ANY] * n_out, scratch_shapes=rider.scratch(),
    )(*rider.arrays)


def _pcall(body, *, name, grid, in_specs, out_specs, out_shape, scratch_shapes=(), semantics, vmem_mb, rider=None,
           aliases=None):
    in_specs, out_specs, out_shape = list(in_specs), list(out_specs), list(out_shape)
    scratch_shapes = list(scratch_shapes)
    aliases = dict(aliases or {})
    if rider is None:
        call = pl.pallas_call(body, name=name, grid=grid, in_specs=in_specs, out_specs=out_specs,
                              out_shape=out_shape, scratch_shapes=scratch_shapes, input_output_aliases=aliases,
                              compiler_params=_params(semantics, vmem_mb))
        return lambda *args: (list(call(*args)), None)
    n_in, n_out, n_scr = len(in_specs), len(out_specs), len(scratch_shapes)
    r_in, r_shapes = len(rider.arrays), rider.out_shape()
    r_out = len(r_shapes)
    total = math.prod(grid)
    mid_step = (3 * total) // 4

    def wrapped(*refs):
        bounds = [0, n_in, r_in, n_out, r_out, n_scr]
        for i in range(1, len(bounds)):
            bounds[i] += bounds[i - 1]
        a, ra, o, ro, s = (refs[bounds[i]:bounds[i + 1]] for i in range(5))
        rs = refs[bounds[5]:]
        step = pl.program_id(0)
        for k in range(1, len(grid)):
            step = step * grid[k] + pl.program_id(k)
        pl.when(step == 0)(lambda: rider.start(ra, ro, rs))
        body(*a, *o, *s)
        if rider.has_mid:
            pl.when(step == mid_step)(lambda: rider.mid(ra, ro, rs))
        pl.when(step == total - 1)(lambda: rider.finish(ra, ro, rs))

    call = pl.pallas_call(
        wrapped, name=name, grid=grid, in_specs=in_specs + [ANY] * r_in, out_specs=out_specs + [ANY] * r_out,
        out_shape=out_shape + r_shapes, scratch_shapes=scratch_shapes + rider.scratch(),
        input_output_aliases=aliases, compiler_params=_params(("arbitrary",) * len(grid), vmem_mb))

    def run(*args):
        res = call(*args, *rider.arrays)
        return list(res[:n_out]), list(res[n_out:])

    return run


def _norm_matmul(x, g, w, *, name, out_dtype, tb, bn, relu=False, save_h=False, rider=None):
    t, d = x.shape
    n = w.shape[1]

    def body(x_ref, g_ref, w_ref, o_ref, *rest):
        h_scr = rest[-1]

        @pl.when(pl.program_id(1) == 0)
        def _():
            h = _rms_fwd(x_ref[...], g_ref[...])[0].astype(BF16)
            h_scr[...] = h
            if save_h:
                rest[0][...] = h

        acc = _dot(h_scr[...], w_ref[...])
        if relu:
            acc = jnp.maximum(acc, 0.0)
        o_ref[...] = acc.astype(out_dtype)

    out_shape = [jax.ShapeDtypeStruct((t, n), out_dtype)]
    out_specs = [pl.BlockSpec((tb, bn), lambda i, j: (i, j))]
    if save_h:
        out_shape.append(jax.ShapeDtypeStruct((t, d), BF16))
        out_specs.append(pl.BlockSpec((tb, d), lambda i, j: (i, 0)))
    res, extra = _pcall(
        body, name=name, grid=(t // tb, n // bn),
        in_specs=[pl.BlockSpec((tb, d), lambda i, j: (i, 0)),
                  pl.BlockSpec((1, d), lambda i, j: (0, 0)),
                  pl.BlockSpec((d, bn), lambda i, j: (0, j))],
        out_specs=out_specs, out_shape=out_shape,
        scratch_shapes=[pltpu.VMEM((tb, d), BF16)],
        semantics=("parallel", "arbitrary"), vmem_mb=48, rider=rider,
    )(x, g, w)
    res = res if save_h else res[0]
    return res if rider is None else (res, extra)


def _matmul_nt_normbwd(dy, w, x, g, dres, *, name, tb, also_bf16=False, rider=None):
    t, d = x.shape
    stacked = dy.ndim == 3
    has_res = dres is not None

    def body(dy_ref, w_ref, x_ref, g_ref, *rest):
        rest = list(rest)
        dres_ref = rest.pop(0) if has_res else None
        dx_ref = rest.pop(0)
        dxb_ref = rest.pop(0) if also_bf16 else None
        gg_ref = rest.pop(0)
        i = pl.program_id(0)
        if stacked:
            kb = dy_ref.shape[2]
            dh = _dot_nt(dy_ref[0], w_ref[:, 0:kb])
            for s in range(1, dy_ref.shape[0]):
                dh = dh + _dot_nt(dy_ref[s], w_ref[:, s * kb:(s + 1) * kb])
        else:
            dh = _dot_nt(dy_ref[...], w_ref[...])
        g_v = g_ref[...]
        _, xh, r = _rms_fwd(x_ref[...], g_v)
        dx = _rms_bwd(dh, xh, r, g_v)
        if has_res:
            dx = dx + dres_ref[...]
        dx_ref[...] = dx
        if also_bf16:
            dxb_ref[...] = dx.astype(BF16)
        part = jnp.sum(dh * xh, axis=0, keepdims=True)

        @pl.when(i == 0)
        def _():
            gg_ref[...] = part

        @pl.when(i != 0)
        def _():
            gg_ref[...] += part

    tok = pl.BlockSpec((tb, d), lambda i: (i, 0))
    row = pl.BlockSpec((1, d), lambda i: (0, 0))
    if stacked:
        dy_spec = pl.BlockSpec((dy.shape[0], tb, dy.shape[2]), lambda i: (0, i, 0))
    else:
        dy_spec = pl.BlockSpec((tb, dy.shape[1]), lambda i: (i, 0))
    in_specs = [dy_spec, pl.BlockSpec(w.shape, lambda i: (0, 0)), tok, row]
    args = [dy, w, x, g]
    if has_res:
        in_specs.append(tok)
        args.append(dres)
    out_specs = [tok] + ([tok] if also_bf16 else []) + [row]
    out_shape = ([jax.ShapeDtypeStruct((t, d), F32)] + ([jax.ShapeDtypeStruct((t, d), BF16)] if also_bf16 else [])
                 + [jax.ShapeDtypeStruct((1, d), F32)])
    res, extra = _pcall(
        body, name=name, grid=(t // tb,), in_specs=in_specs, out_specs=out_specs, out_shape=out_shape,
        semantics=("arbitrary",), vmem_mb=56, rider=rider,
    )(*args)
    return res if rider is None else (res, extra)


def _matmul_tn(a, b, *, name, bm, bn, square_a=False, rider=None):
    t, m = a.shape
    stacked = b.ndim == 3
    n = b.shape[0] * bn if stacked else b.shape[1]

    def body(a_ref, b_ref, o_ref):
        av = a_ref[...]
        if square_a:
            av = av.astype(F32)
            av = (av * av).astype(BF16)
        o_ref[...] = _dot_tn(av, b_ref[...]).astype(BF16)

    res, extra = _pcall(
        body, name=name, grid=(m // bm, n // bn),
        in_specs=[pl.BlockSpec((t, bm), lambda i, j: (0, i)),
                  pl.BlockSpec((None, t, bn), lambda i, j: (j, 0, 0)) if stacked
                  else pl.BlockSpec((t, bn), lambda i, j: (0, j))],
        out_specs=[pl.BlockSpec((bm, bn), lambda i, j: (i, j))], out_shape=[jax.ShapeDtypeStruct((m, n), BF16)],
        semantics=("parallel", "parallel"), vmem_mb=56, rider=rider,
    )(a, b)
    return res[0] if rider is None else (res[0], extra)


N_RES = 16
SEG = 128
HALF = N_RES * SEG
TI = 16


def _x4(a):
    return a.reshape(a.shape[0] // HALF, N_RES, SEG, a.shape[1])


def _reorder(a, inverse, name):
    t, c = a.shape
    n_i = SEG // TI
    natural = pl.BlockSpec((TI * N_RES, c), lambda s: (s, 0))
    major = pl.BlockSpec((1, N_RES, TI, c), lambda s: (s // n_i, 0, s % n_i, 0))

    def body(i_ref, o_ref, scr):
        for cb in range(c // BLK):
            cols = slice(cb * BLK, (cb + 1) * BLK)
            slab = scr.at[cb]
            if inverse:
                for r in range(N_RES):
                    slab[pl.ds(r, TI, stride=N_RES), :] = i_ref[0, r, :, cols]
                o_ref[:, cols] = slab[...]
            else:
                slab[...] = i_ref[:, cols]
                for r in range(N_RES):
                    o_ref[0, r, :, cols] = slab[pl.ds(r, TI, stride=N_RES), :]

    scratch = [pltpu.VMEM((c // BLK, TI * N_RES, BLK), a.dtype)]
    if inverse:
        return pl.pallas_call(
            body, name=name, grid=(t // (TI * N_RES),), in_specs=[major], out_specs=natural,
            out_shape=jax.ShapeDtypeStruct((t, c), a.dtype), scratch_shapes=scratch,
            compiler_params=_params(("parallel",), 32))(_x4(a))
    return pl.pallas_call(
        body, name=name, grid=(t // (TI * N_RES),), in_specs=[natural], out_specs=major,
        out_shape=jax.ShapeDtypeStruct((t // HALF, N_RES, SEG, c), a.dtype), scratch_shapes=scratch,
        compiler_params=_params(("parallel",), 32))(a).reshape(t, c)


_PATTERNS = ((1, 16, 8, SEG), (4, 4, 32, 4 * SEG), (16, 1, SEG, 0))
_FIRST = {1: 1, 4: 4, 16: 16}


def _group_rows(d, g):
    a = g >> 4
    if d == 16:
        base = a * HALF + (g & 15) * SEG
        prev = base - HALF
    elif d == 4:
        c = (g >> 2) & 3
        base = a * HALF + (g & 3) * SEG + c * 32
        prev = jnp.where(c > 0, base - 32, base - HALF + 96)
    else:
        c = g & 15
        base = a * HALF + c * 8
        prev = jnp.where(c > 0, base - 8, base - HALF + 120)
    return base, prev


def _load_rows(ref, base, n, rows, stride):
    parts = [ref[pl.ds(pl.multiple_of(base + j * stride, 8), rows), :] for j in range(n)]
    return parts[0] if n == 1 else jnp.concatenate(parts, axis=0)


def _store_rows(ref, base, val, n, rows, stride, add=False):
    for j in range(n):
        sl = pl.ds(pl.multiple_of(base + j * stride, 8), rows)
        piece = val[j * rows:(j + 1) * rows, :]
        if add:
            ref[sl, :] += piece
        else:
            ref[sl, :] = piece


def _band_bias(n, rows):
    shift = rows.bit_length() - 1
    lq = lax.broadcasted_iota(jnp.int32, (BLK, BLK), 0)
    lk = lax.broadcasted_iota(jnp.int32, (BLK, BLK), 1)
    iq = (lq & (rows - 1)) * n + (lq >> shift)
    ik = (lk & (rows - 1)) * n + (lk >> shift)
    zero = jnp.zeros((BLK, BLK), F32)
    return jnp.where(ik >= iq, zero, NEG_INF), jnp.where(ik <= iq, zero, NEG_INF)


def _set_bias(bias_scr, n, rows):
    prev_b, cur_b = _band_bias(n, rows)
    for half in range(2):
        bias_scr[half * BLK:(half + 1) * BLK, 0:BLK] = prev_b
        bias_scr[half * BLK:(half + 1) * BLK, BLK:2 * BLK] = cur_b


def _head_consts():
    lane_lo = lax.broadcasted_iota(jnp.int32, (BLK, BLK), 1) < HEAD_DIM
    return lane_lo, [jnp.where(lane_lo, 1.0, 0.0).astype(BF16), jnp.where(lane_lo, 0.0, 1.0).astype(BF16)]


def _stack_heads(v, head_mask):
    return jnp.concatenate([v * head_mask[0], v * head_mask[1]], axis=0)


def _unstack_heads(v2, lane_lo):
    return jnp.where(lane_lo, v2[:BLK], v2[BLK:])


def _rows_per_head(v, lane_lo):
    rolled = pltpu.roll(v, HEAD_DIM, axis=1)
    return jnp.concatenate([jnp.where(lane_lo, v, rolled), jnp.where(lane_lo, rolled, v)], axis=0)


WIDTH = 4


def _loop(lo, hi, fn, width=None):
    if width is None:
        def body(g, carry):
            fn(g)
            return carry

        if hi > lo:
            lax.fori_loop(lo, hi, body, 0)
        return
    while hi > lo:
        trips = (hi - lo) // width
        if trips:
            def body(i, carry, lo=lo, width=width):
                fn([lo + width * i + j for j in range(width)])
                return carry

            lax.fori_loop(0, trips, body, 0)
            lo += trips * width
        width = max(1, width // 2)


def _mix_weights(l1, l2, l3):
    mx = jnp.maximum(jnp.maximum(l1, l2), l3)
    e1, e2, e3 = jnp.exp(l1 - mx), jnp.exp(l2 - mx), jnp.exp(l3 - mx)
    inv = 1.0 / (e1 + e2 + e3)
    return e1 * inv, e2 * inv, e3 * inv


def _attention_fwd(qkv, rider=None):
    t = qkv.shape[0]
    groups = 16 * (t // HALF)

    def body(q_ref, k_ref, v_ref, attn_ref, l1_ref, l2_ref, l3_ref, o_scr, bias_scr):
        lane_lo, head_mask = _head_consts()
        l_refs = (l1_ref, l2_ref, l3_ref)
        for p, (d, n, rows, stride) in enumerate(_PATTERNS):
            _set_bias(bias_scr, n, rows)
            o_p, l_p = o_scr.at[p], l_refs[p]

            def block(gs, has_prev):
                at = [_group_rows(d, g) for g in gs]

                def load(ref, b):
                    return _load_rows(ref, b, n, rows, stride).astype(BF16)

                q2 = [_stack_heads(load(q_ref, b), head_mask) for b, _ in at]
                k2 = [load(k_ref, b) for b, _ in at]
                v2 = [load(v_ref, b) for b, _ in at]
                if has_prev:
                    k2 = [jnp.concatenate([load(k_ref, pv), k], axis=0) for (_, pv), k in zip(at, k2)]
                    v2 = [jnp.concatenate([load(v_ref, pv), v], axis=0) for (_, pv), v in zip(at, v2)]
                s = [_dot_nt(q, k) for q, k in zip(q2, k2)]
                s = [x * 0.125 + (bias_scr[...] if has_prev else bias_scr[:, BLK:2 * BLK]) for x in s]
                mx = [jnp.max(x, axis=1, keepdims=True) for x in s]
                e = [jnp.exp(x - m) for x, m in zip(s, mx)]
                den = [jnp.sum(x, axis=1, keepdims=True) for x in e]
                pb = [(x * (1.0 / dn)).astype(BF16) for x, dn in zip(e, den)]
                o2 = [_dot(x, v) for x, v in zip(pb, v2)]
                lse2 = [jnp.broadcast_to(m + jnp.log(dn), (2 * BLK, BLK)) for m, dn in zip(mx, den)]
                for (b, _), o, l in zip(at, o2, lse2):
                    _store_rows(o_p, b, _unstack_heads(o, lane_lo), n, rows, stride)
                    _store_rows(l_p, b, _unstack_heads(l, lane_lo), n, rows, stride)

            _loop(0, _FIRST[d], lambda gs: block(gs, False), width=WIDTH)
            _loop(_FIRST[d], groups, lambda gs: block(gs, True), width=WIDTH)

        def mix(i):
            sl = pl.ds(pl.multiple_of(i * 256, 256), 256)
            w = _mix_weights(l1_ref[sl, :], l2_ref[sl, :], l3_ref[sl, :])
            attn_ref[sl, :] = w[0] * o_scr[0, sl, :] + w[1] * o_scr[1, sl, :] + w[2] * o_scr[2, sl, :]

        _loop(0, t // 256, mix)

    def col(c0):
        return pl.BlockSpec((t, BLK), lambda hp: (0, c0 + hp))

    res, extra = _pcall(
        body, name="attention_fwd", grid=(4,), in_specs=[col(0), col(4), col(8)], out_specs=[col(0)] * 4,
        out_shape=[jax.ShapeDtypeStruct((t, 512), F32)] * 4,
        scratch_shapes=[pltpu.VMEM((3, t, BLK), F32), pltpu.VMEM((2 * BLK, 2 * BLK), F32)],
        semantics=("parallel",), vmem_mb=48, rider=rider,
    )(qkv, qkv, qkv)
    return res if rider is None else (res, extra)


def _attention_bwd(qkv, dattn, dsum, lses, dproj, rider=None):
    t = qkv.shape[0]
    groups = 16 * (t // HALF)

    def body(q_ref, k_ref, v_ref, da_ref, ds_ref, l1_ref, l2_ref, l3_ref, kept_ref, out_ref, acc, bias_scr):
        del kept_ref
        lane_lo, head_mask = _head_consts()
        l_refs = (l1_ref, l2_ref, l3_ref)

        def clear(i):
            sl = pl.ds(pl.multiple_of(i * 512, 512), 512)
            for s in range(3):
                acc[s, sl, :] = jnp.zeros((512, BLK), F32)

        _loop(0, t // 512, clear)
        dq_acc, dk_acc, dv_acc = acc.at[0], acc.at[1], acc.at[2]
        for p, (d, n, rows, stride) in enumerate(_PATTERNS):
            _set_bias(bias_scr, n, rows)

            def block(gs, has_prev):
                at = [_group_rows(d, g) for g in gs]

                def load(ref, b):
                    return _load_rows(ref, b, n, rows, stride)

                def put(ref, b, val):
                    _store_rows(ref, b, val, n, rows, stride, add=True)

                def wide(x):
                    return jnp.concatenate([x, x], axis=1) if has_prev else x

                lse = [[load(ref, b) for ref in l_refs] for b, _ in at]
                w = [_mix_weights(*ls)[p] for ls in lse]
                do2 = [_stack_heads((wg * load(da_ref, b)).astype(BF16), head_mask) for wg, (b, _) in zip(w, at)]
                dl2 = [wide(_rows_per_head(wg * load(ds_ref, b), lane_lo)) for wg, (b, _) in zip(w, at)]
                lse2 = [wide(_rows_per_head(ls[p], lane_lo)) for ls in lse]
                q2 = [_stack_heads(load(q_ref, b).astype(BF16), head_mask) for b, _ in at]
                k2 = [load(k_ref, b).astype(BF16) for b, _ in at]
                v2 = [load(v_ref, b).astype(BF16) for b, _ in at]
                if has_prev:
                    k2 = [jnp.concatenate([load(k_ref, pv).astype(BF16), k], axis=0) for (_, pv), k in zip(at, k2)]
                    v2 = [jnp.concatenate([load(v_ref, pv).astype(BF16), v], axis=0) for (_, pv), v in zip(at, v2)]
                s = [_dot_nt(q, k) for q, k in zip(q2, k2)]
                dp = [_dot_nt(do, v) for do, v in zip(do2, v2)]
                pr = [jnp.exp(x * 0.125 + (bias_scr[...] if has_prev else bias_scr[:, BLK:2 * BLK]) - l)
                      for x, l in zip(s, lse2)]
                ds = [(pg * (x - dl) * 0.125).astype(BF16) for pg, x, dl in zip(pr, dp, dl2)]
                dq2 = [_dot(x, k) for x, k in zip(ds, k2)]
                dk2 = [_dot_tn(x, q) for x, q in zip(ds, q2)]
                dv2 = [_dot_tn(pg.astype(BF16), do) for pg, do in zip(pr, do2)]
                for (b, pv), dq, dk, dv in zip(at, dq2, dk2, dv2):
                    put(dq_acc, b, _unstack_heads(dq, lane_lo))
                    if has_prev:
                        put(dk_acc, pv, dk[:BLK])
                        put(dv_acc, pv, dv[:BLK])
                        put(dk_acc, b, dk[BLK:])
                        put(dv_acc, b, dv[BLK:])
                    else:
                        put(dk_acc, b, dk)
                        put(dv_acc, b, dv)

            _loop(0, _FIRST[d], lambda gs: block(gs, False), width=WIDTH)
            _loop(_FIRST[d], groups, lambda gs: block(gs, True), width=WIDTH)

        def emit(i):
            sl = pl.ds(pl.multiple_of(i * 512, 512), 512)
            for s in range(3):
                out_ref[s, sl, :] = acc[s, sl, :].astype(BF16)

        _loop(0, t // 512, emit)

    def col(c0):
        return pl.BlockSpec((t, BLK), lambda hp: (0, c0 + hp))

    res, extra = _pcall(
        body, name="attention_bwd", grid=(4,),
        in_specs=[col(0), col(4), col(8)] + [col(0)] * 5 + [ANY],
        out_specs=[pl.BlockSpec((3, t, BLK), lambda hp: (0, 0, hp))],
        out_shape=[jax.ShapeDtypeStruct(dproj.shape, BF16)],
        scratch_shapes=[pltpu.VMEM((3, t, BLK), F32), pltpu.VMEM((2 * BLK, 2 * BLK), F32)],
        semantics=("parallel",), vmem_mb=56, rider=rider, aliases={8: 0},
    )(qkv, qkv, qkv, dattn, dsum, *lses, dproj)
    return res[0] if rider is None else (res[0], extra)


def _order_specs(t):
    n_i = SEG // TI
    nblk = (t // HALF) * n_i
    per = TI // 8

    def main(c, col=0):
        return pl.BlockSpec((1, N_RES, TI, c), lambda s: (s // n_i, 0, s % n_i, col))

    def before(c, col=0):
        return pl.BlockSpec((1, 2, 8, c), lambda s: (jnp.maximum(s - 1, 0) // n_i, N_RES // 2 - 1,
                                                     (jnp.maximum(s - 1, 0) % n_i) * per + per - 1, col))

    def after(c, col=0):
        return pl.BlockSpec((1, 2, 8, c), lambda s: (jnp.minimum(s + 1, nblk - 1) // n_i, 0,
                                                     (jnp.minimum(s + 1, nblk - 1) % n_i) * per, col))

    return nblk, main, before, after


def _shift_in(v, row_in, up):
    rows = v.shape[0]
    idx = lax.broadcasted_iota(jnp.int32, v.shape, 0)
    fill = jnp.broadcast_to(row_in, v.shape)
    if up:
        return jnp.where(idx == rows - 1, fill, pltpu.roll(v, rows - 1, axis=0))
    return jnp.where(idx == 0, fill, pltpu.roll(v, 1, axis=0))


def _taps_behind(u, before):
    s15 = _shift_in(u[N_RES - 1], before[1, 7:8, :], up=False)
    s14 = _shift_in(u[N_RES - 2], before[0, 7:8, :], up=False)
    m1 = jnp.concatenate([s15[None], u[:N_RES - 1]], axis=0)
    m2 = jnp.concatenate([s14[None], s15[None], u[:N_RES - 2]], axis=0)
    return m1, m2


def _taps_ahead(u, after):
    t0 = _shift_in(u[0], after[0, 0:1, :], up=True)
    t1 = _shift_in(u[1], after[1, 0:1, :], up=True)
    p1 = jnp.concatenate([u[1:], t0[None]], axis=0)
    p2 = jnp.concatenate([u[2:], t0[None], t1[None]], axis=0)
    return p1, p2


def _conv_fwd(gates, before, first, cw):
    bg, cg, xc = gates[..., 0:512], gates[..., 512:1024], gates[..., 1024:1536]
    u = cg * xc
    ub = before[..., 512:1024] * before[..., 1024:1536]
    ub = jnp.where(first, jnp.zeros_like(ub), ub)
    m1, m2 = _taps_behind(u, ub)
    conv = m2 * cw[0:1, :] + m1 * cw[1:2, :] + u * cw[2:3, :]
    return bg, u, m1, m2, conv


def _sum_tokens(v):
    return jnp.sum(jnp.sum(v, axis=0), axis=0, keepdims=True)


def _mixer_fwd(x, attn, gates, cw, g_a, g_c, w_out):
    t, d = x.shape
    nblk, main, before, _ = _order_specs(t)
    rows = N_RES * TI

    def body(x_ref, at_ref, gt_ref, gb_ref, cw_ref, ga_ref, gc_ref, wa_ref, wb_ref, x1_ref, mg_ref):
        an = _rms_fwd(at_ref[0], ga_ref[...])[0].astype(BF16)
        bg, _, _, _, conv = _conv_fwd(gt_ref[0], gb_ref[0], pl.program_id(0) == 0, cw_ref[...])
        cn = _rms_fwd(bg * conv, gc_ref[...])[0].astype(BF16)
        mg_ref[0, :, :, 0:512] = an
        mg_ref[0, :, :, 512:1024] = cn
        y = _dot(an.reshape(rows, 512), wa_ref[...]) + _dot(cn.reshape(rows, 512), wb_ref[...])
        x1_ref[0] = x_ref[0] + y.reshape(N_RES, TI, d)

    const = lambda r, c, i0=0: pl.BlockSpec((r, c), lambda s: (i0, 0))
    x1, merged = pl.pallas_call(
        body, name="mixer_fwd", grid=(nblk,),
        in_specs=[main(d), main(512), main(1536, 1), before(1536, 1), const(3, 512), const(1, 512), const(1, 512),
                  const(512, d), const(512, d, 1)],
        out_specs=[main(d), main(d)],
        out_shape=[jax.ShapeDtypeStruct(_x4(x).shape, F32), jax.ShapeDtypeStruct(_x4(x).shape, BF16)],
        compiler_params=_params(("parallel",), 48),
    )(_x4(x), _x4(attn), _x4(gates), _x4(gates), cw, g_a, g_c, w_out, w_out)
    return x1.reshape(t, d), merged.reshape(t, d)


def _mixer_bwd(dx1, attn, gates, cw, g_a, g_c, w_out, head_sum, rider=None):
    t, d = dx1.shape
    nblk, main, before, _ = _order_specs(t)
    rows = N_RES * TI

    def body(dx_ref, at_ref, gt_ref, gb_ref, cw_ref, ga_ref, gc_ref, wa_ref, wb_ref, hs_ref,
             da_ref, dsum_ref, dy_ref, gga_ref, ggc_ref):
        s = pl.program_id(0)
        dxb = dx_ref[0].reshape(rows, d).astype(BF16)
        dma = _dot_nt(dxb, wa_ref[...]).reshape(N_RES, TI, 512)
        dmc = _dot_nt(dxb, wb_ref[...]).reshape(N_RES, TI, 512)
        attn_v, g_av = at_ref[0], ga_ref[...]
        _, ah, ra = _rms_fwd(attn_v, g_av)
        dattn = _rms_bwd(dma, ah, ra, g_av)
        da_ref[0] = dattn
        z = (dattn * attn_v).reshape(rows, 512)
        hs = hs_ref[...]
        z1 = z.astype(BF16)
        z2 = (z - z1.astype(F32)).astype(BF16)
        z3 = (z - z1.astype(F32) - z2.astype(F32)).astype(BF16)
        dsum_ref[0] = (_dot(z1, hs) + _dot(z2, hs) + _dot(z3, hs)).reshape(N_RES, TI, 512)
        bg, _, _, _, conv = _conv_fwd(gt_ref[0], gb_ref[0], s == 0, cw_ref[...])
        g_cv = gc_ref[...]
        _, yh, rc = _rms_fwd(bg * conv, g_cv)
        dy_ref[0] = _rms_bwd(dmc, yh, rc, g_cv)
        pa, pc = _sum_tokens(dma * ah), _sum_tokens(dmc * yh)

        @pl.when(s == 0)
        def _():
            gga_ref[...] = pa
            ggc_ref[...] = pc

        @pl.when(s != 0)
        def _():
            gga_ref[...] += pa
            ggc_ref[...] += pc

    const = lambda r, c, i0=0: pl.BlockSpec((r, c), lambda s: (i0, 0))
    shape4 = _x4(attn).shape
    res, extra = _pcall(
        body, name="mixer_bwd", grid=(nblk,),
        in_specs=[main(d), main(512), main(1536, 1), before(1536, 1), const(3, 512), const(1, 512), const(1, 512),
                  const(512, d), const(512, d, 1), const(512, 512)],
        out_specs=[main(512)] * 3 + [const(1, 512), const(1, 512)],
        out_shape=[jax.ShapeDtypeStruct(shape4, F32)] * 3 + [jax.ShapeDtypeStruct((1, 512), F32)] * 2,
        semantics=("arbitrary",), vmem_mb=48, rider=rider,
    )(_x4(dx1), _x4(attn), _x4(gates), _x4(gates), cw, g_a, g_c, w_out, w_out, head_sum)
    res = [r.reshape(t, 512) for r in res[:3]] + res[3:]
    return res if rider is None else (res, extra)


def _conv_bwd(dy, gates, cw):
    t = dy.shape[0]
    nblk, main, before, after = _order_specs(t)
    n_i = SEG // TI

    def body(dy_ref, dya_ref, gt_ref, gb_ref, ga_ref, cw_ref, dp_ref, gcw_ref):
        s = pl.program_id(0)
        cw_v, gates_v = cw_ref[...], gt_ref[0]
        bg, u, m1, m2, conv = _conv_fwd(gates_v, gb_ref[0], s == 0, cw_v)
        dy_v = dy_ref[0]
        dconv = dy_v * bg
        dca = dya_ref[0] * ga_ref[0][..., 0:512]
        dca = jnp.where(s == nblk - 1, jnp.zeros_like(dca), dca)
        p1, p2 = _taps_ahead(dconv, dca)
        du = dconv * cw_v[2:3, :] + p1 * cw_v[1:2, :] + p2 * cw_v[0:1, :]
        dp_ref[0, 0] = (dy_v * conv).astype(BF16)
        dp_ref[1, 0] = (du * gates_v[..., 1024:1536]).astype(BF16)
        dp_ref[2, 0] = (du * gates_v[..., 512:1024]).astype(BF16)
        parts = [_sum_tokens(dconv * m2), _sum_tokens(dconv * m1), _sum_tokens(dconv * u)]

        @pl.when(s == 0)
        def _():
            gcw_ref[...] = jnp.zeros_like(gcw_ref)

        for tap in range(3):
            gcw_ref[tap:tap + 1, :] += parts[tap]

    dproj, gcw = pl.pallas_call(
        body, name="conv_bwd", grid=(nblk,),
        in_specs=[main(512), after(512), main(1536, 1), before(1536, 1), after(1536, 1),
                  pl.BlockSpec((3, 512), lambda s: (0, 0))],
        out_specs=[pl.BlockSpec((3, 1, N_RES, TI, 512), lambda s: (1, s // n_i, 0, s % n_i, 0)),
                   pl.BlockSpec((8, 512), lambda s: (0, 0))],
        out_shape=[jax.ShapeDtypeStruct((6, t // HALF, N_RES, SEG, 512), BF16), jax.ShapeDtypeStruct((8, 512), F32)],
        compiler_params=_params(("arbitrary",), 40),
    )(_x4(dy), _x4(dy), _x4(gates), _x4(gates), _x4(gates), cw)
    return dproj.reshape(6, t, 512), gcw


def _xattn_fwd(x1, g, w_q, kv, w_o, *, tb):
    t, d = x1.shape
    hd = d // N_MEM_HEADS
    m = kv.shape[0]

    def body(x_ref, g_ref, wq_ref, k_ref, v_ref, wo_ref, x2_ref, h_ref, q_ref, o_ref):
        xv = x_ref[...]
        h = _rms_fwd(xv, g_ref[...])[0].astype(BF16)
        h_ref[...] = h
        q = _dot(h, wq_ref[...]).astype(BF16)
        q_ref[...] = q
        for hh in range(N_MEM_HEADS):
            sl = slice(hh * hd, (hh + 1) * hd)
            s = _dot_nt(q[:, sl], k_ref[:, sl]) * (1.0 / 16.0)
            e = jnp.exp(s - jnp.max(s, axis=1, keepdims=True))
            p = e / jnp.sum(e, axis=1, keepdims=True)
            o_ref[:, sl] = _dot(p.astype(BF16), v_ref[:, sl]).astype(BF16)
        x2_ref[...] = xv + _dot(o_ref[...], wo_ref[...])

    tok = pl.BlockSpec((tb, d), lambda i: (i, 0))
    full = pl.BlockSpec((d, d), lambda i: (0, 0))
    return pl.pallas_call(
        body, name="xattn_fwd", grid=(t // tb,),
        in_specs=[tok, pl.BlockSpec((1, d), lambda i: (0, 0)), full,
                  pl.BlockSpec((m, d), lambda i: (0, 0)), pl.BlockSpec((m, d), lambda i: (0, 1)), full],
        out_specs=[tok] * 4,
        out_shape=[jax.ShapeDtypeStruct((t, d), F32)] + [jax.ShapeDtypeStruct((t, d), BF16)] * 3,
        compiler_params=_params(("parallel",), 48),
    )(x1, g, w_q, kv, kv, w_o)


def _xattn_bwd(dx2, x1, g, q, w_q, kv, w_o, *, tb, rider=None):
    t, d = x1.shape
    hd = d // N_MEM_HEADS
    m = kv.shape[0]

    def body(dx2_ref, x_ref, g_ref, q_ref, wq_ref, k_ref, v_ref, wo_ref,
             dx1_ref, dx1b_ref, dq_ref, dk_ref, dv_ref, gg_ref):
        i = pl.program_id(0)

        @pl.when(i == 0)
        def _():
            dk_ref[...] = jnp.zeros_like(dk_ref)
            dv_ref[...] = jnp.zeros_like(dv_ref)

        dx2 = dx2_ref[...]
        do = _dot_nt(dx2.astype(BF16), wo_ref[...]).astype(BF16)
        for hh in range(N_MEM_HEADS):
            sl = slice(hh * hd, (hh + 1) * hd)
            qh, kh, vh, doh = q_ref[:, sl], k_ref[:, sl], v_ref[:, sl], do[:, sl]
            s = _dot_nt(qh, kh) * (1.0 / 16.0)
            e = jnp.exp(s - jnp.max(s, axis=1, keepdims=True))
            p = e / jnp.sum(e, axis=1, keepdims=True)
            dp = _dot_nt(doh, vh)
            ds = (p * (dp - jnp.sum(dp * p, axis=1, keepdims=True)) * (1.0 / 16.0)).astype(BF16)
            dq_ref[:, sl] = _dot(ds, kh).astype(BF16)
            dk_ref[:, sl] += _dot_tn(ds, qh)
            dv_ref[:, sl] += _dot_tn(p.astype(BF16), doh)
        dh = _dot_nt(dq_ref[...], wq_ref[...])
        g_v = g_ref[...]
        _, xh, r = _rms_fwd(x_ref[...], g_v)
        dx1 = dx2 + _rms_bwd(dh, xh, r, g_v)
        dx1_ref[...] = dx1
        dx1b_ref[...] = dx1.astype(BF16)
        part = jnp.sum(dh * xh, axis=0, keepdims=True)

        @pl.when(i == 0)
        def _():
            gg_ref[...] = part

        @pl.when(i != 0)
        def _():
            gg_ref[...] += part

    tok = pl.BlockSpec((tb, d), lambda i: (i, 0))
    full = pl.BlockSpec((d, d), lambda i: (0, 0))
    acc = pl.BlockSpec((m, d), lambda i: (0, 0))
    res, extra = _pcall(
        body, name="xattn_bwd", grid=(t // tb,),
        in_specs=[tok, tok, pl.BlockSpec((1, d), lambda i: (0, 0)), tok, full,
                  pl.BlockSpec((m, d), lambda i: (0, 0)), pl.BlockSpec((m, d), lambda i: (0, 1)), full],
        out_specs=[tok, tok, tok, acc, acc, pl.BlockSpec((1, d), lambda i: (0, 0))],
        out_shape=[jax.ShapeDtypeStruct((t, d), F32), jax.ShapeDtypeStruct((t, d), BF16),
                   jax.ShapeDtypeStruct((t, d), BF16),
                   jax.ShapeDtypeStruct((m, d), F32), jax.ShapeDtypeStruct((m, d), F32),
                   jax.ShapeDtypeStruct((1, d), F32)],
        semantics=("arbitrary",), vmem_mb=48, rider=rider,
    )(dx2, x1, g, q, w_q, kv, kv, w_o)
    return res if rider is None else (res, extra)


def _mlp_down_loss(a, w_down, x2, tgt, g, *, tb):
    t, d = x2.shape
    f = a.shape[1]

    def body(a_ref, w_ref, x_ref, t_ref, g_ref, dx_ref, dxb_ref, loss_ref, gg_ref):
        i = pl.program_id(0)
        av = a_ref[...].astype(F32)
        x3 = x_ref[...] + _dot((av * av).astype(BF16), w_ref[...])
        g_v = g_ref[...]
        out, xh, r = _rms_fwd(x3, g_v)
        err = out - t_ref[...]
        dout = err * (1.0 / d)
        dx = _rms_bwd(dout, xh, r, g_v)
        dx_ref[...] = dx
        dxb_ref[...] = dx.astype(BF16)
        part = jnp.sum(dout * xh, axis=0, keepdims=True)
        lpart = 0.5 * jnp.sum(jnp.mean(err * err, axis=-1, keepdims=True), axis=0, keepdims=True)
        lpart = jnp.broadcast_to(lpart, loss_ref.shape)

        @pl.when(i == 0)
        def _():
            gg_ref[...] = part
            loss_ref[...] = lpart

        @pl.when(i != 0)
        def _():
            gg_ref[...] += part
            loss_ref[...] += lpart

    tok = pl.BlockSpec((tb, d), lambda i: (i, 0))
    return pl.pallas_call(
        body, name="mlp_down_loss", grid=(t // tb,),
        in_specs=[pl.BlockSpec((tb, f), lambda i: (i, 0)), pl.BlockSpec((f, d), lambda i: (0, 0)), tok, tok,
                  pl.BlockSpec((1, d), lambda i: (0, 0))],
        out_specs=[tok, tok, pl.BlockSpec((8, 128), lambda i: (0, 0)), pl.BlockSpec((1, d), lambda i: (0, 0))],
        out_shape=[jax.ShapeDtypeStruct((t, d), F32), jax.ShapeDtypeStruct((t, d), BF16),
                   jax.ShapeDtypeStruct((8, 128), F32), jax.ShapeDtypeStruct((1, d), F32)],
        compiler_params=_params(("arbitrary",), 56),
    )(a, w_down, x2, tgt, g)


def _mlp_dpre(dx3, w_down, a, *, tb, bn):
    t, d = dx3.shape
    f = a.shape[1]

    def body(dx_ref, w_ref, a_ref, o_ref):
        o_ref[...] = (2.0 * a_ref[...].astype(F32) * _dot_nt(dx_ref[...], w_ref[...])).astype(BF16)

    return pl.pallas_call(
        body, name="mlp_dpre", grid=(t // tb, f // bn),
        in_specs=[pl.BlockSpec((tb, d), lambda i, j: (i, 0)), pl.BlockSpec((bn, d), lambda i, j: (j, 0)),
                  pl.BlockSpec((tb, bn), lambda i, j: (i, j))],
        out_specs=pl.BlockSpec((tb, bn), lambda i, j: (i, j)),
        out_shape=jax.ShapeDtypeStruct((t, f), BF16),
        compiler_params=_params(("parallel", "arbitrary"), 48),
    )(dx3, w_down, a)


def _adamw(gsum, w, m, v):
    m_new = ADAM_B1 * m + (1.0 - ADAM_B1) * gsum
    v_new = ADAM_B2 * v + (1.0 - ADAM_B2) * (gsum * gsum)
    m_hat = m_new / (1.0 - ADAM_B1 ** ADAM_STEP)
    v_hat = v_new / (1.0 - ADAM_B2 ** ADAM_STEP)
    delta = -ADAM_LR * (m_hat / (jnp.sqrt(v_hat) + ADAM_EPS) + ADAM_WD * w)
    return delta, m_new, v_new


def _sum_adamw(parts, w, m, v, *, name, tr):
    r, c = w.shape

    def body(p_ref, w_ref, m_ref, v_ref, g_ref, d_ref, mo_ref, vo_ref):
        g = p_ref[0].astype(F32)
        for k in range(1, N_DEV):
            g = g + p_ref[k].astype(F32)
        g_ref[...] = g
        d_ref[...], mo_ref[...], vo_ref[...] = _adamw(g, w_ref[...], m_ref[...], v_ref[...])

    blk = pl.BlockSpec((tr, c), lambda i: (i, 0))
    return pl.pallas_call(
        body, name=name, grid=(r // tr,),
        in_specs=[pl.BlockSpec((N_DEV, tr, c), lambda i: (0, i, 0)), blk, blk, blk],
        out_specs=[blk] * 4, out_shape=[jax.ShapeDtypeStruct((r, c), F32)] * 4,
        compiler_params=_params(("parallel",), 40),
    )(parts, w, m, v)


def _sum_small(parts):
    _, r, c = parts.shape

    def body(p_ref, o_ref):
        s = p_ref[0]
        for k in range(1, N_DEV):
            s = s + p_ref[k]
        o_ref[...] = s

    return pl.pallas_call(body, name="sum_small", out_shape=jax.ShapeDtypeStruct((r, c), F32))(parts)


def _adamw_small(g, w, m, v):
    def body(g_ref, w_ref, m_ref, v_ref, d_ref, mo_ref, vo_ref):
        d_ref[...], mo_ref[...], vo_ref[...] = _adamw(g_ref[...], w_ref[...], m_ref[...], v_ref[...])

    return pl.pallas_call(body, name="adamw_small", out_shape=[jax.ShapeDtypeStruct(g.shape, F32)] * 3)(g, w, m, v)


def _head_sum_matrix():
    r = lax.broadcasted_iota(jnp.int32, (512, 512), 0) // HEAD_DIM
    c = lax.broadcasted_iota(jnp.int32, (512, 512), 1) // HEAD_DIM
    return (r == c).astype(BF16)


_SHARD_AXIS = dict(w_in=1, w_out=0, w_q=0, w_kv=1, w_o=0, w_up=1, w_down=0)


class _Weights:
    def __init__(self, full, shards=None):
        self.full = dict(full)
        self.shards = shards

    def rider(self, names):
        if self.shards is None:
            return None
        return _Gather([self.shards[n] for n in names], [_SHARD_AXIS[n] for n in names])

    def arrived(self, names, gathered):
        if gathered is not None:
            self.full.update(zip(names, gathered))

    def __getitem__(self, name):
        return self.full[name]


class _Grads:
    def __init__(self, distributed):
        self.distributed = distributed
        self.local = {}
        self.received = {}

    def add(self, name, g):
        self.local[name] = g

    def rider(self, names):
        if not self.distributed:
            return None
        return _Exchange([self.local[n] for n in names], [_SHARD_AXIS[n] for n in names])

    def arrived(self, names, received):
        if received is not None:
            for n, r in zip(names, received):
                self.received[n] = r


def _ride(fn, *args, rider=None, **kw):
    if rider is None:
        return fn(*args, **kw), None
    return fn(*args, rider=rider, **kw)


def _local_step(x, mem, tgt, gains, weights, cw, grads):
    w_in = weights["w_in"]
    x = _reorder(x, False, "reorder_x")
    tgt = _reorder(tgt, False, "reorder_target")

    names = ["w_q", "w_kv", "w_o"]
    (proj, h1), got = _ride(_norm_matmul, x, gains["g_mix"], w_in, name="proj", out_dtype=F32, tb=1024, bn=768,
                            save_h=True, rider=weights.rider(names))
    weights.arrived(names, got)
    names = ["w_up", "w_down"]
    (attn, *lses), got = _ride(_attention_fwd, proj, rider=weights.rider(names))
    weights.arrived(names, got)
    x1, merged = _mixer_fwd(x, attn, proj, cw, gains["g_attn_out"], gains["g_conv_out"], weights["w_out"])
    kv, mem_n = _norm_matmul(mem, gains["g_mem"], weights["w_kv"], name="mem_kv", out_dtype=BF16, tb=mem.shape[0],
                             bn=1024, save_h=True)
    x2, h2, qm, om = _xattn_fwd(x1, gains["g_xattn"], weights["w_q"], kv, weights["w_o"], tb=256)
    w_up, w_down = weights["w_up"], weights["w_down"]
    a, h3 = _norm_matmul(x2, gains["g_mlp"], w_up, name="mlp_up", out_dtype=BF16, tb=1024, bn=1024, relu=True,
                         save_h=True)
    dx3, dx3b, loss_blk, gg_final = _mlp_down_loss(a, w_down, x2, tgt, gains["g_final"], tb=256)

    dpre = _mlp_dpre(dx3b, w_down, a, tb=1024, bn=1024)
    grads.add("w_down", _matmul_tn(a, dx3b, name="grad_w_down", bm=512, bn=1024, square_a=True))
    gw_up, got = _ride(_matmul_tn, h3, dpre, name="grad_w_up", bm=1024, bn=512, rider=grads.rider(["w_down"]))
    grads.arrived(["w_down"], got)
    grads.add("w_up", gw_up)
    (dx2, dx2b, gg_mlp), got = _ride(_matmul_nt_normbwd, dpre, w_up, x2, gains["g_mlp"], dx3, name="mlp_dx", tb=512,
                                     also_bf16=True, rider=grads.rider(["w_up"]))
    grads.arrived(["w_up"], got)

    grads.add("w_o", _matmul_tn(om, dx2b, name="grad_w_o", bm=1024, bn=512))
    (dx1, dx1b, dqm, dk, dv, gg_xattn), got = _ride(_xattn_bwd, dx2, x1, gains["g_xattn"], qm, weights["w_q"], kv,
                                                    weights["w_o"], tb=256, rider=grads.rider(["w_o"]))
    grads.arrived(["w_o"], got)
    grads.add("w_q", _matmul_tn(h2, dqm, name="grad_w_q", bm=1024, bn=512))
    dkv = jnp.concatenate([dk, dv], axis=1).astype(BF16)
    grads.add("w_kv", _matmul_tn(mem_n, dkv, name="grad_w_kv", bm=1024, bn=1024))
    _, gg_mem = _matmul_nt_normbwd(dkv, weights["w_kv"], mem, gains["g_mem"], None, name="mem_dx", tb=mem.shape[0])

    grads.add("w_out", _matmul_tn(merged, dx1b, name="grad_w_out", bm=1024, bn=512))
    names = ["w_q", "w_out"]
    (dattn, dsum, dy, gg_attn, gg_conv), got = _ride(
        _mixer_bwd, dx1, attn, proj, cw, gains["g_attn_out"], gains["g_conv_out"], weights["w_out"],
        _head_sum_matrix(), rider=grads.rider(names))
    grads.arrived(names, got)
    dproj, gcw = _conv_bwd(dy, proj, cw)
    dproj, got = _ride(_attention_bwd, proj, dattn, dsum, lses, dproj, rider=grads.rider(["w_kv"]))
    grads.arrived(["w_kv"], got)
    grads.add("w_in", _matmul_tn(h1, dproj, name="grad_w_in", bm=1024, bn=512))
    (grad_x, gg_mix), got = _ride(_matmul_nt_normbwd, dproj, w_in, x, gains["g_mix"], dx1, name="mixer_dx",
                                  tb=512, rider=grads.rider(["w_in"]))
    grads.arrived(["w_in"], got)
    grad_x = _reorder(grad_x, True, "reorder_grad_x")

    small = dict(g_mix=gg_mix, g_attn_out=gg_attn, g_conv_out=gg_conv, g_xattn=gg_xattn, g_mem=gg_mem,
                 g_mlp=gg_mlp, g_final=gg_final, conv_w=gcw[0:3], loss=loss_blk[0:1, 0:1])
    return grad_x, small


_BIG = ("w_in", "w_out", "w_q", "w_kv", "w_o", "w_up", "w_down")
_GAIN_ROWS = ("g_mix", "g_xattn", "g_mem", "g_mlp", "g_final")


def _pack_small(vals, conv):
    rows = [vals[k].reshape(1, -1) for k in _GAIN_ROWS]
    rows.append(jnp.concatenate([vals["g_attn_out"].reshape(1, -1), vals["g_conv_out"].reshape(1, -1)], axis=1))
    flat = conv.reshape(1, -1)
    rows.append(jnp.pad(flat, ((0, 0), (0, 1024 - flat.shape[1]))))
    rows.append(jnp.zeros((1, 1024), F32))
    return jnp.concatenate(rows, axis=0)


def kernel(x, mem, g_mix, w_in, conv_w, g_attn_out, g_conv_out, w_out, g_xattn, g_mem, w_q_mem, w_kv_mem, w_o_mem, g_mlp, w_up, w_down, g_final, loss_target, m_g_mix, m_w_in, m_conv_w, m_g_attn_out, m_g_conv_out, m_w_out, m_g_xattn, m_g_mem, m_w_q_mem, m_w_kv_mem, m_w_o_mem, m_g_mlp, m_w_up, m_w_down, m_g_final, v_g_mix, v_w_in, v_conv_w, v_g_attn_out, v_g_conv_out, v_w_out, v_g_xattn, v_g_mem, v_w_q_mem, v_w_kv_mem, v_w_o_mem, v_g_mlp, v_w_up, v_w_down, v_g_final):
    d = x.shape[-1]
    me = 4 * lax.axis_index("x") + 2 * lax.axis_index("y") + lax.axis_index("c")
    w_shards = dict(w_in=w_in, w_out=w_out, w_q=w_q_mem, w_kv=w_kv_mem, w_o=w_o_mem, w_up=w_up, w_down=w_down)
    m_shards = dict(w_in=m_w_in, w_out=m_w_out, w_q=m_w_q_mem, w_kv=m_w_kv_mem, w_o=m_w_o_mem, w_up=m_w_up,
                    w_down=m_w_down)
    v_shards = dict(w_in=v_w_in, w_out=v_w_out, w_q=v_w_q_mem, w_kv=v_w_kv_mem, w_o=v_w_o_mem, w_up=v_w_up,
                    w_down=v_w_down)
    gains = dict(g_mix=g_mix, g_attn_out=g_attn_out, g_conv_out=g_conv_out, g_xattn=g_xattn, g_mem=g_mem,
                 g_mlp=g_mlp, g_final=g_final)
    gains2 = {k: v.reshape(1, -1) for k, v in gains.items()}

    shards = {k: w_shards[k].astype(BF16) for k in _BIG}
    first = _comm_call(_Gather([shards["w_in"], shards["w_out"], conv_w], [1, 0, None]), name="gather_first")
    weights = _Weights({}, shards)
    weights.arrived(["w_in", "w_out"], first[:2])
    cw = first[2].transpose(1, 0, 2).reshape(3, -1)

    grads = _Grads(distributed=True)
    grad_x, small = _local_step(x[0], mem[0], loss_target[0], gains2, weights, cw, grads)

    small_rows = [small[k] for k in _GAIN_ROWS]
    small_rows.append(jnp.concatenate([small["g_attn_out"], small["g_conv_out"]], axis=1))
    small_rows.append(jnp.pad(small["conv_w"], ((0, 0), (0, 512))))
    small_rows.append(jnp.pad(small["loss"], ((0, 6), (0, 1023))))
    small_part = jnp.concatenate(small_rows, axis=0)
    small_received = _comm_call(_Exchange([small_part], [None]), name="exchange_small")[0]

    outs = {}
    tiles = dict(w_in=256, w_out=128, w_q=128, w_kv=256, w_o=128, w_up=256, w_down=256)
    for k in _BIG:
        outs[k] = _sum_adamw(grads.received[k], w_shards[k], m_shards[k], v_shards[k], name=f"adamw_{k}",
                             tr=tiles[k])

    ssum = _sum_small(small_received)
    loss = ssum[9, 0]
    g_small = {k: ssum[i] for i, k in enumerate(_GAIN_ROWS)}
    g_small["g_attn_out"] = ssum[5, 0:512]
    g_small["g_conv_out"] = ssum[5, 512:1024]
    g_conv = lax.dynamic_slice_in_dim(ssum[6:9, 0:512], me * 64, 64, axis=1)
    m_small = dict(g_mix=m_g_mix, g_attn_out=m_g_attn_out, g_conv_out=m_g_conv_out, g_xattn=m_g_xattn,
                   g_mem=m_g_mem, g_mlp=m_g_mlp, g_final=m_g_final)
    v_small = dict(g_mix=v_g_mix, g_attn_out=v_g_attn_out, g_conv_out=v_g_conv_out, g_xattn=v_g_xattn,
                   g_mem=v_g_mem, g_mlp=v_g_mlp, g_final=v_g_final)
    packed = [_pack_small(g_small, g_conv), _pack_small(gains, conv_w), _pack_small(m_small, m_conv_w),
              _pack_small(v_small, v_conv_w)]
    upd = _adamw_small(*packed)

    def unpack(p):
        res = {k: p[i] for i, k in enumerate(_GAIN_ROWS)}
        res["g_attn_out"] = p[5, 0:512]
        res["g_conv_out"] = p[5, 512:1024]
        res["conv_w"] = p[6, 0:192].reshape(3, 64)
        return res

    g_small["conv_w"] = g_conv
    small_out = [g_small] + [unpack(p) for p in upd]
    names = {"g_mix": "g_mix", "w_in": "w_in", "conv_w": "conv_w", "g_attn_out": "g_attn_out",
             "g_conv_out": "g_conv_out", "w_out": "w_out", "g_xattn": "g_xattn", "g_mem": "g_mem",
             "w_q_mem": "w_q", "w_kv_mem": "w_kv", "w_o_mem": "w_o", "g_mlp": "g_mlp", "w_up": "w_up",
             "w_down": "w_down", "g_final": "g_final"}
    result = [loss, grad_x[None]]
    for which in range(4):
        for key in names.values():
            result.append(outs[key][which] if key in outs else small_out[which][key])
    return tuple(result)
```

```python
import math

import jax
import jax.numpy as jnp
from jax import lax
from jax.experimental import pallas as pl
from jax.experimental.pallas import tpu as pltpu

F32 = jnp.float32
BF16 = jnp.bfloat16
NORM_EPS = 1e-6
NEG_INF = -1e30
N_DEV = 8
BLK = 128
HEAD_DIM = 64
N_MEM_HEADS = 4
ADAM_LR = 0.001
ADAM_B1 = 0.9
ADAM_B2 = 0.999
ADAM_EPS = 1e-08
ADAM_WD = 0.01
ADAM_STEP = 10
MESH = pl.DeviceIdType.MESH
ANY = pl.BlockSpec(memory_space=pl.ANY)


def _dot(a, b):
    return jnp.dot(a, b, preferred_element_type=F32)


def _dot_nt(a, b):
    return lax.dot_general(a, b, (((1,), (1,)), ((), ())), preferred_element_type=F32)


def _dot_tn(a, b):
    return lax.dot_general(a, b, (((0,), (0,)), ((), ())), preferred_element_type=F32)


def _params(semantics, vmem_mb):
    return pltpu.CompilerParams(dimension_semantics=semantics, vmem_limit_bytes=vmem_mb << 20)


def _rms_fwd(x, g):
    r = lax.rsqrt(jnp.mean(x * x, axis=-1, keepdims=True) + NORM_EPS)
    xh = x * r
    return xh * g, xh, r


def _rms_bwd(dy, xh, r, g):
    gy = dy * g
    return r * (gy - xh * jnp.mean(xh * gy, axis=-1, keepdims=True))


def _position():
    x, y, c = lax.axis_index("x"), lax.axis_index("y"), lax.axis_index("c")
    return x, y, c


def _block_of(ref, j, axis, shard_shape):
    r, c = shard_shape
    if axis is None:
        return ref.at[j]
    if axis == 0:
        return ref.at[pl.ds(j * r, r), :]
    return ref.at[:, pl.ds(j * c, c)]


class _Gather:
    has_mid = True
    alias_pairs = ()

    def __init__(self, shards, axes):
        self.arrays = list(shards)
        self.axes = list(axes)
        self.n = len(self.arrays)

    def out_shape(self):
        res = []
        for s, axis in zip(self.arrays, self.axes):
            r, c = s.shape
            shape = (N_DEV, r, c) if axis is None else (N_DEV * r, c) if axis == 0 else (r, N_DEV * c)
            res.append(jax.ShapeDtypeStruct(shape, s.dtype))
        return res

    def scratch(self):
        return [pltpu.SemaphoreType.DMA((self.n, 7)), pltpu.SemaphoreType.DMA((self.n, 7)),
                pltpu.SemaphoreType.DMA((self.n,))]

    def _ctx(self, ins, outs, sems):
        send_sems, recv_sems, local_sems = sems
        x, y, c = _position()
        me, sibling = (x, y, c), (x, y, 1 - c)
        chips = [(1 - x, y), (x, 1 - y), (1 - x, 1 - y)]

        def lin(px, py, pc):
            return 4 * px + 2 * py + pc

        def place(a, block):
            return _block_of(outs[a], lin(*block), self.axes[a], self.arrays[a].shape)

        def copy(a, k, block, to, src=None):
            dst = place(a, block)
            return pltpu.make_async_remote_copy(
                src_ref=dst if src is None else src, dst_ref=dst,
                send_sem=send_sems.at[a, k], recv_sem=recv_sems.at[a, k],
                device_id=to, device_id_type=MESH)

        def mine():
            return [pltpu.make_async_copy(ins[a], place(a, me), local_sems.at[a]) for a in range(self.n)]

        def first():
            res = []
            for a in range(self.n):
                res.append(copy(a, 0, me, sibling, src=ins[a]))
                res += [copy(a, 1 + j, me, (*chip, c), src=ins[a]) for j, chip in enumerate(chips)]
            return res

        return c, me, sibling, chips, copy, mine, first

    def start(self, ins, outs, sems):
        _, _, _, _, _, mine, first = self._ctx(ins, outs, sems)
        for cp in mine() + first():
            cp.start()

    def mid(self, ins, outs, sems):
        c, me, sibling, chips, copy, _, _ = self._ctx(ins, outs, sems)
        for j, chip in enumerate(chips):
            for a in range(self.n):
                copy(a, 1 + j, (*chip, c), me).wait_recv()
                copy(a, 4 + j, (*chip, c), sibling).start()

    def finish(self, ins, outs, sems):
        c, me, sibling, chips, copy, mine, first = self._ctx(ins, outs, sems)
        for a in range(self.n):
            copy(a, 0, sibling, me).wait_recv()
            for j, chip in enumerate(chips):
                copy(a, 4 + j, (*chip, 1 - c), me).wait_recv()
        for cp in first():
            cp.wait_send()
        for j, chip in enumerate(chips):
            for a in range(self.n):
                copy(a, 4 + j, (*chip, c), sibling).wait_send()
        for cp in mine():
            cp.wait()


class _Exchange:
    has_mid = False

    def __init__(self, parts, axes, pieces=None, into=None):
        self.n = len(parts)
        self.axes = list(axes)
        self.pieces = list(pieces or [(0, 1)] * self.n)
        into = list(into or [None] * self.n)
        kept = [a for a in range(self.n) if into[a] is not None]
        self.arrays = list(parts) + [into[a] for a in kept]
        self.alias_pairs = [(self.n + i, a) for i, a in enumerate(kept)]

    def _piece(self, a):
        r, c = self.arrays[a].shape
        axis = self.axes[a]
        return (r, c) if axis is None else (r // N_DEV, c) if axis == 0 else (r, c // N_DEV)

    def _rows(self, a):
        k, of = self.pieces[a]
        rows = self._piece(a)[0] // of
        return pl.ds(k * rows, rows)

    def out_shape(self):
        return [jax.ShapeDtypeStruct((N_DEV,) + self._piece(a), self.arrays[a].dtype) for a in range(self.n)]

    def scratch(self):
        return [pltpu.SemaphoreType.DMA((self.n, 7)), pltpu.SemaphoreType.DMA((self.n, 7)),
                pltpu.SemaphoreType.DMA((self.n,))]

    def _ctx(self, ins, outs, sems):
        send_sems, recv_sems, local_sems = sems
        x, y, c = _position()
        me = 4 * x + 2 * y + c

        def src(a, j):
            block = ins[a] if self.axes[a] is None else _block_of(ins[a], j, self.axes[a], self._piece(a))
            return block.at[self._rows(a), :]

        def dst(a, j):
            return outs[a].at[j, self._rows(a), :]

        def local():
            return [pltpu.make_async_copy(src(a, me), dst(a, me), local_sems.at[a]) for a in range(self.n)]

        def remote(inbound):
            res = []
            for a in range(self.n):
                for k in range(1, N_DEV):
                    peer = (1 - x if k & 4 else x, 1 - y if k & 2 else y, 1 - c if k & 1 else c)
                    plin = 4 * peer[0] + 2 * peer[1] + peer[2]
                    res.append(pltpu.make_async_remote_copy(
                        src_ref=src(a, plin), dst_ref=dst(a, plin if inbound else me),
                        send_sem=send_sems.at[a, k - 1], recv_sem=recv_sems.at[a, k - 1],
                        device_id=peer, device_id_type=MESH))
            return res

        return local, remote

    def start(self, ins, outs, sems):
        local, remote = self._ctx(ins, outs, sems)
        for cp in local() + remote(False):
            cp.start()

    def finish(self, ins, outs, sems):
        local, remote = self._ctx(ins, outs, sems)
        for cp in remote(True):
            cp.wait_recv()
        for cp in remote(False):
            cp.wait_send()
        for cp in local():
            cp.wait()


def _comm_call(rider, name):
    n_in, n_out = len(rider.arrays), len(rider.out_shape())

    def body(*refs):
        ins, outs, sems = refs[:n_in], refs[n_in:n_in + n_out], refs[n_in + n_out:]
        rider.start(ins, outs, sems)
        if rider.has_mid:
            rider.mid(ins, outs, sems)
        rider.finish(ins, outs, sems)

    return pl.pallas_call(
        body, name=name, out_shape=rider.out_shape(),
        in_specs=[ANY] * n_in, out_specs=[ANY] * n_out, scratch_shapes=rider.scratch(),
        input_output_aliases=dict(rider.alias_pairs),
    )(*rider.arrays)


def _pcall(body, *, name, grid, in_specs, out_specs, out_shape, scratch_shapes=(), semantics, vmem_mb, rider=None,
           aliases=None):
    in_specs, out_specs, out_shape = list(in_specs), list(out_specs), list(out_shape)
    scratch_shapes = list(scratch_shapes)
    aliases = dict(aliases or {})
    if rider is None:
        call = pl.pallas_call(body, name=name, grid=grid, in_specs=in_specs, out_specs=out_specs,
                              out_shape=out_shape, scratch_shapes=scratch_shapes, input_output_aliases=aliases,
                              compiler_params=_params(semantics, vmem_mb))
        return lambda *args: (list(call(*args)), None)
    n_in, n_out, n_scr = len(in_specs), len(out_specs), len(scratch_shapes)
    r_in, r_shapes = len(rider.arrays), rider.out_shape()
    r_out = len(r_shapes)
    aliases.update({n_in + i: n_out + o for i, o in rider.alias_pairs})
    total = math.prod(grid)
    mid_step = (3 * total) // 4

    def wrapped(*refs):
        bounds = [0, n_in, r_in, n_out, r_out, n_scr]
        for i in range(1, len(bounds)):
            bounds[i] += bounds[i - 1]
        a, ra, o, ro, s = (refs[bounds[i]:bounds[i + 1]] for i in range(5))
        rs = refs[bounds[5]:]
        step = pl.program_id(0)
        for k in range(1, len(grid)):
            step = step * grid[k] + pl.program_id(k)
        pl.when(step == 0)(lambda: rider.start(ra, ro, rs))
        body(*a, *o, *s)
        if rider.has_mid:
            pl.when(step == mid_step)(lambda: rider.mid(ra, ro, rs))
        pl.when(step == total - 1)(lambda: rider.finish(ra, ro, rs))

    call = pl.pallas_call(
        wrapped, name=name, grid=grid, in_specs=in_specs + [ANY] * r_in, out_specs=out_specs + [ANY] * r_out,
        out_shape=out_shape + r_shapes, scratch_shapes=scratch_shapes + rider.scratch(),
        input_output_aliases=aliases, compiler_params=_params(("arbitrary",) * len(grid), vmem_mb))

    def run(*args):
        res = call(*args, *rider.arrays)
        return list(res[:n_out]), list(res[n_out:])

    return run


def _norm_matmul(x, g, w, *, name, out_dtype, tb, bn, relu=False, save_h=False, rider=None):
    t, d = x.shape
    n = w.shape[1]

    def body(x_ref, g_ref, w_ref, o_ref, *rest):
        h_scr = rest[-1]

        @pl.when(pl.program_id(1) == 0)
        def _():
            h = _rms_fwd(x_ref[...], g_ref[...])[0].astype(BF16)
            h_scr[...] = h
            if save_h:
                rest[0][...] = h

        acc = _dot(h_scr[...], w_ref[...])
        if relu:
            acc = jnp.maximum(acc, 0.0)
        o_ref[...] = acc.astype(out_dtype)

    out_shape = [jax.ShapeDtypeStruct((t, n), out_dtype)]
    out_specs = [pl.BlockSpec((tb, bn), lambda i, j: (i, j))]
    if save_h:
        out_shape.append(jax.ShapeDtypeStruct((t, d), BF16))
        out_specs.append(pl.BlockSpec((tb, d), lambda i, j: (i, 0)))
    res, extra = _pcall(
        body, name=name, grid=(t // tb, n // bn),
        in_specs=[pl.BlockSpec((tb, d), lambda i, j: (i, 0)),
                  pl.BlockSpec((1, d), lambda i, j: (0, 0)),
                  pl.BlockSpec((d, bn), lambda i, j: (0, j))],
        out_specs=out_specs, out_shape=out_shape,
        scratch_shapes=[pltpu.VMEM((tb, d), BF16)],
        semantics=("parallel", "arbitrary"), vmem_mb=48, rider=rider,
    )(x, g, w)
    res = res if save_h else res[0]
    return res if rider is None else (res, extra)


def _matmul_nt_normbwd(dy, w, x, g, dres, *, name, tb, also_bf16=False, rider=None):
    t, d = x.shape
    stacked = dy.ndim == 3
    has_res = dres is not None

    def body(dy_ref, w_ref, x_ref, g_ref, *rest):
        rest = list(rest)
        dres_ref = rest.pop(0) if has_res else None
        dx_ref = rest.pop(0)
        dxb_ref = rest.pop(0) if also_bf16 else None
        gg_ref = rest.pop(0)
        i = pl.program_id(0)
        if stacked:
            kb = dy_ref.shape[2]
            dh = _dot_nt(dy_ref[0], w_ref[:, 0:kb])
            for s in range(1, dy_ref.shape[0]):
                dh = dh + _dot_nt(dy_ref[s], w_ref[:, s * kb:(s + 1) * kb])
        else:
            dh = _dot_nt(dy_ref[...], w_ref[...])
        g_v = g_ref[...]
        _, xh, r = _rms_fwd(x_ref[...], g_v)
        dx = _rms_bwd(dh, xh, r, g_v)
        if has_res:
            dx = dx + dres_ref[...]
        dx_ref[...] = dx
        if also_bf16:
            dxb_ref[...] = dx.astype(BF16)
        part = jnp.sum(dh * xh, axis=0, keepdims=True)

        @pl.when(i == 0)
        def _():
            gg_ref[...] = part

        @pl.when(i != 0)
        def _():
            gg_ref[...] += part

    tok = pl.BlockSpec((tb, d), lambda i: (i, 0))
    row = pl.BlockSpec((1, d), lambda i: (0, 0))
    if stacked:
        dy_spec = pl.BlockSpec((dy.shape[0], tb, dy.shape[2]), lambda i: (0, i, 0))
    else:
        dy_spec = pl.BlockSpec((tb, dy.shape[1]), lambda i: (i, 0))
    in_specs = [dy_spec, pl.BlockSpec(w.shape, lambda i: (0, 0)), tok, row]
    args = [dy, w, x, g]
    if has_res:
        in_specs.append(tok)
        args.append(dres)
    out_specs = [tok] + ([tok] if also_bf16 else []) + [row]
    out_shape = ([jax.ShapeDtypeStruct((t, d), F32)] + ([jax.ShapeDtypeStruct((t, d), BF16)] if also_bf16 else [])
                 + [jax.ShapeDtypeStruct((1, d), F32)])
    res, extra = _pcall(
        body, name=name, grid=(t // tb,), in_specs=in_specs, out_specs=out_specs, out_shape=out_shape,
        semantics=("arbitrary",), vmem_mb=56, rider=rider,
    )(*args)
    return res if rider is None else (res, extra)


def _matmul_tn(a, b, *, name, bm, bn, square_a=False, rider=None):
    t, m = a.shape
    stacked = b.ndim == 3
    n = b.shape[0] * bn if stacked else b.shape[1]

    def body(a_ref, b_ref, o_ref):
        av = a_ref[...]
        if square_a:
            av = av.astype(F32)
            av = (av * av).astype(BF16)
        o_ref[...] = _dot_tn(av, b_ref[...]).astype(BF16)

    res, extra = _pcall(
        body, name=name, grid=(m // bm, n // bn),
        in_specs=[pl.BlockSpec((t, bm), lambda i, j: (0, i)),
                  pl.BlockSpec((None, t, bn), lambda i, j: (j, 0, 0)) if stacked
                  else pl.BlockSpec((t, bn), lambda i, j: (0, j))],
        out_specs=[pl.BlockSpec((bm, bn), lambda i, j: (i, j))], out_shape=[jax.ShapeDtypeStruct((m, n), BF16)],
        semantics=("parallel", "parallel"), vmem_mb=56, rider=rider,
    )(a, b)
    return res[0] if rider is None else (res[0], extra)


N_RES = 16
SEG = 128
HALF = N_RES * SEG
TI = 16


def _x4(a):
    return a.reshape(a.shape[0] // HALF, N_RES, SEG, a.shape[1])


def _reorder(arrays, inverse, name, rider=None):
    t, c = arrays[0].shape
    n = len(arrays)
    n_i = SEG // TI
    natural = pl.BlockSpec((TI * N_RES, c), lambda s: (s, 0))
    major = pl.BlockSpec((1, N_RES, TI, c), lambda s: (s // n_i, 0, s % n_i, 0))

    def body(*refs):
        scr = refs[-1]
        for i_ref, o_ref in zip(refs[:n], refs[n:2 * n]):
            for cb in range(c // BLK):
                cols = slice(cb * BLK, (cb + 1) * BLK)
                slab = scr.at[cb]
                if inverse:
                    for r in range(N_RES):
                        slab[pl.ds(r, TI, stride=N_RES), :] = i_ref[0, r, :, cols]
                    o_ref[:, cols] = slab[...]
                else:
                    slab[...] = i_ref[:, cols]
                    for r in range(N_RES):
                        o_ref[0, r, :, cols] = slab[pl.ds(r, TI, stride=N_RES), :]

    shape4 = (t // HALF, N_RES, SEG, c)
    res, extra = _pcall(
        body, name=name, grid=(t // (TI * N_RES),),
        in_specs=[major if inverse else natural] * n, out_specs=[natural if inverse else major] * n,
        out_shape=[jax.ShapeDtypeStruct((t, c) if inverse else shape4, F32)] * n,
        scratch_shapes=[pltpu.VMEM((c // BLK, TI * N_RES, BLK), F32)],
        semantics=("parallel",), vmem_mb=32, rider=rider,
    )(*[_x4(a) if inverse else a for a in arrays])
    res = [r.reshape(t, c) for r in res]
    return res if rider is None else (res, extra)


_PATTERNS = ((1, 16, 8, SEG), (4, 4, 32, 4 * SEG), (16, 1, SEG, 0))
_FIRST = {1: 1, 4: 4, 16: 16}


def _group_rows(d, g):
    a = g >> 4
    if d == 16:
        base = a * HALF + (g & 15) * SEG
        prev = base - HALF
    elif d == 4:
        c = (g >> 2) & 3
        base = a * HALF + (g & 3) * SEG + c * 32
        prev = jnp.where(c > 0, base - 32, base - HALF + 96)
    else:
        c = g & 15
        base = a * HALF + c * 8
        prev = jnp.where(c > 0, base - 8, base - HALF + 120)
    return base, prev


def _load_rows(ref, base, n, rows, stride):
    parts = [ref[pl.ds(pl.multiple_of(base + j * stride, 8), rows), :] for j in range(n)]
    return parts[0] if n == 1 else jnp.concatenate(parts, axis=0)


def _store_rows(ref, base, val, n, rows, stride, add=False):
    for j in range(n):
        sl = pl.ds(pl.multiple_of(base + j * stride, 8), rows)
        piece = val[j * rows:(j + 1) * rows, :]
        if add:
            ref[sl, :] += piece
        else:
            ref[sl, :] = piece


def _band_bias(n, rows):
    shift = rows.bit_length() - 1
    lq = lax.broadcasted_iota(jnp.int32, (BLK, BLK), 0)
    lk = lax.broadcasted_iota(jnp.int32, (BLK, BLK), 1)
    iq = (lq & (rows - 1)) * n + (lq >> shift)
    ik = (lk & (rows - 1)) * n + (lk >> shift)
    zero = jnp.zeros((BLK, BLK), F32)
    return jnp.where(ik >= iq, zero, NEG_INF), jnp.where(ik <= iq, zero, NEG_INF)


def _set_bias(bias_scr, n, rows):
    prev_b, cur_b = _band_bias(n, rows)
    for half in range(2):
        bias_scr[half * BLK:(half + 1) * BLK, 0:BLK] = prev_b
        bias_scr[half * BLK:(half + 1) * BLK, BLK:2 * BLK] = cur_b


def _head_consts():
    lane_lo = lax.broadcasted_iota(jnp.int32, (BLK, BLK), 1) < HEAD_DIM
    return lane_lo, [jnp.where(lane_lo, 1.0, 0.0).astype(BF16), jnp.where(lane_lo, 0.0, 1.0).astype(BF16)]


def _stack_heads(v, head_mask):
    return jnp.concatenate([v * head_mask[0], v * head_mask[1]], axis=0)


def _unstack_heads(v2, lane_lo):
    return jnp.where(lane_lo, v2[:BLK], v2[BLK:])


def _rows_per_head(v, lane_lo):
    rolled = pltpu.roll(v, HEAD_DIM, axis=1)
    return jnp.concatenate([jnp.where(lane_lo, v, rolled), jnp.where(lane_lo, rolled, v)], axis=0)


WIDTH = 4


def _loop(lo, hi, fn, width=None):
    if width is None:
        def body(g, carry):
            fn(g)
            return carry

        if hi > lo:
            lax.fori_loop(lo, hi, body, 0)
        return
    while hi > lo:
        trips = (hi - lo) // width
        if trips:
            def body(i, carry, lo=lo, width=width):
                fn([lo + width * i + j for j in range(width)])
                return carry

            lax.fori_loop(0, trips, body, 0)
            lo += trips * width
        width = max(1, width // 2)


def _mix_weights(l1, l2, l3):
    mx = jnp.maximum(jnp.maximum(l1, l2), l3)
    e1, e2, e3 = jnp.exp(l1 - mx), jnp.exp(l2 - mx), jnp.exp(l3 - mx)
    inv = 1.0 / (e1 + e2 + e3)
    return e1 * inv, e2 * inv, e3 * inv


def _attention_fwd(qkv, rider=None):
    t = qkv.shape[0]
    groups = 16 * (t // HALF)

    def body(q_ref, k_ref, v_ref, attn_ref, l1_ref, l2_ref, l3_ref, o_scr, bias_scr):
        lane_lo, head_mask = _head_consts()
        l_refs = (l1_ref, l2_ref, l3_ref)
        for p, (d, n, rows, stride) in enumerate(_PATTERNS):
            _set_bias(bias_scr, n, rows)
            o_p, l_p = o_scr.at[p], l_refs[p]

            def block(gs, has_prev):
                at = [_group_rows(d, g) for g in gs]

                def load(ref, b):
                    return _load_rows(ref, b, n, rows, stride).astype(BF16)

                q2 = [_stack_heads(load(q_ref, b), head_mask) for b, _ in at]
                k2 = [load(k_ref, b) for b, _ in at]
                v2 = [load(v_ref, b) for b, _ in at]
                if has_prev:
                    k2 = [jnp.concatenate([load(k_ref, pv), k], axis=0) for (_, pv), k in zip(at, k2)]
                    v2 = [jnp.concatenate([load(v_ref, pv), v], axis=0) for (_, pv), v in zip(at, v2)]
                s = [_dot_nt(q, k) for q, k in zip(q2, k2)]
                s = [x * 0.125 + (bias_scr[...] if has_prev else bias_scr[:, BLK:2 * BLK]) for x in s]
                mx = [jnp.max(x, axis=1, keepdims=True) for x in s]
                e = [jnp.exp(x - m) for x, m in zip(s, mx)]
                den = [jnp.sum(x, axis=1, keepdims=True) for x in e]
                pb = [(x * (1.0 / dn)).astype(BF16) for x, dn in zip(e, den)]
                o2 = [_dot(x, v) for x, v in zip(pb, v2)]
                lse2 = [jnp.broadcast_to(m + jnp.log(dn), (2 * BLK, BLK)) for m, dn in zip(mx, den)]
                for (b, _), o, l in zip(at, o2, lse2):
                    _store_rows(o_p, b, _unstack_heads(o, lane_lo), n, rows, stride)
                    _store_rows(l_p, b, _unstack_heads(l, lane_lo), n, rows, stride)

            _loop(0, _FIRST[d], lambda gs: block(gs, False), width=WIDTH)
            _loop(_FIRST[d], groups, lambda gs: block(gs, True), width=WIDTH)

        def mix(i):
            sl = pl.ds(pl.multiple_of(i * 256, 256), 256)
            w = _mix_weights(l1_ref[sl, :], l2_ref[sl, :], l3_ref[sl, :])
            attn_ref[sl, :] = w[0] * o_scr[0, sl, :] + w[1] * o_scr[1, sl, :] + w[2] * o_scr[2, sl, :]

        _loop(0, t // 256, mix)

    def col(c0):
        return pl.BlockSpec((t, BLK), lambda hp: (0, c0 + hp))

    res, extra = _pcall(
        body, name="attention_fwd", grid=(4,), in_specs=[col(0), col(4), col(8)], out_specs=[col(0)] * 4,
        out_shape=[jax.ShapeDtypeStruct((t, 512), F32)] * 4,
        scratch_shapes=[pltpu.VMEM((3, t, BLK), F32), pltpu.VMEM((2 * BLK, 2 * BLK), F32)],
        semantics=("parallel",), vmem_mb=48, rider=rider,
    )(qkv, qkv, qkv)
    return res if rider is None else (res, extra)


def _attention_bwd(qkv, dattn, dsum, lses, dproj, rider=None):
    t = qkv.shape[0]
    groups = 16 * (t // HALF)

    def body(q_ref, k_ref, v_ref, da_ref, ds_ref, l1_ref, l2_ref, l3_ref, kept_ref, out_ref, acc, bias_scr):
        del kept_ref
        lane_lo, head_mask = _head_consts()
        l_refs = (l1_ref, l2_ref, l3_ref)

        def clear(i):
            sl = pl.ds(pl.multiple_of(i * 512, 512), 512)
            for s in range(3):
                acc[s, sl, :] = jnp.zeros((512, BLK), F32)

        _loop(0, t // 512, clear)
        dq_acc, dk_acc, dv_acc = acc.at[0], acc.at[1], acc.at[2]
        for p, (d, n, rows, stride) in enumerate(_PATTERNS):
            _set_bias(bias_scr, n, rows)

            def block(gs, has_prev):
                at = [_group_rows(d, g) for g in gs]

                def load(ref, b):
                    return _load_rows(ref, b, n, rows, stride)

                def put(ref, b, val):
                    _store_rows(ref, b, val, n, rows, stride, add=True)

                def wide(x):
                    return jnp.concatenate([x, x], axis=1) if has_prev else x

                lse = [[load(ref, b) for ref in l_refs] for b, _ in at]
                w = [_mix_weights(*ls)[p] for ls in lse]
                do2 = [_stack_heads((wg * load(da_ref, b)).astype(BF16), head_mask) for wg, (b, _) in zip(w, at)]
                dl2 = [wide(_rows_per_head(wg * load(ds_ref, b), lane_lo)) for wg, (b, _) in zip(w, at)]
                lse2 = [wide(_rows_per_head(ls[p], lane_lo)) for ls in lse]
                q2 = [_stack_heads(load(q_ref, b).astype(BF16), head_mask) for b, _ in at]
                k2 = [load(k_ref, b).astype(BF16) for b, _ in at]
                v2 = [load(v_ref, b).astype(BF16) for b, _ in at]
                if has_prev:
                    k2 = [jnp.concatenate([load(k_ref, pv).astype(BF16), k], axis=0) for (_, pv), k in zip(at, k2)]
                    v2 = [jnp.concatenate([load(v_ref, pv).astype(BF16), v], axis=0) for (_, pv), v in zip(at, v2)]
                s = [_dot_nt(q, k) for q, k in zip(q2, k2)]
                dp = [_dot_nt(do, v) for do, v in zip(do2, v2)]
                pr = [jnp.exp(x * 0.125 + (bias_scr[...] if has_prev else bias_scr[:, BLK:2 * BLK]) - l)
                      for x, l in zip(s, lse2)]
                ds = [(pg * (x - dl) * 0.125).astype(BF16) for pg, x, dl in zip(pr, dp, dl2)]
                dq2 = [_dot(x, k) for x, k in zip(ds, k2)]
                dk2 = [_dot_tn(x, q) for x, q in zip(ds, q2)]
                dv2 = [_dot_tn(pg.astype(BF16), do) for pg, do in zip(pr, do2)]
                for (b, pv), dq, dk, dv in zip(at, dq2, dk2, dv2):
                    put(dq_acc, b, _unstack_heads(dq, lane_lo))
                    if has_prev:
                        put(dk_acc, pv, dk[:BLK])
                        put(dv_acc, pv, dv[:BLK])
                        put(dk_acc, b, dk[BLK:])
                        put(dv_acc, b, dv[BLK:])
                    else:
                        put(dk_acc, b, dk)
                        put(dv_acc, b, dv)

            _loop(0, _FIRST[d], lambda gs: block(gs, False), width=WIDTH)
            _loop(_FIRST[d], groups, lambda gs: block(gs, True), width=WIDTH)

        def emit(i):
            sl = pl.ds(pl.multiple_of(i * 512, 512), 512)
            for s in range(3):
                out_ref[s, sl, :] = acc[s, sl, :].astype(BF16)

        _loop(0, t // 512, emit)

    def col(c0):
        return pl.BlockSpec((t, BLK), lambda hp: (0, c0 + hp))

    res, extra = _pcall(
        body, name="attention_bwd", grid=(4,),
        in_specs=[col(0), col(4), col(8)] + [col(0)] * 5 + [ANY],
        out_specs=[pl.BlockSpec((3, t, BLK), lambda hp: (0, 0, hp))],
        out_shape=[jax.ShapeDtypeStruct(dproj.shape, BF16)],
        scratch_shapes=[pltpu.VMEM((3, t, BLK), F32), pltpu.VMEM((2 * BLK, 2 * BLK), F32)],
        semantics=("parallel",), vmem_mb=56, rider=rider, aliases={8: 0},
    )(qkv, qkv, qkv, dattn, dsum, *lses, dproj)
    return res[0] if rider is None else (res[0], extra)


def _order_specs(t):
    n_i = SEG // TI
    nblk = (t // HALF) * n_i
    per = TI // 8

    def main(c, col=0):
        return pl.BlockSpec((1, N_RES, TI, c), lambda s: (s // n_i, 0, s % n_i, col))

    def before(c, col=0):
        return pl.BlockSpec((1, 2, 8, c), lambda s: (jnp.maximum(s - 1, 0) // n_i, N_RES // 2 - 1,
                                                     (jnp.maximum(s - 1, 0) % n_i) * per + per - 1, col))

    def after(c, col=0):
        return pl.BlockSpec((1, 2, 8, c), lambda s: (jnp.minimum(s + 1, nblk - 1) // n_i, 0,
                                                     (jnp.minimum(s + 1, nblk - 1) % n_i) * per, col))

    return nblk, main, before, after


def _shift_in(v, row_in, up):
    rows = v.shape[0]
    idx = lax.broadcasted_iota(jnp.int32, v.shape, 0)
    fill = jnp.broadcast_to(row_in, v.shape)
    if up:
        return jnp.where(idx == rows - 1, fill, pltpu.roll(v, rows - 1, axis=0))
    return jnp.where(idx == 0, fill, pltpu.roll(v, 1, axis=0))


def _taps_behind(u, before):
    s15 = _shift_in(u[N_RES - 1], before[1, 7:8, :], up=False)
    s14 = _shift_in(u[N_RES - 2], before[0, 7:8, :], up=False)
    m1 = jnp.concatenate([s15[None], u[:N_RES - 1]], axis=0)
    m2 = jnp.concatenate([s14[None], s15[None], u[:N_RES - 2]], axis=0)
    return m1, m2


def _taps_ahead(u, after):
    t0 = _shift_in(u[0], after[0, 0:1, :], up=True)
    t1 = _shift_in(u[1], after[1, 0:1, :], up=True)
    p1 = jnp.concatenate([u[1:], t0[None]], axis=0)
    p2 = jnp.concatenate([u[2:], t0[None], t1[None]], axis=0)
    return p1, p2


def _conv_fwd(gates, before, first, cw):
    bg, cg, xc = gates[..., 0:512], gates[..., 512:1024], gates[..., 1024:1536]
    u = cg * xc
    ub = before[..., 512:1024] * before[..., 1024:1536]
    ub = jnp.where(first, jnp.zeros_like(ub), ub)
    m1, m2 = _taps_behind(u, ub)
    conv = m2 * cw[0:1, :] + m1 * cw[1:2, :] + u * cw[2:3, :]
    return bg, u, m1, m2, conv


def _sum_tokens(v):
    return jnp.sum(jnp.sum(v, axis=0), axis=0, keepdims=True)


def _mixer_fwd(x, attn, gates, cw, g_a, g_c, w_out):
    t, d = x.shape
    nblk, main, before, _ = _order_specs(t)
    rows = N_RES * TI

    def body(x_ref, at_ref, gt_ref, gb_ref, cw_ref, ga_ref, gc_ref, wa_ref, wb_ref, x1_ref, mg_ref):
        an = _rms_fwd(at_ref[0], ga_ref[...])[0].astype(BF16)
        bg, _, _, _, conv = _conv_fwd(gt_ref[0], gb_ref[0], pl.program_id(0) == 0, cw_ref[...])
        cn = _rms_fwd(bg * conv, gc_ref[...])[0].astype(BF16)
        mg_ref[0, :, :, 0:512] = an
        mg_ref[0, :, :, 512:1024] = cn
        y = _dot(an.reshape(rows, 512), wa_ref[...]) + _dot(cn.reshape(rows, 512), wb_ref[...])
        x1_ref[0] = x_ref[0] + y.reshape(N_RES, TI, d)

    const = lambda r, c, i0=0: pl.BlockSpec((r, c), lambda s: (i0, 0))
    x1, merged = pl.pallas_call(
        body, name="mixer_fwd", grid=(nblk,),
        in_specs=[main(d), main(512), main(1536, 1), before(1536, 1), const(3, 512), const(1, 512), const(1, 512),
                  const(512, d), const(512, d, 1)],
        out_specs=[main(d), main(d)],
        out_shape=[jax.ShapeDtypeStruct(_x4(x).shape, F32), jax.ShapeDtypeStruct(_x4(x).shape, BF16)],
        compiler_params=_params(("parallel",), 48),
    )(_x4(x), _x4(attn), _x4(gates), _x4(gates), cw, g_a, g_c, w_out, w_out)
    return x1.reshape(t, d), merged.reshape(t, d)


def _mixer_bwd(dx1, attn, gates, cw, g_a, g_c, w_out, head_sum, rider=None):
    t, d = dx1.shape
    nblk, main, before, _ = _order_specs(t)
    rows = N_RES * TI

    def body(dx_ref, at_ref, gt_ref, gb_ref, cw_ref, ga_ref, gc_ref, wa_ref, wb_ref, hs_ref,
             da_ref, dsum_ref, dy_ref, gga_ref, ggc_ref):
        s = pl.program_id(0)
        dxb = dx_ref[0].reshape(rows, d).astype(BF16)
        dma = _dot_nt(dxb, wa_ref[...]).reshape(N_RES, TI, 512)
        dmc = _dot_nt(dxb, wb_ref[...]).reshape(N_RES, TI, 512)
        attn_v, g_av = at_ref[0], ga_ref[...]
        _, ah, ra = _rms_fwd(attn_v, g_av)
        dattn = _rms_bwd(dma, ah, ra, g_av)
        da_ref[0] = dattn
        z = (dattn * attn_v).reshape(rows, 512)
        hs = hs_ref[...]
        z1 = z.astype(BF16)
        z2 = (z - z1.astype(F32)).astype(BF16)
        z3 = (z - z1.astype(F32) - z2.astype(F32)).astype(BF16)
        dsum_ref[0] = (_dot(z1, hs) + _dot(z2, hs) + _dot(z3, hs)).reshape(N_RES, TI, 512)
        bg, _, _, _, conv = _conv_fwd(gt_ref[0], gb_ref[0], s == 0, cw_ref[...])
        g_cv = gc_ref[...]
        _, yh, rc = _rms_fwd(bg * conv, g_cv)
        dy_ref[0] = _rms_bwd(dmc, yh, rc, g_cv)
        pa, pc = _sum_tokens(dma * ah), _sum_tokens(dmc * yh)

        @pl.when(s == 0)
        def _():
            gga_ref[...] = pa
            ggc_ref[...] = pc

        @pl.when(s != 0)
        def _():
            gga_ref[...] += pa
            ggc_ref[...] += pc

    const = lambda r, c, i0=0: pl.BlockSpec((r, c), lambda s: (i0, 0))
    shape4 = _x4(attn).shape
    res, extra = _pcall(
        body, name="mixer_bwd", grid=(nblk,),
        in_specs=[main(d), main(512), main(1536, 1), before(1536, 1), const(3, 512), const(1, 512), const(1, 512),
                  const(512, d), const(512, d, 1), const(512, 512)],
        out_specs=[main(512)] * 3 + [const(1, 512), const(1, 512)],
        out_shape=[jax.ShapeDtypeStruct(shape4, F32)] * 3 + [jax.ShapeDtypeStruct((1, 512), F32)] * 2,
        semantics=("arbitrary",), vmem_mb=48, rider=rider,
    )(_x4(dx1), _x4(attn), _x4(gates), _x4(gates), cw, g_a, g_c, w_out, w_out, head_sum)
    res = [r.reshape(t, 512) for r in res[:3]] + res[3:]
    return res if rider is None else (res, extra)


def _conv_bwd(dy, gates, cw):
    t = dy.shape[0]
    nblk, main, before, after = _order_specs(t)
    n_i = SEG // TI

    def body(dy_ref, dya_ref, gt_ref, gb_ref, ga_ref, cw_ref, dp_ref, gcw_ref):
        s = pl.program_id(0)
        cw_v, gates_v = cw_ref[...], gt_ref[0]
        bg, u, m1, m2, conv = _conv_fwd(gates_v, gb_ref[0], s == 0, cw_v)
        dy_v = dy_ref[0]
        dconv = dy_v * bg
        dca = dya_ref[0] * ga_ref[0][..., 0:512]
        dca = jnp.where(s == nblk - 1, jnp.zeros_like(dca), dca)
        p1, p2 = _taps_ahead(dconv, dca)
        du = dconv * cw_v[2:3, :] + p1 * cw_v[1:2, :] + p2 * cw_v[0:1, :]
        dp_ref[0, 0] = (dy_v * conv).astype(BF16)
        dp_ref[1, 0] = (du * gates_v[..., 1024:1536]).astype(BF16)
        dp_ref[2, 0] = (du * gates_v[..., 512:1024]).astype(BF16)
        parts = [_sum_tokens(dconv * m2), _sum_tokens(dconv * m1), _sum_tokens(dconv * u)]

        @pl.when(s == 0)
        def _():
            gcw_ref[...] = jnp.zeros_like(gcw_ref)

        for tap in range(3):
            gcw_ref[tap:tap + 1, :] += parts[tap]

    dproj, gcw = pl.pallas_call(
        body, name="conv_bwd", grid=(nblk,),
        in_specs=[main(512), after(512), main(1536, 1), before(1536, 1), after(1536, 1),
                  pl.BlockSpec((3, 512), lambda s: (0, 0))],
        out_specs=[pl.BlockSpec((3, 1, N_RES, TI, 512), lambda s: (1, s // n_i, 0, s % n_i, 0)),
                   pl.BlockSpec((8, 512), lambda s: (0, 0))],
        out_shape=[jax.ShapeDtypeStruct((6, t // HALF, N_RES, SEG, 512), BF16), jax.ShapeDtypeStruct((8, 512), F32)],
        compiler_params=_params(("arbitrary",), 40),
    )(_x4(dy), _x4(dy), _x4(gates), _x4(gates), _x4(gates), cw)
    return dproj.reshape(6, t, 512), gcw


def _xattn_fwd(x1, g, w_q, kv, w_o, *, tb):
    t, d = x1.shape
    hd = d // N_MEM_HEADS
    m = kv.shape[0]

    def body(x_ref, g_ref, wq_ref, k_ref, v_ref, wo_ref, x2_ref, h_ref, q_ref, o_ref):
        xv = x_ref[...]
        h = _rms_fwd(xv, g_ref[...])[0].astype(BF16)
        h_ref[...] = h
        q = _dot(h, wq_ref[...]).astype(BF16)
        q_ref[...] = q
        for hh in range(N_MEM_HEADS):
            sl = slice(hh * hd, (hh + 1) * hd)
            s = _dot_nt(q[:, sl], k_ref[:, sl]) * (1.0 / 16.0)
            e = jnp.exp(s - jnp.max(s, axis=1, keepdims=True))
            p = e / jnp.sum(e, axis=1, keepdims=True)
            o_ref[:, sl] = _dot(p.astype(BF16), v_ref[:, sl]).astype(BF16)
        x2_ref[...] = xv + _dot(o_ref[...], wo_ref[...])

    tok = pl.BlockSpec((tb, d), lambda i: (i, 0))
    full = pl.BlockSpec((d, d), lambda i: (0, 0))
    return pl.pallas_call(
        body, name="xattn_fwd", grid=(t // tb,),
        in_specs=[tok, pl.BlockSpec((1, d), lambda i: (0, 0)), full,
                  pl.BlockSpec((m, d), lambda i: (0, 0)), pl.BlockSpec((m, d), lambda i: (0, 1)), full],
        out_specs=[tok] * 4,
        out_shape=[jax.ShapeDtypeStruct((t, d), F32)] + [jax.ShapeDtypeStruct((t, d), BF16)] * 3,
        compiler_params=_params(("parallel",), 48),
    )(x1, g, w_q, kv, kv, w_o)


def _xattn_bwd(dx2, x1, g, q, w_q, kv, w_o, *, tb, rider=None):
    t, d = x1.shape
    hd = d // N_MEM_HEADS
    m = kv.shape[0]

    def body(dx2_ref, x_ref, g_ref, q_ref, wq_ref, k_ref, v_ref, wo_ref,
             dx1_ref, dx1b_ref, dq_ref, dk_ref, dv_ref, gg_ref):
        i = pl.program_id(0)

        @pl.when(i == 0)
        def _():
            dk_ref[...] = jnp.zeros_like(dk_ref)
            dv_ref[...] = jnp.zeros_like(dv_ref)

        dx2 = dx2_ref[...]
        do = _dot_nt(dx2.astype(BF16), wo_ref[...]).astype(BF16)
        for hh in range(N_MEM_HEADS):
            sl = slice(hh * hd, (hh + 1) * hd)
            qh, kh, vh, doh = q_ref[:, sl], k_ref[:, sl], v_ref[:, sl], do[:, sl]
            s = _dot_nt(qh, kh) * (1.0 / 16.0)
            e = jnp.exp(s - jnp.max(s, axis=1, keepdims=True))
            p = e / jnp.sum(e, axis=1, keepdims=True)
            dp = _dot_nt(doh, vh)
            ds = (p * (dp - jnp.sum(dp * p, axis=1, keepdims=True)) * (1.0 / 16.0)).astype(BF16)
            dq_ref[:, sl] = _dot(ds, kh).astype(BF16)
            dk_ref[:, sl] += _dot_tn(ds, qh)
            dv_ref[:, sl] += _dot_tn(p.astype(BF16), doh)
        dh = _dot_nt(dq_ref[...], wq_ref[...])
        g_v = g_ref[...]
        _, xh, r = _rms_fwd(x_ref[...], g_v)
        dx1 = dx2 + _rms_bwd(dh, xh, r, g_v)
        dx1_ref[...] = dx1
        dx1b_ref[...] = dx1.astype(BF16)
        part = jnp.sum(dh * xh, axis=0, keepdims=True)

        @pl.when(i == 0)
        def _():
            gg_ref[...] = part

        @pl.when(i != 0)
        def _():
            gg_ref[...] += part

    tok = pl.BlockSpec((tb, d), lambda i: (i, 0))
    full = pl.BlockSpec((d, d), lambda i: (0, 0))
    acc = pl.BlockSpec((m, d), lambda i: (0, 0))
    res, extra = _pcall(
        body, name="xattn_bwd", grid=(t // tb,),
        in_specs=[tok, tok, pl.BlockSpec((1, d), lambda i: (0, 0)), tok, full,
                  pl.BlockSpec((m, d), lambda i: (0, 0)), pl.BlockSpec((m, d), lambda i: (0, 1)), full],
        out_specs=[tok, tok, tok, acc, acc, pl.BlockSpec((1, d), lambda i: (0, 0))],
        out_shape=[jax.ShapeDtypeStruct((t, d), F32), jax.ShapeDtypeStruct((t, d), BF16),
                   jax.ShapeDtypeStruct((t, d), BF16),
                   jax.ShapeDtypeStruct((m, d), F32), jax.ShapeDtypeStruct((m, d), F32),
                   jax.ShapeDtypeStruct((1, d), F32)],
        semantics=("arbitrary",), vmem_mb=48, rider=rider,
    )(dx2, x1, g, q, w_q, kv, kv, w_o)
    return res if rider is None else (res, extra)


def _mlp_down_loss(a, w_down, x2, tgt, g, *, tb):
    t, d = x2.shape
    f = a.shape[1]

    def body(a_ref, w_ref, x_ref, t_ref, g_ref, dx_ref, dxb_ref, loss_ref, gg_ref):
        i = pl.program_id(0)
        av = a_ref[...].astype(F32)
        x3 = x_ref[...] + _dot((av * av).astype(BF16), w_ref[...])
        g_v = g_ref[...]
        out, xh, r = _rms_fwd(x3, g_v)
        err = out - t_ref[...]
        dout = err * (1.0 / d)
        dx = _rms_bwd(dout, xh, r, g_v)
        dx_ref[...] = dx
        dxb_ref[...] = dx.astype(BF16)
        part = jnp.sum(dout * xh, axis=0, keepdims=True)
        lpart = 0.5 * jnp.sum(jnp.mean(err * err, axis=-1, keepdims=True), axis=0, keepdims=True)
        lpart = jnp.broadcast_to(lpart, loss_ref.shape)

        @pl.when(i == 0)
        def _():
            gg_ref[...] = part
            loss_ref[...] = lpart

        @pl.when(i != 0)
        def _():
            gg_ref[...] += part
            loss_ref[...] += lpart

    tok = pl.BlockSpec((tb, d), lambda i: (i, 0))
    return pl.pallas_call(
        body, name="mlp_down_loss", grid=(t // tb,),
        in_specs=[pl.BlockSpec((tb, f), lambda i: (i, 0)), pl.BlockSpec((f, d), lambda i: (0, 0)), tok, tok,
                  pl.BlockSpec((1, d), lambda i: (0, 0))],
        out_specs=[tok, tok, pl.BlockSpec((8, 128), lambda i: (0, 0)), pl.BlockSpec((1, d), lambda i: (0, 0))],
        out_shape=[jax.ShapeDtypeStruct((t, d), F32), jax.ShapeDtypeStruct((t, d), BF16),
                   jax.ShapeDtypeStruct((8, 128), F32), jax.ShapeDtypeStruct((1, d), F32)],
        compiler_params=_params(("arbitrary",), 56),
    )(a, w_down, x2, tgt, g)


def _mlp_dpre(dx3, w_down, a, *, tb, bn):
    t, d = dx3.shape
    f = a.shape[1]

    def body(dx_ref, w_ref, a_ref, o_ref):
        o_ref[...] = (2.0 * a_ref[...].astype(F32) * _dot_nt(dx_ref[...], w_ref[...])).astype(BF16)

    return pl.pallas_call(
        body, name="mlp_dpre", grid=(t // tb, f // bn),
        in_specs=[pl.BlockSpec((tb, d), lambda i, j: (i, 0)), pl.BlockSpec((bn, d), lambda i, j: (j, 0)),
                  pl.BlockSpec((tb, bn), lambda i, j: (i, j))],
        out_specs=pl.BlockSpec((tb, bn), lambda i, j: (i, j)),
        out_shape=jax.ShapeDtypeStruct((t, f), BF16),
        compiler_params=_params(("parallel", "arbitrary"), 48),
    )(dx3, w_down, a)


def _adamw(gsum, w, m, v):
    m_new = ADAM_B1 * m + (1.0 - ADAM_B1) * gsum
    v_new = ADAM_B2 * v + (1.0 - ADAM_B2) * (gsum * gsum)
    m_hat = m_new / (1.0 - ADAM_B1 ** ADAM_STEP)
    v_hat = v_new / (1.0 - ADAM_B2 ** ADAM_STEP)
    delta = -ADAM_LR * (m_hat / (jnp.sqrt(v_hat) + ADAM_EPS) + ADAM_WD * w)
    return delta, m_new, v_new


def _sum_adamw(parts, w, m, v, *, name, tr):
    r, c = w.shape

    def body(p_ref, w_ref, m_ref, v_ref, g_ref, d_ref, mo_ref, vo_ref):
        g = p_ref[0].astype(F32)
        for k in range(1, N_DEV):
            g = g + p_ref[k].astype(F32)
        g_ref[...] = g
        d_ref[...], mo_ref[...], vo_ref[...] = _adamw(g, w_ref[...], m_ref[...], v_ref[...])

    blk = pl.BlockSpec((tr, c), lambda i: (i, 0))
    return pl.pallas_call(
        body, name=name, grid=(r // tr,),
        in_specs=[pl.BlockSpec((N_DEV, tr, c), lambda i: (0, i, 0)), blk, blk, blk],
        out_specs=[blk] * 4, out_shape=[jax.ShapeDtypeStruct((r, c), F32)] * 4,
        compiler_params=_params(("parallel",), 40),
    )(parts, w, m, v)


def _sum_small(parts):
    _, r, c = parts.shape

    def body(p_ref, o_ref):
        s = p_ref[0]
        for k in range(1, N_DEV):
            s = s + p_ref[k]
        o_ref[...] = s

    return pl.pallas_call(body, name="sum_small", out_shape=jax.ShapeDtypeStruct((r, c), F32))(parts)


def _adamw_small(g, w, m, v):
    def body(g_ref, w_ref, m_ref, v_ref, d_ref, mo_ref, vo_ref):
        d_ref[...], mo_ref[...], vo_ref[...] = _adamw(g_ref[...], w_ref[...], m_ref[...], v_ref[...])

    return pl.pallas_call(body, name="adamw_small", out_shape=[jax.ShapeDtypeStruct(g.shape, F32)] * 3)(g, w, m, v)


def _head_sum_matrix():
    r = lax.broadcasted_iota(jnp.int32, (512, 512), 0) // HEAD_DIM
    c = lax.broadcasted_iota(jnp.int32, (512, 512), 1) // HEAD_DIM
    return (r == c).astype(BF16)


_SHARD_AXIS = dict(w_in=1, w_out=0, w_q=0, w_kv=1, w_o=0, w_up=1, w_down=0, conv_w=None, small=None)


class _Weights:
    def __init__(self, full, shards=None):
        self.full = dict(full)
        self.shards = shards

    def rider(self, names):
        if self.shards is None:
            return None
        return _Gather([self.shards[n] for n in names], [_SHARD_AXIS[n] for n in names])

    def arrived(self, names, gathered):
        if gathered is not None:
            for n, g in zip(names, gathered):
                self.full[n] = g.transpose(1, 0, 2).reshape(g.shape[1], -1) if n == "conv_w" else g

    def __getitem__(self, name):
        return self.full[name]


class _Grads:
    def __init__(self, distributed):
        self.distributed = distributed
        self.local = {}
        self.received = {}

    def add(self, name, g):
        self.local[name] = g

    def rider(self, names, pieces=None):
        if not self.distributed:
            return None
        pieces = pieces or [(0, 1)] * len(names)
        into = [self.received.get(n) if p[0] else None for n, p in zip(names, pieces)]
        return _Exchange([self.local[n] for n in names], [_SHARD_AXIS[n] for n in names], pieces, into)

    def arrived(self, names, received):
        if received is not None:
            for n, r in zip(names, received):
                self.received[n] = r


def _ride(fn, *args, rider=None, **kw):
    if rider is None:
        return fn(*args, **kw), None
    return fn(*args, rider=rider, **kw)


def _local_step(x, mem, tgt, gains, weights, grads):
    names = ["w_in", "w_out", "conv_w"]
    (x, tgt), got = _ride(_reorder, [x, tgt], False, "reorder_in", rider=weights.rider(names))
    weights.arrived(names, got)
    w_in, cw = weights["w_in"], weights["conv_w"]

    names = ["w_q", "w_kv", "w_o"]
    (proj, h1), got = _ride(_norm_matmul, x, gains["g_mix"], w_in, name="proj", out_dtype=F32, tb=1024, bn=768,
                            save_h=True, rider=weights.rider(names))
    weights.arrived(names, got)
    names = ["w_up", "w_down"]
    (attn, *lses), got = _ride(_attention_fwd, proj, rider=weights.rider(names))
    weights.arrived(names, got)
    x1, merged = _mixer_fwd(x, attn, proj, cw, gains["g_attn_out"], gains["g_conv_out"], weights["w_out"])
    kv, mem_n = _norm_matmul(mem, gains["g_mem"], weights["w_kv"], name="mem_kv", out_dtype=BF16, tb=mem.shape[0],
                             bn=1024, save_h=True)
    x2, h2, qm, om = _xattn_fwd(x1, gains["g_xattn"], weights["w_q"], kv, weights["w_o"], tb=256)
    w_up, w_down = weights["w_up"], weights["w_down"]
    a, h3 = _norm_matmul(x2, gains["g_mlp"], w_up, name="mlp_up", out_dtype=BF16, tb=1024, bn=1024, relu=True,
                         save_h=True)
    dx3, dx3b, loss_blk, gg_final = _mlp_down_loss(a, w_down, x2, tgt, gains["g_final"], tb=256)

    dpre = _mlp_dpre(dx3b, w_down, a, tb=1024, bn=1024)
    grads.add("w_down", _matmul_tn(a, dx3b, name="grad_w_down", bm=512, bn=1024, square_a=True))
    gw_up, got = _ride(_matmul_tn, h3, dpre, name="grad_w_up", bm=1024, bn=512,
                       rider=grads.rider(["w_down"], [(0, 2)]))
    grads.arrived(["w_down"], got)
    grads.add("w_up", gw_up)
    (dx2, dx2b, gg_mlp), got = _ride(_matmul_nt_normbwd, dpre, w_up, x2, gains["g_mlp"], dx3, name="mlp_dx", tb=512,
                                     also_bf16=True, rider=grads.rider(["w_down"], [(1, 2)]))
    grads.arrived(["w_down"], got)

    grads.add("w_o", _matmul_tn(om, dx2b, name="grad_w_o", bm=1024, bn=512))
    names = ["w_up", "w_o"]
    (dx1, dx1b, dqm, dk, dv, gg_xattn), got = _ride(_xattn_bwd, dx2, x1, gains["g_xattn"], qm, weights["w_q"], kv,
                                                    weights["w_o"], tb=256,
                                                    rider=grads.rider(names, [(0, 2), (0, 1)]))
    grads.arrived(names, got)
    grads.add("w_q", _matmul_tn(h2, dqm, name="grad_w_q", bm=1024, bn=512))
    dkv = jnp.concatenate([dk, dv], axis=1).astype(BF16)
    grads.add("w_kv", _matmul_tn(mem_n, dkv, name="grad_w_kv", bm=1024, bn=1024))
    _, gg_mem = _matmul_nt_normbwd(dkv, weights["w_kv"], mem, gains["g_mem"], None, name="mem_dx", tb=mem.shape[0])

    grads.add("w_out", _matmul_tn(merged, dx1b, name="grad_w_out", bm=1024, bn=512))
    (dattn, dsum, dy, gg_attn, gg_conv), got = _ride(
        _mixer_bwd, dx1, attn, proj, cw, gains["g_attn_out"], gains["g_conv_out"], weights["w_out"],
        _head_sum_matrix(), rider=grads.rider(["w_up"], [(1, 2)]))
    grads.arrived(["w_up"], got)
    dproj, gcw = _conv_bwd(dy, proj, cw)
    names = ["w_q", "w_kv", "w_out"]
    dproj, got = _ride(_attention_bwd, proj, dattn, dsum, lses, dproj, rider=grads.rider(names))
    grads.arrived(names, got)
    grads.add("w_in", _matmul_tn(h1, dproj, name="grad_w_in", bm=1024, bn=512))
    (grad_x, gg_mix), got = _ride(_matmul_nt_normbwd, dproj, w_in, x, gains["g_mix"], dx1, name="mixer_dx",
                                  tb=512, rider=grads.rider(["w_in"], [(0, 2)]))
    grads.arrived(["w_in"], got)

    rows = [gg_mix, gg_xattn, gg_mem, gg_mlp, gg_final, jnp.concatenate([gg_attn, gg_conv], axis=1),
            jnp.pad(gcw[0:3], ((0, 0), (0, 512))), jnp.pad(loss_blk[0:1, 0:1], ((0, 6), (0, 1023)))]
    grads.add("small", jnp.concatenate(rows, axis=0))
    names = ["w_in", "small"]
    (grad_x,), got = _ride(_reorder, [grad_x], True, "reorder_out", rider=grads.rider(names, [(1, 2), (0, 1)]))
    grads.arrived(names, got)
    return grad_x


_BIG = ("w_in", "w_out", "w_q", "w_kv", "w_o", "w_up", "w_down")
_GAIN_ROWS = ("g_mix", "g_xattn", "g_mem", "g_mlp", "g_final")


def _pack_small(vals, conv):
    rows = [vals[k].reshape(1, -1) for k in _GAIN_ROWS]
    rows.append(jnp.concatenate([vals["g_attn_out"].reshape(1, -1), vals["g_conv_out"].reshape(1, -1)], axis=1))
    flat = conv.reshape(1, -1)
    rows.append(jnp.pad(flat, ((0, 0), (0, 1024 - flat.shape[1]))))
    rows.append(jnp.zeros((1, 1024), F32))
    return jnp.concatenate(rows, axis=0)


def kernel(x, mem, g_mix, w_in, conv_w, g_attn_out, g_conv_out, w_out, g_xattn, g_mem, w_q_mem, w_kv_mem, w_o_mem, g_mlp, w_up, w_down, g_final, loss_target, m_g_mix, m_w_in, m_conv_w, m_g_attn_out, m_g_conv_out, m_w_out, m_g_xattn, m_g_mem, m_w_q_mem, m_w_kv_mem, m_w_o_mem, m_g_mlp, m_w_up, m_w_down, m_g_final, v_g_mix, v_w_in, v_conv_w, v_g_attn_out, v_g_conv_out, v_w_out, v_g_xattn, v_g_mem, v_w_q_mem, v_w_kv_mem, v_w_o_mem, v_g_mlp, v_w_up, v_w_down, v_g_final):
    d = x.shape[-1]
    me = 4 * lax.axis_index("x") + 2 * lax.axis_index("y") + lax.axis_index("c")
    w_shards = dict(w_in=w_in, w_out=w_out, w_q=w_q_mem, w_kv=w_kv_mem, w_o=w_o_mem, w_up=w_up, w_down=w_down)
    m_shards = dict(w_in=m_w_in, w_out=m_w_out, w_q=m_w_q_mem, w_kv=m_w_kv_mem, w_o=m_w_o_mem, w_up=m_w_up,
                    w_down=m_w_down)
    v_shards = dict(w_in=v_w_in, w_out=v_w_out, w_q=v_w_q_mem, w_kv=v_w_kv_mem, w_o=v_w_o_mem, w_up=v_w_up,
                    w_down=v_w_down)
    gains = dict(g_mix=g_mix, g_attn_out=g_attn_out, g_conv_out=g_conv_out, g_xattn=g_xattn, g_mem=g_mem,
                 g_mlp=g_mlp, g_final=g_final)
    gains2 = {k: v.reshape(1, -1) for k, v in gains.items()}

    shards = {k: w_shards[k].astype(BF16) for k in _BIG}
    shards["conv_w"] = conv_w
    grads = _Grads(distributed=True)
    grad_x = _local_step(x[0], mem[0], loss_target[0], gains2, _Weights({}, shards), grads)
    small_received = grads.received["small"]

    outs = {}
    tiles = dict(w_in=256, w_out=128, w_q=128, w_kv=256, w_o=128, w_up=256, w_down=256)
    for k in _BIG:
        outs[k] = _sum_adamw(grads.received[k], w_shards[k], m_shards[k], v_shards[k], name=f"adamw_{k}",
                             tr=tiles[k])

    ssum = _sum_small(small_received)
    loss = ssum[9, 0]
    g_small = {k: ssum[i] for i, k in enumerate(_GAIN_ROWS)}
    g_small["g_attn_out"] = ssum[5, 0:512]
    g_small["g_conv_out"] = ssum[5, 512:1024]
    g_conv = lax.dynamic_slice_in_dim(ssum[6:9, 0:512], me * 64, 64, axis=1)
    m_small = dict(g_mix=m_g_mix, g_attn_out=m_g_attn_out, g_conv_out=m_g_conv_out, g_xattn=m_g_xattn,
                   g_mem=m_g_mem, g_mlp=m_g_mlp, g_final=m_g_final)
    v_small = dict(g_mix=v_g_mix, g_attn_out=v_g_attn_out, g_conv_out=v_g_conv_out, g_xattn=v_g_xattn,
                   g_mem=v_g_mem, g_mlp=v_g_mlp, g_final=v_g_final)
    packed = [_pack_small(g_small, g_conv), _pack_small(gains, conv_w), _pack_small(m_small, m_conv_w),
              _pack_small(v_small, v_conv_w)]
    upd = _adamw_small(*packed)

    def unpack(p):
        res = {k: p[i] for i, k in enumerate(_GAIN_ROWS)}
        res["g_attn_out"] = p[5, 0:512]
        res["g_conv_out"] = p[5, 512:1024]
        res["conv_w"] = p[6, 0:192].reshape(3, 64)
        return res

    g_small["conv_w"] = g_conv
    small_out = [g_small] + [unpack(p) for p in upd]
    names = {"g_mix": "g_mix", "w_in": "w_in", "conv_w": "conv_w", "g_attn_out": "g_attn_out",
             "g_conv_out": "g_conv_out", "w_out": "w_out", "g_xattn": "g_xattn", "g_mem": "g_mem",
             "w_q_mem": "w_q", "w_kv_mem": "w_kv", "w_o_mem": "w_o", "g_mlp": "g_mlp", "w_up": "w_up",
             "w_down": "w_down", "g_final": "g_final"}
    result = [loss, grad_x[None]]
    for which in range(4):
        for key in names.values():
            result.append(outs[key][which] if key in outs else small_out[which][key])
    return tuple(result)
```

```python
import math

import jax
import jax.numpy as jnp
from jax import lax
from jax.experimental import pallas as pl
from jax.experimental.pallas import tpu as pltpu

F32 = jnp.float32
BF16 = jnp.bfloat16
NORM_EPS = 1e-6
NEG_INF = -1e30
N_DEV = 8
BLK = 128
HEAD_DIM = 64
N_MEM_HEADS = 4
ADAM_LR = 0.001
ADAM_B1 = 0.9
ADAM_B2 = 0.999
ADAM_EPS = 1e-08
ADAM_WD = 0.01
ADAM_STEP = 10
MESH = pl.DeviceIdType.MESH
ANY = pl.BlockSpec(memory_space=pl.ANY)


def _dot(a, b):
    return jnp.dot(a, b, preferred_element_type=F32)


def _dot_nt(a, b):
    return lax.dot_general(a, b, (((1,), (1,)), ((), ())), preferred_element_type=F32)


def _dot_tn(a, b):
    return lax.dot_general(a, b, (((0,), (0,)), ((), ())), preferred_element_type=F32)


def _params(semantics, vmem_mb):
    return pltpu.CompilerParams(dimension_semantics=semantics, vmem_limit_bytes=vmem_mb << 20)


def _rms_fwd(x, g):
    r = lax.rsqrt(jnp.mean(x * x, axis=-1, keepdims=True) + NORM_EPS)
    xh = x * r
    return xh * g, xh, r


def _rms_bwd(dy, xh, r, g):
    gy = dy * g
    return r * (gy - xh * jnp.mean(xh * gy, axis=-1, keepdims=True))


def _position():
    x, y, c = lax.axis_index("x"), lax.axis_index("y"), lax.axis_index("c")
    return x, y, c


def _block_of(ref, j, axis, shard_shape):
    r, c = shard_shape
    if axis is None:
        return ref.at[j]
    if axis == 0:
        return ref.at[pl.ds(j * r, r), :]
    return ref.at[:, pl.ds(j * c, c)]


class _Gather:
    has_mid = True
    alias_pairs = ()

    def __init__(self, shards, axes):
        self.arrays = list(shards)
        self.axes = list(axes)
        self.n = len(self.arrays)

    def out_shape(self):
        res = []
        for s, axis in zip(self.arrays, self.axes):
            r, c = s.shape
            shape = (N_DEV, r, c) if axis is None else (N_DEV * r, c) if axis == 0 else (r, N_DEV * c)
            res.append(jax.ShapeDtypeStruct(shape, s.dtype))
        return res

    def scratch(self):
        return [pltpu.SemaphoreType.DMA((self.n, 7)), pltpu.SemaphoreType.DMA((self.n, 7)),
                pltpu.SemaphoreType.DMA((self.n,))]

    def _ctx(self, ins, outs, sems):
        send_sems, recv_sems, local_sems = sems
        x, y, c = _position()
        me, sibling = (x, y, c), (x, y, 1 - c)
        chips = [(1 - x, y), (x, 1 - y), (1 - x, 1 - y)]

        def lin(px, py, pc):
            return 4 * px + 2 * py + pc

        def place(a, block):
            return _block_of(outs[a], lin(*block), self.axes[a], self.arrays[a].shape)

        def copy(a, k, block, to, src=None):
            dst = place(a, block)
            return pltpu.make_async_remote_copy(
                src_ref=dst if src is None else src, dst_ref=dst,
                send_sem=send_sems.at[a, k], recv_sem=recv_sems.at[a, k],
                device_id=to, device_id_type=MESH)

        def mine():
            return [pltpu.make_async_copy(ins[a], place(a, me), local_sems.at[a]) for a in range(self.n)]

        def first():
            res = []
            for a in range(self.n):
                res.append(copy(a, 0, me, sibling, src=ins[a]))
                res += [copy(a, 1 + j, me, (*chip, c), src=ins[a]) for j, chip in enumerate(chips)]
            return res

        return c, me, sibling, chips, copy, mine, first

    def start(self, ins, outs, sems):
        _, _, _, _, _, mine, first = self._ctx(ins, outs, sems)
        for cp in mine() + first():
            cp.start()

    def mid(self, ins, outs, sems):
        c, me, sibling, chips, copy, _, _ = self._ctx(ins, outs, sems)
        for j, chip in enumerate(chips):
            for a in range(self.n):
                copy(a, 1 + j, (*chip, c), me).wait_recv()
                copy(a, 4 + j, (*chip, c), sibling).start()

    def finish(self, ins, outs, sems):
        c, me, sibling, chips, copy, mine, first = self._ctx(ins, outs, sems)
        for a in range(self.n):
            copy(a, 0, sibling, me).wait_recv()
            for j, chip in enumerate(chips):
                copy(a, 4 + j, (*chip, 1 - c), me).wait_recv()
        for cp in first():
            cp.wait_send()
        for j, chip in enumerate(chips):
            for a in range(self.n):
                copy(a, 4 + j, (*chip, c), sibling).wait_send()
        for cp in mine():
            cp.wait()


class _Exchange:
    has_mid = False

    def __init__(self, parts, axes, pieces=None, into=None):
        self.n = len(parts)
        self.axes = list(axes)
        self.pieces = list(pieces or [(0, 1)] * self.n)
        into = list(into or [None] * self.n)
        kept = [a for a in range(self.n) if into[a] is not None]
        self.arrays = list(parts) + [into[a] for a in kept]
        self.alias_pairs = [(self.n + i, a) for i, a in enumerate(kept)]

    def _piece(self, a):
        r, c = self.arrays[a].shape
        axis = self.axes[a]
        return (r, c) if axis is None else (r // N_DEV, c) if axis == 0 else (r, c // N_DEV)

    def _rows(self, a):
        k, of = self.pieces[a]
        rows = self._piece(a)[0] // of
        return pl.ds(k * rows, rows)

    def out_shape(self):
        return [jax.ShapeDtypeStruct((N_DEV,) + self._piece(a), self.arrays[a].dtype) for a in range(self.n)]

    def scratch(self):
        return [pltpu.SemaphoreType.DMA((self.n, 7)), pltpu.SemaphoreType.DMA((self.n, 7)),
                pltpu.SemaphoreType.DMA((self.n,))]

    def _ctx(self, ins, outs, sems):
        send_sems, recv_sems, local_sems = sems
        x, y, c = _position()
        me = 4 * x + 2 * y + c

        def src(a, j):
            block = ins[a] if self.axes[a] is None else _block_of(ins[a], j, self.axes[a], self._piece(a))
            return block.at[self._rows(a), :]

        def dst(a, j):
            return outs[a].at[j, self._rows(a), :]

        def local():
            return [pltpu.make_async_copy(src(a, me), dst(a, me), local_sems.at[a]) for a in range(self.n)]

        def remote(inbound):
            res = []
            for a in range(self.n):
                for k in range(1, N_DEV):
                    peer = (1 - x if k & 4 else x, 1 - y if k & 2 else y, 1 - c if k & 1 else c)
                    plin = 4 * peer[0] + 2 * peer[1] + peer[2]
                    res.append(pltpu.make_async_remote_copy(
                        src_ref=src(a, plin), dst_ref=dst(a, plin if inbound else me),
                        send_sem=send_sems.at[a, k - 1], recv_sem=recv_sems.at[a, k - 1],
                        device_id=peer, device_id_type=MESH))
            return res

        return local, remote

    def start(self, ins, outs, sems):
        local, remote = self._ctx(ins, outs, sems)
        for cp in local() + remote(False):
            cp.start()

    def finish(self, ins, outs, sems):
        local, remote = self._ctx(ins, outs, sems)
        for cp in remote(True):
            cp.wait_recv()
        for cp in remote(False):
            cp.wait_send()
        for cp in local():
            cp.wait()


def _comm_call(rider, name):
    n_in, n_out = len(rider.arrays), len(rider.out_shape())

    def body(*refs):
        ins, outs, sems = refs[:n_in], refs[n_in:n_in + n_out], refs[n_in + n_out:]
        rider.start(ins, outs, sems)
        if rider.has_mid:
            rider.mid(ins, outs, sems)
        rider.finish(ins, outs, sems)

    return pl.pallas_call(
        body, name=name, out_shape=rider.out_shape(),
        in_specs=[ANY] * n_in, out_specs=[ANY] * n_out, scratch_shapes=rider.scratch(),
        input_output_aliases=dict(rider.alias_pairs),
    )(*rider.arrays)


def _pcall(body, *, name, grid, in_specs, out_specs, out_shape, scratch_shapes=(), semantics, vmem_mb, rider=None,
           aliases=None):
    in_specs, out_specs, out_shape = list(in_specs), list(out_specs), list(out_shape)
    scratch_shapes = list(scratch_shapes)
    aliases = dict(aliases or {})
    if rider is None:
        call = pl.pallas_call(body, name=name, grid=grid, in_specs=in_specs, out_specs=out_specs,
                              out_shape=out_shape, scratch_shapes=scratch_shapes, input_output_aliases=aliases,
                              compiler_params=_params(semantics, vmem_mb))
        return lambda *args: (list(call(*args)), None)
    n_in, n_out, n_scr = len(in_specs), len(out_specs), len(scratch_shapes)
    r_in, r_shapes = len(rider.arrays), rider.out_shape()
    r_out = len(r_shapes)
    aliases.update({n_in + i: n_out + o for i, o in rider.alias_pairs})
    total = math.prod(grid)
    mid_step = (3 * total) // 4

    def wrapped(*refs):
        bounds = [0, n_in, r_in, n_out, r_out, n_scr]
        for i in range(1, len(bounds)):
            bounds[i] += bounds[i - 1]
        a, ra, o, ro, s = (refs[bounds[i]:bounds[i + 1]] for i in range(5))
        rs = refs[bounds[5]:]
        step = pl.program_id(0)
        for k in range(1, len(grid)):
            step = step * grid[k] + pl.program_id(k)
        pl.when(step == 0)(lambda: rider.start(ra, ro, rs))
        body(*a, *o, *s)
        if rider.has_mid:
            pl.when(step == mid_step)(lambda: rider.mid(ra, ro, rs))
        pl.when(step == total - 1)(lambda: rider.finish(ra, ro, rs))

    call = pl.pallas_call(
        wrapped, name=name, grid=grid, in_specs=in_specs + [ANY] * r_in, out_specs=out_specs + [ANY] * r_out,
        out_shape=out_shape + r_shapes, scratch_shapes=scratch_shapes + rider.scratch(),
        input_output_aliases=aliases, compiler_params=_params(("arbitrary",) * len(grid), vmem_mb))

    def run(*args):
        res = call(*args, *rider.arrays)
        return list(res[:n_out]), list(res[n_out:])

    return run


def _norm_matmul(x, g, w, *, name, out_dtype, tb, bn, relu=False, save_h=False, rider=None):
    t, d = x.shape
    n = w.shape[1]

    def body(x_ref, g_ref, w_ref, o_ref, *rest):
        h_scr = rest[-1]

        @pl.when(pl.program_id(1) == 0)
        def _():
            h = _rms_fwd(x_ref[...], g_ref[...])[0].astype(BF16)
            h_scr[...] = h
            if save_h:
                rest[0][...] = h

        acc = _dot(h_scr[...], w_ref[...])
        if relu:
            acc = jnp.maximum(acc, 0.0)
        o_ref[...] = acc.astype(out_dtype)

    out_shape = [jax.ShapeDtypeStruct((t, n), out_dtype)]
    out_specs = [pl.BlockSpec((tb, bn), lambda i, j: (i, j))]
    if save_h:
        out_shape.append(jax.ShapeDtypeStruct((t, d), BF16))
        out_specs.append(pl.BlockSpec((tb, d), lambda i, j: (i, 0)))
    res, extra = _pcall(
        body, name=name, grid=(t // tb, n // bn),
        in_specs=[pl.BlockSpec((tb, d), lambda i, j: (i, 0)),
                  pl.BlockSpec((1, d), lambda i, j: (0, 0)),
                  pl.BlockSpec((d, bn), lambda i, j: (0, j))],
        out_specs=out_specs, out_shape=out_shape,
        scratch_shapes=[pltpu.VMEM((tb, d), BF16)],
        semantics=("parallel", "arbitrary"), vmem_mb=48, rider=rider,
    )(x, g, w)
    res = res if save_h else res[0]
    return res if rider is None else (res, extra)


def _matmul_nt_normbwd(dy, w, x, g, dres, *, name, tb, also_bf16=False, rider=None):
    t, d = x.shape
    stacked = dy.ndim == 3
    has_res = dres is not None

    def body(dy_ref, w_ref, x_ref, g_ref, *rest):
        rest = list(rest)
        dres_ref = rest.pop(0) if has_res else None
        dx_ref = rest.pop(0)
        dxb_ref = rest.pop(0) if also_bf16 else None
        gg_ref = rest.pop(0)
        i = pl.program_id(0)
        if stacked:
            kb = dy_ref.shape[2]
            dh = _dot_nt(dy_ref[0], w_ref[:, 0:kb])
            for s in range(1, dy_ref.shape[0]):
                dh = dh + _dot_nt(dy_ref[s], w_ref[:, s * kb:(s + 1) * kb])
        else:
            dh = _dot_nt(dy_ref[...], w_ref[...])
        g_v = g_ref[...]
        _, xh, r = _rms_fwd(x_ref[...], g_v)
        dx = _rms_bwd(dh, xh, r, g_v)
        if has_res:
            dx = dx + dres_ref[...]
        dx_ref[...] = dx
        if also_bf16:
            dxb_ref[...] = dx.astype(BF16)
        part = jnp.sum(dh * xh, axis=0, keepdims=True)

        @pl.when(i == 0)
        def _():
            gg_ref[...] = part

        @pl.when(i != 0)
        def _():
            gg_ref[...] += part

    tok = pl.BlockSpec((tb, d), lambda i: (i, 0))
    row = pl.BlockSpec((1, d), lambda i: (0, 0))
    if stacked:
        dy_spec = pl.BlockSpec((dy.shape[0], tb, dy.shape[2]), lambda i: (0, i, 0))
    else:
        dy_spec = pl.BlockSpec((tb, dy.shape[1]), lambda i: (i, 0))
    in_specs = [dy_spec, pl.BlockSpec(w.shape, lambda i: (0, 0)), tok, row]
    args = [dy, w, x, g]
    if has_res:
        in_specs.append(tok)
        args.append(dres)
    out_specs = [tok] + ([tok] if also_bf16 else []) + [row]
    out_shape = ([jax.ShapeDtypeStruct((t, d), F32)] + ([jax.ShapeDtypeStruct((t, d), BF16)] if also_bf16 else [])
                 + [jax.ShapeDtypeStruct((1, d), F32)])
    res, extra = _pcall(
        body, name=name, grid=(t // tb,), in_specs=in_specs, out_specs=out_specs, out_shape=out_shape,
        semantics=("arbitrary",), vmem_mb=56, rider=rider,
    )(*args)
    return res if rider is None else (res, extra)


def _matmul_tn(a, b, *, name, bm, bn, square_a=False, rider=None):
    t, m = a.shape
    stacked = b.ndim == 3
    n = b.shape[0] * bn if stacked else b.shape[1]

    def body(a_ref, b_ref, o_ref):
        av = a_ref[...]
        if square_a:
            av = av.astype(F32)
            av = (av * av).astype(BF16)
        o_ref[...] = _dot_tn(av, b_ref[...]).astype(BF16)

    res, extra = _pcall(
        body, name=name, grid=(m // bm, n // bn),
        in_specs=[pl.BlockSpec((t, bm), lambda i, j: (0, i)),
                  pl.BlockSpec((None, t, bn), lambda i, j: (j, 0, 0)) if stacked
                  else pl.BlockSpec((t, bn), lambda i, j: (0, j))],
        out_specs=[pl.BlockSpec((bm, bn), lambda i, j: (i, j))], out_shape=[jax.ShapeDtypeStruct((m, n), BF16)],
        semantics=("parallel", "parallel"), vmem_mb=56, rider=rider,
    )(a, b)
    return res[0] if rider is None else (res[0], extra)


N_RES = 16
SEG = 128
HALF = N_RES * SEG
TI = 16


def _x4(a):
    return a.reshape(a.shape[0] // HALF, N_RES, SEG, a.shape[1])


def _reorder(arrays, inverse, name, rider=None):
    t, c = arrays[0].shape
    n = len(arrays)
    n_i = SEG // TI
    natural = pl.BlockSpec((TI * N_RES, c), lambda s: (s, 0))
    major = pl.BlockSpec((1, N_RES, TI, c), lambda s: (s // n_i, 0, s % n_i, 0))

    def body(*refs):
        scr = refs[-1]
        for i_ref, o_ref in zip(refs[:n], refs[n:2 * n]):
            for cb in range(c // BLK):
                cols = slice(cb * BLK, (cb + 1) * BLK)
                slab = scr.at[cb]
                if inverse:
                    for r in range(N_RES):
                        slab[pl.ds(r, TI, stride=N_RES), :] = i_ref[0, r, :, cols]
                    o_ref[:, cols] = slab[...]
                else:
                    slab[...] = i_ref[:, cols]
                    for r in range(N_RES):
                        o_ref[0, r, :, cols] = slab[pl.ds(r, TI, stride=N_RES), :]

    shape4 = (t // HALF, N_RES, SEG, c)
    res, extra = _pcall(
        body, name=name, grid=(t // (TI * N_RES),),
        in_specs=[major if inverse else natural] * n, out_specs=[natural if inverse else major] * n,
        out_shape=[jax.ShapeDtypeStruct((t, c) if inverse else shape4, F32)] * n,
        scratch_shapes=[pltpu.VMEM((c // BLK, TI * N_RES, BLK), F32)],
        semantics=("parallel",), vmem_mb=32, rider=rider,
    )(*[_x4(a) if inverse else a for a in arrays])
    res = [r.reshape(t, c) for r in res]
    return res if rider is None else (res, extra)


_PATTERNS = ((1, 16, 8, SEG), (4, 4, 32, 4 * SEG), (16, 1, SEG, 0))
_FIRST = {1: 1, 4: 4, 16: 16}


def _group_rows(d, g):
    a = g >> 4
    if d == 16:
        base = a * HALF + (g & 15) * SEG
        prev = base - HALF
    elif d == 4:
        c = (g >> 2) & 3
        base = a * HALF + (g & 3) * SEG + c * 32
        prev = jnp.where(c > 0, base - 32, base - HALF + 96)
    else:
        c = g & 15
        base = a * HALF + c * 8
        prev = jnp.where(c > 0, base - 8, base - HALF + 120)
    return base, prev


def _load_rows(ref, base, n, rows, stride):
    parts = [ref[pl.ds(pl.multiple_of(base + j * stride, 8), rows), :] for j in range(n)]
    return parts[0] if n == 1 else jnp.concatenate(parts, axis=0)


def _store_rows(ref, base, val, n, rows, stride, add=False):
    for j in range(n):
        sl = pl.ds(pl.multiple_of(base + j * stride, 8), rows)
        piece = val[j * rows:(j + 1) * rows, :]
        if add:
            ref[sl, :] += piece
        else:
            ref[sl, :] = piece


def _band_bias(n, rows):
    shift = rows.bit_length() - 1
    lq = lax.broadcasted_iota(jnp.int32, (BLK, BLK), 0)
    lk = lax.broadcasted_iota(jnp.int32, (BLK, BLK), 1)
    iq = (lq & (rows - 1)) * n + (lq >> shift)
    ik = (lk & (rows - 1)) * n + (lk >> shift)
    zero = jnp.zeros((BLK, BLK), F32)
    return jnp.where(ik >= iq, zero, NEG_INF), jnp.where(ik <= iq, zero, NEG_INF)


def _set_bias(bias_scr, n, rows):
    prev_b, cur_b = _band_bias(n, rows)
    for half in range(2):
        bias_scr[half * BLK:(half + 1) * BLK, 0:BLK] = prev_b
        bias_scr[half * BLK:(half + 1) * BLK, BLK:2 * BLK] = cur_b


SCALE = 1.0 / math.sqrt(HEAD_DIM)


def _head_consts(value=1.0):
    lane_lo = lax.broadcasted_iota(jnp.int32, (BLK, BLK), 1) < HEAD_DIM
    return lane_lo, [jnp.where(lane_lo, value, 0.0).astype(BF16), jnp.where(lane_lo, 0.0, value).astype(BF16)]


def _stack_heads(v, head_mask):
    return jnp.concatenate([v * head_mask[0], v * head_mask[1]], axis=0)


def _unstack_heads(v2, lane_lo):
    return jnp.where(lane_lo, v2[:BLK], v2[BLK:])


def _rows_per_head(v, lane_lo):
    rolled = pltpu.roll(v, HEAD_DIM, axis=1)
    return jnp.concatenate([jnp.where(lane_lo, v, rolled), jnp.where(lane_lo, rolled, v)], axis=0)


WIDTH = 4


def _loop(lo, hi, fn, width=None):
    if width is None:
        def body(g, carry):
            fn(g)
            return carry

        if hi > lo:
            lax.fori_loop(lo, hi, body, 0)
        return
    while hi > lo:
        trips = (hi - lo) // width
        if trips:
            def body(i, carry, lo=lo, width=width):
                fn([lo + width * i + j for j in range(width)])
                return carry

            lax.fori_loop(0, trips, body, 0)
            lo += trips * width
        width = max(1, width // 2)


def _mix_weights(l1, l2, l3):
    mx = jnp.maximum(jnp.maximum(l1, l2), l3)
    e1, e2, e3 = jnp.exp(l1 - mx), jnp.exp(l2 - mx), jnp.exp(l3 - mx)
    inv = 1.0 / (e1 + e2 + e3)
    return e1 * inv, e2 * inv, e3 * inv


def _attention_fwd(qkv, rider=None):
    t = qkv.shape[0]
    groups = 16 * (t // HALF)

    def body(q_ref, k_ref, v_ref, attn_ref, l1_ref, l2_ref, l3_ref, o_scr, bias_scr):
        lane_lo, q_mask = _head_consts(SCALE)
        l_refs = (l1_ref, l2_ref, l3_ref)
        for p, (d, n, rows, stride) in enumerate(_PATTERNS):
            _set_bias(bias_scr, n, rows)
            o_p, l_p = o_scr.at[p], l_refs[p]

            def block(gs, has_prev):
                at = [_group_rows(d, g) for g in gs]

                def load(ref, b):
                    return _load_rows(ref, b, n, rows, stride).astype(BF16)

                q2 = [_stack_heads(load(q_ref, b), q_mask) for b, _ in at]
                k2 = [load(k_ref, b) for b, _ in at]
                v2 = [load(v_ref, b) for b, _ in at]
                if has_prev:
                    k2 = [jnp.concatenate([load(k_ref, pv), k], axis=0) for (_, pv), k in zip(at, k2)]
                    v2 = [jnp.concatenate([load(v_ref, pv), v], axis=0) for (_, pv), v in zip(at, v2)]
                s = [_dot_nt(q, k) for q, k in zip(q2, k2)]
                s = [x + (bias_scr[...] if has_prev else bias_scr[:, BLK:2 * BLK]) for x in s]
                mx = [jnp.max(x, axis=1, keepdims=True) for x in s]
                e = [jnp.exp(x - m) for x, m in zip(s, mx)]
                den = [jnp.sum(x, axis=1, keepdims=True) for x in e]
                o2 = [_dot(x.astype(BF16), v) * (1.0 / dn) for x, v, dn in zip(e, v2, den)]
                lse2 = [jnp.broadcast_to(m + jnp.log(dn), (2 * BLK, BLK)) for m, dn in zip(mx, den)]
                for (b, _), o, l in zip(at, o2, lse2):
                    _store_rows(o_p, b, _unstack_heads(o, lane_lo), n, rows, stride)
                    _store_rows(l_p, b, _unstack_heads(l, lane_lo), n, rows, stride)

            _loop(0, _FIRST[d], lambda gs: block(gs, False), width=WIDTH)
            _loop(_FIRST[d], groups, lambda gs: block(gs, True), width=WIDTH)

        def mix(i):
            sl = pl.ds(pl.multiple_of(i * 256, 256), 256)
            w = _mix_weights(l1_ref[sl, :], l2_ref[sl, :], l3_ref[sl, :])
            attn_ref[sl, :] = w[0] * o_scr[0, sl, :] + w[1] * o_scr[1, sl, :] + w[2] * o_scr[2, sl, :]

        _loop(0, t // 256, mix)

    def col(c0):
        return pl.BlockSpec((t, BLK), lambda hp: (0, c0 + hp))

    res, extra = _pcall(
        body, name="attention_fwd", grid=(4,), in_specs=[col(0), col(4), col(8)], out_specs=[col(0)] * 4,
        out_shape=[jax.ShapeDtypeStruct((t, 512), F32)] * 4,
        scratch_shapes=[pltpu.VMEM((3, t, BLK), F32), pltpu.VMEM((2 * BLK, 2 * BLK), F32)],
        semantics=("parallel",), vmem_mb=48, rider=rider,
    )(qkv, qkv, qkv)
    return res if rider is None else (res, extra)


def _attention_bwd(qkv, dattn, dsum, lses, dproj, rider=None):
    t = qkv.shape[0]
    groups = 16 * (t // HALF)

    def body(q_ref, k_ref, v_ref, da_ref, ds_ref, l1_ref, l2_ref, l3_ref, kept_ref, out_ref, acc, bias_scr):
        del kept_ref
        lane_lo, head_mask = _head_consts()
        q_mask = _head_consts(SCALE)[1]
        l_refs = (l1_ref, l2_ref, l3_ref)

        def clear(i):
            sl = pl.ds(pl.multiple_of(i * 512, 512), 512)
            for s in range(3):
                acc[s, sl, :] = jnp.zeros((512, BLK), F32)

        _loop(0, t // 512, clear)
        dq_acc, dk_acc, dv_acc = acc.at[0], acc.at[1], acc.at[2]
        for p, (d, n, rows, stride) in enumerate(_PATTERNS):
            _set_bias(bias_scr, n, rows)

            def block(gs, has_prev):
                at = [_group_rows(d, g) for g in gs]

                def load(ref, b):
                    return _load_rows(ref, b, n, rows, stride)

                def put(ref, b, val):
                    _store_rows(ref, b, val, n, rows, stride, add=True)

                def wide(x):
                    return jnp.concatenate([x, x], axis=1) if has_prev else x

                lse = [[load(ref, b) for ref in l_refs] for b, _ in at]
                w = [_mix_weights(*ls)[p] for ls in lse]
                do2 = [_stack_heads((wg * load(da_ref, b)).astype(BF16), head_mask) for wg, (b, _) in zip(w, at)]
                dl2 = [wide(_rows_per_head(wg * load(ds_ref, b), lane_lo)) for wg, (b, _) in zip(w, at)]
                lse2 = [wide(_rows_per_head(ls[p], lane_lo)) for ls in lse]
                q2 = [_stack_heads(load(q_ref, b).astype(BF16), q_mask) for b, _ in at]
                k2 = [load(k_ref, b).astype(BF16) for b, _ in at]
                v2 = [load(v_ref, b).astype(BF16) for b, _ in at]
                if has_prev:
                    k2 = [jnp.concatenate([load(k_ref, pv).astype(BF16), k], axis=0) for (_, pv), k in zip(at, k2)]
                    v2 = [jnp.concatenate([load(v_ref, pv).astype(BF16), v], axis=0) for (_, pv), v in zip(at, v2)]
                s = [_dot_nt(q, k) for q, k in zip(q2, k2)]
                dp = [_dot_nt(do, v) for do, v in zip(do2, v2)]
                pr = [jnp.exp(x + (bias_scr[...] if has_prev else bias_scr[:, BLK:2 * BLK]) - l)
                      for x, l in zip(s, lse2)]
                ds = [(pg * (x - dl)).astype(BF16) for pg, x, dl in zip(pr, dp, dl2)]
                dq2 = [_dot(x, k) * SCALE for x, k in zip(ds, k2)]
                dk2 = [_dot_tn(x, q) for x, q in zip(ds, q2)]
                dv2 = [_dot_tn(pg.astype(BF16), do) for pg, do in zip(pr, do2)]
                for (b, pv), dq, dk, dv in zip(at, dq2, dk2, dv2):
                    put(dq_acc, b, _unstack_heads(dq, lane_lo))
                    if has_prev:
                        put(dk_acc, pv, dk[:BLK])
                        put(dv_acc, pv, dv[:BLK])
                        put(dk_acc, b, dk[BLK:])
                        put(dv_acc, b, dv[BLK:])
                    else:
                        put(dk_acc, b, dk)
                        put(dv_acc, b, dv)

            _loop(0, _FIRST[d], lambda gs: block(gs, False), width=WIDTH)
            _loop(_FIRST[d], groups, lambda gs: block(gs, True), width=WIDTH)

        def emit(i):
            sl = pl.ds(pl.multiple_of(i * 512, 512), 512)
            for s in range(3):
                out_ref[s, sl, :] = acc[s, sl, :].astype(BF16)

        _loop(0, t // 512, emit)

    def col(c0):
        return pl.BlockSpec((t, BLK), lambda hp: (0, c0 + hp))

    res, extra = _pcall(
        body, name="attention_bwd", grid=(4,),
        in_specs=[col(0), col(4), col(8)] + [col(0)] * 5 + [ANY],
        out_specs=[pl.BlockSpec((3, t, BLK), lambda hp: (0, 0, hp))],
        out_shape=[jax.ShapeDtypeStruct(dproj.shape, BF16)],
        scratch_shapes=[pltpu.VMEM((3, t, BLK), F32), pltpu.VMEM((2 * BLK, 2 * BLK), F32)],
        semantics=("parallel",), vmem_mb=56, rider=rider, aliases={8: 0},
    )(qkv, qkv, qkv, dattn, dsum, *lses, dproj)
    return res[0] if rider is None else (res[0], extra)


def _order_specs(t):
    n_i = SEG // TI
    nblk = (t // HALF) * n_i
    per = TI // 8

    def main(c, col=0):
        return pl.BlockSpec((1, N_RES, TI, c), lambda s: (s // n_i, 0, s % n_i, col))

    def before(c, col=0):
        return pl.BlockSpec((1, 2, 8, c), lambda s: (jnp.maximum(s - 1, 0) // n_i, N_RES // 2 - 1,
                                                     (jnp.maximum(s - 1, 0) % n_i) * per + per - 1, col))

    def after(c, col=0):
        return pl.BlockSpec((1, 2, 8, c), lambda s: (jnp.minimum(s + 1, nblk - 1) // n_i, 0,
                                                     (jnp.minimum(s + 1, nblk - 1) % n_i) * per, col))

    return nblk, main, before, after


def _shift_in(v, row_in, up):
    rows = v.shape[0]
    idx = lax.broadcasted_iota(jnp.int32, v.shape, 0)
    fill = jnp.broadcast_to(row_in, v.shape)
    if up:
        return jnp.where(idx == rows - 1, fill, pltpu.roll(v, rows - 1, axis=0))
    return jnp.where(idx == 0, fill, pltpu.roll(v, 1, axis=0))


def _taps_behind(u, before):
    s15 = _shift_in(u[N_RES - 1], before[1, 7:8, :], up=False)
    s14 = _shift_in(u[N_RES - 2], before[0, 7:8, :], up=False)
    m1 = jnp.concatenate([s15[None], u[:N_RES - 1]], axis=0)
    m2 = jnp.concatenate([s14[None], s15[None], u[:N_RES - 2]], axis=0)
    return m1, m2


def _taps_ahead(u, after):
    t0 = _shift_in(u[0], after[0, 0:1, :], up=True)
    t1 = _shift_in(u[1], after[1, 0:1, :], up=True)
    p1 = jnp.concatenate([u[1:], t0[None]], axis=0)
    p2 = jnp.concatenate([u[2:], t0[None], t1[None]], axis=0)
    return p1, p2


def _conv_fwd(gates, before, first, cw):
    bg, cg, xc = gates[..., 0:512], gates[..., 512:1024], gates[..., 1024:1536]
    u = cg * xc
    ub = before[..., 512:1024] * before[..., 1024:1536]
    ub = jnp.where(first, jnp.zeros_like(ub), ub)
    m1, m2 = _taps_behind(u, ub)
    conv = m2 * cw[0:1, :] + m1 * cw[1:2, :] + u * cw[2:3, :]
    return bg, u, m1, m2, conv


def _sum_tokens(v):
    return jnp.sum(jnp.sum(v, axis=0), axis=0, keepdims=True)


def _mixer_fwd(x, attn, gates, cw, g_a, g_c, w_out):
    t, d = x.shape
    nblk, main, before, _ = _order_specs(t)
    rows = N_RES * TI

    def body(x_ref, at_ref, gt_ref, gb_ref, cw_ref, ga_ref, gc_ref, wa_ref, wb_ref, x1_ref, mg_ref):
        an = _rms_fwd(at_ref[0], ga_ref[...])[0].astype(BF16)
        bg, _, _, _, conv = _conv_fwd(gt_ref[0], gb_ref[0], pl.program_id(0) == 0, cw_ref[...])
        cn = _rms_fwd(bg * conv, gc_ref[...])[0].astype(BF16)
        mg_ref[0, :, :, 0:512] = an
        mg_ref[0, :, :, 512:1024] = cn
        y = _dot(an.reshape(rows, 512), wa_ref[...]) + _dot(cn.reshape(rows, 512), wb_ref[...])
        x1_ref[0] = x_ref[0] + y.reshape(N_RES, TI, d)

    const = lambda r, c, i0=0: pl.BlockSpec((r, c), lambda s: (i0, 0))
    x1, merged = pl.pallas_call(
        body, name="mixer_fwd", grid=(nblk,),
        in_specs=[main(d), main(512), main(1536, 1), before(1536, 1), const(3, 512), const(1, 512), const(1, 512),
                  const(512, d), const(512, d, 1)],
        out_specs=[main(d), main(d)],
        out_shape=[jax.ShapeDtypeStruct(_x4(x).shape, F32), jax.ShapeDtypeStruct(_x4(x).shape, BF16)],
        compiler_params=_params(("parallel",), 48),
    )(_x4(x), _x4(attn), _x4(gates), _x4(gates), cw, g_a, g_c, w_out, w_out)
    return x1.reshape(t, d), merged.reshape(t, d)


def _mixer_bwd(dx1, attn, gates, cw, g_a, g_c, w_out, head_sum, rider=None):
    t, d = dx1.shape
    nblk, main, before, _ = _order_specs(t)
    rows = N_RES * TI

    def body(dx_ref, at_ref, gt_ref, gb_ref, cw_ref, ga_ref, gc_ref, wa_ref, wb_ref, hs_ref,
             da_ref, dsum_ref, dy_ref, gga_ref, ggc_ref):
        s = pl.program_id(0)
        dxb = dx_ref[0].reshape(rows, d).astype(BF16)
        dma = _dot_nt(dxb, wa_ref[...]).reshape(N_RES, TI, 512)
        dmc = _dot_nt(dxb, wb_ref[...]).reshape(N_RES, TI, 512)
        attn_v, g_av = at_ref[0], ga_ref[...]
        _, ah, ra = _rms_fwd(attn_v, g_av)
        dattn = _rms_bwd(dma, ah, ra, g_av)
        da_ref[0] = dattn
        z = (dattn * attn_v).reshape(rows, 512)
        hs = hs_ref[...]
        z1 = z.astype(BF16)
        z2 = (z - z1.astype(F32)).astype(BF16)
        z3 = (z - z1.astype(F32) - z2.astype(F32)).astype(BF16)
        dsum_ref[0] = (_dot(z1, hs) + _dot(z2, hs) + _dot(z3, hs)).reshape(N_RES, TI, 512)
        bg, _, _, _, conv = _conv_fwd(gt_ref[0], gb_ref[0], s == 0, cw_ref[...])
        g_cv = gc_ref[...]
        _, yh, rc = _rms_fwd(bg * conv, g_cv)
        dy_ref[0] = _rms_bwd(dmc, yh, rc, g_cv)
        pa, pc = _sum_tokens(dma * ah), _sum_tokens(dmc * yh)

        @pl.when(s == 0)
        def _():
            gga_ref[...] = pa
            ggc_ref[...] = pc

        @pl.when(s != 0)
        def _():
            gga_ref[...] += pa
            ggc_ref[...] += pc

    const = lambda r, c, i0=0: pl.BlockSpec((r, c), lambda s: (i0, 0))
    shape4 = _x4(attn).shape
    res, extra = _pcall(
        body, name="mixer_bwd", grid=(nblk,),
        in_specs=[main(d), main(512), main(1536, 1), before(1536, 1), const(3, 512), const(1, 512), const(1, 512),
                  const(512, d), const(512, d, 1), const(512, 512)],
        out_specs=[main(512)] * 3 + [const(1, 512), const(1, 512)],
        out_shape=[jax.ShapeDtypeStruct(shape4, F32)] * 3 + [jax.ShapeDtypeStruct((1, 512), F32)] * 2,
        semantics=("arbitrary",), vmem_mb=48, rider=rider,
    )(_x4(dx1), _x4(attn), _x4(gates), _x4(gates), cw, g_a, g_c, w_out, w_out, head_sum)
    res = [r.reshape(t, 512) for r in res[:3]] + res[3:]
    return res if rider is None else (res, extra)


def _conv_bwd(dy, gates, cw):
    t = dy.shape[0]
    nblk, main, before, after = _order_specs(t)
    n_i = SEG // TI

    def body(dy_ref, dya_ref, gt_ref, gb_ref, ga_ref, cw_ref, dp_ref, gcw_ref):
        s = pl.program_id(0)
        cw_v, gates_v = cw_ref[...], gt_ref[0]
        bg, u, m1, m2, conv = _conv_fwd(gates_v, gb_ref[0], s == 0, cw_v)
        dy_v = dy_ref[0]
        dconv = dy_v * bg
        dca = dya_ref[0] * ga_ref[0][..., 0:512]
        dca = jnp.where(s == nblk - 1, jnp.zeros_like(dca), dca)
        p1, p2 = _taps_ahead(dconv, dca)
        du = dconv * cw_v[2:3, :] + p1 * cw_v[1:2, :] + p2 * cw_v[0:1, :]
        dp_ref[0, 0] = (dy_v * conv).astype(BF16)
        dp_ref[1, 0] = (du * gates_v[..., 1024:1536]).astype(BF16)
        dp_ref[2, 0] = (du * gates_v[..., 512:1024]).astype(BF16)
        parts = [_sum_tokens(dconv * m2), _sum_tokens(dconv * m1), _sum_tokens(dconv * u)]

        @pl.when(s == 0)
        def _():
            gcw_ref[...] = jnp.zeros_like(gcw_ref)

        for tap in range(3):
            gcw_ref[tap:tap + 1, :] += parts[tap]

    dproj, gcw = pl.pallas_call(
        body, name="conv_bwd", grid=(nblk,),
        in_specs=[main(512), after(512), main(1536, 1), before(1536, 1), after(1536, 1),
                  pl.BlockSpec((3, 512), lambda s: (0, 0))],
        out_specs=[pl.BlockSpec((3, 1, N_RES, TI, 512), lambda s: (1, s // n_i, 0, s % n_i, 0)),
                   pl.BlockSpec((8, 512), lambda s: (0, 0))],
        out_shape=[jax.ShapeDtypeStruct((6, t // HALF, N_RES, SEG, 512), BF16), jax.ShapeDtypeStruct((8, 512), F32)],
        compiler_params=_params(("arbitrary",), 40),
    )(_x4(dy), _x4(dy), _x4(gates), _x4(gates), _x4(gates), cw)
    return dproj.reshape(6, t, 512), gcw


def _xattn_fwd(x1, g, w_q, kv, w_o, *, tb):
    t, d = x1.shape
    hd = d // N_MEM_HEADS
    m = kv.shape[0]

    def body(x_ref, g_ref, wq_ref, k_ref, v_ref, wo_ref, x2_ref, h_ref, q_ref, o_ref):
        xv = x_ref[...]
        h = _rms_fwd(xv, g_ref[...])[0].astype(BF16)
        h_ref[...] = h
        q = _dot(h, wq_ref[...]).astype(BF16)
        q_ref[...] = q
        for hh in range(N_MEM_HEADS):
            sl = slice(hh * hd, (hh + 1) * hd)
            s = _dot_nt(q[:, sl], k_ref[:, sl]) * (1.0 / 16.0)
            e = jnp.exp(s - jnp.max(s, axis=1, keepdims=True))
            p = e / jnp.sum(e, axis=1, keepdims=True)
            o_ref[:, sl] = _dot(p.astype(BF16), v_ref[:, sl]).astype(BF16)
        x2_ref[...] = xv + _dot(o_ref[...], wo_ref[...])

    tok = pl.BlockSpec((tb, d), lambda i: (i, 0))
    full = pl.BlockSpec((d, d), lambda i: (0, 0))
    return pl.pallas_call(
        body, name="xattn_fwd", grid=(t // tb,),
        in_specs=[tok, pl.BlockSpec((1, d), lambda i: (0, 0)), full,
                  pl.BlockSpec((m, d), lambda i: (0, 0)), pl.BlockSpec((m, d), lambda i: (0, 1)), full],
        out_specs=[tok] * 4,
        out_shape=[jax.ShapeDtypeStruct((t, d), F32)] + [jax.ShapeDtypeStruct((t, d), BF16)] * 3,
        compiler_params=_params(("parallel",), 48),
    )(x1, g, w_q, kv, kv, w_o)


def _xattn_bwd(dx2, x1, g, q, w_q, kv, w_o, *, tb, rider=None):
    t, d = x1.shape
    hd = d // N_MEM_HEADS
    m = kv.shape[0]

    def body(dx2_ref, x_ref, g_ref, q_ref, wq_ref, k_ref, v_ref, wo_ref,
             dx1_ref, dx1b_ref, dq_ref, dk_ref, dv_ref, gg_ref):
        i = pl.program_id(0)

        @pl.when(i == 0)
        def _():
            dk_ref[...] = jnp.zeros_like(dk_ref)
            dv_ref[...] = jnp.zeros_like(dv_ref)

        dx2 = dx2_ref[...]
        do = _dot_nt(dx2.astype(BF16), wo_ref[...]).astype(BF16)
        for hh in range(N_MEM_HEADS):
            sl = slice(hh * hd, (hh + 1) * hd)
            qh, kh, vh, doh = q_ref[:, sl], k_ref[:, sl], v_ref[:, sl], do[:, sl]
            s = _dot_nt(qh, kh) * (1.0 / 16.0)
            e = jnp.exp(s - jnp.max(s, axis=1, keepdims=True))
            p = e / jnp.sum(e, axis=1, keepdims=True)
            dp = _dot_nt(doh, vh)
            ds = (p * (dp - jnp.sum(dp * p, axis=1, keepdims=True)) * (1.0 / 16.0)).astype(BF16)
            dq_ref[:, sl] = _dot(ds, kh).astype(BF16)
            dk_ref[:, sl] += _dot_tn(ds, qh)
            dv_ref[:, sl] += _dot_tn(p.astype(BF16), doh)
        dh = _dot_nt(dq_ref[...], wq_ref[...])
        g_v = g_ref[...]
        _, xh, r = _rms_fwd(x_ref[...], g_v)
        dx1 = dx2 + _rms_bwd(dh, xh, r, g_v)
        dx1_ref[...] = dx1
        dx1b_ref[...] = dx1.astype(BF16)
        part = jnp.sum(dh * xh, axis=0, keepdims=True)

        @pl.when(i == 0)
        def _():
            gg_ref[...] = part

        @pl.when(i != 0)
        def _():
            gg_ref[...] += part

    tok = pl.BlockSpec((tb, d), lambda i: (i, 0))
    full = pl.BlockSpec((d, d), lambda i: (0, 0))
    acc = pl.BlockSpec((m, d), lambda i: (0, 0))
    res, extra = _pcall(
        body, name="xattn_bwd", grid=(t // tb,),
        in_specs=[tok, tok, pl.BlockSpec((1, d), lambda i: (0, 0)), tok, full,
                  pl.BlockSpec((m, d), lambda i: (0, 0)), pl.BlockSpec((m, d), lambda i: (0, 1)), full],
        out_specs=[tok, tok, tok, acc, acc, pl.BlockSpec((1, d), lambda i: (0, 0))],
        out_shape=[jax.ShapeDtypeStruct((t, d), F32), jax.ShapeDtypeStruct((t, d), BF16),
                   jax.ShapeDtypeStruct((t, d), BF16),
                   jax.ShapeDtypeStruct((m, d), F32), jax.ShapeDtypeStruct((m, d), F32),
                   jax.ShapeDtypeStruct((1, d), F32)],
        semantics=("arbitrary",), vmem_mb=48, rider=rider,
    )(dx2, x1, g, q, w_q, kv, kv, w_o)
    return res if rider is None else (res, extra)


def _mlp_down_loss(a, w_down, x2, tgt, g, *, tb):
    t, d = x2.shape
    f = a.shape[1]

    def body(a_ref, w_ref, x_ref, t_ref, g_ref, dx_ref, dxb_ref, loss_ref, gg_ref):
        i = pl.program_id(0)
        av = a_ref[...].astype(F32)
        x3 = x_ref[...] + _dot((av * av).astype(BF16), w_ref[...])
        g_v = g_ref[...]
        out, xh, r = _rms_fwd(x3, g_v)
        err = out - t_ref[...]
        dout = err * (1.0 / d)
        dx = _rms_bwd(dout, xh, r, g_v)
        dx_ref[...] = dx
        dxb_ref[...] = dx.astype(BF16)
        part = jnp.sum(dout * xh, axis=0, keepdims=True)
        lpart = 0.5 * jnp.sum(jnp.mean(err * err, axis=-1, keepdims=True), axis=0, keepdims=True)
        lpart = jnp.broadcast_to(lpart, loss_ref.shape)

        @pl.when(i == 0)
        def _():
            gg_ref[...] = part
            loss_ref[...] = lpart

        @pl.when(i != 0)
        def _():
            gg_ref[...] += part
            loss_ref[...] += lpart

    tok = pl.BlockSpec((tb, d), lambda i: (i, 0))
    return pl.pallas_call(
        body, name="mlp_down_loss", grid=(t // tb,),
        in_specs=[pl.BlockSpec((tb, f), lambda i: (i, 0)), pl.BlockSpec((f, d), lambda i: (0, 0)), tok, tok,
                  pl.BlockSpec((1, d), lambda i: (0, 0))],
        out_specs=[tok, tok, pl.BlockSpec((8, 128), lambda i: (0, 0)), pl.BlockSpec((1, d), lambda i: (0, 0))],
        out_shape=[jax.ShapeDtypeStruct((t, d), F32), jax.ShapeDtypeStruct((t, d), BF16),
                   jax.ShapeDtypeStruct((8, 128), F32), jax.ShapeDtypeStruct((1, d), F32)],
        compiler_params=_params(("arbitrary",), 56),
    )(a, w_down, x2, tgt, g)


def _mlp_dpre(dx3, w_down, a, *, tb, bn):
    t, d = dx3.shape
    f = a.shape[1]

    def body(dx_ref, w_ref, a_ref, o_ref):
        o_ref[...] = (2.0 * a_ref[...].astype(F32) * _dot_nt(dx_ref[...], w_ref[...])).astype(BF16)

    return pl.pallas_call(
        body, name="mlp_dpre", grid=(t // tb, f // bn),
        in_specs=[pl.BlockSpec((tb, d), lambda i, j: (i, 0)), pl.BlockSpec((bn, d), lambda i, j: (j, 0)),
                  pl.BlockSpec((tb, bn), lambda i, j: (i, j))],
        out_specs=pl.BlockSpec((tb, bn), lambda i, j: (i, j)),
        out_shape=jax.ShapeDtypeStruct((t, f), BF16),
        compiler_params=_params(("parallel", "arbitrary"), 48),
    )(dx3, w_down, a)


def _adamw(gsum, w, m, v):
    m_new = ADAM_B1 * m + (1.0 - ADAM_B1) * gsum
    v_new = ADAM_B2 * v + (1.0 - ADAM_B2) * (gsum * gsum)
    m_hat = m_new / (1.0 - ADAM_B1 ** ADAM_STEP)
    v_hat = v_new / (1.0 - ADAM_B2 ** ADAM_STEP)
    delta = -ADAM_LR * (m_hat / (jnp.sqrt(v_hat) + ADAM_EPS) + ADAM_WD * w)
    return delta, m_new, v_new


def _sum_adamw(parts, w, m, v, *, name, tr):
    r, c = w.shape

    def body(p_ref, w_ref, m_ref, v_ref, g_ref, d_ref, mo_ref, vo_ref):
        g = p_ref[0].astype(F32)
        for k in range(1, N_DEV):
            g = g + p_ref[k].astype(F32)
        g_ref[...] = g
        d_ref[...], mo_ref[...], vo_ref[...] = _adamw(g, w_ref[...], m_ref[...], v_ref[...])

    blk = pl.BlockSpec((tr, c), lambda i: (i, 0))
    return pl.pallas_call(
        body, name=name, grid=(r // tr,),
        in_specs=[pl.BlockSpec((N_DEV, tr, c), lambda i: (0, i, 0)), blk, blk, blk],
        out_specs=[blk] * 4, out_shape=[jax.ShapeDtypeStruct((r, c), F32)] * 4,
        compiler_params=_params(("parallel",), 40),
    )(parts, w, m, v)


def _sum_small(parts):
    _, r, c = parts.shape

    def body(p_ref, o_ref):
        s = p_ref[0]
        for k in range(1, N_DEV):
            s = s + p_ref[k]
        o_ref[...] = s

    return pl.pallas_call(body, name="sum_small", out_shape=jax.ShapeDtypeStruct((r, c), F32))(parts)


def _adamw_small(g, w, m, v):
    def body(g_ref, w_ref, m_ref, v_ref, d_ref, mo_ref, vo_ref):
        d_ref[...], mo_ref[...], vo_ref[...] = _adamw(g_ref[...], w_ref[...], m_ref[...], v_ref[...])

    return pl.pallas_call(body, name="adamw_small", out_shape=[jax.ShapeDtypeStruct(g.shape, F32)] * 3)(g, w, m, v)


def _head_sum_matrix():
    r = lax.broadcasted_iota(jnp.int32, (512, 512), 0) // HEAD_DIM
    c = lax.broadcasted_iota(jnp.int32, (512, 512), 1) // HEAD_DIM
    return (r == c).astype(BF16)


_SHARD_AXIS = dict(w_in=1, w_out=0, w_q=0, w_kv=1, w_o=0, w_up=1, w_down=0, conv_w=None, small=None)


class _Weights:
    def __init__(self, full, shards=None):
        self.full = dict(full)
        self.shards = shards

    def rider(self, names):
        if self.shards is None:
            return None
        return _Gather([self.shards[n] for n in names], [_SHARD_AXIS[n] for n in names])

    def arrived(self, names, gathered):
        if gathered is not None:
            for n, g in zip(names, gathered):
                self.full[n] = g.transpose(1, 0, 2).reshape(g.shape[1], -1) if n == "conv_w" else g

    def __getitem__(self, name):
        return self.full[name]


class _Grads:
    def __init__(self, distributed):
        self.distributed = distributed
        self.local = {}
        self.received = {}

    def add(self, name, g):
        self.local[name] = g

    def rider(self, names, pieces=None):
        if not self.distributed:
            return None
        into = [self.received.get(n) for n in names]
        return _Exchange([self.local[n] for n in names], [_SHARD_AXIS[n] for n in names], pieces, into)

    def exchange_now(self, names, pieces, name):
        if self.distributed:
            self.arrived(names, _comm_call(self.rider(names, pieces), name))

    def arrived(self, names, received):
        if received is not None:
            for n, r in zip(names, received):
                self.received[n] = r


def _ride(fn, *args, rider=None, **kw):
    if rider is None:
        return fn(*args, **kw), None
    return fn(*args, rider=rider, **kw)


def _local_step(x, mem, tgt, gains, weights, grads):
    names = ["w_in", "conv_w"]
    (x, tgt), got = _ride(_reorder, [x, tgt], False, "reorder_in", rider=weights.rider(names))
    weights.arrived(names, got)
    w_in, cw = weights["w_in"], weights["conv_w"]

    names = ["w_out", "w_kv"]
    (proj, h1), got = _ride(_norm_matmul, x, gains["g_mix"], w_in, name="proj", out_dtype=F32, tb=1024, bn=768,
                            save_h=True, rider=weights.rider(names))
    weights.arrived(names, got)
    names = ["w_q", "w_o", "w_up"]
    (attn, *lses), got = _ride(_attention_fwd, proj, rider=weights.rider(names))
    weights.arrived(names, got)
    x1, merged = _mixer_fwd(x, attn, proj, cw, gains["g_attn_out"], gains["g_conv_out"], weights["w_out"])
    kv, mem_n = _norm_matmul(mem, gains["g_mem"], weights["w_kv"], name="mem_kv", out_dtype=BF16, tb=mem.shape[0],
                             bn=1024, save_h=True)
    x2, h2, qm, om = _xattn_fwd(x1, gains["g_xattn"], weights["w_q"], kv, weights["w_o"], tb=256)
    w_up = weights["w_up"]
    (a, h3), got = _ride(_norm_matmul, x2, gains["g_mlp"], w_up, name="mlp_up", out_dtype=BF16, tb=1024, bn=1024,
                         relu=True, save_h=True, rider=weights.rider(["w_down"]))
    weights.arrived(["w_down"], got)
    w_down = weights["w_down"]
    dx3, dx3b, loss_blk, gg_final = _mlp_down_loss(a, w_down, x2, tgt, gains["g_final"], tb=256)

    dpre = _mlp_dpre(dx3b, w_down, a, tb=1024, bn=1024)
    grads.add("w_down", _matmul_tn(a, dx3b, name="grad_w_down", bm=512, bn=1024, square_a=True))
    gw_up, got = _ride(_matmul_tn, h3, dpre, name="grad_w_up", bm=1024, bn=512,
                       rider=grads.rider(["w_down"], [(0, 2)]))
    grads.arrived(["w_down"], got)
    grads.add("w_up", gw_up)
    (dx2, dx2b, gg_mlp), got = _ride(_matmul_nt_normbwd, dpre, w_up, x2, gains["g_mlp"], dx3, name="mlp_dx", tb=512,
                                     also_bf16=True, rider=grads.rider(["w_down"], [(1, 2)]))
    grads.arrived(["w_down"], got)

    grads.add("w_o", _matmul_tn(om, dx2b, name="grad_w_o", bm=1024, bn=512))
    names = ["w_up", "w_o"]
    (dx1, dx1b, dqm, dk, dv, gg_xattn), got = _ride(_xattn_bwd, dx2, x1, gains["g_xattn"], qm, weights["w_q"], kv,
                                                    weights["w_o"], tb=256,
                                                    rider=grads.rider(names, [(0, 2), (0, 1)]))
    grads.arrived(names, got)
    grads.add("w_q", _matmul_tn(h2, dqm, name="grad_w_q", bm=1024, bn=512))
    dkv = jnp.concatenate([dk, dv], axis=1).astype(BF16)
    grads.add("w_kv", _matmul_tn(mem_n, dkv, name="grad_w_kv", bm=1024, bn=1024))
    _, gg_mem = _matmul_nt_normbwd(dkv, weights["w_kv"], mem, gains["g_mem"], None, name="mem_dx", tb=mem.shape[0])

    grads.add("w_out", _matmul_tn(merged, dx1b, name="grad_w_out", bm=1024, bn=512))
    (dattn, dsum, dy, gg_attn, gg_conv), got = _ride(
        _mixer_bwd, dx1, attn, proj, cw, gains["g_attn_out"], gains["g_conv_out"], weights["w_out"],
        _head_sum_matrix(), rider=grads.rider(["w_up"], [(1, 2)]))
    grads.arrived(["w_up"], got)
    dproj, gcw = _conv_bwd(dy, proj, cw)
    names = ["w_q", "w_kv", "w_out"]
    dproj, got = _ride(_attention_bwd, proj, dattn, dsum, lses, dproj, rider=grads.rider(names))
    grads.arrived(names, got)
    grads.add("w_in", _matmul_tn(h1, dproj, name="grad_w_in", bm=1024, bn=512))
    (grad_x, gg_mix), got = _ride(_matmul_nt_normbwd, dproj, w_in, x, gains["g_mix"], dx1, name="mixer_dx",
                                  tb=512, rider=grads.rider(["w_in"], [(0, 2)]))
    grads.arrived(["w_in"], got)

    rows = [gg_mix, gg_xattn, gg_mem, gg_mlp, gg_final, jnp.concatenate([gg_attn, gg_conv], axis=1),
            jnp.pad(gcw[0:3], ((0, 0), (0, 512))), jnp.pad(loss_blk[0:1, 0:1], ((0, 6), (0, 1023)))]
    grads.add("small", jnp.concatenate(rows, axis=0))
    (grad_x,) = _reorder([grad_x], True, "reorder_out")
    grads.exchange_now(["w_in", "small"], [(1, 2), (0, 1)], "exchange_last")
    return grad_x


_BIG = ("w_in", "w_out", "w_q", "w_kv", "w_o", "w_up", "w_down")
_GAIN_ROWS = ("g_mix", "g_xattn", "g_mem", "g_mlp", "g_final")


def _pack_small(vals, conv):
    rows = [vals[k].reshape(1, -1) for k in _GAIN_ROWS]
    rows.append(jnp.concatenate([vals["g_attn_out"].reshape(1, -1), vals["g_conv_out"].reshape(1, -1)], axis=1))
    flat = conv.reshape(1, -1)
    rows.append(jnp.pad(flat, ((0, 0), (0, 1024 - flat.shape[1]))))
    rows.append(jnp.zeros((1, 1024), F32))
    return jnp.concatenate(rows, axis=0)


def kernel(x, mem, g_mix, w_in, conv_w, g_attn_out, g_conv_out, w_out, g_xattn, g_mem, w_q_mem, w_kv_mem, w_o_mem, g_mlp, w_up, w_down, g_final, loss_target, m_g_mix, m_w_in, m_conv_w, m_g_attn_out, m_g_conv_out, m_w_out, m_g_xattn, m_g_mem, m_w_q_mem, m_w_kv_mem, m_w_o_mem, m_g_mlp, m_w_up, m_w_down, m_g_final, v_g_mix, v_w_in, v_conv_w, v_g_attn_out, v_g_conv_out, v_w_out, v_g_xattn, v_g_mem, v_w_q_mem, v_w_kv_mem, v_w_o_mem, v_g_mlp, v_w_up, v_w_down, v_g_final):
    d = x.shape[-1]
    me = 4 * lax.axis_index("x") + 2 * lax.axis_index("y") + lax.axis_index("c")
    w_shards = dict(w_in=w_in, w_out=w_out, w_q=w_q_mem, w_kv=w_kv_mem, w_o=w_o_mem, w_up=w_up, w_down=w_down)
    m_shards = dict(w_in=m_w_in, w_out=m_w_out, w_q=m_w_q_mem, w_kv=m_w_kv_mem, w_o=m_w_o_mem, w_up=m_w_up,
                    w_down=m_w_down)
    v_shards = dict(w_in=v_w_in, w_out=v_w_out, w_q=v_w_q_mem, w_kv=v_w_kv_mem, w_o=v_w_o_mem, w_up=v_w_up,
                    w_down=v_w_down)
    gains = dict(g_mix=g_mix, g_attn_out=g_attn_out, g_conv_out=g_conv_out, g_xattn=g_xattn, g_mem=g_mem,
                 g_mlp=g_mlp, g_final=g_final)
    gains2 = {k: v.reshape(1, -1) for k, v in gains.items()}

    shards = {k: w_shards[k].astype(BF16) for k in _BIG}
    shards["conv_w"] = conv_w
    grads = _Grads(distributed=True)
    grad_x = _local_step(x[0], mem[0], loss_target[0], gains2, _Weights({}, shards), grads)
    small_received = grads.received["small"]

    outs = {}
    tiles = dict(w_in=256, w_out=128, w_q=128, w_kv=256, w_o=128, w_up=256, w_down=256)
    for k in _BIG:
        outs[k] = _sum_adamw(grads.received[k], w_shards[k], m_shards[k], v_shards[k], name=f"adamw_{k}",
                             tr=tiles[k])

    ssum = _sum_small(small_received)
    loss = ssum[9, 0]
    g_small = {k: ssum[i] for i, k in enumerate(_GAIN_ROWS)}
    g_small["g_attn_out"] = ssum[5, 0:512]
    g_small["g_conv_out"] = ssum[5, 512:1024]
    g_conv = lax.dynamic_slice_in_dim(ssum[6:9, 0:512], me * 64, 64, axis=1)
    m_small = dict(g_mix=m_g_mix, g_attn_out=m_g_attn_out, g_conv_out=m_g_conv_out, g_xattn=m_g_xattn,
                   g_mem=m_g_mem, g_mlp=m_g_mlp, g_final=m_g_final)
    v_small = dict(g_mix=v_g_mix, g_attn_out=v_g_attn_out, g_conv_out=v_g_conv_out, g_xattn=v_g_xattn,
                   g_mem=v_g_mem, g_mlp=v_g_mlp, g_final=v_g_final)
    packed = [_pack_small(g_small, g_conv), _pack_small(gains, conv_w), _pack_small(m_small, m_conv_w),
              _pack_small(v_small, v_conv_w)]
    upd = _adamw_small(*packed)

    def unpack(p):
        res = {k: p[i] for i, k in enumerate(_GAIN_ROWS)}
        res["g_attn_out"] = p[5, 0:512]
        res["g_conv_out"] = p[5, 512:1024]
        res["conv_w"] = p[6, 0:192].reshape(3, 64)
        return res

    g_small["conv_w"] = g_conv
    small_out = [g_small] + [unpack(p) for p in upd]
    names = {"g_mix": "g_mix", "w_in": "w_in", "conv_w": "conv_w", "g_attn_out": "g_attn_out",
             "g_conv_out": "g_conv_out", "w_out": "w_out", "g_xattn": "g_xattn", "g_mem": "g_mem",
             "w_q_mem": "w_q", "w_kv_mem": "w_kv", "w_o_mem": "w_o", "g_mlp": "g_mlp", "w_up": "w_up",
             "w_down": "w_down", "g_final": "g_final"}
    result = [loss, grad_x[None]]
    for which in range(4):
        for key in names.values():
            result.append(outs[key][which] if key in outs else small_out[which][key])
    return tuple(result)
```

```python
import math

import jax
import jax.numpy as jnp
from jax import lax
from jax.experimental import pallas as pl
from jax.experimental.pallas import tpu as pltpu

F32 = jnp.float32
BF16 = jnp.bfloat16
NORM_EPS = 1e-6
NEG_INF = -1e30
N_DEV = 8
BLK = 128
HEAD_DIM = 64
N_MEM_HEADS = 4
ADAM_LR = 0.001
ADAM_B1 = 0.9
ADAM_B2 = 0.999
ADAM_EPS = 1e-08
ADAM_WD = 0.01
ADAM_STEP = 10
MESH = pl.DeviceIdType.MESH
ANY = pl.BlockSpec(memory_space=pl.ANY)


def _dot(a, b):
    return jnp.dot(a, b, preferred_element_type=F32)


def _dot_nt(a, b):
    return lax.dot_general(a, b, (((1,), (1,)), ((), ())), preferred_element_type=F32)


def _dot_tn(a, b):
    return lax.dot_general(a, b, (((0,), (0,)), ((), ())), preferred_element_type=F32)


def _params(semantics, vmem_mb):
    return pltpu.CompilerParams(dimension_semantics=semantics, vmem_limit_bytes=vmem_mb << 20)


def _rms_fwd(x, g):
    r = lax.rsqrt(jnp.mean(x * x, axis=-1, keepdims=True) + NORM_EPS)
    xh = x * r
    return xh * g, xh, r


def _rms_bwd(dy, xh, r, g):
    gy = dy * g
    return r * (gy - xh * jnp.mean(xh * gy, axis=-1, keepdims=True))


def _position():
    x, y, c = lax.axis_index("x"), lax.axis_index("y"), lax.axis_index("c")
    return x, y, c


def _block_of(ref, j, axis, shard_shape):
    r, c = shard_shape
    if axis is None:
        return ref.at[j]
    if axis == 0:
        return ref.at[pl.ds(j * r, r), :]
    return ref.at[:, pl.ds(j * c, c)]


class _Gather:
    has_mid = True
    alias_pairs = ()

    def __init__(self, shards, axes):
        self.arrays = list(shards)
        self.axes = list(axes)
        self.n = len(self.arrays)

    def out_shape(self):
        res = []
        for s, axis in zip(self.arrays, self.axes):
            r, c = s.shape
            shape = (N_DEV, r, c) if axis is None else (N_DEV * r, c) if axis == 0 else (r, N_DEV * c)
            res.append(jax.ShapeDtypeStruct(shape, s.dtype))
        return res

    def scratch(self):
        return [pltpu.SemaphoreType.DMA((self.n, 7)), pltpu.SemaphoreType.DMA((self.n, 7)),
                pltpu.SemaphoreType.DMA((self.n,))]

    def _ctx(self, ins, outs, sems):
        send_sems, recv_sems, local_sems = sems
        x, y, c = _position()
        me, sibling = (x, y, c), (x, y, 1 - c)
        chips = [(1 - x, y), (x, 1 - y), (1 - x, 1 - y)]

        def lin(px, py, pc):
            return 4 * px + 2 * py + pc

        def place(a, block):
            return _block_of(outs[a], lin(*block), self.axes[a], self.arrays[a].shape)

        def copy(a, k, block, to, src=None):
            dst = place(a, block)
            return pltpu.make_async_remote_copy(
                src_ref=dst if src is None else src, dst_ref=dst,
                send_sem=send_sems.at[a, k], recv_sem=recv_sems.at[a, k],
                device_id=to, device_id_type=MESH)

        def mine():
            return [pltpu.make_async_copy(ins[a], place(a, me), local_sems.at[a]) for a in range(self.n)]

        def first():
            res = []
            for a in range(self.n):
                res.append(copy(a, 0, me, sibling, src=ins[a]))
                res += [copy(a, 1 + j, me, (*chip, c), src=ins[a]) for j, chip in enumerate(chips)]
            return res

        return c, me, sibling, chips, copy, mine, first

    def start(self, ins, outs, sems):
        _, _, _, _, _, mine, first = self._ctx(ins, outs, sems)
        for cp in mine() + first():
            cp.start()

    def mid(self, ins, outs, sems):
        c, me, sibling, chips, copy, _, _ = self._ctx(ins, outs, sems)
        for j, chip in enumerate(chips):
            for a in range(self.n):
                copy(a, 1 + j, (*chip, c), me).wait_recv()
                copy(a, 4 + j, (*chip, c), sibling).start()

    def finish(self, ins, outs, sems):
        c, me, sibling, chips, copy, mine, first = self._ctx(ins, outs, sems)
        for a in range(self.n):
            copy(a, 0, sibling, me).wait_recv()
            for j, chip in enumerate(chips):
                copy(a, 4 + j, (*chip, 1 - c), me).wait_recv()
        for cp in first():
            cp.wait_send()
        for j, chip in enumerate(chips):
            for a in range(self.n):
                copy(a, 4 + j, (*chip, c), sibling).wait_send()
        for cp in mine():
            cp.wait()


class _Exchange:
    has_mid = False

    def __init__(self, parts, axes, pieces=None, into=None):
        self.n = len(parts)
        self.axes = list(axes)
        self.pieces = list(pieces or [(0, 1, 1)] * self.n)
        into = list(into or [None] * self.n)
        kept = [a for a in range(self.n) if into[a] is not None]
        self.arrays = list(parts) + [into[a] for a in kept]
        self.alias_pairs = [(self.n + i, a) for i, a in enumerate(kept)]

    def _piece(self, a):
        r, c = self.arrays[a].shape
        axis = self.axes[a]
        return (r, c) if axis is None else (r // N_DEV, c) if axis == 0 else (r, c // N_DEV)

    def _rows(self, a):
        first, count, of = self.pieces[a]
        unit = self._piece(a)[0] // of
        return pl.ds(first * unit, count * unit)

    def out_shape(self):
        return [jax.ShapeDtypeStruct((N_DEV,) + self._piece(a), self.arrays[a].dtype) for a in range(self.n)]

    def scratch(self):
        return [pltpu.SemaphoreType.DMA((self.n, 7)), pltpu.SemaphoreType.DMA((self.n, 7)),
                pltpu.SemaphoreType.DMA((self.n,))]

    def _ctx(self, ins, outs, sems):
        send_sems, recv_sems, local_sems = sems
        x, y, c = _position()
        me = 4 * x + 2 * y + c

        def src(a, j):
            block = ins[a] if self.axes[a] is None else _block_of(ins[a], j, self.axes[a], self._piece(a))
            return block.at[self._rows(a), :]

        def dst(a, j):
            return outs[a].at[j, self._rows(a), :]

        def local():
            return [pltpu.make_async_copy(src(a, me), dst(a, me), local_sems.at[a]) for a in range(self.n)]

        def remote(inbound):
            res = []
            for a in range(self.n):
                for k in range(1, N_DEV):
                    peer = (1 - x if k & 4 else x, 1 - y if k & 2 else y, 1 - c if k & 1 else c)
                    plin = 4 * peer[0] + 2 * peer[1] + peer[2]
                    res.append(pltpu.make_async_remote_copy(
                        src_ref=src(a, plin), dst_ref=dst(a, plin if inbound else me),
                        send_sem=send_sems.at[a, k - 1], recv_sem=recv_sems.at[a, k - 1],
                        device_id=peer, device_id_type=MESH))
            return res

        return local, remote

    def start(self, ins, outs, sems):
        local, remote = self._ctx(ins, outs, sems)
        for cp in local() + remote(False):
            cp.start()

    def finish(self, ins, outs, sems):
        local, remote = self._ctx(ins, outs, sems)
        for cp in remote(True):
            cp.wait_recv()
        for cp in remote(False):
            cp.wait_send()
        for cp in local():
            cp.wait()


def _comm_call(rider, name):
    n_in, n_out = len(rider.arrays), len(rider.out_shape())

    def body(*refs):
        ins, outs, sems = refs[:n_in], refs[n_in:n_in + n_out], refs[n_in + n_out:]
        rider.start(ins, outs, sems)
        if rider.has_mid:
            rider.mid(ins, outs, sems)
        rider.finish(ins, outs, sems)

    return pl.pallas_call(
        body, name=name, out_shape=rider.out_shape(),
        in_specs=[ANY] * n_in, out_specs=[ANY] * n_out, scratch_shapes=rider.scratch(),
        input_output_aliases=dict(rider.alias_pairs),
    )(*rider.arrays)


def _pcall(body, *, name, grid, in_specs, out_specs, out_shape, scratch_shapes=(), semantics, vmem_mb, rider=None,
           aliases=None):
    in_specs, out_specs, out_shape = list(in_specs), list(out_specs), list(out_shape)
    scratch_shapes = list(scratch_shapes)
    aliases = dict(aliases or {})
    if rider is None:
        call = pl.pallas_call(body, name=name, grid=grid, in_specs=in_specs, out_specs=out_specs,
                              out_shape=out_shape, scratch_shapes=scratch_shapes, input_output_aliases=aliases,
                              compiler_params=_params(semantics, vmem_mb))
        return lambda *args: (list(call(*args)), None)
    n_in, n_out, n_scr = len(in_specs), len(out_specs), len(scratch_shapes)
    r_in, r_shapes = len(rider.arrays), rider.out_shape()
    r_out = len(r_shapes)
    aliases.update({n_in + i: n_out + o for i, o in rider.alias_pairs})
    total = math.prod(grid)
    mid_step = (3 * total) // 4

    def wrapped(*refs):
        bounds = [0, n_in, r_in, n_out, r_out, n_scr]
        for i in range(1, len(bounds)):
            bounds[i] += bounds[i - 1]
        a, ra, o, ro, s = (refs[bounds[i]:bounds[i + 1]] for i in range(5))
        rs = refs[bounds[5]:]
        step = pl.program_id(0)
        for k in range(1, len(grid)):
            step = step * grid[k] + pl.program_id(k)
        pl.when(step == 0)(lambda: rider.start(ra, ro, rs))
        body(*a, *o, *s)
        if rider.has_mid:
            pl.when(step == mid_step)(lambda: rider.mid(ra, ro, rs))
        pl.when(step == total - 1)(lambda: rider.finish(ra, ro, rs))

    call = pl.pallas_call(
        wrapped, name=name, grid=grid, in_specs=in_specs + [ANY] * r_in, out_specs=out_specs + [ANY] * r_out,
        out_shape=out_shape + r_shapes, scratch_shapes=scratch_shapes + rider.scratch(),
        input_output_aliases=aliases, compiler_params=_params(("arbitrary",) * len(grid), vmem_mb))

    def run(*args):
        res = call(*args, *rider.arrays)
        return list(res[:n_out]), list(res[n_out:])

    return run


def _norm_matmul(x, g, w, *, name, out_dtype, tb, bn, relu=False, save_h=False, rider=None):
    t, d = x.shape
    n = w.shape[1]

    def body(x_ref, g_ref, w_ref, o_ref, *rest):
        h_scr = rest[-1]

        @pl.when(pl.program_id(1) == 0)
        def _():
            h = _rms_fwd(x_ref[...], g_ref[...])[0].astype(BF16)
            h_scr[...] = h
            if save_h:
                rest[0][...] = h

        acc = _dot(h_scr[...], w_ref[...])
        if relu:
            acc = jnp.maximum(acc, 0.0)
        o_ref[...] = acc.astype(out_dtype)

    out_shape = [jax.ShapeDtypeStruct((t, n), out_dtype)]
    out_specs = [pl.BlockSpec((tb, bn), lambda i, j: (i, j))]
    if save_h:
        out_shape.append(jax.ShapeDtypeStruct((t, d), BF16))
        out_specs.append(pl.BlockSpec((tb, d), lambda i, j: (i, 0)))
    res, extra = _pcall(
        body, name=name, grid=(t // tb, n // bn),
        in_specs=[pl.BlockSpec((tb, d), lambda i, j: (i, 0)),
                  pl.BlockSpec((1, d), lambda i, j: (0, 0)),
                  pl.BlockSpec((d, bn), lambda i, j: (0, j))],
        out_specs=out_specs, out_shape=out_shape,
        scratch_shapes=[pltpu.VMEM((tb, d), BF16)],
        semantics=("parallel", "arbitrary"), vmem_mb=48, rider=rider,
    )(x, g, w)
    res = res if save_h else res[0]
    return res if rider is None else (res, extra)


def _matmul_nt_normbwd(dy, w, x, g, dres, *, name, tb, also_bf16=False, rider=None):
    t, d = x.shape
    stacked = dy.ndim == 3
    has_res = dres is not None

    def body(dy_ref, w_ref, x_ref, g_ref, *rest):
        rest = list(rest)
        dres_ref = rest.pop(0) if has_res else None
        dx_ref = rest.pop(0)
        dxb_ref = rest.pop(0) if also_bf16 else None
        gg_ref = rest.pop(0)
        i = pl.program_id(0)
        if stacked:
            kb = dy_ref.shape[2]
            dh = _dot_nt(dy_ref[0], w_ref[:, 0:kb])
            for s in range(1, dy_ref.shape[0]):
                dh = dh + _dot_nt(dy_ref[s], w_ref[:, s * kb:(s + 1) * kb])
        else:
            dh = _dot_nt(dy_ref[...], w_ref[...])
        g_v = g_ref[...]
        _, xh, r = _rms_fwd(x_ref[...], g_v)
        dx = _rms_bwd(dh, xh, r, g_v)
        if has_res:
            dx = dx + dres_ref[...]
        dx_ref[...] = dx
        if also_bf16:
            dxb_ref[...] = dx.astype(BF16)
        part = jnp.sum(dh * xh, axis=0, keepdims=True)

        @pl.when(i == 0)
        def _():
            gg_ref[...] = part

        @pl.when(i != 0)
        def _():
            gg_ref[...] += part

    tok = pl.BlockSpec((tb, d), lambda i: (i, 0))
    row = pl.BlockSpec((1, d), lambda i: (0, 0))
    if stacked:
        dy_spec = pl.BlockSpec((dy.shape[0], tb, dy.shape[2]), lambda i: (0, i, 0))
    else:
        dy_spec = pl.BlockSpec((tb, dy.shape[1]), lambda i: (i, 0))
    in_specs = [dy_spec, pl.BlockSpec(w.shape, lambda i: (0, 0)), tok, row]
    args = [dy, w, x, g]
    if has_res:
        in_specs.append(tok)
        args.append(dres)
    out_specs = [tok] + ([tok] if also_bf16 else []) + [row]
    out_shape = ([jax.ShapeDtypeStruct((t, d), F32)] + ([jax.ShapeDtypeStruct((t, d), BF16)] if also_bf16 else [])
                 + [jax.ShapeDtypeStruct((1, d), F32)])
    res, extra = _pcall(
        body, name=name, grid=(t // tb,), in_specs=in_specs, out_specs=out_specs, out_shape=out_shape,
        semantics=("arbitrary",), vmem_mb=56, rider=rider,
    )(*args)
    return res if rider is None else (res, extra)


def _matmul_tn(a, b, *, name, bm, bn, square_a=False, rider=None):
    t, m = a.shape
    stacked = b.ndim == 3
    n = b.shape[0] * bn if stacked else b.shape[1]

    def body(a_ref, b_ref, o_ref):
        av = a_ref[...]
        if square_a:
            av = av.astype(F32)
            av = (av * av).astype(BF16)
        o_ref[...] = _dot_tn(av, b_ref[...]).astype(BF16)

    res, extra = _pcall(
        body, name=name, grid=(m // bm, n // bn),
        in_specs=[pl.BlockSpec((t, bm), lambda i, j: (0, i)),
                  pl.BlockSpec((None, t, bn), lambda i, j: (j, 0, 0)) if stacked
                  else pl.BlockSpec((t, bn), lambda i, j: (0, j))],
        out_specs=[pl.BlockSpec((bm, bn), lambda i, j: (i, j))], out_shape=[jax.ShapeDtypeStruct((m, n), BF16)],
        semantics=("parallel", "parallel"), vmem_mb=56, rider=rider,
    )(a, b)
    return res[0] if rider is None else (res[0], extra)


N_RES = 16
SEG = 128
HALF = N_RES * SEG
TI = 16


def _x4(a):
    return a.reshape(a.shape[0] // HALF, N_RES, SEG, a.shape[1])


def _reorder(arrays, inverse, name, rider=None):
    t, c = arrays[0].shape
    n = len(arrays)
    n_i = SEG // TI
    natural = pl.BlockSpec((TI * N_RES, c), lambda s: (s, 0))
    major = pl.BlockSpec((1, N_RES, TI, c), lambda s: (s // n_i, 0, s % n_i, 0))

    def body(*refs):
        scr = refs[-1]
        for i_ref, o_ref in zip(refs[:n], refs[n:2 * n]):
            for cb in range(c // BLK):
                cols = slice(cb * BLK, (cb + 1) * BLK)
                slab = scr.at[cb]
                if inverse:
                    for r in range(N_RES):
                        slab[pl.ds(r, TI, stride=N_RES), :] = i_ref[0, r, :, cols]
                    o_ref[:, cols] = slab[...]
                else:
                    slab[...] = i_ref[:, cols]
                    for r in range(N_RES):
                        o_ref[0, r, :, cols] = slab[pl.ds(r, TI, stride=N_RES), :]

    shape4 = (t // HALF, N_RES, SEG, c)
    res, extra = _pcall(
        body, name=name, grid=(t // (TI * N_RES),),
        in_specs=[major if inverse else natural] * n, out_specs=[natural if inverse else major] * n,
        out_shape=[jax.ShapeDtypeStruct((t, c) if inverse else shape4, F32)] * n,
        scratch_shapes=[pltpu.VMEM((c // BLK, TI * N_RES, BLK), F32)],
        semantics=("parallel",), vmem_mb=32, rider=rider,
    )(*[_x4(a) if inverse else a for a in arrays])
    res = [r.reshape(t, c) for r in res]
    return res if rider is None else (res, extra)


_PATTERNS = ((1, 16, 8, SEG), (4, 4, 32, 4 * SEG), (16, 1, SEG, 0))
_FIRST = {1: 1, 4: 4, 16: 16}


def _group_rows(d, g):
    a = g >> 4
    if d == 16:
        base = a * HALF + (g & 15) * SEG
        prev = base - HALF
    elif d == 4:
        c = (g >> 2) & 3
        base = a * HALF + (g & 3) * SEG + c * 32
        prev = jnp.where(c > 0, base - 32, base - HALF + 96)
    else:
        c = g & 15
        base = a * HALF + c * 8
        prev = jnp.where(c > 0, base - 8, base - HALF + 120)
    return base, prev


def _load_rows(ref, base, n, rows, stride):
    parts = [ref[pl.ds(pl.multiple_of(base + j * stride, 8), rows), :] for j in range(n)]
    return parts[0] if n == 1 else jnp.concatenate(parts, axis=0)


def _store_rows(ref, base, val, n, rows, stride, add=False):
    for j in range(n):
        sl = pl.ds(pl.multiple_of(base + j * stride, 8), rows)
        piece = val[j * rows:(j + 1) * rows, :]
        if add:
            ref[sl, :] += piece
        else:
            ref[sl, :] = piece


def _band_bias(n, rows):
    shift = rows.bit_length() - 1
    lq = lax.broadcasted_iota(jnp.int32, (BLK, BLK), 0)
    lk = lax.broadcasted_iota(jnp.int32, (BLK, BLK), 1)
    iq = (lq & (rows - 1)) * n + (lq >> shift)
    ik = (lk & (rows - 1)) * n + (lk >> shift)
    zero = jnp.zeros((BLK, BLK), F32)
    return jnp.where(ik >= iq, zero, NEG_INF), jnp.where(ik <= iq, zero, NEG_INF)


def _set_bias(bias_scr, n, rows):
    prev_b, cur_b = _band_bias(n, rows)
    for half in range(2):
        bias_scr[half * BLK:(half + 1) * BLK, 0:BLK] = prev_b
        bias_scr[half * BLK:(half + 1) * BLK, BLK:2 * BLK] = cur_b


SCALE = 1.0 / math.sqrt(HEAD_DIM)


def _head_consts(value=1.0):
    lane_lo = lax.broadcasted_iota(jnp.int32, (BLK, BLK), 1) < HEAD_DIM
    return lane_lo, [jnp.where(lane_lo, value, 0.0).astype(BF16), jnp.where(lane_lo, 0.0, value).astype(BF16)]


def _stack_heads(v, head_mask):
    return jnp.concatenate([v * head_mask[0], v * head_mask[1]], axis=0)


def _unstack_heads(v2, lane_lo):
    return jnp.where(lane_lo, v2[:BLK], v2[BLK:])


def _rows_per_head(v, lane_lo):
    rolled = pltpu.roll(v, HEAD_DIM, axis=1)
    return jnp.concatenate([jnp.where(lane_lo, v, rolled), jnp.where(lane_lo, rolled, v)], axis=0)


WIDTH = 4


def _loop(lo, hi, fn, width=None):
    if width is None:
        def body(g, carry):
            fn(g)
            return carry

        if hi > lo:
            lax.fori_loop(lo, hi, body, 0)
        return
    while hi > lo:
        trips = (hi - lo) // width
        if trips:
            def body(i, carry, lo=lo, width=width):
                fn([lo + width * i + j for j in range(width)])
                return carry

            lax.fori_loop(0, trips, body, 0)
            lo += trips * width
        width = max(1, width // 2)


def _mix_weights(l1, l2, l3):
    mx = jnp.maximum(jnp.maximum(l1, l2), l3)
    e1, e2, e3 = jnp.exp(l1 - mx), jnp.exp(l2 - mx), jnp.exp(l3 - mx)
    inv = 1.0 / (e1 + e2 + e3)
    return e1 * inv, e2 * inv, e3 * inv


def _attention_fwd(qkv, rider=None):
    t = qkv.shape[0]
    groups = 16 * (t // HALF)

    def body(q_ref, k_ref, v_ref, attn_ref, l1_ref, l2_ref, l3_ref, o_scr, bias_scr):
        lane_lo, q_mask = _head_consts(SCALE)
        l_refs = (l1_ref, l2_ref, l3_ref)
        for p, (d, n, rows, stride) in enumerate(_PATTERNS):
            _set_bias(bias_scr, n, rows)
            o_p, l_p = o_scr.at[p], l_refs[p]

            def block(gs, has_prev):
                at = [_group_rows(d, g) for g in gs]

                def load(ref, b):
                    return _load_rows(ref, b, n, rows, stride).astype(BF16)

                q2 = [_stack_heads(load(q_ref, b), q_mask) for b, _ in at]
                k2 = [load(k_ref, b) for b, _ in at]
                v2 = [load(v_ref, b) for b, _ in at]
                if has_prev:
                    k2 = [jnp.concatenate([load(k_ref, pv), k], axis=0) for (_, pv), k in zip(at, k2)]
                    v2 = [jnp.concatenate([load(v_ref, pv), v], axis=0) for (_, pv), v in zip(at, v2)]
                s = [_dot_nt(q, k) for q, k in zip(q2, k2)]
                s = [x + (bias_scr[...] if has_prev else bias_scr[:, BLK:2 * BLK]) for x in s]
                mx = [jnp.max(x, axis=1, keepdims=True) for x in s]
                e = [jnp.exp(x - m) for x, m in zip(s, mx)]
                den = [jnp.sum(x, axis=1, keepdims=True) for x in e]
                o2 = [_dot(x.astype(BF16), v) * (1.0 / dn) for x, v, dn in zip(e, v2, den)]
                lse2 = [jnp.broadcast_to(m + jnp.log(dn), (2 * BLK, BLK)) for m, dn in zip(mx, den)]
                for (b, _), o, l in zip(at, o2, lse2):
                    _store_rows(o_p, b, _unstack_heads(o, lane_lo), n, rows, stride)
                    _store_rows(l_p, b, _unstack_heads(l, lane_lo), n, rows, stride)

            _loop(0, _FIRST[d], lambda gs: block(gs, False), width=WIDTH)
            _loop(_FIRST[d], groups, lambda gs: block(gs, True), width=WIDTH)

        def mix(i):
            sl = pl.ds(pl.multiple_of(i * 256, 256), 256)
            w = _mix_weights(l1_ref[sl, :], l2_ref[sl, :], l3_ref[sl, :])
            attn_ref[sl, :] = w[0] * o_scr[0, sl, :] + w[1] * o_scr[1, sl, :] + w[2] * o_scr[2, sl, :]

        _loop(0, t // 256, mix)

    def col(c0):
        return pl.BlockSpec((t, BLK), lambda hp: (0, c0 + hp))

    res, extra = _pcall(
        body, name="attention_fwd", grid=(4,), in_specs=[col(0), col(4), col(8)], out_specs=[col(0)] * 4,
        out_shape=[jax.ShapeDtypeStruct((t, 512), F32)] * 4,
        scratch_shapes=[pltpu.VMEM((3, t, BLK), F32), pltpu.VMEM((2 * BLK, 2 * BLK), F32)],
        semantics=("parallel",), vmem_mb=48, rider=rider,
    )(qkv, qkv, qkv)
    return res if rider is None else (res, extra)


def _attention_bwd(qkv, dattn, dsum, lses, dproj, rider=None):
    t = qkv.shape[0]
    groups = 16 * (t // HALF)

    def body(q_ref, k_ref, v_ref, da_ref, ds_ref, l1_ref, l2_ref, l3_ref, kept_ref, out_ref, acc, bias_scr):
        del kept_ref
        lane_lo, head_mask = _head_consts()
        q_mask = _head_consts(SCALE)[1]
        l_refs = (l1_ref, l2_ref, l3_ref)

        def clear(i):
            sl = pl.ds(pl.multiple_of(i * 512, 512), 512)
            for s in range(3):
                acc[s, sl, :] = jnp.zeros((512, BLK), F32)

        _loop(0, t // 512, clear)
        dq_acc, dk_acc, dv_acc = acc.at[0], acc.at[1], acc.at[2]
        for p, (d, n, rows, stride) in enumerate(_PATTERNS):
            _set_bias(bias_scr, n, rows)

            def block(gs, has_prev):
                at = [_group_rows(d, g) for g in gs]

                def load(ref, b):
                    return _load_rows(ref, b, n, rows, stride)

                def put(ref, b, val):
                    _store_rows(ref, b, val, n, rows, stride, add=True)

                def wide(x):
                    return jnp.concatenate([x, x], axis=1) if has_prev else x

                lse = [[load(ref, b) for ref in l_refs] for b, _ in at]
                w = [_mix_weights(*ls)[p] for ls in lse]
                do2 = [_stack_heads((wg * load(da_ref, b)).astype(BF16), head_mask) for wg, (b, _) in zip(w, at)]
                dl2 = [wide(_rows_per_head(wg * load(ds_ref, b), lane_lo)) for wg, (b, _) in zip(w, at)]
                lse2 = [wide(_rows_per_head(ls[p], lane_lo)) for ls in lse]
                q2 = [_stack_heads(load(q_ref, b).astype(BF16), q_mask) for b, _ in at]
                k2 = [load(k_ref, b).astype(BF16) for b, _ in at]
                v2 = [load(v_ref, b).astype(BF16) for b, _ in at]
                if has_prev:
                    k2 = [jnp.concatenate([load(k_ref, pv).astype(BF16), k], axis=0) for (_, pv), k in zip(at, k2)]
                    v2 = [jnp.concatenate([load(v_ref, pv).astype(BF16), v], axis=0) for (_, pv), v in zip(at, v2)]
                s = [_dot_nt(q, k) for q, k in zip(q2, k2)]
                dp = [_dot_nt(do, v) for do, v in zip(do2, v2)]
                pr = [jnp.exp(x + (bias_scr[...] if has_prev else bias_scr[:, BLK:2 * BLK]) - l)
                      for x, l in zip(s, lse2)]
                ds = [(pg * (x - dl)).astype(BF16) for pg, x, dl in zip(pr, dp, dl2)]
                dq2 = [_dot(x, k) * SCALE for x, k in zip(ds, k2)]
                dk2 = [_dot_tn(x, q) for x, q in zip(ds, q2)]
                dv2 = [_dot_tn(pg.astype(BF16), do) for pg, do in zip(pr, do2)]
                for (b, pv), dq, dk, dv in zip(at, dq2, dk2, dv2):
                    put(dq_acc, b, _unstack_heads(dq, lane_lo))
                    if has_prev:
                        put(dk_acc, pv, dk[:BLK])
                        put(dv_acc, pv, dv[:BLK])
                        put(dk_acc, b, dk[BLK:])
                        put(dv_acc, b, dv[BLK:])
                    else:
                        put(dk_acc, b, dk)
                        put(dv_acc, b, dv)

            _loop(0, _FIRST[d], lambda gs: block(gs, False), width=WIDTH)
            _loop(_FIRST[d], groups, lambda gs: block(gs, True), width=WIDTH)

        def emit(i):
            sl = pl.ds(pl.multiple_of(i * 512, 512), 512)
            for s in range(3):
                out_ref[s, sl, :] = acc[s, sl, :].astype(BF16)

        _loop(0, t // 512, emit)

    def col(c0):
        return pl.BlockSpec((t, BLK), lambda hp: (0, c0 + hp))

    res, extra = _pcall(
        body, name="attention_bwd", grid=(4,),
        in_specs=[col(0), col(4), col(8)] + [col(0)] * 5 + [ANY],
        out_specs=[pl.BlockSpec((3, t, BLK), lambda hp: (0, 0, hp))],
        out_shape=[jax.ShapeDtypeStruct(dproj.shape, BF16)],
        scratch_shapes=[pltpu.VMEM((3, t, BLK), F32), pltpu.VMEM((2 * BLK, 2 * BLK), F32)],
        semantics=("parallel",), vmem_mb=56, rider=rider, aliases={8: 0},
    )(qkv, qkv, qkv, dattn, dsum, *lses, dproj)
    return res[0] if rider is None else (res[0], extra)


def _order_specs(t):
    n_i = SEG // TI
    nblk = (t // HALF) * n_i
    per = TI // 8

    def main(c, col=0):
        return pl.BlockSpec((1, N_RES, TI, c), lambda s: (s // n_i, 0, s % n_i, col))

    def before(c, col=0):
        return pl.BlockSpec((1, 2, 8, c), lambda s: (jnp.maximum(s - 1, 0) // n_i, N_RES // 2 - 1,
                                                     (jnp.maximum(s - 1, 0) % n_i) * per + per - 1, col))

    def after(c, col=0):
        return pl.BlockSpec((1, 2, 8, c), lambda s: (jnp.minimum(s + 1, nblk - 1) // n_i, 0,
                                                     (jnp.minimum(s + 1, nblk - 1) % n_i) * per, col))

    return nblk, main, before, after


def _shift_in(v, row_in, up):
    rows = v.shape[0]
    idx = lax.broadcasted_iota(jnp.int32, v.shape, 0)
    fill = jnp.broadcast_to(row_in, v.shape)
    if up:
        return jnp.where(idx == rows - 1, fill, pltpu.roll(v, rows - 1, axis=0))
    return jnp.where(idx == 0, fill, pltpu.roll(v, 1, axis=0))


def _taps_behind(u, before):
    s15 = _shift_in(u[N_RES - 1], before[1, 7:8, :], up=False)
    s14 = _shift_in(u[N_RES - 2], before[0, 7:8, :], up=False)
    m1 = jnp.concatenate([s15[None], u[:N_RES - 1]], axis=0)
    m2 = jnp.concatenate([s14[None], s15[None], u[:N_RES - 2]], axis=0)
    return m1, m2


def _taps_ahead(u, after):
    t0 = _shift_in(u[0], after[0, 0:1, :], up=True)
    t1 = _shift_in(u[1], after[1, 0:1, :], up=True)
    p1 = jnp.concatenate([u[1:], t0[None]], axis=0)
    p2 = jnp.concatenate([u[2:], t0[None], t1[None]], axis=0)
    return p1, p2


def _conv_fwd(gates, before, first, cw):
    bg, cg, xc = gates[..., 0:512], gates[..., 512:1024], gates[..., 1024:1536]
    u = cg * xc
    ub = before[..., 512:1024] * before[..., 1024:1536]
    ub = jnp.where(first, jnp.zeros_like(ub), ub)
    m1, m2 = _taps_behind(u, ub)
    conv = m2 * cw[0:1, :] + m1 * cw[1:2, :] + u * cw[2:3, :]
    return bg, u, m1, m2, conv


def _sum_tokens(v):
    return jnp.sum(jnp.sum(v, axis=0), axis=0, keepdims=True)


def _mixer_fwd(x, attn, gates, cw, g_a, g_c, w_out):
    t, d = x.shape
    nblk, main, before, _ = _order_specs(t)
    rows = N_RES * TI

    def body(x_ref, at_ref, gt_ref, gb_ref, cw_ref, ga_ref, gc_ref, wa_ref, wb_ref, x1_ref, mg_ref):
        an = _rms_fwd(at_ref[0], ga_ref[...])[0].astype(BF16)
        bg, _, _, _, conv = _conv_fwd(gt_ref[0], gb_ref[0], pl.program_id(0) == 0, cw_ref[...])
        cn = _rms_fwd(bg * conv, gc_ref[...])[0].astype(BF16)
        mg_ref[0, :, :, 0:512] = an
        mg_ref[0, :, :, 512:1024] = cn
        y = _dot(an.reshape(rows, 512), wa_ref[...]) + _dot(cn.reshape(rows, 512), wb_ref[...])
        x1_ref[0] = x_ref[0] + y.reshape(N_RES, TI, d)

    const = lambda r, c, i0=0: pl.BlockSpec((r, c), lambda s: (i0, 0))
    x1, merged = pl.pallas_call(
        body, name="mixer_fwd", grid=(nblk,),
        in_specs=[main(d), main(512), main(1536, 1), before(1536, 1), const(3, 512), const(1, 512), const(1, 512),
                  const(512, d), const(512, d, 1)],
        out_specs=[main(d), main(d)],
        out_shape=[jax.ShapeDtypeStruct(_x4(x).shape, F32), jax.ShapeDtypeStruct(_x4(x).shape, BF16)],
        compiler_params=_params(("parallel",), 48),
    )(_x4(x), _x4(attn), _x4(gates), _x4(gates), cw, g_a, g_c, w_out, w_out)
    return x1.reshape(t, d), merged.reshape(t, d)


def _mixer_bwd(dx1, attn, gates, cw, g_a, g_c, w_out, head_sum, rider=None):
    t, d = dx1.shape
    nblk, main, before, _ = _order_specs(t)
    rows = N_RES * TI

    def body(dx_ref, at_ref, gt_ref, gb_ref, cw_ref, ga_ref, gc_ref, wa_ref, wb_ref, hs_ref,
             da_ref, dsum_ref, dy_ref, gga_ref, ggc_ref):
        s = pl.program_id(0)
        dxb = dx_ref[0].reshape(rows, d).astype(BF16)
        dma = _dot_nt(dxb, wa_ref[...]).reshape(N_RES, TI, 512)
        dmc = _dot_nt(dxb, wb_ref[...]).reshape(N_RES, TI, 512)
        attn_v, g_av = at_ref[0], ga_ref[...]
        _, ah, ra = _rms_fwd(attn_v, g_av)
        dattn = _rms_bwd(dma, ah, ra, g_av)
        da_ref[0] = dattn
        z = (dattn * attn_v).reshape(rows, 512)
        hs = hs_ref[...]
        z1 = z.astype(BF16)
        z2 = (z - z1.astype(F32)).astype(BF16)
        dsum_ref[0] = (_dot(z1, hs) + _dot(z2, hs)).reshape(N_RES, TI, 512)
        bg, _, _, _, conv = _conv_fwd(gt_ref[0], gb_ref[0], s == 0, cw_ref[...])
        g_cv = gc_ref[...]
        _, yh, rc = _rms_fwd(bg * conv, g_cv)
        dy_ref[0] = _rms_bwd(dmc, yh, rc, g_cv)
        pa, pc = _sum_tokens(dma * ah), _sum_tokens(dmc * yh)

        @pl.when(s == 0)
        def _():
            gga_ref[...] = pa
            ggc_ref[...] = pc

        @pl.when(s != 0)
        def _():
            gga_ref[...] += pa
            ggc_ref[...] += pc

    const = lambda r, c, i0=0: pl.BlockSpec((r, c), lambda s: (i0, 0))
    shape4 = _x4(attn).shape
    res, extra = _pcall(
        body, name="mixer_bwd", grid=(nblk,),
        in_specs=[main(d), main(512), main(1536, 1), before(1536, 1), const(3, 512), const(1, 512), const(1, 512),
                  const(512, d), const(512, d, 1), const(512, 512)],
        out_specs=[main(512)] * 3 + [const(1, 512), const(1, 512)],
        out_shape=[jax.ShapeDtypeStruct(shape4, F32)] * 3 + [jax.ShapeDtypeStruct((1, 512), F32)] * 2,
        semantics=("arbitrary",), vmem_mb=48, rider=rider,
    )(_x4(dx1), _x4(attn), _x4(gates), _x4(gates), cw, g_a, g_c, w_out, w_out, head_sum)
    res = [r.reshape(t, 512) for r in res[:3]] + res[3:]
    return res if rider is None else (res, extra)


def _conv_bwd(dy, gates, cw):
    t = dy.shape[0]
    nblk, main, before, after = _order_specs(t)
    n_i = SEG // TI

    def body(dy_ref, dya_ref, gt_ref, gb_ref, ga_ref, cw_ref, dp_ref, gcw_ref):
        s = pl.program_id(0)
        cw_v, gates_v = cw_ref[...], gt_ref[0]
        bg, u, m1, m2, conv = _conv_fwd(gates_v, gb_ref[0], s == 0, cw_v)
        dy_v = dy_ref[0]
        dconv = dy_v * bg
        dca = dya_ref[0] * ga_ref[0][..., 0:512]
        dca = jnp.where(s == nblk - 1, jnp.zeros_like(dca), dca)
        p1, p2 = _taps_ahead(dconv, dca)
        du = dconv * cw_v[2:3, :] + p1 * cw_v[1:2, :] + p2 * cw_v[0:1, :]
        dp_ref[0, 0] = (dy_v * conv).astype(BF16)
        dp_ref[1, 0] = (du * gates_v[..., 1024:1536]).astype(BF16)
        dp_ref[2, 0] = (du * gates_v[..., 512:1024]).astype(BF16)
        parts = [_sum_tokens(dconv * m2), _sum_tokens(dconv * m1), _sum_tokens(dconv * u)]

        @pl.when(s == 0)
        def _():
            gcw_ref[...] = jnp.zeros_like(gcw_ref)

        for tap in range(3):
            gcw_ref[tap:tap + 1, :] += parts[tap]

    dproj, gcw = pl.pallas_call(
        body, name="conv_bwd", grid=(nblk,),
        in_specs=[main(512), after(512), main(1536, 1), before(1536, 1), after(1536, 1),
                  pl.BlockSpec((3, 512), lambda s: (0, 0))],
        out_specs=[pl.BlockSpec((3, 1, N_RES, TI, 512), lambda s: (1, s // n_i, 0, s % n_i, 0)),
                   pl.BlockSpec((8, 512), lambda s: (0, 0))],
        out_shape=[jax.ShapeDtypeStruct((6, t // HALF, N_RES, SEG, 512), BF16), jax.ShapeDtypeStruct((8, 512), F32)],
        compiler_params=_params(("arbitrary",), 40),
    )(_x4(dy), _x4(dy), _x4(gates), _x4(gates), _x4(gates), cw)
    return dproj.reshape(6, t, 512), gcw


def _xattn_fwd(x1, g, w_q, kv, w_o, *, tb):
    t, d = x1.shape
    hd = d // N_MEM_HEADS
    m = kv.shape[0]

    def body(x_ref, g_ref, wq_ref, k_ref, v_ref, wo_ref, x2_ref, h_ref, q_ref, o_ref):
        xv = x_ref[...]
        h = _rms_fwd(xv, g_ref[...])[0].astype(BF16)
        h_ref[...] = h
        q = _dot(h, wq_ref[...]).astype(BF16)
        q_ref[...] = q
        for hh in range(N_MEM_HEADS):
            sl = slice(hh * hd, (hh + 1) * hd)
            s = _dot_nt(q[:, sl], k_ref[:, sl]) * (1.0 / 16.0)
            e = jnp.exp(s - jnp.max(s, axis=1, keepdims=True))
            p = e / jnp.sum(e, axis=1, keepdims=True)
            o_ref[:, sl] = _dot(p.astype(BF16), v_ref[:, sl]).astype(BF16)
        x2_ref[...] = xv + _dot(o_ref[...], wo_ref[...])

    tok = pl.BlockSpec((tb, d), lambda i: (i, 0))
    full = pl.BlockSpec((d, d), lambda i: (0, 0))
    return pl.pallas_call(
        body, name="xattn_fwd", grid=(t // tb,),
        in_specs=[tok, pl.BlockSpec((1, d), lambda i: (0, 0)), full,
                  pl.BlockSpec((m, d), lambda i: (0, 0)), pl.BlockSpec((m, d), lambda i: (0, 1)), full],
        out_specs=[tok] * 4,
        out_shape=[jax.ShapeDtypeStruct((t, d), F32)] + [jax.ShapeDtypeStruct((t, d), BF16)] * 3,
        compiler_params=_params(("parallel",), 48),
    )(x1, g, w_q, kv, kv, w_o)


def _xattn_bwd(dx2, x1, g, q, w_q, kv, w_o, *, tb, rider=None):
    t, d = x1.shape
    hd = d // N_MEM_HEADS
    m = kv.shape[0]

    def body(dx2_ref, x_ref, g_ref, q_ref, wq_ref, k_ref, v_ref, wo_ref,
             dx1_ref, dx1b_ref, dq_ref, dk_ref, dv_ref, gg_ref):
        i = pl.program_id(0)

        @pl.when(i == 0)
        def _():
            dk_ref[...] = jnp.zeros_like(dk_ref)
            dv_ref[...] = jnp.zeros_like(dv_ref)

        dx2 = dx2_ref[...]
        do = _dot_nt(dx2.astype(BF16), wo_ref[...]).astype(BF16)
        for hh in range(N_MEM_HEADS):
            sl = slice(hh * hd, (hh + 1) * hd)
            qh, kh, vh, doh = q_ref[:, sl], k_ref[:, sl], v_ref[:, sl], do[:, sl]
            s = _dot_nt(qh, kh) * (1.0 / 16.0)
            e = jnp.exp(s - jnp.max(s, axis=1, keepdims=True))
            p = e / jnp.sum(e, axis=1, keepdims=True)
            dp = _dot_nt(doh, vh)
            ds = (p * (dp - jnp.sum(dp * p, axis=1, keepdims=True)) * (1.0 / 16.0)).astype(BF16)
            dq_ref[:, sl] = _dot(ds, kh).astype(BF16)
            dk_ref[:, sl] += _dot_tn(ds, qh)
            dv_ref[:, sl] += _dot_tn(p.astype(BF16), doh)
        dh = _dot_nt(dq_ref[...], wq_ref[...])
        g_v = g_ref[...]
        _, xh, r = _rms_fwd(x_ref[...], g_v)
        dx1 = dx2 + _rms_bwd(dh, xh, r, g_v)
        dx1_ref[...] = dx1
        dx1b_ref[...] = dx1.astype(BF16)
        part = jnp.sum(dh * xh, axis=0, keepdims=True)

        @pl.when(i == 0)
        def _():
            gg_ref[...] = part

        @pl.when(i != 0)
        def _():
            gg_ref[...] += part

    tok = pl.BlockSpec((tb, d), lambda i: (i, 0))
    full = pl.BlockSpec((d, d), lambda i: (0, 0))
    acc = pl.BlockSpec((m, d), lambda i: (0, 0))
    res, extra = _pcall(
        body, name="xattn_bwd", grid=(t // tb,),
        in_specs=[tok, tok, pl.BlockSpec((1, d), lambda i: (0, 0)), tok, full,
                  pl.BlockSpec((m, d), lambda i: (0, 0)), pl.BlockSpec((m, d), lambda i: (0, 1)), full],
        out_specs=[tok, tok, tok, acc, acc, pl.BlockSpec((1, d), lambda i: (0, 0))],
        out_shape=[jax.ShapeDtypeStruct((t, d), F32), jax.ShapeDtypeStruct((t, d), BF16),
                   jax.ShapeDtypeStruct((t, d), BF16),
                   jax.ShapeDtypeStruct((m, d), F32), jax.ShapeDtypeStruct((m, d), F32),
                   jax.ShapeDtypeStruct((1, d), F32)],
        semantics=("arbitrary",), vmem_mb=48, rider=rider,
    )(dx2, x1, g, q, w_q, kv, kv, w_o)
    return res if rider is None else (res, extra)


def _mlp_down_loss(a, w_down, x2, tgt, g, *, tb):
    t, d = x2.shape
    f = a.shape[1]

    def body(a_ref, w_ref, x_ref, t_ref, g_ref, dx_ref, dxb_ref, loss_ref, gg_ref):
        i = pl.program_id(0)
        av = a_ref[...].astype(F32)
        x3 = x_ref[...] + _dot((av * av).astype(BF16), w_ref[...])
        g_v = g_ref[...]
        out, xh, r = _rms_fwd(x3, g_v)
        err = out - t_ref[...]
        dout = err * (1.0 / d)
        dx = _rms_bwd(dout, xh, r, g_v)
        dx_ref[...] = dx
        dxb_ref[...] = dx.astype(BF16)
        part = jnp.sum(dout * xh, axis=0, keepdims=True)
        lpart = 0.5 * jnp.sum(jnp.mean(err * err, axis=-1, keepdims=True), axis=0, keepdims=True)
        lpart = jnp.broadcast_to(lpart, loss_ref.shape)

        @pl.when(i == 0)
        def _():
            gg_ref[...] = part
            loss_ref[...] = lpart

        @pl.when(i != 0)
        def _():
            gg_ref[...] += part
            loss_ref[...] += lpart

    tok = pl.BlockSpec((tb, d), lambda i: (i, 0))
    return pl.pallas_call(
        body, name="mlp_down_loss", grid=(t // tb,),
        in_specs=[pl.BlockSpec((tb, f), lambda i: (i, 0)), pl.BlockSpec((f, d), lambda i: (0, 0)), tok, tok,
                  pl.BlockSpec((1, d), lambda i: (0, 0))],
        out_specs=[tok, tok, pl.BlockSpec((8, 128), lambda i: (0, 0)), pl.BlockSpec((1, d), lambda i: (0, 0))],
        out_shape=[jax.ShapeDtypeStruct((t, d), F32), jax.ShapeDtypeStruct((t, d), BF16),
                   jax.ShapeDtypeStruct((8, 128), F32), jax.ShapeDtypeStruct((1, d), F32)],
        compiler_params=_params(("arbitrary",), 56),
    )(a, w_down, x2, tgt, g)


def _mlp_dpre(dx3, w_down, a, *, tb, bn):
    t, d = dx3.shape
    f = a.shape[1]

    def body(dx_ref, w_ref, a_ref, o_ref):
        o_ref[...] = (2.0 * a_ref[...].astype(F32) * _dot_nt(dx_ref[...], w_ref[...])).astype(BF16)

    return pl.pallas_call(
        body, name="mlp_dpre", grid=(t // tb, f // bn),
        in_specs=[pl.BlockSpec((tb, d), lambda i, j: (i, 0)), pl.BlockSpec((bn, d), lambda i, j: (j, 0)),
                  pl.BlockSpec((tb, bn), lambda i, j: (i, j))],
        out_specs=pl.BlockSpec((tb, bn), lambda i, j: (i, j)),
        out_shape=jax.ShapeDtypeStruct((t, f), BF16),
        compiler_params=_params(("parallel", "arbitrary"), 48),
    )(dx3, w_down, a)


def _adamw(gsum, w, m, v):
    m_new = ADAM_B1 * m + (1.0 - ADAM_B1) * gsum
    v_new = ADAM_B2 * v + (1.0 - ADAM_B2) * (gsum * gsum)
    m_hat = m_new / (1.0 - ADAM_B1 ** ADAM_STEP)
    v_hat = v_new / (1.0 - ADAM_B2 ** ADAM_STEP)
    delta = -ADAM_LR * (m_hat / (jnp.sqrt(v_hat) + ADAM_EPS) + ADAM_WD * w)
    return delta, m_new, v_new


def _sum_adamw(parts, w, m, v, *, name, tr, rider=None):
    r, c = w.shape

    def body(p_ref, w_ref, m_ref, v_ref, g_ref, d_ref, mo_ref, vo_ref):
        g = p_ref[0].astype(F32)
        for k in range(1, N_DEV):
            g = g + p_ref[k].astype(F32)
        g_ref[...] = g
        d_ref[...], mo_ref[...], vo_ref[...] = _adamw(g, w_ref[...], m_ref[...], v_ref[...])

    blk = pl.BlockSpec((tr, c), lambda i: (i, 0))
    res, extra = _pcall(
        body, name=name, grid=(r // tr,),
        in_specs=[pl.BlockSpec((N_DEV, tr, c), lambda i: (0, i, 0)), blk, blk, blk],
        out_specs=[blk] * 4, out_shape=[jax.ShapeDtypeStruct((r, c), F32)] * 4,
        semantics=("parallel",), vmem_mb=40, rider=rider,
    )(parts, w, m, v)
    return res if rider is None else (res, extra)


def _sum_small(parts):
    _, r, c = parts.shape

    def body(p_ref, o_ref):
        s = p_ref[0]
        for k in range(1, N_DEV):
            s = s + p_ref[k]
        o_ref[...] = s

    return pl.pallas_call(body, name="sum_small", out_shape=jax.ShapeDtypeStruct((r, c), F32))(parts)


def _adamw_small(g, w, m, v):
    def body(g_ref, w_ref, m_ref, v_ref, d_ref, mo_ref, vo_ref):
        d_ref[...], mo_ref[...], vo_ref[...] = _adamw(g_ref[...], w_ref[...], m_ref[...], v_ref[...])

    return pl.pallas_call(body, name="adamw_small", out_shape=[jax.ShapeDtypeStruct(g.shape, F32)] * 3)(g, w, m, v)


def _head_sum_matrix():
    r = lax.broadcasted_iota(jnp.int32, (512, 512), 0) // HEAD_DIM
    c = lax.broadcasted_iota(jnp.int32, (512, 512), 1) // HEAD_DIM
    return (r == c).astype(BF16)


_SHARD_AXIS = dict(w_in=1, w_out=0, w_q=0, w_kv=1, w_o=0, w_up=1, w_down=0, conv_w=None, small=None)


class _Weights:
    def __init__(self, full, shards=None):
        self.full = dict(full)
        self.shards = shards

    def rider(self, names):
        if self.shards is None:
            return None
        return _Gather([self.shards[n] for n in names], [_SHARD_AXIS[n] for n in names])

    def arrived(self, names, gathered):
        if gathered is not None:
            for n, g in zip(names, gathered):
                self.full[n] = g.transpose(1, 0, 2).reshape(g.shape[1], -1) if n == "conv_w" else g

    def __getitem__(self, name):
        return self.full[name]


class _Grads:
    def __init__(self, distributed):
        self.distributed = distributed
        self.local = {}
        self.received = {}

    def add(self, name, g):
        self.local[name] = g

    def rider(self, names, pieces=None):
        if not self.distributed:
            return None
        into = [self.received.get(n) for n in names]
        return _Exchange([self.local[n] for n in names], [_SHARD_AXIS[n] for n in names], pieces, into)

    def arrived(self, names, received):
        if received is not None:
            for n, r in zip(names, received):
                self.received[n] = r


def _ride(fn, *args, rider=None, **kw):
    if rider is None:
        return fn(*args, **kw), None
    return fn(*args, rider=rider, **kw)


def _local_step(x, mem, tgt, gains, weights, grads):
    names = ["w_in", "conv_w"]
    (x, tgt), got = _ride(_reorder, [x, tgt], False, "reorder_in", rider=weights.rider(names))
    weights.arrived(names, got)
    w_in, cw = weights["w_in"], weights["conv_w"]

    names = ["w_out", "w_kv"]
    (proj, h1), got = _ride(_norm_matmul, x, gains["g_mix"], w_in, name="proj", out_dtype=F32, tb=1024, bn=768,
                            save_h=True, rider=weights.rider(names))
    weights.arrived(names, got)
    names = ["w_q", "w_o", "w_up"]
    (attn, *lses), got = _ride(_attention_fwd, proj, rider=weights.rider(names))
    weights.arrived(names, got)
    x1, merged = _mixer_fwd(x, attn, proj, cw, gains["g_attn_out"], gains["g_conv_out"], weights["w_out"])
    kv, mem_n = _norm_matmul(mem, gains["g_mem"], weights["w_kv"], name="mem_kv", out_dtype=BF16, tb=mem.shape[0],
                             bn=1024, save_h=True)
    x2, h2, qm, om = _xattn_fwd(x1, gains["g_xattn"], weights["w_q"], kv, weights["w_o"], tb=512)
    w_up = weights["w_up"]
    (a, h3), got = _ride(_norm_matmul, x2, gains["g_mlp"], w_up, name="mlp_up", out_dtype=BF16, tb=1024, bn=1024,
                         relu=True, save_h=True, rider=weights.rider(["w_down"]))
    weights.arrived(["w_down"], got)
    w_down = weights["w_down"]
    dx3, dx3b, loss_blk, gg_final = _mlp_down_loss(a, w_down, x2, tgt, gains["g_final"], tb=256)

    dpre = _mlp_dpre(dx3b, w_down, a, tb=1024, bn=1024)
    grads.add("w_down", _matmul_tn(a, dx3b, name="grad_w_down", bm=512, bn=1024, square_a=True))
    gw_up, got = _ride(_matmul_tn, h3, dpre, name="grad_w_up", bm=1024, bn=512,
                       rider=grads.rider(["w_down"], [(0, 1, 2)]))
    grads.arrived(["w_down"], got)
    grads.add("w_up", gw_up)
    (dx2, dx2b, gg_mlp), got = _ride(_matmul_nt_normbwd, dpre, w_up, x2, gains["g_mlp"], dx3, name="mlp_dx", tb=512,
                                     also_bf16=True, rider=grads.rider(["w_down"], [(1, 1, 2)]))
    grads.arrived(["w_down"], got)

    grads.add("w_o", _matmul_tn(om, dx2b, name="grad_w_o", bm=1024, bn=512))
    names = ["w_up", "w_o"]
    (dx1, dx1b, dqm, dk, dv, gg_xattn), got = _ride(_xattn_bwd, dx2, x1, gains["g_xattn"], qm, weights["w_q"], kv,
                                                    weights["w_o"], tb=512,
                                                    rider=grads.rider(names, [(0, 1, 2), (0, 1, 1)]))
    grads.arrived(names, got)
    grads.add("w_q", _matmul_tn(h2, dqm, name="grad_w_q", bm=1024, bn=512))
    dkv = jnp.concatenate([dk, dv], axis=1).astype(BF16)
    grads.add("w_kv", _matmul_tn(mem_n, dkv, name="grad_w_kv", bm=1024, bn=1024))
    _, gg_mem = _matmul_nt_normbwd(dkv, weights["w_kv"], mem, gains["g_mem"], None, name="mem_dx", tb=mem.shape[0])

    grads.add("w_out", _matmul_tn(merged, dx1b, name="grad_w_out", bm=1024, bn=512))
    (dattn, dsum, dy, gg_attn, gg_conv), got = _ride(
        _mixer_bwd, dx1, attn, proj, cw, gains["g_attn_out"], gains["g_conv_out"], weights["w_out"],
        _head_sum_matrix(), rider=grads.rider(["w_up"], [(1, 1, 2)]))
    grads.arrived(["w_up"], got)
    dproj, gcw = _conv_bwd(dy, proj, cw)
    names = ["w_q", "w_kv", "w_out"]
    dproj, got = _ride(_attention_bwd, proj, dattn, dsum, lses, dproj, rider=grads.rider(names))
    grads.arrived(names, got)
    grads.add("w_in", _matmul_tn(h1, dproj, name="grad_w_in", bm=1024, bn=512))
    (grad_x, gg_mix), got = _ride(_matmul_nt_normbwd, dproj, w_in, x, gains["g_mix"], dx1, name="mixer_dx",
                                  tb=512, rider=grads.rider(["w_in"], [(0, 5, 8)]))
    grads.arrived(["w_in"], got)

    rows = [gg_mix, gg_xattn, gg_mem, gg_mlp, gg_final, jnp.concatenate([gg_attn, gg_conv], axis=1),
            jnp.pad(gcw[0:3], ((0, 0), (0, 512))), jnp.pad(loss_blk[0:1, 0:1], ((0, 6), (0, 1023)))]
    grads.add("small", jnp.concatenate(rows, axis=0))
    (grad_x,) = _reorder([grad_x], True, "reorder_out")
    return grad_x


_BIG = ("w_in", "w_out", "w_q", "w_kv", "w_o", "w_up", "w_down")
_GAIN_ROWS = ("g_mix", "g_xattn", "g_mem", "g_mlp", "g_final")


def _pack_small(vals, conv):
    rows = [vals[k].reshape(1, -1) for k in _GAIN_ROWS]
    rows.append(jnp.concatenate([vals["g_attn_out"].reshape(1, -1), vals["g_conv_out"].reshape(1, -1)], axis=1))
    flat = conv.reshape(1, -1)
    rows.append(jnp.pad(flat, ((0, 0), (0, 1024 - flat.shape[1]))))
    rows.append(jnp.zeros((1, 1024), F32))
    return jnp.concatenate(rows, axis=0)


def kernel(x, mem, g_mix, w_in, conv_w, g_attn_out, g_conv_out, w_out, g_xattn, g_mem, w_q_mem, w_kv_mem, w_o_mem, g_mlp, w_up, w_down, g_final, loss_target, m_g_mix, m_w_in, m_conv_w, m_g_attn_out, m_g_conv_out, m_w_out, m_g_xattn, m_g_mem, m_w_q_mem, m_w_kv_mem, m_w_o_mem, m_g_mlp, m_w_up, m_w_down, m_g_final, v_g_mix, v_w_in, v_conv_w, v_g_attn_out, v_g_conv_out, v_w_out, v_g_xattn, v_g_mem, v_w_q_mem, v_w_kv_mem, v_w_o_mem, v_g_mlp, v_w_up, v_w_down, v_g_final):
    d = x.shape[-1]
    me = 4 * lax.axis_index("x") + 2 * lax.axis_index("y") + lax.axis_index("c")
    w_shards = dict(w_in=w_in, w_out=w_out, w_q=w_q_mem, w_kv=w_kv_mem, w_o=w_o_mem, w_up=w_up, w_down=w_down)
    m_shards = dict(w_in=m_w_in, w_out=m_w_out, w_q=m_w_q_mem, w_kv=m_w_kv_mem, w_o=m_w_o_mem, w_up=m_w_up,
                    w_down=m_w_down)
    v_shards = dict(w_in=v_w_in, w_out=v_w_out, w_q=v_w_q_mem, w_kv=v_w_kv_mem, w_o=v_w_o_mem, w_up=v_w_up,
                    w_down=v_w_down)
    gains = dict(g_mix=g_mix, g_attn_out=g_attn_out, g_conv_out=g_conv_out, g_xattn=g_xattn, g_mem=g_mem,
                 g_mlp=g_mlp, g_final=g_final)
    gains2 = {k: v.reshape(1, -1) for k, v in gains.items()}

    shards = {k: w_shards[k].astype(BF16) for k in _BIG}
    shards["conv_w"] = conv_w
    grads = _Grads(distributed=True)
    grad_x = _local_step(x[0], mem[0], loss_target[0], gains2, _Weights({}, shards), grads)

    outs = {}
    tiles = dict(w_in=256, w_out=128, w_q=128, w_kv=256, w_o=128, w_up=256, w_down=256)
    for k in ("w_up", "w_down", "w_out", "w_q", "w_kv", "w_o", "w_in"):
        last = ["w_in", "small"]
        rider = grads.rider(last, [(5, 3, 8), (0, 1, 1)]) if k == "w_up" else None
        outs[k], got = _ride(_sum_adamw, grads.received[k], w_shards[k], m_shards[k], v_shards[k],
                             name=f"adamw_{k}", tr=tiles[k], rider=rider)
        grads.arrived(last, got)
    small_received = grads.received["small"]

    ssum = _sum_small(small_received)
    loss = ssum[9, 0]
    g_small = {k: ssum[i] for i, k in enumerate(_GAIN_ROWS)}
    g_small["g_attn_out"] = ssum[5, 0:512]
    g_small["g_conv_out"] = ssum[5, 512:1024]
    g_conv = lax.dynamic_slice_in_dim(ssum[6:9, 0:512], me * 64, 64, axis=1)
    m_small = dict(g_mix=m_g_mix, g_attn_out=m_g_attn_out, g_conv_out=m_g_conv_out, g_xattn=m_g_xattn,
                   g_mem=m_g_mem, g_mlp=m_g_mlp, g_final=m_g_final)
    v_small = dict(g_mix=v_g_mix, g_attn_out=v_g_attn_out, g_conv_out=v_g_conv_out, g_xattn=v_g_xattn,
                   g_mem=v_g_mem, g_mlp=v_g_mlp, g_final=v_g_final)
    packed = [_pack_small(g_small, g_conv), _pack_small(gains, conv_w), _pack_small(m_small, m_conv_w),
              _pack_small(v_small, v_conv_w)]
    upd = _adamw_small(*packed)

    def unpack(p):
        res = {k: p[i] for i, k in enumerate(_GAIN_ROWS)}
        res["g_attn_out"] = p[5, 0:512]
        res["g_conv_out"] = p[5, 512:1024]
        res["conv_w"] = p[6, 0:192].reshape(3, 64)
        return res

    g_small["conv_w"] = g_conv
    small_out = [g_small] + [unpack(p) for p in upd]
    names = {"g_mix": "g_mix", "w_in": "w_in", "conv_w": "conv_w", "g_attn_out": "g_attn_out",
             "g_conv_out": "g_conv_out", "w_out": "w_out", "g_xattn": "g_xattn", "g_mem": "g_mem",
             "w_q_mem": "w_q", "w_kv_mem": "w_kv", "w_o_mem": "w_o", "g_mlp": "g_mlp", "w_up": "w_up",
             "w_down": "w_down", "g_final": "g_final"}
    result = [loss, grad_x[None]]
    for which in range(4):
        for key in names.values():
            result.append(outs[key][which] if key in outs else small_out[which][key])
    return tuple(result)
```

```python
import math

import jax
import jax.numpy as jnp
from jax import lax
from jax.experimental import pallas as pl
from jax.experimental.pallas import tpu as pltpu

F32 = jnp.float32
BF16 = jnp.bfloat16
NORM_EPS = 1e-6
NEG_INF = -1e30
N_DEV = 8
BLK = 128
HEAD_DIM = 64
N_MEM_HEADS = 4
ADAM_LR = 0.001
ADAM_B1 = 0.9
ADAM_B2 = 0.999
ADAM_EPS = 1e-08
ADAM_WD = 0.01
ADAM_STEP = 10
MESH = pl.DeviceIdType.MESH
ANY = pl.BlockSpec(memory_space=pl.ANY)


def _dot(a, b):
    return jnp.dot(a, b, preferred_element_type=F32)


def _dot_nt(a, b):
    return lax.dot_general(a, b, (((1,), (1,)), ((), ())), preferred_element_type=F32)


def _dot_tn(a, b):
    return lax.dot_general(a, b, (((0,), (0,)), ((), ())), preferred_element_type=F32)


def _params(semantics, vmem_mb):
    return pltpu.CompilerParams(dimension_semantics=semantics, vmem_limit_bytes=vmem_mb << 20)


def _rms_fwd(x, g):
    r = lax.rsqrt(jnp.mean(x * x, axis=-1, keepdims=True) + NORM_EPS)
    xh = x * r
    return xh * g, xh, r


def _rms_bwd(dy, xh, r, g):
    gy = dy * g
    return r * (gy - xh * jnp.mean(xh * gy, axis=-1, keepdims=True))


def _position():
    x, y, c = lax.axis_index("x"), lax.axis_index("y"), lax.axis_index("c")
    return x, y, c


def _block_of(ref, j, axis, shard_shape):
    r, c = shard_shape
    if axis is None:
        return ref.at[j]
    if axis == 0:
        return ref.at[pl.ds(j * r, r), :]
    return ref.at[:, pl.ds(j * c, c)]


class _Gather:
    has_mid = True
    alias_pairs = ()

    def __init__(self, shards, axes):
        self.arrays = list(shards)
        self.axes = list(axes)
        self.n = len(self.arrays)

    def out_shape(self):
        res = []
        for s, axis in zip(self.arrays, self.axes):
            r, c = s.shape
            shape = (N_DEV, r, c) if axis is None else (N_DEV * r, c) if axis == 0 else (r, N_DEV * c)
            res.append(jax.ShapeDtypeStruct(shape, s.dtype))
        return res

    def scratch(self):
        return [pltpu.SemaphoreType.DMA((self.n, 7)), pltpu.SemaphoreType.DMA((self.n, 7)),
                pltpu.SemaphoreType.DMA((self.n,))]

    def _ctx(self, ins, outs, sems):
        send_sems, recv_sems, local_sems = sems
        x, y, c = _position()
        me, sibling = (x, y, c), (x, y, 1 - c)
        chips = [(1 - x, y), (x, 1 - y), (1 - x, 1 - y)]

        def lin(px, py, pc):
            return 4 * px + 2 * py + pc

        def place(a, block):
            return _block_of(outs[a], lin(*block), self.axes[a], self.arrays[a].shape)

        def copy(a, k, block, to, src=None):
            dst = place(a, block)
            return pltpu.make_async_remote_copy(
                src_ref=dst if src is None else src, dst_ref=dst,
                send_sem=send_sems.at[a, k], recv_sem=recv_sems.at[a, k],
                device_id=to, device_id_type=MESH)

        def mine():
            return [pltpu.make_async_copy(ins[a], place(a, me), local_sems.at[a]) for a in range(self.n)]

        def first():
            res = []
            for a in range(self.n):
                res.append(copy(a, 0, me, sibling, src=ins[a]))
                res += [copy(a, 1 + j, me, (*chip, c), src=ins[a]) for j, chip in enumerate(chips)]
            return res

        return c, me, sibling, chips, copy, mine, first

    def start(self, ins, outs, sems):
        _, _, _, _, _, mine, first = self._ctx(ins, outs, sems)
        for cp in mine() + first():
            cp.start()

    def mid(self, ins, outs, sems):
        c, me, sibling, chips, copy, _, _ = self._ctx(ins, outs, sems)
        for j, chip in enumerate(chips):
            for a in range(self.n):
                copy(a, 1 + j, (*chip, c), me).wait_recv()
                copy(a, 4 + j, (*chip, c), sibling).start()

    def finish(self, ins, outs, sems):
        c, me, sibling, chips, copy, mine, first = self._ctx(ins, outs, sems)
        for a in range(self.n):
            copy(a, 0, sibling, me).wait_recv()
            for j, chip in enumerate(chips):
                copy(a, 4 + j, (*chip, 1 - c), me).wait_recv()
        for cp in first():
            cp.wait_send()
        for j, chip in enumerate(chips):
            for a in range(self.n):
                copy(a, 4 + j, (*chip, c), sibling).wait_send()
        for cp in mine():
            cp.wait()


class _Exchange:
    has_mid = False

    def __init__(self, parts, axes, pieces=None, into=None):
        self.n = len(parts)
        self.axes = list(axes)
        self.pieces = list(pieces or [(0, 1, 1)] * self.n)
        into = list(into or [None] * self.n)
        kept = [a for a in range(self.n) if into[a] is not None]
        self.arrays = list(parts) + [into[a] for a in kept]
        self.alias_pairs = [(self.n + i, a) for i, a in enumerate(kept)]

    def _piece(self, a):
        r, c = self.arrays[a].shape
        axis = self.axes[a]
        return (r, c) if axis is None else (r // N_DEV, c) if axis == 0 else (r, c // N_DEV)

    def _rows(self, a):
        first, count, of = self.pieces[a]
        unit = self._piece(a)[0] // of
        return pl.ds(first * unit, count * unit)

    def out_shape(self):
        return [jax.ShapeDtypeStruct((N_DEV,) + self._piece(a), self.arrays[a].dtype) for a in range(self.n)]

    def scratch(self):
        return [pltpu.SemaphoreType.DMA((self.n, 7)), pltpu.SemaphoreType.DMA((self.n, 7)),
                pltpu.SemaphoreType.DMA((self.n,))]

    def _ctx(self, ins, outs, sems):
        send_sems, recv_sems, local_sems = sems
        x, y, c = _position()
        me = 4 * x + 2 * y + c

        def src(a, j):
            block = ins[a] if self.axes[a] is None else _block_of(ins[a], j, self.axes[a], self._piece(a))
            return block.at[self._rows(a), :]

        def dst(a, j):
            return outs[a].at[j, self._rows(a), :]

        def local():
            return [pltpu.make_async_copy(src(a, me), dst(a, me), local_sems.at[a]) for a in range(self.n)]

        def remote(inbound):
            res = []
            for a in range(self.n):
                for k in range(1, N_DEV):
                    peer = (1 - x if k & 4 else x, 1 - y if k & 2 else y, 1 - c if k & 1 else c)
                    plin = 4 * peer[0] + 2 * peer[1] + peer[2]
                    res.append(pltpu.make_async_remote_copy(
                        src_ref=src(a, plin), dst_ref=dst(a, plin if inbound else me),
                        send_sem=send_sems.at[a, k - 1], recv_sem=recv_sems.at[a, k - 1],
                        device_id=peer, device_id_type=MESH))
            return res

        return local, remote

    def start(self, ins, outs, sems):
        local, remote = self._ctx(ins, outs, sems)
        for cp in local() + remote(False):
            cp.start()

    def finish(self, ins, outs, sems):
        local, remote = self._ctx(ins, outs, sems)
        for cp in remote(True):
            cp.wait_recv()
        for cp in remote(False):
            cp.wait_send()
        for cp in local():
            cp.wait()


def _comm_call(rider, name):
    n_in, n_out = len(rider.arrays), len(rider.out_shape())

    def body(*refs):
        ins, outs, sems = refs[:n_in], refs[n_in:n_in + n_out], refs[n_in + n_out:]
        rider.start(ins, outs, sems)
        if rider.has_mid:
            rider.mid(ins, outs, sems)
        rider.finish(ins, outs, sems)

    return pl.pallas_call(
        body, name=name, out_shape=rider.out_shape(),
        in_specs=[ANY] * n_in, out_specs=[ANY] * n_out, scratch_shapes=rider.scratch(),
        input_output_aliases=dict(rider.alias_pairs),
    )(*rider.arrays)


def _pcall(body, *, name, grid, in_specs, out_specs, out_shape, scratch_shapes=(), semantics, vmem_mb, rider=None,
           aliases=None):
    in_specs, out_specs, out_shape = list(in_specs), list(out_specs), list(out_shape)
    scratch_shapes = list(scratch_shapes)
    aliases = dict(aliases or {})
    if rider is None:
        call = pl.pallas_call(body, name=name, grid=grid, in_specs=in_specs, out_specs=out_specs,
                              out_shape=out_shape, scratch_shapes=scratch_shapes, input_output_aliases=aliases,
                              compiler_params=_params(semantics, vmem_mb))
        return lambda *args: (list(call(*args)), None)
    n_in, n_out, n_scr = len(in_specs), len(out_specs), len(scratch_shapes)
    r_in, r_shapes = len(rider.arrays), rider.out_shape()
    r_out = len(r_shapes)
    aliases.update({n_in + i: n_out + o for i, o in rider.alias_pairs})
    total = math.prod(grid)
    mid_step = (3 * total) // 4

    def wrapped(*refs):
        bounds = [0, n_in, r_in, n_out, r_out, n_scr]
        for i in range(1, len(bounds)):
            bounds[i] += bounds[i - 1]
        a, ra, o, ro, s = (refs[bounds[i]:bounds[i + 1]] for i in range(5))
        rs = refs[bounds[5]:]
        step = pl.program_id(0)
        for k in range(1, len(grid)):
            step = step * grid[k] + pl.program_id(k)
        pl.when(step == 0)(lambda: rider.start(ra, ro, rs))
        body(*a, *o, *s)
        if rider.has_mid:
            pl.when(step == mid_step)(lambda: rider.mid(ra, ro, rs))
        pl.when(step == total - 1)(lambda: rider.finish(ra, ro, rs))

    call = pl.pallas_call(
        wrapped, name=name, grid=grid, in_specs=in_specs + [ANY] * r_in, out_specs=out_specs + [ANY] * r_out,
        out_shape=out_shape + r_shapes, scratch_shapes=scratch_shapes + rider.scratch(),
        input_output_aliases=aliases, compiler_params=_params(("arbitrary",) * len(grid), vmem_mb))

    def run(*args):
        res = call(*args, *rider.arrays)
        return list(res[:n_out]), list(res[n_out:])

    return run


def _norm_matmul(x, g, w, *, name, out_dtype, tb, bn, relu=False, save_h=False, rider=None):
    t, d = x.shape
    n = w.shape[1]

    def body(x_ref, g_ref, w_ref, o_ref, *rest):
        h_scr = rest[-1]

        @pl.when(pl.program_id(1) == 0)
        def _():
            h = _rms_fwd(x_ref[...], g_ref[...])[0].astype(BF16)
            h_scr[...] = h
            if save_h:
                rest[0][...] = h

        acc = _dot(h_scr[...], w_ref[...])
        if relu:
            acc = jnp.maximum(acc, 0.0)
        o_ref[...] = acc.astype(out_dtype)

    out_shape = [jax.ShapeDtypeStruct((t, n), out_dtype)]
    out_specs = [pl.BlockSpec((tb, bn), lambda i, j: (i, j))]
    if save_h:
        out_shape.append(jax.ShapeDtypeStruct((t, d), BF16))
        out_specs.append(pl.BlockSpec((tb, d), lambda i, j: (i, 0)))
    res, extra = _pcall(
        body, name=name, grid=(t // tb, n // bn),
        in_specs=[pl.BlockSpec((tb, d), lambda i, j: (i, 0)),
                  pl.BlockSpec((1, d), lambda i, j: (0, 0)),
                  pl.BlockSpec((d, bn), lambda i, j: (0, j))],
        out_specs=out_specs, out_shape=out_shape,
        scratch_shapes=[pltpu.VMEM((tb, d), BF16)],
        semantics=("parallel", "arbitrary"), vmem_mb=48, rider=rider,
    )(x, g, w)
    res = res if save_h else res[0]
    return res if rider is None else (res, extra)


def _matmul_nt_normbwd(dy, w, x, g, dres, *, name, tb, also_bf16=False, rider=None):
    t, d = x.shape
    stacked = dy.ndim == 3
    has_res = dres is not None

    def body(dy_ref, w_ref, x_ref, g_ref, *rest):
        rest = list(rest)
        dres_ref = rest.pop(0) if has_res else None
        dx_ref = rest.pop(0)
        dxb_ref = rest.pop(0) if also_bf16 else None
        gg_ref = rest.pop(0)
        i = pl.program_id(0)
        if stacked:
            kb = dy_ref.shape[2]
            dh = _dot_nt(dy_ref[0], w_ref[:, 0:kb])
            for s in range(1, dy_ref.shape[0]):
                dh = dh + _dot_nt(dy_ref[s], w_ref[:, s * kb:(s + 1) * kb])
        else:
            dh = _dot_nt(dy_ref[...], w_ref[...])
        g_v = g_ref[...]
        _, xh, r = _rms_fwd(x_ref[...], g_v)
        dx = _rms_bwd(dh, xh, r, g_v)
        if has_res:
            dx = dx + dres_ref[...]
        dx_ref[...] = dx
        if also_bf16:
            dxb_ref[...] = dx.astype(BF16)
        part = jnp.sum(dh * xh, axis=0, keepdims=True)

        @pl.when(i == 0)
        def _():
            gg_ref[...] = part

        @pl.when(i != 0)
        def _():
            gg_ref[...] += part

    tok = pl.BlockSpec((tb, d), lambda i: (i, 0))
    row = pl.BlockSpec((1, d), lambda i: (0, 0))
    if stacked:
        dy_spec = pl.BlockSpec((dy.shape[0], tb, dy.shape[2]), lambda i: (0, i, 0))
    else:
        dy_spec = pl.BlockSpec((tb, dy.shape[1]), lambda i: (i, 0))
    in_specs = [dy_spec, pl.BlockSpec(w.shape, lambda i: (0, 0)), tok, row]
    args = [dy, w, x, g]
    if has_res:
        in_specs.append(tok)
        args.append(dres)
    out_specs = [tok] + ([tok] if also_bf16 else []) + [row]
    out_shape = ([jax.ShapeDtypeStruct((t, d), F32)] + ([jax.ShapeDtypeStruct((t, d), BF16)] if also_bf16 else [])
                 + [jax.ShapeDtypeStruct((1, d), F32)])
    res, extra = _pcall(
        body, name=name, grid=(t // tb,), in_specs=in_specs, out_specs=out_specs, out_shape=out_shape,
        semantics=("arbitrary",), vmem_mb=56, rider=rider,
    )(*args)
    return res if rider is None else (res, extra)


def _matmul_tn(a, b, *, name, bm, bn, square_a=False, rider=None):
    t, m = a.shape
    stacked = b.ndim == 3
    n = b.shape[0] * bn if stacked else b.shape[1]

    def body(a_ref, b_ref, o_ref):
        av = a_ref[...]
        if square_a:
            av = av.astype(F32)
            av = (av * av).astype(BF16)
        o_ref[...] = _dot_tn(av, b_ref[...]).astype(BF16)

    res, extra = _pcall(
        body, name=name, grid=(m // bm, n // bn),
        in_specs=[pl.BlockSpec((t, bm), lambda i, j: (0, i)),
                  pl.BlockSpec((None, t, bn), lambda i, j: (j, 0, 0)) if stacked
                  else pl.BlockSpec((t, bn), lambda i, j: (0, j))],
        out_specs=[pl.BlockSpec((bm, bn), lambda i, j: (i, j))], out_shape=[jax.ShapeDtypeStruct((m, n), BF16)],
        semantics=("parallel", "parallel"), vmem_mb=56, rider=rider,
    )(a, b)
    return res[0] if rider is None else (res[0], extra)


N_RES = 16
SEG = 128
HALF = N_RES * SEG
TI = 16


def _x4(a):
    return a.reshape(a.shape[0] // HALF, N_RES, SEG, a.shape[1])


def _reorder(arrays, inverse, name, rider=None):
    t, c = arrays[0].shape
    n = len(arrays)
    n_i = SEG // TI
    natural = pl.BlockSpec((TI * N_RES, c), lambda s: (s, 0))
    major = pl.BlockSpec((1, N_RES, TI, c), lambda s: (s // n_i, 0, s % n_i, 0))

    def body(*refs):
        scr = refs[-1]
        for i_ref, o_ref in zip(refs[:n], refs[n:2 * n]):
            for cb in range(c // BLK):
                cols = slice(cb * BLK, (cb + 1) * BLK)
                slab = scr.at[cb]
                if inverse:
                    for r in range(N_RES):
                        slab[pl.ds(r, TI, stride=N_RES), :] = i_ref[0, r, :, cols]
                    o_ref[:, cols] = slab[...]
                else:
                    slab[...] = i_ref[:, cols]
                    for r in range(N_RES):
                        o_ref[0, r, :, cols] = slab[pl.ds(r, TI, stride=N_RES), :]

    shape4 = (t // HALF, N_RES, SEG, c)
    res, extra = _pcall(
        body, name=name, grid=(t // (TI * N_RES),),
        in_specs=[major if inverse else natural] * n, out_specs=[natural if inverse else major] * n,
        out_shape=[jax.ShapeDtypeStruct((t, c) if inverse else shape4, F32)] * n,
        scratch_shapes=[pltpu.VMEM((c // BLK, TI * N_RES, BLK), F32)],
        semantics=("parallel",), vmem_mb=32, rider=rider,
    )(*[_x4(a) if inverse else a for a in arrays])
    res = [r.reshape(t, c) for r in res]
    return res if rider is None else (res, extra)


_PATTERNS = ((1, 16, 8, SEG), (4, 4, 32, 4 * SEG), (16, 1, SEG, 0))
_FIRST = {1: 1, 4: 4, 16: 16}


def _group_rows(d, g):
    a = g >> 4
    if d == 16:
        base = a * HALF + (g & 15) * SEG
        prev = base - HALF
    elif d == 4:
        c = (g >> 2) & 3
        base = a * HALF + (g & 3) * SEG + c * 32
        prev = jnp.where(c > 0, base - 32, base - HALF + 96)
    else:
        c = g & 15
        base = a * HALF + c * 8
        prev = jnp.where(c > 0, base - 8, base - HALF + 120)
    return base, prev


def _load_rows(ref, base, n, rows, stride):
    parts = [ref[pl.ds(pl.multiple_of(base + j * stride, 8), rows), :] for j in range(n)]
    return parts[0] if n == 1 else jnp.concatenate(parts, axis=0)


def _store_rows(ref, base, val, n, rows, stride, add=False):
    for j in range(n):
        sl = pl.ds(pl.multiple_of(base + j * stride, 8), rows)
        piece = val[j * rows:(j + 1) * rows, :]
        if add:
            ref[sl, :] += piece
        else:
            ref[sl, :] = piece


def _band_bias(n, rows):
    shift = rows.bit_length() - 1
    lq = lax.broadcasted_iota(jnp.int32, (BLK, BLK), 0)
    lk = lax.broadcasted_iota(jnp.int32, (BLK, BLK), 1)
    iq = (lq & (rows - 1)) * n + (lq >> shift)
    ik = (lk & (rows - 1)) * n + (lk >> shift)
    zero = jnp.zeros((BLK, BLK), F32)
    return jnp.where(ik >= iq, zero, NEG_INF), jnp.where(ik <= iq, zero, NEG_INF)


def _set_bias(bias_scr, n, rows):
    prev_b, cur_b = _band_bias(n, rows)
    for half in range(2):
        bias_scr[half * BLK:(half + 1) * BLK, 0:BLK] = prev_b
        bias_scr[half * BLK:(half + 1) * BLK, BLK:2 * BLK] = cur_b


SCALE = 1.0 / math.sqrt(HEAD_DIM)


def _head_consts(value=1.0):
    lane_lo = lax.broadcasted_iota(jnp.int32, (BLK, BLK), 1) < HEAD_DIM
    return lane_lo, [jnp.where(lane_lo, value, 0.0).astype(BF16), jnp.where(lane_lo, 0.0, value).astype(BF16)]


def _stack_heads(v, head_mask):
    return jnp.concatenate([v * head_mask[0], v * head_mask[1]], axis=0)


def _unstack_heads(v2, lane_lo):
    return jnp.where(lane_lo, v2[:BLK], v2[BLK:])


def _rows_per_head(v, lane_lo):
    rolled = pltpu.roll(v, HEAD_DIM, axis=1)
    return jnp.concatenate([jnp.where(lane_lo, v, rolled), jnp.where(lane_lo, rolled, v)], axis=0)


WIDTH = 4


def _loop(lo, hi, fn, width=None):
    if width is None:
        def body(g, carry):
            fn(g)
            return carry

        if hi > lo:
            lax.fori_loop(lo, hi, body, 0)
        return
    while hi > lo:
        trips = (hi - lo) // width
        if trips:
            def body(i, carry, lo=lo, width=width):
                fn([lo + width * i + j for j in range(width)])
                return carry

            lax.fori_loop(0, trips, body, 0)
            lo += trips * width
        width = max(1, width // 2)


def _mix_weights(l1, l2, l3):
    mx = jnp.maximum(jnp.maximum(l1, l2), l3)
    e1, e2, e3 = jnp.exp(l1 - mx), jnp.exp(l2 - mx), jnp.exp(l3 - mx)
    inv = 1.0 / (e1 + e2 + e3)
    return e1 * inv, e2 * inv, e3 * inv


def _attention_fwd(qkv, rider=None):
    t = qkv.shape[0]
    groups = 16 * (t // HALF)

    def body(q_ref, k_ref, v_ref, attn_ref, l1_ref, l2_ref, l3_ref, o_scr, bias_scr):
        lane_lo, q_mask = _head_consts(SCALE)
        l_refs = (l1_ref, l2_ref, l3_ref)
        for p, (d, n, rows, stride) in enumerate(_PATTERNS):
            _set_bias(bias_scr, n, rows)
            o_p, l_p = o_scr.at[p], l_refs[p]

            def block(gs, has_prev):
                at = [_group_rows(d, g) for g in gs]

                def load(ref, b):
                    return _load_rows(ref, b, n, rows, stride).astype(BF16)

                q2 = [_stack_heads(load(q_ref, b), q_mask) for b, _ in at]
                k2 = [load(k_ref, b) for b, _ in at]
                v2 = [load(v_ref, b) for b, _ in at]
                if has_prev:
                    k2 = [jnp.concatenate([load(k_ref, pv), k], axis=0) for (_, pv), k in zip(at, k2)]
                    v2 = [jnp.concatenate([load(v_ref, pv), v], axis=0) for (_, pv), v in zip(at, v2)]
                s = [_dot_nt(q, k) for q, k in zip(q2, k2)]
                s = [x + (bias_scr[...] if has_prev else bias_scr[:, BLK:2 * BLK]) for x in s]
                mx = [jnp.max(x, axis=1, keepdims=True) for x in s]
                e = [jnp.exp(x - m) for x, m in zip(s, mx)]
                den = [jnp.sum(x, axis=1, keepdims=True) for x in e]
                o2 = [_dot(x.astype(BF16), v) * (1.0 / dn) for x, v, dn in zip(e, v2, den)]
                lse2 = [jnp.broadcast_to(m + jnp.log(dn), (2 * BLK, BLK)) for m, dn in zip(mx, den)]
                for (b, _), o, l in zip(at, o2, lse2):
                    _store_rows(o_p, b, _unstack_heads(o, lane_lo), n, rows, stride)
                    _store_rows(l_p, b, _unstack_heads(l, lane_lo), n, rows, stride)

            _loop(0, _FIRST[d], lambda gs: block(gs, False), width=WIDTH)
            _loop(_FIRST[d], groups, lambda gs: block(gs, True), width=WIDTH)

        def mix(i):
            sl = pl.ds(pl.multiple_of(i * 256, 256), 256)
            w = _mix_weights(l1_ref[sl, :], l2_ref[sl, :], l3_ref[sl, :])
            attn_ref[sl, :] = w[0] * o_scr[0, sl, :] + w[1] * o_scr[1, sl, :] + w[2] * o_scr[2, sl, :]

        _loop(0, t // 256, mix)

    def col(c0):
        return pl.BlockSpec((t, BLK), lambda hp: (0, c0 + hp))

    res, extra = _pcall(
        body, name="attention_fwd", grid=(4,), in_specs=[col(0), col(4), col(8)], out_specs=[col(0)] * 4,
        out_shape=[jax.ShapeDtypeStruct((t, 512), F32)] * 4,
        scratch_shapes=[pltpu.VMEM((3, t, BLK), F32), pltpu.VMEM((2 * BLK, 2 * BLK), F32)],
        semantics=("parallel",), vmem_mb=48, rider=rider,
    )(qkv, qkv, qkv)
    return res if rider is None else (res, extra)


def _attention_bwd(qkv, dattn, dsum, lses, dproj, rider=None):
    t = qkv.shape[0]
    groups = 16 * (t // HALF)

    def body(q_ref, k_ref, v_ref, da_ref, ds_ref, l1_ref, l2_ref, l3_ref, kept_ref, out_ref, acc, bias_scr):
        del kept_ref
        lane_lo, head_mask = _head_consts()
        q_mask = _head_consts(SCALE)[1]
        l_refs = (l1_ref, l2_ref, l3_ref)

        def clear(i):
            sl = pl.ds(pl.multiple_of(i * 512, 512), 512)
            for s in range(3):
                acc[s, sl, :] = jnp.zeros((512, BLK), F32)

        _loop(0, t // 512, clear)
        dq_acc, dk_acc, dv_acc = acc.at[0], acc.at[1], acc.at[2]
        for p, (d, n, rows, stride) in enumerate(_PATTERNS):
            _set_bias(bias_scr, n, rows)

            def block(gs, has_prev):
                at = [_group_rows(d, g) for g in gs]

                def load(ref, b):
                    return _load_rows(ref, b, n, rows, stride)

                def put(ref, b, val):
                    _store_rows(ref, b, val, n, rows, stride, add=True)

                def wide(x):
                    return jnp.concatenate([x, x], axis=1) if has_prev else x

                lse = [[load(ref, b) for ref in l_refs] for b, _ in at]
                w = [_mix_weights(*ls)[p] for ls in lse]
                do2 = [_stack_heads((wg * load(da_ref, b)).astype(BF16), head_mask) for wg, (b, _) in zip(w, at)]
                dl2 = [wide(_rows_per_head(wg * load(ds_ref, b), lane_lo)) for wg, (b, _) in zip(w, at)]
                lse2 = [wide(_rows_per_head(ls[p], lane_lo)) for ls in lse]
                q2 = [_stack_heads(load(q_ref, b).astype(BF16), q_mask) for b, _ in at]
                k2 = [load(k_ref, b).astype(BF16) for b, _ in at]
                v2 = [load(v_ref, b).astype(BF16) for b, _ in at]
                if has_prev:
                    k2 = [jnp.concatenate([load(k_ref, pv).astype(BF16), k], axis=0) for (_, pv), k in zip(at, k2)]
                    v2 = [jnp.concatenate([load(v_ref, pv).astype(BF16), v], axis=0) for (_, pv), v in zip(at, v2)]
                s = [_dot_nt(q, k) for q, k in zip(q2, k2)]
                dp = [_dot_nt(do, v) for do, v in zip(do2, v2)]
                pr = [jnp.exp(x + (bias_scr[...] if has_prev else bias_scr[:, BLK:2 * BLK]) - l)
                      for x, l in zip(s, lse2)]
                ds = [(pg * (x - dl)).astype(BF16) for pg, x, dl in zip(pr, dp, dl2)]
                dq2 = [_dot(x, k) * SCALE for x, k in zip(ds, k2)]
                dk2 = [_dot_tn(x, q) for x, q in zip(ds, q2)]
                dv2 = [_dot_tn(pg.astype(BF16), do) for pg, do in zip(pr, do2)]
                for (b, pv), dq, dk, dv in zip(at, dq2, dk2, dv2):
                    put(dq_acc, b, _unstack_heads(dq, lane_lo))
                    if has_prev:
                        put(dk_acc, pv, dk[:BLK])
                        put(dv_acc, pv, dv[:BLK])
                        put(dk_acc, b, dk[BLK:])
                        put(dv_acc, b, dv[BLK:])
                    else:
                        put(dk_acc, b, dk)
                        put(dv_acc, b, dv)

            _loop(0, _FIRST[d], lambda gs: block(gs, False), width=WIDTH)
            _loop(_FIRST[d], groups, lambda gs: block(gs, True), width=WIDTH)

        def emit(i):
            sl = pl.ds(pl.multiple_of(i * 512, 512), 512)
            for s in range(3):
                out_ref[s, sl, :] = acc[s, sl, :].astype(BF16)

        _loop(0, t // 512, emit)

    def col(c0):
        return pl.BlockSpec((t, BLK), lambda hp: (0, c0 + hp))

    res, extra = _pcall(
        body, name="attention_bwd", grid=(4,),
        in_specs=[col(0), col(4), col(8)] + [col(0)] * 5 + [ANY],
        out_specs=[pl.BlockSpec((3, t, BLK), lambda hp: (0, 0, hp))],
        out_shape=[jax.ShapeDtypeStruct(dproj.shape, BF16)],
        scratch_shapes=[pltpu.VMEM((3, t, BLK), F32), pltpu.VMEM((2 * BLK, 2 * BLK), F32)],
        semantics=("parallel",), vmem_mb=56, rider=rider, aliases={8: 0},
    )(qkv, qkv, qkv, dattn, dsum, *lses, dproj)
    return res[0] if rider is None else (res[0], extra)


def _order_specs(t):
    n_i = SEG // TI
    nblk = (t // HALF) * n_i
    per = TI // 8

    def main(c, col=0):
        return pl.BlockSpec((1, N_RES, TI, c), lambda s: (s // n_i, 0, s % n_i, col))

    def before(c, col=0):
        return pl.BlockSpec((1, 2, 8, c), lambda s: (jnp.maximum(s - 1, 0) // n_i, N_RES // 2 - 1,
                                                     (jnp.maximum(s - 1, 0) % n_i) * per + per - 1, col))

    def after(c, col=0):
        return pl.BlockSpec((1, 2, 8, c), lambda s: (jnp.minimum(s + 1, nblk - 1) // n_i, 0,
                                                     (jnp.minimum(s + 1, nblk - 1) % n_i) * per, col))

    return nblk, main, before, after


def _shift_in(v, row_in, up):
    rows = v.shape[0]
    idx = lax.broadcasted_iota(jnp.int32, v.shape, 0)
    fill = jnp.broadcast_to(row_in, v.shape)
    if up:
        return jnp.where(idx == rows - 1, fill, pltpu.roll(v, rows - 1, axis=0))
    return jnp.where(idx == 0, fill, pltpu.roll(v, 1, axis=0))


def _taps_behind(u, before):
    s15 = _shift_in(u[N_RES - 1], before[1, 7:8, :], up=False)
    s14 = _shift_in(u[N_RES - 2], before[0, 7:8, :], up=False)
    m1 = jnp.concatenate([s15[None], u[:N_RES - 1]], axis=0)
    m2 = jnp.concatenate([s14[None], s15[None], u[:N_RES - 2]], axis=0)
    return m1, m2


def _taps_ahead(u, after):
    t0 = _shift_in(u[0], after[0, 0:1, :], up=True)
    t1 = _shift_in(u[1], after[1, 0:1, :], up=True)
    p1 = jnp.concatenate([u[1:], t0[None]], axis=0)
    p2 = jnp.concatenate([u[2:], t0[None], t1[None]], axis=0)
    return p1, p2


def _conv_fwd(gates, before, first, cw):
    bg, cg, xc = gates[..., 0:512], gates[..., 512:1024], gates[..., 1024:1536]
    u = cg * xc
    ub = before[..., 512:1024] * before[..., 1024:1536]
    ub = jnp.where(first, jnp.zeros_like(ub), ub)
    m1, m2 = _taps_behind(u, ub)
    conv = m2 * cw[0:1, :] + m1 * cw[1:2, :] + u * cw[2:3, :]
    return bg, u, m1, m2, conv


def _sum_tokens(v):
    return jnp.sum(jnp.sum(v, axis=0), axis=0, keepdims=True)


def _mixer_fwd(x, attn, gates, cw, g_a, g_c, w_out):
    t, d = x.shape
    nblk, main, before, _ = _order_specs(t)
    rows = N_RES * TI

    def body(x_ref, at_ref, gt_ref, gb_ref, cw_ref, ga_ref, gc_ref, wa_ref, wb_ref, x1_ref, mg_ref):
        an = _rms_fwd(at_ref[0], ga_ref[...])[0].astype(BF16)
        bg, _, _, _, conv = _conv_fwd(gt_ref[0], gb_ref[0], pl.program_id(0) == 0, cw_ref[...])
        cn = _rms_fwd(bg * conv, gc_ref[...])[0].astype(BF16)
        mg_ref[0, :, :, 0:512] = an
        mg_ref[0, :, :, 512:1024] = cn
        y = _dot(an.reshape(rows, 512), wa_ref[...]) + _dot(cn.reshape(rows, 512), wb_ref[...])
        x1_ref[0] = x_ref[0] + y.reshape(N_RES, TI, d)

    const = lambda r, c, i0=0: pl.BlockSpec((r, c), lambda s: (i0, 0))
    x1, merged = pl.pallas_call(
        body, name="mixer_fwd", grid=(nblk,),
        in_specs=[main(d), main(512), main(1536, 1), before(1536, 1), const(3, 512), const(1, 512), const(1, 512),
                  const(512, d), const(512, d, 1)],
        out_specs=[main(d), main(d)],
        out_shape=[jax.ShapeDtypeStruct(_x4(x).shape, F32), jax.ShapeDtypeStruct(_x4(x).shape, BF16)],
        compiler_params=_params(("parallel",), 48),
    )(_x4(x), _x4(attn), _x4(gates), _x4(gates), cw, g_a, g_c, w_out, w_out)
    return x1.reshape(t, d), merged.reshape(t, d)


def _mixer_bwd(dx1, attn, gates, cw, g_a, g_c, w_out, head_sum, rider=None):
    t, d = dx1.shape
    nblk, main, before, _ = _order_specs(t)
    rows = N_RES * TI

    def body(dx_ref, at_ref, gt_ref, gb_ref, cw_ref, ga_ref, gc_ref, wa_ref, wb_ref, hs_ref,
             da_ref, dsum_ref, dy_ref, gga_ref, ggc_ref):
        s = pl.program_id(0)
        dxb = dx_ref[0].reshape(rows, d).astype(BF16)
        dma = _dot_nt(dxb, wa_ref[...]).reshape(N_RES, TI, 512)
        dmc = _dot_nt(dxb, wb_ref[...]).reshape(N_RES, TI, 512)
        attn_v, g_av = at_ref[0], ga_ref[...]
        _, ah, ra = _rms_fwd(attn_v, g_av)
        dattn = _rms_bwd(dma, ah, ra, g_av)
        da_ref[0] = dattn
        z = (dattn * attn_v).reshape(rows, 512)
        hs = hs_ref[...]
        z1 = z.astype(BF16)
        z2 = (z - z1.astype(F32)).astype(BF16)
        dsum_ref[0] = (_dot(z1, hs) + _dot(z2, hs)).reshape(N_RES, TI, 512)
        bg, _, _, _, conv = _conv_fwd(gt_ref[0], gb_ref[0], s == 0, cw_ref[...])
        g_cv = gc_ref[...]
        _, yh, rc = _rms_fwd(bg * conv, g_cv)
        dy_ref[0] = _rms_bwd(dmc, yh, rc, g_cv)
        pa, pc = _sum_tokens(dma * ah), _sum_tokens(dmc * yh)

        @pl.when(s == 0)
        def _():
            gga_ref[...] = pa
            ggc_ref[...] = pc

        @pl.when(s != 0)
        def _():
            gga_ref[...] += pa
            ggc_ref[...] += pc

    const = lambda r, c, i0=0: pl.BlockSpec((r, c), lambda s: (i0, 0))
    shape4 = _x4(attn).shape
    res, extra = _pcall(
        body, name="mixer_bwd", grid=(nblk,),
        in_specs=[main(d), main(512), main(1536, 1), before(1536, 1), const(3, 512), const(1, 512), const(1, 512),
                  const(512, d), const(512, d, 1), const(512, 512)],
        out_specs=[main(512)] * 3 + [const(1, 512), const(1, 512)],
        out_shape=[jax.ShapeDtypeStruct(shape4, F32)] * 3 + [jax.ShapeDtypeStruct((1, 512), F32)] * 2,
        semantics=("arbitrary",), vmem_mb=48, rider=rider,
    )(_x4(dx1), _x4(attn), _x4(gates), _x4(gates), cw, g_a, g_c, w_out, w_out, head_sum)
    res = [r.reshape(t, 512) for r in res[:3]] + res[3:]
    return res if rider is None else (res, extra)


def _conv_bwd(dy, gates, cw, rider=None):
    t = dy.shape[0]
    nblk, main, before, after = _order_specs(t)
    n_i = SEG // TI

    def body(dy_ref, dya_ref, gt_ref, gb_ref, ga_ref, cw_ref, dp_ref, gcw_ref):
        s = pl.program_id(0)
        cw_v, gates_v = cw_ref[...], gt_ref[0]
        bg, u, m1, m2, conv = _conv_fwd(gates_v, gb_ref[0], s == 0, cw_v)
        dy_v = dy_ref[0]
        dconv = dy_v * bg
        dca = dya_ref[0] * ga_ref[0][..., 0:512]
        dca = jnp.where(s == nblk - 1, jnp.zeros_like(dca), dca)
        p1, p2 = _taps_ahead(dconv, dca)
        du = dconv * cw_v[2:3, :] + p1 * cw_v[1:2, :] + p2 * cw_v[0:1, :]
        dp_ref[0, 0] = (dy_v * conv).astype(BF16)
        dp_ref[1, 0] = (du * gates_v[..., 1024:1536]).astype(BF16)
        dp_ref[2, 0] = (du * gates_v[..., 512:1024]).astype(BF16)
        parts = [_sum_tokens(dconv * m2), _sum_tokens(dconv * m1), _sum_tokens(dconv * u)]

        @pl.when(s == 0)
        def _():
            gcw_ref[...] = jnp.zeros_like(gcw_ref)

        for tap in range(3):
            gcw_ref[tap:tap + 1, :] += parts[tap]

    (dproj, gcw), extra = _pcall(
        body, name="conv_bwd", grid=(nblk,),
        in_specs=[main(512), after(512), main(1536, 1), before(1536, 1), after(1536, 1),
                  pl.BlockSpec((3, 512), lambda s: (0, 0))],
        out_specs=[pl.BlockSpec((3, 1, N_RES, TI, 512), lambda s: (1, s // n_i, 0, s % n_i, 0)),
                   pl.BlockSpec((8, 512), lambda s: (0, 0))],
        out_shape=[jax.ShapeDtypeStruct((6, t // HALF, N_RES, SEG, 512), BF16), jax.ShapeDtypeStruct((8, 512), F32)],
        semantics=("arbitrary",), vmem_mb=40, rider=rider,
    )(_x4(dy), _x4(dy), _x4(gates), _x4(gates), _x4(gates), cw)
    res = (dproj.reshape(6, t, 512), gcw)
    return res if rider is None else (res, extra)


def _xattn_fwd(x1, g, w_q, kv, w_o, *, tb):
    t, d = x1.shape
    hd = d // N_MEM_HEADS
    m = kv.shape[0]

    def body(x_ref, g_ref, wq_ref, k_ref, v_ref, wo_ref, x2_ref, h_ref, q_ref, o_ref):
        xv = x_ref[...]
        h = _rms_fwd(xv, g_ref[...])[0].astype(BF16)
        h_ref[...] = h
        q = _dot(h, wq_ref[...]).astype(BF16)
        q_ref[...] = q
        for hh in range(N_MEM_HEADS):
            sl = slice(hh * hd, (hh + 1) * hd)
            s = _dot_nt(q[:, sl], k_ref[:, sl]) * (1.0 / 16.0)
            e = jnp.exp(s - jnp.max(s, axis=1, keepdims=True))
            p = e / jnp.sum(e, axis=1, keepdims=True)
            o_ref[:, sl] = _dot(p.astype(BF16), v_ref[:, sl]).astype(BF16)
        x2_ref[...] = xv + _dot(o_ref[...], wo_ref[...])

    tok = pl.BlockSpec((tb, d), lambda i: (i, 0))
    full = pl.BlockSpec((d, d), lambda i: (0, 0))
    return pl.pallas_call(
        body, name="xattn_fwd", grid=(t // tb,),
        in_specs=[tok, pl.BlockSpec((1, d), lambda i: (0, 0)), full,
                  pl.BlockSpec((m, d), lambda i: (0, 0)), pl.BlockSpec((m, d), lambda i: (0, 1)), full],
        out_specs=[tok] * 4,
        out_shape=[jax.ShapeDtypeStruct((t, d), F32)] + [jax.ShapeDtypeStruct((t, d), BF16)] * 3,
        compiler_params=_params(("parallel",), 48),
    )(x1, g, w_q, kv, kv, w_o)


def _xattn_bwd(dx2, x1, g, q, w_q, kv, w_o, *, tb, rider=None):
    t, d = x1.shape
    hd = d // N_MEM_HEADS
    m = kv.shape[0]

    def body(dx2_ref, x_ref, g_ref, q_ref, wq_ref, k_ref, v_ref, wo_ref,
             dx1_ref, dx1b_ref, dq_ref, dk_ref, dv_ref, gg_ref):
        i = pl.program_id(0)

        @pl.when(i == 0)
        def _():
            dk_ref[...] = jnp.zeros_like(dk_ref)
            dv_ref[...] = jnp.zeros_like(dv_ref)

        dx2 = dx2_ref[...]
        do = _dot_nt(dx2.astype(BF16), wo_ref[...]).astype(BF16)
        for hh in range(N_MEM_HEADS):
            sl = slice(hh * hd, (hh + 1) * hd)
            qh, kh, vh, doh = q_ref[:, sl], k_ref[:, sl], v_ref[:, sl], do[:, sl]
            s = _dot_nt(qh, kh) * (1.0 / 16.0)
            e = jnp.exp(s - jnp.max(s, axis=1, keepdims=True))
            p = e / jnp.sum(e, axis=1, keepdims=True)
            dp = _dot_nt(doh, vh)
            ds = (p * (dp - jnp.sum(dp * p, axis=1, keepdims=True)) * (1.0 / 16.0)).astype(BF16)
            dq_ref[:, sl] = _dot(ds, kh).astype(BF16)
            dk_ref[:, sl] += _dot_tn(ds, qh)
            dv_ref[:, sl] += _dot_tn(p.astype(BF16), doh)
        dh = _dot_nt(dq_ref[...], wq_ref[...])
        g_v = g_ref[...]
        _, xh, r = _rms_fwd(x_ref[...], g_v)
        dx1 = dx2 + _rms_bwd(dh, xh, r, g_v)
        dx1_ref[...] = dx1
        dx1b_ref[...] = dx1.astype(BF16)
        part = jnp.sum(dh * xh, axis=0, keepdims=True)

        @pl.when(i == 0)
        def _():
            gg_ref[...] = part

        @pl.when(i != 0)
        def _():
            gg_ref[...] += part

    tok = pl.BlockSpec((tb, d), lambda i: (i, 0))
    full = pl.BlockSpec((d, d), lambda i: (0, 0))
    acc = pl.BlockSpec((m, d), lambda i: (0, 0))
    res, extra = _pcall(
        body, name="xattn_bwd", grid=(t // tb,),
        in_specs=[tok, tok, pl.BlockSpec((1, d), lambda i: (0, 0)), tok, full,
                  pl.BlockSpec((m, d), lambda i: (0, 0)), pl.BlockSpec((m, d), lambda i: (0, 1)), full],
        out_specs=[tok, tok, tok, acc, acc, pl.BlockSpec((1, d), lambda i: (0, 0))],
        out_shape=[jax.ShapeDtypeStruct((t, d), F32), jax.ShapeDtypeStruct((t, d), BF16),
                   jax.ShapeDtypeStruct((t, d), BF16),
                   jax.ShapeDtypeStruct((m, d), F32), jax.ShapeDtypeStruct((m, d), F32),
                   jax.ShapeDtypeStruct((1, d), F32)],
        semantics=("arbitrary",), vmem_mb=48, rider=rider,
    )(dx2, x1, g, q, w_q, kv, kv, w_o)
    return res if rider is None else (res, extra)


def _mlp_down_loss(a, w_down, x2, tgt, g, *, tb):
    t, d = x2.shape
    f = a.shape[1]

    def body(a_ref, w_ref, x_ref, t_ref, g_ref, dx_ref, dxb_ref, loss_ref, gg_ref):
        i = pl.program_id(0)
        av = a_ref[...].astype(F32)
        x3 = x_ref[...] + _dot((av * av).astype(BF16), w_ref[...])
        g_v = g_ref[...]
        out, xh, r = _rms_fwd(x3, g_v)
        err = out - t_ref[...]
        dout = err * (1.0 / d)
        dx = _rms_bwd(dout, xh, r, g_v)
        dx_ref[...] = dx
        dxb_ref[...] = dx.astype(BF16)
        part = jnp.sum(dout * xh, axis=0, keepdims=True)
        lpart = 0.5 * jnp.sum(jnp.mean(err * err, axis=-1, keepdims=True), axis=0, keepdims=True)
        lpart = jnp.broadcast_to(lpart, loss_ref.shape)

        @pl.when(i == 0)
        def _():
            gg_ref[...] = part
            loss_ref[...] = lpart

        @pl.when(i != 0)
        def _():
            gg_ref[...] += part
            loss_ref[...] += lpart

    tok = pl.BlockSpec((tb, d), lambda i: (i, 0))
    return pl.pallas_call(
        body, name="mlp_down_loss", grid=(t // tb,),
        in_specs=[pl.BlockSpec((tb, f), lambda i: (i, 0)), pl.BlockSpec((f, d), lambda i: (0, 0)), tok, tok,
                  pl.BlockSpec((1, d), lambda i: (0, 0))],
        out_specs=[tok, tok, pl.BlockSpec((8, 128), lambda i: (0, 0)), pl.BlockSpec((1, d), lambda i: (0, 0))],
        out_shape=[jax.ShapeDtypeStruct((t, d), F32), jax.ShapeDtypeStruct((t, d), BF16),
                   jax.ShapeDtypeStruct((8, 128), F32), jax.ShapeDtypeStruct((1, d), F32)],
        compiler_params=_params(("arbitrary",), 56),
    )(a, w_down, x2, tgt, g)


def _mlp_dpre(dx3, w_down, a, *, tb, bn):
    t, d = dx3.shape
    f = a.shape[1]

    def body(dx_ref, w_ref, a_ref, o_ref):
        o_ref[...] = (2.0 * a_ref[...].astype(F32) * _dot_nt(dx_ref[...], w_ref[...])).astype(BF16)

    return pl.pallas_call(
        body, name="mlp_dpre", grid=(t // tb, f // bn),
        in_specs=[pl.BlockSpec((tb, d), lambda i, j: (i, 0)), pl.BlockSpec((bn, d), lambda i, j: (j, 0)),
                  pl.BlockSpec((tb, bn), lambda i, j: (i, j))],
        out_specs=pl.BlockSpec((tb, bn), lambda i, j: (i, j)),
        out_shape=jax.ShapeDtypeStruct((t, f), BF16),
        compiler_params=_params(("parallel", "arbitrary"), 48),
    )(dx3, w_down, a)


def _adamw(gsum, w, m, v):
    m_new = ADAM_B1 * m + (1.0 - ADAM_B1) * gsum
    v_new = ADAM_B2 * v + (1.0 - ADAM_B2) * (gsum * gsum)
    m_hat = m_new / (1.0 - ADAM_B1 ** ADAM_STEP)
    v_hat = v_new / (1.0 - ADAM_B2 ** ADAM_STEP)
    delta = -ADAM_LR * (m_hat / (jnp.sqrt(v_hat) + ADAM_EPS) + ADAM_WD * w)
    return delta, m_new, v_new


def _sum_adamw(parts, w, m, v, *, name, tr, rider=None):
    r, c = w.shape

    def body(p_ref, w_ref, m_ref, v_ref, g_ref, d_ref, mo_ref, vo_ref):
        g = p_ref[0].astype(F32)
        for k in range(1, N_DEV):
            g = g + p_ref[k].astype(F32)
        g_ref[...] = g
        d_ref[...], mo_ref[...], vo_ref[...] = _adamw(g, w_ref[...], m_ref[...], v_ref[...])

    blk = pl.BlockSpec((tr, c), lambda i: (i, 0))
    res, extra = _pcall(
        body, name=name, grid=(r // tr,),
        in_specs=[pl.BlockSpec((N_DEV, tr, c), lambda i: (0, i, 0)), blk, blk, blk],
        out_specs=[blk] * 4, out_shape=[jax.ShapeDtypeStruct((r, c), F32)] * 4,
        semantics=("parallel",), vmem_mb=40, rider=rider,
    )(parts, w, m, v)
    return res if rider is None else (res, extra)


def _sum_small(parts):
    _, r, c = parts.shape

    def body(p_ref, o_ref):
        s = p_ref[0]
        for k in range(1, N_DEV):
            s = s + p_ref[k]
        o_ref[...] = s

    return pl.pallas_call(body, name="sum_small", out_shape=jax.ShapeDtypeStruct((r, c), F32))(parts)


def _adamw_small(g, w, m, v):
    def body(g_ref, w_ref, m_ref, v_ref, d_ref, mo_ref, vo_ref):
        d_ref[...], mo_ref[...], vo_ref[...] = _adamw(g_ref[...], w_ref[...], m_ref[...], v_ref[...])

    return pl.pallas_call(body, name="adamw_small", out_shape=[jax.ShapeDtypeStruct(g.shape, F32)] * 3)(g, w, m, v)


def _head_sum_matrix():
    r = lax.broadcasted_iota(jnp.int32, (512, 512), 0) // HEAD_DIM
    c = lax.broadcasted_iota(jnp.int32, (512, 512), 1) // HEAD_DIM
    return (r == c).astype(BF16)


_SHARD_AXIS = dict(w_in=1, w_out=0, w_q=0, w_kv=1, w_o=0, w_up=1, w_down=0, conv_w=None, small=None)


class _Weights:
    def __init__(self, full, shards=None):
        self.full = dict(full)
        self.shards = shards

    def rider(self, names):
        if self.shards is None:
            return None
        return _Gather([self.shards[n] for n in names], [_SHARD_AXIS[n] for n in names])

    def arrived(self, names, gathered):
        if gathered is not None:
            for n, g in zip(names, gathered):
                self.full[n] = g.transpose(1, 0, 2).reshape(g.shape[1], -1) if n == "conv_w" else g

    def __getitem__(self, name):
        return self.full[name]


class _Grads:
    def __init__(self, distributed):
        self.distributed = distributed
        self.local = {}
        self.received = {}

    def add(self, name, g):
        self.local[name] = g

    def rider(self, names, pieces=None):
        if not self.distributed:
            return None
        into = [self.received.get(n) for n in names]
        return _Exchange([self.local[n] for n in names], [_SHARD_AXIS[n] for n in names], pieces, into)

    def arrived(self, names, received):
        if received is not None:
            for n, r in zip(names, received):
                self.received[n] = r


def _ride(fn, *args, rider=None, **kw):
    if rider is None:
        return fn(*args, **kw), None
    return fn(*args, rider=rider, **kw)


def _local_step(x, mem, tgt, gains, weights, grads):
    names = ["w_in", "conv_w"]
    (x, tgt), got = _ride(_reorder, [x, tgt], False, "reorder_in", rider=weights.rider(names))
    weights.arrived(names, got)
    w_in, cw = weights["w_in"], weights["conv_w"]

    names = ["w_out", "w_kv"]
    (proj, h1), got = _ride(_norm_matmul, x, gains["g_mix"], w_in, name="proj", out_dtype=F32, tb=1024, bn=768,
                            save_h=True, rider=weights.rider(names))
    weights.arrived(names, got)
    names = ["w_q", "w_o", "w_up"]
    (attn, *lses), got = _ride(_attention_fwd, proj, rider=weights.rider(names))
    weights.arrived(names, got)
    x1, merged = _mixer_fwd(x, attn, proj, cw, gains["g_attn_out"], gains["g_conv_out"], weights["w_out"])
    kv, mem_n = _norm_matmul(mem, gains["g_mem"], weights["w_kv"], name="mem_kv", out_dtype=BF16, tb=mem.shape[0],
                             bn=1024, save_h=True)
    x2, h2, qm, om = _xattn_fwd(x1, gains["g_xattn"], weights["w_q"], kv, weights["w_o"], tb=512)
    w_up = weights["w_up"]
    (a, h3), got = _ride(_norm_matmul, x2, gains["g_mlp"], w_up, name="mlp_up", out_dtype=BF16, tb=1024, bn=1024,
                         relu=True, save_h=True, rider=weights.rider(["w_down"]))
    weights.arrived(["w_down"], got)
    w_down = weights["w_down"]
    dx3, dx3b, loss_blk, gg_final = _mlp_down_loss(a, w_down, x2, tgt, gains["g_final"], tb=256)

    def sending(sends, fn, *args, **kw):
        names = [s[0] for s in sends]
        res, got = _ride(fn, *args, rider=grads.rider(names, [s[1:] for s in sends]), **kw)
        grads.arrived(names, got)
        return res

    dpre = _mlp_dpre(dx3b, w_down, a, tb=1024, bn=1024)
    grads.add("w_down", _matmul_tn(a, dx3b, name="grad_w_down", bm=512, bn=1024, square_a=True))
    grads.add("w_up", sending([("w_down", 0, 3, 8)], _matmul_tn, h3, dpre, name="grad_w_up", bm=1024, bn=512))
    dx2, dx2b, gg_mlp = sending([("w_down", 3, 4, 8)], _matmul_nt_normbwd, dpre, w_up, x2, gains["g_mlp"], dx3,
                                name="mlp_dx", tb=512, also_bf16=True)

    grads.add("w_o", sending([("w_down", 7, 1, 8)], _matmul_tn, om, dx2b, name="grad_w_o", bm=1024, bn=512))
    dx1, dx1b, dqm, dk, dv, gg_xattn = sending([("w_up", 0, 3, 8)], _xattn_bwd, dx2, x1, gains["g_xattn"], qm,
                                               weights["w_q"], kv, weights["w_o"], tb=512)
    grads.add("w_q", sending([("w_o", 0, 1, 2)], _matmul_tn, h2, dqm, name="grad_w_q", bm=1024, bn=512))
    dkv = jnp.concatenate([dk, dv], axis=1).astype(BF16)
    grads.add("w_kv", _matmul_tn(mem_n, dkv, name="grad_w_kv", bm=1024, bn=1024))
    _, gg_mem = _matmul_nt_normbwd(dkv, weights["w_kv"], mem, gains["g_mem"], None, name="mem_dx", tb=mem.shape[0])

    grads.add("w_out", sending([("w_o", 1, 1, 2)], _matmul_tn, merged, dx1b, name="grad_w_out", bm=1024, bn=512))
    dattn, dsum, dy, gg_attn, gg_conv = sending(
        [("w_up", 3, 3, 8)], _mixer_bwd, dx1, attn, proj, cw, gains["g_attn_out"], gains["g_conv_out"],
        weights["w_out"], _head_sum_matrix())
    dproj, gcw = sending([("w_up", 6, 1, 8)], _conv_bwd, dy, proj, cw)
    dproj = sending([("w_up", 7, 1, 8), ("w_q", 0, 1, 1), ("w_kv", 0, 1, 1)], _attention_bwd, proj, dattn, dsum,
                    lses, dproj)
    grads.add("w_in", sending([("w_out", 0, 1, 1)], _matmul_tn, h1, dproj, name="grad_w_in", bm=1024, bn=512))
    grad_x, gg_mix = sending([("w_in", 0, 4, 8)], _matmul_nt_normbwd, dproj, w_in, x, gains["g_mix"], dx1,
                             name="mixer_dx", tb=512)

    rows = [gg_mix, gg_xattn, gg_mem, gg_mlp, gg_final, jnp.concatenate([gg_attn, gg_conv], axis=1),
            jnp.pad(gcw[0:3], ((0, 0), (0, 512))), jnp.pad(loss_blk[0:1, 0:1], ((0, 6), (0, 1023)))]
    grads.add("small", jnp.concatenate(rows, axis=0))
    (grad_x,) = _reorder([grad_x], True, "reorder_out")
    return grad_x


_BIG = ("w_in", "w_out", "w_q", "w_kv", "w_o", "w_up", "w_down")
_GAIN_ROWS = ("g_mix", "g_xattn", "g_mem", "g_mlp", "g_final")


def _pack_small(vals, conv):
    rows = [vals[k].reshape(1, -1) for k in _GAIN_ROWS]
    rows.append(jnp.concatenate([vals["g_attn_out"].reshape(1, -1), vals["g_conv_out"].reshape(1, -1)], axis=1))
    flat = conv.reshape(1, -1)
    rows.append(jnp.pad(flat, ((0, 0), (0, 1024 - flat.shape[1]))))
    rows.append(jnp.zeros((1, 1024), F32))
    return jnp.concatenate(rows, axis=0)


def kernel(x, mem, g_mix, w_in, conv_w, g_attn_out, g_conv_out, w_out, g_xattn, g_mem, w_q_mem, w_kv_mem, w_o_mem, g_mlp, w_up, w_down, g_final, loss_target, m_g_mix, m_w_in, m_conv_w, m_g_attn_out, m_g_conv_out, m_w_out, m_g_xattn, m_g_mem, m_w_q_mem, m_w_kv_mem, m_w_o_mem, m_g_mlp, m_w_up, m_w_down, m_g_final, v_g_mix, v_w_in, v_conv_w, v_g_attn_out, v_g_conv_out, v_w_out, v_g_xattn, v_g_mem, v_w_q_mem, v_w_kv_mem, v_w_o_mem, v_g_mlp, v_w_up, v_w_down, v_g_final):
    d = x.shape[-1]
    me = 4 * lax.axis_index("x") + 2 * lax.axis_index("y") + lax.axis_index("c")
    w_shards = dict(w_in=w_in, w_out=w_out, w_q=w_q_mem, w_kv=w_kv_mem, w_o=w_o_mem, w_up=w_up, w_down=w_down)
    m_shards = dict(w_in=m_w_in, w_out=m_w_out, w_q=m_w_q_mem, w_kv=m_w_kv_mem, w_o=m_w_o_mem, w_up=m_w_up,
                    w_down=m_w_down)
    v_shards = dict(w_in=v_w_in, w_out=v_w_out, w_q=v_w_q_mem, w_kv=v_w_kv_mem, w_o=v_w_o_mem, w_up=v_w_up,
                    w_down=v_w_down)
    gains = dict(g_mix=g_mix, g_attn_out=g_attn_out, g_conv_out=g_conv_out, g_xattn=g_xattn, g_mem=g_mem,
                 g_mlp=g_mlp, g_final=g_final)
    gains2 = {k: v.reshape(1, -1) for k, v in gains.items()}

    shards = {k: w_shards[k].astype(BF16) for k in _BIG}
    shards["conv_w"] = conv_w
    grads = _Grads(distributed=True)
    grad_x = _local_step(x[0], mem[0], loss_target[0], gains2, _Weights({}, shards), grads)

    outs = {}
    tiles = dict(w_in=256, w_out=128, w_q=128, w_kv=256, w_o=128, w_up=256, w_down=256)
    for k in ("w_up", "w_down", "w_out", "w_q", "w_kv", "w_o", "w_in"):
        last = ["w_in", "small"]
        rider = grads.rider(last, [(4, 4, 8), (0, 1, 1)]) if k == "w_up" else None
        outs[k], got = _ride(_sum_adamw, grads.received[k], w_shards[k], m_shards[k], v_shards[k],
                             name=f"adamw_{k}", tr=tiles[k], rider=rider)
        grads.arrived(last, got)
    small_received = grads.received["small"]

    ssum = _sum_small(small_received)
    loss = ssum[9, 0]
    g_small = {k: ssum[i] for i, k in enumerate(_GAIN_ROWS)}
    g_small["g_attn_out"] = ssum[5, 0:512]
    g_small["g_conv_out"] = ssum[5, 512:1024]
    g_conv = lax.dynamic_slice_in_dim(ssum[6:9, 0:512], me * 64, 64, axis=1)
    m_small = dict(g_mix=m_g_mix, g_attn_out=m_g_attn_out, g_conv_out=m_g_conv_out, g_xattn=m_g_xattn,
                   g_mem=m_g_mem, g_mlp=m_g_mlp, g_final=m_g_final)
    v_small = dict(g_mix=v_g_mix, g_attn_out=v_g_attn_out, g_conv_out=v_g_conv_out, g_xattn=v_g_xattn,
                   g_mem=v_g_mem, g_mlp=v_g_mlp, g_final=v_g_final)
    packed = [_pack_small(g_small, g_conv), _pack_small(gains, conv_w), _pack_small(m_small, m_conv_w),
              _pack_small(v_small, v_conv_w)]
    upd = _adamw_small(*packed)

    def unpack(p):
        res = {k: p[i] for i, k in enumerate(_GAIN_ROWS)}
        res["g_attn_out"] = p[5, 0:512]
        res["g_conv_out"] = p[5, 512:1024]
        res["conv_w"] = p[6, 0:192].reshape(3, 64)
        return res

    g_small["conv_w"] = g_conv
    small_out = [g_small] + [unpack(p) for p in upd]
    names = {"g_mix": "g_mix", "w_in": "w_in", "conv_w": "conv_w", "g_attn_out": "g_attn_out",
             "g_conv_out": "g_conv_out", "w_out": "w_out", "g_xattn": "g_xattn", "g_mem": "g_mem",
             "w_q_mem": "w_q", "w_kv_mem": "w_kv", "w_o_mem": "w_o", "g_mlp": "g_mlp", "w_up": "w_up",
             "w_down": "w_down", "g_final": "g_final"}
    result = [loss, grad_x[None]]
    for which in range(4):
        for key in names.values():
            result.append(outs[key][which] if key in outs else small_out[which][key])
    return tuple(result)
```

```python
import math

import jax
import jax.numpy as jnp
from jax import lax
from jax.experimental import pallas as pl
from jax.experimental.pallas import tpu as pltpu

F32 = jnp.float32
BF16 = jnp.bfloat16
NORM_EPS = 1e-6
NEG_INF = -1e30
N_DEV = 8
BLK = 128
HEAD_DIM = 64
N_MEM_HEADS = 4
ADAM_LR = 0.001
ADAM_B1 = 0.9
ADAM_B2 = 0.999
ADAM_EPS = 1e-08
ADAM_WD = 0.01
ADAM_STEP = 10
MESH = pl.DeviceIdType.MESH
ANY = pl.BlockSpec(memory_space=pl.ANY)


def _dot(a, b):
    return jnp.dot(a, b, preferred_element_type=F32)


def _dot_nt(a, b):
    return lax.dot_general(a, b, (((1,), (1,)), ((), ())), preferred_element_type=F32)


def _dot_tn(a, b):
    return lax.dot_general(a, b, (((0,), (0,)), ((), ())), preferred_element_type=F32)


def _params(semantics, vmem_mb):
    return pltpu.CompilerParams(dimension_semantics=semantics, vmem_limit_bytes=vmem_mb << 20)


def _rms_fwd(x, g):
    r = lax.rsqrt(jnp.mean(x * x, axis=-1, keepdims=True) + NORM_EPS)
    xh = x * r
    return xh * g, xh, r


def _rms_bwd(dy, xh, r, g):
    gy = dy * g
    return r * (gy - xh * jnp.mean(xh * gy, axis=-1, keepdims=True))


def _position():
    x, y, c = lax.axis_index("x"), lax.axis_index("y"), lax.axis_index("c")
    return x, y, c


def _block_of(ref, j, axis, shard_shape):
    r, c = shard_shape
    if axis is None:
        return ref.at[j]
    if axis == 0:
        return ref.at[pl.ds(j * r, r), :]
    return ref.at[:, pl.ds(j * c, c)]


class _Gather:
    has_mid = True
    alias_pairs = ()

    def __init__(self, shards, axes, late=False):
        self.arrays = list(shards)
        self.axes = list(axes)
        self.late = late
        self.n = len(self.arrays)

    def out_shape(self):
        res = []
        for s, axis in zip(self.arrays, self.axes):
            r, c = s.shape
            shape = (N_DEV, r, c) if axis is None else (N_DEV * r, c) if axis == 0 else (r, N_DEV * c)
            res.append(jax.ShapeDtypeStruct(shape, s.dtype))
        return res

    def scratch(self):
        return [pltpu.SemaphoreType.DMA((self.n, 7)), pltpu.SemaphoreType.DMA((self.n, 7)),
                pltpu.SemaphoreType.DMA((self.n,))]

    def _ctx(self, ins, outs, sems):
        send_sems, recv_sems, local_sems = sems
        x, y, c = _position()
        me, sibling = (x, y, c), (x, y, 1 - c)
        chips = [(1 - x, y), (x, 1 - y), (1 - x, 1 - y)]

        def lin(px, py, pc):
            return 4 * px + 2 * py + pc

        def place(a, block):
            return _block_of(outs[a], lin(*block), self.axes[a], self.arrays[a].shape)

        def copy(a, k, block, to, src=None):
            dst = place(a, block)
            return pltpu.make_async_remote_copy(
                src_ref=dst if src is None else src, dst_ref=dst,
                send_sem=send_sems.at[a, k], recv_sem=recv_sems.at[a, k],
                device_id=to, device_id_type=MESH)

        def mine():
            return [pltpu.make_async_copy(ins[a], place(a, me), local_sems.at[a]) for a in range(self.n)]

        def first():
            res = []
            for a in range(self.n):
                res.append(copy(a, 0, me, sibling, src=ins[a]))
                res += [copy(a, 1 + j, me, (*chip, c), src=ins[a]) for j, chip in enumerate(chips)]
            return res

        return c, me, sibling, chips, copy, mine, first

    def start(self, ins, outs, sems):
        _, _, _, _, _, mine, first = self._ctx(ins, outs, sems)
        for cp in mine() + first():
            cp.start()

    def mid(self, ins, outs, sems):
        c, me, sibling, chips, copy, _, _ = self._ctx(ins, outs, sems)
        for j, chip in enumerate(chips):
            for a in range(self.n):
                copy(a, 1 + j, (*chip, c), me).wait_recv()
                copy(a, 4 + j, (*chip, c), sibling).start()

    def finish(self, ins, outs, sems):
        c, me, sibling, chips, copy, mine, first = self._ctx(ins, outs, sems)
        for a in range(self.n):
            copy(a, 0, sibling, me).wait_recv()
            for j, chip in enumerate(chips):
                copy(a, 4 + j, (*chip, 1 - c), me).wait_recv()
        for cp in first():
            cp.wait_send()
        for j, chip in enumerate(chips):
            for a in range(self.n):
                copy(a, 4 + j, (*chip, c), sibling).wait_send()
        for cp in mine():
            cp.wait()


class _Exchange:
    has_mid = False

    def __init__(self, parts, axes, pieces=None, into=None):
        self.n = len(parts)
        self.axes = list(axes)
        self.pieces = list(pieces or [(0, 1, 1)] * self.n)
        into = list(into or [None] * self.n)
        kept = [a for a in range(self.n) if into[a] is not None]
        self.arrays = list(parts) + [into[a] for a in kept]
        self.alias_pairs = [(self.n + i, a) for i, a in enumerate(kept)]

    def _piece(self, a):
        r, c = self.arrays[a].shape
        axis = self.axes[a]
        return (r, c) if axis is None else (r // N_DEV, c) if axis == 0 else (r, c // N_DEV)

    def _rows(self, a):
        first, count, of = self.pieces[a]
        unit = self._piece(a)[0] // of
        return pl.ds(first * unit, count * unit)

    def out_shape(self):
        return [jax.ShapeDtypeStruct((N_DEV,) + self._piece(a), self.arrays[a].dtype) for a in range(self.n)]

    def scratch(self):
        return [pltpu.SemaphoreType.DMA((self.n, 7)), pltpu.SemaphoreType.DMA((self.n, 7)),
                pltpu.SemaphoreType.DMA((self.n,))]

    def _ctx(self, ins, outs, sems):
        send_sems, recv_sems, local_sems = sems
        x, y, c = _position()
        me = 4 * x + 2 * y + c

        def src(a, j):
            block = ins[a] if self.axes[a] is None else _block_of(ins[a], j, self.axes[a], self._piece(a))
            return block.at[self._rows(a), :]

        def dst(a, j):
            return outs[a].at[j, self._rows(a), :]

        def local():
            return [pltpu.make_async_copy(src(a, me), dst(a, me), local_sems.at[a]) for a in range(self.n)]

        def remote(inbound):
            res = []
            for a in range(self.n):
                for k in range(1, N_DEV):
                    peer = (1 - x if k & 4 else x, 1 - y if k & 2 else y, 1 - c if k & 1 else c)
                    plin = 4 * peer[0] + 2 * peer[1] + peer[2]
                    res.append(pltpu.make_async_remote_copy(
                        src_ref=src(a, plin), dst_ref=dst(a, plin if inbound else me),
                        send_sem=send_sems.at[a, k - 1], recv_sem=recv_sems.at[a, k - 1],
                        device_id=peer, device_id_type=MESH))
            return res

        return local, remote

    def start(self, ins, outs, sems):
        local, remote = self._ctx(ins, outs, sems)
        for cp in local() + remote(False):
            cp.start()

    def finish(self, ins, outs, sems):
        local, remote = self._ctx(ins, outs, sems)
        for cp in remote(True):
            cp.wait_recv()
        for cp in remote(False):
            cp.wait_send()
        for cp in local():
            cp.wait()


def _comm_call(rider, name):
    n_in, n_out = len(rider.arrays), len(rider.out_shape())

    def body(*refs):
        ins, outs, sems = refs[:n_in], refs[n_in:n_in + n_out], refs[n_in + n_out:]
        rider.start(ins, outs, sems)
        if rider.has_mid:
            rider.mid(ins, outs, sems)
        rider.finish(ins, outs, sems)

    return pl.pallas_call(
        body, name=name, out_shape=rider.out_shape(),
        in_specs=[ANY] * n_in, out_specs=[ANY] * n_out, scratch_shapes=rider.scratch(),
        input_output_aliases=dict(rider.alias_pairs),
    )(*rider.arrays)


def _pcall(body, *, name, grid, in_specs, out_specs, out_shape, scratch_shapes=(), semantics, vmem_mb, rider=None,
           aliases=None):
    in_specs, out_specs, out_shape = list(in_specs), list(out_specs), list(out_shape)
    scratch_shapes = list(scratch_shapes)
    aliases = dict(aliases or {})
    if rider is None:
        call = pl.pallas_call(body, name=name, grid=grid, in_specs=in_specs, out_specs=out_specs,
                              out_shape=out_shape, scratch_shapes=scratch_shapes, input_output_aliases=aliases,
                              compiler_params=_params(semantics, vmem_mb))
        return lambda *args: (list(call(*args)), None)
    n_in, n_out, n_scr = len(in_specs), len(out_specs), len(scratch_shapes)
    r_in, r_shapes = len(rider.arrays), rider.out_shape()
    r_out = len(r_shapes)
    aliases.update({n_in + i: n_out + o for i, o in rider.alias_pairs})
    total = math.prod(grid)
    mid_step = total - 1 if rider.has_mid and rider.late else (3 * total) // 4

    def wrapped(*refs):
        bounds = [0, n_in, r_in, n_out, r_out, n_scr]
        for i in range(1, len(bounds)):
            bounds[i] += bounds[i - 1]
        a, ra, o, ro, s = (refs[bounds[i]:bounds[i + 1]] for i in range(5))
        rs = refs[bounds[5]:]
        step = pl.program_id(0)
        for k in range(1, len(grid)):
            step = step * grid[k] + pl.program_id(k)
        pl.when(step == 0)(lambda: rider.start(ra, ro, rs))
        body(*a, *o, *s)
        if rider.has_mid:
            pl.when(step == mid_step)(lambda: rider.mid(ra, ro, rs))
        pl.when(step == total - 1)(lambda: rider.finish(ra, ro, rs))

    call = pl.pallas_call(
        wrapped, name=name, grid=grid, in_specs=in_specs + [ANY] * r_in, out_specs=out_specs + [ANY] * r_out,
        out_shape=out_shape + r_shapes, scratch_shapes=scratch_shapes + rider.scratch(),
        input_output_aliases=aliases, compiler_params=_params(("arbitrary",) * len(grid), vmem_mb))

    def run(*args):
        res = call(*args, *rider.arrays)
        return list(res[:n_out]), list(res[n_out:])

    return run


def _norm_matmul(x, g, w, *, name, out_dtype, tb, bn, relu=False, save_h=False, rider=None):
    t, d = x.shape
    n = w.shape[1]

    def body(x_ref, g_ref, w_ref, o_ref, *rest):
        h_scr = rest[-1]

        @pl.when(pl.program_id(1) == 0)
        def _():
            h = _rms_fwd(x_ref[...], g_ref[...])[0].astype(BF16)
            h_scr[...] = h
            if save_h:
                rest[0][...] = h

        acc = _dot(h_scr[...], w_ref[...])
        if relu:
            acc = jnp.maximum(acc, 0.0)
        o_ref[...] = acc.astype(out_dtype)

    out_shape = [jax.ShapeDtypeStruct((t, n), out_dtype)]
    out_specs = [pl.BlockSpec((tb, bn), lambda i, j: (i, j))]
    if save_h:
        out_shape.append(jax.ShapeDtypeStruct((t, d), BF16))
        out_specs.append(pl.BlockSpec((tb, d), lambda i, j: (i, 0)))
    res, extra = _pcall(
        body, name=name, grid=(t // tb, n // bn),
        in_specs=[pl.BlockSpec((tb, d), lambda i, j: (i, 0)),
                  pl.BlockSpec((1, d), lambda i, j: (0, 0)),
                  pl.BlockSpec((d, bn), lambda i, j: (0, j))],
        out_specs=out_specs, out_shape=out_shape,
        scratch_shapes=[pltpu.VMEM((tb, d), BF16)],
        semantics=("parallel", "arbitrary"), vmem_mb=48, rider=rider,
    )(x, g, w)
    res = res if save_h else res[0]
    return res if rider is None else (res, extra)


def _matmul_nt_normbwd(dy, w, x, g, dres, *, name, tb, also_bf16=False, rider=None):
    t, d = x.shape
    stacked = dy.ndim == 3
    has_res = dres is not None

    def body(dy_ref, w_ref, x_ref, g_ref, *rest):
        rest = list(rest)
        dres_ref = rest.pop(0) if has_res else None
        dx_ref = rest.pop(0)
        dxb_ref = rest.pop(0) if also_bf16 else None
        gg_ref = rest.pop(0)
        i = pl.program_id(0)
        if stacked:
            kb = dy_ref.shape[2]
            dh = _dot_nt(dy_ref[0], w_ref[:, 0:kb])
            for s in range(1, dy_ref.shape[0]):
                dh = dh + _dot_nt(dy_ref[s], w_ref[:, s * kb:(s + 1) * kb])
        else:
            dh = _dot_nt(dy_ref[...], w_ref[...])
        g_v = g_ref[...]
        _, xh, r = _rms_fwd(x_ref[...], g_v)
        dx = _rms_bwd(dh, xh, r, g_v)
        if has_res:
            dx = dx + dres_ref[...]
        dx_ref[...] = dx
        if also_bf16:
            dxb_ref[...] = dx.astype(BF16)
        part = jnp.sum(dh * xh, axis=0, keepdims=True)

        @pl.when(i == 0)
        def _():
            gg_ref[...] = part

        @pl.when(i != 0)
        def _():
            gg_ref[...] += part

    tok = pl.BlockSpec((tb, d), lambda i: (i, 0))
    row = pl.BlockSpec((1, d), lambda i: (0, 0))
    if stacked:
        dy_spec = pl.BlockSpec((dy.shape[0], tb, dy.shape[2]), lambda i: (0, i, 0))
    else:
        dy_spec = pl.BlockSpec((tb, dy.shape[1]), lambda i: (i, 0))
    in_specs = [dy_spec, pl.BlockSpec(w.shape, lambda i: (0, 0)), tok, row]
    args = [dy, w, x, g]
    if has_res:
        in_specs.append(tok)
        args.append(dres)
    out_specs = [tok] + ([tok] if also_bf16 else []) + [row]
    out_shape = ([jax.ShapeDtypeStruct((t, d), F32)] + ([jax.ShapeDtypeStruct((t, d), BF16)] if also_bf16 else [])
                 + [jax.ShapeDtypeStruct((1, d), F32)])
    res, extra = _pcall(
        body, name=name, grid=(t // tb,), in_specs=in_specs, out_specs=out_specs, out_shape=out_shape,
        semantics=("arbitrary",), vmem_mb=56, rider=rider,
    )(*args)
    return res if rider is None else (res, extra)


def _matmul_tn(a, b, *, name, bm, bn, square_a=False, rider=None):
    t, m = a.shape
    stacked = b.ndim == 3
    n = b.shape[0] * bn if stacked else b.shape[1]

    def body(a_ref, b_ref, o_ref):
        av = a_ref[...]
        if square_a:
            av = av.astype(F32)
            av = (av * av).astype(BF16)
        o_ref[...] = _dot_tn(av, b_ref[...]).astype(BF16)

    res, extra = _pcall(
        body, name=name, grid=(m // bm, n // bn),
        in_specs=[pl.BlockSpec((t, bm), lambda i, j: (0, i)),
                  pl.BlockSpec((None, t, bn), lambda i, j: (j, 0, 0)) if stacked
                  else pl.BlockSpec((t, bn), lambda i, j: (0, j))],
        out_specs=[pl.BlockSpec((bm, bn), lambda i, j: (i, j))], out_shape=[jax.ShapeDtypeStruct((m, n), BF16)],
        semantics=("parallel", "parallel"), vmem_mb=56, rider=rider,
    )(a, b)
    return res[0] if rider is None else (res[0], extra)


N_RES = 16
SEG = 128
HALF = N_RES * SEG
TI = 32


def _x4(a):
    return a.reshape(a.shape[0] // HALF, N_RES, SEG, a.shape[1])


def _reorder(arrays, inverse, name, rider=None):
    t, c = arrays[0].shape
    n = len(arrays)
    n_i = SEG // TI
    natural = pl.BlockSpec((TI * N_RES, c), lambda s: (s, 0))
    major = pl.BlockSpec((1, N_RES, TI, c), lambda s: (s // n_i, 0, s % n_i, 0))

    def body(*refs):
        scr = refs[-1]
        for i_ref, o_ref in zip(refs[:n], refs[n:2 * n]):
            for cb in range(c // BLK):
                cols = slice(cb * BLK, (cb + 1) * BLK)
                slab = scr.at[cb]
                if inverse:
                    for r in range(N_RES):
                        slab[pl.ds(r, TI, stride=N_RES), :] = i_ref[0, r, :, cols]
                    o_ref[:, cols] = slab[...]
                else:
                    slab[...] = i_ref[:, cols]
                    for r in range(N_RES):
                        o_ref[0, r, :, cols] = slab[pl.ds(r, TI, stride=N_RES), :]

    shape4 = (t // HALF, N_RES, SEG, c)
    res, extra = _pcall(
        body, name=name, grid=(t // (TI * N_RES),),
        in_specs=[major if inverse else natural] * n, out_specs=[natural if inverse else major] * n,
        out_shape=[jax.ShapeDtypeStruct((t, c) if inverse else shape4, F32)] * n,
        scratch_shapes=[pltpu.VMEM((c // BLK, TI * N_RES, BLK), F32)],
        semantics=("parallel",), vmem_mb=32, rider=rider,
    )(*[_x4(a) if inverse else a for a in arrays])
    res = [r.reshape(t, c) for r in res]
    return res if rider is None else (res, extra)


_PATTERNS = ((1, 16, 8, SEG), (4, 4, 32, 4 * SEG), (16, 1, SEG, 0))
_FIRST = {1: 1, 4: 4, 16: 16}


def _group_rows(d, g):
    a = g >> 4
    if d == 16:
        base = a * HALF + (g & 15) * SEG
        prev = base - HALF
    elif d == 4:
        c = (g >> 2) & 3
        base = a * HALF + (g & 3) * SEG + c * 32
        prev = jnp.where(c > 0, base - 32, base - HALF + 96)
    else:
        c = g & 15
        base = a * HALF + c * 8
        prev = jnp.where(c > 0, base - 8, base - HALF + 120)
    return base, prev


def _load_rows(ref, base, n, rows, stride):
    parts = [ref[pl.ds(pl.multiple_of(base + j * stride, 8), rows), :] for j in range(n)]
    return parts[0] if n == 1 else jnp.concatenate(parts, axis=0)


def _store_rows(ref, base, val, n, rows, stride, add=False):
    for j in range(n):
        sl = pl.ds(pl.multiple_of(base + j * stride, 8), rows)
        piece = val[j * rows:(j + 1) * rows, :]
        if add:
            ref[sl, :] += piece
        else:
            ref[sl, :] = piece


def _band_bias(n, rows):
    shift = rows.bit_length() - 1
    lq = lax.broadcasted_iota(jnp.int32, (BLK, BLK), 0)
    lk = lax.broadcasted_iota(jnp.int32, (BLK, BLK), 1)
    iq = (lq & (rows - 1)) * n + (lq >> shift)
    ik = (lk & (rows - 1)) * n + (lk >> shift)
    zero = jnp.zeros((BLK, BLK), F32)
    return jnp.where(ik >= iq, zero, NEG_INF), jnp.where(ik <= iq, zero, NEG_INF)


def _set_bias(bias_scr, n, rows):
    prev_b, cur_b = _band_bias(n, rows)
    for half in range(2):
        bias_scr[half * BLK:(half + 1) * BLK, 0:BLK] = prev_b
        bias_scr[half * BLK:(half + 1) * BLK, BLK:2 * BLK] = cur_b


SCALE = 1.0 / math.sqrt(HEAD_DIM)


def _head_consts(value=1.0):
    lane_lo = lax.broadcasted_iota(jnp.int32, (BLK, BLK), 1) < HEAD_DIM
    return lane_lo, [jnp.where(lane_lo, value, 0.0).astype(BF16), jnp.where(lane_lo, 0.0, value).astype(BF16)]


def _stack_heads(v, head_mask):
    return jnp.concatenate([v * head_mask[0], v * head_mask[1]], axis=0)


def _unstack_heads(v2, lane_lo):
    return jnp.where(lane_lo, v2[:BLK], v2[BLK:])


def _rows_per_head(v, lane_lo):
    rolled = pltpu.roll(v, HEAD_DIM, axis=1)
    return jnp.concatenate([jnp.where(lane_lo, v, rolled), jnp.where(lane_lo, rolled, v)], axis=0)


WIDTH = 4


def _loop(lo, hi, fn, width=None):
    if width is None:
        def body(g, carry):
            fn(g)
            return carry

        if hi > lo:
            lax.fori_loop(lo, hi, body, 0)
        return
    while hi > lo:
        trips = (hi - lo) // width
        if trips:
            def body(i, carry, lo=lo, width=width):
                fn([lo + width * i + j for j in range(width)])
                return carry

            lax.fori_loop(0, trips, body, 0)
            lo += trips * width
        width = max(1, width // 2)


def _mix_weights(l1, l2, l3):
    mx = jnp.maximum(jnp.maximum(l1, l2), l3)
    e1, e2, e3 = jnp.exp(l1 - mx), jnp.exp(l2 - mx), jnp.exp(l3 - mx)
    inv = 1.0 / (e1 + e2 + e3)
    return e1 * inv, e2 * inv, e3 * inv


def _attention_fwd(qkv, rider=None):
    t = qkv.shape[0]
    groups = 16 * (t // HALF)

    def body(q_ref, k_ref, v_ref, attn_ref, l1_ref, l2_ref, l3_ref, o_scr, bias_scr):
        lane_lo, q_mask = _head_consts(SCALE)
        l_refs = (l1_ref, l2_ref, l3_ref)
        for p, (d, n, rows, stride) in enumerate(_PATTERNS):
            _set_bias(bias_scr, n, rows)
            o_p, l_p = o_scr.at[p], l_refs[p]

            def block(gs, has_prev):
                at = [_group_rows(d, g) for g in gs]

                def load(ref, b):
                    return _load_rows(ref, b, n, rows, stride).astype(BF16)

                q2 = [_stack_heads(load(q_ref, b), q_mask) for b, _ in at]
                k2 = [load(k_ref, b) for b, _ in at]
                v2 = [load(v_ref, b) for b, _ in at]
                if has_prev:
                    k2 = [jnp.concatenate([load(k_ref, pv), k], axis=0) for (_, pv), k in zip(at, k2)]
                    v2 = [jnp.concatenate([load(v_ref, pv), v], axis=0) for (_, pv), v in zip(at, v2)]
                s = [_dot_nt(q, k) for q, k in zip(q2, k2)]
                s = [x + (bias_scr[...] if has_prev else bias_scr[:, BLK:2 * BLK]) for x in s]
                mx = [jnp.max(x, axis=1, keepdims=True) for x in s]
                e = [jnp.exp(x - m) for x, m in zip(s, mx)]
                den = [jnp.sum(x, axis=1, keepdims=True) for x in e]
                o2 = [_dot(x.astype(BF16), v) * (1.0 / dn) for x, v, dn in zip(e, v2, den)]
                lse2 = [jnp.broadcast_to(m + jnp.log(dn), (2 * BLK, BLK)) for m, dn in zip(mx, den)]
                for (b, _), o, l in zip(at, o2, lse2):
                    _store_rows(o_p, b, _unstack_heads(o, lane_lo), n, rows, stride)
                    _store_rows(l_p, b, _unstack_heads(l, lane_lo), n, rows, stride)

            _loop(0, _FIRST[d], lambda gs: block(gs, False), width=WIDTH)
            _loop(_FIRST[d], groups, lambda gs: block(gs, True), width=WIDTH)

        def mix(i):
            sl = pl.ds(pl.multiple_of(i * 256, 256), 256)
            w = _mix_weights(l1_ref[sl, :], l2_ref[sl, :], l3_ref[sl, :])
            attn_ref[sl, :] = w[0] * o_scr[0, sl, :] + w[1] * o_scr[1, sl, :] + w[2] * o_scr[2, sl, :]

        _loop(0, t // 256, mix)

    def col(c0):
        return pl.BlockSpec((t, BLK), lambda hp: (0, c0 + hp))

    res, extra = _pcall(
        body, name="attention_fwd", grid=(4,), in_specs=[col(0), col(4), col(8)], out_specs=[col(0)] * 4,
        out_shape=[jax.ShapeDtypeStruct((t, 512), F32)] * 4,
        scratch_shapes=[pltpu.VMEM((3, t, BLK), F32), pltpu.VMEM((2 * BLK, 2 * BLK), F32)],
        semantics=("parallel",), vmem_mb=48, rider=rider,
    )(qkv, qkv, qkv)
    return res if rider is None else (res, extra)


def _attention_bwd(qkv, dattn, dsum, lses, dproj, rider=None):
    t = qkv.shape[0]
    groups = 16 * (t // HALF)

    def body(q_ref, k_ref, v_ref, da_ref, ds_ref, l1_ref, l2_ref, l3_ref, kept_ref, out_ref, acc, bias_scr):
        del kept_ref
        lane_lo, head_mask = _head_consts()
        q_mask = _head_consts(SCALE)[1]
        l_refs = (l1_ref, l2_ref, l3_ref)

        def clear(i):
            sl = pl.ds(pl.multiple_of(i * 512, 512), 512)
            for s in range(3):
                acc[s, sl, :] = jnp.zeros((512, BLK), F32)

        _loop(0, t // 512, clear)
        dq_acc, dk_acc, dv_acc = acc.at[0], acc.at[1], acc.at[2]
        for p, (d, n, rows, stride) in enumerate(_PATTERNS):
            _set_bias(bias_scr, n, rows)

            def block(gs, has_prev):
                at = [_group_rows(d, g) for g in gs]

                def load(ref, b):
                    return _load_rows(ref, b, n, rows, stride)

                def put(ref, b, val):
                    _store_rows(ref, b, val, n, rows, stride, add=True)

                def wide(x):
                    return jnp.concatenate([x, x], axis=1) if has_prev else x

                lse = [[load(ref, b) for ref in l_refs] for b, _ in at]
                w = [_mix_weights(*ls)[p] for ls in lse]
                do2 = [_stack_heads((wg * load(da_ref, b)).astype(BF16), head_mask) for wg, (b, _) in zip(w, at)]
                dl2 = [wide(_rows_per_head(wg * load(ds_ref, b), lane_lo)) for wg, (b, _) in zip(w, at)]
                lse2 = [wide(_rows_per_head(ls[p], lane_lo)) for ls in lse]
                q2 = [_stack_heads(load(q_ref, b).astype(BF16), q_mask) for b, _ in at]
                k2 = [load(k_ref, b).astype(BF16) for b, _ in at]
                v2 = [load(v_ref, b).astype(BF16) for b, _ in at]
                if has_prev:
                    k2 = [jnp.concatenate([load(k_ref, pv).astype(BF16), k], axis=0) for (_, pv), k in zip(at, k2)]
                    v2 = [jnp.concatenate([load(v_ref, pv).astype(BF16), v], axis=0) for (_, pv), v in zip(at, v2)]
                s = [_dot_nt(q, k) for q, k in zip(q2, k2)]
                dp = [_dot_nt(do, v) for do, v in zip(do2, v2)]
                pr = [jnp.exp(x + (bias_scr[...] if has_prev else bias_scr[:, BLK:2 * BLK]) - l)
                      for x, l in zip(s, lse2)]
                ds = [(pg * (x - dl)).astype(BF16) for pg, x, dl in zip(pr, dp, dl2)]
                dq2 = [_dot(x, k) * SCALE for x, k in zip(ds, k2)]
                dk2 = [_dot_tn(x, q) for x, q in zip(ds, q2)]
                dv2 = [_dot_tn(pg.astype(BF16), do) for pg, do in zip(pr, do2)]
                for (b, pv), dq, dk, dv in zip(at, dq2, dk2, dv2):
                    put(dq_acc, b, _unstack_heads(dq, lane_lo))
                    if has_prev:
                        put(dk_acc, pv, dk[:BLK])
                        put(dv_acc, pv, dv[:BLK])
                        put(dk_acc, b, dk[BLK:])
                        put(dv_acc, b, dv[BLK:])
                    else:
                        put(dk_acc, b, dk)
                        put(dv_acc, b, dv)

            _loop(0, _FIRST[d], lambda gs: block(gs, False), width=WIDTH)
            _loop(_FIRST[d], groups, lambda gs: block(gs, True), width=WIDTH)

        def emit(i):
            sl = pl.ds(pl.multiple_of(i * 512, 512), 512)
            for s in range(3):
                out_ref[s, sl, :] = acc[s, sl, :].astype(BF16)

        _loop(0, t // 512, emit)

    def col(c0):
        return pl.BlockSpec((t, BLK), lambda hp: (0, c0 + hp))

    res, extra = _pcall(
        body, name="attention_bwd", grid=(4,),
        in_specs=[col(0), col(4), col(8)] + [col(0)] * 5 + [ANY],
        out_specs=[pl.BlockSpec((3, t, BLK), lambda hp: (0, 0, hp))],
        out_shape=[jax.ShapeDtypeStruct(dproj.shape, BF16)],
        scratch_shapes=[pltpu.VMEM((3, t, BLK), F32), pltpu.VMEM((2 * BLK, 2 * BLK), F32)],
        semantics=("parallel",), vmem_mb=56, rider=rider, aliases={8: 0},
    )(qkv, qkv, qkv, dattn, dsum, *lses, dproj)
    return res[0] if rider is None else (res[0], extra)


def _order_specs(t):
    n_i = SEG // TI
    nblk = (t // HALF) * n_i
    per = TI // 8

    def main(c, col=0):
        return pl.BlockSpec((1, N_RES, TI, c), lambda s: (s // n_i, 0, s % n_i, col))

    def before(c, col=0):
        return pl.BlockSpec((1, 2, 8, c), lambda s: (jnp.maximum(s - 1, 0) // n_i, N_RES // 2 - 1,
                                                     (jnp.maximum(s - 1, 0) % n_i) * per + per - 1, col))

    def after(c, col=0):
        return pl.BlockSpec((1, 2, 8, c), lambda s: (jnp.minimum(s + 1, nblk - 1) // n_i, 0,
                                                     (jnp.minimum(s + 1, nblk - 1) % n_i) * per, col))

    return nblk, main, before, after


def _shift_in(v, row_in, up):
    rows = v.shape[0]
    idx = lax.broadcasted_iota(jnp.int32, v.shape, 0)
    fill = jnp.broadcast_to(row_in, v.shape)
    if up:
        return jnp.where(idx == rows - 1, fill, pltpu.roll(v, rows - 1, axis=0))
    return jnp.where(idx == 0, fill, pltpu.roll(v, 1, axis=0))


def _taps_behind(u, before):
    s15 = _shift_in(u[N_RES - 1], before[1, 7:8, :], up=False)
    s14 = _shift_in(u[N_RES - 2], before[0, 7:8, :], up=False)
    m1 = jnp.concatenate([s15[None], u[:N_RES - 1]], axis=0)
    m2 = jnp.concatenate([s14[None], s15[None], u[:N_RES - 2]], axis=0)
    return m1, m2


def _taps_ahead(u, after):
    t0 = _shift_in(u[0], after[0, 0:1, :], up=True)
    t1 = _shift_in(u[1], after[1, 0:1, :], up=True)
    p1 = jnp.concatenate([u[1:], t0[None]], axis=0)
    p2 = jnp.concatenate([u[2:], t0[None], t1[None]], axis=0)
    return p1, p2


def _conv_fwd(gates, before, first, cw):
    bg, cg, xc = gates[..., 0:512], gates[..., 512:1024], gates[..., 1024:1536]
    u = cg * xc
    ub = before[..., 512:1024] * before[..., 1024:1536]
    ub = jnp.where(first, jnp.zeros_like(ub), ub)
    m1, m2 = _taps_behind(u, ub)
    conv = m2 * cw[0:1, :] + m1 * cw[1:2, :] + u * cw[2:3, :]
    return bg, u, m1, m2, conv


def _sum_tokens(v):
    return jnp.sum(jnp.sum(v, axis=0), axis=0, keepdims=True)


def _mixer_fwd(x, attn, gates, cw, g_a, g_c, w_out):
    t, d = x.shape
    nblk, main, before, _ = _order_specs(t)
    rows = N_RES * TI

    def body(x_ref, at_ref, gt_ref, gb_ref, cw_ref, ga_ref, gc_ref, wa_ref, wb_ref, x1_ref, mg_ref):
        an = _rms_fwd(at_ref[0], ga_ref[...])[0].astype(BF16)
        bg, _, _, _, conv = _conv_fwd(gt_ref[0], gb_ref[0], pl.program_id(0) == 0, cw_ref[...])
        cn = _rms_fwd(bg * conv, gc_ref[...])[0].astype(BF16)
        mg_ref[0, :, :, 0:512] = an
        mg_ref[0, :, :, 512:1024] = cn
        y = _dot(an.reshape(rows, 512), wa_ref[...]) + _dot(cn.reshape(rows, 512), wb_ref[...])
        x1_ref[0] = x_ref[0] + y.reshape(N_RES, TI, d)

    const = lambda r, c, i0=0: pl.BlockSpec((r, c), lambda s: (i0, 0))
    x1, merged = pl.pallas_call(
        body, name="mixer_fwd", grid=(nblk,),
        in_specs=[main(d), main(512), main(1536, 1), before(1536, 1), const(3, 512), const(1, 512), const(1, 512),
                  const(512, d), const(512, d, 1)],
        out_specs=[main(d), main(d)],
        out_shape=[jax.ShapeDtypeStruct(_x4(x).shape, F32), jax.ShapeDtypeStruct(_x4(x).shape, BF16)],
        compiler_params=_params(("parallel",), 48),
    )(_x4(x), _x4(attn), _x4(gates), _x4(gates), cw, g_a, g_c, w_out, w_out)
    return x1.reshape(t, d), merged.reshape(t, d)


def _mixer_bwd(dx1, attn, gates, cw, g_a, g_c, w_out, head_sum, rider=None):
    t, d = dx1.shape
    nblk, main, before, _ = _order_specs(t)
    rows = N_RES * TI

    def body(dx_ref, at_ref, gt_ref, gb_ref, cw_ref, ga_ref, gc_ref, wa_ref, wb_ref, hs_ref,
             da_ref, dsum_ref, dy_ref, gga_ref, ggc_ref):
        s = pl.program_id(0)
        dxb = dx_ref[0].reshape(rows, d).astype(BF16)
        dma = _dot_nt(dxb, wa_ref[...]).reshape(N_RES, TI, 512)
        dmc = _dot_nt(dxb, wb_ref[...]).reshape(N_RES, TI, 512)
        attn_v, g_av = at_ref[0], ga_ref[...]
        _, ah, ra = _rms_fwd(attn_v, g_av)
        dattn = _rms_bwd(dma, ah, ra, g_av)
        da_ref[0] = dattn
        z = (dattn * attn_v).reshape(rows, 512)
        hs = hs_ref[...]
        z1 = z.astype(BF16)
        z2 = (z - z1.astype(F32)).astype(BF16)
        dsum_ref[0] = (_dot(z1, hs) + _dot(z2, hs)).reshape(N_RES, TI, 512)
        bg, _, _, _, conv = _conv_fwd(gt_ref[0], gb_ref[0], s == 0, cw_ref[...])
        g_cv = gc_ref[...]
        _, yh, rc = _rms_fwd(bg * conv, g_cv)
        dy_ref[0] = _rms_bwd(dmc, yh, rc, g_cv)
        pa, pc = _sum_tokens(dma * ah), _sum_tokens(dmc * yh)

        @pl.when(s == 0)
        def _():
            gga_ref[...] = pa
            ggc_ref[...] = pc

        @pl.when(s != 0)
        def _():
            gga_ref[...] += pa
            ggc_ref[...] += pc

    const = lambda r, c, i0=0: pl.BlockSpec((r, c), lambda s: (i0, 0))
    shape4 = _x4(attn).shape
    res, extra = _pcall(
        body, name="mixer_bwd", grid=(nblk,),
        in_specs=[main(d), main(512), main(1536, 1), before(1536, 1), const(3, 512), const(1, 512), const(1, 512),
                  const(512, d), const(512, d, 1), const(512, 512)],
        out_specs=[main(512)] * 3 + [const(1, 512), const(1, 512)],
        out_shape=[jax.ShapeDtypeStruct(shape4, F32)] * 3 + [jax.ShapeDtypeStruct((1, 512), F32)] * 2,
        semantics=("arbitrary",), vmem_mb=48, rider=rider,
    )(_x4(dx1), _x4(attn), _x4(gates), _x4(gates), cw, g_a, g_c, w_out, w_out, head_sum)
    res = [r.reshape(t, 512) for r in res[:3]] + res[3:]
    return res if rider is None else (res, extra)


def _conv_bwd(dy, gates, cw, rider=None):
    t = dy.shape[0]
    nblk, main, before, after = _order_specs(t)
    n_i = SEG // TI

    def body(dy_ref, dya_ref, gt_ref, gb_ref, ga_ref, cw_ref, dp_ref, gcw_ref):
        s = pl.program_id(0)
        cw_v, gates_v = cw_ref[...], gt_ref[0]
        bg, u, m1, m2, conv = _conv_fwd(gates_v, gb_ref[0], s == 0, cw_v)
        dy_v = dy_ref[0]
        dconv = dy_v * bg
        dca = dya_ref[0] * ga_ref[0][..., 0:512]
        dca = jnp.where(s == nblk - 1, jnp.zeros_like(dca), dca)
        p1, p2 = _taps_ahead(dconv, dca)
        du = dconv * cw_v[2:3, :] + p1 * cw_v[1:2, :] + p2 * cw_v[0:1, :]
        dp_ref[0, 0] = (dy_v * conv).astype(BF16)
        dp_ref[1, 0] = (du * gates_v[..., 1024:1536]).astype(BF16)
        dp_ref[2, 0] = (du * gates_v[..., 512:1024]).astype(BF16)
        parts = [_sum_tokens(dconv * m2), _sum_tokens(dconv * m1), _sum_tokens(dconv * u)]

        @pl.when(s == 0)
        def _():
            gcw_ref[...] = jnp.zeros_like(gcw_ref)

        for tap in range(3):
            gcw_ref[tap:tap + 1, :] += parts[tap]

    (dproj, gcw), extra = _pcall(
        body, name="conv_bwd", grid=(nblk,),
        in_specs=[main(512), after(512), main(1536, 1), before(1536, 1), after(1536, 1),
                  pl.BlockSpec((3, 512), lambda s: (0, 0))],
        out_specs=[pl.BlockSpec((3, 1, N_RES, TI, 512), lambda s: (1, s // n_i, 0, s % n_i, 0)),
                   pl.BlockSpec((8, 512), lambda s: (0, 0))],
        out_shape=[jax.ShapeDtypeStruct((6, t // HALF, N_RES, SEG, 512), BF16), jax.ShapeDtypeStruct((8, 512), F32)],
        semantics=("arbitrary",), vmem_mb=40, rider=rider,
    )(_x4(dy), _x4(dy), _x4(gates), _x4(gates), _x4(gates), cw)
    res = (dproj.reshape(6, t, 512), gcw)
    return res if rider is None else (res, extra)


def _xattn_fwd(x1, g, w_q, kv, w_o, *, tb):
    t, d = x1.shape
    hd = d // N_MEM_HEADS
    m = kv.shape[0]

    def body(x_ref, g_ref, wq_ref, k_ref, v_ref, wo_ref, x2_ref, h_ref, q_ref, o_ref):
        xv = x_ref[...]
        h = _rms_fwd(xv, g_ref[...])[0].astype(BF16)
        h_ref[...] = h
        q = _dot(h, wq_ref[...]).astype(BF16)
        q_ref[...] = q
        for hh in range(N_MEM_HEADS):
            sl = slice(hh * hd, (hh + 1) * hd)
            s = _dot_nt(q[:, sl], k_ref[:, sl]) * (1.0 / 16.0)
            e = jnp.exp(s - jnp.max(s, axis=1, keepdims=True))
            p = e / jnp.sum(e, axis=1, keepdims=True)
            o_ref[:, sl] = _dot(p.astype(BF16), v_ref[:, sl]).astype(BF16)
        x2_ref[...] = xv + _dot(o_ref[...], wo_ref[...])

    tok = pl.BlockSpec((tb, d), lambda i: (i, 0))
    full = pl.BlockSpec((d, d), lambda i: (0, 0))
    return pl.pallas_call(
        body, name="xattn_fwd", grid=(t // tb,),
        in_specs=[tok, pl.BlockSpec((1, d), lambda i: (0, 0)), full,
                  pl.BlockSpec((m, d), lambda i: (0, 0)), pl.BlockSpec((m, d), lambda i: (0, 1)), full],
        out_specs=[tok] * 4,
        out_shape=[jax.ShapeDtypeStruct((t, d), F32)] + [jax.ShapeDtypeStruct((t, d), BF16)] * 3,
        compiler_params=_params(("parallel",), 48),
    )(x1, g, w_q, kv, kv, w_o)


def _xattn_bwd(dx2, x1, g, q, w_q, kv, w_o, *, tb, rider=None):
    t, d = x1.shape
    hd = d // N_MEM_HEADS
    m = kv.shape[0]

    def body(dx2_ref, x_ref, g_ref, q_ref, wq_ref, k_ref, v_ref, wo_ref,
             dx1_ref, dx1b_ref, dq_ref, dk_ref, dv_ref, gg_ref):
        i = pl.program_id(0)

        @pl.when(i == 0)
        def _():
            dk_ref[...] = jnp.zeros_like(dk_ref)
            dv_ref[...] = jnp.zeros_like(dv_ref)

        dx2 = dx2_ref[...]
        do = _dot_nt(dx2.astype(BF16), wo_ref[...]).astype(BF16)
        for hh in range(N_MEM_HEADS):
            sl = slice(hh * hd, (hh + 1) * hd)
            qh, kh, vh, doh = q_ref[:, sl], k_ref[:, sl], v_ref[:, sl], do[:, sl]
            s = _dot_nt(qh, kh) * (1.0 / 16.0)
            e = jnp.exp(s - jnp.max(s, axis=1, keepdims=True))
            p = e / jnp.sum(e, axis=1, keepdims=True)
            dp = _dot_nt(doh, vh)
            ds = (p * (dp - jnp.sum(dp * p, axis=1, keepdims=True)) * (1.0 / 16.0)).astype(BF16)
            dq_ref[:, sl] = _dot(ds, kh).astype(BF16)
            dk_ref[:, sl] += _dot_tn(ds, qh)
            dv_ref[:, sl] += _dot_tn(p.astype(BF16), doh)
        dh = _dot_nt(dq_ref[...], wq_ref[...])
        g_v = g_ref[...]
        _, xh, r = _rms_fwd(x_ref[...], g_v)
        dx1 = dx2 + _rms_bwd(dh, xh, r, g_v)
        dx1_ref[...] = dx1
        dx1b_ref[...] = dx1.astype(BF16)
        part = jnp.sum(dh * xh, axis=0, keepdims=True)

        @pl.when(i == 0)
        def _():
            gg_ref[...] = part

        @pl.when(i != 0)
        def _():
            gg_ref[...] += part

    tok = pl.BlockSpec((tb, d), lambda i: (i, 0))
    full = pl.BlockSpec((d, d), lambda i: (0, 0))
    acc = pl.BlockSpec((m, d), lambda i: (0, 0))
    res, extra = _pcall(
        body, name="xattn_bwd", grid=(t // tb,),
        in_specs=[tok, tok, pl.BlockSpec((1, d), lambda i: (0, 0)), tok, full,
                  pl.BlockSpec((m, d), lambda i: (0, 0)), pl.BlockSpec((m, d), lambda i: (0, 1)), full],
        out_specs=[tok, tok, tok, acc, acc, pl.BlockSpec((1, d), lambda i: (0, 0))],
        out_shape=[jax.ShapeDtypeStruct((t, d), F32), jax.ShapeDtypeStruct((t, d), BF16),
                   jax.ShapeDtypeStruct((t, d), BF16),
                   jax.ShapeDtypeStruct((m, d), F32), jax.ShapeDtypeStruct((m, d), F32),
                   jax.ShapeDtypeStruct((1, d), F32)],
        semantics=("arbitrary",), vmem_mb=48, rider=rider,
    )(dx2, x1, g, q, w_q, kv, kv, w_o)
    return res if rider is None else (res, extra)


def _mlp_down_loss(a, w_down, x2, tgt, g, *, tb):
    t, d = x2.shape
    f = a.shape[1]

    def body(a_ref, w_ref, x_ref, t_ref, g_ref, dx_ref, dxb_ref, loss_ref, gg_ref):
        i = pl.program_id(0)
        av = a_ref[...].astype(F32)
        x3 = x_ref[...] + _dot((av * av).astype(BF16), w_ref[...])
        g_v = g_ref[...]
        out, xh, r = _rms_fwd(x3, g_v)
        err = out - t_ref[...]
        dout = err * (1.0 / d)
        dx = _rms_bwd(dout, xh, r, g_v)
        dx_ref[...] = dx
        dxb_ref[...] = dx.astype(BF16)
        part = jnp.sum(dout * xh, axis=0, keepdims=True)
        lpart = 0.5 * jnp.sum(jnp.mean(err * err, axis=-1, keepdims=True), axis=0, keepdims=True)
        lpart = jnp.broadcast_to(lpart, loss_ref.shape)

        @pl.when(i == 0)
        def _():
            gg_ref[...] = part
            loss_ref[...] = lpart

        @pl.when(i != 0)
        def _():
            gg_ref[...] += part
            loss_ref[...] += lpart

    tok = pl.BlockSpec((tb, d), lambda i: (i, 0))
    return pl.pallas_call(
        body, name="mlp_down_loss", grid=(t // tb,),
        in_specs=[pl.BlockSpec((tb, f), lambda i: (i, 0)), pl.BlockSpec((f, d), lambda i: (0, 0)), tok, tok,
                  pl.BlockSpec((1, d), lambda i: (0, 0))],
        out_specs=[tok, tok, pl.BlockSpec((8, 128), lambda i: (0, 0)), pl.BlockSpec((1, d), lambda i: (0, 0))],
        out_shape=[jax.ShapeDtypeStruct((t, d), F32), jax.ShapeDtypeStruct((t, d), BF16),
                   jax.ShapeDtypeStruct((8, 128), F32), jax.ShapeDtypeStruct((1, d), F32)],
        compiler_params=_params(("arbitrary",), 56),
    )(a, w_down, x2, tgt, g)


def _mlp_dpre(dx3, w_down, a, *, tb, bn):
    t, d = dx3.shape
    f = a.shape[1]

    def body(dx_ref, w_ref, a_ref, o_ref):
        o_ref[...] = (2.0 * a_ref[...].astype(F32) * _dot_nt(dx_ref[...], w_ref[...])).astype(BF16)

    return pl.pallas_call(
        body, name="mlp_dpre", grid=(t // tb, f // bn),
        in_specs=[pl.BlockSpec((tb, d), lambda i, j: (i, 0)), pl.BlockSpec((bn, d), lambda i, j: (j, 0)),
                  pl.BlockSpec((tb, bn), lambda i, j: (i, j))],
        out_specs=pl.BlockSpec((tb, bn), lambda i, j: (i, j)),
        out_shape=jax.ShapeDtypeStruct((t, f), BF16),
        compiler_params=_params(("parallel", "arbitrary"), 48),
    )(dx3, w_down, a)


def _adamw(gsum, w, m, v):
    m_new = ADAM_B1 * m + (1.0 - ADAM_B1) * gsum
    v_new = ADAM_B2 * v + (1.0 - ADAM_B2) * (gsum * gsum)
    m_hat = m_new / (1.0 - ADAM_B1 ** ADAM_STEP)
    v_hat = v_new / (1.0 - ADAM_B2 ** ADAM_STEP)
    delta = -ADAM_LR * (m_hat / (jnp.sqrt(v_hat) + ADAM_EPS) + ADAM_WD * w)
    return delta, m_new, v_new


def _sum_adamw(parts, w, m, v, *, name, tr, rider=None):
    r, c = w.shape

    def body(p_ref, w_ref, m_ref, v_ref, g_ref, d_ref, mo_ref, vo_ref):
        g = p_ref[0].astype(F32)
        for k in range(1, N_DEV):
            g = g + p_ref[k].astype(F32)
        g_ref[...] = g
        d_ref[...], mo_ref[...], vo_ref[...] = _adamw(g, w_ref[...], m_ref[...], v_ref[...])

    blk = pl.BlockSpec((tr, c), lambda i: (i, 0))
    res, extra = _pcall(
        body, name=name, grid=(r // tr,),
        in_specs=[pl.BlockSpec((N_DEV, tr, c), lambda i: (0, i, 0)), blk, blk, blk],
        out_specs=[blk] * 4, out_shape=[jax.ShapeDtypeStruct((r, c), F32)] * 4,
        semantics=("parallel",), vmem_mb=40, rider=rider,
    )(parts, w, m, v)
    return res if rider is None else (res, extra)


def _sum_small(parts):
    _, r, c = parts.shape

    def body(p_ref, o_ref):
        s = p_ref[0]
        for k in range(1, N_DEV):
            s = s + p_ref[k]
        o_ref[...] = s

    return pl.pallas_call(body, name="sum_small", out_shape=jax.ShapeDtypeStruct((r, c), F32))(parts)


def _adamw_small(g, w, m, v):
    def body(g_ref, w_ref, m_ref, v_ref, d_ref, mo_ref, vo_ref):
        d_ref[...], mo_ref[...], vo_ref[...] = _adamw(g_ref[...], w_ref[...], m_ref[...], v_ref[...])

    return pl.pallas_call(body, name="adamw_small", out_shape=[jax.ShapeDtypeStruct(g.shape, F32)] * 3)(g, w, m, v)


def _head_sum_matrix():
    r = lax.broadcasted_iota(jnp.int32, (512, 512), 0) // HEAD_DIM
    c = lax.broadcasted_iota(jnp.int32, (512, 512), 1) // HEAD_DIM
    return (r == c).astype(BF16)


_SHARD_AXIS = dict(w_in=1, w_out=0, w_q=0, w_kv=1, w_o=0, w_up=1, w_down=0, conv_w=None, small=None)


class _Weights:
    def __init__(self, full, shards=None):
        self.full = dict(full)
        self.shards = shards

    def rider(self, names, late=False):
        if self.shards is None:
            return None
        return _Gather([self.shards[n] for n in names], [_SHARD_AXIS[n] for n in names], late)

    def arrived(self, names, gathered):
        if gathered is not None:
            for n, g in zip(names, gathered):
                self.full[n] = g.transpose(1, 0, 2).reshape(g.shape[1], -1) if n == "conv_w" else g

    def __getitem__(self, name):
        return self.full[name]


class _Grads:
    def __init__(self, distributed):
        self.distributed = distributed
        self.local = {}
        self.received = {}

    def add(self, name, g):
        self.local[name] = g

    def rider(self, names, pieces=None):
        if not self.distributed:
            return None
        into = [self.received.get(n) for n in names]
        return _Exchange([self.local[n] for n in names], [_SHARD_AXIS[n] for n in names], pieces, into)

    def arrived(self, names, received):
        if received is not None:
            for n, r in zip(names, received):
                self.received[n] = r


def _ride(fn, *args, rider=None, **kw):
    if rider is None:
        return fn(*args, **kw), None
    return fn(*args, rider=rider, **kw)


def _local_step(x, mem, tgt, gains, weights, grads):
    names = ["w_in", "conv_w"]
    (x, tgt), got = _ride(_reorder, [x, tgt], False, "reorder_in", rider=weights.rider(names, late=True))
    weights.arrived(names, got)
    w_in, cw = weights["w_in"], weights["conv_w"]

    names = ["w_out", "w_kv"]
    (proj, h1), got = _ride(_norm_matmul, x, gains["g_mix"], w_in, name="proj", out_dtype=F32, tb=1024, bn=768,
                            save_h=True, rider=weights.rider(names))
    weights.arrived(names, got)
    names = ["w_q", "w_o", "w_up"]
    (attn, *lses), got = _ride(_attention_fwd, proj, rider=weights.rider(names))
    weights.arrived(names, got)
    x1, merged = _mixer_fwd(x, attn, proj, cw, gains["g_attn_out"], gains["g_conv_out"], weights["w_out"])
    kv, mem_n = _norm_matmul(mem, gains["g_mem"], weights["w_kv"], name="mem_kv", out_dtype=BF16, tb=mem.shape[0],
                             bn=1024, save_h=True)
    x2, h2, qm, om = _xattn_fwd(x1, gains["g_xattn"], weights["w_q"], kv, weights["w_o"], tb=512)
    w_up = weights["w_up"]
    (a, h3), got = _ride(_norm_matmul, x2, gains["g_mlp"], w_up, name="mlp_up", out_dtype=BF16, tb=1024, bn=1024,
                         relu=True, save_h=True, rider=weights.rider(["w_down"]))
    weights.arrived(["w_down"], got)
    w_down = weights["w_down"]
    dx3, dx3b, loss_blk, gg_final = _mlp_down_loss(a, w_down, x2, tgt, gains["g_final"], tb=256)

    def sending(sends, fn, *args, **kw):
        names = [s[0] for s in sends]
        res, got = _ride(fn, *args, rider=grads.rider(names, [s[1:] for s in sends]), **kw)
        grads.arrived(names, got)
        return res

    dpre = _mlp_dpre(dx3b, w_down, a, tb=1024, bn=1024)
    grads.add("w_down", _matmul_tn(a, dx3b, name="grad_w_down", bm=512, bn=1024, square_a=True))
    grads.add("w_up", sending([("w_down", 0, 3, 8)], _matmul_tn, h3, dpre, name="grad_w_up", bm=1024, bn=512))
    dx2, dx2b, gg_mlp = sending([("w_down", 3, 5, 8)], _matmul_nt_normbwd, dpre, w_up, x2, gains["g_mlp"], dx3,
                                name="mlp_dx", tb=512, also_bf16=True)

    grads.add("w_o", _matmul_tn(om, dx2b, name="grad_w_o", bm=1024, bn=512))
    dx1, dx1b, dqm, dk, dv, gg_xattn = sending([("w_up", 0, 4, 8)], _xattn_bwd, dx2, x1, gains["g_xattn"], qm,
                                               weights["w_q"], kv, weights["w_o"], tb=512)
    grads.add("w_q", _matmul_tn(h2, dqm, name="grad_w_q", bm=1024, bn=512))
    dkv = jnp.concatenate([dk, dv], axis=1).astype(BF16)
    grads.add("w_kv", _matmul_tn(mem_n, dkv, name="grad_w_kv", bm=1024, bn=1024))
    _, gg_mem = _matmul_nt_normbwd(dkv, weights["w_kv"], mem, gains["g_mem"], None, name="mem_dx", tb=mem.shape[0])

    grads.add("w_out", _matmul_tn(merged, dx1b, name="grad_w_out", bm=1024, bn=512))
    dattn, dsum, dy, gg_attn, gg_conv = sending(
        [("w_up", 4, 3, 8)], _mixer_bwd, dx1, attn, proj, cw, gains["g_attn_out"], gains["g_conv_out"],
        weights["w_out"], _head_sum_matrix())
    dproj, gcw = _conv_bwd(dy, proj, cw)
    dproj = sending([("w_up", 7, 1, 8), ("w_o", 0, 1, 1), ("w_q", 0, 1, 1), ("w_kv", 0, 1, 1)], _attention_bwd,
                    proj, dattn, dsum, lses, dproj)
    grads.add("w_in", sending([("w_out", 0, 1, 1)], _matmul_tn, h1, dproj, name="grad_w_in", bm=1024, bn=512))
    grad_x, gg_mix = sending([("w_in", 0, 4, 8)], _matmul_nt_normbwd, dproj, w_in, x, gains["g_mix"], dx1,
                             name="mixer_dx", tb=512)

    rows = [gg_mix, gg_xattn, gg_mem, gg_mlp, gg_final, jnp.concatenate([gg_attn, gg_conv], axis=1),
            jnp.pad(gcw[0:3], ((0, 0), (0, 512))), jnp.pad(loss_blk[0:1, 0:1], ((0, 6), (0, 1023)))]
    grads.add("small", jnp.concatenate(rows, axis=0))
    (grad_x,) = _reorder([grad_x], True, "reorder_out")
    return grad_x


_BIG = ("w_in", "w_out", "w_q", "w_kv", "w_o", "w_up", "w_down")
_GAIN_ROWS = ("g_mix", "g_xattn", "g_mem", "g_mlp", "g_final")


def _pack_small(vals, conv):
    rows = [vals[k].reshape(1, -1) for k in _GAIN_ROWS]
    rows.append(jnp.concatenate([vals["g_attn_out"].reshape(1, -1), vals["g_conv_out"].reshape(1, -1)], axis=1))
    flat = conv.reshape(1, -1)
    rows.append(jnp.pad(flat, ((0, 0), (0, 1024 - flat.shape[1]))))
    rows.append(jnp.zeros((1, 1024), F32))
    return jnp.concatenate(rows, axis=0)


def kernel(x, mem, g_mix, w_in, conv_w, g_attn_out, g_conv_out, w_out, g_xattn, g_mem, w_q_mem, w_kv_mem, w_o_mem, g_mlp, w_up, w_down, g_final, loss_target, m_g_mix, m_w_in, m_conv_w, m_g_attn_out, m_g_conv_out, m_w_out, m_g_xattn, m_g_mem, m_w_q_mem, m_w_kv_mem, m_w_o_mem, m_g_mlp, m_w_up, m_w_down, m_g_final, v_g_mix, v_w_in, v_conv_w, v_g_attn_out, v_g_conv_out, v_w_out, v_g_xattn, v_g_mem, v_w_q_mem, v_w_kv_mem, v_w_o_mem, v_g_mlp, v_w_up, v_w_down, v_g_final):
    d = x.shape[-1]
    me = 4 * lax.axis_index("x") + 2 * lax.axis_index("y") + lax.axis_index("c")
    w_shards = dict(w_in=w_in, w_out=w_out, w_q=w_q_mem, w_kv=w_kv_mem, w_o=w_o_mem, w_up=w_up, w_down=w_down)
    m_shards = dict(w_in=m_w_in, w_out=m_w_out, w_q=m_w_q_mem, w_kv=m_w_kv_mem, w_o=m_w_o_mem, w_up=m_w_up,
                    w_down=m_w_down)
    v_shards = dict(w_in=v_w_in, w_out=v_w_out, w_q=v_w_q_mem, w_kv=v_w_kv_mem, w_o=v_w_o_mem, w_up=v_w_up,
                    w_down=v_w_down)
    gains = dict(g_mix=g_mix, g_attn_out=g_attn_out, g_conv_out=g_conv_out, g_xattn=g_xattn, g_mem=g_mem,
                 g_mlp=g_mlp, g_final=g_final)
    gains2 = {k: v.reshape(1, -1) for k, v in gains.items()}

    shards = {k: w_shards[k].astype(BF16) for k in _BIG}
    shards["conv_w"] = conv_w
    grads = _Grads(distributed=True)
    grad_x = _local_step(x[0], mem[0], loss_target[0], gains2, _Weights({}, shards), grads)

    outs = {}
    tiles = dict(w_in=256, w_out=128, w_q=128, w_kv=256, w_o=128, w_up=256, w_down=256)
    for k in ("w_up", "w_down", "w_out", "w_q", "w_kv", "w_o", "w_in"):
        last = ["w_in", "small"]
        rider = grads.rider(last, [(4, 4, 8), (0, 1, 1)]) if k == "w_up" else None
        outs[k], got = _ride(_sum_adamw, grads.received[k], w_shards[k], m_shards[k], v_shards[k],
                             name=f"adamw_{k}", tr=tiles[k], rider=rider)
        grads.arrived(last, got)
    small_received = grads.received["small"]

    ssum = _sum_small(small_received)
    loss = ssum[9, 0]
    g_small = {k: ssum[i] for i, k in enumerate(_GAIN_ROWS)}
    g_small["g_attn_out"] = ssum[5, 0:512]
    g_small["g_conv_out"] = ssum[5, 512:1024]
    g_conv = lax.dynamic_slice_in_dim(ssum[6:9, 0:512], me * 64, 64, axis=1)
    m_small = dict(g_mix=m_g_mix, g_attn_out=m_g_attn_out, g_conv_out=m_g_conv_out, g_xattn=m_g_xattn,
                   g_mem=m_g_mem, g_mlp=m_g_mlp, g_final=m_g_final)
    v_small = dict(g_mix=v_g_mix, g_attn_out=v_g_attn_out, g_conv_out=v_g_conv_out, g_xattn=v_g_xattn,
                   g_mem=v_g_mem, g_mlp=v_g_mlp, g_final=v_g_final)
    packed = [_pack_small(g_small, g_conv), _pack_small(gains, conv_w), _pack_small(m_small, m_conv_w),
              _pack_small(v_small, v_conv_w)]
    upd = _adamw_small(*packed)

    def unpack(p):
        res = {k: p[i] for i, k in enumerate(_GAIN_ROWS)}
        res["g_attn_out"] = p[5, 0:512]
        res["g_conv_out"] = p[5, 512:1024]
        res["conv_w"] = p[6, 0:192].reshape(3, 64)
        return res

    g_small["conv_w"] = g_conv
    small_out = [g_small] + [unpack(p) for p in upd]
    names = {"g_mix": "g_mix", "w_in": "w_in", "conv_w": "conv_w", "g_attn_out": "g_attn_out",
             "g_conv_out": "g_conv_out", "w_out": "w_out", "g_xattn": "g_xattn", "g_mem": "g_mem",
             "w_q_mem": "w_q", "w_kv_mem": "w_kv", "w_o_mem": "w_o", "g_mlp": "g_mlp", "w_up": "w_up",
             "w_down": "w_down", "g_final": "g_final"}
    result = [loss, grad_x[None]]
    for which in range(4):
        for key in names.values():
            result.append(outs[key][which] if key in outs else small_out[which][key])
    return tuple(result)
```

```python
import math

import jax
import jax.numpy as jnp
from jax import lax
from jax.experimental import pallas as pl
from jax.experimental.pallas import tpu as pltpu

F32 = jnp.float32
BF16 = jnp.bfloat16
NORM_EPS = 1e-6
NEG_INF = -1e30
N_DEV = 8
BLK = 128
HEAD_DIM = 64
N_MEM_HEADS = 4
ADAM_LR = 0.001
ADAM_B1 = 0.9
ADAM_B2 = 0.999
ADAM_EPS = 1e-08
ADAM_WD = 0.01
ADAM_STEP = 10
MESH = pl.DeviceIdType.MESH
ANY = pl.BlockSpec(memory_space=pl.ANY)


def _dot(a, b):
    return jnp.dot(a, b, preferred_element_type=F32)


def _dot_nt(a, b):
    return lax.dot_general(a, b, (((1,), (1,)), ((), ())), preferred_element_type=F32)


def _dot_tn(a, b):
    return lax.dot_general(a, b, (((0,), (0,)), ((), ())), preferred_element_type=F32)


def _params(semantics, vmem_mb):
    return pltpu.CompilerParams(dimension_semantics=semantics, vmem_limit_bytes=vmem_mb << 20)


def _rms_fwd(x, g):
    r = lax.rsqrt(jnp.mean(x * x, axis=-1, keepdims=True) + NORM_EPS)
    xh = x * r
    return xh * g, xh, r


def _rms_bwd(dy, xh, r, g):
    gy = dy * g
    return r * (gy - xh * jnp.mean(xh * gy, axis=-1, keepdims=True))


def _position():
    x, y, c = lax.axis_index("x"), lax.axis_index("y"), lax.axis_index("c")
    return x, y, c


def _block_of(ref, j, axis, shard_shape):
    r, c = shard_shape
    if axis is None:
        return ref.at[j]
    if axis == 0:
        return ref.at[pl.ds(j * r, r), :]
    return ref.at[:, pl.ds(j * c, c)]


class _Gather:
    has_mid = True
    alias_pairs = ()

    def __init__(self, shards, axes, late=False):
        self.arrays = list(shards)
        self.axes = list(axes)
        self.late = late
        self.n = len(self.arrays)

    def out_shape(self):
        res = []
        for s, axis in zip(self.arrays, self.axes):
            r, c = s.shape
            shape = (N_DEV, r, c) if axis is None else (N_DEV * r, c) if axis == 0 else (r, N_DEV * c)
            res.append(jax.ShapeDtypeStruct(shape, s.dtype))
        return res

    def scratch(self):
        return [pltpu.SemaphoreType.DMA((self.n, 7)), pltpu.SemaphoreType.DMA((self.n, 7)),
                pltpu.SemaphoreType.DMA((self.n,))]

    def _ctx(self, ins, outs, sems):
        send_sems, recv_sems, local_sems = sems
        x, y, c = _position()
        me, sibling = (x, y, c), (x, y, 1 - c)
        chips = [(1 - x, y), (x, 1 - y), (1 - x, 1 - y)]

        def lin(px, py, pc):
            return 4 * px + 2 * py + pc

        def place(a, block):
            return _block_of(outs[a], lin(*block), self.axes[a], self.arrays[a].shape)

        def copy(a, k, block, to, src=None):
            dst = place(a, block)
            return pltpu.make_async_remote_copy(
                src_ref=dst if src is None else src, dst_ref=dst,
                send_sem=send_sems.at[a, k], recv_sem=recv_sems.at[a, k],
                device_id=to, device_id_type=MESH)

        def mine():
            return [pltpu.make_async_copy(ins[a], place(a, me), local_sems.at[a]) for a in range(self.n)]

        def first():
            res = []
            for a in range(self.n):
                res.append(copy(a, 0, me, sibling, src=ins[a]))
                res += [copy(a, 1 + j, me, (*chip, c), src=ins[a]) for j, chip in enumerate(chips)]
            return res

        return c, me, sibling, chips, copy, mine, first

    def start(self, ins, outs, sems):
        _, _, _, _, _, mine, first = self._ctx(ins, outs, sems)
        for cp in mine() + first():
            cp.start()

    def mid(self, ins, outs, sems):
        c, me, sibling, chips, copy, _, _ = self._ctx(ins, outs, sems)
        for j, chip in enumerate(chips):
            for a in range(self.n):
                copy(a, 1 + j, (*chip, c), me).wait_recv()
                copy(a, 4 + j, (*chip, c), sibling).start()

    def finish(self, ins, outs, sems):
        c, me, sibling, chips, copy, mine, first = self._ctx(ins, outs, sems)
        for a in range(self.n):
            copy(a, 0, sibling, me).wait_recv()
            for j, chip in enumerate(chips):
                copy(a, 4 + j, (*chip, 1 - c), me).wait_recv()
        for cp in first():
            cp.wait_send()
        for j, chip in enumerate(chips):
            for a in range(self.n):
                copy(a, 4 + j, (*chip, c), sibling).wait_send()
        for cp in mine():
            cp.wait()


class _Exchange:
    has_mid = False

    def __init__(self, parts, axes, pieces=None, into=None):
        self.n = len(parts)
        self.axes = list(axes)
        self.pieces = list(pieces or [(0, 1, 1)] * self.n)
        into = list(into or [None] * self.n)
        kept = [a for a in range(self.n) if into[a] is not None]
        self.arrays = list(parts) + [into[a] for a in kept]
        self.alias_pairs = [(self.n + i, a) for i, a in enumerate(kept)]

    def _piece(self, a):
        r, c = self.arrays[a].shape
        axis = self.axes[a]
        return (r, c) if axis is None else (r // N_DEV, c) if axis == 0 else (r, c // N_DEV)

    def _rows(self, a):
        first, count, of = self.pieces[a]
        unit = self._piece(a)[0] // of
        return pl.ds(first * unit, count * unit)

    def out_shape(self):
        return [jax.ShapeDtypeStruct((N_DEV,) + self._piece(a), self.arrays[a].dtype) for a in range(self.n)]

    def scratch(self):
        return [pltpu.SemaphoreType.DMA((self.n, 7)), pltpu.SemaphoreType.DMA((self.n, 7)),
                pltpu.SemaphoreType.DMA((self.n,))]

    def _ctx(self, ins, outs, sems):
        send_sems, recv_sems, local_sems = sems
        x, y, c = _position()
        me = 4 * x + 2 * y + c

        def src(a, j):
            block = ins[a] if self.axes[a] is None else _block_of(ins[a], j, self.axes[a], self._piece(a))
            return block.at[self._rows(a), :]

        def dst(a, j):
            return outs[a].at[j, self._rows(a), :]

        def local():
            return [pltpu.make_async_copy(src(a, me), dst(a, me), local_sems.at[a]) for a in range(self.n)]

        def remote(inbound):
            res = []
            for a in range(self.n):
                for k in range(1, N_DEV):
                    peer = (1 - x if k & 4 else x, 1 - y if k & 2 else y, 1 - c if k & 1 else c)
                    plin = 4 * peer[0] + 2 * peer[1] + peer[2]
                    res.append(pltpu.make_async_remote_copy(
                        src_ref=src(a, plin), dst_ref=dst(a, plin if inbound else me),
                        send_sem=send_sems.at[a, k - 1], recv_sem=recv_sems.at[a, k - 1],
                        device_id=peer, device_id_type=MESH))
            return res

        return local, remote

    def start(self, ins, outs, sems):
        local, remote = self._ctx(ins, outs, sems)
        for cp in local() + remote(False):
            cp.start()

    def finish(self, ins, outs, sems):
        local, remote = self._ctx(ins, outs, sems)
        for cp in remote(True):
            cp.wait_recv()
        for cp in remote(False):
            cp.wait_send()
        for cp in local():
            cp.wait()


def _comm_call(rider, name):
    n_in, n_out = len(rider.arrays), len(rider.out_shape())

    def body(*refs):
        ins, outs, sems = refs[:n_in], refs[n_in:n_in + n_out], refs[n_in + n_out:]
        rider.start(ins, outs, sems)
        if rider.has_mid:
            rider.mid(ins, outs, sems)
        rider.finish(ins, outs, sems)

    return pl.pallas_call(
        body, name=name, out_shape=rider.out_shape(),
        in_specs=[ANY] * n_in, out_specs=[ANY] * n_out, scratch_shapes=rider.scratch(),
        input_output_aliases=dict(rider.alias_pairs),
    )(*rider.arrays)


def _pcall(body, *, name, grid, in_specs, out_specs, out_shape, scratch_shapes=(), semantics, vmem_mb, rider=None,
           aliases=None):
    in_specs, out_specs, out_shape = list(in_specs), list(out_specs), list(out_shape)
    scratch_shapes = list(scratch_shapes)
    aliases = dict(aliases or {})
    if rider is None:
        call = pl.pallas_call(body, name=name, grid=grid, in_specs=in_specs, out_specs=out_specs,
                              out_shape=out_shape, scratch_shapes=scratch_shapes, input_output_aliases=aliases,
                              compiler_params=_params(semantics, vmem_mb))
        return lambda *args: (list(call(*args)), None)
    n_in, n_out, n_scr = len(in_specs), len(out_specs), len(scratch_shapes)
    r_in, r_shapes = len(rider.arrays), rider.out_shape()
    r_out = len(r_shapes)
    aliases.update({n_in + i: n_out + o for i, o in rider.alias_pairs})
    total = math.prod(grid)
    mid_step = total - 1 if rider.has_mid and rider.late else (3 * total) // 4

    def wrapped(*refs):
        bounds = [0, n_in, r_in, n_out, r_out, n_scr]
        for i in range(1, len(bounds)):
            bounds[i] += bounds[i - 1]
        a, ra, o, ro, s = (refs[bounds[i]:bounds[i + 1]] for i in range(5))
        rs = refs[bounds[5]:]
        step = pl.program_id(0)
        for k in range(1, len(grid)):
            step = step * grid[k] + pl.program_id(k)
        pl.when(step == 0)(lambda: rider.start(ra, ro, rs))
        body(*a, *o, *s)
        if rider.has_mid:
            pl.when(step == mid_step)(lambda: rider.mid(ra, ro, rs))
        pl.when(step == total - 1)(lambda: rider.finish(ra, ro, rs))

    call = pl.pallas_call(
        wrapped, name=name, grid=grid, in_specs=in_specs + [ANY] * r_in, out_specs=out_specs + [ANY] * r_out,
        out_shape=out_shape + r_shapes, scratch_shapes=scratch_shapes + rider.scratch(),
        input_output_aliases=aliases, compiler_params=_params(("arbitrary",) * len(grid), vmem_mb))

    def run(*args):
        res = call(*args, *rider.arrays)
        return list(res[:n_out]), list(res[n_out:])

    return run


def _norm_matmul(x, g, w, *, name, out_dtype, tb, bn, relu=False, save_h=False, rider=None):
    t, d = x.shape
    n = w.shape[1]

    def body(x_ref, g_ref, w_ref, o_ref, *rest):
        h_scr = rest[-1]

        @pl.when(pl.program_id(1) == 0)
        def _():
            h = _rms_fwd(x_ref[...], g_ref[...])[0].astype(BF16)
            h_scr[...] = h
            if save_h:
                rest[0][...] = h

        acc = _dot(h_scr[...], w_ref[...])
        if relu:
            acc = jnp.maximum(acc, 0.0)
        o_ref[...] = acc.astype(out_dtype)

    out_shape = [jax.ShapeDtypeStruct((t, n), out_dtype)]
    out_specs = [pl.BlockSpec((tb, bn), lambda i, j: (i, j))]
    if save_h:
        out_shape.append(jax.ShapeDtypeStruct((t, d), BF16))
        out_specs.append(pl.BlockSpec((tb, d), lambda i, j: (i, 0)))
    res, extra = _pcall(
        body, name=name, grid=(t // tb, n // bn),
        in_specs=[pl.BlockSpec((tb, d), lambda i, j: (i, 0)),
                  pl.BlockSpec((1, d), lambda i, j: (0, 0)),
                  pl.BlockSpec((d, bn), lambda i, j: (0, j))],
        out_specs=out_specs, out_shape=out_shape,
        scratch_shapes=[pltpu.VMEM((tb, d), BF16)],
        semantics=("parallel", "arbitrary"), vmem_mb=48, rider=rider,
    )(x, g, w)
    res = res if save_h else res[0]
    return res if rider is None else (res, extra)


def _matmul_nt_normbwd(dy, w, x, g, dres, *, name, tb, also_bf16=False, rider=None):
    t, d = x.shape
    stacked = dy.ndim == 3
    has_res = dres is not None

    def body(dy_ref, w_ref, x_ref, g_ref, *rest):
        rest = list(rest)
        dres_ref = rest.pop(0) if has_res else None
        dx_ref = rest.pop(0)
        dxb_ref = rest.pop(0) if also_bf16 else None
        gg_ref = rest.pop(0)
        i = pl.program_id(0)
        if stacked:
            kb = dy_ref.shape[2]
            dh = _dot_nt(dy_ref[0], w_ref[:, 0:kb])
            for s in range(1, dy_ref.shape[0]):
                dh = dh + _dot_nt(dy_ref[s], w_ref[:, s * kb:(s + 1) * kb])
        else:
            dh = _dot_nt(dy_ref[...], w_ref[...])
        g_v = g_ref[...]
        _, xh, r = _rms_fwd(x_ref[...], g_v)
        dx = _rms_bwd(dh, xh, r, g_v)
        if has_res:
            dx = dx + dres_ref[...]
        dx_ref[...] = dx
        if also_bf16:
            dxb_ref[...] = dx.astype(BF16)
        part = jnp.sum(dh * xh, axis=0, keepdims=True)

        @pl.when(i == 0)
        def _():
            gg_ref[...] = part

        @pl.when(i != 0)
        def _():
            gg_ref[...] += part

    tok = pl.BlockSpec((tb, d), lambda i: (i, 0))
    row = pl.BlockSpec((1, d), lambda i: (0, 0))
    if stacked:
        dy_spec = pl.BlockSpec((dy.shape[0], tb, dy.shape[2]), lambda i: (0, i, 0))
    else:
        dy_spec = pl.BlockSpec((tb, dy.shape[1]), lambda i: (i, 0))
    in_specs = [dy_spec, pl.BlockSpec(w.shape, lambda i: (0, 0)), tok, row]
    args = [dy, w, x, g]
    if has_res:
        in_specs.append(tok)
        args.append(dres)
    out_specs = [tok] + ([tok] if also_bf16 else []) + [row]
    out_shape = ([jax.ShapeDtypeStruct((t, d), F32)] + ([jax.ShapeDtypeStruct((t, d), BF16)] if also_bf16 else [])
                 + [jax.ShapeDtypeStruct((1, d), F32)])
    res, extra = _pcall(
        body, name=name, grid=(t // tb,), in_specs=in_specs, out_specs=out_specs, out_shape=out_shape,
        semantics=("arbitrary",), vmem_mb=56, rider=rider,
    )(*args)
    return res if rider is None else (res, extra)


def _matmul_tn(a, b, *, name, bm, bn, square_a=False, rider=None):
    t, m = a.shape
    stacked = b.ndim == 3
    n = b.shape[0] * bn if stacked else b.shape[1]

    def body(a_ref, b_ref, o_ref):
        av = a_ref[...]
        if square_a:
            av = av.astype(F32)
            av = (av * av).astype(BF16)
        o_ref[...] = _dot_tn(av, b_ref[...]).astype(BF16)

    res, extra = _pcall(
        body, name=name, grid=(m // bm, n // bn),
        in_specs=[pl.BlockSpec((t, bm), lambda i, j: (0, i)),
                  pl.BlockSpec((None, t, bn), lambda i, j: (j, 0, 0)) if stacked
                  else pl.BlockSpec((t, bn), lambda i, j: (0, j))],
        out_specs=[pl.BlockSpec((bm, bn), lambda i, j: (i, j))], out_shape=[jax.ShapeDtypeStruct((m, n), BF16)],
        semantics=("parallel", "parallel"), vmem_mb=56, rider=rider,
    )(a, b)
    return res[0] if rider is None else (res[0], extra)


N_RES = 16
SEG = 128
HALF = N_RES * SEG
TI = 32


def _x4(a):
    return a.reshape(a.shape[0] // HALF, N_RES, SEG, a.shape[1])


def _reorder(arrays, inverse, name, rider=None):
    t, c = arrays[0].shape
    n = len(arrays)
    n_i = SEG // TI
    natural = pl.BlockSpec((TI * N_RES, c), lambda s: (s, 0))
    major = pl.BlockSpec((1, N_RES, TI, c), lambda s: (s // n_i, 0, s % n_i, 0))

    def body(*refs):
        scr = refs[-1]
        for i_ref, o_ref in zip(refs[:n], refs[n:2 * n]):
            for cb in range(c // BLK):
                cols = slice(cb * BLK, (cb + 1) * BLK)
                slab = scr.at[cb]
                if inverse:
                    for r in range(N_RES):
                        slab[pl.ds(r, TI, stride=N_RES), :] = i_ref[0, r, :, cols]
                    o_ref[:, cols] = slab[...]
                else:
                    slab[...] = i_ref[:, cols]
                    for r in range(N_RES):
                        o_ref[0, r, :, cols] = slab[pl.ds(r, TI, stride=N_RES), :]

    shape4 = (t // HALF, N_RES, SEG, c)
    res, extra = _pcall(
        body, name=name, grid=(t // (TI * N_RES),),
        in_specs=[major if inverse else natural] * n, out_specs=[natural if inverse else major] * n,
        out_shape=[jax.ShapeDtypeStruct((t, c) if inverse else shape4, F32)] * n,
        scratch_shapes=[pltpu.VMEM((c // BLK, TI * N_RES, BLK), F32)],
        semantics=("parallel",), vmem_mb=32, rider=rider,
    )(*[_x4(a) if inverse else a for a in arrays])
    res = [r.reshape(t, c) for r in res]
    return res if rider is None else (res, extra)


_PATTERNS = ((1, 16, 8, SEG), (4, 4, 32, 4 * SEG), (16, 1, SEG, 0))
_FIRST = {1: 1, 4: 4, 16: 16}


def _group_rows(d, g):
    a = g >> 4
    if d == 16:
        base = a * HALF + (g & 15) * SEG
        prev = base - HALF
    elif d == 4:
        c = (g >> 2) & 3
        base = a * HALF + (g & 3) * SEG + c * 32
        prev = jnp.where(c > 0, base - 32, base - HALF + 96)
    else:
        c = g & 15
        base = a * HALF + c * 8
        prev = jnp.where(c > 0, base - 8, base - HALF + 120)
    return base, prev


def _load_rows(ref, base, n, rows, stride):
    parts = [ref[pl.ds(pl.multiple_of(base + j * stride, 8), rows), :] for j in range(n)]
    return parts[0] if n == 1 else jnp.concatenate(parts, axis=0)


def _store_rows(ref, base, val, n, rows, stride, add=False):
    for j in range(n):
        sl = pl.ds(pl.multiple_of(base + j * stride, 8), rows)
        piece = val[j * rows:(j + 1) * rows, :]
        if add:
            ref[sl, :] += piece
        else:
            ref[sl, :] = piece


def _band_bias(n, rows):
    shift = rows.bit_length() - 1
    lq = lax.broadcasted_iota(jnp.int32, (BLK, BLK), 0)
    lk = lax.broadcasted_iota(jnp.int32, (BLK, BLK), 1)
    iq = (lq & (rows - 1)) * n + (lq >> shift)
    ik = (lk & (rows - 1)) * n + (lk >> shift)
    zero = jnp.zeros((BLK, BLK), F32)
    return jnp.where(ik >= iq, zero, NEG_INF), jnp.where(ik <= iq, zero, NEG_INF)


def _set_bias(bias_scr, n, rows):
    prev_b, cur_b = _band_bias(n, rows)
    for half in range(2):
        bias_scr[half * BLK:(half + 1) * BLK, 0:BLK] = prev_b
        bias_scr[half * BLK:(half + 1) * BLK, BLK:2 * BLK] = cur_b


SCALE = 1.0 / math.sqrt(HEAD_DIM)


def _head_consts(value=1.0):
    lane_lo = lax.broadcasted_iota(jnp.int32, (BLK, BLK), 1) < HEAD_DIM
    return lane_lo, [jnp.where(lane_lo, value, 0.0).astype(BF16), jnp.where(lane_lo, 0.0, value).astype(BF16)]


def _stack_heads(v, head_mask):
    return jnp.concatenate([v * head_mask[0], v * head_mask[1]], axis=0)


def _unstack_heads(v2, lane_lo):
    return jnp.where(lane_lo, v2[:BLK], v2[BLK:])


def _rows_per_head(v, lane_lo):
    rolled = pltpu.roll(v, HEAD_DIM, axis=1)
    return jnp.concatenate([jnp.where(lane_lo, v, rolled), jnp.where(lane_lo, rolled, v)], axis=0)


WIDTH = 4


def _loop(lo, hi, fn, width=None):
    if width is None:
        def body(g, carry):
            fn(g)
            return carry

        if hi > lo:
            lax.fori_loop(lo, hi, body, 0)
        return
    while hi > lo:
        trips = (hi - lo) // width
        if trips:
            def body(i, carry, lo=lo, width=width):
                fn([lo + width * i + j for j in range(width)])
                return carry

            lax.fori_loop(0, trips, body, 0)
            lo += trips * width
        width = max(1, width // 2)


def _mix_weights(l1, l2, l3):
    mx = jnp.maximum(jnp.maximum(l1, l2), l3)
    e1, e2, e3 = jnp.exp(l1 - mx), jnp.exp(l2 - mx), jnp.exp(l3 - mx)
    inv = 1.0 / (e1 + e2 + e3)
    return e1 * inv, e2 * inv, e3 * inv


def _attention_fwd(qkv, rider=None):
    t = qkv.shape[0]
    groups = 16 * (t // HALF)

    def body(q_ref, k_ref, v_ref, attn_ref, l1_ref, l2_ref, l3_ref, o_scr, bias_scr):
        lane_lo, q_mask = _head_consts(SCALE)
        l_refs = (l1_ref, l2_ref, l3_ref)
        for p, (d, n, rows, stride) in enumerate(_PATTERNS):
            _set_bias(bias_scr, n, rows)
            o_p, l_p = o_scr.at[p], l_refs[p]

            def block(gs, has_prev):
                at = [_group_rows(d, g) for g in gs]

                def load(ref, b):
                    return _load_rows(ref, b, n, rows, stride).astype(BF16)

                q2 = [_stack_heads(load(q_ref, b), q_mask) for b, _ in at]
                k2 = [load(k_ref, b) for b, _ in at]
                v2 = [load(v_ref, b) for b, _ in at]
                if has_prev:
                    k2 = [jnp.concatenate([load(k_ref, pv), k], axis=0) for (_, pv), k in zip(at, k2)]
                    v2 = [jnp.concatenate([load(v_ref, pv), v], axis=0) for (_, pv), v in zip(at, v2)]
                s = [_dot_nt(q, k) for q, k in zip(q2, k2)]
                s = [x + (bias_scr[...] if has_prev else bias_scr[:, BLK:2 * BLK]) for x in s]
                mx = [jnp.max(x, axis=1, keepdims=True) for x in s]
                e = [jnp.exp(x - m) for x, m in zip(s, mx)]
                den = [jnp.sum(x, axis=1, keepdims=True) for x in e]
                o2 = [_dot(x.astype(BF16), v) * (1.0 / dn) for x, v, dn in zip(e, v2, den)]
                lse2 = [jnp.broadcast_to(m + jnp.log(dn), (2 * BLK, BLK)) for m, dn in zip(mx, den)]
                for (b, _), o, l in zip(at, o2, lse2):
                    _store_rows(o_p, b, _unstack_heads(o, lane_lo), n, rows, stride)
                    _store_rows(l_p, b, _unstack_heads(l, lane_lo), n, rows, stride)

            _loop(0, _FIRST[d], lambda gs: block(gs, False), width=WIDTH)
            _loop(_FIRST[d], groups, lambda gs: block(gs, True), width=WIDTH)

        def mix(i):
            sl = pl.ds(pl.multiple_of(i * 256, 256), 256)
            w = _mix_weights(l1_ref[sl, :], l2_ref[sl, :], l3_ref[sl, :])
            attn_ref[sl, :] = w[0] * o_scr[0, sl, :] + w[1] * o_scr[1, sl, :] + w[2] * o_scr[2, sl, :]

        _loop(0, t // 256, mix)

    def col(c0):
        return pl.BlockSpec((t, BLK), lambda hp: (0, c0 + hp))

    res, extra = _pcall(
        body, name="attention_fwd", grid=(4,), in_specs=[col(0), col(4), col(8)], out_specs=[col(0)] * 4,
        out_shape=[jax.ShapeDtypeStruct((t, 512), F32)] * 4,
        scratch_shapes=[pltpu.VMEM((3, t, BLK), F32), pltpu.VMEM((2 * BLK, 2 * BLK), F32)],
        semantics=("parallel",), vmem_mb=48, rider=rider,
    )(qkv, qkv, qkv)
    return res if rider is None else (res, extra)


def _attention_bwd(qkv, dattn, dsum, lses, dproj, rider=None):
    t = qkv.shape[0]
    groups = 16 * (t // HALF)

    def body(q_ref, k_ref, v_ref, da_ref, ds_ref, l1_ref, l2_ref, l3_ref, kept_ref, out_ref, acc, bias_scr):
        del kept_ref
        lane_lo, head_mask = _head_consts()
        q_mask = _head_consts(SCALE)[1]
        l_refs = (l1_ref, l2_ref, l3_ref)

        def clear(i):
            sl = pl.ds(pl.multiple_of(i * 512, 512), 512)
            for s in range(3):
                acc[s, sl, :] = jnp.zeros((512, BLK), F32)

        _loop(0, t // 512, clear)
        dq_acc, dk_acc, dv_acc = acc.at[0], acc.at[1], acc.at[2]
        for p, (d, n, rows, stride) in enumerate(_PATTERNS):
            _set_bias(bias_scr, n, rows)

            def block(gs, has_prev):
                at = [_group_rows(d, g) for g in gs]

                def load(ref, b):
                    return _load_rows(ref, b, n, rows, stride)

                def put(ref, b, val):
                    _store_rows(ref, b, val, n, rows, stride, add=True)

                def wide(x):
                    return jnp.concatenate([x, x], axis=1) if has_prev else x

                lse = [[load(ref, b) for ref in l_refs] for b, _ in at]
                w = [_mix_weights(*ls)[p] for ls in lse]
                do2 = [_stack_heads((wg * load(da_ref, b)).astype(BF16), head_mask) for wg, (b, _) in zip(w, at)]
                dl2 = [wide(_rows_per_head(wg * load(ds_ref, b), lane_lo)) for wg, (b, _) in zip(w, at)]
                lse2 = [wide(_rows_per_head(ls[p], lane_lo)) for ls in lse]
                q2 = [_stack_heads(load(q_ref, b).astype(BF16), q_mask) for b, _ in at]
                k2 = [load(k_ref, b).astype(BF16) for b, _ in at]
                v2 = [load(v_ref, b).astype(BF16) for b, _ in at]
                if has_prev:
                    k2 = [jnp.concatenate([load(k_ref, pv).astype(BF16), k], axis=0) for (_, pv), k in zip(at, k2)]
                    v2 = [jnp.concatenate([load(v_ref, pv).astype(BF16), v], axis=0) for (_, pv), v in zip(at, v2)]
                s = [_dot_nt(q, k) for q, k in zip(q2, k2)]
                dp = [_dot_nt(do, v) for do, v in zip(do2, v2)]
                pr = [jnp.exp(x + (bias_scr[...] if has_prev else bias_scr[:, BLK:2 * BLK]) - l)
                      for x, l in zip(s, lse2)]
                ds = [(pg * (x - dl)).astype(BF16) for pg, x, dl in zip(pr, dp, dl2)]
                dq2 = [_dot(x, k) * SCALE for x, k in zip(ds, k2)]
                dk2 = [_dot_tn(x, q) for x, q in zip(ds, q2)]
                dv2 = [_dot_tn(pg.astype(BF16), do) for pg, do in zip(pr, do2)]
                for (b, pv), dq, dk, dv in zip(at, dq2, dk2, dv2):
                    put(dq_acc, b, _unstack_heads(dq, lane_lo))
                    if has_prev:
                        put(dk_acc, pv, dk[:BLK])
                        put(dv_acc, pv, dv[:BLK])
                        put(dk_acc, b, dk[BLK:])
                        put(dv_acc, b, dv[BLK:])
                    else:
                        put(dk_acc, b, dk)
                        put(dv_acc, b, dv)

            _loop(0, _FIRST[d], lambda gs: block(gs, False), width=WIDTH)
            _loop(_FIRST[d], groups, lambda gs: block(gs, True), width=WIDTH)

        def emit(i):
            sl = pl.ds(pl.multiple_of(i * 512, 512), 512)
            for s in range(3):
                out_ref[s, sl, :] = acc[s, sl, :].astype(BF16)

        _loop(0, t // 512, emit)

    def col(c0):
        return pl.BlockSpec((t, BLK), lambda hp: (0, c0 + hp))

    res, extra = _pcall(
        body, name="attention_bwd", grid=(4,),
        in_specs=[col(0), col(4), col(8)] + [col(0)] * 5 + [ANY],
        out_specs=[pl.BlockSpec((3, t, BLK), lambda hp: (0, 0, hp))],
        out_shape=[jax.ShapeDtypeStruct(dproj.shape, BF16)],
        scratch_shapes=[pltpu.VMEM((3, t, BLK), F32), pltpu.VMEM((2 * BLK, 2 * BLK), F32)],
        semantics=("parallel",), vmem_mb=56, rider=rider, aliases={8: 0},
    )(qkv, qkv, qkv, dattn, dsum, *lses, dproj)
    return res[0] if rider is None else (res[0], extra)


def _order_specs(t):
    n_i = SEG // TI
    nblk = (t // HALF) * n_i
    per = TI // 8

    def main(c, col=0):
        return pl.BlockSpec((1, N_RES, TI, c), lambda s: (s // n_i, 0, s % n_i, col))

    def before(c, col=0):
        return pl.BlockSpec((1, 2, 8, c), lambda s: (jnp.maximum(s - 1, 0) // n_i, N_RES // 2 - 1,
                                                     (jnp.maximum(s - 1, 0) % n_i) * per + per - 1, col))

    def after(c, col=0):
        return pl.BlockSpec((1, 2, 8, c), lambda s: (jnp.minimum(s + 1, nblk - 1) // n_i, 0,
                                                     (jnp.minimum(s + 1, nblk - 1) % n_i) * per, col))

    return nblk, main, before, after


def _shift_in(v, row_in, up):
    rows = v.shape[0]
    idx = lax.broadcasted_iota(jnp.int32, v.shape, 0)
    fill = jnp.broadcast_to(row_in, v.shape)
    if up:
        return jnp.where(idx == rows - 1, fill, pltpu.roll(v, rows - 1, axis=0))
    return jnp.where(idx == 0, fill, pltpu.roll(v, 1, axis=0))


def _taps_behind(u, before):
    s15 = _shift_in(u[N_RES - 1], before[1, 7:8, :], up=False)
    s14 = _shift_in(u[N_RES - 2], before[0, 7:8, :], up=False)
    m1 = jnp.concatenate([s15[None], u[:N_RES - 1]], axis=0)
    m2 = jnp.concatenate([s14[None], s15[None], u[:N_RES - 2]], axis=0)
    return m1, m2


def _taps_ahead(u, after):
    t0 = _shift_in(u[0], after[0, 0:1, :], up=True)
    t1 = _shift_in(u[1], after[1, 0:1, :], up=True)
    p1 = jnp.concatenate([u[1:], t0[None]], axis=0)
    p2 = jnp.concatenate([u[2:], t0[None], t1[None]], axis=0)
    return p1, p2


def _conv_fwd(gates, before, first, cw):
    bg, cg, xc = gates[..., 0:512], gates[..., 512:1024], gates[..., 1024:1536]
    u = cg * xc
    ub = before[..., 512:1024] * before[..., 1024:1536]
    ub = jnp.where(first, jnp.zeros_like(ub), ub)
    m1, m2 = _taps_behind(u, ub)
    conv = m2 * cw[0:1, :] + m1 * cw[1:2, :] + u * cw[2:3, :]
    return bg, u, m1, m2, conv


def _sum_tokens(v):
    return jnp.sum(jnp.sum(v, axis=0), axis=0, keepdims=True)


def _mixer_fwd(x, attn, gates, cw, g_a, g_c, w_out):
    t, d = x.shape
    nblk, main, before, _ = _order_specs(t)
    rows = N_RES * TI

    def body(x_ref, at_ref, gt_ref, gb_ref, cw_ref, ga_ref, gc_ref, wa_ref, wb_ref, x1_ref, mg_ref):
        an = _rms_fwd(at_ref[0], ga_ref[...])[0].astype(BF16)
        bg, _, _, _, conv = _conv_fwd(gt_ref[0], gb_ref[0], pl.program_id(0) == 0, cw_ref[...])
        cn = _rms_fwd(bg * conv, gc_ref[...])[0].astype(BF16)
        mg_ref[0, :, :, 0:512] = an
        mg_ref[0, :, :, 512:1024] = cn
        y = _dot(an.reshape(rows, 512), wa_ref[...]) + _dot(cn.reshape(rows, 512), wb_ref[...])
        x1_ref[0] = x_ref[0] + y.reshape(N_RES, TI, d)

    const = lambda r, c, i0=0: pl.BlockSpec((r, c), lambda s: (i0, 0))
    x1, merged = pl.pallas_call(
        body, name="mixer_fwd", grid=(nblk,),
        in_specs=[main(d), main(512), main(1536, 1), before(1536, 1), const(3, 512), const(1, 512), const(1, 512),
                  const(512, d), const(512, d, 1)],
        out_specs=[main(d), main(d)],
        out_shape=[jax.ShapeDtypeStruct(_x4(x).shape, F32), jax.ShapeDtypeStruct(_x4(x).shape, BF16)],
        compiler_params=_params(("parallel",), 48),
    )(_x4(x), _x4(attn), _x4(gates), _x4(gates), cw, g_a, g_c, w_out, w_out)
    return x1.reshape(t, d), merged.reshape(t, d)


def _mixer_bwd(dx1, attn, gates, cw, g_a, g_c, w_out, head_sum, rider=None):
    t, d = dx1.shape
    nblk, main, before, _ = _order_specs(t)
    rows = N_RES * TI

    def body(dx_ref, at_ref, gt_ref, gb_ref, cw_ref, ga_ref, gc_ref, wa_ref, wb_ref, hs_ref,
             da_ref, dsum_ref, dy_ref, gga_ref, ggc_ref):
        s = pl.program_id(0)
        dxb = dx_ref[0].reshape(rows, d).astype(BF16)
        dma = _dot_nt(dxb, wa_ref[...]).reshape(N_RES, TI, 512)
        dmc = _dot_nt(dxb, wb_ref[...]).reshape(N_RES, TI, 512)
        attn_v, g_av = at_ref[0], ga_ref[...]
        _, ah, ra = _rms_fwd(attn_v, g_av)
        dattn = _rms_bwd(dma, ah, ra, g_av)
        da_ref[0] = dattn
        z = (dattn * attn_v).reshape(rows, 512)
        hs = hs_ref[...]
        z1 = z.astype(BF16)
        z2 = (z - z1.astype(F32)).astype(BF16)
        dsum_ref[0] = (_dot(z1, hs) + _dot(z2, hs)).reshape(N_RES, TI, 512)
        bg, _, _, _, conv = _conv_fwd(gt_ref[0], gb_ref[0], s == 0, cw_ref[...])
        g_cv = gc_ref[...]
        _, yh, rc = _rms_fwd(bg * conv, g_cv)
        dy_ref[0] = _rms_bwd(dmc, yh, rc, g_cv)
        pa, pc = _sum_tokens(dma * ah), _sum_tokens(dmc * yh)

        @pl.when(s == 0)
        def _():
            gga_ref[...] = pa
            ggc_ref[...] = pc

        @pl.when(s != 0)
        def _():
            gga_ref[...] += pa
            ggc_ref[...] += pc

    const = lambda r, c, i0=0: pl.BlockSpec((r, c), lambda s: (i0, 0))
    shape4 = _x4(attn).shape
    res, extra = _pcall(
        body, name="mixer_bwd", grid=(nblk,),
        in_specs=[main(d), main(512), main(1536, 1), before(1536, 1), const(3, 512), const(1, 512), const(1, 512),
                  const(512, d), const(512, d, 1), const(512, 512)],
        out_specs=[main(512)] * 3 + [const(1, 512), const(1, 512)],
        out_shape=[jax.ShapeDtypeStruct(shape4, F32)] * 3 + [jax.ShapeDtypeStruct((1, 512), F32)] * 2,
        semantics=("arbitrary",), vmem_mb=48, rider=rider,
    )(_x4(dx1), _x4(attn), _x4(gates), _x4(gates), cw, g_a, g_c, w_out, w_out, head_sum)
    res = [r.reshape(t, 512) for r in res[:3]] + res[3:]
    return res if rider is None else (res, extra)


def _conv_bwd(dy, gates, cw, rider=None):
    t = dy.shape[0]
    nblk, main, before, after = _order_specs(t)
    n_i = SEG // TI

    def body(dy_ref, dya_ref, gt_ref, gb_ref, ga_ref, cw_ref, dp_ref, gcw_ref):
        s = pl.program_id(0)
        cw_v, gates_v = cw_ref[...], gt_ref[0]
        bg, u, m1, m2, conv = _conv_fwd(gates_v, gb_ref[0], s == 0, cw_v)
        dy_v = dy_ref[0]
        dconv = dy_v * bg
        dca = dya_ref[0] * ga_ref[0][..., 0:512]
        dca = jnp.where(s == nblk - 1, jnp.zeros_like(dca), dca)
        p1, p2 = _taps_ahead(dconv, dca)
        du = dconv * cw_v[2:3, :] + p1 * cw_v[1:2, :] + p2 * cw_v[0:1, :]
        dp_ref[0, 0] = (dy_v * conv).astype(BF16)
        dp_ref[1, 0] = (du * gates_v[..., 1024:1536]).astype(BF16)
        dp_ref[2, 0] = (du * gates_v[..., 512:1024]).astype(BF16)
        parts = [_sum_tokens(dconv * m2), _sum_tokens(dconv * m1), _sum_tokens(dconv * u)]

        @pl.when(s == 0)
        def _():
            gcw_ref[...] = jnp.zeros_like(gcw_ref)

        for tap in range(3):
            gcw_ref[tap:tap + 1, :] += parts[tap]

    (dproj, gcw), extra = _pcall(
        body, name="conv_bwd", grid=(nblk,),
        in_specs=[main(512), after(512), main(1536, 1), before(1536, 1), after(1536, 1),
                  pl.BlockSpec((3, 512), lambda s: (0, 0))],
        out_specs=[pl.BlockSpec((3, 1, N_RES, TI, 512), lambda s: (1, s // n_i, 0, s % n_i, 0)),
                   pl.BlockSpec((8, 512), lambda s: (0, 0))],
        out_shape=[jax.ShapeDtypeStruct((6, t // HALF, N_RES, SEG, 512), BF16), jax.ShapeDtypeStruct((8, 512), F32)],
        semantics=("arbitrary",), vmem_mb=40, rider=rider,
    )(_x4(dy), _x4(dy), _x4(gates), _x4(gates), _x4(gates), cw)
    res = (dproj.reshape(6, t, 512), gcw)
    return res if rider is None else (res, extra)


def _xattn_fwd(x1, g, w_q, kv, w_o, *, tb):
    t, d = x1.shape
    hd = d // N_MEM_HEADS
    m = kv.shape[0]

    def body(x_ref, g_ref, wq_ref, k_ref, v_ref, wo_ref, x2_ref, h_ref, q_ref, o_ref):
        xv = x_ref[...]
        h = _rms_fwd(xv, g_ref[...])[0].astype(BF16)
        h_ref[...] = h
        q = _dot(h, wq_ref[...]).astype(BF16)
        q_ref[...] = q
        for hh in range(N_MEM_HEADS):
            sl = slice(hh * hd, (hh + 1) * hd)
            s = _dot_nt(q[:, sl], k_ref[:, sl]) * (1.0 / 16.0)
            e = jnp.exp(s - jnp.max(s, axis=1, keepdims=True))
            p = e / jnp.sum(e, axis=1, keepdims=True)
            o_ref[:, sl] = _dot(p.astype(BF16), v_ref[:, sl]).astype(BF16)
        x2_ref[...] = xv + _dot(o_ref[...], wo_ref[...])

    tok = pl.BlockSpec((tb, d), lambda i: (i, 0))
    full = pl.BlockSpec((d, d), lambda i: (0, 0))
    return pl.pallas_call(
        body, name="xattn_fwd", grid=(t // tb,),
        in_specs=[tok, pl.BlockSpec((1, d), lambda i: (0, 0)), full,
                  pl.BlockSpec((m, d), lambda i: (0, 0)), pl.BlockSpec((m, d), lambda i: (0, 1)), full],
        out_specs=[tok] * 4,
        out_shape=[jax.ShapeDtypeStruct((t, d), F32)] + [jax.ShapeDtypeStruct((t, d), BF16)] * 3,
        compiler_params=_params(("parallel",), 48),
    )(x1, g, w_q, kv, kv, w_o)


def _xattn_bwd(dx2, x1, g, q, w_q, kv, w_o, *, tb, rider=None):
    t, d = x1.shape
    hd = d // N_MEM_HEADS
    m = kv.shape[0]

    def body(dx2_ref, x_ref, g_ref, q_ref, wq_ref, k_ref, v_ref, wo_ref,
             dx1_ref, dx1b_ref, dq_ref, dk_ref, dv_ref, gg_ref):
        i = pl.program_id(0)

        @pl.when(i == 0)
        def _():
            dk_ref[...] = jnp.zeros_like(dk_ref)
            dv_ref[...] = jnp.zeros_like(dv_ref)

        dx2 = dx2_ref[...]
        do = _dot_nt(dx2.astype(BF16), wo_ref[...]).astype(BF16)
        for hh in range(N_MEM_HEADS):
            sl = slice(hh * hd, (hh + 1) * hd)
            qh, kh, vh, doh = q_ref[:, sl], k_ref[:, sl], v_ref[:, sl], do[:, sl]
            s = _dot_nt(qh, kh) * (1.0 / 16.0)
            e = jnp.exp(s - jnp.max(s, axis=1, keepdims=True))
            p = e / jnp.sum(e, axis=1, keepdims=True)
            dp = _dot_nt(doh, vh)
            ds = (p * (dp - jnp.sum(dp * p, axis=1, keepdims=True)) * (1.0 / 16.0)).astype(BF16)
            dq_ref[:, sl] = _dot(ds, kh).astype(BF16)
            dk_ref[:, sl] += _dot_tn(ds, qh)
            dv_ref[:, sl] += _dot_tn(p.astype(BF16), doh)
        dh = _dot_nt(dq_ref[...], wq_ref[...])
        g_v = g_ref[...]
        _, xh, r = _rms_fwd(x_ref[...], g_v)
        dx1 = dx2 + _rms_bwd(dh, xh, r, g_v)
        dx1_ref[...] = dx1
        dx1b_ref[...] = dx1.astype(BF16)
        part = jnp.sum(dh * xh, axis=0, keepdims=True)

        @pl.when(i == 0)
        def _():
            gg_ref[...] = part

        @pl.when(i != 0)
        def _():
            gg_ref[...] += part

    tok = pl.BlockSpec((tb, d), lambda i: (i, 0))
    full = pl.BlockSpec((d, d), lambda i: (0, 0))
    acc = pl.BlockSpec((m, d), lambda i: (0, 0))
    res, extra = _pcall(
        body, name="xattn_bwd", grid=(t // tb,),
        in_specs=[tok, tok, pl.BlockSpec((1, d), lambda i: (0, 0)), tok, full,
                  pl.BlockSpec((m, d), lambda i: (0, 0)), pl.BlockSpec((m, d), lambda i: (0, 1)), full],
        out_specs=[tok, tok, tok, acc, acc, pl.BlockSpec((1, d), lambda i: (0, 0))],
        out_shape=[jax.ShapeDtypeStruct((t, d), F32), jax.ShapeDtypeStruct((t, d), BF16),
                   jax.ShapeDtypeStruct((t, d), BF16),
                   jax.ShapeDtypeStruct((m, d), F32), jax.ShapeDtypeStruct((m, d), F32),
                   jax.ShapeDtypeStruct((1, d), F32)],
        semantics=("arbitrary",), vmem_mb=48, rider=rider,
    )(dx2, x1, g, q, w_q, kv, kv, w_o)
    return res if rider is None else (res, extra)


def _mlp_down_loss(a, w_down, x2, tgt, g, *, tb):
    t, d = x2.shape
    f = a.shape[1]

    def body(a_ref, w_ref, x_ref, t_ref, g_ref, dx_ref, dxb_ref, loss_ref, gg_ref):
        i = pl.program_id(0)
        av = a_ref[...].astype(F32)
        x3 = x_ref[...] + _dot((av * av).astype(BF16), w_ref[...])
        g_v = g_ref[...]
        out, xh, r = _rms_fwd(x3, g_v)
        err = out - t_ref[...]
        dout = err * (1.0 / d)
        dx = _rms_bwd(dout, xh, r, g_v)
        dx_ref[...] = dx
        dxb_ref[...] = dx.astype(BF16)
        part = jnp.sum(dout * xh, axis=0, keepdims=True)
        lpart = 0.5 * jnp.sum(jnp.mean(err * err, axis=-1, keepdims=True), axis=0, keepdims=True)
        lpart = jnp.broadcast_to(lpart, loss_ref.shape)

        @pl.when(i == 0)
        def _():
            gg_ref[...] = part
            loss_ref[...] = lpart

        @pl.when(i != 0)
        def _():
            gg_ref[...] += part
            loss_ref[...] += lpart

    tok = pl.BlockSpec((tb, d), lambda i: (i, 0))
    return pl.pallas_call(
        body, name="mlp_down_loss", grid=(t // tb,),
        in_specs=[pl.BlockSpec((tb, f), lambda i: (i, 0)), pl.BlockSpec((f, d), lambda i: (0, 0)), tok, tok,
                  pl.BlockSpec((1, d), lambda i: (0, 0))],
        out_specs=[tok, tok, pl.BlockSpec((8, 128), lambda i: (0, 0)), pl.BlockSpec((1, d), lambda i: (0, 0))],
        out_shape=[jax.ShapeDtypeStruct((t, d), F32), jax.ShapeDtypeStruct((t, d), BF16),
                   jax.ShapeDtypeStruct((8, 128), F32), jax.ShapeDtypeStruct((1, d), F32)],
        compiler_params=_params(("arbitrary",), 56),
    )(a, w_down, x2, tgt, g)


def _mlp_dpre(dx3, w_down, a, *, tb, bn):
    t, d = dx3.shape
    f = a.shape[1]

    def body(dx_ref, w_ref, a_ref, o_ref):
        o_ref[...] = (2.0 * a_ref[...].astype(F32) * _dot_nt(dx_ref[...], w_ref[...])).astype(BF16)

    return pl.pallas_call(
        body, name="mlp_dpre", grid=(t // tb, f // bn),
        in_specs=[pl.BlockSpec((tb, d), lambda i, j: (i, 0)), pl.BlockSpec((bn, d), lambda i, j: (j, 0)),
                  pl.BlockSpec((tb, bn), lambda i, j: (i, j))],
        out_specs=pl.BlockSpec((tb, bn), lambda i, j: (i, j)),
        out_shape=jax.ShapeDtypeStruct((t, f), BF16),
        compiler_params=_params(("parallel", "arbitrary"), 48),
    )(dx3, w_down, a)


def _adamw(gsum, w, m, v):
    m_new = ADAM_B1 * m + (1.0 - ADAM_B1) * gsum
    v_new = ADAM_B2 * v + (1.0 - ADAM_B2) * (gsum * gsum)
    m_hat = m_new / (1.0 - ADAM_B1 ** ADAM_STEP)
    v_hat = v_new / (1.0 - ADAM_B2 ** ADAM_STEP)
    delta = -ADAM_LR * (m_hat / (jnp.sqrt(v_hat) + ADAM_EPS) + ADAM_WD * w)
    return delta, m_new, v_new


def _sum_adamw(parts, w, m, v, *, name, tr, rider=None):
    r, c = w.shape

    def body(p_ref, w_ref, m_ref, v_ref, g_ref, d_ref, mo_ref, vo_ref):
        g = p_ref[0].astype(F32)
        for k in range(1, N_DEV):
            g = g + p_ref[k].astype(F32)
        g_ref[...] = g
        d_ref[...], mo_ref[...], vo_ref[...] = _adamw(g, w_ref[...], m_ref[...], v_ref[...])

    blk = pl.BlockSpec((tr, c), lambda i: (i, 0))
    res, extra = _pcall(
        body, name=name, grid=(r // tr,),
        in_specs=[pl.BlockSpec((N_DEV, tr, c), lambda i: (0, i, 0)), blk, blk, blk],
        out_specs=[blk] * 4, out_shape=[jax.ShapeDtypeStruct((r, c), F32)] * 4,
        semantics=("parallel",), vmem_mb=40, rider=rider,
    )(parts, w, m, v)
    return res if rider is None else (res, extra)


def _sum_small(parts):
    _, r, c = parts.shape

    def body(p_ref, o_ref):
        s = p_ref[0]
        for k in range(1, N_DEV):
            s = s + p_ref[k]
        o_ref[...] = s

    return pl.pallas_call(body, name="sum_small", out_shape=jax.ShapeDtypeStruct((r, c), F32))(parts)


def _adamw_small(g, w, m, v):
    def body(g_ref, w_ref, m_ref, v_ref, d_ref, mo_ref, vo_ref):
        d_ref[...], mo_ref[...], vo_ref[...] = _adamw(g_ref[...], w_ref[...], m_ref[...], v_ref[...])

    return pl.pallas_call(body, name="adamw_small", out_shape=[jax.ShapeDtypeStruct(g.shape, F32)] * 3)(g, w, m, v)


def _head_sum_matrix():
    r = lax.broadcasted_iota(jnp.int32, (512, 512), 0) // HEAD_DIM
    c = lax.broadcasted_iota(jnp.int32, (512, 512), 1) // HEAD_DIM
    return (r == c).astype(BF16)


_SHARD_AXIS = dict(w_in=1, w_out=0, w_q=0, w_kv=1, w_o=0, w_up=1, w_down=0, conv_w=None, small=None)


class _Weights:
    def __init__(self, full, shards=None):
        self.full = dict(full)
        self.shards = shards

    def rider(self, names, late=False):
        if self.shards is None:
            return None
        return _Gather([self.shards[n] for n in names], [_SHARD_AXIS[n] for n in names], late)

    def arrived(self, names, gathered):
        if gathered is not None:
            for n, g in zip(names, gathered):
                self.full[n] = g.transpose(1, 0, 2).reshape(g.shape[1], -1) if n == "conv_w" else g

    def __getitem__(self, name):
        return self.full[name]


class _Grads:
    def __init__(self, distributed):
        self.distributed = distributed
        self.local = {}
        self.received = {}

    def add(self, name, g):
        self.local[name] = g

    def rider(self, names, pieces=None):
        if not self.distributed:
            return None
        into = [self.received.get(n) for n in names]
        return _Exchange([self.local[n] for n in names], [_SHARD_AXIS[n] for n in names], pieces, into)

    def arrived(self, names, received):
        if received is not None:
            for n, r in zip(names, received):
                self.received[n] = r


def _ride(fn, *args, rider=None, **kw):
    if rider is None:
        return fn(*args, **kw), None
    return fn(*args, rider=rider, **kw)


def _local_step(x, mem, tgt, gains, weights, grads):
    names = ["w_in", "conv_w"]
    (x, tgt), got = _ride(_reorder, [x, tgt], False, "reorder_in", rider=weights.rider(names, late=True))
    weights.arrived(names, got)
    w_in, cw = weights["w_in"], weights["conv_w"]

    names = ["w_out", "w_kv"]
    (proj, h1), got = _ride(_norm_matmul, x, gains["g_mix"], w_in, name="proj", out_dtype=F32, tb=1024, bn=768,
                            save_h=True, rider=weights.rider(names))
    weights.arrived(names, got)
    names = ["w_q", "w_o", "w_up"]
    (attn, *lses), got = _ride(_attention_fwd, proj, rider=weights.rider(names))
    weights.arrived(names, got)
    x1, merged = _mixer_fwd(x, attn, proj, cw, gains["g_attn_out"], gains["g_conv_out"], weights["w_out"])
    kv, mem_n = _norm_matmul(mem, gains["g_mem"], weights["w_kv"], name="mem_kv", out_dtype=BF16, tb=mem.shape[0],
                             bn=1024, save_h=True)
    x2, h2, qm, om = _xattn_fwd(x1, gains["g_xattn"], weights["w_q"], kv, weights["w_o"], tb=512)
    w_up = weights["w_up"]
    (a, h3), got = _ride(_norm_matmul, x2, gains["g_mlp"], w_up, name="mlp_up", out_dtype=BF16, tb=1024, bn=1024,
                         relu=True, save_h=True, rider=weights.rider(["w_down"], late=True))
    weights.arrived(["w_down"], got)
    w_down = weights["w_down"]
    dx3, dx3b, loss_blk, gg_final = _mlp_down_loss(a, w_down, x2, tgt, gains["g_final"], tb=256)

    def sending(sends, fn, *args, **kw):
        names = [s[0] for s in sends]
        res, got = _ride(fn, *args, rider=grads.rider(names, [s[1:] for s in sends]), **kw)
        grads.arrived(names, got)
        return res

    dpre = _mlp_dpre(dx3b, w_down, a, tb=1024, bn=1024)
    grads.add("w_down", _matmul_tn(a, dx3b, name="grad_w_down", bm=512, bn=1024, square_a=True))
    grads.add("w_up", sending([("w_down", 0, 3, 8)], _matmul_tn, h3, dpre, name="grad_w_up", bm=1024, bn=512))
    dx2, dx2b, gg_mlp = sending([("w_down", 3, 4, 8)], _matmul_nt_normbwd, dpre, w_up, x2, gains["g_mlp"], dx3,
                                name="mlp_dx", tb=512, also_bf16=True)

    grads.add("w_o", _matmul_tn(om, dx2b, name="grad_w_o", bm=1024, bn=512))
    dx1, dx1b, dqm, dk, dv, gg_xattn = sending([("w_up", 0, 4, 8)], _xattn_bwd, dx2, x1, gains["g_xattn"], qm,
                                               weights["w_q"], kv, weights["w_o"], tb=512)
    grads.add("w_q", _matmul_tn(h2, dqm, name="grad_w_q", bm=1024, bn=512))
    dkv = jnp.concatenate([dk, dv], axis=1).astype(BF16)
    grads.add("w_kv", _matmul_tn(mem_n, dkv, name="grad_w_kv", bm=1024, bn=1024))
    _, gg_mem = _matmul_nt_normbwd(dkv, weights["w_kv"], mem, gains["g_mem"], None, name="mem_dx", tb=mem.shape[0])

    grads.add("w_out", _matmul_tn(merged, dx1b, name="grad_w_out", bm=1024, bn=512))
    dattn, dsum, dy, gg_attn, gg_conv = sending(
        [("w_up", 4, 3, 8)], _mixer_bwd, dx1, attn, proj, cw, gains["g_attn_out"], gains["g_conv_out"],
        weights["w_out"], _head_sum_matrix())
    dproj, gcw = _conv_bwd(dy, proj, cw)
    dproj = sending([("w_down", 7, 1, 8), ("w_up", 7, 1, 8), ("w_o", 0, 1, 1), ("w_q", 0, 1, 1), ("w_kv", 0, 1, 1)],
                    _attention_bwd, proj, dattn, dsum, lses, dproj)
    grads.add("w_in", sending([("w_out", 0, 1, 1)], _matmul_tn, h1, dproj, name="grad_w_in", bm=1024, bn=512))
    grad_x, gg_mix = sending([("w_in", 0, 4, 8)], _matmul_nt_normbwd, dproj, w_in, x, gains["g_mix"], dx1,
                             name="mixer_dx", tb=512)

    rows = [gg_mix, gg_xattn, gg_mem, gg_mlp, gg_final, jnp.concatenate([gg_attn, gg_conv], axis=1),
            jnp.pad(gcw[0:3], ((0, 0), (0, 512))), jnp.pad(loss_blk[0:1, 0:1], ((0, 6), (0, 1023)))]
    grads.add("small", jnp.concatenate(rows, axis=0))
    (grad_x,) = _reorder([grad_x], True, "reorder_out")
    return grad_x


_BIG = ("w_in", "w_out", "w_q", "w_kv", "w_o", "w_up", "w_down")
_GAIN_ROWS = ("g_mix", "g_xattn", "g_mem", "g_mlp", "g_final")


def _pack_small(vals, conv):
    rows = [vals[k].reshape(1, -1) for k in _GAIN_ROWS]
    rows.append(jnp.concatenate([vals["g_attn_out"].reshape(1, -1), vals["g_conv_out"].reshape(1, -1)], axis=1))
    flat = conv.reshape(1, -1)
    rows.append(jnp.pad(flat, ((0, 0), (0, 1024 - flat.shape[1]))))
    rows.append(jnp.zeros((1, 1024), F32))
    return jnp.concatenate(rows, axis=0)


def kernel(x, mem, g_mix, w_in, conv_w, g_attn_out, g_conv_out, w_out, g_xattn, g_mem, w_q_mem, w_kv_mem, w_o_mem, g_mlp, w_up, w_down, g_final, loss_target, m_g_mix, m_w_in, m_conv_w, m_g_attn_out, m_g_conv_out, m_w_out, m_g_xattn, m_g_mem, m_w_q_mem, m_w_kv_mem, m_w_o_mem, m_g_mlp, m_w_up, m_w_down, m_g_final, v_g_mix, v_w_in, v_conv_w, v_g_attn_out, v_g_conv_out, v_w_out, v_g_xattn, v_g_mem, v_w_q_mem, v_w_kv_mem, v_w_o_mem, v_g_mlp, v_w_up, v_w_down, v_g_final):
    d = x.shape[-1]
    me = 4 * lax.axis_index("x") + 2 * lax.axis_index("y") + lax.axis_index("c")
    w_shards = dict(w_in=w_in, w_out=w_out, w_q=w_q_mem, w_kv=w_kv_mem, w_o=w_o_mem, w_up=w_up, w_down=w_down)
    m_shards = dict(w_in=m_w_in, w_out=m_w_out, w_q=m_w_q_mem, w_kv=m_w_kv_mem, w_o=m_w_o_mem, w_up=m_w_up,
                    w_down=m_w_down)
    v_shards = dict(w_in=v_w_in, w_out=v_w_out, w_q=v_w_q_mem, w_kv=v_w_kv_mem, w_o=v_w_o_mem, w_up=v_w_up,
                    w_down=v_w_down)
    gains = dict(g_mix=g_mix, g_attn_out=g_attn_out, g_conv_out=g_conv_out, g_xattn=g_xattn, g_mem=g_mem,
                 g_mlp=g_mlp, g_final=g_final)
    gains2 = {k: v.reshape(1, -1) for k, v in gains.items()}

    shards = {k: w_shards[k].astype(BF16) for k in _BIG}
    shards["conv_w"] = conv_w
    grads = _Grads(distributed=True)
    grad_x = _local_step(x[0], mem[0], loss_target[0], gains2, _Weights({}, shards), grads)

    outs = {}
    tiles = dict(w_in=256, w_out=128, w_q=128, w_kv=256, w_o=128, w_up=256, w_down=256)
    for k in ("w_up", "w_down", "w_out", "w_q", "w_kv", "w_o", "w_in"):
        last = ["w_in", "small"]
        rider = grads.rider(last, [(4, 4, 8), (0, 1, 1)]) if k == "w_up" else None
        outs[k], got = _ride(_sum_adamw, grads.received[k], w_shards[k], m_shards[k], v_shards[k],
                             name=f"adamw_{k}", tr=tiles[k], rider=rider)
        grads.arrived(last, got)
    small_received = grads.received["small"]

    ssum = _sum_small(small_received)
    loss = ssum[9, 0]
    g_small = {k: ssum[i] for i, k in enumerate(_GAIN_ROWS)}
    g_small["g_attn_out"] = ssum[5, 0:512]
    g_small["g_conv_out"] = ssum[5, 512:1024]
    g_conv = lax.dynamic_slice_in_dim(ssum[6:9, 0:512], me * 64, 64, axis=1)
    m_small = dict(g_mix=m_g_mix, g_attn_out=m_g_attn_out, g_conv_out=m_g_conv_out, g_xattn=m_g_xattn,
                   g_mem=m_g_mem, g_mlp=m_g_mlp, g_final=m_g_final)
    v_small = dict(g_mix=v_g_mix, g_attn_out=v_g_attn_out, g_conv_out=v_g_conv_out, g_xattn=v_g_xattn,
                   g_mem=v_g_mem, g_mlp=v_g_mlp, g_final=v_g_final)
    packed = [_pack_small(g_small, g_conv), _pack_small(gains, conv_w), _pack_small(m_small, m_conv_w),
              _pack_small(v_small, v_conv_w)]
    upd = _adamw_small(*packed)

    def unpack(p):
        res = {k: p[i] for i, k in enumerate(_GAIN_ROWS)}
        res["g_attn_out"] = p[5, 0:512]
        res["g_conv_out"] = p[5, 512:1024]
        res["conv_w"] = p[6, 0:192].reshape(3, 64)
        return res

    g_small["conv_w"] = g_conv
    small_out = [g_small] + [unpack(p) for p in upd]
    names = {"g_mix": "g_mix", "w_in": "w_in", "conv_w": "conv_w", "g_attn_out": "g_attn_out",
             "g_conv_out": "g_conv_out", "w_out": "w_out", "g_xattn": "g_xattn", "g_mem": "g_mem",
             "w_q_mem": "w_q", "w_kv_mem": "w_kv", "w_o_mem": "w_o", "g_mlp": "g_mlp", "w_up": "w_up",
             "w_down": "w_down", "g_final": "g_final"}
    result = [loss, grad_x[None]]
    for which in range(4):
        for key in names.values():
            result.append(outs[key][which] if key in outs else small_out[which][key])
    return tuple(result)
```

```python
import math

import jax
import jax.numpy as jnp
from jax import lax
from jax.experimental import pallas as pl
from jax.experimental.pallas import tpu as pltpu

F32 = jnp.float32
BF16 = jnp.bfloat16
NORM_EPS = 1e-6
NEG_INF = -1e30
N_DEV = 8
BLK = 128
HEAD_DIM = 64
N_MEM_HEADS = 4
ADAM_LR = 0.001
ADAM_B1 = 0.9
ADAM_B2 = 0.999
ADAM_EPS = 1e-08
ADAM_WD = 0.01
ADAM_STEP = 10
MESH = pl.DeviceIdType.MESH
ANY = pl.BlockSpec(memory_space=pl.ANY)


def _dot(a, b):
    return jnp.dot(a, b, preferred_element_type=F32)


def _dot_nt(a, b):
    return lax.dot_general(a, b, (((1,), (1,)), ((), ())), preferred_element_type=F32)


def _dot_tn(a, b):
    return lax.dot_general(a, b, (((0,), (0,)), ((), ())), preferred_element_type=F32)


def _params(semantics, vmem_mb):
    return pltpu.CompilerParams(dimension_semantics=semantics, vmem_limit_bytes=vmem_mb << 20)


def _rms_fwd(x, g):
    r = lax.rsqrt(jnp.mean(x * x, axis=-1, keepdims=True) + NORM_EPS)
    xh = x * r
    return xh * g, xh, r


def _rms_bwd(dy, xh, r, g):
    gy = dy * g
    return r * (gy - xh * jnp.mean(xh * gy, axis=-1, keepdims=True))


def _position():
    x, y, c = lax.axis_index("x"), lax.axis_index("y"), lax.axis_index("c")
    return x, y, c


def _block_of(ref, j, axis, shard_shape):
    r, c = shard_shape
    if axis is None:
        return ref.at[j]
    if axis == 0:
        return ref.at[pl.ds(j * r, r), :]
    return ref.at[:, pl.ds(j * c, c)]


class _Gather:
    has_mid = True
    alias_pairs = ()

    def __init__(self, shards, axes, late=False):
        self.arrays = list(shards)
        self.axes = list(axes)
        self.late = late
        self.n = len(self.arrays)

    def out_shape(self):
        res = []
        for s, axis in zip(self.arrays, self.axes):
            r, c = s.shape
            shape = (N_DEV, r, c) if axis is None else (N_DEV * r, c) if axis == 0 else (r, N_DEV * c)
            res.append(jax.ShapeDtypeStruct(shape, s.dtype))
        return res

    def scratch(self):
        return [pltpu.SemaphoreType.DMA((self.n, 7)), pltpu.SemaphoreType.DMA((self.n, 7)),
                pltpu.SemaphoreType.DMA((self.n,))]

    def _ctx(self, ins, outs, sems):
        send_sems, recv_sems, local_sems = sems
        x, y, c = _position()
        me, sibling = (x, y, c), (x, y, 1 - c)
        chips = [(1 - x, y), (x, 1 - y), (1 - x, 1 - y)]

        def lin(px, py, pc):
            return 4 * px + 2 * py + pc

        def place(a, block):
            return _block_of(outs[a], lin(*block), self.axes[a], self.arrays[a].shape)

        def copy(a, k, block, to, src=None):
            dst = place(a, block)
            return pltpu.make_async_remote_copy(
                src_ref=dst if src is None else src, dst_ref=dst,
                send_sem=send_sems.at[a, k], recv_sem=recv_sems.at[a, k],
                device_id=to, device_id_type=MESH)

        def mine():
            return [pltpu.make_async_copy(ins[a], place(a, me), local_sems.at[a]) for a in range(self.n)]

        def first():
            res = []
            for a in range(self.n):
                res.append(copy(a, 0, me, sibling, src=ins[a]))
                res += [copy(a, 1 + j, me, (*chip, c), src=ins[a]) for j, chip in enumerate(chips)]
            return res

        return c, me, sibling, chips, copy, mine, first

    def start(self, ins, outs, sems):
        _, _, _, _, _, mine, first = self._ctx(ins, outs, sems)
        for cp in mine() + first():
            cp.start()

    def mid(self, ins, outs, sems):
        c, me, sibling, chips, copy, _, _ = self._ctx(ins, outs, sems)
        for j, chip in enumerate(chips):
            for a in range(self.n):
                copy(a, 1 + j, (*chip, c), me).wait_recv()
                copy(a, 4 + j, (*chip, c), sibling).start()

    def finish(self, ins, outs, sems):
        c, me, sibling, chips, copy, mine, first = self._ctx(ins, outs, sems)
        for a in range(self.n):
            copy(a, 0, sibling, me).wait_recv()
            for j, chip in enumerate(chips):
                copy(a, 4 + j, (*chip, 1 - c), me).wait_recv()
        for cp in first():
            cp.wait_send()
        for j, chip in enumerate(chips):
            for a in range(self.n):
                copy(a, 4 + j, (*chip, c), sibling).wait_send()
        for cp in mine():
            cp.wait()


class _Exchange:
    has_mid = False

    def __init__(self, parts, axes, pieces=None, into=None):
        self.n = len(parts)
        self.axes = list(axes)
        self.pieces = list(pieces or [(0, 1, 1)] * self.n)
        into = list(into or [None] * self.n)
        kept = [a for a in range(self.n) if into[a] is not None]
        self.arrays = list(parts) + [into[a] for a in kept]
        self.alias_pairs = [(self.n + i, a) for i, a in enumerate(kept)]

    def _piece(self, a):
        r, c = self.arrays[a].shape
        axis = self.axes[a]
        return (r, c) if axis is None else (r // N_DEV, c) if axis == 0 else (r, c // N_DEV)

    def _rows(self, a):
        first, count, of = self.pieces[a]
        unit = self._piece(a)[0] // of
        return pl.ds(first * unit, count * unit)

    def out_shape(self):
        return [jax.ShapeDtypeStruct((N_DEV,) + self._piece(a), self.arrays[a].dtype) for a in range(self.n)]

    def scratch(self):
        return [pltpu.SemaphoreType.DMA((self.n, 7)), pltpu.SemaphoreType.DMA((self.n, 7)),
                pltpu.SemaphoreType.DMA((self.n,))]

    def _ctx(self, ins, outs, sems):
        send_sems, recv_sems, local_sems = sems
        x, y, c = _position()
        me = 4 * x + 2 * y + c

        def src(a, j):
            block = ins[a] if self.axes[a] is None else _block_of(ins[a], j, self.axes[a], self._piece(a))
            return block.at[self._rows(a), :]

        def dst(a, j):
            return outs[a].at[j, self._rows(a), :]

        def local():
            return [pltpu.make_async_copy(src(a, me), dst(a, me), local_sems.at[a]) for a in range(self.n)]

        def remote(inbound):
            res = []
            for a in range(self.n):
                for k in range(1, N_DEV):
                    peer = (1 - x if k & 4 else x, 1 - y if k & 2 else y, 1 - c if k & 1 else c)
                    plin = 4 * peer[0] + 2 * peer[1] + peer[2]
                    res.append(pltpu.make_async_remote_copy(
                        src_ref=src(a, plin), dst_ref=dst(a, plin if inbound else me),
                        send_sem=send_sems.at[a, k - 1], recv_sem=recv_sems.at[a, k - 1],
                        device_id=peer, device_id_type=MESH))
            return res

        return local, remote

    def start(self, ins, outs, sems):
        local, remote = self._ctx(ins, outs, sems)
        for cp in local() + remote(False):
            cp.start()

    def finish(self, ins, outs, sems):
        local, remote = self._ctx(ins, outs, sems)
        for cp in remote(True):
            cp.wait_recv()
        for cp in remote(False):
            cp.wait_send()
        for cp in local():
            cp.wait()


class _PerCopy:
    def __init__(self, ref):
        self.ref = ref

    @property
    def at(self):
        return self

    def __getitem__(self, idx):
        a, k = idx
        return self.ref.at[7 * a + k]


def _exchange_start(rider, name):
    n = rider.n
    parts = rider.arrays[:n]
    kept = dict((a, rider.arrays[i]) for i, a in rider.alias_pairs)
    lands = [kept[a] if a in kept else lax.empty(s.shape, s.dtype) for a, s in enumerate(rider.out_shape())]
    hbm = pl.BlockSpec(memory_space=pltpu.HBM)
    sem = pl.BlockSpec(memory_space=pltpu.SEMAPHORE)

    def body(*refs):
        ins, sems = refs[:n], refs[2 * n:2 * n + 3]
        outs, token = refs[2 * n + 3 + n:2 * n + 3 + 2 * n], refs[-1]
        rider.start(ins, outs, (_PerCopy(sems[0]), _PerCopy(sems[1]), sems[2]))
        token[...] = jnp.zeros_like(token)

    res = pl.pallas_call(
        body, name=name,
        out_shape=[pltpu.SemaphoreType.DMA((7 * n,)), pltpu.SemaphoreType.DMA((7 * n,)),
                   pltpu.SemaphoreType.DMA((n,))] + [pltpu.HBM(p.shape, p.dtype) for p in parts]
                  + [pltpu.HBM(z.shape, z.dtype) for z in lands] + [jax.ShapeDtypeStruct((8, 128), F32)],
        in_specs=[hbm] * (2 * n), out_specs=[sem] * 3 + [hbm] * (2 * n) + [pl.BlockSpec(memory_space=pltpu.VMEM)],
        input_output_aliases={i: 3 + i for i in range(2 * n)},
        compiler_params=pltpu.CompilerParams(has_side_effects=pltpu.SideEffectType.DATAFLOW_SIDE_EFFECTING),
    )(*[pltpu.with_memory_space_constraint(a, pltpu.HBM) for a in parts + lands])
    return res[:3], res[3:3 + n], res[3 + n:3 + 2 * n], res[-1]


def _exchange_wait(rider, started, after, name):
    n = rider.n
    sems, parts, lands, _ = started
    hbm = pl.BlockSpec(memory_space=pltpu.HBM)
    sem = pl.BlockSpec(memory_space=pltpu.SEMAPHORE)

    def body(*refs):
        ins, outs, sem_refs = refs[:n], refs[n:2 * n], refs[2 * n:2 * n + 3]
        rider.finish(ins, outs, (_PerCopy(sem_refs[0]), _PerCopy(sem_refs[1]), sem_refs[2]))

    res = pl.pallas_call(
        body, name=name, out_shape=[pltpu.HBM(a.shape, a.dtype) for a in list(parts) + list(lands)],
        in_specs=[hbm] * (2 * n) + [sem] * 3 + [ANY] * len(after), out_specs=[hbm] * (2 * n),
        input_output_aliases={i: i for i in range(2 * n)},
        compiler_params=pltpu.CompilerParams(has_side_effects=pltpu.SideEffectType.DATAFLOW_SIDE_EFFECTING),
    )(*parts, *lands, *sems, *after)
    return list(res[n:])


def _pcall(body, *, name, grid, in_specs, out_specs, out_shape, scratch_shapes=(), semantics, vmem_mb, rider=None,
           aliases=None):
    in_specs, out_specs, out_shape = list(in_specs), list(out_specs), list(out_shape)
    scratch_shapes = list(scratch_shapes)
    aliases = dict(aliases or {})
    if rider is None:
        call = pl.pallas_call(body, name=name, grid=grid, in_specs=in_specs, out_specs=out_specs,
                              out_shape=out_shape, scratch_shapes=scratch_shapes, input_output_aliases=aliases,
                              compiler_params=_params(semantics, vmem_mb))
        return lambda *args: (list(call(*args)), None)
    n_in, n_out, n_scr = len(in_specs), len(out_specs), len(scratch_shapes)
    r_in, r_shapes = len(rider.arrays), rider.out_shape()
    r_out = len(r_shapes)
    aliases.update({n_in + i: n_out + o for i, o in rider.alias_pairs})
    total = math.prod(grid)
    mid_step = total - 1 if rider.has_mid and rider.late else (3 * total) // 4

    def wrapped(*refs):
        bounds = [0, n_in, r_in, n_out, r_out, n_scr]
        for i in range(1, len(bounds)):
            bounds[i] += bounds[i - 1]
        a, ra, o, ro, s = (refs[bounds[i]:bounds[i + 1]] for i in range(5))
        rs = refs[bounds[5]:]
        step = pl.program_id(0)
        for k in range(1, len(grid)):
            step = step * grid[k] + pl.program_id(k)
        pl.when(step == 0)(lambda: rider.start(ra, ro, rs))
        body(*a, *o, *s)
        if rider.has_mid:
            pl.when(step == mid_step)(lambda: rider.mid(ra, ro, rs))
        pl.when(step == total - 1)(lambda: rider.finish(ra, ro, rs))

    call = pl.pallas_call(
        wrapped, name=name, grid=grid, in_specs=in_specs + [ANY] * r_in, out_specs=out_specs + [ANY] * r_out,
        out_shape=out_shape + r_shapes, scratch_shapes=scratch_shapes + rider.scratch(),
        input_output_aliases=aliases, compiler_params=_params(("arbitrary",) * len(grid), vmem_mb))

    def run(*args):
        res = call(*args, *rider.arrays)
        return list(res[:n_out]), list(res[n_out:])

    return run


def _norm_matmul(x, g, w, *, name, out_dtype, tb, bn, relu=False, save_h=False, rider=None):
    t, d = x.shape
    n = w.shape[1]

    def body(x_ref, g_ref, w_ref, o_ref, *rest):
        h_scr = rest[-1]

        @pl.when(pl.program_id(1) == 0)
        def _():
            h = _rms_fwd(x_ref[...], g_ref[...])[0].astype(BF16)
            h_scr[...] = h
            if save_h:
                rest[0][...] = h

        acc = _dot(h_scr[...], w_ref[...])
        if relu:
            acc = jnp.maximum(acc, 0.0)
        o_ref[...] = acc.astype(out_dtype)

    out_shape = [jax.ShapeDtypeStruct((t, n), out_dtype)]
    out_specs = [pl.BlockSpec((tb, bn), lambda i, j: (i, j))]
    if save_h:
        out_shape.append(jax.ShapeDtypeStruct((t, d), BF16))
        out_specs.append(pl.BlockSpec((tb, d), lambda i, j: (i, 0)))
    res, extra = _pcall(
        body, name=name, grid=(t // tb, n // bn),
        in_specs=[pl.BlockSpec((tb, d), lambda i, j: (i, 0)),
                  pl.BlockSpec((1, d), lambda i, j: (0, 0)),
                  pl.BlockSpec((d, bn), lambda i, j: (0, j))],
        out_specs=out_specs, out_shape=out_shape,
        scratch_shapes=[pltpu.VMEM((tb, d), BF16)],
        semantics=("parallel", "arbitrary"), vmem_mb=48, rider=rider,
    )(x, g, w)
    res = res if save_h else res[0]
    return res if rider is None else (res, extra)


def _matmul_nt_normbwd(dy, w, x, g, dres, *, name, tb, also_bf16=False, rider=None):
    t, d = x.shape
    stacked = dy.ndim == 3
    has_res = dres is not None

    def body(dy_ref, w_ref, x_ref, g_ref, *rest):
        rest = list(rest)
        dres_ref = rest.pop(0) if has_res else None
        dx_ref = rest.pop(0)
        dxb_ref = rest.pop(0) if also_bf16 else None
        gg_ref = rest.pop(0)
        i = pl.program_id(0)
        if stacked:
            kb = dy_ref.shape[2]
            dh = _dot_nt(dy_ref[0], w_ref[:, 0:kb])
            for s in range(1, dy_ref.shape[0]):
                dh = dh + _dot_nt(dy_ref[s], w_ref[:, s * kb:(s + 1) * kb])
        else:
            dh = _dot_nt(dy_ref[...], w_ref[...])
        g_v = g_ref[...]
        _, xh, r = _rms_fwd(x_ref[...], g_v)
        dx = _rms_bwd(dh, xh, r, g_v)
        if has_res:
            dx = dx + dres_ref[...]
        dx_ref[...] = dx
        if also_bf16:
            dxb_ref[...] = dx.astype(BF16)
        part = jnp.sum(dh * xh, axis=0, keepdims=True)

        @pl.when(i == 0)
        def _():
            gg_ref[...] = part

        @pl.when(i != 0)
        def _():
            gg_ref[...] += part

    tok = pl.BlockSpec((tb, d), lambda i: (i, 0))
    row = pl.BlockSpec((1, d), lambda i: (0, 0))
    if stacked:
        dy_spec = pl.BlockSpec((dy.shape[0], tb, dy.shape[2]), lambda i: (0, i, 0))
    else:
        dy_spec = pl.BlockSpec((tb, dy.shape[1]), lambda i: (i, 0))
    in_specs = [dy_spec, pl.BlockSpec(w.shape, lambda i: (0, 0)), tok, row]
    args = [dy, w, x, g]
    if has_res:
        in_specs.append(tok)
        args.append(dres)
    out_specs = [tok] + ([tok] if also_bf16 else []) + [row]
    out_shape = ([jax.ShapeDtypeStruct((t, d), F32)] + ([jax.ShapeDtypeStruct((t, d), BF16)] if also_bf16 else [])
                 + [jax.ShapeDtypeStruct((1, d), F32)])
    res, extra = _pcall(
        body, name=name, grid=(t // tb,), in_specs=in_specs, out_specs=out_specs, out_shape=out_shape,
        semantics=("arbitrary",), vmem_mb=56, rider=rider,
    )(*args)
    return res if rider is None else (res, extra)


def _matmul_tn(a, b, *, name, bm, bn, square_a=False, rider=None):
    t, m = a.shape
    stacked = b.ndim == 3
    n = b.shape[0] * bn if stacked else b.shape[1]

    def body(a_ref, b_ref, o_ref):
        av = a_ref[...]
        if square_a:
            av = av.astype(F32)
            av = (av * av).astype(BF16)
        o_ref[...] = _dot_tn(av, b_ref[...]).astype(BF16)

    res, extra = _pcall(
        body, name=name, grid=(m // bm, n // bn),
        in_specs=[pl.BlockSpec((t, bm), lambda i, j: (0, i)),
                  pl.BlockSpec((None, t, bn), lambda i, j: (j, 0, 0)) if stacked
                  else pl.BlockSpec((t, bn), lambda i, j: (0, j))],
        out_specs=[pl.BlockSpec((bm, bn), lambda i, j: (i, j))], out_shape=[jax.ShapeDtypeStruct((m, n), BF16)],
        semantics=("parallel", "parallel"), vmem_mb=56, rider=rider,
    )(a, b)
    return res[0] if rider is None else (res[0], extra)


N_RES = 16
SEG = 128
HALF = N_RES * SEG
TI = 32


def _x4(a):
    return a.reshape(a.shape[0] // HALF, N_RES, SEG, a.shape[1])


def _reorder(arrays, inverse, name, rider=None):
    t, c = arrays[0].shape
    n = len(arrays)
    n_i = SEG // TI
    natural = pl.BlockSpec((TI * N_RES, c), lambda s: (s, 0))
    major = pl.BlockSpec((1, N_RES, TI, c), lambda s: (s // n_i, 0, s % n_i, 0))

    def body(*refs):
        scr = refs[-1]
        for i_ref, o_ref in zip(refs[:n], refs[n:2 * n]):
            for cb in range(c // BLK):
                cols = slice(cb * BLK, (cb + 1) * BLK)
                slab = scr.at[cb]
                if inverse:
                    for r in range(N_RES):
                        slab[pl.ds(r, TI, stride=N_RES), :] = i_ref[0, r, :, cols]
                    o_ref[:, cols] = slab[...]
                else:
                    slab[...] = i_ref[:, cols]
                    for r in range(N_RES):
                        o_ref[0, r, :, cols] = slab[pl.ds(r, TI, stride=N_RES), :]

    shape4 = (t // HALF, N_RES, SEG, c)
    res, extra = _pcall(
        body, name=name, grid=(t // (TI * N_RES),),
        in_specs=[major if inverse else natural] * n, out_specs=[natural if inverse else major] * n,
        out_shape=[jax.ShapeDtypeStruct((t, c) if inverse else shape4, F32)] * n,
        scratch_shapes=[pltpu.VMEM((c // BLK, TI * N_RES, BLK), F32)],
        semantics=("parallel",), vmem_mb=32, rider=rider,
    )(*[_x4(a) if inverse else a for a in arrays])
    res = [r.reshape(t, c) for r in res]
    return res if rider is None else (res, extra)


_PATTERNS = ((1, 16, 8, SEG), (4, 4, 32, 4 * SEG), (16, 1, SEG, 0))
_FIRST = {1: 1, 4: 4, 16: 16}


def _group_rows(d, g):
    a = g >> 4
    if d == 16:
        base = a * HALF + (g & 15) * SEG
        prev = base - HALF
    elif d == 4:
        c = (g >> 2) & 3
        base = a * HALF + (g & 3) * SEG + c * 32
        prev = jnp.where(c > 0, base - 32, base - HALF + 96)
    else:
        c = g & 15
        base = a * HALF + c * 8
        prev = jnp.where(c > 0, base - 8, base - HALF + 120)
    return base, prev


def _load_rows(ref, base, n, rows, stride):
    parts = [ref[pl.ds(pl.multiple_of(base + j * stride, 8), rows), :] for j in range(n)]
    return parts[0] if n == 1 else jnp.concatenate(parts, axis=0)


def _store_rows(ref, base, val, n, rows, stride, add=False):
    for j in range(n):
        sl = pl.ds(pl.multiple_of(base + j * stride, 8), rows)
        piece = val[j * rows:(j + 1) * rows, :]
        if add:
            ref[sl, :] += piece
        else:
            ref[sl, :] = piece


def _band_bias(n, rows):
    shift = rows.bit_length() - 1
    lq = lax.broadcasted_iota(jnp.int32, (BLK, BLK), 0)
    lk = lax.broadcasted_iota(jnp.int32, (BLK, BLK), 1)
    iq = (lq & (rows - 1)) * n + (lq >> shift)
    ik = (lk & (rows - 1)) * n + (lk >> shift)
    zero = jnp.zeros((BLK, BLK), F32)
    return jnp.where(ik >= iq, zero, NEG_INF), jnp.where(ik <= iq, zero, NEG_INF)


def _set_bias(bias_scr, n, rows):
    prev_b, cur_b = _band_bias(n, rows)
    for half in range(2):
        bias_scr[half * BLK:(half + 1) * BLK, 0:BLK] = prev_b
        bias_scr[half * BLK:(half + 1) * BLK, BLK:2 * BLK] = cur_b


SCALE = 1.0 / math.sqrt(HEAD_DIM)


def _head_consts(value=1.0):
    lane_lo = lax.broadcasted_iota(jnp.int32, (BLK, BLK), 1) < HEAD_DIM
    return lane_lo, [jnp.where(lane_lo, value, 0.0).astype(BF16), jnp.where(lane_lo, 0.0, value).astype(BF16)]


def _stack_heads(v, head_mask):
    return jnp.concatenate([v * head_mask[0], v * head_mask[1]], axis=0)


def _unstack_heads(v2, lane_lo):
    return jnp.where(lane_lo, v2[:BLK], v2[BLK:])


def _rows_per_head(v, lane_lo):
    rolled = pltpu.roll(v, HEAD_DIM, axis=1)
    return jnp.concatenate([jnp.where(lane_lo, v, rolled), jnp.where(lane_lo, rolled, v)], axis=0)


WIDTH = 4


def _loop(lo, hi, fn, width=None):
    if width is None:
        def body(g, carry):
            fn(g)
            return carry

        if hi > lo:
            lax.fori_loop(lo, hi, body, 0)
        return
    while hi > lo:
        trips = (hi - lo) // width
        if trips:
            def body(i, carry, lo=lo, width=width):
                fn([lo + width * i + j for j in range(width)])
                return carry

            lax.fori_loop(0, trips, body, 0)
            lo += trips * width
        width = max(1, width // 2)


def _mix_weights(l1, l2, l3):
    mx = jnp.maximum(jnp.maximum(l1, l2), l3)
    e1, e2, e3 = jnp.exp(l1 - mx), jnp.exp(l2 - mx), jnp.exp(l3 - mx)
    inv = 1.0 / (e1 + e2 + e3)
    return e1 * inv, e2 * inv, e3 * inv


def _attention_fwd(qkv, rider=None):
    t = qkv.shape[0]
    groups = 16 * (t // HALF)

    def body(q_ref, k_ref, v_ref, attn_ref, l1_ref, l2_ref, l3_ref, o_scr, bias_scr):
        lane_lo, q_mask = _head_consts(SCALE)
        l_refs = (l1_ref, l2_ref, l3_ref)
        for p, (d, n, rows, stride) in enumerate(_PATTERNS):
            _set_bias(bias_scr, n, rows)
            o_p, l_p = o_scr.at[p], l_refs[p]

            def block(gs, has_prev):
                at = [_group_rows(d, g) for g in gs]

                def load(ref, b):
                    return _load_rows(ref, b, n, rows, stride).astype(BF16)

                q2 = [_stack_heads(load(q_ref, b), q_mask) for b, _ in at]
                k2 = [load(k_ref, b) for b, _ in at]
                v2 = [load(v_ref, b) for b, _ in at]
                if has_prev:
                    k2 = [jnp.concatenate([load(k_ref, pv), k], axis=0) for (_, pv), k in zip(at, k2)]
                    v2 = [jnp.concatenate([load(v_ref, pv), v], axis=0) for (_, pv), v in zip(at, v2)]
                s = [_dot_nt(q, k) for q, k in zip(q2, k2)]
                s = [x + (bias_scr[...] if has_prev else bias_scr[:, BLK:2 * BLK]) for x in s]
                mx = [jnp.max(x, axis=1, keepdims=True) for x in s]
                e = [jnp.exp(x - m) for x, m in zip(s, mx)]
                den = [jnp.sum(x, axis=1, keepdims=True) for x in e]
                o2 = [_dot(x.astype(BF16), v) * (1.0 / dn) for x, v, dn in zip(e, v2, den)]
                lse2 = [jnp.broadcast_to(m + jnp.log(dn), (2 * BLK, BLK)) for m, dn in zip(mx, den)]
                for (b, _), o, l in zip(at, o2, lse2):
                    _store_rows(o_p, b, _unstack_heads(o, lane_lo), n, rows, stride)
                    _store_rows(l_p, b, _unstack_heads(l, lane_lo), n, rows, stride)

            _loop(0, _FIRST[d], lambda gs: block(gs, False), width=WIDTH)
            _loop(_FIRST[d], groups, lambda gs: block(gs, True), width=WIDTH)

        def mix(i):
            sl = pl.ds(pl.multiple_of(i * 256, 256), 256)
            w = _mix_weights(l1_ref[sl, :], l2_ref[sl, :], l3_ref[sl, :])
            attn_ref[sl, :] = w[0] * o_scr[0, sl, :] + w[1] * o_scr[1, sl, :] + w[2] * o_scr[2, sl, :]

        _loop(0, t // 256, mix)

    def col(c0):
        return pl.BlockSpec((t, BLK), lambda hp: (0, c0 + hp))

    res, extra = _pcall(
        body, name="attention_fwd", grid=(4,), in_specs=[col(0), col(4), col(8)], out_specs=[col(0)] * 4,
        out_shape=[jax.ShapeDtypeStruct((t, 512), F32)] * 4,
        scratch_shapes=[pltpu.VMEM((3, t, BLK), F32), pltpu.VMEM((2 * BLK, 2 * BLK), F32)],
        semantics=("parallel",), vmem_mb=48, rider=rider,
    )(qkv, qkv, qkv)
    return res if rider is None else (res, extra)


def _attention_bwd(qkv, dattn, dsum, lses, dproj, rider=None):
    t = qkv.shape[0]
    groups = 16 * (t // HALF)

    def body(q_ref, k_ref, v_ref, da_ref, ds_ref, l1_ref, l2_ref, l3_ref, kept_ref, out_ref, acc, bias_scr):
        del kept_ref
        lane_lo, head_mask = _head_consts()
        q_mask = _head_consts(SCALE)[1]
        l_refs = (l1_ref, l2_ref, l3_ref)

        def clear(i):
            sl = pl.ds(pl.multiple_of(i * 512, 512), 512)
            for s in range(3):
                acc[s, sl, :] = jnp.zeros((512, BLK), F32)

        _loop(0, t // 512, clear)
        dq_acc, dk_acc, dv_acc = acc.at[0], acc.at[1], acc.at[2]
        for p, (d, n, rows, stride) in enumerate(_PATTERNS):
            _set_bias(bias_scr, n, rows)

            def block(gs, has_prev):
                at = [_group_rows(d, g) for g in gs]

                def load(ref, b):
                    return _load_rows(ref, b, n, rows, stride)

                def put(ref, b, val):
                    _store_rows(ref, b, val, n, rows, stride, add=True)

                def wide(x):
                    return jnp.concatenate([x, x], axis=1) if has_prev else x

                lse = [[load(ref, b) for ref in l_refs] for b, _ in at]
                w = [_mix_weights(*ls)[p] for ls in lse]
                do2 = [_stack_heads((wg * load(da_ref, b)).astype(BF16), head_mask) for wg, (b, _) in zip(w, at)]
                dl2 = [wide(_rows_per_head(wg * load(ds_ref, b), lane_lo)) for wg, (b, _) in zip(w, at)]
                lse2 = [wide(_rows_per_head(ls[p], lane_lo)) for ls in lse]
                q2 = [_stack_heads(load(q_ref, b).astype(BF16), q_mask) for b, _ in at]
                k2 = [load(k_ref, b).astype(BF16) for b, _ in at]
                v2 = [load(v_ref, b).astype(BF16) for b, _ in at]
                if has_prev:
                    k2 = [jnp.concatenate([load(k_ref, pv).astype(BF16), k], axis=0) for (_, pv), k in zip(at, k2)]
                    v2 = [jnp.concatenate([load(v_ref, pv).astype(BF16), v], axis=0) for (_, pv), v in zip(at, v2)]
                s = [_dot_nt(q, k) for q, k in zip(q2, k2)]
                dp = [_dot_nt(do, v) for do, v in zip(do2, v2)]
                pr = [jnp.exp(x + (bias_scr[...] if has_prev else bias_scr[:, BLK:2 * BLK]) - l)
                      for x, l in zip(s, lse2)]
                ds = [(pg * (x - dl)).astype(BF16) for pg, x, dl in zip(pr, dp, dl2)]
                dq2 = [_dot(x, k) * SCALE for x, k in zip(ds, k2)]
                dk2 = [_dot_tn(x, q) for x, q in zip(ds, q2)]
                dv2 = [_dot_tn(pg.astype(BF16), do) for pg, do in zip(pr, do2)]
                for (b, pv), dq, dk, dv in zip(at, dq2, dk2, dv2):
                    put(dq_acc, b, _unstack_heads(dq, lane_lo))
                    if has_prev:
                        put(dk_acc, pv, dk[:BLK])
                        put(dv_acc, pv, dv[:BLK])
                        put(dk_acc, b, dk[BLK:])
                        put(dv_acc, b, dv[BLK:])
                    else:
                        put(dk_acc, b, dk)
                        put(dv_acc, b, dv)

            _loop(0, _FIRST[d], lambda gs: block(gs, False), width=WIDTH)
            _loop(_FIRST[d], groups, lambda gs: block(gs, True), width=WIDTH)

        def emit(i):
            sl = pl.ds(pl.multiple_of(i * 512, 512), 512)
            for s in range(3):
                out_ref[s, sl, :] = acc[s, sl, :].astype(BF16)

        _loop(0, t // 512, emit)

    def col(c0):
        return pl.BlockSpec((t, BLK), lambda hp: (0, c0 + hp))

    res, extra = _pcall(
        body, name="attention_bwd", grid=(4,),
        in_specs=[col(0), col(4), col(8)] + [col(0)] * 5 + [ANY],
        out_specs=[pl.BlockSpec((3, t, BLK), lambda hp: (0, 0, hp))],
        out_shape=[jax.ShapeDtypeStruct(dproj.shape, BF16)],
        scratch_shapes=[pltpu.VMEM((3, t, BLK), F32), pltpu.VMEM((2 * BLK, 2 * BLK), F32)],
        semantics=("parallel",), vmem_mb=56, rider=rider, aliases={8: 0},
    )(qkv, qkv, qkv, dattn, dsum, *lses, dproj)
    return res[0] if rider is None else (res[0], extra)


def _order_specs(t):
    n_i = SEG // TI
    nblk = (t // HALF) * n_i
    per = TI // 8

    def main(c, col=0):
        return pl.BlockSpec((1, N_RES, TI, c), lambda s: (s // n_i, 0, s % n_i, col))

    def before(c, col=0):
        return pl.BlockSpec((1, 2, 8, c), lambda s: (jnp.maximum(s - 1, 0) // n_i, N_RES // 2 - 1,
                                                     (jnp.maximum(s - 1, 0) % n_i) * per + per - 1, col))

    def after(c, col=0):
        return pl.BlockSpec((1, 2, 8, c), lambda s: (jnp.minimum(s + 1, nblk - 1) // n_i, 0,
                                                     (jnp.minimum(s + 1, nblk - 1) % n_i) * per, col))

    return nblk, main, before, after


def _shift_in(v, row_in, up):
    rows = v.shape[0]
    idx = lax.broadcasted_iota(jnp.int32, v.shape, 0)
    fill = jnp.broadcast_to(row_in, v.shape)
    if up:
        return jnp.where(idx == rows - 1, fill, pltpu.roll(v, rows - 1, axis=0))
    return jnp.where(idx == 0, fill, pltpu.roll(v, 1, axis=0))


def _taps_behind(u, before):
    s15 = _shift_in(u[N_RES - 1], before[1, 7:8, :], up=False)
    s14 = _shift_in(u[N_RES - 2], before[0, 7:8, :], up=False)
    m1 = jnp.concatenate([s15[None], u[:N_RES - 1]], axis=0)
    m2 = jnp.concatenate([s14[None], s15[None], u[:N_RES - 2]], axis=0)
    return m1, m2


def _taps_ahead(u, after):
    t0 = _shift_in(u[0], after[0, 0:1, :], up=True)
    t1 = _shift_in(u[1], after[1, 0:1, :], up=True)
    p1 = jnp.concatenate([u[1:], t0[None]], axis=0)
    p2 = jnp.concatenate([u[2:], t0[None], t1[None]], axis=0)
    return p1, p2


def _conv_fwd(gates, before, first, cw):
    bg, cg, xc = gates[..., 0:512], gates[..., 512:1024], gates[..., 1024:1536]
    u = cg * xc
    ub = before[..., 512:1024] * before[..., 1024:1536]
    ub = jnp.where(first, jnp.zeros_like(ub), ub)
    m1, m2 = _taps_behind(u, ub)
    conv = m2 * cw[0:1, :] + m1 * cw[1:2, :] + u * cw[2:3, :]
    return bg, u, m1, m2, conv


def _sum_tokens(v):
    return jnp.sum(jnp.sum(v, axis=0), axis=0, keepdims=True)


def _mixer_fwd(x, attn, gates, cw, g_a, g_c, w_out):
    t, d = x.shape
    nblk, main, before, _ = _order_specs(t)
    rows = N_RES * TI

    def body(x_ref, at_ref, gt_ref, gb_ref, cw_ref, ga_ref, gc_ref, wa_ref, wb_ref, x1_ref, mg_ref):
        an = _rms_fwd(at_ref[0], ga_ref[...])[0].astype(BF16)
        bg, _, _, _, conv = _conv_fwd(gt_ref[0], gb_ref[0], pl.program_id(0) == 0, cw_ref[...])
        cn = _rms_fwd(bg * conv, gc_ref[...])[0].astype(BF16)
        mg_ref[0, :, :, 0:512] = an
        mg_ref[0, :, :, 512:1024] = cn
        y = _dot(an.reshape(rows, 512), wa_ref[...]) + _dot(cn.reshape(rows, 512), wb_ref[...])
        x1_ref[0] = x_ref[0] + y.reshape(N_RES, TI, d)

    const = lambda r, c, i0=0: pl.BlockSpec((r, c), lambda s: (i0, 0))
    x1, merged = pl.pallas_call(
        body, name="mixer_fwd", grid=(nblk,),
        in_specs=[main(d), main(512), main(1536, 1), before(1536, 1), const(3, 512), const(1, 512), const(1, 512),
                  const(512, d), const(512, d, 1)],
        out_specs=[main(d), main(d)],
        out_shape=[jax.ShapeDtypeStruct(_x4(x).shape, F32), jax.ShapeDtypeStruct(_x4(x).shape, BF16)],
        compiler_params=_params(("parallel",), 48),
    )(_x4(x), _x4(attn), _x4(gates), _x4(gates), cw, g_a, g_c, w_out, w_out)
    return x1.reshape(t, d), merged.reshape(t, d)


def _mixer_bwd(dx1, attn, gates, cw, g_a, g_c, w_out, head_sum, rider=None):
    t, d = dx1.shape
    nblk, main, before, _ = _order_specs(t)
    rows = N_RES * TI

    def body(dx_ref, at_ref, gt_ref, gb_ref, cw_ref, ga_ref, gc_ref, wa_ref, wb_ref, hs_ref,
             da_ref, dsum_ref, dy_ref, gga_ref, ggc_ref):
        s = pl.program_id(0)
        dxb = dx_ref[0].reshape(rows, d).astype(BF16)
        dma = _dot_nt(dxb, wa_ref[...]).reshape(N_RES, TI, 512)
        dmc = _dot_nt(dxb, wb_ref[...]).reshape(N_RES, TI, 512)
        attn_v, g_av = at_ref[0], ga_ref[...]
        _, ah, ra = _rms_fwd(attn_v, g_av)
        dattn = _rms_bwd(dma, ah, ra, g_av)
        da_ref[0] = dattn
        z = (dattn * attn_v).reshape(rows, 512)
        hs = hs_ref[...]
        z1 = z.astype(BF16)
        z2 = (z - z1.astype(F32)).astype(BF16)
        dsum_ref[0] = (_dot(z1, hs) + _dot(z2, hs)).reshape(N_RES, TI, 512)
        bg, _, _, _, conv = _conv_fwd(gt_ref[0], gb_ref[0], s == 0, cw_ref[...])
        g_cv = gc_ref[...]
        _, yh, rc = _rms_fwd(bg * conv, g_cv)
        dy_ref[0] = _rms_bwd(dmc, yh, rc, g_cv)
        pa, pc = _sum_tokens(dma * ah), _sum_tokens(dmc * yh)

        @pl.when(s == 0)
        def _():
            gga_ref[...] = pa
            ggc_ref[...] = pc

        @pl.when(s != 0)
        def _():
            gga_ref[...] += pa
            ggc_ref[...] += pc

    const = lambda r, c, i0=0: pl.BlockSpec((r, c), lambda s: (i0, 0))
    shape4 = _x4(attn).shape
    res, extra = _pcall(
        body, name="mixer_bwd", grid=(nblk,),
        in_specs=[main(d), main(512), main(1536, 1), before(1536, 1), const(3, 512), const(1, 512), const(1, 512),
                  const(512, d), const(512, d, 1), const(512, 512)],
        out_specs=[main(512)] * 3 + [const(1, 512), const(1, 512)],
        out_shape=[jax.ShapeDtypeStruct(shape4, F32)] * 3 + [jax.ShapeDtypeStruct((1, 512), F32)] * 2,
        semantics=("arbitrary",), vmem_mb=48, rider=rider,
    )(_x4(dx1), _x4(attn), _x4(gates), _x4(gates), cw, g_a, g_c, w_out, w_out, head_sum)
    res = [r.reshape(t, 512) for r in res[:3]] + res[3:]
    return res if rider is None else (res, extra)


def _conv_bwd(dy, gates, cw, rider=None):
    t = dy.shape[0]
    nblk, main, before, after = _order_specs(t)
    n_i = SEG // TI

    def body(dy_ref, dya_ref, gt_ref, gb_ref, ga_ref, cw_ref, dp_ref, gcw_ref):
        s = pl.program_id(0)
        cw_v, gates_v = cw_ref[...], gt_ref[0]
        bg, u, m1, m2, conv = _conv_fwd(gates_v, gb_ref[0], s == 0, cw_v)
        dy_v = dy_ref[0]
        dconv = dy_v * bg
        dca = dya_ref[0] * ga_ref[0][..., 0:512]
        dca = jnp.where(s == nblk - 1, jnp.zeros_like(dca), dca)
        p1, p2 = _taps_ahead(dconv, dca)
        du = dconv * cw_v[2:3, :] + p1 * cw_v[1:2, :] + p2 * cw_v[0:1, :]
        dp_ref[0, 0] = (dy_v * conv).astype(BF16)
        dp_ref[1, 0] = (du * gates_v[..., 1024:1536]).astype(BF16)
        dp_ref[2, 0] = (du * gates_v[..., 512:1024]).astype(BF16)
        parts = [_sum_tokens(dconv * m2), _sum_tokens(dconv * m1), _sum_tokens(dconv * u)]

        @pl.when(s == 0)
        def _():
            gcw_ref[...] = jnp.zeros_like(gcw_ref)

        for tap in range(3):
            gcw_ref[tap:tap + 1, :] += parts[tap]

    (dproj, gcw), extra = _pcall(
        body, name="conv_bwd", grid=(nblk,),
        in_specs=[main(512), after(512), main(1536, 1), before(1536, 1), after(1536, 1),
                  pl.BlockSpec((3, 512), lambda s: (0, 0))],
        out_specs=[pl.BlockSpec((3, 1, N_RES, TI, 512), lambda s: (1, s // n_i, 0, s % n_i, 0)),
                   pl.BlockSpec((8, 512), lambda s: (0, 0))],
        out_shape=[jax.ShapeDtypeStruct((6, t // HALF, N_RES, SEG, 512), BF16), jax.ShapeDtypeStruct((8, 512), F32)],
        semantics=("arbitrary",), vmem_mb=40, rider=rider,
    )(_x4(dy), _x4(dy), _x4(gates), _x4(gates), _x4(gates), cw)
    res = (dproj.reshape(6, t, 512), gcw)
    return res if rider is None else (res, extra)


def _xattn_fwd(x1, g, w_q, kv, w_o, *, tb):
    t, d = x1.shape
    hd = d // N_MEM_HEADS
    m = kv.shape[0]

    def body(x_ref, g_ref, wq_ref, k_ref, v_ref, wo_ref, x2_ref, h_ref, q_ref, o_ref):
        xv = x_ref[...]
        h = _rms_fwd(xv, g_ref[...])[0].astype(BF16)
        h_ref[...] = h
        q = _dot(h, wq_ref[...]).astype(BF16)
        q_ref[...] = q
        for hh in range(N_MEM_HEADS):
            sl = slice(hh * hd, (hh + 1) * hd)
            s = _dot_nt(q[:, sl], k_ref[:, sl]) * (1.0 / 16.0)
            e = jnp.exp(s - jnp.max(s, axis=1, keepdims=True))
            p = e / jnp.sum(e, axis=1, keepdims=True)
            o_ref[:, sl] = _dot(p.astype(BF16), v_ref[:, sl]).astype(BF16)
        x2_ref[...] = xv + _dot(o_ref[...], wo_ref[...])

    tok = pl.BlockSpec((tb, d), lambda i: (i, 0))
    full = pl.BlockSpec((d, d), lambda i: (0, 0))
    return pl.pallas_call(
        body, name="xattn_fwd", grid=(t // tb,),
        in_specs=[tok, pl.BlockSpec((1, d), lambda i: (0, 0)), full,
                  pl.BlockSpec((m, d), lambda i: (0, 0)), pl.BlockSpec((m, d), lambda i: (0, 1)), full],
        out_specs=[tok] * 4,
        out_shape=[jax.ShapeDtypeStruct((t, d), F32)] + [jax.ShapeDtypeStruct((t, d), BF16)] * 3,
        compiler_params=_params(("parallel",), 48),
    )(x1, g, w_q, kv, kv, w_o)


def _xattn_bwd(dx2, x1, g, q, w_q, kv, w_o, *, tb, rider=None):
    t, d = x1.shape
    hd = d // N_MEM_HEADS
    m = kv.shape[0]

    def body(dx2_ref, x_ref, g_ref, q_ref, wq_ref, k_ref, v_ref, wo_ref,
             dx1_ref, dx1b_ref, dq_ref, dk_ref, dv_ref, gg_ref):
        i = pl.program_id(0)

        @pl.when(i == 0)
        def _():
            dk_ref[...] = jnp.zeros_like(dk_ref)
            dv_ref[...] = jnp.zeros_like(dv_ref)

        dx2 = dx2_ref[...]
        do = _dot_nt(dx2.astype(BF16), wo_ref[...]).astype(BF16)
        for hh in range(N_MEM_HEADS):
            sl = slice(hh * hd, (hh + 1) * hd)
            qh, kh, vh, doh = q_ref[:, sl], k_ref[:, sl], v_ref[:, sl], do[:, sl]
            s = _dot_nt(qh, kh) * (1.0 / 16.0)
            e = jnp.exp(s - jnp.max(s, axis=1, keepdims=True))
            p = e / jnp.sum(e, axis=1, keepdims=True)
            dp = _dot_nt(doh, vh)
            ds = (p * (dp - jnp.sum(dp * p, axis=1, keepdims=True)) * (1.0 / 16.0)).astype(BF16)
            dq_ref[:, sl] = _dot(ds, kh).astype(BF16)
            dk_ref[:, sl] += _dot_tn(ds, qh)
            dv_ref[:, sl] += _dot_tn(p.astype(BF16), doh)
        dh = _dot_nt(dq_ref[...], wq_ref[...])
        g_v = g_ref[...]
        _, xh, r = _rms_fwd(x_ref[...], g_v)
        dx1 = dx2 + _rms_bwd(dh, xh, r, g_v)
        dx1_ref[...] = dx1
        dx1b_ref[...] = dx1.astype(BF16)
        part = jnp.sum(dh * xh, axis=0, keepdims=True)

        @pl.when(i == 0)
        def _():
            gg_ref[...] = part

        @pl.when(i != 0)
        def _():
            gg_ref[...] += part

    tok = pl.BlockSpec((tb, d), lambda i: (i, 0))
    full = pl.BlockSpec((d, d), lambda i: (0, 0))
    acc = pl.BlockSpec((m, d), lambda i: (0, 0))
    res, extra = _pcall(
        body, name="xattn_bwd", grid=(t // tb,),
        in_specs=[tok, tok, pl.BlockSpec((1, d), lambda i: (0, 0)), tok, full,
                  pl.BlockSpec((m, d), lambda i: (0, 0)), pl.BlockSpec((m, d), lambda i: (0, 1)), full],
        out_specs=[tok, tok, tok, acc, acc, pl.BlockSpec((1, d), lambda i: (0, 0))],
        out_shape=[jax.ShapeDtypeStruct((t, d), F32), jax.ShapeDtypeStruct((t, d), BF16),
                   jax.ShapeDtypeStruct((t, d), BF16),
                   jax.ShapeDtypeStruct((m, d), F32), jax.ShapeDtypeStruct((m, d), F32),
                   jax.ShapeDtypeStruct((1, d), F32)],
        semantics=("arbitrary",), vmem_mb=48, rider=rider,
    )(dx2, x1, g, q, w_q, kv, kv, w_o)
    return res if rider is None else (res, extra)


def _mlp_down_loss(a, w_down, x2, tgt, g, *, tb):
    t, d = x2.shape
    f = a.shape[1]

    def body(a_ref, w_ref, x_ref, t_ref, g_ref, dx_ref, dxb_ref, loss_ref, gg_ref):
        i = pl.program_id(0)
        av = a_ref[...].astype(F32)
        x3 = x_ref[...] + _dot((av * av).astype(BF16), w_ref[...])
        g_v = g_ref[...]
        out, xh, r = _rms_fwd(x3, g_v)
        err = out - t_ref[...]
        dout = err * (1.0 / d)
        dx = _rms_bwd(dout, xh, r, g_v)
        dx_ref[...] = dx
        dxb_ref[...] = dx.astype(BF16)
        part = jnp.sum(dout * xh, axis=0, keepdims=True)
        lpart = 0.5 * jnp.sum(jnp.mean(err * err, axis=-1, keepdims=True), axis=0, keepdims=True)
        lpart = jnp.broadcast_to(lpart, loss_ref.shape)

        @pl.when(i == 0)
        def _():
            gg_ref[...] = part
            loss_ref[...] = lpart

        @pl.when(i != 0)
        def _():
            gg_ref[...] += part
            loss_ref[...] += lpart

    tok = pl.BlockSpec((tb, d), lambda i: (i, 0))
    return pl.pallas_call(
        body, name="mlp_down_loss", grid=(t // tb,),
        in_specs=[pl.BlockSpec((tb, f), lambda i: (i, 0)), pl.BlockSpec((f, d), lambda i: (0, 0)), tok, tok,
                  pl.BlockSpec((1, d), lambda i: (0, 0))],
        out_specs=[tok, tok, pl.BlockSpec((8, 128), lambda i: (0, 0)), pl.BlockSpec((1, d), lambda i: (0, 0))],
        out_shape=[jax.ShapeDtypeStruct((t, d), F32), jax.ShapeDtypeStruct((t, d), BF16),
                   jax.ShapeDtypeStruct((8, 128), F32), jax.ShapeDtypeStruct((1, d), F32)],
        compiler_params=_params(("arbitrary",), 56),
    )(a, w_down, x2, tgt, g)


def _mlp_dpre(dx3, w_down, a, *, tb, bn):
    t, d = dx3.shape
    f = a.shape[1]

    def body(dx_ref, w_ref, a_ref, o_ref):
        o_ref[...] = (2.0 * a_ref[...].astype(F32) * _dot_nt(dx_ref[...], w_ref[...])).astype(BF16)

    return pl.pallas_call(
        body, name="mlp_dpre", grid=(t // tb, f // bn),
        in_specs=[pl.BlockSpec((tb, d), lambda i, j: (i, 0)), pl.BlockSpec((bn, d), lambda i, j: (j, 0)),
                  pl.BlockSpec((tb, bn), lambda i, j: (i, j))],
        out_specs=pl.BlockSpec((tb, bn), lambda i, j: (i, j)),
        out_shape=jax.ShapeDtypeStruct((t, f), BF16),
        compiler_params=_params(("parallel", "arbitrary"), 48),
    )(dx3, w_down, a)


def _adamw(gsum, w, m, v):
    m_new = ADAM_B1 * m + (1.0 - ADAM_B1) * gsum
    v_new = ADAM_B2 * v + (1.0 - ADAM_B2) * (gsum * gsum)
    m_hat = m_new / (1.0 - ADAM_B1 ** ADAM_STEP)
    v_hat = v_new / (1.0 - ADAM_B2 ** ADAM_STEP)
    delta = -ADAM_LR * (m_hat / (jnp.sqrt(v_hat) + ADAM_EPS) + ADAM_WD * w)
    return delta, m_new, v_new


def _sum_adamw(parts, w, m, v, *, name, tr, token=None):
    r, c = w.shape

    def body(p_ref, w_ref, m_ref, v_ref, *rest):
        g_ref, d_ref, mo_ref, vo_ref = rest[-4:]
        g = p_ref[0].astype(F32)
        for k in range(1, N_DEV):
            g = g + p_ref[k].astype(F32)
        g_ref[...] = g
        d_ref[...], mo_ref[...], vo_ref[...] = _adamw(g, w_ref[...], m_ref[...], v_ref[...])

    blk = pl.BlockSpec((tr, c), lambda i: (i, 0))
    extra_specs = [] if token is None else [pl.BlockSpec((8, 128), lambda i: (0, 0))]
    return pl.pallas_call(
        body, name=name, grid=(r // tr,),
        in_specs=[pl.BlockSpec((N_DEV, tr, c), lambda i: (0, i, 0)), blk, blk, blk] + extra_specs,
        out_specs=[blk] * 4, out_shape=[jax.ShapeDtypeStruct((r, c), F32)] * 4,
        compiler_params=_params(("parallel",), 40),
    )(parts, w, m, v, *([] if token is None else [token]))


def _sum_small(parts):
    _, r, c = parts.shape

    def body(p_ref, o_ref):
        s = p_ref[0]
        for k in range(1, N_DEV):
            s = s + p_ref[k]
        o_ref[...] = s

    return pl.pallas_call(body, name="sum_small", out_shape=jax.ShapeDtypeStruct((r, c), F32))(parts)


def _adamw_small(g, w, m, v):
    def body(g_ref, w_ref, m_ref, v_ref, d_ref, mo_ref, vo_ref):
        d_ref[...], mo_ref[...], vo_ref[...] = _adamw(g_ref[...], w_ref[...], m_ref[...], v_ref[...])

    return pl.pallas_call(body, name="adamw_small", out_shape=[jax.ShapeDtypeStruct(g.shape, F32)] * 3)(g, w, m, v)


def _head_sum_matrix():
    r = lax.broadcasted_iota(jnp.int32, (512, 512), 0) // HEAD_DIM
    c = lax.broadcasted_iota(jnp.int32, (512, 512), 1) // HEAD_DIM
    return (r == c).astype(BF16)


_SHARD_AXIS = dict(w_in=1, w_out=0, w_q=0, w_kv=1, w_o=0, w_up=1, w_down=0, conv_w=None, small=None)


class _Weights:
    def __init__(self, full, shards=None):
        self.full = dict(full)
        self.shards = shards

    def rider(self, names, late=False):
        if self.shards is None:
            return None
        return _Gather([self.shards[n] for n in names], [_SHARD_AXIS[n] for n in names], late)

    def arrived(self, names, gathered):
        if gathered is not None:
            for n, g in zip(names, gathered):
                self.full[n] = g.transpose(1, 0, 2).reshape(g.shape[1], -1) if n == "conv_w" else g

    def __getitem__(self, name):
        return self.full[name]


class _Grads:
    def __init__(self, distributed):
        self.distributed = distributed
        self.local = {}
        self.received = {}

    def add(self, name, g):
        self.local[name] = g

    def rider(self, names, pieces=None):
        if not self.distributed:
            return None
        into = [self.received.get(n) for n in names]
        return _Exchange([self.local[n] for n in names], [_SHARD_AXIS[n] for n in names], pieces, into)

    def arrived(self, names, received):
        if received is not None:
            for n, r in zip(names, received):
                self.received[n] = r


def _ride(fn, *args, rider=None, **kw):
    if rider is None:
        return fn(*args, **kw), None
    return fn(*args, rider=rider, **kw)


def _local_step(x, mem, tgt, gains, weights, grads):
    names = ["w_in", "conv_w"]
    (x, tgt), got = _ride(_reorder, [x, tgt], False, "reorder_in", rider=weights.rider(names, late=True))
    weights.arrived(names, got)
    w_in, cw = weights["w_in"], weights["conv_w"]

    names = ["w_out", "w_kv"]
    (proj, h1), got = _ride(_norm_matmul, x, gains["g_mix"], w_in, name="proj", out_dtype=F32, tb=1024, bn=768,
                            save_h=True, rider=weights.rider(names))
    weights.arrived(names, got)
    names = ["w_q", "w_o", "w_up"]
    (attn, *lses), got = _ride(_attention_fwd, proj, rider=weights.rider(names))
    weights.arrived(names, got)
    x1, merged = _mixer_fwd(x, attn, proj, cw, gains["g_attn_out"], gains["g_conv_out"], weights["w_out"])
    kv, mem_n = _norm_matmul(mem, gains["g_mem"], weights["w_kv"], name="mem_kv", out_dtype=BF16, tb=mem.shape[0],
                             bn=1024, save_h=True)
    x2, h2, qm, om = _xattn_fwd(x1, gains["g_xattn"], weights["w_q"], kv, weights["w_o"], tb=512)
    w_up = weights["w_up"]
    (a, h3), got = _ride(_norm_matmul, x2, gains["g_mlp"], w_up, name="mlp_up", out_dtype=BF16, tb=1024, bn=1024,
                         relu=True, save_h=True, rider=weights.rider(["w_down"], late=True))
    weights.arrived(["w_down"], got)
    w_down = weights["w_down"]
    dx3, dx3b, loss_blk, gg_final = _mlp_down_loss(a, w_down, x2, tgt, gains["g_final"], tb=256)

    def sending(sends, fn, *args, **kw):
        names = [s[0] for s in sends]
        res, got = _ride(fn, *args, rider=grads.rider(names, [s[1:] for s in sends]), **kw)
        grads.arrived(names, got)
        return res

    dpre = _mlp_dpre(dx3b, w_down, a, tb=1024, bn=1024)
    grads.add("w_down", _matmul_tn(a, dx3b, name="grad_w_down", bm=512, bn=1024, square_a=True))
    grads.add("w_up", sending([("w_down", 0, 3, 8)], _matmul_tn, h3, dpre, name="grad_w_up", bm=1024, bn=512))
    dx2, dx2b, gg_mlp = sending([("w_down", 3, 4, 8)], _matmul_nt_normbwd, dpre, w_up, x2, gains["g_mlp"], dx3,
                                name="mlp_dx", tb=512, also_bf16=True)

    grads.add("w_o", _matmul_tn(om, dx2b, name="grad_w_o", bm=1024, bn=512))
    dx1, dx1b, dqm, dk, dv, gg_xattn = sending([("w_up", 0, 4, 8)], _xattn_bwd, dx2, x1, gains["g_xattn"], qm,
                                               weights["w_q"], kv, weights["w_o"], tb=512)
    grads.add("w_q", _matmul_tn(h2, dqm, name="grad_w_q", bm=1024, bn=512))
    dkv = jnp.concatenate([dk, dv], axis=1).astype(BF16)
    grads.add("w_kv", _matmul_tn(mem_n, dkv, name="grad_w_kv", bm=1024, bn=1024))
    _, gg_mem = _matmul_nt_normbwd(dkv, weights["w_kv"], mem, gains["g_mem"], None, name="mem_dx", tb=mem.shape[0])

    grads.add("w_out", _matmul_tn(merged, dx1b, name="grad_w_out", bm=1024, bn=512))
    dattn, dsum, dy, gg_attn, gg_conv = sending(
        [("w_up", 4, 3, 8)], _mixer_bwd, dx1, attn, proj, cw, gains["g_attn_out"], gains["g_conv_out"],
        weights["w_out"], _head_sum_matrix())
    dproj, gcw = _conv_bwd(dy, proj, cw)
    dproj = sending([("w_down", 7, 1, 8), ("w_up", 7, 1, 8), ("w_o", 0, 1, 1), ("w_q", 0, 1, 1), ("w_kv", 0, 1, 1)],
                    _attention_bwd, proj, dattn, dsum, lses, dproj)
    grads.add("w_in", sending([("w_out", 0, 1, 1)], _matmul_tn, h1, dproj, name="grad_w_in", bm=1024, bn=512))
    grad_x, gg_mix = sending([("w_in", 0, 4, 8)], _matmul_nt_normbwd, dproj, w_in, x, gains["g_mix"], dx1,
                             name="mixer_dx", tb=512)

    def part(v):
        return jnp.pad(v, ((0, SMALL_PART - v.shape[0]), (0, 1024 - v.shape[1])))

    parts = [gg_mix, gg_xattn, gg_mem, gg_mlp, gg_final, jnp.concatenate([gg_attn, gg_conv], axis=1), gcw, loss_blk]
    grads.add("small", jnp.concatenate([part(v) for v in parts], axis=0))
    (grad_x,) = _reorder([grad_x], True, "reorder_out")
    return grad_x


SMALL_PART = 8
_BIG = ("w_in", "w_out", "w_q", "w_kv", "w_o", "w_up", "w_down")
_GAIN_ROWS = ("g_mix", "g_xattn", "g_mem", "g_mlp", "g_final")


def _pack_small(vals, conv):
    rows = [vals[k].reshape(1, -1) for k in _GAIN_ROWS]
    rows.append(jnp.concatenate([vals["g_attn_out"].reshape(1, -1), vals["g_conv_out"].reshape(1, -1)], axis=1))
    flat = conv.reshape(1, -1)
    rows.append(jnp.pad(flat, ((0, 0), (0, 1024 - flat.shape[1]))))
    rows.append(jnp.zeros((1, 1024), F32))
    return jnp.concatenate(rows, axis=0)


def kernel(x, mem, g_mix, w_in, conv_w, g_attn_out, g_conv_out, w_out, g_xattn, g_mem, w_q_mem, w_kv_mem, w_o_mem, g_mlp, w_up, w_down, g_final, loss_target, m_g_mix, m_w_in, m_conv_w, m_g_attn_out, m_g_conv_out, m_w_out, m_g_xattn, m_g_mem, m_w_q_mem, m_w_kv_mem, m_w_o_mem, m_g_mlp, m_w_up, m_w_down, m_g_final, v_g_mix, v_w_in, v_conv_w, v_g_attn_out, v_g_conv_out, v_w_out, v_g_xattn, v_g_mem, v_w_q_mem, v_w_kv_mem, v_w_o_mem, v_g_mlp, v_w_up, v_w_down, v_g_final):
    d = x.shape[-1]
    me = 4 * lax.axis_index("x") + 2 * lax.axis_index("y") + lax.axis_index("c")
    w_shards = dict(w_in=w_in, w_out=w_out, w_q=w_q_mem, w_kv=w_kv_mem, w_o=w_o_mem, w_up=w_up, w_down=w_down)
    m_shards = dict(w_in=m_w_in, w_out=m_w_out, w_q=m_w_q_mem, w_kv=m_w_kv_mem, w_o=m_w_o_mem, w_up=m_w_up,
                    w_down=m_w_down)
    v_shards = dict(w_in=v_w_in, w_out=v_w_out, w_q=v_w_q_mem, w_kv=v_w_kv_mem, w_o=v_w_o_mem, w_up=v_w_up,
                    w_down=v_w_down)
    gains = dict(g_mix=g_mix, g_attn_out=g_attn_out, g_conv_out=g_conv_out, g_xattn=g_xattn, g_mem=g_mem,
                 g_mlp=g_mlp, g_final=g_final)
    gains2 = {k: v.reshape(1, -1) for k, v in gains.items()}

    shards = {k: w_shards[k].astype(BF16) for k in _BIG}
    shards["conv_w"] = conv_w
    grads = _Grads(distributed=True)
    grad_x = _local_step(x[0], mem[0], loss_target[0], gains2, _Weights({}, shards), grads)

    last = grads.rider(["w_in", "small"], [(4, 4, 8), (0, 1, 1)])
    started = _exchange_start(last, "exchange_last_start")
    outs = {}
    tiles = dict(w_in=256, w_out=128, w_q=128, w_kv=256, w_o=128, w_up=256, w_down=256)
    for k in ("w_up", "w_down", "w_out", "w_q", "w_kv", "w_o"):
        outs[k] = _sum_adamw(grads.received[k], w_shards[k], m_shards[k], v_shards[k], name=f"adamw_{k}",
                             tr=tiles[k], token=started[3])
    received_w_in, small_received = _exchange_wait(last, started, [outs[k][0] for k in outs], "exchange_last_wait")
    outs["w_in"] = _sum_adamw(received_w_in, w_shards["w_in"], m_shards["w_in"], v_shards["w_in"], name="adamw_w_in",
                              tr=tiles["w_in"])

    ssum = _sum_small(small_received)
    row = lambda i: ssum[SMALL_PART * i]
    loss = ssum[SMALL_PART * 7, 0]
    g_small = {k: row(i) for i, k in enumerate(_GAIN_ROWS)}
    g_small["g_attn_out"] = row(5)[0:512]
    g_small["g_conv_out"] = row(5)[512:1024]
    taps = ssum[SMALL_PART * 6:SMALL_PART * 6 + 3, 0:512]
    g_conv = lax.dynamic_slice_in_dim(taps, me * 64, 64, axis=1)
    m_small = dict(g_mix=m_g_mix, g_attn_out=m_g_attn_out, g_conv_out=m_g_conv_out, g_xattn=m_g_xattn,
                   g_mem=m_g_mem, g_mlp=m_g_mlp, g_final=m_g_final)
    v_small = dict(g_mix=v_g_mix, g_attn_out=v_g_attn_out, g_conv_out=v_g_conv_out, g_xattn=v_g_xattn,
                   g_mem=v_g_mem, g_mlp=v_g_mlp, g_final=v_g_final)
    packed = [_pack_small(g_small, g_conv), _pack_small(gains, conv_w), _pack_small(m_small, m_conv_w),
              _pack_small(v_small, v_conv_w)]
    upd = _adamw_small(*packed)

    def unpack(p):
        res = {k: p[i] for i, k in enumerate(_GAIN_ROWS)}
        res["g_attn_out"] = p[5, 0:512]
        res["g_conv_out"] = p[5, 512:1024]
        res["conv_w"] = p[6, 0:192].reshape(3, 64)
        return res

    g_small["conv_w"] = g_conv
    small_out = [g_small] + [unpack(p) for p in upd]
    names = {"g_mix": "g_mix", "w_in": "w_in", "conv_w": "conv_w", "g_attn_out": "g_attn_out",
             "g_conv_out": "g_conv_out", "w_out": "w_out", "g_xattn": "g_xattn", "g_mem": "g_mem",
             "w_q_mem": "w_q", "w_kv_mem": "w_kv", "w_o_mem": "w_o", "g_mlp": "g_mlp", "w_up": "w_up",
             "w_down": "w_down", "g_final": "g_final"}
    result = [loss, grad_x[None]]
    for which in range(4):
        for key in names.values():
            result.append(outs[key][which] if key in outs else small_out[which][key])
    return tuple(result)
```

```python
import math

import jax
import jax.numpy as jnp
from jax import lax
from jax.experimental import pallas as pl
from jax.experimental.pallas import tpu as pltpu

F32 = jnp.float32
BF16 = jnp.bfloat16
NORM_EPS = 1e-6
NEG_INF = -1e30
N_DEV = 8
BLK = 128
HEAD_DIM = 64
N_MEM_HEADS = 4
ADAM_LR = 0.001
ADAM_B1 = 0.9
ADAM_B2 = 0.999
ADAM_EPS = 1e-08
ADAM_WD = 0.01
ADAM_STEP = 10
MESH = pl.DeviceIdType.MESH
ANY = pl.BlockSpec(memory_space=pl.ANY)


def _dot(a, b):
    return jnp.dot(a, b, preferred_element_type=F32)


def _dot_nt(a, b):
    return lax.dot_general(a, b, (((1,), (1,)), ((), ())), preferred_element_type=F32)


def _dot_tn(a, b):
    return lax.dot_general(a, b, (((0,), (0,)), ((), ())), preferred_element_type=F32)


def _params(semantics, vmem_mb):
    return pltpu.CompilerParams(dimension_semantics=semantics, vmem_limit_bytes=vmem_mb << 20)


def _rms_fwd(x, g):
    r = lax.rsqrt(jnp.mean(x * x, axis=-1, keepdims=True) + NORM_EPS)
    xh = x * r
    return xh * g, xh, r


def _rms_bwd(dy, xh, r, g):
    gy = dy * g
    return r * (gy - xh * jnp.mean(xh * gy, axis=-1, keepdims=True))


def _position():
    x, y, c = lax.axis_index("x"), lax.axis_index("y"), lax.axis_index("c")
    return x, y, c


def _block_of(ref, j, axis, shard_shape):
    r, c = shard_shape
    if axis is None:
        return ref.at[j]
    if axis == 0:
        return ref.at[pl.ds(j * r, r), :]
    return ref.at[:, pl.ds(j * c, c)]


class _Gather:
    has_mid = True
    alias_pairs = ()

    def __init__(self, shards, axes, late=False):
        self.arrays = list(shards)
        self.axes = list(axes)
        self.late = late
        self.n = len(self.arrays)

    def out_shape(self):
        res = []
        for s, axis in zip(self.arrays, self.axes):
            r, c = s.shape
            shape = (N_DEV, r, c) if axis is None else (N_DEV * r, c) if axis == 0 else (r, N_DEV * c)
            res.append(jax.ShapeDtypeStruct(shape, s.dtype))
        return res

    def scratch(self):
        return [pltpu.SemaphoreType.DMA((self.n, 7)), pltpu.SemaphoreType.DMA((self.n, 7)),
                pltpu.SemaphoreType.DMA((self.n,))]

    def _ctx(self, ins, outs, sems):
        send_sems, recv_sems, local_sems = sems
        x, y, c = _position()
        me, sibling = (x, y, c), (x, y, 1 - c)
        chips = [(1 - x, y), (x, 1 - y), (1 - x, 1 - y)]

        def lin(px, py, pc):
            return 4 * px + 2 * py + pc

        def place(a, block):
            return _block_of(outs[a], lin(*block), self.axes[a], self.arrays[a].shape)

        def copy(a, k, block, to, src=None):
            dst = place(a, block)
            return pltpu.make_async_remote_copy(
                src_ref=dst if src is None else src, dst_ref=dst,
                send_sem=send_sems.at[a, k], recv_sem=recv_sems.at[a, k],
                device_id=to, device_id_type=MESH)

        def mine():
            return [pltpu.make_async_copy(ins[a], place(a, me), local_sems.at[a]) for a in range(self.n)]

        def first():
            res = []
            for a in range(self.n):
                res.append(copy(a, 0, me, sibling, src=ins[a]))
                res += [copy(a, 1 + j, me, (*chip, c), src=ins[a]) for j, chip in enumerate(chips)]
            return res

        return c, me, sibling, chips, copy, mine, first

    def start(self, ins, outs, sems):
        _, _, _, _, _, mine, first = self._ctx(ins, outs, sems)
        for cp in mine() + first():
            cp.start()

    def mid(self, ins, outs, sems):
        c, me, sibling, chips, copy, _, _ = self._ctx(ins, outs, sems)
        for j, chip in enumerate(chips):
            for a in range(self.n):
                copy(a, 1 + j, (*chip, c), me).wait_recv()
                copy(a, 4 + j, (*chip, c), sibling).start()

    def finish(self, ins, outs, sems):
        c, me, sibling, chips, copy, mine, first = self._ctx(ins, outs, sems)
        for a in range(self.n):
            copy(a, 0, sibling, me).wait_recv()
            for j, chip in enumerate(chips):
                copy(a, 4 + j, (*chip, 1 - c), me).wait_recv()
        for cp in first():
            cp.wait_send()
        for j, chip in enumerate(chips):
            for a in range(self.n):
                copy(a, 4 + j, (*chip, c), sibling).wait_send()
        for cp in mine():
            cp.wait()


class _Exchange:
    def __init__(self, parts, axes):
        self.n = len(parts)
        self.axes = list(axes)
        self.arrays = list(parts)

    def _piece(self, a):
        r, c = self.arrays[a].shape
        axis = self.axes[a]
        return (r, c) if axis is None else (r // N_DEV, c) if axis == 0 else (r, c // N_DEV)

    def out_shape(self):
        return [jax.ShapeDtypeStruct((N_DEV,) + self._piece(a), self.arrays[a].dtype) for a in range(self.n)]

    def semaphores(self):
        return [pltpu.SemaphoreType.DMA((7 * self.n,)), pltpu.SemaphoreType.DMA((7 * self.n,)),
                pltpu.SemaphoreType.DMA((self.n,))]

    def _ctx(self, ins, outs, sems):
        send_sems, recv_sems, local_sems = sems
        x, y, c = _position()
        me = 4 * x + 2 * y + c

        def src(a, j):
            return ins[a] if self.axes[a] is None else _block_of(ins[a], j, self.axes[a], self._piece(a))

        def dst(a, j):
            return outs[a].at[j]

        def local():
            return [pltpu.make_async_copy(src(a, me), dst(a, me), local_sems.at[a]) for a in range(self.n)]

        def remote(inbound):
            res = []
            for a in range(self.n):
                for k in range(1, N_DEV):
                    peer = (1 - x if k & 4 else x, 1 - y if k & 2 else y, 1 - c if k & 1 else c)
                    plin = 4 * peer[0] + 2 * peer[1] + peer[2]
                    res.append(pltpu.make_async_remote_copy(
                        src_ref=src(a, plin), dst_ref=dst(a, plin if inbound else me),
                        send_sem=send_sems.at[7 * a + k - 1], recv_sem=recv_sems.at[7 * a + k - 1],
                        device_id=peer, device_id_type=MESH))
            return res

        return local, remote

    def start(self, ins, outs, sems):
        local, remote = self._ctx(ins, outs, sems)
        for cp in local() + remote(False):
            cp.start()

    def finish(self, ins, outs, sems):
        local, remote = self._ctx(ins, outs, sems)
        for cp in remote(True):
            cp.wait_recv()
        for cp in remote(False):
            cp.wait_send()
        for cp in local():
            cp.wait()


def _exchange_start(rider, name):
    n = rider.n
    parts = rider.arrays
    lands = [lax.empty(s.shape, s.dtype) for s in rider.out_shape()]
    hbm = pl.BlockSpec(memory_space=pltpu.HBM)
    sem = pl.BlockSpec(memory_space=pltpu.SEMAPHORE)

    def body(*refs):
        ins, sems = refs[:n], refs[2 * n:2 * n + 3]
        outs, token = refs[2 * n + 3 + n:2 * n + 3 + 2 * n], refs[-1]
        rider.start(ins, outs, sems)
        token[...] = jnp.zeros_like(token)

    res = pl.pallas_call(
        body, name=name,
        out_shape=rider.semaphores() + [pltpu.HBM(p.shape, p.dtype) for p in parts]
                  + [pltpu.HBM(z.shape, z.dtype) for z in lands] + [jax.ShapeDtypeStruct((8, 128), F32)],
        in_specs=[hbm] * (2 * n), out_specs=[sem] * 3 + [hbm] * (2 * n) + [pl.BlockSpec(memory_space=pltpu.VMEM)],
        input_output_aliases={i: 3 + i for i in range(2 * n)},
        compiler_params=pltpu.CompilerParams(has_side_effects=pltpu.SideEffectType.DATAFLOW_SIDE_EFFECTING),
    )(*[pltpu.with_memory_space_constraint(a, pltpu.HBM) for a in parts + lands])
    return res[:3], res[3:3 + n], res[3 + n:3 + 2 * n], res[-1]


def _exchange_wait(rider, started, after, name):
    n = rider.n
    sems, parts, lands, _ = started
    hbm = pl.BlockSpec(memory_space=pltpu.HBM)
    sem = pl.BlockSpec(memory_space=pltpu.SEMAPHORE)

    def body(*refs):
        rider.finish(refs[:n], refs[n:2 * n], refs[2 * n:2 * n + 3])

    res = pl.pallas_call(
        body, name=name, out_shape=[pltpu.HBM(a.shape, a.dtype) for a in list(parts) + list(lands)],
        in_specs=[hbm] * (2 * n) + [sem] * 3 + [ANY] * len(after), out_specs=[hbm] * (2 * n),
        input_output_aliases={i: i for i in range(2 * n)},
        compiler_params=pltpu.CompilerParams(has_side_effects=pltpu.SideEffectType.DATAFLOW_SIDE_EFFECTING),
    )(*parts, *lands, *sems, *after)
    return list(res[n:])


def _pcall(body, *, name, grid, in_specs, out_specs, out_shape, scratch_shapes=(), semantics, vmem_mb, rider=None,
           aliases=None, after=()):
    in_specs, out_specs, out_shape = list(in_specs), list(out_specs), list(out_shape)
    scratch_shapes = list(scratch_shapes)
    aliases = dict(aliases or {})
    if rider is None:
        n_in, after = len(in_specs), list(after)

        def plain(*refs):
            body(*refs[:n_in], *refs[n_in + len(after):])

        call = pl.pallas_call(plain if after else body, name=name, grid=grid, in_specs=in_specs + [ANY] * len(after),
                              out_specs=out_specs, out_shape=out_shape, scratch_shapes=scratch_shapes,
                              input_output_aliases=aliases, compiler_params=_params(semantics, vmem_mb))
        return lambda *args: (list(call(*args, *after)), None)
    n_in, n_out, n_scr = len(in_specs), len(out_specs), len(scratch_shapes)
    r_in, r_shapes = len(rider.arrays), rider.out_shape()
    r_out = len(r_shapes)
    aliases.update({n_in + i: n_out + o for i, o in rider.alias_pairs})
    total = math.prod(grid)
    mid_step = total - 1 if rider.has_mid and rider.late else (3 * total) // 4

    def wrapped(*refs):
        bounds = [0, n_in, r_in, n_out, r_out, n_scr]
        for i in range(1, len(bounds)):
            bounds[i] += bounds[i - 1]
        a, ra, o, ro, s = (refs[bounds[i]:bounds[i + 1]] for i in range(5))
        rs = refs[bounds[5]:]
        step = pl.program_id(0)
        for k in range(1, len(grid)):
            step = step * grid[k] + pl.program_id(k)
        pl.when(step == 0)(lambda: rider.start(ra, ro, rs))
        body(*a, *o, *s)
        if rider.has_mid:
            pl.when(step == mid_step)(lambda: rider.mid(ra, ro, rs))
        pl.when(step == total - 1)(lambda: rider.finish(ra, ro, rs))

    call = pl.pallas_call(
        wrapped, name=name, grid=grid, in_specs=in_specs + [ANY] * r_in, out_specs=out_specs + [ANY] * r_out,
        out_shape=out_shape + r_shapes, scratch_shapes=scratch_shapes + rider.scratch(),
        input_output_aliases=aliases, compiler_params=_params(("arbitrary",) * len(grid), vmem_mb))

    def run(*args):
        res = call(*args, *rider.arrays)
        return list(res[:n_out]), list(res[n_out:])

    return run


def _norm_matmul(x, g, w, *, name, out_dtype, tb, bn, relu=False, save_h=False, rider=None):
    t, d = x.shape
    n = w.shape[1]

    def body(x_ref, g_ref, w_ref, o_ref, *rest):
        h_scr = rest[-1]

        @pl.when(pl.program_id(1) == 0)
        def _():
            h = _rms_fwd(x_ref[...], g_ref[...])[0].astype(BF16)
            h_scr[...] = h
            if save_h:
                rest[0][...] = h

        acc = _dot(h_scr[...], w_ref[...])
        if relu:
            acc = jnp.maximum(acc, 0.0)
        o_ref[...] = acc.astype(out_dtype)

    out_shape = [jax.ShapeDtypeStruct((t, n), out_dtype)]
    out_specs = [pl.BlockSpec((tb, bn), lambda i, j: (i, j))]
    if save_h:
        out_shape.append(jax.ShapeDtypeStruct((t, d), BF16))
        out_specs.append(pl.BlockSpec((tb, d), lambda i, j: (i, 0)))
    res, extra = _pcall(
        body, name=name, grid=(t // tb, n // bn),
        in_specs=[pl.BlockSpec((tb, d), lambda i, j: (i, 0)),
                  pl.BlockSpec((1, d), lambda i, j: (0, 0)),
                  pl.BlockSpec((d, bn), lambda i, j: (0, j))],
        out_specs=out_specs, out_shape=out_shape,
        scratch_shapes=[pltpu.VMEM((tb, d), BF16)],
        semantics=("parallel", "arbitrary"), vmem_mb=48, rider=rider,
    )(x, g, w)
    res = res if save_h else res[0]
    return res if rider is None else (res, extra)


def _matmul_nt_normbwd(dy, w, x, g, dres, *, name, tb, also_bf16=False, after=()):
    t, d = x.shape
    stacked = dy.ndim == 3
    has_res = dres is not None

    def body(dy_ref, w_ref, x_ref, g_ref, *rest):
        rest = list(rest)
        dres_ref = rest.pop(0) if has_res else None
        dx_ref = rest.pop(0)
        dxb_ref = rest.pop(0) if also_bf16 else None
        gg_ref = rest.pop(0)
        i = pl.program_id(0)
        if stacked:
            kb = dy_ref.shape[2]
            dh = _dot_nt(dy_ref[0], w_ref[:, 0:kb])
            for s in range(1, dy_ref.shape[0]):
                dh = dh + _dot_nt(dy_ref[s], w_ref[:, s * kb:(s + 1) * kb])
        else:
            dh = _dot_nt(dy_ref[...], w_ref[...])
        g_v = g_ref[...]
        _, xh, r = _rms_fwd(x_ref[...], g_v)
        dx = _rms_bwd(dh, xh, r, g_v)
        if has_res:
            dx = dx + dres_ref[...]
        dx_ref[...] = dx
        if also_bf16:
            dxb_ref[...] = dx.astype(BF16)
        part = jnp.sum(dh * xh, axis=0, keepdims=True)

        @pl.when(i == 0)
        def _():
            gg_ref[...] = part

        @pl.when(i != 0)
        def _():
            gg_ref[...] += part

    tok = pl.BlockSpec((tb, d), lambda i: (i, 0))
    row = pl.BlockSpec((1, d), lambda i: (0, 0))
    if stacked:
        dy_spec = pl.BlockSpec((dy.shape[0], tb, dy.shape[2]), lambda i: (0, i, 0))
    else:
        dy_spec = pl.BlockSpec((tb, dy.shape[1]), lambda i: (i, 0))
    in_specs = [dy_spec, pl.BlockSpec(w.shape, lambda i: (0, 0)), tok, row]
    args = [dy, w, x, g]
    if has_res:
        in_specs.append(tok)
        args.append(dres)
    out_specs = [tok] + ([tok] if also_bf16 else []) + [row]
    out_shape = ([jax.ShapeDtypeStruct((t, d), F32)] + ([jax.ShapeDtypeStruct((t, d), BF16)] if also_bf16 else [])
                 + [jax.ShapeDtypeStruct((1, d), F32)])
    res, _ = _pcall(
        body, name=name, grid=(t // tb,), in_specs=in_specs, out_specs=out_specs, out_shape=out_shape,
        semantics=("arbitrary",), vmem_mb=56, after=after,
    )(*args)
    return res


def _matmul_tn(a, b, *, name, bm, bn, square_a=False, after=()):
    t, m = a.shape
    stacked = b.ndim == 3
    n = b.shape[0] * bn if stacked else b.shape[1]

    def body(a_ref, b_ref, o_ref):
        av = a_ref[...]
        if square_a:
            av = av.astype(F32)
            av = (av * av).astype(BF16)
        o_ref[...] = _dot_tn(av, b_ref[...]).astype(BF16)

    res, _ = _pcall(
        body, name=name, grid=(m // bm, n // bn),
        in_specs=[pl.BlockSpec((t, bm), lambda i, j: (0, i)),
                  pl.BlockSpec((None, t, bn), lambda i, j: (j, 0, 0)) if stacked
                  else pl.BlockSpec((t, bn), lambda i, j: (0, j))],
        out_specs=[pl.BlockSpec((bm, bn), lambda i, j: (i, j))], out_shape=[jax.ShapeDtypeStruct((m, n), BF16)],
        semantics=("parallel", "parallel"), vmem_mb=56, after=after,
    )(a, b)
    return res[0]


N_RES = 16
SEG = 128
HALF = N_RES * SEG
TI = 32


def _x4(a):
    return a.reshape(a.shape[0] // HALF, N_RES, SEG, a.shape[1])


def _reorder(arrays, inverse, name, rider=None):
    t, c = arrays[0].shape
    n = len(arrays)
    n_i = SEG // TI
    natural = pl.BlockSpec((TI * N_RES, c), lambda s: (s, 0))
    major = pl.BlockSpec((1, N_RES, TI, c), lambda s: (s // n_i, 0, s % n_i, 0))

    def body(*refs):
        scr = refs[-1]
        for i_ref, o_ref in zip(refs[:n], refs[n:2 * n]):
            for cb in range(c // BLK):
                cols = slice(cb * BLK, (cb + 1) * BLK)
                slab = scr.at[cb]
                if inverse:
                    for r in range(N_RES):
                        slab[pl.ds(r, TI, stride=N_RES), :] = i_ref[0, r, :, cols]
                    o_ref[:, cols] = slab[...]
                else:
                    slab[...] = i_ref[:, cols]
                    for r in range(N_RES):
                        o_ref[0, r, :, cols] = slab[pl.ds(r, TI, stride=N_RES), :]

    shape4 = (t // HALF, N_RES, SEG, c)
    res, extra = _pcall(
        body, name=name, grid=(t // (TI * N_RES),),
        in_specs=[major if inverse else natural] * n, out_specs=[natural if inverse else major] * n,
        out_shape=[jax.ShapeDtypeStruct((t, c) if inverse else shape4, F32)] * n,
        scratch_shapes=[pltpu.VMEM((c // BLK, TI * N_RES, BLK), F32)],
        semantics=("parallel",), vmem_mb=32, rider=rider,
    )(*[_x4(a) if inverse else a for a in arrays])
    res = [r.reshape(t, c) for r in res]
    return res if rider is None else (res, extra)


_PATTERNS = ((1, 16, 8, SEG), (4, 4, 32, 4 * SEG), (16, 1, SEG, 0))
_FIRST = {1: 1, 4: 4, 16: 16}


def _group_rows(d, g):
    a = g >> 4
    if d == 16:
        base = a * HALF + (g & 15) * SEG
        prev = base - HALF
    elif d == 4:
        c = (g >> 2) & 3
        base = a * HALF + (g & 3) * SEG + c * 32
        prev = jnp.where(c > 0, base - 32, base - HALF + 96)
    else:
        c = g & 15
        base = a * HALF + c * 8
        prev = jnp.where(c > 0, base - 8, base - HALF + 120)
    return base, prev


def _load_rows(ref, base, n, rows, stride):
    parts = [ref[pl.ds(pl.multiple_of(base + j * stride, 8), rows), :] for j in range(n)]
    return parts[0] if n == 1 else jnp.concatenate(parts, axis=0)


def _store_rows(ref, base, val, n, rows, stride, add=False):
    for j in range(n):
        sl = pl.ds(pl.multiple_of(base + j * stride, 8), rows)
        piece = val[j * rows:(j + 1) * rows, :]
        if add:
            ref[sl, :] += piece
        else:
            ref[sl, :] = piece


def _band_bias(n, rows):
    shift = rows.bit_length() - 1
    lq = lax.broadcasted_iota(jnp.int32, (BLK, BLK), 0)
    lk = lax.broadcasted_iota(jnp.int32, (BLK, BLK), 1)
    iq = (lq & (rows - 1)) * n + (lq >> shift)
    ik = (lk & (rows - 1)) * n + (lk >> shift)
    zero = jnp.zeros((BLK, BLK), F32)
    return jnp.where(ik >= iq, zero, NEG_INF), jnp.where(ik <= iq, zero, NEG_INF)


def _set_bias(bias_scr, n, rows):
    prev_b, cur_b = _band_bias(n, rows)
    for half in range(2):
        bias_scr[half * BLK:(half + 1) * BLK, 0:BLK] = prev_b
        bias_scr[half * BLK:(half + 1) * BLK, BLK:2 * BLK] = cur_b


SCALE = 1.0 / math.sqrt(HEAD_DIM)


def _head_consts(value=1.0):
    lane_lo = lax.broadcasted_iota(jnp.int32, (BLK, BLK), 1) < HEAD_DIM
    return lane_lo, [jnp.where(lane_lo, value, 0.0).astype(BF16), jnp.where(lane_lo, 0.0, value).astype(BF16)]


def _stack_heads(v, head_mask):
    return jnp.concatenate([v * head_mask[0], v * head_mask[1]], axis=0)


def _unstack_heads(v2, lane_lo):
    return jnp.where(lane_lo, v2[:BLK], v2[BLK:])


def _rows_per_head(v, lane_lo):
    rolled = pltpu.roll(v, HEAD_DIM, axis=1)
    return jnp.concatenate([jnp.where(lane_lo, v, rolled), jnp.where(lane_lo, rolled, v)], axis=0)


WIDTH = 4


def _loop(lo, hi, fn, width=None):
    if width is None:
        def body(g, carry):
            fn(g)
            return carry

        if hi > lo:
            lax.fori_loop(lo, hi, body, 0)
        return
    while hi > lo:
        trips = (hi - lo) // width
        if trips:
            def body(i, carry, lo=lo, width=width):
                fn([lo + width * i + j for j in range(width)])
                return carry

            lax.fori_loop(0, trips, body, 0)
            lo += trips * width
        width = max(1, width // 2)


def _mix_weights(l1, l2, l3):
    mx = jnp.maximum(jnp.maximum(l1, l2), l3)
    e1, e2, e3 = jnp.exp(l1 - mx), jnp.exp(l2 - mx), jnp.exp(l3 - mx)
    inv = 1.0 / (e1 + e2 + e3)
    return e1 * inv, e2 * inv, e3 * inv


def _attention_fwd(qkv, rider=None):
    t = qkv.shape[0]
    groups = 16 * (t // HALF)

    def body(q_ref, k_ref, v_ref, attn_ref, l1_ref, l2_ref, l3_ref, o_scr, bias_scr):
        lane_lo, q_mask = _head_consts(SCALE)
        l_refs = (l1_ref, l2_ref, l3_ref)
        for p, (d, n, rows, stride) in enumerate(_PATTERNS):
            _set_bias(bias_scr, n, rows)
            o_p, l_p = o_scr.at[p], l_refs[p]

            def block(gs, has_prev):
                at = [_group_rows(d, g) for g in gs]

                def load(ref, b):
                    return _load_rows(ref, b, n, rows, stride).astype(BF16)

                q2 = [_stack_heads(load(q_ref, b), q_mask) for b, _ in at]
                k2 = [load(k_ref, b) for b, _ in at]
                v2 = [load(v_ref, b) for b, _ in at]
                if has_prev:
                    k2 = [jnp.concatenate([load(k_ref, pv), k], axis=0) for (_, pv), k in zip(at, k2)]
                    v2 = [jnp.concatenate([load(v_ref, pv), v], axis=0) for (_, pv), v in zip(at, v2)]
                s = [_dot_nt(q, k) for q, k in zip(q2, k2)]
                s = [x + (bias_scr[...] if has_prev else bias_scr[:, BLK:2 * BLK]) for x in s]
                mx = [jnp.max(x, axis=1, keepdims=True) for x in s]
                e = [jnp.exp(x - m) for x, m in zip(s, mx)]
                den = [jnp.sum(x, axis=1, keepdims=True) for x in e]
                o2 = [_dot(x.astype(BF16), v) * (1.0 / dn) for x, v, dn in zip(e, v2, den)]
                lse2 = [jnp.broadcast_to(m + jnp.log(dn), (2 * BLK, BLK)) for m, dn in zip(mx, den)]
                for (b, _), o, l in zip(at, o2, lse2):
                    _store_rows(o_p, b, _unstack_heads(o, lane_lo), n, rows, stride)
                    _store_rows(l_p, b, _unstack_heads(l, lane_lo), n, rows, stride)

            _loop(0, _FIRST[d], lambda gs: block(gs, False), width=WIDTH)
            _loop(_FIRST[d], groups, lambda gs: block(gs, True), width=WIDTH)

        def mix(i):
            sl = pl.ds(pl.multiple_of(i * 256, 256), 256)
            w = _mix_weights(l1_ref[sl, :], l2_ref[sl, :], l3_ref[sl, :])
            attn_ref[sl, :] = w[0] * o_scr[0, sl, :] + w[1] * o_scr[1, sl, :] + w[2] * o_scr[2, sl, :]

        _loop(0, t // 256, mix)

    def col(c0):
        return pl.BlockSpec((t, BLK), lambda hp: (0, c0 + hp))

    res, extra = _pcall(
        body, name="attention_fwd", grid=(4,), in_specs=[col(0), col(4), col(8)], out_specs=[col(0)] * 4,
        out_shape=[jax.ShapeDtypeStruct((t, 512), F32)] * 4,
        scratch_shapes=[pltpu.VMEM((3, t, BLK), F32), pltpu.VMEM((2 * BLK, 2 * BLK), F32)],
        semantics=("parallel",), vmem_mb=48, rider=rider,
    )(qkv, qkv, qkv)
    return res if rider is None else (res, extra)


def _attention_bwd(qkv, dattn, dsum, lses, dproj):
    t = qkv.shape[0]
    groups = 16 * (t // HALF)

    def body(q_ref, k_ref, v_ref, da_ref, ds_ref, l1_ref, l2_ref, l3_ref, kept_ref, out_ref, acc, bias_scr):
        del kept_ref
        lane_lo, head_mask = _head_consts()
        q_mask = _head_consts(SCALE)[1]
        l_refs = (l1_ref, l2_ref, l3_ref)

        def clear(i):
            sl = pl.ds(pl.multiple_of(i * 512, 512), 512)
            for s in range(3):
                acc[s, sl, :] = jnp.zeros((512, BLK), F32)

        _loop(0, t // 512, clear)
        dq_acc, dk_acc, dv_acc = acc.at[0], acc.at[1], acc.at[2]
        for p, (d, n, rows, stride) in enumerate(_PATTERNS):
            _set_bias(bias_scr, n, rows)

            def block(gs, has_prev):
                at = [_group_rows(d, g) for g in gs]

                def load(ref, b):
                    return _load_rows(ref, b, n, rows, stride)

                def put(ref, b, val):
                    _store_rows(ref, b, val, n, rows, stride, add=True)

                def wide(x):
                    return jnp.concatenate([x, x], axis=1) if has_prev else x

                lse = [[load(ref, b) for ref in l_refs] for b, _ in at]
                w = [_mix_weights(*ls)[p] for ls in lse]
                do2 = [_stack_heads((wg * load(da_ref, b)).astype(BF16), head_mask) for wg, (b, _) in zip(w, at)]
                dl2 = [wide(_rows_per_head(wg * load(ds_ref, b), lane_lo)) for wg, (b, _) in zip(w, at)]
                lse2 = [wide(_rows_per_head(ls[p], lane_lo)) for ls in lse]
                q2 = [_stack_heads(load(q_ref, b).astype(BF16), q_mask) for b, _ in at]
                k2 = [load(k_ref, b).astype(BF16) for b, _ in at]
                v2 = [load(v_ref, b).astype(BF16) for b, _ in at]
                if has_prev:
                    k2 = [jnp.concatenate([load(k_ref, pv).astype(BF16), k], axis=0) for (_, pv), k in zip(at, k2)]
                    v2 = [jnp.concatenate([load(v_ref, pv).astype(BF16), v], axis=0) for (_, pv), v in zip(at, v2)]
                s = [_dot_nt(q, k) for q, k in zip(q2, k2)]
                dp = [_dot_nt(do, v) for do, v in zip(do2, v2)]
                pr = [jnp.exp(x + (bias_scr[...] if has_prev else bias_scr[:, BLK:2 * BLK]) - l)
                      for x, l in zip(s, lse2)]
                ds = [(pg * (x - dl)).astype(BF16) for pg, x, dl in zip(pr, dp, dl2)]
                dq2 = [_dot(x, k) * SCALE for x, k in zip(ds, k2)]
                dk2 = [_dot_tn(x, q) for x, q in zip(ds, q2)]
                dv2 = [_dot_tn(pg.astype(BF16), do) for pg, do in zip(pr, do2)]
                for (b, pv), dq, dk, dv in zip(at, dq2, dk2, dv2):
                    put(dq_acc, b, _unstack_heads(dq, lane_lo))
                    if has_prev:
                        put(dk_acc, pv, dk[:BLK])
                        put(dv_acc, pv, dv[:BLK])
                        put(dk_acc, b, dk[BLK:])
                        put(dv_acc, b, dv[BLK:])
                    else:
                        put(dk_acc, b, dk)
                        put(dv_acc, b, dv)

            _loop(0, _FIRST[d], lambda gs: block(gs, False), width=WIDTH)
            _loop(_FIRST[d], groups, lambda gs: block(gs, True), width=WIDTH)

        def emit(i):
            sl = pl.ds(pl.multiple_of(i * 512, 512), 512)
            for s in range(3):
                out_ref[s, sl, :] = acc[s, sl, :].astype(BF16)

        _loop(0, t // 512, emit)

    def col(c0):
        return pl.BlockSpec((t, BLK), lambda hp: (0, c0 + hp))

    res, _ = _pcall(
        body, name="attention_bwd", grid=(4,),
        in_specs=[col(0), col(4), col(8)] + [col(0)] * 5 + [ANY],
        out_specs=[pl.BlockSpec((3, t, BLK), lambda hp: (0, 0, hp))],
        out_shape=[jax.ShapeDtypeStruct(dproj.shape, BF16)],
        scratch_shapes=[pltpu.VMEM((3, t, BLK), F32), pltpu.VMEM((2 * BLK, 2 * BLK), F32)],
        semantics=("parallel",), vmem_mb=56, aliases={8: 0},
    )(qkv, qkv, qkv, dattn, dsum, *lses, dproj)
    return res[0]


def _order_specs(t):
    n_i = SEG // TI
    nblk = (t // HALF) * n_i
    per = TI // 8

    def main(c, col=0):
        return pl.BlockSpec((1, N_RES, TI, c), lambda s: (s // n_i, 0, s % n_i, col))

    def before(c, col=0):
        return pl.BlockSpec((1, 2, 8, c), lambda s: (jnp.maximum(s - 1, 0) // n_i, N_RES // 2 - 1,
                                                     (jnp.maximum(s - 1, 0) % n_i) * per + per - 1, col))

    def after(c, col=0):
        return pl.BlockSpec((1, 2, 8, c), lambda s: (jnp.minimum(s + 1, nblk - 1) // n_i, 0,
                                                     (jnp.minimum(s + 1, nblk - 1) % n_i) * per, col))

    return nblk, main, before, after


def _shift_in(v, row_in, up):
    rows = v.shape[0]
    idx = lax.broadcasted_iota(jnp.int32, v.shape, 0)
    fill = jnp.broadcast_to(row_in, v.shape)
    if up:
        return jnp.where(idx == rows - 1, fill, pltpu.roll(v, rows - 1, axis=0))
    return jnp.where(idx == 0, fill, pltpu.roll(v, 1, axis=0))


def _taps_behind(u, before):
    s15 = _shift_in(u[N_RES - 1], before[1, 7:8, :], up=False)
    s14 = _shift_in(u[N_RES - 2], before[0, 7:8, :], up=False)
    m1 = jnp.concatenate([s15[None], u[:N_RES - 1]], axis=0)
    m2 = jnp.concatenate([s14[None], s15[None], u[:N_RES - 2]], axis=0)
    return m1, m2


def _taps_ahead(u, after):
    t0 = _shift_in(u[0], after[0, 0:1, :], up=True)
    t1 = _shift_in(u[1], after[1, 0:1, :], up=True)
    p1 = jnp.concatenate([u[1:], t0[None]], axis=0)
    p2 = jnp.concatenate([u[2:], t0[None], t1[None]], axis=0)
    return p1, p2


def _conv_fwd(gates, before, first, cw):
    bg, cg, xc = gates[..., 0:512], gates[..., 512:1024], gates[..., 1024:1536]
    u = cg * xc
    ub = before[..., 512:1024] * before[..., 1024:1536]
    ub = jnp.where(first, jnp.zeros_like(ub), ub)
    m1, m2 = _taps_behind(u, ub)
    conv = m2 * cw[0:1, :] + m1 * cw[1:2, :] + u * cw[2:3, :]
    return bg, u, m1, m2, conv


def _sum_tokens(v):
    return jnp.sum(jnp.sum(v, axis=0), axis=0, keepdims=True)


def _mixer_fwd(x, attn, gates, cw, g_a, g_c, w_out):
    t, d = x.shape
    nblk, main, before, _ = _order_specs(t)
    rows = N_RES * TI

    def body(x_ref, at_ref, gt_ref, gb_ref, cw_ref, ga_ref, gc_ref, wa_ref, wb_ref, x1_ref, mg_ref):
        an = _rms_fwd(at_ref[0], ga_ref[...])[0].astype(BF16)
        bg, _, _, _, conv = _conv_fwd(gt_ref[0], gb_ref[0], pl.program_id(0) == 0, cw_ref[...])
        cn = _rms_fwd(bg * conv, gc_ref[...])[0].astype(BF16)
        mg_ref[0, :, :, 0:512] = an
        mg_ref[0, :, :, 512:1024] = cn
        y = _dot(an.reshape(rows, 512), wa_ref[...]) + _dot(cn.reshape(rows, 512), wb_ref[...])
        x1_ref[0] = x_ref[0] + y.reshape(N_RES, TI, d)

    const = lambda r, c, i0=0: pl.BlockSpec((r, c), lambda s: (i0, 0))
    x1, merged = pl.pallas_call(
        body, name="mixer_fwd", grid=(nblk,),
        in_specs=[main(d), main(512), main(1536, 1), before(1536, 1), const(3, 512), const(1, 512), const(1, 512),
                  const(512, d), const(512, d, 1)],
        out_specs=[main(d), main(d)],
        out_shape=[jax.ShapeDtypeStruct(_x4(x).shape, F32), jax.ShapeDtypeStruct(_x4(x).shape, BF16)],
        compiler_params=_params(("parallel",), 48),
    )(_x4(x), _x4(attn), _x4(gates), _x4(gates), cw, g_a, g_c, w_out, w_out)
    return x1.reshape(t, d), merged.reshape(t, d)


def _mixer_bwd(dx1, attn, gates, cw, g_a, g_c, w_out, head_sum, after=()):
    t, d = dx1.shape
    nblk, main, before, _ = _order_specs(t)
    rows = N_RES * TI

    def body(dx_ref, at_ref, gt_ref, gb_ref, cw_ref, ga_ref, gc_ref, wa_ref, wb_ref, hs_ref,
             da_ref, dsum_ref, dy_ref, gga_ref, ggc_ref):
        s = pl.program_id(0)
        dxb = dx_ref[0].reshape(rows, d).astype(BF16)
        dma = _dot_nt(dxb, wa_ref[...]).reshape(N_RES, TI, 512)
        dmc = _dot_nt(dxb, wb_ref[...]).reshape(N_RES, TI, 512)
        attn_v, g_av = at_ref[0], ga_ref[...]
        _, ah, ra = _rms_fwd(attn_v, g_av)
        dattn = _rms_bwd(dma, ah, ra, g_av)
        da_ref[0] = dattn
        z = (dattn * attn_v).reshape(rows, 512)
        hs = hs_ref[...]
        z1 = z.astype(BF16)
        z2 = (z - z1.astype(F32)).astype(BF16)
        dsum_ref[0] = (_dot(z1, hs) + _dot(z2, hs)).reshape(N_RES, TI, 512)
        bg, _, _, _, conv = _conv_fwd(gt_ref[0], gb_ref[0], s == 0, cw_ref[...])
        g_cv = gc_ref[...]
        _, yh, rc = _rms_fwd(bg * conv, g_cv)
        dy_ref[0] = _rms_bwd(dmc, yh, rc, g_cv)
        pa, pc = _sum_tokens(dma * ah), _sum_tokens(dmc * yh)

        @pl.when(s == 0)
        def _():
            gga_ref[...] = pa
            ggc_ref[...] = pc

        @pl.when(s != 0)
        def _():
            gga_ref[...] += pa
            ggc_ref[...] += pc

    const = lambda r, c, i0=0: pl.BlockSpec((r, c), lambda s: (i0, 0))
    shape4 = _x4(attn).shape
    res, _ = _pcall(
        body, name="mixer_bwd", grid=(nblk,),
        in_specs=[main(d), main(512), main(1536, 1), before(1536, 1), const(3, 512), const(1, 512), const(1, 512),
                  const(512, d), const(512, d, 1), const(512, 512)],
        out_specs=[main(512)] * 3 + [const(1, 512), const(1, 512)],
        out_shape=[jax.ShapeDtypeStruct(shape4, F32)] * 3 + [jax.ShapeDtypeStruct((1, 512), F32)] * 2,
        semantics=("arbitrary",), vmem_mb=48, after=after,
    )(_x4(dx1), _x4(attn), _x4(gates), _x4(gates), cw, g_a, g_c, w_out, w_out, head_sum)
    return [r.reshape(t, 512) for r in res[:3]] + res[3:]


def _conv_bwd(dy, gates, cw):
    t = dy.shape[0]
    nblk, main, before, after = _order_specs(t)
    n_i = SEG // TI

    def body(dy_ref, dya_ref, gt_ref, gb_ref, ga_ref, cw_ref, dp_ref, gcw_ref):
        s = pl.program_id(0)
        cw_v, gates_v = cw_ref[...], gt_ref[0]
        bg, u, m1, m2, conv = _conv_fwd(gates_v, gb_ref[0], s == 0, cw_v)
        dy_v = dy_ref[0]
        dconv = dy_v * bg
        dca = dya_ref[0] * ga_ref[0][..., 0:512]
        dca = jnp.where(s == nblk - 1, jnp.zeros_like(dca), dca)
        p1, p2 = _taps_ahead(dconv, dca)
        du = dconv * cw_v[2:3, :] + p1 * cw_v[1:2, :] + p2 * cw_v[0:1, :]
        dp_ref[0, 0] = (dy_v * conv).astype(BF16)
        dp_ref[1, 0] = (du * gates_v[..., 1024:1536]).astype(BF16)
        dp_ref[2, 0] = (du * gates_v[..., 512:1024]).astype(BF16)
        parts = [_sum_tokens(dconv * m2), _sum_tokens(dconv * m1), _sum_tokens(dconv * u)]

        @pl.when(s == 0)
        def _():
            gcw_ref[...] = jnp.zeros_like(gcw_ref)

        for tap in range(3):
            gcw_ref[tap:tap + 1, :] += parts[tap]

    (dproj, gcw), _ = _pcall(
        body, name="conv_bwd", grid=(nblk,),
        in_specs=[main(512), after(512), main(1536, 1), before(1536, 1), after(1536, 1),
                  pl.BlockSpec((3, 512), lambda s: (0, 0))],
        out_specs=[pl.BlockSpec((3, 1, N_RES, TI, 512), lambda s: (1, s // n_i, 0, s % n_i, 0)),
                   pl.BlockSpec((8, 512), lambda s: (0, 0))],
        out_shape=[jax.ShapeDtypeStruct((6, t // HALF, N_RES, SEG, 512), BF16), jax.ShapeDtypeStruct((8, 512), F32)],
        semantics=("arbitrary",), vmem_mb=40,
    )(_x4(dy), _x4(dy), _x4(gates), _x4(gates), _x4(gates), cw)
    return dproj.reshape(6, t, 512), gcw


def _xattn_fwd(x1, g, w_q, kv, w_o, *, tb):
    t, d = x1.shape
    hd = d // N_MEM_HEADS
    m = kv.shape[0]

    def body(x_ref, g_ref, wq_ref, k_ref, v_ref, wo_ref, x2_ref, h_ref, q_ref, o_ref):
        xv = x_ref[...]
        h = _rms_fwd(xv, g_ref[...])[0].astype(BF16)
        h_ref[...] = h
        q = _dot(h, wq_ref[...]).astype(BF16)
        q_ref[...] = q
        for hh in range(N_MEM_HEADS):
            sl = slice(hh * hd, (hh + 1) * hd)
            s = _dot_nt(q[:, sl], k_ref[:, sl]) * (1.0 / 16.0)
            e = jnp.exp(s - jnp.max(s, axis=1, keepdims=True))
            p = e / jnp.sum(e, axis=1, keepdims=True)
            o_ref[:, sl] = _dot(p.astype(BF16), v_ref[:, sl]).astype(BF16)
        x2_ref[...] = xv + _dot(o_ref[...], wo_ref[...])

    tok = pl.BlockSpec((tb, d), lambda i: (i, 0))
    full = pl.BlockSpec((d, d), lambda i: (0, 0))
    return pl.pallas_call(
        body, name="xattn_fwd", grid=(t // tb,),
        in_specs=[tok, pl.BlockSpec((1, d), lambda i: (0, 0)), full,
                  pl.BlockSpec((m, d), lambda i: (0, 0)), pl.BlockSpec((m, d), lambda i: (0, 1)), full],
        out_specs=[tok] * 4,
        out_shape=[jax.ShapeDtypeStruct((t, d), F32)] + [jax.ShapeDtypeStruct((t, d), BF16)] * 3,
        compiler_params=_params(("parallel",), 48),
    )(x1, g, w_q, kv, kv, w_o)


def _xattn_bwd(dx2, x1, g, q, w_q, kv, w_o, *, tb, after=()):
    t, d = x1.shape
    hd = d // N_MEM_HEADS
    m = kv.shape[0]

    def body(dx2_ref, x_ref, g_ref, q_ref, wq_ref, k_ref, v_ref, wo_ref,
             dx1_ref, dx1b_ref, dq_ref, dk_ref, dv_ref, gg_ref):
        i = pl.program_id(0)

        @pl.when(i == 0)
        def _():
            dk_ref[...] = jnp.zeros_like(dk_ref)
            dv_ref[...] = jnp.zeros_like(dv_ref)

        dx2 = dx2_ref[...]
        do = _dot_nt(dx2.astype(BF16), wo_ref[...]).astype(BF16)
        for hh in range(N_MEM_HEADS):
            sl = slice(hh * hd, (hh + 1) * hd)
            qh, kh, vh, doh = q_ref[:, sl], k_ref[:, sl], v_ref[:, sl], do[:, sl]
            s = _dot_nt(qh, kh) * (1.0 / 16.0)
            e = jnp.exp(s - jnp.max(s, axis=1, keepdims=True))
            p = e / jnp.sum(e, axis=1, keepdims=True)
            dp = _dot_nt(doh, vh)
            ds = (p * (dp - jnp.sum(dp * p, axis=1, keepdims=True)) * (1.0 / 16.0)).astype(BF16)
            dq_ref[:, sl] = _dot(ds, kh).astype(BF16)
            dk_ref[:, sl] += _dot_tn(ds, qh)
            dv_ref[:, sl] += _dot_tn(p.astype(BF16), doh)
        dh = _dot_nt(dq_ref[...], wq_ref[...])
        g_v = g_ref[...]
        _, xh, r = _rms_fwd(x_ref[...], g_v)
        dx1 = dx2 + _rms_bwd(dh, xh, r, g_v)
        dx1_ref[...] = dx1
        dx1b_ref[...] = dx1.astype(BF16)
        part = jnp.sum(dh * xh, axis=0, keepdims=True)

        @pl.when(i == 0)
        def _():
            gg_ref[...] = part

        @pl.when(i != 0)
        def _():
            gg_ref[...] += part

    tok = pl.BlockSpec((tb, d), lambda i: (i, 0))
    full = pl.BlockSpec((d, d), lambda i: (0, 0))
    acc = pl.BlockSpec((m, d), lambda i: (0, 0))
    res, _ = _pcall(
        body, name="xattn_bwd", grid=(t // tb,),
        in_specs=[tok, tok, pl.BlockSpec((1, d), lambda i: (0, 0)), tok, full,
                  pl.BlockSpec((m, d), lambda i: (0, 0)), pl.BlockSpec((m, d), lambda i: (0, 1)), full],
        out_specs=[tok, tok, tok, acc, acc, pl.BlockSpec((1, d), lambda i: (0, 0))],
        out_shape=[jax.ShapeDtypeStruct((t, d), F32), jax.ShapeDtypeStruct((t, d), BF16),
                   jax.ShapeDtypeStruct((t, d), BF16),
                   jax.ShapeDtypeStruct((m, d), F32), jax.ShapeDtypeStruct((m, d), F32),
                   jax.ShapeDtypeStruct((1, d), F32)],
        semantics=("arbitrary",), vmem_mb=48, after=after,
    )(dx2, x1, g, q, w_q, kv, kv, w_o)
    return res


def _mlp_down_loss(a, w_down, x2, tgt, g, *, tb):
    t, d = x2.shape
    f = a.shape[1]

    def body(a_ref, w_ref, x_ref, t_ref, g_ref, dx_ref, dxb_ref, loss_ref, gg_ref):
        i = pl.program_id(0)
        av = a_ref[...].astype(F32)
        x3 = x_ref[...] + _dot((av * av).astype(BF16), w_ref[...])
        g_v = g_ref[...]
        out, xh, r = _rms_fwd(x3, g_v)
        err = out - t_ref[...]
        dout = err * (1.0 / d)
        dx = _rms_bwd(dout, xh, r, g_v)
        dx_ref[...] = dx
        dxb_ref[...] = dx.astype(BF16)
        part = jnp.sum(dout * xh, axis=0, keepdims=True)
        lpart = 0.5 * jnp.sum(jnp.mean(err * err, axis=-1, keepdims=True), axis=0, keepdims=True)
        lpart = jnp.broadcast_to(lpart, loss_ref.shape)

        @pl.when(i == 0)
        def _():
            gg_ref[...] = part
            loss_ref[...] = lpart

        @pl.when(i != 0)
        def _():
            gg_ref[...] += part
            loss_ref[...] += lpart

    tok = pl.BlockSpec((tb, d), lambda i: (i, 0))
    return pl.pallas_call(
        body, name="mlp_down_loss", grid=(t // tb,),
        in_specs=[pl.BlockSpec((tb, f), lambda i: (i, 0)), pl.BlockSpec((f, d), lambda i: (0, 0)), tok, tok,
                  pl.BlockSpec((1, d), lambda i: (0, 0))],
        out_specs=[tok, tok, pl.BlockSpec((8, 128), lambda i: (0, 0)), pl.BlockSpec((1, d), lambda i: (0, 0))],
        out_shape=[jax.ShapeDtypeStruct((t, d), F32), jax.ShapeDtypeStruct((t, d), BF16),
                   jax.ShapeDtypeStruct((8, 128), F32), jax.ShapeDtypeStruct((1, d), F32)],
        compiler_params=_params(("arbitrary",), 56),
    )(a, w_down, x2, tgt, g)


def _mlp_dpre(dx3, w_down, a, *, tb, bn):
    t, d = dx3.shape
    f = a.shape[1]

    def body(dx_ref, w_ref, a_ref, o_ref):
        o_ref[...] = (2.0 * a_ref[...].astype(F32) * _dot_nt(dx_ref[...], w_ref[...])).astype(BF16)

    return pl.pallas_call(
        body, name="mlp_dpre", grid=(t // tb, f // bn),
        in_specs=[pl.BlockSpec((tb, d), lambda i, j: (i, 0)), pl.BlockSpec((bn, d), lambda i, j: (j, 0)),
                  pl.BlockSpec((tb, bn), lambda i, j: (i, j))],
        out_specs=pl.BlockSpec((tb, bn), lambda i, j: (i, j)),
        out_shape=jax.ShapeDtypeStruct((t, f), BF16),
        compiler_params=_params(("parallel", "arbitrary"), 48),
    )(dx3, w_down, a)


def _adamw(gsum, w, m, v):
    m_new = ADAM_B1 * m + (1.0 - ADAM_B1) * gsum
    v_new = ADAM_B2 * v + (1.0 - ADAM_B2) * (gsum * gsum)
    m_hat = m_new / (1.0 - ADAM_B1 ** ADAM_STEP)
    v_hat = v_new / (1.0 - ADAM_B2 ** ADAM_STEP)
    delta = -ADAM_LR * (m_hat / (jnp.sqrt(v_hat) + ADAM_EPS) + ADAM_WD * w)
    return delta, m_new, v_new


def _sum_adamw(parts, w, m, v, *, name, tr):
    r, c = w.shape

    def body(p_ref, w_ref, m_ref, v_ref, g_ref, d_ref, mo_ref, vo_ref):
        g = p_ref[0].astype(F32)
        for k in range(1, N_DEV):
            g = g + p_ref[k].astype(F32)
        g_ref[...] = g
        d_ref[...], mo_ref[...], vo_ref[...] = _adamw(g, w_ref[...], m_ref[...], v_ref[...])

    blk = pl.BlockSpec((tr, c), lambda i: (i, 0))
    return pl.pallas_call(
        body, name=name, grid=(r // tr,),
        in_specs=[pl.BlockSpec((N_DEV, tr, c), lambda i: (0, i, 0)), blk, blk, blk],
        out_specs=[blk] * 4, out_shape=[jax.ShapeDtypeStruct((r, c), F32)] * 4,
        compiler_params=_params(("parallel",), 40),
    )(parts, w, m, v)


def _sum_small(parts):
    _, r, c = parts.shape

    def body(p_ref, o_ref):
        s = p_ref[0]
        for k in range(1, N_DEV):
            s = s + p_ref[k]
        o_ref[...] = s

    return pl.pallas_call(body, name="sum_small", out_shape=jax.ShapeDtypeStruct((r, c), F32))(parts)


def _adamw_small(g, w, m, v):
    def body(g_ref, w_ref, m_ref, v_ref, d_ref, mo_ref, vo_ref):
        d_ref[...], mo_ref[...], vo_ref[...] = _adamw(g_ref[...], w_ref[...], m_ref[...], v_ref[...])

    return pl.pallas_call(body, name="adamw_small", out_shape=[jax.ShapeDtypeStruct(g.shape, F32)] * 3)(g, w, m, v)


def _head_sum_matrix():
    r = lax.broadcasted_iota(jnp.int32, (512, 512), 0) // HEAD_DIM
    c = lax.broadcasted_iota(jnp.int32, (512, 512), 1) // HEAD_DIM
    return (r == c).astype(BF16)


_SHARD_AXIS = dict(w_in=1, w_out=0, w_q=0, w_kv=1, w_o=0, w_up=1, w_down=0, conv_w=None, small=None)


class _Weights:
    def __init__(self, full, shards=None):
        self.full = dict(full)
        self.shards = shards

    def rider(self, names, late=False):
        if self.shards is None:
            return None
        return _Gather([self.shards[n] for n in names], [_SHARD_AXIS[n] for n in names], late)

    def arrived(self, names, gathered):
        if gathered is not None:
            for n, g in zip(names, gathered):
                self.full[n] = g.transpose(1, 0, 2).reshape(g.shape[1], -1) if n == "conv_w" else g

    def __getitem__(self, name):
        return self.full[name]


class _Grads:
    def __init__(self, distributed):
        self.distributed = distributed
        self.local = {}
        self.pending = {}

    def add(self, name, g):
        self.local[name] = g

    def send(self, *names):
        if not self.distributed:
            return []
        rider = _Exchange([self.local[n] for n in names], [_SHARD_AXIS[n] for n in names])
        started = _exchange_start(rider, "send_" + "_".join(names))
        self.pending[names[0]] = (names, rider, started)
        return [started[3]]

    def wait(self, first_name, after):
        names, rider, started = self.pending.pop(first_name)
        return _exchange_wait(rider, started, after, "wait_" + "_".join(names))


def _ride(fn, *args, rider=None, **kw):
    if rider is None:
        return fn(*args, **kw), None
    return fn(*args, rider=rider, **kw)


def _local_step(x, mem, tgt, gains, weights, grads):
    names = ["w_in", "conv_w"]
    (x, tgt), got = _ride(_reorder, [x, tgt], False, "reorder_in", rider=weights.rider(names, late=True))
    weights.arrived(names, got)
    w_in, cw = weights["w_in"], weights["conv_w"]

    names = ["w_out", "w_kv"]
    (proj, h1), got = _ride(_norm_matmul, x, gains["g_mix"], w_in, name="proj", out_dtype=F32, tb=1024, bn=768,
                            save_h=True, rider=weights.rider(names))
    weights.arrived(names, got)
    names = ["w_q", "w_o", "w_up"]
    (attn, *lses), got = _ride(_attention_fwd, proj, rider=weights.rider(names))
    weights.arrived(names, got)
    x1, merged = _mixer_fwd(x, attn, proj, cw, gains["g_attn_out"], gains["g_conv_out"], weights["w_out"])
    kv, mem_n = _norm_matmul(mem, gains["g_mem"], weights["w_kv"], name="mem_kv", out_dtype=BF16, tb=mem.shape[0],
                             bn=1024, save_h=True)
    x2, h2, qm, om = _xattn_fwd(x1, gains["g_xattn"], weights["w_q"], kv, weights["w_o"], tb=512)
    w_up = weights["w_up"]
    (a, h3), got = _ride(_norm_matmul, x2, gains["g_mlp"], w_up, name="mlp_up", out_dtype=BF16, tb=1024, bn=1024,
                         relu=True, save_h=True, rider=weights.rider(["w_down"], late=True))
    weights.arrived(["w_down"], got)
    w_down = weights["w_down"]
    dx3, dx3b, loss_blk, gg_final = _mlp_down_loss(a, w_down, x2, tgt, gains["g_final"], tb=256)

    dpre = _mlp_dpre(dx3b, w_down, a, tb=1024, bn=1024)
    grads.add("w_down", _matmul_tn(a, dx3b, name="grad_w_down", bm=512, bn=1024, square_a=True))
    sent = grads.send("w_down")
    grads.add("w_up", _matmul_tn(h3, dpre, name="grad_w_up", bm=1024, bn=512, after=sent))
    sent = grads.send("w_up")
    dx2, dx2b, gg_mlp = _matmul_nt_normbwd(dpre, w_up, x2, gains["g_mlp"], dx3, name="mlp_dx", tb=512,
                                           also_bf16=True, after=sent)

    grads.add("w_o", _matmul_tn(om, dx2b, name="grad_w_o", bm=1024, bn=512))
    sent = grads.send("w_o")
    dx1, dx1b, dqm, dk, dv, gg_xattn = _xattn_bwd(dx2, x1, gains["g_xattn"], qm, weights["w_q"], kv, weights["w_o"],
                                                  tb=512, after=sent)
    grads.add("w_q", _matmul_tn(h2, dqm, name="grad_w_q", bm=1024, bn=512))
    dkv = jnp.concatenate([dk, dv], axis=1).astype(BF16)
    grads.add("w_kv", _matmul_tn(mem_n, dkv, name="grad_w_kv", bm=1024, bn=1024))
    _, gg_mem = _matmul_nt_normbwd(dkv, weights["w_kv"], mem, gains["g_mem"], None, name="mem_dx", tb=mem.shape[0])

    grads.add("w_out", _matmul_tn(merged, dx1b, name="grad_w_out", bm=1024, bn=512))
    sent = grads.send("w_q", "w_kv", "w_out")
    dattn, dsum, dy, gg_attn, gg_conv = _mixer_bwd(dx1, attn, proj, cw, gains["g_attn_out"], gains["g_conv_out"],
                                                   weights["w_out"], _head_sum_matrix(), after=sent)
    dproj, gcw = _conv_bwd(dy, proj, cw)
    dproj = _attention_bwd(proj, dattn, dsum, lses, dproj)
    grads.add("w_in", _matmul_tn(h1, dproj, name="grad_w_in", bm=1024, bn=512))
    sent = grads.send("w_in")
    grad_x, gg_mix = _matmul_nt_normbwd(dproj, w_in, x, gains["g_mix"], dx1, name="mixer_dx", tb=512, after=sent)

    def part(v):
        return jnp.pad(v, ((0, SMALL_PART - v.shape[0]), (0, 1024 - v.shape[1])))

    parts = [gg_mix, gg_xattn, gg_mem, gg_mlp, gg_final, jnp.concatenate([gg_attn, gg_conv], axis=1), gcw, loss_blk]
    grads.add("small", jnp.concatenate([part(v) for v in parts], axis=0))
    (grad_x,) = _reorder([grad_x], True, "reorder_out")
    return grad_x


SMALL_PART = 8
_BIG = ("w_in", "w_out", "w_q", "w_kv", "w_o", "w_up", "w_down")
_GAIN_ROWS = ("g_mix", "g_xattn", "g_mem", "g_mlp", "g_final")


def _pack_small(vals, conv):
    rows = [vals[k].reshape(1, -1) for k in _GAIN_ROWS]
    rows.append(jnp.concatenate([vals["g_attn_out"].reshape(1, -1), vals["g_conv_out"].reshape(1, -1)], axis=1))
    flat = conv.reshape(1, -1)
    rows.append(jnp.pad(flat, ((0, 0), (0, 1024 - flat.shape[1]))))
    rows.append(jnp.zeros((1, 1024), F32))
    return jnp.concatenate(rows, axis=0)


def kernel(x, mem, g_mix, w_in, conv_w, g_attn_out, g_conv_out, w_out, g_xattn, g_mem, w_q_mem, w_kv_mem, w_o_mem, g_mlp, w_up, w_down, g_final, loss_target, m_g_mix, m_w_in, m_conv_w, m_g_attn_out, m_g_conv_out, m_w_out, m_g_xattn, m_g_mem, m_w_q_mem, m_w_kv_mem, m_w_o_mem, m_g_mlp, m_w_up, m_w_down, m_g_final, v_g_mix, v_w_in, v_conv_w, v_g_attn_out, v_g_conv_out, v_w_out, v_g_xattn, v_g_mem, v_w_q_mem, v_w_kv_mem, v_w_o_mem, v_g_mlp, v_w_up, v_w_down, v_g_final):
    d = x.shape[-1]
    me = 4 * lax.axis_index("x") + 2 * lax.axis_index("y") + lax.axis_index("c")
    w_shards = dict(w_in=w_in, w_out=w_out, w_q=w_q_mem, w_kv=w_kv_mem, w_o=w_o_mem, w_up=w_up, w_down=w_down)
    m_shards = dict(w_in=m_w_in, w_out=m_w_out, w_q=m_w_q_mem, w_kv=m_w_kv_mem, w_o=m_w_o_mem, w_up=m_w_up,
                    w_down=m_w_down)
    v_shards = dict(w_in=v_w_in, w_out=v_w_out, w_q=v_w_q_mem, w_kv=v_w_kv_mem, w_o=v_w_o_mem, w_up=v_w_up,
                    w_down=v_w_down)
    gains = dict(g_mix=g_mix, g_attn_out=g_attn_out, g_conv_out=g_conv_out, g_xattn=g_xattn, g_mem=g_mem,
                 g_mlp=g_mlp, g_final=g_final)
    gains2 = {k: v.reshape(1, -1) for k, v in gains.items()}

    shards = {k: w_shards[k].astype(BF16) for k in _BIG}
    shards["conv_w"] = conv_w
    grads = _Grads(distributed=True)
    grad_x = _local_step(x[0], mem[0], loss_target[0], gains2, _Weights({}, shards), grads)

    after = grads.send("small")
    outs = {}
    tiles = dict(w_in=256, w_out=128, w_q=128, w_kv=256, w_o=128, w_up=256, w_down=256)
    for group in (("w_down",), ("w_up",), ("w_o",), ("w_q", "w_kv", "w_out"), ("w_in",)):
        for k, received in zip(group, grads.wait(group[0], after)):
            outs[k] = _sum_adamw(received, w_shards[k], m_shards[k], v_shards[k], name=f"adamw_{k}", tr=tiles[k])
            after = [outs[k][0]]
    small_received, = grads.wait("small", after)

    ssum = _sum_small(small_received)
    row = lambda i: ssum[SMALL_PART * i]
    loss = ssum[SMALL_PART * 7, 0]
    g_small = {k: row(i) for i, k in enumerate(_GAIN_ROWS)}
    g_small["g_attn_out"] = row(5)[0:512]
    g_small["g_conv_out"] = row(5)[512:1024]
    taps = ssum[SMALL_PART * 6:SMALL_PART * 6 + 3, 0:512]
    g_conv = lax.dynamic_slice_in_dim(taps, me * 64, 64, axis=1)
    m_small = dict(g_mix=m_g_mix, g_attn_out=m_g_attn_out, g_conv_out=m_g_conv_out, g_xattn=m_g_xattn,
                   g_mem=m_g_mem, g_mlp=m_g_mlp, g_final=m_g_final)
    v_small = dict(g_mix=v_g_mix, g_attn_out=v_g_attn_out, g_conv_out=v_g_conv_out, g_xattn=v_g_xattn,
                   g_mem=v_g_mem, g_mlp=v_g_mlp, g_final=v_g_final)
    packed = [_pack_small(g_small, g_conv), _pack_small(gains, conv_w), _pack_small(m_small, m_conv_w),
              _pack_small(v_small, v_conv_w)]
    upd = _adamw_small(*packed)

    def unpack(p):
        res = {k: p[i] for i, k in enumerate(_GAIN_ROWS)}
        res["g_attn_out"] = p[5, 0:512]
        res["g_conv_out"] = p[5, 512:1024]
        res["conv_w"] = p[6, 0:192].reshape(3, 64)
        return res

    g_small["conv_w"] = g_conv
    small_out = [g_small] + [unpack(p) for p in upd]
    names = {"g_mix": "g_mix", "w_in": "w_in", "conv_w": "conv_w", "g_attn_out": "g_attn_out",
             "g_conv_out": "g_conv_out", "w_out": "w_out", "g_xattn": "g_xattn", "g_mem": "g_mem",
             "w_q_mem": "w_q", "w_kv_mem": "w_kv", "w_o_mem": "w_o", "g_mlp": "g_mlp", "w_up": "w_up",
             "w_down": "w_down", "g_final": "g_final"}
    result = [loss, grad_x[None]]
    for which in range(4):
        for key in names.values():
            result.append(outs[key][which] if key in outs else small_out[which][key])
    return tuple(result)
```

```python
import math

import jax
import jax.numpy as jnp
from jax import lax
from jax.experimental import pallas as pl
from jax.experimental.pallas import tpu as pltpu

F32 = jnp.float32
BF16 = jnp.bfloat16
NORM_EPS = 1e-6
NEG_INF = -1e30
N_DEV = 8
BLK = 128
HEAD_DIM = 64
N_MEM_HEADS = 4
ADAM_LR = 0.001
ADAM_B1 = 0.9
ADAM_B2 = 0.999
ADAM_EPS = 1e-08
ADAM_WD = 0.01
ADAM_STEP = 10
MESH = pl.DeviceIdType.MESH
ANY = pl.BlockSpec(memory_space=pl.ANY)


def _dot(a, b):
    return jnp.dot(a, b, preferred_element_type=F32)


def _dot_nt(a, b):
    return lax.dot_general(a, b, (((1,), (1,)), ((), ())), preferred_element_type=F32)


def _dot_tn(a, b):
    return lax.dot_general(a, b, (((0,), (0,)), ((), ())), preferred_element_type=F32)


def _hbm(*arrays):
    return [pltpu.with_memory_space_constraint(a, pltpu.HBM) for a in arrays]


def _params(semantics, vmem_mb):
    return pltpu.CompilerParams(dimension_semantics=semantics, vmem_limit_bytes=vmem_mb << 20)


def _rms_fwd(x, g):
    r = lax.rsqrt(jnp.mean(x * x, axis=-1, keepdims=True) + NORM_EPS)
    xh = x * r
    return xh * g, xh, r


def _rms_bwd(dy, xh, r, g):
    gy = dy * g
    return r * (gy - xh * jnp.mean(xh * gy, axis=-1, keepdims=True))


def _position():
    x, y, c = lax.axis_index("x"), lax.axis_index("y"), lax.axis_index("c")
    return x, y, c


def _block_of(ref, j, axis, shard_shape):
    r, c = shard_shape
    if axis is None:
        return ref.at[j]
    if axis == 0:
        return ref.at[pl.ds(j * r, r), :]
    return ref.at[:, pl.ds(j * c, c)]


class _Gather:
    has_mid = True
    alias_pairs = ()

    def __init__(self, shards, axes, late=False):
        self.arrays = list(shards)
        self.axes = list(axes)
        self.late = late
        self.n = len(self.arrays)

    def out_shape(self):
        res = []
        for s, axis in zip(self.arrays, self.axes):
            r, c = s.shape
            shape = (N_DEV, r, c) if axis is None else (N_DEV * r, c) if axis == 0 else (r, N_DEV * c)
            res.append(jax.ShapeDtypeStruct(shape, s.dtype))
        return res

    def scratch(self):
        return [pltpu.SemaphoreType.DMA((self.n, 7)), pltpu.SemaphoreType.DMA((self.n, 7)),
                pltpu.SemaphoreType.DMA((self.n,))]

    def _ctx(self, ins, outs, sems):
        send_sems, recv_sems, local_sems = sems
        x, y, c = _position()
        me, sibling = (x, y, c), (x, y, 1 - c)
        chips = [(1 - x, y), (x, 1 - y), (1 - x, 1 - y)]

        def lin(px, py, pc):
            return 4 * px + 2 * py + pc

        def place(a, block):
            return _block_of(outs[a], lin(*block), self.axes[a], self.arrays[a].shape)

        def copy(a, k, block, to, src=None):
            dst = place(a, block)
            return pltpu.make_async_remote_copy(
                src_ref=dst if src is None else src, dst_ref=dst,
                send_sem=send_sems.at[a, k], recv_sem=recv_sems.at[a, k],
                device_id=to, device_id_type=MESH)

        def mine():
            return [pltpu.make_async_copy(ins[a], place(a, me), local_sems.at[a]) for a in range(self.n)]

        def first():
            res = []
            for a in range(self.n):
                res.append(copy(a, 0, me, sibling, src=ins[a]))
                res += [copy(a, 1 + j, me, (*chip, c), src=ins[a]) for j, chip in enumerate(chips)]
            return res

        return c, me, sibling, chips, copy, mine, first

    def start(self, ins, outs, sems):
        _, _, _, _, _, mine, first = self._ctx(ins, outs, sems)
        for cp in mine() + first():
            cp.start()

    def mid(self, ins, outs, sems):
        c, me, sibling, chips, copy, _, _ = self._ctx(ins, outs, sems)
        for j, chip in enumerate(chips):
            for a in range(self.n):
                copy(a, 1 + j, (*chip, c), me).wait_recv()
                copy(a, 4 + j, (*chip, c), sibling).start()

    def finish(self, ins, outs, sems):
        c, me, sibling, chips, copy, mine, first = self._ctx(ins, outs, sems)
        for a in range(self.n):
            copy(a, 0, sibling, me).wait_recv()
            for j, chip in enumerate(chips):
                copy(a, 4 + j, (*chip, 1 - c), me).wait_recv()
        for cp in first():
            cp.wait_send()
        for j, chip in enumerate(chips):
            for a in range(self.n):
                copy(a, 4 + j, (*chip, c), sibling).wait_send()
        for cp in mine():
            cp.wait()


class _Exchange:
    def __init__(self, parts, axes):
        self.n = len(parts)
        self.axes = list(axes)
        self.arrays = list(parts)

    def _piece(self, a):
        r, c = self.arrays[a].shape
        axis = self.axes[a]
        return (r, c) if axis is None else (r // N_DEV, c) if axis == 0 else (r, c // N_DEV)

    def out_shape(self):
        return [jax.ShapeDtypeStruct((N_DEV,) + self._piece(a), self.arrays[a].dtype) for a in range(self.n)]

    def semaphores(self):
        return [pltpu.SemaphoreType.DMA((7 * self.n,)), pltpu.SemaphoreType.DMA((7 * self.n,)),
                pltpu.SemaphoreType.DMA((self.n,))]

    def _ctx(self, ins, outs, sems):
        send_sems, recv_sems, local_sems = sems
        x, y, c = _position()
        me = 4 * x + 2 * y + c

        def src(a, j):
            return ins[a] if self.axes[a] is None else _block_of(ins[a], j, self.axes[a], self._piece(a))

        def dst(a, j):
            return outs[a].at[j]

        def local():
            return [pltpu.make_async_copy(src(a, me), dst(a, me), local_sems.at[a]) for a in range(self.n)]

        def remote(inbound):
            res = []
            for a in range(self.n):
                for k in range(1, N_DEV):
                    peer = (1 - x if k & 4 else x, 1 - y if k & 2 else y, 1 - c if k & 1 else c)
                    plin = 4 * peer[0] + 2 * peer[1] + peer[2]
                    res.append(pltpu.make_async_remote_copy(
                        src_ref=src(a, plin), dst_ref=dst(a, plin if inbound else me),
                        send_sem=send_sems.at[7 * a + k - 1], recv_sem=recv_sems.at[7 * a + k - 1],
                        device_id=peer, device_id_type=MESH))
            return res

        return local, remote

    def start(self, ins, outs, sems):
        local, remote = self._ctx(ins, outs, sems)
        for cp in local() + remote(False):
            cp.start()

    def finish(self, ins, outs, sems):
        local, remote = self._ctx(ins, outs, sems)
        for cp in remote(True):
            cp.wait_recv()
        for cp in remote(False):
            cp.wait_send()
        for cp in local():
            cp.wait()


def _exchange_start(rider, name):
    n = rider.n
    parts = rider.arrays
    lands = [lax.empty(s.shape, s.dtype) for s in rider.out_shape()]
    hbm = pl.BlockSpec(memory_space=pltpu.HBM)
    sem = pl.BlockSpec(memory_space=pltpu.SEMAPHORE)

    def body(*refs):
        ins, sems = refs[:n], refs[2 * n:2 * n + 3]
        outs, token = refs[2 * n + 3 + n:2 * n + 3 + 2 * n], refs[-1]
        rider.start(ins, outs, sems)
        token[...] = jnp.zeros_like(token)

    res = pl.pallas_call(
        body, name=name,
        out_shape=rider.semaphores() + [pltpu.HBM(p.shape, p.dtype) for p in parts]
                  + [pltpu.HBM(z.shape, z.dtype) for z in lands] + [jax.ShapeDtypeStruct((8, 128), F32)],
        in_specs=[hbm] * (2 * n), out_specs=[sem] * 3 + [hbm] * (2 * n) + [pl.BlockSpec(memory_space=pltpu.VMEM)],
        input_output_aliases={i: 3 + i for i in range(2 * n)},
        compiler_params=pltpu.CompilerParams(has_side_effects=pltpu.SideEffectType.DATAFLOW_SIDE_EFFECTING),
    )(*[pltpu.with_memory_space_constraint(a, pltpu.HBM) for a in parts + lands])
    return res[:3], res[3:3 + n], res[3 + n:3 + 2 * n], res[-1]


def _exchange_wait(rider, started, after, name):
    n = rider.n
    sems, parts, lands, _ = started
    hbm = pl.BlockSpec(memory_space=pltpu.HBM)
    sem = pl.BlockSpec(memory_space=pltpu.SEMAPHORE)

    def body(*refs):
        rider.finish(refs[:n], refs[n:2 * n], refs[2 * n:2 * n + 3])

    res = pl.pallas_call(
        body, name=name, out_shape=[pltpu.HBM(a.shape, a.dtype) for a in list(parts) + list(lands)],
        in_specs=[hbm] * (2 * n) + [sem] * 3 + [ANY] * len(after), out_specs=[hbm] * (2 * n),
        input_output_aliases={i: i for i in range(2 * n)},
        compiler_params=pltpu.CompilerParams(has_side_effects=pltpu.SideEffectType.DATAFLOW_SIDE_EFFECTING),
    )(*parts, *lands, *sems, *after)
    return list(res[n:])


def _pcall(body, *, name, grid, in_specs, out_specs, out_shape, scratch_shapes=(), semantics, vmem_mb, rider=None,
           aliases=None, after=()):
    in_specs, out_specs, out_shape = list(in_specs), list(out_specs), list(out_shape)
    scratch_shapes = list(scratch_shapes)
    aliases = dict(aliases or {})
    if rider is None:
        n_in, after = len(in_specs), list(after)

        def plain(*refs):
            body(*refs[:n_in], *refs[n_in + len(after):])

        call = pl.pallas_call(plain if after else body, name=name, grid=grid, in_specs=in_specs + [ANY] * len(after),
                              out_specs=out_specs, out_shape=out_shape, scratch_shapes=scratch_shapes,
                              input_output_aliases=aliases, compiler_params=_params(semantics, vmem_mb))
        return lambda *args: (list(call(*_hbm(*args), *after)), None)
    n_in, n_out, n_scr = len(in_specs), len(out_specs), len(scratch_shapes)
    r_in, r_shapes = len(rider.arrays), rider.out_shape()
    r_out = len(r_shapes)
    aliases.update({n_in + i: n_out + o for i, o in rider.alias_pairs})
    total = math.prod(grid)
    mid_step = total - 1 if rider.has_mid and rider.late else (3 * total) // 4

    def wrapped(*refs):
        bounds = [0, n_in, r_in, n_out, r_out, n_scr]
        for i in range(1, len(bounds)):
            bounds[i] += bounds[i - 1]
        a, ra, o, ro, s = (refs[bounds[i]:bounds[i + 1]] for i in range(5))
        rs = refs[bounds[5]:]
        step = pl.program_id(0)
        for k in range(1, len(grid)):
            step = step * grid[k] + pl.program_id(k)
        pl.when(step == 0)(lambda: rider.start(ra, ro, rs))
        body(*a, *o, *s)
        if rider.has_mid:
            pl.when(step == mid_step)(lambda: rider.mid(ra, ro, rs))
        pl.when(step == total - 1)(lambda: rider.finish(ra, ro, rs))

    call = pl.pallas_call(
        wrapped, name=name, grid=grid, in_specs=in_specs + [ANY] * r_in, out_specs=out_specs + [ANY] * r_out,
        out_shape=out_shape + r_shapes, scratch_shapes=scratch_shapes + rider.scratch(),
        input_output_aliases=aliases, compiler_params=_params(("arbitrary",) * len(grid), vmem_mb))

    def run(*args):
        res = call(*_hbm(*args), *rider.arrays)
        return list(res[:n_out]), list(res[n_out:])

    return run


def _norm_matmul(x, g, w, *, name, out_dtype, tb, bn, relu=False, save_h=False, rider=None):
    t, d = x.shape
    n = w.shape[1]

    def body(x_ref, g_ref, w_ref, o_ref, *rest):
        h_scr = rest[-1]

        @pl.when(pl.program_id(1) == 0)
        def _():
            h = _rms_fwd(x_ref[...], g_ref[...])[0].astype(BF16)
            h_scr[...] = h
            if save_h:
                rest[0][...] = h

        acc = _dot(h_scr[...], w_ref[...])
        if relu:
            acc = jnp.maximum(acc, 0.0)
        o_ref[...] = acc.astype(out_dtype)

    out_shape = [jax.ShapeDtypeStruct((t, n), out_dtype)]
    out_specs = [pl.BlockSpec((tb, bn), lambda i, j: (i, j))]
    if save_h:
        out_shape.append(jax.ShapeDtypeStruct((t, d), BF16))
        out_specs.append(pl.BlockSpec((tb, d), lambda i, j: (i, 0)))
    res, extra = _pcall(
        body, name=name, grid=(t // tb, n // bn),
        in_specs=[pl.BlockSpec((tb, d), lambda i, j: (i, 0)),
                  pl.BlockSpec((1, d), lambda i, j: (0, 0)),
                  pl.BlockSpec((d, bn), lambda i, j: (0, j))],
        out_specs=out_specs, out_shape=out_shape,
        scratch_shapes=[pltpu.VMEM((tb, d), BF16)],
        semantics=("parallel", "arbitrary"), vmem_mb=48, rider=rider,
    )(x, g, w)
    res = res if save_h else res[0]
    return res if rider is None else (res, extra)


def _matmul_nt_normbwd(dy, w, x, g, dres, *, name, tb, also_bf16=False, after=()):
    t, d = x.shape
    stacked = dy.ndim == 3
    has_res = dres is not None

    def body(dy_ref, w_ref, x_ref, g_ref, *rest):
        rest = list(rest)
        dres_ref = rest.pop(0) if has_res else None
        dx_ref = rest.pop(0)
        dxb_ref = rest.pop(0) if also_bf16 else None
        gg_ref = rest.pop(0)
        i = pl.program_id(0)
        if stacked:
            kb = dy_ref.shape[2]
            dh = _dot_nt(dy_ref[0], w_ref[:, 0:kb])
            for s in range(1, dy_ref.shape[0]):
                dh = dh + _dot_nt(dy_ref[s], w_ref[:, s * kb:(s + 1) * kb])
        else:
            dh = _dot_nt(dy_ref[...], w_ref[...])
        g_v = g_ref[...]
        _, xh, r = _rms_fwd(x_ref[...], g_v)
        dx = _rms_bwd(dh, xh, r, g_v)
        if has_res:
            dx = dx + dres_ref[...]
        dx_ref[...] = dx
        if also_bf16:
            dxb_ref[...] = dx.astype(BF16)
        part = jnp.sum(dh * xh, axis=0, keepdims=True)

        @pl.when(i == 0)
        def _():
            gg_ref[...] = part

        @pl.when(i != 0)
        def _():
            gg_ref[...] += part

    tok = pl.BlockSpec((tb, d), lambda i: (i, 0))
    row = pl.BlockSpec((1, d), lambda i: (0, 0))
    if stacked:
        dy_spec = pl.BlockSpec((dy.shape[0], tb, dy.shape[2]), lambda i: (0, i, 0))
    else:
        dy_spec = pl.BlockSpec((tb, dy.shape[1]), lambda i: (i, 0))
    in_specs = [dy_spec, pl.BlockSpec(w.shape, lambda i: (0, 0)), tok, row]
    args = [dy, w, x, g]
    if has_res:
        in_specs.append(tok)
        args.append(dres)
    out_specs = [tok] + ([tok] if also_bf16 else []) + [row]
    out_shape = ([jax.ShapeDtypeStruct((t, d), F32)] + ([jax.ShapeDtypeStruct((t, d), BF16)] if also_bf16 else [])
                 + [jax.ShapeDtypeStruct((1, d), F32)])
    res, _ = _pcall(
        body, name=name, grid=(t // tb,), in_specs=in_specs, out_specs=out_specs, out_shape=out_shape,
        semantics=("arbitrary",), vmem_mb=56, after=after,
    )(*args)
    return res


def _matmul_tn(a, b, *, name, bm, bn, square_a=False, after=()):
    t, m = a.shape
    stacked = b.ndim == 3
    n = b.shape[0] * bn if stacked else b.shape[1]

    def body(a_ref, b_ref, o_ref):
        av = a_ref[...]
        if square_a:
            av = av.astype(F32)
            av = (av * av).astype(BF16)
        o_ref[...] = _dot_tn(av, b_ref[...]).astype(BF16)

    res, _ = _pcall(
        body, name=name, grid=(m // bm, n // bn),
        in_specs=[pl.BlockSpec((t, bm), lambda i, j: (0, i)),
                  pl.BlockSpec((None, t, bn), lambda i, j: (j, 0, 0)) if stacked
                  else pl.BlockSpec((t, bn), lambda i, j: (0, j))],
        out_specs=[pl.BlockSpec((bm, bn), lambda i, j: (i, j))], out_shape=[jax.ShapeDtypeStruct((m, n), BF16)],
        semantics=("parallel", "parallel"), vmem_mb=56, after=after,
    )(a, b)
    return res[0]


N_RES = 16
SEG = 128
HALF = N_RES * SEG
TI = 32


def _x4(a):
    return a.reshape(a.shape[0] // HALF, N_RES, SEG, a.shape[1])


def _reorder(arrays, inverse, name, rider=None):
    t, c = arrays[0].shape
    n = len(arrays)
    n_i = SEG // TI
    natural = pl.BlockSpec((TI * N_RES, c), lambda s: (s, 0))
    major = pl.BlockSpec((1, N_RES, TI, c), lambda s: (s // n_i, 0, s % n_i, 0))

    def body(*refs):
        scr = refs[-1]
        for i_ref, o_ref in zip(refs[:n], refs[n:2 * n]):
            for cb in range(c // BLK):
                cols = slice(cb * BLK, (cb + 1) * BLK)
                slab = scr.at[cb]
                if inverse:
                    for r in range(N_RES):
                        slab[pl.ds(r, TI, stride=N_RES), :] = i_ref[0, r, :, cols]
                    o_ref[:, cols] = slab[...]
                else:
                    slab[...] = i_ref[:, cols]
                    for r in range(N_RES):
                        o_ref[0, r, :, cols] = slab[pl.ds(r, TI, stride=N_RES), :]

    shape4 = (t // HALF, N_RES, SEG, c)
    res, extra = _pcall(
        body, name=name, grid=(t // (TI * N_RES),),
        in_specs=[major if inverse else natural] * n, out_specs=[natural if inverse else major] * n,
        out_shape=[jax.ShapeDtypeStruct((t, c) if inverse else shape4, F32)] * n,
        scratch_shapes=[pltpu.VMEM((c // BLK, TI * N_RES, BLK), F32)],
        semantics=("parallel",), vmem_mb=32, rider=rider,
    )(*[_x4(a) if inverse else a for a in arrays])
    res = [r.reshape(t, c) for r in res]
    return res if rider is None else (res, extra)


_PATTERNS = ((1, 16, 8, SEG), (4, 4, 32, 4 * SEG), (16, 1, SEG, 0))
_FIRST = {1: 1, 4: 4, 16: 16}


def _group_rows(d, g):
    a = g >> 4
    if d == 16:
        base = a * HALF + (g & 15) * SEG
        prev = base - HALF
    elif d == 4:
        c = (g >> 2) & 3
        base = a * HALF + (g & 3) * SEG + c * 32
        prev = jnp.where(c > 0, base - 32, base - HALF + 96)
    else:
        c = g & 15
        base = a * HALF + c * 8
        prev = jnp.where(c > 0, base - 8, base - HALF + 120)
    return base, prev


def _load_rows(ref, base, n, rows, stride):
    parts = [ref[pl.ds(pl.multiple_of(base + j * stride, 8), rows), :] for j in range(n)]
    return parts[0] if n == 1 else jnp.concatenate(parts, axis=0)


def _store_rows(ref, base, val, n, rows, stride, add=False):
    for j in range(n):
        sl = pl.ds(pl.multiple_of(base + j * stride, 8), rows)
        piece = val[j * rows:(j + 1) * rows, :]
        if add:
            ref[sl, :] += piece
        else:
            ref[sl, :] = piece


def _band_bias(n, rows):
    shift = rows.bit_length() - 1
    lq = lax.broadcasted_iota(jnp.int32, (BLK, BLK), 0)
    lk = lax.broadcasted_iota(jnp.int32, (BLK, BLK), 1)
    iq = (lq & (rows - 1)) * n + (lq >> shift)
    ik = (lk & (rows - 1)) * n + (lk >> shift)
    zero = jnp.zeros((BLK, BLK), F32)
    return jnp.where(ik >= iq, zero, NEG_INF), jnp.where(ik <= iq, zero, NEG_INF)


def _set_bias(bias_scr, n, rows):
    prev_b, cur_b = _band_bias(n, rows)
    for half in range(2):
        bias_scr[half * BLK:(half + 1) * BLK, 0:BLK] = prev_b
        bias_scr[half * BLK:(half + 1) * BLK, BLK:2 * BLK] = cur_b


SCALE = 1.0 / math.sqrt(HEAD_DIM)


def _head_consts(value=1.0):
    lane_lo = lax.broadcasted_iota(jnp.int32, (BLK, BLK), 1) < HEAD_DIM
    return lane_lo, [jnp.where(lane_lo, value, 0.0).astype(BF16), jnp.where(lane_lo, 0.0, value).astype(BF16)]


def _stack_heads(v, head_mask):
    return jnp.concatenate([v * head_mask[0], v * head_mask[1]], axis=0)


def _unstack_heads(v2, lane_lo):
    return jnp.where(lane_lo, v2[:BLK], v2[BLK:])


def _rows_per_head(v, lane_lo):
    rolled = pltpu.roll(v, HEAD_DIM, axis=1)
    return jnp.concatenate([jnp.where(lane_lo, v, rolled), jnp.where(lane_lo, rolled, v)], axis=0)


WIDTH = 4


def _loop(lo, hi, fn, width=None):
    if width is None:
        def body(g, carry):
            fn(g)
            return carry

        if hi > lo:
            lax.fori_loop(lo, hi, body, 0)
        return
    while hi > lo:
        trips = (hi - lo) // width
        if trips:
            def body(i, carry, lo=lo, width=width):
                fn([lo + width * i + j for j in range(width)])
                return carry

            lax.fori_loop(0, trips, body, 0)
            lo += trips * width
        width = max(1, width // 2)


def _mix_weights(l1, l2, l3):
    mx = jnp.maximum(jnp.maximum(l1, l2), l3)
    e1, e2, e3 = jnp.exp(l1 - mx), jnp.exp(l2 - mx), jnp.exp(l3 - mx)
    inv = 1.0 / (e1 + e2 + e3)
    return e1 * inv, e2 * inv, e3 * inv


def _attention_fwd(qkv, rider=None):
    t = qkv.shape[0]
    groups = 16 * (t // HALF)

    def body(q_ref, k_ref, v_ref, attn_ref, l1_ref, l2_ref, l3_ref, o_scr, bias_scr):
        lane_lo, q_mask = _head_consts(SCALE)
        l_refs = (l1_ref, l2_ref, l3_ref)
        for p, (d, n, rows, stride) in enumerate(_PATTERNS):
            _set_bias(bias_scr, n, rows)
            o_p, l_p = o_scr.at[p], l_refs[p]

            def block(gs, has_prev):
                at = [_group_rows(d, g) for g in gs]

                def load(ref, b):
                    return _load_rows(ref, b, n, rows, stride).astype(BF16)

                q2 = [_stack_heads(load(q_ref, b), q_mask) for b, _ in at]
                k2 = [load(k_ref, b) for b, _ in at]
                v2 = [load(v_ref, b) for b, _ in at]
                if has_prev:
                    k2 = [jnp.concatenate([load(k_ref, pv), k], axis=0) for (_, pv), k in zip(at, k2)]
                    v2 = [jnp.concatenate([load(v_ref, pv), v], axis=0) for (_, pv), v in zip(at, v2)]
                s = [_dot_nt(q, k) for q, k in zip(q2, k2)]
                s = [x + (bias_scr[...] if has_prev else bias_scr[:, BLK:2 * BLK]) for x in s]
                mx = [jnp.max(x, axis=1, keepdims=True) for x in s]
                e = [jnp.exp(x - m) for x, m in zip(s, mx)]
                den = [jnp.sum(x, axis=1, keepdims=True) for x in e]
                o2 = [_dot(x.astype(BF16), v) * (1.0 / dn) for x, v, dn in zip(e, v2, den)]
                lse2 = [jnp.broadcast_to(m + jnp.log(dn), (2 * BLK, BLK)) for m, dn in zip(mx, den)]
                for (b, _), o, l in zip(at, o2, lse2):
                    _store_rows(o_p, b, _unstack_heads(o, lane_lo), n, rows, stride)
                    _store_rows(l_p, b, _unstack_heads(l, lane_lo), n, rows, stride)

            _loop(0, _FIRST[d], lambda gs: block(gs, False), width=WIDTH)
            _loop(_FIRST[d], groups, lambda gs: block(gs, True), width=WIDTH)

        def mix(i):
            sl = pl.ds(pl.multiple_of(i * 256, 256), 256)
            w = _mix_weights(l1_ref[sl, :], l2_ref[sl, :], l3_ref[sl, :])
            attn_ref[sl, :] = w[0] * o_scr[0, sl, :] + w[1] * o_scr[1, sl, :] + w[2] * o_scr[2, sl, :]

        _loop(0, t // 256, mix)

    def col(c0):
        return pl.BlockSpec((t, BLK), lambda hp: (0, c0 + hp))

    res, extra = _pcall(
        body, name="attention_fwd", grid=(4,), in_specs=[col(0), col(4), col(8)], out_specs=[col(0)] * 4,
        out_shape=[jax.ShapeDtypeStruct((t, 512), F32)] * 4,
        scratch_shapes=[pltpu.VMEM((3, t, BLK), F32), pltpu.VMEM((2 * BLK, 2 * BLK), F32)],
        semantics=("parallel",), vmem_mb=48, rider=rider,
    )(qkv, qkv, qkv)
    return res if rider is None else (res, extra)


def _attention_bwd(qkv, dattn, dsum, lses, dproj):
    t = qkv.shape[0]
    groups = 16 * (t // HALF)

    def body(q_ref, k_ref, v_ref, da_ref, ds_ref, l1_ref, l2_ref, l3_ref, kept_ref, out_ref, acc, bias_scr):
        del kept_ref
        lane_lo, head_mask = _head_consts()
        q_mask = _head_consts(SCALE)[1]
        l_refs = (l1_ref, l2_ref, l3_ref)

        def clear(i):
            sl = pl.ds(pl.multiple_of(i * 512, 512), 512)
            for s in range(3):
                acc[s, sl, :] = jnp.zeros((512, BLK), F32)

        _loop(0, t // 512, clear)
        dq_acc, dk_acc, dv_acc = acc.at[0], acc.at[1], acc.at[2]
        for p, (d, n, rows, stride) in enumerate(_PATTERNS):
            _set_bias(bias_scr, n, rows)

            def block(gs, has_prev):
                at = [_group_rows(d, g) for g in gs]

                def load(ref, b):
                    return _load_rows(ref, b, n, rows, stride)

                def put(ref, b, val):
                    _store_rows(ref, b, val, n, rows, stride, add=True)

                def wide(x):
                    return jnp.concatenate([x, x], axis=1) if has_prev else x

                lse = [[load(ref, b) for ref in l_refs] for b, _ in at]
                w = [_mix_weights(*ls)[p] for ls in lse]
                do2 = [_stack_heads((wg * load(da_ref, b)).astype(BF16), head_mask) for wg, (b, _) in zip(w, at)]
                dl2 = [wide(_rows_per_head(wg * load(ds_ref, b), lane_lo)) for wg, (b, _) in zip(w, at)]
                lse2 = [wide(_rows_per_head(ls[p], lane_lo)) for ls in lse]
                q2 = [_stack_heads(load(q_ref, b).astype(BF16), q_mask) for b, _ in at]
                k2 = [load(k_ref, b).astype(BF16) for b, _ in at]
                v2 = [load(v_ref, b).astype(BF16) for b, _ in at]
                if has_prev:
                    k2 = [jnp.concatenate([load(k_ref, pv).astype(BF16), k], axis=0) for (_, pv), k in zip(at, k2)]
                    v2 = [jnp.concatenate([load(v_ref, pv).astype(BF16), v], axis=0) for (_, pv), v in zip(at, v2)]
                s = [_dot_nt(q, k) for q, k in zip(q2, k2)]
                dp = [_dot_nt(do, v) for do, v in zip(do2, v2)]
                pr = [jnp.exp(x + (bias_scr[...] if has_prev else bias_scr[:, BLK:2 * BLK]) - l)
                      for x, l in zip(s, lse2)]
                ds = [(pg * (x - dl)).astype(BF16) for pg, x, dl in zip(pr, dp, dl2)]
                dq2 = [_dot(x, k) * SCALE for x, k in zip(ds, k2)]
                dk2 = [_dot_tn(x, q) for x, q in zip(ds, q2)]
                dv2 = [_dot_tn(pg.astype(BF16), do) for pg, do in zip(pr, do2)]
                for (b, pv), dq, dk, dv in zip(at, dq2, dk2, dv2):
                    put(dq_acc, b, _unstack_heads(dq, lane_lo))
                    if has_prev:
                        put(dk_acc, pv, dk[:BLK])
                        put(dv_acc, pv, dv[:BLK])
                        put(dk_acc, b, dk[BLK:])
                        put(dv_acc, b, dv[BLK:])
                    else:
                        put(dk_acc, b, dk)
                        put(dv_acc, b, dv)

            _loop(0, _FIRST[d], lambda gs: block(gs, False), width=WIDTH)
            _loop(_FIRST[d], groups, lambda gs: block(gs, True), width=WIDTH)

        def emit(i):
            sl = pl.ds(pl.multiple_of(i * 512, 512), 512)
            for s in range(3):
                out_ref[s, sl, :] = acc[s, sl, :].astype(BF16)

        _loop(0, t // 512, emit)

    def col(c0):
        return pl.BlockSpec((t, BLK), lambda hp: (0, c0 + hp))

    res, _ = _pcall(
        body, name="attention_bwd", grid=(4,),
        in_specs=[col(0), col(4), col(8)] + [col(0)] * 5 + [ANY],
        out_specs=[pl.BlockSpec((3, t, BLK), lambda hp: (0, 0, hp))],
        out_shape=[jax.ShapeDtypeStruct(dproj.shape, BF16)],
        scratch_shapes=[pltpu.VMEM((3, t, BLK), F32), pltpu.VMEM((2 * BLK, 2 * BLK), F32)],
        semantics=("parallel",), vmem_mb=56, aliases={8: 0},
    )(qkv, qkv, qkv, dattn, dsum, *lses, dproj)
    return res[0]


def _order_specs(t):
    n_i = SEG // TI
    nblk = (t // HALF) * n_i
    per = TI // 8

    def main(c, col=0):
        return pl.BlockSpec((1, N_RES, TI, c), lambda s: (s // n_i, 0, s % n_i, col))

    def before(c, col=0):
        return pl.BlockSpec((1, 2, 8, c), lambda s: (jnp.maximum(s - 1, 0) // n_i, N_RES // 2 - 1,
                                                     (jnp.maximum(s - 1, 0) % n_i) * per + per - 1, col))

    def after(c, col=0):
        return pl.BlockSpec((1, 2, 8, c), lambda s: (jnp.minimum(s + 1, nblk - 1) // n_i, 0,
                                                     (jnp.minimum(s + 1, nblk - 1) % n_i) * per, col))

    return nblk, main, before, after


def _shift_in(v, row_in, up):
    rows = v.shape[0]
    idx = lax.broadcasted_iota(jnp.int32, v.shape, 0)
    fill = jnp.broadcast_to(row_in, v.shape)
    if up:
        return jnp.where(idx == rows - 1, fill, pltpu.roll(v, rows - 1, axis=0))
    return jnp.where(idx == 0, fill, pltpu.roll(v, 1, axis=0))


def _taps_behind(u, before):
    s15 = _shift_in(u[N_RES - 1], before[1, 7:8, :], up=False)
    s14 = _shift_in(u[N_RES - 2], before[0, 7:8, :], up=False)
    m1 = jnp.concatenate([s15[None], u[:N_RES - 1]], axis=0)
    m2 = jnp.concatenate([s14[None], s15[None], u[:N_RES - 2]], axis=0)
    return m1, m2


def _taps_ahead(u, after):
    t0 = _shift_in(u[0], after[0, 0:1, :], up=True)
    t1 = _shift_in(u[1], after[1, 0:1, :], up=True)
    p1 = jnp.concatenate([u[1:], t0[None]], axis=0)
    p2 = jnp.concatenate([u[2:], t0[None], t1[None]], axis=0)
    return p1, p2


def _conv_fwd(gates, before, first, cw):
    bg, cg, xc = gates[..., 0:512], gates[..., 512:1024], gates[..., 1024:1536]
    u = cg * xc
    ub = before[..., 512:1024] * before[..., 1024:1536]
    ub = jnp.where(first, jnp.zeros_like(ub), ub)
    m1, m2 = _taps_behind(u, ub)
    conv = m2 * cw[0:1, :] + m1 * cw[1:2, :] + u * cw[2:3, :]
    return bg, u, m1, m2, conv


def _sum_tokens(v):
    return jnp.sum(jnp.sum(v, axis=0), axis=0, keepdims=True)


def _mixer_fwd(x, attn, gates, cw, g_a, g_c, w_out):
    t, d = x.shape
    nblk, main, before, _ = _order_specs(t)
    rows = N_RES * TI

    def body(x_ref, at_ref, gt_ref, gb_ref, cw_ref, ga_ref, gc_ref, wa_ref, wb_ref, x1_ref, mg_ref):
        an = _rms_fwd(at_ref[0], ga_ref[...])[0].astype(BF16)
        bg, _, _, _, conv = _conv_fwd(gt_ref[0], gb_ref[0], pl.program_id(0) == 0, cw_ref[...])
        cn = _rms_fwd(bg * conv, gc_ref[...])[0].astype(BF16)
        mg_ref[0, :, :, 0:512] = an
        mg_ref[0, :, :, 512:1024] = cn
        y = _dot(an.reshape(rows, 512), wa_ref[...]) + _dot(cn.reshape(rows, 512), wb_ref[...])
        x1_ref[0] = x_ref[0] + y.reshape(N_RES, TI, d)

    const = lambda r, c, i0=0: pl.BlockSpec((r, c), lambda s: (i0, 0))
    x1, merged = pl.pallas_call(
        body, name="mixer_fwd", grid=(nblk,),
        in_specs=[main(d), main(512), main(1536, 1), before(1536, 1), const(3, 512), const(1, 512), const(1, 512),
                  const(512, d), const(512, d, 1)],
        out_specs=[main(d), main(d)],
        out_shape=[jax.ShapeDtypeStruct(_x4(x).shape, F32), jax.ShapeDtypeStruct(_x4(x).shape, BF16)],
        compiler_params=_params(("parallel",), 48),
    )(*_hbm(_x4(x), _x4(attn), _x4(gates), _x4(gates), cw, g_a, g_c, w_out, w_out))
    return x1.reshape(t, d), merged.reshape(t, d)


def _mixer_bwd(dx1, attn, gates, cw, g_a, g_c, w_out, head_sum, after=()):
    t, d = dx1.shape
    nblk, main, before, _ = _order_specs(t)
    rows = N_RES * TI

    def body(dx_ref, at_ref, gt_ref, gb_ref, cw_ref, ga_ref, gc_ref, wa_ref, wb_ref, hs_ref,
             da_ref, dsum_ref, dy_ref, gga_ref, ggc_ref):
        s = pl.program_id(0)
        dxb = dx_ref[0].reshape(rows, d).astype(BF16)
        dma = _dot_nt(dxb, wa_ref[...]).reshape(N_RES, TI, 512)
        dmc = _dot_nt(dxb, wb_ref[...]).reshape(N_RES, TI, 512)
        attn_v, g_av = at_ref[0], ga_ref[...]
        _, ah, ra = _rms_fwd(attn_v, g_av)
        dattn = _rms_bwd(dma, ah, ra, g_av)
        da_ref[0] = dattn
        z = (dattn * attn_v).reshape(rows, 512)
        hs = hs_ref[...]
        z1 = z.astype(BF16)
        z2 = (z - z1.astype(F32)).astype(BF16)
        dsum_ref[0] = (_dot(z1, hs) + _dot(z2, hs)).reshape(N_RES, TI, 512)
        bg, _, _, _, conv = _conv_fwd(gt_ref[0], gb_ref[0], s == 0, cw_ref[...])
        g_cv = gc_ref[...]
        _, yh, rc = _rms_fwd(bg * conv, g_cv)
        dy_ref[0] = _rms_bwd(dmc, yh, rc, g_cv)
        pa, pc = _sum_tokens(dma * ah), _sum_tokens(dmc * yh)

        @pl.when(s == 0)
        def _():
            gga_ref[...] = pa
            ggc_ref[...] = pc

        @pl.when(s != 0)
        def _():
            gga_ref[...] += pa
            ggc_ref[...] += pc

    const = lambda r, c, i0=0: pl.BlockSpec((r, c), lambda s: (i0, 0))
    shape4 = _x4(attn).shape
    res, _ = _pcall(
        body, name="mixer_bwd", grid=(nblk,),
        in_specs=[main(d), main(512), main(1536, 1), before(1536, 1), const(3, 512), const(1, 512), const(1, 512),
                  const(512, d), const(512, d, 1), const(512, 512)],
        out_specs=[main(512)] * 3 + [const(1, 512), const(1, 512)],
        out_shape=[jax.ShapeDtypeStruct(shape4, F32)] * 3 + [jax.ShapeDtypeStruct((1, 512), F32)] * 2,
        semantics=("arbitrary",), vmem_mb=48, after=after,
    )(_x4(dx1), _x4(attn), _x4(gates), _x4(gates), cw, g_a, g_c, w_out, w_out, head_sum)
    return [r.reshape(t, 512) for r in res[:3]] + res[3:]


def _conv_bwd(dy, gates, cw):
    t = dy.shape[0]
    nblk, main, before, after = _order_specs(t)
    n_i = SEG // TI

    def body(dy_ref, dya_ref, gt_ref, gb_ref, ga_ref, cw_ref, dp_ref, gcw_ref):
        s = pl.program_id(0)
        cw_v, gates_v = cw_ref[...], gt_ref[0]
        bg, u, m1, m2, conv = _conv_fwd(gates_v, gb_ref[0], s == 0, cw_v)
        dy_v = dy_ref[0]
        dconv = dy_v * bg
        dca = dya_ref[0] * ga_ref[0][..., 0:512]
        dca = jnp.where(s == nblk - 1, jnp.zeros_like(dca), dca)
        p1, p2 = _taps_ahead(dconv, dca)
        du = dconv * cw_v[2:3, :] + p1 * cw_v[1:2, :] + p2 * cw_v[0:1, :]
        dp_ref[0, 0] = (dy_v * conv).astype(BF16)
        dp_ref[1, 0] = (du * gates_v[..., 1024:1536]).astype(BF16)
        dp_ref[2, 0] = (du * gates_v[..., 512:1024]).astype(BF16)
        parts = [_sum_tokens(dconv * m2), _sum_tokens(dconv * m1), _sum_tokens(dconv * u)]

        @pl.when(s == 0)
        def _():
            gcw_ref[...] = jnp.zeros_like(gcw_ref)

        for tap in range(3):
            gcw_ref[tap:tap + 1, :] += parts[tap]

    (dproj, gcw), _ = _pcall(
        body, name="conv_bwd", grid=(nblk,),
        in_specs=[main(512), after(512), main(1536, 1), before(1536, 1), after(1536, 1),
                  pl.BlockSpec((3, 512), lambda s: (0, 0))],
        out_specs=[pl.BlockSpec((3, 1, N_RES, TI, 512), lambda s: (1, s // n_i, 0, s % n_i, 0)),
                   pl.BlockSpec((8, 512), lambda s: (0, 0))],
        out_shape=[jax.ShapeDtypeStruct((6, t // HALF, N_RES, SEG, 512), BF16), jax.ShapeDtypeStruct((8, 512), F32)],
        semantics=("arbitrary",), vmem_mb=40,
    )(_x4(dy), _x4(dy), _x4(gates), _x4(gates), _x4(gates), cw)
    return dproj.reshape(6, t, 512), gcw


def _xattn_fwd(x1, g, w_q, kv, w_o, *, tb):
    t, d = x1.shape
    hd = d // N_MEM_HEADS
    m = kv.shape[0]

    def body(x_ref, g_ref, wq_ref, k_ref, v_ref, wo_ref, x2_ref, h_ref, q_ref, o_ref):
        xv = x_ref[...]
        h = _rms_fwd(xv, g_ref[...])[0].astype(BF16)
        h_ref[...] = h
        q = _dot(h, wq_ref[...]).astype(BF16)
        q_ref[...] = q
        for hh in range(N_MEM_HEADS):
            sl = slice(hh * hd, (hh + 1) * hd)
            s = _dot_nt(q[:, sl], k_ref[:, sl]) * (1.0 / 16.0)
            e = jnp.exp(s - jnp.max(s, axis=1, keepdims=True))
            p = e / jnp.sum(e, axis=1, keepdims=True)
            o_ref[:, sl] = _dot(p.astype(BF16), v_ref[:, sl]).astype(BF16)
        x2_ref[...] = xv + _dot(o_ref[...], wo_ref[...])

    tok = pl.BlockSpec((tb, d), lambda i: (i, 0))
    full = pl.BlockSpec((d, d), lambda i: (0, 0))
    return pl.pallas_call(
        body, name="xattn_fwd", grid=(t // tb,),
        in_specs=[tok, pl.BlockSpec((1, d), lambda i: (0, 0)), full,
                  pl.BlockSpec((m, d), lambda i: (0, 0)), pl.BlockSpec((m, d), lambda i: (0, 1)), full],
        out_specs=[tok] * 4,
        out_shape=[jax.ShapeDtypeStruct((t, d), F32)] + [jax.ShapeDtypeStruct((t, d), BF16)] * 3,
        compiler_params=_params(("parallel",), 48),
    )(*_hbm(x1, g, w_q, kv, kv, w_o))


def _xattn_bwd(dx2, x1, g, q, w_q, kv, w_o, *, tb, after=()):
    t, d = x1.shape
    hd = d // N_MEM_HEADS
    m = kv.shape[0]

    def body(dx2_ref, x_ref, g_ref, q_ref, wq_ref, k_ref, v_ref, wo_ref,
             dx1_ref, dx1b_ref, dq_ref, dk_ref, dv_ref, gg_ref):
        i = pl.program_id(0)

        @pl.when(i == 0)
        def _():
            dk_ref[...] = jnp.zeros_like(dk_ref)
            dv_ref[...] = jnp.zeros_like(dv_ref)

        dx2 = dx2_ref[...]
        do = _dot_nt(dx2.astype(BF16), wo_ref[...]).astype(BF16)
        for hh in range(N_MEM_HEADS):
            sl = slice(hh * hd, (hh + 1) * hd)
            qh, kh, vh, doh = q_ref[:, sl], k_ref[:, sl], v_ref[:, sl], do[:, sl]
            s = _dot_nt(qh, kh) * (1.0 / 16.0)
            e = jnp.exp(s - jnp.max(s, axis=1, keepdims=True))
            p = e / jnp.sum(e, axis=1, keepdims=True)
            dp = _dot_nt(doh, vh)
            ds = (p * (dp - jnp.sum(dp * p, axis=1, keepdims=True)) * (1.0 / 16.0)).astype(BF16)
            dq_ref[:, sl] = _dot(ds, kh).astype(BF16)
            dk_ref[:, sl] += _dot_tn(ds, qh)
            dv_ref[:, sl] += _dot_tn(p.astype(BF16), doh)
        dh = _dot_nt(dq_ref[...], wq_ref[...])
        g_v = g_ref[...]
        _, xh, r = _rms_fwd(x_ref[...], g_v)
        dx1 = dx2 + _rms_bwd(dh, xh, r, g_v)
        dx1_ref[...] = dx1
        dx1b_ref[...] = dx1.astype(BF16)
        part = jnp.sum(dh * xh, axis=0, keepdims=True)

        @pl.when(i == 0)
        def _():
            gg_ref[...] = part

        @pl.when(i != 0)
        def _():
            gg_ref[...] += part

    tok = pl.BlockSpec((tb, d), lambda i: (i, 0))
    full = pl.BlockSpec((d, d), lambda i: (0, 0))
    acc = pl.BlockSpec((m, d), lambda i: (0, 0))
    res, _ = _pcall(
        body, name="xattn_bwd", grid=(t // tb,),
        in_specs=[tok, tok, pl.BlockSpec((1, d), lambda i: (0, 0)), tok, full,
                  pl.BlockSpec((m, d), lambda i: (0, 0)), pl.BlockSpec((m, d), lambda i: (0, 1)), full],
        out_specs=[tok, tok, tok, acc, acc, pl.BlockSpec((1, d), lambda i: (0, 0))],
        out_shape=[jax.ShapeDtypeStruct((t, d), F32), jax.ShapeDtypeStruct((t, d), BF16),
                   jax.ShapeDtypeStruct((t, d), BF16),
                   jax.ShapeDtypeStruct((m, d), F32), jax.ShapeDtypeStruct((m, d), F32),
                   jax.ShapeDtypeStruct((1, d), F32)],
        semantics=("arbitrary",), vmem_mb=48, after=after,
    )(dx2, x1, g, q, w_q, kv, kv, w_o)
    return res


def _mlp_down_loss(a, w_down, x2, tgt, g, *, tb):
    t, d = x2.shape
    f = a.shape[1]

    def body(a_ref, w_ref, x_ref, t_ref, g_ref, dx_ref, dxb_ref, loss_ref, gg_ref):
        i = pl.program_id(0)
        av = a_ref[...].astype(F32)
        x3 = x_ref[...] + _dot((av * av).astype(BF16), w_ref[...])
        g_v = g_ref[...]
        out, xh, r = _rms_fwd(x3, g_v)
        err = out - t_ref[...]
        dout = err * (1.0 / d)
        dx = _rms_bwd(dout, xh, r, g_v)
        dx_ref[...] = dx
        dxb_ref[...] = dx.astype(BF16)
        part = jnp.sum(dout * xh, axis=0, keepdims=True)
        lpart = 0.5 * jnp.sum(jnp.mean(err * err, axis=-1, keepdims=True), axis=0, keepdims=True)
        lpart = jnp.broadcast_to(lpart, loss_ref.shape)

        @pl.when(i == 0)
        def _():
            gg_ref[...] = part
            loss_ref[...] = lpart

        @pl.when(i != 0)
        def _():
            gg_ref[...] += part
            loss_ref[...] += lpart

    tok = pl.BlockSpec((tb, d), lambda i: (i, 0))
    return pl.pallas_call(
        body, name="mlp_down_loss", grid=(t // tb,),
        in_specs=[pl.BlockSpec((tb, f), lambda i: (i, 0)), pl.BlockSpec((f, d), lambda i: (0, 0)), tok, tok,
                  pl.BlockSpec((1, d), lambda i: (0, 0))],
        out_specs=[tok, tok, pl.BlockSpec((8, 128), lambda i: (0, 0)), pl.BlockSpec((1, d), lambda i: (0, 0))],
        out_shape=[jax.ShapeDtypeStruct((t, d), F32), jax.ShapeDtypeStruct((t, d), BF16),
                   jax.ShapeDtypeStruct((8, 128), F32), jax.ShapeDtypeStruct((1, d), F32)],
        compiler_params=_params(("arbitrary",), 56),
    )(*_hbm(a, w_down, x2, tgt, g))


def _mlp_dpre(dx3, w_down, a, *, tb, bn):
    t, d = dx3.shape
    f = a.shape[1]

    def body(dx_ref, w_ref, a_ref, o_ref):
        o_ref[...] = (2.0 * a_ref[...].astype(F32) * _dot_nt(dx_ref[...], w_ref[...])).astype(BF16)

    return pl.pallas_call(
        body, name="mlp_dpre", grid=(t // tb, f // bn),
        in_specs=[pl.BlockSpec((tb, d), lambda i, j: (i, 0)), pl.BlockSpec((bn, d), lambda i, j: (j, 0)),
                  pl.BlockSpec((tb, bn), lambda i, j: (i, j))],
        out_specs=pl.BlockSpec((tb, bn), lambda i, j: (i, j)),
        out_shape=jax.ShapeDtypeStruct((t, f), BF16),
        compiler_params=_params(("parallel", "arbitrary"), 48),
    )(*_hbm(dx3, w_down, a))


def _adamw(gsum, w, m, v):
    m_new = ADAM_B1 * m + (1.0 - ADAM_B1) * gsum
    v_new = ADAM_B2 * v + (1.0 - ADAM_B2) * (gsum * gsum)
    m_hat = m_new / (1.0 - ADAM_B1 ** ADAM_STEP)
    v_hat = v_new / (1.0 - ADAM_B2 ** ADAM_STEP)
    delta = -ADAM_LR * (m_hat / (jnp.sqrt(v_hat) + ADAM_EPS) + ADAM_WD * w)
    return delta, m_new, v_new


def _sum_adamw(parts, w, m, v, *, name, tr):
    r, c = w.shape

    def body(p_ref, w_ref, m_ref, v_ref, g_ref, d_ref, mo_ref, vo_ref):
        g = p_ref[0].astype(F32)
        for k in range(1, N_DEV):
            g = g + p_ref[k].astype(F32)
        g_ref[...] = g
        d_ref[...], mo_ref[...], vo_ref[...] = _adamw(g, w_ref[...], m_ref[...], v_ref[...])

    blk = pl.BlockSpec((tr, c), lambda i: (i, 0))
    return pl.pallas_call(
        body, name=name, grid=(r // tr,),
        in_specs=[pl.BlockSpec((N_DEV, tr, c), lambda i: (0, i, 0)), blk, blk, blk],
        out_specs=[blk] * 4, out_shape=[jax.ShapeDtypeStruct((r, c), F32)] * 4,
        compiler_params=_params(("parallel",), 40),
    )(*_hbm(parts, w, m, v))


def _sum_small(parts):
    _, r, c = parts.shape

    def body(p_ref, o_ref):
        s = p_ref[0]
        for k in range(1, N_DEV):
            s = s + p_ref[k]
        o_ref[...] = s

    return pl.pallas_call(body, name="sum_small", out_shape=jax.ShapeDtypeStruct((r, c), F32))(parts)


def _adamw_small(g, w, m, v):
    def body(g_ref, w_ref, m_ref, v_ref, d_ref, mo_ref, vo_ref):
        d_ref[...], mo_ref[...], vo_ref[...] = _adamw(g_ref[...], w_ref[...], m_ref[...], v_ref[...])

    return pl.pallas_call(body, name="adamw_small", out_shape=[jax.ShapeDtypeStruct(g.shape, F32)] * 3)(g, w, m, v)


def _head_sum_matrix():
    r = lax.broadcasted_iota(jnp.int32, (512, 512), 0) // HEAD_DIM
    c = lax.broadcasted_iota(jnp.int32, (512, 512), 1) // HEAD_DIM
    return (r == c).astype(BF16)


_SHARD_AXIS = dict(w_in=1, w_out=0, w_q=0, w_kv=1, w_o=0, w_up=1, w_down=0, conv_w=None, small=None)


class _Weights:
    def __init__(self, full, shards=None):
        self.full = dict(full)
        self.shards = shards

    def rider(self, names, late=False):
        if self.shards is None:
            return None
        return _Gather([self.shards[n] for n in names], [_SHARD_AXIS[n] for n in names], late)

    def arrived(self, names, gathered):
        if gathered is not None:
            for n, g in zip(names, gathered):
                self.full[n] = g.transpose(1, 0, 2).reshape(g.shape[1], -1) if n == "conv_w" else g

    def __getitem__(self, name):
        return self.full[name]


class _Grads:
    def __init__(self, distributed):
        self.distributed = distributed
        self.local = {}
        self.pending = {}

    def add(self, name, g):
        self.local[name] = g

    def send(self, *names):
        if not self.distributed:
            return []
        rider = _Exchange([self.local[n] for n in names], [_SHARD_AXIS[n] for n in names])
        started = _exchange_start(rider, "send_" + "_".join(names))
        self.pending[names[0]] = (names, rider, started)
        return [started[3]]

    def wait(self, first_name, after):
        names, rider, started = self.pending.pop(first_name)
        return _exchange_wait(rider, started, after, "wait_" + "_".join(names))


def _ride(fn, *args, rider=None, **kw):
    if rider is None:
        return fn(*args, **kw), None
    return fn(*args, rider=rider, **kw)


def _local_step(x, mem, tgt, gains, weights, grads):
    names = ["w_in", "conv_w"]
    (x, tgt), got = _ride(_reorder, [x, tgt], False, "reorder_in", rider=weights.rider(names, late=True))
    weights.arrived(names, got)
    w_in, cw = weights["w_in"], weights["conv_w"]

    names = ["w_out", "w_kv"]
    (proj, h1), got = _ride(_norm_matmul, x, gains["g_mix"], w_in, name="proj", out_dtype=F32, tb=1024, bn=768,
                            save_h=True, rider=weights.rider(names))
    weights.arrived(names, got)
    names = ["w_q", "w_o", "w_up"]
    (attn, *lses), got = _ride(_attention_fwd, proj, rider=weights.rider(names))
    weights.arrived(names, got)
    x1, merged = _mixer_fwd(x, attn, proj, cw, gains["g_attn_out"], gains["g_conv_out"], weights["w_out"])
    kv, mem_n = _norm_matmul(mem, gains["g_mem"], weights["w_kv"], name="mem_kv", out_dtype=BF16, tb=mem.shape[0],
                             bn=1024, save_h=True)
    x2, h2, qm, om = _xattn_fwd(x1, gains["g_xattn"], weights["w_q"], kv, weights["w_o"], tb=512)
    w_up = weights["w_up"]
    (a, h3), got = _ride(_norm_matmul, x2, gains["g_mlp"], w_up, name="mlp_up", out_dtype=BF16, tb=1024, bn=1024,
                         relu=True, save_h=True, rider=weights.rider(["w_down"], late=True))
    weights.arrived(["w_down"], got)
    w_down = weights["w_down"]
    dx3, dx3b, loss_blk, gg_final = _mlp_down_loss(a, w_down, x2, tgt, gains["g_final"], tb=256)

    dpre = _mlp_dpre(dx3b, w_down, a, tb=1024, bn=1024)
    grads.add("w_down", _matmul_tn(a, dx3b, name="grad_w_down", bm=512, bn=1024, square_a=True))
    sent = grads.send("w_down")
    grads.add("w_up", _matmul_tn(h3, dpre, name="grad_w_up", bm=1024, bn=512, after=sent))
    sent = grads.send("w_up")
    dx2, dx2b, gg_mlp = _matmul_nt_normbwd(dpre, w_up, x2, gains["g_mlp"], dx3, name="mlp_dx", tb=512,
                                           also_bf16=True, after=sent)

    grads.add("w_o", _matmul_tn(om, dx2b, name="grad_w_o", bm=1024, bn=512))
    sent = grads.send("w_o")
    dx1, dx1b, dqm, dk, dv, gg_xattn = _xattn_bwd(dx2, x1, gains["g_xattn"], qm, weights["w_q"], kv, weights["w_o"],
                                                  tb=512, after=sent)
    grads.add("w_q", _matmul_tn(h2, dqm, name="grad_w_q", bm=1024, bn=512))
    dkv = jnp.concatenate([dk, dv], axis=1).astype(BF16)
    grads.add("w_kv", _matmul_tn(mem_n, dkv, name="grad_w_kv", bm=1024, bn=1024))
    _, gg_mem = _matmul_nt_normbwd(dkv, weights["w_kv"], mem, gains["g_mem"], None, name="mem_dx", tb=mem.shape[0])

    grads.add("w_out", _matmul_tn(merged, dx1b, name="grad_w_out", bm=1024, bn=512))
    sent = grads.send("w_q", "w_kv", "w_out")
    dattn, dsum, dy, gg_attn, gg_conv = _mixer_bwd(dx1, attn, proj, cw, gains["g_attn_out"], gains["g_conv_out"],
                                                   weights["w_out"], _head_sum_matrix(), after=sent)
    dproj, gcw = _conv_bwd(dy, proj, cw)
    dproj = _attention_bwd(proj, dattn, dsum, lses, dproj)
    grads.add("w_in", _matmul_tn(h1, dproj, name="grad_w_in", bm=1024, bn=512))
    sent = grads.send("w_in")
    grad_x, gg_mix = _matmul_nt_normbwd(dproj, w_in, x, gains["g_mix"], dx1, name="mixer_dx", tb=512, after=sent)

    def part(v):
        return jnp.pad(v, ((0, SMALL_PART - v.shape[0]), (0, 1024 - v.shape[1])))

    parts = [gg_mix, gg_xattn, gg_mem, gg_mlp, gg_final, jnp.concatenate([gg_attn, gg_conv], axis=1), gcw, loss_blk]
    grads.add("small", jnp.concatenate([part(v) for v in parts], axis=0))
    (grad_x,) = _reorder([grad_x], True, "reorder_out")
    return grad_x


SMALL_PART = 8
_BIG = ("w_in", "w_out", "w_q", "w_kv", "w_o", "w_up", "w_down")
_GAIN_ROWS = ("g_mix", "g_xattn", "g_mem", "g_mlp", "g_final")


def _pack_small(vals, conv):
    rows = [vals[k].reshape(1, -1) for k in _GAIN_ROWS]
    rows.append(jnp.concatenate([vals["g_attn_out"].reshape(1, -1), vals["g_conv_out"].reshape(1, -1)], axis=1))
    flat = conv.reshape(1, -1)
    rows.append(jnp.pad(flat, ((0, 0), (0, 1024 - flat.shape[1]))))
    rows.append(jnp.zeros((1, 1024), F32))
    return jnp.concatenate(rows, axis=0)


def kernel(x, mem, g_mix, w_in, conv_w, g_attn_out, g_conv_out, w_out, g_xattn, g_mem, w_q_mem, w_kv_mem, w_o_mem, g_mlp, w_up, w_down, g_final, loss_target, m_g_mix, m_w_in, m_conv_w, m_g_attn_out, m_g_conv_out, m_w_out, m_g_xattn, m_g_mem, m_w_q_mem, m_w_kv_mem, m_w_o_mem, m_g_mlp, m_w_up, m_w_down, m_g_final, v_g_mix, v_w_in, v_conv_w, v_g_attn_out, v_g_conv_out, v_w_out, v_g_xattn, v_g_mem, v_w_q_mem, v_w_kv_mem, v_w_o_mem, v_g_mlp, v_w_up, v_w_down, v_g_final):
    d = x.shape[-1]
    me = 4 * lax.axis_index("x") + 2 * lax.axis_index("y") + lax.axis_index("c")
    w_shards = dict(w_in=w_in, w_out=w_out, w_q=w_q_mem, w_kv=w_kv_mem, w_o=w_o_mem, w_up=w_up, w_down=w_down)
    m_shards = dict(w_in=m_w_in, w_out=m_w_out, w_q=m_w_q_mem, w_kv=m_w_kv_mem, w_o=m_w_o_mem, w_up=m_w_up,
                    w_down=m_w_down)
    v_shards = dict(w_in=v_w_in, w_out=v_w_out, w_q=v_w_q_mem, w_kv=v_w_kv_mem, w_o=v_w_o_mem, w_up=v_w_up,
                    w_down=v_w_down)
    gains = dict(g_mix=g_mix, g_attn_out=g_attn_out, g_conv_out=g_conv_out, g_xattn=g_xattn, g_mem=g_mem,
                 g_mlp=g_mlp, g_final=g_final)
    gains2 = {k: v.reshape(1, -1) for k, v in gains.items()}

    shards = {k: w_shards[k].astype(BF16) for k in _BIG}
    shards["conv_w"] = conv_w
    grads = _Grads(distributed=True)
    grad_x = _local_step(x[0], mem[0], loss_target[0], gains2, _Weights({}, shards), grads)

    after = grads.send("small")
    outs = {}
    tiles = dict(w_in=256, w_out=128, w_q=128, w_kv=256, w_o=128, w_up=256, w_down=256)
    for group in (("w_down",), ("w_up",), ("w_o",), ("w_q", "w_kv", "w_out"), ("w_in",)):
        for k, received in zip(group, grads.wait(group[0], after)):
            outs[k] = _sum_adamw(received, w_shards[k], m_shards[k], v_shards[k], name=f"adamw_{k}", tr=tiles[k])
            after = [outs[k][0]]
    small_received, = grads.wait("small", after)

    ssum = _sum_small(small_received)
    row = lambda i: ssum[SMALL_PART * i]
    loss = ssum[SMALL_PART * 7, 0]
    g_small = {k: row(i) for i, k in enumerate(_GAIN_ROWS)}
    g_small["g_attn_out"] = row(5)[0:512]
    g_small["g_conv_out"] = row(5)[512:1024]
    taps = ssum[SMALL_PART * 6:SMALL_PART * 6 + 3, 0:512]
    g_conv = lax.dynamic_slice_in_dim(taps, me * 64, 64, axis=1)
    m_small = dict(g_mix=m_g_mix, g_attn_out=m_g_attn_out, g_conv_out=m_g_conv_out, g_xattn=m_g_xattn,
                   g_mem=m_g_mem, g_mlp=m_g_mlp, g_final=m_g_final)
    v_small = dict(g_mix=v_g_mix, g_attn_out=v_g_attn_out, g_conv_out=v_g_conv_out, g_xattn=v_g_xattn,
                   g_mem=v_g_mem, g_mlp=v_g_mlp, g_final=v_g_final)
    packed = [_pack_small(g_small, g_conv), _pack_small(gains, conv_w), _pack_small(m_small, m_conv_w),
              _pack_small(v_small, v_conv_w)]
    upd = _adamw_small(*packed)

    def unpack(p):
        res = {k: p[i] for i, k in enumerate(_GAIN_ROWS)}
        res["g_attn_out"] = p[5, 0:512]
        res["g_conv_out"] = p[5, 512:1024]
        res["conv_w"] = p[6, 0:192].reshape(3, 64)
        return res

    g_small["conv_w"] = g_conv
    small_out = [g_small] + [unpack(p) for p in upd]
    names = {"g_mix": "g_mix", "w_in": "w_in", "conv_w": "conv_w", "g_attn_out": "g_attn_out",
             "g_conv_out": "g_conv_out", "w_out": "w_out", "g_xattn": "g_xattn", "g_mem": "g_mem",
             "w_q_mem": "w_q", "w_kv_mem": "w_kv", "w_o_mem": "w_o", "g_mlp": "g_mlp", "w_up": "w_up",
             "w_down": "w_down", "g_final": "g_final"}
    result = [loss, grad_x[None]]
    for which in range(4):
        for key in names.values():
            result.append(outs[key][which] if key in outs else small_out[which][key])
    return tuple(result)
```

```python
import math

import jax
import jax.numpy as jnp
from jax import lax
from jax.experimental import pallas as pl
from jax.experimental.pallas import tpu as pltpu

F32 = jnp.float32
BF16 = jnp.bfloat16
NORM_EPS = 1e-6
NEG_INF = -1e30
N_DEV = 8
BLK = 128
HEAD_DIM = 64
N_MEM_HEADS = 4
ADAM_LR = 0.001
ADAM_B1 = 0.9
ADAM_B2 = 0.999
ADAM_EPS = 1e-08
ADAM_WD = 0.01
ADAM_STEP = 10
MESH = pl.DeviceIdType.MESH
ANY = pl.BlockSpec(memory_space=pl.ANY)


def _dot(a, b):
    return jnp.dot(a, b, preferred_element_type=F32)


def _dot_nt(a, b):
    return lax.dot_general(a, b, (((1,), (1,)), ((), ())), preferred_element_type=F32)


def _dot_tn(a, b):
    return lax.dot_general(a, b, (((0,), (0,)), ((), ())), preferred_element_type=F32)


def _params(semantics, vmem_mb):
    return pltpu.CompilerParams(dimension_semantics=semantics, vmem_limit_bytes=vmem_mb << 20)


def _rms_fwd(x, g):
    r = lax.rsqrt(jnp.mean(x * x, axis=-1, keepdims=True) + NORM_EPS)
    xh = x * r
    return xh * g, xh, r


def _rms_bwd(dy, xh, r, g):
    gy = dy * g
    return r * (gy - xh * jnp.mean(xh * gy, axis=-1, keepdims=True))


def _position():
    x, y, c = lax.axis_index("x"), lax.axis_index("y"), lax.axis_index("c")
    return x, y, c


def _block_of(ref, j, axis, shard_shape):
    r, c = shard_shape
    if axis is None:
        return ref.at[j]
    if axis == 0:
        return ref.at[pl.ds(j * r, r), :]
    return ref.at[:, pl.ds(j * c, c)]


class _Gather:
    has_mid = True
    alias_pairs = ()

    def __init__(self, shards, axes, late=False):
        self.arrays = list(shards)
        self.axes = list(axes)
        self.late = late
        self.n = len(self.arrays)

    def out_shape(self):
        res = []
        for s, axis in zip(self.arrays, self.axes):
            r, c = s.shape
            shape = (N_DEV, r, c) if axis is None else (N_DEV * r, c) if axis == 0 else (r, N_DEV * c)
            res.append(jax.ShapeDtypeStruct(shape, s.dtype))
        return res

    def scratch(self):
        return [pltpu.SemaphoreType.DMA((self.n, 7)), pltpu.SemaphoreType.DMA((self.n, 7)),
                pltpu.SemaphoreType.DMA((self.n,))]

    def _ctx(self, ins, outs, sems):
        send_sems, recv_sems, local_sems = sems
        x, y, c = _position()
        me, sibling = (x, y, c), (x, y, 1 - c)
        chips = [(1 - x, y), (x, 1 - y), (1 - x, 1 - y)]

        def lin(px, py, pc):
            return 4 * px + 2 * py + pc

        def place(a, block):
            return _block_of(outs[a], lin(*block), self.axes[a], self.arrays[a].shape)

        def copy(a, k, block, to, src=None):
            dst = place(a, block)
            return pltpu.make_async_remote_copy(
                src_ref=dst if src is None else src, dst_ref=dst,
                send_sem=send_sems.at[a, k], recv_sem=recv_sems.at[a, k],
                device_id=to, device_id_type=MESH)

        def mine():
            return [pltpu.make_async_copy(ins[a], place(a, me), local_sems.at[a]) for a in range(self.n)]

        def first():
            res = []
            for a in range(self.n):
                res.append(copy(a, 0, me, sibling, src=ins[a]))
                res += [copy(a, 1 + j, me, (*chip, c), src=ins[a]) for j, chip in enumerate(chips)]
            return res

        return c, me, sibling, chips, copy, mine, first

    def start(self, ins, outs, sems):
        _, _, _, _, _, mine, first = self._ctx(ins, outs, sems)
        for cp in mine() + first():
            cp.start()

    def mid(self, ins, outs, sems):
        c, me, sibling, chips, copy, _, _ = self._ctx(ins, outs, sems)
        for j, chip in enumerate(chips):
            for a in range(self.n):
                copy(a, 1 + j, (*chip, c), me).wait_recv()
                copy(a, 4 + j, (*chip, c), sibling).start()

    def finish(self, ins, outs, sems):
        c, me, sibling, chips, copy, mine, first = self._ctx(ins, outs, sems)
        for a in range(self.n):
            copy(a, 0, sibling, me).wait_recv()
            for j, chip in enumerate(chips):
                copy(a, 4 + j, (*chip, 1 - c), me).wait_recv()
        for cp in first():
            cp.wait_send()
        for j, chip in enumerate(chips):
            for a in range(self.n):
                copy(a, 4 + j, (*chip, c), sibling).wait_send()
        for cp in mine():
            cp.wait()


class _Exchange:
    def __init__(self, parts, axes):
        self.n = len(parts)
        self.axes = list(axes)
        self.arrays = list(parts)

    def _piece(self, a):
        r, c = self.arrays[a].shape
        axis = self.axes[a]
        return (r, c) if axis is None else (r // N_DEV, c) if axis == 0 else (r, c // N_DEV)

    def out_shape(self):
        return [jax.ShapeDtypeStruct((N_DEV,) + self._piece(a), self.arrays[a].dtype) for a in range(self.n)]

    def semaphores(self):
        return [pltpu.SemaphoreType.DMA((7 * self.n,)), pltpu.SemaphoreType.DMA((7 * self.n,)),
                pltpu.SemaphoreType.DMA((self.n,))]

    def _ctx(self, ins, outs, sems):
        send_sems, recv_sems, local_sems = sems
        x, y, c = _position()
        me = 4 * x + 2 * y + c

        def src(a, j):
            return ins[a] if self.axes[a] is None else _block_of(ins[a], j, self.axes[a], self._piece(a))

        def dst(a, j):
            return outs[a].at[j]

        def local():
            return [pltpu.make_async_copy(src(a, me), dst(a, me), local_sems.at[a]) for a in range(self.n)]

        def remote(inbound):
            res = []
            for a in range(self.n):
                for k in range(1, N_DEV):
                    peer = (1 - x if k & 4 else x, 1 - y if k & 2 else y, 1 - c if k & 1 else c)
                    plin = 4 * peer[0] + 2 * peer[1] + peer[2]
                    res.append(pltpu.make_async_remote_copy(
                        src_ref=src(a, plin), dst_ref=dst(a, plin if inbound else me),
                        send_sem=send_sems.at[7 * a + k - 1], recv_sem=recv_sems.at[7 * a + k - 1],
                        device_id=peer, device_id_type=MESH))
            return res

        return local, remote

    def start(self, ins, outs, sems):
        local, remote = self._ctx(ins, outs, sems)
        for cp in local() + remote(False):
            cp.start()

    def finish(self, ins, outs, sems):
        local, remote = self._ctx(ins, outs, sems)
        for cp in remote(True):
            cp.wait_recv()
        for cp in remote(False):
            cp.wait_send()
        for cp in local():
            cp.wait()


def _exchange_start(rider, name):
    n = rider.n
    parts = rider.arrays
    lands = [lax.empty(s.shape, s.dtype) for s in rider.out_shape()]
    hbm = pl.BlockSpec(memory_space=pltpu.HBM)
    sem = pl.BlockSpec(memory_space=pltpu.SEMAPHORE)

    def body(*refs):
        ins, sems = refs[:n], refs[2 * n:2 * n + 3]
        outs, token = refs[2 * n + 3 + n:2 * n + 3 + 2 * n], refs[-1]
        rider.start(ins, outs, sems)
        token[...] = jnp.zeros_like(token)

    res = pl.pallas_call(
        body, name=name,
        out_shape=rider.semaphores() + [pltpu.HBM(p.shape, p.dtype) for p in parts]
                  + [pltpu.HBM(z.shape, z.dtype) for z in lands] + [jax.ShapeDtypeStruct((8, 128), F32)],
        in_specs=[hbm] * (2 * n), out_specs=[sem] * 3 + [hbm] * (2 * n) + [pl.BlockSpec(memory_space=pltpu.VMEM)],
        input_output_aliases={i: 3 + i for i in range(2 * n)},
        compiler_params=pltpu.CompilerParams(has_side_effects=pltpu.SideEffectType.DATAFLOW_SIDE_EFFECTING),
    )(*[pltpu.with_memory_space_constraint(a, pltpu.HBM) for a in parts + lands])
    return res[:3], res[3:3 + n], res[3 + n:3 + 2 * n], res[-1]


def _exchange_wait(rider, started, after, name):
    n = rider.n
    sems, parts, lands, _ = started
    hbm = pl.BlockSpec(memory_space=pltpu.HBM)
    sem = pl.BlockSpec(memory_space=pltpu.SEMAPHORE)

    def body(*refs):
        rider.finish(refs[:n], refs[n:2 * n], refs[2 * n:2 * n + 3])

    res = pl.pallas_call(
        body, name=name, out_shape=[pltpu.HBM(a.shape, a.dtype) for a in list(parts) + list(lands)],
        in_specs=[hbm] * (2 * n) + [sem] * 3 + [ANY] * len(after), out_specs=[hbm] * (2 * n),
        input_output_aliases={i: i for i in range(2 * n)},
        compiler_params=pltpu.CompilerParams(has_side_effects=pltpu.SideEffectType.DATAFLOW_SIDE_EFFECTING),
    )(*parts, *lands, *sems, *after)
    return list(res[n:])


def _pcall(body, *, name, grid, in_specs, out_specs, out_shape, scratch_shapes=(), semantics, vmem_mb, rider=None,
           aliases=None, after=()):
    in_specs, out_specs, out_shape = list(in_specs), list(out_specs), list(out_shape)
    scratch_shapes = list(scratch_shapes)
    aliases = dict(aliases or {})
    if rider is None:
        n_in, after = len(in_specs), list(after)

        def plain(*refs):
            body(*refs[:n_in], *refs[n_in + len(after):])

        call = pl.pallas_call(plain if after else body, name=name, grid=grid, in_specs=in_specs + [ANY] * len(after),
                              out_specs=out_specs, out_shape=out_shape, scratch_shapes=scratch_shapes,
                              input_output_aliases=aliases, compiler_params=_params(semantics, vmem_mb))
        return lambda *args: (list(call(*args, *after)), None)
    n_in, n_out, n_scr = len(in_specs), len(out_specs), len(scratch_shapes)
    r_in, r_shapes = len(rider.arrays), rider.out_shape()
    r_out = len(r_shapes)
    aliases.update({n_in + i: n_out + o for i, o in rider.alias_pairs})
    total = math.prod(grid)
    mid_step = total - 1 if rider.has_mid and rider.late else (3 * total) // 4

    def wrapped(*refs):
        bounds = [0, n_in, r_in, n_out, r_out, n_scr]
        for i in range(1, len(bounds)):
            bounds[i] += bounds[i - 1]
        a, ra, o, ro, s = (refs[bounds[i]:bounds[i + 1]] for i in range(5))
        rs = refs[bounds[5]:]
        step = pl.program_id(0)
        for k in range(1, len(grid)):
            step = step * grid[k] + pl.program_id(k)
        pl.when(step == 0)(lambda: rider.start(ra, ro, rs))
        body(*a, *o, *s)
        if rider.has_mid:
            pl.when(step == mid_step)(lambda: rider.mid(ra, ro, rs))
        pl.when(step == total - 1)(lambda: rider.finish(ra, ro, rs))

    call = pl.pallas_call(
        wrapped, name=name, grid=grid, in_specs=in_specs + [ANY] * r_in, out_specs=out_specs + [ANY] * r_out,
        out_shape=out_shape + r_shapes, scratch_shapes=scratch_shapes + rider.scratch(),
        input_output_aliases=aliases, compiler_params=_params(("arbitrary",) * len(grid), vmem_mb))

    def run(*args):
        res = call(*args, *rider.arrays)
        return list(res[:n_out]), list(res[n_out:])

    return run


def _norm_matmul(x, g, w, *, name, out_dtype, tb, bn, relu=False, save_h=False, rider=None):
    t, d = x.shape
    n = w.shape[1]

    def body(x_ref, g_ref, w_ref, o_ref, *rest):
        h_scr = rest[-1]

        @pl.when(pl.program_id(1) == 0)
        def _():
            h = _rms_fwd(x_ref[...], g_ref[...])[0].astype(BF16)
            h_scr[...] = h
            if save_h:
                rest[0][...] = h

        acc = _dot(h_scr[...], w_ref[...])
        if relu:
            acc = jnp.maximum(acc, 0.0)
        o_ref[...] = acc.astype(out_dtype)

    out_shape = [jax.ShapeDtypeStruct((t, n), out_dtype)]
    out_specs = [pl.BlockSpec((tb, bn), lambda i, j: (i, j))]
    if save_h:
        out_shape.append(jax.ShapeDtypeStruct((t, d), BF16))
        out_specs.append(pl.BlockSpec((tb, d), lambda i, j: (i, 0)))
    res, extra = _pcall(
        body, name=name, grid=(t // tb, n // bn),
        in_specs=[pl.BlockSpec((tb, d), lambda i, j: (i, 0)),
                  pl.BlockSpec((1, d), lambda i, j: (0, 0)),
                  pl.BlockSpec((d, bn), lambda i, j: (0, j))],
        out_specs=out_specs, out_shape=out_shape,
        scratch_shapes=[pltpu.VMEM((tb, d), BF16)],
        semantics=("parallel", "arbitrary"), vmem_mb=48, rider=rider,
    )(x, g, w)
    res = res if save_h else res[0]
    return res if rider is None else (res, extra)


def _matmul_nt_normbwd(dy, w, x, g, dres, *, name, tb, also_bf16=False, to_natural=False, after=()):
    t, d = x.shape
    stacked = dy.ndim == 3
    has_res = dres is not None
    n_i = SEG // TI
    if to_natural:
        tb = N_RES * TI

    def body(dy_ref, w_ref, x_ref, g_ref, *rest):
        rest = list(rest)
        dres_ref = rest.pop(0) if has_res else None
        dx_ref = rest.pop(0)
        dxb_ref = rest.pop(0) if also_bf16 else None
        gg_ref = rest.pop(0)
        i = pl.program_id(0)

        def rows(ref, *lead):
            v = ref[lead] if lead else ref[...]
            return v[0].reshape(tb, v.shape[-1]) if to_natural else v

        if stacked:
            kb = dy_ref.shape[-1]
            dh = _dot_nt(rows(dy_ref, 0), w_ref[:, 0:kb])
            for s in range(1, dy_ref.shape[0]):
                dh = dh + _dot_nt(rows(dy_ref, s), w_ref[:, s * kb:(s + 1) * kb])
        else:
            dh = _dot_nt(rows(dy_ref), w_ref[...])
        g_v = g_ref[...]
        _, xh, r = _rms_fwd(rows(x_ref), g_v)
        dx = _rms_bwd(dh, xh, r, g_v)
        if has_res:
            dx = dx + rows(dres_ref)
        if to_natural:
            scr = rest.pop(0)
            for cb in range(d // BLK):
                cols = slice(cb * BLK, (cb + 1) * BLK)
                slab = scr.at[cb]
                for res in range(N_RES):
                    slab[pl.ds(res, TI, stride=N_RES), :] = dx[res * TI:(res + 1) * TI, cols]
                dx_ref[:, cols] = slab[...]
        else:
            dx_ref[...] = dx
        if also_bf16:
            dxb_ref[...] = dx.astype(BF16)
        part = jnp.sum(dh * xh, axis=0, keepdims=True)

        @pl.when(i == 0)
        def _():
            gg_ref[...] = part

        @pl.when(i != 0)
        def _():
            gg_ref[...] += part

    tok = pl.BlockSpec((tb, d), lambda i: (i, 0))
    row = pl.BlockSpec((1, d), lambda i: (0, 0))
    if to_natural:
        act = pl.BlockSpec((1, N_RES, TI, d), lambda i: (i // n_i, 0, i % n_i, 0))
        dy_spec = pl.BlockSpec((dy.shape[0], 1, N_RES, TI, dy.shape[2]), lambda i: (0, i // n_i, 0, i % n_i, 0))
        dy, x = dy.reshape(dy.shape[0], t // HALF, N_RES, SEG, dy.shape[2]), _x4(x)
        dres = _x4(dres) if has_res else None
    elif stacked:
        act, dy_spec = tok, pl.BlockSpec((dy.shape[0], tb, dy.shape[2]), lambda i: (0, i, 0))
    else:
        act, dy_spec = tok, pl.BlockSpec((tb, dy.shape[1]), lambda i: (i, 0))
    in_specs = [dy_spec, pl.BlockSpec(w.shape, lambda i: (0, 0)), act, row]
    args = [dy, w, x, g]
    if has_res:
        in_specs.append(act)
        args.append(dres)
    out_specs = [tok] + ([tok] if also_bf16 else []) + [row]
    out_shape = ([jax.ShapeDtypeStruct((t, d), F32)] + ([jax.ShapeDtypeStruct((t, d), BF16)] if also_bf16 else [])
                 + [jax.ShapeDtypeStruct((1, d), F32)])
    res, _ = _pcall(
        body, name=name, grid=(t // tb,), in_specs=in_specs, out_specs=out_specs, out_shape=out_shape,
        scratch_shapes=[pltpu.VMEM((d // BLK, tb, BLK), F32)] if to_natural else [],
        semantics=("arbitrary",), vmem_mb=56, after=after,
    )(*args)
    return res


def _matmul_tn(a, b, *, name, bm, bn, square_a=False, after=()):
    t, m = a.shape
    stacked = b.ndim == 3
    n = b.shape[0] * bn if stacked else b.shape[1]

    def body(a_ref, b_ref, o_ref):
        av = a_ref[...]
        if square_a:
            av = av.astype(F32)
            av = (av * av).astype(BF16)
        o_ref[...] = _dot_tn(av, b_ref[...]).astype(BF16)

    res, _ = _pcall(
        body, name=name, grid=(m // bm, n // bn),
        in_specs=[pl.BlockSpec((t, bm), lambda i, j: (0, i)),
                  pl.BlockSpec((None, t, bn), lambda i, j: (j, 0, 0)) if stacked
                  else pl.BlockSpec((t, bn), lambda i, j: (0, j))],
        out_specs=[pl.BlockSpec((bm, bn), lambda i, j: (i, j))], out_shape=[jax.ShapeDtypeStruct((m, n), BF16)],
        semantics=("parallel", "parallel"), vmem_mb=56, after=after,
    )(a, b)
    return res[0]


N_RES = 16
SEG = 128
HALF = N_RES * SEG
TI = 32


def _x4(a):
    return a.reshape(a.shape[0] // HALF, N_RES, SEG, a.shape[1])


def _reorder(arrays, name, rider=None):
    t, c = arrays[0].shape
    n = len(arrays)
    n_i = SEG // TI

    def body(*refs):
        scr = refs[-1]
        for i_ref, o_ref in zip(refs[:n], refs[n:2 * n]):
            for cb in range(c // BLK):
                cols = slice(cb * BLK, (cb + 1) * BLK)
                slab = scr.at[cb]
                slab[...] = i_ref[:, cols]
                for r in range(N_RES):
                    o_ref[0, r, :, cols] = slab[pl.ds(r, TI, stride=N_RES), :]

    res, extra = _pcall(
        body, name=name, grid=(t // (TI * N_RES),),
        in_specs=[pl.BlockSpec((TI * N_RES, c), lambda s: (s, 0))] * n,
        out_specs=[pl.BlockSpec((1, N_RES, TI, c), lambda s: (s // n_i, 0, s % n_i, 0))] * n,
        out_shape=[jax.ShapeDtypeStruct((t // HALF, N_RES, SEG, c), F32)] * n,
        scratch_shapes=[pltpu.VMEM((c // BLK, TI * N_RES, BLK), F32)],
        semantics=("parallel",), vmem_mb=32, rider=rider,
    )(*arrays)
    res = [r.reshape(t, c) for r in res]
    return res if rider is None else (res, extra)


_PATTERNS = ((1, 16, 8, SEG), (4, 4, 32, 4 * SEG), (16, 1, SEG, 0))
_FIRST = {1: 1, 4: 4, 16: 16}


def _group_rows(d, g):
    a = g >> 4
    if d == 16:
        base = a * HALF + (g & 15) * SEG
        prev = base - HALF
    elif d == 4:
        c = (g >> 2) & 3
        base = a * HALF + (g & 3) * SEG + c * 32
        prev = jnp.where(c > 0, base - 32, base - HALF + 96)
    else:
        c = g & 15
        base = a * HALF + c * 8
        prev = jnp.where(c > 0, base - 8, base - HALF + 120)
    return base, prev


def _load_rows(ref, base, n, rows, stride):
    parts = [ref[pl.ds(pl.multiple_of(base + j * stride, 8), rows), :] for j in range(n)]
    return parts[0] if n == 1 else jnp.concatenate(parts, axis=0)


def _store_rows(ref, base, val, n, rows, stride, add=False):
    for j in range(n):
        sl = pl.ds(pl.multiple_of(base + j * stride, 8), rows)
        piece = val[j * rows:(j + 1) * rows, :]
        if add:
            ref[sl, :] += piece
        else:
            ref[sl, :] = piece


def _band_bias(n, rows):
    shift = rows.bit_length() - 1
    lq = lax.broadcasted_iota(jnp.int32, (BLK, BLK), 0)
    lk = lax.broadcasted_iota(jnp.int32, (BLK, BLK), 1)
    iq = (lq & (rows - 1)) * n + (lq >> shift)
    ik = (lk & (rows - 1)) * n + (lk >> shift)
    zero = jnp.zeros((BLK, BLK), F32)
    return jnp.where(ik >= iq, zero, NEG_INF), jnp.where(ik <= iq, zero, NEG_INF)


def _set_bias(bias_scr, n, rows):
    prev_b, cur_b = _band_bias(n, rows)
    for half in range(2):
        bias_scr[half * BLK:(half + 1) * BLK, 0:BLK] = prev_b
        bias_scr[half * BLK:(half + 1) * BLK, BLK:2 * BLK] = cur_b


SCALE = 1.0 / math.sqrt(HEAD_DIM)


def _head_consts(value=1.0):
    lane_lo = lax.broadcasted_iota(jnp.int32, (BLK, BLK), 1) < HEAD_DIM
    return lane_lo, [jnp.where(lane_lo, value, 0.0).astype(BF16), jnp.where(lane_lo, 0.0, value).astype(BF16)]


def _stack_heads(v, head_mask):
    return jnp.concatenate([v * head_mask[0], v * head_mask[1]], axis=0)


def _unstack_heads(v2, lane_lo):
    return jnp.where(lane_lo, v2[:BLK], v2[BLK:])


def _rows_per_head(v, lane_lo):
    rolled = pltpu.roll(v, HEAD_DIM, axis=1)
    return jnp.concatenate([jnp.where(lane_lo, v, rolled), jnp.where(lane_lo, rolled, v)], axis=0)


WIDTH = 4


def _loop(lo, hi, fn, width=None):
    if width is None:
        def body(g, carry):
            fn(g)
            return carry

        if hi > lo:
            lax.fori_loop(lo, hi, body, 0)
        return
    while hi > lo:
        trips = (hi - lo) // width
        if trips:
            def body(i, carry, lo=lo, width=width):
                fn([lo + width * i + j for j in range(width)])
                return carry

            lax.fori_loop(0, trips, body, 0)
            lo += trips * width
        width = max(1, width // 2)


def _mix_weights(l1, l2, l3):
    mx = jnp.maximum(jnp.maximum(l1, l2), l3)
    e1, e2, e3 = jnp.exp(l1 - mx), jnp.exp(l2 - mx), jnp.exp(l3 - mx)
    inv = 1.0 / (e1 + e2 + e3)
    return e1 * inv, e2 * inv, e3 * inv


def _attention_fwd(qkv, rider=None):
    t = qkv.shape[0]
    groups = 16 * (t // HALF)

    def body(q_ref, k_ref, v_ref, attn_ref, l1_ref, l2_ref, l3_ref, o_scr, bias_scr):
        lane_lo, q_mask = _head_consts(SCALE)
        l_refs = (l1_ref, l2_ref, l3_ref)
        for p, (d, n, rows, stride) in enumerate(_PATTERNS):
            _set_bias(bias_scr, n, rows)
            o_p, l_p = o_scr.at[p], l_refs[p]

            def block(gs, has_prev):
                at = [_group_rows(d, g) for g in gs]

                def load(ref, b):
                    return _load_rows(ref, b, n, rows, stride).astype(BF16)

                q2 = [_stack_heads(load(q_ref, b), q_mask) for b, _ in at]
                k2 = [load(k_ref, b) for b, _ in at]
                v2 = [load(v_ref, b) for b, _ in at]
                if has_prev:
                    k2 = [jnp.concatenate([load(k_ref, pv), k], axis=0) for (_, pv), k in zip(at, k2)]
                    v2 = [jnp.concatenate([load(v_ref, pv), v], axis=0) for (_, pv), v in zip(at, v2)]
                s = [_dot_nt(q, k) for q, k in zip(q2, k2)]
                s = [x + (bias_scr[...] if has_prev else bias_scr[:, BLK:2 * BLK]) for x in s]
                mx = [jnp.max(x, axis=1, keepdims=True) for x in s]
                e = [jnp.exp(x - m) for x, m in zip(s, mx)]
                den = [jnp.sum(x, axis=1, keepdims=True) for x in e]
                o2 = [_dot(x.astype(BF16), v) * (1.0 / dn) for x, v, dn in zip(e, v2, den)]
                lse2 = [jnp.broadcast_to(m + jnp.log(dn), (2 * BLK, BLK)) for m, dn in zip(mx, den)]
                for (b, _), o, l in zip(at, o2, lse2):
                    _store_rows(o_p, b, _unstack_heads(o, lane_lo), n, rows, stride)
                    _store_rows(l_p, b, _unstack_heads(l, lane_lo), n, rows, stride)

            _loop(0, _FIRST[d], lambda gs: block(gs, False), width=WIDTH)
            _loop(_FIRST[d], groups, lambda gs: block(gs, True), width=WIDTH)

        def mix(i):
            sl = pl.ds(pl.multiple_of(i * 256, 256), 256)
            w = _mix_weights(l1_ref[sl, :], l2_ref[sl, :], l3_ref[sl, :])
            attn_ref[sl, :] = w[0] * o_scr[0, sl, :] + w[1] * o_scr[1, sl, :] + w[2] * o_scr[2, sl, :]

        _loop(0, t // 256, mix)

    def col(c0):
        return pl.BlockSpec((t, BLK), lambda hp: (0, c0 + hp))

    res, extra = _pcall(
        body, name="attention_fwd", grid=(4,), in_specs=[col(0), col(4), col(8)], out_specs=[col(0)] * 4,
        out_shape=[jax.ShapeDtypeStruct((t, 512), F32)] * 4,
        scratch_shapes=[pltpu.VMEM((3, t, BLK), F32), pltpu.VMEM((2 * BLK, 2 * BLK), F32)],
        semantics=("parallel",), vmem_mb=48, rider=rider,
    )(qkv, qkv, qkv)
    return res if rider is None else (res, extra)


def _attention_bwd(qkv, dattn, dsum, lses, dproj):
    t = qkv.shape[0]
    groups = 16 * (t // HALF)

    def body(q_ref, k_ref, v_ref, da_ref, ds_ref, l1_ref, l2_ref, l3_ref, kept_ref, out_ref, acc, bias_scr):
        del kept_ref
        lane_lo, head_mask = _head_consts()
        q_mask = _head_consts(SCALE)[1]
        l_refs = (l1_ref, l2_ref, l3_ref)

        def clear(i):
            sl = pl.ds(pl.multiple_of(i * 512, 512), 512)
            for s in range(3):
                acc[s, sl, :] = jnp.zeros((512, BLK), F32)

        _loop(0, t // 512, clear)
        dq_acc, dk_acc, dv_acc = acc.at[0], acc.at[1], acc.at[2]
        for p, (d, n, rows, stride) in enumerate(_PATTERNS):
            _set_bias(bias_scr, n, rows)

            def block(gs, has_prev):
                at = [_group_rows(d, g) for g in gs]

                def load(ref, b):
                    return _load_rows(ref, b, n, rows, stride)

                def put(ref, b, val):
                    _store_rows(ref, b, val, n, rows, stride, add=True)

                def wide(x):
                    return jnp.concatenate([x, x], axis=1) if has_prev else x

                lse = [[load(ref, b) for ref in l_refs] for b, _ in at]
                w = [_mix_weights(*ls)[p] for ls in lse]
                do2 = [_stack_heads((wg * load(da_ref, b)).astype(BF16), head_mask) for wg, (b, _) in zip(w, at)]
                dl2 = [wide(_rows_per_head(wg * load(ds_ref, b), lane_lo)) for wg, (b, _) in zip(w, at)]
                lse2 = [wide(_rows_per_head(ls[p], lane_lo)) for ls in lse]
                q2 = [_stack_heads(load(q_ref, b).astype(BF16), q_mask) for b, _ in at]
                k2 = [load(k_ref, b).astype(BF16) for b, _ in at]
                v2 = [load(v_ref, b).astype(BF16) for b, _ in at]
                if has_prev:
                    k2 = [jnp.concatenate([load(k_ref, pv).astype(BF16), k], axis=0) for (_, pv), k in zip(at, k2)]
                    v2 = [jnp.concatenate([load(v_ref, pv).astype(BF16), v], axis=0) for (_, pv), v in zip(at, v2)]
                s = [_dot_nt(q, k) for q, k in zip(q2, k2)]
                dp = [_dot_nt(do, v) for do, v in zip(do2, v2)]
                pr = [jnp.exp(x + (bias_scr[...] if has_prev else bias_scr[:, BLK:2 * BLK]) - l)
                      for x, l in zip(s, lse2)]
                ds = [(pg * (x - dl)).astype(BF16) for pg, x, dl in zip(pr, dp, dl2)]
                dq2 = [_dot(x, k) * SCALE for x, k in zip(ds, k2)]
                dk2 = [_dot_tn(x, q) for x, q in zip(ds, q2)]
                dv2 = [_dot_tn(pg.astype(BF16), do) for pg, do in zip(pr, do2)]
                for (b, pv), dq, dk, dv in zip(at, dq2, dk2, dv2):
                    put(dq_acc, b, _unstack_heads(dq, lane_lo))
                    if has_prev:
                        put(dk_acc, pv, dk[:BLK])
                        put(dv_acc, pv, dv[:BLK])
                        put(dk_acc, b, dk[BLK:])
                        put(dv_acc, b, dv[BLK:])
                    else:
                        put(dk_acc, b, dk)
                        put(dv_acc, b, dv)

            _loop(0, _FIRST[d], lambda gs: block(gs, False), width=WIDTH)
            _loop(_FIRST[d], groups, lambda gs: block(gs, True), width=WIDTH)

        def emit(i):
            sl = pl.ds(pl.multiple_of(i * 512, 512), 512)
            for s in range(3):
                out_ref[s, sl, :] = acc[s, sl, :].astype(BF16)

        _loop(0, t // 512, emit)

    def col(c0):
        return pl.BlockSpec((t, BLK), lambda hp: (0, c0 + hp))

    res, _ = _pcall(
        body, name="attention_bwd", grid=(4,),
        in_specs=[col(0), col(4), col(8)] + [col(0)] * 5 + [ANY],
        out_specs=[pl.BlockSpec((3, t, BLK), lambda hp: (0, 0, hp))],
        out_shape=[jax.ShapeDtypeStruct(dproj.shape, BF16)],
        scratch_shapes=[pltpu.VMEM((3, t, BLK), F32), pltpu.VMEM((2 * BLK, 2 * BLK), F32)],
        semantics=("parallel",), vmem_mb=56, aliases={8: 0},
    )(qkv, qkv, qkv, dattn, dsum, *lses, dproj)
    return res[0]


def _order_specs(t):
    n_i = SEG // TI
    nblk = (t // HALF) * n_i
    per = TI // 8

    def main(c, col=0):
        return pl.BlockSpec((1, N_RES, TI, c), lambda s: (s // n_i, 0, s % n_i, col))

    def before(c, col=0):
        return pl.BlockSpec((1, 2, 8, c), lambda s: (jnp.maximum(s - 1, 0) // n_i, N_RES // 2 - 1,
                                                     (jnp.maximum(s - 1, 0) % n_i) * per + per - 1, col))

    def after(c, col=0):
        return pl.BlockSpec((1, 2, 8, c), lambda s: (jnp.minimum(s + 1, nblk - 1) // n_i, 0,
                                                     (jnp.minimum(s + 1, nblk - 1) % n_i) * per, col))

    return nblk, main, before, after


def _shift_in(v, row_in, up):
    rows = v.shape[0]
    idx = lax.broadcasted_iota(jnp.int32, v.shape, 0)
    fill = jnp.broadcast_to(row_in, v.shape)
    if up:
        return jnp.where(idx == rows - 1, fill, pltpu.roll(v, rows - 1, axis=0))
    return jnp.where(idx == 0, fill, pltpu.roll(v, 1, axis=0))


def _taps_behind(u, before):
    s15 = _shift_in(u[N_RES - 1], before[1, 7:8, :], up=False)
    s14 = _shift_in(u[N_RES - 2], before[0, 7:8, :], up=False)
    m1 = jnp.concatenate([s15[None], u[:N_RES - 1]], axis=0)
    m2 = jnp.concatenate([s14[None], s15[None], u[:N_RES - 2]], axis=0)
    return m1, m2


def _taps_ahead(u, after):
    t0 = _shift_in(u[0], after[0, 0:1, :], up=True)
    t1 = _shift_in(u[1], after[1, 0:1, :], up=True)
    p1 = jnp.concatenate([u[1:], t0[None]], axis=0)
    p2 = jnp.concatenate([u[2:], t0[None], t1[None]], axis=0)
    return p1, p2


def _conv_fwd(gates, before, first, cw):
    bg, cg, xc = gates[..., 0:512], gates[..., 512:1024], gates[..., 1024:1536]
    u = cg * xc
    ub = before[..., 512:1024] * before[..., 1024:1536]
    ub = jnp.where(first, jnp.zeros_like(ub), ub)
    m1, m2 = _taps_behind(u, ub)
    conv = m2 * cw[0:1, :] + m1 * cw[1:2, :] + u * cw[2:3, :]
    return bg, u, m1, m2, conv


def _sum_tokens(v):
    return jnp.sum(jnp.sum(v, axis=0), axis=0, keepdims=True)


def _mixer_fwd(x, attn, gates, cw, g_a, g_c, w_out):
    t, d = x.shape
    nblk, main, before, _ = _order_specs(t)
    rows = N_RES * TI

    def body(x_ref, at_ref, gt_ref, gb_ref, cw_ref, ga_ref, gc_ref, wa_ref, wb_ref, x1_ref, mg_ref):
        an = _rms_fwd(at_ref[0], ga_ref[...])[0].astype(BF16)
        bg, _, _, _, conv = _conv_fwd(gt_ref[0], gb_ref[0], pl.program_id(0) == 0, cw_ref[...])
        cn = _rms_fwd(bg * conv, gc_ref[...])[0].astype(BF16)
        mg_ref[0, :, :, 0:512] = an
        mg_ref[0, :, :, 512:1024] = cn
        y = _dot(an.reshape(rows, 512), wa_ref[...]) + _dot(cn.reshape(rows, 512), wb_ref[...])
        x1_ref[0] = x_ref[0] + y.reshape(N_RES, TI, d)

    const = lambda r, c, i0=0: pl.BlockSpec((r, c), lambda s: (i0, 0))
    x1, merged = pl.pallas_call(
        body, name="mixer_fwd", grid=(nblk,),
        in_specs=[main(d), main(512), main(1536, 1), before(1536, 1), const(3, 512), const(1, 512), const(1, 512),
                  const(512, d), const(512, d, 1)],
        out_specs=[main(d), main(d)],
        out_shape=[jax.ShapeDtypeStruct(_x4(x).shape, F32), jax.ShapeDtypeStruct(_x4(x).shape, BF16)],
        compiler_params=_params(("parallel",), 48),
    )(_x4(x), _x4(attn), _x4(gates), _x4(gates), cw, g_a, g_c, w_out, w_out)
    return x1.reshape(t, d), merged.reshape(t, d)


def _mixer_bwd(dx1, attn, gates, cw, g_a, g_c, w_out, head_sum, after=()):
    t, d = dx1.shape
    nblk, main, before, _ = _order_specs(t)
    rows = N_RES * TI

    def body(dx_ref, at_ref, gt_ref, gb_ref, cw_ref, ga_ref, gc_ref, wa_ref, wb_ref, hs_ref,
             da_ref, dsum_ref, dy_ref, gga_ref, ggc_ref):
        s = pl.program_id(0)
        dxb = dx_ref[0].reshape(rows, d).astype(BF16)
        dma = _dot_nt(dxb, wa_ref[...]).reshape(N_RES, TI, 512)
        dmc = _dot_nt(dxb, wb_ref[...]).reshape(N_RES, TI, 512)
        attn_v, g_av = at_ref[0], ga_ref[...]
        _, ah, ra = _rms_fwd(attn_v, g_av)
        dattn = _rms_bwd(dma, ah, ra, g_av)
        da_ref[0] = dattn
        z = (dattn * attn_v).reshape(rows, 512)
        hs = hs_ref[...]
        z1 = z.astype(BF16)
        z2 = (z - z1.astype(F32)).astype(BF16)
        dsum_ref[0] = (_dot(z1, hs) + _dot(z2, hs)).reshape(N_RES, TI, 512)
        bg, _, _, _, conv = _conv_fwd(gt_ref[0], gb_ref[0], s == 0, cw_ref[...])
        g_cv = gc_ref[...]
        _, yh, rc = _rms_fwd(bg * conv, g_cv)
        dy_ref[0] = _rms_bwd(dmc, yh, rc, g_cv)
        pa, pc = _sum_tokens(dma * ah), _sum_tokens(dmc * yh)

        @pl.when(s == 0)
        def _():
            gga_ref[...] = pa
            ggc_ref[...] = pc

        @pl.when(s != 0)
        def _():
            gga_ref[...] += pa
            ggc_ref[...] += pc

    const = lambda r, c, i0=0: pl.BlockSpec((r, c), lambda s: (i0, 0))
    shape4 = _x4(attn).shape
    res, _ = _pcall(
        body, name="mixer_bwd", grid=(nblk,),
        in_specs=[main(d), main(512), main(1536, 1), before(1536, 1), const(3, 512), const(1, 512), const(1, 512),
                  const(512, d), const(512, d, 1), const(512, 512)],
        out_specs=[main(512)] * 3 + [const(1, 512), const(1, 512)],
        out_shape=[jax.ShapeDtypeStruct(shape4, F32)] * 3 + [jax.ShapeDtypeStruct((1, 512), F32)] * 2,
        semantics=("arbitrary",), vmem_mb=48, after=after,
    )(_x4(dx1), _x4(attn), _x4(gates), _x4(gates), cw, g_a, g_c, w_out, w_out, head_sum)
    return [r.reshape(t, 512) for r in res[:3]] + res[3:]


def _conv_bwd(dy, gates, cw):
    t = dy.shape[0]
    nblk, main, before, after = _order_specs(t)
    n_i = SEG // TI

    def body(dy_ref, dya_ref, gt_ref, gb_ref, ga_ref, cw_ref, dp_ref, gcw_ref):
        s = pl.program_id(0)
        cw_v, gates_v = cw_ref[...], gt_ref[0]
        bg, u, m1, m2, conv = _conv_fwd(gates_v, gb_ref[0], s == 0, cw_v)
        dy_v = dy_ref[0]
        dconv = dy_v * bg
        dca = dya_ref[0] * ga_ref[0][..., 0:512]
        dca = jnp.where(s == nblk - 1, jnp.zeros_like(dca), dca)
        p1, p2 = _taps_ahead(dconv, dca)
        du = dconv * cw_v[2:3, :] + p1 * cw_v[1:2, :] + p2 * cw_v[0:1, :]
        dp_ref[0, 0] = (dy_v * conv).astype(BF16)
        dp_ref[1, 0] = (du * gates_v[..., 1024:1536]).astype(BF16)
        dp_ref[2, 0] = (du * gates_v[..., 512:1024]).astype(BF16)
        parts = [_sum_tokens(dconv * m2), _sum_tokens(dconv * m1), _sum_tokens(dconv * u)]

        @pl.when(s == 0)
        def _():
            gcw_ref[...] = jnp.zeros_like(gcw_ref)

        for tap in range(3):
            gcw_ref[tap:tap + 1, :] += parts[tap]

    (dproj, gcw), _ = _pcall(
        body, name="conv_bwd", grid=(nblk,),
        in_specs=[main(512), after(512), main(1536, 1), before(1536, 1), after(1536, 1),
                  pl.BlockSpec((3, 512), lambda s: (0, 0))],
        out_specs=[pl.BlockSpec((3, 1, N_RES, TI, 512), lambda s: (1, s // n_i, 0, s % n_i, 0)),
                   pl.BlockSpec((8, 512), lambda s: (0, 0))],
        out_shape=[jax.ShapeDtypeStruct((6, t // HALF, N_RES, SEG, 512), BF16), jax.ShapeDtypeStruct((8, 512), F32)],
        semantics=("arbitrary",), vmem_mb=40,
    )(_x4(dy), _x4(dy), _x4(gates), _x4(gates), _x4(gates), cw)
    return dproj.reshape(6, t, 512), gcw


def _xattn_fwd(x1, g, w_q, kv, w_o, *, tb):
    t, d = x1.shape
    hd = d // N_MEM_HEADS
    m = kv.shape[0]

    def body(x_ref, g_ref, wq_ref, k_ref, v_ref, wo_ref, x2_ref, h_ref, q_ref, o_ref):
        xv = x_ref[...]
        h = _rms_fwd(xv, g_ref[...])[0].astype(BF16)
        h_ref[...] = h
        q = _dot(h, wq_ref[...]).astype(BF16)
        q_ref[...] = q
        for hh in range(N_MEM_HEADS):
            sl = slice(hh * hd, (hh + 1) * hd)
            s = _dot_nt(q[:, sl], k_ref[:, sl]) * (1.0 / 16.0)
            e = jnp.exp(s - jnp.max(s, axis=1, keepdims=True))
            p = e / jnp.sum(e, axis=1, keepdims=True)
            o_ref[:, sl] = _dot(p.astype(BF16), v_ref[:, sl]).astype(BF16)
        x2_ref[...] = xv + _dot(o_ref[...], wo_ref[...])

    tok = pl.BlockSpec((tb, d), lambda i: (i, 0))
    full = pl.BlockSpec((d, d), lambda i: (0, 0))
    return pl.pallas_call(
        body, name="xattn_fwd", grid=(t // tb,),
        in_specs=[tok, pl.BlockSpec((1, d), lambda i: (0, 0)), full,
                  pl.BlockSpec((m, d), lambda i: (0, 0)), pl.BlockSpec((m, d), lambda i: (0, 1)), full],
        out_specs=[tok] * 4,
        out_shape=[jax.ShapeDtypeStruct((t, d), F32)] + [jax.ShapeDtypeStruct((t, d), BF16)] * 3,
        compiler_params=_params(("parallel",), 48),
    )(x1, g, w_q, kv, kv, w_o)


def _xattn_bwd(dx2, x1, g, q, w_q, kv, w_o, *, tb, after=()):
    t, d = x1.shape
    hd = d // N_MEM_HEADS
    m = kv.shape[0]

    def body(dx2_ref, x_ref, g_ref, q_ref, wq_ref, k_ref, v_ref, wo_ref,
             dx1_ref, dx1b_ref, dq_ref, dk_ref, dv_ref, gg_ref):
        i = pl.program_id(0)

        @pl.when(i == 0)
        def _():
            dk_ref[...] = jnp.zeros_like(dk_ref)
            dv_ref[...] = jnp.zeros_like(dv_ref)

        dx2 = dx2_ref[...]
        do = _dot_nt(dx2.astype(BF16), wo_ref[...]).astype(BF16)
        for hh in range(N_MEM_HEADS):
            sl = slice(hh * hd, (hh + 1) * hd)
            qh, kh, vh, doh = q_ref[:, sl], k_ref[:, sl], v_ref[:, sl], do[:, sl]
            s = _dot_nt(qh, kh) * (1.0 / 16.0)
            e = jnp.exp(s - jnp.max(s, axis=1, keepdims=True))
            p = e / jnp.sum(e, axis=1, keepdims=True)
            dp = _dot_nt(doh, vh)
            ds = (p * (dp - jnp.sum(dp * p, axis=1, keepdims=True)) * (1.0 / 16.0)).astype(BF16)
            dq_ref[:, sl] = _dot(ds, kh).astype(BF16)
            dk_ref[:, sl] += _dot_tn(ds, qh)
            dv_ref[:, sl] += _dot_tn(p.astype(BF16), doh)
        dh = _dot_nt(dq_ref[...], wq_ref[...])
        g_v = g_ref[...]
        _, xh, r = _rms_fwd(x_ref[...], g_v)
        dx1 = dx2 + _rms_bwd(dh, xh, r, g_v)
        dx1_ref[...] = dx1
        dx1b_ref[...] = dx1.astype(BF16)
        part = jnp.sum(dh * xh, axis=0, keepdims=True)

        @pl.when(i == 0)
        def _():
            gg_ref[...] = part

        @pl.when(i != 0)
        def _():
            gg_ref[...] += part

    tok = pl.BlockSpec((tb, d), lambda i: (i, 0))
    full = pl.BlockSpec((d, d), lambda i: (0, 0))
    acc = pl.BlockSpec((m, d), lambda i: (0, 0))
    res, _ = _pcall(
        body, name="xattn_bwd", grid=(t // tb,),
        in_specs=[tok, tok, pl.BlockSpec((1, d), lambda i: (0, 0)), tok, full,
                  pl.BlockSpec((m, d), lambda i: (0, 0)), pl.BlockSpec((m, d), lambda i: (0, 1)), full],
        out_specs=[tok, tok, tok, acc, acc, pl.BlockSpec((1, d), lambda i: (0, 0))],
        out_shape=[jax.ShapeDtypeStruct((t, d), F32), jax.ShapeDtypeStruct((t, d), BF16),
                   jax.ShapeDtypeStruct((t, d), BF16),
                   jax.ShapeDtypeStruct((m, d), F32), jax.ShapeDtypeStruct((m, d), F32),
                   jax.ShapeDtypeStruct((1, d), F32)],
        semantics=("arbitrary",), vmem_mb=48, after=after,
    )(dx2, x1, g, q, w_q, kv, kv, w_o)
    return res


def _mlp_down_loss(a, w_down, x2, tgt, g, *, tb):
    t, d = x2.shape
    f = a.shape[1]

    def body(a_ref, w_ref, x_ref, t_ref, g_ref, dx_ref, dxb_ref, loss_ref, gg_ref):
        i = pl.program_id(0)
        av = a_ref[...].astype(F32)
        x3 = x_ref[...] + _dot((av * av).astype(BF16), w_ref[...])
        g_v = g_ref[...]
        out, xh, r = _rms_fwd(x3, g_v)
        err = out - t_ref[...]
        dout = err * (1.0 / d)
        dx = _rms_bwd(dout, xh, r, g_v)
        dx_ref[...] = dx
        dxb_ref[...] = dx.astype(BF16)
        part = jnp.sum(dout * xh, axis=0, keepdims=True)
        lpart = 0.5 * jnp.sum(jnp.mean(err * err, axis=-1, keepdims=True), axis=0, keepdims=True)
        lpart = jnp.broadcast_to(lpart, loss_ref.shape)

        @pl.when(i == 0)
        def _():
            gg_ref[...] = part
            loss_ref[...] = lpart

        @pl.when(i != 0)
        def _():
            gg_ref[...] += part
            loss_ref[...] += lpart

    tok = pl.BlockSpec((tb, d), lambda i: (i, 0))
    return pl.pallas_call(
        body, name="mlp_down_loss", grid=(t // tb,),
        in_specs=[pl.BlockSpec((tb, f), lambda i: (i, 0)), pl.BlockSpec((f, d), lambda i: (0, 0)), tok, tok,
                  pl.BlockSpec((1, d), lambda i: (0, 0))],
        out_specs=[tok, tok, pl.BlockSpec((8, 128), lambda i: (0, 0)), pl.BlockSpec((1, d), lambda i: (0, 0))],
        out_shape=[jax.ShapeDtypeStruct((t, d), F32), jax.ShapeDtypeStruct((t, d), BF16),
                   jax.ShapeDtypeStruct((8, 128), F32), jax.ShapeDtypeStruct((1, d), F32)],
        compiler_params=_params(("arbitrary",), 56),
    )(a, w_down, x2, tgt, g)


def _mlp_dpre(dx3, w_down, a, *, tb, bn):
    t, d = dx3.shape
    f = a.shape[1]

    def body(dx_ref, w_ref, a_ref, o_ref):
        o_ref[...] = (2.0 * a_ref[...].astype(F32) * _dot_nt(dx_ref[...], w_ref[...])).astype(BF16)

    return pl.pallas_call(
        body, name="mlp_dpre", grid=(t // tb, f // bn),
        in_specs=[pl.BlockSpec((tb, d), lambda i, j: (i, 0)), pl.BlockSpec((bn, d), lambda i, j: (j, 0)),
                  pl.BlockSpec((tb, bn), lambda i, j: (i, j))],
        out_specs=pl.BlockSpec((tb, bn), lambda i, j: (i, j)),
        out_shape=jax.ShapeDtypeStruct((t, f), BF16),
        compiler_params=_params(("parallel", "arbitrary"), 48),
    )(dx3, w_down, a)


def _adamw(gsum, w, m, v):
    m_new = ADAM_B1 * m + (1.0 - ADAM_B1) * gsum
    v_new = ADAM_B2 * v + (1.0 - ADAM_B2) * (gsum * gsum)
    m_hat = m_new / (1.0 - ADAM_B1 ** ADAM_STEP)
    v_hat = v_new / (1.0 - ADAM_B2 ** ADAM_STEP)
    delta = -ADAM_LR * (m_hat / (jnp.sqrt(v_hat) + ADAM_EPS) + ADAM_WD * w)
    return delta, m_new, v_new


def _sum_adamw(parts, w, m, v, *, name, tr):
    r, c = w.shape

    def body(p_ref, w_ref, m_ref, v_ref, g_ref, d_ref, mo_ref, vo_ref):
        g = p_ref[0].astype(F32)
        for k in range(1, N_DEV):
            g = g + p_ref[k].astype(F32)
        g_ref[...] = g
        d_ref[...], mo_ref[...], vo_ref[...] = _adamw(g, w_ref[...], m_ref[...], v_ref[...])

    blk = pl.BlockSpec((tr, c), lambda i: (i, 0))
    return pl.pallas_call(
        body, name=name, grid=(r // tr,),
        in_specs=[pl.BlockSpec((N_DEV, tr, c), lambda i: (0, i, 0)), blk, blk, blk],
        out_specs=[blk] * 4, out_shape=[jax.ShapeDtypeStruct((r, c), F32)] * 4,
        compiler_params=_params(("parallel",), 40),
    )(parts, w, m, v)


def _sum_small(parts):
    _, r, c = parts.shape

    def body(p_ref, o_ref):
        s = p_ref[0]
        for k in range(1, N_DEV):
            s = s + p_ref[k]
        o_ref[...] = s

    return pl.pallas_call(body, name="sum_small", out_shape=jax.ShapeDtypeStruct((r, c), F32))(parts)


def _adamw_small(g, w, m, v):
    def body(g_ref, w_ref, m_ref, v_ref, d_ref, mo_ref, vo_ref):
        d_ref[...], mo_ref[...], vo_ref[...] = _adamw(g_ref[...], w_ref[...], m_ref[...], v_ref[...])

    return pl.pallas_call(body, name="adamw_small", out_shape=[jax.ShapeDtypeStruct(g.shape, F32)] * 3)(g, w, m, v)


def _head_sum_matrix():
    r = lax.broadcasted_iota(jnp.int32, (512, 512), 0) // HEAD_DIM
    c = lax.broadcasted_iota(jnp.int32, (512, 512), 1) // HEAD_DIM
    return (r == c).astype(BF16)


_SHARD_AXIS = dict(w_in=1, w_out=0, w_q=0, w_kv=1, w_o=0, w_up=1, w_down=0, conv_w=None, small=None)


class _Weights:
    def __init__(self, full, shards=None):
        self.full = dict(full)
        self.shards = shards

    def rider(self, names, late=False):
        if self.shards is None:
            return None
        return _Gather([self.shards[n] for n in names], [_SHARD_AXIS[n] for n in names], late)

    def arrived(self, names, gathered):
        if gathered is not None:
            for n, g in zip(names, gathered):
                self.full[n] = g.transpose(1, 0, 2).reshape(g.shape[1], -1) if n == "conv_w" else g

    def __getitem__(self, name):
        return self.full[name]


class _Grads:
    def __init__(self, distributed):
        self.distributed = distributed
        self.local = {}
        self.pending = {}

    def add(self, name, g):
        self.local[name] = g

    def send(self, *names):
        if not self.distributed:
            return []
        rider = _Exchange([self.local[n] for n in names], [_SHARD_AXIS[n] for n in names])
        started = _exchange_start(rider, "send_" + "_".join(names))
        self.pending[names[0]] = (names, rider, started)
        return [started[3]]

    def wait(self, first_name, after):
        names, rider, started = self.pending.pop(first_name)
        return _exchange_wait(rider, started, after, "wait_" + "_".join(names))


def _ride(fn, *args, rider=None, **kw):
    if rider is None:
        return fn(*args, **kw), None
    return fn(*args, rider=rider, **kw)


def _local_step(x, mem, tgt, gains, weights, grads):
    names = ["w_in", "conv_w"]
    (x, tgt), got = _ride(_reorder, [x, tgt], "reorder_in", rider=weights.rider(names, late=True))
    weights.arrived(names, got)
    w_in, cw = weights["w_in"], weights["conv_w"]

    names = ["w_out", "w_kv"]
    (proj, h1), got = _ride(_norm_matmul, x, gains["g_mix"], w_in, name="proj", out_dtype=F32, tb=1024, bn=768,
                            save_h=True, rider=weights.rider(names))
    weights.arrived(names, got)
    names = ["w_q", "w_o", "w_up"]
    (attn, *lses), got = _ride(_attention_fwd, proj, rider=weights.rider(names))
    weights.arrived(names, got)
    x1, merged = _mixer_fwd(x, attn, proj, cw, gains["g_attn_out"], gains["g_conv_out"], weights["w_out"])
    kv, mem_n = _norm_matmul(mem, gains["g_mem"], weights["w_kv"], name="mem_kv", out_dtype=BF16, tb=mem.shape[0],
                             bn=1024, save_h=True)
    x2, h2, qm, om = _xattn_fwd(x1, gains["g_xattn"], weights["w_q"], kv, weights["w_o"], tb=512)
    w_up = weights["w_up"]
    (a, h3), got = _ride(_norm_matmul, x2, gains["g_mlp"], w_up, name="mlp_up", out_dtype=BF16, tb=1024, bn=1024,
                         relu=True, save_h=True, rider=weights.rider(["w_down"], late=True))
    weights.arrived(["w_down"], got)
    w_down = weights["w_down"]
    dx3, dx3b, loss_blk, gg_final = _mlp_down_loss(a, w_down, x2, tgt, gains["g_final"], tb=256)

    dpre = _mlp_dpre(dx3b, w_down, a, tb=1024, bn=1024)
    grads.add("w_down", _matmul_tn(a, dx3b, name="grad_w_down", bm=512, bn=1024, square_a=True))
    sent = grads.send("w_down")
    grads.add("w_up", _matmul_tn(h3, dpre, name="grad_w_up", bm=1024, bn=512, after=sent))
    sent = grads.send("w_up")
    dx2, dx2b, gg_mlp = _matmul_nt_normbwd(dpre, w_up, x2, gains["g_mlp"], dx3, name="mlp_dx", tb=512,
                                           also_bf16=True, after=sent)

    grads.add("w_o", _matmul_tn(om, dx2b, name="grad_w_o", bm=1024, bn=512))
    sent = grads.send("w_o")
    dx1, dx1b, dqm, dk, dv, gg_xattn = _xattn_bwd(dx2, x1, gains["g_xattn"], qm, weights["w_q"], kv, weights["w_o"],
                                                  tb=512, after=sent)
    grads.add("w_q", _matmul_tn(h2, dqm, name="grad_w_q", bm=1024, bn=512))
    dkv = jnp.concatenate([dk, dv], axis=1).astype(BF16)
    grads.add("w_kv", _matmul_tn(mem_n, dkv, name="grad_w_kv", bm=1024, bn=1024))
    _, gg_mem = _matmul_nt_normbwd(dkv, weights["w_kv"], mem, gains["g_mem"], None, name="mem_dx", tb=mem.shape[0])

    grads.add("w_out", _matmul_tn(merged, dx1b, name="grad_w_out", bm=1024, bn=512))
    sent = grads.send("w_q", "w_kv", "w_out")
    dattn, dsum, dy, gg_attn, gg_conv = _mixer_bwd(dx1, attn, proj, cw, gains["g_attn_out"], gains["g_conv_out"],
                                                   weights["w_out"], _head_sum_matrix(), after=sent)
    dproj, gcw = _conv_bwd(dy, proj, cw)
    dproj = _attention_bwd(proj, dattn, dsum, lses, dproj)
    grads.add("w_in", _matmul_tn(h1, dproj, name="grad_w_in", bm=1024, bn=512))
    sent = grads.send("w_in")
    grad_x, gg_mix = _matmul_nt_normbwd(dproj, w_in, x, gains["g_mix"], dx1, name="mixer_dx", tb=512,
                                        to_natural=True, after=sent)

    def part(v):
        return jnp.pad(v, ((0, SMALL_PART - v.shape[0]), (0, 1024 - v.shape[1])))

    parts = [gg_mix, gg_xattn, gg_mem, gg_mlp, gg_final, jnp.concatenate([gg_attn, gg_conv], axis=1), gcw, loss_blk]
    grads.add("small", jnp.concatenate([part(v) for v in parts], axis=0))
    return grad_x


SMALL_PART = 8
_BIG = ("w_in", "w_out", "w_q", "w_kv", "w_o", "w_up", "w_down")
_GAIN_ROWS = ("g_mix", "g_xattn", "g_mem", "g_mlp", "g_final")


def _pack_small(vals, conv):
    rows = [vals[k].reshape(1, -1) for k in _GAIN_ROWS]
    rows.append(jnp.concatenate([vals["g_attn_out"].reshape(1, -1), vals["g_conv_out"].reshape(1, -1)], axis=1))
    flat = conv.reshape(1, -1)
    rows.append(jnp.pad(flat, ((0, 0), (0, 1024 - flat.shape[1]))))
    rows.append(jnp.zeros((1, 1024), F32))
    return jnp.concatenate(rows, axis=0)


def kernel(x, mem, g_mix, w_in, conv_w, g_attn_out, g_conv_out, w_out, g_xattn, g_mem, w_q_mem, w_kv_mem, w_o_mem, g_mlp, w_up, w_down, g_final, loss_target, m_g_mix, m_w_in, m_conv_w, m_g_attn_out, m_g_conv_out, m_w_out, m_g_xattn, m_g_mem, m_w_q_mem, m_w_kv_mem, m_w_o_mem, m_g_mlp, m_w_up, m_w_down, m_g_final, v_g_mix, v_w_in, v_conv_w, v_g_attn_out, v_g_conv_out, v_w_out, v_g_xattn, v_g_mem, v_w_q_mem, v_w_kv_mem, v_w_o_mem, v_g_mlp, v_w_up, v_w_down, v_g_final):
    d = x.shape[-1]
    me = 4 * lax.axis_index("x") + 2 * lax.axis_index("y") + lax.axis_index("c")
    w_shards = dict(w_in=w_in, w_out=w_out, w_q=w_q_mem, w_kv=w_kv_mem, w_o=w_o_mem, w_up=w_up, w_down=w_down)
    m_shards = dict(w_in=m_w_in, w_out=m_w_out, w_q=m_w_q_mem, w_kv=m_w_kv_mem, w_o=m_w_o_mem, w_up=m_w_up,
                    w_down=m_w_down)
    v_shards = dict(w_in=v_w_in, w_out=v_w_out, w_q=v_w_q_mem, w_kv=v_w_kv_mem, w_o=v_w_o_mem, w_up=v_w_up,
                    w_down=v_w_down)
    gains = dict(g_mix=g_mix, g_attn_out=g_attn_out, g_conv_out=g_conv_out, g_xattn=g_xattn, g_mem=g_mem,
                 g_mlp=g_mlp, g_final=g_final)
    gains2 = {k: v.reshape(1, -1) for k, v in gains.items()}

    shards = {k: w_shards[k].astype(BF16) for k in _BIG}
    shards["conv_w"] = conv_w
    grads = _Grads(distributed=True)
    grad_x = _local_step(x[0], mem[0], loss_target[0], gains2, _Weights({}, shards), grads)

    after = grads.send("small")
    outs = {}
    tiles = dict(w_in=256, w_out=128, w_q=128, w_kv=256, w_o=128, w_up=256, w_down=256)
    for group in (("w_down",), ("w_up",), ("w_o",), ("w_q", "w_kv", "w_out"), ("w_in",)):
        for k, received in zip(group, grads.wait(group[0], after)):
            outs[k] = _sum_adamw(received, w_shards[k], m_shards[k], v_shards[k], name=f"adamw_{k}", tr=tiles[k])
            after = [outs[k][0]]
    small_received, = grads.wait("small", after)

    ssum = _sum_small(small_received)
    row = lambda i: ssum[SMALL_PART * i]
    loss = ssum[SMALL_PART * 7, 0]
    g_small = {k: row(i) for i, k in enumerate(_GAIN_ROWS)}
    g_small["g_attn_out"] = row(5)[0:512]
    g_small["g_conv_out"] = row(5)[512:1024]
    taps = ssum[SMALL_PART * 6:SMALL_PART * 6 + 3, 0:512]
    g_conv = lax.dynamic_slice_in_dim(taps, me * 64, 64, axis=1)
    m_small = dict(g_mix=m_g_mix, g_attn_out=m_g_attn_out, g_conv_out=m_g_conv_out, g_xattn=m_g_xattn,
                   g_mem=m_g_mem, g_mlp=m_g_mlp, g_final=m_g_final)
    v_small = dict(g_mix=v_g_mix, g_attn_out=v_g_attn_out, g_conv_out=v_g_conv_out, g_xattn=v_g_xattn,
                   g_mem=v_g_mem, g_mlp=v_g_mlp, g_final=v_g_final)
    packed = [_pack_small(g_small, g_conv), _pack_small(gains, conv_w), _pack_small(m_small, m_conv_w),
              _pack_small(v_small, v_conv_w)]
    upd = _adamw_small(*packed)

    def unpack(p):
        res = {k: p[i] for i, k in enumerate(_GAIN_ROWS)}
        res["g_attn_out"] = p[5, 0:512]
        res["g_conv_out"] = p[5, 512:1024]
        res["conv_w"] = p[6, 0:192].reshape(3, 64)
        return res

    g_small["conv_w"] = g_conv
    small_out = [g_small] + [unpack(p) for p in upd]
    names = {"g_mix": "g_mix", "w_in": "w_in", "conv_w": "conv_w", "g_attn_out": "g_attn_out",
             "g_conv_out": "g_conv_out", "w_out": "w_out", "g_xattn": "g_xattn", "g_mem": "g_mem",
             "w_q_mem": "w_q", "w_kv_mem": "w_kv", "w_o_mem": "w_o", "g_mlp": "g_mlp", "w_up": "w_up",
             "w_down": "w_down", "g_final": "g_final"}
    result = [loss, grad_x[None]]
    for which in range(4):
        for key in names.values():
            result.append(outs[key][which] if key in outs else small_out[which][key])
    return tuple(result)
```

```python
import math

import jax
import jax.numpy as jnp
from jax import lax
from jax.experimental import pallas as pl
from jax.experimental.pallas import tpu as pltpu

F32 = jnp.float32
BF16 = jnp.bfloat16
NORM_EPS = 1e-6
NEG_INF = -1e30
N_DEV = 8
BLK = 128
HEAD_DIM = 64
N_MEM_HEADS = 4
ADAM_LR = 0.001
ADAM_B1 = 0.9
ADAM_B2 = 0.999
ADAM_EPS = 1e-08
ADAM_WD = 0.01
ADAM_STEP = 10
MESH = pl.DeviceIdType.MESH
ANY = pl.BlockSpec(memory_space=pl.ANY)


def _dot(a, b):
    return jnp.dot(a, b, preferred_element_type=F32)


def _dot_nt(a, b):
    return lax.dot_general(a, b, (((1,), (1,)), ((), ())), preferred_element_type=F32)


def _dot_tn(a, b):
    return lax.dot_general(a, b, (((0,), (0,)), ((), ())), preferred_element_type=F32)


def _params(semantics, vmem_mb):
    return pltpu.CompilerParams(dimension_semantics=semantics, vmem_limit_bytes=vmem_mb << 20)


def _rms_fwd(x, g):
    r = lax.rsqrt(jnp.mean(x * x, axis=-1, keepdims=True) + NORM_EPS)
    xh = x * r
    return xh * g, xh, r


def _rms_bwd(dy, xh, r, g):
    gy = dy * g
    return r * (gy - xh * jnp.mean(xh * gy, axis=-1, keepdims=True))


def _position():
    x, y, c = lax.axis_index("x"), lax.axis_index("y"), lax.axis_index("c")
    return x, y, c


def _block_of(ref, j, axis, shard_shape):
    r, c = shard_shape
    if axis is None:
        return ref.at[j]
    if axis == 0:
        return ref.at[pl.ds(j * r, r), :]
    return ref.at[:, pl.ds(j * c, c)]


class _Gather:
    has_mid = True
    alias_pairs = ()

    def __init__(self, shards, axes, late=False):
        self.arrays = list(shards)
        self.axes = list(axes)
        self.late = late
        self.n = len(self.arrays)

    def out_shape(self):
        res = []
        for s, axis in zip(self.arrays, self.axes):
            r, c = s.shape
            shape = (N_DEV, r, c) if axis is None else (N_DEV * r, c) if axis == 0 else (r, N_DEV * c)
            res.append(jax.ShapeDtypeStruct(shape, s.dtype))
        return res

    def scratch(self):
        return [pltpu.SemaphoreType.DMA((self.n, 7)), pltpu.SemaphoreType.DMA((self.n, 7)),
                pltpu.SemaphoreType.DMA((self.n,))]

    def _ctx(self, ins, outs, sems):
        send_sems, recv_sems, local_sems = sems
        x, y, c = _position()
        me, sibling = (x, y, c), (x, y, 1 - c)
        chips = [(1 - x, y), (x, 1 - y), (1 - x, 1 - y)]

        def lin(px, py, pc):
            return 4 * px + 2 * py + pc

        def place(a, block):
            return _block_of(outs[a], lin(*block), self.axes[a], self.arrays[a].shape)

        def copy(a, k, block, to, src=None):
            dst = place(a, block)
            return pltpu.make_async_remote_copy(
                src_ref=dst if src is None else src, dst_ref=dst,
                send_sem=send_sems.at[a, k], recv_sem=recv_sems.at[a, k],
                device_id=to, device_id_type=MESH)

        def mine():
            return [pltpu.make_async_copy(ins[a], place(a, me), local_sems.at[a]) for a in range(self.n)]

        def first():
            res = []
            for a in range(self.n):
                res.append(copy(a, 0, me, sibling, src=ins[a]))
                res += [copy(a, 1 + j, me, (*chip, c), src=ins[a]) for j, chip in enumerate(chips)]
            return res

        return c, me, sibling, chips, copy, mine, first

    def start(self, ins, outs, sems):
        _, _, _, _, _, mine, first = self._ctx(ins, outs, sems)
        for cp in mine() + first():
            cp.start()

    def mid(self, ins, outs, sems):
        c, me, sibling, chips, copy, _, _ = self._ctx(ins, outs, sems)
        for j, chip in enumerate(chips):
            for a in range(self.n):
                copy(a, 1 + j, (*chip, c), me).wait_recv()
                copy(a, 4 + j, (*chip, c), sibling).start()

    def finish(self, ins, outs, sems):
        c, me, sibling, chips, copy, mine, first = self._ctx(ins, outs, sems)
        for a in range(self.n):
            copy(a, 0, sibling, me).wait_recv()
            for j, chip in enumerate(chips):
                copy(a, 4 + j, (*chip, 1 - c), me).wait_recv()
        for cp in first():
            cp.wait_send()
        for j, chip in enumerate(chips):
            for a in range(self.n):
                copy(a, 4 + j, (*chip, c), sibling).wait_send()
        for cp in mine():
            cp.wait()


class _Exchange:
    def __init__(self, parts, axes):
        self.n = len(parts)
        self.axes = list(axes)
        self.arrays = list(parts)

    def _piece(self, a):
        r, c = self.arrays[a].shape
        axis = self.axes[a]
        return (r, c) if axis is None else (r // N_DEV, c) if axis == 0 else (r, c // N_DEV)

    def out_shape(self):
        return [jax.ShapeDtypeStruct((N_DEV,) + self._piece(a), self.arrays[a].dtype) for a in range(self.n)]

    def semaphores(self):
        return [pltpu.SemaphoreType.DMA((7 * self.n,)), pltpu.SemaphoreType.DMA((7 * self.n,)),
                pltpu.SemaphoreType.DMA((self.n,))]

    def _ctx(self, ins, outs, sems):
        send_sems, recv_sems, local_sems = sems
        x, y, c = _position()
        me = 4 * x + 2 * y + c

        def src(a, j):
            return ins[a] if self.axes[a] is None else _block_of(ins[a], j, self.axes[a], self._piece(a))

        def dst(a, j):
            return outs[a].at[j]

        def local():
            return [pltpu.make_async_copy(src(a, me), dst(a, me), local_sems.at[a]) for a in range(self.n)]

        def remote(inbound):
            res = []
            for a in range(self.n):
                for k in range(1, N_DEV):
                    peer = (1 - x if k & 4 else x, 1 - y if k & 2 else y, 1 - c if k & 1 else c)
                    plin = 4 * peer[0] + 2 * peer[1] + peer[2]
                    res.append(pltpu.make_async_remote_copy(
                        src_ref=src(a, plin), dst_ref=dst(a, plin if inbound else me),
                        send_sem=send_sems.at[7 * a + k - 1], recv_sem=recv_sems.at[7 * a + k - 1],
                        device_id=peer, device_id_type=MESH))
            return res

        return local, remote

    def start(self, ins, outs, sems):
        local, remote = self._ctx(ins, outs, sems)
        for cp in local() + remote(False):
            cp.start()

    def finish(self, ins, outs, sems):
        local, remote = self._ctx(ins, outs, sems)
        for cp in remote(True):
            cp.wait_recv()
        for cp in remote(False):
            cp.wait_send()
        for cp in local():
            cp.wait()


def _exchange_start(rider, name):
    n = rider.n
    parts = rider.arrays
    lands = [lax.empty(s.shape, s.dtype) for s in rider.out_shape()]
    hbm = pl.BlockSpec(memory_space=pltpu.HBM)
    sem = pl.BlockSpec(memory_space=pltpu.SEMAPHORE)

    def body(*refs):
        ins, sems = refs[:n], refs[2 * n:2 * n + 3]
        outs, token = refs[2 * n + 3 + n:2 * n + 3 + 2 * n], refs[-1]
        rider.start(ins, outs, sems)
        token[...] = jnp.zeros_like(token)

    res = pl.pallas_call(
        body, name=name,
        out_shape=rider.semaphores() + [pltpu.HBM(p.shape, p.dtype) for p in parts]
                  + [pltpu.HBM(z.shape, z.dtype) for z in lands] + [jax.ShapeDtypeStruct((8, 128), F32)],
        in_specs=[hbm] * (2 * n), out_specs=[sem] * 3 + [hbm] * (2 * n) + [pl.BlockSpec(memory_space=pltpu.VMEM)],
        input_output_aliases={i: 3 + i for i in range(2 * n)},
        compiler_params=pltpu.CompilerParams(has_side_effects=pltpu.SideEffectType.DATAFLOW_SIDE_EFFECTING),
    )(*[pltpu.with_memory_space_constraint(a, pltpu.HBM) for a in parts + lands])
    return res[:3], res[3:3 + n], res[3 + n:3 + 2 * n], res[-1]


def _exchange_wait(rider, started, after, name):
    n = rider.n
    sems, parts, lands, _ = started
    hbm = pl.BlockSpec(memory_space=pltpu.HBM)
    sem = pl.BlockSpec(memory_space=pltpu.SEMAPHORE)

    def body(*refs):
        rider.finish(refs[:n], refs[n:2 * n], refs[2 * n:2 * n + 3])

    res = pl.pallas_call(
        body, name=name, out_shape=[pltpu.HBM(a.shape, a.dtype) for a in list(parts) + list(lands)],
        in_specs=[hbm] * (2 * n) + [sem] * 3 + [ANY] * len(after), out_specs=[hbm] * (2 * n),
        input_output_aliases={i: i for i in range(2 * n)},
        compiler_params=pltpu.CompilerParams(has_side_effects=pltpu.SideEffectType.DATAFLOW_SIDE_EFFECTING),
    )(*parts, *lands, *sems, *after)
    return list(res[n:])


def _pcall(body, *, name, grid, in_specs, out_specs, out_shape, scratch_shapes=(), semantics, vmem_mb, rider=None,
           aliases=None, after=()):
    in_specs, out_specs, out_shape = list(in_specs), list(out_specs), list(out_shape)
    scratch_shapes = list(scratch_shapes)
    aliases = dict(aliases or {})
    if rider is None:
        n_in, after = len(in_specs), list(after)

        def plain(*refs):
            body(*refs[:n_in], *refs[n_in + len(after):])

        call = pl.pallas_call(plain if after else body, name=name, grid=grid, in_specs=in_specs + [ANY] * len(after),
                              out_specs=out_specs, out_shape=out_shape, scratch_shapes=scratch_shapes,
                              input_output_aliases=aliases, compiler_params=_params(semantics, vmem_mb))
        return lambda *args: (list(call(*args, *after)), None)
    n_in, n_out, n_scr = len(in_specs), len(out_specs), len(scratch_shapes)
    r_in, r_shapes = len(rider.arrays), rider.out_shape()
    r_out = len(r_shapes)
    aliases.update({n_in + i: n_out + o for i, o in rider.alias_pairs})
    total = math.prod(grid)
    mid_step = total - 1 if rider.has_mid and rider.late else (3 * total) // 4

    def wrapped(*refs):
        bounds = [0, n_in, r_in, n_out, r_out, n_scr]
        for i in range(1, len(bounds)):
            bounds[i] += bounds[i - 1]
        a, ra, o, ro, s = (refs[bounds[i]:bounds[i + 1]] for i in range(5))
        rs = refs[bounds[5]:]
        step = pl.program_id(0)
        for k in range(1, len(grid)):
            step = step * grid[k] + pl.program_id(k)
        pl.when(step == 0)(lambda: rider.start(ra, ro, rs))
        body(*a, *o, *s)
        if rider.has_mid:
            pl.when(step == mid_step)(lambda: rider.mid(ra, ro, rs))
        pl.when(step == total - 1)(lambda: rider.finish(ra, ro, rs))

    call = pl.pallas_call(
        wrapped, name=name, grid=grid, in_specs=in_specs + [ANY] * r_in, out_specs=out_specs + [ANY] * r_out,
        out_shape=out_shape + r_shapes, scratch_shapes=scratch_shapes + rider.scratch(),
        input_output_aliases=aliases, compiler_params=_params(("arbitrary",) * len(grid), vmem_mb))

    def run(*args):
        res = call(*args, *rider.arrays)
        return list(res[:n_out]), list(res[n_out:])

    return run


def _norm_matmul(x, g, w, *, name, out_dtype, tb, bn, relu=False, save_h=False, rider=None):
    t, d = x.shape
    n = w.shape[1]

    def body(x_ref, g_ref, w_ref, o_ref, *rest):
        h_scr = rest[-1]

        @pl.when(pl.program_id(1) == 0)
        def _():
            h = _rms_fwd(x_ref[...], g_ref[...])[0].astype(BF16)
            h_scr[...] = h
            if save_h:
                rest[0][...] = h

        acc = _dot(h_scr[...], w_ref[...])
        if relu:
            acc = jnp.maximum(acc, 0.0)
        o_ref[...] = acc.astype(out_dtype)

    out_shape = [jax.ShapeDtypeStruct((t, n), out_dtype)]
    out_specs = [pl.BlockSpec((tb, bn), lambda i, j: (i, j))]
    if save_h:
        out_shape.append(jax.ShapeDtypeStruct((t, d), BF16))
        out_specs.append(pl.BlockSpec((tb, d), lambda i, j: (i, 0)))
    res, extra = _pcall(
        body, name=name, grid=(t // tb, n // bn),
        in_specs=[pl.BlockSpec((tb, d), lambda i, j: (i, 0)),
                  pl.BlockSpec((1, d), lambda i, j: (0, 0)),
                  pl.BlockSpec((d, bn), lambda i, j: (0, j))],
        out_specs=out_specs, out_shape=out_shape,
        scratch_shapes=[pltpu.VMEM((tb, d), BF16)],
        semantics=("parallel", "arbitrary"), vmem_mb=48, rider=rider,
    )(x, g, w)
    res = res if save_h else res[0]
    return res if rider is None else (res, extra)


def _matmul_nt_normbwd(dy, w, x, g, dres, *, name, tb, also_bf16=False, to_natural=False, after=()):
    t, d = x.shape
    stacked = dy.ndim == 3
    has_res = dres is not None
    n_i = SEG // TI
    if to_natural:
        tb = N_RES * TI

    def body(dy_ref, w_ref, x_ref, g_ref, *rest):
        rest = list(rest)
        dres_ref = rest.pop(0) if has_res else None
        dx_ref = rest.pop(0)
        dxb_ref = rest.pop(0) if also_bf16 else None
        gg_ref = rest.pop(0)
        i = pl.program_id(0)

        def rows(ref, *lead):
            v = ref[lead] if lead else ref[...]
            return v[0].reshape(tb, v.shape[-1]) if to_natural else v

        if stacked:
            kb = dy_ref.shape[-1]
            dh = _dot_nt(rows(dy_ref, 0), w_ref[:, 0:kb])
            for s in range(1, dy_ref.shape[0]):
                dh = dh + _dot_nt(rows(dy_ref, s), w_ref[:, s * kb:(s + 1) * kb])
        else:
            dh = _dot_nt(rows(dy_ref), w_ref[...])
        g_v = g_ref[...]
        _, xh, r = _rms_fwd(rows(x_ref), g_v)
        dx = _rms_bwd(dh, xh, r, g_v)
        if has_res:
            dx = dx + rows(dres_ref)
        if to_natural:
            scr = rest.pop(0)
            for cb in range(d // BLK):
                cols = slice(cb * BLK, (cb + 1) * BLK)
                slab = scr.at[cb]
                for res in range(N_RES):
                    slab[pl.ds(res, TI, stride=N_RES), :] = dx[res * TI:(res + 1) * TI, cols]
                dx_ref[:, cols] = slab[...]
        else:
            dx_ref[...] = dx
        if also_bf16:
            dxb_ref[...] = dx.astype(BF16)
        part = jnp.sum(dh * xh, axis=0, keepdims=True)

        @pl.when(i == 0)
        def _():
            gg_ref[...] = part

        @pl.when(i != 0)
        def _():
            gg_ref[...] += part

    tok = pl.BlockSpec((tb, d), lambda i: (i, 0))
    row = pl.BlockSpec((1, d), lambda i: (0, 0))
    if to_natural:
        act = pl.BlockSpec((1, N_RES, TI, d), lambda i: (i // n_i, 0, i % n_i, 0))
        dy_spec = pl.BlockSpec((dy.shape[0], 1, N_RES, TI, dy.shape[2]), lambda i: (0, i // n_i, 0, i % n_i, 0))
        dy, x = dy.reshape(dy.shape[0], t // HALF, N_RES, SEG, dy.shape[2]), _x4(x)
        dres = _x4(dres) if has_res else None
    elif stacked:
        act, dy_spec = tok, pl.BlockSpec((dy.shape[0], tb, dy.shape[2]), lambda i: (0, i, 0))
    else:
        act, dy_spec = tok, pl.BlockSpec((tb, dy.shape[1]), lambda i: (i, 0))
    in_specs = [dy_spec, pl.BlockSpec(w.shape, lambda i: (0, 0)), act, row]
    args = [dy, w, x, g]
    if has_res:
        in_specs.append(act)
        args.append(dres)
    out_specs = [tok] + ([tok] if also_bf16 else []) + [row]
    out_shape = ([jax.ShapeDtypeStruct((t, d), F32)] + ([jax.ShapeDtypeStruct((t, d), BF16)] if also_bf16 else [])
                 + [jax.ShapeDtypeStruct((1, d), F32)])
    res, _ = _pcall(
        body, name=name, grid=(t // tb,), in_specs=in_specs, out_specs=out_specs, out_shape=out_shape,
        scratch_shapes=[pltpu.VMEM((d // BLK, tb, BLK), F32)] if to_natural else [],
        semantics=("arbitrary",), vmem_mb=56, after=after,
    )(*args)
    return res


def _matmul_tn(a, b, *, name, bm, bn, square_a=False, after=()):
    t, m = a.shape
    stacked = b.ndim == 3
    n = b.shape[0] * bn if stacked else b.shape[1]

    def body(a_ref, b_ref, o_ref):
        av = a_ref[...]
        if square_a:
            av = av.astype(F32)
            av = (av * av).astype(BF16)
        o_ref[...] = _dot_tn(av, b_ref[...]).astype(BF16)

    res, _ = _pcall(
        body, name=name, grid=(m // bm, n // bn),
        in_specs=[pl.BlockSpec((t, bm), lambda i, j: (0, i)),
                  pl.BlockSpec((None, t, bn), lambda i, j: (j, 0, 0)) if stacked
                  else pl.BlockSpec((t, bn), lambda i, j: (0, j))],
        out_specs=[pl.BlockSpec((bm, bn), lambda i, j: (i, j))], out_shape=[jax.ShapeDtypeStruct((m, n), BF16)],
        semantics=("parallel", "parallel"), vmem_mb=56, after=after,
    )(a, b)
    return res[0]


N_RES = 16
SEG = 128
HALF = N_RES * SEG
TI = 32


def _x4(a):
    return a.reshape(a.shape[0] // HALF, N_RES, SEG, a.shape[1])


def _reorder(arrays, name, rider=None):
    t, c = arrays[0].shape
    n = len(arrays)
    n_i = SEG // TI

    def body(*refs):
        scr = refs[-1]
        for i_ref, o_ref in zip(refs[:n], refs[n:2 * n]):
            for cb in range(c // BLK):
                cols = slice(cb * BLK, (cb + 1) * BLK)
                slab = scr.at[cb]
                slab[...] = i_ref[:, cols]
                for r in range(N_RES):
                    o_ref[0, r, :, cols] = slab[pl.ds(r, TI, stride=N_RES), :]

    res, extra = _pcall(
        body, name=name, grid=(t // (TI * N_RES),),
        in_specs=[pl.BlockSpec((TI * N_RES, c), lambda s: (s, 0))] * n,
        out_specs=[pl.BlockSpec((1, N_RES, TI, c), lambda s: (s // n_i, 0, s % n_i, 0))] * n,
        out_shape=[jax.ShapeDtypeStruct((t // HALF, N_RES, SEG, c), F32)] * n,
        scratch_shapes=[pltpu.VMEM((c // BLK, TI * N_RES, BLK), F32)],
        semantics=("parallel",), vmem_mb=32, rider=rider,
    )(*arrays)
    res = [r.reshape(t, c) for r in res]
    return res if rider is None else (res, extra)


_PATTERNS = ((1, 16, 8, SEG), (4, 4, 32, 4 * SEG), (16, 1, SEG, 0))
_FIRST = {1: 1, 4: 4, 16: 16}


def _group_rows(d, g):
    a = g >> 4
    if d == 16:
        base = a * HALF + (g & 15) * SEG
        prev = base - HALF
    elif d == 4:
        c = (g >> 2) & 3
        base = a * HALF + (g & 3) * SEG + c * 32
        prev = jnp.where(c > 0, base - 32, base - HALF + 96)
    else:
        c = g & 15
        base = a * HALF + c * 8
        prev = jnp.where(c > 0, base - 8, base - HALF + 120)
    return base, prev


def _load_rows(ref, base, n, rows, stride):
    parts = [ref[pl.ds(pl.multiple_of(base + j * stride, 8), rows), :] for j in range(n)]
    return parts[0] if n == 1 else jnp.concatenate(parts, axis=0)


def _store_rows(ref, base, val, n, rows, stride, add=False):
    for j in range(n):
        sl = pl.ds(pl.multiple_of(base + j * stride, 8), rows)
        piece = val[j * rows:(j + 1) * rows, :]
        if add:
            ref[sl, :] += piece
        else:
            ref[sl, :] = piece


def _band_bias(n, rows):
    shift = rows.bit_length() - 1
    lq = lax.broadcasted_iota(jnp.int32, (BLK, BLK), 0)
    lk = lax.broadcasted_iota(jnp.int32, (BLK, BLK), 1)
    iq = (lq & (rows - 1)) * n + (lq >> shift)
    ik = (lk & (rows - 1)) * n + (lk >> shift)
    zero = jnp.zeros((BLK, BLK), F32)
    return jnp.where(ik >= iq, zero, NEG_INF), jnp.where(ik <= iq, zero, NEG_INF)


def _set_bias(bias_scr, n, rows):
    prev_b, cur_b = _band_bias(n, rows)
    for half in range(2):
        bias_scr[half * BLK:(half + 1) * BLK, 0:BLK] = prev_b
        bias_scr[half * BLK:(half + 1) * BLK, BLK:2 * BLK] = cur_b


SCALE = 1.0 / math.sqrt(HEAD_DIM)


def _head_consts(value=1.0):
    lane_lo = lax.broadcasted_iota(jnp.int32, (BLK, BLK), 1) < HEAD_DIM
    return lane_lo, [jnp.where(lane_lo, value, 0.0).astype(BF16), jnp.where(lane_lo, 0.0, value).astype(BF16)]


def _stack_heads(v, head_mask):
    return jnp.concatenate([v * head_mask[0], v * head_mask[1]], axis=0)


def _unstack_heads(v2, lane_lo):
    return jnp.where(lane_lo, v2[:BLK], v2[BLK:])


def _rows_per_head(v, lane_lo):
    rolled = pltpu.roll(v, HEAD_DIM, axis=1)
    return jnp.concatenate([jnp.where(lane_lo, v, rolled), jnp.where(lane_lo, rolled, v)], axis=0)


WIDTH = 4


def _loop(lo, hi, fn, width=None):
    if width is None:
        def body(g, carry):
            fn(g)
            return carry

        if hi > lo:
            lax.fori_loop(lo, hi, body, 0)
        return
    while hi > lo:
        trips = (hi - lo) // width
        if trips:
            def body(i, carry, lo=lo, width=width):
                fn([lo + width * i + j for j in range(width)])
                return carry

            lax.fori_loop(0, trips, body, 0)
            lo += trips * width
        width = max(1, width // 2)


def _mix_weights(l1, l2, l3):
    mx = jnp.maximum(jnp.maximum(l1, l2), l3)
    e1, e2, e3 = jnp.exp(l1 - mx), jnp.exp(l2 - mx), jnp.exp(l3 - mx)
    inv = 1.0 / (e1 + e2 + e3)
    return e1 * inv, e2 * inv, e3 * inv


def _attention_fwd(qkv, rider=None):
    t = qkv.shape[0]
    groups = 16 * (t // HALF)

    def body(q_ref, k_ref, v_ref, attn_ref, l1_ref, l2_ref, l3_ref, o_scr, bias_scr):
        lane_lo, q_mask = _head_consts(SCALE)
        l_refs = (l1_ref, l2_ref, l3_ref)
        for p, (d, n, rows, stride) in enumerate(_PATTERNS):
            _set_bias(bias_scr, n, rows)
            o_p, l_p = o_scr.at[p], l_refs[p]

            def block(gs, has_prev):
                at = [_group_rows(d, g) for g in gs]

                def load(ref, b):
                    return _load_rows(ref, b, n, rows, stride).astype(BF16)

                q2 = [_stack_heads(load(q_ref, b), q_mask) for b, _ in at]
                k2 = [load(k_ref, b) for b, _ in at]
                v2 = [load(v_ref, b) for b, _ in at]
                if has_prev:
                    k2 = [jnp.concatenate([load(k_ref, pv), k], axis=0) for (_, pv), k in zip(at, k2)]
                    v2 = [jnp.concatenate([load(v_ref, pv), v], axis=0) for (_, pv), v in zip(at, v2)]
                s = [_dot_nt(q, k) for q, k in zip(q2, k2)]
                s = [x + (bias_scr[...] if has_prev else bias_scr[:, BLK:2 * BLK]) for x in s]
                mx = [jnp.max(x, axis=1, keepdims=True) for x in s]
                e = [jnp.exp(x - m) for x, m in zip(s, mx)]
                den = [jnp.sum(x, axis=1, keepdims=True) for x in e]
                o2 = [_dot(x.astype(BF16), v) * (1.0 / dn) for x, v, dn in zip(e, v2, den)]
                lse2 = [jnp.broadcast_to(m + jnp.log(dn), (2 * BLK, BLK)) for m, dn in zip(mx, den)]
                for (b, _), o, l in zip(at, o2, lse2):
                    _store_rows(o_p, b, _unstack_heads(o, lane_lo), n, rows, stride)
                    _store_rows(l_p, b, _unstack_heads(l, lane_lo), n, rows, stride)

            _loop(0, _FIRST[d], lambda gs: block(gs, False), width=WIDTH)
            _loop(_FIRST[d], groups, lambda gs: block(gs, True), width=WIDTH)

        def mix(i):
            sl = pl.ds(pl.multiple_of(i * 256, 256), 256)
            w = _mix_weights(l1_ref[sl, :], l2_ref[sl, :], l3_ref[sl, :])
            attn_ref[sl, :] = w[0] * o_scr[0, sl, :] + w[1] * o_scr[1, sl, :] + w[2] * o_scr[2, sl, :]

        _loop(0, t // 256, mix)

    def col(c0):
        return pl.BlockSpec((t, BLK), lambda hp: (0, c0 + hp))

    res, extra = _pcall(
        body, name="attention_fwd", grid=(4,), in_specs=[col(0), col(4), col(8)], out_specs=[col(0)] * 4,
        out_shape=[jax.ShapeDtypeStruct((t, 512), F32)] * 4,
        scratch_shapes=[pltpu.VMEM((3, t, BLK), F32), pltpu.VMEM((2 * BLK, 2 * BLK), F32)],
        semantics=("parallel",), vmem_mb=48, rider=rider,
    )(qkv, qkv, qkv)
    return res if rider is None else (res, extra)


def _attention_bwd(qkv, dattn, dsum, lses, dproj):
    t = qkv.shape[0]
    groups = 16 * (t // HALF)

    def body(q_ref, k_ref, v_ref, da_ref, ds_ref, l1_ref, l2_ref, l3_ref, kept_ref, out_ref, acc, bias_scr):
        del kept_ref
        lane_lo, head_mask = _head_consts()
        q_mask = _head_consts(SCALE)[1]
        l_refs = (l1_ref, l2_ref, l3_ref)

        def clear(i):
            sl = pl.ds(pl.multiple_of(i * 512, 512), 512)
            for s in range(3):
                acc[s, sl, :] = jnp.zeros((512, BLK), F32)

        _loop(0, t // 512, clear)
        dq_acc, dk_acc, dv_acc = acc.at[0], acc.at[1], acc.at[2]
        for p, (d, n, rows, stride) in enumerate(_PATTERNS):
            _set_bias(bias_scr, n, rows)

            def block(gs, has_prev):
                at = [_group_rows(d, g) for g in gs]

                def load(ref, b):
                    return _load_rows(ref, b, n, rows, stride)

                def put(ref, b, val):
                    _store_rows(ref, b, val, n, rows, stride, add=True)

                def wide(x):
                    return jnp.concatenate([x, x], axis=1) if has_prev else x

                lse = [[load(ref, b) for ref in l_refs] for b, _ in at]
                w = [_mix_weights(*ls)[p] for ls in lse]
                do2 = [_stack_heads((wg * load(da_ref, b)).astype(BF16), head_mask) for wg, (b, _) in zip(w, at)]
                dl2 = [wide(_rows_per_head(wg * load(ds_ref, b), lane_lo)) for wg, (b, _) in zip(w, at)]
                lse2 = [wide(_rows_per_head(ls[p], lane_lo)) for ls in lse]
                q2 = [_stack_heads(load(q_ref, b).astype(BF16), q_mask) for b, _ in at]
                k2 = [load(k_ref, b).astype(BF16) for b, _ in at]
                v2 = [load(v_ref, b).astype(BF16) for b, _ in at]
                if has_prev:
                    k2 = [jnp.concatenate([load(k_ref, pv).astype(BF16), k], axis=0) for (_, pv), k in zip(at, k2)]
                    v2 = [jnp.concatenate([load(v_ref, pv).astype(BF16), v], axis=0) for (_, pv), v in zip(at, v2)]
                s = [_dot_nt(q, k) for q, k in zip(q2, k2)]
                dp = [_dot_nt(do, v) for do, v in zip(do2, v2)]
                pr = [jnp.exp(x + (bias_scr[...] if has_prev else bias_scr[:, BLK:2 * BLK]) - l)
                      for x, l in zip(s, lse2)]
                ds = [(pg * (x - dl)).astype(BF16) for pg, x, dl in zip(pr, dp, dl2)]
                dq2 = [_dot(x, k) * SCALE for x, k in zip(ds, k2)]
                dk2 = [_dot_tn(x, q) for x, q in zip(ds, q2)]
                dv2 = [_dot_tn(pg.astype(BF16), do) for pg, do in zip(pr, do2)]
                for (b, pv), dq, dk, dv in zip(at, dq2, dk2, dv2):
                    put(dq_acc, b, _unstack_heads(dq, lane_lo))
                    if has_prev:
                        put(dk_acc, pv, dk[:BLK])
                        put(dv_acc, pv, dv[:BLK])
                        put(dk_acc, b, dk[BLK:])
                        put(dv_acc, b, dv[BLK:])
                    else:
                        put(dk_acc, b, dk)
                        put(dv_acc, b, dv)

            _loop(0, _FIRST[d], lambda gs: block(gs, False), width=WIDTH)
            _loop(_FIRST[d], groups, lambda gs: block(gs, True), width=WIDTH)

        def emit(i):
            sl = pl.ds(pl.multiple_of(i * 512, 512), 512)
            for s in range(3):
                out_ref[s, sl, :] = acc[s, sl, :].astype(BF16)

        _loop(0, t // 512, emit)

    def col(c0):
        return pl.BlockSpec((t, BLK), lambda hp: (0, c0 + hp))

    res, _ = _pcall(
        body, name="attention_bwd", grid=(4,),
        in_specs=[col(0), col(4), col(8)] + [col(0)] * 5 + [ANY],
        out_specs=[pl.BlockSpec((3, t, BLK), lambda hp: (0, 0, hp))],
        out_shape=[jax.ShapeDtypeStruct(dproj.shape, BF16)],
        scratch_shapes=[pltpu.VMEM((3, t, BLK), F32), pltpu.VMEM((2 * BLK, 2 * BLK), F32)],
        semantics=("parallel",), vmem_mb=56, aliases={8: 0},
    )(qkv, qkv, qkv, dattn, dsum, *lses, dproj)
    return res[0]


def _order_specs(t):
    n_i = SEG // TI
    nblk = (t // HALF) * n_i
    per = TI // 8

    def main(c, col=0):
        return pl.BlockSpec((1, N_RES, TI, c), lambda s: (s // n_i, 0, s % n_i, col))

    def before(c, col=0):
        return pl.BlockSpec((1, 2, 8, c), lambda s: (jnp.maximum(s - 1, 0) // n_i, N_RES // 2 - 1,
                                                     (jnp.maximum(s - 1, 0) % n_i) * per + per - 1, col))

    def after(c, col=0):
        return pl.BlockSpec((1, 2, 8, c), lambda s: (jnp.minimum(s + 1, nblk - 1) // n_i, 0,
                                                     (jnp.minimum(s + 1, nblk - 1) % n_i) * per, col))

    return nblk, main, before, after


def _shift_in(v, row_in, up):
    rows = v.shape[0]
    idx = lax.broadcasted_iota(jnp.int32, v.shape, 0)
    fill = jnp.broadcast_to(row_in, v.shape)
    if up:
        return jnp.where(idx == rows - 1, fill, pltpu.roll(v, rows - 1, axis=0))
    return jnp.where(idx == 0, fill, pltpu.roll(v, 1, axis=0))


def _taps_behind(u, before):
    s15 = _shift_in(u[N_RES - 1], before[1, 7:8, :], up=False)
    s14 = _shift_in(u[N_RES - 2], before[0, 7:8, :], up=False)
    m1 = jnp.concatenate([s15[None], u[:N_RES - 1]], axis=0)
    m2 = jnp.concatenate([s14[None], s15[None], u[:N_RES - 2]], axis=0)
    return m1, m2


def _taps_ahead(u, after):
    t0 = _shift_in(u[0], after[0, 0:1, :], up=True)
    t1 = _shift_in(u[1], after[1, 0:1, :], up=True)
    p1 = jnp.concatenate([u[1:], t0[None]], axis=0)
    p2 = jnp.concatenate([u[2:], t0[None], t1[None]], axis=0)
    return p1, p2


def _conv_fwd(gates, before, first, cw):
    bg, cg, xc = gates[..., 0:512], gates[..., 512:1024], gates[..., 1024:1536]
    u = cg * xc
    ub = before[..., 512:1024] * before[..., 1024:1536]
    ub = jnp.where(first, jnp.zeros_like(ub), ub)
    m1, m2 = _taps_behind(u, ub)
    conv = m2 * cw[0:1, :] + m1 * cw[1:2, :] + u * cw[2:3, :]
    return bg, u, m1, m2, conv


def _sum_tokens(v):
    return jnp.sum(jnp.sum(v, axis=0), axis=0, keepdims=True)


def _mixer_fwd(x, attn, gates, cw, g_a, g_c, w_out):
    t, d = x.shape
    nblk, main, before, _ = _order_specs(t)
    rows = N_RES * TI

    def body(x_ref, at_ref, gt_ref, gb_ref, cw_ref, ga_ref, gc_ref, wa_ref, wb_ref, x1_ref, mg_ref):
        an = _rms_fwd(at_ref[0], ga_ref[...])[0].astype(BF16)
        bg, _, _, _, conv = _conv_fwd(gt_ref[0], gb_ref[0], pl.program_id(0) == 0, cw_ref[...])
        cn = _rms_fwd(bg * conv, gc_ref[...])[0].astype(BF16)
        mg_ref[0, :, :, 0:512] = an
        mg_ref[0, :, :, 512:1024] = cn
        y = _dot(an.reshape(rows, 512), wa_ref[...]) + _dot(cn.reshape(rows, 512), wb_ref[...])
        x1_ref[0] = x_ref[0] + y.reshape(N_RES, TI, d)

    const = lambda r, c, i0=0: pl.BlockSpec((r, c), lambda s: (i0, 0))
    x1, merged = pl.pallas_call(
        body, name="mixer_fwd", grid=(nblk,),
        in_specs=[main(d), main(512), main(1536, 1), before(1536, 1), const(3, 512), const(1, 512), const(1, 512),
                  const(512, d), const(512, d, 1)],
        out_specs=[main(d), main(d)],
        out_shape=[jax.ShapeDtypeStruct(_x4(x).shape, F32), jax.ShapeDtypeStruct(_x4(x).shape, BF16)],
        compiler_params=_params(("parallel",), 48),
    )(_x4(x), _x4(attn), _x4(gates), _x4(gates), cw, g_a, g_c, w_out, w_out)
    return x1.reshape(t, d), merged.reshape(t, d)


def _mixer_bwd(dx1, attn, gates, cw, g_a, g_c, w_out, head_sum, after=()):
    t, d = dx1.shape
    nblk, main, before, _ = _order_specs(t)
    rows = N_RES * TI

    def body(dx_ref, at_ref, gt_ref, gb_ref, cw_ref, ga_ref, gc_ref, wa_ref, wb_ref, hs_ref,
             da_ref, dsum_ref, dy_ref, gga_ref, ggc_ref):
        s = pl.program_id(0)
        dxb = dx_ref[0].reshape(rows, d).astype(BF16)
        dma = _dot_nt(dxb, wa_ref[...]).reshape(N_RES, TI, 512)
        dmc = _dot_nt(dxb, wb_ref[...]).reshape(N_RES, TI, 512)
        attn_v, g_av = at_ref[0], ga_ref[...]
        _, ah, ra = _rms_fwd(attn_v, g_av)
        dattn = _rms_bwd(dma, ah, ra, g_av)
        da_ref[0] = dattn
        z = (dattn * attn_v).reshape(rows, 512)
        hs = hs_ref[...]
        z1 = z.astype(BF16)
        z2 = (z - z1.astype(F32)).astype(BF16)
        dsum_ref[0] = (_dot(z1, hs) + _dot(z2, hs)).reshape(N_RES, TI, 512)
        bg, _, _, _, conv = _conv_fwd(gt_ref[0], gb_ref[0], s == 0, cw_ref[...])
        g_cv = gc_ref[...]
        _, yh, rc = _rms_fwd(bg * conv, g_cv)
        dy_ref[0] = _rms_bwd(dmc, yh, rc, g_cv)
        pa, pc = _sum_tokens(dma * ah), _sum_tokens(dmc * yh)

        @pl.when(s == 0)
        def _():
            gga_ref[...] = pa
            ggc_ref[...] = pc

        @pl.when(s != 0)
        def _():
            gga_ref[...] += pa
            ggc_ref[...] += pc

    const = lambda r, c, i0=0: pl.BlockSpec((r, c), lambda s: (i0, 0))
    shape4 = _x4(attn).shape
    res, _ = _pcall(
        body, name="mixer_bwd", grid=(nblk,),
        in_specs=[main(d), main(512), main(1536, 1), before(1536, 1), const(3, 512), const(1, 512), const(1, 512),
                  const(512, d), const(512, d, 1), const(512, 512)],
        out_specs=[main(512)] * 3 + [const(1, 512), const(1, 512)],
        out_shape=[jax.ShapeDtypeStruct(shape4, F32)] * 3 + [jax.ShapeDtypeStruct((1, 512), F32)] * 2,
        semantics=("arbitrary",), vmem_mb=48, after=after,
    )(_x4(dx1), _x4(attn), _x4(gates), _x4(gates), cw, g_a, g_c, w_out, w_out, head_sum)
    return [r.reshape(t, 512) for r in res[:3]] + res[3:]


def _conv_bwd(dy, gates, cw):
    t = dy.shape[0]
    nblk, main, before, after = _order_specs(t)
    n_i = SEG // TI

    def body(dy_ref, dya_ref, gt_ref, gb_ref, ga_ref, cw_ref, dp_ref, gcw_ref):
        s = pl.program_id(0)
        cw_v, gates_v = cw_ref[...], gt_ref[0]
        bg, u, m1, m2, conv = _conv_fwd(gates_v, gb_ref[0], s == 0, cw_v)
        dy_v = dy_ref[0]
        dconv = dy_v * bg
        dca = dya_ref[0] * ga_ref[0][..., 0:512]
        dca = jnp.where(s == nblk - 1, jnp.zeros_like(dca), dca)
        p1, p2 = _taps_ahead(dconv, dca)
        du = dconv * cw_v[2:3, :] + p1 * cw_v[1:2, :] + p2 * cw_v[0:1, :]
        dp_ref[0, 0] = (dy_v * conv).astype(BF16)
        dp_ref[1, 0] = (du * gates_v[..., 1024:1536]).astype(BF16)
        dp_ref[2, 0] = (du * gates_v[..., 512:1024]).astype(BF16)
        parts = [_sum_tokens(dconv * m2), _sum_tokens(dconv * m1), _sum_tokens(dconv * u)]

        @pl.when(s == 0)
        def _():
            gcw_ref[...] = jnp.zeros_like(gcw_ref)

        for tap in range(3):
            gcw_ref[tap:tap + 1, :] += parts[tap]

    (dproj, gcw), _ = _pcall(
        body, name="conv_bwd", grid=(nblk,),
        in_specs=[main(512), after(512), main(1536, 1), before(1536, 1), after(1536, 1),
                  pl.BlockSpec((3, 512), lambda s: (0, 0))],
        out_specs=[pl.BlockSpec((3, 1, N_RES, TI, 512), lambda s: (1, s // n_i, 0, s % n_i, 0)),
                   pl.BlockSpec((8, 512), lambda s: (0, 0))],
        out_shape=[jax.ShapeDtypeStruct((6, t // HALF, N_RES, SEG, 512), BF16), jax.ShapeDtypeStruct((8, 512), F32)],
        semantics=("arbitrary",), vmem_mb=40,
    )(_x4(dy), _x4(dy), _x4(gates), _x4(gates), _x4(gates), cw)
    return dproj.reshape(6, t, 512), gcw


def _xattn_fwd(x1, g, w_q, kv, w_o, *, tb):
    t, d = x1.shape
    hd = d // N_MEM_HEADS
    m = kv.shape[0]

    def body(x_ref, g_ref, wq_ref, k_ref, v_ref, wo_ref, x2_ref, h_ref, q_ref, o_ref):
        xv = x_ref[...]
        h = _rms_fwd(xv, g_ref[...])[0].astype(BF16)
        h_ref[...] = h
        q = _dot(h, wq_ref[...]).astype(BF16)
        q_ref[...] = q
        for hh in range(N_MEM_HEADS):
            sl = slice(hh * hd, (hh + 1) * hd)
            s = _dot_nt(q[:, sl], k_ref[:, sl]) * (1.0 / 16.0)
            e = jnp.exp(s - jnp.max(s, axis=1, keepdims=True))
            p = e / jnp.sum(e, axis=1, keepdims=True)
            o_ref[:, sl] = _dot(p.astype(BF16), v_ref[:, sl]).astype(BF16)
        x2_ref[...] = xv + _dot(o_ref[...], wo_ref[...])

    tok = pl.BlockSpec((tb, d), lambda i: (i, 0))
    full = pl.BlockSpec((d, d), lambda i: (0, 0))
    return pl.pallas_call(
        body, name="xattn_fwd", grid=(t // tb,),
        in_specs=[tok, pl.BlockSpec((1, d), lambda i: (0, 0)), full,
                  pl.BlockSpec((m, d), lambda i: (0, 0)), pl.BlockSpec((m, d), lambda i: (0, 1)), full],
        out_specs=[tok] * 4,
        out_shape=[jax.ShapeDtypeStruct((t, d), F32)] + [jax.ShapeDtypeStruct((t, d), BF16)] * 3,
        compiler_params=_params(("parallel",), 48),
    )(x1, g, w_q, kv, kv, w_o)


def _xattn_bwd(dx2, x1, g, q, w_q, kv, w_o, *, tb, after=()):
    t, d = x1.shape
    hd = d // N_MEM_HEADS
    m = kv.shape[0]

    def body(dx2_ref, x_ref, g_ref, q_ref, wq_ref, k_ref, v_ref, wo_ref,
             dx1_ref, dx1b_ref, dq_ref, dk_ref, dv_ref, gg_ref):
        i = pl.program_id(0)

        @pl.when(i == 0)
        def _():
            dk_ref[...] = jnp.zeros_like(dk_ref)
            dv_ref[...] = jnp.zeros_like(dv_ref)

        dx2 = dx2_ref[...]
        do = _dot_nt(dx2.astype(BF16), wo_ref[...]).astype(BF16)
        for hh in range(N_MEM_HEADS):
            sl = slice(hh * hd, (hh + 1) * hd)
            qh, kh, vh, doh = q_ref[:, sl], k_ref[:, sl], v_ref[:, sl], do[:, sl]
            s = _dot_nt(qh, kh) * (1.0 / 16.0)
            e = jnp.exp(s - jnp.max(s, axis=1, keepdims=True))
            p = e / jnp.sum(e, axis=1, keepdims=True)
            dp = _dot_nt(doh, vh)
            ds = (p * (dp - jnp.sum(dp * p, axis=1, keepdims=True)) * (1.0 / 16.0)).astype(BF16)
            dq_ref[:, sl] = _dot(ds, kh).astype(BF16)
            dk_ref[:, sl] += _dot_tn(ds, qh)
            dv_ref[:, sl] += _dot_tn(p.astype(BF16), doh)
        dh = _dot_nt(dq_ref[...], wq_ref[...])
        g_v = g_ref[...]
        _, xh, r = _rms_fwd(x_ref[...], g_v)
        dx1 = dx2 + _rms_bwd(dh, xh, r, g_v)
        dx1_ref[...] = dx1
        dx1b_ref[...] = dx1.astype(BF16)
        part = jnp.sum(dh * xh, axis=0, keepdims=True)

        @pl.when(i == 0)
        def _():
            gg_ref[...] = part

        @pl.when(i != 0)
        def _():
            gg_ref[...] += part

    tok = pl.BlockSpec((tb, d), lambda i: (i, 0))
    full = pl.BlockSpec((d, d), lambda i: (0, 0))
    acc = pl.BlockSpec((m, d), lambda i: (0, 0))
    res, _ = _pcall(
        body, name="xattn_bwd", grid=(t // tb,),
        in_specs=[tok, tok, pl.BlockSpec((1, d), lambda i: (0, 0)), tok, full,
                  pl.BlockSpec((m, d), lambda i: (0, 0)), pl.BlockSpec((m, d), lambda i: (0, 1)), full],
        out_specs=[tok, tok, tok, acc, acc, pl.BlockSpec((1, d), lambda i: (0, 0))],
        out_shape=[jax.ShapeDtypeStruct((t, d), F32), jax.ShapeDtypeStruct((t, d), BF16),
                   jax.ShapeDtypeStruct((t, d), BF16),
                   jax.ShapeDtypeStruct((m, d), F32), jax.ShapeDtypeStruct((m, d), F32),
                   jax.ShapeDtypeStruct((1, d), F32)],
        semantics=("arbitrary",), vmem_mb=48, after=after,
    )(dx2, x1, g, q, w_q, kv, kv, w_o)
    return res


def _mlp_down_loss(a, w_down, x2, tgt, g, *, tb):
    t, d = x2.shape
    f = a.shape[1]

    def body(a_ref, w_ref, x_ref, t_ref, g_ref, dx_ref, dxb_ref, loss_ref, gg_ref):
        i = pl.program_id(0)
        av = a_ref[...].astype(F32)
        x3 = x_ref[...] + _dot((av * av).astype(BF16), w_ref[...])
        g_v = g_ref[...]
        out, xh, r = _rms_fwd(x3, g_v)
        err = out - t_ref[...]
        dout = err * (1.0 / d)
        dx = _rms_bwd(dout, xh, r, g_v)
        dx_ref[...] = dx
        dxb_ref[...] = dx.astype(BF16)
        part = jnp.sum(dout * xh, axis=0, keepdims=True)
        lpart = 0.5 * jnp.sum(jnp.mean(err * err, axis=-1, keepdims=True), axis=0, keepdims=True)
        lpart = jnp.broadcast_to(lpart, loss_ref.shape)

        @pl.when(i == 0)
        def _():
            gg_ref[...] = part
            loss_ref[...] = lpart

        @pl.when(i != 0)
        def _():
            gg_ref[...] += part
            loss_ref[...] += lpart

    tok = pl.BlockSpec((tb, d), lambda i: (i, 0))
    return pl.pallas_call(
        body, name="mlp_down_loss", grid=(t // tb,),
        in_specs=[pl.BlockSpec((tb, f), lambda i: (i, 0)), pl.BlockSpec((f, d), lambda i: (0, 0)), tok, tok,
                  pl.BlockSpec((1, d), lambda i: (0, 0))],
        out_specs=[tok, tok, pl.BlockSpec((8, 128), lambda i: (0, 0)), pl.BlockSpec((1, d), lambda i: (0, 0))],
        out_shape=[jax.ShapeDtypeStruct((t, d), F32), jax.ShapeDtypeStruct((t, d), BF16),
                   jax.ShapeDtypeStruct((8, 128), F32), jax.ShapeDtypeStruct((1, d), F32)],
        compiler_params=_params(("arbitrary",), 56),
    )(a, w_down, x2, tgt, g)


def _mlp_dpre(dx3, w_down, a, *, tb, bn):
    t, d = dx3.shape
    f = a.shape[1]

    def body(dx_ref, w_ref, a_ref, o_ref):
        o_ref[...] = (2.0 * a_ref[...].astype(F32) * _dot_nt(dx_ref[...], w_ref[...])).astype(BF16)

    return pl.pallas_call(
        body, name="mlp_dpre", grid=(t // tb, f // bn),
        in_specs=[pl.BlockSpec((tb, d), lambda i, j: (i, 0)), pl.BlockSpec((bn, d), lambda i, j: (j, 0)),
                  pl.BlockSpec((tb, bn), lambda i, j: (i, j))],
        out_specs=pl.BlockSpec((tb, bn), lambda i, j: (i, j)),
        out_shape=jax.ShapeDtypeStruct((t, f), BF16),
        compiler_params=_params(("parallel", "arbitrary"), 48),
    )(dx3, w_down, a)


def _adamw(gsum, w, m, v):
    m_new = ADAM_B1 * m + (1.0 - ADAM_B1) * gsum
    v_new = ADAM_B2 * v + (1.0 - ADAM_B2) * (gsum * gsum)
    m_hat = m_new / (1.0 - ADAM_B1 ** ADAM_STEP)
    v_hat = v_new / (1.0 - ADAM_B2 ** ADAM_STEP)
    delta = -ADAM_LR * (m_hat / (jnp.sqrt(v_hat) + ADAM_EPS) + ADAM_WD * w)
    return delta, m_new, v_new


def _sum_adamw(parts, w, m, v, *, name, tr):
    r, c = w.shape

    def body(p_ref, w_ref, m_ref, v_ref, g_ref, d_ref, mo_ref, vo_ref):
        g = p_ref[0].astype(F32)
        for k in range(1, N_DEV):
            g = g + p_ref[k].astype(F32)
        g_ref[...] = g
        d_ref[...], mo_ref[...], vo_ref[...] = _adamw(g, w_ref[...], m_ref[...], v_ref[...])

    blk = pl.BlockSpec((tr, c), lambda i: (i, 0))
    return pl.pallas_call(
        body, name=name, grid=(r // tr,),
        in_specs=[pl.BlockSpec((N_DEV, tr, c), lambda i: (0, i, 0)), blk, blk, blk],
        out_specs=[blk] * 4, out_shape=[jax.ShapeDtypeStruct((r, c), F32)] * 4,
        compiler_params=_params(("parallel",), 40),
    )(parts, w, m, v)


def _sum_small(parts):
    _, r, c = parts.shape

    def body(p_ref, o_ref):
        s = p_ref[0]
        for k in range(1, N_DEV):
            s = s + p_ref[k]
        o_ref[...] = s

    return pl.pallas_call(body, name="sum_small", out_shape=jax.ShapeDtypeStruct((r, c), F32))(parts)


def _adamw_small(g, w, m, v):
    def body(g_ref, w_ref, m_ref, v_ref, d_ref, mo_ref, vo_ref):
        d_ref[...], mo_ref[...], vo_ref[...] = _adamw(g_ref[...], w_ref[...], m_ref[...], v_ref[...])

    return pl.pallas_call(body, name="adamw_small", out_shape=[jax.ShapeDtypeStruct(g.shape, F32)] * 3)(g, w, m, v)


def _head_sum_matrix():
    r = lax.broadcasted_iota(jnp.int32, (512, 512), 0) // HEAD_DIM
    c = lax.broadcasted_iota(jnp.int32, (512, 512), 1) // HEAD_DIM
    return (r == c).astype(BF16)


_SHARD_AXIS = dict(w_in=1, w_out=0, w_q=0, w_kv=1, w_o=0, w_up=1, w_down=0, conv_w=None, small=None)


class _Weights:
    def __init__(self, full, shards=None):
        self.full = dict(full)
        self.shards = shards

    def rider(self, names, late=False):
        if self.shards is None:
            return None
        return _Gather([self.shards[n] for n in names], [_SHARD_AXIS[n] for n in names], late)

    def arrived(self, names, gathered):
        if gathered is not None:
            for n, g in zip(names, gathered):
                self.full[n] = g.transpose(1, 0, 2).reshape(g.shape[1], -1) if n == "conv_w" else g

    def __getitem__(self, name):
        return self.full[name]


class _Grads:
    def __init__(self, distributed):
        self.distributed = distributed
        self.local = {}
        self.pending = {}

    def add(self, name, g):
        self.local[name] = g

    def send(self, *names):
        if not self.distributed:
            return []
        rider = _Exchange([self.local[n] for n in names], [_SHARD_AXIS[n] for n in names])
        started = _exchange_start(rider, "send_" + "_".join(names))
        self.pending[names[0]] = (names, rider, started)
        return [started[3]]

    def wait(self, first_name, after):
        names, rider, started = self.pending.pop(first_name)
        return _exchange_wait(rider, started, after, "wait_" + "_".join(names))


def _ride(fn, *args, rider=None, **kw):
    if rider is None:
        return fn(*args, **kw), None
    return fn(*args, rider=rider, **kw)


def _local_step(x, mem, tgt, gains, weights, grads):
    names = ["w_in", "conv_w"]
    (x, tgt), got = _ride(_reorder, [x, tgt], "reorder_in", rider=weights.rider(names, late=True))
    weights.arrived(names, got)
    w_in, cw = weights["w_in"], weights["conv_w"]

    names = ["w_out", "w_kv"]
    (proj, h1), got = _ride(_norm_matmul, x, gains["g_mix"], w_in, name="proj", out_dtype=F32, tb=1024, bn=768,
                            save_h=True, rider=weights.rider(names))
    weights.arrived(names, got)
    names = ["w_q", "w_o", "w_up"]
    (attn, *lses), got = _ride(_attention_fwd, proj, rider=weights.rider(names))
    weights.arrived(names, got)
    x1, merged = _mixer_fwd(x, attn, proj, cw, gains["g_attn_out"], gains["g_conv_out"], weights["w_out"])
    kv, mem_n = _norm_matmul(mem, gains["g_mem"], weights["w_kv"], name="mem_kv", out_dtype=BF16, tb=mem.shape[0],
                             bn=1024, save_h=True)
    x2, h2, qm, om = _xattn_fwd(x1, gains["g_xattn"], weights["w_q"], kv, weights["w_o"], tb=512)
    w_up = weights["w_up"]
    (a, h3), got = _ride(_norm_matmul, x2, gains["g_mlp"], w_up, name="mlp_up", out_dtype=BF16, tb=1024, bn=1024,
                         relu=True, save_h=True, rider=weights.rider(["w_down"], late=True))
    weights.arrived(["w_down"], got)
    w_down = weights["w_down"]
    dx3, dx3b, loss_blk, gg_final = _mlp_down_loss(a, w_down, x2, tgt, gains["g_final"], tb=512)

    dpre = _mlp_dpre(dx3b, w_down, a, tb=1024, bn=1024)
    grads.add("w_down", _matmul_tn(a, dx3b, name="grad_w_down", bm=512, bn=1024, square_a=True))
    sent = grads.send("w_down")
    grads.add("w_up", _matmul_tn(h3, dpre, name="grad_w_up", bm=1024, bn=512, after=sent))
    sent = grads.send("w_up")
    dx2, dx2b, gg_mlp = _matmul_nt_normbwd(dpre, w_up, x2, gains["g_mlp"], dx3, name="mlp_dx", tb=512,
                                           also_bf16=True, after=sent)

    grads.add("w_o", _matmul_tn(om, dx2b, name="grad_w_o", bm=1024, bn=512))
    dx1, dx1b, dqm, dk, dv, gg_xattn = _xattn_bwd(dx2, x1, gains["g_xattn"], qm, weights["w_q"], kv, weights["w_o"],
                                                  tb=512)
    grads.add("w_q", _matmul_tn(h2, dqm, name="grad_w_q", bm=1024, bn=512))
    dkv = jnp.concatenate([dk, dv], axis=1).astype(BF16)
    grads.add("w_kv", _matmul_tn(mem_n, dkv, name="grad_w_kv", bm=1024, bn=1024))
    _, gg_mem = _matmul_nt_normbwd(dkv, weights["w_kv"], mem, gains["g_mem"], None, name="mem_dx", tb=mem.shape[0])

    grads.add("w_out", _matmul_tn(merged, dx1b, name="grad_w_out", bm=1024, bn=512))
    sent = grads.send("w_o", "w_q", "w_kv", "w_out")
    dattn, dsum, dy, gg_attn, gg_conv = _mixer_bwd(dx1, attn, proj, cw, gains["g_attn_out"], gains["g_conv_out"],
                                                   weights["w_out"], _head_sum_matrix(), after=sent)
    dproj, gcw = _conv_bwd(dy, proj, cw)
    dproj = _attention_bwd(proj, dattn, dsum, lses, dproj)
    grads.add("w_in", _matmul_tn(h1, dproj, name="grad_w_in", bm=1024, bn=512))
    sent = grads.send("w_in")
    grad_x, gg_mix = _matmul_nt_normbwd(dproj, w_in, x, gains["g_mix"], dx1, name="mixer_dx", tb=512,
                                        to_natural=True, after=sent)

    def part(v):
        return jnp.pad(v, ((0, SMALL_PART - v.shape[0]), (0, 1024 - v.shape[1])))

    parts = [gg_mix, gg_xattn, gg_mem, gg_mlp, gg_final, jnp.concatenate([gg_attn, gg_conv], axis=1), gcw, loss_blk]
    grads.add("small", jnp.concatenate([part(v) for v in parts], axis=0))
    return grad_x


SMALL_PART = 8
_BIG = ("w_in", "w_out", "w_q", "w_kv", "w_o", "w_up", "w_down")
_GAIN_ROWS = ("g_mix", "g_xattn", "g_mem", "g_mlp", "g_final")


def _pack_small(vals, conv):
    rows = [vals[k].reshape(1, -1) for k in _GAIN_ROWS]
    rows.append(jnp.concatenate([vals["g_attn_out"].reshape(1, -1), vals["g_conv_out"].reshape(1, -1)], axis=1))
    flat = conv.reshape(1, -1)
    rows.append(jnp.pad(flat, ((0, 0), (0, 1024 - flat.shape[1]))))
    rows.append(jnp.zeros((1, 1024), F32))
    return jnp.concatenate(rows, axis=0)


def kernel(x, mem, g_mix, w_in, conv_w, g_attn_out, g_conv_out, w_out, g_xattn, g_mem, w_q_mem, w_kv_mem, w_o_mem, g_mlp, w_up, w_down, g_final, loss_target, m_g_mix, m_w_in, m_conv_w, m_g_attn_out, m_g_conv_out, m_w_out, m_g_xattn, m_g_mem, m_w_q_mem, m_w_kv_mem, m_w_o_mem, m_g_mlp, m_w_up, m_w_down, m_g_final, v_g_mix, v_w_in, v_conv_w, v_g_attn_out, v_g_conv_out, v_w_out, v_g_xattn, v_g_mem, v_w_q_mem, v_w_kv_mem, v_w_o_mem, v_g_mlp, v_w_up, v_w_down, v_g_final):
    d = x.shape[-1]
    me = 4 * lax.axis_index("x") + 2 * lax.axis_index("y") + lax.axis_index("c")
    w_shards = dict(w_in=w_in, w_out=w_out, w_q=w_q_mem, w_kv=w_kv_mem, w_o=w_o_mem, w_up=w_up, w_down=w_down)
    m_shards = dict(w_in=m_w_in, w_out=m_w_out, w_q=m_w_q_mem, w_kv=m_w_kv_mem, w_o=m_w_o_mem, w_up=m_w_up,
                    w_down=m_w_down)
    v_shards = dict(w_in=v_w_in, w_out=v_w_out, w_q=v_w_q_mem, w_kv=v_w_kv_mem, w_o=v_w_o_mem, w_up=v_w_up,
                    w_down=v_w_down)
    gains = dict(g_mix=g_mix, g_attn_out=g_attn_out, g_conv_out=g_conv_out, g_xattn=g_xattn, g_mem=g_mem,
                 g_mlp=g_mlp, g_final=g_final)
    gains2 = {k: v.reshape(1, -1) for k, v in gains.items()}

    shards = {k: w_shards[k].astype(BF16) for k in _BIG}
    shards["conv_w"] = conv_w
    grads = _Grads(distributed=True)
    grad_x = _local_step(x[0], mem[0], loss_target[0], gains2, _Weights({}, shards), grads)

    after = grads.send("small")
    outs = {}
    tiles = dict(w_in=256, w_out=128, w_q=128, w_kv=256, w_o=128, w_up=256, w_down=256)
    for group in (("w_down",), ("w_up",), ("w_o", "w_q", "w_kv", "w_out"), ("w_in",)):
        for k, received in zip(group, grads.wait(group[0], after)):
            outs[k] = _sum_adamw(received, w_shards[k], m_shards[k], v_shards[k], name=f"adamw_{k}", tr=tiles[k])
            after = [outs[k][0]]
    small_received, = grads.wait("small", after)

    ssum = _sum_small(small_received)
    row = lambda i: ssum[SMALL_PART * i]
    loss = ssum[SMALL_PART * 7, 0]
    g_small = {k: row(i) for i, k in enumerate(_GAIN_ROWS)}
    g_small["g_attn_out"] = row(5)[0:512]
    g_small["g_conv_out"] = row(5)[512:1024]
    taps = ssum[SMALL_PART * 6:SMALL_PART * 6 + 3, 0:512]
    g_conv = lax.dynamic_slice_in_dim(taps, me * 64, 64, axis=1)
    m_small = dict(g_mix=m_g_mix, g_attn_out=m_g_attn_out, g_conv_out=m_g_conv_out, g_xattn=m_g_xattn,
                   g_mem=m_g_mem, g_mlp=m_g_mlp, g_final=m_g_final)
    v_small = dict(g_mix=v_g_mix, g_attn_out=v_g_attn_out, g_conv_out=v_g_conv_out, g_xattn=v_g_xattn,
                   g_mem=v_g_mem, g_mlp=v_g_mlp, g_final=v_g_final)
    packed = [_pack_small(g_small, g_conv), _pack_small(gains, conv_w), _pack_small(m_small, m_conv_w),
              _pack_small(v_small, v_conv_w)]
    upd = _adamw_small(*packed)

    def unpack(p):
        res = {k: p[i] for i, k in enumerate(_GAIN_ROWS)}
        res["g_attn_out"] = p[5, 0:512]
        res["g_conv_out"] = p[5, 512:1024]
        res["conv_w"] = p[6, 0:192].reshape(3, 64)
        return res

    g_small["conv_w"] = g_conv
    small_out = [g_small] + [unpack(p) for p in upd]
    names = {"g_mix": "g_mix", "w_in": "w_in", "conv_w": "conv_w", "g_attn_out": "g_attn_out",
             "g_conv_out": "g_conv_out", "w_out": "w_out", "g_xattn": "g_xattn", "g_mem": "g_mem",
             "w_q_mem": "w_q", "w_kv_mem": "w_kv", "w_o_mem": "w_o", "g_mlp": "g_mlp", "w_up": "w_up",
             "w_down": "w_down", "g_final": "g_final"}
    result = [loss, grad_x[None]]
    for which in range(4):
        for key in names.values():
            result.append(outs[key][which] if key in outs else small_out[which][key])
    return tuple(result)
```

```python
import math

import jax
import jax.numpy as jnp
from jax import lax
from jax.experimental import pallas as pl
from jax.experimental.pallas import tpu as pltpu

F32 = jnp.float32
BF16 = jnp.bfloat16
NORM_EPS = 1e-6
NEG_INF = -1e30
N_DEV = 8
BLK = 128
HEAD_DIM = 64
N_MEM_HEADS = 4
ADAM_LR = 0.001
ADAM_B1 = 0.9
ADAM_B2 = 0.999
ADAM_EPS = 1e-08
ADAM_WD = 0.01
ADAM_STEP = 10
MESH = pl.DeviceIdType.MESH
ANY = pl.BlockSpec(memory_space=pl.ANY)


def _dot(a, b):
    return jnp.dot(a, b, preferred_element_type=F32)


def _dot_nt(a, b):
    return lax.dot_general(a, b, (((1,), (1,)), ((), ())), preferred_element_type=F32)


def _dot_tn(a, b):
    return lax.dot_general(a, b, (((0,), (0,)), ((), ())), preferred_element_type=F32)


def _params(semantics, vmem_mb):
    return pltpu.CompilerParams(dimension_semantics=semantics, vmem_limit_bytes=vmem_mb << 20)


def _rms_fwd(x, g):
    r = lax.rsqrt(jnp.mean(x * x, axis=-1, keepdims=True) + NORM_EPS)
    xh = x * r
    return xh * g, xh, r


def _rms_bwd(dy, xh, r, g):
    gy = dy * g
    return r * (gy - xh * jnp.mean(xh * gy, axis=-1, keepdims=True))


def _position():
    x, y, c = lax.axis_index("x"), lax.axis_index("y"), lax.axis_index("c")
    return x, y, c


def _block_of(ref, j, axis, shard_shape):
    r, c = shard_shape
    if axis is None:
        return ref.at[j]
    if axis == 0:
        return ref.at[pl.ds(j * r, r), :]
    return ref.at[:, pl.ds(j * c, c)]


class _Gather:
    has_mid = True
    alias_pairs = ()

    def __init__(self, shards, axes, late=False):
        self.arrays = list(shards)
        self.axes = list(axes)
        self.late = late
        self.n = len(self.arrays)

    def out_shape(self):
        res = []
        for s, axis in zip(self.arrays, self.axes):
            r, c = s.shape
            shape = (N_DEV, r, c) if axis is None else (N_DEV * r, c) if axis == 0 else (r, N_DEV * c)
            res.append(jax.ShapeDtypeStruct(shape, s.dtype))
        return res

    def scratch(self):
        return [pltpu.SemaphoreType.DMA((self.n, 7)), pltpu.SemaphoreType.DMA((self.n, 7)),
                pltpu.SemaphoreType.DMA((self.n,))]

    def _ctx(self, ins, outs, sems):
        send_sems, recv_sems, local_sems = sems
        x, y, c = _position()
        me, sibling = (x, y, c), (x, y, 1 - c)
        chips = [(1 - x, y), (x, 1 - y), (1 - x, 1 - y)]

        def lin(px, py, pc):
            return 4 * px + 2 * py + pc

        def place(a, block):
            return _block_of(outs[a], lin(*block), self.axes[a], self.arrays[a].shape)

        def copy(a, k, block, to, src=None):
            dst = place(a, block)
            return pltpu.make_async_remote_copy(
                src_ref=dst if src is None else src, dst_ref=dst,
                send_sem=send_sems.at[a, k], recv_sem=recv_sems.at[a, k],
                device_id=to, device_id_type=MESH)

        def mine():
            return [pltpu.make_async_copy(ins[a], place(a, me), local_sems.at[a]) for a in range(self.n)]

        def first():
            res = []
            for a in range(self.n):
                res.append(copy(a, 0, me, sibling, src=ins[a]))
                res += [copy(a, 1 + j, me, (*chip, c), src=ins[a]) for j, chip in enumerate(chips)]
            return res

        return c, me, sibling, chips, copy, mine, first

    def start(self, ins, outs, sems):
        _, _, _, _, _, mine, first = self._ctx(ins, outs, sems)
        for cp in mine() + first():
            cp.start()

    def mid(self, ins, outs, sems):
        c, me, sibling, chips, copy, _, _ = self._ctx(ins, outs, sems)
        for j, chip in enumerate(chips):
            for a in range(self.n):
                copy(a, 1 + j, (*chip, c), me).wait_recv()
                copy(a, 4 + j, (*chip, c), sibling).start()

    def finish(self, ins, outs, sems):
        c, me, sibling, chips, copy, mine, first = self._ctx(ins, outs, sems)
        for a in range(self.n):
            copy(a, 0, sibling, me).wait_recv()
            for j, chip in enumerate(chips):
                copy(a, 4 + j, (*chip, 1 - c), me).wait_recv()
        for cp in first():
            cp.wait_send()
        for j, chip in enumerate(chips):
            for a in range(self.n):
                copy(a, 4 + j, (*chip, c), sibling).wait_send()
        for cp in mine():
            cp.wait()


class _Exchange:
    def __init__(self, parts, axes):
        self.n = len(parts)
        self.axes = list(axes)
        self.arrays = list(parts)

    def _piece(self, a):
        r, c = self.arrays[a].shape
        axis = self.axes[a]
        return (r, c) if axis is None else (r // N_DEV, c) if axis == 0 else (r, c // N_DEV)

    def out_shape(self):
        return [jax.ShapeDtypeStruct((N_DEV,) + self._piece(a), self.arrays[a].dtype) for a in range(self.n)]

    def semaphores(self):
        return [pltpu.SemaphoreType.DMA((7 * self.n,)), pltpu.SemaphoreType.DMA((7 * self.n,)),
                pltpu.SemaphoreType.DMA((self.n,))]

    def _ctx(self, ins, outs, sems):
        send_sems, recv_sems, local_sems = sems
        x, y, c = _position()
        me = 4 * x + 2 * y + c

        def src(a, j):
            return ins[a] if self.axes[a] is None else _block_of(ins[a], j, self.axes[a], self._piece(a))

        def dst(a, j):
            return outs[a].at[j]

        def local():
            return [pltpu.make_async_copy(src(a, me), dst(a, me), local_sems.at[a]) for a in range(self.n)]

        def remote(inbound):
            res = []
            for a in range(self.n):
                for k in range(1, N_DEV):
                    peer = (1 - x if k & 4 else x, 1 - y if k & 2 else y, 1 - c if k & 1 else c)
                    plin = 4 * peer[0] + 2 * peer[1] + peer[2]
                    res.append(pltpu.make_async_remote_copy(
                        src_ref=src(a, plin), dst_ref=dst(a, plin if inbound else me),
                        send_sem=send_sems.at[7 * a + k - 1], recv_sem=recv_sems.at[7 * a + k - 1],
                        device_id=peer, device_id_type=MESH))
            return res

        return local, remote

    def start(self, ins, outs, sems):
        local, remote = self._ctx(ins, outs, sems)
        for cp in local() + remote(False):
            cp.start()

    def finish(self, ins, outs, sems):
        local, remote = self._ctx(ins, outs, sems)
        for cp in remote(True):
            cp.wait_recv()
        for cp in remote(False):
            cp.wait_send()
        for cp in local():
            cp.wait()


def _exchange_start(rider, name):
    n = rider.n
    parts = rider.arrays
    lands = [lax.empty(s.shape, s.dtype) for s in rider.out_shape()]
    hbm = pl.BlockSpec(memory_space=pltpu.HBM)
    sem = pl.BlockSpec(memory_space=pltpu.SEMAPHORE)

    def body(*refs):
        ins, sems = refs[:n], refs[2 * n:2 * n + 3]
        outs, token = refs[2 * n + 3 + n:2 * n + 3 + 2 * n], refs[-1]
        rider.start(ins, outs, sems)
        token[...] = jnp.zeros_like(token)

    res = pl.pallas_call(
        body, name=name,
        out_shape=rider.semaphores() + [pltpu.HBM(p.shape, p.dtype) for p in parts]
                  + [pltpu.HBM(z.shape, z.dtype) for z in lands] + [jax.ShapeDtypeStruct((8, 128), F32)],
        in_specs=[hbm] * (2 * n), out_specs=[sem] * 3 + [hbm] * (2 * n) + [pl.BlockSpec(memory_space=pltpu.VMEM)],
        input_output_aliases={i: 3 + i for i in range(2 * n)},
        compiler_params=pltpu.CompilerParams(has_side_effects=pltpu.SideEffectType.DATAFLOW_SIDE_EFFECTING),
    )(*[pltpu.with_memory_space_constraint(a, pltpu.HBM) for a in parts + lands])
    return res[:3], res[3:3 + n], res[3 + n:3 + 2 * n], res[-1]


def _exchange_wait(rider, started, after, name):
    n = rider.n
    sems, parts, lands, _ = started
    hbm = pl.BlockSpec(memory_space=pltpu.HBM)
    sem = pl.BlockSpec(memory_space=pltpu.SEMAPHORE)

    def body(*refs):
        rider.finish(refs[:n], refs[n:2 * n], refs[2 * n:2 * n + 3])

    res = pl.pallas_call(
        body, name=name, out_shape=[pltpu.HBM(a.shape, a.dtype) for a in list(parts) + list(lands)],
        in_specs=[hbm] * (2 * n) + [sem] * 3 + [ANY] * len(after), out_specs=[hbm] * (2 * n),
        input_output_aliases={i: i for i in range(2 * n)},
        compiler_params=pltpu.CompilerParams(has_side_effects=pltpu.SideEffectType.DATAFLOW_SIDE_EFFECTING),
    )(*parts, *lands, *sems, *after)
    return list(res[n:])


def _pcall(body, *, name, grid, in_specs, out_specs, out_shape, scratch_shapes=(), semantics, vmem_mb, rider=None,
           aliases=None, after=()):
    in_specs, out_specs, out_shape = list(in_specs), list(out_specs), list(out_shape)
    scratch_shapes = list(scratch_shapes)
    aliases = dict(aliases or {})
    if rider is None:
        n_in, after = len(in_specs), list(after)

        def plain(*refs):
            body(*refs[:n_in], *refs[n_in + len(after):])

        call = pl.pallas_call(plain if after else body, name=name, grid=grid, in_specs=in_specs + [ANY] * len(after),
                              out_specs=out_specs, out_shape=out_shape, scratch_shapes=scratch_shapes,
                              input_output_aliases=aliases, compiler_params=_params(semantics, vmem_mb))
        return lambda *args: (list(call(*args, *after)), None)
    n_in, n_out, n_scr = len(in_specs), len(out_specs), len(scratch_shapes)
    r_in, r_shapes = len(rider.arrays), rider.out_shape()
    r_out = len(r_shapes)
    aliases.update({n_in + i: n_out + o for i, o in rider.alias_pairs})
    total = math.prod(grid)
    mid_step = total - 1 if rider.has_mid and rider.late else (3 * total) // 4

    def wrapped(*refs):
        bounds = [0, n_in, r_in, n_out, r_out, n_scr]
        for i in range(1, len(bounds)):
            bounds[i] += bounds[i - 1]
        a, ra, o, ro, s = (refs[bounds[i]:bounds[i + 1]] for i in range(5))
        rs = refs[bounds[5]:]
        step = pl.program_id(0)
        for k in range(1, len(grid)):
            step = step * grid[k] + pl.program_id(k)
        pl.when(step == 0)(lambda: rider.start(ra, ro, rs))
        body(*a, *o, *s)
        if rider.has_mid:
            pl.when(step == mid_step)(lambda: rider.mid(ra, ro, rs))
        pl.when(step == total - 1)(lambda: rider.finish(ra, ro, rs))

    call = pl.pallas_call(
        wrapped, name=name, grid=grid, in_specs=in_specs + [ANY] * r_in, out_specs=out_specs + [ANY] * r_out,
        out_shape=out_shape + r_shapes, scratch_shapes=scratch_shapes + rider.scratch(),
        input_output_aliases=aliases, compiler_params=_params(("arbitrary",) * len(grid), vmem_mb))

    def run(*args):
        res = call(*args, *rider.arrays)
        return list(res[:n_out]), list(res[n_out:])

    return run


def _norm_matmul(x, g, w, *, name, out_dtype, tb, bn, relu=False, save_h=False, rider=None):
    t, d = x.shape
    n = w.shape[1]

    def body(x_ref, g_ref, w_ref, o_ref, *rest):
        h_scr = rest[-1]

        @pl.when(pl.program_id(1) == 0)
        def _():
            h = _rms_fwd(x_ref[...], g_ref[...])[0].astype(BF16)
            h_scr[...] = h
            if save_h:
                rest[0][...] = h

        acc = _dot(h_scr[...], w_ref[...])
        if relu:
            acc = jnp.maximum(acc, 0.0)
        o_ref[...] = acc.astype(out_dtype)

    out_shape = [jax.ShapeDtypeStruct((t, n), out_dtype)]
    out_specs = [pl.BlockSpec((tb, bn), lambda i, j: (i, j))]
    if save_h:
        out_shape.append(jax.ShapeDtypeStruct((t, d), BF16))
        out_specs.append(pl.BlockSpec((tb, d), lambda i, j: (i, 0)))
    res, extra = _pcall(
        body, name=name, grid=(t // tb, n // bn),
        in_specs=[pl.BlockSpec((tb, d), lambda i, j: (i, 0)),
                  pl.BlockSpec((1, d), lambda i, j: (0, 0)),
                  pl.BlockSpec((d, bn), lambda i, j: (0, j))],
        out_specs=out_specs, out_shape=out_shape,
        scratch_shapes=[pltpu.VMEM((tb, d), BF16)],
        semantics=("parallel", "arbitrary"), vmem_mb=48, rider=rider,
    )(x, g, w)
    res = res if save_h else res[0]
    return res if rider is None else (res, extra)


def _matmul_nt_normbwd(dy, w, x, g, dres, *, name, tb, also_bf16=False, to_natural=False, after=()):
    t, d = x.shape
    stacked = dy.ndim == 3
    has_res = dres is not None
    n_i = SEG // TI
    if to_natural:
        tb = N_RES * TI

    def body(dy_ref, w_ref, x_ref, g_ref, *rest):
        rest = list(rest)
        dres_ref = rest.pop(0) if has_res else None
        dx_ref = rest.pop(0)
        dxb_ref = rest.pop(0) if also_bf16 else None
        gg_ref = rest.pop(0)
        i = pl.program_id(0)

        def rows(ref, *lead):
            v = ref[lead] if lead else ref[...]
            return v[0].reshape(tb, v.shape[-1]) if to_natural else v

        if stacked:
            kb = dy_ref.shape[-1]
            dh = _dot_nt(rows(dy_ref, 0), w_ref[:, 0:kb])
            for s in range(1, dy_ref.shape[0]):
                dh = dh + _dot_nt(rows(dy_ref, s), w_ref[:, s * kb:(s + 1) * kb])
        else:
            dh = _dot_nt(rows(dy_ref), w_ref[...])
        g_v = g_ref[...]
        _, xh, r = _rms_fwd(rows(x_ref), g_v)
        dx = _rms_bwd(dh, xh, r, g_v)
        if has_res:
            dx = dx + rows(dres_ref)
        if to_natural:
            scr = rest.pop(0)
            for cb in range(d // BLK):
                cols = slice(cb * BLK, (cb + 1) * BLK)
                slab = scr.at[cb]
                for res in range(N_RES):
                    slab[pl.ds(res, TI, stride=N_RES), :] = dx[res * TI:(res + 1) * TI, cols]
                dx_ref[:, cols] = slab[...]
        else:
            dx_ref[...] = dx
        if also_bf16:
            dxb_ref[...] = dx.astype(BF16)
        part = jnp.sum(dh * xh, axis=0, keepdims=True)

        @pl.when(i == 0)
        def _():
            gg_ref[...] = part

        @pl.when(i != 0)
        def _():
            gg_ref[...] += part

    tok = pl.BlockSpec((tb, d), lambda i: (i, 0))
    row = pl.BlockSpec((1, d), lambda i: (0, 0))
    if to_natural:
        act = pl.BlockSpec((1, N_RES, TI, d), lambda i: (i // n_i, 0, i % n_i, 0))
        dy_spec = pl.BlockSpec((dy.shape[0], 1, N_RES, TI, dy.shape[2]), lambda i: (0, i // n_i, 0, i % n_i, 0))
        dy, x = dy.reshape(dy.shape[0], t // HALF, N_RES, SEG, dy.shape[2]), _x4(x)
        dres = _x4(dres) if has_res else None
    elif stacked:
        act, dy_spec = tok, pl.BlockSpec((dy.shape[0], tb, dy.shape[2]), lambda i: (0, i, 0))
    else:
        act, dy_spec = tok, pl.BlockSpec((tb, dy.shape[1]), lambda i: (i, 0))
    in_specs = [dy_spec, pl.BlockSpec(w.shape, lambda i: (0, 0)), act, row]
    args = [dy, w, x, g]
    if has_res:
        in_specs.append(act)
        args.append(dres)
    out_specs = [tok] + ([tok] if also_bf16 else []) + [row]
    out_shape = ([jax.ShapeDtypeStruct((t, d), F32)] + ([jax.ShapeDtypeStruct((t, d), BF16)] if also_bf16 else [])
                 + [jax.ShapeDtypeStruct((1, d), F32)])
    res, _ = _pcall(
        body, name=name, grid=(t // tb,), in_specs=in_specs, out_specs=out_specs, out_shape=out_shape,
        scratch_shapes=[pltpu.VMEM((d // BLK, tb, BLK), F32)] if to_natural else [],
        semantics=("arbitrary",), vmem_mb=56, after=after,
    )(*args)
    return res


def _matmul_tn(a, b, *, name, bm, bn, square_a=False, after=()):
    t, m = a.shape
    stacked = b.ndim == 3
    n = b.shape[0] * bn if stacked else b.shape[1]

    def body(a_ref, b_ref, o_ref):
        av = a_ref[...]
        if square_a:
            av = av.astype(F32)
            av = (av * av).astype(BF16)
        o_ref[...] = _dot_tn(av, b_ref[...]).astype(BF16)

    res, _ = _pcall(
        body, name=name, grid=(m // bm, n // bn),
        in_specs=[pl.BlockSpec((t, bm), lambda i, j: (0, i)),
                  pl.BlockSpec((None, t, bn), lambda i, j: (j, 0, 0)) if stacked
                  else pl.BlockSpec((t, bn), lambda i, j: (0, j))],
        out_specs=[pl.BlockSpec((bm, bn), lambda i, j: (i, j))], out_shape=[jax.ShapeDtypeStruct((m, n), BF16)],
        semantics=("parallel", "parallel"), vmem_mb=56, after=after,
    )(a, b)
    return res[0]


N_RES = 16
SEG = 128
HALF = N_RES * SEG
TI = 32


def _x4(a):
    return a.reshape(a.shape[0] // HALF, N_RES, SEG, a.shape[1])


def _reorder(arrays, name, rider=None):
    t, c = arrays[0].shape
    n = len(arrays)
    n_i = SEG // TI

    def body(*refs):
        scr = refs[-1]
        for i_ref, o_ref in zip(refs[:n], refs[n:2 * n]):
            for cb in range(c // BLK):
                cols = slice(cb * BLK, (cb + 1) * BLK)
                slab = scr.at[cb]
                slab[...] = i_ref[:, cols]
                for r in range(N_RES):
                    o_ref[0, r, :, cols] = slab[pl.ds(r, TI, stride=N_RES), :]

    res, extra = _pcall(
        body, name=name, grid=(t // (TI * N_RES),),
        in_specs=[pl.BlockSpec((TI * N_RES, c), lambda s: (s, 0))] * n,
        out_specs=[pl.BlockSpec((1, N_RES, TI, c), lambda s: (s // n_i, 0, s % n_i, 0))] * n,
        out_shape=[jax.ShapeDtypeStruct((t // HALF, N_RES, SEG, c), F32)] * n,
        scratch_shapes=[pltpu.VMEM((c // BLK, TI * N_RES, BLK), F32)],
        semantics=("parallel",), vmem_mb=32, rider=rider,
    )(*arrays)
    res = [r.reshape(t, c) for r in res]
    return res if rider is None else (res, extra)


_PATTERNS = ((1, 16, 8, SEG), (4, 4, 32, 4 * SEG), (16, 1, SEG, 0))
_FIRST = {1: 1, 4: 4, 16: 16}


def _group_rows(d, g):
    a = g >> 4
    if d == 16:
        base = a * HALF + (g & 15) * SEG
        prev = base - HALF
    elif d == 4:
        c = (g >> 2) & 3
        base = a * HALF + (g & 3) * SEG + c * 32
        prev = jnp.where(c > 0, base - 32, base - HALF + 96)
    else:
        c = g & 15
        base = a * HALF + c * 8
        prev = jnp.where(c > 0, base - 8, base - HALF + 120)
    return base, prev


def _load_rows(ref, base, n, rows, stride):
    parts = [ref[pl.ds(pl.multiple_of(base + j * stride, 8), rows), :] for j in range(n)]
    return parts[0] if n == 1 else jnp.concatenate(parts, axis=0)


def _store_rows(ref, base, val, n, rows, stride, add=False):
    for j in range(n):
        sl = pl.ds(pl.multiple_of(base + j * stride, 8), rows)
        piece = val[j * rows:(j + 1) * rows, :]
        if add:
            ref[sl, :] += piece
        else:
            ref[sl, :] = piece


def _band_bias(n, rows):
    shift = rows.bit_length() - 1
    lq = lax.broadcasted_iota(jnp.int32, (BLK, BLK), 0)
    lk = lax.broadcasted_iota(jnp.int32, (BLK, BLK), 1)
    iq = (lq & (rows - 1)) * n + (lq >> shift)
    ik = (lk & (rows - 1)) * n + (lk >> shift)
    zero = jnp.zeros((BLK, BLK), F32)
    return jnp.where(ik >= iq, zero, NEG_INF), jnp.where(ik <= iq, zero, NEG_INF)


def _set_bias(bias_scr, n, rows):
    prev_b, cur_b = _band_bias(n, rows)
    for half in range(2):
        bias_scr[half * BLK:(half + 1) * BLK, 0:BLK] = prev_b
        bias_scr[half * BLK:(half + 1) * BLK, BLK:2 * BLK] = cur_b


SCALE = 1.0 / math.sqrt(HEAD_DIM)


def _head_consts(value=1.0):
    lane_lo = lax.broadcasted_iota(jnp.int32, (BLK, BLK), 1) < HEAD_DIM
    return lane_lo, [jnp.where(lane_lo, value, 0.0).astype(BF16), jnp.where(lane_lo, 0.0, value).astype(BF16)]


def _stack_heads(v, head_mask):
    return jnp.concatenate([v * head_mask[0], v * head_mask[1]], axis=0)


def _unstack_heads(v2, lane_lo):
    return jnp.where(lane_lo, v2[:BLK], v2[BLK:])


def _rows_per_head(v, lane_lo):
    rolled = pltpu.roll(v, HEAD_DIM, axis=1)
    return jnp.concatenate([jnp.where(lane_lo, v, rolled), jnp.where(lane_lo, rolled, v)], axis=0)


WIDTH = 4


def _loop(lo, hi, fn, width=None):
    if width is None:
        def body(g, carry):
            fn(g)
            return carry

        if hi > lo:
            lax.fori_loop(lo, hi, body, 0)
        return
    while hi > lo:
        trips = (hi - lo) // width
        if trips:
            def body(i, carry, lo=lo, width=width):
                fn([lo + width * i + j for j in range(width)])
                return carry

            lax.fori_loop(0, trips, body, 0)
            lo += trips * width
        width = max(1, width // 2)


def _mix_weights(l1, l2, l3):
    mx = jnp.maximum(jnp.maximum(l1, l2), l3)
    e1, e2, e3 = jnp.exp(l1 - mx), jnp.exp(l2 - mx), jnp.exp(l3 - mx)
    inv = 1.0 / (e1 + e2 + e3)
    return e1 * inv, e2 * inv, e3 * inv


def _attention_fwd(qkv, rider=None):
    t = qkv.shape[0]
    groups = 16 * (t // HALF)

    def body(q_ref, k_ref, v_ref, attn_ref, l1_ref, l2_ref, l3_ref, o_scr, bias_scr):
        lane_lo, q_mask = _head_consts(SCALE)
        l_refs = (l1_ref, l2_ref, l3_ref)
        for p, (d, n, rows, stride) in enumerate(_PATTERNS):
            _set_bias(bias_scr, n, rows)
            o_p, l_p = o_scr.at[p], l_refs[p]

            def block(gs, has_prev):
                at = [_group_rows(d, g) for g in gs]

                def load(ref, b):
                    return _load_rows(ref, b, n, rows, stride).astype(BF16)

                q2 = [_stack_heads(load(q_ref, b), q_mask) for b, _ in at]
                k2 = [load(k_ref, b) for b, _ in at]
                v2 = [load(v_ref, b) for b, _ in at]
                if has_prev:
                    k2 = [jnp.concatenate([load(k_ref, pv), k], axis=0) for (_, pv), k in zip(at, k2)]
                    v2 = [jnp.concatenate([load(v_ref, pv), v], axis=0) for (_, pv), v in zip(at, v2)]
                s = [_dot_nt(q, k) for q, k in zip(q2, k2)]
                s = [x + (bias_scr[...] if has_prev else bias_scr[:, BLK:2 * BLK]) for x in s]
                mx = [jnp.max(x, axis=1, keepdims=True) for x in s]
                e = [jnp.exp(x - m) for x, m in zip(s, mx)]
                den = [jnp.sum(x, axis=1, keepdims=True) for x in e]
                o2 = [_dot(x.astype(BF16), v) * (1.0 / dn) for x, v, dn in zip(e, v2, den)]
                lse2 = [jnp.broadcast_to(m + jnp.log(dn), (2 * BLK, BLK)) for m, dn in zip(mx, den)]
                for (b, _), o, l in zip(at, o2, lse2):
                    _store_rows(o_p, b, _unstack_heads(o, lane_lo), n, rows, stride)
                    _store_rows(l_p, b, _unstack_heads(l, lane_lo), n, rows, stride)

            _loop(0, _FIRST[d], lambda gs: block(gs, False), width=WIDTH)
            _loop(_FIRST[d], groups, lambda gs: block(gs, True), width=WIDTH)

        def mix(i):
            sl = pl.ds(pl.multiple_of(i * 256, 256), 256)
            w = _mix_weights(l1_ref[sl, :], l2_ref[sl, :], l3_ref[sl, :])
            attn_ref[sl, :] = w[0] * o_scr[0, sl, :] + w[1] * o_scr[1, sl, :] + w[2] * o_scr[2, sl, :]

        _loop(0, t // 256, mix)

    def col(c0):
        return pl.BlockSpec((t, BLK), lambda hp: (0, c0 + hp))

    res, extra = _pcall(
        body, name="attention_fwd", grid=(4,), in_specs=[col(0), col(4), col(8)], out_specs=[col(0)] * 4,
        out_shape=[jax.ShapeDtypeStruct((t, 512), F32)] * 4,
        scratch_shapes=[pltpu.VMEM((3, t, BLK), F32), pltpu.VMEM((2 * BLK, 2 * BLK), F32)],
        semantics=("parallel",), vmem_mb=48, rider=rider,
    )(qkv, qkv, qkv)
    return res if rider is None else (res, extra)


def _attention_bwd(qkv, dattn, dsum, lses, dproj):
    t = qkv.shape[0]
    groups = 16 * (t // HALF)

    def body(q_ref, k_ref, v_ref, da_ref, ds_ref, l1_ref, l2_ref, l3_ref, kept_ref, out_ref, acc, bias_scr):
        del kept_ref
        lane_lo, head_mask = _head_consts()
        q_mask = _head_consts(SCALE)[1]
        l_refs = (l1_ref, l2_ref, l3_ref)

        def clear(i):
            sl = pl.ds(pl.multiple_of(i * 512, 512), 512)
            for s in range(3):
                acc[s, sl, :] = jnp.zeros((512, BLK), F32)

        _loop(0, t // 512, clear)
        dq_acc, dk_acc, dv_acc = acc.at[0], acc.at[1], acc.at[2]
        for p, (d, n, rows, stride) in enumerate(_PATTERNS):
            _set_bias(bias_scr, n, rows)

            def block(gs, has_prev):
                at = [_group_rows(d, g) for g in gs]

                def load(ref, b):
                    return _load_rows(ref, b, n, rows, stride)

                def put(ref, b, val):
                    _store_rows(ref, b, val, n, rows, stride, add=True)

                def wide(x):
                    return jnp.concatenate([x, x], axis=1) if has_prev else x

                lse = [[load(ref, b) for ref in l_refs] for b, _ in at]
                w = [_mix_weights(*ls)[p] for ls in lse]
                do2 = [_stack_heads((wg * load(da_ref, b)).astype(BF16), head_mask) for wg, (b, _) in zip(w, at)]
                dl2 = [wide(_rows_per_head(wg * load(ds_ref, b), lane_lo)) for wg, (b, _) in zip(w, at)]
                lse2 = [wide(_rows_per_head(ls[p], lane_lo)) for ls in lse]
                q2 = [_stack_heads(load(q_ref, b).astype(BF16), q_mask) for b, _ in at]
                k2 = [load(k_ref, b).astype(BF16) for b, _ in at]
                v2 = [load(v_ref, b).astype(BF16) for b, _ in at]
                if has_prev:
                    k2 = [jnp.concatenate([load(k_ref, pv).astype(BF16), k], axis=0) for (_, pv), k in zip(at, k2)]
                    v2 = [jnp.concatenate([load(v_ref, pv).astype(BF16), v], axis=0) for (_, pv), v in zip(at, v2)]
                s = [_dot_nt(q, k) for q, k in zip(q2, k2)]
                dp = [_dot_nt(do, v) for do, v in zip(do2, v2)]
                pr = [jnp.exp(x + (bias_scr[...] if has_prev else bias_scr[:, BLK:2 * BLK]) - l)
                      for x, l in zip(s, lse2)]
                ds = [(pg * (x - dl)).astype(BF16) for pg, x, dl in zip(pr, dp, dl2)]
                dq2 = [_dot(x, k) * SCALE for x, k in zip(ds, k2)]
                dk2 = [_dot_tn(x, q) for x, q in zip(ds, q2)]
                dv2 = [_dot_tn(pg.astype(BF16), do) for pg, do in zip(pr, do2)]
                for (b, pv), dq, dk, dv in zip(at, dq2, dk2, dv2):
                    put(dq_acc, b, _unstack_heads(dq, lane_lo))
                    if has_prev:
                        put(dk_acc, pv, dk[:BLK])
                        put(dv_acc, pv, dv[:BLK])
                        put(dk_acc, b, dk[BLK:])
                        put(dv_acc, b, dv[BLK:])
                    else:
                        put(dk_acc, b, dk)
                        put(dv_acc, b, dv)

            _loop(0, _FIRST[d], lambda gs: block(gs, False), width=WIDTH)
            _loop(_FIRST[d], groups, lambda gs: block(gs, True), width=WIDTH)

        def emit(i):
            sl = pl.ds(pl.multiple_of(i * 512, 512), 512)
            for s in range(3):
                out_ref[s, sl, :] = acc[s, sl, :].astype(BF16)

        _loop(0, t // 512, emit)

    def col(c0):
        return pl.BlockSpec((t, BLK), lambda hp: (0, c0 + hp))

    res, _ = _pcall(
        body, name="attention_bwd", grid=(4,),
        in_specs=[col(0), col(4), col(8)] + [col(0)] * 5 + [ANY],
        out_specs=[pl.BlockSpec((3, t, BLK), lambda hp: (0, 0, hp))],
        out_shape=[jax.ShapeDtypeStruct(dproj.shape, BF16)],
        scratch_shapes=[pltpu.VMEM((3, t, BLK), F32), pltpu.VMEM((2 * BLK, 2 * BLK), F32)],
        semantics=("parallel",), vmem_mb=56, aliases={8: 0},
    )(qkv, qkv, qkv, dattn, dsum, *lses, dproj)
    return res[0]


def _order_specs(t):
    n_i = SEG // TI
    nblk = (t // HALF) * n_i
    per = TI // 8

    def main(c, col=0):
        return pl.BlockSpec((1, N_RES, TI, c), lambda s: (s // n_i, 0, s % n_i, col))

    def before(c, col=0):
        return pl.BlockSpec((1, 2, 8, c), lambda s: (jnp.maximum(s - 1, 0) // n_i, N_RES // 2 - 1,
                                                     (jnp.maximum(s - 1, 0) % n_i) * per + per - 1, col))

    def after(c, col=0):
        return pl.BlockSpec((1, 2, 8, c), lambda s: (jnp.minimum(s + 1, nblk - 1) // n_i, 0,
                                                     (jnp.minimum(s + 1, nblk - 1) % n_i) * per, col))

    return nblk, main, before, after


def _shift_in(v, row_in, up):
    rows = v.shape[0]
    idx = lax.broadcasted_iota(jnp.int32, v.shape, 0)
    fill = jnp.broadcast_to(row_in, v.shape)
    if up:
        return jnp.where(idx == rows - 1, fill, pltpu.roll(v, rows - 1, axis=0))
    return jnp.where(idx == 0, fill, pltpu.roll(v, 1, axis=0))


def _taps_behind(u, before):
    s15 = _shift_in(u[N_RES - 1], before[1, 7:8, :], up=False)
    s14 = _shift_in(u[N_RES - 2], before[0, 7:8, :], up=False)
    m1 = jnp.concatenate([s15[None], u[:N_RES - 1]], axis=0)
    m2 = jnp.concatenate([s14[None], s15[None], u[:N_RES - 2]], axis=0)
    return m1, m2


def _taps_ahead(u, after):
    t0 = _shift_in(u[0], after[0, 0:1, :], up=True)
    t1 = _shift_in(u[1], after[1, 0:1, :], up=True)
    p1 = jnp.concatenate([u[1:], t0[None]], axis=0)
    p2 = jnp.concatenate([u[2:], t0[None], t1[None]], axis=0)
    return p1, p2


def _conv_fwd(gates, before, first, cw):
    bg, cg, xc = gates[..., 0:512], gates[..., 512:1024], gates[..., 1024:1536]
    u = cg * xc
    ub = before[..., 512:1024] * before[..., 1024:1536]
    ub = jnp.where(first, jnp.zeros_like(ub), ub)
    m1, m2 = _taps_behind(u, ub)
    conv = m2 * cw[0:1, :] + m1 * cw[1:2, :] + u * cw[2:3, :]
    return bg, u, m1, m2, conv


def _sum_tokens(v):
    return jnp.sum(jnp.sum(v, axis=0), axis=0, keepdims=True)


def _mixer_fwd(x, attn, gates, cw, g_a, g_c, w_out):
    t, d = x.shape
    nblk, main, before, _ = _order_specs(t)
    rows = N_RES * TI

    def body(x_ref, at_ref, gt_ref, gb_ref, cw_ref, ga_ref, gc_ref, wa_ref, wb_ref, x1_ref, mg_ref):
        an = _rms_fwd(at_ref[0], ga_ref[...])[0].astype(BF16)
        bg, _, _, _, conv = _conv_fwd(gt_ref[0], gb_ref[0], pl.program_id(0) == 0, cw_ref[...])
        cn = _rms_fwd(bg * conv, gc_ref[...])[0].astype(BF16)
        mg_ref[0, :, :, 0:512] = an
        mg_ref[0, :, :, 512:1024] = cn
        y = _dot(an.reshape(rows, 512), wa_ref[...]) + _dot(cn.reshape(rows, 512), wb_ref[...])
        x1_ref[0] = x_ref[0] + y.reshape(N_RES, TI, d)

    const = lambda r, c, i0=0: pl.BlockSpec((r, c), lambda s: (i0, 0))
    x1, merged = pl.pallas_call(
        body, name="mixer_fwd", grid=(nblk,),
        in_specs=[main(d), main(512), main(1536, 1), before(1536, 1), const(3, 512), const(1, 512), const(1, 512),
                  const(512, d), const(512, d, 1)],
        out_specs=[main(d), main(d)],
        out_shape=[jax.ShapeDtypeStruct(_x4(x).shape, F32), jax.ShapeDtypeStruct(_x4(x).shape, BF16)],
        compiler_params=_params(("parallel",), 48),
    )(_x4(x), _x4(attn), _x4(gates), _x4(gates), cw, g_a, g_c, w_out, w_out)
    return x1.reshape(t, d), merged.reshape(t, d)


def _mixer_bwd(dx1, attn, gates, cw, g_a, g_c, w_out, head_sum, after=()):
    t, d = dx1.shape
    nblk, main, before, _ = _order_specs(t)
    rows = N_RES * TI

    def body(dx_ref, at_ref, gt_ref, gb_ref, cw_ref, ga_ref, gc_ref, wa_ref, wb_ref, hs_ref,
             da_ref, dsum_ref, dy_ref, gga_ref, ggc_ref):
        s = pl.program_id(0)
        dxb = dx_ref[0].reshape(rows, d).astype(BF16)
        dma = _dot_nt(dxb, wa_ref[...]).reshape(N_RES, TI, 512)
        dmc = _dot_nt(dxb, wb_ref[...]).reshape(N_RES, TI, 512)
        attn_v, g_av = at_ref[0], ga_ref[...]
        _, ah, ra = _rms_fwd(attn_v, g_av)
        dattn = _rms_bwd(dma, ah, ra, g_av)
        da_ref[0] = dattn
        z = (dattn * attn_v).reshape(rows, 512)
        hs = hs_ref[...]
        z1 = z.astype(BF16)
        z2 = (z - z1.astype(F32)).astype(BF16)
        dsum_ref[0] = (_dot(z1, hs) + _dot(z2, hs)).reshape(N_RES, TI, 512)
        bg, _, _, _, conv = _conv_fwd(gt_ref[0], gb_ref[0], s == 0, cw_ref[...])
        g_cv = gc_ref[...]
        _, yh, rc = _rms_fwd(bg * conv, g_cv)
        dy_ref[0] = _rms_bwd(dmc, yh, rc, g_cv)
        pa, pc = _sum_tokens(dma * ah), _sum_tokens(dmc * yh)

        @pl.when(s == 0)
        def _():
            gga_ref[...] = pa
            ggc_ref[...] = pc

        @pl.when(s != 0)
        def _():
            gga_ref[...] += pa
            ggc_ref[...] += pc

    const = lambda r, c, i0=0: pl.BlockSpec((r, c), lambda s: (i0, 0))
    shape4 = _x4(attn).shape
    res, _ = _pcall(
        body, name="mixer_bwd", grid=(nblk,),
        in_specs=[main(d), main(512), main(1536, 1), before(1536, 1), const(3, 512), const(1, 512), const(1, 512),
                  const(512, d), const(512, d, 1), const(512, 512)],
        out_specs=[main(512)] * 3 + [const(1, 512), const(1, 512)],
        out_shape=[jax.ShapeDtypeStruct(shape4, F32)] * 3 + [jax.ShapeDtypeStruct((1, 512), F32)] * 2,
        semantics=("arbitrary",), vmem_mb=48, after=after,
    )(_x4(dx1), _x4(attn), _x4(gates), _x4(gates), cw, g_a, g_c, w_out, w_out, head_sum)
    return [r.reshape(t, 512) for r in res[:3]] + res[3:]


def _conv_bwd(dy, gates, cw):
    t = dy.shape[0]
    nblk, main, before, after = _order_specs(t)
    n_i = SEG // TI

    def body(dy_ref, dya_ref, gt_ref, gb_ref, ga_ref, cw_ref, dp_ref, gcw_ref):
        s = pl.program_id(0)
        cw_v, gates_v = cw_ref[...], gt_ref[0]
        bg, u, m1, m2, conv = _conv_fwd(gates_v, gb_ref[0], s == 0, cw_v)
        dy_v = dy_ref[0]
        dconv = dy_v * bg
        dca = dya_ref[0] * ga_ref[0][..., 0:512]
        dca = jnp.where(s == nblk - 1, jnp.zeros_like(dca), dca)
        p1, p2 = _taps_ahead(dconv, dca)
        du = dconv * cw_v[2:3, :] + p1 * cw_v[1:2, :] + p2 * cw_v[0:1, :]
        dp_ref[0, 0] = (dy_v * conv).astype(BF16)
        dp_ref[1, 0] = (du * gates_v[..., 1024:1536]).astype(BF16)
        dp_ref[2, 0] = (du * gates_v[..., 512:1024]).astype(BF16)
        parts = [_sum_tokens(dconv * m2), _sum_tokens(dconv * m1), _sum_tokens(dconv * u)]

        @pl.when(s == 0)
        def _():
            gcw_ref[...] = jnp.zeros_like(gcw_ref)

        for tap in range(3):
            gcw_ref[tap:tap + 1, :] += parts[tap]

    (dproj, gcw), _ = _pcall(
        body, name="conv_bwd", grid=(nblk,),
        in_specs=[main(512), after(512), main(1536, 1), before(1536, 1), after(1536, 1),
                  pl.BlockSpec((3, 512), lambda s: (0, 0))],
        out_specs=[pl.BlockSpec((3, 1, N_RES, TI, 512), lambda s: (1, s // n_i, 0, s % n_i, 0)),
                   pl.BlockSpec((8, 512), lambda s: (0, 0))],
        out_shape=[jax.ShapeDtypeStruct((6, t // HALF, N_RES, SEG, 512), BF16), jax.ShapeDtypeStruct((8, 512), F32)],
        semantics=("arbitrary",), vmem_mb=40,
    )(_x4(dy), _x4(dy), _x4(gates), _x4(gates), _x4(gates), cw)
    return dproj.reshape(6, t, 512), gcw


def _xattn_fwd(x1, g, w_q, kv, w_o, *, tb):
    t, d = x1.shape
    hd = d // N_MEM_HEADS
    m = kv.shape[0]

    def body(x_ref, g_ref, wq_ref, k_ref, v_ref, wo_ref, x2_ref, h_ref, q_ref, o_ref):
        xv = x_ref[...]
        h = _rms_fwd(xv, g_ref[...])[0].astype(BF16)
        h_ref[...] = h
        q = _dot(h, wq_ref[...]).astype(BF16)
        q_ref[...] = q
        for hh in range(N_MEM_HEADS):
            sl = slice(hh * hd, (hh + 1) * hd)
            s = _dot_nt(q[:, sl], k_ref[:, sl]) * (1.0 / 16.0)
            e = jnp.exp(s - jnp.max(s, axis=1, keepdims=True))
            p = e / jnp.sum(e, axis=1, keepdims=True)
            o_ref[:, sl] = _dot(p.astype(BF16), v_ref[:, sl]).astype(BF16)
        x2_ref[...] = xv + _dot(o_ref[...], wo_ref[...])

    tok = pl.BlockSpec((tb, d), lambda i: (i, 0))
    full = pl.BlockSpec((d, d), lambda i: (0, 0))
    return pl.pallas_call(
        body, name="xattn_fwd", grid=(t // tb,),
        in_specs=[tok, pl.BlockSpec((1, d), lambda i: (0, 0)), full,
                  pl.BlockSpec((m, d), lambda i: (0, 0)), pl.BlockSpec((m, d), lambda i: (0, 1)), full],
        out_specs=[tok] * 4,
        out_shape=[jax.ShapeDtypeStruct((t, d), F32)] + [jax.ShapeDtypeStruct((t, d), BF16)] * 3,
        compiler_params=_params(("parallel",), 48),
    )(x1, g, w_q, kv, kv, w_o)


def _xattn_bwd(dx2, x1, g, q, w_q, kv, w_o, *, tb, after=()):
    t, d = x1.shape
    hd = d // N_MEM_HEADS
    m = kv.shape[0]

    def body(dx2_ref, x_ref, g_ref, q_ref, wq_ref, k_ref, v_ref, wo_ref,
             dx1_ref, dx1b_ref, dq_ref, dk_ref, dv_ref, gg_ref):
        i = pl.program_id(0)

        @pl.when(i == 0)
        def _():
            dk_ref[...] = jnp.zeros_like(dk_ref)
            dv_ref[...] = jnp.zeros_like(dv_ref)

        dx2 = dx2_ref[...]
        do = _dot_nt(dx2.astype(BF16), wo_ref[...]).astype(BF16)
        for hh in range(N_MEM_HEADS):
            sl = slice(hh * hd, (hh + 1) * hd)
            qh, kh, vh, doh = q_ref[:, sl], k_ref[:, sl], v_ref[:, sl], do[:, sl]
            s = _dot_nt(qh, kh) * (1.0 / 16.0)
            e = jnp.exp(s - jnp.max(s, axis=1, keepdims=True))
            p = e / jnp.sum(e, axis=1, keepdims=True)
            dp = _dot_nt(doh, vh)
            ds = (p * (dp - jnp.sum(dp * p, axis=1, keepdims=True)) * (1.0 / 16.0)).astype(BF16)
            dq_ref[:, sl] = _dot(ds, kh).astype(BF16)
            dk_ref[:, sl] += _dot_tn(ds, qh)
            dv_ref[:, sl] += _dot_tn(p.astype(BF16), doh)
        dh = _dot_nt(dq_ref[...], wq_ref[...])
        g_v = g_ref[...]
        _, xh, r = _rms_fwd(x_ref[...], g_v)
        dx1 = dx2 + _rms_bwd(dh, xh, r, g_v)
        dx1_ref[...] = dx1
        dx1b_ref[...] = dx1.astype(BF16)
        part = jnp.sum(dh * xh, axis=0, keepdims=True)

        @pl.when(i == 0)
        def _():
            gg_ref[...] = part

        @pl.when(i != 0)
        def _():
            gg_ref[...] += part

    tok = pl.BlockSpec((tb, d), lambda i: (i, 0))
    full = pl.BlockSpec((d, d), lambda i: (0, 0))
    acc = pl.BlockSpec((m, d), lambda i: (0, 0))
    res, _ = _pcall(
        body, name="xattn_bwd", grid=(t // tb,),
        in_specs=[tok, tok, pl.BlockSpec((1, d), lambda i: (0, 0)), tok, full,
                  pl.BlockSpec((m, d), lambda i: (0, 0)), pl.BlockSpec((m, d), lambda i: (0, 1)), full],
        out_specs=[tok, tok, tok, acc, acc, pl.BlockSpec((1, d), lambda i: (0, 0))],
        out_shape=[jax.ShapeDtypeStruct((t, d), F32), jax.ShapeDtypeStruct((t, d), BF16),
                   jax.ShapeDtypeStruct((t, d), BF16),
                   jax.ShapeDtypeStruct((m, d), F32), jax.ShapeDtypeStruct((m, d), F32),
                   jax.ShapeDtypeStruct((1, d), F32)],
        semantics=("arbitrary",), vmem_mb=48, after=after,
    )(dx2, x1, g, q, w_q, kv, kv, w_o)
    return res


def _mlp_down_loss(a, w_down, x2, tgt, g, *, tb):
    t, d = x2.shape
    f = a.shape[1]

    def body(a_ref, w_ref, x_ref, t_ref, g_ref, dx_ref, dxb_ref, loss_ref, gg_ref):
        i = pl.program_id(0)
        av = a_ref[...].astype(F32)
        x3 = x_ref[...] + _dot((av * av).astype(BF16), w_ref[...])
        g_v = g_ref[...]
        out, xh, r = _rms_fwd(x3, g_v)
        err = out - t_ref[...]
        dout = err * (1.0 / d)
        dx = _rms_bwd(dout, xh, r, g_v)
        dx_ref[...] = dx
        dxb_ref[...] = dx.astype(BF16)
        part = jnp.sum(dout * xh, axis=0, keepdims=True)
        lpart = 0.5 * jnp.sum(jnp.mean(err * err, axis=-1, keepdims=True), axis=0, keepdims=True)
        lpart = jnp.broadcast_to(lpart, loss_ref.shape)

        @pl.when(i == 0)
        def _():
            gg_ref[...] = part
            loss_ref[...] = lpart

        @pl.when(i != 0)
        def _():
            gg_ref[...] += part
            loss_ref[...] += lpart

    tok = pl.BlockSpec((tb, d), lambda i: (i, 0))
    return pl.pallas_call(
        body, name="mlp_down_loss", grid=(t // tb,),
        in_specs=[pl.BlockSpec((tb, f), lambda i: (i, 0)), pl.BlockSpec((f, d), lambda i: (0, 0)), tok, tok,
                  pl.BlockSpec((1, d), lambda i: (0, 0))],
        out_specs=[tok, tok, pl.BlockSpec((8, 128), lambda i: (0, 0)), pl.BlockSpec((1, d), lambda i: (0, 0))],
        out_shape=[jax.ShapeDtypeStruct((t, d), F32), jax.ShapeDtypeStruct((t, d), BF16),
                   jax.ShapeDtypeStruct((8, 128), F32), jax.ShapeDtypeStruct((1, d), F32)],
        compiler_params=_params(("arbitrary",), 56),
    )(a, w_down, x2, tgt, g)


def _mlp_dpre(dx3, w_down, a, *, tb, bn):
    t, d = dx3.shape
    f = a.shape[1]

    def body(dx_ref, w_ref, a_ref, o_ref):
        o_ref[...] = (2.0 * a_ref[...].astype(F32) * _dot_nt(dx_ref[...], w_ref[...])).astype(BF16)

    return pl.pallas_call(
        body, name="mlp_dpre", grid=(t // tb, f // bn),
        in_specs=[pl.BlockSpec((tb, d), lambda i, j: (i, 0)), pl.BlockSpec((bn, d), lambda i, j: (j, 0)),
                  pl.BlockSpec((tb, bn), lambda i, j: (i, j))],
        out_specs=pl.BlockSpec((tb, bn), lambda i, j: (i, j)),
        out_shape=jax.ShapeDtypeStruct((t, f), BF16),
        compiler_params=_params(("parallel", "arbitrary"), 48),
    )(dx3, w_down, a)


def _adamw(gsum, w, m, v):
    m_new = ADAM_B1 * m + (1.0 - ADAM_B1) * gsum
    v_new = ADAM_B2 * v + (1.0 - ADAM_B2) * (gsum * gsum)
    m_hat = m_new / (1.0 - ADAM_B1 ** ADAM_STEP)
    v_hat = v_new / (1.0 - ADAM_B2 ** ADAM_STEP)
    delta = -ADAM_LR * (m_hat / (jnp.sqrt(v_hat) + ADAM_EPS) + ADAM_WD * w)
    return delta, m_new, v_new


def _sum_adamw(parts, w, m, v, *, name, tr):
    r, c = w.shape

    def body(p_ref, w_ref, m_ref, v_ref, g_ref, d_ref, mo_ref, vo_ref):
        g = p_ref[0].astype(F32)
        for k in range(1, N_DEV):
            g = g + p_ref[k].astype(F32)
        g_ref[...] = g
        d_ref[...], mo_ref[...], vo_ref[...] = _adamw(g, w_ref[...], m_ref[...], v_ref[...])

    blk = pl.BlockSpec((tr, c), lambda i: (i, 0))
    return pl.pallas_call(
        body, name=name, grid=(r // tr,),
        in_specs=[pl.BlockSpec((N_DEV, tr, c), lambda i: (0, i, 0)), blk, blk, blk],
        out_specs=[blk] * 4, out_shape=[jax.ShapeDtypeStruct((r, c), F32)] * 4,
        compiler_params=_params(("parallel",), 40),
    )(parts, w, m, v)


def _sum_small(parts):
    _, r, c = parts.shape

    def body(p_ref, o_ref):
        s = p_ref[0]
        for k in range(1, N_DEV):
            s = s + p_ref[k]
        o_ref[...] = s

    return pl.pallas_call(body, name="sum_small", out_shape=jax.ShapeDtypeStruct((r, c), F32))(parts)


def _adamw_small(g, w, m, v):
    def body(g_ref, w_ref, m_ref, v_ref, d_ref, mo_ref, vo_ref):
        d_ref[...], mo_ref[...], vo_ref[...] = _adamw(g_ref[...], w_ref[...], m_ref[...], v_ref[...])

    return pl.pallas_call(body, name="adamw_small", out_shape=[jax.ShapeDtypeStruct(g.shape, F32)] * 3)(g, w, m, v)


def _head_sum_matrix():
    r = lax.broadcasted_iota(jnp.int32, (512, 512), 0) // HEAD_DIM
    c = lax.broadcasted_iota(jnp.int32, (512, 512), 1) // HEAD_DIM
    return (r == c).astype(BF16)


_SHARD_AXIS = dict(w_in=1, w_out=0, w_q=0, w_kv=1, w_o=0, w_up=1, w_down=0, conv_w=None, small=None)


class _Weights:
    def __init__(self, full, shards=None):
        self.full = dict(full)
        self.shards = shards

    def rider(self, names, late=False):
        if self.shards is None:
            return None
        return _Gather([self.shards[n] for n in names], [_SHARD_AXIS[n] for n in names], late)

    def arrived(self, names, gathered):
        if gathered is not None:
            for n, g in zip(names, gathered):
                self.full[n] = g.transpose(1, 0, 2).reshape(g.shape[1], -1) if n == "conv_w" else g

    def __getitem__(self, name):
        return self.full[name]


class _Grads:
    def __init__(self, distributed):
        self.distributed = distributed
        self.local = {}
        self.pending = {}

    def add(self, name, g):
        self.local[name] = g

    def send(self, *names):
        if not self.distributed:
            return []
        rider = _Exchange([self.local[n] for n in names], [_SHARD_AXIS[n] for n in names])
        started = _exchange_start(rider, "send_" + "_".join(names))
        self.pending[names[0]] = (names, rider, started)
        return [started[3]]

    def wait(self, first_name, after):
        names, rider, started = self.pending.pop(first_name)
        return _exchange_wait(rider, started, after, "wait_" + "_".join(names))


def _ride(fn, *args, rider=None, **kw):
    if rider is None:
        return fn(*args, **kw), None
    return fn(*args, rider=rider, **kw)


def _local_step(x, mem, tgt, gains, weights, grads):
    names = ["w_in", "conv_w"]
    (x, tgt), got = _ride(_reorder, [x, tgt], "reorder_in", rider=weights.rider(names, late=True))
    weights.arrived(names, got)
    w_in, cw = weights["w_in"], weights["conv_w"]

    names = ["w_out"]
    (proj, h1), got = _ride(_norm_matmul, x, gains["g_mix"], w_in, name="proj", out_dtype=F32, tb=1024, bn=768,
                            save_h=True, rider=weights.rider(names))
    weights.arrived(names, got)
    names = ["w_kv", "w_q", "w_o", "w_up"]
    (attn, *lses), got = _ride(_attention_fwd, proj, rider=weights.rider(names))
    weights.arrived(names, got)
    x1, merged = _mixer_fwd(x, attn, proj, cw, gains["g_attn_out"], gains["g_conv_out"], weights["w_out"])
    kv, mem_n = _norm_matmul(mem, gains["g_mem"], weights["w_kv"], name="mem_kv", out_dtype=BF16, tb=mem.shape[0],
                             bn=1024, save_h=True)
    x2, h2, qm, om = _xattn_fwd(x1, gains["g_xattn"], weights["w_q"], kv, weights["w_o"], tb=512)
    w_up = weights["w_up"]
    (a, h3), got = _ride(_norm_matmul, x2, gains["g_mlp"], w_up, name="mlp_up", out_dtype=BF16, tb=1024, bn=1024,
                         relu=True, save_h=True, rider=weights.rider(["w_down"], late=True))
    weights.arrived(["w_down"], got)
    w_down = weights["w_down"]
    dx3, dx3b, loss_blk, gg_final = _mlp_down_loss(a, w_down, x2, tgt, gains["g_final"], tb=512)

    dpre = _mlp_dpre(dx3b, w_down, a, tb=1024, bn=2048)
    grads.add("w_down", _matmul_tn(a, dx3b, name="grad_w_down", bm=512, bn=1024, square_a=True))
    sent = grads.send("w_down")
    grads.add("w_up", _matmul_tn(h3, dpre, name="grad_w_up", bm=1024, bn=512, after=sent))
    sent = grads.send("w_up")
    dx2, dx2b, gg_mlp = _matmul_nt_normbwd(dpre, w_up, x2, gains["g_mlp"], dx3, name="mlp_dx", tb=512,
                                           also_bf16=True, after=sent)

    grads.add("w_o", _matmul_tn(om, dx2b, name="grad_w_o", bm=1024, bn=512))
    dx1, dx1b, dqm, dk, dv, gg_xattn = _xattn_bwd(dx2, x1, gains["g_xattn"], qm, weights["w_q"], kv, weights["w_o"],
                                                  tb=512)
    grads.add("w_q", _matmul_tn(h2, dqm, name="grad_w_q", bm=1024, bn=512))
    dkv = jnp.concatenate([dk, dv], axis=1).astype(BF16)
    grads.add("w_kv", _matmul_tn(mem_n, dkv, name="grad_w_kv", bm=1024, bn=1024))
    _, gg_mem = _matmul_nt_normbwd(dkv, weights["w_kv"], mem, gains["g_mem"], None, name="mem_dx", tb=mem.shape[0])

    grads.add("w_out", _matmul_tn(merged, dx1b, name="grad_w_out", bm=1024, bn=512))
    sent = grads.send("w_o", "w_q", "w_kv", "w_out")
    dattn, dsum, dy, gg_attn, gg_conv = _mixer_bwd(dx1, attn, proj, cw, gains["g_attn_out"], gains["g_conv_out"],
                                                   weights["w_out"], _head_sum_matrix(), after=sent)
    dproj, gcw = _conv_bwd(dy, proj, cw)
    dproj = _attention_bwd(proj, dattn, dsum, lses, dproj)
    grads.add("w_in", _matmul_tn(h1, dproj, name="grad_w_in", bm=1024, bn=512))
    sent = grads.send("w_in")
    grad_x, gg_mix = _matmul_nt_normbwd(dproj, w_in, x, gains["g_mix"], dx1, name="mixer_dx", tb=512,
                                        to_natural=True, after=sent)

    def part(v):
        return jnp.pad(v, ((0, SMALL_PART - v.shape[0]), (0, 1024 - v.shape[1])))

    parts = [gg_mix, gg_xattn, gg_mem, gg_mlp, gg_final, jnp.concatenate([gg_attn, gg_conv], axis=1), gcw, loss_blk]
    grads.add("small", jnp.concatenate([part(v) for v in parts], axis=0))
    return grad_x


SMALL_PART = 8
_BIG = ("w_in", "w_out", "w_q", "w_kv", "w_o", "w_up", "w_down")
_GAIN_ROWS = ("g_mix", "g_xattn", "g_mem", "g_mlp", "g_final")


def _pack_small(vals, conv):
    rows = [vals[k].reshape(1, -1) for k in _GAIN_ROWS]
    rows.append(jnp.concatenate([vals["g_attn_out"].reshape(1, -1), vals["g_conv_out"].reshape(1, -1)], axis=1))
    flat = conv.reshape(1, -1)
    rows.append(jnp.pad(flat, ((0, 0), (0, 1024 - flat.shape[1]))))
    rows.append(jnp.zeros((1, 1024), F32))
    return jnp.concatenate(rows, axis=0)


def kernel(x, mem, g_mix, w_in, conv_w, g_attn_out, g_conv_out, w_out, g_xattn, g_mem, w_q_mem, w_kv_mem, w_o_mem, g_mlp, w_up, w_down, g_final, loss_target, m_g_mix, m_w_in, m_conv_w, m_g_attn_out, m_g_conv_out, m_w_out, m_g_xattn, m_g_mem, m_w_q_mem, m_w_kv_mem, m_w_o_mem, m_g_mlp, m_w_up, m_w_down, m_g_final, v_g_mix, v_w_in, v_conv_w, v_g_attn_out, v_g_conv_out, v_w_out, v_g_xattn, v_g_mem, v_w_q_mem, v_w_kv_mem, v_w_o_mem, v_g_mlp, v_w_up, v_w_down, v_g_final):
    d = x.shape[-1]
    me = 4 * lax.axis_index("x") + 2 * lax.axis_index("y") + lax.axis_index("c")
    w_shards = dict(w_in=w_in, w_out=w_out, w_q=w_q_mem, w_kv=w_kv_mem, w_o=w_o_mem, w_up=w_up, w_down=w_down)
    m_shards = dict(w_in=m_w_in, w_out=m_w_out, w_q=m_w_q_mem, w_kv=m_w_kv_mem, w_o=m_w_o_mem, w_up=m_w_up,
                    w_down=m_w_down)
    v_shards = dict(w_in=v_w_in, w_out=v_w_out, w_q=v_w_q_mem, w_kv=v_w_kv_mem, w_o=v_w_o_mem, w_up=v_w_up,
                    w_down=v_w_down)
    gains = dict(g_mix=g_mix, g_attn_out=g_attn_out, g_conv_out=g_conv_out, g_xattn=g_xattn, g_mem=g_mem,
                 g_mlp=g_mlp, g_final=g_final)
    gains2 = {k: v.reshape(1, -1) for k, v in gains.items()}

    shards = {k: w_shards[k].astype(BF16) for k in _BIG}
    shards["conv_w"] = conv_w
    grads = _Grads(distributed=True)
    grad_x = _local_step(x[0], mem[0], loss_target[0], gains2, _Weights({}, shards), grads)

    after = grads.send("small")
    outs = {}
    tiles = dict(w_in=256, w_out=128, w_q=128, w_kv=256, w_o=128, w_up=256, w_down=256)
    for group in (("w_down",), ("w_up",), ("w_o", "w_q", "w_kv", "w_out"), ("w_in",)):
        for k, received in zip(group, grads.wait(group[0], after)):
            outs[k] = _sum_adamw(received, w_shards[k], m_shards[k], v_shards[k], name=f"adamw_{k}", tr=tiles[k])
            after = [outs[k][0]]
    small_received, = grads.wait("small", after)

    ssum = _sum_small(small_received)
    row = lambda i: ssum[SMALL_PART * i]
    loss = ssum[SMALL_PART * 7, 0]
    g_small = {k: row(i) for i, k in enumerate(_GAIN_ROWS)}
    g_small["g_attn_out"] = row(5)[0:512]
    g_small["g_conv_out"] = row(5)[512:1024]
    taps = ssum[SMALL_PART * 6:SMALL_PART * 6 + 3, 0:512]
    g_conv = lax.dynamic_slice_in_dim(taps, me * 64, 64, axis=1)
    m_small = dict(g_mix=m_g_mix, g_attn_out=m_g_attn_out, g_conv_out=m_g_conv_out, g_xattn=m_g_xattn,
                   g_mem=m_g_mem, g_mlp=m_g_mlp, g_final=m_g_final)
    v_small = dict(g_mix=v_g_mix, g_attn_out=v_g_attn_out, g_conv_out=v_g_conv_out, g_xattn=v_g_xattn,
                   g_mem=v_g_mem, g_mlp=v_g_mlp, g_final=v_g_final)
    packed = [_pack_small(g_small, g_conv), _pack_small(gains, conv_w), _pack_small(m_small, m_conv_w),
              _pack_small(v_small, v_conv_w)]
    upd = _adamw_small(*packed)

    def unpack(p):
        res = {k: p[i] for i, k in enumerate(_GAIN_ROWS)}
        res["g_attn_out"] = p[5, 0:512]
        res["g_conv_out"] = p[5, 512:1024]
        res["conv_w"] = p[6, 0:192].reshape(3, 64)
        return res

    g_small["conv_w"] = g_conv
    small_out = [g_small] + [unpack(p) for p in upd]
    names = {"g_mix": "g_mix", "w_in": "w_in", "conv_w": "conv_w", "g_attn_out": "g_attn_out",
             "g_conv_out": "g_conv_out", "w_out": "w_out", "g_xattn": "g_xattn", "g_mem": "g_mem",
             "w_q_mem": "w_q", "w_kv_mem": "w_kv", "w_o_mem": "w_o", "g_mlp": "g_mlp", "w_up": "w_up",
             "w_down": "w_down", "g_final": "g_final"}
    result = [loss, grad_x[None]]
    for which in range(4):
        for key in names.values():
            result.append(outs[key][which] if key in outs else small_out[which][key])
    return tuple(result)
```

```python
import math

import jax
import jax.numpy as jnp
from jax import lax
from jax.experimental import pallas as pl
from jax.experimental.pallas import tpu as pltpu

F32 = jnp.float32
BF16 = jnp.bfloat16
NORM_EPS = 1e-6
NEG_INF = -1e30
N_DEV = 8
BLK = 128
HEAD_DIM = 64
N_MEM_HEADS = 4
ADAM_LR = 0.001
ADAM_B1 = 0.9
ADAM_B2 = 0.999
ADAM_EPS = 1e-08
ADAM_WD = 0.01
ADAM_STEP = 10
MESH = pl.DeviceIdType.MESH
ANY = pl.BlockSpec(memory_space=pl.ANY)


def _dot(a, b):
    return jnp.dot(a, b, preferred_element_type=F32)


def _dot_nt(a, b):
    return lax.dot_general(a, b, (((1,), (1,)), ((), ())), preferred_element_type=F32)


def _dot_tn(a, b):
    return lax.dot_general(a, b, (((0,), (0,)), ((), ())), preferred_element_type=F32)


def _params(semantics, vmem_mb):
    return pltpu.CompilerParams(dimension_semantics=semantics, vmem_limit_bytes=vmem_mb << 20)


def _rms_fwd(x, g):
    r = lax.rsqrt(jnp.mean(x * x, axis=-1, keepdims=True) + NORM_EPS)
    xh = x * r
    return xh * g, xh, r


def _rms_bwd(dy, xh, r, g):
    gy = dy * g
    return r * (gy - xh * jnp.mean(xh * gy, axis=-1, keepdims=True))


def _position():
    x, y, c = lax.axis_index("x"), lax.axis_index("y"), lax.axis_index("c")
    return x, y, c


def _block_of(ref, j, axis, shard_shape):
    r, c = shard_shape
    if axis is None:
        return ref.at[j]
    if axis == 0:
        return ref.at[pl.ds(j * r, r), :]
    return ref.at[:, pl.ds(j * c, c)]


class _Gather:
    has_mid = True
    alias_pairs = ()

    def __init__(self, shards, axes, late=False):
        self.arrays = list(shards)
        self.axes = list(axes)
        self.late = late
        self.n = len(self.arrays)

    def out_shape(self):
        res = []
        for s, axis in zip(self.arrays, self.axes):
            r, c = s.shape
            shape = (N_DEV, r, c) if axis is None else (N_DEV * r, c) if axis == 0 else (r, N_DEV * c)
            res.append(jax.ShapeDtypeStruct(shape, s.dtype))
        return res

    def scratch(self):
        return [pltpu.SemaphoreType.DMA((self.n, 7)), pltpu.SemaphoreType.DMA((self.n, 7)),
                pltpu.SemaphoreType.DMA((self.n,))]

    def _ctx(self, ins, outs, sems):
        send_sems, recv_sems, local_sems = sems
        x, y, c = _position()
        me, sibling = (x, y, c), (x, y, 1 - c)
        chips = [(1 - x, y), (x, 1 - y), (1 - x, 1 - y)]

        def lin(px, py, pc):
            return 4 * px + 2 * py + pc

        def place(a, block):
            return _block_of(outs[a], lin(*block), self.axes[a], self.arrays[a].shape)

        def copy(a, k, block, to, src=None):
            dst = place(a, block)
            return pltpu.make_async_remote_copy(
                src_ref=dst if src is None else src, dst_ref=dst,
                send_sem=send_sems.at[a, k], recv_sem=recv_sems.at[a, k],
                device_id=to, device_id_type=MESH)

        def mine():
            return [pltpu.make_async_copy(ins[a], place(a, me), local_sems.at[a]) for a in range(self.n)]

        def first():
            res = []
            for a in range(self.n):
                res.append(copy(a, 0, me, sibling, src=ins[a]))
                res += [copy(a, 1 + j, me, (*chip, c), src=ins[a]) for j, chip in enumerate(chips)]
            return res

        return c, me, sibling, chips, copy, mine, first

    def start(self, ins, outs, sems):
        _, _, _, _, _, mine, first = self._ctx(ins, outs, sems)
        for cp in mine() + first():
            cp.start()

    def mid(self, ins, outs, sems):
        c, me, sibling, chips, copy, _, _ = self._ctx(ins, outs, sems)
        for j, chip in enumerate(chips):
            for a in range(self.n):
                copy(a, 1 + j, (*chip, c), me).wait_recv()
                copy(a, 4 + j, (*chip, c), sibling).start()

    def finish(self, ins, outs, sems):
        c, me, sibling, chips, copy, mine, first = self._ctx(ins, outs, sems)
        for a in range(self.n):
            copy(a, 0, sibling, me).wait_recv()
            for j, chip in enumerate(chips):
                copy(a, 4 + j, (*chip, 1 - c), me).wait_recv()
        for cp in first():
            cp.wait_send()
        for j, chip in enumerate(chips):
            for a in range(self.n):
                copy(a, 4 + j, (*chip, c), sibling).wait_send()
        for cp in mine():
            cp.wait()


class _Exchange:
    def __init__(self, parts, axes):
        self.n = len(parts)
        self.axes = list(axes)
        self.arrays = list(parts)

    def _piece(self, a):
        r, c = self.arrays[a].shape
        axis = self.axes[a]
        return (r, c) if axis is None else (r // N_DEV, c) if axis == 0 else (r, c // N_DEV)

    def out_shape(self):
        return [jax.ShapeDtypeStruct((N_DEV,) + self._piece(a), self.arrays[a].dtype) for a in range(self.n)]

    def semaphores(self):
        return [pltpu.SemaphoreType.DMA((7 * self.n,)), pltpu.SemaphoreType.DMA((7 * self.n,)),
                pltpu.SemaphoreType.DMA((self.n,))]

    def _ctx(self, ins, outs, sems):
        send_sems, recv_sems, local_sems = sems
        x, y, c = _position()
        me = 4 * x + 2 * y + c

        def src(a, j):
            return ins[a] if self.axes[a] is None else _block_of(ins[a], j, self.axes[a], self._piece(a))

        def dst(a, j):
            return outs[a].at[j]

        def local():
            return [pltpu.make_async_copy(src(a, me), dst(a, me), local_sems.at[a]) for a in range(self.n)]

        def remote(inbound):
            res = []
            for a in range(self.n):
                for k in range(1, N_DEV):
                    peer = (1 - x if k & 4 else x, 1 - y if k & 2 else y, 1 - c if k & 1 else c)
                    plin = 4 * peer[0] + 2 * peer[1] + peer[2]
                    res.append(pltpu.make_async_remote_copy(
                        src_ref=src(a, plin), dst_ref=dst(a, plin if inbound else me),
                        send_sem=send_sems.at[7 * a + k - 1], recv_sem=recv_sems.at[7 * a + k - 1],
                        device_id=peer, device_id_type=MESH))
            return res

        return local, remote

    def start(self, ins, outs, sems):
        local, remote = self._ctx(ins, outs, sems)
        for cp in local() + remote(False):
            cp.start()

    def finish(self, ins, outs, sems):
        local, remote = self._ctx(ins, outs, sems)
        for cp in remote(True):
            cp.wait_recv()
        for cp in remote(False):
            cp.wait_send()
        for cp in local():
            cp.wait()


def _exchange_start(rider, name):
    n = rider.n
    parts = rider.arrays
    lands = [lax.empty(s.shape, s.dtype) for s in rider.out_shape()]
    hbm = pl.BlockSpec(memory_space=pltpu.HBM)
    sem = pl.BlockSpec(memory_space=pltpu.SEMAPHORE)

    def body(*refs):
        ins, sems = refs[:n], refs[2 * n:2 * n + 3]
        outs, token = refs[2 * n + 3 + n:2 * n + 3 + 2 * n], refs[-1]
        rider.start(ins, outs, sems)
        token[...] = jnp.zeros_like(token)

    res = pl.pallas_call(
        body, name=name,
        out_shape=rider.semaphores() + [pltpu.HBM(p.shape, p.dtype) for p in parts]
                  + [pltpu.HBM(z.shape, z.dtype) for z in lands] + [jax.ShapeDtypeStruct((8, 128), F32)],
        in_specs=[hbm] * (2 * n), out_specs=[sem] * 3 + [hbm] * (2 * n) + [pl.BlockSpec(memory_space=pltpu.VMEM)],
        input_output_aliases={i: 3 + i for i in range(2 * n)},
        compiler_params=pltpu.CompilerParams(has_side_effects=pltpu.SideEffectType.DATAFLOW_SIDE_EFFECTING),
    )(*[pltpu.with_memory_space_constraint(a, pltpu.HBM) for a in parts + lands])
    return res[:3], res[3:3 + n], res[3 + n:3 + 2 * n], res[-1]


def _exchange_wait(rider, started, after, name):
    n = rider.n
    sems, parts, lands, _ = started
    hbm = pl.BlockSpec(memory_space=pltpu.HBM)
    sem = pl.BlockSpec(memory_space=pltpu.SEMAPHORE)

    def body(*refs):
        rider.finish(refs[:n], refs[n:2 * n], refs[2 * n:2 * n + 3])

    res = pl.pallas_call(
        body, name=name, out_shape=[pltpu.HBM(a.shape, a.dtype) for a in list(parts) + list(lands)],
        in_specs=[hbm] * (2 * n) + [sem] * 3 + [ANY] * len(after), out_specs=[hbm] * (2 * n),
        input_output_aliases={i: i for i in range(2 * n)},
        compiler_params=pltpu.CompilerParams(has_side_effects=pltpu.SideEffectType.DATAFLOW_SIDE_EFFECTING),
    )(*parts, *lands, *sems, *after)
    return list(res[n:])


def _pcall(body, *, name, grid, in_specs, out_specs, out_shape, scratch_shapes=(), semantics, vmem_mb, rider=None,
           aliases=None, after=()):
    in_specs, out_specs, out_shape = list(in_specs), list(out_specs), list(out_shape)
    scratch_shapes = list(scratch_shapes)
    aliases = dict(aliases or {})
    if rider is None:
        n_in, after = len(in_specs), list(after)

        def plain(*refs):
            body(*refs[:n_in], *refs[n_in + len(after):])

        call = pl.pallas_call(plain if after else body, name=name, grid=grid, in_specs=in_specs + [ANY] * len(after),
                              out_specs=out_specs, out_shape=out_shape, scratch_shapes=scratch_shapes,
                              input_output_aliases=aliases, compiler_params=_params(semantics, vmem_mb))
        return lambda *args: (list(call(*args, *after)), None)
    n_in, n_out, n_scr = len(in_specs), len(out_specs), len(scratch_shapes)
    r_in, r_shapes = len(rider.arrays), rider.out_shape()
    r_out = len(r_shapes)
    aliases.update({n_in + i: n_out + o for i, o in rider.alias_pairs})
    total = math.prod(grid)
    mid_step = total - 1 if rider.has_mid and rider.late else (3 * total) // 4

    def wrapped(*refs):
        bounds = [0, n_in, r_in, n_out, r_out, n_scr]
        for i in range(1, len(bounds)):
            bounds[i] += bounds[i - 1]
        a, ra, o, ro, s = (refs[bounds[i]:bounds[i + 1]] for i in range(5))
        rs = refs[bounds[5]:]
        step = pl.program_id(0)
        for k in range(1, len(grid)):
            step = step * grid[k] + pl.program_id(k)
        pl.when(step == 0)(lambda: rider.start(ra, ro, rs))
        body(*a, *o, *s)
        if rider.has_mid:
            pl.when(step == mid_step)(lambda: rider.mid(ra, ro, rs))
        pl.when(step == total - 1)(lambda: rider.finish(ra, ro, rs))

    call = pl.pallas_call(
        wrapped, name=name, grid=grid, in_specs=in_specs + [ANY] * r_in, out_specs=out_specs + [ANY] * r_out,
        out_shape=out_shape + r_shapes, scratch_shapes=scratch_shapes + rider.scratch(),
        input_output_aliases=aliases, compiler_params=_params(("arbitrary",) * len(grid), vmem_mb))

    def run(*args):
        res = call(*args, *rider.arrays)
        return list(res[:n_out]), list(res[n_out:])

    return run


def _norm_matmul(x, g, w, *, name, out_dtype, tb, bn, relu=False, save_h=False, rider=None):
    t, d = x.shape
    n = w.shape[1]

    def body(x_ref, g_ref, w_ref, o_ref, *rest):
        h_scr = rest[-1]

        @pl.when(pl.program_id(1) == 0)
        def _():
            h = _rms_fwd(x_ref[...], g_ref[...])[0].astype(BF16)
            h_scr[...] = h
            if save_h:
                rest[0][...] = h

        acc = _dot(h_scr[...], w_ref[...])
        if relu:
            acc = jnp.maximum(acc, 0.0)
        o_ref[...] = acc.astype(out_dtype)

    out_shape = [jax.ShapeDtypeStruct((t, n), out_dtype)]
    out_specs = [pl.BlockSpec((tb, bn), lambda i, j: (i, j))]
    if save_h:
        out_shape.append(jax.ShapeDtypeStruct((t, d), BF16))
        out_specs.append(pl.BlockSpec((tb, d), lambda i, j: (i, 0)))
    res, extra = _pcall(
        body, name=name, grid=(t // tb, n // bn),
        in_specs=[pl.BlockSpec((tb, d), lambda i, j: (i, 0)),
                  pl.BlockSpec((1, d), lambda i, j: (0, 0)),
                  pl.BlockSpec((d, bn), lambda i, j: (0, j))],
        out_specs=out_specs, out_shape=out_shape,
        scratch_shapes=[pltpu.VMEM((tb, d), BF16)],
        semantics=("parallel", "arbitrary"), vmem_mb=48, rider=rider,
    )(x, g, w)
    res = res if save_h else res[0]
    return res if rider is None else (res, extra)


def _matmul_nt_normbwd(dy, w, x, g, dres, *, name, tb, also_bf16=False, to_natural=False, after=()):
    t, d = x.shape
    stacked = dy.ndim == 3
    has_res = dres is not None
    n_i = SEG // TI
    if to_natural:
        tb = N_RES * TI

    def body(dy_ref, w_ref, x_ref, g_ref, *rest):
        rest = list(rest)
        dres_ref = rest.pop(0) if has_res else None
        dx_ref = rest.pop(0)
        dxb_ref = rest.pop(0) if also_bf16 else None
        gg_ref = rest.pop(0)
        i = pl.program_id(0)

        def rows(ref, *lead):
            v = ref[lead] if lead else ref[...]
            return v[0].reshape(tb, v.shape[-1]) if to_natural else v

        if stacked:
            kb = dy_ref.shape[-1]
            dh = _dot_nt(rows(dy_ref, 0), w_ref[:, 0:kb])
            for s in range(1, dy_ref.shape[0]):
                dh = dh + _dot_nt(rows(dy_ref, s), w_ref[:, s * kb:(s + 1) * kb])
        else:
            dh = _dot_nt(rows(dy_ref), w_ref[...])
        g_v = g_ref[...]
        _, xh, r = _rms_fwd(rows(x_ref), g_v)
        dx = _rms_bwd(dh, xh, r, g_v)
        if has_res:
            dx = dx + rows(dres_ref)
        if to_natural:
            scr = rest.pop(0)
            for cb in range(d // BLK):
                cols = slice(cb * BLK, (cb + 1) * BLK)
                slab = scr.at[cb]
                for res in range(N_RES):
                    slab[pl.ds(res, TI, stride=N_RES), :] = dx[res * TI:(res + 1) * TI, cols]
                dx_ref[:, cols] = slab[...]
        else:
            dx_ref[...] = dx
        if also_bf16:
            dxb_ref[...] = dx.astype(BF16)
        part = jnp.sum(dh * xh, axis=0, keepdims=True)

        @pl.when(i == 0)
        def _():
            gg_ref[...] = part

        @pl.when(i != 0)
        def _():
            gg_ref[...] += part

    tok = pl.BlockSpec((tb, d), lambda i: (i, 0))
    row = pl.BlockSpec((1, d), lambda i: (0, 0))
    if to_natural:
        act = pl.BlockSpec((1, N_RES, TI, d), lambda i: (i // n_i, 0, i % n_i, 0))
        dy_spec = pl.BlockSpec((dy.shape[0], 1, N_RES, TI, dy.shape[2]), lambda i: (0, i // n_i, 0, i % n_i, 0))
        dy, x = dy.reshape(dy.shape[0], t // HALF, N_RES, SEG, dy.shape[2]), _x4(x)
        dres = _x4(dres) if has_res else None
    elif stacked:
        act, dy_spec = tok, pl.BlockSpec((dy.shape[0], tb, dy.shape[2]), lambda i: (0, i, 0))
    else:
        act, dy_spec = tok, pl.BlockSpec((tb, dy.shape[1]), lambda i: (i, 0))
    in_specs = [dy_spec, pl.BlockSpec(w.shape, lambda i: (0, 0)), act, row]
    args = [dy, w, x, g]
    if has_res:
        in_specs.append(act)
        args.append(dres)
    out_specs = [tok] + ([tok] if also_bf16 else []) + [row]
    out_shape = ([jax.ShapeDtypeStruct((t, d), F32)] + ([jax.ShapeDtypeStruct((t, d), BF16)] if also_bf16 else [])
                 + [jax.ShapeDtypeStruct((1, d), F32)])
    res, _ = _pcall(
        body, name=name, grid=(t // tb,), in_specs=in_specs, out_specs=out_specs, out_shape=out_shape,
        scratch_shapes=[pltpu.VMEM((d // BLK, tb, BLK), F32)] if to_natural else [],
        semantics=("arbitrary",), vmem_mb=56, after=after,
    )(*args)
    return res


def _matmul_tn(a, b, *, name, bm, bn, square_a=False, after=()):
    t, m = a.shape
    stacked = b.ndim == 3
    n = b.shape[0] * bn if stacked else b.shape[1]

    def body(a_ref, b_ref, o_ref):
        av = a_ref[...]
        if square_a:
            av = av.astype(F32)
            av = (av * av).astype(BF16)
        o_ref[...] = _dot_tn(av, b_ref[...]).astype(BF16)

    res, _ = _pcall(
        body, name=name, grid=(m // bm, n // bn),
        in_specs=[pl.BlockSpec((t, bm), lambda i, j: (0, i)),
                  pl.BlockSpec((None, t, bn), lambda i, j: (j, 0, 0)) if stacked
                  else pl.BlockSpec((t, bn), lambda i, j: (0, j))],
        out_specs=[pl.BlockSpec((bm, bn), lambda i, j: (i, j))], out_shape=[jax.ShapeDtypeStruct((m, n), BF16)],
        semantics=("parallel", "parallel"), vmem_mb=56, after=after,
    )(a, b)
    return res[0]


N_RES = 16
SEG = 128
HALF = N_RES * SEG
TI = 32


def _x4(a):
    return a.reshape(a.shape[0] // HALF, N_RES, SEG, a.shape[1])


def _reorder(arrays, name, rider=None):
    t, c = arrays[0].shape
    n = len(arrays)
    n_i = SEG // TI

    def body(*refs):
        scr = refs[-1]
        for i_ref, o_ref in zip(refs[:n], refs[n:2 * n]):
            for cb in range(c // BLK):
                cols = slice(cb * BLK, (cb + 1) * BLK)
                slab = scr.at[cb]
                slab[...] = i_ref[:, cols]
                for r in range(N_RES):
                    o_ref[0, r, :, cols] = slab[pl.ds(r, TI, stride=N_RES), :]

    res, extra = _pcall(
        body, name=name, grid=(t // (TI * N_RES),),
        in_specs=[pl.BlockSpec((TI * N_RES, c), lambda s: (s, 0))] * n,
        out_specs=[pl.BlockSpec((1, N_RES, TI, c), lambda s: (s // n_i, 0, s % n_i, 0))] * n,
        out_shape=[jax.ShapeDtypeStruct((t // HALF, N_RES, SEG, c), F32)] * n,
        scratch_shapes=[pltpu.VMEM((c // BLK, TI * N_RES, BLK), F32)],
        semantics=("parallel",), vmem_mb=32, rider=rider,
    )(*arrays)
    res = [r.reshape(t, c) for r in res]
    return res if rider is None else (res, extra)


_PATTERNS = ((1, 16, 8, SEG), (4, 4, 32, 4 * SEG), (16, 1, SEG, 0))
_FIRST = {1: 1, 4: 4, 16: 16}


def _group_rows(d, g):
    a = g >> 4
    if d == 16:
        base = a * HALF + (g & 15) * SEG
        prev = base - HALF
    elif d == 4:
        c = (g >> 2) & 3
        base = a * HALF + (g & 3) * SEG + c * 32
        prev = jnp.where(c > 0, base - 32, base - HALF + 96)
    else:
        c = g & 15
        base = a * HALF + c * 8
        prev = jnp.where(c > 0, base - 8, base - HALF + 120)
    return base, prev


def _load_rows(ref, base, n, rows, stride):
    parts = [ref[pl.ds(pl.multiple_of(base + j * stride, 8), rows), :] for j in range(n)]
    return parts[0] if n == 1 else jnp.concatenate(parts, axis=0)


def _store_rows(ref, base, val, n, rows, stride, add=False):
    for j in range(n):
        sl = pl.ds(pl.multiple_of(base + j * stride, 8), rows)
        piece = val[j * rows:(j + 1) * rows, :]
        if add:
            ref[sl, :] += piece
        else:
            ref[sl, :] = piece


def _band_bias(n, rows):
    shift = rows.bit_length() - 1
    lq = lax.broadcasted_iota(jnp.int32, (BLK, BLK), 0)
    lk = lax.broadcasted_iota(jnp.int32, (BLK, BLK), 1)
    iq = (lq & (rows - 1)) * n + (lq >> shift)
    ik = (lk & (rows - 1)) * n + (lk >> shift)
    zero = jnp.zeros((BLK, BLK), F32)
    return jnp.where(ik >= iq, zero, NEG_INF), jnp.where(ik <= iq, zero, NEG_INF)


def _set_bias(bias_scr, n, rows):
    prev_b, cur_b = _band_bias(n, rows)
    for half in range(2):
        bias_scr[half * BLK:(half + 1) * BLK, 0:BLK] = prev_b
        bias_scr[half * BLK:(half + 1) * BLK, BLK:2 * BLK] = cur_b


SCALE = 1.0 / math.sqrt(HEAD_DIM)


def _head_consts(value=1.0):
    lane_lo = lax.broadcasted_iota(jnp.int32, (BLK, BLK), 1) < HEAD_DIM
    return lane_lo, [jnp.where(lane_lo, value, 0.0).astype(BF16), jnp.where(lane_lo, 0.0, value).astype(BF16)]


def _stack_heads(v, head_mask):
    return jnp.concatenate([v * head_mask[0], v * head_mask[1]], axis=0)


def _unstack_heads(v2, lane_lo):
    return jnp.where(lane_lo, v2[:BLK], v2[BLK:])


def _rows_per_head(v, lane_lo):
    rolled = pltpu.roll(v, HEAD_DIM, axis=1)
    return jnp.concatenate([jnp.where(lane_lo, v, rolled), jnp.where(lane_lo, rolled, v)], axis=0)


WIDTH = 4


def _loop(lo, hi, fn, width=None):
    if width is None:
        def body(g, carry):
            fn(g)
            return carry

        if hi > lo:
            lax.fori_loop(lo, hi, body, 0)
        return
    while hi > lo:
        trips = (hi - lo) // width
        if trips:
            def body(i, carry, lo=lo, width=width):
                fn([lo + width * i + j for j in range(width)])
                return carry

            lax.fori_loop(0, trips, body, 0)
            lo += trips * width
        width = max(1, width // 2)


def _mix_weights(l1, l2, l3):
    mx = jnp.maximum(jnp.maximum(l1, l2), l3)
    e1, e2, e3 = jnp.exp(l1 - mx), jnp.exp(l2 - mx), jnp.exp(l3 - mx)
    inv = 1.0 / (e1 + e2 + e3)
    return e1 * inv, e2 * inv, e3 * inv


def _attention_fwd(qkv, rider=None):
    t = qkv.shape[0]
    groups = 16 * (t // HALF)

    def body(q_ref, k_ref, v_ref, attn_ref, l1_ref, l2_ref, l3_ref, o_scr, bias_scr):
        lane_lo, q_mask = _head_consts(SCALE)
        l_refs = (l1_ref, l2_ref, l3_ref)
        for p, (d, n, rows, stride) in enumerate(_PATTERNS):
            _set_bias(bias_scr, n, rows)
            o_p, l_p = o_scr.at[p], l_refs[p]

            def block(gs, has_prev):
                at = [_group_rows(d, g) for g in gs]

                def load(ref, b):
                    return _load_rows(ref, b, n, rows, stride).astype(BF16)

                q2 = [_stack_heads(load(q_ref, b), q_mask) for b, _ in at]
                k2 = [load(k_ref, b) for b, _ in at]
                v2 = [load(v_ref, b) for b, _ in at]
                if has_prev:
                    k2 = [jnp.concatenate([load(k_ref, pv), k], axis=0) for (_, pv), k in zip(at, k2)]
                    v2 = [jnp.concatenate([load(v_ref, pv), v], axis=0) for (_, pv), v in zip(at, v2)]
                s = [_dot_nt(q, k) for q, k in zip(q2, k2)]
                s = [x + (bias_scr[...] if has_prev else bias_scr[:, BLK:2 * BLK]) for x in s]
                mx = [jnp.max(x, axis=1, keepdims=True) for x in s]
                e = [jnp.exp(x - m) for x, m in zip(s, mx)]
                den = [jnp.sum(x, axis=1, keepdims=True) for x in e]
                o2 = [_dot(x.astype(BF16), v) * (1.0 / dn) for x, v, dn in zip(e, v2, den)]
                lse2 = [jnp.broadcast_to(m + jnp.log(dn), (2 * BLK, BLK)) for m, dn in zip(mx, den)]
                for (b, _), o, l in zip(at, o2, lse2):
                    _store_rows(o_p, b, _unstack_heads(o, lane_lo), n, rows, stride)
                    _store_rows(l_p, b, _unstack_heads(l, lane_lo), n, rows, stride)

            _loop(0, _FIRST[d], lambda gs: block(gs, False), width=2 * WIDTH)
            _loop(_FIRST[d], groups, lambda gs: block(gs, True), width=2 * WIDTH)

        def mix(i):
            sl = pl.ds(pl.multiple_of(i * 256, 256), 256)
            w = _mix_weights(l1_ref[sl, :], l2_ref[sl, :], l3_ref[sl, :])
            attn_ref[sl, :] = w[0] * o_scr[0, sl, :] + w[1] * o_scr[1, sl, :] + w[2] * o_scr[2, sl, :]

        _loop(0, t // 256, mix)

    def col(c0):
        return pl.BlockSpec((t, BLK), lambda hp: (0, c0 + hp))

    res, extra = _pcall(
        body, name="attention_fwd", grid=(4,), in_specs=[col(0), col(4), col(8)], out_specs=[col(0)] * 4,
        out_shape=[jax.ShapeDtypeStruct((t, 512), F32)] * 4,
        scratch_shapes=[pltpu.VMEM((3, t, BLK), F32), pltpu.VMEM((2 * BLK, 2 * BLK), F32)],
        semantics=("parallel",), vmem_mb=48, rider=rider,
    )(qkv, qkv, qkv)
    return res if rider is None else (res, extra)


def _attention_bwd(qkv, dattn, dsum, lses, dproj):
    t = qkv.shape[0]
    groups = 16 * (t // HALF)

    def body(q_ref, k_ref, v_ref, da_ref, ds_ref, l1_ref, l2_ref, l3_ref, kept_ref, out_ref, acc, bias_scr):
        del kept_ref
        lane_lo, head_mask = _head_consts()
        q_mask = _head_consts(SCALE)[1]
        l_refs = (l1_ref, l2_ref, l3_ref)

        def clear(i):
            sl = pl.ds(pl.multiple_of(i * 512, 512), 512)
            for s in range(3):
                acc[s, sl, :] = jnp.zeros((512, BLK), F32)

        _loop(0, t // 512, clear)
        dq_acc, dk_acc, dv_acc = acc.at[0], acc.at[1], acc.at[2]
        for p, (d, n, rows, stride) in enumerate(_PATTERNS):
            _set_bias(bias_scr, n, rows)

            def block(gs, has_prev):
                at = [_group_rows(d, g) for g in gs]

                def load(ref, b):
                    return _load_rows(ref, b, n, rows, stride)

                def put(ref, b, val):
                    _store_rows(ref, b, val, n, rows, stride, add=True)

                def wide(x):
                    return jnp.concatenate([x, x], axis=1) if has_prev else x

                lse = [[load(ref, b) for ref in l_refs] for b, _ in at]
                w = [_mix_weights(*ls)[p] for ls in lse]
                do2 = [_stack_heads((wg * load(da_ref, b)).astype(BF16), head_mask) for wg, (b, _) in zip(w, at)]
                dl2 = [wide(_rows_per_head(wg * load(ds_ref, b), lane_lo)) for wg, (b, _) in zip(w, at)]
                lse2 = [wide(_rows_per_head(ls[p], lane_lo)) for ls in lse]
                q2 = [_stack_heads(load(q_ref, b).astype(BF16), q_mask) for b, _ in at]
                k2 = [load(k_ref, b).astype(BF16) for b, _ in at]
                v2 = [load(v_ref, b).astype(BF16) for b, _ in at]
                if has_prev:
                    k2 = [jnp.concatenate([load(k_ref, pv).astype(BF16), k], axis=0) for (_, pv), k in zip(at, k2)]
                    v2 = [jnp.concatenate([load(v_ref, pv).astype(BF16), v], axis=0) for (_, pv), v in zip(at, v2)]
                s = [_dot_nt(q, k) for q, k in zip(q2, k2)]
                dp = [_dot_nt(do, v) for do, v in zip(do2, v2)]
                pr = [jnp.exp(x + (bias_scr[...] if has_prev else bias_scr[:, BLK:2 * BLK]) - l)
                      for x, l in zip(s, lse2)]
                ds = [(pg * (x - dl)).astype(BF16) for pg, x, dl in zip(pr, dp, dl2)]
                dq2 = [_dot(x, k) * SCALE for x, k in zip(ds, k2)]
                dk2 = [_dot_tn(x, q) for x, q in zip(ds, q2)]
                dv2 = [_dot_tn(pg.astype(BF16), do) for pg, do in zip(pr, do2)]
                for (b, pv), dq, dk, dv in zip(at, dq2, dk2, dv2):
                    put(dq_acc, b, _unstack_heads(dq, lane_lo))
                    if has_prev:
                        put(dk_acc, pv, dk[:BLK])
                        put(dv_acc, pv, dv[:BLK])
                        put(dk_acc, b, dk[BLK:])
                        put(dv_acc, b, dv[BLK:])
                    else:
                        put(dk_acc, b, dk)
                        put(dv_acc, b, dv)

            _loop(0, _FIRST[d], lambda gs: block(gs, False), width=WIDTH)
            _loop(_FIRST[d], groups, lambda gs: block(gs, True), width=WIDTH)

        def emit(i):
            sl = pl.ds(pl.multiple_of(i * 512, 512), 512)
            for s in range(3):
                out_ref[s, sl, :] = acc[s, sl, :].astype(BF16)

        _loop(0, t // 512, emit)

    def col(c0):
        return pl.BlockSpec((t, BLK), lambda hp: (0, c0 + hp))

    res, _ = _pcall(
        body, name="attention_bwd", grid=(4,),
        in_specs=[col(0), col(4), col(8)] + [col(0)] * 5 + [ANY],
        out_specs=[pl.BlockSpec((3, t, BLK), lambda hp: (0, 0, hp))],
        out_shape=[jax.ShapeDtypeStruct(dproj.shape, BF16)],
        scratch_shapes=[pltpu.VMEM((3, t, BLK), F32), pltpu.VMEM((2 * BLK, 2 * BLK), F32)],
        semantics=("parallel",), vmem_mb=56, aliases={8: 0},
    )(qkv, qkv, qkv, dattn, dsum, *lses, dproj)
    return res[0]


def _order_specs(t):
    n_i = SEG // TI
    nblk = (t // HALF) * n_i
    per = TI // 8

    def main(c, col=0):
        return pl.BlockSpec((1, N_RES, TI, c), lambda s: (s // n_i, 0, s % n_i, col))

    def before(c, col=0):
        return pl.BlockSpec((1, 2, 8, c), lambda s: (jnp.maximum(s - 1, 0) // n_i, N_RES // 2 - 1,
                                                     (jnp.maximum(s - 1, 0) % n_i) * per + per - 1, col))

    def after(c, col=0):
        return pl.BlockSpec((1, 2, 8, c), lambda s: (jnp.minimum(s + 1, nblk - 1) // n_i, 0,
                                                     (jnp.minimum(s + 1, nblk - 1) % n_i) * per, col))

    return nblk, main, before, after


def _shift_in(v, row_in, up):
    rows = v.shape[0]
    idx = lax.broadcasted_iota(jnp.int32, v.shape, 0)
    fill = jnp.broadcast_to(row_in, v.shape)
    if up:
        return jnp.where(idx == rows - 1, fill, pltpu.roll(v, rows - 1, axis=0))
    return jnp.where(idx == 0, fill, pltpu.roll(v, 1, axis=0))


def _taps_behind(u, before):
    s15 = _shift_in(u[N_RES - 1], before[1, 7:8, :], up=False)
    s14 = _shift_in(u[N_RES - 2], before[0, 7:8, :], up=False)
    m1 = jnp.concatenate([s15[None], u[:N_RES - 1]], axis=0)
    m2 = jnp.concatenate([s14[None], s15[None], u[:N_RES - 2]], axis=0)
    return m1, m2


def _taps_ahead(u, after):
    t0 = _shift_in(u[0], after[0, 0:1, :], up=True)
    t1 = _shift_in(u[1], after[1, 0:1, :], up=True)
    p1 = jnp.concatenate([u[1:], t0[None]], axis=0)
    p2 = jnp.concatenate([u[2:], t0[None], t1[None]], axis=0)
    return p1, p2


def _conv_fwd(gates, before, first, cw):
    bg, cg, xc = gates[..., 0:512], gates[..., 512:1024], gates[..., 1024:1536]
    u = cg * xc
    ub = before[..., 512:1024] * before[..., 1024:1536]
    ub = jnp.where(first, jnp.zeros_like(ub), ub)
    m1, m2 = _taps_behind(u, ub)
    conv = m2 * cw[0:1, :] + m1 * cw[1:2, :] + u * cw[2:3, :]
    return bg, u, m1, m2, conv


def _sum_tokens(v):
    return jnp.sum(jnp.sum(v, axis=0), axis=0, keepdims=True)


def _mixer_fwd(x, attn, gates, cw, g_a, g_c, w_out):
    t, d = x.shape
    nblk, main, before, _ = _order_specs(t)
    rows = N_RES * TI

    def body(x_ref, at_ref, gt_ref, gb_ref, cw_ref, ga_ref, gc_ref, wa_ref, wb_ref, x1_ref, mg_ref):
        an = _rms_fwd(at_ref[0], ga_ref[...])[0].astype(BF16)
        bg, _, _, _, conv = _conv_fwd(gt_ref[0], gb_ref[0], pl.program_id(0) == 0, cw_ref[...])
        cn = _rms_fwd(bg * conv, gc_ref[...])[0].astype(BF16)
        mg_ref[0, :, :, 0:512] = an
        mg_ref[0, :, :, 512:1024] = cn
        y = _dot(an.reshape(rows, 512), wa_ref[...]) + _dot(cn.reshape(rows, 512), wb_ref[...])
        x1_ref[0] = x_ref[0] + y.reshape(N_RES, TI, d)

    const = lambda r, c, i0=0: pl.BlockSpec((r, c), lambda s: (i0, 0))
    x1, merged = pl.pallas_call(
        body, name="mixer_fwd", grid=(nblk,),
        in_specs=[main(d), main(512), main(1536, 1), before(1536, 1), const(3, 512), const(1, 512), const(1, 512),
                  const(512, d), const(512, d, 1)],
        out_specs=[main(d), main(d)],
        out_shape=[jax.ShapeDtypeStruct(_x4(x).shape, F32), jax.ShapeDtypeStruct(_x4(x).shape, BF16)],
        compiler_params=_params(("parallel",), 48),
    )(_x4(x), _x4(attn), _x4(gates), _x4(gates), cw, g_a, g_c, w_out, w_out)
    return x1.reshape(t, d), merged.reshape(t, d)


def _mixer_bwd(dx1, attn, gates, cw, g_a, g_c, w_out, head_sum, after=()):
    t, d = dx1.shape
    nblk, main, before, _ = _order_specs(t)
    rows = N_RES * TI

    def body(dx_ref, at_ref, gt_ref, gb_ref, cw_ref, ga_ref, gc_ref, wa_ref, wb_ref, hs_ref,
             da_ref, dsum_ref, dy_ref, gga_ref, ggc_ref):
        s = pl.program_id(0)
        dxb = dx_ref[0].reshape(rows, d).astype(BF16)
        dma = _dot_nt(dxb, wa_ref[...]).reshape(N_RES, TI, 512)
        dmc = _dot_nt(dxb, wb_ref[...]).reshape(N_RES, TI, 512)
        attn_v, g_av = at_ref[0], ga_ref[...]
        _, ah, ra = _rms_fwd(attn_v, g_av)
        dattn = _rms_bwd(dma, ah, ra, g_av)
        da_ref[0] = dattn
        z = (dattn * attn_v).reshape(rows, 512)
        hs = hs_ref[...]
        z1 = z.astype(BF16)
        z2 = (z - z1.astype(F32)).astype(BF16)
        dsum_ref[0] = (_dot(z1, hs) + _dot(z2, hs)).reshape(N_RES, TI, 512)
        bg, _, _, _, conv = _conv_fwd(gt_ref[0], gb_ref[0], s == 0, cw_ref[...])
        g_cv = gc_ref[...]
        _, yh, rc = _rms_fwd(bg * conv, g_cv)
        dy_ref[0] = _rms_bwd(dmc, yh, rc, g_cv)
        pa, pc = _sum_tokens(dma * ah), _sum_tokens(dmc * yh)

        @pl.when(s == 0)
        def _():
            gga_ref[...] = pa
            ggc_ref[...] = pc

        @pl.when(s != 0)
        def _():
            gga_ref[...] += pa
            ggc_ref[...] += pc

    const = lambda r, c, i0=0: pl.BlockSpec((r, c), lambda s: (i0, 0))
    shape4 = _x4(attn).shape
    res, _ = _pcall(
        body, name="mixer_bwd", grid=(nblk,),
        in_specs=[main(d), main(512), main(1536, 1), before(1536, 1), const(3, 512), const(1, 512), const(1, 512),
                  const(512, d), const(512, d, 1), const(512, 512)],
        out_specs=[main(512)] * 3 + [const(1, 512), const(1, 512)],
        out_shape=[jax.ShapeDtypeStruct(shape4, F32)] * 3 + [jax.ShapeDtypeStruct((1, 512), F32)] * 2,
        semantics=("arbitrary",), vmem_mb=48, after=after,
    )(_x4(dx1), _x4(attn), _x4(gates), _x4(gates), cw, g_a, g_c, w_out, w_out, head_sum)
    return [r.reshape(t, 512) for r in res[:3]] + res[3:]


def _conv_bwd(dy, gates, cw):
    t = dy.shape[0]
    nblk, main, before, after = _order_specs(t)
    n_i = SEG // TI

    def body(dy_ref, dya_ref, gt_ref, gb_ref, ga_ref, cw_ref, dp_ref, gcw_ref):
        s = pl.program_id(0)
        cw_v, gates_v = cw_ref[...], gt_ref[0]
        bg, u, m1, m2, conv = _conv_fwd(gates_v, gb_ref[0], s == 0, cw_v)
        dy_v = dy_ref[0]
        dconv = dy_v * bg
        dca = dya_ref[0] * ga_ref[0][..., 0:512]
        dca = jnp.where(s == nblk - 1, jnp.zeros_like(dca), dca)
        p1, p2 = _taps_ahead(dconv, dca)
        du = dconv * cw_v[2:3, :] + p1 * cw_v[1:2, :] + p2 * cw_v[0:1, :]
        dp_ref[0, 0] = (dy_v * conv).astype(BF16)
        dp_ref[1, 0] = (du * gates_v[..., 1024:1536]).astype(BF16)
        dp_ref[2, 0] = (du * gates_v[..., 512:1024]).astype(BF16)
        parts = [_sum_tokens(dconv * m2), _sum_tokens(dconv * m1), _sum_tokens(dconv * u)]

        @pl.when(s == 0)
        def _():
            gcw_ref[...] = jnp.zeros_like(gcw_ref)

        for tap in range(3):
            gcw_ref[tap:tap + 1, :] += parts[tap]

    (dproj, gcw), _ = _pcall(
        body, name="conv_bwd", grid=(nblk,),
        in_specs=[main(512), after(512), main(1536, 1), before(1536, 1), after(1536, 1),
                  pl.BlockSpec((3, 512), lambda s: (0, 0))],
        out_specs=[pl.BlockSpec((3, 1, N_RES, TI, 512), lambda s: (1, s // n_i, 0, s % n_i, 0)),
                   pl.BlockSpec((8, 512), lambda s: (0, 0))],
        out_shape=[jax.ShapeDtypeStruct((6, t // HALF, N_RES, SEG, 512), BF16), jax.ShapeDtypeStruct((8, 512), F32)],
        semantics=("arbitrary",), vmem_mb=40,
    )(_x4(dy), _x4(dy), _x4(gates), _x4(gates), _x4(gates), cw)
    return dproj.reshape(6, t, 512), gcw


def _xattn_fwd(x1, g, w_q, kv, w_o, *, tb):
    t, d = x1.shape
    hd = d // N_MEM_HEADS
    m = kv.shape[0]

    def body(x_ref, g_ref, wq_ref, k_ref, v_ref, wo_ref, x2_ref, h_ref, q_ref, o_ref):
        xv = x_ref[...]
        h = _rms_fwd(xv, g_ref[...])[0].astype(BF16)
        h_ref[...] = h
        q = _dot(h, wq_ref[...]).astype(BF16)
        q_ref[...] = q
        for hh in range(N_MEM_HEADS):
            sl = slice(hh * hd, (hh + 1) * hd)
            s = _dot_nt(q[:, sl], k_ref[:, sl]) * (1.0 / 16.0)
            e = jnp.exp(s - jnp.max(s, axis=1, keepdims=True))
            p = e / jnp.sum(e, axis=1, keepdims=True)
            o_ref[:, sl] = _dot(p.astype(BF16), v_ref[:, sl]).astype(BF16)
        x2_ref[...] = xv + _dot(o_ref[...], wo_ref[...])

    tok = pl.BlockSpec((tb, d), lambda i: (i, 0))
    full = pl.BlockSpec((d, d), lambda i: (0, 0))
    return pl.pallas_call(
        body, name="xattn_fwd", grid=(t // tb,),
        in_specs=[tok, pl.BlockSpec((1, d), lambda i: (0, 0)), full,
                  pl.BlockSpec((m, d), lambda i: (0, 0)), pl.BlockSpec((m, d), lambda i: (0, 1)), full],
        out_specs=[tok] * 4,
        out_shape=[jax.ShapeDtypeStruct((t, d), F32)] + [jax.ShapeDtypeStruct((t, d), BF16)] * 3,
        compiler_params=_params(("parallel",), 48),
    )(x1, g, w_q, kv, kv, w_o)


def _xattn_bwd(dx2, x1, g, q, w_q, kv, w_o, *, tb, after=()):
    t, d = x1.shape
    hd = d // N_MEM_HEADS
    m = kv.shape[0]

    def body(dx2_ref, x_ref, g_ref, q_ref, wq_ref, k_ref, v_ref, wo_ref,
             dx1_ref, dx1b_ref, dq_ref, dk_ref, dv_ref, gg_ref):
        i = pl.program_id(0)

        @pl.when(i == 0)
        def _():
            dk_ref[...] = jnp.zeros_like(dk_ref)
            dv_ref[...] = jnp.zeros_like(dv_ref)

        dx2 = dx2_ref[...]
        do = _dot_nt(dx2.astype(BF16), wo_ref[...]).astype(BF16)
        for hh in range(N_MEM_HEADS):
            sl = slice(hh * hd, (hh + 1) * hd)
            qh, kh, vh, doh = q_ref[:, sl], k_ref[:, sl], v_ref[:, sl], do[:, sl]
            s = _dot_nt(qh, kh) * (1.0 / 16.0)
            e = jnp.exp(s - jnp.max(s, axis=1, keepdims=True))
            p = e / jnp.sum(e, axis=1, keepdims=True)
            dp = _dot_nt(doh, vh)
            ds = (p * (dp - jnp.sum(dp * p, axis=1, keepdims=True)) * (1.0 / 16.0)).astype(BF16)
            dq_ref[:, sl] = _dot(ds, kh).astype(BF16)
            dk_ref[:, sl] += _dot_tn(ds, qh)
            dv_ref[:, sl] += _dot_tn(p.astype(BF16), doh)
        dh = _dot_nt(dq_ref[...], wq_ref[...])
        g_v = g_ref[...]
        _, xh, r = _rms_fwd(x_ref[...], g_v)
        dx1 = dx2 + _rms_bwd(dh, xh, r, g_v)
        dx1_ref[...] = dx1
        dx1b_ref[...] = dx1.astype(BF16)
        part = jnp.sum(dh * xh, axis=0, keepdims=True)

        @pl.when(i == 0)
        def _():
            gg_ref[...] = part

        @pl.when(i != 0)
        def _():
            gg_ref[...] += part

    tok = pl.BlockSpec((tb, d), lambda i: (i, 0))
    full = pl.BlockSpec((d, d), lambda i: (0, 0))
    acc = pl.BlockSpec((m, d), lambda i: (0, 0))
    res, _ = _pcall(
        body, name="xattn_bwd", grid=(t // tb,),
        in_specs=[tok, tok, pl.BlockSpec((1, d), lambda i: (0, 0)), tok, full,
                  pl.BlockSpec((m, d), lambda i: (0, 0)), pl.BlockSpec((m, d), lambda i: (0, 1)), full],
        out_specs=[tok, tok, tok, acc, acc, pl.BlockSpec((1, d), lambda i: (0, 0))],
        out_shape=[jax.ShapeDtypeStruct((t, d), F32), jax.ShapeDtypeStruct((t, d), BF16),
                   jax.ShapeDtypeStruct((t, d), BF16),
                   jax.ShapeDtypeStruct((m, d), F32), jax.ShapeDtypeStruct((m, d), F32),
                   jax.ShapeDtypeStruct((1, d), F32)],
        semantics=("arbitrary",), vmem_mb=48, after=after,
    )(dx2, x1, g, q, w_q, kv, kv, w_o)
    return res


def _mlp_down_loss(a, w_down, x2, tgt, g, *, tb):
    t, d = x2.shape
    f = a.shape[1]

    def body(a_ref, w_ref, x_ref, t_ref, g_ref, dx_ref, dxb_ref, loss_ref, gg_ref):
        i = pl.program_id(0)
        av = a_ref[...].astype(F32)
        x3 = x_ref[...] + _dot((av * av).astype(BF16), w_ref[...])
        g_v = g_ref[...]
        out, xh, r = _rms_fwd(x3, g_v)
        err = out - t_ref[...]
        dout = err * (1.0 / d)
        dx = _rms_bwd(dout, xh, r, g_v)
        dx_ref[...] = dx
        dxb_ref[...] = dx.astype(BF16)
        part = jnp.sum(dout * xh, axis=0, keepdims=True)
        lpart = 0.5 * jnp.sum(jnp.mean(err * err, axis=-1, keepdims=True), axis=0, keepdims=True)
        lpart = jnp.broadcast_to(lpart, loss_ref.shape)

        @pl.when(i == 0)
        def _():
            gg_ref[...] = part
            loss_ref[...] = lpart

        @pl.when(i != 0)
        def _():
            gg_ref[...] += part
            loss_ref[...] += lpart

    tok = pl.BlockSpec((tb, d), lambda i: (i, 0))
    return pl.pallas_call(
        body, name="mlp_down_loss", grid=(t // tb,),
        in_specs=[pl.BlockSpec((tb, f), lambda i: (i, 0)), pl.BlockSpec((f, d), lambda i: (0, 0)), tok, tok,
                  pl.BlockSpec((1, d), lambda i: (0, 0))],
        out_specs=[tok, tok, pl.BlockSpec((8, 128), lambda i: (0, 0)), pl.BlockSpec((1, d), lambda i: (0, 0))],
        out_shape=[jax.ShapeDtypeStruct((t, d), F32), jax.ShapeDtypeStruct((t, d), BF16),
                   jax.ShapeDtypeStruct((8, 128), F32), jax.ShapeDtypeStruct((1, d), F32)],
        compiler_params=_params(("arbitrary",), 56),
    )(a, w_down, x2, tgt, g)


def _mlp_dpre(dx3, w_down, a, *, tb, bn):
    t, d = dx3.shape
    f = a.shape[1]

    def body(dx_ref, w_ref, a_ref, o_ref):
        o_ref[...] = (2.0 * a_ref[...].astype(F32) * _dot_nt(dx_ref[...], w_ref[...])).astype(BF16)

    return pl.pallas_call(
        body, name="mlp_dpre", grid=(t // tb, f // bn),
        in_specs=[pl.BlockSpec((tb, d), lambda i, j: (i, 0)), pl.BlockSpec((bn, d), lambda i, j: (j, 0)),
                  pl.BlockSpec((tb, bn), lambda i, j: (i, j))],
        out_specs=pl.BlockSpec((tb, bn), lambda i, j: (i, j)),
        out_shape=jax.ShapeDtypeStruct((t, f), BF16),
        compiler_params=_params(("parallel", "arbitrary"), 48),
    )(dx3, w_down, a)


def _adamw(gsum, w, m, v):
    m_new = ADAM_B1 * m + (1.0 - ADAM_B1) * gsum
    v_new = ADAM_B2 * v + (1.0 - ADAM_B2) * (gsum * gsum)
    m_hat = m_new / (1.0 - ADAM_B1 ** ADAM_STEP)
    v_hat = v_new / (1.0 - ADAM_B2 ** ADAM_STEP)
    delta = -ADAM_LR * (m_hat / (jnp.sqrt(v_hat) + ADAM_EPS) + ADAM_WD * w)
    return delta, m_new, v_new


def _sum_adamw(parts, w, m, v, *, name, tr):
    r, c = w.shape

    def body(p_ref, w_ref, m_ref, v_ref, g_ref, d_ref, mo_ref, vo_ref):
        g = p_ref[0].astype(F32)
        for k in range(1, N_DEV):
            g = g + p_ref[k].astype(F32)
        g_ref[...] = g
        d_ref[...], mo_ref[...], vo_ref[...] = _adamw(g, w_ref[...], m_ref[...], v_ref[...])

    blk = pl.BlockSpec((tr, c), lambda i: (i, 0))
    return pl.pallas_call(
        body, name=name, grid=(r // tr,),
        in_specs=[pl.BlockSpec((N_DEV, tr, c), lambda i: (0, i, 0)), blk, blk, blk],
        out_specs=[blk] * 4, out_shape=[jax.ShapeDtypeStruct((r, c), F32)] * 4,
        compiler_params=_params(("parallel",), 40),
    )(parts, w, m, v)


def _sum_small(parts):
    _, r, c = parts.shape

    def body(p_ref, o_ref):
        s = p_ref[0]
        for k in range(1, N_DEV):
            s = s + p_ref[k]
        o_ref[...] = s

    return pl.pallas_call(body, name="sum_small", out_shape=jax.ShapeDtypeStruct((r, c), F32))(parts)


def _adamw_small(g, w, m, v):
    def body(g_ref, w_ref, m_ref, v_ref, d_ref, mo_ref, vo_ref):
        d_ref[...], mo_ref[...], vo_ref[...] = _adamw(g_ref[...], w_ref[...], m_ref[...], v_ref[...])

    return pl.pallas_call(body, name="adamw_small", out_shape=[jax.ShapeDtypeStruct(g.shape, F32)] * 3)(g, w, m, v)


def _head_sum_matrix():
    r = lax.broadcasted_iota(jnp.int32, (512, 512), 0) // HEAD_DIM
    c = lax.broadcasted_iota(jnp.int32, (512, 512), 1) // HEAD_DIM
    return (r == c).astype(BF16)


_SHARD_AXIS = dict(w_in=1, w_out=0, w_q=0, w_kv=1, w_o=0, w_up=1, w_down=0, conv_w=None, small=None)


class _Weights:
    def __init__(self, full, shards=None):
        self.full = dict(full)
        self.shards = shards

    def rider(self, names, late=False):
        if self.shards is None:
            return None
        return _Gather([self.shards[n] for n in names], [_SHARD_AXIS[n] for n in names], late)

    def arrived(self, names, gathered):
        if gathered is not None:
            for n, g in zip(names, gathered):
                self.full[n] = g.transpose(1, 0, 2).reshape(g.shape[1], -1) if n == "conv_w" else g

    def __getitem__(self, name):
        return self.full[name]


class _Grads:
    def __init__(self, distributed):
        self.distributed = distributed
        self.local = {}
        self.pending = {}

    def add(self, name, g):
        self.local[name] = g

    def send(self, *names):
        if not self.distributed:
            return []
        rider = _Exchange([self.local[n] for n in names], [_SHARD_AXIS[n] for n in names])
        started = _exchange_start(rider, "send_" + "_".join(names))
        self.pending[names[0]] = (names, rider, started)
        return [started[3]]

    def wait(self, first_name, after):
        names, rider, started = self.pending.pop(first_name)
        return _exchange_wait(rider, started, after, "wait_" + "_".join(names))


def _ride(fn, *args, rider=None, **kw):
    if rider is None:
        return fn(*args, **kw), None
    return fn(*args, rider=rider, **kw)


def _local_step(x, mem, tgt, gains, weights, grads):
    names = ["w_in", "conv_w"]
    (x, tgt), got = _ride(_reorder, [x, tgt], "reorder_in", rider=weights.rider(names, late=True))
    weights.arrived(names, got)
    w_in, cw = weights["w_in"], weights["conv_w"]

    names = ["w_out", "w_kv"]
    (proj, h1), got = _ride(_norm_matmul, x, gains["g_mix"], w_in, name="proj", out_dtype=F32, tb=1024, bn=768,
                            save_h=True, rider=weights.rider(names))
    weights.arrived(names, got)
    names = ["w_q", "w_o", "w_up"]
    (attn, *lses), got = _ride(_attention_fwd, proj, rider=weights.rider(names))
    weights.arrived(names, got)
    x1, merged = _mixer_fwd(x, attn, proj, cw, gains["g_attn_out"], gains["g_conv_out"], weights["w_out"])
    kv, mem_n = _norm_matmul(mem, gains["g_mem"], weights["w_kv"], name="mem_kv", out_dtype=BF16, tb=mem.shape[0],
                             bn=1024, save_h=True)
    x2, h2, qm, om = _xattn_fwd(x1, gains["g_xattn"], weights["w_q"], kv, weights["w_o"], tb=512)
    w_up = weights["w_up"]
    (a, h3), got = _ride(_norm_matmul, x2, gains["g_mlp"], w_up, name="mlp_up", out_dtype=BF16, tb=1024, bn=2048,
                         relu=True, save_h=True, rider=weights.rider(["w_down"], late=True))
    weights.arrived(["w_down"], got)
    w_down = weights["w_down"]
    dx3, dx3b, loss_blk, gg_final = _mlp_down_loss(a, w_down, x2, tgt, gains["g_final"], tb=512)

    dpre = _mlp_dpre(dx3b, w_down, a, tb=1024, bn=2048)
    grads.add("w_down", _matmul_tn(a, dx3b, name="grad_w_down", bm=512, bn=1024, square_a=True))
    sent = grads.send("w_down")
    grads.add("w_up", _matmul_tn(h3, dpre, name="grad_w_up", bm=1024, bn=512, after=sent))
    sent = grads.send("w_up")
    dx2, dx2b, gg_mlp = _matmul_nt_normbwd(dpre, w_up, x2, gains["g_mlp"], dx3, name="mlp_dx", tb=512,
                                           also_bf16=True, after=sent)

    grads.add("w_o", _matmul_tn(om, dx2b, name="grad_w_o", bm=1024, bn=512))
    dx1, dx1b, dqm, dk, dv, gg_xattn = _xattn_bwd(dx2, x1, gains["g_xattn"], qm, weights["w_q"], kv, weights["w_o"],
                                                  tb=512)
    grads.add("w_q", _matmul_tn(h2, dqm, name="grad_w_q", bm=1024, bn=512))
    dkv = jnp.concatenate([dk, dv], axis=1).astype(BF16)
    grads.add("w_kv", _matmul_tn(mem_n, dkv, name="grad_w_kv", bm=1024, bn=1024))
    _, gg_mem = _matmul_nt_normbwd(dkv, weights["w_kv"], mem, gains["g_mem"], None, name="mem_dx", tb=mem.shape[0])

    grads.add("w_out", _matmul_tn(merged, dx1b, name="grad_w_out", bm=1024, bn=512))
    sent = grads.send("w_o", "w_q", "w_kv", "w_out")
    dattn, dsum, dy, gg_attn, gg_conv = _mixer_bwd(dx1, attn, proj, cw, gains["g_attn_out"], gains["g_conv_out"],
                                                   weights["w_out"], _head_sum_matrix(), after=sent)
    dproj, gcw = _conv_bwd(dy, proj, cw)
    dproj = _attention_bwd(proj, dattn, dsum, lses, dproj)
    grads.add("w_in", _matmul_tn(h1, dproj, name="grad_w_in", bm=1024, bn=512))
    sent = grads.send("w_in")
    grad_x, gg_mix = _matmul_nt_normbwd(dproj, w_in, x, gains["g_mix"], dx1, name="mixer_dx", tb=512,
                                        to_natural=True, after=sent)

    def part(v):
        return jnp.pad(v, ((0, SMALL_PART - v.shape[0]), (0, 1024 - v.shape[1])))

    parts = [gg_mix, gg_xattn, gg_mem, gg_mlp, gg_final, jnp.concatenate([gg_attn, gg_conv], axis=1), gcw, loss_blk]
    grads.add("small", jnp.concatenate([part(v) for v in parts], axis=0))
    return grad_x


SMALL_PART = 8
_BIG = ("w_in", "w_out", "w_q", "w_kv", "w_o", "w_up", "w_down")
_GAIN_ROWS = ("g_mix", "g_xattn", "g_mem", "g_mlp", "g_final")


def _pack_small(vals, conv):
    rows = [vals[k].reshape(1, -1) for k in _GAIN_ROWS]
    rows.append(jnp.concatenate([vals["g_attn_out"].reshape(1, -1), vals["g_conv_out"].reshape(1, -1)], axis=1))
    flat = conv.reshape(1, -1)
    rows.append(jnp.pad(flat, ((0, 0), (0, 1024 - flat.shape[1]))))
    rows.append(jnp.zeros((1, 1024), F32))
    return jnp.concatenate(rows, axis=0)


def kernel(x, mem, g_mix, w_in, conv_w, g_attn_out, g_conv_out, w_out, g_xattn, g_mem, w_q_mem, w_kv_mem, w_o_mem, g_mlp, w_up, w_down, g_final, loss_target, m_g_mix, m_w_in, m_conv_w, m_g_attn_out, m_g_conv_out, m_w_out, m_g_xattn, m_g_mem, m_w_q_mem, m_w_kv_mem, m_w_o_mem, m_g_mlp, m_w_up, m_w_down, m_g_final, v_g_mix, v_w_in, v_conv_w, v_g_attn_out, v_g_conv_out, v_w_out, v_g_xattn, v_g_mem, v_w_q_mem, v_w_kv_mem, v_w_o_mem, v_g_mlp, v_w_up, v_w_down, v_g_final):
    d = x.shape[-1]
    me = 4 * lax.axis_index("x") + 2 * lax.axis_index("y") + lax.axis_index("c")
    w_shards = dict(w_in=w_in, w_out=w_out, w_q=w_q_mem, w_kv=w_kv_mem, w_o=w_o_mem, w_up=w_up, w_down=w_down)
    m_shards = dict(w_in=m_w_in, w_out=m_w_out, w_q=m_w_q_mem, w_kv=m_w_kv_mem, w_o=m_w_o_mem, w_up=m_w_up,
                    w_down=m_w_down)
    v_shards = dict(w_in=v_w_in, w_out=v_w_out, w_q=v_w_q_mem, w_kv=v_w_kv_mem, w_o=v_w_o_mem, w_up=v_w_up,
                    w_down=v_w_down)
    gains = dict(g_mix=g_mix, g_attn_out=g_attn_out, g_conv_out=g_conv_out, g_xattn=g_xattn, g_mem=g_mem,
                 g_mlp=g_mlp, g_final=g_final)
    gains2 = {k: v.reshape(1, -1) for k, v in gains.items()}

    shards = {k: w_shards[k].astype(BF16) for k in _BIG}
    shards["conv_w"] = conv_w
    grads = _Grads(distributed=True)
    grad_x = _local_step(x[0], mem[0], loss_target[0], gains2, _Weights({}, shards), grads)

    after = grads.send("small")
    outs = {}
    tiles = dict(w_in=256, w_out=128, w_q=128, w_kv=256, w_o=128, w_up=256, w_down=256)
    for group in (("w_down",), ("w_up",), ("w_o", "w_q", "w_kv", "w_out"), ("w_in",)):
        for k, received in zip(group, grads.wait(group[0], after)):
            outs[k] = _sum_adamw(received, w_shards[k], m_shards[k], v_shards[k], name=f"adamw_{k}", tr=tiles[k])
            after = [outs[k][0]]
    small_received, = grads.wait("small", after)

    ssum = _sum_small(small_received)
    row = lambda i: ssum[SMALL_PART * i]
    loss = ssum[SMALL_PART * 7, 0]
    g_small = {k: row(i) for i, k in enumerate(_GAIN_ROWS)}
    g_small["g_attn_out"] = row(5)[0:512]
    g_small["g_conv_out"] = row(5)[512:1024]
    taps = ssum[SMALL_PART * 6:SMALL_PART * 6 + 3, 0:512]
    g_conv = lax.dynamic_slice_in_dim(taps, me * 64, 64, axis=1)
    m_small = dict(g_mix=m_g_mix, g_attn_out=m_g_attn_out, g_conv_out=m_g_conv_out, g_xattn=m_g_xattn,
                   g_mem=m_g_mem, g_mlp=m_g_mlp, g_final=m_g_final)
    v_small = dict(g_mix=v_g_mix, g_attn_out=v_g_attn_out, g_conv_out=v_g_conv_out, g_xattn=v_g_xattn,
                   g_mem=v_g_mem, g_mlp=v_g_mlp, g_final=v_g_final)
    packed = [_pack_small(g_small, g_conv), _pack_small(gains, conv_w), _pack_small(m_small, m_conv_w),
              _pack_small(v_small, v_conv_w)]
    upd = _adamw_small(*packed)

    def unpack(p):
        res = {k: p[i] for i, k in enumerate(_GAIN_ROWS)}
        res["g_attn_out"] = p[5, 0:512]
        res["g_conv_out"] = p[5, 512:1024]
        res["conv_w"] = p[6, 0:192].reshape(3, 64)
        return res

    g_small["conv_w"] = g_conv
    small_out = [g_small] + [unpack(p) for p in upd]
    names = {"g_mix": "g_mix", "w_in": "w_in", "conv_w": "conv_w", "g_attn_out": "g_attn_out",
             "g_conv_out": "g_conv_out", "w_out": "w_out", "g_xattn": "g_xattn", "g_mem": "g_mem",
             "w_q_mem": "w_q", "w_kv_mem": "w_kv", "w_o_mem": "w_o", "g_mlp": "g_mlp", "w_up": "w_up",
             "w_down": "w_down", "g_final": "g_final"}
    result = [loss, grad_x[None]]
    for which in range(4):
        for key in names.values():
            result.append(outs[key][which] if key in outs else small_out[which][key])
    return tuple(result)
```

```python
import math

import jax
import jax.numpy as jnp
from jax import lax
from jax.experimental import pallas as pl
from jax.experimental.pallas import tpu as pltpu

F32 = jnp.float32
BF16 = jnp.bfloat16
NORM_EPS = 1e-6
NEG_INF = -1e30
N_DEV = 8
BLK = 128
HEAD_DIM = 64
N_MEM_HEADS = 4
ADAM_LR = 0.001
ADAM_B1 = 0.9
ADAM_B2 = 0.999
ADAM_EPS = 1e-08
ADAM_WD = 0.01
ADAM_STEP = 10
MESH = pl.DeviceIdType.MESH
ANY = pl.BlockSpec(memory_space=pl.ANY)


def _dot(a, b):
    return jnp.dot(a, b, preferred_element_type=F32)


def _dot_nt(a, b):
    return lax.dot_general(a, b, (((1,), (1,)), ((), ())), preferred_element_type=F32)


def _dot_tn(a, b):
    return lax.dot_general(a, b, (((0,), (0,)), ((), ())), preferred_element_type=F32)


def _params(semantics, vmem_mb):
    return pltpu.CompilerParams(dimension_semantics=semantics, vmem_limit_bytes=vmem_mb << 20)


def _rms_fwd(x, g):
    r = lax.rsqrt(jnp.mean(x * x, axis=-1, keepdims=True) + NORM_EPS)
    xh = x * r
    return xh * g, xh, r


def _rms_bwd(dy, xh, r, g):
    gy = dy * g
    return r * (gy - xh * jnp.mean(xh * gy, axis=-1, keepdims=True))


def _position():
    x, y, c = lax.axis_index("x"), lax.axis_index("y"), lax.axis_index("c")
    return x, y, c


def _block_of(ref, j, axis, shard_shape):
    r, c = shard_shape
    if axis is None:
        return ref.at[j]
    if axis == 0:
        return ref.at[pl.ds(j * r, r), :]
    return ref.at[:, pl.ds(j * c, c)]


class _Gather:
    has_mid = True
    alias_pairs = ()

    def __init__(self, shards, axes, late=False):
        self.arrays = list(shards)
        self.axes = list(axes)
        self.late = late
        self.n = len(self.arrays)

    def out_shape(self):
        res = []
        for s, axis in zip(self.arrays, self.axes):
            r, c = s.shape
            shape = (N_DEV, r, c) if axis is None else (N_DEV * r, c) if axis == 0 else (r, N_DEV * c)
            res.append(jax.ShapeDtypeStruct(shape, s.dtype))
        return res

    def scratch(self):
        return [pltpu.SemaphoreType.DMA((self.n, 7)), pltpu.SemaphoreType.DMA((self.n, 7)),
                pltpu.SemaphoreType.DMA((self.n,))]

    def _ctx(self, ins, outs, sems):
        send_sems, recv_sems, local_sems = sems
        x, y, c = _position()
        me, sibling = (x, y, c), (x, y, 1 - c)
        chips = [(1 - x, y), (x, 1 - y), (1 - x, 1 - y)]

        def lin(px, py, pc):
            return 4 * px + 2 * py + pc

        def place(a, block):
            return _block_of(outs[a], lin(*block), self.axes[a], self.arrays[a].shape)

        def copy(a, k, block, to, src=None):
            dst = place(a, block)
            return pltpu.make_async_remote_copy(
                src_ref=dst if src is None else src, dst_ref=dst,
                send_sem=send_sems.at[a, k], recv_sem=recv_sems.at[a, k],
                device_id=to, device_id_type=MESH)

        def mine():
            return [pltpu.make_async_copy(ins[a], place(a, me), local_sems.at[a]) for a in range(self.n)]

        def first():
            res = []
            for a in range(self.n):
                res.append(copy(a, 0, me, sibling, src=ins[a]))
                res += [copy(a, 1 + j, me, (*chip, c), src=ins[a]) for j, chip in enumerate(chips)]
            return res

        return c, me, sibling, chips, copy, mine, first

    def start(self, ins, outs, sems):
        _, _, _, _, _, mine, first = self._ctx(ins, outs, sems)
        for cp in mine() + first():
            cp.start()

    def mid(self, ins, outs, sems):
        c, me, sibling, chips, copy, _, _ = self._ctx(ins, outs, sems)
        for j, chip in enumerate(chips):
            for a in range(self.n):
                copy(a, 1 + j, (*chip, c), me).wait_recv()
                copy(a, 4 + j, (*chip, c), sibling).start()

    def finish(self, ins, outs, sems):
        c, me, sibling, chips, copy, mine, first = self._ctx(ins, outs, sems)
        for a in range(self.n):
            copy(a, 0, sibling, me).wait_recv()
            for j, chip in enumerate(chips):
                copy(a, 4 + j, (*chip, 1 - c), me).wait_recv()
        for cp in first():
            cp.wait_send()
        for j, chip in enumerate(chips):
            for a in range(self.n):
                copy(a, 4 + j, (*chip, c), sibling).wait_send()
        for cp in mine():
            cp.wait()


class _Exchange:
    def __init__(self, parts, axes):
        self.n = len(parts)
        self.axes = list(axes)
        self.arrays = list(parts)

    def _piece(self, a):
        r, c = self.arrays[a].shape
        axis = self.axes[a]
        return (r, c) if axis is None else (r // N_DEV, c) if axis == 0 else (r, c // N_DEV)

    def out_shape(self):
        return [jax.ShapeDtypeStruct((N_DEV,) + self._piece(a), self.arrays[a].dtype) for a in range(self.n)]

    def semaphores(self):
        return [pltpu.SemaphoreType.DMA((7 * self.n,)), pltpu.SemaphoreType.DMA((7 * self.n,)),
                pltpu.SemaphoreType.DMA((self.n,))]

    def _ctx(self, ins, outs, sems):
        send_sems, recv_sems, local_sems = sems
        x, y, c = _position()
        me = 4 * x + 2 * y + c

        def src(a, j):
            return ins[a] if self.axes[a] is None else _block_of(ins[a], j, self.axes[a], self._piece(a))

        def dst(a, j):
            return outs[a].at[j]

        def local():
            return [pltpu.make_async_copy(src(a, me), dst(a, me), local_sems.at[a]) for a in range(self.n)]

        def remote(inbound):
            res = []
            for a in range(self.n):
                for k in range(1, N_DEV):
                    peer = (1 - x if k & 4 else x, 1 - y if k & 2 else y, 1 - c if k & 1 else c)
                    plin = 4 * peer[0] + 2 * peer[1] + peer[2]
                    res.append(pltpu.make_async_remote_copy(
                        src_ref=src(a, plin), dst_ref=dst(a, plin if inbound else me),
                        send_sem=send_sems.at[7 * a + k - 1], recv_sem=recv_sems.at[7 * a + k - 1],
                        device_id=peer, device_id_type=MESH))
            return res

        return local, remote

    def start(self, ins, outs, sems):
        local, remote = self._ctx(ins, outs, sems)
        for cp in local() + remote(False):
            cp.start()

    def finish(self, ins, outs, sems):
        local, remote = self._ctx(ins, outs, sems)
        for cp in remote(True):
            cp.wait_recv()
        for cp in remote(False):
            cp.wait_send()
        for cp in local():
            cp.wait()


def _exchange_start(rider, name):
    n = rider.n
    parts = rider.arrays
    lands = [lax.empty(s.shape, s.dtype) for s in rider.out_shape()]
    hbm = pl.BlockSpec(memory_space=pltpu.HBM)
    sem = pl.BlockSpec(memory_space=pltpu.SEMAPHORE)

    def body(*refs):
        ins, sems = refs[:n], refs[2 * n:2 * n + 3]
        outs, token = refs[2 * n + 3 + n:2 * n + 3 + 2 * n], refs[-1]
        rider.start(ins, outs, sems)
        token[...] = jnp.zeros_like(token)

    res = pl.pallas_call(
        body, name=name,
        out_shape=rider.semaphores() + [pltpu.HBM(p.shape, p.dtype) for p in parts]
                  + [pltpu.HBM(z.shape, z.dtype) for z in lands] + [jax.ShapeDtypeStruct((8, 128), F32)],
        in_specs=[hbm] * (2 * n), out_specs=[sem] * 3 + [hbm] * (2 * n) + [pl.BlockSpec(memory_space=pltpu.VMEM)],
        input_output_aliases={i: 3 + i for i in range(2 * n)},
        compiler_params=pltpu.CompilerParams(has_side_effects=pltpu.SideEffectType.DATAFLOW_SIDE_EFFECTING),
    )(*[pltpu.with_memory_space_constraint(a, pltpu.HBM) for a in parts + lands])
    return res[:3], res[3:3 + n], res[3 + n:3 + 2 * n], res[-1]


def _exchange_wait(rider, started, after, name):
    n = rider.n
    sems, parts, lands, _ = started
    hbm = pl.BlockSpec(memory_space=pltpu.HBM)
    sem = pl.BlockSpec(memory_space=pltpu.SEMAPHORE)

    def body(*refs):
        rider.finish(refs[:n], refs[n:2 * n], refs[2 * n:2 * n + 3])

    res = pl.pallas_call(
        body, name=name, out_shape=[pltpu.HBM(a.shape, a.dtype) for a in list(parts) + list(lands)],
        in_specs=[hbm] * (2 * n) + [sem] * 3 + [ANY] * len(after), out_specs=[hbm] * (2 * n),
        input_output_aliases={i: i for i in range(2 * n)},
        compiler_params=pltpu.CompilerParams(has_side_effects=pltpu.SideEffectType.DATAFLOW_SIDE_EFFECTING),
    )(*parts, *lands, *sems, *after)
    return list(res[n:])


def _pcall(body, *, name, grid, in_specs, out_specs, out_shape, scratch_shapes=(), semantics, vmem_mb, rider=None,
           aliases=None, after=()):
    in_specs, out_specs, out_shape = list(in_specs), list(out_specs), list(out_shape)
    scratch_shapes = list(scratch_shapes)
    aliases = dict(aliases or {})
    if rider is None:
        n_in, after = len(in_specs), list(after)

        def plain(*refs):
            body(*refs[:n_in], *refs[n_in + len(after):])

        call = pl.pallas_call(plain if after else body, name=name, grid=grid, in_specs=in_specs + [ANY] * len(after),
                              out_specs=out_specs, out_shape=out_shape, scratch_shapes=scratch_shapes,
                              input_output_aliases=aliases, compiler_params=_params(semantics, vmem_mb))
        return lambda *args: (list(call(*args, *after)), None)
    n_in, n_out, n_scr = len(in_specs), len(out_specs), len(scratch_shapes)
    r_in, r_shapes = len(rider.arrays), rider.out_shape()
    r_out = len(r_shapes)
    aliases.update({n_in + i: n_out + o for i, o in rider.alias_pairs})
    total = math.prod(grid)
    mid_step = total - 1 if rider.has_mid and rider.late else (3 * total) // 4

    def wrapped(*refs):
        bounds = [0, n_in, r_in, n_out, r_out, n_scr]
        for i in range(1, len(bounds)):
            bounds[i] += bounds[i - 1]
        a, ra, o, ro, s = (refs[bounds[i]:bounds[i + 1]] for i in range(5))
        rs = refs[bounds[5]:]
        step = pl.program_id(0)
        for k in range(1, len(grid)):
            step = step * grid[k] + pl.program_id(k)
        pl.when(step == 0)(lambda: rider.start(ra, ro, rs))
        body(*a, *o, *s)
        if rider.has_mid:
            pl.when(step == mid_step)(lambda: rider.mid(ra, ro, rs))
        pl.when(step == total - 1)(lambda: rider.finish(ra, ro, rs))

    call = pl.pallas_call(
        wrapped, name=name, grid=grid, in_specs=in_specs + [ANY] * r_in, out_specs=out_specs + [ANY] * r_out,
        out_shape=out_shape + r_shapes, scratch_shapes=scratch_shapes + rider.scratch(),
        input_output_aliases=aliases, compiler_params=_params(("arbitrary",) * len(grid), vmem_mb))

    def run(*args):
        res = call(*args, *rider.arrays)
        return list(res[:n_out]), list(res[n_out:])

    return run


def _norm_matmul(x, g, w, *, name, out_dtype, tb, bn, relu=False, save_h=False, rider=None):
    t, d = x.shape
    n = w.shape[1]

    def body(x_ref, g_ref, w_ref, o_ref, *rest):
        h_scr = rest[-1]

        @pl.when(pl.program_id(1) == 0)
        def _():
            h = _rms_fwd(x_ref[...], g_ref[...])[0].astype(BF16)
            h_scr[...] = h
            if save_h:
                rest[0][...] = h

        acc = _dot(h_scr[...], w_ref[...])
        if relu:
            acc = jnp.maximum(acc, 0.0)
        o_ref[...] = acc.astype(out_dtype)

    out_shape = [jax.ShapeDtypeStruct((t, n), out_dtype)]
    out_specs = [pl.BlockSpec((tb, bn), lambda i, j: (i, j))]
    if save_h:
        out_shape.append(jax.ShapeDtypeStruct((t, d), BF16))
        out_specs.append(pl.BlockSpec((tb, d), lambda i, j: (i, 0)))
    res, extra = _pcall(
        body, name=name, grid=(t // tb, n // bn),
        in_specs=[pl.BlockSpec((tb, d), lambda i, j: (i, 0)),
                  pl.BlockSpec((1, d), lambda i, j: (0, 0)),
                  pl.BlockSpec((d, bn), lambda i, j: (0, j))],
        out_specs=out_specs, out_shape=out_shape,
        scratch_shapes=[pltpu.VMEM((tb, d), BF16)],
        semantics=("parallel", "arbitrary"), vmem_mb=48, rider=rider,
    )(x, g, w)
    res = res if save_h else res[0]
    return res if rider is None else (res, extra)


def _matmul_nt_normbwd(dy, w, x, g, dres, *, name, tb, also_bf16=False, to_natural=False, after=()):
    t, d = x.shape
    stacked = dy.ndim == 3
    has_res = dres is not None
    n_i = SEG // TI
    if to_natural:
        tb = N_RES * TI

    def body(dy_ref, w_ref, x_ref, g_ref, *rest):
        rest = list(rest)
        dres_ref = rest.pop(0) if has_res else None
        dx_ref = rest.pop(0)
        dxb_ref = rest.pop(0) if also_bf16 else None
        gg_ref = rest.pop(0)
        i = pl.program_id(0)

        def rows(ref, *lead):
            v = ref[lead] if lead else ref[...]
            return v[0].reshape(tb, v.shape[-1]) if to_natural else v

        if stacked:
            kb = dy_ref.shape[-1]
            dh = _dot_nt(rows(dy_ref, 0), w_ref[:, 0:kb])
            for s in range(1, dy_ref.shape[0]):
                dh = dh + _dot_nt(rows(dy_ref, s), w_ref[:, s * kb:(s + 1) * kb])
        else:
            dh = _dot_nt(rows(dy_ref), w_ref[...])
        g_v = g_ref[...]
        _, xh, r = _rms_fwd(rows(x_ref), g_v)
        dx = _rms_bwd(dh, xh, r, g_v)
        if has_res:
            dx = dx + rows(dres_ref)
        if to_natural:
            scr = rest.pop(0)
            for cb in range(d // BLK):
                cols = slice(cb * BLK, (cb + 1) * BLK)
                slab = scr.at[cb]
                for res in range(N_RES):
                    slab[pl.ds(res, TI, stride=N_RES), :] = dx[res * TI:(res + 1) * TI, cols]
                dx_ref[:, cols] = slab[...]
        else:
            dx_ref[...] = dx
        if also_bf16:
            dxb_ref[...] = dx.astype(BF16)
        part = jnp.sum(dh * xh, axis=0, keepdims=True)

        @pl.when(i == 0)
        def _():
            gg_ref[...] = part

        @pl.when(i != 0)
        def _():
            gg_ref[...] += part

    tok = pl.BlockSpec((tb, d), lambda i: (i, 0))
    row = pl.BlockSpec((1, d), lambda i: (0, 0))
    if to_natural:
        act = pl.BlockSpec((1, N_RES, TI, d), lambda i: (i // n_i, 0, i % n_i, 0))
        dy_spec = pl.BlockSpec((dy.shape[0], 1, N_RES, TI, dy.shape[2]), lambda i: (0, i // n_i, 0, i % n_i, 0))
        dy, x = dy.reshape(dy.shape[0], t // HALF, N_RES, SEG, dy.shape[2]), _x4(x)
        dres = _x4(dres) if has_res else None
    elif stacked:
        act, dy_spec = tok, pl.BlockSpec((dy.shape[0], tb, dy.shape[2]), lambda i: (0, i, 0))
    else:
        act, dy_spec = tok, pl.BlockSpec((tb, dy.shape[1]), lambda i: (i, 0))
    in_specs = [dy_spec, pl.BlockSpec(w.shape, lambda i: (0, 0)), act, row]
    args = [dy, w, x, g]
    if has_res:
        in_specs.append(act)
        args.append(dres)
    out_specs = [tok] + ([tok] if also_bf16 else []) + [row]
    out_shape = ([jax.ShapeDtypeStruct((t, d), F32)] + ([jax.ShapeDtypeStruct((t, d), BF16)] if also_bf16 else [])
                 + [jax.ShapeDtypeStruct((1, d), F32)])
    res, _ = _pcall(
        body, name=name, grid=(t // tb,), in_specs=in_specs, out_specs=out_specs, out_shape=out_shape,
        scratch_shapes=[pltpu.VMEM((d // BLK, tb, BLK), F32)] if to_natural else [],
        semantics=("arbitrary",), vmem_mb=56, after=after,
    )(*args)
    return res


def _matmul_tn(a, b, *, name, bm, bn, square_a=False, after=()):
    t, m = a.shape
    stacked = b.ndim == 3
    n = b.shape[0] * bn if stacked else b.shape[1]

    def body(a_ref, b_ref, o_ref):
        av = a_ref[...]
        if square_a:
            av = av.astype(F32)
            av = (av * av).astype(BF16)
        o_ref[...] = _dot_tn(av, b_ref[...]).astype(BF16)

    res, _ = _pcall(
        body, name=name, grid=(m // bm, n // bn),
        in_specs=[pl.BlockSpec((t, bm), lambda i, j: (0, i)),
                  pl.BlockSpec((None, t, bn), lambda i, j: (j, 0, 0)) if stacked
                  else pl.BlockSpec((t, bn), lambda i, j: (0, j))],
        out_specs=[pl.BlockSpec((bm, bn), lambda i, j: (i, j))], out_shape=[jax.ShapeDtypeStruct((m, n), BF16)],
        semantics=("parallel", "parallel"), vmem_mb=56, after=after,
    )(a, b)
    return res[0]


N_RES = 16
SEG = 128
HALF = N_RES * SEG
TI = 32


def _x4(a):
    return a.reshape(a.shape[0] // HALF, N_RES, SEG, a.shape[1])


def _reorder(arrays, name, rider=None):
    t, c = arrays[0].shape
    n = len(arrays)
    n_i = SEG // TI

    def body(*refs):
        scr = refs[-1]
        for i_ref, o_ref in zip(refs[:n], refs[n:2 * n]):
            for cb in range(c // BLK):
                cols = slice(cb * BLK, (cb + 1) * BLK)
                slab = scr.at[cb]
                slab[...] = i_ref[:, cols]
                for r in range(N_RES):
                    o_ref[0, r, :, cols] = slab[pl.ds(r, TI, stride=N_RES), :]

    res, extra = _pcall(
        body, name=name, grid=(t // (TI * N_RES),),
        in_specs=[pl.BlockSpec((TI * N_RES, c), lambda s: (s, 0))] * n,
        out_specs=[pl.BlockSpec((1, N_RES, TI, c), lambda s: (s // n_i, 0, s % n_i, 0))] * n,
        out_shape=[jax.ShapeDtypeStruct((t // HALF, N_RES, SEG, c), F32)] * n,
        scratch_shapes=[pltpu.VMEM((c // BLK, TI * N_RES, BLK), F32)],
        semantics=("parallel",), vmem_mb=32, rider=rider,
    )(*arrays)
    res = [r.reshape(t, c) for r in res]
    return res if rider is None else (res, extra)


_PATTERNS = ((1, 16, 8, SEG), (4, 4, 32, 4 * SEG), (16, 1, SEG, 0))
_FIRST = {1: 1, 4: 4, 16: 16}


def _group_rows(d, g):
    a = g >> 4
    if d == 16:
        base = a * HALF + (g & 15) * SEG
        prev = base - HALF
    elif d == 4:
        c = (g >> 2) & 3
        base = a * HALF + (g & 3) * SEG + c * 32
        prev = jnp.where(c > 0, base - 32, base - HALF + 96)
    else:
        c = g & 15
        base = a * HALF + c * 8
        prev = jnp.where(c > 0, base - 8, base - HALF + 120)
    return base, prev


def _load_rows(ref, base, n, rows, stride):
    parts = [ref[pl.ds(pl.multiple_of(base + j * stride, 8), rows), :] for j in range(n)]
    return parts[0] if n == 1 else jnp.concatenate(parts, axis=0)


def _store_rows(ref, base, val, n, rows, stride, add=False):
    for j in range(n):
        sl = pl.ds(pl.multiple_of(base + j * stride, 8), rows)
        piece = val[j * rows:(j + 1) * rows, :]
        if add:
            ref[sl, :] += piece
        else:
            ref[sl, :] = piece


def _band_bias(n, rows):
    shift = rows.bit_length() - 1
    lq = lax.broadcasted_iota(jnp.int32, (BLK, BLK), 0)
    lk = lax.broadcasted_iota(jnp.int32, (BLK, BLK), 1)
    iq = (lq & (rows - 1)) * n + (lq >> shift)
    ik = (lk & (rows - 1)) * n + (lk >> shift)
    zero = jnp.zeros((BLK, BLK), F32)
    return jnp.where(ik >= iq, zero, NEG_INF), jnp.where(ik <= iq, zero, NEG_INF)


def _set_bias(bias_scr, n, rows):
    prev_b, cur_b = _band_bias(n, rows)
    for half in range(2):
        bias_scr[half * BLK:(half + 1) * BLK, 0:BLK] = prev_b
        bias_scr[half * BLK:(half + 1) * BLK, BLK:2 * BLK] = cur_b


SCALE = 1.0 / math.sqrt(HEAD_DIM)


def _head_consts(value=1.0):
    lane_lo = lax.broadcasted_iota(jnp.int32, (BLK, BLK), 1) < HEAD_DIM
    return lane_lo, [jnp.where(lane_lo, value, 0.0).astype(BF16), jnp.where(lane_lo, 0.0, value).astype(BF16)]


def _stack_heads(v, head_mask):
    return jnp.concatenate([v * head_mask[0], v * head_mask[1]], axis=0)


def _unstack_heads(v2, lane_lo):
    return jnp.where(lane_lo, v2[:BLK], v2[BLK:])


def _rows_per_head(v, lane_lo):
    rolled = pltpu.roll(v, HEAD_DIM, axis=1)
    return jnp.concatenate([jnp.where(lane_lo, v, rolled), jnp.where(lane_lo, rolled, v)], axis=0)


WIDTH = 4


def _loop(lo, hi, fn, width=None):
    if width is None:
        def body(g, carry):
            fn(g)
            return carry

        if hi > lo:
            lax.fori_loop(lo, hi, body, 0)
        return
    while hi > lo:
        trips = (hi - lo) // width
        if trips:
            def body(i, carry, lo=lo, width=width):
                fn([lo + width * i + j for j in range(width)])
                return carry

            lax.fori_loop(0, trips, body, 0)
            lo += trips * width
        width = max(1, width // 2)


def _mix_weights(l1, l2, l3):
    mx = jnp.maximum(jnp.maximum(l1, l2), l3)
    e1, e2, e3 = jnp.exp(l1 - mx), jnp.exp(l2 - mx), jnp.exp(l3 - mx)
    inv = 1.0 / (e1 + e2 + e3)
    return e1 * inv, e2 * inv, e3 * inv


def _attention_fwd(qkv, rider=None):
    t = qkv.shape[0]
    groups = 16 * (t // HALF)

    def body(q_ref, k_ref, v_ref, attn_ref, l1_ref, l2_ref, l3_ref, o_scr, bias_scr):
        lane_lo, q_mask = _head_consts(SCALE)
        l_refs = (l1_ref, l2_ref, l3_ref)
        for p, (d, n, rows, stride) in enumerate(_PATTERNS):
            _set_bias(bias_scr, n, rows)
            o_p, l_p = o_scr.at[p], l_refs[p]

            def block(gs, has_prev):
                at = [_group_rows(d, g) for g in gs]

                def load(ref, b):
                    return _load_rows(ref, b, n, rows, stride).astype(BF16)

                q2 = [_stack_heads(load(q_ref, b), q_mask) for b, _ in at]
                k2 = [load(k_ref, b) for b, _ in at]
                v2 = [load(v_ref, b) for b, _ in at]
                if has_prev:
                    k2 = [jnp.concatenate([load(k_ref, pv), k], axis=0) for (_, pv), k in zip(at, k2)]
                    v2 = [jnp.concatenate([load(v_ref, pv), v], axis=0) for (_, pv), v in zip(at, v2)]
                s = [_dot_nt(q, k) for q, k in zip(q2, k2)]
                s = [x + (bias_scr[...] if has_prev else bias_scr[:, BLK:2 * BLK]) for x in s]
                mx = [jnp.max(x, axis=1, keepdims=True) for x in s]
                e = [jnp.exp(x - m) for x, m in zip(s, mx)]
                den = [jnp.sum(x, axis=1, keepdims=True) for x in e]
                o2 = [_dot(x.astype(BF16), v) * (1.0 / dn) for x, v, dn in zip(e, v2, den)]
                lse2 = [jnp.broadcast_to(m + jnp.log(dn), (2 * BLK, BLK)) for m, dn in zip(mx, den)]
                for (b, _), o, l in zip(at, o2, lse2):
                    _store_rows(o_p, b, _unstack_heads(o, lane_lo), n, rows, stride)
                    _store_rows(l_p, b, _unstack_heads(l, lane_lo), n, rows, stride)

            _loop(0, _FIRST[d], lambda gs: block(gs, False), width=2 * WIDTH)
            _loop(_FIRST[d], groups, lambda gs: block(gs, True), width=2 * WIDTH)

        def mix(i):
            sl = pl.ds(pl.multiple_of(i * 256, 256), 256)
            w = _mix_weights(l1_ref[sl, :], l2_ref[sl, :], l3_ref[sl, :])
            attn_ref[sl, :] = w[0] * o_scr[0, sl, :] + w[1] * o_scr[1, sl, :] + w[2] * o_scr[2, sl, :]

        _loop(0, t // 256, mix)

    def col(c0):
        return pl.BlockSpec((t, BLK), lambda hp: (0, c0 + hp))

    res, extra = _pcall(
        body, name="attention_fwd", grid=(4,), in_specs=[col(0), col(4), col(8)], out_specs=[col(0)] * 4,
        out_shape=[jax.ShapeDtypeStruct((t, 512), F32)] * 4,
        scratch_shapes=[pltpu.VMEM((3, t, BLK), F32), pltpu.VMEM((2 * BLK, 2 * BLK), F32)],
        semantics=("parallel",), vmem_mb=48, rider=rider,
    )(qkv, qkv, qkv)
    return res if rider is None else (res, extra)


def _attention_bwd(qkv, dattn, dsum, lses, dproj):
    t = qkv.shape[0]
    groups = 16 * (t // HALF)

    def body(q_ref, k_ref, v_ref, da_ref, ds_ref, l1_ref, l2_ref, l3_ref, kept_ref, out_ref, acc, bias_scr):
        del kept_ref
        lane_lo, head_mask = _head_consts()
        q_mask = _head_consts(SCALE)[1]
        l_refs = (l1_ref, l2_ref, l3_ref)

        def clear(i):
            sl = pl.ds(pl.multiple_of(i * 512, 512), 512)
            for s in range(3):
                acc[s, sl, :] = jnp.zeros((512, BLK), F32)

        _loop(0, t // 512, clear)
        dq_acc, dk_acc, dv_acc = acc.at[0], acc.at[1], acc.at[2]
        for p, (d, n, rows, stride) in enumerate(_PATTERNS):
            _set_bias(bias_scr, n, rows)

            def block(gs, has_prev):
                at = [_group_rows(d, g) for g in gs]

                def load(ref, b):
                    return _load_rows(ref, b, n, rows, stride)

                def put(ref, b, val):
                    _store_rows(ref, b, val, n, rows, stride, add=True)

                def wide(x):
                    return jnp.concatenate([x, x], axis=1) if has_prev else x

                lse = [[load(ref, b) for ref in l_refs] for b, _ in at]
                w = [_mix_weights(*ls)[p] for ls in lse]
                do2 = [_stack_heads((wg * load(da_ref, b)).astype(BF16), head_mask) for wg, (b, _) in zip(w, at)]
                dl2 = [wide(_rows_per_head(wg * load(ds_ref, b), lane_lo)) for wg, (b, _) in zip(w, at)]
                lse2 = [wide(_rows_per_head(ls[p], lane_lo)) for ls in lse]
                q2 = [_stack_heads(load(q_ref, b).astype(BF16), q_mask) for b, _ in at]
                k2 = [load(k_ref, b).astype(BF16) for b, _ in at]
                v2 = [load(v_ref, b).astype(BF16) for b, _ in at]
                if has_prev:
                    k2 = [jnp.concatenate([load(k_ref, pv).astype(BF16), k], axis=0) for (_, pv), k in zip(at, k2)]
                    v2 = [jnp.concatenate([load(v_ref, pv).astype(BF16), v], axis=0) for (_, pv), v in zip(at, v2)]
                s = [_dot_nt(q, k) for q, k in zip(q2, k2)]
                dp = [_dot_nt(do, v) for do, v in zip(do2, v2)]
                pr = [jnp.exp(x + (bias_scr[...] if has_prev else bias_scr[:, BLK:2 * BLK]) - l)
                      for x, l in zip(s, lse2)]
                ds = [(pg * (x - dl)).astype(BF16) for pg, x, dl in zip(pr, dp, dl2)]
                dq2 = [_dot(x, k) * SCALE for x, k in zip(ds, k2)]
                dk2 = [_dot_tn(x, q) for x, q in zip(ds, q2)]
                dv2 = [_dot_tn(pg.astype(BF16), do) for pg, do in zip(pr, do2)]
                for (b, pv), dq, dk, dv in zip(at, dq2, dk2, dv2):
                    put(dq_acc, b, _unstack_heads(dq, lane_lo))
                    if has_prev:
                        put(dk_acc, pv, dk[:BLK])
                        put(dv_acc, pv, dv[:BLK])
                        put(dk_acc, b, dk[BLK:])
                        put(dv_acc, b, dv[BLK:])
                    else:
                        put(dk_acc, b, dk)
                        put(dv_acc, b, dv)

            _loop(0, _FIRST[d], lambda gs: block(gs, False), width=WIDTH)
            _loop(_FIRST[d], groups, lambda gs: block(gs, True), width=WIDTH)

        def emit(i):
            sl = pl.ds(pl.multiple_of(i * 512, 512), 512)
            for s in range(3):
                out_ref[s, sl, :] = acc[s, sl, :].astype(BF16)

        _loop(0, t // 512, emit)

    def col(c0):
        return pl.BlockSpec((t, BLK), lambda hp: (0, c0 + hp))

    res, _ = _pcall(
        body, name="attention_bwd", grid=(4,),
        in_specs=[col(0), col(4), col(8)] + [col(0)] * 5 + [ANY],
        out_specs=[pl.BlockSpec((3, t, BLK), lambda hp: (0, 0, hp))],
        out_shape=[jax.ShapeDtypeStruct(dproj.shape, BF16)],
        scratch_shapes=[pltpu.VMEM((3, t, BLK), F32), pltpu.VMEM((2 * BLK, 2 * BLK), F32)],
        semantics=("parallel",), vmem_mb=56, aliases={8: 0},
    )(qkv, qkv, qkv, dattn, dsum, *lses, dproj)
    return res[0]


def _order_specs(t):
    n_i = SEG // TI
    nblk = (t // HALF) * n_i
    per = TI // 8

    def main(c, col=0):
        return pl.BlockSpec((1, N_RES, TI, c), lambda s: (s // n_i, 0, s % n_i, col))

    def before(c, col=0):
        return pl.BlockSpec((1, 2, 8, c), lambda s: (jnp.maximum(s - 1, 0) // n_i, N_RES // 2 - 1,
                                                     (jnp.maximum(s - 1, 0) % n_i) * per + per - 1, col))

    def after(c, col=0):
        return pl.BlockSpec((1, 2, 8, c), lambda s: (jnp.minimum(s + 1, nblk - 1) // n_i, 0,
                                                     (jnp.minimum(s + 1, nblk - 1) % n_i) * per, col))

    return nblk, main, before, after


def _shift_in(v, row_in, up):
    rows = v.shape[0]
    idx = lax.broadcasted_iota(jnp.int32, v.shape, 0)
    fill = jnp.broadcast_to(row_in, v.shape)
    if up:
        return jnp.where(idx == rows - 1, fill, pltpu.roll(v, rows - 1, axis=0))
    return jnp.where(idx == 0, fill, pltpu.roll(v, 1, axis=0))


def _taps_behind(u, before):
    s15 = _shift_in(u[N_RES - 1], before[1, 7:8, :], up=False)
    s14 = _shift_in(u[N_RES - 2], before[0, 7:8, :], up=False)
    m1 = jnp.concatenate([s15[None], u[:N_RES - 1]], axis=0)
    m2 = jnp.concatenate([s14[None], s15[None], u[:N_RES - 2]], axis=0)
    return m1, m2


def _taps_ahead(u, after):
    t0 = _shift_in(u[0], after[0, 0:1, :], up=True)
    t1 = _shift_in(u[1], after[1, 0:1, :], up=True)
    p1 = jnp.concatenate([u[1:], t0[None]], axis=0)
    p2 = jnp.concatenate([u[2:], t0[None], t1[None]], axis=0)
    return p1, p2


def _conv_fwd(gates, before, first, cw):
    bg, cg, xc = gates[..., 0:512], gates[..., 512:1024], gates[..., 1024:1536]
    u = cg * xc
    ub = before[..., 512:1024] * before[..., 1024:1536]
    ub = jnp.where(first, jnp.zeros_like(ub), ub)
    m1, m2 = _taps_behind(u, ub)
    conv = m2 * cw[0:1, :] + m1 * cw[1:2, :] + u * cw[2:3, :]
    return bg, u, m1, m2, conv


def _sum_tokens(v):
    return jnp.sum(jnp.sum(v, axis=0), axis=0, keepdims=True)


def _mixer_fwd(x, attn, gates, cw, g_a, g_c, w_out):
    t, d = x.shape
    nblk, main, before, _ = _order_specs(t)
    rows = N_RES * TI

    def body(x_ref, at_ref, gt_ref, gb_ref, cw_ref, ga_ref, gc_ref, wa_ref, wb_ref, x1_ref, mg_ref):
        an = _rms_fwd(at_ref[0], ga_ref[...])[0].astype(BF16)
        bg, _, _, _, conv = _conv_fwd(gt_ref[0], gb_ref[0], pl.program_id(0) == 0, cw_ref[...])
        cn = _rms_fwd(bg * conv, gc_ref[...])[0].astype(BF16)
        mg_ref[0, :, :, 0:512] = an
        mg_ref[0, :, :, 512:1024] = cn
        y = _dot(an.reshape(rows, 512), wa_ref[...]) + _dot(cn.reshape(rows, 512), wb_ref[...])
        x1_ref[0] = x_ref[0] + y.reshape(N_RES, TI, d)

    const = lambda r, c, i0=0: pl.BlockSpec((r, c), lambda s: (i0, 0))
    x1, merged = pl.pallas_call(
        body, name="mixer_fwd", grid=(nblk,),
        in_specs=[main(d), main(512), main(1536, 1), before(1536, 1), const(3, 512), const(1, 512), const(1, 512),
                  const(512, d), const(512, d, 1)],
        out_specs=[main(d), main(d)],
        out_shape=[jax.ShapeDtypeStruct(_x4(x).shape, F32), jax.ShapeDtypeStruct(_x4(x).shape, BF16)],
        compiler_params=_params(("parallel",), 48),
    )(_x4(x), _x4(attn), _x4(gates), _x4(gates), cw, g_a, g_c, w_out, w_out)
    return x1.reshape(t, d), merged.reshape(t, d)


def _mixer_bwd(dx1, attn, gates, cw, g_a, g_c, w_out, head_sum, after=()):
    t, d = dx1.shape
    nblk, main, before, _ = _order_specs(t)
    rows = N_RES * TI

    def body(dx_ref, at_ref, gt_ref, gb_ref, cw_ref, ga_ref, gc_ref, wa_ref, wb_ref, hs_ref,
             da_ref, dsum_ref, dy_ref, gga_ref, ggc_ref):
        s = pl.program_id(0)
        dxb = dx_ref[0].reshape(rows, d).astype(BF16)
        dma = _dot_nt(dxb, wa_ref[...]).reshape(N_RES, TI, 512)
        dmc = _dot_nt(dxb, wb_ref[...]).reshape(N_RES, TI, 512)
        attn_v, g_av = at_ref[0], ga_ref[...]
        _, ah, ra = _rms_fwd(attn_v, g_av)
        dattn = _rms_bwd(dma, ah, ra, g_av)
        da_ref[0] = dattn
        z = (dattn * attn_v).reshape(rows, 512)
        hs = hs_ref[...]
        z1 = z.astype(BF16)
        z2 = (z - z1.astype(F32)).astype(BF16)
        dsum_ref[0] = (_dot(z1, hs) + _dot(z2, hs)).reshape(N_RES, TI, 512)
        bg, _, _, _, conv = _conv_fwd(gt_ref[0], gb_ref[0], s == 0, cw_ref[...])
        g_cv = gc_ref[...]
        _, yh, rc = _rms_fwd(bg * conv, g_cv)
        dy_ref[0] = _rms_bwd(dmc, yh, rc, g_cv)
        pa, pc = _sum_tokens(dma * ah), _sum_tokens(dmc * yh)

        @pl.when(s == 0)
        def _():
            gga_ref[...] = pa
            ggc_ref[...] = pc

        @pl.when(s != 0)
        def _():
            gga_ref[...] += pa
            ggc_ref[...] += pc

    const = lambda r, c, i0=0: pl.BlockSpec((r, c), lambda s: (i0, 0))
    shape4 = _x4(attn).shape
    res, _ = _pcall(
        body, name="mixer_bwd", grid=(nblk,),
        in_specs=[main(d), main(512), main(1536, 1), before(1536, 1), const(3, 512), const(1, 512), const(1, 512),
                  const(512, d), const(512, d, 1), const(512, 512)],
        out_specs=[main(512)] * 3 + [const(1, 512), const(1, 512)],
        out_shape=[jax.ShapeDtypeStruct(shape4, F32)] * 3 + [jax.ShapeDtypeStruct((1, 512), F32)] * 2,
        semantics=("arbitrary",), vmem_mb=48, after=after,
    )(_x4(dx1), _x4(attn), _x4(gates), _x4(gates), cw, g_a, g_c, w_out, w_out, head_sum)
    return [r.reshape(t, 512) for r in res[:3]] + res[3:]


def _conv_bwd(dy, gates, cw):
    t = dy.shape[0]
    nblk, main, before, after = _order_specs(t)
    n_i = SEG // TI

    def body(dy_ref, dya_ref, gt_ref, gb_ref, ga_ref, cw_ref, dp_ref, gcw_ref):
        s = pl.program_id(0)
        cw_v, gates_v = cw_ref[...], gt_ref[0]
        bg, u, m1, m2, conv = _conv_fwd(gates_v, gb_ref[0], s == 0, cw_v)
        dy_v = dy_ref[0]
        dconv = dy_v * bg
        dca = dya_ref[0] * ga_ref[0][..., 0:512]
        dca = jnp.where(s == nblk - 1, jnp.zeros_like(dca), dca)
        p1, p2 = _taps_ahead(dconv, dca)
        du = dconv * cw_v[2:3, :] + p1 * cw_v[1:2, :] + p2 * cw_v[0:1, :]
        dp_ref[0, 0] = (dy_v * conv).astype(BF16)
        dp_ref[1, 0] = (du * gates_v[..., 1024:1536]).astype(BF16)
        dp_ref[2, 0] = (du * gates_v[..., 512:1024]).astype(BF16)
        parts = [_sum_tokens(dconv * m2), _sum_tokens(dconv * m1), _sum_tokens(dconv * u)]

        @pl.when(s == 0)
        def _():
            gcw_ref[...] = jnp.zeros_like(gcw_ref)

        for tap in range(3):
            gcw_ref[tap:tap + 1, :] += parts[tap]

    (dproj, gcw), _ = _pcall(
        body, name="conv_bwd", grid=(nblk,),
        in_specs=[main(512), after(512), main(1536, 1), before(1536, 1), after(1536, 1),
                  pl.BlockSpec((3, 512), lambda s: (0, 0))],
        out_specs=[pl.BlockSpec((3, 1, N_RES, TI, 512), lambda s: (1, s // n_i, 0, s % n_i, 0)),
                   pl.BlockSpec((8, 512), lambda s: (0, 0))],
        out_shape=[jax.ShapeDtypeStruct((6, t // HALF, N_RES, SEG, 512), BF16), jax.ShapeDtypeStruct((8, 512), F32)],
        semantics=("arbitrary",), vmem_mb=40,
    )(_x4(dy), _x4(dy), _x4(gates), _x4(gates), _x4(gates), cw)
    return dproj.reshape(6, t, 512), gcw


def _xattn_fwd(x1, g, w_q, kv, w_o, *, tb):
    t, d = x1.shape
    hd = d // N_MEM_HEADS
    m = kv.shape[0]

    def body(x_ref, g_ref, wq_ref, k_ref, v_ref, wo_ref, x2_ref, h_ref, q_ref, o_ref):
        xv = x_ref[...]
        h = _rms_fwd(xv, g_ref[...])[0].astype(BF16)
        h_ref[...] = h
        q = _dot(h, wq_ref[...]).astype(BF16)
        q_ref[...] = q
        for hh in range(N_MEM_HEADS):
            sl = slice(hh * hd, (hh + 1) * hd)
            s = _dot_nt(q[:, sl], k_ref[:, sl]) * (1.0 / 16.0)
            e = jnp.exp(s - jnp.max(s, axis=1, keepdims=True))
            p = e / jnp.sum(e, axis=1, keepdims=True)
            o_ref[:, sl] = _dot(p.astype(BF16), v_ref[:, sl]).astype(BF16)
        x2_ref[...] = xv + _dot(o_ref[...], wo_ref[...])

    tok = pl.BlockSpec((tb, d), lambda i: (i, 0))
    full = pl.BlockSpec((d, d), lambda i: (0, 0))
    return pl.pallas_call(
        body, name="xattn_fwd", grid=(t // tb,),
        in_specs=[tok, pl.BlockSpec((1, d), lambda i: (0, 0)), full,
                  pl.BlockSpec((m, d), lambda i: (0, 0)), pl.BlockSpec((m, d), lambda i: (0, 1)), full],
        out_specs=[tok] * 4,
        out_shape=[jax.ShapeDtypeStruct((t, d), F32)] + [jax.ShapeDtypeStruct((t, d), BF16)] * 3,
        compiler_params=_params(("parallel",), 48),
    )(x1, g, w_q, kv, kv, w_o)


def _xattn_bwd(dx2, x1, g, q, w_q, kv, w_o, *, tb, after=()):
    t, d = x1.shape
    hd = d // N_MEM_HEADS
    m = kv.shape[0]

    def body(dx2_ref, x_ref, g_ref, q_ref, wq_ref, k_ref, v_ref, wo_ref,
             dx1_ref, dx1b_ref, dq_ref, dk_ref, dv_ref, gg_ref):
        i = pl.program_id(0)

        @pl.when(i == 0)
        def _():
            dk_ref[...] = jnp.zeros_like(dk_ref)
            dv_ref[...] = jnp.zeros_like(dv_ref)

        dx2 = dx2_ref[...]
        do = _dot_nt(dx2.astype(BF16), wo_ref[...]).astype(BF16)
        for hh in range(N_MEM_HEADS):
            sl = slice(hh * hd, (hh + 1) * hd)
            qh, kh, vh, doh = q_ref[:, sl], k_ref[:, sl], v_ref[:, sl], do[:, sl]
            s = _dot_nt(qh, kh) * (1.0 / 16.0)
            e = jnp.exp(s - jnp.max(s, axis=1, keepdims=True))
            p = e / jnp.sum(e, axis=1, keepdims=True)
            dp = _dot_nt(doh, vh)
            ds = (p * (dp - jnp.sum(dp * p, axis=1, keepdims=True)) * (1.0 / 16.0)).astype(BF16)
            dq_ref[:, sl] = _dot(ds, kh).astype(BF16)
            dk_ref[:, sl] += _dot_tn(ds, qh)
            dv_ref[:, sl] += _dot_tn(p.astype(BF16), doh)
        dh = _dot_nt(dq_ref[...], wq_ref[...])
        g_v = g_ref[...]
        _, xh, r = _rms_fwd(x_ref[...], g_v)
        dx1 = dx2 + _rms_bwd(dh, xh, r, g_v)
        dx1_ref[...] = dx1
        dx1b_ref[...] = dx1.astype(BF16)
        part = jnp.sum(dh * xh, axis=0, keepdims=True)

        @pl.when(i == 0)
        def _():
            gg_ref[...] = part

        @pl.when(i != 0)
        def _():
            gg_ref[...] += part

    tok = pl.BlockSpec((tb, d), lambda i: (i, 0))
    full = pl.BlockSpec((d, d), lambda i: (0, 0))
    acc = pl.BlockSpec((m, d), lambda i: (0, 0))
    res, _ = _pcall(
        body, name="xattn_bwd", grid=(t // tb,),
        in_specs=[tok, tok, pl.BlockSpec((1, d), lambda i: (0, 0)), tok, full,
                  pl.BlockSpec((m, d), lambda i: (0, 0)), pl.BlockSpec((m, d), lambda i: (0, 1)), full],
        out_specs=[tok, tok, tok, acc, acc, pl.BlockSpec((1, d), lambda i: (0, 0))],
        out_shape=[jax.ShapeDtypeStruct((t, d), F32), jax.ShapeDtypeStruct((t, d), BF16),
                   jax.ShapeDtypeStruct((t, d), BF16),
                   jax.ShapeDtypeStruct((m, d), F32), jax.ShapeDtypeStruct((m, d), F32),
                   jax.ShapeDtypeStruct((1, d), F32)],
        semantics=("arbitrary",), vmem_mb=48, after=after,
    )(dx2, x1, g, q, w_q, kv, kv, w_o)
    return res


def _mlp_down_loss(a, w_down, x2, tgt, g, *, tb):
    t, d = x2.shape
    f = a.shape[1]

    def body(a_ref, w_ref, x_ref, t_ref, g_ref, dx_ref, dxb_ref, loss_ref, gg_ref):
        i = pl.program_id(0)
        av = a_ref[...].astype(F32)
        x3 = x_ref[...] + _dot((av * av).astype(BF16), w_ref[...])
        g_v = g_ref[...]
        out, xh, r = _rms_fwd(x3, g_v)
        err = out - t_ref[...]
        dout = err * (1.0 / d)
        dx = _rms_bwd(dout, xh, r, g_v)
        dx_ref[...] = dx
        dxb_ref[...] = dx.astype(BF16)
        part = jnp.sum(dout * xh, axis=0, keepdims=True)
        lpart = 0.5 * jnp.sum(jnp.mean(err * err, axis=-1, keepdims=True), axis=0, keepdims=True)
        lpart = jnp.broadcast_to(lpart, loss_ref.shape)

        @pl.when(i == 0)
        def _():
            gg_ref[...] = part
            loss_ref[...] = lpart

        @pl.when(i != 0)
        def _():
            gg_ref[...] += part
            loss_ref[...] += lpart

    tok = pl.BlockSpec((tb, d), lambda i: (i, 0))
    return pl.pallas_call(
        body, name="mlp_down_loss", grid=(t // tb,),
        in_specs=[pl.BlockSpec((tb, f), lambda i: (i, 0)), pl.BlockSpec((f, d), lambda i: (0, 0)), tok, tok,
                  pl.BlockSpec((1, d), lambda i: (0, 0))],
        out_specs=[tok, tok, pl.BlockSpec((8, 128), lambda i: (0, 0)), pl.BlockSpec((1, d), lambda i: (0, 0))],
        out_shape=[jax.ShapeDtypeStruct((t, d), F32), jax.ShapeDtypeStruct((t, d), BF16),
                   jax.ShapeDtypeStruct((8, 128), F32), jax.ShapeDtypeStruct((1, d), F32)],
        compiler_params=_params(("arbitrary",), 56),
    )(a, w_down, x2, tgt, g)


def _mlp_dpre(dx3, w_down, a, *, tb, bn):
    t, d = dx3.shape
    f = a.shape[1]

    def body(dx_ref, w_ref, a_ref, o_ref):
        o_ref[...] = (2.0 * a_ref[...].astype(F32) * _dot_nt(dx_ref[...], w_ref[...])).astype(BF16)

    return pl.pallas_call(
        body, name="mlp_dpre", grid=(t // tb, f // bn),
        in_specs=[pl.BlockSpec((tb, d), lambda i, j: (i, 0)), pl.BlockSpec((bn, d), lambda i, j: (j, 0)),
                  pl.BlockSpec((tb, bn), lambda i, j: (i, j))],
        out_specs=pl.BlockSpec((tb, bn), lambda i, j: (i, j)),
        out_shape=jax.ShapeDtypeStruct((t, f), BF16),
        compiler_params=_params(("parallel", "arbitrary"), 48),
    )(dx3, w_down, a)


def _adamw(gsum, w, m, v):
    m_new = ADAM_B1 * m + (1.0 - ADAM_B1) * gsum
    v_new = ADAM_B2 * v + (1.0 - ADAM_B2) * (gsum * gsum)
    m_hat = m_new / (1.0 - ADAM_B1 ** ADAM_STEP)
    v_hat = v_new / (1.0 - ADAM_B2 ** ADAM_STEP)
    delta = -ADAM_LR * (m_hat / (jnp.sqrt(v_hat) + ADAM_EPS) + ADAM_WD * w)
    return delta, m_new, v_new


def _sum_adamw(parts, w, m, v, *, name, tr):
    r, c = w.shape

    def body(p_ref, w_ref, m_ref, v_ref, g_ref, d_ref, mo_ref, vo_ref):
        g = p_ref[0].astype(F32)
        for k in range(1, N_DEV):
            g = g + p_ref[k].astype(F32)
        g_ref[...] = g
        d_ref[...], mo_ref[...], vo_ref[...] = _adamw(g, w_ref[...], m_ref[...], v_ref[...])

    blk = pl.BlockSpec((tr, c), lambda i: (i, 0))
    return pl.pallas_call(
        body, name=name, grid=(r // tr,),
        in_specs=[pl.BlockSpec((N_DEV, tr, c), lambda i: (0, i, 0)), blk, blk, blk],
        out_specs=[blk] * 4, out_shape=[jax.ShapeDtypeStruct((r, c), F32)] * 4,
        compiler_params=_params(("parallel",), 40),
    )(parts, w, m, v)


def _sum_small(parts):
    _, r, c = parts.shape

    def body(p_ref, o_ref):
        s = p_ref[0]
        for k in range(1, N_DEV):
            s = s + p_ref[k]
        o_ref[...] = s

    return pl.pallas_call(body, name="sum_small", out_shape=jax.ShapeDtypeStruct((r, c), F32))(parts)


def _adamw_small(g, w, m, v):
    def body(g_ref, w_ref, m_ref, v_ref, d_ref, mo_ref, vo_ref):
        d_ref[...], mo_ref[...], vo_ref[...] = _adamw(g_ref[...], w_ref[...], m_ref[...], v_ref[...])

    return pl.pallas_call(body, name="adamw_small", out_shape=[jax.ShapeDtypeStruct(g.shape, F32)] * 3)(g, w, m, v)


def _head_sum_matrix():
    r = lax.broadcasted_iota(jnp.int32, (512, 512), 0) // HEAD_DIM
    c = lax.broadcasted_iota(jnp.int32, (512, 512), 1) // HEAD_DIM
    return (r == c).astype(BF16)


_SHARD_AXIS = dict(w_in=1, w_out=0, w_q=0, w_kv=1, w_o=0, w_up=1, w_down=0, conv_w=None, small=None)


class _Weights:
    def __init__(self, full, shards=None):
        self.full = dict(full)
        self.shards = shards

    def rider(self, names, late=False):
        if self.shards is None:
            return None
        return _Gather([self.shards[n] for n in names], [_SHARD_AXIS[n] for n in names], late)

    def arrived(self, names, gathered):
        if gathered is not None:
            for n, g in zip(names, gathered):
                self.full[n] = g.transpose(1, 0, 2).reshape(g.shape[1], -1) if n == "conv_w" else g

    def __getitem__(self, name):
        return self.full[name]


class _Grads:
    def __init__(self, distributed):
        self.distributed = distributed
        self.local = {}
        self.pending = {}

    def add(self, name, g):
        self.local[name] = g

    def send(self, *names):
        if not self.distributed:
            return []
        rider = _Exchange([self.local[n] for n in names], [_SHARD_AXIS[n] for n in names])
        started = _exchange_start(rider, "send_" + "_".join(names))
        self.pending[names[0]] = (names, rider, started)
        return [started[3]]

    def wait(self, first_name, after):
        names, rider, started = self.pending.pop(first_name)
        return _exchange_wait(rider, started, after, "wait_" + "_".join(names))


def _ride(fn, *args, rider=None, **kw):
    if rider is None:
        return fn(*args, **kw), None
    return fn(*args, rider=rider, **kw)


def _local_step(x, mem, tgt, gains, weights, grads):
    names = ["w_in", "conv_w"]
    (x, tgt), got = _ride(_reorder, [x, tgt], "reorder_in", rider=weights.rider(names, late=True))
    weights.arrived(names, got)
    w_in, cw = weights["w_in"], weights["conv_w"]

    names = ["w_out", "w_kv"]
    (proj, h1), got = _ride(_norm_matmul, x, gains["g_mix"], w_in, name="proj", out_dtype=F32, tb=1024, bn=1536,
                            save_h=True, rider=weights.rider(names))
    weights.arrived(names, got)
    names = ["w_q", "w_o", "w_up"]
    (attn, *lses), got = _ride(_attention_fwd, proj, rider=weights.rider(names))
    weights.arrived(names, got)
    x1, merged = _mixer_fwd(x, attn, proj, cw, gains["g_attn_out"], gains["g_conv_out"], weights["w_out"])
    kv, mem_n = _norm_matmul(mem, gains["g_mem"], weights["w_kv"], name="mem_kv", out_dtype=BF16, tb=mem.shape[0],
                             bn=1024, save_h=True)
    x2, h2, qm, om = _xattn_fwd(x1, gains["g_xattn"], weights["w_q"], kv, weights["w_o"], tb=512)
    w_up = weights["w_up"]
    (a, h3), got = _ride(_norm_matmul, x2, gains["g_mlp"], w_up, name="mlp_up", out_dtype=BF16, tb=1024, bn=2048,
                         relu=True, save_h=True, rider=weights.rider(["w_down"], late=True))
    weights.arrived(["w_down"], got)
    w_down = weights["w_down"]
    dx3, dx3b, loss_blk, gg_final = _mlp_down_loss(a, w_down, x2, tgt, gains["g_final"], tb=512)

    dpre = _mlp_dpre(dx3b, w_down, a, tb=1024, bn=2048)
    grads.add("w_down", _matmul_tn(a, dx3b, name="grad_w_down", bm=512, bn=1024, square_a=True))
    sent = grads.send("w_down")
    grads.add("w_up", _matmul_tn(h3, dpre, name="grad_w_up", bm=1024, bn=1024, after=sent))
    sent = grads.send("w_up")
    dx2, dx2b, gg_mlp = _matmul_nt_normbwd(dpre, w_up, x2, gains["g_mlp"], dx3, name="mlp_dx", tb=512,
                                           also_bf16=True, after=sent)

    grads.add("w_o", _matmul_tn(om, dx2b, name="grad_w_o", bm=512, bn=512))
    dx1, dx1b, dqm, dk, dv, gg_xattn = _xattn_bwd(dx2, x1, gains["g_xattn"], qm, weights["w_q"], kv, weights["w_o"],
                                                  tb=512)
    grads.add("w_q", _matmul_tn(h2, dqm, name="grad_w_q", bm=1024, bn=512))
    dkv = jnp.concatenate([dk, dv], axis=1).astype(BF16)
    grads.add("w_kv", _matmul_tn(mem_n, dkv, name="grad_w_kv", bm=1024, bn=1024))
    _, gg_mem = _matmul_nt_normbwd(dkv, weights["w_kv"], mem, gains["g_mem"], None, name="mem_dx", tb=mem.shape[0])

    grads.add("w_out", _matmul_tn(merged, dx1b, name="grad_w_out", bm=1024, bn=512))
    sent = grads.send("w_o", "w_q", "w_kv", "w_out")
    dattn, dsum, dy, gg_attn, gg_conv = _mixer_bwd(dx1, attn, proj, cw, gains["g_attn_out"], gains["g_conv_out"],
                                                   weights["w_out"], _head_sum_matrix(), after=sent)
    dproj, gcw = _conv_bwd(dy, proj, cw)
    dproj = _attention_bwd(proj, dattn, dsum, lses, dproj)
    grads.add("w_in", _matmul_tn(h1, dproj, name="grad_w_in", bm=1024, bn=512))
    sent = grads.send("w_in")
    grad_x, gg_mix = _matmul_nt_normbwd(dproj, w_in, x, gains["g_mix"], dx1, name="mixer_dx", tb=512,
                                        to_natural=True, after=sent)

    def part(v):
        return jnp.pad(v, ((0, SMALL_PART - v.shape[0]), (0, 1024 - v.shape[1])))

    parts = [gg_mix, gg_xattn, gg_mem, gg_mlp, gg_final, jnp.concatenate([gg_attn, gg_conv], axis=1), gcw, loss_blk]
    grads.add("small", jnp.concatenate([part(v) for v in parts], axis=0))
    return grad_x


SMALL_PART = 8
_BIG = ("w_in", "w_out", "w_q", "w_kv", "w_o", "w_up", "w_down")
_GAIN_ROWS = ("g_mix", "g_xattn", "g_mem", "g_mlp", "g_final")


def _pack_small(vals, conv):
    rows = [vals[k].reshape(1, -1) for k in _GAIN_ROWS]
    rows.append(jnp.concatenate([vals["g_attn_out"].reshape(1, -1), vals["g_conv_out"].reshape(1, -1)], axis=1))
    flat = conv.reshape(1, -1)
    rows.append(jnp.pad(flat, ((0, 0), (0, 1024 - flat.shape[1]))))
    rows.append(jnp.zeros((1, 1024), F32))
    return jnp.concatenate(rows, axis=0)


def kernel(x, mem, g_mix, w_in, conv_w, g_attn_out, g_conv_out, w_out, g_xattn, g_mem, w_q_mem, w_kv_mem, w_o_mem, g_mlp, w_up, w_down, g_final, loss_target, m_g_mix, m_w_in, m_conv_w, m_g_attn_out, m_g_conv_out, m_w_out, m_g_xattn, m_g_mem, m_w_q_mem, m_w_kv_mem, m_w_o_mem, m_g_mlp, m_w_up, m_w_down, m_g_final, v_g_mix, v_w_in, v_conv_w, v_g_attn_out, v_g_conv_out, v_w_out, v_g_xattn, v_g_mem, v_w_q_mem, v_w_kv_mem, v_w_o_mem, v_g_mlp, v_w_up, v_w_down, v_g_final):
    d = x.shape[-1]
    me = 4 * lax.axis_index("x") + 2 * lax.axis_index("y") + lax.axis_index("c")
    w_shards = dict(w_in=w_in, w_out=w_out, w_q=w_q_mem, w_kv=w_kv_mem, w_o=w_o_mem, w_up=w_up, w_down=w_down)
    m_shards = dict(w_in=m_w_in, w_out=m_w_out, w_q=m_w_q_mem, w_kv=m_w_kv_mem, w_o=m_w_o_mem, w_up=m_w_up,
                    w_down=m_w_down)
    v_shards = dict(w_in=v_w_in, w_out=v_w_out, w_q=v_w_q_mem, w_kv=v_w_kv_mem, w_o=v_w_o_mem, w_up=v_w_up,
                    w_down=v_w_down)
    gains = dict(g_mix=g_mix, g_attn_out=g_attn_out, g_conv_out=g_conv_out, g_xattn=g_xattn, g_mem=g_mem,
                 g_mlp=g_mlp, g_final=g_final)
    gains2 = {k: v.reshape(1, -1) for k, v in gains.items()}

    shards = {k: w_shards[k].astype(BF16) for k in _BIG}
    shards["conv_w"] = conv_w
    grads = _Grads(distributed=True)
    grad_x = _local_step(x[0], mem[0], loss_target[0], gains2, _Weights({}, shards), grads)

    after = grads.send("small")
    outs = {}
    tiles = dict(w_in=256, w_out=128, w_q=128, w_kv=256, w_o=128, w_up=256, w_down=256)
    for group in (("w_down",), ("w_up",), ("w_o", "w_q", "w_kv", "w_out"), ("w_in",)):
        for k, received in zip(group, grads.wait(group[0], after)):
            outs[k] = _sum_adamw(received, w_shards[k], m_shards[k], v_shards[k], name=f"adamw_{k}", tr=tiles[k])
            after = [outs[k][0]]
    small_received, = grads.wait("small", after)

    ssum = _sum_small(small_received)
    row = lambda i: ssum[SMALL_PART * i]
    loss = ssum[SMALL_PART * 7, 0]
    g_small = {k: row(i) for i, k in enumerate(_GAIN_ROWS)}
    g_small["g_attn_out"] = row(5)[0:512]
    g_small["g_conv_out"] = row(5)[512:1024]
    taps = ssum[SMALL_PART * 6:SMALL_PART * 6 + 3, 0:512]
    g_conv = lax.dynamic_slice_in_dim(taps, me * 64, 64, axis=1)
    m_small = dict(g_mix=m_g_mix, g_attn_out=m_g_attn_out, g_conv_out=m_g_conv_out, g_xattn=m_g_xattn,
                   g_mem=m_g_mem, g_mlp=m_g_mlp, g_final=m_g_final)
    v_small = dict(g_mix=v_g_mix, g_attn_out=v_g_attn_out, g_conv_out=v_g_conv_out, g_xattn=v_g_xattn,
                   g_mem=v_g_mem, g_mlp=v_g_mlp, g_final=v_g_final)
    packed = [_pack_small(g_small, g_conv), _pack_small(gains, conv_w), _pack_small(m_small, m_conv_w),
              _pack_small(v_small, v_conv_w)]
    upd = _adamw_small(*packed)

    def unpack(p):
        res = {k: p[i] for i, k in enumerate(_GAIN_ROWS)}
        res["g_attn_out"] = p[5, 0:512]
        res["g_conv_out"] = p[5, 512:1024]
        res["conv_w"] = p[6, 0:192].reshape(3, 64)
        return res

    g_small["conv_w"] = g_conv
    small_out = [g_small] + [unpack(p) for p in upd]
    names = {"g_mix": "g_mix", "w_in": "w_in", "conv_w": "conv_w", "g_attn_out": "g_attn_out",
             "g_conv_out": "g_conv_out", "w_out": "w_out", "g_xattn": "g_xattn", "g_mem": "g_mem",
             "w_q_mem": "w_q", "w_kv_mem": "w_kv", "w_o_mem": "w_o", "g_mlp": "g_mlp", "w_up": "w_up",
             "w_down": "w_down", "g_final": "g_final"}
    result = [loss, grad_x[None]]
    for which in range(4):
        for key in names.values():
            result.append(outs[key][which] if key in outs else small_out[which][key])
    return tuple(result)
```

```python
import math

import jax
import jax.numpy as jnp
from jax import lax
from jax.experimental import pallas as pl
from jax.experimental.pallas import tpu as pltpu

F32 = jnp.float32
BF16 = jnp.bfloat16
NORM_EPS = 1e-6
NEG_INF = -1e30
N_DEV = 8
BLK = 128
HEAD_DIM = 64
N_MEM_HEADS = 4
ADAM_LR = 0.001
ADAM_B1 = 0.9
ADAM_B2 = 0.999
ADAM_EPS = 1e-08
ADAM_WD = 0.01
ADAM_STEP = 10
MESH = pl.DeviceIdType.MESH
ANY = pl.BlockSpec(memory_space=pl.ANY)


def _dot(a, b):
    return jnp.dot(a, b, preferred_element_type=F32)


def _dot_nt(a, b):
    return lax.dot_general(a, b, (((1,), (1,)), ((), ())), preferred_element_type=F32)


def _dot_tn(a, b):
    return lax.dot_general(a, b, (((0,), (0,)), ((), ())), preferred_element_type=F32)


def _params(semantics, vmem_mb):
    return pltpu.CompilerParams(dimension_semantics=semantics, vmem_limit_bytes=vmem_mb << 20)


def _rms_fwd(x, g):
    r = lax.rsqrt(jnp.mean(x * x, axis=-1, keepdims=True) + NORM_EPS)
    xh = x * r
    return xh * g, xh, r


def _rms_bwd(dy, xh, r, g):
    gy = dy * g
    return r * (gy - xh * jnp.mean(xh * gy, axis=-1, keepdims=True))


def _position():
    x, y, c = lax.axis_index("x"), lax.axis_index("y"), lax.axis_index("c")
    return x, y, c


def _block_of(ref, j, axis, shard_shape):
    r, c = shard_shape
    if axis is None:
        return ref.at[j]
    if axis == 0:
        return ref.at[pl.ds(j * r, r), :]
    return ref.at[:, pl.ds(j * c, c)]


class _Gather:
    has_mid = True
    alias_pairs = ()

    def __init__(self, shards, axes, late=False):
        self.arrays = list(shards)
        self.axes = list(axes)
        self.late = late
        self.n = len(self.arrays)

    def out_shape(self):
        res = []
        for s, axis in zip(self.arrays, self.axes):
            r, c = s.shape
            shape = (N_DEV, r, c) if axis is None else (N_DEV * r, c) if axis == 0 else (r, N_DEV * c)
            res.append(jax.ShapeDtypeStruct(shape, s.dtype))
        return res

    def scratch(self):
        return [pltpu.SemaphoreType.DMA((self.n, 7)), pltpu.SemaphoreType.DMA((self.n, 7)),
                pltpu.SemaphoreType.DMA((self.n,))]

    def _ctx(self, ins, outs, sems):
        send_sems, recv_sems, local_sems = sems
        x, y, c = _position()
        me, sibling = (x, y, c), (x, y, 1 - c)
        chips = [(1 - x, y), (x, 1 - y), (1 - x, 1 - y)]

        def lin(px, py, pc):
            return 4 * px + 2 * py + pc

        def place(a, block):
            return _block_of(outs[a], lin(*block), self.axes[a], self.arrays[a].shape)

        def copy(a, k, block, to, src=None):
            dst = place(a, block)
            return pltpu.make_async_remote_copy(
                src_ref=dst if src is None else src, dst_ref=dst,
                send_sem=send_sems.at[a, k], recv_sem=recv_sems.at[a, k],
                device_id=to, device_id_type=MESH)

        def mine():
            return [pltpu.make_async_copy(ins[a], place(a, me), local_sems.at[a]) for a in range(self.n)]

        def first():
            res = []
            for a in range(self.n):
                res.append(copy(a, 0, me, sibling, src=ins[a]))
                res += [copy(a, 1 + j, me, (*chip, c), src=ins[a]) for j, chip in enumerate(chips)]
            return res

        return c, me, sibling, chips, copy, mine, first

    def start(self, ins, outs, sems):
        _, _, _, _, _, mine, first = self._ctx(ins, outs, sems)
        for cp in mine() + first():
            cp.start()

    def mid(self, ins, outs, sems):
        c, me, sibling, chips, copy, _, _ = self._ctx(ins, outs, sems)
        for j, chip in enumerate(chips):
            for a in range(self.n):
                copy(a, 1 + j, (*chip, c), me).wait_recv()
                copy(a, 4 + j, (*chip, c), sibling).start()

    def finish(self, ins, outs, sems):
        c, me, sibling, chips, copy, mine, first = self._ctx(ins, outs, sems)
        for a in range(self.n):
            copy(a, 0, sibling, me).wait_recv()
            for j, chip in enumerate(chips):
                copy(a, 4 + j, (*chip, 1 - c), me).wait_recv()
        for cp in first():
            cp.wait_send()
        for j, chip in enumerate(chips):
            for a in range(self.n):
                copy(a, 4 + j, (*chip, c), sibling).wait_send()
        for cp in mine():
            cp.wait()


class _Exchange:
    def __init__(self, parts, axes):
        self.n = len(parts)
        self.axes = list(axes)
        self.arrays = list(parts)

    def _piece(self, a):
        r, c = self.arrays[a].shape
        axis = self.axes[a]
        return (r, c) if axis is None else (r // N_DEV, c) if axis == 0 else (r, c // N_DEV)

    def out_shape(self):
        return [jax.ShapeDtypeStruct((N_DEV,) + self._piece(a), self.arrays[a].dtype) for a in range(self.n)]

    def semaphores(self):
        return [pltpu.SemaphoreType.DMA((7 * self.n,)), pltpu.SemaphoreType.DMA((7 * self.n,)),
                pltpu.SemaphoreType.DMA((self.n,))]

    def _ctx(self, ins, outs, sems):
        send_sems, recv_sems, local_sems = sems
        x, y, c = _position()
        me = 4 * x + 2 * y + c

        def src(a, j):
            return ins[a] if self.axes[a] is None else _block_of(ins[a], j, self.axes[a], self._piece(a))

        def dst(a, j):
            return outs[a].at[j]

        def local():
            return [pltpu.make_async_copy(src(a, me), dst(a, me), local_sems.at[a]) for a in range(self.n)]

        def remote(inbound):
            res = []
            for a in range(self.n):
                for k in range(1, N_DEV):
                    peer = (1 - x if k & 4 else x, 1 - y if k & 2 else y, 1 - c if k & 1 else c)
                    plin = 4 * peer[0] + 2 * peer[1] + peer[2]
                    res.append(pltpu.make_async_remote_copy(
                        src_ref=src(a, plin), dst_ref=dst(a, plin if inbound else me),
                        send_sem=send_sems.at[7 * a + k - 1], recv_sem=recv_sems.at[7 * a + k - 1],
                        device_id=peer, device_id_type=MESH))
            return res

        return local, remote

    def start(self, ins, outs, sems):
        local, remote = self._ctx(ins, outs, sems)
        for cp in local() + remote(False):
            cp.start()

    def finish(self, ins, outs, sems):
        local, remote = self._ctx(ins, outs, sems)
        for cp in remote(True):
            cp.wait_recv()
        for cp in remote(False):
            cp.wait_send()
        for cp in local():
            cp.wait()


def _exchange_start(rider, name):
    n = rider.n
    parts = rider.arrays
    lands = [lax.empty(s.shape, s.dtype) for s in rider.out_shape()]
    hbm = pl.BlockSpec(memory_space=pltpu.HBM)
    sem = pl.BlockSpec(memory_space=pltpu.SEMAPHORE)

    def body(*refs):
        ins, sems = refs[:n], refs[2 * n:2 * n + 3]
        outs, token = refs[2 * n + 3 + n:2 * n + 3 + 2 * n], refs[-1]
        rider.start(ins, outs, sems)
        token[...] = jnp.zeros_like(token)

    res = pl.pallas_call(
        body, name=name,
        out_shape=rider.semaphores() + [pltpu.HBM(p.shape, p.dtype) for p in parts]
                  + [pltpu.HBM(z.shape, z.dtype) for z in lands] + [jax.ShapeDtypeStruct((8, 128), F32)],
        in_specs=[hbm] * (2 * n), out_specs=[sem] * 3 + [hbm] * (2 * n) + [pl.BlockSpec(memory_space=pltpu.VMEM)],
        input_output_aliases={i: 3 + i for i in range(2 * n)},
        compiler_params=pltpu.CompilerParams(has_side_effects=pltpu.SideEffectType.DATAFLOW_SIDE_EFFECTING),
    )(*[pltpu.with_memory_space_constraint(a, pltpu.HBM) for a in parts + lands])
    return res[:3], res[3:3 + n], res[3 + n:3 + 2 * n], res[-1]


def _exchange_wait(rider, started, after, name):
    n = rider.n
    sems, parts, lands, _ = started
    hbm = pl.BlockSpec(memory_space=pltpu.HBM)
    sem = pl.BlockSpec(memory_space=pltpu.SEMAPHORE)

    def body(*refs):
        rider.finish(refs[:n], refs[n:2 * n], refs[2 * n:2 * n + 3])

    res = pl.pallas_call(
        body, name=name, out_shape=[pltpu.HBM(a.shape, a.dtype) for a in list(parts) + list(lands)],
        in_specs=[hbm] * (2 * n) + [sem] * 3 + [ANY] * len(after), out_specs=[hbm] * (2 * n),
        input_output_aliases={i: i for i in range(2 * n)},
        compiler_params=pltpu.CompilerParams(has_side_effects=pltpu.SideEffectType.DATAFLOW_SIDE_EFFECTING),
    )(*parts, *lands, *sems, *after)
    return list(res[n:])


def _pcall(body, *, name, grid, in_specs, out_specs, out_shape, scratch_shapes=(), semantics, vmem_mb, rider=None,
           aliases=None, after=()):
    in_specs, out_specs, out_shape = list(in_specs), list(out_specs), list(out_shape)
    scratch_shapes = list(scratch_shapes)
    aliases = dict(aliases or {})
    if rider is None:
        n_in, after = len(in_specs), list(after)

        def plain(*refs):
            body(*refs[:n_in], *refs[n_in + len(after):])

        call = pl.pallas_call(plain if after else body, name=name, grid=grid, in_specs=in_specs + [ANY] * len(after),
                              out_specs=out_specs, out_shape=out_shape, scratch_shapes=scratch_shapes,
                              input_output_aliases=aliases, compiler_params=_params(semantics, vmem_mb))
        return lambda *args: (list(call(*args, *after)), None)
    n_in, n_out, n_scr = len(in_specs), len(out_specs), len(scratch_shapes)
    r_in, r_shapes = len(rider.arrays), rider.out_shape()
    r_out = len(r_shapes)
    aliases.update({n_in + i: n_out + o for i, o in rider.alias_pairs})
    total = math.prod(grid)
    mid_step = total - 1 if rider.has_mid and rider.late else (3 * total) // 4

    def wrapped(*refs):
        bounds = [0, n_in, r_in, n_out, r_out, n_scr]
        for i in range(1, len(bounds)):
            bounds[i] += bounds[i - 1]
        a, ra, o, ro, s = (refs[bounds[i]:bounds[i + 1]] for i in range(5))
        rs = refs[bounds[5]:]
        step = pl.program_id(0)
        for k in range(1, len(grid)):
            step = step * grid[k] + pl.program_id(k)
        pl.when(step == 0)(lambda: rider.start(ra, ro, rs))
        body(*a, *o, *s)
        if rider.has_mid:
            pl.when(step == mid_step)(lambda: rider.mid(ra, ro, rs))
        pl.when(step == total - 1)(lambda: rider.finish(ra, ro, rs))

    call = pl.pallas_call(
        wrapped, name=name, grid=grid, in_specs=in_specs + [ANY] * r_in, out_specs=out_specs + [ANY] * r_out,
        out_shape=out_shape + r_shapes, scratch_shapes=scratch_shapes + rider.scratch(),
        input_output_aliases=aliases, compiler_params=_params(("arbitrary",) * len(grid), vmem_mb))

    def run(*args):
        res = call(*args, *rider.arrays)
        return list(res[:n_out]), list(res[n_out:])

    return run


def _norm_matmul(x, g, w, *, name, out_dtype, tb, bn, relu=False, save_h=False, rider=None):
    t, d = x.shape
    n = w.shape[1]

    def body(x_ref, g_ref, w_ref, o_ref, *rest):
        h_scr = rest[-1]

        @pl.when(pl.program_id(1) == 0)
        def _():
            h = _rms_fwd(x_ref[...], g_ref[...])[0].astype(BF16)
            h_scr[...] = h
            if save_h:
                rest[0][...] = h

        acc = _dot(h_scr[...], w_ref[...])
        if relu:
            acc = jnp.maximum(acc, 0.0)
        o_ref[...] = acc.astype(out_dtype)

    out_shape = [jax.ShapeDtypeStruct((t, n), out_dtype)]
    out_specs = [pl.BlockSpec((tb, bn), lambda i, j: (i, j))]
    if save_h:
        out_shape.append(jax.ShapeDtypeStruct((t, d), BF16))
        out_specs.append(pl.BlockSpec((tb, d), lambda i, j: (i, 0)))
    res, extra = _pcall(
        body, name=name, grid=(t // tb, n // bn),
        in_specs=[pl.BlockSpec((tb, d), lambda i, j: (i, 0)),
                  pl.BlockSpec((1, d), lambda i, j: (0, 0)),
                  pl.BlockSpec((d, bn), lambda i, j: (0, j))],
        out_specs=out_specs, out_shape=out_shape,
        scratch_shapes=[pltpu.VMEM((tb, d), BF16)],
        semantics=("parallel", "arbitrary"), vmem_mb=48, rider=rider,
    )(x, g, w)
    res = res if save_h else res[0]
    return res if rider is None else (res, extra)


def _proj(x, g, w, *, tb, rider=None):
    t, d = x.shape
    half = w.shape[1] // 2

    def body(x_ref, g_ref, w_ref, qkv_ref, gates_ref, h_ref, h_scr):
        j = pl.program_id(1)

        @pl.when(j == 0)
        def _():
            h = _rms_fwd(x_ref[...], g_ref[...])[0].astype(BF16)
            h_scr[...] = h
            h_ref[...] = h

        acc = _dot(h_scr[...], w_ref[...])

        @pl.when(j == 0)
        def _():
            qkv_ref[...] = acc

        @pl.when(j == 1)
        def _():
            gates_ref[...] = acc.astype(BF16)

    tok = lambda c: pl.BlockSpec((tb, c), lambda i, j: (i, 0))
    res, extra = _pcall(
        body, name="proj", grid=(t // tb, 2),
        in_specs=[tok(d), pl.BlockSpec((1, d), lambda i, j: (0, 0)), pl.BlockSpec((d, half), lambda i, j: (0, j))],
        out_specs=[tok(half), tok(half), tok(d)],
        out_shape=[jax.ShapeDtypeStruct((t, half), F32), jax.ShapeDtypeStruct((t, half), BF16),
                   jax.ShapeDtypeStruct((t, d), BF16)],
        scratch_shapes=[pltpu.VMEM((tb, d), BF16)],
        semantics=("parallel", "arbitrary"), vmem_mb=48, rider=rider,
    )(x, g, w)
    return res if rider is None else (res, extra)


def _matmul_nt_normbwd(dy, w, x, g, dres, *, name, tb, also_bf16=False, to_natural=False, after=()):
    t, d = x.shape
    stacked = dy.ndim == 3
    has_res = dres is not None
    n_i = SEG // TI
    if to_natural:
        tb = N_RES * TI

    def body(dy_ref, w_ref, x_ref, g_ref, *rest):
        rest = list(rest)
        dres_ref = rest.pop(0) if has_res else None
        dx_ref = rest.pop(0)
        dxb_ref = rest.pop(0) if also_bf16 else None
        gg_ref = rest.pop(0)
        i = pl.program_id(0)

        def rows(ref, *lead):
            v = ref[lead] if lead else ref[...]
            return v[0].reshape(tb, v.shape[-1]) if to_natural else v

        if stacked:
            kb = dy_ref.shape[-1]
            dh = _dot_nt(rows(dy_ref, 0), w_ref[:, 0:kb])
            for s in range(1, dy_ref.shape[0]):
                dh = dh + _dot_nt(rows(dy_ref, s), w_ref[:, s * kb:(s + 1) * kb])
        else:
            dh = _dot_nt(rows(dy_ref), w_ref[...])
        g_v = g_ref[...]
        _, xh, r = _rms_fwd(rows(x_ref), g_v)
        dx = _rms_bwd(dh, xh, r, g_v)
        if has_res:
            dx = dx + rows(dres_ref)
        if to_natural:
            scr = rest.pop(0)
            for cb in range(d // BLK):
                cols = slice(cb * BLK, (cb + 1) * BLK)
                slab = scr.at[cb]
                for res in range(N_RES):
                    slab[pl.ds(res, TI, stride=N_RES), :] = dx[res * TI:(res + 1) * TI, cols]
                dx_ref[:, cols] = slab[...]
        else:
            dx_ref[...] = dx
        if also_bf16:
            dxb_ref[...] = dx.astype(BF16)
        part = jnp.sum(dh * xh, axis=0, keepdims=True)

        @pl.when(i == 0)
        def _():
            gg_ref[...] = part

        @pl.when(i != 0)
        def _():
            gg_ref[...] += part

    tok = pl.BlockSpec((tb, d), lambda i: (i, 0))
    row = pl.BlockSpec((1, d), lambda i: (0, 0))
    if to_natural:
        act = pl.BlockSpec((1, N_RES, TI, d), lambda i: (i // n_i, 0, i % n_i, 0))
        dy_spec = pl.BlockSpec((dy.shape[0], 1, N_RES, TI, dy.shape[2]), lambda i: (0, i // n_i, 0, i % n_i, 0))
        dy, x = dy.reshape(dy.shape[0], t // HALF, N_RES, SEG, dy.shape[2]), _x4(x)
        dres = _x4(dres) if has_res else None
    elif stacked:
        act, dy_spec = tok, pl.BlockSpec((dy.shape[0], tb, dy.shape[2]), lambda i: (0, i, 0))
    else:
        act, dy_spec = tok, pl.BlockSpec((tb, dy.shape[1]), lambda i: (i, 0))
    in_specs = [dy_spec, pl.BlockSpec(w.shape, lambda i: (0, 0)), act, row]
    args = [dy, w, x, g]
    if has_res:
        in_specs.append(act)
        args.append(dres)
    out_specs = [tok] + ([tok] if also_bf16 else []) + [row]
    out_shape = ([jax.ShapeDtypeStruct((t, d), F32)] + ([jax.ShapeDtypeStruct((t, d), BF16)] if also_bf16 else [])
                 + [jax.ShapeDtypeStruct((1, d), F32)])
    res, _ = _pcall(
        body, name=name, grid=(t // tb,), in_specs=in_specs, out_specs=out_specs, out_shape=out_shape,
        scratch_shapes=[pltpu.VMEM((d // BLK, tb, BLK), F32)] if to_natural else [],
        semantics=("arbitrary",), vmem_mb=56, after=after,
    )(*args)
    return res


def _matmul_tn(a, b, *, name, bm, bn, square_a=False, after=()):
    t, m = a.shape
    stacked = b.ndim == 3
    n = b.shape[0] * bn if stacked else b.shape[1]

    def body(a_ref, b_ref, o_ref):
        av = a_ref[...]
        if square_a:
            av = av.astype(F32)
            av = (av * av).astype(BF16)
        o_ref[...] = _dot_tn(av, b_ref[...]).astype(BF16)

    res, _ = _pcall(
        body, name=name, grid=(m // bm, n // bn),
        in_specs=[pl.BlockSpec((t, bm), lambda i, j: (0, i)),
                  pl.BlockSpec((None, t, bn), lambda i, j: (j, 0, 0)) if stacked
                  else pl.BlockSpec((t, bn), lambda i, j: (0, j))],
        out_specs=[pl.BlockSpec((bm, bn), lambda i, j: (i, j))], out_shape=[jax.ShapeDtypeStruct((m, n), BF16)],
        semantics=("parallel", "parallel"), vmem_mb=56, after=after,
    )(a, b)
    return res[0]


N_RES = 16
SEG = 128
HALF = N_RES * SEG
TI = 32
HALO = 16


def _x4(a):
    return a.reshape(a.shape[0] // HALF, N_RES, SEG, a.shape[1])


def _reorder(arrays, name, rider=None):
    t, c = arrays[0].shape
    n = len(arrays)
    n_i = SEG // TI

    def body(*refs):
        scr = refs[-1]
        for i_ref, o_ref in zip(refs[:n], refs[n:2 * n]):
            for cb in range(c // BLK):
                cols = slice(cb * BLK, (cb + 1) * BLK)
                slab = scr.at[cb]
                slab[...] = i_ref[:, cols]
                for r in range(N_RES):
                    o_ref[0, r, :, cols] = slab[pl.ds(r, TI, stride=N_RES), :]

    res, extra = _pcall(
        body, name=name, grid=(t // (TI * N_RES),),
        in_specs=[pl.BlockSpec((TI * N_RES, c), lambda s: (s, 0))] * n,
        out_specs=[pl.BlockSpec((1, N_RES, TI, c), lambda s: (s // n_i, 0, s % n_i, 0))] * n,
        out_shape=[jax.ShapeDtypeStruct((t // HALF, N_RES, SEG, c), F32)] * n,
        scratch_shapes=[pltpu.VMEM((c // BLK, TI * N_RES, BLK), F32)],
        semantics=("parallel",), vmem_mb=32, rider=rider,
    )(*arrays)
    res = [r.reshape(t, c) for r in res]
    return res if rider is None else (res, extra)


_PATTERNS = ((1, 16, 8, SEG), (4, 4, 32, 4 * SEG), (16, 1, SEG, 0))
_FIRST = {1: 1, 4: 4, 16: 16}


def _group_rows(d, g):
    a = g >> 4
    if d == 16:
        base = a * HALF + (g & 15) * SEG
        prev = base - HALF
    elif d == 4:
        c = (g >> 2) & 3
        base = a * HALF + (g & 3) * SEG + c * 32
        prev = jnp.where(c > 0, base - 32, base - HALF + 96)
    else:
        c = g & 15
        base = a * HALF + c * 8
        prev = jnp.where(c > 0, base - 8, base - HALF + 120)
    return base, prev


def _load_rows(ref, base, n, rows, stride):
    parts = [ref[pl.ds(pl.multiple_of(base + j * stride, 8), rows), :] for j in range(n)]
    return parts[0] if n == 1 else jnp.concatenate(parts, axis=0)


def _store_rows(ref, base, val, n, rows, stride, add=False):
    for j in range(n):
        sl = pl.ds(pl.multiple_of(base + j * stride, 8), rows)
        piece = val[j * rows:(j + 1) * rows, :]
        if add:
            ref[sl, :] += piece
        else:
            ref[sl, :] = piece


def _band_bias(n, rows):
    shift = rows.bit_length() - 1
    lq = lax.broadcasted_iota(jnp.int32, (BLK, BLK), 0)
    lk = lax.broadcasted_iota(jnp.int32, (BLK, BLK), 1)
    iq = (lq & (rows - 1)) * n + (lq >> shift)
    ik = (lk & (rows - 1)) * n + (lk >> shift)
    zero = jnp.zeros((BLK, BLK), F32)
    return jnp.where(ik >= iq, zero, NEG_INF), jnp.where(ik <= iq, zero, NEG_INF)


def _set_bias(bias_scr, n, rows):
    prev_b, cur_b = _band_bias(n, rows)
    for half in range(2):
        bias_scr[half * BLK:(half + 1) * BLK, 0:BLK] = prev_b
        bias_scr[half * BLK:(half + 1) * BLK, BLK:2 * BLK] = cur_b


SCALE = 1.0 / math.sqrt(HEAD_DIM)


def _head_consts(value=1.0):
    lane_lo = lax.broadcasted_iota(jnp.int32, (BLK, BLK), 1) < HEAD_DIM
    return lane_lo, [jnp.where(lane_lo, value, 0.0).astype(BF16), jnp.where(lane_lo, 0.0, value).astype(BF16)]


def _stack_heads(v, head_mask):
    return jnp.concatenate([v * head_mask[0], v * head_mask[1]], axis=0)


def _unstack_heads(v2, lane_lo):
    return jnp.where(lane_lo, v2[:BLK], v2[BLK:])


def _rows_per_head(v, lane_lo):
    rolled = pltpu.roll(v, HEAD_DIM, axis=1)
    return jnp.concatenate([jnp.where(lane_lo, v, rolled), jnp.where(lane_lo, rolled, v)], axis=0)


WIDTH = 4


def _loop(lo, hi, fn, width=None):
    if width is None:
        def body(g, carry):
            fn(g)
            return carry

        if hi > lo:
            lax.fori_loop(lo, hi, body, 0)
        return
    while hi > lo:
        trips = (hi - lo) // width
        if trips:
            def body(i, carry, lo=lo, width=width):
                fn([lo + width * i + j for j in range(width)])
                return carry

            lax.fori_loop(0, trips, body, 0)
            lo += trips * width
        width = max(1, width // 2)


def _mix_weights(l1, l2, l3):
    mx = jnp.maximum(jnp.maximum(l1, l2), l3)
    e1, e2, e3 = jnp.exp(l1 - mx), jnp.exp(l2 - mx), jnp.exp(l3 - mx)
    inv = 1.0 / (e1 + e2 + e3)
    return e1 * inv, e2 * inv, e3 * inv


def _attention_fwd(qkv, rider=None):
    t = qkv.shape[0]
    groups = 16 * (t // HALF)

    def body(q_ref, k_ref, v_ref, attn_ref, l1_ref, l2_ref, l3_ref, o_scr, bias_scr):
        lane_lo, q_mask = _head_consts(SCALE)
        l_refs = (l1_ref, l2_ref, l3_ref)
        for p, (d, n, rows, stride) in enumerate(_PATTERNS):
            _set_bias(bias_scr, n, rows)
            o_p, l_p = o_scr.at[p], l_refs[p]

            def block(gs, has_prev):
                at = [_group_rows(d, g) for g in gs]

                def load(ref, b):
                    return _load_rows(ref, b, n, rows, stride).astype(BF16)

                q2 = [_stack_heads(load(q_ref, b), q_mask) for b, _ in at]
                k2 = [load(k_ref, b) for b, _ in at]
                v2 = [load(v_ref, b) for b, _ in at]
                if has_prev:
                    k2 = [jnp.concatenate([load(k_ref, pv), k], axis=0) for (_, pv), k in zip(at, k2)]
                    v2 = [jnp.concatenate([load(v_ref, pv), v], axis=0) for (_, pv), v in zip(at, v2)]
                s = [_dot_nt(q, k) for q, k in zip(q2, k2)]
                s = [x + (bias_scr[...] if has_prev else bias_scr[:, BLK:2 * BLK]) for x in s]
                mx = [jnp.max(x, axis=1, keepdims=True) for x in s]
                e = [jnp.exp(x - m) for x, m in zip(s, mx)]
                den = [jnp.sum(x, axis=1, keepdims=True) for x in e]
                o2 = [_dot(x.astype(BF16), v) * (1.0 / dn) for x, v, dn in zip(e, v2, den)]
                lse2 = [jnp.broadcast_to(m + jnp.log(dn), (2 * BLK, BLK)) for m, dn in zip(mx, den)]
                for (b, _), o, l in zip(at, o2, lse2):
                    _store_rows(o_p, b, _unstack_heads(o, lane_lo), n, rows, stride)
                    _store_rows(l_p, b, _unstack_heads(l, lane_lo), n, rows, stride)

            _loop(0, _FIRST[d], lambda gs: block(gs, False), width=2 * WIDTH)
            _loop(_FIRST[d], groups, lambda gs: block(gs, True), width=2 * WIDTH)

        def mix(i):
            sl = pl.ds(pl.multiple_of(i * 256, 256), 256)
            w = _mix_weights(l1_ref[sl, :], l2_ref[sl, :], l3_ref[sl, :])
            attn_ref[sl, :] = w[0] * o_scr[0, sl, :] + w[1] * o_scr[1, sl, :] + w[2] * o_scr[2, sl, :]

        _loop(0, t // 256, mix)

    def col(c0):
        return pl.BlockSpec((t, BLK), lambda hp: (0, c0 + hp))

    res, extra = _pcall(
        body, name="attention_fwd", grid=(4,), in_specs=[col(0), col(4), col(8)], out_specs=[col(0)] * 4,
        out_shape=[jax.ShapeDtypeStruct((t, 512), F32)] * 4,
        scratch_shapes=[pltpu.VMEM((3, t, BLK), F32), pltpu.VMEM((2 * BLK, 2 * BLK), F32)],
        semantics=("parallel",), vmem_mb=48, rider=rider,
    )(qkv, qkv, qkv)
    return res if rider is None else (res, extra)


def _attention_bwd(qkv, dattn, dsum, lses, dproj):
    t = qkv.shape[0]
    groups = 16 * (t // HALF)

    def body(q_ref, k_ref, v_ref, da_ref, ds_ref, l1_ref, l2_ref, l3_ref, kept_ref, out_ref, acc, bias_scr):
        del kept_ref
        lane_lo, head_mask = _head_consts()
        q_mask = _head_consts(SCALE)[1]
        l_refs = (l1_ref, l2_ref, l3_ref)

        def clear(i):
            sl = pl.ds(pl.multiple_of(i * 512, 512), 512)
            for s in range(3):
                acc[s, sl, :] = jnp.zeros((512, BLK), F32)

        _loop(0, t // 512, clear)
        dq_acc, dk_acc, dv_acc = acc.at[0], acc.at[1], acc.at[2]
        for p, (d, n, rows, stride) in enumerate(_PATTERNS):
            _set_bias(bias_scr, n, rows)

            def block(gs, has_prev):
                at = [_group_rows(d, g) for g in gs]

                def load(ref, b):
                    return _load_rows(ref, b, n, rows, stride)

                def put(ref, b, val):
                    _store_rows(ref, b, val, n, rows, stride, add=True)

                def wide(x):
                    return jnp.concatenate([x, x], axis=1) if has_prev else x

                lse = [[load(ref, b) for ref in l_refs] for b, _ in at]
                w = [_mix_weights(*ls)[p] for ls in lse]
                do2 = [_stack_heads((wg * load(da_ref, b)).astype(BF16), head_mask) for wg, (b, _) in zip(w, at)]
                dl2 = [wide(_rows_per_head(wg * load(ds_ref, b), lane_lo)) for wg, (b, _) in zip(w, at)]
                lse2 = [wide(_rows_per_head(ls[p], lane_lo)) for ls in lse]
                q2 = [_stack_heads(load(q_ref, b).astype(BF16), q_mask) for b, _ in at]
                k2 = [load(k_ref, b).astype(BF16) for b, _ in at]
                v2 = [load(v_ref, b).astype(BF16) for b, _ in at]
                if has_prev:
                    k2 = [jnp.concatenate([load(k_ref, pv).astype(BF16), k], axis=0) for (_, pv), k in zip(at, k2)]
                    v2 = [jnp.concatenate([load(v_ref, pv).astype(BF16), v], axis=0) for (_, pv), v in zip(at, v2)]
                s = [_dot_nt(q, k) for q, k in zip(q2, k2)]
                dp = [_dot_nt(do, v) for do, v in zip(do2, v2)]
                pr = [jnp.exp(x + (bias_scr[...] if has_prev else bias_scr[:, BLK:2 * BLK]) - l)
                      for x, l in zip(s, lse2)]
                ds = [(pg * (x - dl)).astype(BF16) for pg, x, dl in zip(pr, dp, dl2)]
                dq2 = [_dot(x, k) * SCALE for x, k in zip(ds, k2)]
                dk2 = [_dot_tn(x, q) for x, q in zip(ds, q2)]
                dv2 = [_dot_tn(pg.astype(BF16), do) for pg, do in zip(pr, do2)]
                for (b, pv), dq, dk, dv in zip(at, dq2, dk2, dv2):
                    put(dq_acc, b, _unstack_heads(dq, lane_lo))
                    if has_prev:
                        put(dk_acc, pv, dk[:BLK])
                        put(dv_acc, pv, dv[:BLK])
                        put(dk_acc, b, dk[BLK:])
                        put(dv_acc, b, dv[BLK:])
                    else:
                        put(dk_acc, b, dk)
                        put(dv_acc, b, dv)

            _loop(0, _FIRST[d], lambda gs: block(gs, False), width=WIDTH)
            _loop(_FIRST[d], groups, lambda gs: block(gs, True), width=WIDTH)

        def emit(i):
            sl = pl.ds(pl.multiple_of(i * 512, 512), 512)
            for s in range(3):
                out_ref[s, sl, :] = acc[s, sl, :].astype(BF16)

        _loop(0, t // 512, emit)

    def col(c0):
        return pl.BlockSpec((t, BLK), lambda hp: (0, c0 + hp))

    res, _ = _pcall(
        body, name="attention_bwd", grid=(4,),
        in_specs=[col(0), col(4), col(8)] + [col(0)] * 5 + [ANY],
        out_specs=[pl.BlockSpec((3, t, BLK), lambda hp: (0, 0, hp))],
        out_shape=[jax.ShapeDtypeStruct(dproj.shape, BF16)],
        scratch_shapes=[pltpu.VMEM((3, t, BLK), F32), pltpu.VMEM((2 * BLK, 2 * BLK), F32)],
        semantics=("parallel",), vmem_mb=56, aliases={8: 0},
    )(qkv, qkv, qkv, dattn, dsum, *lses, dproj)
    return res[0]


def _order_specs(t):
    n_i = SEG // TI
    nblk = (t // HALF) * n_i
    per = TI // HALO

    def main(c, col=0):
        return pl.BlockSpec((1, N_RES, TI, c), lambda s: (s // n_i, 0, s % n_i, col))

    def before(c, col=0):
        return pl.BlockSpec((1, 2, HALO, c), lambda s: (jnp.maximum(s - 1, 0) // n_i, N_RES // 2 - 1,
                                                        (jnp.maximum(s - 1, 0) % n_i) * per + per - 1, col))

    def after(c, col=0):
        return pl.BlockSpec((1, 2, HALO, c), lambda s: (jnp.minimum(s + 1, nblk - 1) // n_i, 0,
                                                        (jnp.minimum(s + 1, nblk - 1) % n_i) * per, col))

    return nblk, main, before, after


def _shift_in(v, row_in, up):
    rows = v.shape[0]
    idx = lax.broadcasted_iota(jnp.int32, v.shape, 0)
    fill = jnp.broadcast_to(row_in, v.shape)
    if up:
        return jnp.where(idx == rows - 1, fill, pltpu.roll(v, rows - 1, axis=0))
    return jnp.where(idx == 0, fill, pltpu.roll(v, 1, axis=0))


def _taps_behind(u, before):
    s15 = _shift_in(u[N_RES - 1], before[1, HALO - 1:HALO, :], up=False)
    s14 = _shift_in(u[N_RES - 2], before[0, HALO - 1:HALO, :], up=False)
    m1 = jnp.concatenate([s15[None], u[:N_RES - 1]], axis=0)
    m2 = jnp.concatenate([s14[None], s15[None], u[:N_RES - 2]], axis=0)
    return m1, m2


def _taps_ahead(u, after):
    t0 = _shift_in(u[0], after[0, 0:1, :], up=True)
    t1 = _shift_in(u[1], after[1, 0:1, :], up=True)
    p1 = jnp.concatenate([u[1:], t0[None]], axis=0)
    p2 = jnp.concatenate([u[2:], t0[None], t1[None]], axis=0)
    return p1, p2


def _conv_fwd(gates, before, first, cw):
    gates, before = gates.astype(F32), before.astype(F32)
    bg, cg, xc = gates[..., 0:512], gates[..., 512:1024], gates[..., 1024:1536]
    u = cg * xc
    ub = before[..., 512:1024] * before[..., 1024:1536]
    ub = jnp.where(first, jnp.zeros_like(ub), ub)
    m1, m2 = _taps_behind(u, ub)
    conv = m2 * cw[0:1, :] + m1 * cw[1:2, :] + u * cw[2:3, :]
    return bg, u, m1, m2, conv


def _sum_tokens(v):
    return jnp.sum(jnp.sum(v, axis=0), axis=0, keepdims=True)


def _mixer_fwd(x, attn, gates, cw, g_a, g_c, w_out):
    t, d = x.shape
    nblk, main, before, _ = _order_specs(t)
    rows = N_RES * TI

    def body(x_ref, at_ref, gt_ref, gb_ref, cw_ref, ga_ref, gc_ref, wa_ref, wb_ref, x1_ref, mg_ref):
        an = _rms_fwd(at_ref[0], ga_ref[...])[0].astype(BF16)
        bg, _, _, _, conv = _conv_fwd(gt_ref[0], gb_ref[0], pl.program_id(0) == 0, cw_ref[...])
        cn = _rms_fwd(bg * conv, gc_ref[...])[0].astype(BF16)
        mg_ref[0, :, :, 0:512] = an
        mg_ref[0, :, :, 512:1024] = cn
        y = _dot(an.reshape(rows, 512), wa_ref[...]) + _dot(cn.reshape(rows, 512), wb_ref[...])
        x1_ref[0] = x_ref[0] + y.reshape(N_RES, TI, d)

    const = lambda r, c, i0=0: pl.BlockSpec((r, c), lambda s: (i0, 0))
    x1, merged = pl.pallas_call(
        body, name="mixer_fwd", grid=(nblk,),
        in_specs=[main(d), main(512), main(1536), before(1536), const(3, 512), const(1, 512), const(1, 512),
                  const(512, d), const(512, d, 1)],
        out_specs=[main(d), main(d)],
        out_shape=[jax.ShapeDtypeStruct(_x4(x).shape, F32), jax.ShapeDtypeStruct(_x4(x).shape, BF16)],
        compiler_params=_params(("parallel",), 48),
    )(_x4(x), _x4(attn), _x4(gates), _x4(gates), cw, g_a, g_c, w_out, w_out)
    return x1.reshape(t, d), merged.reshape(t, d)


def _mixer_bwd(dx1, attn, gates, cw, g_a, g_c, w_out, head_sum, after=()):
    t, d = dx1.shape
    nblk, main, before, _ = _order_specs(t)
    rows = N_RES * TI

    def body(dx_ref, at_ref, gt_ref, gb_ref, cw_ref, ga_ref, gc_ref, wa_ref, wb_ref, hs_ref,
             da_ref, dsum_ref, dy_ref, gga_ref, ggc_ref):
        s = pl.program_id(0)
        dxb = dx_ref[0].reshape(rows, d).astype(BF16)
        dma = _dot_nt(dxb, wa_ref[...]).reshape(N_RES, TI, 512)
        dmc = _dot_nt(dxb, wb_ref[...]).reshape(N_RES, TI, 512)
        attn_v, g_av = at_ref[0], ga_ref[...]
        _, ah, ra = _rms_fwd(attn_v, g_av)
        dattn = _rms_bwd(dma, ah, ra, g_av)
        da_ref[0] = dattn
        z = (dattn * attn_v).reshape(rows, 512)
        hs = hs_ref[...]
        z1 = z.astype(BF16)
        z2 = (z - z1.astype(F32)).astype(BF16)
        dsum_ref[0] = (_dot(z1, hs) + _dot(z2, hs)).reshape(N_RES, TI, 512)
        bg, _, _, _, conv = _conv_fwd(gt_ref[0], gb_ref[0], s == 0, cw_ref[...])
        g_cv = gc_ref[...]
        _, yh, rc = _rms_fwd(bg * conv, g_cv)
        dy_ref[0] = _rms_bwd(dmc, yh, rc, g_cv)
        pa, pc = _sum_tokens(dma * ah), _sum_tokens(dmc * yh)

        @pl.when(s == 0)
        def _():
            gga_ref[...] = pa
            ggc_ref[...] = pc

        @pl.when(s != 0)
        def _():
            gga_ref[...] += pa
            ggc_ref[...] += pc

    const = lambda r, c, i0=0: pl.BlockSpec((r, c), lambda s: (i0, 0))
    shape4 = _x4(attn).shape
    res, _ = _pcall(
        body, name="mixer_bwd", grid=(nblk,),
        in_specs=[main(d), main(512), main(1536), before(1536), const(3, 512), const(1, 512), const(1, 512),
                  const(512, d), const(512, d, 1), const(512, 512)],
        out_specs=[main(512)] * 3 + [const(1, 512), const(1, 512)],
        out_shape=[jax.ShapeDtypeStruct(shape4, F32)] * 3 + [jax.ShapeDtypeStruct((1, 512), F32)] * 2,
        semantics=("arbitrary",), vmem_mb=48, after=after,
    )(_x4(dx1), _x4(attn), _x4(gates), _x4(gates), cw, g_a, g_c, w_out, w_out, head_sum)
    return [r.reshape(t, 512) for r in res[:3]] + res[3:]


def _conv_bwd(dy, gates, cw):
    t = dy.shape[0]
    nblk, main, before, after = _order_specs(t)
    n_i = SEG // TI

    def body(dy_ref, dya_ref, gt_ref, gb_ref, ga_ref, cw_ref, dp_ref, gcw_ref):
        s = pl.program_id(0)
        cw_v, gates_v = cw_ref[...], gt_ref[0]
        bg, u, m1, m2, conv = _conv_fwd(gates_v, gb_ref[0], s == 0, cw_v)
        dy_v = dy_ref[0]
        dconv = dy_v * bg
        dca = dya_ref[0] * ga_ref[0][..., 0:512].astype(F32)
        dca = jnp.where(s == nblk - 1, jnp.zeros_like(dca), dca)
        p1, p2 = _taps_ahead(dconv, dca)
        du = dconv * cw_v[2:3, :] + p1 * cw_v[1:2, :] + p2 * cw_v[0:1, :]
        dp_ref[0, 0] = (dy_v * conv).astype(BF16)
        dp_ref[1, 0] = (du * gates_v[..., 1024:1536].astype(F32)).astype(BF16)
        dp_ref[2, 0] = (du * gates_v[..., 512:1024].astype(F32)).astype(BF16)
        parts = [_sum_tokens(dconv * m2), _sum_tokens(dconv * m1), _sum_tokens(dconv * u)]

        @pl.when(s == 0)
        def _():
            gcw_ref[...] = jnp.zeros_like(gcw_ref)

        for tap in range(3):
            gcw_ref[tap:tap + 1, :] += parts[tap]

    (dproj, gcw), _ = _pcall(
        body, name="conv_bwd", grid=(nblk,),
        in_specs=[main(512), after(512), main(1536), before(1536), after(1536),
                  pl.BlockSpec((3, 512), lambda s: (0, 0))],
        out_specs=[pl.BlockSpec((3, 1, N_RES, TI, 512), lambda s: (1, s // n_i, 0, s % n_i, 0)),
                   pl.BlockSpec((8, 512), lambda s: (0, 0))],
        out_shape=[jax.ShapeDtypeStruct((6, t // HALF, N_RES, SEG, 512), BF16), jax.ShapeDtypeStruct((8, 512), F32)],
        semantics=("arbitrary",), vmem_mb=40,
    )(_x4(dy), _x4(dy), _x4(gates), _x4(gates), _x4(gates), cw)
    return dproj.reshape(6, t, 512), gcw


def _xattn_fwd(x1, g, w_q, kv, w_o, *, tb):
    t, d = x1.shape
    hd = d // N_MEM_HEADS
    m = kv.shape[0]

    def body(x_ref, g_ref, wq_ref, k_ref, v_ref, wo_ref, x2_ref, h_ref, q_ref, o_ref):
        xv = x_ref[...]
        h = _rms_fwd(xv, g_ref[...])[0].astype(BF16)
        h_ref[...] = h
        q = _dot(h, wq_ref[...]).astype(BF16)
        q_ref[...] = q
        for hh in range(N_MEM_HEADS):
            sl = slice(hh * hd, (hh + 1) * hd)
            s = _dot_nt(q[:, sl], k_ref[:, sl]) * (1.0 / 16.0)
            e = jnp.exp(s - jnp.max(s, axis=1, keepdims=True))
            p = e / jnp.sum(e, axis=1, keepdims=True)
            o_ref[:, sl] = _dot(p.astype(BF16), v_ref[:, sl]).astype(BF16)
        x2_ref[...] = xv + _dot(o_ref[...], wo_ref[...])

    tok = pl.BlockSpec((tb, d), lambda i: (i, 0))
    full = pl.BlockSpec((d, d), lambda i: (0, 0))
    return pl.pallas_call(
        body, name="xattn_fwd", grid=(t // tb,),
        in_specs=[tok, pl.BlockSpec((1, d), lambda i: (0, 0)), full,
                  pl.BlockSpec((m, d), lambda i: (0, 0)), pl.BlockSpec((m, d), lambda i: (0, 1)), full],
        out_specs=[tok] * 4,
        out_shape=[jax.ShapeDtypeStruct((t, d), F32)] + [jax.ShapeDtypeStruct((t, d), BF16)] * 3,
        compiler_params=_params(("parallel",), 48),
    )(x1, g, w_q, kv, kv, w_o)


def _xattn_bwd(dx2, x1, g, q, w_q, kv, w_o, *, tb, after=()):
    t, d = x1.shape
    hd = d // N_MEM_HEADS
    m = kv.shape[0]

    def body(dx2_ref, x_ref, g_ref, q_ref, wq_ref, k_ref, v_ref, wo_ref,
             dx1_ref, dx1b_ref, dq_ref, dk_ref, dv_ref, gg_ref):
        i = pl.program_id(0)

        @pl.when(i == 0)
        def _():
            dk_ref[...] = jnp.zeros_like(dk_ref)
            dv_ref[...] = jnp.zeros_like(dv_ref)

        dx2 = dx2_ref[...]
        do = _dot_nt(dx2.astype(BF16), wo_ref[...]).astype(BF16)
        for hh in range(N_MEM_HEADS):
            sl = slice(hh * hd, (hh + 1) * hd)
            qh, kh, vh, doh = q_ref[:, sl], k_ref[:, sl], v_ref[:, sl], do[:, sl]
            s = _dot_nt(qh, kh) * (1.0 / 16.0)
            e = jnp.exp(s - jnp.max(s, axis=1, keepdims=True))
            p = e / jnp.sum(e, axis=1, keepdims=True)
            dp = _dot_nt(doh, vh)
            ds = (p * (dp - jnp.sum(dp * p, axis=1, keepdims=True)) * (1.0 / 16.0)).astype(BF16)
            dq_ref[:, sl] = _dot(ds, kh).astype(BF16)
            dk_ref[:, sl] += _dot_tn(ds, qh)
            dv_ref[:, sl] += _dot_tn(p.astype(BF16), doh)
        dh = _dot_nt(dq_ref[...], wq_ref[...])
        g_v = g_ref[...]
        _, xh, r = _rms_fwd(x_ref[...], g_v)
        dx1 = dx2 + _rms_bwd(dh, xh, r, g_v)
        dx1_ref[...] = dx1
        dx1b_ref[...] = dx1.astype(BF16)
        part = jnp.sum(dh * xh, axis=0, keepdims=True)

        @pl.when(i == 0)
        def _():
            gg_ref[...] = part

        @pl.when(i != 0)
        def _():
            gg_ref[...] += part

    tok = pl.BlockSpec((tb, d), lambda i: (i, 0))
    full = pl.BlockSpec((d, d), lambda i: (0, 0))
    acc = pl.BlockSpec((m, d), lambda i: (0, 0))
    res, _ = _pcall(
        body, name="xattn_bwd", grid=(t // tb,),
        in_specs=[tok, tok, pl.BlockSpec((1, d), lambda i: (0, 0)), tok, full,
                  pl.BlockSpec((m, d), lambda i: (0, 0)), pl.BlockSpec((m, d), lambda i: (0, 1)), full],
        out_specs=[tok, tok, tok, acc, acc, pl.BlockSpec((1, d), lambda i: (0, 0))],
        out_shape=[jax.ShapeDtypeStruct((t, d), F32), jax.ShapeDtypeStruct((t, d), BF16),
                   jax.ShapeDtypeStruct((t, d), BF16),
                   jax.ShapeDtypeStruct((m, d), F32), jax.ShapeDtypeStruct((m, d), F32),
                   jax.ShapeDtypeStruct((1, d), F32)],
        semantics=("arbitrary",), vmem_mb=48, after=after,
    )(dx2, x1, g, q, w_q, kv, kv, w_o)
    return res


def _mlp_down_loss(a, w_down, x2, tgt, g, *, tb):
    t, d = x2.shape
    f = a.shape[1]

    def body(a_ref, w_ref, x_ref, t_ref, g_ref, dx_ref, dxb_ref, loss_ref, gg_ref):
        i = pl.program_id(0)
        av = a_ref[...].astype(F32)
        x3 = x_ref[...] + _dot((av * av).astype(BF16), w_ref[...])
        g_v = g_ref[...]
        out, xh, r = _rms_fwd(x3, g_v)
        err = out - t_ref[...]
        dout = err * (1.0 / d)
        dx = _rms_bwd(dout, xh, r, g_v)
        dx_ref[...] = dx
        dxb_ref[...] = dx.astype(BF16)
        part = jnp.sum(dout * xh, axis=0, keepdims=True)
        lpart = 0.5 * jnp.sum(jnp.mean(err * err, axis=-1, keepdims=True), axis=0, keepdims=True)
        lpart = jnp.broadcast_to(lpart, loss_ref.shape)

        @pl.when(i == 0)
        def _():
            gg_ref[...] = part
            loss_ref[...] = lpart

        @pl.when(i != 0)
        def _():
            gg_ref[...] += part
            loss_ref[...] += lpart

    tok = pl.BlockSpec((tb, d), lambda i: (i, 0))
    return pl.pallas_call(
        body, name="mlp_down_loss", grid=(t // tb,),
        in_specs=[pl.BlockSpec((tb, f), lambda i: (i, 0)), pl.BlockSpec((f, d), lambda i: (0, 0)), tok, tok,
                  pl.BlockSpec((1, d), lambda i: (0, 0))],
        out_specs=[tok, tok, pl.BlockSpec((8, 128), lambda i: (0, 0)), pl.BlockSpec((1, d), lambda i: (0, 0))],
        out_shape=[jax.ShapeDtypeStruct((t, d), F32), jax.ShapeDtypeStruct((t, d), BF16),
                   jax.ShapeDtypeStruct((8, 128), F32), jax.ShapeDtypeStruct((1, d), F32)],
        compiler_params=_params(("arbitrary",), 56),
    )(a, w_down, x2, tgt, g)


def _mlp_dpre(dx3, w_down, a, *, tb, bn):
    t, d = dx3.shape
    f = a.shape[1]

    def body(dx_ref, w_ref, a_ref, o_ref):
        o_ref[...] = (2.0 * a_ref[...].astype(F32) * _dot_nt(dx_ref[...], w_ref[...])).astype(BF16)

    return pl.pallas_call(
        body, name="mlp_dpre", grid=(t // tb, f // bn),
        in_specs=[pl.BlockSpec((tb, d), lambda i, j: (i, 0)), pl.BlockSpec((bn, d), lambda i, j: (j, 0)),
                  pl.BlockSpec((tb, bn), lambda i, j: (i, j))],
        out_specs=pl.BlockSpec((tb, bn), lambda i, j: (i, j)),
        out_shape=jax.ShapeDtypeStruct((t, f), BF16),
        compiler_params=_params(("parallel", "arbitrary"), 48),
    )(dx3, w_down, a)


def _adamw(gsum, w, m, v):
    m_new = ADAM_B1 * m + (1.0 - ADAM_B1) * gsum
    v_new = ADAM_B2 * v + (1.0 - ADAM_B2) * (gsum * gsum)
    m_hat = m_new / (1.0 - ADAM_B1 ** ADAM_STEP)
    v_hat = v_new / (1.0 - ADAM_B2 ** ADAM_STEP)
    delta = -ADAM_LR * (m_hat / (jnp.sqrt(v_hat) + ADAM_EPS) + ADAM_WD * w)
    return delta, m_new, v_new


def _sum_adamw(parts, w, m, v, *, name, tr):
    r, c = w.shape

    def body(p_ref, w_ref, m_ref, v_ref, g_ref, d_ref, mo_ref, vo_ref):
        g = p_ref[0].astype(F32)
        for k in range(1, N_DEV):
            g = g + p_ref[k].astype(F32)
        g_ref[...] = g
        d_ref[...], mo_ref[...], vo_ref[...] = _adamw(g, w_ref[...], m_ref[...], v_ref[...])

    blk = pl.BlockSpec((tr, c), lambda i: (i, 0))
    return pl.pallas_call(
        body, name=name, grid=(r // tr,),
        in_specs=[pl.BlockSpec((N_DEV, tr, c), lambda i: (0, i, 0)), blk, blk, blk],
        out_specs=[blk] * 4, out_shape=[jax.ShapeDtypeStruct((r, c), F32)] * 4,
        compiler_params=_params(("parallel",), 40),
    )(parts, w, m, v)


def _sum_small(parts):
    _, r, c = parts.shape

    def body(p_ref, o_ref):
        s = p_ref[0]
        for k in range(1, N_DEV):
            s = s + p_ref[k]
        o_ref[...] = s

    return pl.pallas_call(body, name="sum_small", out_shape=jax.ShapeDtypeStruct((r, c), F32))(parts)


def _adamw_small(g, w, m, v):
    def body(g_ref, w_ref, m_ref, v_ref, d_ref, mo_ref, vo_ref):
        d_ref[...], mo_ref[...], vo_ref[...] = _adamw(g_ref[...], w_ref[...], m_ref[...], v_ref[...])

    return pl.pallas_call(body, name="adamw_small", out_shape=[jax.ShapeDtypeStruct(g.shape, F32)] * 3)(g, w, m, v)


def _head_sum_matrix():
    r = lax.broadcasted_iota(jnp.int32, (512, 512), 0) // HEAD_DIM
    c = lax.broadcasted_iota(jnp.int32, (512, 512), 1) // HEAD_DIM
    return (r == c).astype(BF16)


_SHARD_AXIS = dict(w_in=1, w_out=0, w_q=0, w_kv=1, w_o=0, w_up=1, w_down=0, conv_w=None, small=None)


class _Weights:
    def __init__(self, full, shards=None):
        self.full = dict(full)
        self.shards = shards

    def rider(self, names, late=False):
        if self.shards is None:
            return None
        return _Gather([self.shards[n] for n in names], [_SHARD_AXIS[n] for n in names], late)

    def arrived(self, names, gathered):
        if gathered is not None:
            for n, g in zip(names, gathered):
                self.full[n] = g.transpose(1, 0, 2).reshape(g.shape[1], -1) if n == "conv_w" else g

    def __getitem__(self, name):
        return self.full[name]


class _Grads:
    def __init__(self, distributed):
        self.distributed = distributed
        self.local = {}
        self.pending = {}

    def add(self, name, g):
        self.local[name] = g

    def send(self, *names):
        if not self.distributed:
            return []
        rider = _Exchange([self.local[n] for n in names], [_SHARD_AXIS[n] for n in names])
        started = _exchange_start(rider, "send_" + "_".join(names))
        self.pending[names[0]] = (names, rider, started)
        return [started[3]]

    def wait(self, first_name, after):
        names, rider, started = self.pending.pop(first_name)
        return _exchange_wait(rider, started, after, "wait_" + "_".join(names))


def _ride(fn, *args, rider=None, **kw):
    if rider is None:
        return fn(*args, **kw), None
    return fn(*args, rider=rider, **kw)


def _local_step(x, mem, tgt, gains, weights, grads):
    names = ["w_in", "conv_w"]
    (x, tgt), got = _ride(_reorder, [x, tgt], "reorder_in", rider=weights.rider(names, late=True))
    weights.arrived(names, got)
    w_in, cw = weights["w_in"], weights["conv_w"]

    names = ["w_out", "w_kv"]
    (qkv, gates, h1), got = _ride(_proj, x, gains["g_mix"], w_in, tb=1024, rider=weights.rider(names))
    weights.arrived(names, got)
    names = ["w_q", "w_o", "w_up"]
    (attn, *lses), got = _ride(_attention_fwd, qkv, rider=weights.rider(names))
    weights.arrived(names, got)
    x1, merged = _mixer_fwd(x, attn, gates, cw, gains["g_attn_out"], gains["g_conv_out"], weights["w_out"])
    kv, mem_n = _norm_matmul(mem, gains["g_mem"], weights["w_kv"], name="mem_kv", out_dtype=BF16, tb=mem.shape[0],
                             bn=1024, save_h=True)
    x2, h2, qm, om = _xattn_fwd(x1, gains["g_xattn"], weights["w_q"], kv, weights["w_o"], tb=512)
    w_up = weights["w_up"]
    (a, h3), got = _ride(_norm_matmul, x2, gains["g_mlp"], w_up, name="mlp_up", out_dtype=BF16, tb=1024, bn=2048,
                         relu=True, save_h=True, rider=weights.rider(["w_down"], late=True))
    weights.arrived(["w_down"], got)
    w_down = weights["w_down"]
    dx3, dx3b, loss_blk, gg_final = _mlp_down_loss(a, w_down, x2, tgt, gains["g_final"], tb=512)

    dpre = _mlp_dpre(dx3b, w_down, a, tb=1024, bn=2048)
    grads.add("w_down", _matmul_tn(a, dx3b, name="grad_w_down", bm=512, bn=1024, square_a=True))
    sent = grads.send("w_down")
    grads.add("w_up", _matmul_tn(h3, dpre, name="grad_w_up", bm=1024, bn=1024, after=sent))
    sent = grads.send("w_up")
    dx2, dx2b, gg_mlp = _matmul_nt_normbwd(dpre, w_up, x2, gains["g_mlp"], dx3, name="mlp_dx", tb=512,
                                           also_bf16=True, after=sent)

    grads.add("w_o", _matmul_tn(om, dx2b, name="grad_w_o", bm=512, bn=512))
    dx1, dx1b, dqm, dk, dv, gg_xattn = _xattn_bwd(dx2, x1, gains["g_xattn"], qm, weights["w_q"], kv, weights["w_o"],
                                                  tb=512)
    grads.add("w_q", _matmul_tn(h2, dqm, name="grad_w_q", bm=1024, bn=512))
    dkv = jnp.concatenate([dk, dv], axis=1).astype(BF16)
    grads.add("w_kv", _matmul_tn(mem_n, dkv, name="grad_w_kv", bm=1024, bn=1024))
    _, gg_mem = _matmul_nt_normbwd(dkv, weights["w_kv"], mem, gains["g_mem"], None, name="mem_dx", tb=mem.shape[0])

    grads.add("w_out", _matmul_tn(merged, dx1b, name="grad_w_out", bm=1024, bn=512))
    sent = grads.send("w_o", "w_q", "w_kv", "w_out")
    dattn, dsum, dy, gg_attn, gg_conv = _mixer_bwd(dx1, attn, gates, cw, gains["g_attn_out"], gains["g_conv_out"],
                                                   weights["w_out"], _head_sum_matrix(), after=sent)
    dproj, gcw = _conv_bwd(dy, gates, cw)
    dproj = _attention_bwd(qkv, dattn, dsum, lses, dproj)
    grads.add("w_in", _matmul_tn(h1, dproj, name="grad_w_in", bm=1024, bn=512))
    sent = grads.send("w_in")
    grad_x, gg_mix = _matmul_nt_normbwd(dproj, w_in, x, gains["g_mix"], dx1, name="mixer_dx", tb=512,
                                        to_natural=True, after=sent)

    def part(v):
        return jnp.pad(v, ((0, SMALL_PART - v.shape[0]), (0, 1024 - v.shape[1])))

    parts = [gg_mix, gg_xattn, gg_mem, gg_mlp, gg_final, jnp.concatenate([gg_attn, gg_conv], axis=1), gcw, loss_blk]
    grads.add("small", jnp.concatenate([part(v) for v in parts], axis=0))
    return grad_x


SMALL_PART = 8
_BIG = ("w_in", "w_out", "w_q", "w_kv", "w_o", "w_up", "w_down")
_GAIN_ROWS = ("g_mix", "g_xattn", "g_mem", "g_mlp", "g_final")


def _pack_small(vals, conv):
    rows = [vals[k].reshape(1, -1) for k in _GAIN_ROWS]
    rows.append(jnp.concatenate([vals["g_attn_out"].reshape(1, -1), vals["g_conv_out"].reshape(1, -1)], axis=1))
    flat = conv.reshape(1, -1)
    rows.append(jnp.pad(flat, ((0, 0), (0, 1024 - flat.shape[1]))))
    rows.append(jnp.zeros((1, 1024), F32))
    return jnp.concatenate(rows, axis=0)


def kernel(x, mem, g_mix, w_in, conv_w, g_attn_out, g_conv_out, w_out, g_xattn, g_mem, w_q_mem, w_kv_mem, w_o_mem, g_mlp, w_up, w_down, g_final, loss_target, m_g_mix, m_w_in, m_conv_w, m_g_attn_out, m_g_conv_out, m_w_out, m_g_xattn, m_g_mem, m_w_q_mem, m_w_kv_mem, m_w_o_mem, m_g_mlp, m_w_up, m_w_down, m_g_final, v_g_mix, v_w_in, v_conv_w, v_g_attn_out, v_g_conv_out, v_w_out, v_g_xattn, v_g_mem, v_w_q_mem, v_w_kv_mem, v_w_o_mem, v_g_mlp, v_w_up, v_w_down, v_g_final):
    d = x.shape[-1]
    me = 4 * lax.axis_index("x") + 2 * lax.axis_index("y") + lax.axis_index("c")
    w_shards = dict(w_in=w_in, w_out=w_out, w_q=w_q_mem, w_kv=w_kv_mem, w_o=w_o_mem, w_up=w_up, w_down=w_down)
    m_shards = dict(w_in=m_w_in, w_out=m_w_out, w_q=m_w_q_mem, w_kv=m_w_kv_mem, w_o=m_w_o_mem, w_up=m_w_up,
                    w_down=m_w_down)
    v_shards = dict(w_in=v_w_in, w_out=v_w_out, w_q=v_w_q_mem, w_kv=v_w_kv_mem, w_o=v_w_o_mem, w_up=v_w_up,
                    w_down=v_w_down)
    gains = dict(g_mix=g_mix, g_attn_out=g_attn_out, g_conv_out=g_conv_out, g_xattn=g_xattn, g_mem=g_mem,
                 g_mlp=g_mlp, g_final=g_final)
    gains2 = {k: v.reshape(1, -1) for k, v in gains.items()}

    shards = {k: w_shards[k].astype(BF16) for k in _BIG}
    shards["conv_w"] = conv_w
    grads = _Grads(distributed=True)
    grad_x = _local_step(x[0], mem[0], loss_target[0], gains2, _Weights({}, shards), grads)

    after = grads.send("small")
    outs = {}
    tiles = dict(w_in=256, w_out=128, w_q=128, w_kv=256, w_o=128, w_up=256, w_down=256)
    for group in (("w_down",), ("w_up",), ("w_o", "w_q", "w_kv", "w_out"), ("w_in",)):
        for k, received in zip(group, grads.wait(group[0], after)):
            outs[k] = _sum_adamw(received, w_shards[k], m_shards[k], v_shards[k], name=f"adamw_{k}", tr=tiles[k])
            after = [outs[k][0]]
    small_received, = grads.wait("small", after)

    ssum = _sum_small(small_received)
    row = lambda i: ssum[SMALL_PART * i]
    loss = ssum[SMALL_PART * 7, 0]
    g_small = {k: row(i) for i, k in enumerate(_GAIN_ROWS)}
    g_small["g_attn_out"] = row(5)[0:512]
    g_small["g_conv_out"] = row(5)[512:1024]
    taps = ssum[SMALL_PART * 6:SMALL_PART * 6 + 3, 0:512]
    g_conv = lax.dynamic_slice_in_dim(taps, me * 64, 64, axis=1)
    m_small = dict(g_mix=m_g_mix, g_attn_out=m_g_attn_out, g_conv_out=m_g_conv_out, g_xattn=m_g_xattn,
                   g_mem=m_g_mem, g_mlp=m_g_mlp, g_final=m_g_final)
    v_small = dict(g_mix=v_g_mix, g_attn_out=v_g_attn_out, g_conv_out=v_g_conv_out, g_xattn=v_g_xattn,
                   g_mem=v_g_mem, g_mlp=v_g_mlp, g_final=v_g_final)
    packed = [_pack_small(g_small, g_conv), _pack_small(gains, conv_w), _pack_small(m_small, m_conv_w),
              _pack_small(v_small, v_conv_w)]
    upd = _adamw_small(*packed)

    def unpack(p):
        res = {k: p[i] for i, k in enumerate(_GAIN_ROWS)}
        res["g_attn_out"] = p[5, 0:512]
        res["g_conv_out"] = p[5, 512:1024]
        res["conv_w"] = p[6, 0:192].reshape(3, 64)
        return res

    g_small["conv_w"] = g_conv
    small_out = [g_small] + [unpack(p) for p in upd]
    names = {"g_mix": "g_mix", "w_in": "w_in", "conv_w": "conv_w", "g_attn_out": "g_attn_out",
             "g_conv_out": "g_conv_out", "w_out": "w_out", "g_xattn": "g_xattn", "g_mem": "g_mem",
             "w_q_mem": "w_q", "w_kv_mem": "w_kv", "w_o_mem": "w_o", "g_mlp": "g_mlp", "w_up": "w_up",
             "w_down": "w_down", "g_final": "g_final"}
    result = [loss, grad_x[None]]
    for which in range(4):
        for key in names.values():
            result.append(outs[key][which] if key in outs else small_out[which][key])
    return tuple(result)
```

```python
import math

import jax
import jax.numpy as jnp
from jax import lax
from jax.experimental import pallas as pl
from jax.experimental.pallas import tpu as pltpu

F32 = jnp.float32
BF16 = jnp.bfloat16
NORM_EPS = 1e-6
NEG_INF = -1e30
N_DEV = 8
BLK = 128
HEAD_DIM = 64
N_MEM_HEADS = 4
ADAM_LR = 0.001
ADAM_B1 = 0.9
ADAM_B2 = 0.999
ADAM_EPS = 1e-08
ADAM_WD = 0.01
ADAM_STEP = 10
MESH = pl.DeviceIdType.MESH
ANY = pl.BlockSpec(memory_space=pl.ANY)


def _dot(a, b):
    return jnp.dot(a, b, preferred_element_type=F32)


def _dot_nt(a, b):
    return lax.dot_general(a, b, (((1,), (1,)), ((), ())), preferred_element_type=F32)


def _dot_tn(a, b):
    return lax.dot_general(a, b, (((0,), (0,)), ((), ())), preferred_element_type=F32)


def _params(semantics, vmem_mb):
    return pltpu.CompilerParams(dimension_semantics=semantics, vmem_limit_bytes=vmem_mb << 20)


def _rms_fwd(x, g):
    r = lax.rsqrt(jnp.mean(x * x, axis=-1, keepdims=True) + NORM_EPS)
    xh = x * r
    return xh * g, xh, r


def _rms_bwd(dy, xh, r, g):
    gy = dy * g
    return r * (gy - xh * jnp.mean(xh * gy, axis=-1, keepdims=True))


def _position():
    x, y, c = lax.axis_index("x"), lax.axis_index("y"), lax.axis_index("c")
    return x, y, c


def _block_of(ref, j, axis, shard_shape):
    r, c = shard_shape
    if axis is None:
        return ref.at[j]
    if axis == 0:
        return ref.at[pl.ds(j * r, r), :]
    return ref.at[:, pl.ds(j * c, c)]


class _Gather:
    has_mid = True
    alias_pairs = ()

    def __init__(self, shards, axes, late=False):
        self.arrays = list(shards)
        self.axes = list(axes)
        self.late = late
        self.n = len(self.arrays)

    def out_shape(self):
        res = []
        for s, axis in zip(self.arrays, self.axes):
            r, c = s.shape
            shape = (N_DEV, r, c) if axis is None else (N_DEV * r, c) if axis == 0 else (r, N_DEV * c)
            res.append(jax.ShapeDtypeStruct(shape, s.dtype))
        return res

    def scratch(self):
        return [pltpu.SemaphoreType.DMA((self.n, 7)), pltpu.SemaphoreType.DMA((self.n, 7)),
                pltpu.SemaphoreType.DMA((self.n,))]

    def _ctx(self, ins, outs, sems):
        send_sems, recv_sems, local_sems = sems
        x, y, c = _position()
        me, sibling = (x, y, c), (x, y, 1 - c)
        chips = [(1 - x, y), (x, 1 - y), (1 - x, 1 - y)]

        def lin(px, py, pc):
            return 4 * px + 2 * py + pc

        def place(a, block):
            return _block_of(outs[a], lin(*block), self.axes[a], self.arrays[a].shape)

        def copy(a, k, block, to, src=None):
            dst = place(a, block)
            return pltpu.make_async_remote_copy(
                src_ref=dst if src is None else src, dst_ref=dst,
                send_sem=send_sems.at[a, k], recv_sem=recv_sems.at[a, k],
                device_id=to, device_id_type=MESH)

        def mine():
            return [pltpu.make_async_copy(ins[a], place(a, me), local_sems.at[a]) for a in range(self.n)]

        def first():
            res = []
            for a in range(self.n):
                res.append(copy(a, 0, me, sibling, src=ins[a]))
                res += [copy(a, 1 + j, me, (*chip, c), src=ins[a]) for j, chip in enumerate(chips)]
            return res

        return c, me, sibling, chips, copy, mine, first

    def start(self, ins, outs, sems):
        _, _, _, _, _, mine, first = self._ctx(ins, outs, sems)
        for cp in mine() + first():
            cp.start()

    def mid(self, ins, outs, sems):
        c, me, sibling, chips, copy, _, _ = self._ctx(ins, outs, sems)
        for j, chip in enumerate(chips):
            for a in range(self.n):
                copy(a, 1 + j, (*chip, c), me).wait_recv()
                copy(a, 4 + j, (*chip, c), sibling).start()

    def finish(self, ins, outs, sems):
        c, me, sibling, chips, copy, mine, first = self._ctx(ins, outs, sems)
        for a in range(self.n):
            copy(a, 0, sibling, me).wait_recv()
            for j, chip in enumerate(chips):
                copy(a, 4 + j, (*chip, 1 - c), me).wait_recv()
        for cp in first():
            cp.wait_send()
        for j, chip in enumerate(chips):
            for a in range(self.n):
                copy(a, 4 + j, (*chip, c), sibling).wait_send()
        for cp in mine():
            cp.wait()


class _Exchange:
    def __init__(self, parts, axes):
        self.n = len(parts)
        self.axes = list(axes)
        self.arrays = list(parts)

    def _piece(self, a):
        r, c = self.arrays[a].shape
        axis = self.axes[a]
        return (r, c) if axis is None else (r // N_DEV, c) if axis == 0 else (r, c // N_DEV)

    def out_shape(self):
        return [jax.ShapeDtypeStruct((N_DEV,) + self._piece(a), self.arrays[a].dtype) for a in range(self.n)]

    def semaphores(self):
        return [pltpu.SemaphoreType.DMA((7 * self.n,)), pltpu.SemaphoreType.DMA((7 * self.n,)),
                pltpu.SemaphoreType.DMA((self.n,))]

    def _ctx(self, ins, outs, sems):
        send_sems, recv_sems, local_sems = sems
        x, y, c = _position()
        me = 4 * x + 2 * y + c

        def src(a, j):
            return ins[a] if self.axes[a] is None else _block_of(ins[a], j, self.axes[a], self._piece(a))

        def dst(a, j):
            return outs[a].at[j]

        def local():
            return [pltpu.make_async_copy(src(a, me), dst(a, me), local_sems.at[a]) for a in range(self.n)]

        def remote(inbound):
            res = []
            for a in range(self.n):
                for k in range(1, N_DEV):
                    peer = (1 - x if k & 4 else x, 1 - y if k & 2 else y, 1 - c if k & 1 else c)
                    plin = 4 * peer[0] + 2 * peer[1] + peer[2]
                    res.append(pltpu.make_async_remote_copy(
                        src_ref=src(a, plin), dst_ref=dst(a, plin if inbound else me),
                        send_sem=send_sems.at[7 * a + k - 1], recv_sem=recv_sems.at[7 * a + k - 1],
                        device_id=peer, device_id_type=MESH))
            return res

        return local, remote

    def start(self, ins, outs, sems):
        local, remote = self._ctx(ins, outs, sems)
        for cp in local() + remote(False):
            cp.start()

    def finish(self, ins, outs, sems):
        local, remote = self._ctx(ins, outs, sems)
        for cp in remote(True):
            cp.wait_recv()
        for cp in remote(False):
            cp.wait_send()
        for cp in local():
            cp.wait()


def _exchange_start(rider, name):
    n = rider.n
    parts = rider.arrays
    lands = [lax.empty(s.shape, s.dtype) for s in rider.out_shape()]
    hbm = pl.BlockSpec(memory_space=pltpu.HBM)
    sem = pl.BlockSpec(memory_space=pltpu.SEMAPHORE)

    def body(*refs):
        ins, sems = refs[:n], refs[2 * n:2 * n + 3]
        outs, token = refs[2 * n + 3 + n:2 * n + 3 + 2 * n], refs[-1]
        rider.start(ins, outs, sems)
        token[...] = jnp.zeros_like(token)

    res = pl.pallas_call(
        body, name=name,
        out_shape=rider.semaphores() + [pltpu.HBM(p.shape, p.dtype) for p in parts]
                  + [pltpu.HBM(z.shape, z.dtype) for z in lands] + [jax.ShapeDtypeStruct((8, 128), F32)],
        in_specs=[hbm] * (2 * n), out_specs=[sem] * 3 + [hbm] * (2 * n) + [pl.BlockSpec(memory_space=pltpu.VMEM)],
        input_output_aliases={i: 3 + i for i in range(2 * n)},
        compiler_params=pltpu.CompilerParams(has_side_effects=pltpu.SideEffectType.DATAFLOW_SIDE_EFFECTING),
    )(*[pltpu.with_memory_space_constraint(a, pltpu.HBM) for a in parts + lands])
    return res[:3], res[3:3 + n], res[3 + n:3 + 2 * n], res[-1]


def _exchange_wait(rider, started, after, name):
    n = rider.n
    sems, parts, lands, _ = started
    hbm = pl.BlockSpec(memory_space=pltpu.HBM)
    sem = pl.BlockSpec(memory_space=pltpu.SEMAPHORE)

    def body(*refs):
        rider.finish(refs[:n], refs[n:2 * n], refs[2 * n:2 * n + 3])

    res = pl.pallas_call(
        body, name=name, out_shape=[pltpu.HBM(a.shape, a.dtype) for a in list(parts) + list(lands)],
        in_specs=[hbm] * (2 * n) + [sem] * 3 + [ANY] * len(after), out_specs=[hbm] * (2 * n),
        input_output_aliases={i: i for i in range(2 * n)},
        compiler_params=pltpu.CompilerParams(has_side_effects=pltpu.SideEffectType.DATAFLOW_SIDE_EFFECTING),
    )(*parts, *lands, *sems, *after)
    return list(res[n:])


def _pcall(body, *, name, grid, in_specs, out_specs, out_shape, scratch_shapes=(), semantics, vmem_mb, rider=None,
           aliases=None, after=()):
    in_specs, out_specs, out_shape = list(in_specs), list(out_specs), list(out_shape)
    scratch_shapes = list(scratch_shapes)
    aliases = dict(aliases or {})
    if rider is None:
        n_in, after = len(in_specs), list(after)

        def plain(*refs):
            body(*refs[:n_in], *refs[n_in + len(after):])

        call = pl.pallas_call(plain if after else body, name=name, grid=grid, in_specs=in_specs + [ANY] * len(after),
                              out_specs=out_specs, out_shape=out_shape, scratch_shapes=scratch_shapes,
                              input_output_aliases=aliases, compiler_params=_params(semantics, vmem_mb))
        return lambda *args: (list(call(*args, *after)), None)
    n_in, n_out, n_scr = len(in_specs), len(out_specs), len(scratch_shapes)
    r_in, r_shapes = len(rider.arrays), rider.out_shape()
    r_out = len(r_shapes)
    aliases.update({n_in + i: n_out + o for i, o in rider.alias_pairs})
    total = math.prod(grid)
    mid_step = total - 1 if rider.has_mid and rider.late else (3 * total) // 4

    def wrapped(*refs):
        bounds = [0, n_in, r_in, n_out, r_out, n_scr]
        for i in range(1, len(bounds)):
            bounds[i] += bounds[i - 1]
        a, ra, o, ro, s = (refs[bounds[i]:bounds[i + 1]] for i in range(5))
        rs = refs[bounds[5]:]
        step = pl.program_id(0)
        for k in range(1, len(grid)):
            step = step * grid[k] + pl.program_id(k)
        pl.when(step == 0)(lambda: rider.start(ra, ro, rs))
        body(*a, *o, *s)
        if rider.has_mid:
            pl.when(step == mid_step)(lambda: rider.mid(ra, ro, rs))
        pl.when(step == total - 1)(lambda: rider.finish(ra, ro, rs))

    call = pl.pallas_call(
        wrapped, name=name, grid=grid, in_specs=in_specs + [ANY] * r_in, out_specs=out_specs + [ANY] * r_out,
        out_shape=out_shape + r_shapes, scratch_shapes=scratch_shapes + rider.scratch(),
        input_output_aliases=aliases, compiler_params=_params(("arbitrary",) * len(grid), vmem_mb))

    def run(*args):
        res = call(*args, *rider.arrays)
        return list(res[:n_out]), list(res[n_out:])

    return run


def _norm_matmul(x, g, w, *, name, out_dtype, tb, bn, relu=False, save_h=False, rider=None):
    t, d = x.shape
    n = w.shape[1]

    def body(x_ref, g_ref, w_ref, o_ref, *rest):
        h_scr = rest[-1]

        @pl.when(pl.program_id(1) == 0)
        def _():
            h = _rms_fwd(x_ref[...], g_ref[...])[0].astype(BF16)
            h_scr[...] = h
            if save_h:
                rest[0][...] = h

        acc = _dot(h_scr[...], w_ref[...])
        if relu:
            acc = jnp.maximum(acc, 0.0)
        o_ref[...] = acc.astype(out_dtype)

    out_shape = [jax.ShapeDtypeStruct((t, n), out_dtype)]
    out_specs = [pl.BlockSpec((tb, bn), lambda i, j: (i, j))]
    if save_h:
        out_shape.append(jax.ShapeDtypeStruct((t, d), BF16))
        out_specs.append(pl.BlockSpec((tb, d), lambda i, j: (i, 0)))
    res, extra = _pcall(
        body, name=name, grid=(t // tb, n // bn),
        in_specs=[pl.BlockSpec((tb, d), lambda i, j: (i, 0)),
                  pl.BlockSpec((1, d), lambda i, j: (0, 0)),
                  pl.BlockSpec((d, bn), lambda i, j: (0, j))],
        out_specs=out_specs, out_shape=out_shape,
        scratch_shapes=[pltpu.VMEM((tb, d), BF16)],
        semantics=("parallel", "arbitrary"), vmem_mb=48, rider=rider,
    )(x, g, w)
    res = res if save_h else res[0]
    return res if rider is None else (res, extra)


def _proj(x, g, w, *, tb, rider=None):
    t, d = x.shape
    half = w.shape[1] // 2

    def body(x_ref, g_ref, w_ref, qkv_ref, gates_ref, h_ref, h_scr):
        j = pl.program_id(1)

        @pl.when(j == 0)
        def _():
            h = _rms_fwd(x_ref[...], g_ref[...])[0].astype(BF16)
            h_scr[...] = h
            h_ref[...] = h

        acc = _dot(h_scr[...], w_ref[...])

        @pl.when(j == 0)
        def _():
            qkv_ref[...] = acc

        @pl.when(j == 1)
        def _():
            gates_ref[...] = acc.astype(BF16)

    tok = lambda c: pl.BlockSpec((tb, c), lambda i, j: (i, 0))
    res, extra = _pcall(
        body, name="proj", grid=(t // tb, 2),
        in_specs=[tok(d), pl.BlockSpec((1, d), lambda i, j: (0, 0)), pl.BlockSpec((d, half), lambda i, j: (0, j))],
        out_specs=[tok(half), tok(half), tok(d)],
        out_shape=[jax.ShapeDtypeStruct((t, half), F32), jax.ShapeDtypeStruct((t, half), BF16),
                   jax.ShapeDtypeStruct((t, d), BF16)],
        scratch_shapes=[pltpu.VMEM((tb, d), BF16)],
        semantics=("parallel", "arbitrary"), vmem_mb=48, rider=rider,
    )(x, g, w)
    return res if rider is None else (res, extra)


def _matmul_nt_normbwd(dy, w, x, g, dres, *, name, tb, also_bf16=False, to_natural=False, after=()):
    t, d = x.shape
    stacked = dy.ndim == 3
    has_res = dres is not None
    n_i = SEG // TI
    if to_natural:
        tb = N_RES * TI

    def body(dy_ref, w_ref, x_ref, g_ref, *rest):
        rest = list(rest)
        dres_ref = rest.pop(0) if has_res else None
        dx_ref = rest.pop(0)
        dxb_ref = rest.pop(0) if also_bf16 else None
        gg_ref = rest.pop(0)
        i = pl.program_id(0)

        def rows(ref, *lead):
            v = ref[lead] if lead else ref[...]
            return v[0].reshape(tb, v.shape[-1]) if to_natural else v

        if stacked:
            kb = dy_ref.shape[-1]
            dh = _dot_nt(rows(dy_ref, 0), w_ref[:, 0:kb])
            for s in range(1, dy_ref.shape[0]):
                dh = dh + _dot_nt(rows(dy_ref, s), w_ref[:, s * kb:(s + 1) * kb])
        else:
            dh = _dot_nt(rows(dy_ref), w_ref[...])
        g_v = g_ref[...]
        _, xh, r = _rms_fwd(rows(x_ref), g_v)
        dx = _rms_bwd(dh, xh, r, g_v)
        if has_res:
            dx = dx + rows(dres_ref)
        if to_natural:
            scr = rest.pop(0)
            for cb in range(d // BLK):
                cols = slice(cb * BLK, (cb + 1) * BLK)
                slab = scr.at[cb]
                for res in range(N_RES):
                    slab[pl.ds(res, TI, stride=N_RES), :] = dx[res * TI:(res + 1) * TI, cols]
                dx_ref[:, cols] = slab[...]
        else:
            dx_ref[...] = dx
        if also_bf16:
            dxb_ref[...] = dx.astype(BF16)
        part = jnp.sum(dh * xh, axis=0, keepdims=True)

        @pl.when(i == 0)
        def _():
            gg_ref[...] = part

        @pl.when(i != 0)
        def _():
            gg_ref[...] += part

    tok = pl.BlockSpec((tb, d), lambda i: (i, 0))
    row = pl.BlockSpec((1, d), lambda i: (0, 0))
    if to_natural:
        act = pl.BlockSpec((1, N_RES, TI, d), lambda i: (i // n_i, 0, i % n_i, 0))
        dy_spec = pl.BlockSpec((dy.shape[0], 1, N_RES, TI, dy.shape[2]), lambda i: (0, i // n_i, 0, i % n_i, 0))
        dy, x = dy.reshape(dy.shape[0], t // HALF, N_RES, SEG, dy.shape[2]), _x4(x)
        dres = _x4(dres) if has_res else None
    elif stacked:
        act, dy_spec = tok, pl.BlockSpec((dy.shape[0], tb, dy.shape[2]), lambda i: (0, i, 0))
    else:
        act, dy_spec = tok, pl.BlockSpec((tb, dy.shape[1]), lambda i: (i, 0))
    in_specs = [dy_spec, pl.BlockSpec(w.shape, lambda i: (0, 0)), act, row]
    args = [dy, w, x, g]
    if has_res:
        in_specs.append(act)
        args.append(dres)
    out_specs = [tok] + ([tok] if also_bf16 else []) + [row]
    out_shape = ([jax.ShapeDtypeStruct((t, d), F32)] + ([jax.ShapeDtypeStruct((t, d), BF16)] if also_bf16 else [])
                 + [jax.ShapeDtypeStruct((1, d), F32)])
    res, _ = _pcall(
        body, name=name, grid=(t // tb,), in_specs=in_specs, out_specs=out_specs, out_shape=out_shape,
        scratch_shapes=[pltpu.VMEM((d // BLK, tb, BLK), F32)] if to_natural else [],
        semantics=("arbitrary",), vmem_mb=56, after=after,
    )(*args)
    return res


def _matmul_tn(a, b, *, name, bm, bn, square_a=False, after=()):
    t, m = a.shape
    stacked = b.ndim == 3
    n = b.shape[0] * bn if stacked else b.shape[1]

    def body(a_ref, b_ref, o_ref):
        av = a_ref[...]
        if square_a:
            av = av.astype(F32)
            av = (av * av).astype(BF16)
        o_ref[...] = _dot_tn(av, b_ref[...]).astype(BF16)

    res, _ = _pcall(
        body, name=name, grid=(m // bm, n // bn),
        in_specs=[pl.BlockSpec((t, bm), lambda i, j: (0, i)),
                  pl.BlockSpec((None, t, bn), lambda i, j: (j, 0, 0)) if stacked
                  else pl.BlockSpec((t, bn), lambda i, j: (0, j))],
        out_specs=[pl.BlockSpec((bm, bn), lambda i, j: (i, j))], out_shape=[jax.ShapeDtypeStruct((m, n), BF16)],
        semantics=("parallel", "parallel"), vmem_mb=56, after=after,
    )(a, b)
    return res[0]


N_RES = 16
SEG = 128
HALF = N_RES * SEG
TI = 32
HALO = 16


def _x4(a):
    return a.reshape(a.shape[0] // HALF, N_RES, SEG, a.shape[1])


def _reorder(arrays, name, rider=None):
    t, c = arrays[0].shape
    n = len(arrays)
    n_i = SEG // TI

    def body(*refs):
        scr = refs[-1]
        for i_ref, o_ref in zip(refs[:n], refs[n:2 * n]):
            for cb in range(c // BLK):
                cols = slice(cb * BLK, (cb + 1) * BLK)
                slab = scr.at[cb]
                slab[...] = i_ref[:, cols]
                for r in range(N_RES):
                    o_ref[0, r, :, cols] = slab[pl.ds(r, TI, stride=N_RES), :]

    res, extra = _pcall(
        body, name=name, grid=(t // (TI * N_RES),),
        in_specs=[pl.BlockSpec((TI * N_RES, c), lambda s: (s, 0))] * n,
        out_specs=[pl.BlockSpec((1, N_RES, TI, c), lambda s: (s // n_i, 0, s % n_i, 0))] * n,
        out_shape=[jax.ShapeDtypeStruct((t // HALF, N_RES, SEG, c), F32)] * n,
        scratch_shapes=[pltpu.VMEM((c // BLK, TI * N_RES, BLK), F32)],
        semantics=("parallel",), vmem_mb=32, rider=rider,
    )(*arrays)
    res = [r.reshape(t, c) for r in res]
    return res if rider is None else (res, extra)


_PATTERNS = ((1, 16, 8, SEG), (4, 4, 32, 4 * SEG), (16, 1, SEG, 0))
_FIRST = {1: 1, 4: 4, 16: 16}


def _group_rows(d, g):
    a = g >> 4
    if d == 16:
        base = a * HALF + (g & 15) * SEG
        prev = base - HALF
    elif d == 4:
        c = (g >> 2) & 3
        base = a * HALF + (g & 3) * SEG + c * 32
        prev = jnp.where(c > 0, base - 32, base - HALF + 96)
    else:
        c = g & 15
        base = a * HALF + c * 8
        prev = jnp.where(c > 0, base - 8, base - HALF + 120)
    return base, prev


def _load_rows(ref, base, n, rows, stride):
    parts = [ref[pl.ds(pl.multiple_of(base + j * stride, 8), rows), :] for j in range(n)]
    return parts[0] if n == 1 else jnp.concatenate(parts, axis=0)


def _store_rows(ref, base, val, n, rows, stride, add=False):
    for j in range(n):
        sl = pl.ds(pl.multiple_of(base + j * stride, 8), rows)
        piece = val[j * rows:(j + 1) * rows, :]
        if add:
            ref[sl, :] += piece
        else:
            ref[sl, :] = piece


def _band_bias(n, rows):
    shift = rows.bit_length() - 1
    lq = lax.broadcasted_iota(jnp.int32, (BLK, BLK), 0)
    lk = lax.broadcasted_iota(jnp.int32, (BLK, BLK), 1)
    iq = (lq & (rows - 1)) * n + (lq >> shift)
    ik = (lk & (rows - 1)) * n + (lk >> shift)
    zero = jnp.zeros((BLK, BLK), F32)
    return jnp.where(ik >= iq, zero, NEG_INF), jnp.where(ik <= iq, zero, NEG_INF)


def _set_bias(bias_scr, n, rows):
    prev_b, cur_b = _band_bias(n, rows)
    for half in range(2):
        bias_scr[half * BLK:(half + 1) * BLK, 0:BLK] = prev_b
        bias_scr[half * BLK:(half + 1) * BLK, BLK:2 * BLK] = cur_b


SCALE = 1.0 / math.sqrt(HEAD_DIM)


def _head_consts(value=1.0):
    lane_lo = lax.broadcasted_iota(jnp.int32, (BLK, BLK), 1) < HEAD_DIM
    return lane_lo, [jnp.where(lane_lo, value, 0.0).astype(BF16), jnp.where(lane_lo, 0.0, value).astype(BF16)]


def _stack_heads(v, head_mask):
    return jnp.concatenate([v * head_mask[0], v * head_mask[1]], axis=0)


def _unstack_heads(v2, lane_lo):
    return jnp.where(lane_lo, v2[:BLK], v2[BLK:])


def _rows_per_head(v, lane_lo):
    rolled = pltpu.roll(v, HEAD_DIM, axis=1)
    return jnp.concatenate([jnp.where(lane_lo, v, rolled), jnp.where(lane_lo, rolled, v)], axis=0)


WIDTH = 4


def _loop(lo, hi, fn, width=None):
    if width is None:
        def body(g, carry):
            fn(g)
            return carry

        if hi > lo:
            lax.fori_loop(lo, hi, body, 0)
        return
    while hi > lo:
        trips = (hi - lo) // width
        if trips:
            def body(i, carry, lo=lo, width=width):
                fn([lo + width * i + j for j in range(width)])
                return carry

            lax.fori_loop(0, trips, body, 0)
            lo += trips * width
        width = max(1, width // 2)


def _mix_weights(l1, l2, l3):
    mx = jnp.maximum(jnp.maximum(l1, l2), l3)
    e1, e2, e3 = jnp.exp(l1 - mx), jnp.exp(l2 - mx), jnp.exp(l3 - mx)
    inv = 1.0 / (e1 + e2 + e3)
    return e1 * inv, e2 * inv, e3 * inv


def _attention_fwd(qkv, rider=None):
    t = qkv.shape[0]
    groups = 16 * (t // HALF)

    def body(q_ref, k_ref, v_ref, attn_ref, l1_ref, l2_ref, l3_ref, o_scr, bias_scr):
        lane_lo, q_mask = _head_consts(SCALE)
        l_refs = (l1_ref, l2_ref, l3_ref)
        for p, (d, n, rows, stride) in enumerate(_PATTERNS):
            _set_bias(bias_scr, n, rows)
            o_p, l_p = o_scr.at[p], l_refs[p]

            def block(gs, has_prev):
                at = [_group_rows(d, g) for g in gs]

                def load(ref, b):
                    return _load_rows(ref, b, n, rows, stride).astype(BF16)

                q2 = [_stack_heads(load(q_ref, b), q_mask) for b, _ in at]
                k2 = [load(k_ref, b) for b, _ in at]
                v2 = [load(v_ref, b) for b, _ in at]
                if has_prev:
                    k2 = [jnp.concatenate([load(k_ref, pv), k], axis=0) for (_, pv), k in zip(at, k2)]
                    v2 = [jnp.concatenate([load(v_ref, pv), v], axis=0) for (_, pv), v in zip(at, v2)]
                s = [_dot_nt(q, k) for q, k in zip(q2, k2)]
                s = [x + (bias_scr[...] if has_prev else bias_scr[:, BLK:2 * BLK]) for x in s]
                mx = [jnp.max(x, axis=1, keepdims=True) for x in s]
                e = [jnp.exp(x - m) for x, m in zip(s, mx)]
                den = [jnp.sum(x, axis=1, keepdims=True) for x in e]
                o2 = [_dot(x.astype(BF16), v) * (1.0 / dn) for x, v, dn in zip(e, v2, den)]
                lse2 = [jnp.broadcast_to(m + jnp.log(dn), (2 * BLK, BLK)) for m, dn in zip(mx, den)]
                for (b, _), o, l in zip(at, o2, lse2):
                    _store_rows(o_p, b, _unstack_heads(o, lane_lo), n, rows, stride)
                    _store_rows(l_p, b, _unstack_heads(l, lane_lo), n, rows, stride)

            _loop(0, _FIRST[d], lambda gs: block(gs, False), width=2 * WIDTH)
            _loop(_FIRST[d], groups, lambda gs: block(gs, True), width=2 * WIDTH)

        def mix(i):
            sl = pl.ds(pl.multiple_of(i * 256, 256), 256)
            w = _mix_weights(l1_ref[sl, :], l2_ref[sl, :], l3_ref[sl, :])
            attn_ref[sl, :] = w[0] * o_scr[0, sl, :] + w[1] * o_scr[1, sl, :] + w[2] * o_scr[2, sl, :]

        _loop(0, t // 256, mix)

    def col(c0):
        return pl.BlockSpec((t, BLK), lambda hp: (0, c0 + hp))

    res, extra = _pcall(
        body, name="attention_fwd", grid=(4,), in_specs=[col(0), col(4), col(8)], out_specs=[col(0)] * 4,
        out_shape=[jax.ShapeDtypeStruct((t, 512), F32)] * 4,
        scratch_shapes=[pltpu.VMEM((3, t, BLK), F32), pltpu.VMEM((2 * BLK, 2 * BLK), F32)],
        semantics=("parallel",), vmem_mb=48, rider=rider,
    )(qkv, qkv, qkv)
    return res if rider is None else (res, extra)


def _attention_bwd(qkv, dattn, dsum, lses, dproj):
    t = qkv.shape[0]
    groups = 16 * (t // HALF)

    def body(q_ref, k_ref, v_ref, da_ref, ds_ref, l1_ref, l2_ref, l3_ref, kept_ref, out_ref, acc, bias_scr):
        del kept_ref
        lane_lo, head_mask = _head_consts()
        q_mask = _head_consts(SCALE)[1]
        l_refs = (l1_ref, l2_ref, l3_ref)

        def clear(i):
            sl = pl.ds(pl.multiple_of(i * 512, 512), 512)
            for s in range(3):
                acc[s, sl, :] = jnp.zeros((512, BLK), F32)

        _loop(0, t // 512, clear)
        dq_acc, dk_acc, dv_acc = acc.at[0], acc.at[1], acc.at[2]
        for p, (d, n, rows, stride) in enumerate(_PATTERNS):
            _set_bias(bias_scr, n, rows)

            def block(gs, has_prev):
                at = [_group_rows(d, g) for g in gs]

                def load(ref, b):
                    return _load_rows(ref, b, n, rows, stride)

                def put(ref, b, val):
                    _store_rows(ref, b, val, n, rows, stride, add=True)

                def wide(x):
                    return jnp.concatenate([x, x], axis=1) if has_prev else x

                lse = [[load(ref, b) for ref in l_refs] for b, _ in at]
                w = [_mix_weights(*ls)[p] for ls in lse]
                do2 = [_stack_heads((wg * load(da_ref, b)).astype(BF16), head_mask) for wg, (b, _) in zip(w, at)]
                dl2 = [wide(_rows_per_head(wg * load(ds_ref, b), lane_lo)) for wg, (b, _) in zip(w, at)]
                lse2 = [wide(_rows_per_head(ls[p], lane_lo)) for ls in lse]
                q2 = [_stack_heads(load(q_ref, b).astype(BF16), q_mask) for b, _ in at]
                k2 = [load(k_ref, b).astype(BF16) for b, _ in at]
                v2 = [load(v_ref, b).astype(BF16) for b, _ in at]
                if has_prev:
                    k2 = [jnp.concatenate([load(k_ref, pv).astype(BF16), k], axis=0) for (_, pv), k in zip(at, k2)]
                    v2 = [jnp.concatenate([load(v_ref, pv).astype(BF16), v], axis=0) for (_, pv), v in zip(at, v2)]
                s = [_dot_nt(q, k) for q, k in zip(q2, k2)]
                dp = [_dot_nt(do, v) for do, v in zip(do2, v2)]
                pr = [jnp.exp(x + (bias_scr[...] if has_prev else bias_scr[:, BLK:2 * BLK]) - l)
                      for x, l in zip(s, lse2)]
                ds = [(pg * (x - dl)).astype(BF16) for pg, x, dl in zip(pr, dp, dl2)]
                dq2 = [_dot(x, k) * SCALE for x, k in zip(ds, k2)]
                dk2 = [_dot_tn(x, q) for x, q in zip(ds, q2)]
                dv2 = [_dot_tn(pg.astype(BF16), do) for pg, do in zip(pr, do2)]
                for (b, pv), dq, dk, dv in zip(at, dq2, dk2, dv2):
                    put(dq_acc, b, _unstack_heads(dq, lane_lo))
                    if has_prev:
                        put(dk_acc, pv, dk[:BLK])
                        put(dv_acc, pv, dv[:BLK])
                        put(dk_acc, b, dk[BLK:])
                        put(dv_acc, b, dv[BLK:])
                    else:
                        put(dk_acc, b, dk)
                        put(dv_acc, b, dv)

            _loop(0, _FIRST[d], lambda gs: block(gs, False), width=WIDTH)
            _loop(_FIRST[d], groups, lambda gs: block(gs, True), width=WIDTH)

        def emit(i):
            sl = pl.ds(pl.multiple_of(i * 512, 512), 512)
            for s in range(3):
                out_ref[s, sl, :] = acc[s, sl, :].astype(BF16)

        _loop(0, t // 512, emit)

    def col(c0):
        return pl.BlockSpec((t, BLK), lambda hp: (0, c0 + hp))

    res, _ = _pcall(
        body, name="attention_bwd", grid=(4,),
        in_specs=[col(0), col(4), col(8)] + [col(0)] * 5 + [ANY],
        out_specs=[pl.BlockSpec((3, t, BLK), lambda hp: (0, 0, hp))],
        out_shape=[jax.ShapeDtypeStruct(dproj.shape, BF16)],
        scratch_shapes=[pltpu.VMEM((3, t, BLK), F32), pltpu.VMEM((2 * BLK, 2 * BLK), F32)],
        semantics=("parallel",), vmem_mb=56, aliases={8: 0},
    )(qkv, qkv, qkv, dattn, dsum, *lses, dproj)
    return res[0]


def _order_specs(t):
    n_i = SEG // TI
    nblk = (t // HALF) * n_i
    per = TI // HALO

    def main(c, col=0):
        return pl.BlockSpec((1, N_RES, TI, c), lambda s: (s // n_i, 0, s % n_i, col))

    def before(c, col=0):
        return pl.BlockSpec((1, 2, HALO, c), lambda s: (jnp.maximum(s - 1, 0) // n_i, N_RES // 2 - 1,
                                                        (jnp.maximum(s - 1, 0) % n_i) * per + per - 1, col))

    def after(c, col=0):
        return pl.BlockSpec((1, 2, HALO, c), lambda s: (jnp.minimum(s + 1, nblk - 1) // n_i, 0,
                                                        (jnp.minimum(s + 1, nblk - 1) % n_i) * per, col))

    return nblk, main, before, after


def _shift_in(v, row_in, up):
    rows = v.shape[0]
    idx = lax.broadcasted_iota(jnp.int32, v.shape, 0)
    fill = jnp.broadcast_to(row_in, v.shape)
    if up:
        return jnp.where(idx == rows - 1, fill, pltpu.roll(v, rows - 1, axis=0))
    return jnp.where(idx == 0, fill, pltpu.roll(v, 1, axis=0))


def _taps_behind(u, before):
    s15 = _shift_in(u[N_RES - 1], before[1, HALO - 1:HALO, :], up=False)
    s14 = _shift_in(u[N_RES - 2], before[0, HALO - 1:HALO, :], up=False)
    m1 = jnp.concatenate([s15[None], u[:N_RES - 1]], axis=0)
    m2 = jnp.concatenate([s14[None], s15[None], u[:N_RES - 2]], axis=0)
    return m1, m2


def _taps_ahead(u, after):
    t0 = _shift_in(u[0], after[0, 0:1, :], up=True)
    t1 = _shift_in(u[1], after[1, 0:1, :], up=True)
    p1 = jnp.concatenate([u[1:], t0[None]], axis=0)
    p2 = jnp.concatenate([u[2:], t0[None], t1[None]], axis=0)
    return p1, p2


def _conv_fwd(gates, before, first, cw):
    gates, before = gates.astype(F32), before.astype(F32)
    bg, cg, xc = gates[..., 0:512], gates[..., 512:1024], gates[..., 1024:1536]
    u = cg * xc
    ub = before[..., 512:1024] * before[..., 1024:1536]
    ub = jnp.where(first, jnp.zeros_like(ub), ub)
    m1, m2 = _taps_behind(u, ub)
    conv = m2 * cw[0:1, :] + m1 * cw[1:2, :] + u * cw[2:3, :]
    return bg, u, m1, m2, conv


def _sum_tokens(v):
    return jnp.sum(jnp.sum(v, axis=0), axis=0, keepdims=True)


def _mixer_fwd(x, attn, gates, cw, g_a, g_c, w_out):
    t, d = x.shape
    nblk, main, before, _ = _order_specs(t)
    rows = N_RES * TI

    def body(x_ref, at_ref, gt_ref, gb_ref, cw_ref, ga_ref, gc_ref, wa_ref, wb_ref, x1_ref, mg_ref):
        an = _rms_fwd(at_ref[0], ga_ref[...])[0].astype(BF16)
        bg, _, _, _, conv = _conv_fwd(gt_ref[0], gb_ref[0], pl.program_id(0) == 0, cw_ref[...])
        cn = _rms_fwd(bg * conv, gc_ref[...])[0].astype(BF16)
        mg_ref[0, :, :, 0:512] = an
        mg_ref[0, :, :, 512:1024] = cn
        y = _dot(an.reshape(rows, 512), wa_ref[...]) + _dot(cn.reshape(rows, 512), wb_ref[...])
        x1_ref[0] = x_ref[0] + y.reshape(N_RES, TI, d)

    const = lambda r, c, i0=0: pl.BlockSpec((r, c), lambda s: (i0, 0))
    x1, merged = pl.pallas_call(
        body, name="mixer_fwd", grid=(nblk,),
        in_specs=[main(d), main(512), main(1536), before(1536), const(3, 512), const(1, 512), const(1, 512),
                  const(512, d), const(512, d, 1)],
        out_specs=[main(d), main(d)],
        out_shape=[jax.ShapeDtypeStruct(_x4(x).shape, F32), jax.ShapeDtypeStruct(_x4(x).shape, BF16)],
        compiler_params=_params(("parallel",), 48),
    )(_x4(x), _x4(attn), _x4(gates), _x4(gates), cw, g_a, g_c, w_out, w_out)
    return x1.reshape(t, d), merged.reshape(t, d)


def _mixer_bwd(dx1, attn, gates, cw, g_a, g_c, w_out, head_sum, after=()):
    t, d = dx1.shape
    nblk, main, before, _ = _order_specs(t)
    rows = N_RES * TI

    def body(dx_ref, at_ref, gt_ref, gb_ref, cw_ref, ga_ref, gc_ref, wa_ref, wb_ref, hs_ref,
             da_ref, dsum_ref, dy_ref, gga_ref, ggc_ref):
        s = pl.program_id(0)
        dxb = dx_ref[0].reshape(rows, d).astype(BF16)
        dma = _dot_nt(dxb, wa_ref[...]).reshape(N_RES, TI, 512)
        dmc = _dot_nt(dxb, wb_ref[...]).reshape(N_RES, TI, 512)
        attn_v, g_av = at_ref[0], ga_ref[...]
        _, ah, ra = _rms_fwd(attn_v, g_av)
        dattn = _rms_bwd(dma, ah, ra, g_av)
        da_ref[0] = dattn
        z = (dattn * attn_v).reshape(rows, 512)
        hs = hs_ref[...]
        z1 = z.astype(BF16)
        z2 = (z - z1.astype(F32)).astype(BF16)
        dsum_ref[0] = (_dot(z1, hs) + _dot(z2, hs)).reshape(N_RES, TI, 512)
        bg, _, _, _, conv = _conv_fwd(gt_ref[0], gb_ref[0], s == 0, cw_ref[...])
        g_cv = gc_ref[...]
        _, yh, rc = _rms_fwd(bg * conv, g_cv)
        dy_ref[0] = _rms_bwd(dmc, yh, rc, g_cv)
        pa, pc = _sum_tokens(dma * ah), _sum_tokens(dmc * yh)

        @pl.when(s == 0)
        def _():
            gga_ref[...] = pa
            ggc_ref[...] = pc

        @pl.when(s != 0)
        def _():
            gga_ref[...] += pa
            ggc_ref[...] += pc

    const = lambda r, c, i0=0: pl.BlockSpec((r, c), lambda s: (i0, 0))
    shape4 = _x4(attn).shape
    res, _ = _pcall(
        body, name="mixer_bwd", grid=(nblk,),
        in_specs=[main(d), main(512), main(1536), before(1536), const(3, 512), const(1, 512), const(1, 512),
                  const(512, d), const(512, d, 1), const(512, 512)],
        out_specs=[main(512)] * 3 + [const(1, 512), const(1, 512)],
        out_shape=[jax.ShapeDtypeStruct(shape4, F32)] * 3 + [jax.ShapeDtypeStruct((1, 512), F32)] * 2,
        semantics=("arbitrary",), vmem_mb=48, after=after,
    )(_x4(dx1), _x4(attn), _x4(gates), _x4(gates), cw, g_a, g_c, w_out, w_out, head_sum)
    return [r.reshape(t, 512) for r in res[:3]] + res[3:]


def _conv_bwd(dy, gates, cw):
    t = dy.shape[0]
    nblk, main, before, after = _order_specs(t)
    n_i = SEG // TI

    def body(dy_ref, dya_ref, gt_ref, gb_ref, ga_ref, cw_ref, dp_ref, gcw_ref):
        s = pl.program_id(0)
        cw_v, gates_v = cw_ref[...], gt_ref[0]
        bg, u, m1, m2, conv = _conv_fwd(gates_v, gb_ref[0], s == 0, cw_v)
        dy_v = dy_ref[0]
        dconv = dy_v * bg
        dca = dya_ref[0] * ga_ref[0][..., 0:512].astype(F32)
        dca = jnp.where(s == nblk - 1, jnp.zeros_like(dca), dca)
        p1, p2 = _taps_ahead(dconv, dca)
        du = dconv * cw_v[2:3, :] + p1 * cw_v[1:2, :] + p2 * cw_v[0:1, :]
        dp_ref[0, 0] = (dy_v * conv).astype(BF16)
        dp_ref[1, 0] = (du * gates_v[..., 1024:1536].astype(F32)).astype(BF16)
        dp_ref[2, 0] = (du * gates_v[..., 512:1024].astype(F32)).astype(BF16)
        parts = [_sum_tokens(dconv * m2), _sum_tokens(dconv * m1), _sum_tokens(dconv * u)]

        @pl.when(s == 0)
        def _():
            gcw_ref[...] = jnp.zeros_like(gcw_ref)

        for tap in range(3):
            gcw_ref[tap:tap + 1, :] += parts[tap]

    (dproj, gcw), _ = _pcall(
        body, name="conv_bwd", grid=(nblk,),
        in_specs=[main(512), after(512), main(1536), before(1536), after(1536),
                  pl.BlockSpec((3, 512), lambda s: (0, 0))],
        out_specs=[pl.BlockSpec((3, 1, N_RES, TI, 512), lambda s: (1, s // n_i, 0, s % n_i, 0)),
                   pl.BlockSpec((8, 512), lambda s: (0, 0))],
        out_shape=[jax.ShapeDtypeStruct((6, t // HALF, N_RES, SEG, 512), BF16), jax.ShapeDtypeStruct((8, 512), F32)],
        semantics=("arbitrary",), vmem_mb=40,
    )(_x4(dy), _x4(dy), _x4(gates), _x4(gates), _x4(gates), cw)
    return dproj.reshape(6, t, 512), gcw


def _xattn_fwd(x1, g, w_q, kv, w_o, *, tb):
    t, d = x1.shape
    hd = d // N_MEM_HEADS
    m = kv.shape[0]

    def body(x_ref, g_ref, wq_ref, k_ref, v_ref, wo_ref, x2_ref, h_ref, q_ref, o_ref):
        xv = x_ref[...]
        h = _rms_fwd(xv, g_ref[...])[0].astype(BF16)
        h_ref[...] = h
        q = _dot(h, wq_ref[...]).astype(BF16)
        q_ref[...] = q
        for hh in range(N_MEM_HEADS):
            sl = slice(hh * hd, (hh + 1) * hd)
            s = _dot_nt(q[:, sl], k_ref[:, sl]) * (1.0 / 16.0)
            e = jnp.exp(s - jnp.max(s, axis=1, keepdims=True))
            p = e / jnp.sum(e, axis=1, keepdims=True)
            o_ref[:, sl] = _dot(p.astype(BF16), v_ref[:, sl]).astype(BF16)
        x2_ref[...] = xv + _dot(o_ref[...], wo_ref[...])

    tok = pl.BlockSpec((tb, d), lambda i: (i, 0))
    full = pl.BlockSpec((d, d), lambda i: (0, 0))
    return pl.pallas_call(
        body, name="xattn_fwd", grid=(t // tb,),
        in_specs=[tok, pl.BlockSpec((1, d), lambda i: (0, 0)), full,
                  pl.BlockSpec((m, d), lambda i: (0, 0)), pl.BlockSpec((m, d), lambda i: (0, 1)), full],
        out_specs=[tok] * 4,
        out_shape=[jax.ShapeDtypeStruct((t, d), F32)] + [jax.ShapeDtypeStruct((t, d), BF16)] * 3,
        compiler_params=_params(("parallel",), 48),
    )(x1, g, w_q, kv, kv, w_o)


def _xattn_bwd(dx2, x1, g, q, w_q, kv, w_o, *, tb, after=()):
    t, d = x1.shape
    hd = d // N_MEM_HEADS
    m = kv.shape[0]

    def body(dx2_ref, x_ref, g_ref, q_ref, wq_ref, k_ref, v_ref, wo_ref,
             dx1_ref, dx1b_ref, dq_ref, dk_ref, dv_ref, gg_ref):
        i = pl.program_id(0)

        @pl.when(i == 0)
        def _():
            dk_ref[...] = jnp.zeros_like(dk_ref)
            dv_ref[...] = jnp.zeros_like(dv_ref)

        dx2 = dx2_ref[...]
        do = _dot_nt(dx2.astype(BF16), wo_ref[...]).astype(BF16)
        for hh in range(N_MEM_HEADS):
            sl = slice(hh * hd, (hh + 1) * hd)
            qh, kh, vh, doh = q_ref[:, sl], k_ref[:, sl], v_ref[:, sl], do[:, sl]
            s = _dot_nt(qh, kh) * (1.0 / 16.0)
            e = jnp.exp(s - jnp.max(s, axis=1, keepdims=True))
            p = e / jnp.sum(e, axis=1, keepdims=True)
            dp = _dot_nt(doh, vh)
            ds = (p * (dp - jnp.sum(dp * p, axis=1, keepdims=True)) * (1.0 / 16.0)).astype(BF16)
            dq_ref[:, sl] = _dot(ds, kh).astype(BF16)
            dk_ref[:, sl] += _dot_tn(ds, qh)
            dv_ref[:, sl] += _dot_tn(p.astype(BF16), doh)
        dh = _dot_nt(dq_ref[...], wq_ref[...])
        g_v = g_ref[...]
        _, xh, r = _rms_fwd(x_ref[...], g_v)
        dx1 = dx2 + _rms_bwd(dh, xh, r, g_v)
        dx1_ref[...] = dx1
        dx1b_ref[...] = dx1.astype(BF16)
        part = jnp.sum(dh * xh, axis=0, keepdims=True)

        @pl.when(i == 0)
        def _():
            gg_ref[...] = part

        @pl.when(i != 0)
        def _():
            gg_ref[...] += part

    tok = pl.BlockSpec((tb, d), lambda i: (i, 0))
    full = pl.BlockSpec((d, d), lambda i: (0, 0))
    acc = pl.BlockSpec((m, d), lambda i: (0, 0))
    res, _ = _pcall(
        body, name="xattn_bwd", grid=(t // tb,),
        in_specs=[tok, tok, pl.BlockSpec((1, d), lambda i: (0, 0)), tok, full,
                  pl.BlockSpec((m, d), lambda i: (0, 0)), pl.BlockSpec((m, d), lambda i: (0, 1)), full],
        out_specs=[tok, tok, tok, acc, acc, pl.BlockSpec((1, d), lambda i: (0, 0))],
        out_shape=[jax.ShapeDtypeStruct((t, d), F32), jax.ShapeDtypeStruct((t, d), BF16),
                   jax.ShapeDtypeStruct((t, d), BF16),
                   jax.ShapeDtypeStruct((m, d), F32), jax.ShapeDtypeStruct((m, d), F32),
                   jax.ShapeDtypeStruct((1, d), F32)],
        semantics=("arbitrary",), vmem_mb=48, after=after,
    )(dx2, x1, g, q, w_q, kv, kv, w_o)
    return res


def _mlp_down_loss(a, w_down, x2, tgt, g, *, tb):
    t, d = x2.shape
    f = a.shape[1]

    def body(a_ref, w_ref, x_ref, t_ref, g_ref, dx_ref, dxb_ref, loss_ref, gg_ref):
        i = pl.program_id(0)
        av = a_ref[...].astype(F32)
        x3 = x_ref[...] + _dot((av * av).astype(BF16), w_ref[...])
        g_v = g_ref[...]
        out, xh, r = _rms_fwd(x3, g_v)
        err = out - t_ref[...]
        dout = err * (1.0 / d)
        dx = _rms_bwd(dout, xh, r, g_v)
        dx_ref[...] = dx
        dxb_ref[...] = dx.astype(BF16)
        part = jnp.sum(dout * xh, axis=0, keepdims=True)
        lpart = 0.5 * jnp.sum(jnp.mean(err * err, axis=-1, keepdims=True), axis=0, keepdims=True)
        lpart = jnp.broadcast_to(lpart, loss_ref.shape)

        @pl.when(i == 0)
        def _():
            gg_ref[...] = part
            loss_ref[...] = lpart

        @pl.when(i != 0)
        def _():
            gg_ref[...] += part
            loss_ref[...] += lpart

    tok = pl.BlockSpec((tb, d), lambda i: (i, 0))
    return pl.pallas_call(
        body, name="mlp_down_loss", grid=(t // tb,),
        in_specs=[pl.BlockSpec((tb, f), lambda i: (i, 0)), pl.BlockSpec((f, d), lambda i: (0, 0)), tok, tok,
                  pl.BlockSpec((1, d), lambda i: (0, 0))],
        out_specs=[tok, tok, pl.BlockSpec((8, 128), lambda i: (0, 0)), pl.BlockSpec((1, d), lambda i: (0, 0))],
        out_shape=[jax.ShapeDtypeStruct((t, d), F32), jax.ShapeDtypeStruct((t, d), BF16),
                   jax.ShapeDtypeStruct((8, 128), F32), jax.ShapeDtypeStruct((1, d), F32)],
        compiler_params=_params(("arbitrary",), 56),
    )(a, w_down, x2, tgt, g)


def _mlp_dpre(dx3, w_down, a, *, tb, bn):
    t, d = dx3.shape
    f = a.shape[1]

    def body(dx_ref, w_ref, a_ref, o_ref):
        o_ref[...] = (2.0 * a_ref[...].astype(F32) * _dot_nt(dx_ref[...], w_ref[...])).astype(BF16)

    return pl.pallas_call(
        body, name="mlp_dpre", grid=(t // tb, f // bn),
        in_specs=[pl.BlockSpec((tb, d), lambda i, j: (i, 0)), pl.BlockSpec((bn, d), lambda i, j: (j, 0)),
                  pl.BlockSpec((tb, bn), lambda i, j: (i, j))],
        out_specs=pl.BlockSpec((tb, bn), lambda i, j: (i, j)),
        out_shape=jax.ShapeDtypeStruct((t, f), BF16),
        compiler_params=_params(("parallel", "arbitrary"), 48),
    )(dx3, w_down, a)


def _adamw(gsum, w, m, v):
    m_new = ADAM_B1 * m + (1.0 - ADAM_B1) * gsum
    v_new = ADAM_B2 * v + (1.0 - ADAM_B2) * (gsum * gsum)
    m_hat = m_new / (1.0 - ADAM_B1 ** ADAM_STEP)
    v_hat = v_new / (1.0 - ADAM_B2 ** ADAM_STEP)
    delta = -ADAM_LR * (m_hat / (jnp.sqrt(v_hat) + ADAM_EPS) + ADAM_WD * w)
    return delta, m_new, v_new


def _sum_adamw(parts, w, m, v, *, name, tr):
    r, c = w.shape

    def body(p_ref, w_ref, m_ref, v_ref, g_ref, d_ref, mo_ref, vo_ref):
        g = p_ref[0].astype(F32)
        for k in range(1, N_DEV):
            g = g + p_ref[k].astype(F32)
        g_ref[...] = g
        d_ref[...], mo_ref[...], vo_ref[...] = _adamw(g, w_ref[...], m_ref[...], v_ref[...])

    blk = pl.BlockSpec((tr, c), lambda i: (i, 0))
    return pl.pallas_call(
        body, name=name, grid=(r // tr,),
        in_specs=[pl.BlockSpec((N_DEV, tr, c), lambda i: (0, i, 0)), blk, blk, blk],
        out_specs=[blk] * 4, out_shape=[jax.ShapeDtypeStruct((r, c), F32)] * 4,
        compiler_params=_params(("parallel",), 40),
    )(*[pltpu.with_memory_space_constraint(a, pltpu.HBM) for a in (parts, w, m, v)])


def _sum_small(parts):
    _, r, c = parts.shape

    def body(p_ref, o_ref):
        s = p_ref[0]
        for k in range(1, N_DEV):
            s = s + p_ref[k]
        o_ref[...] = s

    return pl.pallas_call(body, name="sum_small", out_shape=jax.ShapeDtypeStruct((r, c), F32))(parts)


def _adamw_small(g, w, m, v):
    def body(g_ref, w_ref, m_ref, v_ref, d_ref, mo_ref, vo_ref):
        d_ref[...], mo_ref[...], vo_ref[...] = _adamw(g_ref[...], w_ref[...], m_ref[...], v_ref[...])

    return pl.pallas_call(body, name="adamw_small", out_shape=[jax.ShapeDtypeStruct(g.shape, F32)] * 3)(g, w, m, v)


def _head_sum_matrix():
    r = lax.broadcasted_iota(jnp.int32, (512, 512), 0) // HEAD_DIM
    c = lax.broadcasted_iota(jnp.int32, (512, 512), 1) // HEAD_DIM
    return (r == c).astype(BF16)


_SHARD_AXIS = dict(w_in=1, w_out=0, w_q=0, w_kv=1, w_o=0, w_up=1, w_down=0, conv_w=None, small=None)


class _Weights:
    def __init__(self, full, shards=None):
        self.full = dict(full)
        self.shards = shards

    def rider(self, names, late=False):
        if self.shards is None:
            return None
        return _Gather([self.shards[n] for n in names], [_SHARD_AXIS[n] for n in names], late)

    def arrived(self, names, gathered):
        if gathered is not None:
            for n, g in zip(names, gathered):
                self.full[n] = g.transpose(1, 0, 2).reshape(g.shape[1], -1) if n == "conv_w" else g

    def __getitem__(self, name):
        return self.full[name]


class _Grads:
    def __init__(self, distributed):
        self.distributed = distributed
        self.local = {}
        self.pending = {}

    def add(self, name, g):
        self.local[name] = g

    def send(self, *names):
        if not self.distributed:
            return []
        rider = _Exchange([self.local[n] for n in names], [_SHARD_AXIS[n] for n in names])
        started = _exchange_start(rider, "send_" + "_".join(names))
        self.pending[names[0]] = (names, rider, started)
        return [started[3]]

    def wait(self, first_name, after):
        names, rider, started = self.pending.pop(first_name)
        return _exchange_wait(rider, started, after, "wait_" + "_".join(names))


def _ride(fn, *args, rider=None, **kw):
    if rider is None:
        return fn(*args, **kw), None
    return fn(*args, rider=rider, **kw)


def _local_step(x, mem, tgt, gains, weights, grads):
    names = ["w_in", "conv_w"]
    (x, tgt), got = _ride(_reorder, [x, tgt], "reorder_in", rider=weights.rider(names, late=True))
    weights.arrived(names, got)
    w_in, cw = weights["w_in"], weights["conv_w"]

    names = ["w_out", "w_kv"]
    (qkv, gates, h1), got = _ride(_proj, x, gains["g_mix"], w_in, tb=1024, rider=weights.rider(names))
    weights.arrived(names, got)
    names = ["w_q", "w_o", "w_up"]
    (attn, *lses), got = _ride(_attention_fwd, qkv, rider=weights.rider(names))
    weights.arrived(names, got)
    x1, merged = _mixer_fwd(x, attn, gates, cw, gains["g_attn_out"], gains["g_conv_out"], weights["w_out"])
    kv, mem_n = _norm_matmul(mem, gains["g_mem"], weights["w_kv"], name="mem_kv", out_dtype=BF16, tb=mem.shape[0],
                             bn=1024, save_h=True)
    x2, h2, qm, om = _xattn_fwd(x1, gains["g_xattn"], weights["w_q"], kv, weights["w_o"], tb=512)
    w_up = weights["w_up"]
    (a, h3), got = _ride(_norm_matmul, x2, gains["g_mlp"], w_up, name="mlp_up", out_dtype=BF16, tb=1024, bn=2048,
                         relu=True, save_h=True, rider=weights.rider(["w_down"], late=True))
    weights.arrived(["w_down"], got)
    w_down = weights["w_down"]
    dx3, dx3b, loss_blk, gg_final = _mlp_down_loss(a, w_down, x2, tgt, gains["g_final"], tb=512)

    dpre = _mlp_dpre(dx3b, w_down, a, tb=1024, bn=2048)
    grads.add("w_down", _matmul_tn(a, dx3b, name="grad_w_down", bm=512, bn=1024, square_a=True))
    sent = grads.send("w_down")
    grads.add("w_up", _matmul_tn(h3, dpre, name="grad_w_up", bm=1024, bn=1024, after=sent))
    sent = grads.send("w_up")
    dx2, dx2b, gg_mlp = _matmul_nt_normbwd(dpre, w_up, x2, gains["g_mlp"], dx3, name="mlp_dx", tb=512,
                                           also_bf16=True, after=sent)

    grads.add("w_o", _matmul_tn(om, dx2b, name="grad_w_o", bm=512, bn=512))
    dx1, dx1b, dqm, dk, dv, gg_xattn = _xattn_bwd(dx2, x1, gains["g_xattn"], qm, weights["w_q"], kv, weights["w_o"],
                                                  tb=512)
    grads.add("w_q", _matmul_tn(h2, dqm, name="grad_w_q", bm=1024, bn=512))
    dkv = jnp.concatenate([dk, dv], axis=1).astype(BF16)
    grads.add("w_kv", _matmul_tn(mem_n, dkv, name="grad_w_kv", bm=1024, bn=1024))
    _, gg_mem = _matmul_nt_normbwd(dkv, weights["w_kv"], mem, gains["g_mem"], None, name="mem_dx", tb=mem.shape[0])

    grads.add("w_out", _matmul_tn(merged, dx1b, name="grad_w_out", bm=1024, bn=512))
    sent = grads.send("w_o", "w_q", "w_kv", "w_out")
    dattn, dsum, dy, gg_attn, gg_conv = _mixer_bwd(dx1, attn, gates, cw, gains["g_attn_out"], gains["g_conv_out"],
                                                   weights["w_out"], _head_sum_matrix(), after=sent)
    dproj, gcw = _conv_bwd(dy, gates, cw)
    dproj = _attention_bwd(qkv, dattn, dsum, lses, dproj)
    grads.add("w_in", _matmul_tn(h1, dproj, name="grad_w_in", bm=1024, bn=512))
    sent = grads.send("w_in")
    grad_x, gg_mix = _matmul_nt_normbwd(dproj, w_in, x, gains["g_mix"], dx1, name="mixer_dx", tb=512,
                                        to_natural=True, after=sent)

    def part(v):
        return jnp.pad(v, ((0, SMALL_PART - v.shape[0]), (0, 1024 - v.shape[1])))

    parts = [gg_mix, gg_xattn, gg_mem, gg_mlp, gg_final, jnp.concatenate([gg_attn, gg_conv], axis=1), gcw, loss_blk]
    grads.add("small", jnp.concatenate([part(v) for v in parts], axis=0))
    return grad_x


SMALL_PART = 8
_BIG = ("w_in", "w_out", "w_q", "w_kv", "w_o", "w_up", "w_down")
_GAIN_ROWS = ("g_mix", "g_xattn", "g_mem", "g_mlp", "g_final")


def _pack_small(vals, conv):
    rows = [vals[k].reshape(1, -1) for k in _GAIN_ROWS]
    rows.append(jnp.concatenate([vals["g_attn_out"].reshape(1, -1), vals["g_conv_out"].reshape(1, -1)], axis=1))
    flat = conv.reshape(1, -1)
    rows.append(jnp.pad(flat, ((0, 0), (0, 1024 - flat.shape[1]))))
    rows.append(jnp.zeros((1, 1024), F32))
    return jnp.concatenate(rows, axis=0)


def kernel(x, mem, g_mix, w_in, conv_w, g_attn_out, g_conv_out, w_out, g_xattn, g_mem, w_q_mem, w_kv_mem, w_o_mem, g_mlp, w_up, w_down, g_final, loss_target, m_g_mix, m_w_in, m_conv_w, m_g_attn_out, m_g_conv_out, m_w_out, m_g_xattn, m_g_mem, m_w_q_mem, m_w_kv_mem, m_w_o_mem, m_g_mlp, m_w_up, m_w_down, m_g_final, v_g_mix, v_w_in, v_conv_w, v_g_attn_out, v_g_conv_out, v_w_out, v_g_xattn, v_g_mem, v_w_q_mem, v_w_kv_mem, v_w_o_mem, v_g_mlp, v_w_up, v_w_down, v_g_final):
    d = x.shape[-1]
    me = 4 * lax.axis_index("x") + 2 * lax.axis_index("y") + lax.axis_index("c")
    w_shards = dict(w_in=w_in, w_out=w_out, w_q=w_q_mem, w_kv=w_kv_mem, w_o=w_o_mem, w_up=w_up, w_down=w_down)
    m_shards = dict(w_in=m_w_in, w_out=m_w_out, w_q=m_w_q_mem, w_kv=m_w_kv_mem, w_o=m_w_o_mem, w_up=m_w_up,
                    w_down=m_w_down)
    v_shards = dict(w_in=v_w_in, w_out=v_w_out, w_q=v_w_q_mem, w_kv=v_w_kv_mem, w_o=v_w_o_mem, w_up=v_w_up,
                    w_down=v_w_down)
    gains = dict(g_mix=g_mix, g_attn_out=g_attn_out, g_conv_out=g_conv_out, g_xattn=g_xattn, g_mem=g_mem,
                 g_mlp=g_mlp, g_final=g_final)
    gains2 = {k: v.reshape(1, -1) for k, v in gains.items()}

    shards = {k: w_shards[k].astype(BF16) for k in _BIG}
    shards["conv_w"] = conv_w
    grads = _Grads(distributed=True)
    grad_x = _local_step(x[0], mem[0], loss_target[0], gains2, _Weights({}, shards), grads)

    after = grads.send("small")
    outs = {}
    tiles = dict(w_in=256, w_out=128, w_q=128, w_kv=256, w_o=128, w_up=256, w_down=256)
    for group in (("w_down",), ("w_up",), ("w_o", "w_q", "w_kv", "w_out"), ("w_in",)):
        for k, received in zip(group, grads.wait(group[0], after)):
            outs[k] = _sum_adamw(received, w_shards[k], m_shards[k], v_shards[k], name=f"adamw_{k}", tr=tiles[k])
            after = [outs[k][0]]
    small_received, = grads.wait("small", after)

    ssum = _sum_small(small_received)
    row = lambda i: ssum[SMALL_PART * i]
    loss = ssum[SMALL_PART * 7, 0]
    g_small = {k: row(i) for i, k in enumerate(_GAIN_ROWS)}
    g_small["g_attn_out"] = row(5)[0:512]
    g_small["g_conv_out"] = row(5)[512:1024]
    taps = ssum[SMALL_PART * 6:SMALL_PART * 6 + 3, 0:512]
    g_conv = lax.dynamic_slice_in_dim(taps, me * 64, 64, axis=1)
    m_small = dict(g_mix=m_g_mix, g_attn_out=m_g_attn_out, g_conv_out=m_g_conv_out, g_xattn=m_g_xattn,
                   g_mem=m_g_mem, g_mlp=m_g_mlp, g_final=m_g_final)
    v_small = dict(g_mix=v_g_mix, g_attn_out=v_g_attn_out, g_conv_out=v_g_conv_out, g_xattn=v_g_xattn,
                   g_mem=v_g_mem, g_mlp=v_g_mlp, g_final=v_g_final)
    packed = [_pack_small(g_small, g_conv), _pack_small(gains, conv_w), _pack_small(m_small, m_conv_w),
              _pack_small(v_small, v_conv_w)]
    upd = _adamw_small(*packed)

    def unpack(p):
        res = {k: p[i] for i, k in enumerate(_GAIN_ROWS)}
        res["g_attn_out"] = p[5, 0:512]
        res["g_conv_out"] = p[5, 512:1024]
        res["conv_w"] = p[6, 0:192].reshape(3, 64)
        return res

    g_small["conv_w"] = g_conv
    small_out = [g_small] + [unpack(p) for p in upd]
    names = {"g_mix": "g_mix", "w_in": "w_in", "conv_w": "conv_w", "g_attn_out": "g_attn_out",
             "g_conv_out": "g_conv_out", "w_out": "w_out", "g_xattn": "g_xattn", "g_mem": "g_mem",
             "w_q_mem": "w_q", "w_kv_mem": "w_kv", "w_o_mem": "w_o", "g_mlp": "g_mlp", "w_up": "w_up",
             "w_down": "w_down", "g_final": "g_final"}
    result = [loss, grad_x[None]]
    for which in range(4):
        for key in names.values():
            result.append(outs[key][which] if key in outs else small_out[which][key])
    return tuple(result)
```

```python
import math

import jax
import jax.numpy as jnp
from jax import lax
from jax.experimental import pallas as pl
from jax.experimental.pallas import tpu as pltpu

F32 = jnp.float32
BF16 = jnp.bfloat16
NORM_EPS = 1e-6
NEG_INF = -1e30
N_DEV = 8
BLK = 128
HEAD_DIM = 64
N_MEM_HEADS = 4
ADAM_LR = 0.001
ADAM_B1 = 0.9
ADAM_B2 = 0.999
ADAM_EPS = 1e-08
ADAM_WD = 0.01
ADAM_STEP = 10
MESH = pl.DeviceIdType.MESH
ANY = pl.BlockSpec(memory_space=pl.ANY)


def _dot(a, b):
    return jnp.dot(a, b, preferred_element_type=F32)


def _dot_nt(a, b):
    return lax.dot_general(a, b, (((1,), (1,)), ((), ())), preferred_element_type=F32)


def _dot_tn(a, b):
    return lax.dot_general(a, b, (((0,), (0,)), ((), ())), preferred_element_type=F32)


def _params(semantics, vmem_mb):
    return pltpu.CompilerParams(dimension_semantics=semantics, vmem_limit_bytes=vmem_mb << 20)


def _rms_fwd(x, g):
    r = lax.rsqrt(jnp.mean(x * x, axis=-1, keepdims=True) + NORM_EPS)
    xh = x * r
    return xh * g, xh, r


def _rms_bwd(dy, xh, r, g):
    gy = dy * g
    return r * (gy - xh * jnp.mean(xh * gy, axis=-1, keepdims=True))


def _position():
    x, y, c = lax.axis_index("x"), lax.axis_index("y"), lax.axis_index("c")
    return x, y, c


def _block_of(ref, j, axis, shard_shape):
    r, c = shard_shape
    if axis is None:
        return ref.at[j]
    if axis == 0:
        return ref.at[pl.ds(j * r, r), :]
    return ref.at[:, pl.ds(j * c, c)]


class _Gather:
    has_mid = True
    alias_pairs = ()

    def __init__(self, shards, axes, late=False):
        self.arrays = list(shards)
        self.axes = list(axes)
        self.late = late
        self.n = len(self.arrays)

    def out_shape(self):
        res = []
        for s, axis in zip(self.arrays, self.axes):
            r, c = s.shape
            shape = (N_DEV, r, c) if axis is None else (N_DEV * r, c) if axis == 0 else (r, N_DEV * c)
            res.append(jax.ShapeDtypeStruct(shape, s.dtype))
        return res

    def scratch(self):
        return [pltpu.SemaphoreType.DMA((self.n, 7)), pltpu.SemaphoreType.DMA((self.n, 7)),
                pltpu.SemaphoreType.DMA((self.n,))]

    def _ctx(self, ins, outs, sems):
        send_sems, recv_sems, local_sems = sems
        x, y, c = _position()
        me, sibling = (x, y, c), (x, y, 1 - c)
        chips = [(1 - x, y), (x, 1 - y), (1 - x, 1 - y)]

        def lin(px, py, pc):
            return 4 * px + 2 * py + pc

        def place(a, block):
            return _block_of(outs[a], lin(*block), self.axes[a], self.arrays[a].shape)

        def copy(a, k, block, to, src=None):
            dst = place(a, block)
            return pltpu.make_async_remote_copy(
                src_ref=dst if src is None else src, dst_ref=dst,
                send_sem=send_sems.at[a, k], recv_sem=recv_sems.at[a, k],
                device_id=to, device_id_type=MESH)

        def mine():
            return [pltpu.make_async_copy(ins[a], place(a, me), local_sems.at[a]) for a in range(self.n)]

        def first():
            res = []
            for a in range(self.n):
                res.append(copy(a, 0, me, sibling, src=ins[a]))
                res += [copy(a, 1 + j, me, (*chip, c), src=ins[a]) for j, chip in enumerate(chips)]
            return res

        return c, me, sibling, chips, copy, mine, first

    def start(self, ins, outs, sems):
        _, _, _, _, _, mine, first = self._ctx(ins, outs, sems)
        for cp in mine() + first():
            cp.start()

    def mid(self, ins, outs, sems):
        c, me, sibling, chips, copy, _, _ = self._ctx(ins, outs, sems)
        for j, chip in enumerate(chips):
            for a in range(self.n):
                copy(a, 1 + j, (*chip, c), me).wait_recv()
                copy(a, 4 + j, (*chip, c), sibling).start()

    def finish(self, ins, outs, sems):
        c, me, sibling, chips, copy, mine, first = self._ctx(ins, outs, sems)
        for a in range(self.n):
            copy(a, 0, sibling, me).wait_recv()
            for j, chip in enumerate(chips):
                copy(a, 4 + j, (*chip, 1 - c), me).wait_recv()
        for cp in first():
            cp.wait_send()
        for j, chip in enumerate(chips):
            for a in range(self.n):
                copy(a, 4 + j, (*chip, c), sibling).wait_send()
        for cp in mine():
            cp.wait()


class _Exchange:
    def __init__(self, parts, axes):
        self.n = len(parts)
        self.axes = list(axes)
        self.arrays = list(parts)

    def _piece(self, a):
        r, c = self.arrays[a].shape
        axis = self.axes[a]
        return (r, c) if axis is None else (r // N_DEV, c) if axis == 0 else (r, c // N_DEV)

    def out_shape(self):
        return [jax.ShapeDtypeStruct((N_DEV,) + self._piece(a), self.arrays[a].dtype) for a in range(self.n)]

    def semaphores(self):
        return [pltpu.SemaphoreType.DMA((7 * self.n,)), pltpu.SemaphoreType.DMA((7 * self.n,)),
                pltpu.SemaphoreType.DMA((self.n,))]

    def _ctx(self, ins, outs, sems):
        send_sems, recv_sems, local_sems = sems
        x, y, c = _position()
        me = 4 * x + 2 * y + c

        def src(a, j):
            return ins[a] if self.axes[a] is None else _block_of(ins[a], j, self.axes[a], self._piece(a))

        def dst(a, j):
            return outs[a].at[j]

        def local():
            return [pltpu.make_async_copy(src(a, me), dst(a, me), local_sems.at[a]) for a in range(self.n)]

        def remote(inbound):
            res = []
            for a in range(self.n):
                for k in range(1, N_DEV):
                    peer = (1 - x if k & 4 else x, 1 - y if k & 2 else y, 1 - c if k & 1 else c)
                    plin = 4 * peer[0] + 2 * peer[1] + peer[2]
                    res.append(pltpu.make_async_remote_copy(
                        src_ref=src(a, plin), dst_ref=dst(a, plin if inbound else me),
                        send_sem=send_sems.at[7 * a + k - 1], recv_sem=recv_sems.at[7 * a + k - 1],
                        device_id=peer, device_id_type=MESH))
            return res

        return local, remote

    def start(self, ins, outs, sems):
        local, remote = self._ctx(ins, outs, sems)
        for cp in local() + remote(False):
            cp.start()

    def finish(self, ins, outs, sems):
        local, remote = self._ctx(ins, outs, sems)
        for cp in remote(True):
            cp.wait_recv()
        for cp in remote(False):
            cp.wait_send()
        for cp in local():
            cp.wait()


def _exchange_start(rider, name):
    n = rider.n
    parts = rider.arrays
    lands = [lax.empty(s.shape, s.dtype) for s in rider.out_shape()]
    hbm = pl.BlockSpec(memory_space=pltpu.HBM)
    sem = pl.BlockSpec(memory_space=pltpu.SEMAPHORE)

    def body(*refs):
        ins, sems = refs[:n], refs[2 * n:2 * n + 3]
        outs, token = refs[2 * n + 3 + n:2 * n + 3 + 2 * n], refs[-1]
        rider.start(ins, outs, sems)
        token[...] = jnp.zeros_like(token)

    res = pl.pallas_call(
        body, name=name,
        out_shape=rider.semaphores() + [pltpu.HBM(p.shape, p.dtype) for p in parts]
                  + [pltpu.HBM(z.shape, z.dtype) for z in lands] + [jax.ShapeDtypeStruct((8, 128), F32)],
        in_specs=[hbm] * (2 * n), out_specs=[sem] * 3 + [hbm] * (2 * n) + [pl.BlockSpec(memory_space=pltpu.VMEM)],
        input_output_aliases={i: 3 + i for i in range(2 * n)},
        compiler_params=pltpu.CompilerParams(has_side_effects=pltpu.SideEffectType.DATAFLOW_SIDE_EFFECTING),
    )(*[pltpu.with_memory_space_constraint(a, pltpu.HBM) for a in parts + lands])
    return res[:3], res[3:3 + n], res[3 + n:3 + 2 * n], res[-1]


def _exchange_wait(rider, started, after, name):
    n = rider.n
    sems, parts, lands, _ = started
    hbm = pl.BlockSpec(memory_space=pltpu.HBM)
    sem = pl.BlockSpec(memory_space=pltpu.SEMAPHORE)

    def body(*refs):
        rider.finish(refs[:n], refs[n:2 * n], refs[2 * n:2 * n + 3])

    res = pl.pallas_call(
        body, name=name, out_shape=[pltpu.HBM(a.shape, a.dtype) for a in list(parts) + list(lands)],
        in_specs=[hbm] * (2 * n) + [sem] * 3 + [ANY] * len(after), out_specs=[hbm] * (2 * n),
        input_output_aliases={i: i for i in range(2 * n)},
        compiler_params=pltpu.CompilerParams(has_side_effects=pltpu.SideEffectType.DATAFLOW_SIDE_EFFECTING),
    )(*parts, *lands, *sems, *after)
    return list(res[n:])


def _pcall(body, *, name, grid, in_specs, out_specs, out_shape, scratch_shapes=(), semantics, vmem_mb, rider=None,
           aliases=None, after=()):
    in_specs, out_specs, out_shape = list(in_specs), list(out_specs), list(out_shape)
    scratch_shapes = list(scratch_shapes)
    aliases = dict(aliases or {})
    if rider is None:
        n_in, after = len(in_specs), list(after)

        def plain(*refs):
            body(*refs[:n_in], *refs[n_in + len(after):])

        call = pl.pallas_call(plain if after else body, name=name, grid=grid, in_specs=in_specs + [ANY] * len(after),
                              out_specs=out_specs, out_shape=out_shape, scratch_shapes=scratch_shapes,
                              input_output_aliases=aliases, compiler_params=_params(semantics, vmem_mb))
        return lambda *args: (list(call(*args, *after)), None)
    n_in, n_out, n_scr = len(in_specs), len(out_specs), len(scratch_shapes)
    r_in, r_shapes = len(rider.arrays), rider.out_shape()
    r_out = len(r_shapes)
    aliases.update({n_in + i: n_out + o for i, o in rider.alias_pairs})
    total = math.prod(grid)
    mid_step = total - 1 if rider.has_mid and rider.late else (3 * total) // 4

    def wrapped(*refs):
        bounds = [0, n_in, r_in, n_out, r_out, n_scr]
        for i in range(1, len(bounds)):
            bounds[i] += bounds[i - 1]
        a, ra, o, ro, s = (refs[bounds[i]:bounds[i + 1]] for i in range(5))
        rs = refs[bounds[5]:]
        step = pl.program_id(0)
        for k in range(1, len(grid)):
            step = step * grid[k] + pl.program_id(k)
        pl.when(step == 0)(lambda: rider.start(ra, ro, rs))
        body(*a, *o, *s)
        if rider.has_mid:
            pl.when(step == mid_step)(lambda: rider.mid(ra, ro, rs))
        pl.when(step == total - 1)(lambda: rider.finish(ra, ro, rs))

    call = pl.pallas_call(
        wrapped, name=name, grid=grid, in_specs=in_specs + [ANY] * r_in, out_specs=out_specs + [ANY] * r_out,
        out_shape=out_shape + r_shapes, scratch_shapes=scratch_shapes + rider.scratch(),
        input_output_aliases=aliases, compiler_params=_params(("arbitrary",) * len(grid), vmem_mb))

    def run(*args):
        res = call(*args, *rider.arrays)
        return list(res[:n_out]), list(res[n_out:])

    return run


def _norm_matmul(x, g, w, *, name, out_dtype, tb, bn, relu=False, save_h=False, rider=None):
    t, d = x.shape
    n = w.shape[1]

    def body(x_ref, g_ref, w_ref, o_ref, *rest):
        h_scr = rest[-1]

        @pl.when(pl.program_id(1) == 0)
        def _():
            h = _rms_fwd(x_ref[...], g_ref[...])[0].astype(BF16)
            h_scr[...] = h
            if save_h:
                rest[0][...] = h

        acc = _dot(h_scr[...], w_ref[...])
        if relu:
            acc = jnp.maximum(acc, 0.0)
        o_ref[...] = acc.astype(out_dtype)

    out_shape = [jax.ShapeDtypeStruct((t, n), out_dtype)]
    out_specs = [pl.BlockSpec((tb, bn), lambda i, j: (i, j))]
    if save_h:
        out_shape.append(jax.ShapeDtypeStruct((t, d), BF16))
        out_specs.append(pl.BlockSpec((tb, d), lambda i, j: (i, 0)))
    res, extra = _pcall(
        body, name=name, grid=(t // tb, n // bn),
        in_specs=[pl.BlockSpec((tb, d), lambda i, j: (i, 0)),
                  pl.BlockSpec((1, d), lambda i, j: (0, 0)),
                  pl.BlockSpec((d, bn), lambda i, j: (0, j))],
        out_specs=out_specs, out_shape=out_shape,
        scratch_shapes=[pltpu.VMEM((tb, d), BF16)],
        semantics=("parallel", "arbitrary"), vmem_mb=48, rider=rider,
    )(x, g, w)
    res = res if save_h else res[0]
    return res if rider is None else (res, extra)


def _proj(x, g, w, *, tb, rider=None):
    t, d = x.shape
    half = w.shape[1] // 2

    def body(x_ref, g_ref, w_ref, qkv_ref, gates_ref, h_ref, h_scr):
        j = pl.program_id(1)

        @pl.when(j == 0)
        def _():
            h = _rms_fwd(x_ref[...], g_ref[...])[0].astype(BF16)
            h_scr[...] = h
            h_ref[...] = h

        acc = _dot(h_scr[...], w_ref[...])

        @pl.when(j == 0)
        def _():
            qkv_ref[...] = acc

        @pl.when(j == 1)
        def _():
            gates_ref[...] = acc.astype(BF16)

    tok = lambda c: pl.BlockSpec((tb, c), lambda i, j: (i, 0))
    res, extra = _pcall(
        body, name="proj", grid=(t // tb, 2),
        in_specs=[tok(d), pl.BlockSpec((1, d), lambda i, j: (0, 0)), pl.BlockSpec((d, half), lambda i, j: (0, j))],
        out_specs=[tok(half), tok(half), tok(d)],
        out_shape=[jax.ShapeDtypeStruct((t, half), F32), jax.ShapeDtypeStruct((t, half), BF16),
                   jax.ShapeDtypeStruct((t, d), BF16)],
        scratch_shapes=[pltpu.VMEM((tb, d), BF16)],
        semantics=("parallel", "arbitrary"), vmem_mb=48, rider=rider,
    )(x, g, w)
    return res if rider is None else (res, extra)


def _matmul_nt_normbwd(dy, w, x, g, dres, *, name, tb, also_bf16=False, to_natural=False, after=()):
    t, d = x.shape
    stacked = dy.ndim == 3
    has_res = dres is not None
    n_i = SEG // TI
    if to_natural:
        tb = N_RES * TI

    def body(dy_ref, w_ref, x_ref, g_ref, *rest):
        rest = list(rest)
        dres_ref = rest.pop(0) if has_res else None
        dx_ref = rest.pop(0)
        dxb_ref = rest.pop(0) if also_bf16 else None
        gg_ref = rest.pop(0)
        i = pl.program_id(0)

        def rows(ref, *lead):
            v = ref[lead] if lead else ref[...]
            return v[0].reshape(tb, v.shape[-1]) if to_natural else v

        if stacked:
            kb = dy_ref.shape[-1]
            dh = _dot_nt(rows(dy_ref, 0), w_ref[:, 0:kb])
            for s in range(1, dy_ref.shape[0]):
                dh = dh + _dot_nt(rows(dy_ref, s), w_ref[:, s * kb:(s + 1) * kb])
        else:
            dh = _dot_nt(rows(dy_ref), w_ref[...])
        g_v = g_ref[...]
        _, xh, r = _rms_fwd(rows(x_ref), g_v)
        dx = _rms_bwd(dh, xh, r, g_v)
        if has_res:
            dx = dx + rows(dres_ref)
        if to_natural:
            scr = rest.pop(0)
            for cb in range(d // BLK):
                cols = slice(cb * BLK, (cb + 1) * BLK)
                slab = scr.at[cb]
                for res in range(N_RES):
                    slab[pl.ds(res, TI, stride=N_RES), :] = dx[res * TI:(res + 1) * TI, cols]
                dx_ref[:, cols] = slab[...]
        else:
            dx_ref[...] = dx
        if also_bf16:
            dxb_ref[...] = dx.astype(BF16)
        part = jnp.sum(dh * xh, axis=0, keepdims=True)

        @pl.when(i == 0)
        def _():
            gg_ref[...] = part

        @pl.when(i != 0)
        def _():
            gg_ref[...] += part

    tok = pl.BlockSpec((tb, d), lambda i: (i, 0))
    row = pl.BlockSpec((1, d), lambda i: (0, 0))
    if to_natural:
        act = pl.BlockSpec((1, N_RES, TI, d), lambda i: (i // n_i, 0, i % n_i, 0))
        dy_spec = pl.BlockSpec((dy.shape[0], 1, N_RES, TI, dy.shape[2]), lambda i: (0, i // n_i, 0, i % n_i, 0))
        dy, x = dy.reshape(dy.shape[0], t // HALF, N_RES, SEG, dy.shape[2]), _x4(x)
        dres = _x4(dres) if has_res else None
    elif stacked:
        act, dy_spec = tok, pl.BlockSpec((dy.shape[0], tb, dy.shape[2]), lambda i: (0, i, 0))
    else:
        act, dy_spec = tok, pl.BlockSpec((tb, dy.shape[1]), lambda i: (i, 0))
    in_specs = [dy_spec, pl.BlockSpec(w.shape, lambda i: (0, 0)), act, row]
    args = [dy, w, x, g]
    if has_res:
        in_specs.append(act)
        args.append(dres)
    out_specs = [tok] + ([tok] if also_bf16 else []) + [row]
    out_shape = ([jax.ShapeDtypeStruct((t, d), F32)] + ([jax.ShapeDtypeStruct((t, d), BF16)] if also_bf16 else [])
                 + [jax.ShapeDtypeStruct((1, d), F32)])
    res, _ = _pcall(
        body, name=name, grid=(t // tb,), in_specs=in_specs, out_specs=out_specs, out_shape=out_shape,
        scratch_shapes=[pltpu.VMEM((d // BLK, tb, BLK), F32)] if to_natural else [],
        semantics=("arbitrary",), vmem_mb=56, after=after,
    )(*args)
    return res


def _matmul_tn(a, b, *, name, bm, bn, square_a=False, after=()):
    t, m = a.shape
    stacked = b.ndim == 3
    n = b.shape[0] * bn if stacked else b.shape[1]

    def body(a_ref, b_ref, o_ref):
        av = a_ref[...]
        if square_a:
            av = av.astype(F32)
            av = (av * av).astype(BF16)
        o_ref[...] = _dot_tn(av, b_ref[...]).astype(BF16)

    res, _ = _pcall(
        body, name=name, grid=(m // bm, n // bn),
        in_specs=[pl.BlockSpec((t, bm), lambda i, j: (0, i)),
                  pl.BlockSpec((None, t, bn), lambda i, j: (j, 0, 0)) if stacked
                  else pl.BlockSpec((t, bn), lambda i, j: (0, j))],
        out_specs=[pl.BlockSpec((bm, bn), lambda i, j: (i, j))], out_shape=[jax.ShapeDtypeStruct((m, n), BF16)],
        semantics=("parallel", "parallel"), vmem_mb=56, after=after,
    )(a, b)
    return res[0]


N_RES = 16
SEG = 128
HALF = N_RES * SEG
TI = 32
HALO = 16


def _x4(a):
    return a.reshape(a.shape[0] // HALF, N_RES, SEG, a.shape[1])


def _reorder(arrays, name, rider=None):
    t, c = arrays[0].shape
    n = len(arrays)
    n_i = SEG // TI

    def body(*refs):
        scr = refs[-1]
        for i_ref, o_ref in zip(refs[:n], refs[n:2 * n]):
            for cb in range(c // BLK):
                cols = slice(cb * BLK, (cb + 1) * BLK)
                slab = scr.at[cb]
                slab[...] = i_ref[:, cols]
                for r in range(N_RES):
                    o_ref[0, r, :, cols] = slab[pl.ds(r, TI, stride=N_RES), :]

    res, extra = _pcall(
        body, name=name, grid=(t // (TI * N_RES),),
        in_specs=[pl.BlockSpec((TI * N_RES, c), lambda s: (s, 0))] * n,
        out_specs=[pl.BlockSpec((1, N_RES, TI, c), lambda s: (s // n_i, 0, s % n_i, 0))] * n,
        out_shape=[jax.ShapeDtypeStruct((t // HALF, N_RES, SEG, c), F32)] * n,
        scratch_shapes=[pltpu.VMEM((c // BLK, TI * N_RES, BLK), F32)],
        semantics=("parallel",), vmem_mb=32, rider=rider,
    )(*arrays)
    res = [r.reshape(t, c) for r in res]
    return res if rider is None else (res, extra)


_PATTERNS = ((1, 16, 8, SEG), (4, 4, 32, 4 * SEG), (16, 1, SEG, 0))
_FIRST = {1: 1, 4: 4, 16: 16}


def _group_rows(d, g):
    a = g >> 4
    if d == 16:
        base = a * HALF + (g & 15) * SEG
        prev = base - HALF
    elif d == 4:
        c = (g >> 2) & 3
        base = a * HALF + (g & 3) * SEG + c * 32
        prev = jnp.where(c > 0, base - 32, base - HALF + 96)
    else:
        c = g & 15
        base = a * HALF + c * 8
        prev = jnp.where(c > 0, base - 8, base - HALF + 120)
    return base, prev


def _load_rows(ref, base, n, rows, stride):
    parts = [ref[pl.ds(pl.multiple_of(base + j * stride, 8), rows), :] for j in range(n)]
    return parts[0] if n == 1 else jnp.concatenate(parts, axis=0)


def _store_rows(ref, base, val, n, rows, stride, add=False):
    for j in range(n):
        sl = pl.ds(pl.multiple_of(base + j * stride, 8), rows)
        piece = val[j * rows:(j + 1) * rows, :]
        if add:
            ref[sl, :] += piece
        else:
            ref[sl, :] = piece


def _band_bias(n, rows):
    shift = rows.bit_length() - 1
    lq = lax.broadcasted_iota(jnp.int32, (BLK, BLK), 0)
    lk = lax.broadcasted_iota(jnp.int32, (BLK, BLK), 1)
    iq = (lq & (rows - 1)) * n + (lq >> shift)
    ik = (lk & (rows - 1)) * n + (lk >> shift)
    zero = jnp.zeros((BLK, BLK), F32)
    return jnp.where(ik >= iq, zero, NEG_INF), jnp.where(ik <= iq, zero, NEG_INF)


def _set_bias(bias_scr, n, rows):
    prev_b, cur_b = _band_bias(n, rows)
    for half in range(2):
        bias_scr[half * BLK:(half + 1) * BLK, 0:BLK] = prev_b
        bias_scr[half * BLK:(half + 1) * BLK, BLK:2 * BLK] = cur_b


SCALE = 1.0 / math.sqrt(HEAD_DIM)


def _head_consts(value=1.0):
    lane_lo = lax.broadcasted_iota(jnp.int32, (BLK, BLK), 1) < HEAD_DIM
    return lane_lo, [jnp.where(lane_lo, value, 0.0).astype(BF16), jnp.where(lane_lo, 0.0, value).astype(BF16)]


def _stack_heads(v, head_mask):
    return jnp.concatenate([v * head_mask[0], v * head_mask[1]], axis=0)


def _unstack_heads(v2, lane_lo):
    return jnp.where(lane_lo, v2[:BLK], v2[BLK:])


def _rows_per_head(v, lane_lo):
    rolled = pltpu.roll(v, HEAD_DIM, axis=1)
    return jnp.concatenate([jnp.where(lane_lo, v, rolled), jnp.where(lane_lo, rolled, v)], axis=0)


WIDTH = 4


def _loop(lo, hi, fn, width=None):
    if width is None:
        def body(g, carry):
            fn(g)
            return carry

        if hi > lo:
            lax.fori_loop(lo, hi, body, 0)
        return
    while hi > lo:
        trips = (hi - lo) // width
        if trips:
            def body(i, carry, lo=lo, width=width):
                fn([lo + width * i + j for j in range(width)])
                return carry

            lax.fori_loop(0, trips, body, 0)
            lo += trips * width
        width = max(1, width // 2)


def _mix_weights(l1, l2, l3):
    mx = jnp.maximum(jnp.maximum(l1, l2), l3)
    e1, e2, e3 = jnp.exp(l1 - mx), jnp.exp(l2 - mx), jnp.exp(l3 - mx)
    inv = 1.0 / (e1 + e2 + e3)
    return e1 * inv, e2 * inv, e3 * inv


def _attention_fwd(qkv, rider=None):
    t = qkv.shape[0]
    groups = 16 * (t // HALF)

    def body(q_ref, k_ref, v_ref, attn_ref, l1_ref, l2_ref, l3_ref, o_scr, bias_scr):
        lane_lo, q_mask = _head_consts(SCALE)
        l_refs = (l1_ref, l2_ref, l3_ref)
        for p, (d, n, rows, stride) in enumerate(_PATTERNS):
            _set_bias(bias_scr, n, rows)
            o_p, l_p = o_scr.at[p], l_refs[p]

            def block(gs, has_prev):
                at = [_group_rows(d, g) for g in gs]

                def load(ref, b):
                    return _load_rows(ref, b, n, rows, stride).astype(BF16)

                q2 = [_stack_heads(load(q_ref, b), q_mask) for b, _ in at]
                k2 = [load(k_ref, b) for b, _ in at]
                v2 = [load(v_ref, b) for b, _ in at]
                if has_prev:
                    k2 = [jnp.concatenate([load(k_ref, pv), k], axis=0) for (_, pv), k in zip(at, k2)]
                    v2 = [jnp.concatenate([load(v_ref, pv), v], axis=0) for (_, pv), v in zip(at, v2)]
                s = [_dot_nt(q, k) for q, k in zip(q2, k2)]
                s = [x + (bias_scr[...] if has_prev else bias_scr[:, BLK:2 * BLK]) for x in s]
                mx = [jnp.max(x, axis=1, keepdims=True) for x in s]
                e = [jnp.exp(x - m) for x, m in zip(s, mx)]
                den = [jnp.sum(x, axis=1, keepdims=True) for x in e]
                o2 = [_dot(x.astype(BF16), v) * (1.0 / dn) for x, v, dn in zip(e, v2, den)]
                lse2 = [jnp.broadcast_to(m + jnp.log(dn), (2 * BLK, BLK)) for m, dn in zip(mx, den)]
                for (b, _), o, l in zip(at, o2, lse2):
                    _store_rows(o_p, b, _unstack_heads(o, lane_lo), n, rows, stride)
                    _store_rows(l_p, b, _unstack_heads(l, lane_lo), n, rows, stride)

            _loop(0, _FIRST[d], lambda gs: block(gs, False), width=2 * WIDTH)
            _loop(_FIRST[d], groups, lambda gs: block(gs, True), width=2 * WIDTH)

        def mix(i):
            sl = pl.ds(pl.multiple_of(i * 256, 256), 256)
            w = _mix_weights(l1_ref[sl, :], l2_ref[sl, :], l3_ref[sl, :])
            attn_ref[sl, :] = w[0] * o_scr[0, sl, :] + w[1] * o_scr[1, sl, :] + w[2] * o_scr[2, sl, :]

        _loop(0, t // 256, mix)

    def col(c0):
        return pl.BlockSpec((t, BLK), lambda hp: (0, c0 + hp))

    res, extra = _pcall(
        body, name="attention_fwd", grid=(4,), in_specs=[col(0), col(4), col(8)], out_specs=[col(0)] * 4,
        out_shape=[jax.ShapeDtypeStruct((t, 512), F32)] * 4,
        scratch_shapes=[pltpu.VMEM((3, t, BLK), F32), pltpu.VMEM((2 * BLK, 2 * BLK), F32)],
        semantics=("parallel",), vmem_mb=48, rider=rider,
    )(qkv, qkv, qkv)
    return res if rider is None else (res, extra)


def _attention_bwd(qkv, dattn, dsum, lses, dproj):
    t = qkv.shape[0]
    groups = 16 * (t // HALF)

    def body(q_ref, k_ref, v_ref, da_ref, ds_ref, l1_ref, l2_ref, l3_ref, kept_ref, out_ref, acc, bias_scr):
        del kept_ref
        lane_lo, head_mask = _head_consts()
        q_mask = _head_consts(SCALE)[1]
        l_refs = (l1_ref, l2_ref, l3_ref)

        def clear(i):
            sl = pl.ds(pl.multiple_of(i * 512, 512), 512)
            for s in range(3):
                acc[s, sl, :] = jnp.zeros((512, BLK), F32)

        _loop(0, t // 512, clear)
        dq_acc, dk_acc, dv_acc = acc.at[0], acc.at[1], acc.at[2]
        for p, (d, n, rows, stride) in enumerate(_PATTERNS):
            _set_bias(bias_scr, n, rows)

            def block(gs, has_prev):
                at = [_group_rows(d, g) for g in gs]

                def load(ref, b):
                    return _load_rows(ref, b, n, rows, stride)

                def put(ref, b, val):
                    _store_rows(ref, b, val, n, rows, stride, add=True)

                def wide(x):
                    return jnp.concatenate([x, x], axis=1) if has_prev else x

                lse = [[load(ref, b) for ref in l_refs] for b, _ in at]
                w = [_mix_weights(*ls)[p] for ls in lse]
                do2 = [_stack_heads((wg * load(da_ref, b)).astype(BF16), head_mask) for wg, (b, _) in zip(w, at)]
                dl2 = [wide(_rows_per_head(wg * load(ds_ref, b), lane_lo)) for wg, (b, _) in zip(w, at)]
                lse2 = [wide(_rows_per_head(ls[p], lane_lo)) for ls in lse]
                q2 = [_stack_heads(load(q_ref, b).astype(BF16), q_mask) for b, _ in at]
                k2 = [load(k_ref, b).astype(BF16) for b, _ in at]
                v2 = [load(v_ref, b).astype(BF16) for b, _ in at]
                if has_prev:
                    k2 = [jnp.concatenate([load(k_ref, pv).astype(BF16), k], axis=0) for (_, pv), k in zip(at, k2)]
                    v2 = [jnp.concatenate([load(v_ref, pv).astype(BF16), v], axis=0) for (_, pv), v in zip(at, v2)]
                s = [_dot_nt(q, k) for q, k in zip(q2, k2)]
                dp = [_dot_nt(do, v) for do, v in zip(do2, v2)]
                pr = [jnp.exp(x + (bias_scr[...] if has_prev else bias_scr[:, BLK:2 * BLK]) - l)
                      for x, l in zip(s, lse2)]
                ds = [(pg * (x - dl)).astype(BF16) for pg, x, dl in zip(pr, dp, dl2)]
                dq2 = [_dot(x, k) * SCALE for x, k in zip(ds, k2)]
                dk2 = [_dot_tn(x, q) for x, q in zip(ds, q2)]
                dv2 = [_dot_tn(pg.astype(BF16), do) for pg, do in zip(pr, do2)]
                for (b, pv), dq, dk, dv in zip(at, dq2, dk2, dv2):
                    put(dq_acc, b, _unstack_heads(dq, lane_lo))
                    if has_prev:
                        put(dk_acc, pv, dk[:BLK])
                        put(dv_acc, pv, dv[:BLK])
                        put(dk_acc, b, dk[BLK:])
                        put(dv_acc, b, dv[BLK:])
                    else:
                        put(dk_acc, b, dk)
                        put(dv_acc, b, dv)

            _loop(0, _FIRST[d], lambda gs: block(gs, False), width=WIDTH)
            _loop(_FIRST[d], groups, lambda gs: block(gs, True), width=WIDTH)

        def emit(i):
            sl = pl.ds(pl.multiple_of(i * 512, 512), 512)
            for s in range(3):
                out_ref[s, sl, :] = acc[s, sl, :].astype(BF16)

        _loop(0, t // 512, emit)

    def col(c0):
        return pl.BlockSpec((t, BLK), lambda hp: (0, c0 + hp))

    res, _ = _pcall(
        body, name="attention_bwd", grid=(4,),
        in_specs=[col(0), col(4), col(8)] + [col(0)] * 5 + [ANY],
        out_specs=[pl.BlockSpec((3, t, BLK), lambda hp: (0, 0, hp))],
        out_shape=[jax.ShapeDtypeStruct(dproj.shape, BF16)],
        scratch_shapes=[pltpu.VMEM((3, t, BLK), F32), pltpu.VMEM((2 * BLK, 2 * BLK), F32)],
        semantics=("parallel",), vmem_mb=56, aliases={8: 0},
    )(qkv, qkv, qkv, dattn, dsum, *lses, dproj)
    return res[0]


def _order_specs(t):
    n_i = SEG // TI
    nblk = (t // HALF) * n_i
    per = TI // HALO

    def main(c, col=0):
        return pl.BlockSpec((1, N_RES, TI, c), lambda s: (s // n_i, 0, s % n_i, col))

    def before(c, col=0):
        return pl.BlockSpec((1, 2, HALO, c), lambda s: (jnp.maximum(s - 1, 0) // n_i, N_RES // 2 - 1,
                                                        (jnp.maximum(s - 1, 0) % n_i) * per + per - 1, col))

    def after(c, col=0):
        return pl.BlockSpec((1, 2, HALO, c), lambda s: (jnp.minimum(s + 1, nblk - 1) // n_i, 0,
                                                        (jnp.minimum(s + 1, nblk - 1) % n_i) * per, col))

    return nblk, main, before, after


def _shift_in(v, row_in, up):
    rows = v.shape[0]
    idx = lax.broadcasted_iota(jnp.int32, v.shape, 0)
    fill = jnp.broadcast_to(row_in, v.shape)
    if up:
        return jnp.where(idx == rows - 1, fill, pltpu.roll(v, rows - 1, axis=0))
    return jnp.where(idx == 0, fill, pltpu.roll(v, 1, axis=0))


def _taps_behind(u, before):
    s15 = _shift_in(u[N_RES - 1], before[1, HALO - 1:HALO, :], up=False)
    s14 = _shift_in(u[N_RES - 2], before[0, HALO - 1:HALO, :], up=False)
    m1 = jnp.concatenate([s15[None], u[:N_RES - 1]], axis=0)
    m2 = jnp.concatenate([s14[None], s15[None], u[:N_RES - 2]], axis=0)
    return m1, m2


def _taps_ahead(u, after):
    t0 = _shift_in(u[0], after[0, 0:1, :], up=True)
    t1 = _shift_in(u[1], after[1, 0:1, :], up=True)
    p1 = jnp.concatenate([u[1:], t0[None]], axis=0)
    p2 = jnp.concatenate([u[2:], t0[None], t1[None]], axis=0)
    return p1, p2


def _conv_fwd(gates, before, first, cw):
    gates, before = gates.astype(F32), before.astype(F32)
    bg, cg, xc = gates[..., 0:512], gates[..., 512:1024], gates[..., 1024:1536]
    u = cg * xc
    ub = before[..., 512:1024] * before[..., 1024:1536]
    ub = jnp.where(first, jnp.zeros_like(ub), ub)
    m1, m2 = _taps_behind(u, ub)
    conv = m2 * cw[0:1, :] + m1 * cw[1:2, :] + u * cw[2:3, :]
    return bg, u, m1, m2, conv


def _sum_tokens(v):
    return jnp.sum(jnp.sum(v, axis=0), axis=0, keepdims=True)


def _mixer_fwd(x, attn, gates, cw, g_a, g_c, w_out):
    t, d = x.shape
    nblk, main, before, _ = _order_specs(t)
    rows = N_RES * TI

    def body(x_ref, at_ref, gt_ref, gb_ref, cw_ref, ga_ref, gc_ref, wa_ref, wb_ref, x1_ref, mg_ref):
        an = _rms_fwd(at_ref[0], ga_ref[...])[0].astype(BF16)
        bg, _, _, _, conv = _conv_fwd(gt_ref[0], gb_ref[0], pl.program_id(0) == 0, cw_ref[...])
        cn = _rms_fwd(bg * conv, gc_ref[...])[0].astype(BF16)
        mg_ref[0, :, :, 0:512] = an
        mg_ref[0, :, :, 512:1024] = cn
        y = _dot(an.reshape(rows, 512), wa_ref[...]) + _dot(cn.reshape(rows, 512), wb_ref[...])
        x1_ref[0] = x_ref[0] + y.reshape(N_RES, TI, d)

    const = lambda r, c, i0=0: pl.BlockSpec((r, c), lambda s: (i0, 0))
    x1, merged = pl.pallas_call(
        body, name="mixer_fwd", grid=(nblk,),
        in_specs=[main(d), main(512), main(1536), before(1536), const(3, 512), const(1, 512), const(1, 512),
                  const(512, d), const(512, d, 1)],
        out_specs=[main(d), main(d)],
        out_shape=[jax.ShapeDtypeStruct(_x4(x).shape, F32), jax.ShapeDtypeStruct(_x4(x).shape, BF16)],
        compiler_params=_params(("parallel",), 48),
    )(_x4(x), _x4(attn), _x4(gates), _x4(gates), cw, g_a, g_c, w_out, w_out)
    return x1.reshape(t, d), merged.reshape(t, d)


def _mixer_bwd(dx1, attn, gates, cw, g_a, g_c, w_out, head_sum, after=()):
    t, d = dx1.shape
    nblk, main, before, _ = _order_specs(t)
    rows = N_RES * TI

    def body(dx_ref, at_ref, gt_ref, gb_ref, cw_ref, ga_ref, gc_ref, wa_ref, wb_ref, hs_ref,
             da_ref, dsum_ref, dy_ref, gga_ref, ggc_ref):
        s = pl.program_id(0)
        dxb = dx_ref[0].reshape(rows, d).astype(BF16)
        dma = _dot_nt(dxb, wa_ref[...]).reshape(N_RES, TI, 512)
        dmc = _dot_nt(dxb, wb_ref[...]).reshape(N_RES, TI, 512)
        attn_v, g_av = at_ref[0], ga_ref[...]
        _, ah, ra = _rms_fwd(attn_v, g_av)
        dattn = _rms_bwd(dma, ah, ra, g_av)
        da_ref[0] = dattn
        z = (dattn * attn_v).reshape(rows, 512)
        hs = hs_ref[...]
        z1 = z.astype(BF16)
        z2 = (z - z1.astype(F32)).astype(BF16)
        dsum_ref[0] = (_dot(z1, hs) + _dot(z2, hs)).reshape(N_RES, TI, 512)
        bg, _, _, _, conv = _conv_fwd(gt_ref[0], gb_ref[0], s == 0, cw_ref[...])
        g_cv = gc_ref[...]
        _, yh, rc = _rms_fwd(bg * conv, g_cv)
        dy_ref[0] = _rms_bwd(dmc, yh, rc, g_cv)
        pa, pc = _sum_tokens(dma * ah), _sum_tokens(dmc * yh)

        @pl.when(s == 0)
        def _():
            gga_ref[...] = pa
            ggc_ref[...] = pc

        @pl.when(s != 0)
        def _():
            gga_ref[...] += pa
            ggc_ref[...] += pc

    const = lambda r, c, i0=0: pl.BlockSpec((r, c), lambda s: (i0, 0))
    shape4 = _x4(attn).shape
    res, _ = _pcall(
        body, name="mixer_bwd", grid=(nblk,),
        in_specs=[main(d), main(512), main(1536), before(1536), const(3, 512), const(1, 512), const(1, 512),
                  const(512, d), const(512, d, 1), const(512, 512)],
        out_specs=[main(512)] * 3 + [const(1, 512), const(1, 512)],
        out_shape=[jax.ShapeDtypeStruct(shape4, F32)] * 3 + [jax.ShapeDtypeStruct((1, 512), F32)] * 2,
        semantics=("arbitrary",), vmem_mb=48, after=after,
    )(_x4(dx1), _x4(attn), _x4(gates), _x4(gates), cw, g_a, g_c, w_out, w_out, head_sum)
    return [r.reshape(t, 512) for r in res[:3]] + res[3:]


def _conv_bwd(dy, gates, cw):
    t = dy.shape[0]
    nblk, main, before, after = _order_specs(t)
    n_i = SEG // TI

    def body(dy_ref, dya_ref, gt_ref, gb_ref, ga_ref, cw_ref, dp_ref, gcw_ref):
        s = pl.program_id(0)
        cw_v, gates_v = cw_ref[...], gt_ref[0]
        bg, u, m1, m2, conv = _conv_fwd(gates_v, gb_ref[0], s == 0, cw_v)
        dy_v = dy_ref[0]
        dconv = dy_v * bg
        dca = dya_ref[0] * ga_ref[0][..., 0:512].astype(F32)
        dca = jnp.where(s == nblk - 1, jnp.zeros_like(dca), dca)
        p1, p2 = _taps_ahead(dconv, dca)
        du = dconv * cw_v[2:3, :] + p1 * cw_v[1:2, :] + p2 * cw_v[0:1, :]
        dp_ref[0, 0] = (dy_v * conv).astype(BF16)
        dp_ref[1, 0] = (du * gates_v[..., 1024:1536].astype(F32)).astype(BF16)
        dp_ref[2, 0] = (du * gates_v[..., 512:1024].astype(F32)).astype(BF16)
        parts = [_sum_tokens(dconv * m2), _sum_tokens(dconv * m1), _sum_tokens(dconv * u)]

        @pl.when(s == 0)
        def _():
            gcw_ref[...] = jnp.zeros_like(gcw_ref)

        for tap in range(3):
            gcw_ref[tap:tap + 1, :] += parts[tap]

    (dproj, gcw), _ = _pcall(
        body, name="conv_bwd", grid=(nblk,),
        in_specs=[main(512), after(512), main(1536), before(1536), after(1536),
                  pl.BlockSpec((3, 512), lambda s: (0, 0))],
        out_specs=[pl.BlockSpec((3, 1, N_RES, TI, 512), lambda s: (1, s // n_i, 0, s % n_i, 0)),
                   pl.BlockSpec((8, 512), lambda s: (0, 0))],
        out_shape=[jax.ShapeDtypeStruct((6, t // HALF, N_RES, SEG, 512), BF16), jax.ShapeDtypeStruct((8, 512), F32)],
        semantics=("arbitrary",), vmem_mb=40,
    )(_x4(dy), _x4(dy), _x4(gates), _x4(gates), _x4(gates), cw)
    return dproj.reshape(6, t, 512), gcw


def _xattn_fwd(x1, g, w_q, kv, w_o, *, tb):
    t, d = x1.shape
    hd = d // N_MEM_HEADS
    m = kv.shape[0]

    def body(x_ref, g_ref, wq_ref, k_ref, v_ref, wo_ref, x2_ref, h_ref, q_ref, o_ref):
        xv = x_ref[...]
        h = _rms_fwd(xv, g_ref[...])[0].astype(BF16)
        h_ref[...] = h
        q = _dot(h, wq_ref[...]).astype(BF16)
        q_ref[...] = q
        for hh in range(N_MEM_HEADS):
            sl = slice(hh * hd, (hh + 1) * hd)
            s = _dot_nt(q[:, sl], k_ref[:, sl]) * (1.0 / 16.0)
            e = jnp.exp(s - jnp.max(s, axis=1, keepdims=True))
            p = e / jnp.sum(e, axis=1, keepdims=True)
            o_ref[:, sl] = _dot(p.astype(BF16), v_ref[:, sl]).astype(BF16)
        x2_ref[...] = xv + _dot(o_ref[...], wo_ref[...])

    tok = pl.BlockSpec((tb, d), lambda i: (i, 0))
    full = pl.BlockSpec((d, d), lambda i: (0, 0))
    return pl.pallas_call(
        body, name="xattn_fwd", grid=(t // tb,),
        in_specs=[tok, pl.BlockSpec((1, d), lambda i: (0, 0)), full,
                  pl.BlockSpec((m, d), lambda i: (0, 0)), pl.BlockSpec((m, d), lambda i: (0, 1)), full],
        out_specs=[tok] * 4,
        out_shape=[jax.ShapeDtypeStruct((t, d), F32)] + [jax.ShapeDtypeStruct((t, d), BF16)] * 3,
        compiler_params=_params(("parallel",), 48),
    )(x1, g, w_q, kv, kv, w_o)


def _xattn_bwd(dx2, x1, g, q, w_q, kv, w_o, *, tb, after=()):
    t, d = x1.shape
    hd = d // N_MEM_HEADS
    m = kv.shape[0]

    def body(dx2_ref, x_ref, g_ref, q_ref, wq_ref, k_ref, v_ref, wo_ref,
             dx1_ref, dx1b_ref, dq_ref, dk_ref, dv_ref, gg_ref):
        i = pl.program_id(0)

        @pl.when(i == 0)
        def _():
            dk_ref[...] = jnp.zeros_like(dk_ref)
            dv_ref[...] = jnp.zeros_like(dv_ref)

        dx2 = dx2_ref[...]
        do = _dot_nt(dx2.astype(BF16), wo_ref[...]).astype(BF16)
        for hh in range(N_MEM_HEADS):
            sl = slice(hh * hd, (hh + 1) * hd)
            qh, kh, vh, doh = q_ref[:, sl], k_ref[:, sl], v_ref[:, sl], do[:, sl]
            s = _dot_nt(qh, kh) * (1.0 / 16.0)
            e = jnp.exp(s - jnp.max(s, axis=1, keepdims=True))
            p = e / jnp.sum(e, axis=1, keepdims=True)
            dp = _dot_nt(doh, vh)
            ds = (p * (dp - jnp.sum(dp * p, axis=1, keepdims=True)) * (1.0 / 16.0)).astype(BF16)
            dq_ref[:, sl] = _dot(ds, kh).astype(BF16)
            dk_ref[:, sl] += _dot_tn(ds, qh)
            dv_ref[:, sl] += _dot_tn(p.astype(BF16), doh)
        dh = _dot_nt(dq_ref[...], wq_ref[...])
        g_v = g_ref[...]
        _, xh, r = _rms_fwd(x_ref[...], g_v)
        dx1 = dx2 + _rms_bwd(dh, xh, r, g_v)
        dx1_ref[...] = dx1
        dx1b_ref[...] = dx1.astype(BF16)
        part = jnp.sum(dh * xh, axis=0, keepdims=True)

        @pl.when(i == 0)
        def _():
            gg_ref[...] = part

        @pl.when(i != 0)
        def _():
            gg_ref[...] += part

    tok = pl.BlockSpec((tb, d), lambda i: (i, 0))
    full = pl.BlockSpec((d, d), lambda i: (0, 0))
    acc = pl.BlockSpec((m, d), lambda i: (0, 0))
    res, _ = _pcall(
        body, name="xattn_bwd", grid=(t // tb,),
        in_specs=[tok, tok, pl.BlockSpec((1, d), lambda i: (0, 0)), tok, full,
                  pl.BlockSpec((m, d), lambda i: (0, 0)), pl.BlockSpec((m, d), lambda i: (0, 1)), full],
        out_specs=[tok, tok, tok, acc, acc, pl.BlockSpec((1, d), lambda i: (0, 0))],
        out_shape=[jax.ShapeDtypeStruct((t, d), F32), jax.ShapeDtypeStruct((t, d), BF16),
                   jax.ShapeDtypeStruct((t, d), BF16),
                   jax.ShapeDtypeStruct((m, d), F32), jax.ShapeDtypeStruct((m, d), F32),
                   jax.ShapeDtypeStruct((1, d), F32)],
        semantics=("arbitrary",), vmem_mb=48, after=after,
    )(dx2, x1, g, q, w_q, kv, kv, w_o)
    return res


def _mlp_down_loss(a, w_down, x2, tgt, g, *, tb):
    t, d = x2.shape
    f = a.shape[1]

    def body(a_ref, w_ref, x_ref, t_ref, g_ref, dx_ref, dxb_ref, loss_ref, gg_ref):
        i = pl.program_id(0)
        av = a_ref[...].astype(F32)
        x3 = x_ref[...] + _dot((av * av).astype(BF16), w_ref[...])
        g_v = g_ref[...]
        out, xh, r = _rms_fwd(x3, g_v)
        err = out - t_ref[...]
        dout = err * (1.0 / d)
        dx = _rms_bwd(dout, xh, r, g_v)
        dx_ref[...] = dx
        dxb_ref[...] = dx.astype(BF16)
        part = jnp.sum(dout * xh, axis=0, keepdims=True)
        lpart = 0.5 * jnp.sum(jnp.mean(err * err, axis=-1, keepdims=True), axis=0, keepdims=True)
        lpart = jnp.broadcast_to(lpart, loss_ref.shape)

        @pl.when(i == 0)
        def _():
            gg_ref[...] = part
            loss_ref[...] = lpart

        @pl.when(i != 0)
        def _():
            gg_ref[...] += part
            loss_ref[...] += lpart

    tok = pl.BlockSpec((tb, d), lambda i: (i, 0))
    return pl.pallas_call(
        body, name="mlp_down_loss", grid=(t // tb,),
        in_specs=[pl.BlockSpec((tb, f), lambda i: (i, 0)), pl.BlockSpec((f, d), lambda i: (0, 0)), tok, tok,
                  pl.BlockSpec((1, d), lambda i: (0, 0))],
        out_specs=[tok, tok, pl.BlockSpec((8, 128), lambda i: (0, 0)), pl.BlockSpec((1, d), lambda i: (0, 0))],
        out_shape=[jax.ShapeDtypeStruct((t, d), F32), jax.ShapeDtypeStruct((t, d), BF16),
                   jax.ShapeDtypeStruct((8, 128), F32), jax.ShapeDtypeStruct((1, d), F32)],
        compiler_params=_params(("arbitrary",), 56),
    )(a, w_down, x2, tgt, g)


def _mlp_dpre(dx3, w_down, a, *, tb, bn):
    t, d = dx3.shape
    f = a.shape[1]

    def body(dx_ref, w_ref, a_ref, o_ref):
        o_ref[...] = (2.0 * a_ref[...].astype(F32) * _dot_nt(dx_ref[...], w_ref[...])).astype(BF16)

    return pl.pallas_call(
        body, name="mlp_dpre", grid=(t // tb, f // bn),
        in_specs=[pl.BlockSpec((tb, d), lambda i, j: (i, 0)), pl.BlockSpec((bn, d), lambda i, j: (j, 0)),
                  pl.BlockSpec((tb, bn), lambda i, j: (i, j))],
        out_specs=pl.BlockSpec((tb, bn), lambda i, j: (i, j)),
        out_shape=jax.ShapeDtypeStruct((t, f), BF16),
        compiler_params=_params(("parallel", "arbitrary"), 48),
    )(dx3, w_down, a)


def _adamw(gsum, w, m, v):
    m_new = ADAM_B1 * m + (1.0 - ADAM_B1) * gsum
    v_new = ADAM_B2 * v + (1.0 - ADAM_B2) * (gsum * gsum)
    m_hat = m_new / (1.0 - ADAM_B1 ** ADAM_STEP)
    v_hat = v_new / (1.0 - ADAM_B2 ** ADAM_STEP)
    delta = -ADAM_LR * (m_hat / (jnp.sqrt(v_hat) + ADAM_EPS) + ADAM_WD * w)
    return delta, m_new, v_new


def _sum_adamw(parts, w, m, v, *, name, tr):
    r, c = w.shape

    def body(p_ref, w_ref, m_ref, v_ref, g_ref, d_ref, mo_ref, vo_ref):
        g = p_ref[0].astype(F32)
        for k in range(1, N_DEV):
            g = g + p_ref[k].astype(F32)
        g_ref[...] = g
        d_ref[...], mo_ref[...], vo_ref[...] = _adamw(g, w_ref[...], m_ref[...], v_ref[...])

    blk = pl.BlockSpec((tr, c), lambda i: (i, 0))
    return pl.pallas_call(
        body, name=name, grid=(r // tr,),
        in_specs=[pl.BlockSpec((N_DEV, tr, c), lambda i: (0, i, 0)), blk, blk, blk],
        out_specs=[blk] * 4, out_shape=[jax.ShapeDtypeStruct((r, c), F32)] * 4,
        compiler_params=_params(("parallel",), 40),
    )(*[pltpu.with_memory_space_constraint(a, pltpu.HBM) for a in (parts, w, m, v)])


def _sum_small(parts):
    _, r, c = parts.shape

    def body(p_ref, o_ref):
        s = p_ref[0]
        for k in range(1, N_DEV):
            s = s + p_ref[k]
        o_ref[...] = s

    return pl.pallas_call(body, name="sum_small", out_shape=jax.ShapeDtypeStruct((r, c), F32))(parts)


def _adamw_small(g, w, m, v):
    def body(g_ref, w_ref, m_ref, v_ref, d_ref, mo_ref, vo_ref):
        d_ref[...], mo_ref[...], vo_ref[...] = _adamw(g_ref[...], w_ref[...], m_ref[...], v_ref[...])

    return pl.pallas_call(body, name="adamw_small", out_shape=[jax.ShapeDtypeStruct(g.shape, F32)] * 3)(g, w, m, v)


def _head_sum_matrix():
    r = lax.broadcasted_iota(jnp.int32, (512, 512), 0) // HEAD_DIM
    c = lax.broadcasted_iota(jnp.int32, (512, 512), 1) // HEAD_DIM
    return (r == c).astype(BF16)


_SHARD_AXIS = dict(w_in=1, w_out=0, w_q=0, w_kv=1, w_o=0, w_up=1, w_down=0, conv_w=None, small=None)


class _Weights:
    def __init__(self, full, shards=None):
        self.full = dict(full)
        self.shards = shards

    def rider(self, names, late=False):
        if self.shards is None:
            return None
        return _Gather([self.shards[n] for n in names], [_SHARD_AXIS[n] for n in names], late)

    def arrived(self, names, gathered):
        if gathered is not None:
            for n, g in zip(names, gathered):
                self.full[n] = g.transpose(1, 0, 2).reshape(g.shape[1], -1) if n == "conv_w" else g

    def __getitem__(self, name):
        return self.full[name]


class _Grads:
    def __init__(self, distributed):
        self.distributed = distributed
        self.local = {}
        self.pending = {}

    def add(self, name, g):
        self.local[name] = g

    def send(self, *names):
        if not self.distributed:
            return []
        rider = _Exchange([self.local[n] for n in names], [_SHARD_AXIS[n] for n in names])
        started = _exchange_start(rider, "send_" + "_".join(names))
        self.pending[names[0]] = (names, rider, started)
        return [started[3]]

    def wait(self, first_name, after):
        names, rider, started = self.pending.pop(first_name)
        return _exchange_wait(rider, started, after, "wait_" + "_".join(names))


def _ride(fn, *args, rider=None, **kw):
    if rider is None:
        return fn(*args, **kw), None
    return fn(*args, rider=rider, **kw)


def _local_step(x, mem, tgt, gains, weights, grads):
    names = ["w_in", "conv_w"]
    (x, tgt), got = _ride(_reorder, [x, tgt], "reorder_in", rider=weights.rider(names, late=True))
    weights.arrived(names, got)
    w_in, cw = weights["w_in"], weights["conv_w"]

    names = ["w_out", "w_kv"]
    (qkv, gates, h1), got = _ride(_proj, x, gains["g_mix"], w_in, tb=1024, rider=weights.rider(names))
    weights.arrived(names, got)
    names = ["w_q", "w_o", "w_up"]
    (attn, *lses), got = _ride(_attention_fwd, qkv, rider=weights.rider(names))
    weights.arrived(names, got)
    x1, merged = _mixer_fwd(x, attn, gates, cw, gains["g_attn_out"], gains["g_conv_out"], weights["w_out"])
    kv, mem_n = _norm_matmul(mem, gains["g_mem"], weights["w_kv"], name="mem_kv", out_dtype=BF16, tb=mem.shape[0],
                             bn=1024, save_h=True)
    x2, h2, qm, om = _xattn_fwd(x1, gains["g_xattn"], weights["w_q"], kv, weights["w_o"], tb=512)
    w_up = weights["w_up"]
    (a, h3), got = _ride(_norm_matmul, x2, gains["g_mlp"], w_up, name="mlp_up", out_dtype=BF16, tb=1024, bn=2048,
                         relu=True, save_h=True, rider=weights.rider(["w_down"], late=True))
    weights.arrived(["w_down"], got)
    w_down = weights["w_down"]
    dx3, dx3b, loss_blk, gg_final = _mlp_down_loss(a, w_down, x2, tgt, gains["g_final"], tb=512)

    dpre = _mlp_dpre(dx3b, w_down, a, tb=1024, bn=2048)
    grads.add("w_down", _matmul_tn(a, dx3b, name="grad_w_down", bm=512, bn=1024, square_a=True))
    grads.add("w_up", _matmul_tn(h3, dpre, name="grad_w_up", bm=1024, bn=1024))
    sent = grads.send("w_down", "w_up")
    dx2, dx2b, gg_mlp = _matmul_nt_normbwd(dpre, w_up, x2, gains["g_mlp"], dx3, name="mlp_dx", tb=512,
                                           also_bf16=True, after=sent)

    grads.add("w_o", _matmul_tn(om, dx2b, name="grad_w_o", bm=512, bn=512))
    dx1, dx1b, dqm, dk, dv, gg_xattn = _xattn_bwd(dx2, x1, gains["g_xattn"], qm, weights["w_q"], kv, weights["w_o"],
                                                  tb=512)
    grads.add("w_q", _matmul_tn(h2, dqm, name="grad_w_q", bm=1024, bn=512))
    dkv = jnp.concatenate([dk, dv], axis=1).astype(BF16)
    grads.add("w_kv", _matmul_tn(mem_n, dkv, name="grad_w_kv", bm=1024, bn=1024))
    _, gg_mem = _matmul_nt_normbwd(dkv, weights["w_kv"], mem, gains["g_mem"], None, name="mem_dx", tb=mem.shape[0])

    grads.add("w_out", _matmul_tn(merged, dx1b, name="grad_w_out", bm=1024, bn=512))
    sent = grads.send("w_o", "w_q", "w_kv", "w_out")
    dattn, dsum, dy, gg_attn, gg_conv = _mixer_bwd(dx1, attn, gates, cw, gains["g_attn_out"], gains["g_conv_out"],
                                                   weights["w_out"], _head_sum_matrix(), after=sent)
    dproj, gcw = _conv_bwd(dy, gates, cw)
    dproj = _attention_bwd(qkv, dattn, dsum, lses, dproj)
    grads.add("w_in", _matmul_tn(h1, dproj, name="grad_w_in", bm=1024, bn=512))
    sent = grads.send("w_in")
    grad_x, gg_mix = _matmul_nt_normbwd(dproj, w_in, x, gains["g_mix"], dx1, name="mixer_dx", tb=512,
                                        to_natural=True, after=sent)

    def part(v):
        return jnp.pad(v, ((0, SMALL_PART - v.shape[0]), (0, 1024 - v.shape[1])))

    parts = [gg_mix, gg_xattn, gg_mem, gg_mlp, gg_final, jnp.concatenate([gg_attn, gg_conv], axis=1), gcw, loss_blk]
    grads.add("small", jnp.concatenate([part(v) for v in parts], axis=0))
    return grad_x


SMALL_PART = 8
_BIG = ("w_in", "w_out", "w_q", "w_kv", "w_o", "w_up", "w_down")
_GAIN_ROWS = ("g_mix", "g_xattn", "g_mem", "g_mlp", "g_final")


def _pack_small(vals, conv):
    rows = [vals[k].reshape(1, -1) for k in _GAIN_ROWS]
    rows.append(jnp.concatenate([vals["g_attn_out"].reshape(1, -1), vals["g_conv_out"].reshape(1, -1)], axis=1))
    flat = conv.reshape(1, -1)
    rows.append(jnp.pad(flat, ((0, 0), (0, 1024 - flat.shape[1]))))
    rows.append(jnp.zeros((1, 1024), F32))
    return jnp.concatenate(rows, axis=0)


def kernel(x, mem, g_mix, w_in, conv_w, g_attn_out, g_conv_out, w_out, g_xattn, g_mem, w_q_mem, w_kv_mem, w_o_mem, g_mlp, w_up, w_down, g_final, loss_target, m_g_mix, m_w_in, m_conv_w, m_g_attn_out, m_g_conv_out, m_w_out, m_g_xattn, m_g_mem, m_w_q_mem, m_w_kv_mem, m_w_o_mem, m_g_mlp, m_w_up, m_w_down, m_g_final, v_g_mix, v_w_in, v_conv_w, v_g_attn_out, v_g_conv_out, v_w_out, v_g_xattn, v_g_mem, v_w_q_mem, v_w_kv_mem, v_w_o_mem, v_g_mlp, v_w_up, v_w_down, v_g_final):
    d = x.shape[-1]
    me = 4 * lax.axis_index("x") + 2 * lax.axis_index("y") + lax.axis_index("c")
    w_shards = dict(w_in=w_in, w_out=w_out, w_q=w_q_mem, w_kv=w_kv_mem, w_o=w_o_mem, w_up=w_up, w_down=w_down)
    m_shards = dict(w_in=m_w_in, w_out=m_w_out, w_q=m_w_q_mem, w_kv=m_w_kv_mem, w_o=m_w_o_mem, w_up=m_w_up,
                    w_down=m_w_down)
    v_shards = dict(w_in=v_w_in, w_out=v_w_out, w_q=v_w_q_mem, w_kv=v_w_kv_mem, w_o=v_w_o_mem, w_up=v_w_up,
                    w_down=v_w_down)
    gains = dict(g_mix=g_mix, g_attn_out=g_attn_out, g_conv_out=g_conv_out, g_xattn=g_xattn, g_mem=g_mem,
                 g_mlp=g_mlp, g_final=g_final)
    gains2 = {k: v.reshape(1, -1) for k, v in gains.items()}

    shards = {k: w_shards[k].astype(BF16) for k in _BIG}
    shards["conv_w"] = conv_w
    grads = _Grads(distributed=True)
    grad_x = _local_step(x[0], mem[0], loss_target[0], gains2, _Weights({}, shards), grads)

    after = grads.send("small")
    outs = {}
    tiles = dict(w_in=256, w_out=128, w_q=128, w_kv=256, w_o=128, w_up=256, w_down=256)
    for group in (("w_down", "w_up"), ("w_o", "w_q", "w_kv", "w_out"), ("w_in",)):
        for k, received in zip(group, grads.wait(group[0], after)):
            outs[k] = _sum_adamw(received, w_shards[k], m_shards[k], v_shards[k], name=f"adamw_{k}", tr=tiles[k])
            after = [outs[k][0]]
    small_received, = grads.wait("small", after)

    ssum = _sum_small(small_received)
    row = lambda i: ssum[SMALL_PART * i]
    loss = ssum[SMALL_PART * 7, 0]
    g_small = {k: row(i) for i, k in enumerate(_GAIN_ROWS)}
    g_small["g_attn_out"] = row(5)[0:512]
    g_small["g_conv_out"] = row(5)[512:1024]
    taps = ssum[SMALL_PART * 6:SMALL_PART * 6 + 3, 0:512]
    g_conv = lax.dynamic_slice_in_dim(taps, me * 64, 64, axis=1)
    m_small = dict(g_mix=m_g_mix, g_attn_out=m_g_attn_out, g_conv_out=m_g_conv_out, g_xattn=m_g_xattn,
                   g_mem=m_g_mem, g_mlp=m_g_mlp, g_final=m_g_final)
    v_small = dict(g_mix=v_g_mix, g_attn_out=v_g_attn_out, g_conv_out=v_g_conv_out, g_xattn=v_g_xattn,
                   g_mem=v_g_mem, g_mlp=v_g_mlp, g_final=v_g_final)
    packed = [_pack_small(g_small, g_conv), _pack_small(gains, conv_w), _pack_small(m_small, m_conv_w),
              _pack_small(v_small, v_conv_w)]
    upd = _adamw_small(*packed)

    def unpack(p):
        res = {k: p[i] for i, k in enumerate(_GAIN_ROWS)}
        res["g_attn_out"] = p[5, 0:512]
        res["g_conv_out"] = p[5, 512:1024]
        res["conv_w"] = p[6, 0:192].reshape(3, 64)
        return res

    g_small["conv_w"] = g_conv
    small_out = [g_small] + [unpack(p) for p in upd]
    names = {"g_mix": "g_mix", "w_in": "w_in", "conv_w": "conv_w", "g_attn_out": "g_attn_out",
             "g_conv_out": "g_conv_out", "w_out": "w_out", "g_xattn": "g_xattn", "g_mem": "g_mem",
             "w_q_mem": "w_q", "w_kv_mem": "w_kv", "w_o_mem": "w_o", "g_mlp": "g_mlp", "w_up": "w_up",
             "w_down": "w_down", "g_final": "g_final"}
    result = [loss, grad_x[None]]
    for which in range(4):
        for key in names.values():
            result.append(outs[key][which] if key in outs else small_out[which][key])
    return tuple(result)
```

```python
import math

import jax
import jax.numpy as jnp
from jax import lax
from jax.experimental import pallas as pl
from jax.experimental.pallas import tpu as pltpu

F32 = jnp.float32
BF16 = jnp.bfloat16
NORM_EPS = 1e-6
NEG_INF = -1e30
N_DEV = 8
BLK = 128
HEAD_DIM = 64
N_MEM_HEADS = 4
ADAM_LR = 0.001
ADAM_B1 = 0.9
ADAM_B2 = 0.999
ADAM_EPS = 1e-08
ADAM_WD = 0.01
ADAM_STEP = 10
MESH = pl.DeviceIdType.MESH
ANY = pl.BlockSpec(memory_space=pl.ANY)


def _dot(a, b):
    return jnp.dot(a, b, preferred_element_type=F32)


def _dot_nt(a, b):
    return lax.dot_general(a, b, (((1,), (1,)), ((), ())), preferred_element_type=F32)


def _dot_tn(a, b):
    return lax.dot_general(a, b, (((0,), (0,)), ((), ())), preferred_element_type=F32)


def _params(semantics, vmem_mb):
    return pltpu.CompilerParams(dimension_semantics=semantics, vmem_limit_bytes=vmem_mb << 20)


def _rms_fwd(x, g):
    r = lax.rsqrt(jnp.mean(x * x, axis=-1, keepdims=True) + NORM_EPS)
    xh = x * r
    return xh * g, xh, r


def _rms_bwd(dy, xh, r, g):
    gy = dy * g
    return r * (gy - xh * jnp.mean(xh * gy, axis=-1, keepdims=True))


def _position():
    x, y, c = lax.axis_index("x"), lax.axis_index("y"), lax.axis_index("c")
    return x, y, c


def _block_of(ref, j, axis, shard_shape):
    r, c = shard_shape
    if axis is None:
        return ref.at[j]
    if axis == 0:
        return ref.at[pl.ds(j * r, r), :]
    return ref.at[:, pl.ds(j * c, c)]


class _Gather:
    has_mid = True
    alias_pairs = ()

    def __init__(self, shards, axes, late=False):
        self.arrays = list(shards)
        self.axes = list(axes)
        self.late = late
        self.n = len(self.arrays)

    def out_shape(self):
        res = []
        for s, axis in zip(self.arrays, self.axes):
            r, c = s.shape
            shape = (N_DEV, r, c) if axis is None else (N_DEV * r, c) if axis == 0 else (r, N_DEV * c)
            res.append(jax.ShapeDtypeStruct(shape, s.dtype))
        return res

    def scratch(self):
        return [pltpu.SemaphoreType.DMA((self.n, 7)), pltpu.SemaphoreType.DMA((self.n, 7)),
                pltpu.SemaphoreType.DMA((self.n,))]

    def _ctx(self, ins, outs, sems):
        send_sems, recv_sems, local_sems = sems
        x, y, c = _position()
        me, sibling = (x, y, c), (x, y, 1 - c)
        chips = [(1 - x, y), (x, 1 - y), (1 - x, 1 - y)]

        def lin(px, py, pc):
            return 4 * px + 2 * py + pc

        def place(a, block):
            return _block_of(outs[a], lin(*block), self.axes[a], self.arrays[a].shape)

        def copy(a, k, block, to, src=None):
            dst = place(a, block)
            return pltpu.make_async_remote_copy(
                src_ref=dst if src is None else src, dst_ref=dst,
                send_sem=send_sems.at[a, k], recv_sem=recv_sems.at[a, k],
                device_id=to, device_id_type=MESH)

        def mine():
            return [pltpu.make_async_copy(ins[a], place(a, me), local_sems.at[a]) for a in range(self.n)]

        def first():
            res = []
            for a in range(self.n):
                res.append(copy(a, 0, me, sibling, src=ins[a]))
                res += [copy(a, 1 + j, me, (*chip, c), src=ins[a]) for j, chip in enumerate(chips)]
            return res

        return c, me, sibling, chips, copy, mine, first

    def start(self, ins, outs, sems):
        _, _, _, _, _, mine, first = self._ctx(ins, outs, sems)
        for cp in mine() + first():
            cp.start()

    def mid(self, ins, outs, sems):
        c, me, sibling, chips, copy, _, _ = self._ctx(ins, outs, sems)
        for j, chip in enumerate(chips):
            for a in range(self.n):
                copy(a, 1 + j, (*chip, c), me).wait_recv()
                copy(a, 4 + j, (*chip, c), sibling).start()

    def finish(self, ins, outs, sems):
        c, me, sibling, chips, copy, mine, first = self._ctx(ins, outs, sems)
        for a in range(self.n):
            copy(a, 0, sibling, me).wait_recv()
            for j, chip in enumerate(chips):
                copy(a, 4 + j, (*chip, 1 - c), me).wait_recv()
        for cp in first():
            cp.wait_send()
        for j, chip in enumerate(chips):
            for a in range(self.n):
                copy(a, 4 + j, (*chip, c), sibling).wait_send()
        for cp in mine():
            cp.wait()


class _Exchange:
    def __init__(self, parts, axes):
        self.n = len(parts)
        self.axes = list(axes)
        self.arrays = list(parts)

    def _piece(self, a):
        r, c = self.arrays[a].shape
        axis = self.axes[a]
        return (r, c) if axis is None else (r // N_DEV, c) if axis == 0 else (r, c // N_DEV)

    def out_shape(self):
        return [jax.ShapeDtypeStruct((N_DEV,) + self._piece(a), self.arrays[a].dtype) for a in range(self.n)]

    def semaphores(self):
        return [pltpu.SemaphoreType.DMA((7 * self.n,)), pltpu.SemaphoreType.DMA((7 * self.n,)),
                pltpu.SemaphoreType.DMA((self.n,))]

    def _ctx(self, ins, outs, sems):
        send_sems, recv_sems, local_sems = sems
        x, y, c = _position()
        me = 4 * x + 2 * y + c

        def src(a, j):
            return ins[a] if self.axes[a] is None else _block_of(ins[a], j, self.axes[a], self._piece(a))

        def dst(a, j):
            return outs[a].at[j]

        def local():
            return [pltpu.make_async_copy(src(a, me), dst(a, me), local_sems.at[a]) for a in range(self.n)]

        def remote(inbound):
            res = []
            for a in range(self.n):
                for k in range(1, N_DEV):
                    peer = (1 - x if k & 4 else x, 1 - y if k & 2 else y, 1 - c if k & 1 else c)
                    plin = 4 * peer[0] + 2 * peer[1] + peer[2]
                    res.append(pltpu.make_async_remote_copy(
                        src_ref=src(a, plin), dst_ref=dst(a, plin if inbound else me),
                        send_sem=send_sems.at[7 * a + k - 1], recv_sem=recv_sems.at[7 * a + k - 1],
                        device_id=peer, device_id_type=MESH))
            return res

        return local, remote

    def start(self, ins, outs, sems):
        local, remote = self._ctx(ins, outs, sems)
        for cp in local() + remote(False):
            cp.start()

    def finish(self, ins, outs, sems):
        local, remote = self._ctx(ins, outs, sems)
        for cp in remote(True):
            cp.wait_recv()
        for cp in remote(False):
            cp.wait_send()
        for cp in local():
            cp.wait()


def _exchange_start(rider, name):
    n = rider.n
    parts = rider.arrays
    lands = [lax.empty(s.shape, s.dtype) for s in rider.out_shape()]
    hbm = pl.BlockSpec(memory_space=pltpu.HBM)
    sem = pl.BlockSpec(memory_space=pltpu.SEMAPHORE)

    def body(*refs):
        ins, sems = refs[:n], refs[2 * n:2 * n + 3]
        outs, token = refs[2 * n + 3 + n:2 * n + 3 + 2 * n], refs[-1]
        rider.start(ins, outs, sems)
        token[...] = jnp.zeros_like(token)

    res = pl.pallas_call(
        body, name=name,
        out_shape=rider.semaphores() + [pltpu.HBM(p.shape, p.dtype) for p in parts]
                  + [pltpu.HBM(z.shape, z.dtype) for z in lands] + [jax.ShapeDtypeStruct((8, 128), F32)],
        in_specs=[hbm] * (2 * n), out_specs=[sem] * 3 + [hbm] * (2 * n) + [pl.BlockSpec(memory_space=pltpu.VMEM)],
        input_output_aliases={i: 3 + i for i in range(2 * n)},
        compiler_params=pltpu.CompilerParams(has_side_effects=pltpu.SideEffectType.DATAFLOW_SIDE_EFFECTING),
    )(*[pltpu.with_memory_space_constraint(a, pltpu.HBM) for a in parts + lands])
    return res[:3], res[3:3 + n], res[3 + n:3 + 2 * n], res[-1]


def _exchange_wait(rider, started, after, name):
    n = rider.n
    sems, parts, lands, _ = started
    hbm = pl.BlockSpec(memory_space=pltpu.HBM)
    sem = pl.BlockSpec(memory_space=pltpu.SEMAPHORE)

    def body(*refs):
        rider.finish(refs[:n], refs[n:2 * n], refs[2 * n:2 * n + 3])

    res = pl.pallas_call(
        body, name=name, out_shape=[pltpu.HBM(a.shape, a.dtype) for a in list(parts) + list(lands)],
        in_specs=[hbm] * (2 * n) + [sem] * 3 + [ANY] * len(after), out_specs=[hbm] * (2 * n),
        input_output_aliases={i: i for i in range(2 * n)},
        compiler_params=pltpu.CompilerParams(has_side_effects=pltpu.SideEffectType.DATAFLOW_SIDE_EFFECTING),
    )(*parts, *lands, *sems, *after)
    return list(res[n:])


def _pcall(body, *, name, grid, in_specs, out_specs, out_shape, scratch_shapes=(), semantics, vmem_mb, rider=None,
           aliases=None, after=()):
    in_specs, out_specs, out_shape = list(in_specs), list(out_specs), list(out_shape)
    scratch_shapes = list(scratch_shapes)
    aliases = dict(aliases or {})
    if rider is None:
        n_in, after = len(in_specs), list(after)

        def plain(*refs):
            body(*refs[:n_in], *refs[n_in + len(after):])

        call = pl.pallas_call(plain if after else body, name=name, grid=grid, in_specs=in_specs + [ANY] * len(after),
                              out_specs=out_specs, out_shape=out_shape, scratch_shapes=scratch_shapes,
                              input_output_aliases=aliases, compiler_params=_params(semantics, vmem_mb))
        return lambda *args: (list(call(*args, *after)), None)
    n_in, n_out, n_scr = len(in_specs), len(out_specs), len(scratch_shapes)
    r_in, r_shapes = len(rider.arrays), rider.out_shape()
    r_out = len(r_shapes)
    aliases.update({n_in + i: n_out + o for i, o in rider.alias_pairs})
    total = math.prod(grid)
    mid_step = total - 1 if rider.has_mid and rider.late else (3 * total) // 4

    def wrapped(*refs):
        bounds = [0, n_in, r_in, n_out, r_out, n_scr]
        for i in range(1, len(bounds)):
            bounds[i] += bounds[i - 1]
        a, ra, o, ro, s = (refs[bounds[i]:bounds[i + 1]] for i in range(5))
        rs = refs[bounds[5]:]
        step = pl.program_id(0)
        for k in range(1, len(grid)):
            step = step * grid[k] + pl.program_id(k)
        pl.when(step == 0)(lambda: rider.start(ra, ro, rs))
        body(*a, *o, *s)
        if rider.has_mid:
            pl.when(step == mid_step)(lambda: rider.mid(ra, ro, rs))
        pl.when(step == total - 1)(lambda: rider.finish(ra, ro, rs))

    call = pl.pallas_call(
        wrapped, name=name, grid=grid, in_specs=in_specs + [ANY] * r_in, out_specs=out_specs + [ANY] * r_out,
        out_shape=out_shape + r_shapes, scratch_shapes=scratch_shapes + rider.scratch(),
        input_output_aliases=aliases, compiler_params=_params(("arbitrary",) * len(grid), vmem_mb))

    def run(*args):
        res = call(*args, *rider.arrays)
        return list(res[:n_out]), list(res[n_out:])

    return run


def _norm_matmul(x, g, w, *, name, out_dtype, tb, bn, relu=False, save_h=False, rider=None):
    t, d = x.shape
    n = w.shape[1]

    def body(x_ref, g_ref, w_ref, o_ref, *rest):
        h_scr = rest[-1]

        @pl.when(pl.program_id(1) == 0)
        def _():
            h = _rms_fwd(x_ref[...], g_ref[...])[0].astype(BF16)
            h_scr[...] = h
            if save_h:
                rest[0][...] = h

        acc = _dot(h_scr[...], w_ref[...])
        if relu:
            acc = jnp.maximum(acc, 0.0)
        o_ref[...] = acc.astype(out_dtype)

    out_shape = [jax.ShapeDtypeStruct((t, n), out_dtype)]
    out_specs = [pl.BlockSpec((tb, bn), lambda i, j: (i, j))]
    if save_h:
        out_shape.append(jax.ShapeDtypeStruct((t, d), BF16))
        out_specs.append(pl.BlockSpec((tb, d), lambda i, j: (i, 0)))
    res, extra = _pcall(
        body, name=name, grid=(t // tb, n // bn),
        in_specs=[pl.BlockSpec((tb, d), lambda i, j: (i, 0)),
                  pl.BlockSpec((1, d), lambda i, j: (0, 0)),
                  pl.BlockSpec((d, bn), lambda i, j: (0, j))],
        out_specs=out_specs, out_shape=out_shape,
        scratch_shapes=[pltpu.VMEM((tb, d), BF16)],
        semantics=("parallel", "arbitrary"), vmem_mb=48, rider=rider,
    )(x, g, w)
    res = res if save_h else res[0]
    return res if rider is None else (res, extra)


def _proj(x, g, w, *, tb, rider=None):
    t, d = x.shape
    half = w.shape[1] // 2

    def body(x_ref, g_ref, w_ref, qkv_ref, gates_ref, h_ref, h_scr):
        j = pl.program_id(1)

        @pl.when(j == 0)
        def _():
            h = _rms_fwd(x_ref[...], g_ref[...])[0].astype(BF16)
            h_scr[...] = h
            h_ref[...] = h

        acc = _dot(h_scr[...], w_ref[...])

        @pl.when(j == 0)
        def _():
            qkv_ref[...] = acc

        @pl.when(j == 1)
        def _():
            gates_ref[...] = acc.astype(BF16)

    tok = lambda c: pl.BlockSpec((tb, c), lambda i, j: (i, 0))
    res, extra = _pcall(
        body, name="proj", grid=(t // tb, 2),
        in_specs=[tok(d), pl.BlockSpec((1, d), lambda i, j: (0, 0)), pl.BlockSpec((d, half), lambda i, j: (0, j))],
        out_specs=[tok(half), tok(half), tok(d)],
        out_shape=[jax.ShapeDtypeStruct((t, half), F32), jax.ShapeDtypeStruct((t, half), BF16),
                   jax.ShapeDtypeStruct((t, d), BF16)],
        scratch_shapes=[pltpu.VMEM((tb, d), BF16)],
        semantics=("parallel", "arbitrary"), vmem_mb=48, rider=rider,
    )(x, g, w)
    return res if rider is None else (res, extra)


def _matmul_nt_normbwd(dy, w, x, g, dres, *, name, tb, also_bf16=False, to_natural=False, after=()):
    t, d = x.shape
    stacked = dy.ndim == 3
    has_res = dres is not None
    n_i = SEG // TI
    if to_natural:
        tb = N_RES * TI

    def body(dy_ref, w_ref, x_ref, g_ref, *rest):
        rest = list(rest)
        dres_ref = rest.pop(0) if has_res else None
        dx_ref = rest.pop(0)
        dxb_ref = rest.pop(0) if also_bf16 else None
        gg_ref = rest.pop(0)
        i = pl.program_id(0)

        def rows(ref, *lead):
            v = ref[lead] if lead else ref[...]
            return v[0].reshape(tb, v.shape[-1]) if to_natural else v

        if stacked:
            kb = dy_ref.shape[-1]
            dh = _dot_nt(rows(dy_ref, 0), w_ref[:, 0:kb])
            for s in range(1, dy_ref.shape[0]):
                dh = dh + _dot_nt(rows(dy_ref, s), w_ref[:, s * kb:(s + 1) * kb])
        else:
            dh = _dot_nt(rows(dy_ref), w_ref[...])
        g_v = g_ref[...]
        _, xh, r = _rms_fwd(rows(x_ref), g_v)
        dx = _rms_bwd(dh, xh, r, g_v)
        if has_res:
            dx = dx + rows(dres_ref)
        if to_natural:
            scr = rest.pop(0)
            for cb in range(d // BLK):
                cols = slice(cb * BLK, (cb + 1) * BLK)
                slab = scr.at[cb]
                for res in range(N_RES):
                    slab[pl.ds(res, TI, stride=N_RES), :] = dx[res * TI:(res + 1) * TI, cols]
                dx_ref[:, cols] = slab[...]
        else:
            dx_ref[...] = dx
        if also_bf16:
            dxb_ref[...] = dx.astype(BF16)
        part = jnp.sum(dh * xh, axis=0, keepdims=True)

        @pl.when(i == 0)
        def _():
            gg_ref[...] = part

        @pl.when(i != 0)
        def _():
            gg_ref[...] += part

    tok = pl.BlockSpec((tb, d), lambda i: (i, 0))
    row = pl.BlockSpec((1, d), lambda i: (0, 0))
    if to_natural:
        act = pl.BlockSpec((1, N_RES, TI, d), lambda i: (i // n_i, 0, i % n_i, 0))
        dy_spec = pl.BlockSpec((dy.shape[0], 1, N_RES, TI, dy.shape[2]), lambda i: (0, i // n_i, 0, i % n_i, 0))
        dy, x = dy.reshape(dy.shape[0], t // HALF, N_RES, SEG, dy.shape[2]), _x4(x)
        dres = _x4(dres) if has_res else None
    elif stacked:
        act, dy_spec = tok, pl.BlockSpec((dy.shape[0], tb, dy.shape[2]), lambda i: (0, i, 0))
    else:
        act, dy_spec = tok, pl.BlockSpec((tb, dy.shape[1]), lambda i: (i, 0))
    in_specs = [dy_spec, pl.BlockSpec(w.shape, lambda i: (0, 0)), act, row]
    args = [dy, w, x, g]
    if has_res:
        in_specs.append(act)
        args.append(dres)
    out_specs = [tok] + ([tok] if also_bf16 else []) + [row]
    out_shape = ([jax.ShapeDtypeStruct((t, d), F32)] + ([jax.ShapeDtypeStruct((t, d), BF16)] if also_bf16 else [])
                 + [jax.ShapeDtypeStruct((1, d), F32)])
    res, _ = _pcall(
        body, name=name, grid=(t // tb,), in_specs=in_specs, out_specs=out_specs, out_shape=out_shape,
        scratch_shapes=[pltpu.VMEM((d // BLK, tb, BLK), F32)] if to_natural else [],
        semantics=("arbitrary",), vmem_mb=56, after=after,
    )(*args)
    return res


def _matmul_tn(a, b, *, name, bm, bn, square_a=False, after=()):
    t, m = a.shape
    stacked = b.ndim == 3
    n = b.shape[0] * bn if stacked else b.shape[1]

    def body(a_ref, b_ref, o_ref):
        av = a_ref[...]
        if square_a:
            av = av.astype(F32)
            av = (av * av).astype(BF16)
        o_ref[...] = _dot_tn(av, b_ref[...]).astype(BF16)

    res, _ = _pcall(
        body, name=name, grid=(m // bm, n // bn),
        in_specs=[pl.BlockSpec((t, bm), lambda i, j: (0, i)),
                  pl.BlockSpec((None, t, bn), lambda i, j: (j, 0, 0)) if stacked
                  else pl.BlockSpec((t, bn), lambda i, j: (0, j))],
        out_specs=[pl.BlockSpec((bm, bn), lambda i, j: (i, j))], out_shape=[jax.ShapeDtypeStruct((m, n), BF16)],
        semantics=("parallel", "parallel"), vmem_mb=56, after=after,
    )(a, b)
    return res[0]


N_RES = 16
SEG = 128
HALF = N_RES * SEG
TI = 32
HALO = 16


def _x4(a):
    return a.reshape(a.shape[0] // HALF, N_RES, SEG, a.shape[1])


def _reorder(arrays, name, rider=None):
    t, c = arrays[0].shape
    n = len(arrays)
    n_i = SEG // TI

    def body(*refs):
        scr = refs[-1]
        for i_ref, o_ref in zip(refs[:n], refs[n:2 * n]):
            for cb in range(c // BLK):
                cols = slice(cb * BLK, (cb + 1) * BLK)
                slab = scr.at[cb]
                slab[...] = i_ref[:, cols]
                for r in range(N_RES):
                    o_ref[0, r, :, cols] = slab[pl.ds(r, TI, stride=N_RES), :]

    res, extra = _pcall(
        body, name=name, grid=(t // (TI * N_RES),),
        in_specs=[pl.BlockSpec((TI * N_RES, c), lambda s: (s, 0))] * n,
        out_specs=[pl.BlockSpec((1, N_RES, TI, c), lambda s: (s // n_i, 0, s % n_i, 0))] * n,
        out_shape=[jax.ShapeDtypeStruct((t // HALF, N_RES, SEG, c), F32)] * n,
        scratch_shapes=[pltpu.VMEM((c // BLK, TI * N_RES, BLK), F32)],
        semantics=("parallel",), vmem_mb=32, rider=rider,
    )(*arrays)
    res = [r.reshape(t, c) for r in res]
    return res if rider is None else (res, extra)


_PATTERNS = ((1, 16, 8, SEG), (4, 4, 32, 4 * SEG), (16, 1, SEG, 0))
_FIRST = {1: 1, 4: 4, 16: 16}


def _group_rows(d, g):
    a = g >> 4
    if d == 16:
        base = a * HALF + (g & 15) * SEG
        prev = base - HALF
    elif d == 4:
        c = (g >> 2) & 3
        base = a * HALF + (g & 3) * SEG + c * 32
        prev = jnp.where(c > 0, base - 32, base - HALF + 96)
    else:
        c = g & 15
        base = a * HALF + c * 8
        prev = jnp.where(c > 0, base - 8, base - HALF + 120)
    return base, prev


def _load_rows(ref, base, n, rows, stride):
    parts = [ref[pl.ds(pl.multiple_of(base + j * stride, 8), rows), :] for j in range(n)]
    return parts[0] if n == 1 else jnp.concatenate(parts, axis=0)


def _store_rows(ref, base, val, n, rows, stride, add=False):
    for j in range(n):
        sl = pl.ds(pl.multiple_of(base + j * stride, 8), rows)
        piece = val[j * rows:(j + 1) * rows, :]
        if add:
            ref[sl, :] += piece
        else:
            ref[sl, :] = piece


def _band_bias(n, rows):
    shift = rows.bit_length() - 1
    lq = lax.broadcasted_iota(jnp.int32, (BLK, BLK), 0)
    lk = lax.broadcasted_iota(jnp.int32, (BLK, BLK), 1)
    iq = (lq & (rows - 1)) * n + (lq >> shift)
    ik = (lk & (rows - 1)) * n + (lk >> shift)
    zero = jnp.zeros((BLK, BLK), F32)
    return jnp.where(ik >= iq, zero, NEG_INF), jnp.where(ik <= iq, zero, NEG_INF)


def _set_bias(bias_scr, n, rows):
    prev_b, cur_b = _band_bias(n, rows)
    for half in range(2):
        bias_scr[half * BLK:(half + 1) * BLK, 0:BLK] = prev_b
        bias_scr[half * BLK:(half + 1) * BLK, BLK:2 * BLK] = cur_b


SCALE = 1.0 / math.sqrt(HEAD_DIM)


def _head_consts(value=1.0):
    lane_lo = lax.broadcasted_iota(jnp.int32, (BLK, BLK), 1) < HEAD_DIM
    return lane_lo, [jnp.where(lane_lo, value, 0.0).astype(BF16), jnp.where(lane_lo, 0.0, value).astype(BF16)]


def _stack_heads(v, head_mask):
    return jnp.concatenate([v * head_mask[0], v * head_mask[1]], axis=0)


def _unstack_heads(v2, lane_lo):
    return jnp.where(lane_lo, v2[:BLK], v2[BLK:])


def _rows_per_head(v, lane_lo):
    rolled = pltpu.roll(v, HEAD_DIM, axis=1)
    return jnp.concatenate([jnp.where(lane_lo, v, rolled), jnp.where(lane_lo, rolled, v)], axis=0)


WIDTH = 4


def _loop(lo, hi, fn, width=None):
    if width is None:
        def body(g, carry):
            fn(g)
            return carry

        if hi > lo:
            lax.fori_loop(lo, hi, body, 0)
        return
    while hi > lo:
        trips = (hi - lo) // width
        if trips:
            def body(i, carry, lo=lo, width=width):
                fn([lo + width * i + j for j in range(width)])
                return carry

            lax.fori_loop(0, trips, body, 0)
            lo += trips * width
        width = max(1, width // 2)


def _mix_weights(l1, l2, l3):
    mx = jnp.maximum(jnp.maximum(l1, l2), l3)
    e1, e2, e3 = jnp.exp(l1 - mx), jnp.exp(l2 - mx), jnp.exp(l3 - mx)
    inv = 1.0 / (e1 + e2 + e3)
    return e1 * inv, e2 * inv, e3 * inv


def _attention_fwd(qkv, rider=None):
    t = qkv.shape[0]
    groups = 16 * (t // HALF)

    def body(q_ref, k_ref, v_ref, attn_ref, l1_ref, l2_ref, l3_ref, o_scr, bias_scr):
        lane_lo, q_mask = _head_consts(SCALE)
        l_refs = (l1_ref, l2_ref, l3_ref)
        for p, (d, n, rows, stride) in enumerate(_PATTERNS):
            _set_bias(bias_scr, n, rows)
            o_p, l_p = o_scr.at[p], l_refs[p]

            def block(gs, has_prev):
                at = [_group_rows(d, g) for g in gs]

                def load(ref, b):
                    return _load_rows(ref, b, n, rows, stride).astype(BF16)

                q2 = [_stack_heads(load(q_ref, b), q_mask) for b, _ in at]
                k2 = [load(k_ref, b) for b, _ in at]
                v2 = [load(v_ref, b) for b, _ in at]
                if has_prev:
                    k2 = [jnp.concatenate([load(k_ref, pv), k], axis=0) for (_, pv), k in zip(at, k2)]
                    v2 = [jnp.concatenate([load(v_ref, pv), v], axis=0) for (_, pv), v in zip(at, v2)]
                s = [_dot_nt(q, k) for q, k in zip(q2, k2)]
                s = [x + (bias_scr[...] if has_prev else bias_scr[:, BLK:2 * BLK]) for x in s]
                mx = [jnp.max(x, axis=1, keepdims=True) for x in s]
                e = [jnp.exp(x - m) for x, m in zip(s, mx)]
                den = [jnp.sum(x, axis=1, keepdims=True) for x in e]
                o2 = [_dot(x.astype(BF16), v) * (1.0 / dn) for x, v, dn in zip(e, v2, den)]
                lse2 = [jnp.broadcast_to(m + jnp.log(dn), (2 * BLK, BLK)) for m, dn in zip(mx, den)]
                for (b, _), o, l in zip(at, o2, lse2):
                    _store_rows(o_p, b, _unstack_heads(o, lane_lo), n, rows, stride)
                    _store_rows(l_p, b, _unstack_heads(l, lane_lo), n, rows, stride)

            _loop(0, _FIRST[d], lambda gs: block(gs, False), width=2 * WIDTH)
            _loop(_FIRST[d], groups, lambda gs: block(gs, True), width=2 * WIDTH)

        def mix(i):
            sl = pl.ds(pl.multiple_of(i * 256, 256), 256)
            w = _mix_weights(l1_ref[sl, :], l2_ref[sl, :], l3_ref[sl, :])
            attn_ref[sl, :] = w[0] * o_scr[0, sl, :] + w[1] * o_scr[1, sl, :] + w[2] * o_scr[2, sl, :]

        _loop(0, t // 256, mix)

    def col(c0):
        return pl.BlockSpec((t, BLK), lambda hp: (0, c0 + hp))

    res, extra = _pcall(
        body, name="attention_fwd", grid=(4,), in_specs=[col(0), col(4), col(8)], out_specs=[col(0)] * 4,
        out_shape=[jax.ShapeDtypeStruct((t, 512), F32)] * 4,
        scratch_shapes=[pltpu.VMEM((3, t, BLK), F32), pltpu.VMEM((2 * BLK, 2 * BLK), F32)],
        semantics=("parallel",), vmem_mb=48, rider=rider,
    )(qkv, qkv, qkv)
    return res if rider is None else (res, extra)


def _attention_bwd(qkv, dattn, dsum, lses, dproj):
    t = qkv.shape[0]
    groups = 16 * (t // HALF)

    def body(q_ref, k_ref, v_ref, da_ref, ds_ref, l1_ref, l2_ref, l3_ref, kept_ref, out_ref, acc, bias_scr):
        del kept_ref
        lane_lo, head_mask = _head_consts()
        q_mask = _head_consts(SCALE)[1]
        l_refs = (l1_ref, l2_ref, l3_ref)

        def clear(i):
            sl = pl.ds(pl.multiple_of(i * 512, 512), 512)
            for s in range(3):
                acc[s, sl, :] = jnp.zeros((512, BLK), F32)

        _loop(0, t // 512, clear)
        dq_acc, dk_acc, dv_acc = acc.at[0], acc.at[1], acc.at[2]
        for p, (d, n, rows, stride) in enumerate(_PATTERNS):
            _set_bias(bias_scr, n, rows)

            def block(gs, has_prev):
                at = [_group_rows(d, g) for g in gs]

                def load(ref, b):
                    return _load_rows(ref, b, n, rows, stride)

                def put(ref, b, val):
                    _store_rows(ref, b, val, n, rows, stride, add=True)

                def wide(x):
                    return jnp.concatenate([x, x], axis=1) if has_prev else x

                lse = [[load(ref, b) for ref in l_refs] for b, _ in at]
                w = [_mix_weights(*ls)[p] for ls in lse]
                do2 = [_stack_heads((wg * load(da_ref, b)).astype(BF16), head_mask) for wg, (b, _) in zip(w, at)]
                dl2 = [wide(_rows_per_head(wg * load(ds_ref, b), lane_lo)) for wg, (b, _) in zip(w, at)]
                lse2 = [wide(_rows_per_head(ls[p], lane_lo)) for ls in lse]
                q2 = [_stack_heads(load(q_ref, b).astype(BF16), q_mask) for b, _ in at]
                k2 = [load(k_ref, b).astype(BF16) for b, _ in at]
                v2 = [load(v_ref, b).astype(BF16) for b, _ in at]
                if has_prev:
                    k2 = [jnp.concatenate([load(k_ref, pv).astype(BF16), k], axis=0) for (_, pv), k in zip(at, k2)]
                    v2 = [jnp.concatenate([load(v_ref, pv).astype(BF16), v], axis=0) for (_, pv), v in zip(at, v2)]
                s = [_dot_nt(q, k) for q, k in zip(q2, k2)]
                dp = [_dot_nt(do, v) for do, v in zip(do2, v2)]
                pr = [jnp.exp(x + (bias_scr[...] if has_prev else bias_scr[:, BLK:2 * BLK]) - l)
                      for x, l in zip(s, lse2)]
                ds = [(pg * (x - dl)).astype(BF16) for pg, x, dl in zip(pr, dp, dl2)]
                dq2 = [_dot(x, k) * SCALE for x, k in zip(ds, k2)]
                dk2 = [_dot_tn(x, q) for x, q in zip(ds, q2)]
                dv2 = [_dot_tn(pg.astype(BF16), do) for pg, do in zip(pr, do2)]
                for (b, pv), dq, dk, dv in zip(at, dq2, dk2, dv2):
                    put(dq_acc, b, _unstack_heads(dq, lane_lo))
                    if has_prev:
                        put(dk_acc, pv, dk[:BLK])
                        put(dv_acc, pv, dv[:BLK])
                        put(dk_acc, b, dk[BLK:])
                        put(dv_acc, b, dv[BLK:])
                    else:
                        put(dk_acc, b, dk)
                        put(dv_acc, b, dv)

            _loop(0, _FIRST[d], lambda gs: block(gs, False), width=WIDTH)
            _loop(_FIRST[d], groups, lambda gs: block(gs, True), width=WIDTH)

        def emit(i):
            sl = pl.ds(pl.multiple_of(i * 512, 512), 512)
            for s in range(3):
                out_ref[s, sl, :] = acc[s, sl, :].astype(BF16)

        _loop(0, t // 512, emit)

    def col(c0):
        return pl.BlockSpec((t, BLK), lambda hp: (0, c0 + hp))

    res, _ = _pcall(
        body, name="attention_bwd", grid=(4,),
        in_specs=[col(0), col(4), col(8)] + [col(0)] * 5 + [ANY],
        out_specs=[pl.BlockSpec((3, t, BLK), lambda hp: (0, 0, hp))],
        out_shape=[jax.ShapeDtypeStruct(dproj.shape, BF16)],
        scratch_shapes=[pltpu.VMEM((3, t, BLK), F32), pltpu.VMEM((2 * BLK, 2 * BLK), F32)],
        semantics=("parallel",), vmem_mb=56, aliases={8: 0},
    )(qkv, qkv, qkv, dattn, dsum, *lses, dproj)
    return res[0]


def _order_specs(t):
    n_i = SEG // TI
    nblk = (t // HALF) * n_i
    per = TI // HALO

    def main(c, col=0):
        return pl.BlockSpec((1, N_RES, TI, c), lambda s: (s // n_i, 0, s % n_i, col))

    def before(c, col=0):
        return pl.BlockSpec((1, 2, HALO, c), lambda s: (jnp.maximum(s - 1, 0) // n_i, N_RES // 2 - 1,
                                                        (jnp.maximum(s - 1, 0) % n_i) * per + per - 1, col))

    def after(c, col=0):
        return pl.BlockSpec((1, 2, HALO, c), lambda s: (jnp.minimum(s + 1, nblk - 1) // n_i, 0,
                                                        (jnp.minimum(s + 1, nblk - 1) % n_i) * per, col))

    return nblk, main, before, after


def _shift_in(v, row_in, up):
    rows = v.shape[0]
    idx = lax.broadcasted_iota(jnp.int32, v.shape, 0)
    fill = jnp.broadcast_to(row_in, v.shape)
    if up:
        return jnp.where(idx == rows - 1, fill, pltpu.roll(v, rows - 1, axis=0))
    return jnp.where(idx == 0, fill, pltpu.roll(v, 1, axis=0))


def _taps_behind(u, before):
    s15 = _shift_in(u[N_RES - 1], before[1, HALO - 1:HALO, :], up=False)
    s14 = _shift_in(u[N_RES - 2], before[0, HALO - 1:HALO, :], up=False)
    m1 = jnp.concatenate([s15[None], u[:N_RES - 1]], axis=0)
    m2 = jnp.concatenate([s14[None], s15[None], u[:N_RES - 2]], axis=0)
    return m1, m2


def _taps_ahead(u, after):
    t0 = _shift_in(u[0], after[0, 0:1, :], up=True)
    t1 = _shift_in(u[1], after[1, 0:1, :], up=True)
    p1 = jnp.concatenate([u[1:], t0[None]], axis=0)
    p2 = jnp.concatenate([u[2:], t0[None], t1[None]], axis=0)
    return p1, p2


def _conv_fwd(gates, before, first, cw):
    gates, before = gates.astype(F32), before.astype(F32)
    bg, cg, xc = gates[..., 0:512], gates[..., 512:1024], gates[..., 1024:1536]
    u = cg * xc
    ub = before[..., 512:1024] * before[..., 1024:1536]
    ub = jnp.where(first, jnp.zeros_like(ub), ub)
    m1, m2 = _taps_behind(u, ub)
    conv = m2 * cw[0:1, :] + m1 * cw[1:2, :] + u * cw[2:3, :]
    return bg, u, m1, m2, conv


def _sum_tokens(v):
    return jnp.sum(jnp.sum(v, axis=0), axis=0, keepdims=True)


def _mixer_fwd(x, attn, gates, cw, g_a, g_c, w_out):
    t, d = x.shape
    nblk, main, before, _ = _order_specs(t)
    rows = N_RES * TI

    def body(x_ref, at_ref, gt_ref, gb_ref, cw_ref, ga_ref, gc_ref, wa_ref, wb_ref, x1_ref, mg_ref):
        an = _rms_fwd(at_ref[0], ga_ref[...])[0].astype(BF16)
        bg, _, _, _, conv = _conv_fwd(gt_ref[0], gb_ref[0], pl.program_id(0) == 0, cw_ref[...])
        cn = _rms_fwd(bg * conv, gc_ref[...])[0].astype(BF16)
        mg_ref[0, :, :, 0:512] = an
        mg_ref[0, :, :, 512:1024] = cn
        y = _dot(an.reshape(rows, 512), wa_ref[...]) + _dot(cn.reshape(rows, 512), wb_ref[...])
        x1_ref[0] = x_ref[0] + y.reshape(N_RES, TI, d)

    const = lambda r, c, i0=0: pl.BlockSpec((r, c), lambda s: (i0, 0))
    x1, merged = pl.pallas_call(
        body, name="mixer_fwd", grid=(nblk,),
        in_specs=[main(d), main(512), main(1536), before(1536), const(3, 512), const(1, 512), const(1, 512),
                  const(512, d), const(512, d, 1)],
        out_specs=[main(d), main(d)],
        out_shape=[jax.ShapeDtypeStruct(_x4(x).shape, F32), jax.ShapeDtypeStruct(_x4(x).shape, BF16)],
        compiler_params=_params(("parallel",), 48),
    )(_x4(x), _x4(attn), _x4(gates), _x4(gates), cw, g_a, g_c, w_out, w_out)
    return x1.reshape(t, d), merged.reshape(t, d)


def _mixer_bwd(dx1, attn, gates, cw, g_a, g_c, w_out, head_sum, after=()):
    t, d = dx1.shape
    nblk, main, before, _ = _order_specs(t)
    rows = N_RES * TI

    def body(dx_ref, at_ref, gt_ref, gb_ref, cw_ref, ga_ref, gc_ref, wa_ref, wb_ref, hs_ref,
             da_ref, dsum_ref, dy_ref, gga_ref, ggc_ref):
        s = pl.program_id(0)
        dxb = dx_ref[0].reshape(rows, d).astype(BF16)
        dma = _dot_nt(dxb, wa_ref[...]).reshape(N_RES, TI, 512)
        dmc = _dot_nt(dxb, wb_ref[...]).reshape(N_RES, TI, 512)
        attn_v, g_av = at_ref[0], ga_ref[...]
        _, ah, ra = _rms_fwd(attn_v, g_av)
        dattn = _rms_bwd(dma, ah, ra, g_av)
        da_ref[0] = dattn
        z = (dattn * attn_v).reshape(rows, 512)
        hs = hs_ref[...]
        z1 = z.astype(BF16)
        z2 = (z - z1.astype(F32)).astype(BF16)
        dsum_ref[0] = (_dot(z1, hs) + _dot(z2, hs)).reshape(N_RES, TI, 512)
        bg, _, _, _, conv = _conv_fwd(gt_ref[0], gb_ref[0], s == 0, cw_ref[...])
        g_cv = gc_ref[...]
        _, yh, rc = _rms_fwd(bg * conv, g_cv)
        dy_ref[0] = _rms_bwd(dmc, yh, rc, g_cv)
        pa, pc = _sum_tokens(dma * ah), _sum_tokens(dmc * yh)

        @pl.when(s == 0)
        def _():
            gga_ref[...] = pa
            ggc_ref[...] = pc

        @pl.when(s != 0)
        def _():
            gga_ref[...] += pa
            ggc_ref[...] += pc

    const = lambda r, c, i0=0: pl.BlockSpec((r, c), lambda s: (i0, 0))
    shape4 = _x4(attn).shape
    res, _ = _pcall(
        body, name="mixer_bwd", grid=(nblk,),
        in_specs=[main(d), main(512), main(1536), before(1536), const(3, 512), const(1, 512), const(1, 512),
                  const(512, d), const(512, d, 1), const(512, 512)],
        out_specs=[main(512)] * 3 + [const(1, 512), const(1, 512)],
        out_shape=[jax.ShapeDtypeStruct(shape4, F32)] * 3 + [jax.ShapeDtypeStruct((1, 512), F32)] * 2,
        semantics=("arbitrary",), vmem_mb=48, after=after,
    )(_x4(dx1), _x4(attn), _x4(gates), _x4(gates), cw, g_a, g_c, w_out, w_out, head_sum)
    return [r.reshape(t, 512) for r in res[:3]] + res[3:]


def _conv_bwd(dy, gates, cw):
    t = dy.shape[0]
    nblk, main, before, after = _order_specs(t)
    n_i = SEG // TI

    def body(dy_ref, dya_ref, gt_ref, gb_ref, ga_ref, cw_ref, dp_ref, gcw_ref):
        s = pl.program_id(0)
        cw_v, gates_v = cw_ref[...], gt_ref[0]
        bg, u, m1, m2, conv = _conv_fwd(gates_v, gb_ref[0], s == 0, cw_v)
        dy_v = dy_ref[0]
        dconv = dy_v * bg
        dca = dya_ref[0] * ga_ref[0][..., 0:512].astype(F32)
        dca = jnp.where(s == nblk - 1, jnp.zeros_like(dca), dca)
        p1, p2 = _taps_ahead(dconv, dca)
        du = dconv * cw_v[2:3, :] + p1 * cw_v[1:2, :] + p2 * cw_v[0:1, :]
        dp_ref[0, 0] = (dy_v * conv).astype(BF16)
        dp_ref[1, 0] = (du * gates_v[..., 1024:1536].astype(F32)).astype(BF16)
        dp_ref[2, 0] = (du * gates_v[..., 512:1024].astype(F32)).astype(BF16)
        parts = [_sum_tokens(dconv * m2), _sum_tokens(dconv * m1), _sum_tokens(dconv * u)]

        @pl.when(s == 0)
        def _():
            gcw_ref[...] = jnp.zeros_like(gcw_ref)

        for tap in range(3):
            gcw_ref[tap:tap + 1, :] += parts[tap]

    (dproj, gcw), _ = _pcall(
        body, name="conv_bwd", grid=(nblk,),
        in_specs=[main(512), after(512), main(1536), before(1536), after(1536),
                  pl.BlockSpec((3, 512), lambda s: (0, 0))],
        out_specs=[pl.BlockSpec((3, 1, N_RES, TI, 512), lambda s: (1, s // n_i, 0, s % n_i, 0)),
                   pl.BlockSpec((8, 512), lambda s: (0, 0))],
        out_shape=[jax.ShapeDtypeStruct((6, t // HALF, N_RES, SEG, 512), BF16), jax.ShapeDtypeStruct((8, 512), F32)],
        semantics=("arbitrary",), vmem_mb=40,
    )(_x4(dy), _x4(dy), _x4(gates), _x4(gates), _x4(gates), cw)
    return dproj.reshape(6, t, 512), gcw


def _xattn_fwd(x1, g, w_q, kv, w_o, *, tb):
    t, d = x1.shape
    hd = d // N_MEM_HEADS
    m = kv.shape[0]

    def body(x_ref, g_ref, wq_ref, k_ref, v_ref, wo_ref, x2_ref, h_ref, q_ref, o_ref):
        xv = x_ref[...]
        h = _rms_fwd(xv, g_ref[...])[0].astype(BF16)
        h_ref[...] = h
        q = _dot(h, wq_ref[...]).astype(BF16)
        q_ref[...] = q
        for hh in range(N_MEM_HEADS):
            sl = slice(hh * hd, (hh + 1) * hd)
            s = _dot_nt(q[:, sl], k_ref[:, sl]) * (1.0 / 16.0)
            e = jnp.exp(s - jnp.max(s, axis=1, keepdims=True))
            p = e / jnp.sum(e, axis=1, keepdims=True)
            o_ref[:, sl] = _dot(p.astype(BF16), v_ref[:, sl]).astype(BF16)
        x2_ref[...] = xv + _dot(o_ref[...], wo_ref[...])

    tok = pl.BlockSpec((tb, d), lambda i: (i, 0))
    full = pl.BlockSpec((d, d), lambda i: (0, 0))
    return pl.pallas_call(
        body, name="xattn_fwd", grid=(t // tb,),
        in_specs=[tok, pl.BlockSpec((1, d), lambda i: (0, 0)), full,
                  pl.BlockSpec((m, d), lambda i: (0, 0)), pl.BlockSpec((m, d), lambda i: (0, 1)), full],
        out_specs=[tok] * 4,
        out_shape=[jax.ShapeDtypeStruct((t, d), F32)] + [jax.ShapeDtypeStruct((t, d), BF16)] * 3,
        compiler_params=_params(("parallel",), 48),
    )(x1, g, w_q, kv, kv, w_o)


def _xattn_bwd(dx2, x1, g, q, w_q, kv, w_o, *, tb, after=()):
    t, d = x1.shape
    hd = d // N_MEM_HEADS
    m = kv.shape[0]

    def body(dx2_ref, x_ref, g_ref, q_ref, wq_ref, k_ref, v_ref, wo_ref,
             dx1_ref, dx1b_ref, dq_ref, dk_ref, dv_ref, gg_ref):
        i = pl.program_id(0)

        @pl.when(i == 0)
        def _():
            dk_ref[...] = jnp.zeros_like(dk_ref)
            dv_ref[...] = jnp.zeros_like(dv_ref)

        dx2 = dx2_ref[...]
        do = _dot_nt(dx2.astype(BF16), wo_ref[...]).astype(BF16)
        for hh in range(N_MEM_HEADS):
            sl = slice(hh * hd, (hh + 1) * hd)
            qh, kh, vh, doh = q_ref[:, sl], k_ref[:, sl], v_ref[:, sl], do[:, sl]
            s = _dot_nt(qh, kh) * (1.0 / 16.0)
            e = jnp.exp(s - jnp.max(s, axis=1, keepdims=True))
            p = e / jnp.sum(e, axis=1, keepdims=True)
            dp = _dot_nt(doh, vh)
            ds = (p * (dp - jnp.sum(dp * p, axis=1, keepdims=True)) * (1.0 / 16.0)).astype(BF16)
            dq_ref[:, sl] = _dot(ds, kh).astype(BF16)
            dk_ref[:, sl] += _dot_tn(ds, qh)
            dv_ref[:, sl] += _dot_tn(p.astype(BF16), doh)
        dh = _dot_nt(dq_ref[...], wq_ref[...])
        g_v = g_ref[...]
        _, xh, r = _rms_fwd(x_ref[...], g_v)
        dx1 = dx2 + _rms_bwd(dh, xh, r, g_v)
        dx1_ref[...] = dx1
        dx1b_ref[...] = dx1.astype(BF16)
        part = jnp.sum(dh * xh, axis=0, keepdims=True)

        @pl.when(i == 0)
        def _():
            gg_ref[...] = part

        @pl.when(i != 0)
        def _():
            gg_ref[...] += part

    tok = pl.BlockSpec((tb, d), lambda i: (i, 0))
    full = pl.BlockSpec((d, d), lambda i: (0, 0))
    acc = pl.BlockSpec((m, d), lambda i: (0, 0))
    res, _ = _pcall(
        body, name="xattn_bwd", grid=(t // tb,),
        in_specs=[tok, tok, pl.BlockSpec((1, d), lambda i: (0, 0)), tok, full,
                  pl.BlockSpec((m, d), lambda i: (0, 0)), pl.BlockSpec((m, d), lambda i: (0, 1)), full],
        out_specs=[tok, tok, tok, acc, acc, pl.BlockSpec((1, d), lambda i: (0, 0))],
        out_shape=[jax.ShapeDtypeStruct((t, d), F32), jax.ShapeDtypeStruct((t, d), BF16),
                   jax.ShapeDtypeStruct((t, d), BF16),
                   jax.ShapeDtypeStruct((m, d), F32), jax.ShapeDtypeStruct((m, d), F32),
                   jax.ShapeDtypeStruct((1, d), F32)],
        semantics=("arbitrary",), vmem_mb=48, after=after,
    )(dx2, x1, g, q, w_q, kv, kv, w_o)
    return res


def _mlp_down_loss(a, w_down, x2, tgt, g, *, tb):
    t, d = x2.shape
    f = a.shape[1]

    def body(a_ref, w_ref, x_ref, t_ref, g_ref, dx_ref, dxb_ref, loss_ref, gg_ref):
        i = pl.program_id(0)
        av = a_ref[...].astype(F32)
        x3 = x_ref[...] + _dot((av * av).astype(BF16), w_ref[...])
        g_v = g_ref[...]
        out, xh, r = _rms_fwd(x3, g_v)
        err = out - t_ref[...]
        dout = err * (1.0 / d)
        dx = _rms_bwd(dout, xh, r, g_v)
        dx_ref[...] = dx
        dxb_ref[...] = dx.astype(BF16)
        part = jnp.sum(dout * xh, axis=0, keepdims=True)
        lpart = 0.5 * jnp.sum(jnp.mean(err * err, axis=-1, keepdims=True), axis=0, keepdims=True)
        lpart = jnp.broadcast_to(lpart, loss_ref.shape)

        @pl.when(i == 0)
        def _():
            gg_ref[...] = part
            loss_ref[...] = lpart

        @pl.when(i != 0)
        def _():
            gg_ref[...] += part
            loss_ref[...] += lpart

    tok = pl.BlockSpec((tb, d), lambda i: (i, 0))
    return pl.pallas_call(
        body, name="mlp_down_loss", grid=(t // tb,),
        in_specs=[pl.BlockSpec((tb, f), lambda i: (i, 0)), pl.BlockSpec((f, d), lambda i: (0, 0)), tok, tok,
                  pl.BlockSpec((1, d), lambda i: (0, 0))],
        out_specs=[tok, tok, pl.BlockSpec((8, 128), lambda i: (0, 0)), pl.BlockSpec((1, d), lambda i: (0, 0))],
        out_shape=[jax.ShapeDtypeStruct((t, d), F32), jax.ShapeDtypeStruct((t, d), BF16),
                   jax.ShapeDtypeStruct((8, 128), F32), jax.ShapeDtypeStruct((1, d), F32)],
        compiler_params=_params(("arbitrary",), 56),
    )(a, w_down, x2, tgt, g)


def _mlp_dpre(dx3, w_down, a, *, tb, bn):
    t, d = dx3.shape
    f = a.shape[1]

    def body(dx_ref, w_ref, a_ref, o_ref):
        o_ref[...] = (2.0 * a_ref[...].astype(F32) * _dot_nt(dx_ref[...], w_ref[...])).astype(BF16)

    return pl.pallas_call(
        body, name="mlp_dpre", grid=(t // tb, f // bn),
        in_specs=[pl.BlockSpec((tb, d), lambda i, j: (i, 0)), pl.BlockSpec((bn, d), lambda i, j: (j, 0)),
                  pl.BlockSpec((tb, bn), lambda i, j: (i, j))],
        out_specs=pl.BlockSpec((tb, bn), lambda i, j: (i, j)),
        out_shape=jax.ShapeDtypeStruct((t, f), BF16),
        compiler_params=_params(("parallel", "arbitrary"), 48),
    )(dx3, w_down, a)


def _adamw(gsum, w, m, v):
    m_new = ADAM_B1 * m + (1.0 - ADAM_B1) * gsum
    v_new = ADAM_B2 * v + (1.0 - ADAM_B2) * (gsum * gsum)
    m_hat = m_new / (1.0 - ADAM_B1 ** ADAM_STEP)
    v_hat = v_new / (1.0 - ADAM_B2 ** ADAM_STEP)
    delta = -ADAM_LR * (m_hat / (jnp.sqrt(v_hat) + ADAM_EPS) + ADAM_WD * w)
    return delta, m_new, v_new


def _sum_adamw(parts, w, m, v, *, name, tr):
    r, c = w.shape

    def body(p_ref, w_ref, m_ref, v_ref, g_ref, d_ref, mo_ref, vo_ref):
        g = p_ref[0].astype(F32)
        for k in range(1, N_DEV):
            g = g + p_ref[k].astype(F32)
        g_ref[...] = g
        d_ref[...], mo_ref[...], vo_ref[...] = _adamw(g, w_ref[...], m_ref[...], v_ref[...])

    blk = pl.BlockSpec((tr, c), lambda i: (i, 0))
    return pl.pallas_call(
        body, name=name, grid=(r // tr,),
        in_specs=[pl.BlockSpec((N_DEV, tr, c), lambda i: (0, i, 0)), blk, blk, blk],
        out_specs=[blk] * 4, out_shape=[jax.ShapeDtypeStruct((r, c), F32)] * 4,
        compiler_params=_params(("parallel",), 40),
    )(*[pltpu.with_memory_space_constraint(a, pltpu.HBM) for a in (parts, w, m, v)])


def _sum_small(parts):
    _, r, c = parts.shape

    def body(p_ref, o_ref):
        s = p_ref[0]
        for k in range(1, N_DEV):
            s = s + p_ref[k]
        o_ref[...] = s

    return pl.pallas_call(body, name="sum_small", out_shape=jax.ShapeDtypeStruct((r, c), F32))(parts)


def _adamw_small(g, w, m, v):
    def body(g_ref, w_ref, m_ref, v_ref, d_ref, mo_ref, vo_ref):
        d_ref[...], mo_ref[...], vo_ref[...] = _adamw(g_ref[...], w_ref[...], m_ref[...], v_ref[...])

    return pl.pallas_call(body, name="adamw_small", out_shape=[jax.ShapeDtypeStruct(g.shape, F32)] * 3)(g, w, m, v)


def _head_sum_matrix():
    r = lax.broadcasted_iota(jnp.int32, (512, 512), 0) // HEAD_DIM
    c = lax.broadcasted_iota(jnp.int32, (512, 512), 1) // HEAD_DIM
    return (r == c).astype(BF16)


_SHARD_AXIS = dict(w_in=1, w_out=0, w_q=0, w_kv=1, w_o=0, w_up=1, w_down=0, conv_w=None, small=None)


class _Weights:
    def __init__(self, full, shards=None):
        self.full = dict(full)
        self.shards = shards

    def rider(self, names, late=False):
        if self.shards is None:
            return None
        return _Gather([self.shards[n] for n in names], [_SHARD_AXIS[n] for n in names], late)

    def arrived(self, names, gathered):
        if gathered is not None:
            for n, g in zip(names, gathered):
                self.full[n] = g.transpose(1, 0, 2).reshape(g.shape[1], -1) if n == "conv_w" else g

    def __getitem__(self, name):
        return self.full[name]


class _Grads:
    def __init__(self, distributed):
        self.distributed = distributed
        self.local = {}
        self.pending = {}

    def add(self, name, g):
        self.local[name] = g

    def send(self, *names):
        if not self.distributed:
            return []
        rider = _Exchange([self.local[n] for n in names], [_SHARD_AXIS[n] for n in names])
        started = _exchange_start(rider, "send_" + "_".join(names))
        self.pending[names[0]] = (names, rider, started)
        return [started[3]]

    def wait(self, first_name, after):
        names, rider, started = self.pending.pop(first_name)
        return _exchange_wait(rider, started, after, "wait_" + "_".join(names))


def _ride(fn, *args, rider=None, **kw):
    if rider is None:
        return fn(*args, **kw), None
    return fn(*args, rider=rider, **kw)


def _local_step(x, mem, tgt, gains, weights, grads):
    names = ["w_in", "conv_w"]
    (x, tgt), got = _ride(_reorder, [x, tgt], "reorder_in", rider=weights.rider(names, late=True))
    weights.arrived(names, got)
    w_in, cw = weights["w_in"], weights["conv_w"]

    names = ["w_out", "w_kv"]
    (qkv, gates, h1), got = _ride(_proj, x, gains["g_mix"], w_in, tb=1024, rider=weights.rider(names, late=True))
    weights.arrived(names, got)
    names = ["w_q", "w_o", "w_up"]
    (attn, *lses), got = _ride(_attention_fwd, qkv, rider=weights.rider(names))
    weights.arrived(names, got)
    x1, merged = _mixer_fwd(x, attn, gates, cw, gains["g_attn_out"], gains["g_conv_out"], weights["w_out"])
    kv, mem_n = _norm_matmul(mem, gains["g_mem"], weights["w_kv"], name="mem_kv", out_dtype=BF16, tb=mem.shape[0],
                             bn=1024, save_h=True)
    x2, h2, qm, om = _xattn_fwd(x1, gains["g_xattn"], weights["w_q"], kv, weights["w_o"], tb=512)
    w_up = weights["w_up"]
    (a, h3), got = _ride(_norm_matmul, x2, gains["g_mlp"], w_up, name="mlp_up", out_dtype=BF16, tb=1024, bn=2048,
                         relu=True, save_h=True, rider=weights.rider(["w_down"], late=True))
    weights.arrived(["w_down"], got)
    w_down = weights["w_down"]
    dx3, dx3b, loss_blk, gg_final = _mlp_down_loss(a, w_down, x2, tgt, gains["g_final"], tb=512)

    dpre = _mlp_dpre(dx3b, w_down, a, tb=1024, bn=2048)
    grads.add("w_down", _matmul_tn(a, dx3b, name="grad_w_down", bm=512, bn=1024, square_a=True))
    sent = grads.send("w_down")
    grads.add("w_up", _matmul_tn(h3, dpre, name="grad_w_up", bm=1024, bn=1024, after=sent))
    sent = grads.send("w_up")
    dx2, dx2b, gg_mlp = _matmul_nt_normbwd(dpre, w_up, x2, gains["g_mlp"], dx3, name="mlp_dx", tb=512,
                                           also_bf16=True, after=sent)

    grads.add("w_o", _matmul_tn(om, dx2b, name="grad_w_o", bm=512, bn=512))
    dx1, dx1b, dqm, dk, dv, gg_xattn = _xattn_bwd(dx2, x1, gains["g_xattn"], qm, weights["w_q"], kv, weights["w_o"],
                                                  tb=512)
    grads.add("w_q", _matmul_tn(h2, dqm, name="grad_w_q", bm=1024, bn=512))
    dkv = jnp.concatenate([dk, dv], axis=1).astype(BF16)
    grads.add("w_kv", _matmul_tn(mem_n, dkv, name="grad_w_kv", bm=1024, bn=1024))
    _, gg_mem = _matmul_nt_normbwd(dkv, weights["w_kv"], mem, gains["g_mem"], None, name="mem_dx", tb=mem.shape[0])

    grads.add("w_out", _matmul_tn(merged, dx1b, name="grad_w_out", bm=1024, bn=512))
    sent = grads.send("w_o", "w_q", "w_kv", "w_out")
    dattn, dsum, dy, gg_attn, gg_conv = _mixer_bwd(dx1, attn, gates, cw, gains["g_attn_out"], gains["g_conv_out"],
                                                   weights["w_out"], _head_sum_matrix(), after=sent)
    dproj, gcw = _conv_bwd(dy, gates, cw)
    dproj = _attention_bwd(qkv, dattn, dsum, lses, dproj)
    grads.add("w_in", _matmul_tn(h1, dproj, name="grad_w_in", bm=1024, bn=512))
    sent = grads.send("w_in")
    grad_x, gg_mix = _matmul_nt_normbwd(dproj, w_in, x, gains["g_mix"], dx1, name="mixer_dx", tb=512,
                                        to_natural=True, after=sent)

    def part(v):
        return jnp.pad(v, ((0, SMALL_PART - v.shape[0]), (0, 1024 - v.shape[1])))

    parts = [gg_mix, gg_xattn, gg_mem, gg_mlp, gg_final, jnp.concatenate([gg_attn, gg_conv], axis=1), gcw, loss_blk]
    grads.add("small", jnp.concatenate([part(v) for v in parts], axis=0))
    return grad_x


SMALL_PART = 8
_BIG = ("w_in", "w_out", "w_q", "w_kv", "w_o", "w_up", "w_down")
_GAIN_ROWS = ("g_mix", "g_xattn", "g_mem", "g_mlp", "g_final")


def _pack_small(vals, conv):
    rows = [vals[k].reshape(1, -1) for k in _GAIN_ROWS]
    rows.append(jnp.concatenate([vals["g_attn_out"].reshape(1, -1), vals["g_conv_out"].reshape(1, -1)], axis=1))
    flat = conv.reshape(1, -1)
    rows.append(jnp.pad(flat, ((0, 0), (0, 1024 - flat.shape[1]))))
    rows.append(jnp.zeros((1, 1024), F32))
    return jnp.concatenate(rows, axis=0)


def kernel(x, mem, g_mix, w_in, conv_w, g_attn_out, g_conv_out, w_out, g_xattn, g_mem, w_q_mem, w_kv_mem, w_o_mem, g_mlp, w_up, w_down, g_final, loss_target, m_g_mix, m_w_in, m_conv_w, m_g_attn_out, m_g_conv_out, m_w_out, m_g_xattn, m_g_mem, m_w_q_mem, m_w_kv_mem, m_w_o_mem, m_g_mlp, m_w_up, m_w_down, m_g_final, v_g_mix, v_w_in, v_conv_w, v_g_attn_out, v_g_conv_out, v_w_out, v_g_xattn, v_g_mem, v_w_q_mem, v_w_kv_mem, v_w_o_mem, v_g_mlp, v_w_up, v_w_down, v_g_final):
    d = x.shape[-1]
    me = 4 * lax.axis_index("x") + 2 * lax.axis_index("y") + lax.axis_index("c")
    w_shards = dict(w_in=w_in, w_out=w_out, w_q=w_q_mem, w_kv=w_kv_mem, w_o=w_o_mem, w_up=w_up, w_down=w_down)
    m_shards = dict(w_in=m_w_in, w_out=m_w_out, w_q=m_w_q_mem, w_kv=m_w_kv_mem, w_o=m_w_o_mem, w_up=m_w_up,
                    w_down=m_w_down)
    v_shards = dict(w_in=v_w_in, w_out=v_w_out, w_q=v_w_q_mem, w_kv=v_w_kv_mem, w_o=v_w_o_mem, w_up=v_w_up,
                    w_down=v_w_down)
    gains = dict(g_mix=g_mix, g_attn_out=g_attn_out, g_conv_out=g_conv_out, g_xattn=g_xattn, g_mem=g_mem,
                 g_mlp=g_mlp, g_final=g_final)
    gains2 = {k: v.reshape(1, -1) for k, v in gains.items()}

    shards = {k: w_shards[k].astype(BF16) for k in _BIG}
    shards["conv_w"] = conv_w
    grads = _Grads(distributed=True)
    grad_x = _local_step(x[0], mem[0], loss_target[0], gains2, _Weights({}, shards), grads)

    after = grads.send("small")
    outs = {}
    tiles = dict(w_in=256, w_out=128, w_q=128, w_kv=256, w_o=128, w_up=256, w_down=256)
    for group in (("w_down",), ("w_up",), ("w_o", "w_q", "w_kv", "w_out"), ("w_in",)):
        for k, received in zip(group, grads.wait(group[0], after)):
            outs[k] = _sum_adamw(received, w_shards[k], m_shards[k], v_shards[k], name=f"adamw_{k}", tr=tiles[k])
            after = [outs[k][0]]
    small_received, = grads.wait("small", after)

    ssum = _sum_small(small_received)
    row = lambda i: ssum[SMALL_PART * i]
    loss = ssum[SMALL_PART * 7, 0]
    g_small = {k: row(i) for i, k in enumerate(_GAIN_ROWS)}
    g_small["g_attn_out"] = row(5)[0:512]
    g_small["g_conv_out"] = row(5)[512:1024]
    taps = ssum[SMALL_PART * 6:SMALL_PART * 6 + 3, 0:512]
    g_conv = lax.dynamic_slice_in_dim(taps, me * 64, 64, axis=1)
    m_small = dict(g_mix=m_g_mix, g_attn_out=m_g_attn_out, g_conv_out=m_g_conv_out, g_xattn=m_g_xattn,
                   g_mem=m_g_mem, g_mlp=m_g_mlp, g_final=m_g_final)
    v_small = dict(g_mix=v_g_mix, g_attn_out=v_g_attn_out, g_conv_out=v_g_conv_out, g_xattn=v_g_xattn,
                   g_mem=v_g_mem, g_mlp=v_g_mlp, g_final=v_g_final)
    packed = [_pack_small(g_small, g_conv), _pack_small(gains, conv_w), _pack_small(m_small, m_conv_w),
              _pack_small(v_small, v_conv_w)]
    upd = _adamw_small(*packed)

    def unpack(p):
        res = {k: p[i] for i, k in enumerate(_GAIN_ROWS)}
        res["g_attn_out"] = p[5, 0:512]
        res["g_conv_out"] = p[5, 512:1024]
        res["conv_w"] = p[6, 0:192].reshape(3, 64)
        return res

    g_small["conv_w"] = g_conv
    small_out = [g_small] + [unpack(p) for p in upd]
    names = {"g_mix": "g_mix", "w_in": "w_in", "conv_w": "conv_w", "g_attn_out": "g_attn_out",
             "g_conv_out": "g_conv_out", "w_out": "w_out", "g_xattn": "g_xattn", "g_mem": "g_mem",
             "w_q_mem": "w_q", "w_kv_mem": "w_kv", "w_o_mem": "w_o", "g_mlp": "g_mlp", "w_up": "w_up",
             "w_down": "w_down", "g_final": "g_final"}
    result = [loss, grad_x[None]]
    for which in range(4):
        for key in names.values():
            result.append(outs[key][which] if key in outs else small_out[which][key])
    return tuple(result)
```

```python
import math

import jax
import jax.numpy as jnp
from jax import lax
from jax.experimental import pallas as pl
from jax.experimental.pallas import tpu as pltpu

F32 = jnp.float32
BF16 = jnp.bfloat16
NORM_EPS = 1e-6
NEG_INF = -1e30
N_DEV = 8
BLK = 128
HEAD_DIM = 64
N_MEM_HEADS = 4
ADAM_LR = 0.001
ADAM_B1 = 0.9
ADAM_B2 = 0.999
ADAM_EPS = 1e-08
ADAM_WD = 0.01
ADAM_STEP = 10
MESH = pl.DeviceIdType.MESH
ANY = pl.BlockSpec(memory_space=pl.ANY)


def _dot(a, b):
    return jnp.dot(a, b, preferred_element_type=F32)


def _dot_nt(a, b):
    return lax.dot_general(a, b, (((1,), (1,)), ((), ())), preferred_element_type=F32)


def _dot_tn(a, b):
    return lax.dot_general(a, b, (((0,), (0,)), ((), ())), preferred_element_type=F32)


def _params(semantics, vmem_mb):
    return pltpu.CompilerParams(dimension_semantics=semantics, vmem_limit_bytes=vmem_mb << 20)


def _rms_fwd(x, g):
    r = lax.rsqrt(jnp.mean(x * x, axis=-1, keepdims=True) + NORM_EPS)
    xh = x * r
    return xh * g, xh, r


def _rms_bwd(dy, xh, r, g):
    gy = dy * g
    return r * (gy - xh * jnp.mean(xh * gy, axis=-1, keepdims=True))


def _position():
    x, y, c = lax.axis_index("x"), lax.axis_index("y"), lax.axis_index("c")
    return x, y, c


def _block_of(ref, j, axis, shard_shape):
    r, c = shard_shape
    if axis is None:
        return ref.at[j]
    if axis == 0:
        return ref.at[pl.ds(j * r, r), :]
    return ref.at[:, pl.ds(j * c, c)]


class _Gather:
    has_mid = True
    alias_pairs = ()

    def __init__(self, shards, axes, late=False):
        self.arrays = list(shards)
        self.axes = list(axes)
        self.late = late
        self.n = len(self.arrays)

    def out_shape(self):
        res = []
        for s, axis in zip(self.arrays, self.axes):
            r, c = s.shape
            shape = (N_DEV, r, c) if axis is None else (N_DEV * r, c) if axis == 0 else (r, N_DEV * c)
            res.append(jax.ShapeDtypeStruct(shape, s.dtype))
        return res

    def scratch(self):
        return [pltpu.SemaphoreType.DMA((self.n, 7)), pltpu.SemaphoreType.DMA((self.n, 7)),
                pltpu.SemaphoreType.DMA((self.n,))]

    def _ctx(self, ins, outs, sems):
        send_sems, recv_sems, local_sems = sems
        x, y, c = _position()
        me, sibling = (x, y, c), (x, y, 1 - c)
        chips = [(1 - x, y), (x, 1 - y), (1 - x, 1 - y)]

        def lin(px, py, pc):
            return 4 * px + 2 * py + pc

        def place(a, block):
            return _block_of(outs[a], lin(*block), self.axes[a], self.arrays[a].shape)

        def copy(a, k, block, to, src=None):
            dst = place(a, block)
            return pltpu.make_async_remote_copy(
                src_ref=dst if src is None else src, dst_ref=dst,
                send_sem=send_sems.at[a, k], recv_sem=recv_sems.at[a, k],
                device_id=to, device_id_type=MESH)

        def mine():
            return [pltpu.make_async_copy(ins[a], place(a, me), local_sems.at[a]) for a in range(self.n)]

        def first():
            res = []
            for a in range(self.n):
                res.append(copy(a, 0, me, sibling, src=ins[a]))
                res += [copy(a, 1 + j, me, (*chip, c), src=ins[a]) for j, chip in enumerate(chips)]
            return res

        return c, me, sibling, chips, copy, mine, first

    def start(self, ins, outs, sems):
        _, _, _, _, _, mine, first = self._ctx(ins, outs, sems)
        for cp in mine() + first():
            cp.start()

    def mid(self, ins, outs, sems):
        c, me, sibling, chips, copy, _, _ = self._ctx(ins, outs, sems)
        for j, chip in enumerate(chips):
            for a in range(self.n):
                copy(a, 1 + j, (*chip, c), me).wait_recv()
                copy(a, 4 + j, (*chip, c), sibling).start()

    def finish(self, ins, outs, sems):
        c, me, sibling, chips, copy, mine, first = self._ctx(ins, outs, sems)
        for a in range(self.n):
            copy(a, 0, sibling, me).wait_recv()
            for j, chip in enumerate(chips):
                copy(a, 4 + j, (*chip, 1 - c), me).wait_recv()
        for cp in first():
            cp.wait_send()
        for j, chip in enumerate(chips):
            for a in range(self.n):
                copy(a, 4 + j, (*chip, c), sibling).wait_send()
        for cp in mine():
            cp.wait()


class _Exchange:
    def __init__(self, parts, axes):
        self.n = len(parts)
        self.axes = list(axes)
        self.arrays = list(parts)

    def _piece(self, a):
        r, c = self.arrays[a].shape
        axis = self.axes[a]
        return (r, c) if axis is None else (r // N_DEV, c) if axis == 0 else (r, c // N_DEV)

    def out_shape(self):
        return [jax.ShapeDtypeStruct((N_DEV,) + self._piece(a), self.arrays[a].dtype) for a in range(self.n)]

    def semaphores(self):
        return [pltpu.SemaphoreType.DMA((7 * self.n,)), pltpu.SemaphoreType.DMA((7 * self.n,)),
                pltpu.SemaphoreType.DMA((self.n,))]

    def _ctx(self, ins, outs, sems):
        send_sems, recv_sems, local_sems = sems
        x, y, c = _position()
        me = 4 * x + 2 * y + c

        def src(a, j):
            return ins[a] if self.axes[a] is None else _block_of(ins[a], j, self.axes[a], self._piece(a))

        def dst(a, j):
            return outs[a].at[j]

        def local():
            return [pltpu.make_async_copy(src(a, me), dst(a, me), local_sems.at[a]) for a in range(self.n)]

        def remote(inbound):
            res = []
            for a in range(self.n):
                for k in range(1, N_DEV):
                    peer = (1 - x if k & 4 else x, 1 - y if k & 2 else y, 1 - c if k & 1 else c)
                    plin = 4 * peer[0] + 2 * peer[1] + peer[2]
                    res.append(pltpu.make_async_remote_copy(
                        src_ref=src(a, plin), dst_ref=dst(a, plin if inbound else me),
                        send_sem=send_sems.at[7 * a + k - 1], recv_sem=recv_sems.at[7 * a + k - 1],
                        device_id=peer, device_id_type=MESH))
            return res

        return local, remote

    def start(self, ins, outs, sems):
        local, remote = self._ctx(ins, outs, sems)
        for cp in local() + remote(False):
            cp.start()

    def finish(self, ins, outs, sems):
        local, remote = self._ctx(ins, outs, sems)
        for cp in remote(True):
            cp.wait_recv()
        for cp in remote(False):
            cp.wait_send()
        for cp in local():
            cp.wait()


def _exchange_start(rider, name):
    n = rider.n
    parts = rider.arrays
    lands = [lax.empty(s.shape, s.dtype) for s in rider.out_shape()]
    hbm = pl.BlockSpec(memory_space=pltpu.HBM)
    sem = pl.BlockSpec(memory_space=pltpu.SEMAPHORE)

    def body(*refs):
        ins, sems = refs[:n], refs[2 * n:2 * n + 3]
        outs, token = refs[2 * n + 3 + n:2 * n + 3 + 2 * n], refs[-1]
        rider.start(ins, outs, sems)
        token[...] = jnp.zeros_like(token)

    res = pl.pallas_call(
        body, name=name,
        out_shape=rider.semaphores() + [pltpu.HBM(p.shape, p.dtype) for p in parts]
                  + [pltpu.HBM(z.shape, z.dtype) for z in lands] + [jax.ShapeDtypeStruct((8, 128), F32)],
        in_specs=[hbm] * (2 * n), out_specs=[sem] * 3 + [hbm] * (2 * n) + [pl.BlockSpec(memory_space=pltpu.VMEM)],
        input_output_aliases={i: 3 + i for i in range(2 * n)},
        compiler_params=pltpu.CompilerParams(has_side_effects=pltpu.SideEffectType.DATAFLOW_SIDE_EFFECTING),
    )(*[pltpu.with_memory_space_constraint(a, pltpu.HBM) for a in parts + lands])
    return res[:3], res[3:3 + n], res[3 + n:3 + 2 * n], res[-1]


def _exchange_wait(rider, started, after, name):
    n = rider.n
    sems, parts, lands, _ = started
    hbm = pl.BlockSpec(memory_space=pltpu.HBM)
    sem = pl.BlockSpec(memory_space=pltpu.SEMAPHORE)

    def body(*refs):
        rider.finish(refs[:n], refs[n:2 * n], refs[2 * n:2 * n + 3])

    res = pl.pallas_call(
        body, name=name, out_shape=[pltpu.HBM(a.shape, a.dtype) for a in list(parts) + list(lands)],
        in_specs=[hbm] * (2 * n) + [sem] * 3 + [ANY] * len(after), out_specs=[hbm] * (2 * n),
        input_output_aliases={i: i for i in range(2 * n)},
        compiler_params=pltpu.CompilerParams(has_side_effects=pltpu.SideEffectType.DATAFLOW_SIDE_EFFECTING),
    )(*parts, *lands, *sems, *after)
    return list(res[n:])


def _pcall(body, *, name, grid, in_specs, out_specs, out_shape, scratch_shapes=(), semantics, vmem_mb, rider=None,
           aliases=None, after=()):
    in_specs, out_specs, out_shape = list(in_specs), list(out_specs), list(out_shape)
    scratch_shapes = list(scratch_shapes)
    aliases = dict(aliases or {})
    if rider is None:
        n_in, after = len(in_specs), list(after)

        def plain(*refs):
            body(*refs[:n_in], *refs[n_in + len(after):])

        call = pl.pallas_call(plain if after else body, name=name, grid=grid, in_specs=in_specs + [ANY] * len(after),
                              out_specs=out_specs, out_shape=out_shape, scratch_shapes=scratch_shapes,
                              input_output_aliases=aliases, compiler_params=_params(semantics, vmem_mb))
        return lambda *args: (list(call(*args, *after)), None)
    n_in, n_out, n_scr = len(in_specs), len(out_specs), len(scratch_shapes)
    r_in, r_shapes = len(rider.arrays), rider.out_shape()
    r_out = len(r_shapes)
    aliases.update({n_in + i: n_out + o for i, o in rider.alias_pairs})
    total = math.prod(grid)
    mid_step = total - 1 if rider.has_mid and rider.late else (3 * total) // 4

    def wrapped(*refs):
        bounds = [0, n_in, r_in, n_out, r_out, n_scr]
        for i in range(1, len(bounds)):
            bounds[i] += bounds[i - 1]
        a, ra, o, ro, s = (refs[bounds[i]:bounds[i + 1]] for i in range(5))
        rs = refs[bounds[5]:]
        step = pl.program_id(0)
        for k in range(1, len(grid)):
            step = step * grid[k] + pl.program_id(k)
        pl.when(step == 0)(lambda: rider.start(ra, ro, rs))
        body(*a, *o, *s)
        if rider.has_mid:
            pl.when(step == mid_step)(lambda: rider.mid(ra, ro, rs))
        pl.when(step == total - 1)(lambda: rider.finish(ra, ro, rs))

    call = pl.pallas_call(
        wrapped, name=name, grid=grid, in_specs=in_specs + [ANY] * r_in, out_specs=out_specs + [ANY] * r_out,
        out_shape=out_shape + r_shapes, scratch_shapes=scratch_shapes + rider.scratch(),
        input_output_aliases=aliases, compiler_params=_params(("arbitrary",) * len(grid), vmem_mb))

    def run(*args):
        res = call(*args, *rider.arrays)
        return list(res[:n_out]), list(res[n_out:])

    return run


def _norm_matmul(x, g, w, *, name, out_dtype, tb, bn, relu=False, save_h=False, rider=None):
    t, d = x.shape
    n = w.shape[1]

    def body(x_ref, g_ref, w_ref, o_ref, *rest):
        h_scr = rest[-1]

        @pl.when(pl.program_id(1) == 0)
        def _():
            h = _rms_fwd(x_ref[...], g_ref[...])[0].astype(BF16)
            h_scr[...] = h
            if save_h:
                rest[0][...] = h

        acc = _dot(h_scr[...], w_ref[...])
        if relu:
            acc = jnp.maximum(acc, 0.0)
        o_ref[...] = acc.astype(out_dtype)

    out_shape = [jax.ShapeDtypeStruct((t, n), out_dtype)]
    out_specs = [pl.BlockSpec((tb, bn), lambda i, j: (i, j))]
    if save_h:
        out_shape.append(jax.ShapeDtypeStruct((t, d), BF16))
        out_specs.append(pl.BlockSpec((tb, d), lambda i, j: (i, 0)))
    res, extra = _pcall(
        body, name=name, grid=(t // tb, n // bn),
        in_specs=[pl.BlockSpec((tb, d), lambda i, j: (i, 0)),
                  pl.BlockSpec((1, d), lambda i, j: (0, 0)),
                  pl.BlockSpec((d, bn), lambda i, j: (0, j))],
        out_specs=out_specs, out_shape=out_shape,
        scratch_shapes=[pltpu.VMEM((tb, d), BF16)],
        semantics=("parallel", "arbitrary"), vmem_mb=48, rider=rider,
    )(x, g, w)
    res = res if save_h else res[0]
    return res if rider is None else (res, extra)


def _proj(x, g, w, *, tb, rider=None):
    t, d = x.shape
    half = w.shape[1] // 2

    def body(x_ref, g_ref, w_ref, qkv_ref, gates_ref, h_ref, h_scr):
        j = pl.program_id(1)

        @pl.when(j == 0)
        def _():
            h = _rms_fwd(x_ref[...], g_ref[...])[0].astype(BF16)
            h_scr[...] = h
            h_ref[...] = h

        acc = _dot(h_scr[...], w_ref[...])

        @pl.when(j == 0)
        def _():
            qkv_ref[...] = acc

        @pl.when(j == 1)
        def _():
            gates_ref[...] = acc.astype(BF16)

    tok = lambda c: pl.BlockSpec((tb, c), lambda i, j: (i, 0))
    res, extra = _pcall(
        body, name="proj", grid=(t // tb, 2),
        in_specs=[tok(d), pl.BlockSpec((1, d), lambda i, j: (0, 0)), pl.BlockSpec((d, half), lambda i, j: (0, j))],
        out_specs=[tok(half), tok(half), tok(d)],
        out_shape=[jax.ShapeDtypeStruct((t, half), F32), jax.ShapeDtypeStruct((t, half), BF16),
                   jax.ShapeDtypeStruct((t, d), BF16)],
        scratch_shapes=[pltpu.VMEM((tb, d), BF16)],
        semantics=("parallel", "arbitrary"), vmem_mb=48, rider=rider,
    )(x, g, w)
    return res if rider is None else (res, extra)


def _matmul_nt_normbwd(dy, w, x, g, dres, *, name, tb, also_bf16=False, to_natural=False, after=()):
    t, d = x.shape
    stacked = dy.ndim == 3
    has_res = dres is not None
    n_i = SEG // TI
    if to_natural:
        tb = N_RES * TI

    def body(dy_ref, w_ref, x_ref, g_ref, *rest):
        rest = list(rest)
        dres_ref = rest.pop(0) if has_res else None
        dx_ref = rest.pop(0)
        dxb_ref = rest.pop(0) if also_bf16 else None
        gg_ref = rest.pop(0)
        i = pl.program_id(0)

        def rows(ref, *lead):
            v = ref[lead] if lead else ref[...]
            return v[0].reshape(tb, v.shape[-1]) if to_natural else v

        if stacked:
            kb = dy_ref.shape[-1]
            dh = _dot_nt(rows(dy_ref, 0), w_ref[:, 0:kb])
            for s in range(1, dy_ref.shape[0]):
                dh = dh + _dot_nt(rows(dy_ref, s), w_ref[:, s * kb:(s + 1) * kb])
        else:
            dh = _dot_nt(rows(dy_ref), w_ref[...])
        g_v = g_ref[...]
        _, xh, r = _rms_fwd(rows(x_ref), g_v)
        dx = _rms_bwd(dh, xh, r, g_v)
        if has_res:
            dx = dx + rows(dres_ref)
        if to_natural:
            scr = rest.pop(0)
            for cb in range(d // BLK):
                cols = slice(cb * BLK, (cb + 1) * BLK)
                slab = scr.at[cb]
                for res in range(N_RES):
                    slab[pl.ds(res, TI, stride=N_RES), :] = dx[res * TI:(res + 1) * TI, cols]
                dx_ref[:, cols] = slab[...]
        else:
            dx_ref[...] = dx
        if also_bf16:
            dxb_ref[...] = dx.astype(BF16)
        part = jnp.sum(dh * xh, axis=0, keepdims=True)

        @pl.when(i == 0)
        def _():
            gg_ref[...] = part

        @pl.when(i != 0)
        def _():
            gg_ref[...] += part

    tok = pl.BlockSpec((tb, d), lambda i: (i, 0))
    row = pl.BlockSpec((1, d), lambda i: (0, 0))
    if to_natural:
        act = pl.BlockSpec((1, N_RES, TI, d), lambda i: (i // n_i, 0, i % n_i, 0))
        dy_spec = pl.BlockSpec((dy.shape[0], 1, N_RES, TI, dy.shape[2]), lambda i: (0, i // n_i, 0, i % n_i, 0))
        dy, x = dy.reshape(dy.shape[0], t // HALF, N_RES, SEG, dy.shape[2]), _x4(x)
        dres = _x4(dres) if has_res else None
    elif stacked:
        act, dy_spec = tok, pl.BlockSpec((dy.shape[0], tb, dy.shape[2]), lambda i: (0, i, 0))
    else:
        act, dy_spec = tok, pl.BlockSpec((tb, dy.shape[1]), lambda i: (i, 0))
    in_specs = [dy_spec, pl.BlockSpec(w.shape, lambda i: (0, 0)), act, row]
    args = [dy, w, x, g]
    if has_res:
        in_specs.append(act)
        args.append(dres)
    out_specs = [tok] + ([tok] if also_bf16 else []) + [row]
    out_shape = ([jax.ShapeDtypeStruct((t, d), F32)] + ([jax.ShapeDtypeStruct((t, d), BF16)] if also_bf16 else [])
                 + [jax.ShapeDtypeStruct((1, d), F32)])
    res, _ = _pcall(
        body, name=name, grid=(t // tb,), in_specs=in_specs, out_specs=out_specs, out_shape=out_shape,
        scratch_shapes=[pltpu.VMEM((d // BLK, tb, BLK), F32)] if to_natural else [],
        semantics=("arbitrary",), vmem_mb=56, after=after,
    )(*args)
    return res


def _matmul_tn(a, b, *, name, bm, bn, square_a=False, after=()):
    t, m = a.shape
    stacked = b.ndim == 3
    n = b.shape[0] * bn if stacked else b.shape[1]

    def body(a_ref, b_ref, o_ref):
        av = a_ref[...]
        if square_a:
            av = av * av
        o_ref[...] = _dot_tn(av, b_ref[...]).astype(BF16)

    res, _ = _pcall(
        body, name=name, grid=(m // bm, n // bn),
        in_specs=[pl.BlockSpec((t, bm), lambda i, j: (0, i)),
                  pl.BlockSpec((None, t, bn), lambda i, j: (j, 0, 0)) if stacked
                  else pl.BlockSpec((t, bn), lambda i, j: (0, j))],
        out_specs=[pl.BlockSpec((bm, bn), lambda i, j: (i, j))], out_shape=[jax.ShapeDtypeStruct((m, n), BF16)],
        semantics=("parallel", "parallel"), vmem_mb=56, after=after,
    )(a, b)
    return res[0]


N_RES = 16
SEG = 128
HALF = N_RES * SEG
TI = 32
HALO = 16


def _x4(a):
    return a.reshape(a.shape[0] // HALF, N_RES, SEG, a.shape[1])


def _reorder(arrays, name, rider=None):
    t, c = arrays[0].shape
    n = len(arrays)
    n_i = SEG // TI

    def body(*refs):
        scr = refs[-1]
        for i_ref, o_ref in zip(refs[:n], refs[n:2 * n]):
            for cb in range(c // BLK):
                cols = slice(cb * BLK, (cb + 1) * BLK)
                slab = scr.at[cb]
                slab[...] = i_ref[:, cols]
                for r in range(N_RES):
                    o_ref[0, r, :, cols] = slab[pl.ds(r, TI, stride=N_RES), :]

    res, extra = _pcall(
        body, name=name, grid=(t // (TI * N_RES),),
        in_specs=[pl.BlockSpec((TI * N_RES, c), lambda s: (s, 0))] * n,
        out_specs=[pl.BlockSpec((1, N_RES, TI, c), lambda s: (s // n_i, 0, s % n_i, 0))] * n,
        out_shape=[jax.ShapeDtypeStruct((t // HALF, N_RES, SEG, c), F32)] * n,
        scratch_shapes=[pltpu.VMEM((c // BLK, TI * N_RES, BLK), F32)],
        semantics=("parallel",), vmem_mb=32, rider=rider,
    )(*arrays)
    res = [r.reshape(t, c) for r in res]
    return res if rider is None else (res, extra)


_PATTERNS = ((1, 16, 8, SEG), (4, 4, 32, 4 * SEG), (16, 1, SEG, 0))
_FIRST = {1: 1, 4: 4, 16: 16}


def _group_rows(d, g):
    a = g >> 4
    if d == 16:
        base = a * HALF + (g & 15) * SEG
        prev = base - HALF
    elif d == 4:
        c = (g >> 2) & 3
        base = a * HALF + (g & 3) * SEG + c * 32
        prev = jnp.where(c > 0, base - 32, base - HALF + 96)
    else:
        c = g & 15
        base = a * HALF + c * 8
        prev = jnp.where(c > 0, base - 8, base - HALF + 120)
    return base, prev


def _load_rows(ref, base, n, rows, stride):
    parts = [ref[pl.ds(pl.multiple_of(base + j * stride, 8), rows), :] for j in range(n)]
    return parts[0] if n == 1 else jnp.concatenate(parts, axis=0)


def _store_rows(ref, base, val, n, rows, stride, add=False):
    for j in range(n):
        sl = pl.ds(pl.multiple_of(base + j * stride, 8), rows)
        piece = val[j * rows:(j + 1) * rows, :]
        if add:
            ref[sl, :] += piece
        else:
            ref[sl, :] = piece


def _band_bias(n, rows):
    shift = rows.bit_length() - 1
    lq = lax.broadcasted_iota(jnp.int32, (BLK, BLK), 0)
    lk = lax.broadcasted_iota(jnp.int32, (BLK, BLK), 1)
    iq = (lq & (rows - 1)) * n + (lq >> shift)
    ik = (lk & (rows - 1)) * n + (lk >> shift)
    zero = jnp.zeros((BLK, BLK), F32)
    return jnp.where(ik >= iq, zero, NEG_INF), jnp.where(ik <= iq, zero, NEG_INF)


def _set_bias(bias_scr, n, rows):
    prev_b, cur_b = _band_bias(n, rows)
    for half in range(2):
        bias_scr[half * BLK:(half + 1) * BLK, 0:BLK] = prev_b
        bias_scr[half * BLK:(half + 1) * BLK, BLK:2 * BLK] = cur_b


SCALE = 1.0 / math.sqrt(HEAD_DIM)


def _head_consts(value=1.0):
    lane_lo = lax.broadcasted_iota(jnp.int32, (BLK, BLK), 1) < HEAD_DIM
    return lane_lo, [jnp.where(lane_lo, value, 0.0).astype(BF16), jnp.where(lane_lo, 0.0, value).astype(BF16)]


def _stack_heads(v, head_mask):
    return jnp.concatenate([v * head_mask[0], v * head_mask[1]], axis=0)


def _unstack_heads(v2, lane_lo):
    return jnp.where(lane_lo, v2[:BLK], v2[BLK:])


def _rows_per_head(v, lane_lo):
    rolled = pltpu.roll(v, HEAD_DIM, axis=1)
    return jnp.concatenate([jnp.where(lane_lo, v, rolled), jnp.where(lane_lo, rolled, v)], axis=0)


WIDTH = 4


def _loop(lo, hi, fn, width=None):
    if width is None:
        def body(g, carry):
            fn(g)
            return carry

        if hi > lo:
            lax.fori_loop(lo, hi, body, 0)
        return
    while hi > lo:
        trips = (hi - lo) // width
        if trips:
            def body(i, carry, lo=lo, width=width):
                fn([lo + width * i + j for j in range(width)])
                return carry

            lax.fori_loop(0, trips, body, 0)
            lo += trips * width
        width = max(1, width // 2)


def _mix_weights(l1, l2, l3):
    mx = jnp.maximum(jnp.maximum(l1, l2), l3)
    e1, e2, e3 = jnp.exp(l1 - mx), jnp.exp(l2 - mx), jnp.exp(l3 - mx)
    inv = 1.0 / (e1 + e2 + e3)
    return e1 * inv, e2 * inv, e3 * inv


def _attention_fwd(qkv, rider=None):
    t = qkv.shape[0]
    groups = 16 * (t // HALF)

    def body(q_ref, k_ref, v_ref, attn_ref, l1_ref, l2_ref, l3_ref, o_scr, bias_scr):
        lane_lo, q_mask = _head_consts(SCALE)
        l_refs = (l1_ref, l2_ref, l3_ref)
        for p, (d, n, rows, stride) in enumerate(_PATTERNS):
            _set_bias(bias_scr, n, rows)
            o_p, l_p = o_scr.at[p], l_refs[p]

            def block(gs, has_prev):
                at = [_group_rows(d, g) for g in gs]

                def load(ref, b):
                    return _load_rows(ref, b, n, rows, stride).astype(BF16)

                q2 = [_stack_heads(load(q_ref, b), q_mask) for b, _ in at]
                k2 = [load(k_ref, b) for b, _ in at]
                v2 = [load(v_ref, b) for b, _ in at]
                if has_prev:
                    k2 = [jnp.concatenate([load(k_ref, pv), k], axis=0) for (_, pv), k in zip(at, k2)]
                    v2 = [jnp.concatenate([load(v_ref, pv), v], axis=0) for (_, pv), v in zip(at, v2)]
                s = [_dot_nt(q, k) for q, k in zip(q2, k2)]
                s = [x + (bias_scr[...] if has_prev else bias_scr[:, BLK:2 * BLK]) for x in s]
                mx = [jnp.max(x, axis=1, keepdims=True) for x in s]
                e = [jnp.exp(x - m) for x, m in zip(s, mx)]
                den = [jnp.sum(x, axis=1, keepdims=True) for x in e]
                o2 = [_dot(x.astype(BF16), v) * (1.0 / dn) for x, v, dn in zip(e, v2, den)]
                lse2 = [jnp.broadcast_to(m + jnp.log(dn), (2 * BLK, BLK)) for m, dn in zip(mx, den)]
                for (b, _), o, l in zip(at, o2, lse2):
                    _store_rows(o_p, b, _unstack_heads(o, lane_lo), n, rows, stride)
                    _store_rows(l_p, b, _unstack_heads(l, lane_lo), n, rows, stride)

            _loop(0, _FIRST[d], lambda gs: block(gs, False), width=2 * WIDTH)
            _loop(_FIRST[d], groups, lambda gs: block(gs, True), width=2 * WIDTH)

        def mix(i):
            sl = pl.ds(pl.multiple_of(i * 256, 256), 256)
            w = _mix_weights(l1_ref[sl, :], l2_ref[sl, :], l3_ref[sl, :])
            attn_ref[sl, :] = w[0] * o_scr[0, sl, :] + w[1] * o_scr[1, sl, :] + w[2] * o_scr[2, sl, :]

        _loop(0, t // 256, mix)

    def col(c0):
        return pl.BlockSpec((t, BLK), lambda hp: (0, c0 + hp))

    res, extra = _pcall(
        body, name="attention_fwd", grid=(4,), in_specs=[col(0), col(4), col(8)], out_specs=[col(0)] * 4,
        out_shape=[jax.ShapeDtypeStruct((t, 512), F32)] * 4,
        scratch_shapes=[pltpu.VMEM((3, t, BLK), F32), pltpu.VMEM((2 * BLK, 2 * BLK), F32)],
        semantics=("parallel",), vmem_mb=48, rider=rider,
    )(qkv, qkv, qkv)
    return res if rider is None else (res, extra)


def _attention_bwd(qkv, dattn, dsum, lses, dproj):
    t = qkv.shape[0]
    groups = 16 * (t // HALF)

    def body(q_ref, k_ref, v_ref, da_ref, ds_ref, l1_ref, l2_ref, l3_ref, kept_ref, out_ref, acc, bias_scr):
        del kept_ref
        lane_lo, head_mask = _head_consts()
        q_mask = _head_consts(SCALE)[1]
        l_refs = (l1_ref, l2_ref, l3_ref)

        def clear(i):
            sl = pl.ds(pl.multiple_of(i * 512, 512), 512)
            for s in range(3):
                acc[s, sl, :] = jnp.zeros((512, BLK), F32)

        _loop(0, t // 512, clear)
        dq_acc, dk_acc, dv_acc = acc.at[0], acc.at[1], acc.at[2]
        for p, (d, n, rows, stride) in enumerate(_PATTERNS):
            _set_bias(bias_scr, n, rows)

            def block(gs, has_prev):
                at = [_group_rows(d, g) for g in gs]

                def load(ref, b):
                    return _load_rows(ref, b, n, rows, stride)

                def put(ref, b, val):
                    _store_rows(ref, b, val, n, rows, stride, add=True)

                def wide(x):
                    return jnp.concatenate([x, x], axis=1) if has_prev else x

                lse = [[load(ref, b) for ref in l_refs] for b, _ in at]
                w = [_mix_weights(*ls)[p] for ls in lse]
                do2 = [_stack_heads((wg * load(da_ref, b)).astype(BF16), head_mask) for wg, (b, _) in zip(w, at)]
                dl2 = [wide(_rows_per_head(wg * load(ds_ref, b), lane_lo)) for wg, (b, _) in zip(w, at)]
                lse2 = [wide(_rows_per_head(ls[p], lane_lo)) for ls in lse]
                q2 = [_stack_heads(load(q_ref, b).astype(BF16), q_mask) for b, _ in at]
                k2 = [load(k_ref, b).astype(BF16) for b, _ in at]
                v2 = [load(v_ref, b).astype(BF16) for b, _ in at]
                if has_prev:
                    k2 = [jnp.concatenate([load(k_ref, pv).astype(BF16), k], axis=0) for (_, pv), k in zip(at, k2)]
                    v2 = [jnp.concatenate([load(v_ref, pv).astype(BF16), v], axis=0) for (_, pv), v in zip(at, v2)]
                s = [_dot_nt(q, k) for q, k in zip(q2, k2)]
                dp = [_dot_nt(do, v) for do, v in zip(do2, v2)]
                pr = [jnp.exp(x + (bias_scr[...] if has_prev else bias_scr[:, BLK:2 * BLK]) - l)
                      for x, l in zip(s, lse2)]
                ds = [(pg * (x - dl)).astype(BF16) for pg, x, dl in zip(pr, dp, dl2)]
                dq2 = [_dot(x, k) * SCALE for x, k in zip(ds, k2)]
                dk2 = [_dot_tn(x, q) for x, q in zip(ds, q2)]
                dv2 = [_dot_tn(pg.astype(BF16), do) for pg, do in zip(pr, do2)]
                for (b, pv), dq, dk, dv in zip(at, dq2, dk2, dv2):
                    put(dq_acc, b, _unstack_heads(dq, lane_lo))
                    if has_prev:
                        put(dk_acc, pv, dk[:BLK])
                        put(dv_acc, pv, dv[:BLK])
                        put(dk_acc, b, dk[BLK:])
                        put(dv_acc, b, dv[BLK:])
                    else:
                        put(dk_acc, b, dk)
                        put(dv_acc, b, dv)

            _loop(0, _FIRST[d], lambda gs: block(gs, False), width=WIDTH)
            _loop(_FIRST[d], groups, lambda gs: block(gs, True), width=WIDTH)

        def emit(i):
            sl = pl.ds(pl.multiple_of(i * 512, 512), 512)
            for s in range(3):
                out_ref[s, sl, :] = acc[s, sl, :].astype(BF16)

        _loop(0, t // 512, emit)

    def col(c0):
        return pl.BlockSpec((t, BLK), lambda hp: (0, c0 + hp))

    res, _ = _pcall(
        body, name="attention_bwd", grid=(4,),
        in_specs=[col(0), col(4), col(8)] + [col(0)] * 5 + [ANY],
        out_specs=[pl.BlockSpec((3, t, BLK), lambda hp: (0, 0, hp))],
        out_shape=[jax.ShapeDtypeStruct(dproj.shape, BF16)],
        scratch_shapes=[pltpu.VMEM((3, t, BLK), F32), pltpu.VMEM((2 * BLK, 2 * BLK), F32)],
        semantics=("parallel",), vmem_mb=56, aliases={8: 0},
    )(qkv, qkv, qkv, dattn, dsum, *lses, dproj)
    return res[0]


def _order_specs(t):
    n_i = SEG // TI
    nblk = (t // HALF) * n_i
    per = TI // HALO

    def main(c, col=0):
        return pl.BlockSpec((1, N_RES, TI, c), lambda s: (s // n_i, 0, s % n_i, col))

    def before(c, col=0):
        return pl.BlockSpec((1, 2, HALO, c), lambda s: (jnp.maximum(s - 1, 0) // n_i, N_RES // 2 - 1,
                                                        (jnp.maximum(s - 1, 0) % n_i) * per + per - 1, col))

    def after(c, col=0):
        return pl.BlockSpec((1, 2, HALO, c), lambda s: (jnp.minimum(s + 1, nblk - 1) // n_i, 0,
                                                        (jnp.minimum(s + 1, nblk - 1) % n_i) * per, col))

    return nblk, main, before, after


def _shift_in(v, row_in, up):
    rows = v.shape[0]
    idx = lax.broadcasted_iota(jnp.int32, v.shape, 0)
    fill = jnp.broadcast_to(row_in, v.shape)
    if up:
        return jnp.where(idx == rows - 1, fill, pltpu.roll(v, rows - 1, axis=0))
    return jnp.where(idx == 0, fill, pltpu.roll(v, 1, axis=0))


def _taps_behind(u, before):
    s15 = _shift_in(u[N_RES - 1], before[1, HALO - 1:HALO, :], up=False)
    s14 = _shift_in(u[N_RES - 2], before[0, HALO - 1:HALO, :], up=False)
    m1 = jnp.concatenate([s15[None], u[:N_RES - 1]], axis=0)
    m2 = jnp.concatenate([s14[None], s15[None], u[:N_RES - 2]], axis=0)
    return m1, m2


def _taps_ahead(u, after):
    t0 = _shift_in(u[0], after[0, 0:1, :], up=True)
    t1 = _shift_in(u[1], after[1, 0:1, :], up=True)
    p1 = jnp.concatenate([u[1:], t0[None]], axis=0)
    p2 = jnp.concatenate([u[2:], t0[None], t1[None]], axis=0)
    return p1, p2


def _conv_fwd(gates, before, first, cw):
    gates, before = gates.astype(F32), before.astype(F32)
    bg, cg, xc = gates[..., 0:512], gates[..., 512:1024], gates[..., 1024:1536]
    u = cg * xc
    ub = before[..., 512:1024] * before[..., 1024:1536]
    ub = jnp.where(first, jnp.zeros_like(ub), ub)
    m1, m2 = _taps_behind(u, ub)
    conv = m2 * cw[0:1, :] + m1 * cw[1:2, :] + u * cw[2:3, :]
    return bg, u, m1, m2, conv


def _sum_tokens(v):
    return jnp.sum(jnp.sum(v, axis=0), axis=0, keepdims=True)


def _mixer_fwd(x, attn, gates, cw, g_a, g_c, w_out):
    t, d = x.shape
    nblk, main, before, _ = _order_specs(t)
    rows = N_RES * TI

    def body(x_ref, at_ref, gt_ref, gb_ref, cw_ref, ga_ref, gc_ref, wa_ref, wb_ref, x1_ref, mg_ref):
        an = _rms_fwd(at_ref[0], ga_ref[...])[0].astype(BF16)
        bg, _, _, _, conv = _conv_fwd(gt_ref[0], gb_ref[0], pl.program_id(0) == 0, cw_ref[...])
        cn = _rms_fwd(bg * conv, gc_ref[...])[0].astype(BF16)
        mg_ref[0, :, :, 0:512] = an
        mg_ref[0, :, :, 512:1024] = cn
        y = _dot(an.reshape(rows, 512), wa_ref[...]) + _dot(cn.reshape(rows, 512), wb_ref[...])
        x1_ref[0] = x_ref[0] + y.reshape(N_RES, TI, d)

    const = lambda r, c, i0=0: pl.BlockSpec((r, c), lambda s: (i0, 0))
    x1, merged = pl.pallas_call(
        body, name="mixer_fwd", grid=(nblk,),
        in_specs=[main(d), main(512), main(1536), before(1536), const(3, 512), const(1, 512), const(1, 512),
                  const(512, d), const(512, d, 1)],
        out_specs=[main(d), main(d)],
        out_shape=[jax.ShapeDtypeStruct(_x4(x).shape, F32), jax.ShapeDtypeStruct(_x4(x).shape, BF16)],
        compiler_params=_params(("parallel",), 48),
    )(_x4(x), _x4(attn), _x4(gates), _x4(gates), cw, g_a, g_c, w_out, w_out)
    return x1.reshape(t, d), merged.reshape(t, d)


def _mixer_bwd(dx1, attn, gates, cw, g_a, g_c, w_out, head_sum, after=()):
    t, d = dx1.shape
    nblk, main, before, _ = _order_specs(t)
    rows = N_RES * TI

    def body(dx_ref, at_ref, gt_ref, gb_ref, cw_ref, ga_ref, gc_ref, wa_ref, wb_ref, hs_ref,
             da_ref, dsum_ref, dy_ref, gga_ref, ggc_ref):
        s = pl.program_id(0)
        dxb = dx_ref[0].reshape(rows, d).astype(BF16)
        dma = _dot_nt(dxb, wa_ref[...]).reshape(N_RES, TI, 512)
        dmc = _dot_nt(dxb, wb_ref[...]).reshape(N_RES, TI, 512)
        attn_v, g_av = at_ref[0], ga_ref[...]
        _, ah, ra = _rms_fwd(attn_v, g_av)
        dattn = _rms_bwd(dma, ah, ra, g_av)
        da_ref[0] = dattn
        z = (dattn * attn_v).reshape(rows, 512)
        hs = hs_ref[...]
        z1 = z.astype(BF16)
        z2 = (z - z1.astype(F32)).astype(BF16)
        dsum_ref[0] = (_dot(z1, hs) + _dot(z2, hs)).reshape(N_RES, TI, 512)
        bg, _, _, _, conv = _conv_fwd(gt_ref[0], gb_ref[0], s == 0, cw_ref[...])
        g_cv = gc_ref[...]
        _, yh, rc = _rms_fwd(bg * conv, g_cv)
        dy_ref[0] = _rms_bwd(dmc, yh, rc, g_cv)
        pa, pc = _sum_tokens(dma * ah), _sum_tokens(dmc * yh)

        @pl.when(s == 0)
        def _():
            gga_ref[...] = pa
            ggc_ref[...] = pc

        @pl.when(s != 0)
        def _():
            gga_ref[...] += pa
            ggc_ref[...] += pc

    const = lambda r, c, i0=0: pl.BlockSpec((r, c), lambda s: (i0, 0))
    shape4 = _x4(attn).shape
    res, _ = _pcall(
        body, name="mixer_bwd", grid=(nblk,),
        in_specs=[main(d), main(512), main(1536), before(1536), const(3, 512), const(1, 512), const(1, 512),
                  const(512, d), const(512, d, 1), const(512, 512)],
        out_specs=[main(512)] * 3 + [const(1, 512), const(1, 512)],
        out_shape=[jax.ShapeDtypeStruct(shape4, F32)] * 3 + [jax.ShapeDtypeStruct((1, 512), F32)] * 2,
        semantics=("arbitrary",), vmem_mb=48, after=after,
    )(_x4(dx1), _x4(attn), _x4(gates), _x4(gates), cw, g_a, g_c, w_out, w_out, head_sum)
    return [r.reshape(t, 512) for r in res[:3]] + res[3:]


def _conv_bwd(dy, gates, cw):
    t = dy.shape[0]
    nblk, main, before, after = _order_specs(t)
    n_i = SEG // TI

    def body(dy_ref, dya_ref, gt_ref, gb_ref, ga_ref, cw_ref, dp_ref, gcw_ref):
        s = pl.program_id(0)
        cw_v, gates_v = cw_ref[...], gt_ref[0]
        bg, u, m1, m2, conv = _conv_fwd(gates_v, gb_ref[0], s == 0, cw_v)
        dy_v = dy_ref[0]
        dconv = dy_v * bg
        dca = dya_ref[0] * ga_ref[0][..., 0:512].astype(F32)
        dca = jnp.where(s == nblk - 1, jnp.zeros_like(dca), dca)
        p1, p2 = _taps_ahead(dconv, dca)
        du = dconv * cw_v[2:3, :] + p1 * cw_v[1:2, :] + p2 * cw_v[0:1, :]
        dp_ref[0, 0] = (dy_v * conv).astype(BF16)
        dp_ref[1, 0] = (du * gates_v[..., 1024:1536].astype(F32)).astype(BF16)
        dp_ref[2, 0] = (du * gates_v[..., 512:1024].astype(F32)).astype(BF16)
        parts = [_sum_tokens(dconv * m2), _sum_tokens(dconv * m1), _sum_tokens(dconv * u)]

        @pl.when(s == 0)
        def _():
            gcw_ref[...] = jnp.zeros_like(gcw_ref)

        for tap in range(3):
            gcw_ref[tap:tap + 1, :] += parts[tap]

    (dproj, gcw), _ = _pcall(
        body, name="conv_bwd", grid=(nblk,),
        in_specs=[main(512), after(512), main(1536), before(1536), after(1536),
                  pl.BlockSpec((3, 512), lambda s: (0, 0))],
        out_specs=[pl.BlockSpec((3, 1, N_RES, TI, 512), lambda s: (1, s // n_i, 0, s % n_i, 0)),
                   pl.BlockSpec((8, 512), lambda s: (0, 0))],
        out_shape=[jax.ShapeDtypeStruct((6, t // HALF, N_RES, SEG, 512), BF16), jax.ShapeDtypeStruct((8, 512), F32)],
        semantics=("arbitrary",), vmem_mb=40,
    )(_x4(dy), _x4(dy), _x4(gates), _x4(gates), _x4(gates), cw)
    return dproj.reshape(6, t, 512), gcw


def _xattn_fwd(x1, g, w_q, kv, w_o, *, tb):
    t, d = x1.shape
    hd = d // N_MEM_HEADS
    m = kv.shape[0]

    def body(x_ref, g_ref, wq_ref, k_ref, v_ref, wo_ref, x2_ref, h_ref, q_ref, o_ref):
        xv = x_ref[...]
        h = _rms_fwd(xv, g_ref[...])[0].astype(BF16)
        h_ref[...] = h
        q = _dot(h, wq_ref[...]).astype(BF16)
        q_ref[...] = q
        for hh in range(N_MEM_HEADS):
            sl = slice(hh * hd, (hh + 1) * hd)
            s = _dot_nt(q[:, sl], k_ref[:, sl]) * (1.0 / 16.0)
            e = jnp.exp(s - jnp.max(s, axis=1, keepdims=True))
            p = e / jnp.sum(e, axis=1, keepdims=True)
            o_ref[:, sl] = _dot(p.astype(BF16), v_ref[:, sl]).astype(BF16)
        x2_ref[...] = xv + _dot(o_ref[...], wo_ref[...])

    tok = pl.BlockSpec((tb, d), lambda i: (i, 0))
    full = pl.BlockSpec((d, d), lambda i: (0, 0))
    return pl.pallas_call(
        body, name="xattn_fwd", grid=(t // tb,),
        in_specs=[tok, pl.BlockSpec((1, d), lambda i: (0, 0)), full,
                  pl.BlockSpec((m, d), lambda i: (0, 0)), pl.BlockSpec((m, d), lambda i: (0, 1)), full],
        out_specs=[tok] * 4,
        out_shape=[jax.ShapeDtypeStruct((t, d), F32)] + [jax.ShapeDtypeStruct((t, d), BF16)] * 3,
        compiler_params=_params(("parallel",), 48),
    )(x1, g, w_q, kv, kv, w_o)


def _xattn_bwd(dx2, x1, g, q, w_q, kv, w_o, *, tb, after=()):
    t, d = x1.shape
    hd = d // N_MEM_HEADS
    m = kv.shape[0]

    def body(dx2_ref, x_ref, g_ref, q_ref, wq_ref, k_ref, v_ref, wo_ref,
             dx1_ref, dx1b_ref, dq_ref, dk_ref, dv_ref, gg_ref):
        i = pl.program_id(0)

        @pl.when(i == 0)
        def _():
            dk_ref[...] = jnp.zeros_like(dk_ref)
            dv_ref[...] = jnp.zeros_like(dv_ref)

        dx2 = dx2_ref[...]
        do = _dot_nt(dx2.astype(BF16), wo_ref[...]).astype(BF16)
        for hh in range(N_MEM_HEADS):
            sl = slice(hh * hd, (hh + 1) * hd)
            qh, kh, vh, doh = q_ref[:, sl], k_ref[:, sl], v_ref[:, sl], do[:, sl]
            s = _dot_nt(qh, kh) * (1.0 / 16.0)
            e = jnp.exp(s - jnp.max(s, axis=1, keepdims=True))
            p = e / jnp.sum(e, axis=1, keepdims=True)
            dp = _dot_nt(doh, vh)
            ds = (p * (dp - jnp.sum(dp * p, axis=1, keepdims=True)) * (1.0 / 16.0)).astype(BF16)
            dq_ref[:, sl] = _dot(ds, kh).astype(BF16)
            dk_ref[:, sl] += _dot_tn(ds, qh)
            dv_ref[:, sl] += _dot_tn(p.astype(BF16), doh)
        dh = _dot_nt(dq_ref[...], wq_ref[...])
        g_v = g_ref[...]
        _, xh, r = _rms_fwd(x_ref[...], g_v)
        dx1 = dx2 + _rms_bwd(dh, xh, r, g_v)
        dx1_ref[...] = dx1
        dx1b_ref[...] = dx1.astype(BF16)
        part = jnp.sum(dh * xh, axis=0, keepdims=True)

        @pl.when(i == 0)
        def _():
            gg_ref[...] = part

        @pl.when(i != 0)
        def _():
            gg_ref[...] += part

    tok = pl.BlockSpec((tb, d), lambda i: (i, 0))
    full = pl.BlockSpec((d, d), lambda i: (0, 0))
    acc = pl.BlockSpec((m, d), lambda i: (0, 0))
    res, _ = _pcall(
        body, name="xattn_bwd", grid=(t // tb,),
        in_specs=[tok, tok, pl.BlockSpec((1, d), lambda i: (0, 0)), tok, full,
                  pl.BlockSpec((m, d), lambda i: (0, 0)), pl.BlockSpec((m, d), lambda i: (0, 1)), full],
        out_specs=[tok, tok, tok, acc, acc, pl.BlockSpec((1, d), lambda i: (0, 0))],
        out_shape=[jax.ShapeDtypeStruct((t, d), F32), jax.ShapeDtypeStruct((t, d), BF16),
                   jax.ShapeDtypeStruct((t, d), BF16),
                   jax.ShapeDtypeStruct((m, d), F32), jax.ShapeDtypeStruct((m, d), F32),
                   jax.ShapeDtypeStruct((1, d), F32)],
        semantics=("arbitrary",), vmem_mb=48, after=after,
    )(dx2, x1, g, q, w_q, kv, kv, w_o)
    return res


def _mlp_down_loss(a, w_down, x2, tgt, g, *, tb):
    t, d = x2.shape
    f = a.shape[1]

    def body(a_ref, w_ref, x_ref, t_ref, g_ref, dx_ref, dxb_ref, loss_ref, gg_ref):
        i = pl.program_id(0)
        av = a_ref[...]
        x3 = x_ref[...] + _dot(av * av, w_ref[...])
        g_v = g_ref[...]
        out, xh, r = _rms_fwd(x3, g_v)
        err = out - t_ref[...]
        dout = err * (1.0 / d)
        dx = _rms_bwd(dout, xh, r, g_v)
        dx_ref[...] = dx
        dxb_ref[...] = dx.astype(BF16)
        part = jnp.sum(dout * xh, axis=0, keepdims=True)
        lpart = 0.5 * jnp.sum(jnp.mean(err * err, axis=-1, keepdims=True), axis=0, keepdims=True)
        lpart = jnp.broadcast_to(lpart, loss_ref.shape)

        @pl.when(i == 0)
        def _():
            gg_ref[...] = part
            loss_ref[...] = lpart

        @pl.when(i != 0)
        def _():
            gg_ref[...] += part
            loss_ref[...] += lpart

    tok = pl.BlockSpec((tb, d), lambda i: (i, 0))
    return pl.pallas_call(
        body, name="mlp_down_loss", grid=(t // tb,),
        in_specs=[pl.BlockSpec((tb, f), lambda i: (i, 0)), pl.BlockSpec((f, d), lambda i: (0, 0)), tok, tok,
                  pl.BlockSpec((1, d), lambda i: (0, 0))],
        out_specs=[tok, tok, pl.BlockSpec((8, 128), lambda i: (0, 0)), pl.BlockSpec((1, d), lambda i: (0, 0))],
        out_shape=[jax.ShapeDtypeStruct((t, d), F32), jax.ShapeDtypeStruct((t, d), BF16),
                   jax.ShapeDtypeStruct((8, 128), F32), jax.ShapeDtypeStruct((1, d), F32)],
        compiler_params=_params(("arbitrary",), 56),
    )(a, w_down, x2, tgt, g)


def _mlp_dpre(dx3, w_down, a, *, tb, bn):
    t, d = dx3.shape
    f = a.shape[1]

    def body(dx_ref, w_ref, a_ref, o_ref):
        o_ref[...] = (2.0 * a_ref[...].astype(F32) * _dot_nt(dx_ref[...], w_ref[...])).astype(BF16)

    return pl.pallas_call(
        body, name="mlp_dpre", grid=(t // tb, f // bn),
        in_specs=[pl.BlockSpec((tb, d), lambda i, j: (i, 0)), pl.BlockSpec((bn, d), lambda i, j: (j, 0)),
                  pl.BlockSpec((tb, bn), lambda i, j: (i, j))],
        out_specs=pl.BlockSpec((tb, bn), lambda i, j: (i, j)),
        out_shape=jax.ShapeDtypeStruct((t, f), BF16),
        compiler_params=_params(("parallel", "arbitrary"), 48),
    )(dx3, w_down, a)


def _adamw(gsum, w, m, v):
    m_new = ADAM_B1 * m + (1.0 - ADAM_B1) * gsum
    v_new = ADAM_B2 * v + (1.0 - ADAM_B2) * (gsum * gsum)
    m_hat = m_new / (1.0 - ADAM_B1 ** ADAM_STEP)
    v_hat = v_new / (1.0 - ADAM_B2 ** ADAM_STEP)
    delta = -ADAM_LR * (m_hat / (jnp.sqrt(v_hat) + ADAM_EPS) + ADAM_WD * w)
    return delta, m_new, v_new


def _sum_adamw(parts, w, m, v, *, name, tr):
    r, c = w.shape

    def body(p_ref, w_ref, m_ref, v_ref, g_ref, d_ref, mo_ref, vo_ref):
        g = p_ref[0].astype(F32)
        for k in range(1, N_DEV):
            g = g + p_ref[k].astype(F32)
        g_ref[...] = g
        d_ref[...], mo_ref[...], vo_ref[...] = _adamw(g, w_ref[...], m_ref[...], v_ref[...])

    blk = pl.BlockSpec((tr, c), lambda i: (i, 0))
    return pl.pallas_call(
        body, name=name, grid=(r // tr,),
        in_specs=[pl.BlockSpec((N_DEV, tr, c), lambda i: (0, i, 0)), blk, blk, blk],
        out_specs=[blk] * 4, out_shape=[jax.ShapeDtypeStruct((r, c), F32)] * 4,
        compiler_params=_params(("parallel",), 40),
    )(*[pltpu.with_memory_space_constraint(a, pltpu.HBM) for a in (parts, w, m, v)])


def _sum_small(parts):
    _, r, c = parts.shape

    def body(p_ref, o_ref):
        s = p_ref[0]
        for k in range(1, N_DEV):
            s = s + p_ref[k]
        o_ref[...] = s

    return pl.pallas_call(body, name="sum_small", out_shape=jax.ShapeDtypeStruct((r, c), F32))(parts)


def _adamw_small(g, w, m, v):
    def body(g_ref, w_ref, m_ref, v_ref, d_ref, mo_ref, vo_ref):
        d_ref[...], mo_ref[...], vo_ref[...] = _adamw(g_ref[...], w_ref[...], m_ref[...], v_ref[...])

    return pl.pallas_call(body, name="adamw_small", out_shape=[jax.ShapeDtypeStruct(g.shape, F32)] * 3)(g, w, m, v)


def _head_sum_matrix():
    r = lax.broadcasted_iota(jnp.int32, (512, 512), 0) // HEAD_DIM
    c = lax.broadcasted_iota(jnp.int32, (512, 512), 1) // HEAD_DIM
    return (r == c).astype(BF16)


_SHARD_AXIS = dict(w_in=1, w_out=0, w_q=0, w_kv=1, w_o=0, w_up=1, w_down=0, conv_w=None, small=None)


class _Weights:
    def __init__(self, full, shards=None):
        self.full = dict(full)
        self.shards = shards

    def rider(self, names, late=False):
        if self.shards is None:
            return None
        return _Gather([self.shards[n] for n in names], [_SHARD_AXIS[n] for n in names], late)

    def arrived(self, names, gathered):
        if gathered is not None:
            for n, g in zip(names, gathered):
                self.full[n] = g.transpose(1, 0, 2).reshape(g.shape[1], -1) if n == "conv_w" else g

    def __getitem__(self, name):
        return self.full[name]


class _Grads:
    def __init__(self, distributed):
        self.distributed = distributed
        self.local = {}
        self.pending = {}

    def add(self, name, g):
        self.local[name] = g

    def send(self, *names):
        if not self.distributed:
            return []
        rider = _Exchange([self.local[n] for n in names], [_SHARD_AXIS[n] for n in names])
        started = _exchange_start(rider, "send_" + "_".join(names))
        self.pending[names[0]] = (names, rider, started)
        return [started[3]]

    def wait(self, first_name, after):
        names, rider, started = self.pending.pop(first_name)
        return _exchange_wait(rider, started, after, "wait_" + "_".join(names))


def _ride(fn, *args, rider=None, **kw):
    if rider is None:
        return fn(*args, **kw), None
    return fn(*args, rider=rider, **kw)


def _local_step(x, mem, tgt, gains, weights, grads):
    names = ["w_in", "conv_w"]
    (x, tgt), got = _ride(_reorder, [x, tgt], "reorder_in", rider=weights.rider(names, late=True))
    weights.arrived(names, got)
    w_in, cw = weights["w_in"], weights["conv_w"]

    names = ["w_out", "w_kv"]
    (qkv, gates, h1), got = _ride(_proj, x, gains["g_mix"], w_in, tb=1024, rider=weights.rider(names))
    weights.arrived(names, got)
    names = ["w_q", "w_o", "w_up"]
    (attn, *lses), got = _ride(_attention_fwd, qkv, rider=weights.rider(names))
    weights.arrived(names, got)
    x1, merged = _mixer_fwd(x, attn, gates, cw, gains["g_attn_out"], gains["g_conv_out"], weights["w_out"])
    kv, mem_n = _norm_matmul(mem, gains["g_mem"], weights["w_kv"], name="mem_kv", out_dtype=BF16, tb=mem.shape[0],
                             bn=1024, save_h=True)
    x2, h2, qm, om = _xattn_fwd(x1, gains["g_xattn"], weights["w_q"], kv, weights["w_o"], tb=512)
    w_up = weights["w_up"]
    (a, h3), got = _ride(_norm_matmul, x2, gains["g_mlp"], w_up, name="mlp_up", out_dtype=BF16, tb=1024, bn=2048,
                         relu=True, save_h=True, rider=weights.rider(["w_down"], late=True))
    weights.arrived(["w_down"], got)
    w_down = weights["w_down"]
    dx3, dx3b, loss_blk, gg_final = _mlp_down_loss(a, w_down, x2, tgt, gains["g_final"], tb=512)

    dpre = _mlp_dpre(dx3b, w_down, a, tb=1024, bn=2048)
    grads.add("w_down", _matmul_tn(a, dx3b, name="grad_w_down", bm=512, bn=1024, square_a=True))
    sent = grads.send("w_down")
    grads.add("w_up", _matmul_tn(h3, dpre, name="grad_w_up", bm=1024, bn=1024, after=sent))
    sent = grads.send("w_up")
    dx2, dx2b, gg_mlp = _matmul_nt_normbwd(dpre, w_up, x2, gains["g_mlp"], dx3, name="mlp_dx", tb=512,
                                           also_bf16=True, after=sent)

    grads.add("w_o", _matmul_tn(om, dx2b, name="grad_w_o", bm=512, bn=512))
    dx1, dx1b, dqm, dk, dv, gg_xattn = _xattn_bwd(dx2, x1, gains["g_xattn"], qm, weights["w_q"], kv, weights["w_o"],
                                                  tb=512)
    grads.add("w_q", _matmul_tn(h2, dqm, name="grad_w_q", bm=1024, bn=512))
    dkv = jnp.concatenate([dk, dv], axis=1).astype(BF16)
    grads.add("w_kv", _matmul_tn(mem_n, dkv, name="grad_w_kv", bm=1024, bn=1024))
    _, gg_mem = _matmul_nt_normbwd(dkv, weights["w_kv"], mem, gains["g_mem"], None, name="mem_dx", tb=mem.shape[0])

    grads.add("w_out", _matmul_tn(merged, dx1b, name="grad_w_out", bm=1024, bn=512))
    sent = grads.send("w_o", "w_q", "w_kv", "w_out")
    dattn, dsum, dy, gg_attn, gg_conv = _mixer_bwd(dx1, attn, gates, cw, gains["g_attn_out"], gains["g_conv_out"],
                                                   weights["w_out"], _head_sum_matrix(), after=sent)
    dproj, gcw = _conv_bwd(dy, gates, cw)
    dproj = _attention_bwd(qkv, dattn, dsum, lses, dproj)
    grads.add("w_in", _matmul_tn(h1, dproj, name="grad_w_in", bm=1024, bn=512))
    sent = grads.send("w_in")
    grad_x, gg_mix = _matmul_nt_normbwd(dproj, w_in, x, gains["g_mix"], dx1, name="mixer_dx", tb=512,
                                        to_natural=True, after=sent)

    def part(v):
        return jnp.pad(v, ((0, SMALL_PART - v.shape[0]), (0, 1024 - v.shape[1])))

    parts = [gg_mix, gg_xattn, gg_mem, gg_mlp, gg_final, jnp.concatenate([gg_attn, gg_conv], axis=1), gcw, loss_blk]
    grads.add("small", jnp.concatenate([part(v) for v in parts], axis=0))
    return grad_x


SMALL_PART = 8
_BIG = ("w_in", "w_out", "w_q", "w_kv", "w_o", "w_up", "w_down")
_GAIN_ROWS = ("g_mix", "g_xattn", "g_mem", "g_mlp", "g_final")


def _pack_small(vals, conv):
    rows = [vals[k].reshape(1, -1) for k in _GAIN_ROWS]
    rows.append(jnp.concatenate([vals["g_attn_out"].reshape(1, -1), vals["g_conv_out"].reshape(1, -1)], axis=1))
    flat = conv.reshape(1, -1)
    rows.append(jnp.pad(flat, ((0, 0), (0, 1024 - flat.shape[1]))))
    rows.append(jnp.zeros((1, 1024), F32))
    return jnp.concatenate(rows, axis=0)


def kernel(x, mem, g_mix, w_in, conv_w, g_attn_out, g_conv_out, w_out, g_xattn, g_mem, w_q_mem, w_kv_mem, w_o_mem, g_mlp, w_up, w_down, g_final, loss_target, m_g_mix, m_w_in, m_conv_w, m_g_attn_out, m_g_conv_out, m_w_out, m_g_xattn, m_g_mem, m_w_q_mem, m_w_kv_mem, m_w_o_mem, m_g_mlp, m_w_up, m_w_down, m_g_final, v_g_mix, v_w_in, v_conv_w, v_g_attn_out, v_g_conv_out, v_w_out, v_g_xattn, v_g_mem, v_w_q_mem, v_w_kv_mem, v_w_o_mem, v_g_mlp, v_w_up, v_w_down, v_g_final):
    d = x.shape[-1]
    me = 4 * lax.axis_index("x") + 2 * lax.axis_index("y") + lax.axis_index("c")
    w_shards = dict(w_in=w_in, w_out=w_out, w_q=w_q_mem, w_kv=w_kv_mem, w_o=w_o_mem, w_up=w_up, w_down=w_down)
    m_shards = dict(w_in=m_w_in, w_out=m_w_out, w_q=m_w_q_mem, w_kv=m_w_kv_mem, w_o=m_w_o_mem, w_up=m_w_up,
                    w_down=m_w_down)
    v_shards = dict(w_in=v_w_in, w_out=v_w_out, w_q=v_w_q_mem, w_kv=v_w_kv_mem, w_o=v_w_o_mem, w_up=v_w_up,
                    w_down=v_w_down)
    gains = dict(g_mix=g_mix, g_attn_out=g_attn_out, g_conv_out=g_conv_out, g_xattn=g_xattn, g_mem=g_mem,
                 g_mlp=g_mlp, g_final=g_final)
    gains2 = {k: v.reshape(1, -1) for k, v in gains.items()}

    shards = {k: w_shards[k].astype(BF16) for k in _BIG}
    shards["conv_w"] = conv_w
    grads = _Grads(distributed=True)
    grad_x = _local_step(x[0], mem[0], loss_target[0], gains2, _Weights({}, shards), grads)

    after = grads.send("small")
    outs = {}
    tiles = dict(w_in=256, w_out=128, w_q=128, w_kv=256, w_o=128, w_up=256, w_down=256)
    for group in (("w_down",), ("w_up",), ("w_o", "w_q", "w_kv", "w_out"), ("w_in",)):
        for k, received in zip(group, grads.wait(group[0], after)):
            outs[k] = _sum_adamw(received, w_shards[k], m_shards[k], v_shards[k], name=f"adamw_{k}", tr=tiles[k])
            after = [outs[k][0]]
    small_received, = grads.wait("small", after)

    ssum = _sum_small(small_received)
    row = lambda i: ssum[SMALL_PART * i]
    loss = ssum[SMALL_PART * 7, 0]
    g_small = {k: row(i) for i, k in enumerate(_GAIN_ROWS)}
    g_small["g_attn_out"] = row(5)[0:512]
    g_small["g_conv_out"] = row(5)[512:1024]
    taps = ssum[SMALL_PART * 6:SMALL_PART * 6 + 3, 0:512]
    g_conv = lax.dynamic_slice_in_dim(taps, me * 64, 64, axis=1)
    m_small = dict(g_mix=m_g_mix, g_attn_out=m_g_attn_out, g_conv_out=m_g_conv_out, g_xattn=m_g_xattn,
                   g_mem=m_g_mem, g_mlp=m_g_mlp, g_final=m_g_final)
    v_small = dict(g_mix=v_g_mix, g_attn_out=v_g_attn_out, g_conv_out=v_g_conv_out, g_xattn=v_g_xattn,
                   g_mem=v_g_mem, g_mlp=v_g_mlp, g_final=v_g_final)
    packed = [_pack_small(g_small, g_conv), _pack_small(gains, conv_w), _pack_small(m_small, m_conv_w),
              _pack_small(v_small, v_conv_w)]
    upd = _adamw_small(*packed)

    def unpack(p):
        res = {k: p[i] for i, k in enumerate(_GAIN_ROWS)}
        res["g_attn_out"] = p[5, 0:512]
        res["g_conv_out"] = p[5, 512:1024]
        res["conv_w"] = p[6, 0:192].reshape(3, 64)
        return res

    g_small["conv_w"] = g_conv
    small_out = [g_small] + [unpack(p) for p in upd]
    names = {"g_mix": "g_mix", "w_in": "w_in", "conv_w": "conv_w", "g_attn_out": "g_attn_out",
             "g_conv_out": "g_conv_out", "w_out": "w_out", "g_xattn": "g_xattn", "g_mem": "g_mem",
             "w_q_mem": "w_q", "w_kv_mem": "w_kv", "w_o_mem": "w_o", "g_mlp": "g_mlp", "w_up": "w_up",
             "w_down": "w_down", "g_final": "g_final"}
    result = [loss, grad_x[None]]
    for which in range(4):
        for key in names.values():
            result.append(outs[key][which] if key in outs else small_out[which][key])
    return tuple(result)
```

```python
import math

import jax
import jax.numpy as jnp
from jax import lax
from jax.experimental import pallas as pl
from jax.experimental.pallas import tpu as pltpu

F32 = jnp.float32
BF16 = jnp.bfloat16
NORM_EPS = 1e-6
NEG_INF = -1e30
N_DEV = 8
BLK = 128
HEAD_DIM = 64
N_MEM_HEADS = 4
ADAM_LR = 0.001
ADAM_B1 = 0.9
ADAM_B2 = 0.999
ADAM_EPS = 1e-08
ADAM_WD = 0.01
ADAM_STEP = 10
MESH = pl.DeviceIdType.MESH
ANY = pl.BlockSpec(memory_space=pl.ANY)


def _dot(a, b):
    return jnp.dot(a, b, preferred_element_type=F32)


def _dot_nt(a, b):
    return lax.dot_general(a, b, (((1,), (1,)), ((), ())), preferred_element_type=F32)


def _dot_tn(a, b):
    return lax.dot_general(a, b, (((0,), (0,)), ((), ())), preferred_element_type=F32)


def _params(semantics, vmem_mb):
    return pltpu.CompilerParams(dimension_semantics=semantics, vmem_limit_bytes=vmem_mb << 20)


def _rms_fwd(x, g):
    r = lax.rsqrt(jnp.mean(x * x, axis=-1, keepdims=True) + NORM_EPS)
    xh = x * r
    return xh * g, xh, r


def _rms_bwd(dy, xh, r, g):
    gy = dy * g
    return r * (gy - xh * jnp.mean(xh * gy, axis=-1, keepdims=True))


def _position():
    x, y, c = lax.axis_index("x"), lax.axis_index("y"), lax.axis_index("c")
    return x, y, c


def _block_of(ref, j, axis, shard_shape):
    r, c = shard_shape
    if axis is None:
        return ref.at[j]
    if axis == 0:
        return ref.at[pl.ds(j * r, r), :]
    return ref.at[:, pl.ds(j * c, c)]


class _Gather:
    has_mid = True
    alias_pairs = ()

    def __init__(self, shards, axes, late=False):
        self.arrays = list(shards)
        self.axes = list(axes)
        self.late = late
        self.n = len(self.arrays)

    def out_shape(self):
        res = []
        for s, axis in zip(self.arrays, self.axes):
            r, c = s.shape
            shape = (N_DEV, r, c) if axis is None else (N_DEV * r, c) if axis == 0 else (r, N_DEV * c)
            res.append(jax.ShapeDtypeStruct(shape, s.dtype))
        return res

    def scratch(self):
        return [pltpu.SemaphoreType.DMA((self.n, 7)), pltpu.SemaphoreType.DMA((self.n, 7)),
                pltpu.SemaphoreType.DMA((self.n,))]

    def _ctx(self, ins, outs, sems):
        send_sems, recv_sems, local_sems = sems
        x, y, c = _position()
        me, sibling = (x, y, c), (x, y, 1 - c)
        chips = [(1 - x, y), (x, 1 - y), (1 - x, 1 - y)]

        def lin(px, py, pc):
            return 4 * px + 2 * py + pc

        def place(a, block):
            return _block_of(outs[a], lin(*block), self.axes[a], self.arrays[a].shape)

        def copy(a, k, block, to, src=None):
            dst = place(a, block)
            return pltpu.make_async_remote_copy(
                src_ref=dst if src is None else src, dst_ref=dst,
                send_sem=send_sems.at[a, k], recv_sem=recv_sems.at[a, k],
                device_id=to, device_id_type=MESH)

        def mine():
            return [pltpu.make_async_copy(ins[a], place(a, me), local_sems.at[a]) for a in range(self.n)]

        def first():
            res = []
            for a in range(self.n):
                res.append(copy(a, 0, me, sibling, src=ins[a]))
                res += [copy(a, 1 + j, me, (*chip, c), src=ins[a]) for j, chip in enumerate(chips)]
            return res

        return c, me, sibling, chips, copy, mine, first

    def start(self, ins, outs, sems):
        _, _, _, _, _, mine, first = self._ctx(ins, outs, sems)
        for cp in mine() + first():
            cp.start()

    def mid(self, ins, outs, sems):
        c, me, sibling, chips, copy, _, _ = self._ctx(ins, outs, sems)
        for j, chip in enumerate(chips):
            for a in range(self.n):
                copy(a, 1 + j, (*chip, c), me).wait_recv()
                copy(a, 4 + j, (*chip, c), sibling).start()

    def finish(self, ins, outs, sems):
        c, me, sibling, chips, copy, mine, first = self._ctx(ins, outs, sems)
        for a in range(self.n):
            copy(a, 0, sibling, me).wait_recv()
            for j, chip in enumerate(chips):
                copy(a, 4 + j, (*chip, 1 - c), me).wait_recv()
        for cp in first():
            cp.wait_send()
        for j, chip in enumerate(chips):
            for a in range(self.n):
                copy(a, 4 + j, (*chip, c), sibling).wait_send()
        for cp in mine():
            cp.wait()


class _Exchange:
    def __init__(self, parts, axes):
        self.n = len(parts)
        self.axes = list(axes)
        self.arrays = list(parts)

    def _piece(self, a):
        r, c = self.arrays[a].shape
        axis = self.axes[a]
        return (r, c) if axis is None else (r // N_DEV, c) if axis == 0 else (r, c // N_DEV)

    def out_shape(self):
        return [jax.ShapeDtypeStruct((N_DEV,) + self._piece(a), self.arrays[a].dtype) for a in range(self.n)]

    def semaphores(self):
        return [pltpu.SemaphoreType.DMA((7 * self.n,)), pltpu.SemaphoreType.DMA((7 * self.n,)),
                pltpu.SemaphoreType.DMA((self.n,))]

    def _ctx(self, ins, outs, sems):
        send_sems, recv_sems, local_sems = sems
        x, y, c = _position()
        me = 4 * x + 2 * y + c

        def src(a, j):
            return ins[a] if self.axes[a] is None else _block_of(ins[a], j, self.axes[a], self._piece(a))

        def dst(a, j):
            return outs[a].at[j]

        def local():
            return [pltpu.make_async_copy(src(a, me), dst(a, me), local_sems.at[a]) for a in range(self.n)]

        def remote(inbound):
            res = []
            for a in range(self.n):
                for k in range(1, N_DEV):
                    peer = (1 - x if k & 4 else x, 1 - y if k & 2 else y, 1 - c if k & 1 else c)
                    plin = 4 * peer[0] + 2 * peer[1] + peer[2]
                    res.append(pltpu.make_async_remote_copy(
                        src_ref=src(a, plin), dst_ref=dst(a, plin if inbound else me),
                        send_sem=send_sems.at[7 * a + k - 1], recv_sem=recv_sems.at[7 * a + k - 1],
                        device_id=peer, device_id_type=MESH))
            return res

        return local, remote

    def start(self, ins, outs, sems):
        local, remote = self._ctx(ins, outs, sems)
        for cp in local() + remote(False):
            cp.start()

    def finish(self, ins, outs, sems):
        local, remote = self._ctx(ins, outs, sems)
        for cp in remote(True):
            cp.wait_recv()
        for cp in remote(False):
            cp.wait_send()
        for cp in local():
            cp.wait()


def _exchange_start(rider, name):
    n = rider.n
    parts = rider.arrays
    lands = [lax.empty(s.shape, s.dtype) for s in rider.out_shape()]
    hbm = pl.BlockSpec(memory_space=pltpu.HBM)
    sem = pl.BlockSpec(memory_space=pltpu.SEMAPHORE)

    def body(*refs):
        ins, sems = refs[:n], refs[2 * n:2 * n + 3]
        outs, token = refs[2 * n + 3 + n:2 * n + 3 + 2 * n], refs[-1]
        rider.start(ins, outs, sems)
        token[...] = jnp.zeros_like(token)

    res = pl.pallas_call(
        body, name=name,
        out_shape=rider.semaphores() + [pltpu.HBM(p.shape, p.dtype) for p in parts]
                  + [pltpu.HBM(z.shape, z.dtype) for z in lands] + [jax.ShapeDtypeStruct((8, 128), F32)],
        in_specs=[hbm] * (2 * n), out_specs=[sem] * 3 + [hbm] * (2 * n) + [pl.BlockSpec(memory_space=pltpu.VMEM)],
        input_output_aliases={i: 3 + i for i in range(2 * n)},
        compiler_params=pltpu.CompilerParams(has_side_effects=pltpu.SideEffectType.DATAFLOW_SIDE_EFFECTING),
    )(*[pltpu.with_memory_space_constraint(a, pltpu.HBM) for a in parts + lands])
    return res[:3], res[3:3 + n], res[3 + n:3 + 2 * n], res[-1]


def _exchange_wait(rider, started, after, name):
    n = rider.n
    sems, parts, lands, _ = started
    hbm = pl.BlockSpec(memory_space=pltpu.HBM)
    sem = pl.BlockSpec(memory_space=pltpu.SEMAPHORE)

    def body(*refs):
        rider.finish(refs[:n], refs[n:2 * n], refs[2 * n:2 * n + 3])

    res = pl.pallas_call(
        body, name=name, out_shape=[pltpu.HBM(a.shape, a.dtype) for a in list(parts) + list(lands)],
        in_specs=[hbm] * (2 * n) + [sem] * 3 + [ANY] * len(after), out_specs=[hbm] * (2 * n),
        input_output_aliases={i: i for i in range(2 * n)},
        compiler_params=pltpu.CompilerParams(has_side_effects=pltpu.SideEffectType.DATAFLOW_SIDE_EFFECTING),
    )(*parts, *lands, *sems, *after)
    return list(res[n:])


def _pcall(body, *, name, grid, in_specs, out_specs, out_shape, scratch_shapes=(), semantics, vmem_mb, rider=None,
           aliases=None, after=()):
    in_specs, out_specs, out_shape = list(in_specs), list(out_specs), list(out_shape)
    scratch_shapes = list(scratch_shapes)
    aliases = dict(aliases or {})
    if rider is None:
        n_in, after = len(in_specs), list(after)

        def plain(*refs):
            body(*refs[:n_in], *refs[n_in + len(after):])

        call = pl.pallas_call(plain if after else body, name=name, grid=grid, in_specs=in_specs + [ANY] * len(after),
                              out_specs=out_specs, out_shape=out_shape, scratch_shapes=scratch_shapes,
                              input_output_aliases=aliases, compiler_params=_params(semantics, vmem_mb))
        return lambda *args: (list(call(*args, *after)), None)
    n_in, n_out, n_scr = len(in_specs), len(out_specs), len(scratch_shapes)
    r_in, r_shapes = len(rider.arrays), rider.out_shape()
    r_out = len(r_shapes)
    aliases.update({n_in + i: n_out + o for i, o in rider.alias_pairs})
    total = math.prod(grid)
    mid_step = total - 1 if rider.has_mid and rider.late else (3 * total) // 4

    def wrapped(*refs):
        bounds = [0, n_in, r_in, n_out, r_out, n_scr]
        for i in range(1, len(bounds)):
            bounds[i] += bounds[i - 1]
        a, ra, o, ro, s = (refs[bounds[i]:bounds[i + 1]] for i in range(5))
        rs = refs[bounds[5]:]
        step = pl.program_id(0)
        for k in range(1, len(grid)):
            step = step * grid[k] + pl.program_id(k)
        pl.when(step == 0)(lambda: rider.start(ra, ro, rs))
        body(*a, *o, *s)
        if rider.has_mid:
            pl.when(step == mid_step)(lambda: rider.mid(ra, ro, rs))
        pl.when(step == total - 1)(lambda: rider.finish(ra, ro, rs))

    call = pl.pallas_call(
        wrapped, name=name, grid=grid, in_specs=in_specs + [ANY] * r_in, out_specs=out_specs + [ANY] * r_out,
        out_shape=out_shape + r_shapes, scratch_shapes=scratch_shapes + rider.scratch(),
        input_output_aliases=aliases, compiler_params=_params(("arbitrary",) * len(grid), vmem_mb))

    def run(*args):
        res = call(*args, *rider.arrays)
        return list(res[:n_out]), list(res[n_out:])

    return run


def _norm_matmul(x, g, w, *, name, out_dtype, tb, bn, relu=False, save_h=False, rider=None):
    t, d = x.shape
    n = w.shape[1]

    def body(x_ref, g_ref, w_ref, o_ref, *rest):
        h_scr = rest[-1]

        @pl.when(pl.program_id(1) == 0)
        def _():
            h = _rms_fwd(x_ref[...], g_ref[...])[0].astype(BF16)
            h_scr[...] = h
            if save_h:
                rest[0][...] = h

        acc = _dot(h_scr[...], w_ref[...])
        if relu:
            acc = jnp.maximum(acc, 0.0)
        o_ref[...] = acc.astype(out_dtype)

    out_shape = [jax.ShapeDtypeStruct((t, n), out_dtype)]
    out_specs = [pl.BlockSpec((tb, bn), lambda i, j: (i, j))]
    if save_h:
        out_shape.append(jax.ShapeDtypeStruct((t, d), BF16))
        out_specs.append(pl.BlockSpec((tb, d), lambda i, j: (i, 0)))
    res, extra = _pcall(
        body, name=name, grid=(t // tb, n // bn),
        in_specs=[pl.BlockSpec((tb, d), lambda i, j: (i, 0)),
                  pl.BlockSpec((1, d), lambda i, j: (0, 0)),
                  pl.BlockSpec((d, bn), lambda i, j: (0, j))],
        out_specs=out_specs, out_shape=out_shape,
        scratch_shapes=[pltpu.VMEM((tb, d), BF16)],
        semantics=("parallel", "arbitrary"), vmem_mb=48, rider=rider,
    )(x, g, w)
    res = res if save_h else res[0]
    return res if rider is None else (res, extra)


def _proj(x, g, w, *, tb, rider=None):
    t, d = x.shape
    half = w.shape[1] // 2

    def body(x_ref, g_ref, w_ref, qkv_ref, gates_ref, h_ref, h_scr):
        j = pl.program_id(1)

        @pl.when(j == 0)
        def _():
            h = _rms_fwd(x_ref[...], g_ref[...])[0].astype(BF16)
            h_scr[...] = h
            h_ref[...] = h

        acc = _dot(h_scr[...], w_ref[...])

        @pl.when(j == 0)
        def _():
            qkv_ref[...] = acc

        @pl.when(j == 1)
        def _():
            gates_ref[...] = acc.astype(BF16)

    tok = lambda c: pl.BlockSpec((tb, c), lambda i, j: (i, 0))
    res, extra = _pcall(
        body, name="proj", grid=(t // tb, 2),
        in_specs=[tok(d), pl.BlockSpec((1, d), lambda i, j: (0, 0)), pl.BlockSpec((d, half), lambda i, j: (0, j))],
        out_specs=[tok(half), tok(half), tok(d)],
        out_shape=[jax.ShapeDtypeStruct((t, half), F32), jax.ShapeDtypeStruct((t, half), BF16),
                   jax.ShapeDtypeStruct((t, d), BF16)],
        scratch_shapes=[pltpu.VMEM((tb, d), BF16)],
        semantics=("parallel", "arbitrary"), vmem_mb=48, rider=rider,
    )(x, g, w)
    return res if rider is None else (res, extra)


def _matmul_nt_normbwd(dy, w, x, g, dres, *, name, tb, to_natural=False, after=()):
    t, d = x.shape
    stacked = dy.ndim == 3
    has_res = dres is not None
    n_i = SEG // TI
    if to_natural:
        tb = N_RES * TI

    def body(dy_ref, w_ref, x_ref, g_ref, *rest):
        rest = list(rest)
        dres_ref = rest.pop(0) if has_res else None
        dx_ref = rest.pop(0)
        gg_ref = rest.pop(0)
        i = pl.program_id(0)

        def rows(ref, *lead):
            v = ref[lead] if lead else ref[...]
            return v[0].reshape(tb, v.shape[-1]) if to_natural else v

        if stacked:
            kb = dy_ref.shape[-1]
            dh = _dot_nt(rows(dy_ref, 0), w_ref[:, 0:kb])
            for s in range(1, dy_ref.shape[0]):
                dh = dh + _dot_nt(rows(dy_ref, s), w_ref[:, s * kb:(s + 1) * kb])
        else:
            dh = _dot_nt(rows(dy_ref), w_ref[...])
        g_v = g_ref[...]
        _, xh, r = _rms_fwd(rows(x_ref), g_v)
        dx = _rms_bwd(dh, xh, r, g_v)
        if has_res:
            dx = dx + rows(dres_ref)
        if to_natural:
            scr = rest.pop(0)
            for cb in range(d // BLK):
                cols = slice(cb * BLK, (cb + 1) * BLK)
                slab = scr.at[cb]
                for res in range(N_RES):
                    slab[pl.ds(res, TI, stride=N_RES), :] = dx[res * TI:(res + 1) * TI, cols]
                dx_ref[:, cols] = slab[...]
        else:
            dx_ref[...] = dx
        part = jnp.sum(dh * xh, axis=0, keepdims=True)

        @pl.when(i == 0)
        def _():
            gg_ref[...] = part

        @pl.when(i != 0)
        def _():
            gg_ref[...] += part

    tok = pl.BlockSpec((tb, d), lambda i: (i, 0))
    row = pl.BlockSpec((1, d), lambda i: (0, 0))
    if to_natural:
        act = pl.BlockSpec((1, N_RES, TI, d), lambda i: (i // n_i, 0, i % n_i, 0))
        dy_spec = pl.BlockSpec((dy.shape[0], 1, N_RES, TI, dy.shape[2]), lambda i: (0, i // n_i, 0, i % n_i, 0))
        dy, x = dy.reshape(dy.shape[0], t // HALF, N_RES, SEG, dy.shape[2]), _x4(x)
        dres = _x4(dres) if has_res else None
    elif stacked:
        act, dy_spec = tok, pl.BlockSpec((dy.shape[0], tb, dy.shape[2]), lambda i: (0, i, 0))
    else:
        act, dy_spec = tok, pl.BlockSpec((tb, dy.shape[1]), lambda i: (i, 0))
    in_specs = [dy_spec, pl.BlockSpec(w.shape, lambda i: (0, 0)), act, row]
    args = [dy, w, x, g]
    if has_res:
        in_specs.append(act)
        args.append(dres)
    res, _ = _pcall(
        body, name=name, grid=(t // tb,), in_specs=in_specs, out_specs=[tok, row],
        out_shape=[jax.ShapeDtypeStruct((t, d), F32), jax.ShapeDtypeStruct((1, d), F32)],
        scratch_shapes=[pltpu.VMEM((d // BLK, tb, BLK), F32)] if to_natural else [],
        semantics=("arbitrary",), vmem_mb=56, after=after,
    )(*args)
    return res


def _matmul_tn(a, b, *, name, bm, bn, square_a=False, after=()):
    t, m = a.shape
    stacked = b.ndim == 3
    n = b.shape[0] * bn if stacked else b.shape[1]

    def body(a_ref, b_ref, o_ref):
        av = a_ref[...]
        if square_a:
            av = av.astype(F32)
            av = (av * av).astype(BF16)
        o_ref[...] = _dot_tn(av, b_ref[...]).astype(BF16)

    res, _ = _pcall(
        body, name=name, grid=(m // bm, n // bn),
        in_specs=[pl.BlockSpec((t, bm), lambda i, j: (0, i)),
                  pl.BlockSpec((None, t, bn), lambda i, j: (j, 0, 0)) if stacked
                  else pl.BlockSpec((t, bn), lambda i, j: (0, j))],
        out_specs=[pl.BlockSpec((bm, bn), lambda i, j: (i, j))], out_shape=[jax.ShapeDtypeStruct((m, n), BF16)],
        semantics=("parallel", "parallel"), vmem_mb=56, after=after,
    )(a, b)
    return res[0]


N_RES = 16
SEG = 128
HALF = N_RES * SEG
TI = 32
HALO = 16


def _x4(a):
    return a.reshape(a.shape[0] // HALF, N_RES, SEG, a.shape[1])


def _reorder(arrays, name, rider=None):
    t, c = arrays[0].shape
    n = len(arrays)
    n_i = SEG // TI

    def body(*refs):
        scr = refs[-1]
        for i_ref, o_ref in zip(refs[:n], refs[n:2 * n]):
            for cb in range(c // BLK):
                cols = slice(cb * BLK, (cb + 1) * BLK)
                slab = scr.at[cb]
                slab[...] = i_ref[:, cols]
                for r in range(N_RES):
                    o_ref[0, r, :, cols] = slab[pl.ds(r, TI, stride=N_RES), :]

    res, extra = _pcall(
        body, name=name, grid=(t // (TI * N_RES),),
        in_specs=[pl.BlockSpec((TI * N_RES, c), lambda s: (s, 0))] * n,
        out_specs=[pl.BlockSpec((1, N_RES, TI, c), lambda s: (s // n_i, 0, s % n_i, 0))] * n,
        out_shape=[jax.ShapeDtypeStruct((t // HALF, N_RES, SEG, c), F32)] * n,
        scratch_shapes=[pltpu.VMEM((c // BLK, TI * N_RES, BLK), F32)],
        semantics=("parallel",), vmem_mb=32, rider=rider,
    )(*arrays)
    res = [r.reshape(t, c) for r in res]
    return res if rider is None else (res, extra)


_PATTERNS = ((1, 16, 8, SEG), (4, 4, 32, 4 * SEG), (16, 1, SEG, 0))
_FIRST = {1: 1, 4: 4, 16: 16}


def _group_rows(d, g):
    a = g >> 4
    if d == 16:
        base = a * HALF + (g & 15) * SEG
        prev = base - HALF
    elif d == 4:
        c = (g >> 2) & 3
        base = a * HALF + (g & 3) * SEG + c * 32
        prev = jnp.where(c > 0, base - 32, base - HALF + 96)
    else:
        c = g & 15
        base = a * HALF + c * 8
        prev = jnp.where(c > 0, base - 8, base - HALF + 120)
    return base, prev


def _load_rows(ref, base, n, rows, stride):
    parts = [ref[pl.ds(pl.multiple_of(base + j * stride, 8), rows), :] for j in range(n)]
    return parts[0] if n == 1 else jnp.concatenate(parts, axis=0)


def _store_rows(ref, base, val, n, rows, stride, add=False):
    for j in range(n):
        sl = pl.ds(pl.multiple_of(base + j * stride, 8), rows)
        piece = val[j * rows:(j + 1) * rows, :]
        if add:
            ref[sl, :] += piece
        else:
            ref[sl, :] = piece


def _band_bias(n, rows):
    shift = rows.bit_length() - 1
    lq = lax.broadcasted_iota(jnp.int32, (BLK, BLK), 0)
    lk = lax.broadcasted_iota(jnp.int32, (BLK, BLK), 1)
    iq = (lq & (rows - 1)) * n + (lq >> shift)
    ik = (lk & (rows - 1)) * n + (lk >> shift)
    zero = jnp.zeros((BLK, BLK), F32)
    return jnp.where(ik >= iq, zero, NEG_INF), jnp.where(ik <= iq, zero, NEG_INF)


def _set_bias(bias_scr, n, rows):
    prev_b, cur_b = _band_bias(n, rows)
    for half in range(2):
        bias_scr[half * BLK:(half + 1) * BLK, 0:BLK] = prev_b
        bias_scr[half * BLK:(half + 1) * BLK, BLK:2 * BLK] = cur_b


SCALE = 1.0 / math.sqrt(HEAD_DIM)


def _head_consts(value=1.0):
    lane_lo = lax.broadcasted_iota(jnp.int32, (BLK, BLK), 1) < HEAD_DIM
    return lane_lo, [jnp.where(lane_lo, value, 0.0).astype(BF16), jnp.where(lane_lo, 0.0, value).astype(BF16)]


def _stack_heads(v, head_mask):
    return jnp.concatenate([v * head_mask[0], v * head_mask[1]], axis=0)


def _unstack_heads(v2, lane_lo):
    return jnp.where(lane_lo, v2[:BLK], v2[BLK:])


def _rows_per_head(v, lane_lo):
    rolled = pltpu.roll(v, HEAD_DIM, axis=1)
    return jnp.concatenate([jnp.where(lane_lo, v, rolled), jnp.where(lane_lo, rolled, v)], axis=0)


WIDTH = 4


def _loop(lo, hi, fn, width=None):
    if width is None:
        def body(g, carry):
            fn(g)
            return carry

        if hi > lo:
            lax.fori_loop(lo, hi, body, 0)
        return
    while hi > lo:
        trips = (hi - lo) // width
        if trips:
            def body(i, carry, lo=lo, width=width):
                fn([lo + width * i + j for j in range(width)])
                return carry

            lax.fori_loop(0, trips, body, 0)
            lo += trips * width
        width = max(1, width // 2)


def _mix_weights(l1, l2, l3):
    mx = jnp.maximum(jnp.maximum(l1, l2), l3)
    e1, e2, e3 = jnp.exp(l1 - mx), jnp.exp(l2 - mx), jnp.exp(l3 - mx)
    inv = 1.0 / (e1 + e2 + e3)
    return e1 * inv, e2 * inv, e3 * inv


def _attention_fwd(qkv, rider=None):
    t = qkv.shape[0]
    groups = 16 * (t // HALF)

    def body(q_ref, k_ref, v_ref, attn_ref, l1_ref, l2_ref, l3_ref, o_scr, bias_scr):
        lane_lo, q_mask = _head_consts(SCALE)
        l_refs = (l1_ref, l2_ref, l3_ref)
        for p, (d, n, rows, stride) in enumerate(_PATTERNS):
            _set_bias(bias_scr, n, rows)
            o_p, l_p = o_scr.at[p], l_refs[p]

            def block(gs, has_prev):
                at = [_group_rows(d, g) for g in gs]

                def load(ref, b):
                    return _load_rows(ref, b, n, rows, stride).astype(BF16)

                q2 = [_stack_heads(load(q_ref, b), q_mask) for b, _ in at]
                k2 = [load(k_ref, b) for b, _ in at]
                v2 = [load(v_ref, b) for b, _ in at]
                if has_prev:
                    k2 = [jnp.concatenate([load(k_ref, pv), k], axis=0) for (_, pv), k in zip(at, k2)]
                    v2 = [jnp.concatenate([load(v_ref, pv), v], axis=0) for (_, pv), v in zip(at, v2)]
                s = [_dot_nt(q, k) for q, k in zip(q2, k2)]
                s = [x + (bias_scr[...] if has_prev else bias_scr[:, BLK:2 * BLK]) for x in s]
                mx = [jnp.max(x, axis=1, keepdims=True) for x in s]
                e = [jnp.exp(x - m) for x, m in zip(s, mx)]
                den = [jnp.sum(x, axis=1, keepdims=True) for x in e]
                o2 = [_dot(x.astype(BF16), v) * (1.0 / dn) for x, v, dn in zip(e, v2, den)]
                lse2 = [jnp.broadcast_to(m + jnp.log(dn), (2 * BLK, BLK)) for m, dn in zip(mx, den)]
                for (b, _), o, l in zip(at, o2, lse2):
                    _store_rows(o_p, b, _unstack_heads(o, lane_lo), n, rows, stride)
                    _store_rows(l_p, b, _unstack_heads(l, lane_lo), n, rows, stride)

            _loop(0, _FIRST[d], lambda gs: block(gs, False), width=2 * WIDTH)
            _loop(_FIRST[d], groups, lambda gs: block(gs, True), width=2 * WIDTH)

        def mix(i):
            sl = pl.ds(pl.multiple_of(i * 256, 256), 256)
            w = _mix_weights(l1_ref[sl, :], l2_ref[sl, :], l3_ref[sl, :])
            attn_ref[sl, :] = w[0] * o_scr[0, sl, :] + w[1] * o_scr[1, sl, :] + w[2] * o_scr[2, sl, :]

        _loop(0, t // 256, mix)

    def col(c0):
        return pl.BlockSpec((t, BLK), lambda hp: (0, c0 + hp))

    res, extra = _pcall(
        body, name="attention_fwd", grid=(4,), in_specs=[col(0), col(4), col(8)], out_specs=[col(0)] * 4,
        out_shape=[jax.ShapeDtypeStruct((t, 512), F32)] * 4,
        scratch_shapes=[pltpu.VMEM((3, t, BLK), F32), pltpu.VMEM((2 * BLK, 2 * BLK), F32)],
        semantics=("parallel",), vmem_mb=48, rider=rider,
    )(qkv, qkv, qkv)
    return res if rider is None else (res, extra)


def _attention_bwd(qkv, dattn, dsum, lses, dproj):
    t = qkv.shape[0]
    groups = 16 * (t // HALF)

    def body(q_ref, k_ref, v_ref, da_ref, ds_ref, l1_ref, l2_ref, l3_ref, kept_ref, out_ref, acc, bias_scr):
        del kept_ref
        lane_lo, head_mask = _head_consts()
        q_mask = _head_consts(SCALE)[1]
        l_refs = (l1_ref, l2_ref, l3_ref)

        def clear(i):
            sl = pl.ds(pl.multiple_of(i * 512, 512), 512)
            for s in range(3):
                acc[s, sl, :] = jnp.zeros((512, BLK), F32)

        _loop(0, t // 512, clear)
        dq_acc, dk_acc, dv_acc = acc.at[0], acc.at[1], acc.at[2]
        for p, (d, n, rows, stride) in enumerate(_PATTERNS):
            _set_bias(bias_scr, n, rows)

            def block(gs, has_prev):
                at = [_group_rows(d, g) for g in gs]

                def load(ref, b):
                    return _load_rows(ref, b, n, rows, stride)

                def put(ref, b, val):
                    _store_rows(ref, b, val, n, rows, stride, add=True)

                def wide(x):
                    return jnp.concatenate([x, x], axis=1) if has_prev else x

                lse = [[load(ref, b) for ref in l_refs] for b, _ in at]
                w = [_mix_weights(*ls)[p] for ls in lse]
                do2 = [_stack_heads((wg * load(da_ref, b)).astype(BF16), head_mask) for wg, (b, _) in zip(w, at)]
                dl2 = [wide(_rows_per_head(wg * load(ds_ref, b), lane_lo)) for wg, (b, _) in zip(w, at)]
                lse2 = [wide(_rows_per_head(ls[p], lane_lo)) for ls in lse]
                q2 = [_stack_heads(load(q_ref, b).astype(BF16), q_mask) for b, _ in at]
                k2 = [load(k_ref, b).astype(BF16) for b, _ in at]
                v2 = [load(v_ref, b).astype(BF16) for b, _ in at]
                if has_prev:
                    k2 = [jnp.concatenate([load(k_ref, pv).astype(BF16), k], axis=0) for (_, pv), k in zip(at, k2)]
                    v2 = [jnp.concatenate([load(v_ref, pv).astype(BF16), v], axis=0) for (_, pv), v in zip(at, v2)]
                s = [_dot_nt(q, k) for q, k in zip(q2, k2)]
                dp = [_dot_nt(do, v) for do, v in zip(do2, v2)]
                pr = [jnp.exp(x + (bias_scr[...] if has_prev else bias_scr[:, BLK:2 * BLK]) - l)
                      for x, l in zip(s, lse2)]
                ds = [(pg * (x - dl)).astype(BF16) for pg, x, dl in zip(pr, dp, dl2)]
                dq2 = [_dot(x, k) * SCALE for x, k in zip(ds, k2)]
                dk2 = [_dot_tn(x, q) for x, q in zip(ds, q2)]
                dv2 = [_dot_tn(pg.astype(BF16), do) for pg, do in zip(pr, do2)]
                for (b, pv), dq, dk, dv in zip(at, dq2, dk2, dv2):
                    put(dq_acc, b, _unstack_heads(dq, lane_lo))
                    if has_prev:
                        put(dk_acc, pv, dk[:BLK])
                        put(dv_acc, pv, dv[:BLK])
                        put(dk_acc, b, dk[BLK:])
                        put(dv_acc, b, dv[BLK:])
                    else:
                        put(dk_acc, b, dk)
                        put(dv_acc, b, dv)

            _loop(0, _FIRST[d], lambda gs: block(gs, False), width=WIDTH)
            _loop(_FIRST[d], groups, lambda gs: block(gs, True), width=WIDTH)

        def emit(i):
            sl = pl.ds(pl.multiple_of(i * 512, 512), 512)
            for s in range(3):
                out_ref[s, sl, :] = acc[s, sl, :].astype(BF16)

        _loop(0, t // 512, emit)

    def col(c0):
        return pl.BlockSpec((t, BLK), lambda hp: (0, c0 + hp))

    res, _ = _pcall(
        body, name="attention_bwd", grid=(4,),
        in_specs=[col(0), col(4), col(8)] + [col(0)] * 5 + [ANY],
        out_specs=[pl.BlockSpec((3, t, BLK), lambda hp: (0, 0, hp))],
        out_shape=[jax.ShapeDtypeStruct(dproj.shape, BF16)],
        scratch_shapes=[pltpu.VMEM((3, t, BLK), F32), pltpu.VMEM((2 * BLK, 2 * BLK), F32)],
        semantics=("parallel",), vmem_mb=56, aliases={8: 0},
    )(qkv, qkv, qkv, dattn, dsum, *lses, dproj)
    return res[0]


def _order_specs(t):
    n_i = SEG // TI
    nblk = (t // HALF) * n_i
    per = TI // HALO

    def main(c, col=0):
        return pl.BlockSpec((1, N_RES, TI, c), lambda s: (s // n_i, 0, s % n_i, col))

    def before(c, col=0):
        return pl.BlockSpec((1, 2, HALO, c), lambda s: (jnp.maximum(s - 1, 0) // n_i, N_RES // 2 - 1,
                                                        (jnp.maximum(s - 1, 0) % n_i) * per + per - 1, col))

    def after(c, col=0):
        return pl.BlockSpec((1, 2, HALO, c), lambda s: (jnp.minimum(s + 1, nblk - 1) // n_i, 0,
                                                        (jnp.minimum(s + 1, nblk - 1) % n_i) * per, col))

    return nblk, main, before, after


def _shift_in(v, row_in, up):
    rows = v.shape[0]
    idx = lax.broadcasted_iota(jnp.int32, v.shape, 0)
    fill = jnp.broadcast_to(row_in, v.shape)
    if up:
        return jnp.where(idx == rows - 1, fill, pltpu.roll(v, rows - 1, axis=0))
    return jnp.where(idx == 0, fill, pltpu.roll(v, 1, axis=0))


def _taps_behind(u, before):
    s15 = _shift_in(u[N_RES - 1], before[1, HALO - 1:HALO, :], up=False)
    s14 = _shift_in(u[N_RES - 2], before[0, HALO - 1:HALO, :], up=False)
    m1 = jnp.concatenate([s15[None], u[:N_RES - 1]], axis=0)
    m2 = jnp.concatenate([s14[None], s15[None], u[:N_RES - 2]], axis=0)
    return m1, m2


def _taps_ahead(u, after):
    t0 = _shift_in(u[0], after[0, 0:1, :], up=True)
    t1 = _shift_in(u[1], after[1, 0:1, :], up=True)
    p1 = jnp.concatenate([u[1:], t0[None]], axis=0)
    p2 = jnp.concatenate([u[2:], t0[None], t1[None]], axis=0)
    return p1, p2


def _conv_fwd(gates, before, first, cw):
    gates, before = gates.astype(F32), before.astype(F32)
    bg, cg, xc = gates[..., 0:512], gates[..., 512:1024], gates[..., 1024:1536]
    u = cg * xc
    ub = before[..., 512:1024] * before[..., 1024:1536]
    ub = jnp.where(first, jnp.zeros_like(ub), ub)
    m1, m2 = _taps_behind(u, ub)
    conv = m2 * cw[0:1, :] + m1 * cw[1:2, :] + u * cw[2:3, :]
    return bg, u, m1, m2, conv


def _sum_tokens(v):
    return jnp.sum(jnp.sum(v, axis=0), axis=0, keepdims=True)


def _mixer_fwd(x, attn, gates, cw, g_a, g_c, w_out):
    t, d = x.shape
    nblk, main, before, _ = _order_specs(t)
    rows = N_RES * TI

    def body(x_ref, at_ref, gt_ref, gb_ref, cw_ref, ga_ref, gc_ref, wa_ref, wb_ref, x1_ref, mg_ref):
        an = _rms_fwd(at_ref[0], ga_ref[...])[0].astype(BF16)
        bg, _, _, _, conv = _conv_fwd(gt_ref[0], gb_ref[0], pl.program_id(0) == 0, cw_ref[...])
        cn = _rms_fwd(bg * conv, gc_ref[...])[0].astype(BF16)
        mg_ref[0, :, :, 0:512] = an
        mg_ref[0, :, :, 512:1024] = cn
        y = _dot(an.reshape(rows, 512), wa_ref[...]) + _dot(cn.reshape(rows, 512), wb_ref[...])
        x1_ref[0] = x_ref[0] + y.reshape(N_RES, TI, d)

    const = lambda r, c, i0=0: pl.BlockSpec((r, c), lambda s: (i0, 0))
    x1, merged = pl.pallas_call(
        body, name="mixer_fwd", grid=(nblk,),
        in_specs=[main(d), main(512), main(1536), before(1536), const(3, 512), const(1, 512), const(1, 512),
                  const(512, d), const(512, d, 1)],
        out_specs=[main(d), main(d)],
        out_shape=[jax.ShapeDtypeStruct(_x4(x).shape, F32), jax.ShapeDtypeStruct(_x4(x).shape, BF16)],
        compiler_params=_params(("parallel",), 48),
    )(_x4(x), _x4(attn), _x4(gates), _x4(gates), cw, g_a, g_c, w_out, w_out)
    return x1.reshape(t, d), merged.reshape(t, d)


def _mixer_bwd(dx1, merged, attn, gates, cw, g_a, g_c, w_out, head_sum, after=()):
    t, d = dx1.shape
    nblk, main, before, _ = _order_specs(t)
    rows = N_RES * TI

    def body(dx_ref, mg_ref, at_ref, gt_ref, gb_ref, cw_ref, ga_ref, gc_ref, wa_ref, wb_ref, hs_ref,
             da_ref, dsum_ref, dy_ref, gga_ref, ggc_ref, gw_ref, acc_w):
        s = pl.program_id(0)
        dxb = dx_ref[0].reshape(rows, d).astype(BF16)

        @pl.when(s == 0)
        def _():
            acc_w[...] = jnp.zeros_like(acc_w)

        acc_w[...] += _dot_tn(mg_ref[0].reshape(rows, d), dxb)

        @pl.when(s == nblk - 1)
        def _():
            gw_ref[...] = acc_w[...].astype(BF16)

        dma = _dot_nt(dxb, wa_ref[...]).reshape(N_RES, TI, 512)
        dmc = _dot_nt(dxb, wb_ref[...]).reshape(N_RES, TI, 512)
        attn_v, g_av = at_ref[0], ga_ref[...]
        _, ah, ra = _rms_fwd(attn_v, g_av)
        dattn = _rms_bwd(dma, ah, ra, g_av)
        da_ref[0] = dattn
        z = (dattn * attn_v).reshape(rows, 512)
        hs = hs_ref[...]
        z1 = z.astype(BF16)
        z2 = (z - z1.astype(F32)).astype(BF16)
        dsum_ref[0] = (_dot(z1, hs) + _dot(z2, hs)).reshape(N_RES, TI, 512)
        bg, _, _, _, conv = _conv_fwd(gt_ref[0], gb_ref[0], s == 0, cw_ref[...])
        g_cv = gc_ref[...]
        _, yh, rc = _rms_fwd(bg * conv, g_cv)
        dy_ref[0] = _rms_bwd(dmc, yh, rc, g_cv)
        pa, pc = _sum_tokens(dma * ah), _sum_tokens(dmc * yh)

        @pl.when(s == 0)
        def _():
            gga_ref[...] = pa
            ggc_ref[...] = pc

        @pl.when(s != 0)
        def _():
            gga_ref[...] += pa
            ggc_ref[...] += pc

    const = lambda r, c, i0=0: pl.BlockSpec((r, c), lambda s: (i0, 0))
    shape4 = _x4(attn).shape
    res, _ = _pcall(
        body, name="mixer_bwd", grid=(nblk,),
        in_specs=[main(d), main(d), main(512), main(1536), before(1536), const(3, 512), const(1, 512), const(1, 512),
                  const(512, d), const(512, d, 1), const(512, 512)],
        out_specs=[main(512)] * 3 + [const(1, 512), const(1, 512), const(d, d)],
        out_shape=[jax.ShapeDtypeStruct(shape4, F32)] * 3 + [jax.ShapeDtypeStruct((1, 512), F32)] * 2
        + [jax.ShapeDtypeStruct((d, d), BF16)],
        scratch_shapes=[pltpu.VMEM((d, d), F32)],
        semantics=("arbitrary",), vmem_mb=48, after=after,
    )(_x4(dx1), _x4(merged), _x4(attn), _x4(gates), _x4(gates), cw, g_a, g_c, w_out, w_out, head_sum)
    return [r.reshape(t, 512) for r in res[:3]] + res[3:]


def _conv_bwd(dy, gates, cw, after=()):
    t = dy.shape[0]
    nblk, main, before, ahead = _order_specs(t)
    n_i = SEG // TI

    def body(dy_ref, dya_ref, gt_ref, gb_ref, ga_ref, cw_ref, dp_ref, gcw_ref):
        s = pl.program_id(0)
        cw_v, gates_v = cw_ref[...], gt_ref[0]
        bg, u, m1, m2, conv = _conv_fwd(gates_v, gb_ref[0], s == 0, cw_v)
        dy_v = dy_ref[0]
        dconv = dy_v * bg
        dca = dya_ref[0] * ga_ref[0][..., 0:512].astype(F32)
        dca = jnp.where(s == nblk - 1, jnp.zeros_like(dca), dca)
        p1, p2 = _taps_ahead(dconv, dca)
        du = dconv * cw_v[2:3, :] + p1 * cw_v[1:2, :] + p2 * cw_v[0:1, :]
        dp_ref[0, 0] = (dy_v * conv).astype(BF16)
        dp_ref[1, 0] = (du * gates_v[..., 1024:1536].astype(F32)).astype(BF16)
        dp_ref[2, 0] = (du * gates_v[..., 512:1024].astype(F32)).astype(BF16)
        parts = [_sum_tokens(dconv * m2), _sum_tokens(dconv * m1), _sum_tokens(dconv * u)]

        @pl.when(s == 0)
        def _():
            gcw_ref[...] = jnp.zeros_like(gcw_ref)

        for tap in range(3):
            gcw_ref[tap:tap + 1, :] += parts[tap]

    (dproj, gcw), _ = _pcall(
        body, name="conv_bwd", grid=(nblk,),
        in_specs=[main(512), ahead(512), main(1536), before(1536), ahead(1536),
                  pl.BlockSpec((3, 512), lambda s: (0, 0))],
        out_specs=[pl.BlockSpec((3, 1, N_RES, TI, 512), lambda s: (1, s // n_i, 0, s % n_i, 0)),
                   pl.BlockSpec((8, 512), lambda s: (0, 0))],
        out_shape=[jax.ShapeDtypeStruct((6, t // HALF, N_RES, SEG, 512), BF16), jax.ShapeDtypeStruct((8, 512), F32)],
        semantics=("arbitrary",), vmem_mb=40, after=after,
    )(_x4(dy), _x4(dy), _x4(gates), _x4(gates), _x4(gates), cw)
    return dproj.reshape(6, t, 512), gcw


def _xattn_fwd(x1, g, w_q, kv, w_o, *, tb):
    t, d = x1.shape
    hd = d // N_MEM_HEADS
    m = kv.shape[0]

    def body(x_ref, g_ref, wq_ref, k_ref, v_ref, wo_ref, x2_ref, h_ref, q_ref, o_ref):
        xv = x_ref[...]
        h = _rms_fwd(xv, g_ref[...])[0].astype(BF16)
        h_ref[...] = h
        q = _dot(h, wq_ref[...]).astype(BF16)
        q_ref[...] = q
        for hh in range(N_MEM_HEADS):
            sl = slice(hh * hd, (hh + 1) * hd)
            s = _dot_nt(q[:, sl], k_ref[:, sl]) * (1.0 / 16.0)
            e = jnp.exp(s - jnp.max(s, axis=1, keepdims=True))
            p = e / jnp.sum(e, axis=1, keepdims=True)
            o_ref[:, sl] = _dot(p.astype(BF16), v_ref[:, sl]).astype(BF16)
        x2_ref[...] = xv + _dot(o_ref[...], wo_ref[...])

    tok = pl.BlockSpec((tb, d), lambda i: (i, 0))
    full = pl.BlockSpec((d, d), lambda i: (0, 0))
    return pl.pallas_call(
        body, name="xattn_fwd", grid=(t // tb,),
        in_specs=[tok, pl.BlockSpec((1, d), lambda i: (0, 0)), full,
                  pl.BlockSpec((m, d), lambda i: (0, 0)), pl.BlockSpec((m, d), lambda i: (0, 1)), full],
        out_specs=[tok] * 4,
        out_shape=[jax.ShapeDtypeStruct((t, d), F32)] + [jax.ShapeDtypeStruct((t, d), BF16)] * 3,
        compiler_params=_params(("parallel",), 48),
    )(x1, g, w_q, kv, kv, w_o)


def _xattn_bwd(dx2, x1, g, q, h, o, w_q, kv, w_o, *, tb, after=()):
    t, d = x1.shape
    hd = d // N_MEM_HEADS
    m = kv.shape[0]
    n_i = t // tb

    def body(dx2_ref, x_ref, g_ref, q_ref, h_ref, o_ref, wq_ref, k_ref, v_ref, wo_ref,
             dx1_ref, dk_ref, dv_ref, gg_ref, gwq_ref, gwo_ref, dq_ref, acc_q, acc_o):
        i = pl.program_id(0)

        @pl.when(i == 0)
        def _():
            dk_ref[...] = jnp.zeros_like(dk_ref)
            dv_ref[...] = jnp.zeros_like(dv_ref)
            acc_q[...] = jnp.zeros_like(acc_q)
            acc_o[...] = jnp.zeros_like(acc_o)

        dx2 = dx2_ref[...]
        dx2b = dx2.astype(BF16)
        acc_o[...] += _dot_tn(o_ref[...], dx2b)
        do = _dot_nt(dx2b, wo_ref[...]).astype(BF16)
        for hh in range(N_MEM_HEADS):
            sl = slice(hh * hd, (hh + 1) * hd)
            qh, kh, vh, doh = q_ref[:, sl], k_ref[:, sl], v_ref[:, sl], do[:, sl]
            s = _dot_nt(qh, kh) * (1.0 / 16.0)
            e = jnp.exp(s - jnp.max(s, axis=1, keepdims=True))
            p = e / jnp.sum(e, axis=1, keepdims=True)
            dp = _dot_nt(doh, vh)
            ds = (p * (dp - jnp.sum(dp * p, axis=1, keepdims=True)) * (1.0 / 16.0)).astype(BF16)
            dq_ref[:, sl] = _dot(ds, kh).astype(BF16)
            dk_ref[:, sl] += _dot_tn(ds, qh)
            dv_ref[:, sl] += _dot_tn(p.astype(BF16), doh)
        dq = dq_ref[...]
        acc_q[...] += _dot_tn(h_ref[...], dq)
        dh = _dot_nt(dq, wq_ref[...])
        g_v = g_ref[...]
        _, xh, r = _rms_fwd(x_ref[...], g_v)
        dx1 = dx2 + _rms_bwd(dh, xh, r, g_v)
        dx1_ref[...] = dx1
        part = jnp.sum(dh * xh, axis=0, keepdims=True)

        @pl.when(i == 0)
        def _():
            gg_ref[...] = part

        @pl.when(i != 0)
        def _():
            gg_ref[...] += part

        @pl.when(i == n_i - 1)
        def _():
            gwq_ref[...] = acc_q[...].astype(BF16)
            gwo_ref[...] = acc_o[...].astype(BF16)

    tok = pl.BlockSpec((tb, d), lambda i: (i, 0))
    full = pl.BlockSpec((d, d), lambda i: (0, 0))
    acc = pl.BlockSpec((m, d), lambda i: (0, 0))
    res, _ = _pcall(
        body, name="xattn_bwd", grid=(n_i,),
        in_specs=[tok, tok, pl.BlockSpec((1, d), lambda i: (0, 0)), tok, tok, tok, full,
                  pl.BlockSpec((m, d), lambda i: (0, 0)), pl.BlockSpec((m, d), lambda i: (0, 1)), full],
        out_specs=[tok, acc, acc, pl.BlockSpec((1, d), lambda i: (0, 0)), full, full],
        out_shape=[jax.ShapeDtypeStruct((t, d), F32),
                   jax.ShapeDtypeStruct((m, d), F32), jax.ShapeDtypeStruct((m, d), F32),
                   jax.ShapeDtypeStruct((1, d), F32),
                   jax.ShapeDtypeStruct((d, d), BF16), jax.ShapeDtypeStruct((d, d), BF16)],
        scratch_shapes=[pltpu.VMEM((tb, d), BF16), pltpu.VMEM((d, d), F32), pltpu.VMEM((d, d), F32)],
        semantics=("arbitrary",), vmem_mb=48, after=after,
    )(dx2, x1, g, q, h, o, w_q, kv, kv, w_o)
    return res


def _mlp_down_loss(a, w_down, x2, tgt, g, *, tb):
    t, d = x2.shape
    f = a.shape[1]

    def body(a_ref, w_ref, x_ref, t_ref, g_ref, dx_ref, dxb_ref, loss_ref, gg_ref):
        i = pl.program_id(0)
        av = a_ref[...].astype(F32)
        x3 = x_ref[...] + _dot((av * av).astype(BF16), w_ref[...])
        g_v = g_ref[...]
        out, xh, r = _rms_fwd(x3, g_v)
        err = out - t_ref[...]
        dout = err * (1.0 / d)
        dx = _rms_bwd(dout, xh, r, g_v)
        dx_ref[...] = dx
        dxb_ref[...] = dx.astype(BF16)
        part = jnp.sum(dout * xh, axis=0, keepdims=True)
        lpart = 0.5 * jnp.sum(jnp.mean(err * err, axis=-1, keepdims=True), axis=0, keepdims=True)
        lpart = jnp.broadcast_to(lpart, loss_ref.shape)

        @pl.when(i == 0)
        def _():
            gg_ref[...] = part
            loss_ref[...] = lpart

        @pl.when(i != 0)
        def _():
            gg_ref[...] += part
            loss_ref[...] += lpart

    tok = pl.BlockSpec((tb, d), lambda i: (i, 0))
    return pl.pallas_call(
        body, name="mlp_down_loss", grid=(t // tb,),
        in_specs=[pl.BlockSpec((tb, f), lambda i: (i, 0)), pl.BlockSpec((f, d), lambda i: (0, 0)), tok, tok,
                  pl.BlockSpec((1, d), lambda i: (0, 0))],
        out_specs=[tok, tok, pl.BlockSpec((8, 128), lambda i: (0, 0)), pl.BlockSpec((1, d), lambda i: (0, 0))],
        out_shape=[jax.ShapeDtypeStruct((t, d), F32), jax.ShapeDtypeStruct((t, d), BF16),
                   jax.ShapeDtypeStruct((8, 128), F32), jax.ShapeDtypeStruct((1, d), F32)],
        compiler_params=_params(("arbitrary",), 56),
    )(a, w_down, x2, tgt, g)


def _mlp_dpre(dx3, w_down, a, *, tb, bn):
    t, d = dx3.shape
    f = a.shape[1]

    def body(dx_ref, w_ref, a_ref, o_ref):
        o_ref[...] = (2.0 * a_ref[...].astype(F32) * _dot_nt(dx_ref[...], w_ref[...])).astype(BF16)

    return pl.pallas_call(
        body, name="mlp_dpre", grid=(t // tb, f // bn),
        in_specs=[pl.BlockSpec((tb, d), lambda i, j: (i, 0)), pl.BlockSpec((bn, d), lambda i, j: (j, 0)),
                  pl.BlockSpec((tb, bn), lambda i, j: (i, j))],
        out_specs=pl.BlockSpec((tb, bn), lambda i, j: (i, j)),
        out_shape=jax.ShapeDtypeStruct((t, f), BF16),
        compiler_params=_params(("parallel", "arbitrary"), 48),
    )(dx3, w_down, a)


def _adamw(gsum, w, m, v):
    m_new = ADAM_B1 * m + (1.0 - ADAM_B1) * gsum
    v_new = ADAM_B2 * v + (1.0 - ADAM_B2) * (gsum * gsum)
    m_hat = m_new / (1.0 - ADAM_B1 ** ADAM_STEP)
    v_hat = v_new / (1.0 - ADAM_B2 ** ADAM_STEP)
    delta = -ADAM_LR * (m_hat / (jnp.sqrt(v_hat) + ADAM_EPS) + ADAM_WD * w)
    return delta, m_new, v_new


def _sum_adamw(parts, w, m, v, *, name, tr):
    r, c = w.shape

    def body(p_ref, w_ref, m_ref, v_ref, g_ref, d_ref, mo_ref, vo_ref):
        g = p_ref[0].astype(F32)
        for k in range(1, N_DEV):
            g = g + p_ref[k].astype(F32)
        g_ref[...] = g
        d_ref[...], mo_ref[...], vo_ref[...] = _adamw(g, w_ref[...], m_ref[...], v_ref[...])

    blk = pl.BlockSpec((tr, c), lambda i: (i, 0))
    return pl.pallas_call(
        body, name=name, grid=(r // tr,),
        in_specs=[pl.BlockSpec((N_DEV, tr, c), lambda i: (0, i, 0)), blk, blk, blk],
        out_specs=[blk] * 4, out_shape=[jax.ShapeDtypeStruct((r, c), F32)] * 4,
        compiler_params=_params(("parallel",), 40),
    )(*[pltpu.with_memory_space_constraint(a, pltpu.HBM) for a in (parts, w, m, v)])


def _sum_small(parts):
    _, r, c = parts.shape

    def body(p_ref, o_ref):
        s = p_ref[0]
        for k in range(1, N_DEV):
            s = s + p_ref[k]
        o_ref[...] = s

    return pl.pallas_call(body, name="sum_small", out_shape=jax.ShapeDtypeStruct((r, c), F32))(parts)


def _adamw_small(g, w, m, v):
    def body(g_ref, w_ref, m_ref, v_ref, d_ref, mo_ref, vo_ref):
        d_ref[...], mo_ref[...], vo_ref[...] = _adamw(g_ref[...], w_ref[...], m_ref[...], v_ref[...])

    return pl.pallas_call(body, name="adamw_small", out_shape=[jax.ShapeDtypeStruct(g.shape, F32)] * 3)(g, w, m, v)


def _head_sum_matrix():
    r = lax.broadcasted_iota(jnp.int32, (512, 512), 0) // HEAD_DIM
    c = lax.broadcasted_iota(jnp.int32, (512, 512), 1) // HEAD_DIM
    return (r == c).astype(BF16)


_SHARD_AXIS = dict(w_in=1, w_out=0, w_q=0, w_kv=1, w_o=0, w_up=1, w_down=0, conv_w=None, small=None)


class _Weights:
    def __init__(self, full, shards=None):
        self.full = dict(full)
        self.shards = shards

    def rider(self, names, late=False):
        if self.shards is None:
            return None
        return _Gather([self.shards[n] for n in names], [_SHARD_AXIS[n] for n in names], late)

    def arrived(self, names, gathered):
        if gathered is not None:
            for n, g in zip(names, gathered):
                self.full[n] = g.transpose(1, 0, 2).reshape(g.shape[1], -1) if n == "conv_w" else g

    def __getitem__(self, name):
        return self.full[name]


class _Grads:
    def __init__(self, distributed):
        self.distributed = distributed
        self.local = {}
        self.pending = {}

    def add(self, name, g):
        self.local[name] = g

    def send(self, *names):
        if not self.distributed:
            return []
        rider = _Exchange([self.local[n] for n in names], [_SHARD_AXIS[n] for n in names])
        started = _exchange_start(rider, "send_" + "_".join(names))
        self.pending[names[0]] = (names, rider, started)
        return [started[3]]

    def wait(self, first_name, after):
        names, rider, started = self.pending.pop(first_name)
        return _exchange_wait(rider, started, after, "wait_" + "_".join(names))


def _ride(fn, *args, rider=None, **kw):
    if rider is None:
        return fn(*args, **kw), None
    return fn(*args, rider=rider, **kw)


def _local_step(x, mem, tgt, gains, weights, grads):
    names = ["w_in", "conv_w"]
    (x, tgt), got = _ride(_reorder, [x, tgt], "reorder_in", rider=weights.rider(names, late=True))
    weights.arrived(names, got)
    w_in, cw = weights["w_in"], weights["conv_w"]

    names = ["w_out", "w_kv"]
    (qkv, gates, h1), got = _ride(_proj, x, gains["g_mix"], w_in, tb=1024, rider=weights.rider(names))
    weights.arrived(names, got)
    names = ["w_q", "w_o", "w_up"]
    (attn, *lses), got = _ride(_attention_fwd, qkv, rider=weights.rider(names))
    weights.arrived(names, got)
    x1, merged = _mixer_fwd(x, attn, gates, cw, gains["g_attn_out"], gains["g_conv_out"], weights["w_out"])
    kv, mem_n = _norm_matmul(mem, gains["g_mem"], weights["w_kv"], name="mem_kv", out_dtype=BF16, tb=mem.shape[0],
                             bn=1024, save_h=True)
    x2, h2, qm, om = _xattn_fwd(x1, gains["g_xattn"], weights["w_q"], kv, weights["w_o"], tb=512)
    w_up = weights["w_up"]
    (a, h3), got = _ride(_norm_matmul, x2, gains["g_mlp"], w_up, name="mlp_up", out_dtype=BF16, tb=1024, bn=2048,
                         relu=True, save_h=True, rider=weights.rider(["w_down"], late=True))
    weights.arrived(["w_down"], got)
    w_down = weights["w_down"]
    dx3, dx3b, loss_blk, gg_final = _mlp_down_loss(a, w_down, x2, tgt, gains["g_final"], tb=512)

    dpre = _mlp_dpre(dx3b, w_down, a, tb=1024, bn=2048)
    grads.add("w_down", _matmul_tn(a, dx3b, name="grad_w_down", bm=512, bn=1024, square_a=True))
    sent = grads.send("w_down")
    grads.add("w_up", _matmul_tn(h3, dpre, name="grad_w_up", bm=1024, bn=1024, after=sent))
    sent = grads.send("w_up")
    dx2, gg_mlp = _matmul_nt_normbwd(dpre, w_up, x2, gains["g_mlp"], dx3, name="mlp_dx", tb=512, after=sent)

    dx1, dk, dv, gg_xattn, gw_q, gw_o = _xattn_bwd(dx2, x1, gains["g_xattn"], qm, h2, om, weights["w_q"], kv,
                                                    weights["w_o"], tb=256)
    grads.add("w_q", gw_q)
    grads.add("w_o", gw_o)
    dkv = jnp.concatenate([dk, dv], axis=1).astype(BF16)
    grads.add("w_kv", _matmul_tn(mem_n, dkv, name="grad_w_kv", bm=1024, bn=1024))
    _, gg_mem = _matmul_nt_normbwd(dkv, weights["w_kv"], mem, gains["g_mem"], None, name="mem_dx", tb=mem.shape[0])

    dattn, dsum, dy, gg_attn, gg_conv, gw_out = _mixer_bwd(dx1, merged, attn, gates, cw, gains["g_attn_out"],
                                                           gains["g_conv_out"], weights["w_out"], _head_sum_matrix())
    grads.add("w_out", gw_out)
    sent = grads.send("w_o", "w_q", "w_kv", "w_out")
    dproj, gcw = _conv_bwd(dy, gates, cw, after=sent)
    dproj = _attention_bwd(qkv, dattn, dsum, lses, dproj)
    grads.add("w_in", _matmul_tn(h1, dproj, name="grad_w_in", bm=1024, bn=512))
    sent = grads.send("w_in")
    grad_x, gg_mix = _matmul_nt_normbwd(dproj, w_in, x, gains["g_mix"], dx1, name="mixer_dx", tb=512,
                                        to_natural=True, after=sent)

    def part(v):
        return jnp.pad(v, ((0, SMALL_PART - v.shape[0]), (0, 1024 - v.shape[1])))

    parts = [gg_mix, gg_xattn, gg_mem, gg_mlp, gg_final, jnp.concatenate([gg_attn, gg_conv], axis=1), gcw, loss_blk]
    grads.add("small", jnp.concatenate([part(v) for v in parts], axis=0))
    return grad_x


SMALL_PART = 8
_BIG = ("w_in", "w_out", "w_q", "w_kv", "w_o", "w_up", "w_down")
_GAIN_ROWS = ("g_mix", "g_xattn", "g_mem", "g_mlp", "g_final")


def _pack_small(vals, conv):
    rows = [vals[k].reshape(1, -1) for k in _GAIN_ROWS]
    rows.append(jnp.concatenate([vals["g_attn_out"].reshape(1, -1), vals["g_conv_out"].reshape(1, -1)], axis=1))
    flat = conv.reshape(1, -1)
    rows.append(jnp.pad(flat, ((0, 0), (0, 1024 - flat.shape[1]))))
    rows.append(jnp.zeros((1, 1024), F32))
    return jnp.concatenate(rows, axis=0)


def kernel(x, mem, g_mix, w_in, conv_w, g_attn_out, g_conv_out, w_out, g_xattn, g_mem, w_q_mem, w_kv_mem, w_o_mem, g_mlp, w_up, w_down, g_final, loss_target, m_g_mix, m_w_in, m_conv_w, m_g_attn_out, m_g_conv_out, m_w_out, m_g_xattn, m_g_mem, m_w_q_mem, m_w_kv_mem, m_w_o_mem, m_g_mlp, m_w_up, m_w_down, m_g_final, v_g_mix, v_w_in, v_conv_w, v_g_attn_out, v_g_conv_out, v_w_out, v_g_xattn, v_g_mem, v_w_q_mem, v_w_kv_mem, v_w_o_mem, v_g_mlp, v_w_up, v_w_down, v_g_final):
    d = x.shape[-1]
    me = 4 * lax.axis_index("x") + 2 * lax.axis_index("y") + lax.axis_index("c")
    w_shards = dict(w_in=w_in, w_out=w_out, w_q=w_q_mem, w_kv=w_kv_mem, w_o=w_o_mem, w_up=w_up, w_down=w_down)
    m_shards = dict(w_in=m_w_in, w_out=m_w_out, w_q=m_w_q_mem, w_kv=m_w_kv_mem, w_o=m_w_o_mem, w_up=m_w_up,
                    w_down=m_w_down)
    v_shards = dict(w_in=v_w_in, w_out=v_w_out, w_q=v_w_q_mem, w_kv=v_w_kv_mem, w_o=v_w_o_mem, w_up=v_w_up,
                    w_down=v_w_down)
    gains = dict(g_mix=g_mix, g_attn_out=g_attn_out, g_conv_out=g_conv_out, g_xattn=g_xattn, g_mem=g_mem,
                 g_mlp=g_mlp, g_final=g_final)
    gains2 = {k: v.reshape(1, -1) for k, v in gains.items()}

    shards = {k: w_shards[k].astype(BF16) for k in _BIG}
    shards["conv_w"] = conv_w
    grads = _Grads(distributed=True)
    grad_x = _local_step(x[0], mem[0], loss_target[0], gains2, _Weights({}, shards), grads)

    after = grads.send("small")
    outs = {}
    tiles = dict(w_in=256, w_out=128, w_q=128, w_kv=256, w_o=128, w_up=256, w_down=256)
    for group in (("w_down",), ("w_up",), ("w_o", "w_q", "w_kv", "w_out"), ("w_in",)):
        for k, received in zip(group, grads.wait(group[0], after)):
            outs[k] = _sum_adamw(received, w_shards[k], m_shards[k], v_shards[k], name=f"adamw_{k}", tr=tiles[k])
            after = [outs[k][0]]
    small_received, = grads.wait("small", after)

    ssum = _sum_small(small_received)
    row = lambda i: ssum[SMALL_PART * i]
    loss = ssum[SMALL_PART * 7, 0]
    g_small = {k: row(i) for i, k in enumerate(_GAIN_ROWS)}
    g_small["g_attn_out"] = row(5)[0:512]
    g_small["g_conv_out"] = row(5)[512:1024]
    taps = ssum[SMALL_PART * 6:SMALL_PART * 6 + 3, 0:512]
    g_conv = lax.dynamic_slice_in_dim(taps, me * 64, 64, axis=1)
    m_small = dict(g_mix=m_g_mix, g_attn_out=m_g_attn_out, g_conv_out=m_g_conv_out, g_xattn=m_g_xattn,
                   g_mem=m_g_mem, g_mlp=m_g_mlp, g_final=m_g_final)
    v_small = dict(g_mix=v_g_mix, g_attn_out=v_g_attn_out, g_conv_out=v_g_conv_out, g_xattn=v_g_xattn,
                   g_mem=v_g_mem, g_mlp=v_g_mlp, g_final=v_g_final)
    packed = [_pack_small(g_small, g_conv), _pack_small(gains, conv_w), _pack_small(m_small, m_conv_w),
              _pack_small(v_small, v_conv_w)]
    upd = _adamw_small(*packed)

    def unpack(p):
        res = {k: p[i] for i, k in enumerate(_GAIN_ROWS)}
        res["g_attn_out"] = p[5, 0:512]
        res["g_conv_out"] = p[5, 512:1024]
        res["conv_w"] = p[6, 0:192].reshape(3, 64)
        return res

    g_small["conv_w"] = g_conv
    small_out = [g_small] + [unpack(p) for p in upd]
    names = {"g_mix": "g_mix", "w_in": "w_in", "conv_w": "conv_w", "g_attn_out": "g_attn_out",
             "g_conv_out": "g_conv_out", "w_out": "w_out", "g_xattn": "g_xattn", "g_mem": "g_mem",
             "w_q_mem": "w_q", "w_kv_mem": "w_kv", "w_o_mem": "w_o", "g_mlp": "g_mlp", "w_up": "w_up",
             "w_down": "w_down", "g_final": "g_final"}
    result = [loss, grad_x[None]]
    for which in range(4):
        for key in names.values():
            result.append(outs[key][which] if key in outs else small_out[which][key])
    return tuple(result)
```

```python
import math

import jax
import jax.numpy as jnp
from jax import lax
from jax.experimental import pallas as pl
from jax.experimental.pallas import tpu as pltpu

F32 = jnp.float32
BF16 = jnp.bfloat16
NORM_EPS = 1e-6
NEG_INF = -1e30
N_DEV = 8
BLK = 128
HEAD_DIM = 64
N_MEM_HEADS = 4
ADAM_LR = 0.001
ADAM_B1 = 0.9
ADAM_B2 = 0.999
ADAM_EPS = 1e-08
ADAM_WD = 0.01
ADAM_STEP = 10
MESH = pl.DeviceIdType.MESH
ANY = pl.BlockSpec(memory_space=pl.ANY)


def _dot(a, b):
    return jnp.dot(a, b, preferred_element_type=F32)


def _dot_nt(a, b):
    return lax.dot_general(a, b, (((1,), (1,)), ((), ())), preferred_element_type=F32)


def _dot_tn(a, b):
    return lax.dot_general(a, b, (((0,), (0,)), ((), ())), preferred_element_type=F32)


def _params(semantics, vmem_mb):
    return pltpu.CompilerParams(dimension_semantics=semantics, vmem_limit_bytes=vmem_mb << 20)


def _rms_fwd(x, g):
    r = lax.rsqrt(jnp.mean(x * x, axis=-1, keepdims=True) + NORM_EPS)
    xh = x * r
    return xh * g, xh, r


def _rms_bwd(dy, xh, r, g):
    gy = dy * g
    return r * (gy - xh * jnp.mean(xh * gy, axis=-1, keepdims=True))


def _position():
    x, y, c = lax.axis_index("x"), lax.axis_index("y"), lax.axis_index("c")
    return x, y, c


def _block_of(ref, j, axis, shard_shape):
    r, c = shard_shape
    if axis is None:
        return ref.at[j]
    if axis == 0:
        return ref.at[pl.ds(j * r, r), :]
    return ref.at[:, pl.ds(j * c, c)]


class _Gather:
    has_mid = True
    alias_pairs = ()

    def __init__(self, shards, axes, late=False):
        self.arrays = list(shards)
        self.axes = list(axes)
        self.late = late
        self.n = len(self.arrays)

    def out_shape(self):
        res = []
        for s, axis in zip(self.arrays, self.axes):
            r, c = s.shape
            shape = (N_DEV, r, c) if axis is None else (N_DEV * r, c) if axis == 0 else (r, N_DEV * c)
            res.append(jax.ShapeDtypeStruct(shape, s.dtype))
        return res

    def scratch(self):
        return [pltpu.SemaphoreType.DMA((self.n, 7)), pltpu.SemaphoreType.DMA((self.n, 7)),
                pltpu.SemaphoreType.DMA((self.n,))]

    def _ctx(self, ins, outs, sems):
        send_sems, recv_sems, local_sems = sems
        x, y, c = _position()
        me, sibling = (x, y, c), (x, y, 1 - c)
        chips = [(1 - x, y), (x, 1 - y), (1 - x, 1 - y)]

        def lin(px, py, pc):
            return 4 * px + 2 * py + pc

        def place(a, block):
            return _block_of(outs[a], lin(*block), self.axes[a], self.arrays[a].shape)

        def copy(a, k, block, to, src=None):
            dst = place(a, block)
            return pltpu.make_async_remote_copy(
                src_ref=dst if src is None else src, dst_ref=dst,
                send_sem=send_sems.at[a, k], recv_sem=recv_sems.at[a, k],
                device_id=to, device_id_type=MESH)

        def mine():
            return [pltpu.make_async_copy(ins[a], place(a, me), local_sems.at[a]) for a in range(self.n)]

        def first():
            res = []
            for a in range(self.n):
                res.append(copy(a, 0, me, sibling, src=ins[a]))
                res += [copy(a, 1 + j, me, (*chip, c), src=ins[a]) for j, chip in enumerate(chips)]
            return res

        return c, me, sibling, chips, copy, mine, first

    def start(self, ins, outs, sems):
        _, _, _, _, _, mine, first = self._ctx(ins, outs, sems)
        for cp in mine() + first():
            cp.start()

    def mid(self, ins, outs, sems):
        c, me, sibling, chips, copy, _, _ = self._ctx(ins, outs, sems)
        for j, chip in enumerate(chips):
            for a in range(self.n):
                copy(a, 1 + j, (*chip, c), me).wait_recv()
                copy(a, 4 + j, (*chip, c), sibling).start()

    def finish(self, ins, outs, sems):
        c, me, sibling, chips, copy, mine, first = self._ctx(ins, outs, sems)
        for a in range(self.n):
            copy(a, 0, sibling, me).wait_recv()
            for j, chip in enumerate(chips):
                copy(a, 4 + j, (*chip, 1 - c), me).wait_recv()
        for cp in first():
            cp.wait_send()
        for j, chip in enumerate(chips):
            for a in range(self.n):
                copy(a, 4 + j, (*chip, c), sibling).wait_send()
        for cp in mine():
            cp.wait()


class _Exchange:
    def __init__(self, parts, axes):
        self.n = len(parts)
        self.axes = list(axes)
        self.arrays = list(parts)

    def _piece(self, a):
        r, c = self.arrays[a].shape
        axis = self.axes[a]
        return (r, c) if axis is None else (r // N_DEV, c) if axis == 0 else (r, c // N_DEV)

    def out_shape(self):
        return [jax.ShapeDtypeStruct((N_DEV,) + self._piece(a), self.arrays[a].dtype) for a in range(self.n)]

    def semaphores(self):
        return [pltpu.SemaphoreType.DMA((7 * self.n,)), pltpu.SemaphoreType.DMA((7 * self.n,)),
                pltpu.SemaphoreType.DMA((self.n,))]

    def _ctx(self, ins, outs, sems):
        send_sems, recv_sems, local_sems = sems
        x, y, c = _position()
        me = 4 * x + 2 * y + c

        def src(a, j):
            return ins[a] if self.axes[a] is None else _block_of(ins[a], j, self.axes[a], self._piece(a))

        def dst(a, j):
            return outs[a].at[j]

        def local():
            return [pltpu.make_async_copy(src(a, me), dst(a, me), local_sems.at[a]) for a in range(self.n)]

        def remote(inbound):
            res = []
            for a in range(self.n):
                for k in range(1, N_DEV):
                    peer = (1 - x if k & 4 else x, 1 - y if k & 2 else y, 1 - c if k & 1 else c)
                    plin = 4 * peer[0] + 2 * peer[1] + peer[2]
                    res.append(pltpu.make_async_remote_copy(
                        src_ref=src(a, plin), dst_ref=dst(a, plin if inbound else me),
                        send_sem=send_sems.at[7 * a + k - 1], recv_sem=recv_sems.at[7 * a + k - 1],
                        device_id=peer, device_id_type=MESH))
            return res

        return local, remote

    def start(self, ins, outs, sems):
        local, remote = self._ctx(ins, outs, sems)
        for cp in local() + remote(False):
            cp.start()

    def finish(self, ins, outs, sems):
        local, remote = self._ctx(ins, outs, sems)
        for cp in remote(True):
            cp.wait_recv()
        for cp in remote(False):
            cp.wait_send()
        for cp in local():
            cp.wait()


def _exchange_start(rider, name):
    n = rider.n
    parts = rider.arrays
    lands = [lax.empty(s.shape, s.dtype) for s in rider.out_shape()]
    hbm = pl.BlockSpec(memory_space=pltpu.HBM)
    sem = pl.BlockSpec(memory_space=pltpu.SEMAPHORE)

    def body(*refs):
        ins, sems = refs[:n], refs[2 * n:2 * n + 3]
        outs, token = refs[2 * n + 3 + n:2 * n + 3 + 2 * n], refs[-1]
        rider.start(ins, outs, sems)
        token[...] = jnp.zeros_like(token)

    res = pl.pallas_call(
        body, name=name,
        out_shape=rider.semaphores() + [pltpu.HBM(p.shape, p.dtype) for p in parts]
                  + [pltpu.HBM(z.shape, z.dtype) for z in lands] + [jax.ShapeDtypeStruct((8, 128), F32)],
        in_specs=[hbm] * (2 * n), out_specs=[sem] * 3 + [hbm] * (2 * n) + [pl.BlockSpec(memory_space=pltpu.VMEM)],
        input_output_aliases={i: 3 + i for i in range(2 * n)},
        compiler_params=pltpu.CompilerParams(has_side_effects=pltpu.SideEffectType.DATAFLOW_SIDE_EFFECTING),
    )(*[pltpu.with_memory_space_constraint(a, pltpu.HBM) for a in parts + lands])
    return res[:3], res[3:3 + n], res[3 + n:3 + 2 * n], res[-1]


def _exchange_wait(rider, started, after, name):
    n = rider.n
    sems, parts, lands, _ = started
    hbm = pl.BlockSpec(memory_space=pltpu.HBM)
    sem = pl.BlockSpec(memory_space=pltpu.SEMAPHORE)

    def body(*refs):
        rider.finish(refs[:n], refs[n:2 * n], refs[2 * n:2 * n + 3])

    res = pl.pallas_call(
        body, name=name, out_shape=[pltpu.HBM(a.shape, a.dtype) for a in list(parts) + list(lands)],
        in_specs=[hbm] * (2 * n) + [sem] * 3 + [ANY] * len(after), out_specs=[hbm] * (2 * n),
        input_output_aliases={i: i for i in range(2 * n)},
        compiler_params=pltpu.CompilerParams(has_side_effects=pltpu.SideEffectType.DATAFLOW_SIDE_EFFECTING),
    )(*parts, *lands, *sems, *after)
    return list(res[n:])


def _pcall(body, *, name, grid, in_specs, out_specs, out_shape, scratch_shapes=(), semantics, vmem_mb, rider=None,
           aliases=None, after=()):
    in_specs, out_specs, out_shape = list(in_specs), list(out_specs), list(out_shape)
    scratch_shapes = list(scratch_shapes)
    aliases = dict(aliases or {})
    if rider is None:
        n_in, after = len(in_specs), list(after)

        def plain(*refs):
            body(*refs[:n_in], *refs[n_in + len(after):])

        call = pl.pallas_call(plain if after else body, name=name, grid=grid, in_specs=in_specs + [ANY] * len(after),
                              out_specs=out_specs, out_shape=out_shape, scratch_shapes=scratch_shapes,
                              input_output_aliases=aliases, compiler_params=_params(semantics, vmem_mb))
        return lambda *args: (list(call(*args, *after)), None)
    n_in, n_out, n_scr = len(in_specs), len(out_specs), len(scratch_shapes)
    r_in, r_shapes = len(rider.arrays), rider.out_shape()
    r_out = len(r_shapes)
    aliases.update({n_in + i: n_out + o for i, o in rider.alias_pairs})
    total = math.prod(grid)
    mid_step = total - 1 if rider.has_mid and rider.late else (3 * total) // 4

    def wrapped(*refs):
        bounds = [0, n_in, r_in, n_out, r_out, n_scr]
        for i in range(1, len(bounds)):
            bounds[i] += bounds[i - 1]
        a, ra, o, ro, s = (refs[bounds[i]:bounds[i + 1]] for i in range(5))
        rs = refs[bounds[5]:]
        step = pl.program_id(0)
        for k in range(1, len(grid)):
            step = step * grid[k] + pl.program_id(k)
        pl.when(step == 0)(lambda: rider.start(ra, ro, rs))
        body(*a, *o, *s)
        if rider.has_mid:
            pl.when(step == mid_step)(lambda: rider.mid(ra, ro, rs))
        pl.when(step == total - 1)(lambda: rider.finish(ra, ro, rs))

    call = pl.pallas_call(
        wrapped, name=name, grid=grid, in_specs=in_specs + [ANY] * r_in, out_specs=out_specs + [ANY] * r_out,
        out_shape=out_shape + r_shapes, scratch_shapes=scratch_shapes + rider.scratch(),
        input_output_aliases=aliases, compiler_params=_params(("arbitrary",) * len(grid), vmem_mb))

    def run(*args):
        res = call(*args, *rider.arrays)
        return list(res[:n_out]), list(res[n_out:])

    return run


def _norm_matmul(x, g, w, *, name, out_dtype, tb, bn, relu=False, save_h=False, rider=None):
    t, d = x.shape
    n = w.shape[1]

    def body(x_ref, g_ref, w_ref, o_ref, *rest):
        h_scr = rest[-1]

        @pl.when(pl.program_id(1) == 0)
        def _():
            h = _rms_fwd(x_ref[...], g_ref[...])[0].astype(BF16)
            h_scr[...] = h
            if save_h:
                rest[0][...] = h

        acc = _dot(h_scr[...], w_ref[...])
        if relu:
            acc = jnp.maximum(acc, 0.0)
        o_ref[...] = acc.astype(out_dtype)

    out_shape = [jax.ShapeDtypeStruct((t, n), out_dtype)]
    out_specs = [pl.BlockSpec((tb, bn), lambda i, j: (i, j))]
    if save_h:
        out_shape.append(jax.ShapeDtypeStruct((t, d), BF16))
        out_specs.append(pl.BlockSpec((tb, d), lambda i, j: (i, 0)))
    res, extra = _pcall(
        body, name=name, grid=(t // tb, n // bn),
        in_specs=[pl.BlockSpec((tb, d), lambda i, j: (i, 0)),
                  pl.BlockSpec((1, d), lambda i, j: (0, 0)),
                  pl.BlockSpec((d, bn), lambda i, j: (0, j))],
        out_specs=out_specs, out_shape=out_shape,
        scratch_shapes=[pltpu.VMEM((tb, d), BF16)],
        semantics=("parallel", "arbitrary"), vmem_mb=48, rider=rider,
    )(x, g, w)
    res = res if save_h else res[0]
    return res if rider is None else (res, extra)


def _proj(x, g, w, *, tb, rider=None):
    t, d = x.shape
    half = w.shape[1] // 2

    def body(x_ref, g_ref, w_ref, qkv_ref, gates_ref, h_ref, h_scr):
        j = pl.program_id(1)

        @pl.when(j == 0)
        def _():
            h = _rms_fwd(x_ref[...], g_ref[...])[0].astype(BF16)
            h_scr[...] = h
            h_ref[...] = h

        acc = _dot(h_scr[...], w_ref[...])

        @pl.when(j == 0)
        def _():
            qkv_ref[...] = acc

        @pl.when(j == 1)
        def _():
            gates_ref[...] = acc.astype(BF16)

    tok = lambda c: pl.BlockSpec((tb, c), lambda i, j: (i, 0))
    res, extra = _pcall(
        body, name="proj", grid=(t // tb, 2),
        in_specs=[tok(d), pl.BlockSpec((1, d), lambda i, j: (0, 0)), pl.BlockSpec((d, half), lambda i, j: (0, j))],
        out_specs=[tok(half), tok(half), tok(d)],
        out_shape=[jax.ShapeDtypeStruct((t, half), F32), jax.ShapeDtypeStruct((t, half), BF16),
                   jax.ShapeDtypeStruct((t, d), BF16)],
        scratch_shapes=[pltpu.VMEM((tb, d), BF16)],
        semantics=("parallel", "arbitrary"), vmem_mb=48, rider=rider,
    )(x, g, w)
    return res if rider is None else (res, extra)


def _matmul_nt_normbwd(dy, w, x, g, dres, *, name, tb, also_bf16=False, to_natural=False, after=()):
    t, d = x.shape
    stacked = dy.ndim == 3
    has_res = dres is not None
    n_i = SEG // TI
    if to_natural:
        tb = N_RES * TI

    def body(dy_ref, w_ref, x_ref, g_ref, *rest):
        rest = list(rest)
        dres_ref = rest.pop(0) if has_res else None
        dx_ref = rest.pop(0)
        dxb_ref = rest.pop(0) if also_bf16 else None
        gg_ref = rest.pop(0)
        i = pl.program_id(0)

        def rows(ref, *lead):
            v = ref[lead] if lead else ref[...]
            return v[0].reshape(tb, v.shape[-1]) if to_natural else v

        if stacked:
            kb = dy_ref.shape[-1]
            dh = _dot_nt(rows(dy_ref, 0), w_ref[:, 0:kb])
            for s in range(1, dy_ref.shape[0]):
                dh = dh + _dot_nt(rows(dy_ref, s), w_ref[:, s * kb:(s + 1) * kb])
        else:
            dh = _dot_nt(rows(dy_ref), w_ref[...])
        g_v = g_ref[...]
        _, xh, r = _rms_fwd(rows(x_ref), g_v)
        dx = _rms_bwd(dh, xh, r, g_v)
        if has_res:
            dx = dx + rows(dres_ref)
        if to_natural:
            scr = rest.pop(0)
            for cb in range(d // BLK):
                cols = slice(cb * BLK, (cb + 1) * BLK)
                slab = scr.at[cb]
                for res in range(N_RES):
                    slab[pl.ds(res, TI, stride=N_RES), :] = dx[res * TI:(res + 1) * TI, cols]
                dx_ref[:, cols] = slab[...]
        else:
            dx_ref[...] = dx
        if also_bf16:
            dxb_ref[...] = dx.astype(BF16)
        part = jnp.sum(dh * xh, axis=0, keepdims=True)

        @pl.when(i == 0)
        def _():
            gg_ref[...] = part

        @pl.when(i != 0)
        def _():
            gg_ref[...] += part

    tok = pl.BlockSpec((tb, d), lambda i: (i, 0))
    row = pl.BlockSpec((1, d), lambda i: (0, 0))
    if to_natural:
        act = pl.BlockSpec((1, N_RES, TI, d), lambda i: (i // n_i, 0, i % n_i, 0))
        dy_spec = pl.BlockSpec((dy.shape[0], 1, N_RES, TI, dy.shape[2]), lambda i: (0, i // n_i, 0, i % n_i, 0))
        dy, x = dy.reshape(dy.shape[0], t // HALF, N_RES, SEG, dy.shape[2]), _x4(x)
        dres = _x4(dres) if has_res else None
    elif stacked:
        act, dy_spec = tok, pl.BlockSpec((dy.shape[0], tb, dy.shape[2]), lambda i: (0, i, 0))
    else:
        act, dy_spec = tok, pl.BlockSpec((tb, dy.shape[1]), lambda i: (i, 0))
    in_specs = [dy_spec, pl.BlockSpec(w.shape, lambda i: (0, 0)), act, row]
    args = [dy, w, x, g]
    if has_res:
        in_specs.append(act)
        args.append(dres)
    out_specs = [tok] + ([tok] if also_bf16 else []) + [row]
    out_shape = ([jax.ShapeDtypeStruct((t, d), F32)] + ([jax.ShapeDtypeStruct((t, d), BF16)] if also_bf16 else [])
                 + [jax.ShapeDtypeStruct((1, d), F32)])
    res, _ = _pcall(
        body, name=name, grid=(t // tb,), in_specs=in_specs, out_specs=out_specs, out_shape=out_shape,
        scratch_shapes=[pltpu.VMEM((d // BLK, tb, BLK), F32)] if to_natural else [],
        semantics=("arbitrary",), vmem_mb=56, after=after,
    )(*args)
    return res


def _matmul_tn(a, b, *, name, bm, bn, square_a=False, after=()):
    t, m = a.shape
    stacked = b.ndim == 3
    n = b.shape[0] * bn if stacked else b.shape[1]

    def body(a_ref, b_ref, o_ref):
        av = a_ref[...]
        if square_a:
            av = av.astype(F32)
            av = (av * av).astype(BF16)
        o_ref[...] = _dot_tn(av, b_ref[...]).astype(BF16)

    res, _ = _pcall(
        body, name=name, grid=(m // bm, n // bn),
        in_specs=[pl.BlockSpec((t, bm), lambda i, j: (0, i)),
                  pl.BlockSpec((None, t, bn), lambda i, j: (j, 0, 0)) if stacked
                  else pl.BlockSpec((t, bn), lambda i, j: (0, j))],
        out_specs=[pl.BlockSpec((bm, bn), lambda i, j: (i, j))], out_shape=[jax.ShapeDtypeStruct((m, n), BF16)],
        semantics=("parallel", "parallel"), vmem_mb=56, after=after,
    )(a, b)
    return res[0]


N_RES = 16
SEG = 128
HALF = N_RES * SEG
TI = 32
HALO = 16


def _x4(a):
    return a.reshape(a.shape[0] // HALF, N_RES, SEG, a.shape[1])


def _reorder(arrays, name, rider=None):
    t, c = arrays[0].shape
    n = len(arrays)
    n_i = SEG // TI

    def body(*refs):
        scr = refs[-1]
        for i_ref, o_ref in zip(refs[:n], refs[n:2 * n]):
            for cb in range(c // BLK):
                cols = slice(cb * BLK, (cb + 1) * BLK)
                slab = scr.at[cb]
                slab[...] = i_ref[:, cols]
                for r in range(N_RES):
                    o_ref[0, r, :, cols] = slab[pl.ds(r, TI, stride=N_RES), :]

    res, extra = _pcall(
        body, name=name, grid=(t // (TI * N_RES),),
        in_specs=[pl.BlockSpec((TI * N_RES, c), lambda s: (s, 0))] * n,
        out_specs=[pl.BlockSpec((1, N_RES, TI, c), lambda s: (s // n_i, 0, s % n_i, 0))] * n,
        out_shape=[jax.ShapeDtypeStruct((t // HALF, N_RES, SEG, c), F32)] * n,
        scratch_shapes=[pltpu.VMEM((c // BLK, TI * N_RES, BLK), F32)],
        semantics=("parallel",), vmem_mb=32, rider=rider,
    )(*arrays)
    res = [r.reshape(t, c) for r in res]
    return res if rider is None else (res, extra)


_PATTERNS = ((1, 16, 8, SEG), (4, 4, 32, 4 * SEG), (16, 1, SEG, 0))
_FIRST = {1: 1, 4: 4, 16: 16}


def _group_rows(d, g):
    a = g >> 4
    if d == 16:
        base = a * HALF + (g & 15) * SEG
        prev = base - HALF
    elif d == 4:
        c = (g >> 2) & 3
        base = a * HALF + (g & 3) * SEG + c * 32
        prev = jnp.where(c > 0, base - 32, base - HALF + 96)
    else:
        c = g & 15
        base = a * HALF + c * 8
        prev = jnp.where(c > 0, base - 8, base - HALF + 120)
    return base, prev


def _load_rows(ref, base, n, rows, stride):
    parts = [ref[pl.ds(pl.multiple_of(base + j * stride, 8), rows), :] for j in range(n)]
    return parts[0] if n == 1 else jnp.concatenate(parts, axis=0)


def _store_rows(ref, base, val, n, rows, stride, add=False):
    for j in range(n):
        sl = pl.ds(pl.multiple_of(base + j * stride, 8), rows)
        piece = val[j * rows:(j + 1) * rows, :]
        if add:
            ref[sl, :] += piece
        else:
            ref[sl, :] = piece


def _band_bias(n, rows):
    shift = rows.bit_length() - 1
    lq = lax.broadcasted_iota(jnp.int32, (BLK, BLK), 0)
    lk = lax.broadcasted_iota(jnp.int32, (BLK, BLK), 1)
    iq = (lq & (rows - 1)) * n + (lq >> shift)
    ik = (lk & (rows - 1)) * n + (lk >> shift)
    zero = jnp.zeros((BLK, BLK), F32)
    return jnp.where(ik >= iq, zero, NEG_INF), jnp.where(ik <= iq, zero, NEG_INF)


def _set_bias(bias_scr, n, rows):
    prev_b, cur_b = _band_bias(n, rows)
    for half in range(2):
        bias_scr[half * BLK:(half + 1) * BLK, 0:BLK] = prev_b
        bias_scr[half * BLK:(half + 1) * BLK, BLK:2 * BLK] = cur_b


SCALE = 1.0 / math.sqrt(HEAD_DIM)


def _head_consts(value=1.0):
    lane_lo = lax.broadcasted_iota(jnp.int32, (BLK, BLK), 1) < HEAD_DIM
    return lane_lo, [jnp.where(lane_lo, value, 0.0).astype(BF16), jnp.where(lane_lo, 0.0, value).astype(BF16)]


def _stack_heads(v, head_mask):
    return jnp.concatenate([v * head_mask[0], v * head_mask[1]], axis=0)


def _unstack_heads(v2, lane_lo):
    return jnp.where(lane_lo, v2[:BLK], v2[BLK:])


def _rows_per_head(v, lane_lo):
    rolled = pltpu.roll(v, HEAD_DIM, axis=1)
    return jnp.concatenate([jnp.where(lane_lo, v, rolled), jnp.where(lane_lo, rolled, v)], axis=0)


WIDTH = 4


def _loop(lo, hi, fn, width=None):
    if width is None:
        def body(g, carry):
            fn(g)
            return carry

        if hi > lo:
            lax.fori_loop(lo, hi, body, 0)
        return
    while hi > lo:
        trips = (hi - lo) // width
        if trips:
            def body(i, carry, lo=lo, width=width):
                fn([lo + width * i + j for j in range(width)])
                return carry

            lax.fori_loop(0, trips, body, 0)
            lo += trips * width
        width = max(1, width // 2)


def _mix_weights(l1, l2, l3):
    mx = jnp.maximum(jnp.maximum(l1, l2), l3)
    e1, e2, e3 = jnp.exp(l1 - mx), jnp.exp(l2 - mx), jnp.exp(l3 - mx)
    inv = 1.0 / (e1 + e2 + e3)
    return e1 * inv, e2 * inv, e3 * inv


def _attention_fwd(qkv, rider=None):
    t = qkv.shape[0]
    groups = 16 * (t // HALF)

    def body(q_ref, k_ref, v_ref, attn_ref, l1_ref, l2_ref, l3_ref, o_scr, bias_scr):
        lane_lo, q_mask = _head_consts(SCALE)
        l_refs = (l1_ref, l2_ref, l3_ref)
        for p, (d, n, rows, stride) in enumerate(_PATTERNS):
            _set_bias(bias_scr, n, rows)
            o_p, l_p = o_scr.at[p], l_refs[p]

            def block(gs, has_prev):
                at = [_group_rows(d, g) for g in gs]

                def load(ref, b):
                    return _load_rows(ref, b, n, rows, stride).astype(BF16)

                q2 = [_stack_heads(load(q_ref, b), q_mask) for b, _ in at]
                k2 = [load(k_ref, b) for b, _ in at]
                v2 = [load(v_ref, b) for b, _ in at]
                if has_prev:
                    k2 = [jnp.concatenate([load(k_ref, pv), k], axis=0) for (_, pv), k in zip(at, k2)]
                    v2 = [jnp.concatenate([load(v_ref, pv), v], axis=0) for (_, pv), v in zip(at, v2)]
                s = [_dot_nt(q, k) for q, k in zip(q2, k2)]
                s = [x + (bias_scr[...] if has_prev else bias_scr[:, BLK:2 * BLK]) for x in s]
                mx = [jnp.max(x, axis=1, keepdims=True) for x in s]
                e = [jnp.exp(x - m) for x, m in zip(s, mx)]
                den = [jnp.sum(x, axis=1, keepdims=True) for x in e]
                o2 = [_dot(x.astype(BF16), v) * (1.0 / dn) for x, v, dn in zip(e, v2, den)]
                lse2 = [jnp.broadcast_to(m + jnp.log(dn), (2 * BLK, BLK)) for m, dn in zip(mx, den)]
                for (b, _), o, l in zip(at, o2, lse2):
                    _store_rows(o_p, b, _unstack_heads(o, lane_lo), n, rows, stride)
                    _store_rows(l_p, b, _unstack_heads(l, lane_lo), n, rows, stride)

            _loop(0, _FIRST[d], lambda gs: block(gs, False), width=2 * WIDTH)
            _loop(_FIRST[d], groups, lambda gs: block(gs, True), width=2 * WIDTH)

        def mix(i):
            sl = pl.ds(pl.multiple_of(i * 256, 256), 256)
            w = _mix_weights(l1_ref[sl, :], l2_ref[sl, :], l3_ref[sl, :])
            attn_ref[sl, :] = w[0] * o_scr[0, sl, :] + w[1] * o_scr[1, sl, :] + w[2] * o_scr[2, sl, :]

        _loop(0, t // 256, mix)

    def col(c0):
        return pl.BlockSpec((t, BLK), lambda hp: (0, c0 + hp))

    res, extra = _pcall(
        body, name="attention_fwd", grid=(4,), in_specs=[col(0), col(4), col(8)], out_specs=[col(0)] * 4,
        out_shape=[jax.ShapeDtypeStruct((t, 512), F32)] * 4,
        scratch_shapes=[pltpu.VMEM((3, t, BLK), F32), pltpu.VMEM((2 * BLK, 2 * BLK), F32)],
        semantics=("parallel",), vmem_mb=48, rider=rider,
    )(qkv, qkv, qkv)
    return res if rider is None else (res, extra)


def _attention_bwd(qkv, dattn, dsum, lses, dproj):
    t = qkv.shape[0]
    groups = 16 * (t // HALF)

    def body(q_ref, k_ref, v_ref, da_ref, ds_ref, l1_ref, l2_ref, l3_ref, kept_ref, out_ref, acc, bias_scr):
        del kept_ref
        lane_lo, head_mask = _head_consts()
        q_mask = _head_consts(SCALE)[1]
        l_refs = (l1_ref, l2_ref, l3_ref)

        def clear(i):
            sl = pl.ds(pl.multiple_of(i * 512, 512), 512)
            for s in range(3):
                acc[s, sl, :] = jnp.zeros((512, BLK), F32)

        _loop(0, t // 512, clear)
        dq_acc, dk_acc, dv_acc = acc.at[0], acc.at[1], acc.at[2]
        for p, (d, n, rows, stride) in enumerate(_PATTERNS):
            _set_bias(bias_scr, n, rows)

            def block(gs, has_prev):
                at = [_group_rows(d, g) for g in gs]

                def load(ref, b):
                    return _load_rows(ref, b, n, rows, stride)

                def put(ref, b, val):
                    _store_rows(ref, b, val, n, rows, stride, add=True)

                def wide(x):
                    return jnp.concatenate([x, x], axis=1) if has_prev else x

                lse = [[load(ref, b) for ref in l_refs] for b, _ in at]
                w = [_mix_weights(*ls)[p] for ls in lse]
                do2 = [_stack_heads((wg * load(da_ref, b)).astype(BF16), head_mask) for wg, (b, _) in zip(w, at)]
                dl2 = [wide(_rows_per_head(wg * load(ds_ref, b), lane_lo)) for wg, (b, _) in zip(w, at)]
                lse2 = [wide(_rows_per_head(ls[p], lane_lo)) for ls in lse]
                q2 = [_stack_heads(load(q_ref, b).astype(BF16), q_mask) for b, _ in at]
                k2 = [load(k_ref, b).astype(BF16) for b, _ in at]
                v2 = [load(v_ref, b).astype(BF16) for b, _ in at]
                if has_prev:
                    k2 = [jnp.concatenate([load(k_ref, pv).astype(BF16), k], axis=0) for (_, pv), k in zip(at, k2)]
                    v2 = [jnp.concatenate([load(v_ref, pv).astype(BF16), v], axis=0) for (_, pv), v in zip(at, v2)]
                s = [_dot_nt(q, k) for q, k in zip(q2, k2)]
                dp = [_dot_nt(do, v) for do, v in zip(do2, v2)]
                pr = [jnp.exp(x + (bias_scr[...] if has_prev else bias_scr[:, BLK:2 * BLK]) - l)
                      for x, l in zip(s, lse2)]
                ds = [(pg * (x - dl)).astype(BF16) for pg, x, dl in zip(pr, dp, dl2)]
                dq2 = [_dot(x, k) * SCALE for x, k in zip(ds, k2)]
                dk2 = [_dot_tn(x, q) for x, q in zip(ds, q2)]
                dv2 = [_dot_tn(pg.astype(BF16), do) for pg, do in zip(pr, do2)]
                for (b, pv), dq, dk, dv in zip(at, dq2, dk2, dv2):
                    put(dq_acc, b, _unstack_heads(dq, lane_lo))
                    if has_prev:
                        put(dk_acc, pv, dk[:BLK])
                        put(dv_acc, pv, dv[:BLK])
                        put(dk_acc, b, dk[BLK:])
                        put(dv_acc, b, dv[BLK:])
                    else:
                        put(dk_acc, b, dk)
                        put(dv_acc, b, dv)

            _loop(0, _FIRST[d], lambda gs: block(gs, False), width=WIDTH)
            _loop(_FIRST[d], groups, lambda gs: block(gs, True), width=WIDTH)

        def emit(i):
            sl = pl.ds(pl.multiple_of(i * 512, 512), 512)
            for s in range(3):
                out_ref[s, sl, :] = acc[s, sl, :].astype(BF16)

        _loop(0, t // 512, emit)

    def col(c0):
        return pl.BlockSpec((t, BLK), lambda hp: (0, c0 + hp))

    res, _ = _pcall(
        body, name="attention_bwd", grid=(4,),
        in_specs=[col(0), col(4), col(8)] + [col(0)] * 5 + [ANY],
        out_specs=[pl.BlockSpec((3, t, BLK), lambda hp: (0, 0, hp))],
        out_shape=[jax.ShapeDtypeStruct(dproj.shape, BF16)],
        scratch_shapes=[pltpu.VMEM((3, t, BLK), F32), pltpu.VMEM((2 * BLK, 2 * BLK), F32)],
        semantics=("parallel",), vmem_mb=56, aliases={8: 0},
    )(qkv, qkv, qkv, dattn, dsum, *lses, dproj)
    return res[0]


def _order_specs(t):
    n_i = SEG // TI
    nblk = (t // HALF) * n_i
    per = TI // HALO

    def main(c, col=0):
        return pl.BlockSpec((1, N_RES, TI, c), lambda s: (s // n_i, 0, s % n_i, col))

    def before(c, col=0):
        return pl.BlockSpec((1, 2, HALO, c), lambda s: (jnp.maximum(s - 1, 0) // n_i, N_RES // 2 - 1,
                                                        (jnp.maximum(s - 1, 0) % n_i) * per + per - 1, col))

    def after(c, col=0):
        return pl.BlockSpec((1, 2, HALO, c), lambda s: (jnp.minimum(s + 1, nblk - 1) // n_i, 0,
                                                        (jnp.minimum(s + 1, nblk - 1) % n_i) * per, col))

    return nblk, main, before, after


def _shift_in(v, row_in, up):
    rows = v.shape[0]
    idx = lax.broadcasted_iota(jnp.int32, v.shape, 0)
    fill = jnp.broadcast_to(row_in, v.shape)
    if up:
        return jnp.where(idx == rows - 1, fill, pltpu.roll(v, rows - 1, axis=0))
    return jnp.where(idx == 0, fill, pltpu.roll(v, 1, axis=0))


def _taps_behind(u, before):
    s15 = _shift_in(u[N_RES - 1], before[1, HALO - 1:HALO, :], up=False)
    s14 = _shift_in(u[N_RES - 2], before[0, HALO - 1:HALO, :], up=False)
    m1 = jnp.concatenate([s15[None], u[:N_RES - 1]], axis=0)
    m2 = jnp.concatenate([s14[None], s15[None], u[:N_RES - 2]], axis=0)
    return m1, m2


def _taps_ahead(u, after):
    t0 = _shift_in(u[0], after[0, 0:1, :], up=True)
    t1 = _shift_in(u[1], after[1, 0:1, :], up=True)
    p1 = jnp.concatenate([u[1:], t0[None]], axis=0)
    p2 = jnp.concatenate([u[2:], t0[None], t1[None]], axis=0)
    return p1, p2


def _conv_fwd(gates, before, first, cw):
    gates, before = gates.astype(F32), before.astype(F32)
    bg, cg, xc = gates[..., 0:512], gates[..., 512:1024], gates[..., 1024:1536]
    u = cg * xc
    ub = before[..., 512:1024] * before[..., 1024:1536]
    ub = jnp.where(first, jnp.zeros_like(ub), ub)
    m1, m2 = _taps_behind(u, ub)
    conv = m2 * cw[0:1, :] + m1 * cw[1:2, :] + u * cw[2:3, :]
    return bg, u, m1, m2, conv


def _sum_tokens(v):
    return jnp.sum(jnp.sum(v, axis=0), axis=0, keepdims=True)


def _mixer_fwd(x, attn, gates, cw, g_a, g_c, w_out):
    t, d = x.shape
    nblk, main, before, _ = _order_specs(t)
    rows = N_RES * TI

    def body(x_ref, at_ref, gt_ref, gb_ref, cw_ref, ga_ref, gc_ref, wa_ref, wb_ref, x1_ref, mg_ref):
        an = _rms_fwd(at_ref[0], ga_ref[...])[0].astype(BF16)
        bg, _, _, _, conv = _conv_fwd(gt_ref[0], gb_ref[0], pl.program_id(0) == 0, cw_ref[...])
        cn = _rms_fwd(bg * conv, gc_ref[...])[0].astype(BF16)
        mg_ref[0, :, :, 0:512] = an
        mg_ref[0, :, :, 512:1024] = cn
        y = _dot(an.reshape(rows, 512), wa_ref[...]) + _dot(cn.reshape(rows, 512), wb_ref[...])
        x1_ref[0] = x_ref[0] + y.reshape(N_RES, TI, d)

    const = lambda r, c, i0=0: pl.BlockSpec((r, c), lambda s: (i0, 0))
    x1, merged = pl.pallas_call(
        body, name="mixer_fwd", grid=(nblk,),
        in_specs=[main(d), main(512), main(1536), before(1536), const(3, 512), const(1, 512), const(1, 512),
                  const(512, d), const(512, d, 1)],
        out_specs=[main(d), main(d)],
        out_shape=[jax.ShapeDtypeStruct(_x4(x).shape, F32), jax.ShapeDtypeStruct(_x4(x).shape, BF16)],
        compiler_params=_params(("parallel",), 48),
    )(_x4(x), _x4(attn), _x4(gates), _x4(gates), cw, g_a, g_c, w_out, w_out)
    return x1.reshape(t, d), merged.reshape(t, d)


def _mixer_bwd(dx1, merged, attn, gates, cw, g_a, g_c, w_out, head_sum, after=()):
    t, d = dx1.shape
    nblk, main, before, _ = _order_specs(t)
    rows = N_RES * TI

    def body(dx_ref, mg_ref, at_ref, gt_ref, gb_ref, cw_ref, ga_ref, gc_ref, wa_ref, wb_ref, hs_ref,
             da_ref, dsum_ref, dy_ref, gga_ref, ggc_ref, gw_ref, acc_w):
        s = pl.program_id(0)
        dxb = dx_ref[0].reshape(rows, d).astype(BF16)

        @pl.when(s == 0)
        def _():
            acc_w[...] = jnp.zeros_like(acc_w)

        acc_w[...] += _dot_tn(mg_ref[0].reshape(rows, d), dxb)

        @pl.when(s == nblk - 1)
        def _():
            gw_ref[...] = acc_w[...].astype(BF16)

        dma = _dot_nt(dxb, wa_ref[...]).reshape(N_RES, TI, 512)
        dmc = _dot_nt(dxb, wb_ref[...]).reshape(N_RES, TI, 512)
        attn_v, g_av = at_ref[0], ga_ref[...]
        _, ah, ra = _rms_fwd(attn_v, g_av)
        dattn = _rms_bwd(dma, ah, ra, g_av)
        da_ref[0] = dattn
        z = (dattn * attn_v).reshape(rows, 512)
        hs = hs_ref[...]
        z1 = z.astype(BF16)
        z2 = (z - z1.astype(F32)).astype(BF16)
        dsum_ref[0] = (_dot(z1, hs) + _dot(z2, hs)).reshape(N_RES, TI, 512)
        bg, _, _, _, conv = _conv_fwd(gt_ref[0], gb_ref[0], s == 0, cw_ref[...])
        g_cv = gc_ref[...]
        _, yh, rc = _rms_fwd(bg * conv, g_cv)
        dy_ref[0] = _rms_bwd(dmc, yh, rc, g_cv)
        pa, pc = _sum_tokens(dma * ah), _sum_tokens(dmc * yh)

        @pl.when(s == 0)
        def _():
            gga_ref[...] = pa
            ggc_ref[...] = pc

        @pl.when(s != 0)
        def _():
            gga_ref[...] += pa
            ggc_ref[...] += pc

    const = lambda r, c, i0=0: pl.BlockSpec((r, c), lambda s: (i0, 0))
    shape4 = _x4(attn).shape
    res, _ = _pcall(
        body, name="mixer_bwd", grid=(nblk,),
        in_specs=[main(d), main(d), main(512), main(1536), before(1536), const(3, 512), const(1, 512), const(1, 512),
                  const(512, d), const(512, d, 1), const(512, 512)],
        out_specs=[main(512)] * 3 + [const(1, 512), const(1, 512), const(d, d)],
        out_shape=[jax.ShapeDtypeStruct(shape4, F32)] * 3 + [jax.ShapeDtypeStruct((1, 512), F32)] * 2
        + [jax.ShapeDtypeStruct((d, d), BF16)],
        scratch_shapes=[pltpu.VMEM((d, d), F32)],
        semantics=("arbitrary",), vmem_mb=48, after=after,
    )(_x4(dx1), _x4(merged), _x4(attn), _x4(gates), _x4(gates), cw, g_a, g_c, w_out, w_out, head_sum)
    return [r.reshape(t, 512) for r in res[:3]] + res[3:]


def _conv_bwd(dy, gates, cw, after=()):
    t = dy.shape[0]
    nblk, main, before, ahead = _order_specs(t)
    n_i = SEG // TI

    def body(dy_ref, dya_ref, gt_ref, gb_ref, ga_ref, cw_ref, dp_ref, gcw_ref):
        s = pl.program_id(0)
        cw_v, gates_v = cw_ref[...], gt_ref[0]
        bg, u, m1, m2, conv = _conv_fwd(gates_v, gb_ref[0], s == 0, cw_v)
        dy_v = dy_ref[0]
        dconv = dy_v * bg
        dca = dya_ref[0] * ga_ref[0][..., 0:512].astype(F32)
        dca = jnp.where(s == nblk - 1, jnp.zeros_like(dca), dca)
        p1, p2 = _taps_ahead(dconv, dca)
        du = dconv * cw_v[2:3, :] + p1 * cw_v[1:2, :] + p2 * cw_v[0:1, :]
        dp_ref[0, 0] = (dy_v * conv).astype(BF16)
        dp_ref[1, 0] = (du * gates_v[..., 1024:1536].astype(F32)).astype(BF16)
        dp_ref[2, 0] = (du * gates_v[..., 512:1024].astype(F32)).astype(BF16)
        parts = [_sum_tokens(dconv * m2), _sum_tokens(dconv * m1), _sum_tokens(dconv * u)]

        @pl.when(s == 0)
        def _():
            gcw_ref[...] = jnp.zeros_like(gcw_ref)

        for tap in range(3):
            gcw_ref[tap:tap + 1, :] += parts[tap]

    (dproj, gcw), _ = _pcall(
        body, name="conv_bwd", grid=(nblk,),
        in_specs=[main(512), ahead(512), main(1536), before(1536), ahead(1536),
                  pl.BlockSpec((3, 512), lambda s: (0, 0))],
        out_specs=[pl.BlockSpec((3, 1, N_RES, TI, 512), lambda s: (1, s // n_i, 0, s % n_i, 0)),
                   pl.BlockSpec((8, 512), lambda s: (0, 0))],
        out_shape=[jax.ShapeDtypeStruct((6, t // HALF, N_RES, SEG, 512), BF16), jax.ShapeDtypeStruct((8, 512), F32)],
        semantics=("arbitrary",), vmem_mb=40, after=after,
    )(_x4(dy), _x4(dy), _x4(gates), _x4(gates), _x4(gates), cw)
    return dproj.reshape(6, t, 512), gcw


def _xattn_fwd(x1, g, w_q, kv, w_o, *, tb):
    t, d = x1.shape
    hd = d // N_MEM_HEADS
    m = kv.shape[0]

    def body(x_ref, g_ref, wq_ref, k_ref, v_ref, wo_ref, x2_ref, h_ref, q_ref, o_ref):
        xv = x_ref[...]
        h = _rms_fwd(xv, g_ref[...])[0].astype(BF16)
        h_ref[...] = h
        q = _dot(h, wq_ref[...]).astype(BF16)
        q_ref[...] = q
        for hh in range(N_MEM_HEADS):
            sl = slice(hh * hd, (hh + 1) * hd)
            s = _dot_nt(q[:, sl], k_ref[:, sl]) * (1.0 / 16.0)
            e = jnp.exp(s - jnp.max(s, axis=1, keepdims=True))
            p = e / jnp.sum(e, axis=1, keepdims=True)
            o_ref[:, sl] = _dot(p.astype(BF16), v_ref[:, sl]).astype(BF16)
        x2_ref[...] = xv + _dot(o_ref[...], wo_ref[...])

    tok = pl.BlockSpec((tb, d), lambda i: (i, 0))
    full = pl.BlockSpec((d, d), lambda i: (0, 0))
    return pl.pallas_call(
        body, name="xattn_fwd", grid=(t // tb,),
        in_specs=[tok, pl.BlockSpec((1, d), lambda i: (0, 0)), full,
                  pl.BlockSpec((m, d), lambda i: (0, 0)), pl.BlockSpec((m, d), lambda i: (0, 1)), full],
        out_specs=[tok] * 4,
        out_shape=[jax.ShapeDtypeStruct((t, d), F32)] + [jax.ShapeDtypeStruct((t, d), BF16)] * 3,
        compiler_params=_params(("parallel",), 48),
    )(x1, g, w_q, kv, kv, w_o)


def _xattn_bwd(dx2, x1, g, q, w_q, kv, w_o, *, tb, after=()):
    t, d = x1.shape
    hd = d // N_MEM_HEADS
    m = kv.shape[0]

    def body(dx2_ref, x_ref, g_ref, q_ref, wq_ref, k_ref, v_ref, wo_ref,
             dx1_ref, dq_ref, dk_ref, dv_ref, gg_ref):
        i = pl.program_id(0)

        @pl.when(i == 0)
        def _():
            dk_ref[...] = jnp.zeros_like(dk_ref)
            dv_ref[...] = jnp.zeros_like(dv_ref)

        dx2 = dx2_ref[...]
        do = _dot_nt(dx2.astype(BF16), wo_ref[...]).astype(BF16)
        for hh in range(N_MEM_HEADS):
            sl = slice(hh * hd, (hh + 1) * hd)
            qh, kh, vh, doh = q_ref[:, sl], k_ref[:, sl], v_ref[:, sl], do[:, sl]
            s = _dot_nt(qh, kh) * (1.0 / 16.0)
            e = jnp.exp(s - jnp.max(s, axis=1, keepdims=True))
            p = e / jnp.sum(e, axis=1, keepdims=True)
            dp = _dot_nt(doh, vh)
            ds = (p * (dp - jnp.sum(dp * p, axis=1, keepdims=True)) * (1.0 / 16.0)).astype(BF16)
            dq_ref[:, sl] = _dot(ds, kh).astype(BF16)
            dk_ref[:, sl] += _dot_tn(ds, qh)
            dv_ref[:, sl] += _dot_tn(p.astype(BF16), doh)
        dh = _dot_nt(dq_ref[...], wq_ref[...])
        g_v = g_ref[...]
        _, xh, r = _rms_fwd(x_ref[...], g_v)
        dx1 = dx2 + _rms_bwd(dh, xh, r, g_v)
        dx1_ref[...] = dx1
        part = jnp.sum(dh * xh, axis=0, keepdims=True)

        @pl.when(i == 0)
        def _():
            gg_ref[...] = part

        @pl.when(i != 0)
        def _():
            gg_ref[...] += part

    tok = pl.BlockSpec((tb, d), lambda i: (i, 0))
    full = pl.BlockSpec((d, d), lambda i: (0, 0))
    acc = pl.BlockSpec((m, d), lambda i: (0, 0))
    res, _ = _pcall(
        body, name="xattn_bwd", grid=(t // tb,),
        in_specs=[tok, tok, pl.BlockSpec((1, d), lambda i: (0, 0)), tok, full,
                  pl.BlockSpec((m, d), lambda i: (0, 0)), pl.BlockSpec((m, d), lambda i: (0, 1)), full],
        out_specs=[tok, tok, acc, acc, pl.BlockSpec((1, d), lambda i: (0, 0))],
        out_shape=[jax.ShapeDtypeStruct((t, d), F32), jax.ShapeDtypeStruct((t, d), BF16),
                   jax.ShapeDtypeStruct((m, d), F32), jax.ShapeDtypeStruct((m, d), F32),
                   jax.ShapeDtypeStruct((1, d), F32)],
        semantics=("arbitrary",), vmem_mb=48, after=after,
    )(dx2, x1, g, q, w_q, kv, kv, w_o)
    return res


def _mlp_down_loss(a, w_down, x2, tgt, g, *, tb):
    t, d = x2.shape
    f = a.shape[1]

    def body(a_ref, w_ref, x_ref, t_ref, g_ref, dx_ref, dxb_ref, loss_ref, gg_ref):
        i = pl.program_id(0)
        av = a_ref[...].astype(F32)
        x3 = x_ref[...] + _dot((av * av).astype(BF16), w_ref[...])
        g_v = g_ref[...]
        out, xh, r = _rms_fwd(x3, g_v)
        err = out - t_ref[...]
        dout = err * (1.0 / d)
        dx = _rms_bwd(dout, xh, r, g_v)
        dx_ref[...] = dx
        dxb_ref[...] = dx.astype(BF16)
        part = jnp.sum(dout * xh, axis=0, keepdims=True)
        lpart = 0.5 * jnp.sum(jnp.mean(err * err, axis=-1, keepdims=True), axis=0, keepdims=True)
        lpart = jnp.broadcast_to(lpart, loss_ref.shape)

        @pl.when(i == 0)
        def _():
            gg_ref[...] = part
            loss_ref[...] = lpart

        @pl.when(i != 0)
        def _():
            gg_ref[...] += part
            loss_ref[...] += lpart

    tok = pl.BlockSpec((tb, d), lambda i: (i, 0))
    return pl.pallas_call(
        body, name="mlp_down_loss", grid=(t // tb,),
        in_specs=[pl.BlockSpec((tb, f), lambda i: (i, 0)), pl.BlockSpec((f, d), lambda i: (0, 0)), tok, tok,
                  pl.BlockSpec((1, d), lambda i: (0, 0))],
        out_specs=[tok, tok, pl.BlockSpec((8, 128), lambda i: (0, 0)), pl.BlockSpec((1, d), lambda i: (0, 0))],
        out_shape=[jax.ShapeDtypeStruct((t, d), F32), jax.ShapeDtypeStruct((t, d), BF16),
                   jax.ShapeDtypeStruct((8, 128), F32), jax.ShapeDtypeStruct((1, d), F32)],
        compiler_params=_params(("arbitrary",), 56),
    )(a, w_down, x2, tgt, g)


def _mlp_dpre(dx3, w_down, a, *, tb, bn):
    t, d = dx3.shape
    f = a.shape[1]

    def body(dx_ref, w_ref, a_ref, o_ref):
        o_ref[...] = (2.0 * a_ref[...].astype(F32) * _dot_nt(dx_ref[...], w_ref[...])).astype(BF16)

    return pl.pallas_call(
        body, name="mlp_dpre", grid=(t // tb, f // bn),
        in_specs=[pl.BlockSpec((tb, d), lambda i, j: (i, 0)), pl.BlockSpec((bn, d), lambda i, j: (j, 0)),
                  pl.BlockSpec((tb, bn), lambda i, j: (i, j))],
        out_specs=pl.BlockSpec((tb, bn), lambda i, j: (i, j)),
        out_shape=jax.ShapeDtypeStruct((t, f), BF16),
        compiler_params=_params(("parallel", "arbitrary"), 48),
    )(dx3, w_down, a)


def _adamw(gsum, w, m, v):
    m_new = ADAM_B1 * m + (1.0 - ADAM_B1) * gsum
    v_new = ADAM_B2 * v + (1.0 - ADAM_B2) * (gsum * gsum)
    m_hat = m_new / (1.0 - ADAM_B1 ** ADAM_STEP)
    v_hat = v_new / (1.0 - ADAM_B2 ** ADAM_STEP)
    delta = -ADAM_LR * (m_hat / (jnp.sqrt(v_hat) + ADAM_EPS) + ADAM_WD * w)
    return delta, m_new, v_new


def _sum_adamw(parts, w, m, v, *, name, tr):
    r, c = w.shape

    def body(p_ref, w_ref, m_ref, v_ref, g_ref, d_ref, mo_ref, vo_ref):
        g = p_ref[0].astype(F32)
        for k in range(1, N_DEV):
            g = g + p_ref[k].astype(F32)
        g_ref[...] = g
        d_ref[...], mo_ref[...], vo_ref[...] = _adamw(g, w_ref[...], m_ref[...], v_ref[...])

    blk = pl.BlockSpec((tr, c), lambda i: (i, 0))
    return pl.pallas_call(
        body, name=name, grid=(r // tr,),
        in_specs=[pl.BlockSpec((N_DEV, tr, c), lambda i: (0, i, 0)), blk, blk, blk],
        out_specs=[blk] * 4, out_shape=[jax.ShapeDtypeStruct((r, c), F32)] * 4,
        compiler_params=_params(("parallel",), 40),
    )(*[pltpu.with_memory_space_constraint(a, pltpu.HBM) for a in (parts, w, m, v)])


def _sum_small(parts):
    _, r, c = parts.shape

    def body(p_ref, o_ref):
        s = p_ref[0]
        for k in range(1, N_DEV):
            s = s + p_ref[k]
        o_ref[...] = s

    return pl.pallas_call(body, name="sum_small", out_shape=jax.ShapeDtypeStruct((r, c), F32))(parts)


def _adamw_small(g, w, m, v):
    def body(g_ref, w_ref, m_ref, v_ref, d_ref, mo_ref, vo_ref):
        d_ref[...], mo_ref[...], vo_ref[...] = _adamw(g_ref[...], w_ref[...], m_ref[...], v_ref[...])

    return pl.pallas_call(body, name="adamw_small", out_shape=[jax.ShapeDtypeStruct(g.shape, F32)] * 3)(g, w, m, v)


def _head_sum_matrix():
    r = lax.broadcasted_iota(jnp.int32, (512, 512), 0) // HEAD_DIM
    c = lax.broadcasted_iota(jnp.int32, (512, 512), 1) // HEAD_DIM
    return (r == c).astype(BF16)


_SHARD_AXIS = dict(w_in=1, w_out=0, w_q=0, w_kv=1, w_o=0, w_up=1, w_down=0, conv_w=None, small=None)


class _Weights:
    def __init__(self, full, shards=None):
        self.full = dict(full)
        self.shards = shards

    def rider(self, names, late=False):
        if self.shards is None:
            return None
        return _Gather([self.shards[n] for n in names], [_SHARD_AXIS[n] for n in names], late)

    def arrived(self, names, gathered):
        if gathered is not None:
            for n, g in zip(names, gathered):
                self.full[n] = g.transpose(1, 0, 2).reshape(g.shape[1], -1) if n == "conv_w" else g

    def __getitem__(self, name):
        return self.full[name]


class _Grads:
    def __init__(self, distributed):
        self.distributed = distributed
        self.local = {}
        self.pending = {}

    def add(self, name, g):
        self.local[name] = g

    def send(self, *names):
        if not self.distributed:
            return []
        rider = _Exchange([self.local[n] for n in names], [_SHARD_AXIS[n] for n in names])
        started = _exchange_start(rider, "send_" + "_".join(names))
        self.pending[names[0]] = (names, rider, started)
        return [started[3]]

    def wait(self, first_name, after):
        names, rider, started = self.pending.pop(first_name)
        return _exchange_wait(rider, started, after, "wait_" + "_".join(names))


def _ride(fn, *args, rider=None, **kw):
    if rider is None:
        return fn(*args, **kw), None
    return fn(*args, rider=rider, **kw)


def _local_step(x, mem, tgt, gains, weights, grads):
    names = ["w_in", "conv_w"]
    (x, tgt), got = _ride(_reorder, [x, tgt], "reorder_in", rider=weights.rider(names, late=True))
    weights.arrived(names, got)
    w_in, cw = weights["w_in"], weights["conv_w"]

    names = ["w_out", "w_kv"]
    (qkv, gates, h1), got = _ride(_proj, x, gains["g_mix"], w_in, tb=1024, rider=weights.rider(names))
    weights.arrived(names, got)
    names = ["w_q", "w_o", "w_up"]
    (attn, *lses), got = _ride(_attention_fwd, qkv, rider=weights.rider(names))
    weights.arrived(names, got)
    x1, merged = _mixer_fwd(x, attn, gates, cw, gains["g_attn_out"], gains["g_conv_out"], weights["w_out"])
    kv, mem_n = _norm_matmul(mem, gains["g_mem"], weights["w_kv"], name="mem_kv", out_dtype=BF16, tb=mem.shape[0],
                             bn=1024, save_h=True)
    x2, h2, qm, om = _xattn_fwd(x1, gains["g_xattn"], weights["w_q"], kv, weights["w_o"], tb=512)
    w_up = weights["w_up"]
    (a, h3), got = _ride(_norm_matmul, x2, gains["g_mlp"], w_up, name="mlp_up", out_dtype=BF16, tb=1024, bn=2048,
                         relu=True, save_h=True, rider=weights.rider(["w_down"], late=True))
    weights.arrived(["w_down"], got)
    w_down = weights["w_down"]
    dx3, dx3b, loss_blk, gg_final = _mlp_down_loss(a, w_down, x2, tgt, gains["g_final"], tb=512)

    dpre = _mlp_dpre(dx3b, w_down, a, tb=1024, bn=2048)
    grads.add("w_down", _matmul_tn(a, dx3b, name="grad_w_down", bm=512, bn=1024, square_a=True))
    sent = grads.send("w_down")
    grads.add("w_up", _matmul_tn(h3, dpre, name="grad_w_up", bm=1024, bn=1024, after=sent))
    sent = grads.send("w_up")
    dx2, dx2b, gg_mlp = _matmul_nt_normbwd(dpre, w_up, x2, gains["g_mlp"], dx3, name="mlp_dx", tb=512,
                                           also_bf16=True, after=sent)

    grads.add("w_o", _matmul_tn(om, dx2b, name="grad_w_o", bm=512, bn=512))
    dx1, dqm, dk, dv, gg_xattn = _xattn_bwd(dx2, x1, gains["g_xattn"], qm, weights["w_q"], kv, weights["w_o"], tb=512)
    grads.add("w_q", _matmul_tn(h2, dqm, name="grad_w_q", bm=1024, bn=512))
    dkv = jnp.concatenate([dk, dv], axis=1).astype(BF16)
    grads.add("w_kv", _matmul_tn(mem_n, dkv, name="grad_w_kv", bm=1024, bn=1024))
    _, gg_mem = _matmul_nt_normbwd(dkv, weights["w_kv"], mem, gains["g_mem"], None, name="mem_dx", tb=mem.shape[0])

    dattn, dsum, dy, gg_attn, gg_conv, gw_out = _mixer_bwd(dx1, merged, attn, gates, cw, gains["g_attn_out"],
                                                           gains["g_conv_out"], weights["w_out"], _head_sum_matrix())
    grads.add("w_out", gw_out)
    sent = grads.send("w_o", "w_q", "w_kv", "w_out")
    dproj, gcw = _conv_bwd(dy, gates, cw, after=sent)
    dproj = _attention_bwd(qkv, dattn, dsum, lses, dproj)
    grads.add("w_in", _matmul_tn(h1, dproj, name="grad_w_in", bm=1024, bn=512))
    sent = grads.send("w_in")
    grad_x, gg_mix = _matmul_nt_normbwd(dproj, w_in, x, gains["g_mix"], dx1, name="mixer_dx", tb=512,
                                        to_natural=True, after=sent)

    def part(v):
        return jnp.pad(v, ((0, SMALL_PART - v.shape[0]), (0, 1024 - v.shape[1])))

    parts = [gg_mix, gg_xattn, gg_mem, gg_mlp, gg_final, jnp.concatenate([gg_attn, gg_conv], axis=1), gcw, loss_blk]
    grads.add("small", jnp.concatenate([part(v) for v in parts], axis=0))
    return grad_x


SMALL_PART = 8
_BIG = ("w_in", "w_out", "w_q", "w_kv", "w_o", "w_up", "w_down")
_GAIN_ROWS = ("g_mix", "g_xattn", "g_mem", "g_mlp", "g_final")


def _pack_small(vals, conv):
    rows = [vals[k].reshape(1, -1) for k in _GAIN_ROWS]
    rows.append(jnp.concatenate([vals["g_attn_out"].reshape(1, -1), vals["g_conv_out"].reshape(1, -1)], axis=1))
    flat = conv.reshape(1, -1)
    rows.append(jnp.pad(flat, ((0, 0), (0, 1024 - flat.shape[1]))))
    rows.append(jnp.zeros((1, 1024), F32))
    return jnp.concatenate(rows, axis=0)


def kernel(x, mem, g_mix, w_in, conv_w, g_attn_out, g_conv_out, w_out, g_xattn, g_mem, w_q_mem, w_kv_mem, w_o_mem, g_mlp, w_up, w_down, g_final, loss_target, m_g_mix, m_w_in, m_conv_w, m_g_attn_out, m_g_conv_out, m_w_out, m_g_xattn, m_g_mem, m_w_q_mem, m_w_kv_mem, m_w_o_mem, m_g_mlp, m_w_up, m_w_down, m_g_final, v_g_mix, v_w_in, v_conv_w, v_g_attn_out, v_g_conv_out, v_w_out, v_g_xattn, v_g_mem, v_w_q_mem, v_w_kv_mem, v_w_o_mem, v_g_mlp, v_w_up, v_w_down, v_g_final):
    d = x.shape[-1]
    me = 4 * lax.axis_index("x") + 2 * lax.axis_index("y") + lax.axis_index("c")
    w_shards = dict(w_in=w_in, w_out=w_out, w_q=w_q_mem, w_kv=w_kv_mem, w_o=w_o_mem, w_up=w_up, w_down=w_down)
    m_shards = dict(w_in=m_w_in, w_out=m_w_out, w_q=m_w_q_mem, w_kv=m_w_kv_mem, w_o=m_w_o_mem, w_up=m_w_up,
                    w_down=m_w_down)
    v_shards = dict(w_in=v_w_in, w_out=v_w_out, w_q=v_w_q_mem, w_kv=v_w_kv_mem, w_o=v_w_o_mem, w_up=v_w_up,
                    w_down=v_w_down)
    gains = dict(g_mix=g_mix, g_attn_out=g_attn_out, g_conv_out=g_conv_out, g_xattn=g_xattn, g_mem=g_mem,
                 g_mlp=g_mlp, g_final=g_final)
    gains2 = {k: v.reshape(1, -1) for k, v in gains.items()}

    shards = {k: w_shards[k].astype(BF16) for k in _BIG}
    shards["conv_w"] = conv_w
    grads = _Grads(distributed=True)
    grad_x = _local_step(x[0], mem[0], loss_target[0], gains2, _Weights({}, shards), grads)

    after = grads.send("small")
    outs = {}
    tiles = dict(w_in=256, w_out=128, w_q=128, w_kv=256, w_o=128, w_up=256, w_down=256)
    for group in (("w_down",), ("w_up",), ("w_o", "w_q", "w_kv", "w_out"), ("w_in",)):
        for k, received in zip(group, grads.wait(group[0], after)):
            outs[k] = _sum_adamw(received, w_shards[k], m_shards[k], v_shards[k], name=f"adamw_{k}", tr=tiles[k])
            after = [outs[k][0]]
    small_received, = grads.wait("small", after)

    ssum = _sum_small(small_received)
    row = lambda i: ssum[SMALL_PART * i]
    loss = ssum[SMALL_PART * 7, 0]
    g_small = {k: row(i) for i, k in enumerate(_GAIN_ROWS)}
    g_small["g_attn_out"] = row(5)[0:512]
    g_small["g_conv_out"] = row(5)[512:1024]
    taps = ssum[SMALL_PART * 6:SMALL_PART * 6 + 3, 0:512]
    g_conv = lax.dynamic_slice_in_dim(taps, me * 64, 64, axis=1)
    m_small = dict(g_mix=m_g_mix, g_attn_out=m_g_attn_out, g_conv_out=m_g_conv_out, g_xattn=m_g_xattn,
                   g_mem=m_g_mem, g_mlp=m_g_mlp, g_final=m_g_final)
    v_small = dict(g_mix=v_g_mix, g_attn_out=v_g_attn_out, g_conv_out=v_g_conv_out, g_xattn=v_g_xattn,
                   g_mem=v_g_mem, g_mlp=v_g_mlp, g_final=v_g_final)
    packed = [_pack_small(g_small, g_conv), _pack_small(gains, conv_w), _pack_small(m_small, m_conv_w),
              _pack_small(v_small, v_conv_w)]
    upd = _adamw_small(*packed)

    def unpack(p):
        res = {k: p[i] for i, k in enumerate(_GAIN_ROWS)}
        res["g_attn_out"] = p[5, 0:512]
        res["g_conv_out"] = p[5, 512:1024]
        res["conv_w"] = p[6, 0:192].reshape(3, 64)
        return res

    g_small["conv_w"] = g_conv
    small_out = [g_small] + [unpack(p) for p in upd]
    names = {"g_mix": "g_mix", "w_in": "w_in", "conv_w": "conv_w", "g_attn_out": "g_attn_out",
             "g_conv_out": "g_conv_out", "w_out": "w_out", "g_xattn": "g_xattn", "g_mem": "g_mem",
             "w_q_mem": "w_q", "w_kv_mem": "w_kv", "w_o_mem": "w_o", "g_mlp": "g_mlp", "w_up": "w_up",
             "w_down": "w_down", "g_final": "g_final"}
    result = [loss, grad_x[None]]
    for which in range(4):
        for key in names.values():
            result.append(outs[key][which] if key in outs else small_out[which][key])
    return tuple(result)
```

```python
import math

import jax
import jax.numpy as jnp
from jax import lax
from jax.experimental import pallas as pl
from jax.experimental.pallas import tpu as pltpu

F32 = jnp.float32
BF16 = jnp.bfloat16
NORM_EPS = 1e-6
NEG_INF = -1e30
N_DEV = 8
BLK = 128
HEAD_DIM = 64
N_MEM_HEADS = 4
ADAM_LR = 0.001
ADAM_B1 = 0.9
ADAM_B2 = 0.999
ADAM_EPS = 1e-08
ADAM_WD = 0.01
ADAM_STEP = 10
MESH = pl.DeviceIdType.MESH
ANY = pl.BlockSpec(memory_space=pl.ANY)


def _dot(a, b):
    return jnp.dot(a, b, preferred_element_type=F32)


def _dot_nt(a, b):
    return lax.dot_general(a, b, (((1,), (1,)), ((), ())), preferred_element_type=F32)


def _dot_tn(a, b):
    return lax.dot_general(a, b, (((0,), (0,)), ((), ())), preferred_element_type=F32)


def _params(semantics, vmem_mb):
    return pltpu.CompilerParams(dimension_semantics=semantics, vmem_limit_bytes=vmem_mb << 20)


def _rms_fwd(x, g):
    r = lax.rsqrt(jnp.mean(x * x, axis=-1, keepdims=True) + NORM_EPS)
    xh = x * r
    return xh * g, xh, r


def _rms_bwd(dy, xh, r, g):
    gy = dy * g
    return r * (gy - xh * jnp.mean(xh * gy, axis=-1, keepdims=True))


def _position():
    x, y, c = lax.axis_index("x"), lax.axis_index("y"), lax.axis_index("c")
    return x, y, c


def _block_of(ref, j, axis, shard_shape):
    r, c = shard_shape
    if axis is None:
        return ref.at[j]
    if axis == 0:
        return ref.at[pl.ds(j * r, r), :]
    return ref.at[:, pl.ds(j * c, c)]


class _Gather:
    has_mid = True
    alias_pairs = ()

    def __init__(self, shards, axes, late=False):
        self.arrays = list(shards)
        self.axes = list(axes)
        self.late = late
        self.n = len(self.arrays)

    def out_shape(self):
        res = []
        for s, axis in zip(self.arrays, self.axes):
            r, c = s.shape
            shape = (N_DEV, r, c) if axis is None else (N_DEV * r, c) if axis == 0 else (r, N_DEV * c)
            res.append(jax.ShapeDtypeStruct(shape, s.dtype))
        return res

    def scratch(self):
        return [pltpu.SemaphoreType.DMA((self.n, 7)), pltpu.SemaphoreType.DMA((self.n, 7)),
                pltpu.SemaphoreType.DMA((self.n,))]

    def _ctx(self, ins, outs, sems):
        send_sems, recv_sems, local_sems = sems
        x, y, c = _position()
        me, sibling = (x, y, c), (x, y, 1 - c)
        chips = [(1 - x, y), (x, 1 - y), (1 - x, 1 - y)]

        def lin(px, py, pc):
            return 4 * px + 2 * py + pc

        def place(a, block):
            return _block_of(outs[a], lin(*block), self.axes[a], self.arrays[a].shape)

        def copy(a, k, block, to, src=None):
            dst = place(a, block)
            return pltpu.make_async_remote_copy(
                src_ref=dst if src is None else src, dst_ref=dst,
                send_sem=send_sems.at[a, k], recv_sem=recv_sems.at[a, k],
                device_id=to, device_id_type=MESH)

        def mine():
            return [pltpu.make_async_copy(ins[a], place(a, me), local_sems.at[a]) for a in range(self.n)]

        def first():
            res = []
            for a in range(self.n):
                res.append(copy(a, 0, me, sibling, src=ins[a]))
                res += [copy(a, 1 + j, me, (*chip, c), src=ins[a]) for j, chip in enumerate(chips)]
            return res

        return c, me, sibling, chips, copy, mine, first

    def start(self, ins, outs, sems):
        _, _, _, _, _, mine, first = self._ctx(ins, outs, sems)
        for cp in mine() + first():
            cp.start()

    def mid(self, ins, outs, sems):
        c, me, sibling, chips, copy, _, _ = self._ctx(ins, outs, sems)
        for j, chip in enumerate(chips):
            for a in range(self.n):
                copy(a, 1 + j, (*chip, c), me).wait_recv()
                copy(a, 4 + j, (*chip, c), sibling).start()

    def finish(self, ins, outs, sems):
        c, me, sibling, chips, copy, mine, first = self._ctx(ins, outs, sems)
        for a in range(self.n):
            copy(a, 0, sibling, me).wait_recv()
            for j, chip in enumerate(chips):
                copy(a, 4 + j, (*chip, 1 - c), me).wait_recv()
        for cp in first():
            cp.wait_send()
        for j, chip in enumerate(chips):
            for a in range(self.n):
                copy(a, 4 + j, (*chip, c), sibling).wait_send()
        for cp in mine():
            cp.wait()


class _Exchange:
    def __init__(self, parts, axes):
        self.n = len(parts)
        self.axes = list(axes)
        self.arrays = list(parts)

    def _piece(self, a):
        r, c = self.arrays[a].shape
        axis = self.axes[a]
        return (r, c) if axis is None else (r // N_DEV, c) if axis == 0 else (r, c // N_DEV)

    def out_shape(self):
        return [jax.ShapeDtypeStruct((N_DEV,) + self._piece(a), self.arrays[a].dtype) for a in range(self.n)]

    def semaphores(self):
        return [pltpu.SemaphoreType.DMA((7 * self.n,)), pltpu.SemaphoreType.DMA((7 * self.n,)),
                pltpu.SemaphoreType.DMA((self.n,))]

    def _ctx(self, ins, outs, sems):
        send_sems, recv_sems, local_sems = sems
        x, y, c = _position()
        me = 4 * x + 2 * y + c

        def src(a, j):
            return ins[a] if self.axes[a] is None else _block_of(ins[a], j, self.axes[a], self._piece(a))

        def dst(a, j):
            return outs[a].at[j]

        def local():
            return [pltpu.make_async_copy(src(a, me), dst(a, me), local_sems.at[a]) for a in range(self.n)]

        def remote(inbound):
            res = []
            for a in range(self.n):
                for k in range(1, N_DEV):
                    peer = (1 - x if k & 4 else x, 1 - y if k & 2 else y, 1 - c if k & 1 else c)
                    plin = 4 * peer[0] + 2 * peer[1] + peer[2]
                    res.append(pltpu.make_async_remote_copy(
                        src_ref=src(a, plin), dst_ref=dst(a, plin if inbound else me),
                        send_sem=send_sems.at[7 * a + k - 1], recv_sem=recv_sems.at[7 * a + k - 1],
                        device_id=peer, device_id_type=MESH))
            return res

        return local, remote

    def start(self, ins, outs, sems):
        local, remote = self._ctx(ins, outs, sems)
        for cp in local() + remote(False):
            cp.start()

    def finish(self, ins, outs, sems):
        local, remote = self._ctx(ins, outs, sems)
        for cp in remote(True):
            cp.wait_recv()
        for cp in remote(False):
            cp.wait_send()
        for cp in local():
            cp.wait()


def _exchange_start(rider, name):
    n = rider.n
    parts = rider.arrays
    lands = [lax.empty(s.shape, s.dtype) for s in rider.out_shape()]
    hbm = pl.BlockSpec(memory_space=pltpu.HBM)
    sem = pl.BlockSpec(memory_space=pltpu.SEMAPHORE)

    def body(*refs):
        ins, sems = refs[:n], refs[2 * n:2 * n + 3]
        outs, token = refs[2 * n + 3 + n:2 * n + 3 + 2 * n], refs[-1]
        rider.start(ins, outs, sems)
        token[...] = jnp.zeros_like(token)

    res = pl.pallas_call(
        body, name=name,
        out_shape=rider.semaphores() + [pltpu.HBM(p.shape, p.dtype) for p in parts]
                  + [pltpu.HBM(z.shape, z.dtype) for z in lands] + [jax.ShapeDtypeStruct((8, 128), F32)],
        in_specs=[hbm] * (2 * n), out_specs=[sem] * 3 + [hbm] * (2 * n) + [pl.BlockSpec(memory_space=pltpu.VMEM)],
        input_output_aliases={i: 3 + i for i in range(2 * n)},
        compiler_params=pltpu.CompilerParams(has_side_effects=pltpu.SideEffectType.DATAFLOW_SIDE_EFFECTING),
    )(*[pltpu.with_memory_space_constraint(a, pltpu.HBM) for a in parts + lands])
    return res[:3], res[3:3 + n], res[3 + n:3 + 2 * n], res[-1]


def _exchange_wait(rider, started, after, name):
    n = rider.n
    sems, parts, lands, _ = started
    hbm = pl.BlockSpec(memory_space=pltpu.HBM)
    sem = pl.BlockSpec(memory_space=pltpu.SEMAPHORE)

    def body(*refs):
        rider.finish(refs[:n], refs[n:2 * n], refs[2 * n:2 * n + 3])

    res = pl.pallas_call(
        body, name=name, out_shape=[pltpu.HBM(a.shape, a.dtype) for a in list(parts) + list(lands)],
        in_specs=[hbm] * (2 * n) + [sem] * 3 + [ANY] * len(after), out_specs=[hbm] * (2 * n),
        input_output_aliases={i: i for i in range(2 * n)},
        compiler_params=pltpu.CompilerParams(has_side_effects=pltpu.SideEffectType.DATAFLOW_SIDE_EFFECTING),
    )(*parts, *lands, *sems, *after)
    return list(res[n:])


def _pcall(body, *, name, grid, in_specs, out_specs, out_shape, scratch_shapes=(), semantics, vmem_mb, rider=None,
           aliases=None, after=()):
    in_specs, out_specs, out_shape = list(in_specs), list(out_specs), list(out_shape)
    scratch_shapes = list(scratch_shapes)
    aliases = dict(aliases or {})
    if rider is None:
        n_in, after = len(in_specs), list(after)

        def plain(*refs):
            body(*refs[:n_in], *refs[n_in + len(after):])

        call = pl.pallas_call(plain if after else body, name=name, grid=grid, in_specs=in_specs + [ANY] * len(after),
                              out_specs=out_specs, out_shape=out_shape, scratch_shapes=scratch_shapes,
                              input_output_aliases=aliases, compiler_params=_params(semantics, vmem_mb))
        return lambda *args: (list(call(*args, *after)), None)
    n_in, n_out, n_scr = len(in_specs), len(out_specs), len(scratch_shapes)
    r_in, r_shapes = len(rider.arrays), rider.out_shape()
    r_out = len(r_shapes)
    aliases.update({n_in + i: n_out + o for i, o in rider.alias_pairs})
    total = math.prod(grid)
    mid_step = total - 1 if rider.has_mid and rider.late else (3 * total) // 4

    def wrapped(*refs):
        bounds = [0, n_in, r_in, n_out, r_out, n_scr]
        for i in range(1, len(bounds)):
            bounds[i] += bounds[i - 1]
        a, ra, o, ro, s = (refs[bounds[i]:bounds[i + 1]] for i in range(5))
        rs = refs[bounds[5]:]
        step = pl.program_id(0)
        for k in range(1, len(grid)):
            step = step * grid[k] + pl.program_id(k)
        pl.when(step == 0)(lambda: rider.start(ra, ro, rs))
        body(*a, *o, *s)
        if rider.has_mid:
            pl.when(step == mid_step)(lambda: rider.mid(ra, ro, rs))
        pl.when(step == total - 1)(lambda: rider.finish(ra, ro, rs))

    call = pl.pallas_call(
        wrapped, name=name, grid=grid, in_specs=in_specs + [ANY] * r_in, out_specs=out_specs + [ANY] * r_out,
        out_shape=out_shape + r_shapes, scratch_shapes=scratch_shapes + rider.scratch(),
        input_output_aliases=aliases, compiler_params=_params(("arbitrary",) * len(grid), vmem_mb))

    def run(*args):
        res = call(*args, *rider.arrays)
        return list(res[:n_out]), list(res[n_out:])

    return run


def _norm_matmul(x, g, w, *, name, out_dtype, tb, bn, relu=False, save_h=False, rider=None):
    t, d = x.shape
    n = w.shape[1]

    def body(x_ref, g_ref, w_ref, o_ref, *rest):
        h_scr = rest[-1]

        @pl.when(pl.program_id(1) == 0)
        def _():
            h = _rms_fwd(x_ref[...], g_ref[...])[0].astype(BF16)
            h_scr[...] = h
            if save_h:
                rest[0][...] = h

        acc = _dot(h_scr[...], w_ref[...])
        if relu:
            acc = jnp.maximum(acc, 0.0)
        o_ref[...] = acc.astype(out_dtype)

    out_shape = [jax.ShapeDtypeStruct((t, n), out_dtype)]
    out_specs = [pl.BlockSpec((tb, bn), lambda i, j: (i, j))]
    if save_h:
        out_shape.append(jax.ShapeDtypeStruct((t, d), BF16))
        out_specs.append(pl.BlockSpec((tb, d), lambda i, j: (i, 0)))
    res, extra = _pcall(
        body, name=name, grid=(t // tb, n // bn),
        in_specs=[pl.BlockSpec((tb, d), lambda i, j: (i, 0)),
                  pl.BlockSpec((1, d), lambda i, j: (0, 0)),
                  pl.BlockSpec((d, bn), lambda i, j: (0, j))],
        out_specs=out_specs, out_shape=out_shape,
        scratch_shapes=[pltpu.VMEM((tb, d), BF16)],
        semantics=("parallel", "arbitrary"), vmem_mb=48, rider=rider,
    )(x, g, w)
    res = res if save_h else res[0]
    return res if rider is None else (res, extra)


def _proj(x, g, w, *, tb, rider=None):
    t, d = x.shape
    half = w.shape[1] // 2

    def body(x_ref, g_ref, w_ref, qkv_ref, gates_ref, h_ref, h_scr):
        j = pl.program_id(1)

        @pl.when(j == 0)
        def _():
            h = _rms_fwd(x_ref[...], g_ref[...])[0].astype(BF16)
            h_scr[...] = h
            h_ref[...] = h

        acc = _dot(h_scr[...], w_ref[...])

        @pl.when(j == 0)
        def _():
            qkv_ref[...] = acc

        @pl.when(j == 1)
        def _():
            gates_ref[...] = acc.astype(BF16)

    tok = lambda c: pl.BlockSpec((tb, c), lambda i, j: (i, 0))
    res, extra = _pcall(
        body, name="proj", grid=(t // tb, 2),
        in_specs=[tok(d), pl.BlockSpec((1, d), lambda i, j: (0, 0)), pl.BlockSpec((d, half), lambda i, j: (0, j))],
        out_specs=[tok(half), tok(half), tok(d)],
        out_shape=[jax.ShapeDtypeStruct((t, half), F32), jax.ShapeDtypeStruct((t, half), BF16),
                   jax.ShapeDtypeStruct((t, d), BF16)],
        scratch_shapes=[pltpu.VMEM((tb, d), BF16)],
        semantics=("parallel", "arbitrary"), vmem_mb=48, rider=rider,
    )(x, g, w)
    return res if rider is None else (res, extra)


def _matmul_nt_normbwd(dy, w, x, g, dres, *, name, tb, also_bf16=False, to_natural=False, after=()):
    t, d = x.shape
    stacked = dy.ndim == 3
    has_res = dres is not None
    n_i = SEG // TI
    if to_natural:
        tb = N_RES * TI

    def body(dy_ref, w_ref, x_ref, g_ref, *rest):
        rest = list(rest)
        dres_ref = rest.pop(0) if has_res else None
        dx_ref = rest.pop(0)
        dxb_ref = rest.pop(0) if also_bf16 else None
        gg_ref = rest.pop(0)
        i = pl.program_id(0)

        def rows(ref, *lead):
            v = ref[lead] if lead else ref[...]
            return v[0].reshape(tb, v.shape[-1]) if to_natural else v

        if stacked:
            kb = dy_ref.shape[-1]
            dh = _dot_nt(rows(dy_ref, 0), w_ref[:, 0:kb])
            for s in range(1, dy_ref.shape[0]):
                dh = dh + _dot_nt(rows(dy_ref, s), w_ref[:, s * kb:(s + 1) * kb])
        else:
            dh = _dot_nt(rows(dy_ref), w_ref[...])
        g_v = g_ref[...]
        _, xh, r = _rms_fwd(rows(x_ref), g_v)
        dx = _rms_bwd(dh, xh, r, g_v)
        if has_res:
            dx = dx + rows(dres_ref)
        if to_natural:
            scr = rest.pop(0)
            for cb in range(d // BLK):
                cols = slice(cb * BLK, (cb + 1) * BLK)
                slab = scr.at[cb]
                for res in range(N_RES):
                    slab[pl.ds(res, TI, stride=N_RES), :] = dx[res * TI:(res + 1) * TI, cols]
                dx_ref[:, cols] = slab[...]
        else:
            dx_ref[...] = dx
        if also_bf16:
            dxb_ref[...] = dx.astype(BF16)
        part = jnp.sum(dh * xh, axis=0, keepdims=True)

        @pl.when(i == 0)
        def _():
            gg_ref[...] = part

        @pl.when(i != 0)
        def _():
            gg_ref[...] += part

    tok = pl.BlockSpec((tb, d), lambda i: (i, 0))
    row = pl.BlockSpec((1, d), lambda i: (0, 0))
    if to_natural:
        act = pl.BlockSpec((1, N_RES, TI, d), lambda i: (i // n_i, 0, i % n_i, 0))
        dy_spec = pl.BlockSpec((dy.shape[0], 1, N_RES, TI, dy.shape[2]), lambda i: (0, i // n_i, 0, i % n_i, 0))
        dy, x = dy.reshape(dy.shape[0], t // HALF, N_RES, SEG, dy.shape[2]), _x4(x)
        dres = _x4(dres) if has_res else None
    elif stacked:
        act, dy_spec = tok, pl.BlockSpec((dy.shape[0], tb, dy.shape[2]), lambda i: (0, i, 0))
    else:
        act, dy_spec = tok, pl.BlockSpec((tb, dy.shape[1]), lambda i: (i, 0))
    in_specs = [dy_spec, pl.BlockSpec(w.shape, lambda i: (0, 0)), act, row]
    args = [dy, w, x, g]
    if has_res:
        in_specs.append(act)
        args.append(dres)
    out_specs = [tok] + ([tok] if also_bf16 else []) + [row]
    out_shape = ([jax.ShapeDtypeStruct((t, d), F32)] + ([jax.ShapeDtypeStruct((t, d), BF16)] if also_bf16 else [])
                 + [jax.ShapeDtypeStruct((1, d), F32)])
    res, _ = _pcall(
        body, name=name, grid=(t // tb,), in_specs=in_specs, out_specs=out_specs, out_shape=out_shape,
        scratch_shapes=[pltpu.VMEM((d // BLK, tb, BLK), F32)] if to_natural else [],
        semantics=("arbitrary",), vmem_mb=56, after=after,
    )(*args)
    return res


def _matmul_tn(a, b, *, name, bm, bn, square_a=False, after=()):
    t, m = a.shape
    stacked = b.ndim == 3
    n = b.shape[0] * bn if stacked else b.shape[1]

    def body(a_ref, b_ref, o_ref):
        av = a_ref[...]
        if square_a:
            av = av.astype(F32)
            av = (av * av).astype(BF16)
        o_ref[...] = _dot_tn(av, b_ref[...]).astype(BF16)

    res, _ = _pcall(
        body, name=name, grid=(m // bm, n // bn),
        in_specs=[pl.BlockSpec((t, bm), lambda i, j: (0, i)),
                  pl.BlockSpec((None, t, bn), lambda i, j: (j, 0, 0)) if stacked
                  else pl.BlockSpec((t, bn), lambda i, j: (0, j))],
        out_specs=[pl.BlockSpec((bm, bn), lambda i, j: (i, j))], out_shape=[jax.ShapeDtypeStruct((m, n), BF16)],
        semantics=("parallel", "parallel"), vmem_mb=56, after=after,
    )(a, b)
    return res[0]


N_RES = 16
SEG = 128
HALF = N_RES * SEG
TI = 32
HALO = 16


def _x4(a):
    return a.reshape(a.shape[0] // HALF, N_RES, SEG, a.shape[1])


def _reorder(arrays, name, to_bf16=(), rider=None):
    t, c = arrays[0].shape
    n, k = len(arrays), len(to_bf16)
    n_i = SEG // TI
    n_s = t // (TI * N_RES)

    def body(*refs):
        scr = refs[-1]
        for i_ref, o_ref in zip(refs[n:n + k], refs[2 * n + k:2 * (n + k)]):
            o_ref[...] = i_ref[...].astype(BF16)
        for i_ref, o_ref in zip(refs[:n], refs[n + k:2 * n + k]):
            for cb in range(c // BLK):
                cols = slice(cb * BLK, (cb + 1) * BLK)
                slab = scr.at[cb]
                slab[...] = i_ref[:, cols]
                for r in range(N_RES):
                    o_ref[0, r, :, cols] = slab[pl.ds(r, TI, stride=N_RES), :]

    rows = [pl.BlockSpec((w.shape[0] // n_s, w.shape[1]), lambda s: (s, 0)) for w in to_bf16]
    res, extra = _pcall(
        body, name=name, grid=(n_s,),
        in_specs=[pl.BlockSpec((TI * N_RES, c), lambda s: (s, 0))] * n + rows,
        out_specs=[pl.BlockSpec((1, N_RES, TI, c), lambda s: (s // n_i, 0, s % n_i, 0))] * n + rows,
        out_shape=[jax.ShapeDtypeStruct((t // HALF, N_RES, SEG, c), F32)] * n
        + [jax.ShapeDtypeStruct(w.shape, BF16) for w in to_bf16],
        scratch_shapes=[pltpu.VMEM((c // BLK, TI * N_RES, BLK), F32)],
        semantics=("parallel",), vmem_mb=32, rider=rider,
    )(*arrays, *to_bf16)
    res = [r.reshape(t, c) for r in res[:n]] + res[n:]
    return res if rider is None else (res, extra)


_PATTERNS = ((1, 16, 8, SEG), (4, 4, 32, 4 * SEG), (16, 1, SEG, 0))
_FIRST = {1: 1, 4: 4, 16: 16}


def _group_rows(d, g):
    a = g >> 4
    if d == 16:
        base = a * HALF + (g & 15) * SEG
        prev = base - HALF
    elif d == 4:
        c = (g >> 2) & 3
        base = a * HALF + (g & 3) * SEG + c * 32
        prev = jnp.where(c > 0, base - 32, base - HALF + 96)
    else:
        c = g & 15
        base = a * HALF + c * 8
        prev = jnp.where(c > 0, base - 8, base - HALF + 120)
    return base, prev


def _load_rows(ref, base, n, rows, stride):
    parts = [ref[pl.ds(pl.multiple_of(base + j * stride, 8), rows), :] for j in range(n)]
    return parts[0] if n == 1 else jnp.concatenate(parts, axis=0)


def _store_rows(ref, base, val, n, rows, stride, add=False):
    for j in range(n):
        sl = pl.ds(pl.multiple_of(base + j * stride, 8), rows)
        piece = val[j * rows:(j + 1) * rows, :]
        if add:
            ref[sl, :] += piece
        else:
            ref[sl, :] = piece


def _band_bias(n, rows):
    shift = rows.bit_length() - 1
    lq = lax.broadcasted_iota(jnp.int32, (BLK, BLK), 0)
    lk = lax.broadcasted_iota(jnp.int32, (BLK, BLK), 1)
    iq = (lq & (rows - 1)) * n + (lq >> shift)
    ik = (lk & (rows - 1)) * n + (lk >> shift)
    zero = jnp.zeros((BLK, BLK), F32)
    return jnp.where(ik >= iq, zero, NEG_INF), jnp.where(ik <= iq, zero, NEG_INF)


def _set_bias(bias_scr, n, rows):
    prev_b, cur_b = _band_bias(n, rows)
    for half in range(2):
        bias_scr[half * BLK:(half + 1) * BLK, 0:BLK] = prev_b
        bias_scr[half * BLK:(half + 1) * BLK, BLK:2 * BLK] = cur_b


SCALE = 1.0 / math.sqrt(HEAD_DIM)


def _head_consts(value=1.0):
    lane_lo = lax.broadcasted_iota(jnp.int32, (BLK, BLK), 1) < HEAD_DIM
    return lane_lo, [jnp.where(lane_lo, value, 0.0).astype(BF16), jnp.where(lane_lo, 0.0, value).astype(BF16)]


def _stack_heads(v, head_mask):
    return jnp.concatenate([v * head_mask[0], v * head_mask[1]], axis=0)


def _unstack_heads(v2, lane_lo):
    return jnp.where(lane_lo, v2[:BLK], v2[BLK:])


def _rows_per_head(v, lane_lo):
    rolled = pltpu.roll(v, HEAD_DIM, axis=1)
    return jnp.concatenate([jnp.where(lane_lo, v, rolled), jnp.where(lane_lo, rolled, v)], axis=0)


WIDTH = 4


def _loop(lo, hi, fn, width=None):
    if width is None:
        def body(g, carry):
            fn(g)
            return carry

        if hi > lo:
            lax.fori_loop(lo, hi, body, 0)
        return
    while hi > lo:
        trips = (hi - lo) // width
        if trips:
            def body(i, carry, lo=lo, width=width):
                fn([lo + width * i + j for j in range(width)])
                return carry

            lax.fori_loop(0, trips, body, 0)
            lo += trips * width
        width = max(1, width // 2)


def _mix_weights(l1, l2, l3):
    mx = jnp.maximum(jnp.maximum(l1, l2), l3)
    e1, e2, e3 = jnp.exp(l1 - mx), jnp.exp(l2 - mx), jnp.exp(l3 - mx)
    inv = 1.0 / (e1 + e2 + e3)
    return e1 * inv, e2 * inv, e3 * inv


def _attention_fwd(qkv, rider=None):
    t = qkv.shape[0]
    groups = 16 * (t // HALF)

    def body(q_ref, k_ref, v_ref, attn_ref, l1_ref, l2_ref, l3_ref, o_scr, bias_scr):
        lane_lo, q_mask = _head_consts(SCALE)
        l_refs = (l1_ref, l2_ref, l3_ref)
        for p, (d, n, rows, stride) in enumerate(_PATTERNS):
            _set_bias(bias_scr, n, rows)
            o_p, l_p = o_scr.at[p], l_refs[p]

            def block(gs, has_prev):
                at = [_group_rows(d, g) for g in gs]

                def load(ref, b):
                    return _load_rows(ref, b, n, rows, stride).astype(BF16)

                q2 = [_stack_heads(load(q_ref, b), q_mask) for b, _ in at]
                k2 = [load(k_ref, b) for b, _ in at]
                v2 = [load(v_ref, b) for b, _ in at]
                if has_prev:
                    k2 = [jnp.concatenate([load(k_ref, pv), k], axis=0) for (_, pv), k in zip(at, k2)]
                    v2 = [jnp.concatenate([load(v_ref, pv), v], axis=0) for (_, pv), v in zip(at, v2)]
                s = [_dot_nt(q, k) for q, k in zip(q2, k2)]
                s = [x + (bias_scr[...] if has_prev else bias_scr[:, BLK:2 * BLK]) for x in s]
                mx = [jnp.max(x, axis=1, keepdims=True) for x in s]
                e = [jnp.exp(x - m) for x, m in zip(s, mx)]
                den = [jnp.sum(x, axis=1, keepdims=True) for x in e]
                o2 = [_dot(x.astype(BF16), v) * (1.0 / dn) for x, v, dn in zip(e, v2, den)]
                lse2 = [jnp.broadcast_to(m + jnp.log(dn), (2 * BLK, BLK)) for m, dn in zip(mx, den)]
                for (b, _), o, l in zip(at, o2, lse2):
                    _store_rows(o_p, b, _unstack_heads(o, lane_lo), n, rows, stride)
                    _store_rows(l_p, b, _unstack_heads(l, lane_lo), n, rows, stride)

            _loop(0, _FIRST[d], lambda gs: block(gs, False), width=2 * WIDTH)
            _loop(_FIRST[d], groups, lambda gs: block(gs, True), width=2 * WIDTH)

        def mix(i):
            sl = pl.ds(pl.multiple_of(i * 256, 256), 256)
            w = _mix_weights(l1_ref[sl, :], l2_ref[sl, :], l3_ref[sl, :])
            attn_ref[sl, :] = w[0] * o_scr[0, sl, :] + w[1] * o_scr[1, sl, :] + w[2] * o_scr[2, sl, :]

        _loop(0, t // 256, mix)

    def col(c0):
        return pl.BlockSpec((t, BLK), lambda hp: (0, c0 + hp))

    res, extra = _pcall(
        body, name="attention_fwd", grid=(4,), in_specs=[col(0), col(4), col(8)], out_specs=[col(0)] * 4,
        out_shape=[jax.ShapeDtypeStruct((t, 512), F32)] * 4,
        scratch_shapes=[pltpu.VMEM((3, t, BLK), F32), pltpu.VMEM((2 * BLK, 2 * BLK), F32)],
        semantics=("parallel",), vmem_mb=48, rider=rider,
    )(qkv, qkv, qkv)
    return res if rider is None else (res, extra)


def _attention_bwd(qkv, dattn, dsum, lses, dproj):
    t = qkv.shape[0]
    groups = 16 * (t // HALF)

    def body(q_ref, k_ref, v_ref, da_ref, ds_ref, l1_ref, l2_ref, l3_ref, kept_ref, out_ref, acc, bias_scr):
        del kept_ref
        lane_lo, head_mask = _head_consts()
        q_mask = _head_consts(SCALE)[1]
        l_refs = (l1_ref, l2_ref, l3_ref)

        def clear(i):
            sl = pl.ds(pl.multiple_of(i * 512, 512), 512)
            for s in range(3):
                acc[s, sl, :] = jnp.zeros((512, BLK), F32)

        _loop(0, t // 512, clear)
        dq_acc, dk_acc, dv_acc = acc.at[0], acc.at[1], acc.at[2]
        for p, (d, n, rows, stride) in enumerate(_PATTERNS):
            _set_bias(bias_scr, n, rows)

            def block(gs, has_prev):
                at = [_group_rows(d, g) for g in gs]

                def load(ref, b):
                    return _load_rows(ref, b, n, rows, stride)

                def put(ref, b, val):
                    _store_rows(ref, b, val, n, rows, stride, add=True)

                def wide(x):
                    return jnp.concatenate([x, x], axis=1) if has_prev else x

                lse = [[load(ref, b) for ref in l_refs] for b, _ in at]
                w = [_mix_weights(*ls)[p] for ls in lse]
                do2 = [_stack_heads((wg * load(da_ref, b)).astype(BF16), head_mask) for wg, (b, _) in zip(w, at)]
                dl2 = [wide(_rows_per_head(wg * load(ds_ref, b), lane_lo)) for wg, (b, _) in zip(w, at)]
                lse2 = [wide(_rows_per_head(ls[p], lane_lo)) for ls in lse]
                q2 = [_stack_heads(load(q_ref, b).astype(BF16), q_mask) for b, _ in at]
                k2 = [load(k_ref, b).astype(BF16) for b, _ in at]
                v2 = [load(v_ref, b).astype(BF16) for b, _ in at]
                if has_prev:
                    k2 = [jnp.concatenate([load(k_ref, pv).astype(BF16), k], axis=0) for (_, pv), k in zip(at, k2)]
                    v2 = [jnp.concatenate([load(v_ref, pv).astype(BF16), v], axis=0) for (_, pv), v in zip(at, v2)]
                s = [_dot_nt(q, k) for q, k in zip(q2, k2)]
                dp = [_dot_nt(do, v) for do, v in zip(do2, v2)]
                pr = [jnp.exp(x + (bias_scr[...] if has_prev else bias_scr[:, BLK:2 * BLK]) - l)
                      for x, l in zip(s, lse2)]
                ds = [(pg * (x - dl)).astype(BF16) for pg, x, dl in zip(pr, dp, dl2)]
                dq2 = [_dot(x, k) * SCALE for x, k in zip(ds, k2)]
                dk2 = [_dot_tn(x, q) for x, q in zip(ds, q2)]
                dv2 = [_dot_tn(pg.astype(BF16), do) for pg, do in zip(pr, do2)]
                for (b, pv), dq, dk, dv in zip(at, dq2, dk2, dv2):
                    put(dq_acc, b, _unstack_heads(dq, lane_lo))
                    if has_prev:
                        put(dk_acc, pv, dk[:BLK])
                        put(dv_acc, pv, dv[:BLK])
                        put(dk_acc, b, dk[BLK:])
                        put(dv_acc, b, dv[BLK:])
                    else:
                        put(dk_acc, b, dk)
                        put(dv_acc, b, dv)

            _loop(0, _FIRST[d], lambda gs: block(gs, False), width=WIDTH)
            _loop(_FIRST[d], groups, lambda gs: block(gs, True), width=WIDTH)

        def emit(i):
            sl = pl.ds(pl.multiple_of(i * 512, 512), 512)
            for s in range(3):
                out_ref[s, sl, :] = acc[s, sl, :].astype(BF16)

        _loop(0, t // 512, emit)

    def col(c0):
        return pl.BlockSpec((t, BLK), lambda hp: (0, c0 + hp))

    res, _ = _pcall(
        body, name="attention_bwd", grid=(4,),
        in_specs=[col(0), col(4), col(8)] + [col(0)] * 5 + [ANY],
        out_specs=[pl.BlockSpec((3, t, BLK), lambda hp: (0, 0, hp))],
        out_shape=[jax.ShapeDtypeStruct(dproj.shape, BF16)],
        scratch_shapes=[pltpu.VMEM((3, t, BLK), F32), pltpu.VMEM((2 * BLK, 2 * BLK), F32)],
        semantics=("parallel",), vmem_mb=56, aliases={8: 0},
    )(qkv, qkv, qkv, dattn, dsum, *lses, dproj)
    return res[0]


def _order_specs(t):
    n_i = SEG // TI
    nblk = (t // HALF) * n_i
    per = TI // HALO

    def main(c, col=0):
        return pl.BlockSpec((1, N_RES, TI, c), lambda s: (s // n_i, 0, s % n_i, col))

    def before(c, col=0):
        return pl.BlockSpec((1, 2, HALO, c), lambda s: (jnp.maximum(s - 1, 0) // n_i, N_RES // 2 - 1,
                                                        (jnp.maximum(s - 1, 0) % n_i) * per + per - 1, col))

    def after(c, col=0):
        return pl.BlockSpec((1, 2, HALO, c), lambda s: (jnp.minimum(s + 1, nblk - 1) // n_i, 0,
                                                        (jnp.minimum(s + 1, nblk - 1) % n_i) * per, col))

    return nblk, main, before, after


def _shift_in(v, row_in, up):
    rows = v.shape[0]
    idx = lax.broadcasted_iota(jnp.int32, v.shape, 0)
    fill = jnp.broadcast_to(row_in, v.shape)
    if up:
        return jnp.where(idx == rows - 1, fill, pltpu.roll(v, rows - 1, axis=0))
    return jnp.where(idx == 0, fill, pltpu.roll(v, 1, axis=0))


def _taps_behind(u, before):
    s15 = _shift_in(u[N_RES - 1], before[1, HALO - 1:HALO, :], up=False)
    s14 = _shift_in(u[N_RES - 2], before[0, HALO - 1:HALO, :], up=False)
    m1 = jnp.concatenate([s15[None], u[:N_RES - 1]], axis=0)
    m2 = jnp.concatenate([s14[None], s15[None], u[:N_RES - 2]], axis=0)
    return m1, m2


def _taps_ahead(u, after):
    t0 = _shift_in(u[0], after[0, 0:1, :], up=True)
    t1 = _shift_in(u[1], after[1, 0:1, :], up=True)
    p1 = jnp.concatenate([u[1:], t0[None]], axis=0)
    p2 = jnp.concatenate([u[2:], t0[None], t1[None]], axis=0)
    return p1, p2


def _conv_fwd(gates, before, first, cw):
    gates, before = gates.astype(F32), before.astype(F32)
    bg, cg, xc = gates[..., 0:512], gates[..., 512:1024], gates[..., 1024:1536]
    u = cg * xc
    ub = before[..., 512:1024] * before[..., 1024:1536]
    ub = jnp.where(first, jnp.zeros_like(ub), ub)
    m1, m2 = _taps_behind(u, ub)
    conv = m2 * cw[0:1, :] + m1 * cw[1:2, :] + u * cw[2:3, :]
    return bg, u, m1, m2, conv


def _sum_tokens(v):
    return jnp.sum(jnp.sum(v, axis=0), axis=0, keepdims=True)


def _mixer_fwd(x, attn, gates, cw, g_a, g_c, w_out):
    t, d = x.shape
    nblk, main, before, _ = _order_specs(t)
    rows = N_RES * TI

    def body(x_ref, at_ref, gt_ref, gb_ref, cw_ref, ga_ref, gc_ref, wa_ref, wb_ref, x1_ref, mg_ref):
        an = _rms_fwd(at_ref[0], ga_ref[...])[0].astype(BF16)
        bg, _, _, _, conv = _conv_fwd(gt_ref[0], gb_ref[0], pl.program_id(0) == 0, cw_ref[...])
        cn = _rms_fwd(bg * conv, gc_ref[...])[0].astype(BF16)
        mg_ref[0, :, :, 0:512] = an
        mg_ref[0, :, :, 512:1024] = cn
        y = _dot(an.reshape(rows, 512), wa_ref[...]) + _dot(cn.reshape(rows, 512), wb_ref[...])
        x1_ref[0] = x_ref[0] + y.reshape(N_RES, TI, d)

    const = lambda r, c, i0=0: pl.BlockSpec((r, c), lambda s: (i0, 0))
    x1, merged = pl.pallas_call(
        body, name="mixer_fwd", grid=(nblk,),
        in_specs=[main(d), main(512), main(1536), before(1536), const(3, 512), const(1, 512), const(1, 512),
                  const(512, d), const(512, d, 1)],
        out_specs=[main(d), main(d)],
        out_shape=[jax.ShapeDtypeStruct(_x4(x).shape, F32), jax.ShapeDtypeStruct(_x4(x).shape, BF16)],
        compiler_params=_params(("parallel",), 48),
    )(_x4(x), _x4(attn), _x4(gates), _x4(gates), cw, g_a, g_c, w_out, w_out)
    return x1.reshape(t, d), merged.reshape(t, d)


def _mixer_bwd(dx1, merged, attn, gates, cw, g_a, g_c, w_out, head_sum, after=()):
    t, d = dx1.shape
    nblk, main, before, _ = _order_specs(t)
    rows = N_RES * TI

    def body(dx_ref, mg_ref, at_ref, gt_ref, gb_ref, cw_ref, ga_ref, gc_ref, wa_ref, wb_ref, hs_ref,
             da_ref, dsum_ref, dy_ref, gga_ref, ggc_ref, gw_ref, acc_w):
        s = pl.program_id(0)
        dxb = dx_ref[0].reshape(rows, d).astype(BF16)

        @pl.when(s == 0)
        def _():
            acc_w[...] = jnp.zeros_like(acc_w)

        acc_w[...] += _dot_tn(mg_ref[0].reshape(rows, d), dxb)

        @pl.when(s == nblk - 1)
        def _():
            gw_ref[...] = acc_w[...].astype(BF16)

        dma = _dot_nt(dxb, wa_ref[...]).reshape(N_RES, TI, 512)
        dmc = _dot_nt(dxb, wb_ref[...]).reshape(N_RES, TI, 512)
        attn_v, g_av = at_ref[0], ga_ref[...]
        _, ah, ra = _rms_fwd(attn_v, g_av)
        dattn = _rms_bwd(dma, ah, ra, g_av)
        da_ref[0] = dattn
        z = (dattn * attn_v).reshape(rows, 512)
        hs = hs_ref[...]
        z1 = z.astype(BF16)
        z2 = (z - z1.astype(F32)).astype(BF16)
        dsum_ref[0] = (_dot(z1, hs) + _dot(z2, hs)).reshape(N_RES, TI, 512)
        bg, _, _, _, conv = _conv_fwd(gt_ref[0], gb_ref[0], s == 0, cw_ref[...])
        g_cv = gc_ref[...]
        _, yh, rc = _rms_fwd(bg * conv, g_cv)
        dy_ref[0] = _rms_bwd(dmc, yh, rc, g_cv)
        pa, pc = _sum_tokens(dma * ah), _sum_tokens(dmc * yh)

        @pl.when(s == 0)
        def _():
            gga_ref[...] = pa
            ggc_ref[...] = pc

        @pl.when(s != 0)
        def _():
            gga_ref[...] += pa
            ggc_ref[...] += pc

    const = lambda r, c, i0=0: pl.BlockSpec((r, c), lambda s: (i0, 0))
    shape4 = _x4(attn).shape
    res, _ = _pcall(
        body, name="mixer_bwd", grid=(nblk,),
        in_specs=[main(d), main(d), main(512), main(1536), before(1536), const(3, 512), const(1, 512), const(1, 512),
                  const(512, d), const(512, d, 1), const(512, 512)],
        out_specs=[main(512)] * 3 + [const(1, 512), const(1, 512), const(d, d)],
        out_shape=[jax.ShapeDtypeStruct(shape4, F32)] * 3 + [jax.ShapeDtypeStruct((1, 512), F32)] * 2
        + [jax.ShapeDtypeStruct((d, d), BF16)],
        scratch_shapes=[pltpu.VMEM((d, d), F32)],
        semantics=("arbitrary",), vmem_mb=48, after=after,
    )(_x4(dx1), _x4(merged), _x4(attn), _x4(gates), _x4(gates), cw, g_a, g_c, w_out, w_out, head_sum)
    return [r.reshape(t, 512) for r in res[:3]] + res[3:]


def _conv_bwd(dy, gates, cw, after=()):
    t = dy.shape[0]
    nblk, main, before, ahead = _order_specs(t)
    n_i = SEG // TI

    def body(dy_ref, dya_ref, gt_ref, gb_ref, ga_ref, cw_ref, dp_ref, gcw_ref):
        s = pl.program_id(0)
        cw_v, gates_v = cw_ref[...], gt_ref[0]
        bg, u, m1, m2, conv = _conv_fwd(gates_v, gb_ref[0], s == 0, cw_v)
        dy_v = dy_ref[0]
        dconv = dy_v * bg
        dca = dya_ref[0] * ga_ref[0][..., 0:512].astype(F32)
        dca = jnp.where(s == nblk - 1, jnp.zeros_like(dca), dca)
        p1, p2 = _taps_ahead(dconv, dca)
        du = dconv * cw_v[2:3, :] + p1 * cw_v[1:2, :] + p2 * cw_v[0:1, :]
        dp_ref[0, 0] = (dy_v * conv).astype(BF16)
        dp_ref[1, 0] = (du * gates_v[..., 1024:1536].astype(F32)).astype(BF16)
        dp_ref[2, 0] = (du * gates_v[..., 512:1024].astype(F32)).astype(BF16)
        parts = [_sum_tokens(dconv * m2), _sum_tokens(dconv * m1), _sum_tokens(dconv * u)]

        @pl.when(s == 0)
        def _():
            gcw_ref[...] = jnp.zeros_like(gcw_ref)

        for tap in range(3):
            gcw_ref[tap:tap + 1, :] += parts[tap]

    (dproj, gcw), _ = _pcall(
        body, name="conv_bwd", grid=(nblk,),
        in_specs=[main(512), ahead(512), main(1536), before(1536), ahead(1536),
                  pl.BlockSpec((3, 512), lambda s: (0, 0))],
        out_specs=[pl.BlockSpec((3, 1, N_RES, TI, 512), lambda s: (1, s // n_i, 0, s % n_i, 0)),
                   pl.BlockSpec((8, 512), lambda s: (0, 0))],
        out_shape=[jax.ShapeDtypeStruct((6, t // HALF, N_RES, SEG, 512), BF16), jax.ShapeDtypeStruct((8, 512), F32)],
        semantics=("arbitrary",), vmem_mb=40, after=after,
    )(_x4(dy), _x4(dy), _x4(gates), _x4(gates), _x4(gates), cw)
    return dproj.reshape(6, t, 512), gcw


def _xattn_fwd(x1, g, w_q, kv, w_o, *, tb):
    t, d = x1.shape
    hd = d // N_MEM_HEADS
    m = kv.shape[0]

    def body(x_ref, g_ref, wq_ref, k_ref, v_ref, wo_ref, x2_ref, h_ref, q_ref, o_ref):
        xv = x_ref[...]
        h = _rms_fwd(xv, g_ref[...])[0].astype(BF16)
        h_ref[...] = h
        q = _dot(h, wq_ref[...]).astype(BF16)
        q_ref[...] = q
        for hh in range(N_MEM_HEADS):
            sl = slice(hh * hd, (hh + 1) * hd)
            s = _dot_nt(q[:, sl], k_ref[:, sl]) * (1.0 / 16.0)
            e = jnp.exp(s - jnp.max(s, axis=1, keepdims=True))
            p = e / jnp.sum(e, axis=1, keepdims=True)
            o_ref[:, sl] = _dot(p.astype(BF16), v_ref[:, sl]).astype(BF16)
        x2_ref[...] = xv + _dot(o_ref[...], wo_ref[...])

    tok = pl.BlockSpec((tb, d), lambda i: (i, 0))
    full = pl.BlockSpec((d, d), lambda i: (0, 0))
    return pl.pallas_call(
        body, name="xattn_fwd", grid=(t // tb,),
        in_specs=[tok, pl.BlockSpec((1, d), lambda i: (0, 0)), full,
                  pl.BlockSpec((m, d), lambda i: (0, 0)), pl.BlockSpec((m, d), lambda i: (0, 1)), full],
        out_specs=[tok] * 4,
        out_shape=[jax.ShapeDtypeStruct((t, d), F32)] + [jax.ShapeDtypeStruct((t, d), BF16)] * 3,
        compiler_params=_params(("parallel",), 48),
    )(x1, g, w_q, kv, kv, w_o)


def _xattn_bwd(dx2, x1, g, q, w_q, kv, w_o, *, tb, after=()):
    t, d = x1.shape
    hd = d // N_MEM_HEADS
    m = kv.shape[0]

    def body(dx2_ref, x_ref, g_ref, q_ref, wq_ref, k_ref, v_ref, wo_ref,
             dx1_ref, dq_ref, dk_ref, dv_ref, gg_ref):
        i = pl.program_id(0)

        @pl.when(i == 0)
        def _():
            dk_ref[...] = jnp.zeros_like(dk_ref)
            dv_ref[...] = jnp.zeros_like(dv_ref)

        dx2 = dx2_ref[...]
        do = _dot_nt(dx2.astype(BF16), wo_ref[...]).astype(BF16)
        for hh in range(N_MEM_HEADS):
            sl = slice(hh * hd, (hh + 1) * hd)
            qh, kh, vh, doh = q_ref[:, sl], k_ref[:, sl], v_ref[:, sl], do[:, sl]
            s = _dot_nt(qh, kh) * (1.0 / 16.0)
            e = jnp.exp(s - jnp.max(s, axis=1, keepdims=True))
            p = e / jnp.sum(e, axis=1, keepdims=True)
            dp = _dot_nt(doh, vh)
            ds = (p * (dp - jnp.sum(dp * p, axis=1, keepdims=True)) * (1.0 / 16.0)).astype(BF16)
            dq_ref[:, sl] = _dot(ds, kh).astype(BF16)
            dk_ref[:, sl] += _dot_tn(ds, qh)
            dv_ref[:, sl] += _dot_tn(p.astype(BF16), doh)
        dh = _dot_nt(dq_ref[...], wq_ref[...])
        g_v = g_ref[...]
        _, xh, r = _rms_fwd(x_ref[...], g_v)
        dx1 = dx2 + _rms_bwd(dh, xh, r, g_v)
        dx1_ref[...] = dx1
        part = jnp.sum(dh * xh, axis=0, keepdims=True)

        @pl.when(i == 0)
        def _():
            gg_ref[...] = part

        @pl.when(i != 0)
        def _():
            gg_ref[...] += part

    tok = pl.BlockSpec((tb, d), lambda i: (i, 0))
    full = pl.BlockSpec((d, d), lambda i: (0, 0))
    acc = pl.BlockSpec((m, d), lambda i: (0, 0))
    res, _ = _pcall(
        body, name="xattn_bwd", grid=(t // tb,),
        in_specs=[tok, tok, pl.BlockSpec((1, d), lambda i: (0, 0)), tok, full,
                  pl.BlockSpec((m, d), lambda i: (0, 0)), pl.BlockSpec((m, d), lambda i: (0, 1)), full],
        out_specs=[tok, tok, acc, acc, pl.BlockSpec((1, d), lambda i: (0, 0))],
        out_shape=[jax.ShapeDtypeStruct((t, d), F32), jax.ShapeDtypeStruct((t, d), BF16),
                   jax.ShapeDtypeStruct((m, d), F32), jax.ShapeDtypeStruct((m, d), F32),
                   jax.ShapeDtypeStruct((1, d), F32)],
        semantics=("arbitrary",), vmem_mb=48, after=after,
    )(dx2, x1, g, q, w_q, kv, kv, w_o)
    return res


def _mlp_down_loss(a, w_down, x2, tgt, g, *, tb):
    t, d = x2.shape
    f = a.shape[1]

    def body(a_ref, w_ref, x_ref, t_ref, g_ref, dx_ref, dxb_ref, loss_ref, gg_ref):
        i = pl.program_id(0)
        av = a_ref[...].astype(F32)
        x3 = x_ref[...] + _dot((av * av).astype(BF16), w_ref[...])
        g_v = g_ref[...]
        out, xh, r = _rms_fwd(x3, g_v)
        err = out - t_ref[...]
        dout = err * (1.0 / d)
        dx = _rms_bwd(dout, xh, r, g_v)
        dx_ref[...] = dx
        dxb_ref[...] = dx.astype(BF16)
        part = jnp.sum(dout * xh, axis=0, keepdims=True)
        lpart = 0.5 * jnp.sum(jnp.mean(err * err, axis=-1, keepdims=True), axis=0, keepdims=True)
        lpart = jnp.broadcast_to(lpart, loss_ref.shape)

        @pl.when(i == 0)
        def _():
            gg_ref[...] = part
            loss_ref[...] = lpart

        @pl.when(i != 0)
        def _():
            gg_ref[...] += part
            loss_ref[...] += lpart

    tok = pl.BlockSpec((tb, d), lambda i: (i, 0))
    return pl.pallas_call(
        body, name="mlp_down_loss", grid=(t // tb,),
        in_specs=[pl.BlockSpec((tb, f), lambda i: (i, 0)), pl.BlockSpec((f, d), lambda i: (0, 0)), tok, tok,
                  pl.BlockSpec((1, d), lambda i: (0, 0))],
        out_specs=[tok, tok, pl.BlockSpec((8, 128), lambda i: (0, 0)), pl.BlockSpec((1, d), lambda i: (0, 0))],
        out_shape=[jax.ShapeDtypeStruct((t, d), F32), jax.ShapeDtypeStruct((t, d), BF16),
                   jax.ShapeDtypeStruct((8, 128), F32), jax.ShapeDtypeStruct((1, d), F32)],
        compiler_params=_params(("arbitrary",), 56),
    )(a, w_down, x2, tgt, g)


def _mlp_dpre(dx3, w_down, a, *, tb, bn):
    t, d = dx3.shape
    f = a.shape[1]

    def body(dx_ref, w_ref, a_ref, o_ref):
        o_ref[...] = (2.0 * a_ref[...].astype(F32) * _dot_nt(dx_ref[...], w_ref[...])).astype(BF16)

    return pl.pallas_call(
        body, name="mlp_dpre", grid=(t // tb, f // bn),
        in_specs=[pl.BlockSpec((tb, d), lambda i, j: (i, 0)), pl.BlockSpec((bn, d), lambda i, j: (j, 0)),
                  pl.BlockSpec((tb, bn), lambda i, j: (i, j))],
        out_specs=pl.BlockSpec((tb, bn), lambda i, j: (i, j)),
        out_shape=jax.ShapeDtypeStruct((t, f), BF16),
        compiler_params=_params(("parallel", "arbitrary"), 48),
    )(dx3, w_down, a)


def _adamw(gsum, w, m, v):
    m_new = ADAM_B1 * m + (1.0 - ADAM_B1) * gsum
    v_new = ADAM_B2 * v + (1.0 - ADAM_B2) * (gsum * gsum)
    m_hat = m_new / (1.0 - ADAM_B1 ** ADAM_STEP)
    v_hat = v_new / (1.0 - ADAM_B2 ** ADAM_STEP)
    delta = -ADAM_LR * (m_hat / (jnp.sqrt(v_hat) + ADAM_EPS) + ADAM_WD * w)
    return delta, m_new, v_new


def _sum_adamw(parts, w, m, v, *, name, tr):
    r, c = w.shape

    def body(p_ref, w_ref, m_ref, v_ref, g_ref, d_ref, mo_ref, vo_ref):
        g = p_ref[0].astype(F32)
        for k in range(1, N_DEV):
            g = g + p_ref[k].astype(F32)
        g_ref[...] = g
        d_ref[...], mo_ref[...], vo_ref[...] = _adamw(g, w_ref[...], m_ref[...], v_ref[...])

    blk = pl.BlockSpec((tr, c), lambda i: (i, 0))
    return pl.pallas_call(
        body, name=name, grid=(r // tr,),
        in_specs=[pl.BlockSpec((N_DEV, tr, c), lambda i: (0, i, 0)), blk, blk, blk],
        out_specs=[blk] * 4, out_shape=[jax.ShapeDtypeStruct((r, c), F32)] * 4,
        compiler_params=_params(("parallel",), 40),
    )(*[pltpu.with_memory_space_constraint(a, pltpu.HBM) for a in (parts, w, m, v)])


def _sum_small(parts):
    _, r, c = parts.shape

    def body(p_ref, o_ref):
        s = p_ref[0]
        for k in range(1, N_DEV):
            s = s + p_ref[k]
        o_ref[...] = s

    return pl.pallas_call(body, name="sum_small", out_shape=jax.ShapeDtypeStruct((r, c), F32))(parts)


def _adamw_small(g, w, m, v):
    def body(g_ref, w_ref, m_ref, v_ref, d_ref, mo_ref, vo_ref):
        d_ref[...], mo_ref[...], vo_ref[...] = _adamw(g_ref[...], w_ref[...], m_ref[...], v_ref[...])

    return pl.pallas_call(body, name="adamw_small", out_shape=[jax.ShapeDtypeStruct(g.shape, F32)] * 3)(g, w, m, v)


def _head_sum_matrix():
    r = lax.broadcasted_iota(jnp.int32, (512, 512), 0) // HEAD_DIM
    c = lax.broadcasted_iota(jnp.int32, (512, 512), 1) // HEAD_DIM
    return (r == c).astype(BF16)


_SHARD_AXIS = dict(w_in=1, w_out=0, w_q=0, w_kv=1, w_o=0, w_up=1, w_down=0, conv_w=None, small=None)


class _Weights:
    def __init__(self, full, shards=None, raw=None):
        self.full = dict(full)
        self.shards = shards
        self.raw = dict(raw or {})

    def cast(self, arrays):
        if self.raw:
            self.shards.update(zip(self.raw, arrays))

    def rider(self, names, late=False):
        if self.shards is None:
            return None
        return _Gather([self.shards[n] for n in names], [_SHARD_AXIS[n] for n in names], late)

    def arrived(self, names, gathered):
        if gathered is not None:
            for n, g in zip(names, gathered):
                self.full[n] = g.transpose(1, 0, 2).reshape(g.shape[1], -1) if n == "conv_w" else g

    def __getitem__(self, name):
        return self.full[name]


class _Grads:
    def __init__(self, distributed):
        self.distributed = distributed
        self.local = {}
        self.pending = {}

    def add(self, name, g):
        self.local[name] = g

    def send(self, *names):
        if not self.distributed:
            return []
        rider = _Exchange([self.local[n] for n in names], [_SHARD_AXIS[n] for n in names])
        started = _exchange_start(rider, "send_" + "_".join(names))
        self.pending[names[0]] = (names, rider, started)
        return [started[3]]

    def wait(self, first_name, after):
        names, rider, started = self.pending.pop(first_name)
        return _exchange_wait(rider, started, after, "wait_" + "_".join(names))


def _ride(fn, *args, rider=None, **kw):
    if rider is None:
        return fn(*args, **kw), None
    return fn(*args, rider=rider, **kw)


def _local_step(x, mem, tgt, gains, weights, grads):
    names = ["w_in", "conv_w"]
    (x, tgt, *as_bf16), got = _ride(_reorder, [x, tgt], "reorder_in", to_bf16=list(weights.raw.values()),
                                    rider=weights.rider(names, late=True))
    weights.arrived(names, got)
    weights.cast(as_bf16)
    w_in, cw = weights["w_in"], weights["conv_w"]

    names = ["w_out", "w_kv"]
    (qkv, gates, h1), got = _ride(_proj, x, gains["g_mix"], w_in, tb=1024, rider=weights.rider(names))
    weights.arrived(names, got)
    names = ["w_q", "w_o", "w_up"]
    (attn, *lses), got = _ride(_attention_fwd, qkv, rider=weights.rider(names))
    weights.arrived(names, got)
    x1, merged = _mixer_fwd(x, attn, gates, cw, gains["g_attn_out"], gains["g_conv_out"], weights["w_out"])
    kv, mem_n = _norm_matmul(mem, gains["g_mem"], weights["w_kv"], name="mem_kv", out_dtype=BF16, tb=mem.shape[0],
                             bn=1024, save_h=True)
    x2, h2, qm, om = _xattn_fwd(x1, gains["g_xattn"], weights["w_q"], kv, weights["w_o"], tb=512)
    w_up = weights["w_up"]
    (a, h3), got = _ride(_norm_matmul, x2, gains["g_mlp"], w_up, name="mlp_up", out_dtype=BF16, tb=1024, bn=2048,
                         relu=True, save_h=True, rider=weights.rider(["w_down"], late=True))
    weights.arrived(["w_down"], got)
    w_down = weights["w_down"]
    dx3, dx3b, loss_blk, gg_final = _mlp_down_loss(a, w_down, x2, tgt, gains["g_final"], tb=512)

    dpre = _mlp_dpre(dx3b, w_down, a, tb=1024, bn=2048)
    grads.add("w_down", _matmul_tn(a, dx3b, name="grad_w_down", bm=512, bn=1024, square_a=True))
    sent = grads.send("w_down")
    grads.add("w_up", _matmul_tn(h3, dpre, name="grad_w_up", bm=1024, bn=1024, after=sent))
    sent = grads.send("w_up")
    dx2, dx2b, gg_mlp = _matmul_nt_normbwd(dpre, w_up, x2, gains["g_mlp"], dx3, name="mlp_dx", tb=512,
                                           also_bf16=True, after=sent)

    grads.add("w_o", _matmul_tn(om, dx2b, name="grad_w_o", bm=512, bn=512))
    dx1, dqm, dk, dv, gg_xattn = _xattn_bwd(dx2, x1, gains["g_xattn"], qm, weights["w_q"], kv, weights["w_o"], tb=512)
    grads.add("w_q", _matmul_tn(h2, dqm, name="grad_w_q", bm=1024, bn=512))
    dkv = jnp.concatenate([dk, dv], axis=1).astype(BF16)
    grads.add("w_kv", _matmul_tn(mem_n, dkv, name="grad_w_kv", bm=1024, bn=1024))
    _, gg_mem = _matmul_nt_normbwd(dkv, weights["w_kv"], mem, gains["g_mem"], None, name="mem_dx", tb=mem.shape[0])

    dattn, dsum, dy, gg_attn, gg_conv, gw_out = _mixer_bwd(dx1, merged, attn, gates, cw, gains["g_attn_out"],
                                                           gains["g_conv_out"], weights["w_out"], _head_sum_matrix())
    grads.add("w_out", gw_out)
    sent = grads.send("w_o", "w_q", "w_kv", "w_out")
    dproj, gcw = _conv_bwd(dy, gates, cw, after=sent)
    dproj = _attention_bwd(qkv, dattn, dsum, lses, dproj)
    grads.add("w_in", _matmul_tn(h1, dproj, name="grad_w_in", bm=1024, bn=512))
    sent = grads.send("w_in")
    grad_x, gg_mix = _matmul_nt_normbwd(dproj, w_in, x, gains["g_mix"], dx1, name="mixer_dx", tb=512,
                                        to_natural=True, after=sent)

    def part(v):
        return jnp.pad(v, ((0, SMALL_PART - v.shape[0]), (0, 1024 - v.shape[1])))

    parts = [gg_mix, gg_xattn, gg_mem, gg_mlp, gg_final, jnp.concatenate([gg_attn, gg_conv], axis=1), gcw, loss_blk]
    grads.add("small", jnp.concatenate([part(v) for v in parts], axis=0))
    return grad_x


SMALL_PART = 8
_BIG = ("w_in", "w_out", "w_q", "w_kv", "w_o", "w_up", "w_down")
_GAIN_ROWS = ("g_mix", "g_xattn", "g_mem", "g_mlp", "g_final")


def _pack_small(vals, conv):
    rows = [vals[k].reshape(1, -1) for k in _GAIN_ROWS]
    rows.append(jnp.concatenate([vals["g_attn_out"].reshape(1, -1), vals["g_conv_out"].reshape(1, -1)], axis=1))
    flat = conv.reshape(1, -1)
    rows.append(jnp.pad(flat, ((0, 0), (0, 1024 - flat.shape[1]))))
    rows.append(jnp.zeros((1, 1024), F32))
    return jnp.concatenate(rows, axis=0)


def kernel(x, mem, g_mix, w_in, conv_w, g_attn_out, g_conv_out, w_out, g_xattn, g_mem, w_q_mem, w_kv_mem, w_o_mem, g_mlp, w_up, w_down, g_final, loss_target, m_g_mix, m_w_in, m_conv_w, m_g_attn_out, m_g_conv_out, m_w_out, m_g_xattn, m_g_mem, m_w_q_mem, m_w_kv_mem, m_w_o_mem, m_g_mlp, m_w_up, m_w_down, m_g_final, v_g_mix, v_w_in, v_conv_w, v_g_attn_out, v_g_conv_out, v_w_out, v_g_xattn, v_g_mem, v_w_q_mem, v_w_kv_mem, v_w_o_mem, v_g_mlp, v_w_up, v_w_down, v_g_final):
    d = x.shape[-1]
    me = 4 * lax.axis_index("x") + 2 * lax.axis_index("y") + lax.axis_index("c")
    w_shards = dict(w_in=w_in, w_out=w_out, w_q=w_q_mem, w_kv=w_kv_mem, w_o=w_o_mem, w_up=w_up, w_down=w_down)
    m_shards = dict(w_in=m_w_in, w_out=m_w_out, w_q=m_w_q_mem, w_kv=m_w_kv_mem, w_o=m_w_o_mem, w_up=m_w_up,
                    w_down=m_w_down)
    v_shards = dict(w_in=v_w_in, w_out=v_w_out, w_q=v_w_q_mem, w_kv=v_w_kv_mem, w_o=v_w_o_mem, w_up=v_w_up,
                    w_down=v_w_down)
    gains = dict(g_mix=g_mix, g_attn_out=g_attn_out, g_conv_out=g_conv_out, g_xattn=g_xattn, g_mem=g_mem,
                 g_mlp=g_mlp, g_final=g_final)
    gains2 = {k: v.reshape(1, -1) for k, v in gains.items()}

    shards = dict(w_in=w_in.astype(BF16), conv_w=conv_w)
    later = {k: w_shards[k] for k in _BIG if k != "w_in"}
    grads = _Grads(distributed=True)
    grad_x = _local_step(x[0], mem[0], loss_target[0], gains2, _Weights({}, shards, later), grads)

    after = grads.send("small")
    outs = {}
    tiles = dict(w_in=256, w_out=128, w_q=128, w_kv=256, w_o=128, w_up=256, w_down=256)
    for group in (("w_down",), ("w_up",), ("w_o", "w_q", "w_kv", "w_out"), ("w_in",)):
        for k, received in zip(group, grads.wait(group[0], after)):
            outs[k] = _sum_adamw(received, w_shards[k], m_shards[k], v_shards[k], name=f"adamw_{k}", tr=tiles[k])
            after = [outs[k][0]]
    small_received, = grads.wait("small", after)

    ssum = _sum_small(small_received)
    row = lambda i: ssum[SMALL_PART * i]
    loss = ssum[SMALL_PART * 7, 0]
    g_small = {k: row(i) for i, k in enumerate(_GAIN_ROWS)}
    g_small["g_attn_out"] = row(5)[0:512]
    g_small["g_conv_out"] = row(5)[512:1024]
    taps = ssum[SMALL_PART * 6:SMALL_PART * 6 + 3, 0:512]
    g_conv = lax.dynamic_slice_in_dim(taps, me * 64, 64, axis=1)
    m_small = dict(g_mix=m_g_mix, g_attn_out=m_g_attn_out, g_conv_out=m_g_conv_out, g_xattn=m_g_xattn,
                   g_mem=m_g_mem, g_mlp=m_g_mlp, g_final=m_g_final)
    v_small = dict(g_mix=v_g_mix, g_attn_out=v_g_attn_out, g_conv_out=v_g_conv_out, g_xattn=v_g_xattn,
                   g_mem=v_g_mem, g_mlp=v_g_mlp, g_final=v_g_final)
    packed = [_pack_small(g_small, g_conv), _pack_small(gains, conv_w), _pack_small(m_small, m_conv_w),
              _pack_small(v_small, v_conv_w)]
    upd = _adamw_small(*packed)

    def unpack(p):
        res = {k: p[i] for i, k in enumerate(_GAIN_ROWS)}
        res["g_attn_out"] = p[5, 0:512]
        res["g_conv_out"] = p[5, 512:1024]
        res["conv_w"] = p[6, 0:192].reshape(3, 64)
        return res

    g_small["conv_w"] = g_conv
    small_out = [g_small] + [unpack(p) for p in upd]
    names = {"g_mix": "g_mix", "w_in": "w_in", "conv_w": "conv_w", "g_attn_out": "g_attn_out",
             "g_conv_out": "g_conv_out", "w_out": "w_out", "g_xattn": "g_xattn", "g_mem": "g_mem",
             "w_q_mem": "w_q", "w_kv_mem": "w_kv", "w_o_mem": "w_o", "g_mlp": "g_mlp", "w_up": "w_up",
             "w_down": "w_down", "g_final": "g_final"}
    result = [loss, grad_x[None]]
    for which in range(4):
        for key in names.values():
            result.append(outs[key][which] if key in outs else small_out[which][key])
    return tuple(result)
```

```python
import math

import jax
import jax.numpy as jnp
from jax import lax
from jax.experimental import pallas as pl
from jax.experimental.pallas import tpu as pltpu

F32 = jnp.float32
BF16 = jnp.bfloat16
NORM_EPS = 1e-6
NEG_INF = -1e30
N_DEV = 8
BLK = 128
HEAD_DIM = 64
N_MEM_HEADS = 4
ADAM_LR = 0.001
ADAM_B1 = 0.9
ADAM_B2 = 0.999
ADAM_EPS = 1e-08
ADAM_WD = 0.01
ADAM_STEP = 10
MESH = pl.DeviceIdType.MESH
ANY = pl.BlockSpec(memory_space=pl.ANY)


def _dot(a, b):
    return jnp.dot(a, b, preferred_element_type=F32)


def _dot_nt(a, b):
    return lax.dot_general(a, b, (((1,), (1,)), ((), ())), preferred_element_type=F32)


def _dot_tn(a, b):
    return lax.dot_general(a, b, (((0,), (0,)), ((), ())), preferred_element_type=F32)


def _params(semantics, vmem_mb):
    return pltpu.CompilerParams(dimension_semantics=semantics, vmem_limit_bytes=vmem_mb << 20)


def _rms_fwd(x, g):
    r = lax.rsqrt(jnp.mean(x * x, axis=-1, keepdims=True) + NORM_EPS)
    xh = x * r
    return xh * g, xh, r


def _rms_bwd(dy, xh, r, g):
    gy = dy * g
    return r * (gy - xh * jnp.mean(xh * gy, axis=-1, keepdims=True))


def _position():
    x, y, c = lax.axis_index("x"), lax.axis_index("y"), lax.axis_index("c")
    return x, y, c


def _block_of(ref, j, axis, shard_shape):
    r, c = shard_shape
    if axis is None:
        return ref.at[j]
    if axis == 0:
        return ref.at[pl.ds(j * r, r), :]
    return ref.at[:, pl.ds(j * c, c)]


class _Gather:
    has_mid = True
    alias_pairs = ()

    def __init__(self, shards, axes, late=False):
        self.arrays = list(shards)
        self.axes = list(axes)
        self.late = late
        self.n = len(self.arrays)

    def out_shape(self):
        res = []
        for s, axis in zip(self.arrays, self.axes):
            r, c = s.shape
            shape = (N_DEV, r, c) if axis is None else (N_DEV * r, c) if axis == 0 else (r, N_DEV * c)
            res.append(jax.ShapeDtypeStruct(shape, s.dtype))
        return res

    def scratch(self):
        return [pltpu.SemaphoreType.DMA((self.n, 7)), pltpu.SemaphoreType.DMA((self.n, 7)),
                pltpu.SemaphoreType.DMA((self.n,))]

    def _ctx(self, ins, outs, sems):
        send_sems, recv_sems, local_sems = sems
        x, y, c = _position()
        me, sibling = (x, y, c), (x, y, 1 - c)
        chips = [(1 - x, y), (x, 1 - y), (1 - x, 1 - y)]

        def lin(px, py, pc):
            return 4 * px + 2 * py + pc

        def place(a, block):
            return _block_of(outs[a], lin(*block), self.axes[a], self.arrays[a].shape)

        def copy(a, k, block, to, src=None):
            dst = place(a, block)
            return pltpu.make_async_remote_copy(
                src_ref=dst if src is None else src, dst_ref=dst,
                send_sem=send_sems.at[a, k], recv_sem=recv_sems.at[a, k],
                device_id=to, device_id_type=MESH)

        def mine():
            return [pltpu.make_async_copy(ins[a], place(a, me), local_sems.at[a]) for a in range(self.n)]

        def first():
            res = []
            for a in range(self.n):
                res.append(copy(a, 0, me, sibling, src=ins[a]))
                res += [copy(a, 1 + j, me, (*chip, c), src=ins[a]) for j, chip in enumerate(chips)]
            return res

        return c, me, sibling, chips, copy, mine, first

    def start(self, ins, outs, sems):
        _, _, _, _, _, mine, first = self._ctx(ins, outs, sems)
        for cp in mine() + first():
            cp.start()

    def mid(self, ins, outs, sems):
        c, me, sibling, chips, copy, _, _ = self._ctx(ins, outs, sems)
        for j, chip in enumerate(chips):
            for a in range(self.n):
                copy(a, 1 + j, (*chip, c), me).wait_recv()
                copy(a, 4 + j, (*chip, c), sibling).start()

    def finish(self, ins, outs, sems):
        c, me, sibling, chips, copy, mine, first = self._ctx(ins, outs, sems)
        for a in range(self.n):
            copy(a, 0, sibling, me).wait_recv()
            for j, chip in enumerate(chips):
                copy(a, 4 + j, (*chip, 1 - c), me).wait_recv()
        for cp in first():
            cp.wait_send()
        for j, chip in enumerate(chips):
            for a in range(self.n):
                copy(a, 4 + j, (*chip, c), sibling).wait_send()
        for cp in mine():
            cp.wait()


class _Exchange:
    def __init__(self, parts, axes):
        self.n = len(parts)
        self.axes = list(axes)
        self.arrays = list(parts)

    def _piece(self, a):
        r, c = self.arrays[a].shape
        axis = self.axes[a]
        return (r, c) if axis is None else (r // N_DEV, c) if axis == 0 else (r, c // N_DEV)

    def out_shape(self):
        return [jax.ShapeDtypeStruct((N_DEV,) + self._piece(a), self.arrays[a].dtype) for a in range(self.n)]

    def semaphores(self):
        return [pltpu.SemaphoreType.DMA((7 * self.n,)), pltpu.SemaphoreType.DMA((7 * self.n,)),
                pltpu.SemaphoreType.DMA((self.n,))]

    def _ctx(self, ins, outs, sems):
        send_sems, recv_sems, local_sems = sems
        x, y, c = _position()
        me = 4 * x + 2 * y + c

        def src(a, j):
            return ins[a] if self.axes[a] is None else _block_of(ins[a], j, self.axes[a], self._piece(a))

        def dst(a, j):
            return outs[a].at[j]

        def local():
            return [pltpu.make_async_copy(src(a, me), dst(a, me), local_sems.at[a]) for a in range(self.n)]

        def remote(inbound):
            res = []
            for a in range(self.n):
                for k in range(1, N_DEV):
                    peer = (1 - x if k & 4 else x, 1 - y if k & 2 else y, 1 - c if k & 1 else c)
                    plin = 4 * peer[0] + 2 * peer[1] + peer[2]
                    res.append(pltpu.make_async_remote_copy(
                        src_ref=src(a, plin), dst_ref=dst(a, plin if inbound else me),
                        send_sem=send_sems.at[7 * a + k - 1], recv_sem=recv_sems.at[7 * a + k - 1],
                        device_id=peer, device_id_type=MESH))
            return res

        return local, remote

    def start(self, ins, outs, sems):
        local, remote = self._ctx(ins, outs, sems)
        for cp in local() + remote(False):
            cp.start()

    def finish(self, ins, outs, sems):
        local, remote = self._ctx(ins, outs, sems)
        for cp in remote(True):
            cp.wait_recv()
        for cp in remote(False):
            cp.wait_send()
        for cp in local():
            cp.wait()


def _exchange_start(rider, name):
    n = rider.n
    parts = rider.arrays
    lands = [lax.empty(s.shape, s.dtype) for s in rider.out_shape()]
    hbm = pl.BlockSpec(memory_space=pltpu.HBM)
    sem = pl.BlockSpec(memory_space=pltpu.SEMAPHORE)

    def body(*refs):
        ins, sems = refs[:n], refs[2 * n:2 * n + 3]
        outs, token = refs[2 * n + 3 + n:2 * n + 3 + 2 * n], refs[-1]
        rider.start(ins, outs, sems)
        token[...] = jnp.zeros_like(token)

    res = pl.pallas_call(
        body, name=name,
        out_shape=rider.semaphores() + [pltpu.HBM(p.shape, p.dtype) for p in parts]
                  + [pltpu.HBM(z.shape, z.dtype) for z in lands] + [jax.ShapeDtypeStruct((8, 128), F32)],
        in_specs=[hbm] * (2 * n), out_specs=[sem] * 3 + [hbm] * (2 * n) + [pl.BlockSpec(memory_space=pltpu.VMEM)],
        input_output_aliases={i: 3 + i for i in range(2 * n)},
        compiler_params=pltpu.CompilerParams(has_side_effects=pltpu.SideEffectType.DATAFLOW_SIDE_EFFECTING),
    )(*[pltpu.with_memory_space_constraint(a, pltpu.HBM) for a in parts + lands])
    return res[:3], res[3:3 + n], res[3 + n:3 + 2 * n], res[-1]


def _exchange_wait(rider, started, after, name):
    n = rider.n
    sems, parts, lands, _ = started
    hbm = pl.BlockSpec(memory_space=pltpu.HBM)
    sem = pl.BlockSpec(memory_space=pltpu.SEMAPHORE)

    def body(*refs):
        rider.finish(refs[:n], refs[n:2 * n], refs[2 * n:2 * n + 3])

    res = pl.pallas_call(
        body, name=name, out_shape=[pltpu.HBM(a.shape, a.dtype) for a in list(parts) + list(lands)],
        in_specs=[hbm] * (2 * n) + [sem] * 3 + [ANY] * len(after), out_specs=[hbm] * (2 * n),
        input_output_aliases={i: i for i in range(2 * n)},
        compiler_params=pltpu.CompilerParams(has_side_effects=pltpu.SideEffectType.DATAFLOW_SIDE_EFFECTING),
    )(*parts, *lands, *sems, *after)
    return list(res[n:])


def _pcall(body, *, name, grid, in_specs, out_specs, out_shape, scratch_shapes=(), semantics, vmem_mb, rider=None,
           aliases=None, after=()):
    in_specs, out_specs, out_shape = list(in_specs), list(out_specs), list(out_shape)
    scratch_shapes = list(scratch_shapes)
    aliases = dict(aliases or {})
    if rider is None:
        n_in, after = len(in_specs), list(after)

        def plain(*refs):
            body(*refs[:n_in], *refs[n_in + len(after):])

        call = pl.pallas_call(plain if after else body, name=name, grid=grid, in_specs=in_specs + [ANY] * len(after),
                              out_specs=out_specs, out_shape=out_shape, scratch_shapes=scratch_shapes,
                              input_output_aliases=aliases, compiler_params=_params(semantics, vmem_mb))
        return lambda *args: (list(call(*args, *after)), None)
    n_in, n_out, n_scr = len(in_specs), len(out_specs), len(scratch_shapes)
    r_in, r_shapes = len(rider.arrays), rider.out_shape()
    r_out = len(r_shapes)
    aliases.update({n_in + i: n_out + o for i, o in rider.alias_pairs})
    total = math.prod(grid)
    mid_step = total - 1 if rider.has_mid and rider.late else (3 * total) // 4

    def wrapped(*refs):
        bounds = [0, n_in, r_in, n_out, r_out, n_scr]
        for i in range(1, len(bounds)):
            bounds[i] += bounds[i - 1]
        a, ra, o, ro, s = (refs[bounds[i]:bounds[i + 1]] for i in range(5))
        rs = refs[bounds[5]:]
        step = pl.program_id(0)
        for k in range(1, len(grid)):
            step = step * grid[k] + pl.program_id(k)
        pl.when(step == 0)(lambda: rider.start(ra, ro, rs))
        body(*a, *o, *s)
        if rider.has_mid:
            pl.when(step == mid_step)(lambda: rider.mid(ra, ro, rs))
        pl.when(step == total - 1)(lambda: rider.finish(ra, ro, rs))

    call = pl.pallas_call(
        wrapped, name=name, grid=grid, in_specs=in_specs + [ANY] * r_in, out_specs=out_specs + [ANY] * r_out,
        out_shape=out_shape + r_shapes, scratch_shapes=scratch_shapes + rider.scratch(),
        input_output_aliases=aliases, compiler_params=_params(("arbitrary",) * len(grid), vmem_mb))

    def run(*args):
        res = call(*args, *rider.arrays)
        return list(res[:n_out]), list(res[n_out:])

    return run


def _norm_matmul(x, g, w, *, name, out_dtype, tb, bn, relu=False, save_h=False, rider=None):
    t, d = x.shape
    n = w.shape[1]

    def body(x_ref, g_ref, w_ref, o_ref, *rest):
        h_scr = rest[-1]

        @pl.when(pl.program_id(1) == 0)
        def _():
            h = _rms_fwd(x_ref[...], g_ref[...])[0].astype(BF16)
            h_scr[...] = h
            if save_h:
                rest[0][...] = h

        acc = _dot(h_scr[...], w_ref[...])
        if relu:
            acc = jnp.maximum(acc, 0.0)
        o_ref[...] = acc.astype(out_dtype)

    out_shape = [jax.ShapeDtypeStruct((t, n), out_dtype)]
    out_specs = [pl.BlockSpec((tb, bn), lambda i, j: (i, j))]
    if save_h:
        out_shape.append(jax.ShapeDtypeStruct((t, d), BF16))
        out_specs.append(pl.BlockSpec((tb, d), lambda i, j: (i, 0)))
    res, extra = _pcall(
        body, name=name, grid=(t // tb, n // bn),
        in_specs=[pl.BlockSpec((tb, d), lambda i, j: (i, 0)),
                  pl.BlockSpec((1, d), lambda i, j: (0, 0)),
                  pl.BlockSpec((d, bn), lambda i, j: (0, j))],
        out_specs=out_specs, out_shape=out_shape,
        scratch_shapes=[pltpu.VMEM((tb, d), BF16)],
        semantics=("parallel", "arbitrary"), vmem_mb=48, rider=rider,
    )(x, g, w)
    res = res if save_h else res[0]
    return res if rider is None else (res, extra)


def _proj(x, g, w, *, tb, rider=None):
    t, d = x.shape
    half = w.shape[1] // 2

    def body(x_ref, g_ref, w_ref, qkv_ref, gates_ref, h_ref, h_scr):
        j = pl.program_id(1)

        @pl.when(j == 0)
        def _():
            h = _rms_fwd(x_ref[...], g_ref[...])[0].astype(BF16)
            h_scr[...] = h
            h_ref[...] = h

        acc = _dot(h_scr[...], w_ref[...])

        @pl.when(j == 0)
        def _():
            qkv_ref[...] = acc

        @pl.when(j == 1)
        def _():
            gates_ref[...] = acc.astype(BF16)

    tok = lambda c: pl.BlockSpec((tb, c), lambda i, j: (i, 0))
    res, extra = _pcall(
        body, name="proj", grid=(t // tb, 2),
        in_specs=[tok(d), pl.BlockSpec((1, d), lambda i, j: (0, 0)), pl.BlockSpec((d, half), lambda i, j: (0, j))],
        out_specs=[tok(half), tok(half), tok(d)],
        out_shape=[jax.ShapeDtypeStruct((t, half), F32), jax.ShapeDtypeStruct((t, half), BF16),
                   jax.ShapeDtypeStruct((t, d), BF16)],
        scratch_shapes=[pltpu.VMEM((tb, d), BF16)],
        semantics=("parallel", "arbitrary"), vmem_mb=48, rider=rider,
    )(x, g, w)
    return res if rider is None else (res, extra)


def _matmul_nt_normbwd(dy, w, x, g, dres, *, name, tb, also_bf16=False, to_natural=False, after=()):
    t, d = x.shape
    stacked = dy.ndim == 3
    has_res = dres is not None
    n_i = SEG // TI
    if to_natural:
        tb = N_RES * TI

    def body(dy_ref, w_ref, x_ref, g_ref, *rest):
        rest = list(rest)
        dres_ref = rest.pop(0) if has_res else None
        dx_ref = rest.pop(0)
        dxb_ref = rest.pop(0) if also_bf16 else None
        gg_ref = rest.pop(0)
        i = pl.program_id(0)

        def rows(ref, *lead):
            v = ref[lead] if lead else ref[...]
            return v[0].reshape(tb, v.shape[-1]) if to_natural else v

        if stacked:
            kb = dy_ref.shape[-1]
            dh = _dot_nt(rows(dy_ref, 0), w_ref[:, 0:kb])
            for s in range(1, dy_ref.shape[0]):
                dh = dh + _dot_nt(rows(dy_ref, s), w_ref[:, s * kb:(s + 1) * kb])
        else:
            dh = _dot_nt(rows(dy_ref), w_ref[...])
        g_v = g_ref[...]
        _, xh, r = _rms_fwd(rows(x_ref), g_v)
        dx = _rms_bwd(dh, xh, r, g_v)
        if has_res:
            dx = dx + rows(dres_ref)
        if to_natural:
            scr = rest.pop(0)
            for cb in range(d // BLK):
                cols = slice(cb * BLK, (cb + 1) * BLK)
                slab = scr.at[cb]
                for res in range(N_RES):
                    slab[pl.ds(res, TI, stride=N_RES), :] = dx[res * TI:(res + 1) * TI, cols]
                dx_ref[:, cols] = slab[...]
        else:
            dx_ref[...] = dx
        if also_bf16:
            dxb_ref[...] = dx.astype(BF16)
        part = jnp.sum(dh * xh, axis=0, keepdims=True)

        @pl.when(i == 0)
        def _():
            gg_ref[...] = part

        @pl.when(i != 0)
        def _():
            gg_ref[...] += part

    tok = pl.BlockSpec((tb, d), lambda i: (i, 0))
    row = pl.BlockSpec((1, d), lambda i: (0, 0))
    if to_natural:
        act = pl.BlockSpec((1, N_RES, TI, d), lambda i: (i // n_i, 0, i % n_i, 0))
        dy_spec = pl.BlockSpec((dy.shape[0], 1, N_RES, TI, dy.shape[2]), lambda i: (0, i // n_i, 0, i % n_i, 0))
        dy, x = dy.reshape(dy.shape[0], t // HALF, N_RES, SEG, dy.shape[2]), _x4(x)
        dres = _x4(dres) if has_res else None
    elif stacked:
        act, dy_spec = tok, pl.BlockSpec((dy.shape[0], tb, dy.shape[2]), lambda i: (0, i, 0))
    else:
        act, dy_spec = tok, pl.BlockSpec((tb, dy.shape[1]), lambda i: (i, 0))
    in_specs = [dy_spec, pl.BlockSpec(w.shape, lambda i: (0, 0)), act, row]
    args = [dy, w, x, g]
    if has_res:
        in_specs.append(act)
        args.append(dres)
    out_specs = [tok] + ([tok] if also_bf16 else []) + [row]
    out_shape = ([jax.ShapeDtypeStruct((t, d), F32)] + ([jax.ShapeDtypeStruct((t, d), BF16)] if also_bf16 else [])
                 + [jax.ShapeDtypeStruct((1, d), F32)])
    res, _ = _pcall(
        body, name=name, grid=(t // tb,), in_specs=in_specs, out_specs=out_specs, out_shape=out_shape,
        scratch_shapes=[pltpu.VMEM((d // BLK, tb, BLK), F32)] if to_natural else [],
        semantics=("arbitrary",), vmem_mb=56, after=after,
    )(*args)
    return res


def _matmul_tn(a, b, *, name, bm, bn, square_a=False, after=()):
    t, m = a.shape
    stacked = b.ndim == 3
    n = b.shape[0] * bn if stacked else b.shape[1]

    def body(a_ref, b_ref, o_ref):
        av = a_ref[...]
        if square_a:
            av = av.astype(F32)
            av = (av * av).astype(BF16)
        o_ref[...] = _dot_tn(av, b_ref[...]).astype(BF16)

    res, _ = _pcall(
        body, name=name, grid=(m // bm, n // bn),
        in_specs=[pl.BlockSpec((t, bm), lambda i, j: (0, i)),
                  pl.BlockSpec((None, t, bn), lambda i, j: (j, 0, 0)) if stacked
                  else pl.BlockSpec((t, bn), lambda i, j: (0, j))],
        out_specs=[pl.BlockSpec((bm, bn), lambda i, j: (i, j))], out_shape=[jax.ShapeDtypeStruct((m, n), BF16)],
        semantics=("parallel", "parallel"), vmem_mb=56, after=after,
    )(a, b)
    return res[0]


N_RES = 16
SEG = 128
HALF = N_RES * SEG
TI = 32
HALO = 16


def _x4(a):
    return a.reshape(a.shape[0] // HALF, N_RES, SEG, a.shape[1])


def _reorder(arrays, name, rider=None):
    t, c = arrays[0].shape
    n = len(arrays)
    n_i = SEG // TI

    def body(*refs):
        scr = refs[-1]
        for i_ref, o_ref in zip(refs[:n], refs[n:2 * n]):
            for cb in range(c // BLK):
                cols = slice(cb * BLK, (cb + 1) * BLK)
                slab = scr.at[cb]
                slab[...] = i_ref[:, cols]
                for r in range(N_RES):
                    o_ref[0, r, :, cols] = slab[pl.ds(r, TI, stride=N_RES), :]

    res, extra = _pcall(
        body, name=name, grid=(t // (TI * N_RES),),
        in_specs=[pl.BlockSpec((TI * N_RES, c), lambda s: (s, 0))] * n,
        out_specs=[pl.BlockSpec((1, N_RES, TI, c), lambda s: (s // n_i, 0, s % n_i, 0))] * n,
        out_shape=[jax.ShapeDtypeStruct((t // HALF, N_RES, SEG, c), F32)] * n,
        scratch_shapes=[pltpu.VMEM((c // BLK, TI * N_RES, BLK), F32)],
        semantics=("parallel",), vmem_mb=32, rider=rider,
    )(*arrays)
    res = [r.reshape(t, c) for r in res]
    return res if rider is None else (res, extra)


_PATTERNS = ((1, 16, 8, SEG), (4, 4, 32, 4 * SEG), (16, 1, SEG, 0))
_FIRST = {1: 1, 4: 4, 16: 16}


def _group_rows(d, g):
    a = g >> 4
    if d == 16:
        base = a * HALF + (g & 15) * SEG
        prev = base - HALF
    elif d == 4:
        c = (g >> 2) & 3
        base = a * HALF + (g & 3) * SEG + c * 32
        prev = jnp.where(c > 0, base - 32, base - HALF + 96)
    else:
        c = g & 15
        base = a * HALF + c * 8
        prev = jnp.where(c > 0, base - 8, base - HALF + 120)
    return base, prev


def _load_rows(ref, base, n, rows, stride):
    parts = [ref[pl.ds(pl.multiple_of(base + j * stride, 8), rows), :] for j in range(n)]
    return parts[0] if n == 1 else jnp.concatenate(parts, axis=0)


def _store_rows(ref, base, val, n, rows, stride, add=False):
    for j in range(n):
        sl = pl.ds(pl.multiple_of(base + j * stride, 8), rows)
        piece = val[j * rows:(j + 1) * rows, :]
        if add:
            ref[sl, :] += piece
        else:
            ref[sl, :] = piece


def _band_bias(n, rows):
    shift = rows.bit_length() - 1
    lq = lax.broadcasted_iota(jnp.int32, (BLK, BLK), 0)
    lk = lax.broadcasted_iota(jnp.int32, (BLK, BLK), 1)
    iq = (lq & (rows - 1)) * n + (lq >> shift)
    ik = (lk & (rows - 1)) * n + (lk >> shift)
    zero = jnp.zeros((BLK, BLK), F32)
    return jnp.where(ik >= iq, zero, NEG_INF), jnp.where(ik <= iq, zero, NEG_INF)


def _set_bias(bias_scr, n, rows):
    prev_b, cur_b = _band_bias(n, rows)
    for half in range(2):
        bias_scr[half * BLK:(half + 1) * BLK, 0:BLK] = prev_b
        bias_scr[half * BLK:(half + 1) * BLK, BLK:2 * BLK] = cur_b


SCALE = 1.0 / math.sqrt(HEAD_DIM)


def _head_consts(value=1.0):
    lane_lo = lax.broadcasted_iota(jnp.int32, (BLK, BLK), 1) < HEAD_DIM
    return lane_lo, [jnp.where(lane_lo, value, 0.0).astype(BF16), jnp.where(lane_lo, 0.0, value).astype(BF16)]


def _stack_heads(v, head_mask):
    return jnp.concatenate([v * head_mask[0], v * head_mask[1]], axis=0)


def _unstack_heads(v2, lane_lo):
    return jnp.where(lane_lo, v2[:BLK], v2[BLK:])


def _rows_per_head(v, lane_lo):
    rolled = pltpu.roll(v, HEAD_DIM, axis=1)
    return jnp.concatenate([jnp.where(lane_lo, v, rolled), jnp.where(lane_lo, rolled, v)], axis=0)


WIDTH = 4


def _loop(lo, hi, fn, width=None):
    if width is None:
        def body(g, carry):
            fn(g)
            return carry

        if hi > lo:
            lax.fori_loop(lo, hi, body, 0)
        return
    while hi > lo:
        trips = (hi - lo) // width
        if trips:
            def body(i, carry, lo=lo, width=width):
                fn([lo + width * i + j for j in range(width)])
                return carry

            lax.fori_loop(0, trips, body, 0)
            lo += trips * width
        width = max(1, width // 2)


def _mix_weights(l1, l2, l3):
    mx = jnp.maximum(jnp.maximum(l1, l2), l3)
    e1, e2, e3 = jnp.exp(l1 - mx), jnp.exp(l2 - mx), jnp.exp(l3 - mx)
    inv = 1.0 / (e1 + e2 + e3)
    return e1 * inv, e2 * inv, e3 * inv


def _attention_fwd(qkv, rider=None):
    t = qkv.shape[0]
    groups = 16 * (t // HALF)

    def body(q_ref, k_ref, v_ref, attn_ref, l1_ref, l2_ref, l3_ref, o_scr, bias_scr):
        lane_lo, q_mask = _head_consts(SCALE)
        l_refs = (l1_ref, l2_ref, l3_ref)
        for p, (d, n, rows, stride) in enumerate(_PATTERNS):
            _set_bias(bias_scr, n, rows)
            o_p, l_p = o_scr.at[p], l_refs[p]

            def block(gs, has_prev):
                at = [_group_rows(d, g) for g in gs]

                def load(ref, b):
                    return _load_rows(ref, b, n, rows, stride).astype(BF16)

                q2 = [_stack_heads(load(q_ref, b), q_mask) for b, _ in at]
                k2 = [load(k_ref, b) for b, _ in at]
                v2 = [load(v_ref, b) for b, _ in at]
                if has_prev:
                    k2 = [jnp.concatenate([load(k_ref, pv), k], axis=0) for (_, pv), k in zip(at, k2)]
                    v2 = [jnp.concatenate([load(v_ref, pv), v], axis=0) for (_, pv), v in zip(at, v2)]
                s = [_dot_nt(q, k) for q, k in zip(q2, k2)]
                s = [x + (bias_scr[...] if has_prev else bias_scr[:, BLK:2 * BLK]) for x in s]
                mx = [jnp.max(x, axis=1, keepdims=True) for x in s]
                e = [jnp.exp(x - m) for x, m in zip(s, mx)]
                den = [jnp.sum(x, axis=1, keepdims=True) for x in e]
                o2 = [_dot(x.astype(BF16), v) * (1.0 / dn) for x, v, dn in zip(e, v2, den)]
                lse2 = [jnp.broadcast_to(m + jnp.log(dn), (2 * BLK, BLK)) for m, dn in zip(mx, den)]
                for (b, _), o, l in zip(at, o2, lse2):
                    _store_rows(o_p, b, _unstack_heads(o, lane_lo), n, rows, stride)
                    _store_rows(l_p, b, _unstack_heads(l, lane_lo), n, rows, stride)

            _loop(0, _FIRST[d], lambda gs: block(gs, False), width=2 * WIDTH)
            _loop(_FIRST[d], groups, lambda gs: block(gs, True), width=2 * WIDTH)

        def mix(i):
            sl = pl.ds(pl.multiple_of(i * 256, 256), 256)
            w = _mix_weights(l1_ref[sl, :], l2_ref[sl, :], l3_ref[sl, :])
            attn_ref[sl, :] = w[0] * o_scr[0, sl, :] + w[1] * o_scr[1, sl, :] + w[2] * o_scr[2, sl, :]

        _loop(0, t // 256, mix)

    def col(c0):
        return pl.BlockSpec((t, BLK), lambda hp: (0, c0 + hp))

    res, extra = _pcall(
        body, name="attention_fwd", grid=(4,), in_specs=[col(0), col(4), col(8)], out_specs=[col(0)] * 4,
        out_shape=[jax.ShapeDtypeStruct((t, 512), F32)] * 4,
        scratch_shapes=[pltpu.VMEM((3, t, BLK), F32), pltpu.VMEM((2 * BLK, 2 * BLK), F32)],
        semantics=("parallel",), vmem_mb=48, rider=rider,
    )(qkv, qkv, qkv)
    return res if rider is None else (res, extra)


def _attention_bwd(qkv, dattn, dsum, lses, dproj):
    t = qkv.shape[0]
    groups = 16 * (t // HALF)

    def body(q_ref, k_ref, v_ref, da_ref, ds_ref, l1_ref, l2_ref, l3_ref, kept_ref, out_ref, acc, bias_scr):
        del kept_ref
        lane_lo, head_mask = _head_consts()
        q_mask = _head_consts(SCALE)[1]
        l_refs = (l1_ref, l2_ref, l3_ref)

        def clear(i):
            sl = pl.ds(pl.multiple_of(i * 512, 512), 512)
            for s in range(3):
                acc[s, sl, :] = jnp.zeros((512, BLK), F32)

        _loop(0, t // 512, clear)
        dq_acc, dk_acc, dv_acc = acc.at[0], acc.at[1], acc.at[2]
        for p, (d, n, rows, stride) in enumerate(_PATTERNS):
            _set_bias(bias_scr, n, rows)

            def block(gs, has_prev):
                at = [_group_rows(d, g) for g in gs]

                def load(ref, b):
                    return _load_rows(ref, b, n, rows, stride)

                def put(ref, b, val):
                    _store_rows(ref, b, val, n, rows, stride, add=True)

                def wide(x):
                    return jnp.concatenate([x, x], axis=1) if has_prev else x

                lse = [[load(ref, b) for ref in l_refs] for b, _ in at]
                w = [_mix_weights(*ls)[p] for ls in lse]
                do2 = [_stack_heads((wg * load(da_ref, b)).astype(BF16), head_mask) for wg, (b, _) in zip(w, at)]
                dl2 = [wide(_rows_per_head(wg * load(ds_ref, b), lane_lo)) for wg, (b, _) in zip(w, at)]
                lse2 = [wide(_rows_per_head(ls[p], lane_lo)) for ls in lse]
                q2 = [_stack_heads(load(q_ref, b).astype(BF16), q_mask) for b, _ in at]
                k2 = [load(k_ref, b).astype(BF16) for b, _ in at]
                v2 = [load(v_ref, b).astype(BF16) for b, _ in at]
                if has_prev:
                    k2 = [jnp.concatenate([load(k_ref, pv).astype(BF16), k], axis=0) for (_, pv), k in zip(at, k2)]
                    v2 = [jnp.concatenate([load(v_ref, pv).astype(BF16), v], axis=0) for (_, pv), v in zip(at, v2)]
                s = [_dot_nt(q, k) for q, k in zip(q2, k2)]
                dp = [_dot_nt(do, v) for do, v in zip(do2, v2)]
                pr = [jnp.exp(x + (bias_scr[...] if has_prev else bias_scr[:, BLK:2 * BLK]) - l)
                      for x, l in zip(s, lse2)]
                ds = [(pg * (x - dl)).astype(BF16) for pg, x, dl in zip(pr, dp, dl2)]
                dq2 = [_dot(x, k) * SCALE for x, k in zip(ds, k2)]
                dk2 = [_dot_tn(x, q) for x, q in zip(ds, q2)]
                dv2 = [_dot_tn(pg.astype(BF16), do) for pg, do in zip(pr, do2)]
                for (b, pv), dq, dk, dv in zip(at, dq2, dk2, dv2):
                    put(dq_acc, b, _unstack_heads(dq, lane_lo))
                    if has_prev:
                        put(dk_acc, pv, dk[:BLK])
                        put(dv_acc, pv, dv[:BLK])
                        put(dk_acc, b, dk[BLK:])
                        put(dv_acc, b, dv[BLK:])
                    else:
                        put(dk_acc, b, dk)
                        put(dv_acc, b, dv)

            _loop(0, _FIRST[d], lambda gs: block(gs, False), width=WIDTH)
            _loop(_FIRST[d], groups, lambda gs: block(gs, True), width=WIDTH)

        def emit(i):
            sl = pl.ds(pl.multiple_of(i * 512, 512), 512)
            for s in range(3):
                out_ref[s, sl, :] = acc[s, sl, :].astype(BF16)

        _loop(0, t // 512, emit)

    def col(c0):
        return pl.BlockSpec((t, BLK), lambda hp: (0, c0 + hp))

    res, _ = _pcall(
        body, name="attention_bwd", grid=(4,),
        in_specs=[col(0), col(4), col(8)] + [col(0)] * 5 + [ANY],
        out_specs=[pl.BlockSpec((3, t, BLK), lambda hp: (0, 0, hp))],
        out_shape=[jax.ShapeDtypeStruct(dproj.shape, BF16)],
        scratch_shapes=[pltpu.VMEM((3, t, BLK), F32), pltpu.VMEM((2 * BLK, 2 * BLK), F32)],
        semantics=("parallel",), vmem_mb=56, aliases={8: 0},
    )(qkv, qkv, qkv, dattn, dsum, *lses, dproj)
    return res[0]


def _order_specs(t):
    n_i = SEG // TI
    nblk = (t // HALF) * n_i
    per = TI // HALO

    def main(c, col=0):
        return pl.BlockSpec((1, N_RES, TI, c), lambda s: (s // n_i, 0, s % n_i, col))

    def before(c, col=0):
        return pl.BlockSpec((1, 2, HALO, c), lambda s: (jnp.maximum(s - 1, 0) // n_i, N_RES // 2 - 1,
                                                        (jnp.maximum(s - 1, 0) % n_i) * per + per - 1, col))

    def after(c, col=0):
        return pl.BlockSpec((1, 2, HALO, c), lambda s: (jnp.minimum(s + 1, nblk - 1) // n_i, 0,
                                                        (jnp.minimum(s + 1, nblk - 1) % n_i) * per, col))

    return nblk, main, before, after


def _shift_in(v, row_in, up):
    rows = v.shape[0]
    idx = lax.broadcasted_iota(jnp.int32, v.shape, 0)
    fill = jnp.broadcast_to(row_in, v.shape)
    if up:
        return jnp.where(idx == rows - 1, fill, pltpu.roll(v, rows - 1, axis=0))
    return jnp.where(idx == 0, fill, pltpu.roll(v, 1, axis=0))


def _taps_behind(u, before):
    s15 = _shift_in(u[N_RES - 1], before[1, HALO - 1:HALO, :], up=False)
    s14 = _shift_in(u[N_RES - 2], before[0, HALO - 1:HALO, :], up=False)
    m1 = jnp.concatenate([s15[None], u[:N_RES - 1]], axis=0)
    m2 = jnp.concatenate([s14[None], s15[None], u[:N_RES - 2]], axis=0)
    return m1, m2


def _taps_ahead(u, after):
    t0 = _shift_in(u[0], after[0, 0:1, :], up=True)
    t1 = _shift_in(u[1], after[1, 0:1, :], up=True)
    p1 = jnp.concatenate([u[1:], t0[None]], axis=0)
    p2 = jnp.concatenate([u[2:], t0[None], t1[None]], axis=0)
    return p1, p2


def _conv_fwd(gates, before, first, cw):
    gates, before = gates.astype(F32), before.astype(F32)
    bg, cg, xc = gates[..., 0:512], gates[..., 512:1024], gates[..., 1024:1536]
    u = cg * xc
    ub = before[..., 512:1024] * before[..., 1024:1536]
    ub = jnp.where(first, jnp.zeros_like(ub), ub)
    m1, m2 = _taps_behind(u, ub)
    conv = m2 * cw[0:1, :] + m1 * cw[1:2, :] + u * cw[2:3, :]
    return bg, u, m1, m2, conv


def _sum_tokens(v):
    return jnp.sum(jnp.sum(v, axis=0), axis=0, keepdims=True)


def _mixer_fwd(x, attn, gates, cw, g_a, g_c, w_out):
    t, d = x.shape
    nblk, main, before, _ = _order_specs(t)
    rows = N_RES * TI

    def body(x_ref, at_ref, gt_ref, gb_ref, cw_ref, ga_ref, gc_ref, wa_ref, wb_ref, x1_ref, mg_ref):
        an = _rms_fwd(at_ref[0], ga_ref[...])[0].astype(BF16)
        bg, _, _, _, conv = _conv_fwd(gt_ref[0], gb_ref[0], pl.program_id(0) == 0, cw_ref[...])
        cn = _rms_fwd(bg * conv, gc_ref[...])[0].astype(BF16)
        mg_ref[0, :, :, 0:512] = an
        mg_ref[0, :, :, 512:1024] = cn
        y = _dot(an.reshape(rows, 512), wa_ref[...]) + _dot(cn.reshape(rows, 512), wb_ref[...])
        x1_ref[0] = x_ref[0] + y.reshape(N_RES, TI, d)

    const = lambda r, c, i0=0: pl.BlockSpec((r, c), lambda s: (i0, 0))
    x1, merged = pl.pallas_call(
        body, name="mixer_fwd", grid=(nblk,),
        in_specs=[main(d), main(512), main(1536), before(1536), const(3, 512), const(1, 512), const(1, 512),
                  const(512, d), const(512, d, 1)],
        out_specs=[main(d), main(d)],
        out_shape=[jax.ShapeDtypeStruct(_x4(x).shape, F32), jax.ShapeDtypeStruct(_x4(x).shape, BF16)],
        compiler_params=_params(("parallel",), 48),
    )(_x4(x), _x4(attn), _x4(gates), _x4(gates), cw, g_a, g_c, w_out, w_out)
    return x1.reshape(t, d), merged.reshape(t, d)


def _mixer_bwd(dx1, merged, attn, gates, cw, g_a, g_c, w_out, head_sum, after=()):
    t, d = dx1.shape
    nblk, main, before, _ = _order_specs(t)
    rows = N_RES * TI

    def body(dx_ref, mg_ref, at_ref, gt_ref, gb_ref, cw_ref, ga_ref, gc_ref, wa_ref, wb_ref, hs_ref,
             da_ref, dsum_ref, dy_ref, gga_ref, ggc_ref, gw_ref, acc_w):
        s = pl.program_id(0)
        dxb = dx_ref[0].reshape(rows, d).astype(BF16)

        @pl.when(s == 0)
        def _():
            acc_w[...] = jnp.zeros_like(acc_w)

        acc_w[...] += _dot_tn(mg_ref[0].reshape(rows, d), dxb)

        @pl.when(s == nblk - 1)
        def _():
            gw_ref[...] = acc_w[...].astype(BF16)

        dma = _dot_nt(dxb, wa_ref[...]).reshape(N_RES, TI, 512)
        dmc = _dot_nt(dxb, wb_ref[...]).reshape(N_RES, TI, 512)
        attn_v, g_av = at_ref[0], ga_ref[...]
        _, ah, ra = _rms_fwd(attn_v, g_av)
        dattn = _rms_bwd(dma, ah, ra, g_av)
        da_ref[0] = dattn
        z = (dattn * attn_v).reshape(rows, 512)
        hs = hs_ref[...]
        z1 = z.astype(BF16)
        z2 = (z - z1.astype(F32)).astype(BF16)
        dsum_ref[0] = (_dot(z1, hs) + _dot(z2, hs)).reshape(N_RES, TI, 512)
        bg, _, _, _, conv = _conv_fwd(gt_ref[0], gb_ref[0], s == 0, cw_ref[...])
        g_cv = gc_ref[...]
        _, yh, rc = _rms_fwd(bg * conv, g_cv)
        dy_ref[0] = _rms_bwd(dmc, yh, rc, g_cv)
        pa, pc = _sum_tokens(dma * ah), _sum_tokens(dmc * yh)

        @pl.when(s == 0)
        def _():
            gga_ref[...] = pa
            ggc_ref[...] = pc

        @pl.when(s != 0)
        def _():
            gga_ref[...] += pa
            ggc_ref[...] += pc

    const = lambda r, c, i0=0: pl.BlockSpec((r, c), lambda s: (i0, 0))
    shape4 = _x4(attn).shape
    res, _ = _pcall(
        body, name="mixer_bwd", grid=(nblk,),
        in_specs=[main(d), main(d), main(512), main(1536), before(1536), const(3, 512), const(1, 512), const(1, 512),
                  const(512, d), const(512, d, 1), const(512, 512)],
        out_specs=[main(512)] * 3 + [const(1, 512), const(1, 512), const(d, d)],
        out_shape=[jax.ShapeDtypeStruct(shape4, F32)] * 3 + [jax.ShapeDtypeStruct((1, 512), F32)] * 2
        + [jax.ShapeDtypeStruct((d, d), BF16)],
        scratch_shapes=[pltpu.VMEM((d, d), F32)],
        semantics=("arbitrary",), vmem_mb=48, after=after,
    )(_x4(dx1), _x4(merged), _x4(attn), _x4(gates), _x4(gates), cw, g_a, g_c, w_out, w_out, head_sum)
    return [r.reshape(t, 512) for r in res[:3]] + res[3:]


def _conv_bwd(dy, gates, cw, after=()):
    t = dy.shape[0]
    nblk, main, before, ahead = _order_specs(t)
    n_i = SEG // TI

    def body(dy_ref, dya_ref, gt_ref, gb_ref, ga_ref, cw_ref, dp_ref, gcw_ref):
        s = pl.program_id(0)
        cw_v, gates_v = cw_ref[...], gt_ref[0]
        bg, u, m1, m2, conv = _conv_fwd(gates_v, gb_ref[0], s == 0, cw_v)
        dy_v = dy_ref[0]
        dconv = dy_v * bg
        dca = dya_ref[0] * ga_ref[0][..., 0:512].astype(F32)
        dca = jnp.where(s == nblk - 1, jnp.zeros_like(dca), dca)
        p1, p2 = _taps_ahead(dconv, dca)
        du = dconv * cw_v[2:3, :] + p1 * cw_v[1:2, :] + p2 * cw_v[0:1, :]
        dp_ref[0, 0] = (dy_v * conv).astype(BF16)
        dp_ref[1, 0] = (du * gates_v[..., 1024:1536].astype(F32)).astype(BF16)
        dp_ref[2, 0] = (du * gates_v[..., 512:1024].astype(F32)).astype(BF16)
        parts = [_sum_tokens(dconv * m2), _sum_tokens(dconv * m1), _sum_tokens(dconv * u)]

        @pl.when(s == 0)
        def _():
            gcw_ref[...] = jnp.zeros_like(gcw_ref)

        for tap in range(3):
            gcw_ref[tap:tap + 1, :] += parts[tap]

    (dproj, gcw), _ = _pcall(
        body, name="conv_bwd", grid=(nblk,),
        in_specs=[main(512), ahead(512), main(1536), before(1536), ahead(1536),
                  pl.BlockSpec((3, 512), lambda s: (0, 0))],
        out_specs=[pl.BlockSpec((3, 1, N_RES, TI, 512), lambda s: (1, s // n_i, 0, s % n_i, 0)),
                   pl.BlockSpec((8, 512), lambda s: (0, 0))],
        out_shape=[jax.ShapeDtypeStruct((6, t // HALF, N_RES, SEG, 512), BF16), jax.ShapeDtypeStruct((8, 512), F32)],
        semantics=("arbitrary",), vmem_mb=40, after=after,
    )(_x4(dy), _x4(dy), _x4(gates), _x4(gates), _x4(gates), cw)
    return dproj.reshape(6, t, 512), gcw


def _xattn_fwd(x1, g, w_q, kv, w_o, *, tb):
    t, d = x1.shape
    hd = d // N_MEM_HEADS
    m = kv.shape[0]

    def body(x_ref, g_ref, wq_ref, k_ref, v_ref, wo_ref, x2_ref, h_ref, q_ref, o_ref):
        xv = x_ref[...]
        h = _rms_fwd(xv, g_ref[...])[0].astype(BF16)
        h_ref[...] = h
        q = _dot(h, wq_ref[...]).astype(BF16)
        q_ref[...] = q
        for hh in range(N_MEM_HEADS):
            sl = slice(hh * hd, (hh + 1) * hd)
            s = _dot_nt(q[:, sl], k_ref[:, sl]) * (1.0 / 16.0)
            e = jnp.exp(s - jnp.max(s, axis=1, keepdims=True))
            p = e / jnp.sum(e, axis=1, keepdims=True)
            o_ref[:, sl] = _dot(p.astype(BF16), v_ref[:, sl]).astype(BF16)
        x2_ref[...] = xv + _dot(o_ref[...], wo_ref[...])

    tok = pl.BlockSpec((tb, d), lambda i: (i, 0))
    full = pl.BlockSpec((d, d), lambda i: (0, 0))
    return pl.pallas_call(
        body, name="xattn_fwd", grid=(t // tb,),
        in_specs=[tok, pl.BlockSpec((1, d), lambda i: (0, 0)), full,
                  pl.BlockSpec((m, d), lambda i: (0, 0)), pl.BlockSpec((m, d), lambda i: (0, 1)), full],
        out_specs=[tok] * 4,
        out_shape=[jax.ShapeDtypeStruct((t, d), F32)] + [jax.ShapeDtypeStruct((t, d), BF16)] * 3,
        compiler_params=_params(("parallel",), 48),
    )(x1, g, w_q, kv, kv, w_o)


def _xattn_bwd(dx2, x1, g, q, w_q, kv, w_o, *, tb, after=()):
    t, d = x1.shape
    hd = d // N_MEM_HEADS
    m = kv.shape[0]

    def body(dx2_ref, x_ref, g_ref, q_ref, wq_ref, k_ref, v_ref, wo_ref,
             dx1_ref, dq_ref, dk_ref, dv_ref, gg_ref):
        i = pl.program_id(0)

        @pl.when(i == 0)
        def _():
            dk_ref[...] = jnp.zeros_like(dk_ref)
            dv_ref[...] = jnp.zeros_like(dv_ref)

        dx2 = dx2_ref[...]
        do = _dot_nt(dx2.astype(BF16), wo_ref[...]).astype(BF16)
        for hh in range(N_MEM_HEADS):
            sl = slice(hh * hd, (hh + 1) * hd)
            qh, kh, vh, doh = q_ref[:, sl], k_ref[:, sl], v_ref[:, sl], do[:, sl]
            s = _dot_nt(qh, kh) * (1.0 / 16.0)
            e = jnp.exp(s - jnp.max(s, axis=1, keepdims=True))
            p = e / jnp.sum(e, axis=1, keepdims=True)
            dp = _dot_nt(doh, vh)
            ds = (p * (dp - jnp.sum(dp * p, axis=1, keepdims=True)) * (1.0 / 16.0)).astype(BF16)
            dq_ref[:, sl] = _dot(ds, kh).astype(BF16)
            dk_ref[:, sl] += _dot_tn(ds, qh)
            dv_ref[:, sl] += _dot_tn(p.astype(BF16), doh)
        dh = _dot_nt(dq_ref[...], wq_ref[...])
        g_v = g_ref[...]
        _, xh, r = _rms_fwd(x_ref[...], g_v)
        dx1 = dx2 + _rms_bwd(dh, xh, r, g_v)
        dx1_ref[...] = dx1
        part = jnp.sum(dh * xh, axis=0, keepdims=True)

        @pl.when(i == 0)
        def _():
            gg_ref[...] = part

        @pl.when(i != 0)
        def _():
            gg_ref[...] += part

    tok = pl.BlockSpec((tb, d), lambda i: (i, 0))
    full = pl.BlockSpec((d, d), lambda i: (0, 0))
    acc = pl.BlockSpec((m, d), lambda i: (0, 0))
    res, _ = _pcall(
        body, name="xattn_bwd", grid=(t // tb,),
        in_specs=[tok, tok, pl.BlockSpec((1, d), lambda i: (0, 0)), tok, full,
                  pl.BlockSpec((m, d), lambda i: (0, 0)), pl.BlockSpec((m, d), lambda i: (0, 1)), full],
        out_specs=[tok, tok, acc, acc, pl.BlockSpec((1, d), lambda i: (0, 0))],
        out_shape=[jax.ShapeDtypeStruct((t, d), F32), jax.ShapeDtypeStruct((t, d), BF16),
                   jax.ShapeDtypeStruct((m, d), F32), jax.ShapeDtypeStruct((m, d), F32),
                   jax.ShapeDtypeStruct((1, d), F32)],
        semantics=("arbitrary",), vmem_mb=48, after=after,
    )(dx2, x1, g, q, w_q, kv, kv, w_o)
    return res


def _mlp_down_loss(a, w_down, x2, tgt, g, *, tb):
    t, d = x2.shape
    f = a.shape[1]

    def body(a_ref, w_ref, x_ref, t_ref, g_ref, dx_ref, dxb_ref, loss_ref, gg_ref):
        i = pl.program_id(0)
        av = a_ref[...].astype(F32)
        x3 = x_ref[...] + _dot((av * av).astype(BF16), w_ref[...])
        g_v = g_ref[...]
        out, xh, r = _rms_fwd(x3, g_v)
        err = out - t_ref[...]
        dout = err * (1.0 / d)
        dx = _rms_bwd(dout, xh, r, g_v)
        dx_ref[...] = dx
        dxb_ref[...] = dx.astype(BF16)
        part = jnp.sum(dout * xh, axis=0, keepdims=True)
        lpart = 0.5 * jnp.sum(jnp.mean(err * err, axis=-1, keepdims=True), axis=0, keepdims=True)
        lpart = jnp.broadcast_to(lpart, loss_ref.shape)

        @pl.when(i == 0)
        def _():
            gg_ref[...] = part
            loss_ref[...] = lpart

        @pl.when(i != 0)
        def _():
            gg_ref[...] += part
            loss_ref[...] += lpart

    tok = pl.BlockSpec((tb, d), lambda i: (i, 0))
    return pl.pallas_call(
        body, name="mlp_down_loss", grid=(t // tb,),
        in_specs=[pl.BlockSpec((tb, f), lambda i: (i, 0)), pl.BlockSpec((f, d), lambda i: (0, 0)), tok, tok,
                  pl.BlockSpec((1, d), lambda i: (0, 0))],
        out_specs=[tok, tok, pl.BlockSpec((8, 128), lambda i: (0, 0)), pl.BlockSpec((1, d), lambda i: (0, 0))],
        out_shape=[jax.ShapeDtypeStruct((t, d), F32), jax.ShapeDtypeStruct((t, d), BF16),
                   jax.ShapeDtypeStruct((8, 128), F32), jax.ShapeDtypeStruct((1, d), F32)],
        compiler_params=_params(("arbitrary",), 56),
    )(a, w_down, x2, tgt, g)


def _mlp_dpre(dx3, w_down, a, *, tb, bn):
    t, d = dx3.shape
    f = a.shape[1]

    def body(dx_ref, w_ref, a_ref, o_ref):
        o_ref[...] = (2.0 * a_ref[...].astype(F32) * _dot_nt(dx_ref[...], w_ref[...])).astype(BF16)

    return pl.pallas_call(
        body, name="mlp_dpre", grid=(t // tb, f // bn),
        in_specs=[pl.BlockSpec((tb, d), lambda i, j: (i, 0)), pl.BlockSpec((bn, d), lambda i, j: (j, 0)),
                  pl.BlockSpec((tb, bn), lambda i, j: (i, j))],
        out_specs=pl.BlockSpec((tb, bn), lambda i, j: (i, j)),
        out_shape=jax.ShapeDtypeStruct((t, f), BF16),
        compiler_params=_params(("parallel", "arbitrary"), 48),
    )(dx3, w_down, a)


def _adamw(gsum, w, m, v):
    m_new = ADAM_B1 * m + (1.0 - ADAM_B1) * gsum
    v_new = ADAM_B2 * v + (1.0 - ADAM_B2) * (gsum * gsum)
    m_hat = m_new / (1.0 - ADAM_B1 ** ADAM_STEP)
    v_hat = v_new / (1.0 - ADAM_B2 ** ADAM_STEP)
    delta = -ADAM_LR * (m_hat / (jnp.sqrt(v_hat) + ADAM_EPS) + ADAM_WD * w)
    return delta, m_new, v_new


def _sum_adamw(parts, w, m, v, *, name, tr):
    r, c = w.shape

    def body(p_ref, w_ref, m_ref, v_ref, g_ref, d_ref, mo_ref, vo_ref):
        g = p_ref[0].astype(F32)
        for k in range(1, N_DEV):
            g = g + p_ref[k].astype(F32)
        g_ref[...] = g
        d_ref[...], mo_ref[...], vo_ref[...] = _adamw(g, w_ref[...], m_ref[...], v_ref[...])

    blk = pl.BlockSpec((tr, c), lambda i: (i, 0))
    return pl.pallas_call(
        body, name=name, grid=(r // tr,),
        in_specs=[pl.BlockSpec((N_DEV, tr, c), lambda i: (0, i, 0)), blk, blk, blk],
        out_specs=[blk] * 4, out_shape=[jax.ShapeDtypeStruct((r, c), F32)] * 4,
        compiler_params=_params(("parallel",), 40),
    )(*[pltpu.with_memory_space_constraint(a, pltpu.HBM) for a in (parts, w, m, v)])


SMALL_PART = 8
_GAIN_ROWS = ("g_mix", "g_xattn", "g_mem", "g_mlp", "g_final")
_SMALL = _GAIN_ROWS + ("g_attn_out", "g_conv_out", "conv_w")
CONV_SHARD = 512 // N_DEV


def _update_small(parts, me, w, m, v):
    n = len(_SMALL)

    def body(me_ref, p_ref, *refs):
        ins, loss_ref, outs = refs[:3 * n], refs[3 * n], refs[3 * n + 1:]

        def total(i):
            lo = SMALL_PART * i
            s = p_ref[0, lo:lo + SMALL_PART, :]
            for k in range(1, N_DEV):
                s = s + p_ref[k, lo:lo + SMALL_PART, :]
            return s

        grads = {k: total(i)[0:1] for i, k in enumerate(_GAIN_ROWS)}
        both = total(5)[0:1]
        grads["g_attn_out"], grads["g_conv_out"] = both[:, 0:512], both[:, 512:1024]
        taps = total(6)
        mine = jnp.zeros((SMALL_PART, BLK), F32)
        for j in range(N_DEV):
            lo = j * CONV_SHARD // BLK * BLK
            blk = taps[:, lo:lo + BLK]
            if j * CONV_SHARD != lo:
                blk = pltpu.roll(blk, BLK - (j * CONV_SHARD - lo), axis=1)
            mine = jnp.where(me_ref[0] == j, blk, mine)
        grads["conv_w"] = mine[0:3, 0:CONV_SHARD]
        loss_ref[...] = total(7)[0:1, 0:1]
        for i, k in enumerate(_SMALL):
            g_ref, d_ref, mo_ref, vo_ref = outs[4 * i:4 * i + 4]
            g_ref[...] = grads[k]
            d_ref[...], mo_ref[...], vo_ref[...] = _adamw(grads[k], ins[i][...], ins[n + i][...], ins[2 * n + i][...])

    vmem = pl.BlockSpec(memory_space=pltpu.VMEM)
    args = [d[k] for d in (w, m, v) for k in _SMALL]
    res = pl.pallas_call(
        body, name="update_small",
        in_specs=[pl.BlockSpec(memory_space=pltpu.SMEM)] + [vmem] * (1 + 3 * n),
        out_shape=[jax.ShapeDtypeStruct((1, 1), F32)] + [jax.ShapeDtypeStruct(w[k].shape, F32) for k in _SMALL
                                                         for _ in range(4)],
    )(me, parts, *args)
    return res[0], {k: res[1 + 4 * i:5 + 4 * i] for i, k in enumerate(_SMALL)}


def _head_sum_matrix():
    r = lax.broadcasted_iota(jnp.int32, (512, 512), 0) // HEAD_DIM
    c = lax.broadcasted_iota(jnp.int32, (512, 512), 1) // HEAD_DIM
    return (r == c).astype(BF16)


_SHARD_AXIS = dict(w_in=1, w_out=0, w_q=0, w_kv=1, w_o=0, w_up=1, w_down=0, conv_w=None, small=None)


class _Weights:
    def __init__(self, full, shards=None):
        self.full = dict(full)
        self.shards = shards

    def rider(self, names, late=False):
        if self.shards is None:
            return None
        return _Gather([self.shards[n] for n in names], [_SHARD_AXIS[n] for n in names], late)

    def arrived(self, names, gathered):
        if gathered is not None:
            for n, g in zip(names, gathered):
                self.full[n] = g.transpose(1, 0, 2).reshape(g.shape[1], -1) if n == "conv_w" else g

    def __getitem__(self, name):
        return self.full[name]


class _Grads:
    def __init__(self, distributed):
        self.distributed = distributed
        self.local = {}
        self.pending = {}

    def add(self, name, g):
        self.local[name] = g

    def send(self, *names):
        if not self.distributed:
            return []
        rider = _Exchange([self.local[n] for n in names], [_SHARD_AXIS[n] for n in names])
        started = _exchange_start(rider, "send_" + "_".join(names))
        self.pending[names[0]] = (names, rider, started)
        return [started[3]]

    def wait(self, first_name, after):
        names, rider, started = self.pending.pop(first_name)
        return _exchange_wait(rider, started, after, "wait_" + "_".join(names))


def _ride(fn, *args, rider=None, **kw):
    if rider is None:
        return fn(*args, **kw), None
    return fn(*args, rider=rider, **kw)


def _local_step(x, mem, tgt, gains, weights, grads):
    names = ["w_in", "conv_w"]
    (x, tgt), got = _ride(_reorder, [x, tgt], "reorder_in", rider=weights.rider(names, late=True))
    weights.arrived(names, got)
    w_in, cw = weights["w_in"], weights["conv_w"]

    names = ["w_out", "w_kv"]
    (qkv, gates, h1), got = _ride(_proj, x, gains["g_mix"], w_in, tb=1024, rider=weights.rider(names))
    weights.arrived(names, got)
    names = ["w_q", "w_o", "w_up"]
    (attn, *lses), got = _ride(_attention_fwd, qkv, rider=weights.rider(names))
    weights.arrived(names, got)
    x1, merged = _mixer_fwd(x, attn, gates, cw, gains["g_attn_out"], gains["g_conv_out"], weights["w_out"])
    kv, mem_n = _norm_matmul(mem, gains["g_mem"], weights["w_kv"], name="mem_kv", out_dtype=BF16, tb=mem.shape[0],
                             bn=1024, save_h=True)
    x2, h2, qm, om = _xattn_fwd(x1, gains["g_xattn"], weights["w_q"], kv, weights["w_o"], tb=512)
    w_up = weights["w_up"]
    (a, h3), got = _ride(_norm_matmul, x2, gains["g_mlp"], w_up, name="mlp_up", out_dtype=BF16, tb=1024, bn=2048,
                         relu=True, save_h=True, rider=weights.rider(["w_down"], late=True))
    weights.arrived(["w_down"], got)
    w_down = weights["w_down"]
    dx3, dx3b, loss_blk, gg_final = _mlp_down_loss(a, w_down, x2, tgt, gains["g_final"], tb=512)

    dpre = _mlp_dpre(dx3b, w_down, a, tb=1024, bn=2048)
    grads.add("w_down", _matmul_tn(a, dx3b, name="grad_w_down", bm=512, bn=1024, square_a=True))
    sent = grads.send("w_down")
    grads.add("w_up", _matmul_tn(h3, dpre, name="grad_w_up", bm=1024, bn=1024, after=sent))
    sent = grads.send("w_up")
    dx2, dx2b, gg_mlp = _matmul_nt_normbwd(dpre, w_up, x2, gains["g_mlp"], dx3, name="mlp_dx", tb=512,
                                           also_bf16=True, after=sent)

    grads.add("w_o", _matmul_tn(om, dx2b, name="grad_w_o", bm=512, bn=512))
    dx1, dqm, dk, dv, gg_xattn = _xattn_bwd(dx2, x1, gains["g_xattn"], qm, weights["w_q"], kv, weights["w_o"], tb=512)
    grads.add("w_q", _matmul_tn(h2, dqm, name="grad_w_q", bm=1024, bn=512))
    dkv = jnp.concatenate([dk, dv], axis=1).astype(BF16)
    grads.add("w_kv", _matmul_tn(mem_n, dkv, name="grad_w_kv", bm=1024, bn=1024))
    _, gg_mem = _matmul_nt_normbwd(dkv, weights["w_kv"], mem, gains["g_mem"], None, name="mem_dx", tb=mem.shape[0])

    dattn, dsum, dy, gg_attn, gg_conv, gw_out = _mixer_bwd(dx1, merged, attn, gates, cw, gains["g_attn_out"],
                                                           gains["g_conv_out"], weights["w_out"], _head_sum_matrix())
    grads.add("w_out", gw_out)
    sent = grads.send("w_o", "w_q", "w_kv", "w_out")
    dproj, gcw = _conv_bwd(dy, gates, cw, after=sent)
    dproj = _attention_bwd(qkv, dattn, dsum, lses, dproj)
    grads.add("w_in", _matmul_tn(h1, dproj, name="grad_w_in", bm=1024, bn=512))
    sent = grads.send("w_in")
    grad_x, gg_mix = _matmul_nt_normbwd(dproj, w_in, x, gains["g_mix"], dx1, name="mixer_dx", tb=512,
                                        to_natural=True, after=sent)

    def part(v):
        return jnp.pad(v, ((0, SMALL_PART - v.shape[0]), (0, 1024 - v.shape[1])))

    parts = [gg_mix, gg_xattn, gg_mem, gg_mlp, gg_final, jnp.concatenate([gg_attn, gg_conv], axis=1), gcw, loss_blk]
    grads.add("small", jnp.concatenate([part(v) for v in parts], axis=0))
    return grad_x


_BIG = ("w_in", "w_out", "w_q", "w_kv", "w_o", "w_up", "w_down")


def kernel(x, mem, g_mix, w_in, conv_w, g_attn_out, g_conv_out, w_out, g_xattn, g_mem, w_q_mem, w_kv_mem, w_o_mem, g_mlp, w_up, w_down, g_final, loss_target, m_g_mix, m_w_in, m_conv_w, m_g_attn_out, m_g_conv_out, m_w_out, m_g_xattn, m_g_mem, m_w_q_mem, m_w_kv_mem, m_w_o_mem, m_g_mlp, m_w_up, m_w_down, m_g_final, v_g_mix, v_w_in, v_conv_w, v_g_attn_out, v_g_conv_out, v_w_out, v_g_xattn, v_g_mem, v_w_q_mem, v_w_kv_mem, v_w_o_mem, v_g_mlp, v_w_up, v_w_down, v_g_final):
    d = x.shape[-1]
    me = 4 * lax.axis_index("x") + 2 * lax.axis_index("y") + lax.axis_index("c")
    w_shards = dict(w_in=w_in, w_out=w_out, w_q=w_q_mem, w_kv=w_kv_mem, w_o=w_o_mem, w_up=w_up, w_down=w_down)
    m_shards = dict(w_in=m_w_in, w_out=m_w_out, w_q=m_w_q_mem, w_kv=m_w_kv_mem, w_o=m_w_o_mem, w_up=m_w_up,
                    w_down=m_w_down)
    v_shards = dict(w_in=v_w_in, w_out=v_w_out, w_q=v_w_q_mem, w_kv=v_w_kv_mem, w_o=v_w_o_mem, w_up=v_w_up,
                    w_down=v_w_down)
    gains = dict(g_mix=g_mix, g_attn_out=g_attn_out, g_conv_out=g_conv_out, g_xattn=g_xattn, g_mem=g_mem,
                 g_mlp=g_mlp, g_final=g_final)
    gains2 = {k: v.reshape(1, -1) for k, v in gains.items()}

    shards = {k: w_shards[k].astype(BF16) for k in _BIG}
    shards["conv_w"] = conv_w
    grads = _Grads(distributed=True)
    grad_x = _local_step(x[0], mem[0], loss_target[0], gains2, _Weights({}, shards), grads)

    after = grads.send("small")
    outs = {}
    tiles = dict(w_in=256, w_out=128, w_q=128, w_kv=256, w_o=128, w_up=256, w_down=256)
    for group in (("w_down",), ("w_up",), ("w_o", "w_q", "w_kv", "w_out"), ("w_in",)):
        for k, received in zip(group, grads.wait(group[0], after)):
            outs[k] = _sum_adamw(received, w_shards[k], m_shards[k], v_shards[k], name=f"adamw_{k}", tr=tiles[k])
            after = [outs[k][0]]
    small_received, = grads.wait("small", after)

    m_small = dict(g_mix=m_g_mix, g_attn_out=m_g_attn_out, g_conv_out=m_g_conv_out, g_xattn=m_g_xattn,
                   g_mem=m_g_mem, g_mlp=m_g_mlp, g_final=m_g_final)
    v_small = dict(g_mix=v_g_mix, g_attn_out=v_g_attn_out, g_conv_out=v_g_conv_out, g_xattn=v_g_xattn,
                   g_mem=v_g_mem, g_mlp=v_g_mlp, g_final=v_g_final)
    as_rows = lambda vals, conv: dict({k: a.reshape(1, -1) for k, a in vals.items()}, conv_w=conv)
    loss, small_out = _update_small(small_received, me.reshape(1), as_rows(gains, conv_w),
                                    as_rows(m_small, m_conv_w), as_rows(v_small, v_conv_w))
    small_out = {k: [a.reshape(dict(gains, conv_w=conv_w)[k].shape) for a in res] for k, res in small_out.items()}
    names = {"g_mix": "g_mix", "w_in": "w_in", "conv_w": "conv_w", "g_attn_out": "g_attn_out",
             "g_conv_out": "g_conv_out", "w_out": "w_out", "g_xattn": "g_xattn", "g_mem": "g_mem",
             "w_q_mem": "w_q", "w_kv_mem": "w_kv", "w_o_mem": "w_o", "g_mlp": "g_mlp", "w_up": "w_up",
             "w_down": "w_down", "g_final": "g_final"}
    result = [loss.reshape(()), grad_x[None]]
    for which in range(4):
        for key in names.values():
            result.append(outs[key][which] if key in outs else small_out[key][which])
    return tuple(result)
```

```python
import math

import jax
import jax.numpy as jnp
from jax import lax
from jax.experimental import pallas as pl
from jax.experimental.pallas import tpu as pltpu

F32 = jnp.float32
BF16 = jnp.bfloat16
NORM_EPS = 1e-6
NEG_INF = -1e30
N_DEV = 8
BLK = 128
HEAD_DIM = 64
N_MEM_HEADS = 4
ADAM_LR = 0.001
ADAM_B1 = 0.9
ADAM_B2 = 0.999
ADAM_EPS = 1e-08
ADAM_WD = 0.01
ADAM_STEP = 10
MESH = pl.DeviceIdType.MESH
ANY = pl.BlockSpec(memory_space=pl.ANY)


def _dot(a, b):
    return jnp.dot(a, b, preferred_element_type=F32)


def _dot_nt(a, b):
    return lax.dot_general(a, b, (((1,), (1,)), ((), ())), preferred_element_type=F32)


def _dot_tn(a, b):
    return lax.dot_general(a, b, (((0,), (0,)), ((), ())), preferred_element_type=F32)


def _params(semantics, vmem_mb):
    return pltpu.CompilerParams(dimension_semantics=semantics, vmem_limit_bytes=vmem_mb << 20)


def _rms_fwd(x, g):
    r = lax.rsqrt(jnp.mean(x * x, axis=-1, keepdims=True) + NORM_EPS)
    xh = x * r
    return xh * g, xh, r


def _rms_bwd(dy, xh, r, g):
    gy = dy * g
    return r * (gy - xh * jnp.mean(xh * gy, axis=-1, keepdims=True))


def _position():
    x, y, c = lax.axis_index("x"), lax.axis_index("y"), lax.axis_index("c")
    return x, y, c


def _block_of(ref, j, axis, shard_shape):
    r, c = shard_shape
    if axis is None:
        return ref.at[j]
    if axis == 0:
        return ref.at[pl.ds(j * r, r), :]
    return ref.at[:, pl.ds(j * c, c)]


class _Gather:
    has_mid = True
    alias_pairs = ()

    def __init__(self, shards, axes, late=False):
        self.arrays = list(shards)
        self.axes = list(axes)
        self.late = late
        self.n = len(self.arrays)

    def out_shape(self):
        res = []
        for s, axis in zip(self.arrays, self.axes):
            r, c = s.shape
            shape = (N_DEV, r, c) if axis is None else (N_DEV * r, c) if axis == 0 else (r, N_DEV * c)
            res.append(jax.ShapeDtypeStruct(shape, s.dtype))
        return res

    def scratch(self):
        return [pltpu.SemaphoreType.DMA((self.n, 7)), pltpu.SemaphoreType.DMA((self.n, 7)),
                pltpu.SemaphoreType.DMA((self.n,))]

    def _ctx(self, ins, outs, sems):
        send_sems, recv_sems, local_sems = sems
        x, y, c = _position()
        me, sibling = (x, y, c), (x, y, 1 - c)
        chips = [(1 - x, y), (x, 1 - y), (1 - x, 1 - y)]

        def lin(px, py, pc):
            return 4 * px + 2 * py + pc

        def place(a, block):
            return _block_of(outs[a], lin(*block), self.axes[a], self.arrays[a].shape)

        def copy(a, k, block, to, src=None):
            dst = place(a, block)
            return pltpu.make_async_remote_copy(
                src_ref=dst if src is None else src, dst_ref=dst,
                send_sem=send_sems.at[a, k], recv_sem=recv_sems.at[a, k],
                device_id=to, device_id_type=MESH)

        def mine():
            return [pltpu.make_async_copy(ins[a], place(a, me), local_sems.at[a]) for a in range(self.n)]

        def first():
            res = []
            for a in range(self.n):
                res.append(copy(a, 0, me, sibling, src=ins[a]))
                res += [copy(a, 1 + j, me, (*chip, c), src=ins[a]) for j, chip in enumerate(chips)]
            return res

        return c, me, sibling, chips, copy, mine, first

    def start(self, ins, outs, sems):
        _, _, _, _, _, mine, first = self._ctx(ins, outs, sems)
        for cp in mine() + first():
            cp.start()

    def mid(self, ins, outs, sems):
        c, me, sibling, chips, copy, _, _ = self._ctx(ins, outs, sems)
        for j, chip in enumerate(chips):
            for a in range(self.n):
                copy(a, 1 + j, (*chip, c), me).wait_recv()
                copy(a, 4 + j, (*chip, c), sibling).start()

    def finish(self, ins, outs, sems):
        c, me, sibling, chips, copy, mine, first = self._ctx(ins, outs, sems)
        for a in range(self.n):
            copy(a, 0, sibling, me).wait_recv()
            for j, chip in enumerate(chips):
                copy(a, 4 + j, (*chip, 1 - c), me).wait_recv()
        for cp in first():
            cp.wait_send()
        for j, chip in enumerate(chips):
            for a in range(self.n):
                copy(a, 4 + j, (*chip, c), sibling).wait_send()
        for cp in mine():
            cp.wait()


class _Exchange:
    def __init__(self, parts, axes):
        self.n = len(parts)
        self.axes = list(axes)
        self.arrays = list(parts)

    def _piece(self, a):
        r, c = self.arrays[a].shape
        axis = self.axes[a]
        return (r, c) if axis is None else (r // N_DEV, c) if axis == 0 else (r, c // N_DEV)

    def out_shape(self):
        return [jax.ShapeDtypeStruct((N_DEV,) + self._piece(a), self.arrays[a].dtype) for a in range(self.n)]

    def semaphores(self):
        return [pltpu.SemaphoreType.DMA((7 * self.n,)), pltpu.SemaphoreType.DMA((7 * self.n,)),
                pltpu.SemaphoreType.DMA((self.n,))]

    def _ctx(self, ins, outs, sems):
        send_sems, recv_sems, local_sems = sems
        x, y, c = _position()
        me = 4 * x + 2 * y + c

        def src(a, j):
            return ins[a] if self.axes[a] is None else _block_of(ins[a], j, self.axes[a], self._piece(a))

        def dst(a, j):
            return outs[a].at[j]

        def local():
            return [pltpu.make_async_copy(src(a, me), dst(a, me), local_sems.at[a]) for a in range(self.n)]

        def remote(inbound):
            res = []
            for a in range(self.n):
                for k in range(1, N_DEV):
                    peer = (1 - x if k & 4 else x, 1 - y if k & 2 else y, 1 - c if k & 1 else c)
                    plin = 4 * peer[0] + 2 * peer[1] + peer[2]
                    res.append(pltpu.make_async_remote_copy(
                        src_ref=src(a, plin), dst_ref=dst(a, plin if inbound else me),
                        send_sem=send_sems.at[7 * a + k - 1], recv_sem=recv_sems.at[7 * a + k - 1],
                        device_id=peer, device_id_type=MESH))
            return res

        return local, remote

    def start(self, ins, outs, sems):
        local, remote = self._ctx(ins, outs, sems)
        for cp in local() + remote(False):
            cp.start()

    def finish(self, ins, outs, sems):
        local, remote = self._ctx(ins, outs, sems)
        for cp in remote(True):
            cp.wait_recv()
        for cp in remote(False):
            cp.wait_send()
        for cp in local():
            cp.wait()


def _exchange_start(rider, name):
    n = rider.n
    parts = rider.arrays
    lands = [lax.empty(s.shape, s.dtype) for s in rider.out_shape()]
    hbm = pl.BlockSpec(memory_space=pltpu.HBM)
    sem = pl.BlockSpec(memory_space=pltpu.SEMAPHORE)

    def body(*refs):
        ins, sems = refs[:n], refs[2 * n:2 * n + 3]
        outs, token = refs[2 * n + 3 + n:2 * n + 3 + 2 * n], refs[-1]
        rider.start(ins, outs, sems)
        token[...] = jnp.zeros_like(token)

    res = pl.pallas_call(
        body, name=name,
        out_shape=rider.semaphores() + [pltpu.HBM(p.shape, p.dtype) for p in parts]
                  + [pltpu.HBM(z.shape, z.dtype) for z in lands] + [jax.ShapeDtypeStruct((8, 128), F32)],
        in_specs=[hbm] * (2 * n), out_specs=[sem] * 3 + [hbm] * (2 * n) + [pl.BlockSpec(memory_space=pltpu.VMEM)],
        input_output_aliases={i: 3 + i for i in range(2 * n)},
        compiler_params=pltpu.CompilerParams(has_side_effects=pltpu.SideEffectType.DATAFLOW_SIDE_EFFECTING),
    )(*[pltpu.with_memory_space_constraint(a, pltpu.HBM) for a in parts + lands])
    return res[:3], res[3:3 + n], res[3 + n:3 + 2 * n], res[-1]


def _exchange_wait(rider, started, after, name):
    n = rider.n
    sems, parts, lands, _ = started
    hbm = pl.BlockSpec(memory_space=pltpu.HBM)
    sem = pl.BlockSpec(memory_space=pltpu.SEMAPHORE)

    def body(*refs):
        rider.finish(refs[:n], refs[n:2 * n], refs[2 * n:2 * n + 3])

    res = pl.pallas_call(
        body, name=name, out_shape=[pltpu.HBM(a.shape, a.dtype) for a in list(parts) + list(lands)],
        in_specs=[hbm] * (2 * n) + [sem] * 3 + [ANY] * len(after), out_specs=[hbm] * (2 * n),
        input_output_aliases={i: i for i in range(2 * n)},
        compiler_params=pltpu.CompilerParams(has_side_effects=pltpu.SideEffectType.DATAFLOW_SIDE_EFFECTING),
    )(*parts, *lands, *sems, *after)
    return list(res[n:])


def _pcall(body, *, name, grid, in_specs, out_specs, out_shape, scratch_shapes=(), semantics, vmem_mb, rider=None,
           aliases=None, after=()):
    in_specs, out_specs, out_shape = list(in_specs), list(out_specs), list(out_shape)
    scratch_shapes = list(scratch_shapes)
    aliases = dict(aliases or {})
    if rider is None:
        n_in, after = len(in_specs), list(after)

        def plain(*refs):
            body(*refs[:n_in], *refs[n_in + len(after):])

        call = pl.pallas_call(plain if after else body, name=name, grid=grid, in_specs=in_specs + [ANY] * len(after),
                              out_specs=out_specs, out_shape=out_shape, scratch_shapes=scratch_shapes,
                              input_output_aliases=aliases, compiler_params=_params(semantics, vmem_mb))
        return lambda *args: (list(call(*args, *after)), None)
    n_in, n_out, n_scr = len(in_specs), len(out_specs), len(scratch_shapes)
    r_in, r_shapes = len(rider.arrays), [pltpu.HBM(s.shape, s.dtype) for s in rider.out_shape()]
    r_out = len(r_shapes)
    hbm = pl.BlockSpec(memory_space=pltpu.HBM)
    aliases.update({n_in + i: n_out + o for i, o in rider.alias_pairs})
    total = math.prod(grid)
    mid_step = total - 1 if rider.has_mid and rider.late else (3 * total) // 4

    def wrapped(*refs):
        bounds = [0, n_in, r_in, n_out, r_out, n_scr]
        for i in range(1, len(bounds)):
            bounds[i] += bounds[i - 1]
        a, ra, o, ro, s = (refs[bounds[i]:bounds[i + 1]] for i in range(5))
        rs = refs[bounds[5]:]
        step = pl.program_id(0)
        for k in range(1, len(grid)):
            step = step * grid[k] + pl.program_id(k)
        pl.when(step == 0)(lambda: rider.start(ra, ro, rs))
        body(*a, *o, *s)
        if rider.has_mid:
            pl.when(step == mid_step)(lambda: rider.mid(ra, ro, rs))
        pl.when(step == total - 1)(lambda: rider.finish(ra, ro, rs))

    call = pl.pallas_call(
        wrapped, name=name, grid=grid, in_specs=in_specs + [ANY] * r_in, out_specs=out_specs + [hbm] * r_out,
        out_shape=out_shape + r_shapes, scratch_shapes=scratch_shapes + rider.scratch(),
        input_output_aliases=aliases, compiler_params=_params(("arbitrary",) * len(grid), vmem_mb))

    def run(*args):
        res = call(*args, *rider.arrays)
        return list(res[:n_out]), list(res[n_out:])

    return run


def _norm_matmul(x, g, w, *, name, out_dtype, tb, bn, relu=False, save_h=False, rider=None):
    t, d = x.shape
    n = w.shape[1]

    def body(x_ref, g_ref, w_ref, o_ref, *rest):
        h_scr = rest[-1]

        @pl.when(pl.program_id(1) == 0)
        def _():
            h = _rms_fwd(x_ref[...], g_ref[...])[0].astype(BF16)
            h_scr[...] = h
            if save_h:
                rest[0][...] = h

        acc = _dot(h_scr[...], w_ref[...])
        if relu:
            acc = jnp.maximum(acc, 0.0)
        o_ref[...] = acc.astype(out_dtype)

    out_shape = [jax.ShapeDtypeStruct((t, n), out_dtype)]
    out_specs = [pl.BlockSpec((tb, bn), lambda i, j: (i, j))]
    if save_h:
        out_shape.append(jax.ShapeDtypeStruct((t, d), BF16))
        out_specs.append(pl.BlockSpec((tb, d), lambda i, j: (i, 0)))
    res, extra = _pcall(
        body, name=name, grid=(t // tb, n // bn),
        in_specs=[pl.BlockSpec((tb, d), lambda i, j: (i, 0)),
                  pl.BlockSpec((1, d), lambda i, j: (0, 0)),
                  pl.BlockSpec((d, bn), lambda i, j: (0, j))],
        out_specs=out_specs, out_shape=out_shape,
        scratch_shapes=[pltpu.VMEM((tb, d), BF16)],
        semantics=("parallel", "arbitrary"), vmem_mb=48, rider=rider,
    )(x, g, w)
    res = res if save_h else res[0]
    return res if rider is None else (res, extra)


def _proj(x, g, w, *, tb, rider=None):
    t, d = x.shape
    half = w.shape[1] // 2

    def body(x_ref, g_ref, w_ref, qkv_ref, gates_ref, h_ref, h_scr):
        j = pl.program_id(1)

        @pl.when(j == 0)
        def _():
            h = _rms_fwd(x_ref[...], g_ref[...])[0].astype(BF16)
            h_scr[...] = h
            h_ref[...] = h

        acc = _dot(h_scr[...], w_ref[...])

        @pl.when(j == 0)
        def _():
            qkv_ref[...] = acc

        @pl.when(j == 1)
        def _():
            gates_ref[...] = acc.astype(BF16)

    tok = lambda c: pl.BlockSpec((tb, c), lambda i, j: (i, 0))
    res, extra = _pcall(
        body, name="proj", grid=(t // tb, 2),
        in_specs=[tok(d), pl.BlockSpec((1, d), lambda i, j: (0, 0)), pl.BlockSpec((d, half), lambda i, j: (0, j))],
        out_specs=[tok(half), tok(half), tok(d)],
        out_shape=[jax.ShapeDtypeStruct((t, half), F32), jax.ShapeDtypeStruct((t, half), BF16),
                   jax.ShapeDtypeStruct((t, d), BF16)],
        scratch_shapes=[pltpu.VMEM((tb, d), BF16)],
        semantics=("parallel", "arbitrary"), vmem_mb=48, rider=rider,
    )(x, g, w)
    return res if rider is None else (res, extra)


def _matmul_nt_normbwd(dy, w, x, g, dres, *, name, tb, also_bf16=False, to_natural=False, after=()):
    t, d = x.shape
    stacked = dy.ndim == 3
    has_res = dres is not None
    n_i = SEG // TI
    if to_natural:
        tb = N_RES * TI

    def body(dy_ref, w_ref, x_ref, g_ref, *rest):
        rest = list(rest)
        dres_ref = rest.pop(0) if has_res else None
        dx_ref = rest.pop(0)
        dxb_ref = rest.pop(0) if also_bf16 else None
        gg_ref = rest.pop(0)
        i = pl.program_id(0)

        def rows(ref, *lead):
            v = ref[lead] if lead else ref[...]
            return v[0].reshape(tb, v.shape[-1]) if to_natural else v

        if stacked:
            kb = dy_ref.shape[-1]
            dh = _dot_nt(rows(dy_ref, 0), w_ref[:, 0:kb])
            for s in range(1, dy_ref.shape[0]):
                dh = dh + _dot_nt(rows(dy_ref, s), w_ref[:, s * kb:(s + 1) * kb])
        else:
            dh = _dot_nt(rows(dy_ref), w_ref[...])
        g_v = g_ref[...]
        _, xh, r = _rms_fwd(rows(x_ref), g_v)
        dx = _rms_bwd(dh, xh, r, g_v)
        if has_res:
            dx = dx + rows(dres_ref)
        if to_natural:
            scr = rest.pop(0)
            for cb in range(d // BLK):
                cols = slice(cb * BLK, (cb + 1) * BLK)
                slab = scr.at[cb]
                for res in range(N_RES):
                    slab[pl.ds(res, TI, stride=N_RES), :] = dx[res * TI:(res + 1) * TI, cols]
                dx_ref[:, cols] = slab[...]
        else:
            dx_ref[...] = dx
        if also_bf16:
            dxb_ref[...] = dx.astype(BF16)
        part = jnp.sum(dh * xh, axis=0, keepdims=True)

        @pl.when(i == 0)
        def _():
            gg_ref[...] = part

        @pl.when(i != 0)
        def _():
            gg_ref[...] += part

    tok = pl.BlockSpec((tb, d), lambda i: (i, 0))
    row = pl.BlockSpec((1, d), lambda i: (0, 0))
    if to_natural:
        act = pl.BlockSpec((1, N_RES, TI, d), lambda i: (i // n_i, 0, i % n_i, 0))
        dy_spec = pl.BlockSpec((dy.shape[0], 1, N_RES, TI, dy.shape[2]), lambda i: (0, i // n_i, 0, i % n_i, 0))
        dy, x = dy.reshape(dy.shape[0], t // HALF, N_RES, SEG, dy.shape[2]), _x4(x)
        dres = _x4(dres) if has_res else None
    elif stacked:
        act, dy_spec = tok, pl.BlockSpec((dy.shape[0], tb, dy.shape[2]), lambda i: (0, i, 0))
    else:
        act, dy_spec = tok, pl.BlockSpec((tb, dy.shape[1]), lambda i: (i, 0))
    in_specs = [dy_spec, pl.BlockSpec(w.shape, lambda i: (0, 0)), act, row]
    args = [dy, w, x, g]
    if has_res:
        in_specs.append(act)
        args.append(dres)
    out_specs = [tok] + ([tok] if also_bf16 else []) + [row]
    out_shape = ([jax.ShapeDtypeStruct((t, d), F32)] + ([jax.ShapeDtypeStruct((t, d), BF16)] if also_bf16 else [])
                 + [jax.ShapeDtypeStruct((1, d), F32)])
    res, _ = _pcall(
        body, name=name, grid=(t // tb,), in_specs=in_specs, out_specs=out_specs, out_shape=out_shape,
        scratch_shapes=[pltpu.VMEM((d // BLK, tb, BLK), F32)] if to_natural else [],
        semantics=("arbitrary",), vmem_mb=56, after=after,
    )(*args)
    return res


def _matmul_tn(a, b, *, name, bm, bn, square_a=False, after=()):
    t, m = a.shape
    stacked = b.ndim == 3
    n = b.shape[0] * bn if stacked else b.shape[1]

    def body(a_ref, b_ref, o_ref):
        av = a_ref[...]
        if square_a:
            av = av.astype(F32)
            av = (av * av).astype(BF16)
        o_ref[...] = _dot_tn(av, b_ref[...]).astype(BF16)

    res, _ = _pcall(
        body, name=name, grid=(m // bm, n // bn),
        in_specs=[pl.BlockSpec((t, bm), lambda i, j: (0, i)),
                  pl.BlockSpec((None, t, bn), lambda i, j: (j, 0, 0)) if stacked
                  else pl.BlockSpec((t, bn), lambda i, j: (0, j))],
        out_specs=[pl.BlockSpec((bm, bn), lambda i, j: (i, j))], out_shape=[jax.ShapeDtypeStruct((m, n), BF16)],
        semantics=("parallel", "parallel"), vmem_mb=56, after=after,
    )(a, b)
    return res[0]


N_RES = 16
SEG = 128
HALF = N_RES * SEG
TI = 32
HALO = 16


def _x4(a):
    return a.reshape(a.shape[0] // HALF, N_RES, SEG, a.shape[1])


def _reorder(arrays, name, rider=None):
    t, c = arrays[0].shape
    n = len(arrays)
    n_i = SEG // TI

    def body(*refs):
        scr = refs[-1]
        for i_ref, o_ref in zip(refs[:n], refs[n:2 * n]):
            for cb in range(c // BLK):
                cols = slice(cb * BLK, (cb + 1) * BLK)
                slab = scr.at[cb]
                slab[...] = i_ref[:, cols]
                for r in range(N_RES):
                    o_ref[0, r, :, cols] = slab[pl.ds(r, TI, stride=N_RES), :]

    res, extra = _pcall(
        body, name=name, grid=(t // (TI * N_RES),),
        in_specs=[pl.BlockSpec((TI * N_RES, c), lambda s: (s, 0))] * n,
        out_specs=[pl.BlockSpec((1, N_RES, TI, c), lambda s: (s // n_i, 0, s % n_i, 0))] * n,
        out_shape=[jax.ShapeDtypeStruct((t // HALF, N_RES, SEG, c), F32)] * n,
        scratch_shapes=[pltpu.VMEM((c // BLK, TI * N_RES, BLK), F32)],
        semantics=("parallel",), vmem_mb=32, rider=rider,
    )(*arrays)
    res = [r.reshape(t, c) for r in res]
    return res if rider is None else (res, extra)


_PATTERNS = ((1, 16, 8, SEG), (4, 4, 32, 4 * SEG), (16, 1, SEG, 0))
_FIRST = {1: 1, 4: 4, 16: 16}


def _group_rows(d, g):
    a = g >> 4
    if d == 16:
        base = a * HALF + (g & 15) * SEG
        prev = base - HALF
    elif d == 4:
        c = (g >> 2) & 3
        base = a * HALF + (g & 3) * SEG + c * 32
        prev = jnp.where(c > 0, base - 32, base - HALF + 96)
    else:
        c = g & 15
        base = a * HALF + c * 8
        prev = jnp.where(c > 0, base - 8, base - HALF + 120)
    return base, prev


def _load_rows(ref, base, n, rows, stride):
    parts = [ref[pl.ds(pl.multiple_of(base + j * stride, 8), rows), :] for j in range(n)]
    return parts[0] if n == 1 else jnp.concatenate(parts, axis=0)


def _store_rows(ref, base, val, n, rows, stride, add=False):
    for j in range(n):
        sl = pl.ds(pl.multiple_of(base + j * stride, 8), rows)
        piece = val[j * rows:(j + 1) * rows, :]
        if add:
            ref[sl, :] += piece
        else:
            ref[sl, :] = piece


def _band_bias(n, rows):
    shift = rows.bit_length() - 1
    lq = lax.broadcasted_iota(jnp.int32, (BLK, BLK), 0)
    lk = lax.broadcasted_iota(jnp.int32, (BLK, BLK), 1)
    iq = (lq & (rows - 1)) * n + (lq >> shift)
    ik = (lk & (rows - 1)) * n + (lk >> shift)
    zero = jnp.zeros((BLK, BLK), F32)
    return jnp.where(ik >= iq, zero, NEG_INF), jnp.where(ik <= iq, zero, NEG_INF)


def _set_bias(bias_scr, n, rows):
    prev_b, cur_b = _band_bias(n, rows)
    for half in range(2):
        bias_scr[half * BLK:(half + 1) * BLK, 0:BLK] = prev_b
        bias_scr[half * BLK:(half + 1) * BLK, BLK:2 * BLK] = cur_b


SCALE = 1.0 / math.sqrt(HEAD_DIM)


def _head_consts(value=1.0):
    lane_lo = lax.broadcasted_iota(jnp.int32, (BLK, BLK), 1) < HEAD_DIM
    return lane_lo, [jnp.where(lane_lo, value, 0.0).astype(BF16), jnp.where(lane_lo, 0.0, value).astype(BF16)]


def _stack_heads(v, head_mask):
    return jnp.concatenate([v * head_mask[0], v * head_mask[1]], axis=0)


def _unstack_heads(v2, lane_lo):
    return jnp.where(lane_lo, v2[:BLK], v2[BLK:])


def _rows_per_head(v, lane_lo):
    rolled = pltpu.roll(v, HEAD_DIM, axis=1)
    return jnp.concatenate([jnp.where(lane_lo, v, rolled), jnp.where(lane_lo, rolled, v)], axis=0)


WIDTH = 4


def _loop(lo, hi, fn, width=None):
    if width is None:
        def body(g, carry):
            fn(g)
            return carry

        if hi > lo:
            lax.fori_loop(lo, hi, body, 0)
        return
    while hi > lo:
        trips = (hi - lo) // width
        if trips:
            def body(i, carry, lo=lo, width=width):
                fn([lo + width * i + j for j in range(width)])
                return carry

            lax.fori_loop(0, trips, body, 0)
            lo += trips * width
        width = max(1, width // 2)


def _mix_weights(l1, l2, l3):
    mx = jnp.maximum(jnp.maximum(l1, l2), l3)
    e1, e2, e3 = jnp.exp(l1 - mx), jnp.exp(l2 - mx), jnp.exp(l3 - mx)
    inv = 1.0 / (e1 + e2 + e3)
    return e1 * inv, e2 * inv, e3 * inv


def _attention_fwd(qkv, rider=None):
    t = qkv.shape[0]
    groups = 16 * (t // HALF)

    def body(q_ref, k_ref, v_ref, attn_ref, l1_ref, l2_ref, l3_ref, o_scr, bias_scr):
        lane_lo, q_mask = _head_consts(SCALE)
        l_refs = (l1_ref, l2_ref, l3_ref)
        for p, (d, n, rows, stride) in enumerate(_PATTERNS):
            _set_bias(bias_scr, n, rows)
            o_p, l_p = o_scr.at[p], l_refs[p]

            def block(gs, has_prev):
                at = [_group_rows(d, g) for g in gs]

                def load(ref, b):
                    return _load_rows(ref, b, n, rows, stride).astype(BF16)

                q2 = [_stack_heads(load(q_ref, b), q_mask) for b, _ in at]
                k2 = [load(k_ref, b) for b, _ in at]
                v2 = [load(v_ref, b) for b, _ in at]
                if has_prev:
                    k2 = [jnp.concatenate([load(k_ref, pv), k], axis=0) for (_, pv), k in zip(at, k2)]
                    v2 = [jnp.concatenate([load(v_ref, pv), v], axis=0) for (_, pv), v in zip(at, v2)]
                s = [_dot_nt(q, k) for q, k in zip(q2, k2)]
                s = [x + (bias_scr[...] if has_prev else bias_scr[:, BLK:2 * BLK]) for x in s]
                mx = [jnp.max(x, axis=1, keepdims=True) for x in s]
                e = [jnp.exp(x - m) for x, m in zip(s, mx)]
                den = [jnp.sum(x, axis=1, keepdims=True) for x in e]
                o2 = [_dot(x.astype(BF16), v) * (1.0 / dn) for x, v, dn in zip(e, v2, den)]
                lse2 = [jnp.broadcast_to(m + jnp.log(dn), (2 * BLK, BLK)) for m, dn in zip(mx, den)]
                for (b, _), o, l in zip(at, o2, lse2):
                    _store_rows(o_p, b, _unstack_heads(o, lane_lo), n, rows, stride)
                    _store_rows(l_p, b, _unstack_heads(l, lane_lo), n, rows, stride)

            _loop(0, _FIRST[d], lambda gs: block(gs, False), width=2 * WIDTH)
            _loop(_FIRST[d], groups, lambda gs: block(gs, True), width=2 * WIDTH)

        def mix(i):
            sl = pl.ds(pl.multiple_of(i * 256, 256), 256)
            w = _mix_weights(l1_ref[sl, :], l2_ref[sl, :], l3_ref[sl, :])
            attn_ref[sl, :] = w[0] * o_scr[0, sl, :] + w[1] * o_scr[1, sl, :] + w[2] * o_scr[2, sl, :]

        _loop(0, t // 256, mix)

    def col(c0):
        return pl.BlockSpec((t, BLK), lambda hp: (0, c0 + hp))

    res, extra = _pcall(
        body, name="attention_fwd", grid=(4,), in_specs=[col(0), col(4), col(8)], out_specs=[col(0)] * 4,
        out_shape=[jax.ShapeDtypeStruct((t, 512), F32)] * 4,
        scratch_shapes=[pltpu.VMEM((3, t, BLK), F32), pltpu.VMEM((2 * BLK, 2 * BLK), F32)],
        semantics=("parallel",), vmem_mb=48, rider=rider,
    )(qkv, qkv, qkv)
    return res if rider is None else (res, extra)


def _attention_bwd(qkv, dattn, dsum, lses, dproj):
    t = qkv.shape[0]
    groups = 16 * (t // HALF)

    def body(q_ref, k_ref, v_ref, da_ref, ds_ref, l1_ref, l2_ref, l3_ref, kept_ref, out_ref, acc, bias_scr):
        del kept_ref
        lane_lo, head_mask = _head_consts()
        q_mask = _head_consts(SCALE)[1]
        l_refs = (l1_ref, l2_ref, l3_ref)

        def clear(i):
            sl = pl.ds(pl.multiple_of(i * 512, 512), 512)
            for s in range(3):
                acc[s, sl, :] = jnp.zeros((512, BLK), F32)

        _loop(0, t // 512, clear)
        dq_acc, dk_acc, dv_acc = acc.at[0], acc.at[1], acc.at[2]
        for p, (d, n, rows, stride) in enumerate(_PATTERNS):
            _set_bias(bias_scr, n, rows)

            def block(gs, has_prev):
                at = [_group_rows(d, g) for g in gs]

                def load(ref, b):
                    return _load_rows(ref, b, n, rows, stride)

                def put(ref, b, val):
                    _store_rows(ref, b, val, n, rows, stride, add=True)

                def wide(x):
                    return jnp.concatenate([x, x], axis=1) if has_prev else x

                lse = [[load(ref, b) for ref in l_refs] for b, _ in at]
                w = [_mix_weights(*ls)[p] for ls in lse]
                do2 = [_stack_heads((wg * load(da_ref, b)).astype(BF16), head_mask) for wg, (b, _) in zip(w, at)]
                dl2 = [wide(_rows_per_head(wg * load(ds_ref, b), lane_lo)) for wg, (b, _) in zip(w, at)]
                lse2 = [wide(_rows_per_head(ls[p], lane_lo)) for ls in lse]
                q2 = [_stack_heads(load(q_ref, b).astype(BF16), q_mask) for b, _ in at]
                k2 = [load(k_ref, b).astype(BF16) for b, _ in at]
                v2 = [load(v_ref, b).astype(BF16) for b, _ in at]
                if has_prev:
                    k2 = [jnp.concatenate([load(k_ref, pv).astype(BF16), k], axis=0) for (_, pv), k in zip(at, k2)]
                    v2 = [jnp.concatenate([load(v_ref, pv).astype(BF16), v], axis=0) for (_, pv), v in zip(at, v2)]
                s = [_dot_nt(q, k) for q, k in zip(q2, k2)]
                dp = [_dot_nt(do, v) for do, v in zip(do2, v2)]
                pr = [jnp.exp(x + (bias_scr[...] if has_prev else bias_scr[:, BLK:2 * BLK]) - l)
                      for x, l in zip(s, lse2)]
                ds = [(pg * (x - dl)).astype(BF16) for pg, x, dl in zip(pr, dp, dl2)]
                dq2 = [_dot(x, k) * SCALE for x, k in zip(ds, k2)]
                dk2 = [_dot_tn(x, q) for x, q in zip(ds, q2)]
                dv2 = [_dot_tn(pg.astype(BF16), do) for pg, do in zip(pr, do2)]
                for (b, pv), dq, dk, dv in zip(at, dq2, dk2, dv2):
                    put(dq_acc, b, _unstack_heads(dq, lane_lo))
                    if has_prev:
                        put(dk_acc, pv, dk[:BLK])
                        put(dv_acc, pv, dv[:BLK])
                        put(dk_acc, b, dk[BLK:])
                        put(dv_acc, b, dv[BLK:])
                    else:
                        put(dk_acc, b, dk)
                        put(dv_acc, b, dv)

            _loop(0, _FIRST[d], lambda gs: block(gs, False), width=WIDTH)
            _loop(_FIRST[d], groups, lambda gs: block(gs, True), width=WIDTH)

        def emit(i):
            sl = pl.ds(pl.multiple_of(i * 512, 512), 512)
            for s in range(3):
                out_ref[s, sl, :] = acc[s, sl, :].astype(BF16)

        _loop(0, t // 512, emit)

    def col(c0):
        return pl.BlockSpec((t, BLK), lambda hp: (0, c0 + hp))

    res, _ = _pcall(
        body, name="attention_bwd", grid=(4,),
        in_specs=[col(0), col(4), col(8)] + [col(0)] * 5 + [ANY],
        out_specs=[pl.BlockSpec((3, t, BLK), lambda hp: (0, 0, hp))],
        out_shape=[jax.ShapeDtypeStruct(dproj.shape, BF16)],
        scratch_shapes=[pltpu.VMEM((3, t, BLK), F32), pltpu.VMEM((2 * BLK, 2 * BLK), F32)],
        semantics=("parallel",), vmem_mb=56, aliases={8: 0},
    )(qkv, qkv, qkv, dattn, dsum, *lses, dproj)
    return res[0]


def _order_specs(t):
    n_i = SEG // TI
    nblk = (t // HALF) * n_i
    per = TI // HALO

    def main(c, col=0):
        return pl.BlockSpec((1, N_RES, TI, c), lambda s: (s // n_i, 0, s % n_i, col))

    def before(c, col=0):
        return pl.BlockSpec((1, 2, HALO, c), lambda s: (jnp.maximum(s - 1, 0) // n_i, N_RES // 2 - 1,
                                                        (jnp.maximum(s - 1, 0) % n_i) * per + per - 1, col))

    def after(c, col=0):
        return pl.BlockSpec((1, 2, HALO, c), lambda s: (jnp.minimum(s + 1, nblk - 1) // n_i, 0,
                                                        (jnp.minimum(s + 1, nblk - 1) % n_i) * per, col))

    return nblk, main, before, after


def _shift_in(v, row_in, up):
    rows = v.shape[0]
    idx = lax.broadcasted_iota(jnp.int32, v.shape, 0)
    fill = jnp.broadcast_to(row_in, v.shape)
    if up:
        return jnp.where(idx == rows - 1, fill, pltpu.roll(v, rows - 1, axis=0))
    return jnp.where(idx == 0, fill, pltpu.roll(v, 1, axis=0))


def _taps_behind(u, before):
    s15 = _shift_in(u[N_RES - 1], before[1, HALO - 1:HALO, :], up=False)
    s14 = _shift_in(u[N_RES - 2], before[0, HALO - 1:HALO, :], up=False)
    m1 = jnp.concatenate([s15[None], u[:N_RES - 1]], axis=0)
    m2 = jnp.concatenate([s14[None], s15[None], u[:N_RES - 2]], axis=0)
    return m1, m2


def _taps_ahead(u, after):
    t0 = _shift_in(u[0], after[0, 0:1, :], up=True)
    t1 = _shift_in(u[1], after[1, 0:1, :], up=True)
    p1 = jnp.concatenate([u[1:], t0[None]], axis=0)
    p2 = jnp.concatenate([u[2:], t0[None], t1[None]], axis=0)
    return p1, p2


def _conv_fwd(gates, before, first, cw):
    gates, before = gates.astype(F32), before.astype(F32)
    bg, cg, xc = gates[..., 0:512], gates[..., 512:1024], gates[..., 1024:1536]
    u = cg * xc
    ub = before[..., 512:1024] * before[..., 1024:1536]
    ub = jnp.where(first, jnp.zeros_like(ub), ub)
    m1, m2 = _taps_behind(u, ub)
    conv = m2 * cw[0:1, :] + m1 * cw[1:2, :] + u * cw[2:3, :]
    return bg, u, m1, m2, conv


def _sum_tokens(v):
    return jnp.sum(jnp.sum(v, axis=0), axis=0, keepdims=True)


def _mixer_fwd(x, attn, gates, cw, g_a, g_c, w_out):
    t, d = x.shape
    nblk, main, before, _ = _order_specs(t)
    rows = N_RES * TI

    def body(x_ref, at_ref, gt_ref, gb_ref, cw_ref, ga_ref, gc_ref, wa_ref, wb_ref, x1_ref, mg_ref):
        an = _rms_fwd(at_ref[0], ga_ref[...])[0].astype(BF16)
        bg, _, _, _, conv = _conv_fwd(gt_ref[0], gb_ref[0], pl.program_id(0) == 0, cw_ref[...])
        cn = _rms_fwd(bg * conv, gc_ref[...])[0].astype(BF16)
        mg_ref[0, :, :, 0:512] = an
        mg_ref[0, :, :, 512:1024] = cn
        y = _dot(an.reshape(rows, 512), wa_ref[...]) + _dot(cn.reshape(rows, 512), wb_ref[...])
        x1_ref[0] = x_ref[0] + y.reshape(N_RES, TI, d)

    const = lambda r, c, i0=0: pl.BlockSpec((r, c), lambda s: (i0, 0))
    x1, merged = pl.pallas_call(
        body, name="mixer_fwd", grid=(nblk,),
        in_specs=[main(d), main(512), main(1536), before(1536), const(3, 512), const(1, 512), const(1, 512),
                  const(512, d), const(512, d, 1)],
        out_specs=[main(d), main(d)],
        out_shape=[jax.ShapeDtypeStruct(_x4(x).shape, F32), jax.ShapeDtypeStruct(_x4(x).shape, BF16)],
        compiler_params=_params(("parallel",), 48),
    )(_x4(x), _x4(attn), _x4(gates), _x4(gates), cw, g_a, g_c, w_out, w_out)
    return x1.reshape(t, d), merged.reshape(t, d)


def _mixer_bwd(dx1, merged, attn, gates, cw, g_a, g_c, w_out, head_sum, after=()):
    t, d = dx1.shape
    nblk, main, before, _ = _order_specs(t)
    rows = N_RES * TI

    def body(dx_ref, mg_ref, at_ref, gt_ref, gb_ref, cw_ref, ga_ref, gc_ref, wa_ref, wb_ref, hs_ref,
             da_ref, dsum_ref, dy_ref, gga_ref, ggc_ref, gw_ref, acc_w):
        s = pl.program_id(0)
        dxb = dx_ref[0].reshape(rows, d).astype(BF16)

        @pl.when(s == 0)
        def _():
            acc_w[...] = jnp.zeros_like(acc_w)

        acc_w[...] += _dot_tn(mg_ref[0].reshape(rows, d), dxb)

        @pl.when(s == nblk - 1)
        def _():
            gw_ref[...] = acc_w[...].astype(BF16)

        dma = _dot_nt(dxb, wa_ref[...]).reshape(N_RES, TI, 512)
        dmc = _dot_nt(dxb, wb_ref[...]).reshape(N_RES, TI, 512)
        attn_v, g_av = at_ref[0], ga_ref[...]
        _, ah, ra = _rms_fwd(attn_v, g_av)
        dattn = _rms_bwd(dma, ah, ra, g_av)
        da_ref[0] = dattn
        z = (dattn * attn_v).reshape(rows, 512)
        hs = hs_ref[...]
        z1 = z.astype(BF16)
        z2 = (z - z1.astype(F32)).astype(BF16)
        dsum_ref[0] = (_dot(z1, hs) + _dot(z2, hs)).reshape(N_RES, TI, 512)
        bg, _, _, _, conv = _conv_fwd(gt_ref[0], gb_ref[0], s == 0, cw_ref[...])
        g_cv = gc_ref[...]
        _, yh, rc = _rms_fwd(bg * conv, g_cv)
        dy_ref[0] = _rms_bwd(dmc, yh, rc, g_cv)
        pa, pc = _sum_tokens(dma * ah), _sum_tokens(dmc * yh)

        @pl.when(s == 0)
        def _():
            gga_ref[...] = pa
            ggc_ref[...] = pc

        @pl.when(s != 0)
        def _():
            gga_ref[...] += pa
            ggc_ref[...] += pc

    const = lambda r, c, i0=0: pl.BlockSpec((r, c), lambda s: (i0, 0))
    shape4 = _x4(attn).shape
    res, _ = _pcall(
        body, name="mixer_bwd", grid=(nblk,),
        in_specs=[main(d), main(d), main(512), main(1536), before(1536), const(3, 512), const(1, 512), const(1, 512),
                  const(512, d), const(512, d, 1), const(512, 512)],
        out_specs=[main(512)] * 3 + [const(1, 512), const(1, 512), const(d, d)],
        out_shape=[jax.ShapeDtypeStruct(shape4, F32)] * 3 + [jax.ShapeDtypeStruct((1, 512), F32)] * 2
        + [jax.ShapeDtypeStruct((d, d), BF16)],
        scratch_shapes=[pltpu.VMEM((d, d), F32)],
        semantics=("arbitrary",), vmem_mb=48, after=after,
    )(_x4(dx1), _x4(merged), _x4(attn), _x4(gates), _x4(gates), cw, g_a, g_c, w_out, w_out, head_sum)
    return [r.reshape(t, 512) for r in res[:3]] + res[3:]


def _conv_bwd(dy, gates, cw, after=()):
    t = dy.shape[0]
    nblk, main, before, ahead = _order_specs(t)
    n_i = SEG // TI
    dy4 = pltpu.with_memory_space_constraint(_x4(dy), pltpu.HBM)

    def body(dy_ref, dya_ref, gt_ref, gb_ref, ga_ref, cw_ref, dp_ref, gcw_ref):
        s = pl.program_id(0)
        cw_v, gates_v = cw_ref[...], gt_ref[0]
        bg, u, m1, m2, conv = _conv_fwd(gates_v, gb_ref[0], s == 0, cw_v)
        dy_v = dy_ref[0]
        dconv = dy_v * bg
        dca = dya_ref[0] * ga_ref[0][..., 0:512].astype(F32)
        dca = jnp.where(s == nblk - 1, jnp.zeros_like(dca), dca)
        p1, p2 = _taps_ahead(dconv, dca)
        du = dconv * cw_v[2:3, :] + p1 * cw_v[1:2, :] + p2 * cw_v[0:1, :]
        dp_ref[0, 0] = (dy_v * conv).astype(BF16)
        dp_ref[1, 0] = (du * gates_v[..., 1024:1536].astype(F32)).astype(BF16)
        dp_ref[2, 0] = (du * gates_v[..., 512:1024].astype(F32)).astype(BF16)
        parts = [_sum_tokens(dconv * m2), _sum_tokens(dconv * m1), _sum_tokens(dconv * u)]

        @pl.when(s == 0)
        def _():
            gcw_ref[...] = jnp.zeros_like(gcw_ref)

        for tap in range(3):
            gcw_ref[tap:tap + 1, :] += parts[tap]

    (dproj, gcw), _ = _pcall(
        body, name="conv_bwd", grid=(nblk,),
        in_specs=[main(512), ahead(512), main(1536), before(1536), ahead(1536),
                  pl.BlockSpec((3, 512), lambda s: (0, 0))],
        out_specs=[pl.BlockSpec((3, 1, N_RES, TI, 512), lambda s: (1, s // n_i, 0, s % n_i, 0)),
                   pl.BlockSpec((8, 512), lambda s: (0, 0))],
        out_shape=[jax.ShapeDtypeStruct((6, t // HALF, N_RES, SEG, 512), BF16), jax.ShapeDtypeStruct((8, 512), F32)],
        semantics=("arbitrary",), vmem_mb=40, after=after,
    )(dy4, dy4, _x4(gates), _x4(gates), _x4(gates), cw)
    return dproj.reshape(6, t, 512), gcw


def _xattn_fwd(x1, g, w_q, kv, w_o, *, tb):
    t, d = x1.shape
    hd = d // N_MEM_HEADS
    m = kv.shape[0]

    def body(x_ref, g_ref, wq_ref, k_ref, v_ref, wo_ref, x2_ref, h_ref, q_ref, o_ref):
        xv = x_ref[...]
        h = _rms_fwd(xv, g_ref[...])[0].astype(BF16)
        h_ref[...] = h
        q = _dot(h, wq_ref[...]).astype(BF16)
        q_ref[...] = q
        for hh in range(N_MEM_HEADS):
            sl = slice(hh * hd, (hh + 1) * hd)
            s = _dot_nt(q[:, sl], k_ref[:, sl]) * (1.0 / 16.0)
            e = jnp.exp(s - jnp.max(s, axis=1, keepdims=True))
            p = e / jnp.sum(e, axis=1, keepdims=True)
            o_ref[:, sl] = _dot(p.astype(BF16), v_ref[:, sl]).astype(BF16)
        x2_ref[...] = xv + _dot(o_ref[...], wo_ref[...])

    tok = pl.BlockSpec((tb, d), lambda i: (i, 0))
    full = pl.BlockSpec((d, d), lambda i: (0, 0))
    return pl.pallas_call(
        body, name="xattn_fwd", grid=(t // tb,),
        in_specs=[tok, pl.BlockSpec((1, d), lambda i: (0, 0)), full,
                  pl.BlockSpec((m, d), lambda i: (0, 0)), pl.BlockSpec((m, d), lambda i: (0, 1)), full],
        out_specs=[tok] * 4,
        out_shape=[jax.ShapeDtypeStruct((t, d), F32)] + [jax.ShapeDtypeStruct((t, d), BF16)] * 3,
        compiler_params=_params(("parallel",), 48),
    )(x1, g, w_q, kv, kv, w_o)


def _xattn_bwd(dx2, x1, g, q, w_q, kv, w_o, *, tb, after=()):
    t, d = x1.shape
    hd = d // N_MEM_HEADS
    m = kv.shape[0]

    def body(dx2_ref, x_ref, g_ref, q_ref, wq_ref, k_ref, v_ref, wo_ref,
             dx1_ref, dq_ref, dk_ref, dv_ref, gg_ref):
        i = pl.program_id(0)

        @pl.when(i == 0)
        def _():
            dk_ref[...] = jnp.zeros_like(dk_ref)
            dv_ref[...] = jnp.zeros_like(dv_ref)

        dx2 = dx2_ref[...]
        do = _dot_nt(dx2.astype(BF16), wo_ref[...]).astype(BF16)
        for hh in range(N_MEM_HEADS):
            sl = slice(hh * hd, (hh + 1) * hd)
            qh, kh, vh, doh = q_ref[:, sl], k_ref[:, sl], v_ref[:, sl], do[:, sl]
            s = _dot_nt(qh, kh) * (1.0 / 16.0)
            e = jnp.exp(s - jnp.max(s, axis=1, keepdims=True))
            p = e / jnp.sum(e, axis=1, keepdims=True)
            dp = _dot_nt(doh, vh)
            ds = (p * (dp - jnp.sum(dp * p, axis=1, keepdims=True)) * (1.0 / 16.0)).astype(BF16)
            dq_ref[:, sl] = _dot(ds, kh).astype(BF16)
            dk_ref[:, sl] += _dot_tn(ds, qh)
            dv_ref[:, sl] += _dot_tn(p.astype(BF16), doh)
        dh = _dot_nt(dq_ref[...], wq_ref[...])
        g_v = g_ref[...]
        _, xh, r = _rms_fwd(x_ref[...], g_v)
        dx1 = dx2 + _rms_bwd(dh, xh, r, g_v)
        dx1_ref[...] = dx1
        part = jnp.sum(dh * xh, axis=0, keepdims=True)

        @pl.when(i == 0)
        def _():
            gg_ref[...] = part

        @pl.when(i != 0)
        def _():
            gg_ref[...] += part

    tok = pl.BlockSpec((tb, d), lambda i: (i, 0))
    full = pl.BlockSpec((d, d), lambda i: (0, 0))
    acc = pl.BlockSpec((m, d), lambda i: (0, 0))
    res, _ = _pcall(
        body, name="xattn_bwd", grid=(t // tb,),
        in_specs=[tok, tok, pl.BlockSpec((1, d), lambda i: (0, 0)), tok, full,
                  pl.BlockSpec((m, d), lambda i: (0, 0)), pl.BlockSpec((m, d), lambda i: (0, 1)), full],
        out_specs=[tok, tok, acc, acc, pl.BlockSpec((1, d), lambda i: (0, 0))],
        out_shape=[jax.ShapeDtypeStruct((t, d), F32), jax.ShapeDtypeStruct((t, d), BF16),
                   jax.ShapeDtypeStruct((m, d), F32), jax.ShapeDtypeStruct((m, d), F32),
                   jax.ShapeDtypeStruct((1, d), F32)],
        semantics=("arbitrary",), vmem_mb=48, after=after,
    )(dx2, x1, g, q, w_q, kv, kv, w_o)
    return res


def _mlp_down_loss(a, w_down, x2, tgt, g, *, tb):
    t, d = x2.shape
    f = a.shape[1]

    def body(a_ref, w_ref, x_ref, t_ref, g_ref, dx_ref, dxb_ref, loss_ref, gg_ref):
        i = pl.program_id(0)
        av = a_ref[...].astype(F32)
        x3 = x_ref[...] + _dot((av * av).astype(BF16), w_ref[...])
        g_v = g_ref[...]
        out, xh, r = _rms_fwd(x3, g_v)
        err = out - t_ref[...]
        dout = err * (1.0 / d)
        dx = _rms_bwd(dout, xh, r, g_v)
        dx_ref[...] = dx
        dxb_ref[...] = dx.astype(BF16)
        part = jnp.sum(dout * xh, axis=0, keepdims=True)
        lpart = 0.5 * jnp.sum(jnp.mean(err * err, axis=-1, keepdims=True), axis=0, keepdims=True)
        lpart = jnp.broadcast_to(lpart, loss_ref.shape)

        @pl.when(i == 0)
        def _():
            gg_ref[...] = part
            loss_ref[...] = lpart

        @pl.when(i != 0)
        def _():
            gg_ref[...] += part
            loss_ref[...] += lpart

    tok = pl.BlockSpec((tb, d), lambda i: (i, 0))
    return pl.pallas_call(
        body, name="mlp_down_loss", grid=(t // tb,),
        in_specs=[pl.BlockSpec((tb, f), lambda i: (i, 0)), pl.BlockSpec((f, d), lambda i: (0, 0)), tok, tok,
                  pl.BlockSpec((1, d), lambda i: (0, 0))],
        out_specs=[tok, tok, pl.BlockSpec((8, 128), lambda i: (0, 0)), pl.BlockSpec((1, d), lambda i: (0, 0))],
        out_shape=[jax.ShapeDtypeStruct((t, d), F32), jax.ShapeDtypeStruct((t, d), BF16),
                   jax.ShapeDtypeStruct((8, 128), F32), jax.ShapeDtypeStruct((1, d), F32)],
        compiler_params=_params(("arbitrary",), 56),
    )(a, w_down, x2, tgt, g)


def _mlp_dpre(dx3, w_down, a, *, tb, bn):
    t, d = dx3.shape
    f = a.shape[1]

    def body(dx_ref, w_ref, a_ref, o_ref):
        o_ref[...] = (2.0 * a_ref[...].astype(F32) * _dot_nt(dx_ref[...], w_ref[...])).astype(BF16)

    return pl.pallas_call(
        body, name="mlp_dpre", grid=(t // tb, f // bn),
        in_specs=[pl.BlockSpec((tb, d), lambda i, j: (i, 0)), pl.BlockSpec((bn, d), lambda i, j: (j, 0)),
                  pl.BlockSpec((tb, bn), lambda i, j: (i, j))],
        out_specs=pl.BlockSpec((tb, bn), lambda i, j: (i, j)),
        out_shape=jax.ShapeDtypeStruct((t, f), BF16),
        compiler_params=_params(("parallel", "arbitrary"), 48),
    )(dx3, w_down, a)


def _adamw(gsum, w, m, v):
    m_new = ADAM_B1 * m + (1.0 - ADAM_B1) * gsum
    v_new = ADAM_B2 * v + (1.0 - ADAM_B2) * (gsum * gsum)
    m_hat = m_new / (1.0 - ADAM_B1 ** ADAM_STEP)
    v_hat = v_new / (1.0 - ADAM_B2 ** ADAM_STEP)
    delta = -ADAM_LR * (m_hat / (jnp.sqrt(v_hat) + ADAM_EPS) + ADAM_WD * w)
    return delta, m_new, v_new


def _sum_adamw(parts, w, m, v, *, name, tr):
    r, c = w.shape

    def body(p_ref, w_ref, m_ref, v_ref, g_ref, d_ref, mo_ref, vo_ref):
        g = p_ref[0].astype(F32)
        for k in range(1, N_DEV):
            g = g + p_ref[k].astype(F32)
        g_ref[...] = g
        d_ref[...], mo_ref[...], vo_ref[...] = _adamw(g, w_ref[...], m_ref[...], v_ref[...])

    blk = pl.BlockSpec((tr, c), lambda i: (i, 0))
    return pl.pallas_call(
        body, name=name, grid=(r // tr,),
        in_specs=[pl.BlockSpec((N_DEV, tr, c), lambda i: (0, i, 0)), blk, blk, blk],
        out_specs=[blk] * 4, out_shape=[jax.ShapeDtypeStruct((r, c), F32)] * 4,
        compiler_params=_params(("parallel",), 40),
    )(*[pltpu.with_memory_space_constraint(a, pltpu.HBM) for a in (parts, w, m, v)])


SMALL_PART = 8
_GAIN_ROWS = ("g_mix", "g_xattn", "g_mem", "g_mlp", "g_final")
_SMALL = _GAIN_ROWS + ("g_attn_out", "g_conv_out", "conv_w")
CONV_SHARD = 512 // N_DEV


def _update_small(parts, me, w, m, v):
    n = len(_SMALL)

    def body(me_ref, p_ref, *refs):
        ins, loss_ref, outs = refs[:3 * n], refs[3 * n], refs[3 * n + 1:]

        def total(i):
            lo = SMALL_PART * i
            s = p_ref[0, lo:lo + SMALL_PART, :]
            for k in range(1, N_DEV):
                s = s + p_ref[k, lo:lo + SMALL_PART, :]
            return s

        grads = {k: total(i)[0:1] for i, k in enumerate(_GAIN_ROWS)}
        both = total(5)[0:1]
        grads["g_attn_out"], grads["g_conv_out"] = both[:, 0:512], both[:, 512:1024]
        taps = total(6)
        mine = jnp.zeros((SMALL_PART, BLK), F32)
        for j in range(N_DEV):
            lo = j * CONV_SHARD // BLK * BLK
            blk = taps[:, lo:lo + BLK]
            if j * CONV_SHARD != lo:
                blk = pltpu.roll(blk, BLK - (j * CONV_SHARD - lo), axis=1)
            mine = jnp.where(me_ref[0] == j, blk, mine)
        grads["conv_w"] = mine[0:3, 0:CONV_SHARD]
        loss_ref[...] = total(7)[0:1, 0:1]
        for i, k in enumerate(_SMALL):
            g_ref, d_ref, mo_ref, vo_ref = outs[4 * i:4 * i + 4]
            g_ref[...] = grads[k]
            d_ref[...], mo_ref[...], vo_ref[...] = _adamw(grads[k], ins[i][...], ins[n + i][...], ins[2 * n + i][...])

    vmem = pl.BlockSpec(memory_space=pltpu.VMEM)
    args = [d[k] for d in (w, m, v) for k in _SMALL]
    res = pl.pallas_call(
        body, name="update_small",
        in_specs=[pl.BlockSpec(memory_space=pltpu.SMEM)] + [vmem] * (1 + 3 * n),
        out_shape=[jax.ShapeDtypeStruct((1, 1), F32)] + [jax.ShapeDtypeStruct(w[k].shape, F32) for k in _SMALL
                                                         for _ in range(4)],
    )(me, parts, *args)
    return res[0], {k: res[1 + 4 * i:5 + 4 * i] for i, k in enumerate(_SMALL)}


def _head_sum_matrix():
    r = lax.broadcasted_iota(jnp.int32, (512, 512), 0) // HEAD_DIM
    c = lax.broadcasted_iota(jnp.int32, (512, 512), 1) // HEAD_DIM
    return (r == c).astype(BF16)


_SHARD_AXIS = dict(w_in=1, w_out=0, w_q=0, w_kv=1, w_o=0, w_up=1, w_down=0, conv_w=None, small=None)


class _Weights:
    def __init__(self, full, shards=None):
        self.full = dict(full)
        self.shards = shards

    def rider(self, names, late=False):
        if self.shards is None:
            return None
        return _Gather([self.shards[n] for n in names], [_SHARD_AXIS[n] for n in names], late)

    def arrived(self, names, gathered):
        if gathered is not None:
            for n, g in zip(names, gathered):
                self.full[n] = g.transpose(1, 0, 2).reshape(g.shape[1], -1) if n == "conv_w" else g

    def __getitem__(self, name):
        return self.full[name]


class _Grads:
    def __init__(self, distributed):
        self.distributed = distributed
        self.local = {}
        self.pending = {}

    def add(self, name, g):
        self.local[name] = g

    def send(self, *names):
        if not self.distributed:
            return []
        rider = _Exchange([self.local[n] for n in names], [_SHARD_AXIS[n] for n in names])
        started = _exchange_start(rider, "send_" + "_".join(names))
        self.pending[names[0]] = (names, rider, started)
        return [started[3]]

    def wait(self, first_name, after):
        names, rider, started = self.pending.pop(first_name)
        return _exchange_wait(rider, started, after, "wait_" + "_".join(names))


def _ride(fn, *args, rider=None, **kw):
    if rider is None:
        return fn(*args, **kw), None
    return fn(*args, rider=rider, **kw)


def _local_step(x, mem, tgt, gains, weights, grads):
    names = ["w_in", "conv_w"]
    (x, tgt), got = _ride(_reorder, [x, tgt], "reorder_in", rider=weights.rider(names, late=True))
    weights.arrived(names, got)
    w_in, cw = weights["w_in"], weights["conv_w"]

    names = ["w_out", "w_kv"]
    (qkv, gates, h1), got = _ride(_proj, x, gains["g_mix"], w_in, tb=1024, rider=weights.rider(names))
    weights.arrived(names, got)
    names = ["w_q", "w_o", "w_up"]
    (attn, *lses), got = _ride(_attention_fwd, qkv, rider=weights.rider(names))
    weights.arrived(names, got)
    x1, merged = _mixer_fwd(x, attn, gates, cw, gains["g_attn_out"], gains["g_conv_out"], weights["w_out"])
    kv, mem_n = _norm_matmul(mem, gains["g_mem"], weights["w_kv"], name="mem_kv", out_dtype=BF16, tb=mem.shape[0],
                             bn=1024, save_h=True)
    x2, h2, qm, om = _xattn_fwd(x1, gains["g_xattn"], weights["w_q"], kv, weights["w_o"], tb=512)
    w_up = weights["w_up"]
    (a, h3), got = _ride(_norm_matmul, x2, gains["g_mlp"], w_up, name="mlp_up", out_dtype=BF16, tb=1024, bn=2048,
                         relu=True, save_h=True, rider=weights.rider(["w_down"], late=True))
    weights.arrived(["w_down"], got)
    w_down = weights["w_down"]
    dx3, dx3b, loss_blk, gg_final = _mlp_down_loss(a, w_down, x2, tgt, gains["g_final"], tb=512)

    dpre = _mlp_dpre(dx3b, w_down, a, tb=1024, bn=2048)
    grads.add("w_down", _matmul_tn(a, dx3b, name="grad_w_down", bm=512, bn=1024, square_a=True))
    sent = grads.send("w_down")
    grads.add("w_up", _matmul_tn(h3, dpre, name="grad_w_up", bm=1024, bn=1024, after=sent))
    sent = grads.send("w_up")
    dx2, dx2b, gg_mlp = _matmul_nt_normbwd(dpre, w_up, x2, gains["g_mlp"], dx3, name="mlp_dx", tb=512,
                                           also_bf16=True, after=sent)

    grads.add("w_o", _matmul_tn(om, dx2b, name="grad_w_o", bm=512, bn=512))
    dx1, dqm, dk, dv, gg_xattn = _xattn_bwd(dx2, x1, gains["g_xattn"], qm, weights["w_q"], kv, weights["w_o"], tb=512)
    grads.add("w_q", _matmul_tn(h2, dqm, name="grad_w_q", bm=1024, bn=512))
    dkv = jnp.concatenate([dk, dv], axis=1).astype(BF16)
    grads.add("w_kv", _matmul_tn(mem_n, dkv, name="grad_w_kv", bm=1024, bn=1024))
    _, gg_mem = _matmul_nt_normbwd(dkv, weights["w_kv"], mem, gains["g_mem"], None, name="mem_dx", tb=mem.shape[0])

    dattn, dsum, dy, gg_attn, gg_conv, gw_out = _mixer_bwd(dx1, merged, attn, gates, cw, gains["g_attn_out"],
                                                           gains["g_conv_out"], weights["w_out"], _head_sum_matrix())
    grads.add("w_out", gw_out)
    sent = grads.send("w_o", "w_q", "w_kv", "w_out")
    dproj, gcw = _conv_bwd(dy, gates, cw, after=sent)
    dproj = _attention_bwd(qkv, dattn, dsum, lses, dproj)
    grads.add("w_in", _matmul_tn(h1, dproj, name="grad_w_in", bm=1024, bn=512))
    sent = grads.send("w_in")
    grad_x, gg_mix = _matmul_nt_normbwd(dproj, w_in, x, gains["g_mix"], dx1, name="mixer_dx", tb=512,
                                        to_natural=True, after=sent)

    def part(v):
        return jnp.pad(v, ((0, SMALL_PART - v.shape[0]), (0, 1024 - v.shape[1])))

    parts = [gg_mix, gg_xattn, gg_mem, gg_mlp, gg_final, jnp.concatenate([gg_attn, gg_conv], axis=1), gcw, loss_blk]
    grads.add("small", jnp.concatenate([part(v) for v in parts], axis=0))
    return grad_x


_BIG = ("w_in", "w_out", "w_q", "w_kv", "w_o", "w_up", "w_down")


def kernel(x, mem, g_mix, w_in, conv_w, g_attn_out, g_conv_out, w_out, g_xattn, g_mem, w_q_mem, w_kv_mem, w_o_mem, g_mlp, w_up, w_down, g_final, loss_target, m_g_mix, m_w_in, m_conv_w, m_g_attn_out, m_g_conv_out, m_w_out, m_g_xattn, m_g_mem, m_w_q_mem, m_w_kv_mem, m_w_o_mem, m_g_mlp, m_w_up, m_w_down, m_g_final, v_g_mix, v_w_in, v_conv_w, v_g_attn_out, v_g_conv_out, v_w_out, v_g_xattn, v_g_mem, v_w_q_mem, v_w_kv_mem, v_w_o_mem, v_g_mlp, v_w_up, v_w_down, v_g_final):
    d = x.shape[-1]
    me = 4 * lax.axis_index("x") + 2 * lax.axis_index("y") + lax.axis_index("c")
    w_shards = dict(w_in=w_in, w_out=w_out, w_q=w_q_mem, w_kv=w_kv_mem, w_o=w_o_mem, w_up=w_up, w_down=w_down)
    m_shards = dict(w_in=m_w_in, w_out=m_w_out, w_q=m_w_q_mem, w_kv=m_w_kv_mem, w_o=m_w_o_mem, w_up=m_w_up,
                    w_down=m_w_down)
    v_shards = dict(w_in=v_w_in, w_out=v_w_out, w_q=v_w_q_mem, w_kv=v_w_kv_mem, w_o=v_w_o_mem, w_up=v_w_up,
                    w_down=v_w_down)
    gains = dict(g_mix=g_mix, g_attn_out=g_attn_out, g_conv_out=g_conv_out, g_xattn=g_xattn, g_mem=g_mem,
                 g_mlp=g_mlp, g_final=g_final)
    gains2 = {k: v.reshape(1, -1) for k, v in gains.items()}

    shards = {k: w_shards[k].astype(BF16) for k in _BIG}
    shards["conv_w"] = conv_w
    grads = _Grads(distributed=True)
    grad_x = _local_step(x[0], mem[0], loss_target[0], gains2, _Weights({}, shards), grads)

    after = grads.send("small")
    outs = {}
    tiles = dict(w_in=256, w_out=128, w_q=128, w_kv=256, w_o=128, w_up=256, w_down=256)
    for group in (("w_down",), ("w_up",), ("w_o", "w_q", "w_kv", "w_out"), ("w_in",)):
        for k, received in zip(group, grads.wait(group[0], after)):
            outs[k] = _sum_adamw(received, w_shards[k], m_shards[k], v_shards[k], name=f"adamw_{k}", tr=tiles[k])
            after = [outs[k][0]]
    small_received, = grads.wait("small", after)

    m_small = dict(g_mix=m_g_mix, g_attn_out=m_g_attn_out, g_conv_out=m_g_conv_out, g_xattn=m_g_xattn,
                   g_mem=m_g_mem, g_mlp=m_g_mlp, g_final=m_g_final)
    v_small = dict(g_mix=v_g_mix, g_attn_out=v_g_attn_out, g_conv_out=v_g_conv_out, g_xattn=v_g_xattn,
                   g_mem=v_g_mem, g_mlp=v_g_mlp, g_final=v_g_final)
    as_rows = lambda vals, conv: dict({k: a.reshape(1, -1) for k, a in vals.items()}, conv_w=conv)
    loss, small_out = _update_small(small_received, me.reshape(1), as_rows(gains, conv_w),
                                    as_rows(m_small, m_conv_w), as_rows(v_small, v_conv_w))
    small_out = {k: [a.reshape(dict(gains, conv_w=conv_w)[k].shape) for a in res] for k, res in small_out.items()}
    names = {"g_mix": "g_mix", "w_in": "w_in", "conv_w": "conv_w", "g_attn_out": "g_attn_out",
             "g_conv_out": "g_conv_out", "w_out": "w_out", "g_xattn": "g_xattn", "g_mem": "g_mem",
             "w_q_mem": "w_q", "w_kv_mem": "w_kv", "w_o_mem": "w_o", "g_mlp": "g_mlp", "w_up": "w_up",
             "w_down": "w_down", "g_final": "g_final"}
    result = [loss.reshape(()), grad_x[None]]
    for which in range(4):
        for key in names.values():
            result.append(outs[key][which] if key in outs else small_out[key][which])
    return tuple(result)
```

```python
import math

import jax
import jax.numpy as jnp
from jax import lax
from jax.experimental import pallas as pl
from jax.experimental.pallas import tpu as pltpu

F32 = jnp.float32
BF16 = jnp.bfloat16
NORM_EPS = 1e-6
NEG_INF = -1e30
N_DEV = 8
BLK = 128
HEAD_DIM = 64
N_MEM_HEADS = 4
ADAM_LR = 0.001
ADAM_B1 = 0.9
ADAM_B2 = 0.999
ADAM_EPS = 1e-08
ADAM_WD = 0.01
ADAM_STEP = 10
MESH = pl.DeviceIdType.MESH
ANY = pl.BlockSpec(memory_space=pl.ANY)


def _dot(a, b):
    return jnp.dot(a, b, preferred_element_type=F32)


def _dot_nt(a, b):
    return lax.dot_general(a, b, (((1,), (1,)), ((), ())), preferred_element_type=F32)


def _dot_tn(a, b):
    return lax.dot_general(a, b, (((0,), (0,)), ((), ())), preferred_element_type=F32)


def _params(semantics, vmem_mb):
    return pltpu.CompilerParams(dimension_semantics=semantics, vmem_limit_bytes=vmem_mb << 20)


def _rms_fwd(x, g):
    r = lax.rsqrt(jnp.mean(x * x, axis=-1, keepdims=True) + NORM_EPS)
    xh = x * r
    return xh * g, xh, r


def _rms_bwd(dy, xh, r, g):
    gy = dy * g
    return r * (gy - xh * jnp.mean(xh * gy, axis=-1, keepdims=True))


def _position():
    x, y, c = lax.axis_index("x"), lax.axis_index("y"), lax.axis_index("c")
    return x, y, c


def _block_of(ref, j, axis, shard_shape):
    r, c = shard_shape
    if axis is None:
        return ref.at[j]
    if axis == 0:
        return ref.at[pl.ds(j * r, r), :]
    return ref.at[:, pl.ds(j * c, c)]


class _Gather:
    has_mid = True
    alias_pairs = ()

    def __init__(self, shards, axes, late=False):
        self.arrays = list(shards)
        self.axes = list(axes)
        self.late = late
        self.n = len(self.arrays)

    def out_shape(self):
        res = []
        for s, axis in zip(self.arrays, self.axes):
            r, c = s.shape
            shape = (N_DEV, r, c) if axis is None else (N_DEV * r, c) if axis == 0 else (r, N_DEV * c)
            res.append(jax.ShapeDtypeStruct(shape, s.dtype))
        return res

    def scratch(self):
        return [pltpu.SemaphoreType.DMA((self.n, 7)), pltpu.SemaphoreType.DMA((self.n, 7)),
                pltpu.SemaphoreType.DMA((self.n,))]

    def _ctx(self, ins, outs, sems):
        send_sems, recv_sems, local_sems = sems
        x, y, c = _position()
        me, sibling = (x, y, c), (x, y, 1 - c)
        chips = [(1 - x, y), (x, 1 - y), (1 - x, 1 - y)]

        def lin(px, py, pc):
            return 4 * px + 2 * py + pc

        def place(a, block):
            return _block_of(outs[a], lin(*block), self.axes[a], self.arrays[a].shape)

        def copy(a, k, block, to, src=None):
            dst = place(a, block)
            return pltpu.make_async_remote_copy(
                src_ref=dst if src is None else src, dst_ref=dst,
                send_sem=send_sems.at[a, k], recv_sem=recv_sems.at[a, k],
                device_id=to, device_id_type=MESH)

        def mine():
            return [pltpu.make_async_copy(ins[a], place(a, me), local_sems.at[a]) for a in range(self.n)]

        def first():
            res = []
            for a in range(self.n):
                res.append(copy(a, 0, me, sibling, src=ins[a]))
                res += [copy(a, 1 + j, me, (*chip, c), src=ins[a]) for j, chip in enumerate(chips)]
            return res

        return c, me, sibling, chips, copy, mine, first

    def start(self, ins, outs, sems):
        _, _, _, _, _, mine, first = self._ctx(ins, outs, sems)
        for cp in mine() + first():
            cp.start()

    def mid(self, ins, outs, sems):
        c, me, sibling, chips, copy, _, _ = self._ctx(ins, outs, sems)
        for j, chip in enumerate(chips):
            for a in range(self.n):
                copy(a, 1 + j, (*chip, c), me).wait_recv()
                copy(a, 4 + j, (*chip, c), sibling).start()

    def finish(self, ins, outs, sems):
        c, me, sibling, chips, copy, mine, first = self._ctx(ins, outs, sems)
        for a in range(self.n):
            copy(a, 0, sibling, me).wait_recv()
            for j, chip in enumerate(chips):
                copy(a, 4 + j, (*chip, 1 - c), me).wait_recv()
        for cp in first():
            cp.wait_send()
        for j, chip in enumerate(chips):
            for a in range(self.n):
                copy(a, 4 + j, (*chip, c), sibling).wait_send()
        for cp in mine():
            cp.wait()


class _Exchange:
    def __init__(self, parts, axes):
        self.n = len(parts)
        self.axes = list(axes)
        self.arrays = list(parts)

    def _piece(self, a):
        r, c = self.arrays[a].shape
        axis = self.axes[a]
        return (r, c) if axis is None else (r // N_DEV, c) if axis == 0 else (r, c // N_DEV)

    def out_shape(self):
        return [jax.ShapeDtypeStruct((N_DEV,) + self._piece(a), self.arrays[a].dtype) for a in range(self.n)]

    def semaphores(self):
        return [pltpu.SemaphoreType.DMA((7 * self.n,)), pltpu.SemaphoreType.DMA((7 * self.n,)),
                pltpu.SemaphoreType.DMA((self.n,))]

    def _ctx(self, ins, outs, sems):
        send_sems, recv_sems, local_sems = sems
        x, y, c = _position()
        me = 4 * x + 2 * y + c

        def src(a, j):
            return ins[a] if self.axes[a] is None else _block_of(ins[a], j, self.axes[a], self._piece(a))

        def dst(a, j):
            return outs[a].at[j]

        def local():
            return [pltpu.make_async_copy(src(a, me), dst(a, me), local_sems.at[a]) for a in range(self.n)]

        def remote(inbound):
            res = []
            for a in range(self.n):
                for k in range(1, N_DEV):
                    peer = (1 - x if k & 4 else x, 1 - y if k & 2 else y, 1 - c if k & 1 else c)
                    plin = 4 * peer[0] + 2 * peer[1] + peer[2]
                    res.append(pltpu.make_async_remote_copy(
                        src_ref=src(a, plin), dst_ref=dst(a, plin if inbound else me),
                        send_sem=send_sems.at[7 * a + k - 1], recv_sem=recv_sems.at[7 * a + k - 1],
                        device_id=peer, device_id_type=MESH))
            return res

        return local, remote

    def start(self, ins, outs, sems):
        local, remote = self._ctx(ins, outs, sems)
        for cp in local() + remote(False):
            cp.start()

    def finish(self, ins, outs, sems):
        local, remote = self._ctx(ins, outs, sems)
        for cp in remote(True):
            cp.wait_recv()
        for cp in remote(False):
            cp.wait_send()
        for cp in local():
            cp.wait()


def _exchange_start(rider, name):
    n = rider.n
    parts = rider.arrays
    lands = [lax.empty(s.shape, s.dtype) for s in rider.out_shape()]
    hbm = pl.BlockSpec(memory_space=pltpu.HBM)
    sem = pl.BlockSpec(memory_space=pltpu.SEMAPHORE)

    def body(*refs):
        ins, sems = refs[:n], refs[2 * n:2 * n + 3]
        outs, token = refs[2 * n + 3 + n:2 * n + 3 + 2 * n], refs[-1]
        rider.start(ins, outs, sems)
        token[...] = jnp.zeros_like(token)

    res = pl.pallas_call(
        body, name=name,
        out_shape=rider.semaphores() + [pltpu.HBM(p.shape, p.dtype) for p in parts]
                  + [pltpu.HBM(z.shape, z.dtype) for z in lands] + [jax.ShapeDtypeStruct((8, 128), F32)],
        in_specs=[hbm] * (2 * n), out_specs=[sem] * 3 + [hbm] * (2 * n) + [pl.BlockSpec(memory_space=pltpu.VMEM)],
        input_output_aliases={i: 3 + i for i in range(2 * n)},
        compiler_params=pltpu.CompilerParams(has_side_effects=pltpu.SideEffectType.DATAFLOW_SIDE_EFFECTING),
    )(*[pltpu.with_memory_space_constraint(a, pltpu.HBM) for a in parts + lands])
    return res[:3], res[3:3 + n], res[3 + n:3 + 2 * n], res[-1]


def _exchange_wait(rider, started, after, name):
    n = rider.n
    sems, parts, lands, _ = started
    hbm = pl.BlockSpec(memory_space=pltpu.HBM)
    sem = pl.BlockSpec(memory_space=pltpu.SEMAPHORE)

    def body(*refs):
        rider.finish(refs[:n], refs[n:2 * n], refs[2 * n:2 * n + 3])

    res = pl.pallas_call(
        body, name=name, out_shape=[pltpu.HBM(a.shape, a.dtype) for a in list(parts) + list(lands)],
        in_specs=[hbm] * (2 * n) + [sem] * 3 + [ANY] * len(after), out_specs=[hbm] * (2 * n),
        input_output_aliases={i: i for i in range(2 * n)},
        compiler_params=pltpu.CompilerParams(has_side_effects=pltpu.SideEffectType.DATAFLOW_SIDE_EFFECTING),
    )(*parts, *lands, *sems, *after)
    return list(res[n:])


def _pcall(body, *, name, grid, in_specs, out_specs, out_shape, scratch_shapes=(), semantics, vmem_mb, rider=None,
           aliases=None, after=()):
    in_specs, out_specs, out_shape = list(in_specs), list(out_specs), list(out_shape)
    scratch_shapes = list(scratch_shapes)
    aliases = dict(aliases or {})
    if rider is None:
        n_in, after = len(in_specs), list(after)

        def plain(*refs):
            body(*refs[:n_in], *refs[n_in + len(after):])

        call = pl.pallas_call(plain if after else body, name=name, grid=grid, in_specs=in_specs + [ANY] * len(after),
                              out_specs=out_specs, out_shape=out_shape, scratch_shapes=scratch_shapes,
                              input_output_aliases=aliases, compiler_params=_params(semantics, vmem_mb))
        return lambda *args: (list(call(*args, *after)), None)
    n_in, n_out, n_scr = len(in_specs), len(out_specs), len(scratch_shapes)
    r_in, r_shapes = len(rider.arrays), rider.out_shape()
    r_out = len(r_shapes)
    aliases.update({n_in + i: n_out + o for i, o in rider.alias_pairs})
    total = math.prod(grid)
    mid_step = total - 1 if rider.has_mid and rider.late else (3 * total) // 4

    def wrapped(*refs):
        bounds = [0, n_in, r_in, n_out, r_out, n_scr]
        for i in range(1, len(bounds)):
            bounds[i] += bounds[i - 1]
        a, ra, o, ro, s = (refs[bounds[i]:bounds[i + 1]] for i in range(5))
        rs = refs[bounds[5]:]
        step = pl.program_id(0)
        for k in range(1, len(grid)):
            step = step * grid[k] + pl.program_id(k)
        pl.when(step == 0)(lambda: rider.start(ra, ro, rs))
        body(*a, *o, *s)
        if rider.has_mid:
            pl.when(step == mid_step)(lambda: rider.mid(ra, ro, rs))
        pl.when(step == total - 1)(lambda: rider.finish(ra, ro, rs))

    call = pl.pallas_call(
        wrapped, name=name, grid=grid, in_specs=in_specs + [ANY] * r_in, out_specs=out_specs + [ANY] * r_out,
        out_shape=out_shape + r_shapes, scratch_shapes=scratch_shapes + rider.scratch(),
        input_output_aliases=aliases, compiler_params=_params(("arbitrary",) * len(grid), vmem_mb))

    def run(*args):
        res = call(*args, *rider.arrays)
        return list(res[:n_out]), list(res[n_out:])

    return run


def _norm_matmul(x, g, w, *, name, out_dtype, tb, bn, relu=False, save_h=False, rider=None):
    t, d = x.shape
    n = w.shape[1]

    def body(x_ref, g_ref, w_ref, o_ref, *rest):
        h_scr = rest[-1]

        @pl.when(pl.program_id(1) == 0)
        def _():
            h = _rms_fwd(x_ref[...], g_ref[...])[0].astype(BF16)
            h_scr[...] = h
            if save_h:
                rest[0][...] = h

        acc = _dot(h_scr[...], w_ref[...])
        if relu:
            acc = jnp.maximum(acc, 0.0)
        o_ref[...] = acc.astype(out_dtype)

    out_shape = [jax.ShapeDtypeStruct((t, n), out_dtype)]
    out_specs = [pl.BlockSpec((tb, bn), lambda i, j: (i, j))]
    if save_h:
        out_shape.append(jax.ShapeDtypeStruct((t, d), BF16))
        out_specs.append(pl.BlockSpec((tb, d), lambda i, j: (i, 0)))
    res, extra = _pcall(
        body, name=name, grid=(t // tb, n // bn),
        in_specs=[pl.BlockSpec((tb, d), lambda i, j: (i, 0)),
                  pl.BlockSpec((1, d), lambda i, j: (0, 0)),
                  pl.BlockSpec((d, bn), lambda i, j: (0, j))],
        out_specs=out_specs, out_shape=out_shape,
        scratch_shapes=[pltpu.VMEM((tb, d), BF16)],
        semantics=("parallel", "arbitrary"), vmem_mb=48, rider=rider,
    )(x, g, w)
    res = res if save_h else res[0]
    return res if rider is None else (res, extra)


def _proj(x, g, w, *, tb, rider=None):
    t, d = x.shape
    half = w.shape[1] // 2

    def body(x_ref, g_ref, w_ref, qkv_ref, gates_ref, h_ref, h_scr):
        j = pl.program_id(1)

        @pl.when(j == 0)
        def _():
            h = _rms_fwd(x_ref[...], g_ref[...])[0].astype(BF16)
            h_scr[...] = h
            h_ref[...] = h

        acc = _dot(h_scr[...], w_ref[...])

        @pl.when(j == 0)
        def _():
            qkv_ref[...] = acc

        @pl.when(j == 1)
        def _():
            gates_ref[...] = acc.astype(BF16)

    tok = lambda c: pl.BlockSpec((tb, c), lambda i, j: (i, 0))
    res, extra = _pcall(
        body, name="proj", grid=(t // tb, 2),
        in_specs=[tok(d), pl.BlockSpec((1, d), lambda i, j: (0, 0)), pl.BlockSpec((d, half), lambda i, j: (0, j))],
        out_specs=[tok(half), tok(half), tok(d)],
        out_shape=[jax.ShapeDtypeStruct((t, half), F32), jax.ShapeDtypeStruct((t, half), BF16),
                   jax.ShapeDtypeStruct((t, d), BF16)],
        scratch_shapes=[pltpu.VMEM((tb, d), BF16)],
        semantics=("parallel", "arbitrary"), vmem_mb=48, rider=rider,
    )(x, g, w)
    return res if rider is None else (res, extra)


def _matmul_nt_normbwd(dy, w, x, g, dres, *, name, tb, also_bf16=False, to_natural=False, after=()):
    t, d = x.shape
    stacked = dy.ndim == 3
    has_res = dres is not None
    n_i = SEG // TI
    if to_natural:
        tb = N_RES * TI

    def body(dy_ref, w_ref, x_ref, g_ref, *rest):
        rest = list(rest)
        dres_ref = rest.pop(0) if has_res else None
        dx_ref = rest.pop(0)
        dxb_ref = rest.pop(0) if also_bf16 else None
        gg_ref = rest.pop(0)
        i = pl.program_id(0)

        def rows(ref, *lead):
            v = ref[lead] if lead else ref[...]
            return v[0].reshape(tb, v.shape[-1]) if to_natural else v

        if stacked:
            kb = dy_ref.shape[-1]
            dh = _dot_nt(rows(dy_ref, 0), w_ref[:, 0:kb])
            for s in range(1, dy_ref.shape[0]):
                dh = dh + _dot_nt(rows(dy_ref, s), w_ref[:, s * kb:(s + 1) * kb])
        else:
            dh = _dot_nt(rows(dy_ref), w_ref[...])
        g_v = g_ref[...]
        _, xh, r = _rms_fwd(rows(x_ref), g_v)
        dx = _rms_bwd(dh, xh, r, g_v)
        if has_res:
            dx = dx + rows(dres_ref)
        if to_natural:
            scr = rest.pop(0)
            for cb in range(d // BLK):
                cols = slice(cb * BLK, (cb + 1) * BLK)
                slab = scr.at[cb]
                for res in range(N_RES):
                    slab[pl.ds(res, TI, stride=N_RES), :] = dx[res * TI:(res + 1) * TI, cols]
                dx_ref[:, cols] = slab[...]
        else:
            dx_ref[...] = dx
        if also_bf16:
            dxb_ref[...] = dx.astype(BF16)
        part = jnp.sum(dh * xh, axis=0, keepdims=True)

        @pl.when(i == 0)
        def _():
            gg_ref[...] = part

        @pl.when(i != 0)
        def _():
            gg_ref[...] += part

    tok = pl.BlockSpec((tb, d), lambda i: (i, 0))
    row = pl.BlockSpec((1, d), lambda i: (0, 0))
    if to_natural:
        act = pl.BlockSpec((1, N_RES, TI, d), lambda i: (i // n_i, 0, i % n_i, 0))
        dy_spec = pl.BlockSpec((dy.shape[0], 1, N_RES, TI, dy.shape[2]), lambda i: (0, i // n_i, 0, i % n_i, 0))
        dy, x = dy.reshape(dy.shape[0], t // HALF, N_RES, SEG, dy.shape[2]), _x4(x)
        dres = _x4(dres) if has_res else None
    elif stacked:
        act, dy_spec = tok, pl.BlockSpec((dy.shape[0], tb, dy.shape[2]), lambda i: (0, i, 0))
    else:
        act, dy_spec = tok, pl.BlockSpec((tb, dy.shape[1]), lambda i: (i, 0))
    in_specs = [dy_spec, pl.BlockSpec(w.shape, lambda i: (0, 0)), act, row]
    args = [dy, w, x, g]
    if has_res:
        in_specs.append(act)
        args.append(dres)
    out_specs = [tok] + ([tok] if also_bf16 else []) + [row]
    out_shape = ([jax.ShapeDtypeStruct((t, d), F32)] + ([jax.ShapeDtypeStruct((t, d), BF16)] if also_bf16 else [])
                 + [jax.ShapeDtypeStruct((1, d), F32)])
    res, _ = _pcall(
        body, name=name, grid=(t // tb,), in_specs=in_specs, out_specs=out_specs, out_shape=out_shape,
        scratch_shapes=[pltpu.VMEM((d // BLK, tb, BLK), F32)] if to_natural else [],
        semantics=("arbitrary",), vmem_mb=56, after=after,
    )(*args)
    return res


def _matmul_tn(a, b, *, name, bm, bn, square_a=False, after=()):
    t, m = a.shape
    stacked = b.ndim == 3
    n = b.shape[0] * bn if stacked else b.shape[1]

    def body(a_ref, b_ref, o_ref):
        av = a_ref[...]
        if square_a:
            av = av.astype(F32)
            av = (av * av).astype(BF16)
        o_ref[...] = _dot_tn(av, b_ref[...]).astype(BF16)

    res, _ = _pcall(
        body, name=name, grid=(m // bm, n // bn),
        in_specs=[pl.BlockSpec((t, bm), lambda i, j: (0, i)),
                  pl.BlockSpec((None, t, bn), lambda i, j: (j, 0, 0)) if stacked
                  else pl.BlockSpec((t, bn), lambda i, j: (0, j))],
        out_specs=[pl.BlockSpec((bm, bn), lambda i, j: (i, j))], out_shape=[jax.ShapeDtypeStruct((m, n), BF16)],
        semantics=("parallel", "parallel"), vmem_mb=56, after=after,
    )(a, b)
    return res[0]


N_RES = 16
SEG = 128
HALF = N_RES * SEG
TI = 32
HALO = 16


def _x4(a):
    return a.reshape(a.shape[0] // HALF, N_RES, SEG, a.shape[1])


def _reorder(arrays, name, rider=None):
    t, c = arrays[0].shape
    n = len(arrays)
    n_i = SEG // TI

    def body(*refs):
        scr = refs[-1]
        for i_ref, o_ref in zip(refs[:n], refs[n:2 * n]):
            for cb in range(c // BLK):
                cols = slice(cb * BLK, (cb + 1) * BLK)
                slab = scr.at[cb]
                slab[...] = i_ref[:, cols]
                for r in range(N_RES):
                    o_ref[0, r, :, cols] = slab[pl.ds(r, TI, stride=N_RES), :]

    res, extra = _pcall(
        body, name=name, grid=(t // (TI * N_RES),),
        in_specs=[pl.BlockSpec((TI * N_RES, c), lambda s: (s, 0))] * n,
        out_specs=[pl.BlockSpec((1, N_RES, TI, c), lambda s: (s // n_i, 0, s % n_i, 0))] * n,
        out_shape=[jax.ShapeDtypeStruct((t // HALF, N_RES, SEG, c), F32)] * n,
        scratch_shapes=[pltpu.VMEM((c // BLK, TI * N_RES, BLK), F32)],
        semantics=("parallel",), vmem_mb=32, rider=rider,
    )(*arrays)
    res = [r.reshape(t, c) for r in res]
    return res if rider is None else (res, extra)


_PATTERNS = ((1, 16, 8, SEG), (4, 4, 32, 4 * SEG), (16, 1, SEG, 0))
_FIRST = {1: 1, 4: 4, 16: 16}


def _group_rows(d, g):
    a = g >> 4
    if d == 16:
        base = a * HALF + (g & 15) * SEG
        prev = base - HALF
    elif d == 4:
        c = (g >> 2) & 3
        base = a * HALF + (g & 3) * SEG + c * 32
        prev = jnp.where(c > 0, base - 32, base - HALF + 96)
    else:
        c = g & 15
        base = a * HALF + c * 8
        prev = jnp.where(c > 0, base - 8, base - HALF + 120)
    return base, prev


def _load_rows(ref, base, n, rows, stride):
    parts = [ref[pl.ds(pl.multiple_of(base + j * stride, 8), rows), :] for j in range(n)]
    return parts[0] if n == 1 else jnp.concatenate(parts, axis=0)


def _store_rows(ref, base, val, n, rows, stride, add=False):
    for j in range(n):
        sl = pl.ds(pl.multiple_of(base + j * stride, 8), rows)
        piece = val[j * rows:(j + 1) * rows, :]
        if add:
            ref[sl, :] += piece
        else:
            ref[sl, :] = piece


def _band_bias(n, rows):
    shift = rows.bit_length() - 1
    lq = lax.broadcasted_iota(jnp.int32, (BLK, BLK), 0)
    lk = lax.broadcasted_iota(jnp.int32, (BLK, BLK), 1)
    iq = (lq & (rows - 1)) * n + (lq >> shift)
    ik = (lk & (rows - 1)) * n + (lk >> shift)
    zero = jnp.zeros((BLK, BLK), F32)
    return jnp.where(ik >= iq, zero, NEG_INF), jnp.where(ik <= iq, zero, NEG_INF)


def _set_bias(bias_scr, n, rows):
    prev_b, cur_b = _band_bias(n, rows)
    for half in range(2):
        bias_scr[half * BLK:(half + 1) * BLK, 0:BLK] = prev_b
        bias_scr[half * BLK:(half + 1) * BLK, BLK:2 * BLK] = cur_b


SCALE = 1.0 / math.sqrt(HEAD_DIM)


def _head_consts(value=1.0):
    lane_lo = lax.broadcasted_iota(jnp.int32, (BLK, BLK), 1) < HEAD_DIM
    return lane_lo, [jnp.where(lane_lo, value, 0.0).astype(BF16), jnp.where(lane_lo, 0.0, value).astype(BF16)]


def _stack_heads(v, head_mask):
    return jnp.concatenate([v * head_mask[0], v * head_mask[1]], axis=0)


def _unstack_heads(v2, lane_lo):
    return jnp.where(lane_lo, v2[:BLK], v2[BLK:])


def _rows_per_head(v, lane_lo):
    rolled = pltpu.roll(v, HEAD_DIM, axis=1)
    return jnp.concatenate([jnp.where(lane_lo, v, rolled), jnp.where(lane_lo, rolled, v)], axis=0)


WIDTH = 4


def _loop(lo, hi, fn, width=None):
    if width is None:
        def body(g, carry):
            fn(g)
            return carry

        if hi > lo:
            lax.fori_loop(lo, hi, body, 0)
        return
    while hi > lo:
        trips = (hi - lo) // width
        if trips:
            def body(i, carry, lo=lo, width=width):
                fn([lo + width * i + j for j in range(width)])
                return carry

            lax.fori_loop(0, trips, body, 0)
            lo += trips * width
        width = max(1, width // 2)


def _mix_weights(l1, l2, l3):
    mx = jnp.maximum(jnp.maximum(l1, l2), l3)
    e1, e2, e3 = jnp.exp(l1 - mx), jnp.exp(l2 - mx), jnp.exp(l3 - mx)
    inv = 1.0 / (e1 + e2 + e3)
    return e1 * inv, e2 * inv, e3 * inv


def _attention_fwd(qkv, rider=None):
    t = qkv.shape[0]
    groups = 16 * (t // HALF)

    def body(q_ref, k_ref, v_ref, attn_ref, l1_ref, l2_ref, l3_ref, o_scr, bias_scr):
        lane_lo, q_mask = _head_consts(SCALE)
        l_refs = (l1_ref, l2_ref, l3_ref)
        for p, (d, n, rows, stride) in enumerate(_PATTERNS):
            _set_bias(bias_scr, n, rows)
            o_p, l_p = o_scr.at[p], l_refs[p]

            def block(gs, has_prev):
                at = [_group_rows(d, g) for g in gs]

                def load(ref, b):
                    return _load_rows(ref, b, n, rows, stride).astype(BF16)

                q2 = [_stack_heads(load(q_ref, b), q_mask) for b, _ in at]
                k2 = [load(k_ref, b) for b, _ in at]
                v2 = [load(v_ref, b) for b, _ in at]
                if has_prev:
                    k2 = [jnp.concatenate([load(k_ref, pv), k], axis=0) for (_, pv), k in zip(at, k2)]
                    v2 = [jnp.concatenate([load(v_ref, pv), v], axis=0) for (_, pv), v in zip(at, v2)]
                s = [_dot_nt(q, k) for q, k in zip(q2, k2)]
                s = [x + (bias_scr[...] if has_prev else bias_scr[:, BLK:2 * BLK]) for x in s]
                mx = [jnp.max(x, axis=1, keepdims=True) for x in s]
                e = [jnp.exp(x - m) for x, m in zip(s, mx)]
                den = [jnp.sum(x, axis=1, keepdims=True) for x in e]
                o2 = [_dot(x.astype(BF16), v) * (1.0 / dn) for x, v, dn in zip(e, v2, den)]
                lse2 = [jnp.broadcast_to(m + jnp.log(dn), (2 * BLK, BLK)) for m, dn in zip(mx, den)]
                for (b, _), o, l in zip(at, o2, lse2):
                    _store_rows(o_p, b, _unstack_heads(o, lane_lo), n, rows, stride)
                    _store_rows(l_p, b, _unstack_heads(l, lane_lo), n, rows, stride)

            _loop(0, _FIRST[d], lambda gs: block(gs, False), width=2 * WIDTH)
            _loop(_FIRST[d], groups, lambda gs: block(gs, True), width=2 * WIDTH)

        def mix(i):
            sl = pl.ds(pl.multiple_of(i * 256, 256), 256)
            w = _mix_weights(l1_ref[sl, :], l2_ref[sl, :], l3_ref[sl, :])
            attn_ref[sl, :] = w[0] * o_scr[0, sl, :] + w[1] * o_scr[1, sl, :] + w[2] * o_scr[2, sl, :]

        _loop(0, t // 256, mix)

    def col(c0):
        return pl.BlockSpec((t, BLK), lambda hp: (0, c0 + hp))

    res, extra = _pcall(
        body, name="attention_fwd", grid=(4,), in_specs=[col(0), col(4), col(8)], out_specs=[col(0)] * 4,
        out_shape=[jax.ShapeDtypeStruct((t, 512), F32)] * 4,
        scratch_shapes=[pltpu.VMEM((3, t, BLK), F32), pltpu.VMEM((2 * BLK, 2 * BLK), F32)],
        semantics=("parallel",), vmem_mb=48, rider=rider,
    )(qkv, qkv, qkv)
    return res if rider is None else (res, extra)


def _attention_bwd(qkv, dattn, dsum, lses, dproj):
    t = qkv.shape[0]
    groups = 16 * (t // HALF)

    def body(q_ref, k_ref, v_ref, da_ref, ds_ref, l1_ref, l2_ref, l3_ref, kept_ref, out_ref, acc, bias_scr):
        del kept_ref
        lane_lo, head_mask = _head_consts()
        q_mask = _head_consts(SCALE)[1]
        l_refs = (l1_ref, l2_ref, l3_ref)

        def clear(i):
            sl = pl.ds(pl.multiple_of(i * 512, 512), 512)
            for s in range(3):
                acc[s, sl, :] = jnp.zeros((512, BLK), F32)

        _loop(0, t // 512, clear)
        dq_acc, dk_acc, dv_acc = acc.at[0], acc.at[1], acc.at[2]
        for p, (d, n, rows, stride) in enumerate(_PATTERNS):
            _set_bias(bias_scr, n, rows)

            def block(gs, has_prev):
                at = [_group_rows(d, g) for g in gs]

                def load(ref, b):
                    return _load_rows(ref, b, n, rows, stride)

                def put(ref, b, val):
                    _store_rows(ref, b, val, n, rows, stride, add=True)

                def wide(x):
                    return jnp.concatenate([x, x], axis=1) if has_prev else x

                lse = [[load(ref, b) for ref in l_refs] for b, _ in at]
                w = [_mix_weights(*ls)[p] for ls in lse]
                do2 = [_stack_heads((wg * load(da_ref, b)).astype(BF16), head_mask) for wg, (b, _) in zip(w, at)]
                dl2 = [wide(_rows_per_head(wg * load(ds_ref, b), lane_lo)) for wg, (b, _) in zip(w, at)]
                lse2 = [wide(_rows_per_head(ls[p], lane_lo)) for ls in lse]
                q2 = [_stack_heads(load(q_ref, b).astype(BF16), q_mask) for b, _ in at]
                k2 = [load(k_ref, b).astype(BF16) for b, _ in at]
                v2 = [load(v_ref, b).astype(BF16) for b, _ in at]
                if has_prev:
                    k2 = [jnp.concatenate([load(k_ref, pv).astype(BF16), k], axis=0) for (_, pv), k in zip(at, k2)]
                    v2 = [jnp.concatenate([load(v_ref, pv).astype(BF16), v], axis=0) for (_, pv), v in zip(at, v2)]
                s = [_dot_nt(q, k) for q, k in zip(q2, k2)]
                dp = [_dot_nt(do, v) for do, v in zip(do2, v2)]
                pr = [jnp.exp(x + (bias_scr[...] if has_prev else bias_scr[:, BLK:2 * BLK]) - l)
                      for x, l in zip(s, lse2)]
                ds = [(pg * (x - dl)).astype(BF16) for pg, x, dl in zip(pr, dp, dl2)]
                dq2 = [_dot(x, k) * SCALE for x, k in zip(ds, k2)]
                dk2 = [_dot_tn(x, q) for x, q in zip(ds, q2)]
                dv2 = [_dot_tn(pg.astype(BF16), do) for pg, do in zip(pr, do2)]
                for (b, pv), dq, dk, dv in zip(at, dq2, dk2, dv2):
                    put(dq_acc, b, _unstack_heads(dq, lane_lo))
                    if has_prev:
                        put(dk_acc, pv, dk[:BLK])
                        put(dv_acc, pv, dv[:BLK])
                        put(dk_acc, b, dk[BLK:])
                        put(dv_acc, b, dv[BLK:])
                    else:
                        put(dk_acc, b, dk)
                        put(dv_acc, b, dv)

            _loop(0, _FIRST[d], lambda gs: block(gs, False), width=WIDTH)
            _loop(_FIRST[d], groups, lambda gs: block(gs, True), width=WIDTH)

        def emit(i):
            sl = pl.ds(pl.multiple_of(i * 512, 512), 512)
            for s in range(3):
                out_ref[s, sl, :] = acc[s, sl, :].astype(BF16)

        _loop(0, t // 512, emit)

    def col(c0):
        return pl.BlockSpec((t, BLK), lambda hp: (0, c0 + hp))

    res, _ = _pcall(
        body, name="attention_bwd", grid=(4,),
        in_specs=[col(0), col(4), col(8)] + [col(0)] * 5 + [ANY],
        out_specs=[pl.BlockSpec((3, t, BLK), lambda hp: (0, 0, hp))],
        out_shape=[jax.ShapeDtypeStruct(dproj.shape, BF16)],
        scratch_shapes=[pltpu.VMEM((3, t, BLK), F32), pltpu.VMEM((2 * BLK, 2 * BLK), F32)],
        semantics=("parallel",), vmem_mb=56, aliases={8: 0},
    )(qkv, qkv, qkv, dattn, dsum, *lses, dproj)
    return res[0]


def _order_specs(t):
    n_i = SEG // TI
    nblk = (t // HALF) * n_i
    per = TI // HALO

    def main(c, col=0):
        return pl.BlockSpec((1, N_RES, TI, c), lambda s: (s // n_i, 0, s % n_i, col))

    def before(c, col=0):
        return pl.BlockSpec((1, 2, HALO, c), lambda s: (jnp.maximum(s - 1, 0) // n_i, N_RES // 2 - 1,
                                                        (jnp.maximum(s - 1, 0) % n_i) * per + per - 1, col))

    def after(c, col=0):
        return pl.BlockSpec((1, 2, HALO, c), lambda s: (jnp.minimum(s + 1, nblk - 1) // n_i, 0,
                                                        (jnp.minimum(s + 1, nblk - 1) % n_i) * per, col))

    return nblk, main, before, after


def _shift_in(v, row_in, up):
    rows = v.shape[0]
    idx = lax.broadcasted_iota(jnp.int32, v.shape, 0)
    fill = jnp.broadcast_to(row_in, v.shape)
    if up:
        return jnp.where(idx == rows - 1, fill, pltpu.roll(v, rows - 1, axis=0))
    return jnp.where(idx == 0, fill, pltpu.roll(v, 1, axis=0))


def _taps_behind(u, before):
    s15 = _shift_in(u[N_RES - 1], before[1, HALO - 1:HALO, :], up=False)
    s14 = _shift_in(u[N_RES - 2], before[0, HALO - 1:HALO, :], up=False)
    m1 = jnp.concatenate([s15[None], u[:N_RES - 1]], axis=0)
    m2 = jnp.concatenate([s14[None], s15[None], u[:N_RES - 2]], axis=0)
    return m1, m2


def _taps_ahead(u, after):
    t0 = _shift_in(u[0], after[0, 0:1, :], up=True)
    t1 = _shift_in(u[1], after[1, 0:1, :], up=True)
    p1 = jnp.concatenate([u[1:], t0[None]], axis=0)
    p2 = jnp.concatenate([u[2:], t0[None], t1[None]], axis=0)
    return p1, p2


def _conv_fwd(gates, before, first, cw):
    gates, before = gates.astype(F32), before.astype(F32)
    bg, cg, xc = gates[..., 0:512], gates[..., 512:1024], gates[..., 1024:1536]
    u = cg * xc
    ub = before[..., 512:1024] * before[..., 1024:1536]
    ub = jnp.where(first, jnp.zeros_like(ub), ub)
    m1, m2 = _taps_behind(u, ub)
    conv = m2 * cw[0:1, :] + m1 * cw[1:2, :] + u * cw[2:3, :]
    return bg, u, m1, m2, conv


def _sum_tokens(v):
    return jnp.sum(jnp.sum(v, axis=0), axis=0, keepdims=True)


def _mixer_fwd(x, attn, gates, cw, g_a, g_c, w_out):
    t, d = x.shape
    nblk, main, before, _ = _order_specs(t)
    rows = N_RES * TI

    def body(x_ref, at_ref, gt_ref, gb_ref, cw_ref, ga_ref, gc_ref, wa_ref, wb_ref, x1_ref, mg_ref):
        an = _rms_fwd(at_ref[0], ga_ref[...])[0].astype(BF16)
        bg, _, _, _, conv = _conv_fwd(gt_ref[0], gb_ref[0], pl.program_id(0) == 0, cw_ref[...])
        cn = _rms_fwd(bg * conv, gc_ref[...])[0].astype(BF16)
        mg_ref[0, :, :, 0:512] = an
        mg_ref[0, :, :, 512:1024] = cn
        y = _dot(an.reshape(rows, 512), wa_ref[...]) + _dot(cn.reshape(rows, 512), wb_ref[...])
        x1_ref[0] = x_ref[0] + y.reshape(N_RES, TI, d)

    const = lambda r, c, i0=0: pl.BlockSpec((r, c), lambda s: (i0, 0))
    x1, merged = pl.pallas_call(
        body, name="mixer_fwd", grid=(nblk,),
        in_specs=[main(d), main(512), main(1536), before(1536), const(3, 512), const(1, 512), const(1, 512),
                  const(512, d), const(512, d, 1)],
        out_specs=[main(d), main(d)],
        out_shape=[jax.ShapeDtypeStruct(_x4(x).shape, F32), jax.ShapeDtypeStruct(_x4(x).shape, BF16)],
        compiler_params=_params(("parallel",), 48),
    )(_x4(x), _x4(attn), _x4(gates), _x4(gates), cw, g_a, g_c, w_out, w_out)
    return x1.reshape(t, d), merged.reshape(t, d)


def _mixer_bwd(dx1, merged, attn, gates, cw, g_a, g_c, w_out, head_sum, after=()):
    t, d = dx1.shape
    nblk, main, before, _ = _order_specs(t)
    rows = N_RES * TI

    def body(dx_ref, mg_ref, at_ref, gt_ref, gb_ref, cw_ref, ga_ref, gc_ref, wa_ref, wb_ref, hs_ref,
             da_ref, dsum_ref, dy_ref, gga_ref, ggc_ref, gw_ref, acc_w):
        s = pl.program_id(0)
        dxb = dx_ref[0].reshape(rows, d).astype(BF16)

        @pl.when(s == 0)
        def _():
            acc_w[...] = jnp.zeros_like(acc_w)

        acc_w[...] += _dot_tn(mg_ref[0].reshape(rows, d), dxb)

        @pl.when(s == nblk - 1)
        def _():
            gw_ref[...] = acc_w[...].astype(BF16)

        dma = _dot_nt(dxb, wa_ref[...]).reshape(N_RES, TI, 512)
        dmc = _dot_nt(dxb, wb_ref[...]).reshape(N_RES, TI, 512)
        attn_v, g_av = at_ref[0], ga_ref[...]
        _, ah, ra = _rms_fwd(attn_v, g_av)
        dattn = _rms_bwd(dma, ah, ra, g_av)
        da_ref[0] = dattn
        z = (dattn * attn_v).reshape(rows, 512)
        hs = hs_ref[...]
        z1 = z.astype(BF16)
        z2 = (z - z1.astype(F32)).astype(BF16)
        dsum_ref[0] = (_dot(z1, hs) + _dot(z2, hs)).reshape(N_RES, TI, 512)
        bg, _, _, _, conv = _conv_fwd(gt_ref[0], gb_ref[0], s == 0, cw_ref[...])
        g_cv = gc_ref[...]
        _, yh, rc = _rms_fwd(bg * conv, g_cv)
        dy_ref[0] = _rms_bwd(dmc, yh, rc, g_cv)
        pa, pc = _sum_tokens(dma * ah), _sum_tokens(dmc * yh)

        @pl.when(s == 0)
        def _():
            gga_ref[...] = pa
            ggc_ref[...] = pc

        @pl.when(s != 0)
        def _():
            gga_ref[...] += pa
            ggc_ref[...] += pc

    const = lambda r, c, i0=0: pl.BlockSpec((r, c), lambda s: (i0, 0))
    shape4 = _x4(attn).shape
    res, _ = _pcall(
        body, name="mixer_bwd", grid=(nblk,),
        in_specs=[main(d), main(d), main(512), main(1536), before(1536), const(3, 512), const(1, 512), const(1, 512),
                  const(512, d), const(512, d, 1), const(512, 512)],
        out_specs=[main(512)] * 3 + [const(1, 512), const(1, 512), const(d, d)],
        out_shape=[jax.ShapeDtypeStruct(shape4, F32)] * 3 + [jax.ShapeDtypeStruct((1, 512), F32)] * 2
        + [jax.ShapeDtypeStruct((d, d), BF16)],
        scratch_shapes=[pltpu.VMEM((d, d), F32)],
        semantics=("arbitrary",), vmem_mb=48, after=after,
    )(_x4(dx1), _x4(merged), _x4(attn), _x4(gates), _x4(gates), cw, g_a, g_c, w_out, w_out, head_sum)
    return [r.reshape(t, 512) for r in res[:3]] + res[3:]


def _conv_bwd(dy, gates, cw, after=()):
    t = dy.shape[0]
    nblk, main, before, ahead = _order_specs(t)
    n_i = SEG // TI

    def body(dy_ref, dya_ref, gt_ref, gb_ref, ga_ref, cw_ref, dp_ref, gcw_ref):
        s = pl.program_id(0)
        cw_v, gates_v = cw_ref[...], gt_ref[0]
        bg, u, m1, m2, conv = _conv_fwd(gates_v, gb_ref[0], s == 0, cw_v)
        dy_v = dy_ref[0]
        dconv = dy_v * bg
        dca = dya_ref[0] * ga_ref[0][..., 0:512].astype(F32)
        dca = jnp.where(s == nblk - 1, jnp.zeros_like(dca), dca)
        p1, p2 = _taps_ahead(dconv, dca)
        du = dconv * cw_v[2:3, :] + p1 * cw_v[1:2, :] + p2 * cw_v[0:1, :]
        dp_ref[0, 0] = (dy_v * conv).astype(BF16)
        dp_ref[1, 0] = (du * gates_v[..., 1024:1536].astype(F32)).astype(BF16)
        dp_ref[2, 0] = (du * gates_v[..., 512:1024].astype(F32)).astype(BF16)
        parts = [_sum_tokens(dconv * m2), _sum_tokens(dconv * m1), _sum_tokens(dconv * u)]

        @pl.when(s == 0)
        def _():
            gcw_ref[...] = jnp.zeros_like(gcw_ref)

        for tap in range(3):
            gcw_ref[tap:tap + 1, :] += parts[tap]

    (dproj, gcw), _ = _pcall(
        body, name="conv_bwd", grid=(nblk,),
        in_specs=[main(512), ahead(512), main(1536), before(1536), ahead(1536),
                  pl.BlockSpec((3, 512), lambda s: (0, 0))],
        out_specs=[pl.BlockSpec((3, 1, N_RES, TI, 512), lambda s: (1, s // n_i, 0, s % n_i, 0)),
                   pl.BlockSpec((8, 512), lambda s: (0, 0))],
        out_shape=[jax.ShapeDtypeStruct((6, t // HALF, N_RES, SEG, 512), BF16), jax.ShapeDtypeStruct((8, 512), F32)],
        semantics=("arbitrary",), vmem_mb=40, after=after,
    )(_x4(dy), _x4(dy), _x4(gates), _x4(gates), _x4(gates), cw)
    return dproj.reshape(6, t, 512), gcw


def _xattn_fwd(x1, g, w_q, kv, w_o, *, tb):
    t, d = x1.shape
    hd = d // N_MEM_HEADS
    m = kv.shape[0]

    def body(x_ref, g_ref, wq_ref, k_ref, v_ref, wo_ref, x2_ref, h_ref, q_ref, o_ref):
        xv = x_ref[...]
        h = _rms_fwd(xv, g_ref[...])[0].astype(BF16)
        h_ref[...] = h
        q = _dot(h, wq_ref[...]).astype(BF16)
        q_ref[...] = q
        for hh in range(N_MEM_HEADS):
            sl = slice(hh * hd, (hh + 1) * hd)
            s = _dot_nt(q[:, sl], k_ref[:, sl]) * (1.0 / 16.0)
            e = jnp.exp(s - jnp.max(s, axis=1, keepdims=True))
            p = e / jnp.sum(e, axis=1, keepdims=True)
            o_ref[:, sl] = _dot(p.astype(BF16), v_ref[:, sl]).astype(BF16)
        x2_ref[...] = xv + _dot(o_ref[...], wo_ref[...])

    tok = pl.BlockSpec((tb, d), lambda i: (i, 0))
    full = pl.BlockSpec((d, d), lambda i: (0, 0))
    return pl.pallas_call(
        body, name="xattn_fwd", grid=(t // tb,),
        in_specs=[tok, pl.BlockSpec((1, d), lambda i: (0, 0)), full,
                  pl.BlockSpec((m, d), lambda i: (0, 0)), pl.BlockSpec((m, d), lambda i: (0, 1)), full],
        out_specs=[tok] * 4,
        out_shape=[jax.ShapeDtypeStruct((t, d), F32)] + [jax.ShapeDtypeStruct((t, d), BF16)] * 3,
        compiler_params=_params(("parallel",), 48),
    )(x1, g, w_q, kv, kv, w_o)


def _xattn_bwd(dx2, x1, g, q, w_q, kv, w_o, *, tb, after=()):
    t, d = x1.shape
    hd = d // N_MEM_HEADS
    m = kv.shape[0]

    def body(dx2_ref, x_ref, g_ref, q_ref, wq_ref, k_ref, v_ref, wo_ref,
             dx1_ref, dq_ref, dk_ref, dv_ref, gg_ref):
        i = pl.program_id(0)

        @pl.when(i == 0)
        def _():
            dk_ref[...] = jnp.zeros_like(dk_ref)
            dv_ref[...] = jnp.zeros_like(dv_ref)

        dx2 = dx2_ref[...]
        do = _dot_nt(dx2.astype(BF16), wo_ref[...]).astype(BF16)
        for hh in range(N_MEM_HEADS):
            sl = slice(hh * hd, (hh + 1) * hd)
            qh, kh, vh, doh = q_ref[:, sl], k_ref[:, sl], v_ref[:, sl], do[:, sl]
            s = _dot_nt(qh, kh) * (1.0 / 16.0)
            e = jnp.exp(s - jnp.max(s, axis=1, keepdims=True))
            p = e / jnp.sum(e, axis=1, keepdims=True)
            dp = _dot_nt(doh, vh)
            ds = (p * (dp - jnp.sum(dp * p, axis=1, keepdims=True)) * (1.0 / 16.0)).astype(BF16)
            dq_ref[:, sl] = _dot(ds, kh).astype(BF16)
            dk_ref[:, sl] += _dot_tn(ds, qh)
            dv_ref[:, sl] += _dot_tn(p.astype(BF16), doh)
        dh = _dot_nt(dq_ref[...], wq_ref[...])
        g_v = g_ref[...]
        _, xh, r = _rms_fwd(x_ref[...], g_v)
        dx1 = dx2 + _rms_bwd(dh, xh, r, g_v)
        dx1_ref[...] = dx1
        part = jnp.sum(dh * xh, axis=0, keepdims=True)

        @pl.when(i == 0)
        def _():
            gg_ref[...] = part

        @pl.when(i != 0)
        def _():
            gg_ref[...] += part

    tok = pl.BlockSpec((tb, d), lambda i: (i, 0))
    full = pl.BlockSpec((d, d), lambda i: (0, 0))
    acc = pl.BlockSpec((m, d), lambda i: (0, 0))
    res, _ = _pcall(
        body, name="xattn_bwd", grid=(t // tb,),
        in_specs=[tok, tok, pl.BlockSpec((1, d), lambda i: (0, 0)), tok, full,
                  pl.BlockSpec((m, d), lambda i: (0, 0)), pl.BlockSpec((m, d), lambda i: (0, 1)), full],
        out_specs=[tok, tok, acc, acc, pl.BlockSpec((1, d), lambda i: (0, 0))],
        out_shape=[jax.ShapeDtypeStruct((t, d), F32), jax.ShapeDtypeStruct((t, d), BF16),
                   jax.ShapeDtypeStruct((m, d), F32), jax.ShapeDtypeStruct((m, d), F32),
                   jax.ShapeDtypeStruct((1, d), F32)],
        semantics=("arbitrary",), vmem_mb=48, after=after,
    )(dx2, x1, g, q, w_q, kv, kv, w_o)
    return res


def _mem_bwd(dk, dv, w_kv, mem, mem_n, g):
    m, d = mem.shape

    def body(dk_ref, dv_ref, w_ref, x_ref, h_ref, g_ref, gw_ref, gg_ref):
        h = h_ref[...]
        dh = jnp.zeros((m, d), F32)
        for i, dy_ref in enumerate((dk_ref, dv_ref)):
            cols = slice(i * d, (i + 1) * d)
            dy = dy_ref[...].astype(BF16)
            gw_ref[:, cols] = _dot_tn(h, dy).astype(BF16)
            dh = dh + _dot_nt(dy, w_ref[:, cols])
        xh = _rms_fwd(x_ref[...], g_ref[...])[1]
        gg_ref[...] = jnp.sum(dh * xh, axis=0, keepdims=True)

    return pl.pallas_call(
        body, name="mem_bwd",
        out_shape=[jax.ShapeDtypeStruct(w_kv.shape, BF16), jax.ShapeDtypeStruct((1, d), F32)],
        compiler_params=pltpu.CompilerParams(vmem_limit_bytes=32 << 20),
    )(dk, dv, w_kv, mem, mem_n, g)


def _mlp_down_loss(a, w_down, x2, tgt, g, *, tb):
    t, d = x2.shape
    f = a.shape[1]

    def body(a_ref, w_ref, x_ref, t_ref, g_ref, dx_ref, dxb_ref, loss_ref, gg_ref):
        i = pl.program_id(0)
        av = a_ref[...].astype(F32)
        x3 = x_ref[...] + _dot((av * av).astype(BF16), w_ref[...])
        g_v = g_ref[...]
        out, xh, r = _rms_fwd(x3, g_v)
        err = out - t_ref[...]
        dout = err * (1.0 / d)
        dx = _rms_bwd(dout, xh, r, g_v)
        dx_ref[...] = dx
        dxb_ref[...] = dx.astype(BF16)
        part = jnp.sum(dout * xh, axis=0, keepdims=True)
        lpart = 0.5 * jnp.sum(jnp.mean(err * err, axis=-1, keepdims=True), axis=0, keepdims=True)
        lpart = jnp.broadcast_to(lpart, loss_ref.shape)

        @pl.when(i == 0)
        def _():
            gg_ref[...] = part
            loss_ref[...] = lpart

        @pl.when(i != 0)
        def _():
            gg_ref[...] += part
            loss_ref[...] += lpart

    tok = pl.BlockSpec((tb, d), lambda i: (i, 0))
    return pl.pallas_call(
        body, name="mlp_down_loss", grid=(t // tb,),
        in_specs=[pl.BlockSpec((tb, f), lambda i: (i, 0)), pl.BlockSpec((f, d), lambda i: (0, 0)), tok, tok,
                  pl.BlockSpec((1, d), lambda i: (0, 0))],
        out_specs=[tok, tok, pl.BlockSpec((8, 128), lambda i: (0, 0)), pl.BlockSpec((1, d), lambda i: (0, 0))],
        out_shape=[jax.ShapeDtypeStruct((t, d), F32), jax.ShapeDtypeStruct((t, d), BF16),
                   jax.ShapeDtypeStruct((8, 128), F32), jax.ShapeDtypeStruct((1, d), F32)],
        compiler_params=_params(("arbitrary",), 56),
    )(a, w_down, x2, tgt, g)


def _mlp_dpre(dx3, w_down, a, *, tb, bn):
    t, d = dx3.shape
    f = a.shape[1]

    def body(dx_ref, w_ref, a_ref, o_ref):
        o_ref[...] = (2.0 * a_ref[...].astype(F32) * _dot_nt(dx_ref[...], w_ref[...])).astype(BF16)

    return pl.pallas_call(
        body, name="mlp_dpre", grid=(t // tb, f // bn),
        in_specs=[pl.BlockSpec((tb, d), lambda i, j: (i, 0)), pl.BlockSpec((bn, d), lambda i, j: (j, 0)),
                  pl.BlockSpec((tb, bn), lambda i, j: (i, j))],
        out_specs=pl.BlockSpec((tb, bn), lambda i, j: (i, j)),
        out_shape=jax.ShapeDtypeStruct((t, f), BF16),
        compiler_params=_params(("parallel", "arbitrary"), 48),
    )(dx3, w_down, a)


def _adamw(gsum, w, m, v):
    m_new = ADAM_B1 * m + (1.0 - ADAM_B1) * gsum
    v_new = ADAM_B2 * v + (1.0 - ADAM_B2) * (gsum * gsum)
    m_hat = m_new / (1.0 - ADAM_B1 ** ADAM_STEP)
    v_hat = v_new / (1.0 - ADAM_B2 ** ADAM_STEP)
    delta = -ADAM_LR * (m_hat / (jnp.sqrt(v_hat) + ADAM_EPS) + ADAM_WD * w)
    return delta, m_new, v_new


def _sum_adamw(parts, w, m, v, *, name, tr):
    r, c = w.shape

    def body(p_ref, w_ref, m_ref, v_ref, g_ref, d_ref, mo_ref, vo_ref):
        g = p_ref[0].astype(F32)
        for k in range(1, N_DEV):
            g = g + p_ref[k].astype(F32)
        g_ref[...] = g
        d_ref[...], mo_ref[...], vo_ref[...] = _adamw(g, w_ref[...], m_ref[...], v_ref[...])

    blk = pl.BlockSpec((tr, c), lambda i: (i, 0))
    return pl.pallas_call(
        body, name=name, grid=(r // tr,),
        in_specs=[pl.BlockSpec((N_DEV, tr, c), lambda i: (0, i, 0)), blk, blk, blk],
        out_specs=[blk] * 4, out_shape=[jax.ShapeDtypeStruct((r, c), F32)] * 4,
        compiler_params=_params(("parallel",), 40),
    )(*[pltpu.with_memory_space_constraint(a, pltpu.HBM) for a in (parts, w, m, v)])


SMALL_PART = 8
_GAIN_ROWS = ("g_mix", "g_xattn", "g_mem", "g_mlp", "g_final")
_SMALL = _GAIN_ROWS + ("g_attn_out", "g_conv_out", "conv_w")
CONV_SHARD = 512 // N_DEV


def _update_small(parts, me, w, m, v):
    n = len(_SMALL)

    def body(me_ref, p_ref, *refs):
        ins, loss_ref, outs = refs[:3 * n], refs[3 * n], refs[3 * n + 1:]

        def total(i):
            lo = SMALL_PART * i
            s = p_ref[0, lo:lo + SMALL_PART, :]
            for k in range(1, N_DEV):
                s = s + p_ref[k, lo:lo + SMALL_PART, :]
            return s

        grads = {k: total(i)[0:1] for i, k in enumerate(_GAIN_ROWS)}
        both = total(5)[0:1]
        grads["g_attn_out"], grads["g_conv_out"] = both[:, 0:512], both[:, 512:1024]
        taps = total(6)
        mine = jnp.zeros((SMALL_PART, BLK), F32)
        for j in range(N_DEV):
            lo = j * CONV_SHARD // BLK * BLK
            blk = taps[:, lo:lo + BLK]
            if j * CONV_SHARD != lo:
                blk = pltpu.roll(blk, BLK - (j * CONV_SHARD - lo), axis=1)
            mine = jnp.where(me_ref[0] == j, blk, mine)
        grads["conv_w"] = mine[0:3, 0:CONV_SHARD]
        loss_ref[...] = total(7)[0:1, 0:1]
        for i, k in enumerate(_SMALL):
            g_ref, d_ref, mo_ref, vo_ref = outs[4 * i:4 * i + 4]
            g_ref[...] = grads[k]
            d_ref[...], mo_ref[...], vo_ref[...] = _adamw(grads[k], ins[i][...], ins[n + i][...], ins[2 * n + i][...])

    vmem = pl.BlockSpec(memory_space=pltpu.VMEM)
    args = [d[k] for d in (w, m, v) for k in _SMALL]
    res = pl.pallas_call(
        body, name="update_small",
        in_specs=[pl.BlockSpec(memory_space=pltpu.SMEM)] + [vmem] * (1 + 3 * n),
        out_shape=[jax.ShapeDtypeStruct((1, 1), F32)] + [jax.ShapeDtypeStruct(w[k].shape, F32) for k in _SMALL
                                                         for _ in range(4)],
    )(me, parts, *args)
    return res[0], {k: res[1 + 4 * i:5 + 4 * i] for i, k in enumerate(_SMALL)}


def _head_sum_matrix():
    r = lax.broadcasted_iota(jnp.int32, (512, 512), 0) // HEAD_DIM
    c = lax.broadcasted_iota(jnp.int32, (512, 512), 1) // HEAD_DIM
    return (r == c).astype(BF16)


_SHARD_AXIS = dict(w_in=1, w_out=0, w_q=0, w_kv=1, w_o=0, w_up=1, w_down=0, conv_w=None, small=None)


class _Weights:
    def __init__(self, full, shards=None):
        self.full = dict(full)
        self.shards = shards

    def rider(self, names, late=False):
        if self.shards is None:
            return None
        return _Gather([self.shards[n] for n in names], [_SHARD_AXIS[n] for n in names], late)

    def arrived(self, names, gathered):
        if gathered is not None:
            for n, g in zip(names, gathered):
                self.full[n] = g.transpose(1, 0, 2).reshape(g.shape[1], -1) if n == "conv_w" else g

    def __getitem__(self, name):
        return self.full[name]


class _Grads:
    def __init__(self, distributed):
        self.distributed = distributed
        self.local = {}
        self.pending = {}

    def add(self, name, g):
        self.local[name] = g

    def send(self, *names):
        if not self.distributed:
            return []
        rider = _Exchange([self.local[n] for n in names], [_SHARD_AXIS[n] for n in names])
        started = _exchange_start(rider, "send_" + "_".join(names))
        self.pending[names[0]] = (names, rider, started)
        return [started[3]]

    def wait(self, first_name, after):
        names, rider, started = self.pending.pop(first_name)
        return _exchange_wait(rider, started, after, "wait_" + "_".join(names))


def _ride(fn, *args, rider=None, **kw):
    if rider is None:
        return fn(*args, **kw), None
    return fn(*args, rider=rider, **kw)


def _local_step(x, mem, tgt, gains, weights, grads):
    names = ["w_in", "conv_w"]
    (x, tgt), got = _ride(_reorder, [x, tgt], "reorder_in", rider=weights.rider(names, late=True))
    weights.arrived(names, got)
    w_in, cw = weights["w_in"], weights["conv_w"]

    names = ["w_out", "w_kv"]
    (qkv, gates, h1), got = _ride(_proj, x, gains["g_mix"], w_in, tb=1024, rider=weights.rider(names))
    weights.arrived(names, got)
    names = ["w_q", "w_o", "w_up"]
    (attn, *lses), got = _ride(_attention_fwd, qkv, rider=weights.rider(names))
    weights.arrived(names, got)
    x1, merged = _mixer_fwd(x, attn, gates, cw, gains["g_attn_out"], gains["g_conv_out"], weights["w_out"])
    kv, mem_n = _norm_matmul(mem, gains["g_mem"], weights["w_kv"], name="mem_kv", out_dtype=BF16, tb=mem.shape[0],
                             bn=1024, save_h=True)
    x2, h2, qm, om = _xattn_fwd(x1, gains["g_xattn"], weights["w_q"], kv, weights["w_o"], tb=512)
    w_up = weights["w_up"]
    (a, h3), got = _ride(_norm_matmul, x2, gains["g_mlp"], w_up, name="mlp_up", out_dtype=BF16, tb=1024, bn=2048,
                         relu=True, save_h=True, rider=weights.rider(["w_down"], late=True))
    weights.arrived(["w_down"], got)
    w_down = weights["w_down"]
    dx3, dx3b, loss_blk, gg_final = _mlp_down_loss(a, w_down, x2, tgt, gains["g_final"], tb=512)

    dpre = _mlp_dpre(dx3b, w_down, a, tb=1024, bn=2048)
    grads.add("w_down", _matmul_tn(a, dx3b, name="grad_w_down", bm=512, bn=1024, square_a=True))
    sent = grads.send("w_down")
    grads.add("w_up", _matmul_tn(h3, dpre, name="grad_w_up", bm=1024, bn=1024, after=sent))
    sent = grads.send("w_up")
    dx2, dx2b, gg_mlp = _matmul_nt_normbwd(dpre, w_up, x2, gains["g_mlp"], dx3, name="mlp_dx", tb=512,
                                           also_bf16=True, after=sent)

    grads.add("w_o", _matmul_tn(om, dx2b, name="grad_w_o", bm=512, bn=512))
    dx1, dqm, dk, dv, gg_xattn = _xattn_bwd(dx2, x1, gains["g_xattn"], qm, weights["w_q"], kv, weights["w_o"], tb=512)
    grads.add("w_q", _matmul_tn(h2, dqm, name="grad_w_q", bm=1024, bn=512))
    gw_kv, gg_mem = _mem_bwd(dk, dv, weights["w_kv"], mem, mem_n, gains["g_mem"])
    grads.add("w_kv", gw_kv)

    dattn, dsum, dy, gg_attn, gg_conv, gw_out = _mixer_bwd(dx1, merged, attn, gates, cw, gains["g_attn_out"],
                                                           gains["g_conv_out"], weights["w_out"], _head_sum_matrix())
    grads.add("w_out", gw_out)
    sent = grads.send("w_o", "w_q", "w_kv", "w_out")
    dproj, gcw = _conv_bwd(dy, gates, cw, after=sent)
    dproj = _attention_bwd(qkv, dattn, dsum, lses, dproj)
    grads.add("w_in", _matmul_tn(h1, dproj, name="grad_w_in", bm=1024, bn=512))
    sent = grads.send("w_in")
    grad_x, gg_mix = _matmul_nt_normbwd(dproj, w_in, x, gains["g_mix"], dx1, name="mixer_dx", tb=512,
                                        to_natural=True, after=sent)

    def part(v):
        return jnp.pad(v, ((0, SMALL_PART - v.shape[0]), (0, 1024 - v.shape[1])))

    parts = [gg_mix, gg_xattn, gg_mem, gg_mlp, gg_final, jnp.concatenate([gg_attn, gg_conv], axis=1), gcw, loss_blk]
    grads.add("small", jnp.concatenate([part(v) for v in parts], axis=0))
    return grad_x


_BIG = ("w_in", "w_out", "w_q", "w_kv", "w_o", "w_up", "w_down")


def kernel(x, mem, g_mix, w_in, conv_w, g_attn_out, g_conv_out, w_out, g_xattn, g_mem, w_q_mem, w_kv_mem, w_o_mem, g_mlp, w_up, w_down, g_final, loss_target, m_g_mix, m_w_in, m_conv_w, m_g_attn_out, m_g_conv_out, m_w_out, m_g_xattn, m_g_mem, m_w_q_mem, m_w_kv_mem, m_w_o_mem, m_g_mlp, m_w_up, m_w_down, m_g_final, v_g_mix, v_w_in, v_conv_w, v_g_attn_out, v_g_conv_out, v_w_out, v_g_xattn, v_g_mem, v_w_q_mem, v_w_kv_mem, v_w_o_mem, v_g_mlp, v_w_up, v_w_down, v_g_final):
    d = x.shape[-1]
    me = 4 * lax.axis_index("x") + 2 * lax.axis_index("y") + lax.axis_index("c")
    w_shards = dict(w_in=w_in, w_out=w_out, w_q=w_q_mem, w_kv=w_kv_mem, w_o=w_o_mem, w_up=w_up, w_down=w_down)
    m_shards = dict(w_in=m_w_in, w_out=m_w_out, w_q=m_w_q_mem, w_kv=m_w_kv_mem, w_o=m_w_o_mem, w_up=m_w_up,
                    w_down=m_w_down)
    v_shards = dict(w_in=v_w_in, w_out=v_w_out, w_q=v_w_q_mem, w_kv=v_w_kv_mem, w_o=v_w_o_mem, w_up=v_w_up,
                    w_down=v_w_down)
    gains = dict(g_mix=g_mix, g_attn_out=g_attn_out, g_conv_out=g_conv_out, g_xattn=g_xattn, g_mem=g_mem,
                 g_mlp=g_mlp, g_final=g_final)
    gains2 = {k: v.reshape(1, -1) for k, v in gains.items()}

    shards = {k: w_shards[k].astype(BF16) for k in _BIG}
    shards["conv_w"] = conv_w
    grads = _Grads(distributed=True)
    grad_x = _local_step(x[0], mem[0], loss_target[0], gains2, _Weights({}, shards), grads)

    after = grads.send("small")
    outs = {}
    tiles = dict(w_in=256, w_out=128, w_q=128, w_kv=256, w_o=128, w_up=256, w_down=256)
    for group in (("w_down",), ("w_up",), ("w_o", "w_q", "w_kv", "w_out"), ("w_in",)):
        for k, received in zip(group, grads.wait(group[0], after)):
            outs[k] = _sum_adamw(received, w_shards[k], m_shards[k], v_shards[k], name=f"adamw_{k}", tr=tiles[k])
            after = [outs[k][0]]
    small_received, = grads.wait("small", after)

    m_small = dict(g_mix=m_g_mix, g_attn_out=m_g_attn_out, g_conv_out=m_g_conv_out, g_xattn=m_g_xattn,
                   g_mem=m_g_mem, g_mlp=m_g_mlp, g_final=m_g_final)
    v_small = dict(g_mix=v_g_mix, g_attn_out=v_g_attn_out, g_conv_out=v_g_conv_out, g_xattn=v_g_xattn,
                   g_mem=v_g_mem, g_mlp=v_g_mlp, g_final=v_g_final)
    as_rows = lambda vals, conv: dict({k: a.reshape(1, -1) for k, a in vals.items()}, conv_w=conv)
    loss, small_out = _update_small(small_received, me.reshape(1), as_rows(gains, conv_w),
                                    as_rows(m_small, m_conv_w), as_rows(v_small, v_conv_w))
    small_out = {k: [a.reshape(dict(gains, conv_w=conv_w)[k].shape) for a in res] for k, res in small_out.items()}
    names = {"g_mix": "g_mix", "w_in": "w_in", "conv_w": "conv_w", "g_attn_out": "g_attn_out",
             "g_conv_out": "g_conv_out", "w_out": "w_out", "g_xattn": "g_xattn", "g_mem": "g_mem",
             "w_q_mem": "w_q", "w_kv_mem": "w_kv", "w_o_mem": "w_o", "g_mlp": "g_mlp", "w_up": "w_up",
             "w_down": "w_down", "g_final": "g_final"}
    result = [loss.reshape(()), grad_x[None]]
    for which in range(4):
        for key in names.values():
            result.append(outs[key][which] if key in outs else small_out[key][which])
    return tuple(result)
```

```python
import math

import jax
import jax.numpy as jnp
from jax import lax
from jax.experimental import pallas as pl
from jax.experimental.pallas import tpu as pltpu

F32 = jnp.float32
BF16 = jnp.bfloat16
NORM_EPS = 1e-6
NEG_INF = -1e30
N_DEV = 8
BLK = 128
HEAD_DIM = 64
N_MEM_HEADS = 4
ADAM_LR = 0.001
ADAM_B1 = 0.9
ADAM_B2 = 0.999
ADAM_EPS = 1e-08
ADAM_WD = 0.01
ADAM_STEP = 10
MESH = pl.DeviceIdType.MESH
ANY = pl.BlockSpec(memory_space=pl.ANY)


def _dot(a, b):
    return jnp.dot(a, b, preferred_element_type=F32)


def _dot_nt(a, b):
    return lax.dot_general(a, b, (((1,), (1,)), ((), ())), preferred_element_type=F32)


def _dot_tn(a, b):
    return lax.dot_general(a, b, (((0,), (0,)), ((), ())), preferred_element_type=F32)


def _params(semantics, vmem_mb):
    return pltpu.CompilerParams(dimension_semantics=semantics, vmem_limit_bytes=vmem_mb << 20)


def _rms_fwd(x, g):
    r = lax.rsqrt(jnp.mean(x * x, axis=-1, keepdims=True) + NORM_EPS)
    xh = x * r
    return xh * g, xh, r


def _rms_bwd(dy, xh, r, g):
    gy = dy * g
    return r * (gy - xh * jnp.mean(xh * gy, axis=-1, keepdims=True))


def _position():
    x, y, c = lax.axis_index("x"), lax.axis_index("y"), lax.axis_index("c")
    return x, y, c


def _block_of(ref, j, axis, shard_shape):
    r, c = shard_shape
    if axis is None:
        return ref.at[j]
    if axis == 0:
        return ref.at[pl.ds(j * r, r), :]
    return ref.at[:, pl.ds(j * c, c)]


class _Gather:
    has_mid = True
    alias_pairs = ()

    def __init__(self, shards, axes, late=False):
        self.arrays = list(shards)
        self.axes = list(axes)
        self.late = late
        self.n = len(self.arrays)

    def out_shape(self):
        res = []
        for s, axis in zip(self.arrays, self.axes):
            r, c = s.shape
            shape = (N_DEV, r, c) if axis is None else (N_DEV * r, c) if axis == 0 else (r, N_DEV * c)
            res.append(jax.ShapeDtypeStruct(shape, s.dtype))
        return res

    def scratch(self):
        return [pltpu.SemaphoreType.DMA((self.n, 7)), pltpu.SemaphoreType.DMA((self.n, 7)),
                pltpu.SemaphoreType.DMA((self.n,))]

    def _ctx(self, ins, outs, sems):
        send_sems, recv_sems, local_sems = sems
        x, y, c = _position()
        me, sibling = (x, y, c), (x, y, 1 - c)
        chips = [(1 - x, y), (x, 1 - y), (1 - x, 1 - y)]

        def lin(px, py, pc):
            return 4 * px + 2 * py + pc

        def place(a, block):
            return _block_of(outs[a], lin(*block), self.axes[a], self.arrays[a].shape)

        def copy(a, k, block, to, src=None):
            dst = place(a, block)
            return pltpu.make_async_remote_copy(
                src_ref=dst if src is None else src, dst_ref=dst,
                send_sem=send_sems.at[a, k], recv_sem=recv_sems.at[a, k],
                device_id=to, device_id_type=MESH)

        def mine():
            return [pltpu.make_async_copy(ins[a], place(a, me), local_sems.at[a]) for a in range(self.n)]

        def first():
            res = []
            for a in range(self.n):
                res.append(copy(a, 0, me, sibling, src=ins[a]))
                res += [copy(a, 1 + j, me, (*chip, c), src=ins[a]) for j, chip in enumerate(chips)]
            return res

        return c, me, sibling, chips, copy, mine, first

    def start(self, ins, outs, sems):
        _, _, _, _, _, mine, first = self._ctx(ins, outs, sems)
        for cp in mine() + first():
            cp.start()

    def mid(self, ins, outs, sems):
        c, me, sibling, chips, copy, _, _ = self._ctx(ins, outs, sems)
        for j, chip in enumerate(chips):
            for a in range(self.n):
                copy(a, 1 + j, (*chip, c), me).wait_recv()
                copy(a, 4 + j, (*chip, c), sibling).start()

    def finish(self, ins, outs, sems):
        c, me, sibling, chips, copy, mine, first = self._ctx(ins, outs, sems)
        for a in range(self.n):
            copy(a, 0, sibling, me).wait_recv()
            for j, chip in enumerate(chips):
                copy(a, 4 + j, (*chip, 1 - c), me).wait_recv()
        for cp in first():
            cp.wait_send()
        for j, chip in enumerate(chips):
            for a in range(self.n):
                copy(a, 4 + j, (*chip, c), sibling).wait_send()
        for cp in mine():
            cp.wait()


class _Exchange:
    def __init__(self, parts, axes):
        self.n = len(parts)
        self.axes = list(axes)
        self.arrays = list(parts)

    def _piece(self, a):
        r, c = self.arrays[a].shape
        axis = self.axes[a]
        return (r, c) if axis is None else (r // N_DEV, c) if axis == 0 else (r, c // N_DEV)

    def out_shape(self):
        return [jax.ShapeDtypeStruct((N_DEV,) + self._piece(a), self.arrays[a].dtype) for a in range(self.n)]

    def semaphores(self):
        return [pltpu.SemaphoreType.DMA((7 * self.n,)), pltpu.SemaphoreType.DMA((7 * self.n,)),
                pltpu.SemaphoreType.DMA((self.n,))]

    def _ctx(self, ins, outs, sems):
        send_sems, recv_sems, local_sems = sems
        x, y, c = _position()
        me = 4 * x + 2 * y + c

        def src(a, j):
            return ins[a] if self.axes[a] is None else _block_of(ins[a], j, self.axes[a], self._piece(a))

        def dst(a, j):
            return outs[a].at[j]

        def local():
            return [pltpu.make_async_copy(src(a, me), dst(a, me), local_sems.at[a]) for a in range(self.n)]

        def remote(inbound):
            res = []
            for a in range(self.n):
                for k in range(1, N_DEV):
                    peer = (1 - x if k & 4 else x, 1 - y if k & 2 else y, 1 - c if k & 1 else c)
                    plin = 4 * peer[0] + 2 * peer[1] + peer[2]
                    res.append(pltpu.make_async_remote_copy(
                        src_ref=src(a, plin), dst_ref=dst(a, plin if inbound else me),
                        send_sem=send_sems.at[7 * a + k - 1], recv_sem=recv_sems.at[7 * a + k - 1],
                        device_id=peer, device_id_type=MESH))
            return res

        return local, remote

    def start(self, ins, outs, sems):
        local, remote = self._ctx(ins, outs, sems)
        for cp in local() + remote(False):
            cp.start()

    def finish(self, ins, outs, sems):
        local, remote = self._ctx(ins, outs, sems)
        for cp in remote(True):
            cp.wait_recv()
        for cp in remote(False):
            cp.wait_send()
        for cp in local():
            cp.wait()


def _exchange_start(rider, name):
    n = rider.n
    parts = rider.arrays
    lands = [lax.empty(s.shape, s.dtype) for s in rider.out_shape()]
    hbm = pl.BlockSpec(memory_space=pltpu.HBM)
    sem = pl.BlockSpec(memory_space=pltpu.SEMAPHORE)

    def body(*refs):
        ins, sems = refs[:n], refs[2 * n:2 * n + 3]
        outs, token = refs[2 * n + 3 + n:2 * n + 3 + 2 * n], refs[-1]
        rider.start(ins, outs, sems)
        token[...] = jnp.zeros_like(token)

    res = pl.pallas_call(
        body, name=name,
        out_shape=rider.semaphores() + [pltpu.HBM(p.shape, p.dtype) for p in parts]
                  + [pltpu.HBM(z.shape, z.dtype) for z in lands] + [jax.ShapeDtypeStruct((8, 128), F32)],
        in_specs=[hbm] * (2 * n), out_specs=[sem] * 3 + [hbm] * (2 * n) + [pl.BlockSpec(memory_space=pltpu.VMEM)],
        input_output_aliases={i: 3 + i for i in range(2 * n)},
        compiler_params=pltpu.CompilerParams(has_side_effects=pltpu.SideEffectType.DATAFLOW_SIDE_EFFECTING),
    )(*[pltpu.with_memory_space_constraint(a, pltpu.HBM) for a in parts + lands])
    return res[:3], res[3:3 + n], res[3 + n:3 + 2 * n], res[-1]


def _exchange_wait(rider, started, after, name):
    n = rider.n
    sems, parts, lands, _ = started
    hbm = pl.BlockSpec(memory_space=pltpu.HBM)
    sem = pl.BlockSpec(memory_space=pltpu.SEMAPHORE)

    def body(*refs):
        rider.finish(refs[:n], refs[n:2 * n], refs[2 * n:2 * n + 3])

    res = pl.pallas_call(
        body, name=name, out_shape=[pltpu.HBM(a.shape, a.dtype) for a in list(parts) + list(lands)],
        in_specs=[hbm] * (2 * n) + [sem] * 3 + [ANY] * len(after), out_specs=[hbm] * (2 * n),
        input_output_aliases={i: i for i in range(2 * n)},
        compiler_params=pltpu.CompilerParams(has_side_effects=pltpu.SideEffectType.DATAFLOW_SIDE_EFFECTING),
    )(*parts, *lands, *sems, *after)
    return list(res[n:])


def _pcall(body, *, name, grid, in_specs, out_specs, out_shape, scratch_shapes=(), semantics, vmem_mb, rider=None,
           aliases=None, after=()):
    in_specs, out_specs, out_shape = list(in_specs), list(out_specs), list(out_shape)
    scratch_shapes = list(scratch_shapes)
    aliases = dict(aliases or {})
    if rider is None:
        n_in, after = len(in_specs), list(after)

        def plain(*refs):
            body(*refs[:n_in], *refs[n_in + len(after):])

        call = pl.pallas_call(plain if after else body, name=name, grid=grid, in_specs=in_specs + [ANY] * len(after),
                              out_specs=out_specs, out_shape=out_shape, scratch_shapes=scratch_shapes,
                              input_output_aliases=aliases, compiler_params=_params(semantics, vmem_mb))
        return lambda *args: (list(call(*args, *after)), None)
    n_in, n_out, n_scr = len(in_specs), len(out_specs), len(scratch_shapes)
    r_in, r_shapes = len(rider.arrays), rider.out_shape()
    r_out = len(r_shapes)
    aliases.update({n_in + i: n_out + o for i, o in rider.alias_pairs})
    total = math.prod(grid)
    mid_step = total - 1 if rider.has_mid and rider.late else (3 * total) // 4

    def wrapped(*refs):
        bounds = [0, n_in, r_in, n_out, r_out, n_scr]
        for i in range(1, len(bounds)):
            bounds[i] += bounds[i - 1]
        a, ra, o, ro, s = (refs[bounds[i]:bounds[i + 1]] for i in range(5))
        rs = refs[bounds[5]:]
        step = pl.program_id(0)
        for k in range(1, len(grid)):
            step = step * grid[k] + pl.program_id(k)
        pl.when(step == 0)(lambda: rider.start(ra, ro, rs))
        body(*a, *o, *s)
        if rider.has_mid:
            pl.when(step == mid_step)(lambda: rider.mid(ra, ro, rs))
        pl.when(step == total - 1)(lambda: rider.finish(ra, ro, rs))

    call = pl.pallas_call(
        wrapped, name=name, grid=grid, in_specs=in_specs + [ANY] * r_in, out_specs=out_specs + [ANY] * r_out,
        out_shape=out_shape + r_shapes, scratch_shapes=scratch_shapes + rider.scratch(),
        input_output_aliases=aliases, compiler_params=_params(("arbitrary",) * len(grid), vmem_mb))

    def run(*args):
        res = call(*args, *rider.arrays)
        return list(res[:n_out]), list(res[n_out:])

    return run


def _norm_matmul(x, g, w, *, name, out_dtype, tb, bn, relu=False, save_h=False, rider=None):
    t, d = x.shape
    n = w.shape[1]

    def body(x_ref, g_ref, w_ref, o_ref, *rest):
        h_scr = rest[-1]

        @pl.when(pl.program_id(1) == 0)
        def _():
            h = _rms_fwd(x_ref[...], g_ref[...])[0].astype(BF16)
            h_scr[...] = h
            if save_h:
                rest[0][...] = h

        acc = _dot(h_scr[...], w_ref[...])
        if relu:
            acc = jnp.maximum(acc, 0.0)
        o_ref[...] = acc.astype(out_dtype)

    out_shape = [jax.ShapeDtypeStruct((t, n), out_dtype)]
    out_specs = [pl.BlockSpec((tb, bn), lambda i, j: (i, j))]
    if save_h:
        out_shape.append(jax.ShapeDtypeStruct((t, d), BF16))
        out_specs.append(pl.BlockSpec((tb, d), lambda i, j: (i, 0)))
    res, extra = _pcall(
        body, name=name, grid=(t // tb, n // bn),
        in_specs=[pl.BlockSpec((tb, d), lambda i, j: (i, 0)),
                  pl.BlockSpec((1, d), lambda i, j: (0, 0)),
                  pl.BlockSpec((d, bn), lambda i, j: (0, j))],
        out_specs=out_specs, out_shape=out_shape,
        scratch_shapes=[pltpu.VMEM((tb, d), BF16)],
        semantics=("parallel", "arbitrary"), vmem_mb=48, rider=rider,
    )(x, g, w)
    res = res if save_h else res[0]
    return res if rider is None else (res, extra)


def _proj(x, g, w, *, tb, rider=None):
    t, d = x.shape
    half = w.shape[1] // 2

    def body(x_ref, g_ref, w_ref, qkv_ref, gates_ref, h_ref, h_scr):
        j = pl.program_id(1)

        @pl.when(j == 0)
        def _():
            h = _rms_fwd(x_ref[...], g_ref[...])[0].astype(BF16)
            h_scr[...] = h
            h_ref[...] = h

        acc = _dot(h_scr[...], w_ref[...])

        @pl.when(j == 0)
        def _():
            qkv_ref[...] = acc

        @pl.when(j == 1)
        def _():
            gates_ref[...] = acc.astype(BF16)

    tok = lambda c: pl.BlockSpec((tb, c), lambda i, j: (i, 0))
    res, extra = _pcall(
        body, name="proj", grid=(t // tb, 2),
        in_specs=[tok(d), pl.BlockSpec((1, d), lambda i, j: (0, 0)), pl.BlockSpec((d, half), lambda i, j: (0, j))],
        out_specs=[tok(half), tok(half), tok(d)],
        out_shape=[jax.ShapeDtypeStruct((t, half), F32), jax.ShapeDtypeStruct((t, half), BF16),
                   jax.ShapeDtypeStruct((t, d), BF16)],
        scratch_shapes=[pltpu.VMEM((tb, d), BF16)],
        semantics=("parallel", "arbitrary"), vmem_mb=48, rider=rider,
    )(x, g, w)
    return res if rider is None else (res, extra)


def _matmul_nt_normbwd(dy, w, x, g, dres, *, name, tb, also_bf16=False, to_natural=False, after=()):
    t, d = x.shape
    stacked = dy.ndim == 3
    has_res = dres is not None
    n_i = SEG // TI
    if to_natural:
        tb = N_RES * TI

    def body(dy_ref, w_ref, x_ref, g_ref, *rest):
        rest = list(rest)
        dres_ref = rest.pop(0) if has_res else None
        dx_ref = rest.pop(0)
        dxb_ref = rest.pop(0) if also_bf16 else None
        gg_ref = rest.pop(0)
        i = pl.program_id(0)

        def rows(ref, *lead):
            v = ref[lead] if lead else ref[...]
            return v[0].reshape(tb, v.shape[-1]) if to_natural else v

        if stacked:
            kb = dy_ref.shape[-1]
            dh = _dot_nt(rows(dy_ref, 0), w_ref[:, 0:kb])
            for s in range(1, dy_ref.shape[0]):
                dh = dh + _dot_nt(rows(dy_ref, s), w_ref[:, s * kb:(s + 1) * kb])
        else:
            dh = _dot_nt(rows(dy_ref), w_ref[...])
        g_v = g_ref[...]
        _, xh, r = _rms_fwd(rows(x_ref), g_v)
        dx = _rms_bwd(dh, xh, r, g_v)
        if has_res:
            dx = dx + rows(dres_ref)
        if to_natural:
            scr = rest.pop(0)
            for cb in range(d // BLK):
                cols = slice(cb * BLK, (cb + 1) * BLK)
                slab = scr.at[cb]
                for res in range(N_RES):
                    slab[pl.ds(res, TI, stride=N_RES), :] = dx[res * TI:(res + 1) * TI, cols]
                dx_ref[:, cols] = slab[...]
        else:
            dx_ref[...] = dx
        if also_bf16:
            dxb_ref[...] = dx.astype(BF16)
        part = jnp.sum(dh * xh, axis=0, keepdims=True)

        @pl.when(i == 0)
        def _():
            gg_ref[...] = part

        @pl.when(i != 0)
        def _():
            gg_ref[...] += part

    tok = pl.BlockSpec((tb, d), lambda i: (i, 0))
    row = pl.BlockSpec((1, d), lambda i: (0, 0))
    if to_natural:
        act = pl.BlockSpec((1, N_RES, TI, d), lambda i: (i // n_i, 0, i % n_i, 0))
        dy_spec = pl.BlockSpec((dy.shape[0], 1, N_RES, TI, dy.shape[2]), lambda i: (0, i // n_i, 0, i % n_i, 0))
        dy, x = dy.reshape(dy.shape[0], t // HALF, N_RES, SEG, dy.shape[2]), _x4(x)
        dres = _x4(dres) if has_res else None
    elif stacked:
        act, dy_spec = tok, pl.BlockSpec((dy.shape[0], tb, dy.shape[2]), lambda i: (0, i, 0))
    else:
        act, dy_spec = tok, pl.BlockSpec((tb, dy.shape[1]), lambda i: (i, 0))
    in_specs = [dy_spec, pl.BlockSpec(w.shape, lambda i: (0, 0)), act, row]
    args = [dy, w, x, g]
    if has_res:
        in_specs.append(act)
        args.append(dres)
    out_specs = [tok] + ([tok] if also_bf16 else []) + [row]
    out_shape = ([jax.ShapeDtypeStruct((t, d), F32)] + ([jax.ShapeDtypeStruct((t, d), BF16)] if also_bf16 else [])
                 + [jax.ShapeDtypeStruct((1, d), F32)])
    res, _ = _pcall(
        body, name=name, grid=(t // tb,), in_specs=in_specs, out_specs=out_specs, out_shape=out_shape,
        scratch_shapes=[pltpu.VMEM((d // BLK, tb, BLK), F32)] if to_natural else [],
        semantics=("arbitrary",), vmem_mb=56, after=after,
    )(*args)
    return res


def _matmul_tn(a, b, *, name, bm, bn, square_a=False, after=()):
    t, m = a.shape
    stacked = b.ndim == 3
    n = b.shape[0] * bn if stacked else b.shape[1]

    def body(a_ref, b_ref, o_ref):
        av = a_ref[...]
        if square_a:
            av = av.astype(F32)
            av = (av * av).astype(BF16)
        o_ref[...] = _dot_tn(av, b_ref[...]).astype(BF16)

    res, _ = _pcall(
        body, name=name, grid=(m // bm, n // bn),
        in_specs=[pl.BlockSpec((t, bm), lambda i, j: (0, i)),
                  pl.BlockSpec((None, t, bn), lambda i, j: (j, 0, 0)) if stacked
                  else pl.BlockSpec((t, bn), lambda i, j: (0, j))],
        out_specs=[pl.BlockSpec((bm, bn), lambda i, j: (i, j))], out_shape=[jax.ShapeDtypeStruct((m, n), BF16)],
        semantics=("parallel", "parallel"), vmem_mb=56, after=after,
    )(a, b)
    return res[0]


N_RES = 16
SEG = 128
HALF = N_RES * SEG
TI = 32
HALO = 16


def _x4(a):
    return a.reshape(a.shape[0] // HALF, N_RES, SEG, a.shape[1])


def _reorder(arrays, name, rider=None):
    t, c = arrays[0].shape
    n = len(arrays)
    n_i = SEG // TI

    def body(*refs):
        scr = refs[-1]
        for i_ref, o_ref in zip(refs[:n], refs[n:2 * n]):
            for cb in range(c // BLK):
                cols = slice(cb * BLK, (cb + 1) * BLK)
                slab = scr.at[cb]
                slab[...] = i_ref[:, cols]
                for r in range(N_RES):
                    o_ref[0, r, :, cols] = slab[pl.ds(r, TI, stride=N_RES), :]

    res, extra = _pcall(
        body, name=name, grid=(t // (TI * N_RES),),
        in_specs=[pl.BlockSpec((TI * N_RES, c), lambda s: (s, 0))] * n,
        out_specs=[pl.BlockSpec((1, N_RES, TI, c), lambda s: (s // n_i, 0, s % n_i, 0))] * n,
        out_shape=[jax.ShapeDtypeStruct((t // HALF, N_RES, SEG, c), F32)] * n,
        scratch_shapes=[pltpu.VMEM((c // BLK, TI * N_RES, BLK), F32)],
        semantics=("parallel",), vmem_mb=32, rider=rider,
    )(*arrays)
    res = [r.reshape(t, c) for r in res]
    return res if rider is None else (res, extra)


_PATTERNS = ((1, 16, 8, SEG), (4, 4, 32, 4 * SEG), (16, 1, SEG, 0))
_FIRST = {1: 1, 4: 4, 16: 16}


def _group_rows(d, g):
    a = g >> 4
    if d == 16:
        base = a * HALF + (g & 15) * SEG
        prev = base - HALF
    elif d == 4:
        c = (g >> 2) & 3
        base = a * HALF + (g & 3) * SEG + c * 32
        prev = jnp.where(c > 0, base - 32, base - HALF + 96)
    else:
        c = g & 15
        base = a * HALF + c * 8
        prev = jnp.where(c > 0, base - 8, base - HALF + 120)
    return base, prev


def _load_rows(ref, base, n, rows, stride):
    parts = [ref[pl.ds(pl.multiple_of(base + j * stride, 8), rows), :] for j in range(n)]
    return parts[0] if n == 1 else jnp.concatenate(parts, axis=0)


def _store_rows(ref, base, val, n, rows, stride, add=False):
    for j in range(n):
        sl = pl.ds(pl.multiple_of(base + j * stride, 8), rows)
        piece = val[j * rows:(j + 1) * rows, :]
        if add:
            ref[sl, :] += piece
        else:
            ref[sl, :] = piece


def _band_bias(n, rows):
    shift = rows.bit_length() - 1
    lq = lax.broadcasted_iota(jnp.int32, (BLK, BLK), 0)
    lk = lax.broadcasted_iota(jnp.int32, (BLK, BLK), 1)
    iq = (lq & (rows - 1)) * n + (lq >> shift)
    ik = (lk & (rows - 1)) * n + (lk >> shift)
    zero = jnp.zeros((BLK, BLK), F32)
    return jnp.where(ik >= iq, zero, NEG_INF), jnp.where(ik <= iq, zero, NEG_INF)


def _set_bias(bias_scr, n, rows):
    prev_b, cur_b = _band_bias(n, rows)
    for half in range(2):
        bias_scr[half * BLK:(half + 1) * BLK, 0:BLK] = prev_b
        bias_scr[half * BLK:(half + 1) * BLK, BLK:2 * BLK] = cur_b


SCALE = 1.0 / math.sqrt(HEAD_DIM)


def _head_consts(value=1.0):
    lane_lo = lax.broadcasted_iota(jnp.int32, (BLK, BLK), 1) < HEAD_DIM
    return lane_lo, [jnp.where(lane_lo, value, 0.0).astype(BF16), jnp.where(lane_lo, 0.0, value).astype(BF16)]


def _stack_heads(v, head_mask):
    return jnp.concatenate([v * head_mask[0], v * head_mask[1]], axis=0)


def _unstack_heads(v2, lane_lo):
    return jnp.where(lane_lo, v2[:BLK], v2[BLK:])


def _rows_per_head(v, lane_lo):
    rolled = pltpu.roll(v, HEAD_DIM, axis=1)
    return jnp.concatenate([jnp.where(lane_lo, v, rolled), jnp.where(lane_lo, rolled, v)], axis=0)


WIDTH = 4


def _loop(lo, hi, fn, width=None):
    if width is None:
        def body(g, carry):
            fn(g)
            return carry

        if hi > lo:
            lax.fori_loop(lo, hi, body, 0)
        return
    while hi > lo:
        trips = (hi - lo) // width
        if trips:
            def body(i, carry, lo=lo, width=width):
                fn([lo + width * i + j for j in range(width)])
                return carry

            lax.fori_loop(0, trips, body, 0)
            lo += trips * width
        width = max(1, width // 2)


def _mix_weights(l1, l2, l3):
    mx = jnp.maximum(jnp.maximum(l1, l2), l3)
    e1, e2, e3 = jnp.exp(l1 - mx), jnp.exp(l2 - mx), jnp.exp(l3 - mx)
    inv = 1.0 / (e1 + e2 + e3)
    return e1 * inv, e2 * inv, e3 * inv


def _attention_fwd(qkv, rider=None):
    t = qkv.shape[0]
    groups = 16 * (t // HALF)

    def body(q_ref, k_ref, v_ref, attn_ref, l1_ref, l2_ref, l3_ref, o_scr, bias_scr):
        lane_lo, q_mask = _head_consts(SCALE)
        l_refs = (l1_ref, l2_ref, l3_ref)
        for p, (d, n, rows, stride) in enumerate(_PATTERNS):
            _set_bias(bias_scr, n, rows)
            o_p, l_p = o_scr.at[p], l_refs[p]

            def block(gs, has_prev):
                at = [_group_rows(d, g) for g in gs]

                def load(ref, b):
                    return _load_rows(ref, b, n, rows, stride).astype(BF16)

                q2 = [_stack_heads(load(q_ref, b), q_mask) for b, _ in at]
                k2 = [load(k_ref, b) for b, _ in at]
                v2 = [load(v_ref, b) for b, _ in at]
                if has_prev:
                    k2 = [jnp.concatenate([load(k_ref, pv), k], axis=0) for (_, pv), k in zip(at, k2)]
                    v2 = [jnp.concatenate([load(v_ref, pv), v], axis=0) for (_, pv), v in zip(at, v2)]
                s = [_dot_nt(q, k) for q, k in zip(q2, k2)]
                s = [x + (bias_scr[...] if has_prev else bias_scr[:, BLK:2 * BLK]) for x in s]
                mx = [jnp.max(x, axis=1, keepdims=True) for x in s]
                e = [jnp.exp(x - m) for x, m in zip(s, mx)]
                den = [jnp.sum(x, axis=1, keepdims=True) for x in e]
                o2 = [_dot(x.astype(BF16), v) * (1.0 / dn) for x, v, dn in zip(e, v2, den)]
                lse2 = [jnp.broadcast_to(m + jnp.log(dn), (2 * BLK, BLK)) for m, dn in zip(mx, den)]
                for (b, _), o, l in zip(at, o2, lse2):
                    _store_rows(o_p, b, _unstack_heads(o, lane_lo), n, rows, stride)
                    _store_rows(l_p, b, _unstack_heads(l, lane_lo), n, rows, stride)

            _loop(0, _FIRST[d], lambda gs: block(gs, False), width=2 * WIDTH)
            _loop(_FIRST[d], groups, lambda gs: block(gs, True), width=2 * WIDTH)

        def mix(i):
            sl = pl.ds(pl.multiple_of(i * 256, 256), 256)
            w = _mix_weights(l1_ref[sl, :], l2_ref[sl, :], l3_ref[sl, :])
            attn_ref[sl, :] = w[0] * o_scr[0, sl, :] + w[1] * o_scr[1, sl, :] + w[2] * o_scr[2, sl, :]

        _loop(0, t // 256, mix)

    def col(c0):
        return pl.BlockSpec((t, BLK), lambda hp: (0, c0 + hp))

    res, extra = _pcall(
        body, name="attention_fwd", grid=(4,), in_specs=[col(0), col(4), col(8)], out_specs=[col(0)] * 4,
        out_shape=[jax.ShapeDtypeStruct((t, 512), F32)] * 4,
        scratch_shapes=[pltpu.VMEM((3, t, BLK), F32), pltpu.VMEM((2 * BLK, 2 * BLK), F32)],
        semantics=("parallel",), vmem_mb=48, rider=rider,
    )(qkv, qkv, qkv)
    return res if rider is None else (res, extra)


def _attention_bwd(qkv, dattn, dsum, lses, dproj):
    t = qkv.shape[0]
    groups = 16 * (t // HALF)

    def body(q_ref, k_ref, v_ref, da_ref, ds_ref, l1_ref, l2_ref, l3_ref, kept_ref, out_ref, acc, bias_scr):
        del kept_ref
        lane_lo, head_mask = _head_consts()
        q_mask = _head_consts(SCALE)[1]
        l_refs = (l1_ref, l2_ref, l3_ref)

        def clear(i):
            sl = pl.ds(pl.multiple_of(i * 512, 512), 512)
            for s in range(3):
                acc[s, sl, :] = jnp.zeros((512, BLK), F32)

        _loop(0, t // 512, clear)
        dq_acc, dk_acc, dv_acc = acc.at[0], acc.at[1], acc.at[2]
        for p, (d, n, rows, stride) in enumerate(_PATTERNS):
            _set_bias(bias_scr, n, rows)

            def block(gs, has_prev):
                at = [_group_rows(d, g) for g in gs]

                def load(ref, b):
                    return _load_rows(ref, b, n, rows, stride)

                def put(ref, b, val):
                    _store_rows(ref, b, val, n, rows, stride, add=True)

                def wide(x):
                    return jnp.concatenate([x, x], axis=1) if has_prev else x

                lse = [[load(ref, b) for ref in l_refs] for b, _ in at]
                w = [_mix_weights(*ls)[p] for ls in lse]
                do2 = [_stack_heads((wg * load(da_ref, b)).astype(BF16), head_mask) for wg, (b, _) in zip(w, at)]
                dl2 = [wide(_rows_per_head(wg * load(ds_ref, b), lane_lo)) for wg, (b, _) in zip(w, at)]
                lse2 = [wide(_rows_per_head(ls[p], lane_lo)) for ls in lse]
                q2 = [_stack_heads(load(q_ref, b).astype(BF16), q_mask) for b, _ in at]
                k2 = [load(k_ref, b).astype(BF16) for b, _ in at]
                v2 = [load(v_ref, b).astype(BF16) for b, _ in at]
                if has_prev:
                    k2 = [jnp.concatenate([load(k_ref, pv).astype(BF16), k], axis=0) for (_, pv), k in zip(at, k2)]
                    v2 = [jnp.concatenate([load(v_ref, pv).astype(BF16), v], axis=0) for (_, pv), v in zip(at, v2)]
                s = [_dot_nt(q, k) for q, k in zip(q2, k2)]
                dp = [_dot_nt(do, v) for do, v in zip(do2, v2)]
                pr = [jnp.exp(x + (bias_scr[...] if has_prev else bias_scr[:, BLK:2 * BLK]) - l)
                      for x, l in zip(s, lse2)]
                ds = [(pg * (x - dl)).astype(BF16) for pg, x, dl in zip(pr, dp, dl2)]
                dq2 = [_dot(x, k) * SCALE for x, k in zip(ds, k2)]
                dk2 = [_dot_tn(x, q) for x, q in zip(ds, q2)]
                dv2 = [_dot_tn(pg.astype(BF16), do) for pg, do in zip(pr, do2)]
                for (b, pv), dq, dk, dv in zip(at, dq2, dk2, dv2):
                    put(dq_acc, b, _unstack_heads(dq, lane_lo))
                    if has_prev:
                        put(dk_acc, pv, dk[:BLK])
                        put(dv_acc, pv, dv[:BLK])
                        put(dk_acc, b, dk[BLK:])
                        put(dv_acc, b, dv[BLK:])
                    else:
                        put(dk_acc, b, dk)
                        put(dv_acc, b, dv)

            _loop(0, _FIRST[d], lambda gs: block(gs, False), width=WIDTH)
            _loop(_FIRST[d], groups, lambda gs: block(gs, True), width=WIDTH)

        def emit(i):
            sl = pl.ds(pl.multiple_of(i * 512, 512), 512)
            for s in range(3):
                out_ref[s, sl, :] = acc[s, sl, :].astype(BF16)

        _loop(0, t // 512, emit)

    def col(c0):
        return pl.BlockSpec((t, BLK), lambda hp: (0, c0 + hp))

    res, _ = _pcall(
        body, name="attention_bwd", grid=(4,),
        in_specs=[col(0), col(4), col(8)] + [col(0)] * 5 + [ANY],
        out_specs=[pl.BlockSpec((3, t, BLK), lambda hp: (0, 0, hp))],
        out_shape=[jax.ShapeDtypeStruct(dproj.shape, BF16)],
        scratch_shapes=[pltpu.VMEM((3, t, BLK), F32), pltpu.VMEM((2 * BLK, 2 * BLK), F32)],
        semantics=("parallel",), vmem_mb=56, aliases={8: 0},
    )(qkv, qkv, qkv, dattn, dsum, *lses, dproj)
    return res[0]


def _order_specs(t):
    n_i = SEG // TI
    nblk = (t // HALF) * n_i
    per = TI // HALO

    def main(c, col=0):
        return pl.BlockSpec((1, N_RES, TI, c), lambda s: (s // n_i, 0, s % n_i, col))

    def before(c, col=0):
        return pl.BlockSpec((1, 2, HALO, c), lambda s: (jnp.maximum(s - 1, 0) // n_i, N_RES // 2 - 1,
                                                        (jnp.maximum(s - 1, 0) % n_i) * per + per - 1, col))

    def after(c, col=0):
        return pl.BlockSpec((1, 2, HALO, c), lambda s: (jnp.minimum(s + 1, nblk - 1) // n_i, 0,
                                                        (jnp.minimum(s + 1, nblk - 1) % n_i) * per, col))

    return nblk, main, before, after


def _shift_in(v, row_in, up):
    rows = v.shape[0]
    idx = lax.broadcasted_iota(jnp.int32, v.shape, 0)
    fill = jnp.broadcast_to(row_in, v.shape)
    if up:
        return jnp.where(idx == rows - 1, fill, pltpu.roll(v, rows - 1, axis=0))
    return jnp.where(idx == 0, fill, pltpu.roll(v, 1, axis=0))


def _taps_behind(u, before):
    s15 = _shift_in(u[N_RES - 1], before[1, HALO - 1:HALO, :], up=False)
    s14 = _shift_in(u[N_RES - 2], before[0, HALO - 1:HALO, :], up=False)
    m1 = jnp.concatenate([s15[None], u[:N_RES - 1]], axis=0)
    m2 = jnp.concatenate([s14[None], s15[None], u[:N_RES - 2]], axis=0)
    return m1, m2


def _taps_ahead(u, after):
    t0 = _shift_in(u[0], after[0, 0:1, :], up=True)
    t1 = _shift_in(u[1], after[1, 0:1, :], up=True)
    p1 = jnp.concatenate([u[1:], t0[None]], axis=0)
    p2 = jnp.concatenate([u[2:], t0[None], t1[None]], axis=0)
    return p1, p2


def _conv_fwd(gates, before, first, cw):
    gates, before = gates.astype(F32), before.astype(F32)
    bg, cg, xc = gates[..., 0:512], gates[..., 512:1024], gates[..., 1024:1536]
    u = cg * xc
    ub = before[..., 512:1024] * before[..., 1024:1536]
    ub = jnp.where(first, jnp.zeros_like(ub), ub)
    m1, m2 = _taps_behind(u, ub)
    conv = m2 * cw[0:1, :] + m1 * cw[1:2, :] + u * cw[2:3, :]
    return bg, u, m1, m2, conv


def _sum_tokens(v):
    return jnp.sum(jnp.sum(v, axis=0), axis=0, keepdims=True)


def _mixer_fwd(x, attn, gates, cw, g_a, g_c, w_out):
    t, d = x.shape
    nblk, main, before, _ = _order_specs(t)
    rows = N_RES * TI

    def body(x_ref, at_ref, gt_ref, gb_ref, cw_ref, ga_ref, gc_ref, wa_ref, wb_ref, x1_ref, mg_ref):
        an = _rms_fwd(at_ref[0], ga_ref[...])[0].astype(BF16)
        bg, _, _, _, conv = _conv_fwd(gt_ref[0], gb_ref[0], pl.program_id(0) == 0, cw_ref[...])
        cn = _rms_fwd(bg * conv, gc_ref[...])[0].astype(BF16)
        mg_ref[0, :, :, 0:512] = an
        mg_ref[0, :, :, 512:1024] = cn
        y = _dot(an.reshape(rows, 512), wa_ref[...]) + _dot(cn.reshape(rows, 512), wb_ref[...])
        x1_ref[0] = x_ref[0] + y.reshape(N_RES, TI, d)

    const = lambda r, c, i0=0: pl.BlockSpec((r, c), lambda s: (i0, 0))
    x1, merged = pl.pallas_call(
        body, name="mixer_fwd", grid=(nblk,),
        in_specs=[main(d), main(512), main(1536), before(1536), const(3, 512), const(1, 512), const(1, 512),
                  const(512, d), const(512, d, 1)],
        out_specs=[main(d), main(d)],
        out_shape=[jax.ShapeDtypeStruct(_x4(x).shape, F32), jax.ShapeDtypeStruct(_x4(x).shape, BF16)],
        compiler_params=_params(("parallel",), 48),
    )(_x4(x), _x4(attn), _x4(gates), _x4(gates), cw, g_a, g_c, w_out, w_out)
    return x1.reshape(t, d), merged.reshape(t, d)


def _mixer_bwd(dx1, merged, attn, gates, cw, g_a, g_c, w_out, head_sum, after=()):
    t, d = dx1.shape
    nblk, main, before, _ = _order_specs(t)
    rows = N_RES * TI

    def body(dx_ref, mg_ref, at_ref, gt_ref, gb_ref, cw_ref, ga_ref, gc_ref, wa_ref, wb_ref, hs_ref,
             da_ref, dsum_ref, dy_ref, gga_ref, ggc_ref, gw_ref, acc_w):
        s = pl.program_id(0)
        dxb = dx_ref[0].reshape(rows, d).astype(BF16)

        @pl.when(s == 0)
        def _():
            acc_w[...] = jnp.zeros_like(acc_w)

        acc_w[...] += _dot_tn(mg_ref[0].reshape(rows, d), dxb)

        @pl.when(s == nblk - 1)
        def _():
            gw_ref[...] = acc_w[...].astype(BF16)

        dma = _dot_nt(dxb, wa_ref[...]).reshape(N_RES, TI, 512)
        dmc = _dot_nt(dxb, wb_ref[...]).reshape(N_RES, TI, 512)
        attn_v, g_av = at_ref[0], ga_ref[...]
        _, ah, ra = _rms_fwd(attn_v, g_av)
        dattn = _rms_bwd(dma, ah, ra, g_av)
        da_ref[0] = dattn
        z = (dattn * attn_v).reshape(rows, 512)
        hs = hs_ref[...]
        z1 = z.astype(BF16)
        z2 = (z - z1.astype(F32)).astype(BF16)
        dsum_ref[0] = (_dot(z1, hs) + _dot(z2, hs)).reshape(N_RES, TI, 512)
        bg, _, _, _, conv = _conv_fwd(gt_ref[0], gb_ref[0], s == 0, cw_ref[...])
        g_cv = gc_ref[...]
        _, yh, rc = _rms_fwd(bg * conv, g_cv)
        dy_ref[0] = _rms_bwd(dmc, yh, rc, g_cv)
        pa, pc = _sum_tokens(dma * ah), _sum_tokens(dmc * yh)

        @pl.when(s == 0)
        def _():
            gga_ref[...] = pa
            ggc_ref[...] = pc

        @pl.when(s != 0)
        def _():
            gga_ref[...] += pa
            ggc_ref[...] += pc

    const = lambda r, c, i0=0: pl.BlockSpec((r, c), lambda s: (i0, 0))
    shape4 = _x4(attn).shape
    res, _ = _pcall(
        body, name="mixer_bwd", grid=(nblk,),
        in_specs=[main(d), main(d), main(512), main(1536), before(1536), const(3, 512), const(1, 512), const(1, 512),
                  const(512, d), const(512, d, 1), const(512, 512)],
        out_specs=[main(512)] * 3 + [const(1, 512), const(1, 512), const(d, d)],
        out_shape=[jax.ShapeDtypeStruct(shape4, F32)] * 3 + [jax.ShapeDtypeStruct((1, 512), F32)] * 2
        + [jax.ShapeDtypeStruct((d, d), BF16)],
        scratch_shapes=[pltpu.VMEM((d, d), F32)],
        semantics=("arbitrary",), vmem_mb=48, after=after,
    )(_x4(dx1), _x4(merged), _x4(attn), _x4(gates), _x4(gates), cw, g_a, g_c, w_out, w_out, head_sum)
    return [r.reshape(t, 512) for r in res[:3]] + res[3:]


def _conv_bwd(dy, gates, cw, after=()):
    t = dy.shape[0]
    nblk, main, before, ahead = _order_specs(t)
    n_i = SEG // TI

    def body(dy_ref, dya_ref, gt_ref, gb_ref, ga_ref, cw_ref, dp_ref, gcw_ref):
        s = pl.program_id(0)
        cw_v, gates_v = cw_ref[...], gt_ref[0]
        bg, u, m1, m2, conv = _conv_fwd(gates_v, gb_ref[0], s == 0, cw_v)
        dy_v = dy_ref[0]
        dconv = dy_v * bg
        dca = dya_ref[0] * ga_ref[0][..., 0:512].astype(F32)
        dca = jnp.where(s == nblk - 1, jnp.zeros_like(dca), dca)
        p1, p2 = _taps_ahead(dconv, dca)
        du = dconv * cw_v[2:3, :] + p1 * cw_v[1:2, :] + p2 * cw_v[0:1, :]
        dp_ref[0, 0] = (dy_v * conv).astype(BF16)
        dp_ref[1, 0] = (du * gates_v[..., 1024:1536].astype(F32)).astype(BF16)
        dp_ref[2, 0] = (du * gates_v[..., 512:1024].astype(F32)).astype(BF16)
        parts = [_sum_tokens(dconv * m2), _sum_tokens(dconv * m1), _sum_tokens(dconv * u)]

        @pl.when(s == 0)
        def _():
            gcw_ref[...] = jnp.zeros_like(gcw_ref)

        for tap in range(3):
            gcw_ref[tap:tap + 1, :] += parts[tap]

    (dproj, gcw), _ = _pcall(
        body, name="conv_bwd", grid=(nblk,),
        in_specs=[main(512), ahead(512), main(1536), before(1536), ahead(1536),
                  pl.BlockSpec((3, 512), lambda s: (0, 0))],
        out_specs=[pl.BlockSpec((3, 1, N_RES, TI, 512), lambda s: (1, s // n_i, 0, s % n_i, 0)),
                   pl.BlockSpec((8, 512), lambda s: (0, 0))],
        out_shape=[jax.ShapeDtypeStruct((6, t // HALF, N_RES, SEG, 512), BF16), jax.ShapeDtypeStruct((8, 512), F32)],
        semantics=("arbitrary",), vmem_mb=40, after=after,
    )(_x4(dy), _x4(dy), _x4(gates), _x4(gates), _x4(gates), cw)
    return dproj.reshape(6, t, 512), gcw


def _xattn_fwd(x1, g, w_q, kv, w_o, *, tb):
    t, d = x1.shape
    hd = d // N_MEM_HEADS
    m = kv.shape[0]

    def body(x_ref, g_ref, wq_ref, k_ref, v_ref, wo_ref, x2_ref, h_ref, q_ref, o_ref):
        xv = x_ref[...]
        h = _rms_fwd(xv, g_ref[...])[0].astype(BF16)
        h_ref[...] = h
        q = _dot(h, wq_ref[...]).astype(BF16)
        q_ref[...] = q
        for hh in range(N_MEM_HEADS):
            sl = slice(hh * hd, (hh + 1) * hd)
            s = _dot_nt(q[:, sl], k_ref[:, sl]) * (1.0 / 16.0)
            e = jnp.exp(s - jnp.max(s, axis=1, keepdims=True))
            p = e / jnp.sum(e, axis=1, keepdims=True)
            o_ref[:, sl] = _dot(p.astype(BF16), v_ref[:, sl]).astype(BF16)
        x2_ref[...] = xv + _dot(o_ref[...], wo_ref[...])

    tok = pl.BlockSpec((tb, d), lambda i: (i, 0))
    full = pl.BlockSpec((d, d), lambda i: (0, 0))
    return pl.pallas_call(
        body, name="xattn_fwd", grid=(t // tb,),
        in_specs=[tok, pl.BlockSpec((1, d), lambda i: (0, 0)), full,
                  pl.BlockSpec((m, d), lambda i: (0, 0)), pl.BlockSpec((m, d), lambda i: (0, 1)), full],
        out_specs=[tok] * 4,
        out_shape=[jax.ShapeDtypeStruct((t, d), F32)] + [jax.ShapeDtypeStruct((t, d), BF16)] * 3,
        compiler_params=_params(("parallel",), 48),
    )(x1, g, w_q, kv, kv, w_o)


def _xattn_bwd(dx2, x1, g, q, w_q, kv, w_o, *, tb, after=()):
    t, d = x1.shape
    hd = d // N_MEM_HEADS
    m = kv.shape[0]

    def body(dx2_ref, x_ref, g_ref, q_ref, wq_ref, k_ref, v_ref, wo_ref,
             dx1_ref, dq_ref, dk_ref, dv_ref, gg_ref):
        i = pl.program_id(0)

        @pl.when(i == 0)
        def _():
            dk_ref[...] = jnp.zeros_like(dk_ref)
            dv_ref[...] = jnp.zeros_like(dv_ref)

        dx2 = dx2_ref[...]
        do = _dot_nt(dx2.astype(BF16), wo_ref[...]).astype(BF16)
        for hh in range(N_MEM_HEADS):
            sl = slice(hh * hd, (hh + 1) * hd)
            qh, kh, vh, doh = q_ref[:, sl], k_ref[:, sl], v_ref[:, sl], do[:, sl]
            s = _dot_nt(qh, kh) * (1.0 / 16.0)
            e = jnp.exp(s - jnp.max(s, axis=1, keepdims=True))
            p = e / jnp.sum(e, axis=1, keepdims=True)
            dp = _dot_nt(doh, vh)
            ds = (p * (dp - jnp.sum(dp * p, axis=1, keepdims=True)) * (1.0 / 16.0)).astype(BF16)
            dq_ref[:, sl] = _dot(ds, kh).astype(BF16)
            dk_ref[:, sl] += _dot_tn(ds, qh)
            dv_ref[:, sl] += _dot_tn(p.astype(BF16), doh)
        dh = _dot_nt(dq_ref[...], wq_ref[...])
        g_v = g_ref[...]
        _, xh, r = _rms_fwd(x_ref[...], g_v)
        dx1 = dx2 + _rms_bwd(dh, xh, r, g_v)
        dx1_ref[...] = dx1
        part = jnp.sum(dh * xh, axis=0, keepdims=True)

        @pl.when(i == 0)
        def _():
            gg_ref[...] = part

        @pl.when(i != 0)
        def _():
            gg_ref[...] += part

    tok = pl.BlockSpec((tb, d), lambda i: (i, 0))
    full = pl.BlockSpec((d, d), lambda i: (0, 0))
    acc = pl.BlockSpec((m, d), lambda i: (0, 0))
    res, _ = _pcall(
        body, name="xattn_bwd", grid=(t // tb,),
        in_specs=[tok, tok, pl.BlockSpec((1, d), lambda i: (0, 0)), tok, full,
                  pl.BlockSpec((m, d), lambda i: (0, 0)), pl.BlockSpec((m, d), lambda i: (0, 1)), full],
        out_specs=[tok, tok, acc, acc, pl.BlockSpec((1, d), lambda i: (0, 0))],
        out_shape=[jax.ShapeDtypeStruct((t, d), F32), jax.ShapeDtypeStruct((t, d), BF16),
                   jax.ShapeDtypeStruct((m, d), F32), jax.ShapeDtypeStruct((m, d), F32),
                   jax.ShapeDtypeStruct((1, d), F32)],
        semantics=("arbitrary",), vmem_mb=48, after=after,
    )(dx2, x1, g, q, w_q, kv, kv, w_o)
    return res


def _mem_bwd(dk, dv, w_kv, mem, mem_n, g):
    m, d = mem.shape

    def body(dk_ref, dv_ref, w_ref, x_ref, h_ref, g_ref, gw_ref, gg_ref):
        h = h_ref[...]
        dh = jnp.zeros((m, d), F32)
        for i, dy_ref in enumerate((dk_ref, dv_ref)):
            cols = slice(i * d, (i + 1) * d)
            dy = dy_ref[...].astype(BF16)
            gw_ref[:, cols] = _dot_tn(h, dy).astype(BF16)
            dh = dh + _dot_nt(dy, w_ref[:, cols])
        xh = _rms_fwd(x_ref[...], g_ref[...])[1]
        gg_ref[...] = jnp.sum(dh * xh, axis=0, keepdims=True)

    return pl.pallas_call(
        body, name="mem_bwd",
        out_shape=[jax.ShapeDtypeStruct(w_kv.shape, BF16), jax.ShapeDtypeStruct((1, d), F32)],
        compiler_params=pltpu.CompilerParams(vmem_limit_bytes=32 << 20),
    )(dk, dv, w_kv, mem, mem_n, g)


def _mlp_down_loss(a, w_down, x2, tgt, g, *, tb):
    t, d = x2.shape
    f = a.shape[1]

    def body(a_ref, w_ref, x_ref, t_ref, g_ref, dx_ref, dxb_ref, loss_ref, gg_ref):
        i = pl.program_id(0)
        av = a_ref[...].astype(F32)
        x3 = x_ref[...] + _dot((av * av).astype(BF16), w_ref[...])
        g_v = g_ref[...]
        out, xh, r = _rms_fwd(x3, g_v)
        err = out - t_ref[...]
        dout = err * (1.0 / d)
        dx = _rms_bwd(dout, xh, r, g_v)
        dx_ref[...] = dx
        dxb_ref[...] = dx.astype(BF16)
        part = jnp.sum(dout * xh, axis=0, keepdims=True)
        lpart = 0.5 * jnp.sum(jnp.mean(err * err, axis=-1, keepdims=True), axis=0, keepdims=True)
        lpart = jnp.broadcast_to(lpart, loss_ref.shape)

        @pl.when(i == 0)
        def _():
            gg_ref[...] = part
            loss_ref[...] = lpart

        @pl.when(i != 0)
        def _():
            gg_ref[...] += part
            loss_ref[...] += lpart

    tok = pl.BlockSpec((tb, d), lambda i: (i, 0))
    return pl.pallas_call(
        body, name="mlp_down_loss", grid=(t // tb,),
        in_specs=[pl.BlockSpec((tb, f), lambda i: (i, 0)), pl.BlockSpec((f, d), lambda i: (0, 0)), tok, tok,
                  pl.BlockSpec((1, d), lambda i: (0, 0))],
        out_specs=[tok, tok, pl.BlockSpec((8, 128), lambda i: (0, 0)), pl.BlockSpec((1, d), lambda i: (0, 0))],
        out_shape=[jax.ShapeDtypeStruct((t, d), F32), jax.ShapeDtypeStruct((t, d), BF16),
                   jax.ShapeDtypeStruct((8, 128), F32), jax.ShapeDtypeStruct((1, d), F32)],
        compiler_params=_params(("arbitrary",), 56),
    )(a, w_down, x2, tgt, g)


def _mlp_dpre(dx3, w_down, a, *, tb, bn):
    t, d = dx3.shape
    f = a.shape[1]

    def body(dx_ref, w_ref, a_ref, o_ref):
        o_ref[...] = (2.0 * a_ref[...].astype(F32) * _dot_nt(dx_ref[...], w_ref[...])).astype(BF16)

    return pl.pallas_call(
        body, name="mlp_dpre", grid=(t // tb, f // bn),
        in_specs=[pl.BlockSpec((tb, d), lambda i, j: (i, 0)), pl.BlockSpec((bn, d), lambda i, j: (j, 0)),
                  pl.BlockSpec((tb, bn), lambda i, j: (i, j))],
        out_specs=pl.BlockSpec((tb, bn), lambda i, j: (i, j)),
        out_shape=jax.ShapeDtypeStruct((t, f), BF16),
        compiler_params=_params(("parallel", "arbitrary"), 48),
    )(dx3, w_down, a)


def _adamw(gsum, w, m, v):
    m_new = ADAM_B1 * m + (1.0 - ADAM_B1) * gsum
    v_new = ADAM_B2 * v + (1.0 - ADAM_B2) * (gsum * gsum)
    m_hat = m_new / (1.0 - ADAM_B1 ** ADAM_STEP)
    v_hat = v_new / (1.0 - ADAM_B2 ** ADAM_STEP)
    delta = -ADAM_LR * (m_hat / (jnp.sqrt(v_hat) + ADAM_EPS) + ADAM_WD * w)
    return delta, m_new, v_new


def _sum_adamw(parts, w, m, v, *, name, tr):
    r, c = w.shape

    def body(p_ref, w_ref, m_ref, v_ref, g_ref, d_ref, mo_ref, vo_ref):
        g = p_ref[0].astype(F32)
        for k in range(1, N_DEV):
            g = g + p_ref[k].astype(F32)
        g_ref[...] = g
        d_ref[...], mo_ref[...], vo_ref[...] = _adamw(g, w_ref[...], m_ref[...], v_ref[...])

    blk = pl.BlockSpec((tr, c), lambda i: (i, 0))
    return pl.pallas_call(
        body, name=name, grid=(r // tr,),
        in_specs=[pl.BlockSpec((N_DEV, tr, c), lambda i: (0, i, 0)), blk, blk, blk],
        out_specs=[blk] * 4, out_shape=[jax.ShapeDtypeStruct((r, c), F32)] * 4,
        compiler_params=_params(("parallel",), 40),
    )(*[pltpu.with_memory_space_constraint(a, pltpu.HBM) for a in (parts, w, m, v)])


_GAIN_ROWS = ("g_mix", "g_xattn", "g_mem", "g_mlp", "g_final")
PAIR_ROW = 5
LOSS_ROW = 6
TAPS_ROW = 8
SMALL_ROWS = 16
_SMALL = _GAIN_ROWS + ("g_attn_out", "g_conv_out", "conv_w")
CONV_SHARD = 512 // N_DEV


def _pack_small(gains, gg_attn, gg_conv, gcw, loss_blk):
    def body(*refs):
        o_ref = refs[-1]
        ga_ref, gc_ref, cw_ref, l_ref = refs[len(gains):-1]
        o_ref[...] = jnp.zeros_like(o_ref)
        for i, g_ref in enumerate(refs[:len(gains)]):
            o_ref[i:i + 1, :] = g_ref[...]
        o_ref[PAIR_ROW:PAIR_ROW + 1, 0:512] = ga_ref[...]
        o_ref[PAIR_ROW:PAIR_ROW + 1, 512:1024] = gc_ref[...]
        o_ref[LOSS_ROW:LOSS_ROW + 1, 0:BLK] = l_ref[0:1, :]
        o_ref[TAPS_ROW:SMALL_ROWS, 0:512] = cw_ref[...]

    return pl.pallas_call(body, name="pack_small", out_shape=jax.ShapeDtypeStruct((SMALL_ROWS, 1024), F32))(
        *gains, gg_attn, gg_conv, gcw, loss_blk)


def _update_small(parts, me, w, m, v):
    n = len(_SMALL)

    def body(me_ref, p_ref, *refs):
        ins, loss_ref, outs = refs[:3 * n], refs[3 * n], refs[3 * n + 1:]

        def total(lo, hi):
            s = p_ref[0, lo:hi, :]
            for k in range(1, N_DEV):
                s = s + p_ref[k, lo:hi, :]
            return s

        grads = {k: total(i, i + 1) for i, k in enumerate(_GAIN_ROWS)}
        both = total(PAIR_ROW, PAIR_ROW + 1)
        grads["g_attn_out"], grads["g_conv_out"] = both[:, 0:512], both[:, 512:1024]
        taps = total(TAPS_ROW, SMALL_ROWS)
        mine = jnp.zeros((SMALL_ROWS - TAPS_ROW, BLK), F32)
        for j in range(N_DEV):
            lo = j * CONV_SHARD // BLK * BLK
            blk = taps[:, lo:lo + BLK]
            if j * CONV_SHARD != lo:
                blk = pltpu.roll(blk, BLK - (j * CONV_SHARD - lo), axis=1)
            mine = jnp.where(me_ref[0] == j, blk, mine)
        grads["conv_w"] = mine[0:3, 0:CONV_SHARD]
        loss_ref[...] = total(LOSS_ROW, LOSS_ROW + 1)[:, 0:1]
        for i, k in enumerate(_SMALL):
            g_ref, d_ref, mo_ref, vo_ref = outs[4 * i:4 * i + 4]
            g_ref[...] = grads[k]
            d_ref[...], mo_ref[...], vo_ref[...] = _adamw(grads[k], ins[i][...], ins[n + i][...], ins[2 * n + i][...])

    vmem = pl.BlockSpec(memory_space=pltpu.VMEM)
    args = [d[k] for d in (w, m, v) for k in _SMALL]
    res = pl.pallas_call(
        body, name="update_small",
        in_specs=[pl.BlockSpec(memory_space=pltpu.SMEM)] + [vmem] * (1 + 3 * n),
        out_shape=[jax.ShapeDtypeStruct((1, 1), F32)] + [jax.ShapeDtypeStruct(w[k].shape, F32) for k in _SMALL
                                                         for _ in range(4)],
    )(me, parts, *args)
    return res[0], {k: res[1 + 4 * i:5 + 4 * i] for i, k in enumerate(_SMALL)}


def _head_sum_matrix():
    r = lax.broadcasted_iota(jnp.int32, (512, 512), 0) // HEAD_DIM
    c = lax.broadcasted_iota(jnp.int32, (512, 512), 1) // HEAD_DIM
    return (r == c).astype(BF16)


_SHARD_AXIS = dict(w_in=1, w_out=0, w_q=0, w_kv=1, w_o=0, w_up=1, w_down=0, conv_w=None, small=None)


class _Weights:
    def __init__(self, full, shards=None):
        self.full = dict(full)
        self.shards = shards

    def rider(self, names, late=False):
        if self.shards is None:
            return None
        return _Gather([self.shards[n] for n in names], [_SHARD_AXIS[n] for n in names], late)

    def arrived(self, names, gathered):
        if gathered is not None:
            for n, g in zip(names, gathered):
                self.full[n] = g.transpose(1, 0, 2).reshape(g.shape[1], -1) if n == "conv_w" else g

    def __getitem__(self, name):
        return self.full[name]


class _Grads:
    def __init__(self, distributed):
        self.distributed = distributed
        self.local = {}
        self.pending = {}

    def add(self, name, g):
        self.local[name] = g

    def send(self, *names):
        if not self.distributed:
            return []
        rider = _Exchange([self.local[n] for n in names], [_SHARD_AXIS[n] for n in names])
        started = _exchange_start(rider, "send_" + "_".join(names))
        self.pending[names[0]] = (names, rider, started)
        return [started[3]]

    def wait(self, first_name, after):
        names, rider, started = self.pending.pop(first_name)
        return _exchange_wait(rider, started, after, "wait_" + "_".join(names))


def _ride(fn, *args, rider=None, **kw):
    if rider is None:
        return fn(*args, **kw), None
    return fn(*args, rider=rider, **kw)


def _local_step(x, mem, tgt, gains, weights, grads):
    names = ["w_in", "conv_w"]
    (x, tgt), got = _ride(_reorder, [x, tgt], "reorder_in", rider=weights.rider(names, late=True))
    weights.arrived(names, got)
    w_in, cw = weights["w_in"], weights["conv_w"]

    names = ["w_out", "w_kv"]
    (qkv, gates, h1), got = _ride(_proj, x, gains["g_mix"], w_in, tb=1024, rider=weights.rider(names))
    weights.arrived(names, got)
    names = ["w_q", "w_o", "w_up"]
    (attn, *lses), got = _ride(_attention_fwd, qkv, rider=weights.rider(names))
    weights.arrived(names, got)
    x1, merged = _mixer_fwd(x, attn, gates, cw, gains["g_attn_out"], gains["g_conv_out"], weights["w_out"])
    kv, mem_n = _norm_matmul(mem, gains["g_mem"], weights["w_kv"], name="mem_kv", out_dtype=BF16, tb=mem.shape[0],
                             bn=1024, save_h=True)
    x2, h2, qm, om = _xattn_fwd(x1, gains["g_xattn"], weights["w_q"], kv, weights["w_o"], tb=512)
    w_up = weights["w_up"]
    (a, h3), got = _ride(_norm_matmul, x2, gains["g_mlp"], w_up, name="mlp_up", out_dtype=BF16, tb=1024, bn=2048,
                         relu=True, save_h=True, rider=weights.rider(["w_down"], late=True))
    weights.arrived(["w_down"], got)
    w_down = weights["w_down"]
    dx3, dx3b, loss_blk, gg_final = _mlp_down_loss(a, w_down, x2, tgt, gains["g_final"], tb=512)

    dpre = _mlp_dpre(dx3b, w_down, a, tb=1024, bn=2048)
    grads.add("w_down", _matmul_tn(a, dx3b, name="grad_w_down", bm=512, bn=1024, square_a=True))
    sent = grads.send("w_down")
    grads.add("w_up", _matmul_tn(h3, dpre, name="grad_w_up", bm=1024, bn=1024, after=sent))
    sent = grads.send("w_up")
    dx2, dx2b, gg_mlp = _matmul_nt_normbwd(dpre, w_up, x2, gains["g_mlp"], dx3, name="mlp_dx", tb=512,
                                           also_bf16=True, after=sent)

    grads.add("w_o", _matmul_tn(om, dx2b, name="grad_w_o", bm=512, bn=512))
    dx1, dqm, dk, dv, gg_xattn = _xattn_bwd(dx2, x1, gains["g_xattn"], qm, weights["w_q"], kv, weights["w_o"], tb=512)
    grads.add("w_q", _matmul_tn(h2, dqm, name="grad_w_q", bm=1024, bn=512))
    gw_kv, gg_mem = _mem_bwd(dk, dv, weights["w_kv"], mem, mem_n, gains["g_mem"])
    grads.add("w_kv", gw_kv)

    dattn, dsum, dy, gg_attn, gg_conv, gw_out = _mixer_bwd(dx1, merged, attn, gates, cw, gains["g_attn_out"],
                                                           gains["g_conv_out"], weights["w_out"], _head_sum_matrix())
    grads.add("w_out", gw_out)
    sent = grads.send("w_o", "w_q", "w_kv", "w_out")
    dproj, gcw = _conv_bwd(dy, gates, cw, after=sent)
    dproj = _attention_bwd(qkv, dattn, dsum, lses, dproj)
    grads.add("w_in", _matmul_tn(h1, dproj, name="grad_w_in", bm=1024, bn=512))
    sent = grads.send("w_in")
    grad_x, gg_mix = _matmul_nt_normbwd(dproj, w_in, x, gains["g_mix"], dx1, name="mixer_dx", tb=512,
                                        to_natural=True, after=sent)

    grads.add("small", _pack_small([gg_mix, gg_xattn, gg_mem, gg_mlp, gg_final], gg_attn, gg_conv, gcw, loss_blk))
    return grad_x


_BIG = ("w_in", "w_out", "w_q", "w_kv", "w_o", "w_up", "w_down")


def kernel(x, mem, g_mix, w_in, conv_w, g_attn_out, g_conv_out, w_out, g_xattn, g_mem, w_q_mem, w_kv_mem, w_o_mem, g_mlp, w_up, w_down, g_final, loss_target, m_g_mix, m_w_in, m_conv_w, m_g_attn_out, m_g_conv_out, m_w_out, m_g_xattn, m_g_mem, m_w_q_mem, m_w_kv_mem, m_w_o_mem, m_g_mlp, m_w_up, m_w_down, m_g_final, v_g_mix, v_w_in, v_conv_w, v_g_attn_out, v_g_conv_out, v_w_out, v_g_xattn, v_g_mem, v_w_q_mem, v_w_kv_mem, v_w_o_mem, v_g_mlp, v_w_up, v_w_down, v_g_final):
    d = x.shape[-1]
    me = 4 * lax.axis_index("x") + 2 * lax.axis_index("y") + lax.axis_index("c")
    w_shards = dict(w_in=w_in, w_out=w_out, w_q=w_q_mem, w_kv=w_kv_mem, w_o=w_o_mem, w_up=w_up, w_down=w_down)
    m_shards = dict(w_in=m_w_in, w_out=m_w_out, w_q=m_w_q_mem, w_kv=m_w_kv_mem, w_o=m_w_o_mem, w_up=m_w_up,
                    w_down=m_w_down)
    v_shards = dict(w_in=v_w_in, w_out=v_w_out, w_q=v_w_q_mem, w_kv=v_w_kv_mem, w_o=v_w_o_mem, w_up=v_w_up,
                    w_down=v_w_down)
    gains = dict(g_mix=g_mix, g_attn_out=g_attn_out, g_conv_out=g_conv_out, g_xattn=g_xattn, g_mem=g_mem,
                 g_mlp=g_mlp, g_final=g_final)
    gains2 = {k: v.reshape(1, -1) for k, v in gains.items()}

    shards = {k: w_shards[k].astype(BF16) for k in _BIG}
    shards["conv_w"] = conv_w
    grads = _Grads(distributed=True)
    grad_x = _local_step(x[0], mem[0], loss_target[0], gains2, _Weights({}, shards), grads)

    after = grads.send("small")
    outs = {}
    tiles = dict(w_in=256, w_out=128, w_q=128, w_kv=256, w_o=128, w_up=256, w_down=256)
    for group in (("w_down",), ("w_up",), ("w_o", "w_q", "w_kv", "w_out"), ("w_in",)):
        for k, received in zip(group, grads.wait(group[0], after)):
            outs[k] = _sum_adamw(received, w_shards[k], m_shards[k], v_shards[k], name=f"adamw_{k}", tr=tiles[k])
            after = [outs[k][0]]
    small_received, = grads.wait("small", after)

    m_small = dict(g_mix=m_g_mix, g_attn_out=m_g_attn_out, g_conv_out=m_g_conv_out, g_xattn=m_g_xattn,
                   g_mem=m_g_mem, g_mlp=m_g_mlp, g_final=m_g_final)
    v_small = dict(g_mix=v_g_mix, g_attn_out=v_g_attn_out, g_conv_out=v_g_conv_out, g_xattn=v_g_xattn,
                   g_mem=v_g_mem, g_mlp=v_g_mlp, g_final=v_g_final)
    as_rows = lambda vals, conv: dict({k: a.reshape(1, -1) for k, a in vals.items()}, conv_w=conv)
    loss, small_out = _update_small(small_received, me.reshape(1), as_rows(gains, conv_w),
                                    as_rows(m_small, m_conv_w), as_rows(v_small, v_conv_w))
    small_out = {k: [a.reshape(dict(gains, conv_w=conv_w)[k].shape) for a in res] for k, res in small_out.items()}
    names = {"g_mix": "g_mix", "w_in": "w_in", "conv_w": "conv_w", "g_attn_out": "g_attn_out",
             "g_conv_out": "g_conv_out", "w_out": "w_out", "g_xattn": "g_xattn", "g_mem": "g_mem",
             "w_q_mem": "w_q", "w_kv_mem": "w_kv", "w_o_mem": "w_o", "g_mlp": "g_mlp", "w_up": "w_up",
             "w_down": "w_down", "g_final": "g_final"}
    result = [loss.reshape(()), grad_x[None]]
    for which in range(4):
        for key in names.values():
            result.append(outs[key][which] if key in outs else small_out[key][which])
    return tuple(result)
```

```python
import math

import jax
import jax.numpy as jnp
from jax import lax
from jax.experimental import pallas as pl
from jax.experimental.pallas import tpu as pltpu

F32 = jnp.float32
BF16 = jnp.bfloat16
NORM_EPS = 1e-6
NEG_INF = -1e30
N_DEV = 8
BLK = 128
HEAD_DIM = 64
N_MEM_HEADS = 4
ADAM_LR = 0.001
ADAM_B1 = 0.9
ADAM_B2 = 0.999
ADAM_EPS = 1e-08
ADAM_WD = 0.01
ADAM_STEP = 10
MESH = pl.DeviceIdType.MESH
ANY = pl.BlockSpec(memory_space=pl.ANY)


def _dot(a, b):
    return jnp.dot(a, b, preferred_element_type=F32)


def _dot_nt(a, b):
    return lax.dot_general(a, b, (((1,), (1,)), ((), ())), preferred_element_type=F32)


def _dot_tn(a, b):
    return lax.dot_general(a, b, (((0,), (0,)), ((), ())), preferred_element_type=F32)


def _params(semantics, vmem_mb):
    return pltpu.CompilerParams(dimension_semantics=semantics, vmem_limit_bytes=vmem_mb << 20)


def _rms_fwd(x, g):
    r = lax.rsqrt(jnp.mean(x * x, axis=-1, keepdims=True) + NORM_EPS)
    xh = x * r
    return xh * g, xh, r


def _rms_bwd(dy, xh, r, g):
    gy = dy * g
    return r * (gy - xh * jnp.mean(xh * gy, axis=-1, keepdims=True))


def _position():
    x, y, c = lax.axis_index("x"), lax.axis_index("y"), lax.axis_index("c")
    return x, y, c


def _block_of(ref, j, axis, shard_shape):
    r, c = shard_shape
    if axis is None:
        return ref.at[j]
    if axis == 0:
        return ref.at[pl.ds(j * r, r), :]
    return ref.at[:, pl.ds(j * c, c)]


class _Gather:
    has_mid = True
    alias_pairs = ()

    def __init__(self, shards, axes, late=False):
        self.arrays = list(shards)
        self.axes = list(axes)
        self.late = late
        self.n = len(self.arrays)

    def out_shape(self):
        res = []
        for s, axis in zip(self.arrays, self.axes):
            r, c = s.shape
            shape = (N_DEV, r, c) if axis is None else (N_DEV * r, c) if axis == 0 else (r, N_DEV * c)
            res.append(jax.ShapeDtypeStruct(shape, s.dtype))
        return res

    def scratch(self):
        return [pltpu.SemaphoreType.DMA((self.n, 7)), pltpu.SemaphoreType.DMA((self.n, 7)),
                pltpu.SemaphoreType.DMA((self.n,))]

    def _ctx(self, ins, outs, sems):
        send_sems, recv_sems, local_sems = sems
        x, y, c = _position()
        me, sibling = (x, y, c), (x, y, 1 - c)
        chips = [(1 - x, y), (x, 1 - y), (1 - x, 1 - y)]

        def lin(px, py, pc):
            return 4 * px + 2 * py + pc

        def place(a, block):
            return _block_of(outs[a], lin(*block), self.axes[a], self.arrays[a].shape)

        def copy(a, k, block, to, src=None):
            dst = place(a, block)
            return pltpu.make_async_remote_copy(
                src_ref=dst if src is None else src, dst_ref=dst,
                send_sem=send_sems.at[a, k], recv_sem=recv_sems.at[a, k],
                device_id=to, device_id_type=MESH)

        def mine():
            return [pltpu.make_async_copy(ins[a], place(a, me), local_sems.at[a]) for a in range(self.n)]

        def first():
            res = []
            for a in range(self.n):
                res.append(copy(a, 0, me, sibling, src=ins[a]))
                res += [copy(a, 1 + j, me, (*chip, c), src=ins[a]) for j, chip in enumerate(chips)]
            return res

        return c, me, sibling, chips, copy, mine, first

    def start(self, ins, outs, sems):
        _, _, _, _, _, mine, first = self._ctx(ins, outs, sems)
        for cp in mine() + first():
            cp.start()

    def mid(self, ins, outs, sems):
        c, me, sibling, chips, copy, _, _ = self._ctx(ins, outs, sems)
        for j, chip in enumerate(chips):
            for a in range(self.n):
                copy(a, 1 + j, (*chip, c), me).wait_recv()
                copy(a, 4 + j, (*chip, c), sibling).start()

    def finish(self, ins, outs, sems):
        c, me, sibling, chips, copy, mine, first = self._ctx(ins, outs, sems)
        for a in range(self.n):
            copy(a, 0, sibling, me).wait_recv()
            for j, chip in enumerate(chips):
                copy(a, 4 + j, (*chip, 1 - c), me).wait_recv()
        for cp in first():
            cp.wait_send()
        for j, chip in enumerate(chips):
            for a in range(self.n):
                copy(a, 4 + j, (*chip, c), sibling).wait_send()
        for cp in mine():
            cp.wait()


class _Exchange:
    def __init__(self, parts, axes):
        self.n = len(parts)
        self.axes = list(axes)
        self.arrays = list(parts)

    def _piece(self, a):
        r, c = self.arrays[a].shape
        axis = self.axes[a]
        return (r, c) if axis is None else (r // N_DEV, c) if axis == 0 else (r, c // N_DEV)

    def out_shape(self):
        return [jax.ShapeDtypeStruct((N_DEV,) + self._piece(a), self.arrays[a].dtype) for a in range(self.n)]

    def semaphores(self):
        return [pltpu.SemaphoreType.DMA((7 * self.n,)), pltpu.SemaphoreType.DMA((7 * self.n,)),
                pltpu.SemaphoreType.DMA((self.n,))]

    def _ctx(self, ins, outs, sems):
        send_sems, recv_sems, local_sems = sems
        x, y, c = _position()
        me = 4 * x + 2 * y + c

        def src(a, j):
            return ins[a] if self.axes[a] is None else _block_of(ins[a], j, self.axes[a], self._piece(a))

        def dst(a, j):
            return outs[a].at[j]

        def local():
            return [pltpu.make_async_copy(src(a, me), dst(a, me), local_sems.at[a]) for a in range(self.n)]

        def remote(inbound):
            res = []
            for a in range(self.n):
                for k in range(1, N_DEV):
                    peer = (1 - x if k & 4 else x, 1 - y if k & 2 else y, 1 - c if k & 1 else c)
                    plin = 4 * peer[0] + 2 * peer[1] + peer[2]
                    res.append(pltpu.make_async_remote_copy(
                        src_ref=src(a, plin), dst_ref=dst(a, plin if inbound else me),
                        send_sem=send_sems.at[7 * a + k - 1], recv_sem=recv_sems.at[7 * a + k - 1],
                        device_id=peer, device_id_type=MESH))
            return res

        return local, remote

    def start(self, ins, outs, sems):
        local, remote = self._ctx(ins, outs, sems)
        for cp in local() + remote(False):
            cp.start()

    def finish(self, ins, outs, sems):
        local, remote = self._ctx(ins, outs, sems)
        for cp in remote(True):
            cp.wait_recv()
        for cp in remote(False):
            cp.wait_send()
        for cp in local():
            cp.wait()


def _exchange_start(rider, name):
    n = rider.n
    parts = rider.arrays
    lands = [lax.empty(s.shape, s.dtype) for s in rider.out_shape()]
    hbm = pl.BlockSpec(memory_space=pltpu.HBM)
    sem = pl.BlockSpec(memory_space=pltpu.SEMAPHORE)

    def body(*refs):
        ins, sems = refs[:n], refs[2 * n:2 * n + 3]
        outs, token = refs[2 * n + 3 + n:2 * n + 3 + 2 * n], refs[-1]
        rider.start(ins, outs, sems)
        token[...] = jnp.zeros_like(token)

    res = pl.pallas_call(
        body, name=name,
        out_shape=rider.semaphores() + [pltpu.HBM(p.shape, p.dtype) for p in parts]
                  + [pltpu.HBM(z.shape, z.dtype) for z in lands] + [jax.ShapeDtypeStruct((8, 128), F32)],
        in_specs=[hbm] * (2 * n), out_specs=[sem] * 3 + [hbm] * (2 * n) + [pl.BlockSpec(memory_space=pltpu.VMEM)],
        input_output_aliases={i: 3 + i for i in range(2 * n)},
        compiler_params=pltpu.CompilerParams(has_side_effects=pltpu.SideEffectType.DATAFLOW_SIDE_EFFECTING),
    )(*[pltpu.with_memory_space_constraint(a, pltpu.HBM) for a in parts + lands])
    return res[:3], res[3:3 + n], res[3 + n:3 + 2 * n], res[-1]


def _exchange_wait(rider, started, after, name):
    n = rider.n
    sems, parts, lands, _ = started
    hbm = pl.BlockSpec(memory_space=pltpu.HBM)
    sem = pl.BlockSpec(memory_space=pltpu.SEMAPHORE)

    def body(*refs):
        rider.finish(refs[:n], refs[n:2 * n], refs[2 * n:2 * n + 3])

    res = pl.pallas_call(
        body, name=name, out_shape=[pltpu.HBM(a.shape, a.dtype) for a in list(parts) + list(lands)],
        in_specs=[hbm] * (2 * n) + [sem] * 3 + [ANY] * len(after), out_specs=[hbm] * (2 * n),
        input_output_aliases={i: i for i in range(2 * n)},
        compiler_params=pltpu.CompilerParams(has_side_effects=pltpu.SideEffectType.DATAFLOW_SIDE_EFFECTING),
    )(*parts, *lands, *sems, *after)
    return list(res[n:])


def _pcall(body, *, name, grid, in_specs, out_specs, out_shape, scratch_shapes=(), semantics, vmem_mb, rider=None,
           aliases=None, after=()):
    in_specs, out_specs, out_shape = list(in_specs), list(out_specs), list(out_shape)
    scratch_shapes = list(scratch_shapes)
    aliases = dict(aliases or {})
    if rider is None:
        n_in, after = len(in_specs), list(after)

        def plain(*refs):
            body(*refs[:n_in], *refs[n_in + len(after):])

        call = pl.pallas_call(plain if after else body, name=name, grid=grid, in_specs=in_specs + [ANY] * len(after),
                              out_specs=out_specs, out_shape=out_shape, scratch_shapes=scratch_shapes,
                              input_output_aliases=aliases, compiler_params=_params(semantics, vmem_mb))
        return lambda *args: (list(call(*args, *after)), None)
    n_in, n_out, n_scr = len(in_specs), len(out_specs), len(scratch_shapes)
    r_in, r_shapes = len(rider.arrays), rider.out_shape()
    r_out = len(r_shapes)
    aliases.update({n_in + i: n_out + o for i, o in rider.alias_pairs})
    total = math.prod(grid)
    mid_step = total - 1 if rider.has_mid and rider.late else (3 * total) // 4

    def wrapped(*refs):
        bounds = [0, n_in, r_in, n_out, r_out, n_scr]
        for i in range(1, len(bounds)):
            bounds[i] += bounds[i - 1]
        a, ra, o, ro, s = (refs[bounds[i]:bounds[i + 1]] for i in range(5))
        rs = refs[bounds[5]:]
        step = pl.program_id(0)
        for k in range(1, len(grid)):
            step = step * grid[k] + pl.program_id(k)
        pl.when(step == 0)(lambda: rider.start(ra, ro, rs))
        body(*a, *o, *s)
        if rider.has_mid:
            pl.when(step == mid_step)(lambda: rider.mid(ra, ro, rs))
        pl.when(step == total - 1)(lambda: rider.finish(ra, ro, rs))

    call = pl.pallas_call(
        wrapped, name=name, grid=grid, in_specs=in_specs + [ANY] * r_in, out_specs=out_specs + [ANY] * r_out,
        out_shape=out_shape + r_shapes, scratch_shapes=scratch_shapes + rider.scratch(),
        input_output_aliases=aliases, compiler_params=_params(("arbitrary",) * len(grid), vmem_mb))

    def run(*args):
        res = call(*args, *rider.arrays)
        return list(res[:n_out]), list(res[n_out:])

    return run


def _norm_matmul(x, g, w, *, name, out_dtype, tb, bn, relu=False, save_h=False, rider=None):
    t, d = x.shape
    n = w.shape[1]

    def body(x_ref, g_ref, w_ref, o_ref, *rest):
        h_scr = rest[-1]

        @pl.when(pl.program_id(1) == 0)
        def _():
            h = _rms_fwd(x_ref[...], g_ref[...])[0].astype(BF16)
            h_scr[...] = h
            if save_h:
                rest[0][...] = h

        acc = _dot(h_scr[...], w_ref[...])
        if relu:
            acc = jnp.maximum(acc, 0.0)
        o_ref[...] = acc.astype(out_dtype)

    out_shape = [jax.ShapeDtypeStruct((t, n), out_dtype)]
    out_specs = [pl.BlockSpec((tb, bn), lambda i, j: (i, j))]
    if save_h:
        out_shape.append(jax.ShapeDtypeStruct((t, d), BF16))
        out_specs.append(pl.BlockSpec((tb, d), lambda i, j: (i, 0)))
    res, extra = _pcall(
        body, name=name, grid=(t // tb, n // bn),
        in_specs=[pl.BlockSpec((tb, d), lambda i, j: (i, 0)),
                  pl.BlockSpec((1, d), lambda i, j: (0, 0)),
                  pl.BlockSpec((d, bn), lambda i, j: (0, j))],
        out_specs=out_specs, out_shape=out_shape,
        scratch_shapes=[pltpu.VMEM((tb, d), BF16)],
        semantics=("parallel", "arbitrary"), vmem_mb=48, rider=rider,
    )(x, g, w)
    res = res if save_h else res[0]
    return res if rider is None else (res, extra)


def _proj(x, g, w, *, tb, rider=None):
    t, d = x.shape
    half = w.shape[1] // 2

    def body(x_ref, g_ref, w_ref, qkv_ref, gates_ref, h_ref, h_scr):
        j = pl.program_id(1)

        @pl.when(j == 0)
        def _():
            h = _rms_fwd(x_ref[...], g_ref[...])[0].astype(BF16)
            h_scr[...] = h
            h_ref[...] = h

        acc = _dot(h_scr[...], w_ref[...])

        @pl.when(j == 0)
        def _():
            qkv_ref[...] = acc

        @pl.when(j == 1)
        def _():
            gates_ref[...] = acc.astype(BF16)

    tok = lambda c: pl.BlockSpec((tb, c), lambda i, j: (i, 0))
    res, extra = _pcall(
        body, name="proj", grid=(t // tb, 2),
        in_specs=[tok(d), pl.BlockSpec((1, d), lambda i, j: (0, 0)), pl.BlockSpec((d, half), lambda i, j: (0, j))],
        out_specs=[tok(half), tok(half), tok(d)],
        out_shape=[jax.ShapeDtypeStruct((t, half), F32), jax.ShapeDtypeStruct((t, half), BF16),
                   jax.ShapeDtypeStruct((t, d), BF16)],
        scratch_shapes=[pltpu.VMEM((tb, d), BF16)],
        semantics=("parallel", "arbitrary"), vmem_mb=48, rider=rider,
    )(x, g, w)
    return res if rider is None else (res, extra)


def _matmul_nt_normbwd(dy, w, x, g, dres, *, name, tb, also_bf16=False, to_natural=False, after=()):
    t, d = x.shape
    stacked = dy.ndim == 3
    has_res = dres is not None
    n_i = SEG // TI
    if to_natural:
        tb = N_RES * TI

    def body(dy_ref, w_ref, x_ref, g_ref, *rest):
        rest = list(rest)
        dres_ref = rest.pop(0) if has_res else None
        dx_ref = rest.pop(0)
        dxb_ref = rest.pop(0) if also_bf16 else None
        gg_ref = rest.pop(0)
        i = pl.program_id(0)

        def rows(ref, *lead):
            v = ref[lead] if lead else ref[...]
            return v[0].reshape(tb, v.shape[-1]) if to_natural else v

        if stacked:
            kb = dy_ref.shape[-1]
            dh = _dot_nt(rows(dy_ref, 0), w_ref[:, 0:kb])
            for s in range(1, dy_ref.shape[0]):
                dh = dh + _dot_nt(rows(dy_ref, s), w_ref[:, s * kb:(s + 1) * kb])
        else:
            dh = _dot_nt(rows(dy_ref), w_ref[...])
        g_v = g_ref[...]
        _, xh, r = _rms_fwd(rows(x_ref), g_v)
        dx = _rms_bwd(dh, xh, r, g_v)
        if has_res:
            dx = dx + rows(dres_ref)
        if to_natural:
            scr = rest.pop(0)
            for cb in range(d // BLK):
                cols = slice(cb * BLK, (cb + 1) * BLK)
                slab = scr.at[cb]
                for res in range(N_RES):
                    slab[pl.ds(res, TI, stride=N_RES), :] = dx[res * TI:(res + 1) * TI, cols]
                dx_ref[:, cols] = slab[...]
        else:
            dx_ref[...] = dx
        if also_bf16:
            dxb_ref[...] = dx.astype(BF16)
        part = jnp.sum(dh * xh, axis=0, keepdims=True)

        @pl.when(i == 0)
        def _():
            gg_ref[...] = part

        @pl.when(i != 0)
        def _():
            gg_ref[...] += part

    tok = pl.BlockSpec((tb, d), lambda i: (i, 0))
    row = pl.BlockSpec((1, d), lambda i: (0, 0))
    if to_natural:
        act = pl.BlockSpec((1, N_RES, TI, d), lambda i: (i // n_i, 0, i % n_i, 0))
        dy_spec = pl.BlockSpec((dy.shape[0], 1, N_RES, TI, dy.shape[2]), lambda i: (0, i // n_i, 0, i % n_i, 0))
        dy, x = dy.reshape(dy.shape[0], t // HALF, N_RES, SEG, dy.shape[2]), _x4(x)
        dres = _x4(dres) if has_res else None
    elif stacked:
        act, dy_spec = tok, pl.BlockSpec((dy.shape[0], tb, dy.shape[2]), lambda i: (0, i, 0))
    else:
        act, dy_spec = tok, pl.BlockSpec((tb, dy.shape[1]), lambda i: (i, 0))
    in_specs = [dy_spec, pl.BlockSpec(w.shape, lambda i: (0, 0)), act, row]
    args = [dy, w, x, g]
    if has_res:
        in_specs.append(act)
        args.append(dres)
    out_specs = [tok] + ([tok] if also_bf16 else []) + [row]
    out_shape = ([jax.ShapeDtypeStruct((t, d), F32)] + ([jax.ShapeDtypeStruct((t, d), BF16)] if also_bf16 else [])
                 + [jax.ShapeDtypeStruct((1, d), F32)])
    res, _ = _pcall(
        body, name=name, grid=(t // tb,), in_specs=in_specs, out_specs=out_specs, out_shape=out_shape,
        scratch_shapes=[pltpu.VMEM((d // BLK, tb, BLK), F32)] if to_natural else [],
        semantics=("arbitrary",), vmem_mb=56, after=after,
    )(*args)
    return res


def _matmul_tn(a, b, *, name, bm, bn, square_a=False, after=()):
    t, m = a.shape
    stacked = b.ndim == 3
    n = b.shape[0] * bn if stacked else b.shape[1]

    def body(a_ref, b_ref, o_ref):
        av = a_ref[...]
        if square_a:
            av = av.astype(F32)
            av = (av * av).astype(BF16)
        o_ref[...] = _dot_tn(av, b_ref[...]).astype(BF16)

    res, _ = _pcall(
        body, name=name, grid=(m // bm, n // bn),
        in_specs=[pl.BlockSpec((t, bm), lambda i, j: (0, i)),
                  pl.BlockSpec((None, t, bn), lambda i, j: (j, 0, 0)) if stacked
                  else pl.BlockSpec((t, bn), lambda i, j: (0, j))],
        out_specs=[pl.BlockSpec((bm, bn), lambda i, j: (i, j))], out_shape=[jax.ShapeDtypeStruct((m, n), BF16)],
        semantics=("parallel", "parallel"), vmem_mb=56, after=after,
    )(a, b)
    return res[0]


N_RES = 16
SEG = 128
HALF = N_RES * SEG
TI = 32
HALO = 16


def _x4(a):
    return a.reshape(a.shape[0] // HALF, N_RES, SEG, a.shape[1])


def _reorder(arrays, name, rider=None):
    t, c = arrays[0].shape
    n = len(arrays)
    n_i = SEG // TI

    def body(*refs):
        scr = refs[-1]
        for i_ref, o_ref in zip(refs[:n], refs[n:2 * n]):
            for cb in range(c // BLK):
                cols = slice(cb * BLK, (cb + 1) * BLK)
                slab = scr.at[cb]
                slab[...] = i_ref[:, cols]
                for r in range(N_RES):
                    o_ref[0, r, :, cols] = slab[pl.ds(r, TI, stride=N_RES), :]

    res, extra = _pcall(
        body, name=name, grid=(t // (TI * N_RES),),
        in_specs=[pl.BlockSpec((TI * N_RES, c), lambda s: (s, 0))] * n,
        out_specs=[pl.BlockSpec((1, N_RES, TI, c), lambda s: (s // n_i, 0, s % n_i, 0))] * n,
        out_shape=[jax.ShapeDtypeStruct((t // HALF, N_RES, SEG, c), F32)] * n,
        scratch_shapes=[pltpu.VMEM((c // BLK, TI * N_RES, BLK), F32)],
        semantics=("parallel",), vmem_mb=32, rider=rider,
    )(*arrays)
    res = [r.reshape(t, c) for r in res]
    return res if rider is None else (res, extra)


_PATTERNS = ((1, 16, 8, SEG), (4, 4, 32, 4 * SEG), (16, 1, SEG, 0))
_FIRST = {1: 1, 4: 4, 16: 16}


def _group_rows(d, g):
    a = g >> 4
    if d == 16:
        base = a * HALF + (g & 15) * SEG
        prev = base - HALF
    elif d == 4:
        c = (g >> 2) & 3
        base = a * HALF + (g & 3) * SEG + c * 32
        prev = jnp.where(c > 0, base - 32, base - HALF + 96)
    else:
        c = g & 15
        base = a * HALF + c * 8
        prev = jnp.where(c > 0, base - 8, base - HALF + 120)
    return base, prev


def _load_rows(ref, base, n, rows, stride):
    parts = [ref[pl.ds(pl.multiple_of(base + j * stride, 8), rows), :] for j in range(n)]
    return parts[0] if n == 1 else jnp.concatenate(parts, axis=0)


def _store_rows(ref, base, val, n, rows, stride, add=False):
    for j in range(n):
        sl = pl.ds(pl.multiple_of(base + j * stride, 8), rows)
        piece = val[j * rows:(j + 1) * rows, :]
        if add:
            ref[sl, :] += piece
        else:
            ref[sl, :] = piece


def _band_bias(n, rows):
    shift = rows.bit_length() - 1
    lq = lax.broadcasted_iota(jnp.int32, (BLK, BLK), 0)
    lk = lax.broadcasted_iota(jnp.int32, (BLK, BLK), 1)
    iq = (lq & (rows - 1)) * n + (lq >> shift)
    ik = (lk & (rows - 1)) * n + (lk >> shift)
    zero = jnp.zeros((BLK, BLK), F32)
    return jnp.where(ik >= iq, zero, NEG_INF), jnp.where(ik <= iq, zero, NEG_INF)


def _set_bias(bias_scr, n, rows):
    prev_b, cur_b = _band_bias(n, rows)
    for half in range(2):
        bias_scr[half * BLK:(half + 1) * BLK, 0:BLK] = prev_b
        bias_scr[half * BLK:(half + 1) * BLK, BLK:2 * BLK] = cur_b


SCALE = 1.0 / math.sqrt(HEAD_DIM)


def _head_consts(value=1.0):
    lane_lo = lax.broadcasted_iota(jnp.int32, (BLK, BLK), 1) < HEAD_DIM
    return lane_lo, [jnp.where(lane_lo, value, 0.0).astype(BF16), jnp.where(lane_lo, 0.0, value).astype(BF16)]


def _stack_heads(v, head_mask):
    return jnp.concatenate([v * head_mask[0], v * head_mask[1]], axis=0)


def _unstack_heads(v2, lane_lo):
    return jnp.where(lane_lo, v2[:BLK], v2[BLK:])


def _rows_per_head(v, lane_lo):
    rolled = pltpu.roll(v, HEAD_DIM, axis=1)
    return jnp.concatenate([jnp.where(lane_lo, v, rolled), jnp.where(lane_lo, rolled, v)], axis=0)


WIDTH = 4


def _loop(lo, hi, fn, width=None):
    if width is None:
        def body(g, carry):
            fn(g)
            return carry

        if hi > lo:
            lax.fori_loop(lo, hi, body, 0)
        return
    while hi > lo:
        trips = (hi - lo) // width
        if trips:
            def body(i, carry, lo=lo, width=width):
                fn([lo + width * i + j for j in range(width)])
                return carry

            lax.fori_loop(0, trips, body, 0)
            lo += trips * width
        width = max(1, width // 2)


def _mix_weights(l1, l2, l3):
    mx = jnp.maximum(jnp.maximum(l1, l2), l3)
    e1, e2, e3 = jnp.exp(l1 - mx), jnp.exp(l2 - mx), jnp.exp(l3 - mx)
    inv = 1.0 / (e1 + e2 + e3)
    return e1 * inv, e2 * inv, e3 * inv


def _attention_fwd(qkv, rider=None):
    t = qkv.shape[0]
    groups = 16 * (t // HALF)

    def body(q_ref, k_ref, v_ref, attn_ref, l1_ref, l2_ref, l3_ref, o_scr, bias_scr):
        lane_lo, q_mask = _head_consts(SCALE)
        l_refs = (l1_ref, l2_ref, l3_ref)
        for p, (d, n, rows, stride) in enumerate(_PATTERNS):
            _set_bias(bias_scr, n, rows)
            o_p, l_p = o_scr.at[p], l_refs[p]

            def block(gs, has_prev):
                at = [_group_rows(d, g) for g in gs]

                def load(ref, b):
                    return _load_rows(ref, b, n, rows, stride).astype(BF16)

                q2 = [_stack_heads(load(q_ref, b), q_mask) for b, _ in at]
                k2 = [load(k_ref, b) for b, _ in at]
                v2 = [load(v_ref, b) for b, _ in at]
                if has_prev:
                    k2 = [jnp.concatenate([load(k_ref, pv), k], axis=0) for (_, pv), k in zip(at, k2)]
                    v2 = [jnp.concatenate([load(v_ref, pv), v], axis=0) for (_, pv), v in zip(at, v2)]
                s = [_dot_nt(q, k) for q, k in zip(q2, k2)]
                s = [x + (bias_scr[...] if has_prev else bias_scr[:, BLK:2 * BLK]) for x in s]
                mx = [jnp.max(x, axis=1, keepdims=True) for x in s]
                e = [jnp.exp(x - m) for x, m in zip(s, mx)]
                den = [jnp.sum(x, axis=1, keepdims=True) for x in e]
                o2 = [_dot(x.astype(BF16), v) * (1.0 / dn) for x, v, dn in zip(e, v2, den)]
                lse2 = [jnp.broadcast_to(m + jnp.log(dn), (2 * BLK, BLK)) for m, dn in zip(mx, den)]
                for (b, _), o, l in zip(at, o2, lse2):
                    _store_rows(o_p, b, _unstack_heads(o, lane_lo), n, rows, stride)
                    _store_rows(l_p, b, _unstack_heads(l, lane_lo), n, rows, stride)

            _loop(0, _FIRST[d], lambda gs: block(gs, False), width=2 * WIDTH)
            _loop(_FIRST[d], groups, lambda gs: block(gs, True), width=2 * WIDTH)

        def mix(i):
            sl = pl.ds(pl.multiple_of(i * 256, 256), 256)
            w = _mix_weights(l1_ref[sl, :], l2_ref[sl, :], l3_ref[sl, :])
            attn_ref[sl, :] = w[0] * o_scr[0, sl, :] + w[1] * o_scr[1, sl, :] + w[2] * o_scr[2, sl, :]

        _loop(0, t // 256, mix)

    def col(c0):
        return pl.BlockSpec((t, BLK), lambda hp: (0, c0 + hp))

    res, extra = _pcall(
        body, name="attention_fwd", grid=(4,), in_specs=[col(0), col(4), col(8)], out_specs=[col(0)] * 4,
        out_shape=[jax.ShapeDtypeStruct((t, 512), F32)] * 4,
        scratch_shapes=[pltpu.VMEM((3, t, BLK), F32), pltpu.VMEM((2 * BLK, 2 * BLK), F32)],
        semantics=("parallel",), vmem_mb=48, rider=rider,
    )(qkv, qkv, qkv)
    return res if rider is None else (res, extra)


def _attention_bwd(qkv, dattn, dsum, lses, dproj):
    t = qkv.shape[0]
    groups = 16 * (t // HALF)

    def body(q_ref, k_ref, v_ref, da_ref, ds_ref, l1_ref, l2_ref, l3_ref, kept_ref, out_ref, acc, bias_scr):
        del kept_ref
        lane_lo, head_mask = _head_consts()
        q_mask = _head_consts(SCALE)[1]
        l_refs = (l1_ref, l2_ref, l3_ref)

        def clear(i):
            sl = pl.ds(pl.multiple_of(i * 512, 512), 512)
            for s in range(3):
                acc[s, sl, :] = jnp.zeros((512, BLK), F32)

        _loop(0, t // 512, clear)
        dq_acc, dk_acc, dv_acc = acc.at[0], acc.at[1], acc.at[2]
        for p, (d, n, rows, stride) in enumerate(_PATTERNS):
            _set_bias(bias_scr, n, rows)

            def block(gs, has_prev):
                at = [_group_rows(d, g) for g in gs]

                def load(ref, b):
                    return _load_rows(ref, b, n, rows, stride)

                def put(ref, b, val):
                    _store_rows(ref, b, val, n, rows, stride, add=True)

                def wide(x):
                    return jnp.concatenate([x, x], axis=1) if has_prev else x

                lse = [[load(ref, b) for ref in l_refs] for b, _ in at]
                w = [_mix_weights(*ls)[p] for ls in lse]
                do2 = [_stack_heads((wg * load(da_ref, b)).astype(BF16), head_mask) for wg, (b, _) in zip(w, at)]
                dl2 = [wide(_rows_per_head(wg * load(ds_ref, b), lane_lo)) for wg, (b, _) in zip(w, at)]
                lse2 = [wide(_rows_per_head(ls[p], lane_lo)) for ls in lse]
                q2 = [_stack_heads(load(q_ref, b).astype(BF16), q_mask) for b, _ in at]
                k2 = [load(k_ref, b).astype(BF16) for b, _ in at]
                v2 = [load(v_ref, b).astype(BF16) for b, _ in at]
                if has_prev:
                    k2 = [jnp.concatenate([load(k_ref, pv).astype(BF16), k], axis=0) for (_, pv), k in zip(at, k2)]
                    v2 = [jnp.concatenate([load(v_ref, pv).astype(BF16), v], axis=0) for (_, pv), v in zip(at, v2)]
                s = [_dot_nt(q, k) for q, k in zip(q2, k2)]
                dp = [_dot_nt(do, v) for do, v in zip(do2, v2)]
                pr = [jnp.exp(x + (bias_scr[...] if has_prev else bias_scr[:, BLK:2 * BLK]) - l)
                      for x, l in zip(s, lse2)]
                ds = [(pg * (x - dl)).astype(BF16) for pg, x, dl in zip(pr, dp, dl2)]
                dq2 = [_dot(x, k) * SCALE for x, k in zip(ds, k2)]
                dk2 = [_dot_tn(x, q) for x, q in zip(ds, q2)]
                dv2 = [_dot_tn(pg.astype(BF16), do) for pg, do in zip(pr, do2)]
                for (b, pv), dq, dk, dv in zip(at, dq2, dk2, dv2):
                    put(dq_acc, b, _unstack_heads(dq, lane_lo))
                    if has_prev:
                        put(dk_acc, pv, dk[:BLK])
                        put(dv_acc, pv, dv[:BLK])
                        put(dk_acc, b, dk[BLK:])
                        put(dv_acc, b, dv[BLK:])
                    else:
                        put(dk_acc, b, dk)
                        put(dv_acc, b, dv)

            _loop(0, _FIRST[d], lambda gs: block(gs, False), width=WIDTH)
            _loop(_FIRST[d], groups, lambda gs: block(gs, True), width=WIDTH)

        def emit(i):
            sl = pl.ds(pl.multiple_of(i * 512, 512), 512)
            for s in range(3):
                out_ref[s, sl, :] = acc[s, sl, :].astype(BF16)

        _loop(0, t // 512, emit)

    def col(c0):
        return pl.BlockSpec((t, BLK), lambda hp: (0, c0 + hp))

    res, _ = _pcall(
        body, name="attention_bwd", grid=(4,),
        in_specs=[col(0), col(4), col(8)] + [col(0)] * 5 + [ANY],
        out_specs=[pl.BlockSpec((3, t, BLK), lambda hp: (0, 0, hp))],
        out_shape=[jax.ShapeDtypeStruct(dproj.shape, BF16)],
        scratch_shapes=[pltpu.VMEM((3, t, BLK), F32), pltpu.VMEM((2 * BLK, 2 * BLK), F32)],
        semantics=("parallel",), vmem_mb=56, aliases={8: 0},
    )(qkv, qkv, qkv, dattn, dsum, *lses, dproj)
    return res[0]


def _order_specs(t):
    n_i = SEG // TI
    nblk = (t // HALF) * n_i
    per = TI // HALO

    def main(c, col=0):
        return pl.BlockSpec((1, N_RES, TI, c), lambda s: (s // n_i, 0, s % n_i, col))

    def before(c, col=0):
        return pl.BlockSpec((1, 2, HALO, c), lambda s: (jnp.maximum(s - 1, 0) // n_i, N_RES // 2 - 1,
                                                        (jnp.maximum(s - 1, 0) % n_i) * per + per - 1, col))

    def after(c, col=0):
        return pl.BlockSpec((1, 2, HALO, c), lambda s: (jnp.minimum(s + 1, nblk - 1) // n_i, 0,
                                                        (jnp.minimum(s + 1, nblk - 1) % n_i) * per, col))

    return nblk, main, before, after


def _shift_in(v, row_in, up):
    rows = v.shape[0]
    idx = lax.broadcasted_iota(jnp.int32, v.shape, 0)
    fill = jnp.broadcast_to(row_in, v.shape)
    if up:
        return jnp.where(idx == rows - 1, fill, pltpu.roll(v, rows - 1, axis=0))
    return jnp.where(idx == 0, fill, pltpu.roll(v, 1, axis=0))


def _taps_behind(u, before):
    s15 = _shift_in(u[N_RES - 1], before[1, HALO - 1:HALO, :], up=False)
    s14 = _shift_in(u[N_RES - 2], before[0, HALO - 1:HALO, :], up=False)
    m1 = jnp.concatenate([s15[None], u[:N_RES - 1]], axis=0)
    m2 = jnp.concatenate([s14[None], s15[None], u[:N_RES - 2]], axis=0)
    return m1, m2


def _taps_ahead(u, after):
    t0 = _shift_in(u[0], after[0, 0:1, :], up=True)
    t1 = _shift_in(u[1], after[1, 0:1, :], up=True)
    p1 = jnp.concatenate([u[1:], t0[None]], axis=0)
    p2 = jnp.concatenate([u[2:], t0[None], t1[None]], axis=0)
    return p1, p2


def _conv_fwd(gates, before, first, cw):
    gates, before = gates.astype(F32), before.astype(F32)
    bg, cg, xc = gates[..., 0:512], gates[..., 512:1024], gates[..., 1024:1536]
    u = cg * xc
    ub = before[..., 512:1024] * before[..., 1024:1536]
    ub = jnp.where(first, jnp.zeros_like(ub), ub)
    m1, m2 = _taps_behind(u, ub)
    conv = m2 * cw[0:1, :] + m1 * cw[1:2, :] + u * cw[2:3, :]
    return bg, u, m1, m2, conv


def _sum_tokens(v):
    return jnp.sum(jnp.sum(v, axis=0), axis=0, keepdims=True)


def _mixer_fwd(x, attn, gates, cw, g_a, g_c, w_out):
    t, d = x.shape
    nblk, main, before, _ = _order_specs(t)
    rows = N_RES * TI

    def body(x_ref, at_ref, gt_ref, gb_ref, cw_ref, ga_ref, gc_ref, wa_ref, wb_ref, x1_ref, mg_ref):
        an = _rms_fwd(at_ref[0], ga_ref[...])[0].astype(BF16)
        bg, _, _, _, conv = _conv_fwd(gt_ref[0], gb_ref[0], pl.program_id(0) == 0, cw_ref[...])
        cn = _rms_fwd(bg * conv, gc_ref[...])[0].astype(BF16)
        mg_ref[0, :, :, 0:512] = an
        mg_ref[0, :, :, 512:1024] = cn
        y = _dot(an.reshape(rows, 512), wa_ref[...]) + _dot(cn.reshape(rows, 512), wb_ref[...])
        x1_ref[0] = x_ref[0] + y.reshape(N_RES, TI, d)

    const = lambda r, c, i0=0: pl.BlockSpec((r, c), lambda s: (i0, 0))
    x1, merged = pl.pallas_call(
        body, name="mixer_fwd", grid=(nblk,),
        in_specs=[main(d), main(512), main(1536), before(1536), const(3, 512), const(1, 512), const(1, 512),
                  const(512, d), const(512, d, 1)],
        out_specs=[main(d), main(d)],
        out_shape=[jax.ShapeDtypeStruct(_x4(x).shape, F32), jax.ShapeDtypeStruct(_x4(x).shape, BF16)],
        compiler_params=_params(("parallel",), 48),
    )(_x4(x), _x4(attn), _x4(gates), _x4(gates), cw, g_a, g_c, w_out, w_out)
    return x1.reshape(t, d), merged.reshape(t, d)


def _mixer_bwd(dx1, merged, attn, gates, cw, g_a, g_c, w_out, head_sum, after=()):
    t, d = dx1.shape
    nblk, main, before, _ = _order_specs(t)
    rows = N_RES * TI

    def body(dx_ref, mg_ref, at_ref, gt_ref, gb_ref, cw_ref, ga_ref, gc_ref, wa_ref, wb_ref, hs_ref,
             da_ref, dsum_ref, dy_ref, gga_ref, ggc_ref, gw_ref, acc_w):
        s = pl.program_id(0)
        dxb = dx_ref[0].reshape(rows, d).astype(BF16)

        @pl.when(s == 0)
        def _():
            acc_w[...] = jnp.zeros_like(acc_w)

        acc_w[...] += _dot_tn(mg_ref[0].reshape(rows, d), dxb)

        @pl.when(s == nblk - 1)
        def _():
            gw_ref[...] = acc_w[...].astype(BF16)

        dma = _dot_nt(dxb, wa_ref[...]).reshape(N_RES, TI, 512)
        dmc = _dot_nt(dxb, wb_ref[...]).reshape(N_RES, TI, 512)
        attn_v, g_av = at_ref[0], ga_ref[...]
        _, ah, ra = _rms_fwd(attn_v, g_av)
        dattn = _rms_bwd(dma, ah, ra, g_av)
        da_ref[0] = dattn
        z = (dattn * attn_v).reshape(rows, 512)
        hs = hs_ref[...]
        z1 = z.astype(BF16)
        z2 = (z - z1.astype(F32)).astype(BF16)
        dsum_ref[0] = (_dot(z1, hs) + _dot(z2, hs)).reshape(N_RES, TI, 512)
        bg, _, _, _, conv = _conv_fwd(gt_ref[0], gb_ref[0], s == 0, cw_ref[...])
        g_cv = gc_ref[...]
        _, yh, rc = _rms_fwd(bg * conv, g_cv)
        dy_ref[0] = _rms_bwd(dmc, yh, rc, g_cv)
        pa, pc = _sum_tokens(dma * ah), _sum_tokens(dmc * yh)

        @pl.when(s == 0)
        def _():
            gga_ref[...] = pa
            ggc_ref[...] = pc

        @pl.when(s != 0)
        def _():
            gga_ref[...] += pa
            ggc_ref[...] += pc

    const = lambda r, c, i0=0: pl.BlockSpec((r, c), lambda s: (i0, 0))
    shape4 = _x4(attn).shape
    res, _ = _pcall(
        body, name="mixer_bwd", grid=(nblk,),
        in_specs=[main(d), main(d), main(512), main(1536), before(1536), const(3, 512), const(1, 512), const(1, 512),
                  const(512, d), const(512, d, 1), const(512, 512)],
        out_specs=[main(512)] * 3 + [const(1, 512), const(1, 512), const(d, d)],
        out_shape=[jax.ShapeDtypeStruct(shape4, F32)] * 3 + [jax.ShapeDtypeStruct((1, 512), F32)] * 2
        + [jax.ShapeDtypeStruct((d, d), BF16)],
        scratch_shapes=[pltpu.VMEM((d, d), F32)],
        semantics=("arbitrary",), vmem_mb=48, after=after,
    )(_x4(dx1), _x4(merged), _x4(attn), _x4(gates), _x4(gates), cw, g_a, g_c, w_out, w_out, head_sum)
    return [r.reshape(t, 512) for r in res[:3]] + res[3:]


def _conv_bwd(dy, gates, cw, after=()):
    t = dy.shape[0]
    nblk, main, before, ahead = _order_specs(t)
    n_i = SEG // TI

    def body(dy_ref, dya_ref, gt_ref, gb_ref, ga_ref, cw_ref, dp_ref, gcw_ref):
        s = pl.program_id(0)
        cw_v, gates_v = cw_ref[...], gt_ref[0]
        bg, u, m1, m2, conv = _conv_fwd(gates_v, gb_ref[0], s == 0, cw_v)
        dy_v = dy_ref[0]
        dconv = dy_v * bg
        dca = dya_ref[0] * ga_ref[0][..., 0:512].astype(F32)
        dca = jnp.where(s == nblk - 1, jnp.zeros_like(dca), dca)
        p1, p2 = _taps_ahead(dconv, dca)
        du = dconv * cw_v[2:3, :] + p1 * cw_v[1:2, :] + p2 * cw_v[0:1, :]
        dp_ref[0, 0] = (dy_v * conv).astype(BF16)
        dp_ref[1, 0] = (du * gates_v[..., 1024:1536].astype(F32)).astype(BF16)
        dp_ref[2, 0] = (du * gates_v[..., 512:1024].astype(F32)).astype(BF16)
        parts = [_sum_tokens(dconv * m2), _sum_tokens(dconv * m1), _sum_tokens(dconv * u)]

        @pl.when(s == 0)
        def _():
            gcw_ref[...] = jnp.zeros_like(gcw_ref)

        for tap in range(3):
            gcw_ref[tap:tap + 1, :] += parts[tap]

    (dproj, gcw), _ = _pcall(
        body, name="conv_bwd", grid=(nblk,),
        in_specs=[main(512), ahead(512), main(1536), before(1536), ahead(1536),
                  pl.BlockSpec((3, 512), lambda s: (0, 0))],
        out_specs=[pl.BlockSpec((3, 1, N_RES, TI, 512), lambda s: (1, s // n_i, 0, s % n_i, 0)),
                   pl.BlockSpec((8, 512), lambda s: (0, 0))],
        out_shape=[jax.ShapeDtypeStruct((6, t // HALF, N_RES, SEG, 512), BF16), jax.ShapeDtypeStruct((8, 512), F32)],
        semantics=("arbitrary",), vmem_mb=40, after=after,
    )(_x4(dy), _x4(dy), _x4(gates), _x4(gates), _x4(gates), cw)
    return dproj.reshape(6, t, 512), gcw


def _xattn_fwd(x1, g, w_q, kv, w_o, *, tb):
    t, d = x1.shape
    hd = d // N_MEM_HEADS
    m = kv.shape[0]

    def body(x_ref, g_ref, wq_ref, k_ref, v_ref, wo_ref, x2_ref, h_ref, q_ref, o_ref):
        xv = x_ref[...]
        h = _rms_fwd(xv, g_ref[...])[0].astype(BF16)
        h_ref[...] = h
        q = _dot(h, wq_ref[...]).astype(BF16)
        q_ref[...] = q
        for hh in range(N_MEM_HEADS):
            sl = slice(hh * hd, (hh + 1) * hd)
            s = _dot_nt(q[:, sl], k_ref[:, sl]) * (1.0 / 16.0)
            e = jnp.exp(s - jnp.max(s, axis=1, keepdims=True))
            p = e / jnp.sum(e, axis=1, keepdims=True)
            o_ref[:, sl] = _dot(p.astype(BF16), v_ref[:, sl]).astype(BF16)
        x2_ref[...] = xv + _dot(o_ref[...], wo_ref[...])

    tok = pl.BlockSpec((tb, d), lambda i: (i, 0))
    full = pl.BlockSpec((d, d), lambda i: (0, 0))
    return pl.pallas_call(
        body, name="xattn_fwd", grid=(t // tb,),
        in_specs=[tok, pl.BlockSpec((1, d), lambda i: (0, 0)), full,
                  pl.BlockSpec((m, d), lambda i: (0, 0)), pl.BlockSpec((m, d), lambda i: (0, 1)), full],
        out_specs=[tok] * 4,
        out_shape=[jax.ShapeDtypeStruct((t, d), F32)] + [jax.ShapeDtypeStruct((t, d), BF16)] * 3,
        compiler_params=_params(("parallel",), 48),
    )(x1, g, w_q, kv, kv, w_o)


def _xattn_bwd(dx2, x1, g, q, h, w_q, kv, w_o, *, tb, after=()):
    t, d = x1.shape
    hd = d // N_MEM_HEADS
    m = kv.shape[0]
    n_i = t // tb

    def body(dx2_ref, x_ref, g_ref, q_ref, h_ref, wq_ref, k_ref, v_ref, wo_ref,
             dx1_ref, dk_ref, dv_ref, gg_ref, gw_ref, dq_ref, acc_w):
        i = pl.program_id(0)

        @pl.when(i == 0)
        def _():
            dk_ref[...] = jnp.zeros_like(dk_ref)
            dv_ref[...] = jnp.zeros_like(dv_ref)
            acc_w[...] = jnp.zeros_like(acc_w)

        dx2 = dx2_ref[...]
        do = _dot_nt(dx2.astype(BF16), wo_ref[...]).astype(BF16)
        for hh in range(N_MEM_HEADS):
            sl = slice(hh * hd, (hh + 1) * hd)
            qh, kh, vh, doh = q_ref[:, sl], k_ref[:, sl], v_ref[:, sl], do[:, sl]
            s = _dot_nt(qh, kh) * (1.0 / 16.0)
            e = jnp.exp(s - jnp.max(s, axis=1, keepdims=True))
            p = e / jnp.sum(e, axis=1, keepdims=True)
            dp = _dot_nt(doh, vh)
            ds = (p * (dp - jnp.sum(dp * p, axis=1, keepdims=True)) * (1.0 / 16.0)).astype(BF16)
            dq_ref[:, sl] = _dot(ds, kh).astype(BF16)
            dk_ref[:, sl] += _dot_tn(ds, qh)
            dv_ref[:, sl] += _dot_tn(p.astype(BF16), doh)
        dq = dq_ref[...]
        acc_w[...] += _dot_tn(h_ref[...], dq)
        dh = _dot_nt(dq, wq_ref[...])
        g_v = g_ref[...]
        _, xh, r = _rms_fwd(x_ref[...], g_v)
        dx1 = dx2 + _rms_bwd(dh, xh, r, g_v)
        dx1_ref[...] = dx1
        part = jnp.sum(dh * xh, axis=0, keepdims=True)

        @pl.when(i == 0)
        def _():
            gg_ref[...] = part

        @pl.when(i != 0)
        def _():
            gg_ref[...] += part

        @pl.when(i == n_i - 1)
        def _():
            gw_ref[...] = acc_w[...].astype(BF16)

    tok = pl.BlockSpec((tb, d), lambda i: (i, 0))
    full = pl.BlockSpec((d, d), lambda i: (0, 0))
    acc = pl.BlockSpec((m, d), lambda i: (0, 0))
    res, _ = _pcall(
        body, name="xattn_bwd", grid=(n_i,),
        in_specs=[tok, tok, pl.BlockSpec((1, d), lambda i: (0, 0)), tok, tok, full,
                  pl.BlockSpec((m, d), lambda i: (0, 0)), pl.BlockSpec((m, d), lambda i: (0, 1)), full],
        out_specs=[tok, acc, acc, pl.BlockSpec((1, d), lambda i: (0, 0)), full],
        out_shape=[jax.ShapeDtypeStruct((t, d), F32),
                   jax.ShapeDtypeStruct((m, d), F32), jax.ShapeDtypeStruct((m, d), F32),
                   jax.ShapeDtypeStruct((1, d), F32), jax.ShapeDtypeStruct((d, d), BF16)],
        scratch_shapes=[pltpu.VMEM((tb, d), BF16), pltpu.VMEM((d, d), F32)],
        semantics=("arbitrary",), vmem_mb=56, after=after,
    )(dx2, x1, g, q, h, w_q, kv, kv, w_o)
    return res


def _mem_bwd(dk, dv, w_kv, mem, mem_n, g):
    m, d = mem.shape

    def body(dk_ref, dv_ref, w_ref, x_ref, h_ref, g_ref, gw_ref, gg_ref):
        h = h_ref[...]
        dh = jnp.zeros((m, d), F32)
        for i, dy_ref in enumerate((dk_ref, dv_ref)):
            cols = slice(i * d, (i + 1) * d)
            dy = dy_ref[...].astype(BF16)
            gw_ref[:, cols] = _dot_tn(h, dy).astype(BF16)
            dh = dh + _dot_nt(dy, w_ref[:, cols])
        xh = _rms_fwd(x_ref[...], g_ref[...])[1]
        gg_ref[...] = jnp.sum(dh * xh, axis=0, keepdims=True)

    return pl.pallas_call(
        body, name="mem_bwd",
        out_shape=[jax.ShapeDtypeStruct(w_kv.shape, BF16), jax.ShapeDtypeStruct((1, d), F32)],
        compiler_params=pltpu.CompilerParams(vmem_limit_bytes=32 << 20),
    )(dk, dv, w_kv, mem, mem_n, g)


def _mlp_down_loss(a, w_down, x2, tgt, g, *, tb):
    t, d = x2.shape
    f = a.shape[1]

    def body(a_ref, w_ref, x_ref, t_ref, g_ref, dx_ref, dxb_ref, loss_ref, gg_ref):
        i = pl.program_id(0)
        av = a_ref[...].astype(F32)
        x3 = x_ref[...] + _dot((av * av).astype(BF16), w_ref[...])
        g_v = g_ref[...]
        out, xh, r = _rms_fwd(x3, g_v)
        err = out - t_ref[...]
        dout = err * (1.0 / d)
        dx = _rms_bwd(dout, xh, r, g_v)
        dx_ref[...] = dx
        dxb_ref[...] = dx.astype(BF16)
        part = jnp.sum(dout * xh, axis=0, keepdims=True)
        lpart = 0.5 * jnp.sum(jnp.mean(err * err, axis=-1, keepdims=True), axis=0, keepdims=True)
        lpart = jnp.broadcast_to(lpart, loss_ref.shape)

        @pl.when(i == 0)
        def _():
            gg_ref[...] = part
            loss_ref[...] = lpart

        @pl.when(i != 0)
        def _():
            gg_ref[...] += part
            loss_ref[...] += lpart

    tok = pl.BlockSpec((tb, d), lambda i: (i, 0))
    return pl.pallas_call(
        body, name="mlp_down_loss", grid=(t // tb,),
        in_specs=[pl.BlockSpec((tb, f), lambda i: (i, 0)), pl.BlockSpec((f, d), lambda i: (0, 0)), tok, tok,
                  pl.BlockSpec((1, d), lambda i: (0, 0))],
        out_specs=[tok, tok, pl.BlockSpec((8, 128), lambda i: (0, 0)), pl.BlockSpec((1, d), lambda i: (0, 0))],
        out_shape=[jax.ShapeDtypeStruct((t, d), F32), jax.ShapeDtypeStruct((t, d), BF16),
                   jax.ShapeDtypeStruct((8, 128), F32), jax.ShapeDtypeStruct((1, d), F32)],
        compiler_params=_params(("arbitrary",), 56),
    )(a, w_down, x2, tgt, g)


def _mlp_dpre(dx3, w_down, a, *, tb, bn):
    t, d = dx3.shape
    f = a.shape[1]

    def body(dx_ref, w_ref, a_ref, o_ref):
        o_ref[...] = (2.0 * a_ref[...].astype(F32) * _dot_nt(dx_ref[...], w_ref[...])).astype(BF16)

    return pl.pallas_call(
        body, name="mlp_dpre", grid=(t // tb, f // bn),
        in_specs=[pl.BlockSpec((tb, d), lambda i, j: (i, 0)), pl.BlockSpec((bn, d), lambda i, j: (j, 0)),
                  pl.BlockSpec((tb, bn), lambda i, j: (i, j))],
        out_specs=pl.BlockSpec((tb, bn), lambda i, j: (i, j)),
        out_shape=jax.ShapeDtypeStruct((t, f), BF16),
        compiler_params=_params(("parallel", "arbitrary"), 48),
    )(dx3, w_down, a)


def _adamw(gsum, w, m, v):
    m_new = ADAM_B1 * m + (1.0 - ADAM_B1) * gsum
    v_new = ADAM_B2 * v + (1.0 - ADAM_B2) * (gsum * gsum)
    m_hat = m_new / (1.0 - ADAM_B1 ** ADAM_STEP)
    v_hat = v_new / (1.0 - ADAM_B2 ** ADAM_STEP)
    delta = -ADAM_LR * (m_hat / (jnp.sqrt(v_hat) + ADAM_EPS) + ADAM_WD * w)
    return delta, m_new, v_new


def _sum_adamw(parts, w, m, v, *, name, tr):
    r, c = w.shape

    def body(p_ref, w_ref, m_ref, v_ref, g_ref, d_ref, mo_ref, vo_ref):
        g = p_ref[0].astype(F32)
        for k in range(1, N_DEV):
            g = g + p_ref[k].astype(F32)
        g_ref[...] = g
        d_ref[...], mo_ref[...], vo_ref[...] = _adamw(g, w_ref[...], m_ref[...], v_ref[...])

    blk = pl.BlockSpec((tr, c), lambda i: (i, 0))
    return pl.pallas_call(
        body, name=name, grid=(r // tr,),
        in_specs=[pl.BlockSpec((N_DEV, tr, c), lambda i: (0, i, 0)), blk, blk, blk],
        out_specs=[blk] * 4, out_shape=[jax.ShapeDtypeStruct((r, c), F32)] * 4,
        compiler_params=_params(("parallel",), 40),
    )(*[pltpu.with_memory_space_constraint(a, pltpu.HBM) for a in (parts, w, m, v)])


_GAIN_ROWS = ("g_mix", "g_xattn", "g_mem", "g_mlp", "g_final")
PAIR_ROW = 5
LOSS_ROW = 6
TAPS_ROW = 8
SMALL_ROWS = 16
_SMALL = _GAIN_ROWS + ("g_attn_out", "g_conv_out", "conv_w")
CONV_SHARD = 512 // N_DEV


def _pack_small(gains, gg_attn, gg_conv, gcw, loss_blk):
    def body(*refs):
        o_ref = refs[-1]
        ga_ref, gc_ref, cw_ref, l_ref = refs[len(gains):-1]
        o_ref[...] = jnp.zeros_like(o_ref)
        for i, g_ref in enumerate(refs[:len(gains)]):
            o_ref[i:i + 1, :] = g_ref[...]
        o_ref[PAIR_ROW:PAIR_ROW + 1, 0:512] = ga_ref[...]
        o_ref[PAIR_ROW:PAIR_ROW + 1, 512:1024] = gc_ref[...]
        o_ref[LOSS_ROW:LOSS_ROW + 1, 0:BLK] = l_ref[0:1, :]
        o_ref[TAPS_ROW:SMALL_ROWS, 0:512] = cw_ref[...]

    return pl.pallas_call(body, name="pack_small", out_shape=jax.ShapeDtypeStruct((SMALL_ROWS, 1024), F32))(
        *gains, gg_attn, gg_conv, gcw, loss_blk)


def _update_small(parts, me, w, m, v):
    n = len(_SMALL)

    def body(me_ref, p_ref, *refs):
        ins, loss_ref, outs = refs[:3 * n], refs[3 * n], refs[3 * n + 1:]

        def total(lo, hi):
            s = p_ref[0, lo:hi, :]
            for k in range(1, N_DEV):
                s = s + p_ref[k, lo:hi, :]
            return s

        grads = {k: total(i, i + 1) for i, k in enumerate(_GAIN_ROWS)}
        both = total(PAIR_ROW, PAIR_ROW + 1)
        grads["g_attn_out"], grads["g_conv_out"] = both[:, 0:512], both[:, 512:1024]
        taps = total(TAPS_ROW, SMALL_ROWS)
        mine = jnp.zeros((SMALL_ROWS - TAPS_ROW, BLK), F32)
        for j in range(N_DEV):
            lo = j * CONV_SHARD // BLK * BLK
            blk = taps[:, lo:lo + BLK]
            if j * CONV_SHARD != lo:
                blk = pltpu.roll(blk, BLK - (j * CONV_SHARD - lo), axis=1)
            mine = jnp.where(me_ref[0] == j, blk, mine)
        grads["conv_w"] = mine[0:3, 0:CONV_SHARD]
        loss_ref[...] = total(LOSS_ROW, LOSS_ROW + 1)[:, 0:1]
        for i, k in enumerate(_SMALL):
            g_ref, d_ref, mo_ref, vo_ref = outs[4 * i:4 * i + 4]
            g_ref[...] = grads[k]
            d_ref[...], mo_ref[...], vo_ref[...] = _adamw(grads[k], ins[i][...], ins[n + i][...], ins[2 * n + i][...])

    vmem = pl.BlockSpec(memory_space=pltpu.VMEM)
    args = [d[k] for d in (w, m, v) for k in _SMALL]
    res = pl.pallas_call(
        body, name="update_small",
        in_specs=[pl.BlockSpec(memory_space=pltpu.SMEM)] + [vmem] * (1 + 3 * n),
        out_shape=[jax.ShapeDtypeStruct((1, 1), F32)] + [jax.ShapeDtypeStruct(w[k].shape, F32) for k in _SMALL
                                                         for _ in range(4)],
    )(me, parts, *args)
    return res[0], {k: res[1 + 4 * i:5 + 4 * i] for i, k in enumerate(_SMALL)}


def _head_sum_matrix():
    r = lax.broadcasted_iota(jnp.int32, (512, 512), 0) // HEAD_DIM
    c = lax.broadcasted_iota(jnp.int32, (512, 512), 1) // HEAD_DIM
    return (r == c).astype(BF16)


_SHARD_AXIS = dict(w_in=1, w_out=0, w_q=0, w_kv=1, w_o=0, w_up=1, w_down=0, conv_w=None, small=None)


class _Weights:
    def __init__(self, full, shards=None):
        self.full = dict(full)
        self.shards = shards

    def rider(self, names, late=False):
        if self.shards is None:
            return None
        return _Gather([self.shards[n] for n in names], [_SHARD_AXIS[n] for n in names], late)

    def arrived(self, names, gathered):
        if gathered is not None:
            for n, g in zip(names, gathered):
                self.full[n] = g.transpose(1, 0, 2).reshape(g.shape[1], -1) if n == "conv_w" else g

    def __getitem__(self, name):
        return self.full[name]


class _Grads:
    def __init__(self, distributed):
        self.distributed = distributed
        self.local = {}
        self.pending = {}

    def add(self, name, g):
        self.local[name] = g

    def send(self, *names):
        if not self.distributed:
            return []
        rider = _Exchange([self.local[n] for n in names], [_SHARD_AXIS[n] for n in names])
        started = _exchange_start(rider, "send_" + "_".join(names))
        self.pending[names[0]] = (names, rider, started)
        return [started[3]]

    def wait(self, first_name, after):
        names, rider, started = self.pending.pop(first_name)
        return _exchange_wait(rider, started, after, "wait_" + "_".join(names))


def _ride(fn, *args, rider=None, **kw):
    if rider is None:
        return fn(*args, **kw), None
    return fn(*args, rider=rider, **kw)


def _local_step(x, mem, tgt, gains, weights, grads):
    names = ["w_in", "conv_w"]
    (x, tgt), got = _ride(_reorder, [x, tgt], "reorder_in", rider=weights.rider(names, late=True))
    weights.arrived(names, got)
    w_in, cw = weights["w_in"], weights["conv_w"]

    names = ["w_out", "w_kv"]
    (qkv, gates, h1), got = _ride(_proj, x, gains["g_mix"], w_in, tb=1024, rider=weights.rider(names))
    weights.arrived(names, got)
    names = ["w_q", "w_o", "w_up"]
    (attn, *lses), got = _ride(_attention_fwd, qkv, rider=weights.rider(names))
    weights.arrived(names, got)
    x1, merged = _mixer_fwd(x, attn, gates, cw, gains["g_attn_out"], gains["g_conv_out"], weights["w_out"])
    kv, mem_n = _norm_matmul(mem, gains["g_mem"], weights["w_kv"], name="mem_kv", out_dtype=BF16, tb=mem.shape[0],
                             bn=1024, save_h=True)
    x2, h2, qm, om = _xattn_fwd(x1, gains["g_xattn"], weights["w_q"], kv, weights["w_o"], tb=512)
    w_up = weights["w_up"]
    (a, h3), got = _ride(_norm_matmul, x2, gains["g_mlp"], w_up, name="mlp_up", out_dtype=BF16, tb=1024, bn=2048,
                         relu=True, save_h=True, rider=weights.rider(["w_down"], late=True))
    weights.arrived(["w_down"], got)
    w_down = weights["w_down"]
    dx3, dx3b, loss_blk, gg_final = _mlp_down_loss(a, w_down, x2, tgt, gains["g_final"], tb=512)

    dpre = _mlp_dpre(dx3b, w_down, a, tb=1024, bn=2048)
    grads.add("w_down", _matmul_tn(a, dx3b, name="grad_w_down", bm=512, bn=1024, square_a=True))
    sent = grads.send("w_down")
    grads.add("w_up", _matmul_tn(h3, dpre, name="grad_w_up", bm=1024, bn=1024, after=sent))
    sent = grads.send("w_up")
    dx2, dx2b, gg_mlp = _matmul_nt_normbwd(dpre, w_up, x2, gains["g_mlp"], dx3, name="mlp_dx", tb=512,
                                           also_bf16=True, after=sent)

    grads.add("w_o", _matmul_tn(om, dx2b, name="grad_w_o", bm=512, bn=512))
    dx1, dk, dv, gg_xattn, gw_q = _xattn_bwd(dx2, x1, gains["g_xattn"], qm, h2, weights["w_q"], kv, weights["w_o"],
                                             tb=512)
    grads.add("w_q", gw_q)
    gw_kv, gg_mem = _mem_bwd(dk, dv, weights["w_kv"], mem, mem_n, gains["g_mem"])
    grads.add("w_kv", gw_kv)

    dattn, dsum, dy, gg_attn, gg_conv, gw_out = _mixer_bwd(dx1, merged, attn, gates, cw, gains["g_attn_out"],
                                                           gains["g_conv_out"], weights["w_out"], _head_sum_matrix())
    grads.add("w_out", gw_out)
    sent = grads.send("w_o", "w_q", "w_kv", "w_out")
    dproj, gcw = _conv_bwd(dy, gates, cw, after=sent)
    dproj = _attention_bwd(qkv, dattn, dsum, lses, dproj)
    grads.add("w_in", _matmul_tn(h1, dproj, name="grad_w_in", bm=1024, bn=512))
    sent = grads.send("w_in")
    grad_x, gg_mix = _matmul_nt_normbwd(dproj, w_in, x, gains["g_mix"], dx1, name="mixer_dx", tb=512,
                                        to_natural=True, after=sent)

    grads.add("small", _pack_small([gg_mix, gg_xattn, gg_mem, gg_mlp, gg_final], gg_attn, gg_conv, gcw, loss_blk))
    return grad_x


_BIG = ("w_in", "w_out", "w_q", "w_kv", "w_o", "w_up", "w_down")


def kernel(x, mem, g_mix, w_in, conv_w, g_attn_out, g_conv_out, w_out, g_xattn, g_mem, w_q_mem, w_kv_mem, w_o_mem, g_mlp, w_up, w_down, g_final, loss_target, m_g_mix, m_w_in, m_conv_w, m_g_attn_out, m_g_conv_out, m_w_out, m_g_xattn, m_g_mem, m_w_q_mem, m_w_kv_mem, m_w_o_mem, m_g_mlp, m_w_up, m_w_down, m_g_final, v_g_mix, v_w_in, v_conv_w, v_g_attn_out, v_g_conv_out, v_w_out, v_g_xattn, v_g_mem, v_w_q_mem, v_w_kv_mem, v_w_o_mem, v_g_mlp, v_w_up, v_w_down, v_g_final):
    d = x.shape[-1]
    me = 4 * lax.axis_index("x") + 2 * lax.axis_index("y") + lax.axis_index("c")
    w_shards = dict(w_in=w_in, w_out=w_out, w_q=w_q_mem, w_kv=w_kv_mem, w_o=w_o_mem, w_up=w_up, w_down=w_down)
    m_shards = dict(w_in=m_w_in, w_out=m_w_out, w_q=m_w_q_mem, w_kv=m_w_kv_mem, w_o=m_w_o_mem, w_up=m_w_up,
                    w_down=m_w_down)
    v_shards = dict(w_in=v_w_in, w_out=v_w_out, w_q=v_w_q_mem, w_kv=v_w_kv_mem, w_o=v_w_o_mem, w_up=v_w_up,
                    w_down=v_w_down)
    gains = dict(g_mix=g_mix, g_attn_out=g_attn_out, g_conv_out=g_conv_out, g_xattn=g_xattn, g_mem=g_mem,
                 g_mlp=g_mlp, g_final=g_final)
    gains2 = {k: v.reshape(1, -1) for k, v in gains.items()}

    shards = {k: w_shards[k].astype(BF16) for k in _BIG}
    shards["conv_w"] = conv_w
    grads = _Grads(distributed=True)
    grad_x = _local_step(x[0], mem[0], loss_target[0], gains2, _Weights({}, shards), grads)

    after = grads.send("small")
    outs = {}
    tiles = dict(w_in=256, w_out=128, w_q=128, w_kv=256, w_o=128, w_up=256, w_down=256)
    for group in (("w_down",), ("w_up",), ("w_o", "w_q", "w_kv", "w_out"), ("w_in",)):
        for k, received in zip(group, grads.wait(group[0], after)):
            outs[k] = _sum_adamw(received, w_shards[k], m_shards[k], v_shards[k], name=f"adamw_{k}", tr=tiles[k])
            after = [outs[k][0]]
    small_received, = grads.wait("small", after)

    m_small = dict(g_mix=m_g_mix, g_attn_out=m_g_attn_out, g_conv_out=m_g_conv_out, g_xattn=m_g_xattn,
                   g_mem=m_g_mem, g_mlp=m_g_mlp, g_final=m_g_final)
    v_small = dict(g_mix=v_g_mix, g_attn_out=v_g_attn_out, g_conv_out=v_g_conv_out, g_xattn=v_g_xattn,
                   g_mem=v_g_mem, g_mlp=v_g_mlp, g_final=v_g_final)
    as_rows = lambda vals, conv: dict({k: a.reshape(1, -1) for k, a in vals.items()}, conv_w=conv)
    loss, small_out = _update_small(small_received, me.reshape(1), as_rows(gains, conv_w),
                                    as_rows(m_small, m_conv_w), as_rows(v_small, v_conv_w))
    small_out = {k: [a.reshape(dict(gains, conv_w=conv_w)[k].shape) for a in res] for k, res in small_out.items()}
    names = {"g_mix": "g_mix", "w_in": "w_in", "conv_w": "conv_w", "g_attn_out": "g_attn_out",
             "g_conv_out": "g_conv_out", "w_out": "w_out", "g_xattn": "g_xattn", "g_mem": "g_mem",
             "w_q_mem": "w_q", "w_kv_mem": "w_kv", "w_o_mem": "w_o", "g_mlp": "g_mlp", "w_up": "w_up",
             "w_down": "w_down", "g_final": "g_final"}
    result = [loss.reshape(()), grad_x[None]]
    for which in range(4):
        for key in names.values():
            result.append(outs[key][which] if key in outs else small_out[key][which])
    return tuple(result)
```

```python
import math

import jax
import jax.numpy as jnp
from jax import lax
from jax.experimental import pallas as pl
from jax.experimental.pallas import tpu as pltpu

F32 = jnp.float32
BF16 = jnp.bfloat16
NORM_EPS = 1e-6
NEG_INF = -1e30
N_DEV = 8
BLK = 128
HEAD_DIM = 64
N_MEM_HEADS = 4
ADAM_LR = 0.001
ADAM_B1 = 0.9
ADAM_B2 = 0.999
ADAM_EPS = 1e-08
ADAM_WD = 0.01
ADAM_STEP = 10
MESH = pl.DeviceIdType.MESH
ANY = pl.BlockSpec(memory_space=pl.ANY)


def _dot(a, b):
    return jnp.dot(a, b, preferred_element_type=F32)


def _dot_nt(a, b):
    return lax.dot_general(a, b, (((1,), (1,)), ((), ())), preferred_element_type=F32)


def _dot_tn(a, b):
    return lax.dot_general(a, b, (((0,), (0,)), ((), ())), preferred_element_type=F32)


def _params(semantics, vmem_mb):
    return pltpu.CompilerParams(dimension_semantics=semantics, vmem_limit_bytes=vmem_mb << 20)


def _rms_fwd(x, g):
    r = lax.rsqrt(jnp.mean(x * x, axis=-1, keepdims=True) + NORM_EPS)
    xh = x * r
    return xh * g, xh, r


def _rms_bwd(dy, xh, r, g):
    gy = dy * g
    return r * (gy - xh * jnp.mean(xh * gy, axis=-1, keepdims=True))


def _position():
    x, y, c = lax.axis_index("x"), lax.axis_index("y"), lax.axis_index("c")
    return x, y, c


def _block_of(ref, j, axis, shard_shape):
    r, c = shard_shape
    if axis is None:
        return ref.at[j]
    if axis == 0:
        return ref.at[pl.ds(j * r, r), :]
    return ref.at[:, pl.ds(j * c, c)]


class _Gather:
    has_mid = True
    alias_pairs = ()

    def __init__(self, shards, axes, late=False):
        self.arrays = list(shards)
        self.axes = list(axes)
        self.late = late
        self.n = len(self.arrays)

    def out_shape(self):
        res = []
        for s, axis in zip(self.arrays, self.axes):
            r, c = s.shape
            shape = (N_DEV, r, c) if axis is None else (N_DEV * r, c) if axis == 0 else (r, N_DEV * c)
            res.append(jax.ShapeDtypeStruct(shape, s.dtype))
        return res

    def scratch(self):
        return [pltpu.SemaphoreType.DMA((self.n, 7)), pltpu.SemaphoreType.DMA((self.n, 7)),
                pltpu.SemaphoreType.DMA((self.n,))]

    def _ctx(self, ins, outs, sems):
        send_sems, recv_sems, local_sems = sems
        x, y, c = _position()
        me, sibling = (x, y, c), (x, y, 1 - c)
        chips = [(1 - x, y), (x, 1 - y), (1 - x, 1 - y)]

        def lin(px, py, pc):
            return 4 * px + 2 * py + pc

        def place(a, block):
            return _block_of(outs[a], lin(*block), self.axes[a], self.arrays[a].shape)

        def copy(a, k, block, to, src=None):
            dst = place(a, block)
            return pltpu.make_async_remote_copy(
                src_ref=dst if src is None else src, dst_ref=dst,
                send_sem=send_sems.at[a, k], recv_sem=recv_sems.at[a, k],
                device_id=to, device_id_type=MESH)

        def mine():
            return [pltpu.make_async_copy(ins[a], place(a, me), local_sems.at[a]) for a in range(self.n)]

        def first():
            res = []
            for a in range(self.n):
                res.append(copy(a, 0, me, sibling, src=ins[a]))
                res += [copy(a, 1 + j, me, (*chip, c), src=ins[a]) for j, chip in enumerate(chips)]
            return res

        return c, me, sibling, chips, copy, mine, first

    def start(self, ins, outs, sems):
        _, _, _, _, _, mine, first = self._ctx(ins, outs, sems)
        for cp in mine() + first():
            cp.start()

    def mid(self, ins, outs, sems):
        c, me, sibling, chips, copy, _, _ = self._ctx(ins, outs, sems)
        for j, chip in enumerate(chips):
            for a in range(self.n):
                copy(a, 1 + j, (*chip, c), me).wait_recv()
                copy(a, 4 + j, (*chip, c), sibling).start()

    def finish(self, ins, outs, sems):
        c, me, sibling, chips, copy, mine, first = self._ctx(ins, outs, sems)
        for a in range(self.n):
            copy(a, 0, sibling, me).wait_recv()
            for j, chip in enumerate(chips):
                copy(a, 4 + j, (*chip, 1 - c), me).wait_recv()
        for cp in first():
            cp.wait_send()
        for j, chip in enumerate(chips):
            for a in range(self.n):
                copy(a, 4 + j, (*chip, c), sibling).wait_send()
        for cp in mine():
            cp.wait()


class _Exchange:
    def __init__(self, parts, axes):
        self.n = len(parts)
        self.axes = list(axes)
        self.arrays = list(parts)

    def _piece(self, a):
        r, c = self.arrays[a].shape
        axis = self.axes[a]
        return (r, c) if axis is None else (r // N_DEV, c) if axis == 0 else (r, c // N_DEV)

    def out_shape(self):
        return [jax.ShapeDtypeStruct((N_DEV,) + self._piece(a), self.arrays[a].dtype) for a in range(self.n)]

    def semaphores(self):
        return [pltpu.SemaphoreType.DMA((7 * self.n,)), pltpu.SemaphoreType.DMA((7 * self.n,)),
                pltpu.SemaphoreType.DMA((self.n,))]

    def _ctx(self, ins, outs, sems):
        send_sems, recv_sems, local_sems = sems
        x, y, c = _position()
        me = 4 * x + 2 * y + c

        def src(a, j):
            return ins[a] if self.axes[a] is None else _block_of(ins[a], j, self.axes[a], self._piece(a))

        def dst(a, j):
            return outs[a].at[j]

        def local():
            return [pltpu.make_async_copy(src(a, me), dst(a, me), local_sems.at[a]) for a in range(self.n)]

        def remote(inbound):
            res = []
            for a in range(self.n):
                for k in range(1, N_DEV):
                    peer = (1 - x if k & 4 else x, 1 - y if k & 2 else y, 1 - c if k & 1 else c)
                    plin = 4 * peer[0] + 2 * peer[1] + peer[2]
                    res.append(pltpu.make_async_remote_copy(
                        src_ref=src(a, plin), dst_ref=dst(a, plin if inbound else me),
                        send_sem=send_sems.at[7 * a + k - 1], recv_sem=recv_sems.at[7 * a + k - 1],
                        device_id=peer, device_id_type=MESH))
            return res

        return local, remote

    def start(self, ins, outs, sems):
        local, remote = self._ctx(ins, outs, sems)
        for cp in local() + remote(False):
            cp.start()

    def finish(self, ins, outs, sems):
        local, remote = self._ctx(ins, outs, sems)
        for cp in remote(True):
            cp.wait_recv()
        for cp in remote(False):
            cp.wait_send()
        for cp in local():
            cp.wait()


def _exchange_start(rider, name):
    n = rider.n
    parts = rider.arrays
    lands = [lax.empty(s.shape, s.dtype) for s in rider.out_shape()]
    hbm = pl.BlockSpec(memory_space=pltpu.HBM)
    sem = pl.BlockSpec(memory_space=pltpu.SEMAPHORE)

    def body(*refs):
        ins, sems = refs[:n], refs[2 * n:2 * n + 3]
        outs, token = refs[2 * n + 3 + n:2 * n + 3 + 2 * n], refs[-1]
        rider.start(ins, outs, sems)
        token[...] = jnp.zeros_like(token)

    res = pl.pallas_call(
        body, name=name,
        out_shape=rider.semaphores() + [pltpu.HBM(p.shape, p.dtype) for p in parts]
                  + [pltpu.HBM(z.shape, z.dtype) for z in lands] + [jax.ShapeDtypeStruct((8, 128), F32)],
        in_specs=[hbm] * (2 * n), out_specs=[sem] * 3 + [hbm] * (2 * n) + [pl.BlockSpec(memory_space=pltpu.VMEM)],
        input_output_aliases={i: 3 + i for i in range(2 * n)},
        compiler_params=pltpu.CompilerParams(has_side_effects=pltpu.SideEffectType.DATAFLOW_SIDE_EFFECTING),
    )(*[pltpu.with_memory_space_constraint(a, pltpu.HBM) for a in parts + lands])
    return res[:3], res[3:3 + n], res[3 + n:3 + 2 * n], res[-1]


def _exchange_wait(rider, started, after, name):
    n = rider.n
    sems, parts, lands, _ = started
    hbm = pl.BlockSpec(memory_space=pltpu.HBM)
    sem = pl.BlockSpec(memory_space=pltpu.SEMAPHORE)

    def body(*refs):
        rider.finish(refs[:n], refs[n:2 * n], refs[2 * n:2 * n + 3])

    res = pl.pallas_call(
        body, name=name, out_shape=[pltpu.HBM(a.shape, a.dtype) for a in list(parts) + list(lands)],
        in_specs=[hbm] * (2 * n) + [sem] * 3 + [ANY] * len(after), out_specs=[hbm] * (2 * n),
        input_output_aliases={i: i for i in range(2 * n)},
        compiler_params=pltpu.CompilerParams(has_side_effects=pltpu.SideEffectType.DATAFLOW_SIDE_EFFECTING),
    )(*parts, *lands, *sems, *after)
    return list(res[n:])


def _pcall(body, *, name, grid, in_specs, out_specs, out_shape, scratch_shapes=(), semantics, vmem_mb, rider=None,
           aliases=None, after=()):
    in_specs, out_specs, out_shape = list(in_specs), list(out_specs), list(out_shape)
    scratch_shapes = list(scratch_shapes)
    aliases = dict(aliases or {})
    if rider is None:
        n_in, after = len(in_specs), list(after)

        def plain(*refs):
            body(*refs[:n_in], *refs[n_in + len(after):])

        call = pl.pallas_call(plain if after else body, name=name, grid=grid, in_specs=in_specs + [ANY] * len(after),
                              out_specs=out_specs, out_shape=out_shape, scratch_shapes=scratch_shapes,
                              input_output_aliases=aliases, compiler_params=_params(semantics, vmem_mb))
        return lambda *args: (list(call(*args, *after)), None)
    n_in, n_out, n_scr = len(in_specs), len(out_specs), len(scratch_shapes)
    r_in, r_shapes = len(rider.arrays), rider.out_shape()
    r_out = len(r_shapes)
    aliases.update({n_in + i: n_out + o for i, o in rider.alias_pairs})
    total = math.prod(grid)
    mid_step = total - 1 if rider.has_mid and rider.late else (3 * total) // 4

    def wrapped(*refs):
        bounds = [0, n_in, r_in, n_out, r_out, n_scr]
        for i in range(1, len(bounds)):
            bounds[i] += bounds[i - 1]
        a, ra, o, ro, s = (refs[bounds[i]:bounds[i + 1]] for i in range(5))
        rs = refs[bounds[5]:]
        step = pl.program_id(0)
        for k in range(1, len(grid)):
            step = step * grid[k] + pl.program_id(k)
        pl.when(step == 0)(lambda: rider.start(ra, ro, rs))
        body(*a, *o, *s)
        if rider.has_mid:
            pl.when(step == mid_step)(lambda: rider.mid(ra, ro, rs))
        pl.when(step == total - 1)(lambda: rider.finish(ra, ro, rs))

    call = pl.pallas_call(
        wrapped, name=name, grid=grid, in_specs=in_specs + [ANY] * r_in, out_specs=out_specs + [ANY] * r_out,
        out_shape=out_shape + r_shapes, scratch_shapes=scratch_shapes + rider.scratch(),
        input_output_aliases=aliases, compiler_params=_params(("arbitrary",) * len(grid), vmem_mb))

    def run(*args):
        res = call(*args, *rider.arrays)
        return list(res[:n_out]), list(res[n_out:])

    return run


def _norm_matmul(x, g, w, *, name, out_dtype, tb, bn, relu=False, save_h=False, rider=None):
    t, d = x.shape
    n = w.shape[1]

    def body(x_ref, g_ref, w_ref, o_ref, *rest):
        h_scr = rest[-1]

        @pl.when(pl.program_id(1) == 0)
        def _():
            h = _rms_fwd(x_ref[...], g_ref[...])[0].astype(BF16)
            h_scr[...] = h
            if save_h:
                rest[0][...] = h

        acc = _dot(h_scr[...], w_ref[...])
        if relu:
            acc = jnp.maximum(acc, 0.0)
        o_ref[...] = acc.astype(out_dtype)

    out_shape = [jax.ShapeDtypeStruct((t, n), out_dtype)]
    out_specs = [pl.BlockSpec((tb, bn), lambda i, j: (i, j))]
    if save_h:
        out_shape.append(jax.ShapeDtypeStruct((t, d), BF16))
        out_specs.append(pl.BlockSpec((tb, d), lambda i, j: (i, 0)))
    res, extra = _pcall(
        body, name=name, grid=(t // tb, n // bn),
        in_specs=[pl.BlockSpec((tb, d), lambda i, j: (i, 0)),
                  pl.BlockSpec((1, d), lambda i, j: (0, 0)),
                  pl.BlockSpec((d, bn), lambda i, j: (0, j))],
        out_specs=out_specs, out_shape=out_shape,
        scratch_shapes=[pltpu.VMEM((tb, d), BF16)],
        semantics=("parallel", "arbitrary"), vmem_mb=48, rider=rider,
    )(x, g, w)
    res = res if save_h else res[0]
    return res if rider is None else (res, extra)


def _proj(x, g, w, *, tb, rider=None):
    t, d = x.shape
    half = w.shape[1] // 2

    def body(x_ref, g_ref, w_ref, qkv_ref, gates_ref, h_ref, h_scr):
        j = pl.program_id(1)

        @pl.when(j == 0)
        def _():
            h = _rms_fwd(x_ref[...], g_ref[...])[0].astype(BF16)
            h_scr[...] = h
            h_ref[...] = h

        acc = _dot(h_scr[...], w_ref[...])

        @pl.when(j == 0)
        def _():
            qkv_ref[...] = acc

        @pl.when(j == 1)
        def _():
            gates_ref[...] = acc.astype(BF16)

    tok = lambda c: pl.BlockSpec((tb, c), lambda i, j: (i, 0))
    res, extra = _pcall(
        body, name="proj", grid=(t // tb, 2),
        in_specs=[tok(d), pl.BlockSpec((1, d), lambda i, j: (0, 0)), pl.BlockSpec((d, half), lambda i, j: (0, j))],
        out_specs=[tok(half), tok(half), tok(d)],
        out_shape=[jax.ShapeDtypeStruct((t, half), F32), jax.ShapeDtypeStruct((t, half), BF16),
                   jax.ShapeDtypeStruct((t, d), BF16)],
        scratch_shapes=[pltpu.VMEM((tb, d), BF16)],
        semantics=("parallel", "arbitrary"), vmem_mb=48, rider=rider,
    )(x, g, w)
    return res if rider is None else (res, extra)


def _matmul_nt_normbwd(dy, w, x, g, dres, *, name, tb, also_bf16=False, to_natural=False, after=()):
    t, d = x.shape
    stacked = dy.ndim == 3
    n_i = SEG // TI
    if to_natural:
        tb = N_RES * TI

    def body(dy_ref, w_ref, x_ref, g_ref, dres_ref, *rest):
        rest = list(rest)
        dx_ref = rest.pop(0)
        dxb_ref = rest.pop(0) if also_bf16 else None
        gg_ref = rest.pop(0)
        i = pl.program_id(0)

        def rows(ref, *lead):
            v = ref[lead] if lead else ref[...]
            return v[0].reshape(tb, v.shape[-1]) if to_natural else v

        if stacked:
            kb = dy_ref.shape[-1]
            dh = _dot_nt(rows(dy_ref, 0), w_ref[:, 0:kb])
            for s in range(1, dy_ref.shape[0]):
                dh = dh + _dot_nt(rows(dy_ref, s), w_ref[:, s * kb:(s + 1) * kb])
        else:
            dh = _dot_nt(rows(dy_ref), w_ref[...])
        g_v = g_ref[...]
        _, xh, r = _rms_fwd(rows(x_ref), g_v)
        dx = _rms_bwd(dh, xh, r, g_v) + rows(dres_ref)
        if to_natural:
            scr = rest.pop(0)
            for cb in range(d // BLK):
                cols = slice(cb * BLK, (cb + 1) * BLK)
                slab = scr.at[cb]
                for res in range(N_RES):
                    slab[pl.ds(res, TI, stride=N_RES), :] = dx[res * TI:(res + 1) * TI, cols]
                dx_ref[:, cols] = slab[...]
        else:
            dx_ref[...] = dx
        if also_bf16:
            dxb_ref[...] = dx.astype(BF16)
        part = jnp.sum(dh * xh, axis=0, keepdims=True)

        @pl.when(i == 0)
        def _():
            gg_ref[...] = part

        @pl.when(i != 0)
        def _():
            gg_ref[...] += part

    tok = pl.BlockSpec((tb, d), lambda i: (i, 0))
    row = pl.BlockSpec((1, d), lambda i: (0, 0))
    if to_natural:
        act = pl.BlockSpec((1, N_RES, TI, d), lambda i: (i // n_i, 0, i % n_i, 0))
        dy_spec = pl.BlockSpec((dy.shape[0], 1, N_RES, TI, dy.shape[2]), lambda i: (0, i // n_i, 0, i % n_i, 0))
        dy, x, dres = dy.reshape(dy.shape[0], t // HALF, N_RES, SEG, dy.shape[2]), _x4(x), _x4(dres)
    elif stacked:
        act, dy_spec = tok, pl.BlockSpec((dy.shape[0], tb, dy.shape[2]), lambda i: (0, i, 0))
    else:
        act, dy_spec = tok, pl.BlockSpec((tb, dy.shape[1]), lambda i: (i, 0))
    in_specs = [dy_spec, pl.BlockSpec(w.shape, lambda i: (0, 0)), act, row, act]
    out_specs = [tok] + ([tok] if also_bf16 else []) + [row]
    out_shape = ([jax.ShapeDtypeStruct((t, d), F32)] + ([jax.ShapeDtypeStruct((t, d), BF16)] if also_bf16 else [])
                 + [jax.ShapeDtypeStruct((1, d), F32)])
    res, _ = _pcall(
        body, name=name, grid=(t // tb,), in_specs=in_specs, out_specs=out_specs, out_shape=out_shape,
        scratch_shapes=[pltpu.VMEM((d // BLK, tb, BLK), F32)] if to_natural else [],
        semantics=("arbitrary",), vmem_mb=56, after=after,
    )(dy, w, x, g, dres)
    return res


def _matmul_tn(a, b, *, name, bm, bn, square_a=False, after=()):
    t, m = a.shape
    stacked = b.ndim == 3
    n = b.shape[0] * bn if stacked else b.shape[1]

    def body(a_ref, b_ref, o_ref):
        av = a_ref[...]
        if square_a:
            av = av.astype(F32)
            av = (av * av).astype(BF16)
        o_ref[...] = _dot_tn(av, b_ref[...]).astype(BF16)

    res, _ = _pcall(
        body, name=name, grid=(m // bm, n // bn),
        in_specs=[pl.BlockSpec((t, bm), lambda i, j: (0, i)),
                  pl.BlockSpec((None, t, bn), lambda i, j: (j, 0, 0)) if stacked
                  else pl.BlockSpec((t, bn), lambda i, j: (0, j))],
        out_specs=[pl.BlockSpec((bm, bn), lambda i, j: (i, j))], out_shape=[jax.ShapeDtypeStruct((m, n), BF16)],
        semantics=("parallel", "parallel"), vmem_mb=56, after=after,
    )(a, b)
    return res[0]


N_RES = 16
SEG = 128
HALF = N_RES * SEG
TI = 32
HALO = 16


def _x4(a):
    return a.reshape(a.shape[0] // HALF, N_RES, SEG, a.shape[1])


def _reorder(arrays, name, rider=None):
    t, c = arrays[0].shape
    n = len(arrays)
    n_i = SEG // TI

    def body(*refs):
        scr = refs[-1]
        for i_ref, o_ref in zip(refs[:n], refs[n:2 * n]):
            for cb in range(c // BLK):
                cols = slice(cb * BLK, (cb + 1) * BLK)
                slab = scr.at[cb]
                slab[...] = i_ref[:, cols]
                for r in range(N_RES):
                    o_ref[0, r, :, cols] = slab[pl.ds(r, TI, stride=N_RES), :]

    res, extra = _pcall(
        body, name=name, grid=(t // (TI * N_RES),),
        in_specs=[pl.BlockSpec((TI * N_RES, c), lambda s: (s, 0))] * n,
        out_specs=[pl.BlockSpec((1, N_RES, TI, c), lambda s: (s // n_i, 0, s % n_i, 0))] * n,
        out_shape=[jax.ShapeDtypeStruct((t // HALF, N_RES, SEG, c), F32)] * n,
        scratch_shapes=[pltpu.VMEM((c // BLK, TI * N_RES, BLK), F32)],
        semantics=("parallel",), vmem_mb=32, rider=rider,
    )(*arrays)
    res = [r.reshape(t, c) for r in res]
    return res if rider is None else (res, extra)


_PATTERNS = ((1, 16, 8, SEG), (4, 4, 32, 4 * SEG), (16, 1, SEG, 0))
_FIRST = {1: 1, 4: 4, 16: 16}


def _group_rows(d, g):
    a = g >> 4
    if d == 16:
        base = a * HALF + (g & 15) * SEG
        prev = base - HALF
    elif d == 4:
        c = (g >> 2) & 3
        base = a * HALF + (g & 3) * SEG + c * 32
        prev = jnp.where(c > 0, base - 32, base - HALF + 96)
    else:
        c = g & 15
        base = a * HALF + c * 8
        prev = jnp.where(c > 0, base - 8, base - HALF + 120)
    return base, prev


def _load_rows(ref, base, n, rows, stride):
    parts = [ref[pl.ds(pl.multiple_of(base + j * stride, 8), rows), :] for j in range(n)]
    return parts[0] if n == 1 else jnp.concatenate(parts, axis=0)


def _store_rows(ref, base, val, n, rows, stride, add=False):
    for j in range(n):
        sl = pl.ds(pl.multiple_of(base + j * stride, 8), rows)
        piece = val[j * rows:(j + 1) * rows, :]
        if add:
            ref[sl, :] += piece
        else:
            ref[sl, :] = piece


def _band_bias(n, rows):
    shift = rows.bit_length() - 1
    lq = lax.broadcasted_iota(jnp.int32, (BLK, BLK), 0)
    lk = lax.broadcasted_iota(jnp.int32, (BLK, BLK), 1)
    iq = (lq & (rows - 1)) * n + (lq >> shift)
    ik = (lk & (rows - 1)) * n + (lk >> shift)
    zero = jnp.zeros((BLK, BLK), F32)
    return jnp.where(ik >= iq, zero, NEG_INF), jnp.where(ik <= iq, zero, NEG_INF)


def _set_bias(bias_scr, n, rows):
    prev_b, cur_b = _band_bias(n, rows)
    for half in range(2):
        bias_scr[half * BLK:(half + 1) * BLK, 0:BLK] = prev_b
        bias_scr[half * BLK:(half + 1) * BLK, BLK:2 * BLK] = cur_b


SCALE = 1.0 / math.sqrt(HEAD_DIM)


def _head_consts(value=1.0):
    lane_lo = lax.broadcasted_iota(jnp.int32, (BLK, BLK), 1) < HEAD_DIM
    return lane_lo, [jnp.where(lane_lo, value, 0.0).astype(BF16), jnp.where(lane_lo, 0.0, value).astype(BF16)]


def _stack_heads(v, head_mask):
    return jnp.concatenate([v * head_mask[0], v * head_mask[1]], axis=0)


def _unstack_heads(v2, lane_lo):
    return jnp.where(lane_lo, v2[:BLK], v2[BLK:])


def _rows_per_head(v, lane_lo):
    rolled = pltpu.roll(v, HEAD_DIM, axis=1)
    return jnp.concatenate([jnp.where(lane_lo, v, rolled), jnp.where(lane_lo, rolled, v)], axis=0)


WIDTH = 4


def _loop(lo, hi, fn, width=None):
    if width is None:
        def body(g, carry):
            fn(g)
            return carry

        if hi > lo:
            lax.fori_loop(lo, hi, body, 0)
        return
    while hi > lo:
        trips = (hi - lo) // width
        if trips:
            def body(i, carry, lo=lo, width=width):
                fn([lo + width * i + j for j in range(width)])
                return carry

            lax.fori_loop(0, trips, body, 0)
            lo += trips * width
        width = max(1, width // 2)


def _mix_weights(l1, l2, l3):
    mx = jnp.maximum(jnp.maximum(l1, l2), l3)
    e1, e2, e3 = jnp.exp(l1 - mx), jnp.exp(l2 - mx), jnp.exp(l3 - mx)
    inv = 1.0 / (e1 + e2 + e3)
    return e1 * inv, e2 * inv, e3 * inv


def _attention_fwd(qkv, rider=None):
    t = qkv.shape[0]
    groups = 16 * (t // HALF)

    def body(q_ref, k_ref, v_ref, attn_ref, l1_ref, l2_ref, l3_ref, o_scr, bias_scr):
        lane_lo, q_mask = _head_consts(SCALE)
        l_refs = (l1_ref, l2_ref, l3_ref)
        for p, (d, n, rows, stride) in enumerate(_PATTERNS):
            _set_bias(bias_scr, n, rows)
            o_p, l_p = o_scr.at[p], l_refs[p]

            def block(gs, has_prev):
                at = [_group_rows(d, g) for g in gs]

                def load(ref, b):
                    return _load_rows(ref, b, n, rows, stride).astype(BF16)

                q2 = [_stack_heads(load(q_ref, b), q_mask) for b, _ in at]
                k2 = [load(k_ref, b) for b, _ in at]
                v2 = [load(v_ref, b) for b, _ in at]
                if has_prev:
                    k2 = [jnp.concatenate([load(k_ref, pv), k], axis=0) for (_, pv), k in zip(at, k2)]
                    v2 = [jnp.concatenate([load(v_ref, pv), v], axis=0) for (_, pv), v in zip(at, v2)]
                s = [_dot_nt(q, k) for q, k in zip(q2, k2)]
                s = [x + (bias_scr[...] if has_prev else bias_scr[:, BLK:2 * BLK]) for x in s]
                mx = [jnp.max(x, axis=1, keepdims=True) for x in s]
                e = [jnp.exp(x - m) for x, m in zip(s, mx)]
                den = [jnp.sum(x, axis=1, keepdims=True) for x in e]
                o2 = [_dot(x.astype(BF16), v) * (1.0 / dn) for x, v, dn in zip(e, v2, den)]
                lse2 = [jnp.broadcast_to(m + jnp.log(dn), (2 * BLK, BLK)) for m, dn in zip(mx, den)]
                for (b, _), o, l in zip(at, o2, lse2):
                    _store_rows(o_p, b, _unstack_heads(o, lane_lo), n, rows, stride)
                    _store_rows(l_p, b, _unstack_heads(l, lane_lo), n, rows, stride)

            _loop(0, _FIRST[d], lambda gs: block(gs, False), width=2 * WIDTH)
            _loop(_FIRST[d], groups, lambda gs: block(gs, True), width=2 * WIDTH)

        def mix(i):
            sl = pl.ds(pl.multiple_of(i * 256, 256), 256)
            w = _mix_weights(l1_ref[sl, :], l2_ref[sl, :], l3_ref[sl, :])
            attn_ref[sl, :] = w[0] * o_scr[0, sl, :] + w[1] * o_scr[1, sl, :] + w[2] * o_scr[2, sl, :]

        _loop(0, t // 256, mix)

    def col(c0):
        return pl.BlockSpec((t, BLK), lambda hp: (0, c0 + hp))

    res, extra = _pcall(
        body, name="attention_fwd", grid=(4,), in_specs=[col(0), col(4), col(8)], out_specs=[col(0)] * 4,
        out_shape=[jax.ShapeDtypeStruct((t, 512), F32)] * 4,
        scratch_shapes=[pltpu.VMEM((3, t, BLK), F32), pltpu.VMEM((2 * BLK, 2 * BLK), F32)],
        semantics=("parallel",), vmem_mb=48, rider=rider,
    )(qkv, qkv, qkv)
    return res if rider is None else (res, extra)


def _attention_bwd(qkv, dattn, dsum, lses, dproj):
    t = qkv.shape[0]
    groups = 16 * (t // HALF)

    def body(q_ref, k_ref, v_ref, da_ref, ds_ref, l1_ref, l2_ref, l3_ref, kept_ref, out_ref, acc, bias_scr):
        del kept_ref
        lane_lo, head_mask = _head_consts()
        q_mask = _head_consts(SCALE)[1]
        l_refs = (l1_ref, l2_ref, l3_ref)

        def clear(i):
            sl = pl.ds(pl.multiple_of(i * 512, 512), 512)
            for s in range(3):
                acc[s, sl, :] = jnp.zeros((512, BLK), F32)

        _loop(0, t // 512, clear)
        dq_acc, dk_acc, dv_acc = acc.at[0], acc.at[1], acc.at[2]
        for p, (d, n, rows, stride) in enumerate(_PATTERNS):
            _set_bias(bias_scr, n, rows)

            def block(gs, has_prev):
                at = [_group_rows(d, g) for g in gs]

                def load(ref, b):
                    return _load_rows(ref, b, n, rows, stride)

                def put(ref, b, val):
                    _store_rows(ref, b, val, n, rows, stride, add=True)

                def wide(x):
                    return jnp.concatenate([x, x], axis=1) if has_prev else x

                lse = [[load(ref, b) for ref in l_refs] for b, _ in at]
                w = [_mix_weights(*ls)[p] for ls in lse]
                do2 = [_stack_heads((wg * load(da_ref, b)).astype(BF16), head_mask) for wg, (b, _) in zip(w, at)]
                dl2 = [wide(_rows_per_head(wg * load(ds_ref, b), lane_lo)) for wg, (b, _) in zip(w, at)]
                lse2 = [wide(_rows_per_head(ls[p], lane_lo)) for ls in lse]
                q2 = [_stack_heads(load(q_ref, b).astype(BF16), q_mask) for b, _ in at]
                k2 = [load(k_ref, b).astype(BF16) for b, _ in at]
                v2 = [load(v_ref, b).astype(BF16) for b, _ in at]
                if has_prev:
                    k2 = [jnp.concatenate([load(k_ref, pv).astype(BF16), k], axis=0) for (_, pv), k in zip(at, k2)]
                    v2 = [jnp.concatenate([load(v_ref, pv).astype(BF16), v], axis=0) for (_, pv), v in zip(at, v2)]
                s = [_dot_nt(q, k) for q, k in zip(q2, k2)]
                dp = [_dot_nt(do, v) for do, v in zip(do2, v2)]
                pr = [jnp.exp(x + (bias_scr[...] if has_prev else bias_scr[:, BLK:2 * BLK]) - l)
                      for x, l in zip(s, lse2)]
                ds = [(pg * (x - dl)).astype(BF16) for pg, x, dl in zip(pr, dp, dl2)]
                dq2 = [_dot(x, k) * SCALE for x, k in zip(ds, k2)]
                dk2 = [_dot_tn(x, q) for x, q in zip(ds, q2)]
                dv2 = [_dot_tn(pg.astype(BF16), do) for pg, do in zip(pr, do2)]
                for (b, pv), dq, dk, dv in zip(at, dq2, dk2, dv2):
                    put(dq_acc, b, _unstack_heads(dq, lane_lo))
                    if has_prev:
                        put(dk_acc, pv, dk[:BLK])
                        put(dv_acc, pv, dv[:BLK])
                        put(dk_acc, b, dk[BLK:])
                        put(dv_acc, b, dv[BLK:])
                    else:
                        put(dk_acc, b, dk)
                        put(dv_acc, b, dv)

            _loop(0, _FIRST[d], lambda gs: block(gs, False), width=WIDTH)
            _loop(_FIRST[d], groups, lambda gs: block(gs, True), width=WIDTH)

        def emit(i):
            sl = pl.ds(pl.multiple_of(i * 512, 512), 512)
            for s in range(3):
                out_ref[s, sl, :] = acc[s, sl, :].astype(BF16)

        _loop(0, t // 512, emit)

    def col(c0):
        return pl.BlockSpec((t, BLK), lambda hp: (0, c0 + hp))

    res, _ = _pcall(
        body, name="attention_bwd", grid=(4,),
        in_specs=[col(0), col(4), col(8)] + [col(0)] * 5 + [ANY],
        out_specs=[pl.BlockSpec((3, t, BLK), lambda hp: (0, 0, hp))],
        out_shape=[jax.ShapeDtypeStruct(dproj.shape, BF16)],
        scratch_shapes=[pltpu.VMEM((3, t, BLK), F32), pltpu.VMEM((2 * BLK, 2 * BLK), F32)],
        semantics=("parallel",), vmem_mb=56, aliases={8: 0},
    )(qkv, qkv, qkv, dattn, dsum, *lses, dproj)
    return res[0]


def _order_specs(t):
    n_i = SEG // TI
    nblk = (t // HALF) * n_i
    per = TI // HALO

    def main(c, col=0):
        return pl.BlockSpec((1, N_RES, TI, c), lambda s: (s // n_i, 0, s % n_i, col))

    def before(c, col=0):
        return pl.BlockSpec((1, 2, HALO, c), lambda s: (jnp.maximum(s - 1, 0) // n_i, N_RES // 2 - 1,
                                                        (jnp.maximum(s - 1, 0) % n_i) * per + per - 1, col))

    def after(c, col=0):
        return pl.BlockSpec((1, 2, HALO, c), lambda s: (jnp.minimum(s + 1, nblk - 1) // n_i, 0,
                                                        (jnp.minimum(s + 1, nblk - 1) % n_i) * per, col))

    return nblk, main, before, after


def _shift_in(v, row_in, up):
    rows = v.shape[0]
    idx = lax.broadcasted_iota(jnp.int32, v.shape, 0)
    fill = jnp.broadcast_to(row_in, v.shape)
    if up:
        return jnp.where(idx == rows - 1, fill, pltpu.roll(v, rows - 1, axis=0))
    return jnp.where(idx == 0, fill, pltpu.roll(v, 1, axis=0))


def _taps_behind(u, before):
    s15 = _shift_in(u[N_RES - 1], before[1, HALO - 1:HALO, :], up=False)
    s14 = _shift_in(u[N_RES - 2], before[0, HALO - 1:HALO, :], up=False)
    m1 = jnp.concatenate([s15[None], u[:N_RES - 1]], axis=0)
    m2 = jnp.concatenate([s14[None], s15[None], u[:N_RES - 2]], axis=0)
    return m1, m2


def _taps_ahead(u, after):
    t0 = _shift_in(u[0], after[0, 0:1, :], up=True)
    t1 = _shift_in(u[1], after[1, 0:1, :], up=True)
    p1 = jnp.concatenate([u[1:], t0[None]], axis=0)
    p2 = jnp.concatenate([u[2:], t0[None], t1[None]], axis=0)
    return p1, p2


def _conv_fwd(gates, before, first, cw):
    gates, before = gates.astype(F32), before.astype(F32)
    bg, cg, xc = gates[..., 0:512], gates[..., 512:1024], gates[..., 1024:1536]
    u = cg * xc
    ub = before[..., 512:1024] * before[..., 1024:1536]
    ub = jnp.where(first, jnp.zeros_like(ub), ub)
    m1, m2 = _taps_behind(u, ub)
    conv = m2 * cw[0:1, :] + m1 * cw[1:2, :] + u * cw[2:3, :]
    return bg, u, m1, m2, conv


def _sum_tokens(v):
    return jnp.sum(jnp.sum(v, axis=0), axis=0, keepdims=True)


def _mixer_fwd(x, attn, gates, cw, g_a, g_c, w_out):
    t, d = x.shape
    nblk, main, before, _ = _order_specs(t)
    rows = N_RES * TI

    def body(x_ref, at_ref, gt_ref, gb_ref, cw_ref, ga_ref, gc_ref, wa_ref, wb_ref, x1_ref, mg_ref):
        an = _rms_fwd(at_ref[0], ga_ref[...])[0].astype(BF16)
        bg, _, _, _, conv = _conv_fwd(gt_ref[0], gb_ref[0], pl.program_id(0) == 0, cw_ref[...])
        cn = _rms_fwd(bg * conv, gc_ref[...])[0].astype(BF16)
        mg_ref[0, :, :, 0:512] = an
        mg_ref[0, :, :, 512:1024] = cn
        y = _dot(an.reshape(rows, 512), wa_ref[...]) + _dot(cn.reshape(rows, 512), wb_ref[...])
        x1_ref[0] = x_ref[0] + y.reshape(N_RES, TI, d)

    const = lambda r, c, i0=0: pl.BlockSpec((r, c), lambda s: (i0, 0))
    x1, merged = pl.pallas_call(
        body, name="mixer_fwd", grid=(nblk,),
        in_specs=[main(d), main(512), main(1536), before(1536), const(3, 512), const(1, 512), const(1, 512),
                  const(512, d), const(512, d, 1)],
        out_specs=[main(d), main(d)],
        out_shape=[jax.ShapeDtypeStruct(_x4(x).shape, F32), jax.ShapeDtypeStruct(_x4(x).shape, BF16)],
        compiler_params=_params(("parallel",), 48),
    )(_x4(x), _x4(attn), _x4(gates), _x4(gates), cw, g_a, g_c, w_out, w_out)
    return x1.reshape(t, d), merged.reshape(t, d)


def _mixer_bwd(dx1, merged, attn, gates, cw, g_a, g_c, w_out, head_sum):
    t, d = dx1.shape
    nblk, main, before, _ = _order_specs(t)
    rows = N_RES * TI

    def body(dx_ref, mg_ref, at_ref, gt_ref, gb_ref, cw_ref, ga_ref, gc_ref, wa_ref, wb_ref, hs_ref,
             da_ref, dsum_ref, dy_ref, gga_ref, ggc_ref, gw_ref, acc_w):
        s = pl.program_id(0)
        dxb = dx_ref[0].reshape(rows, d).astype(BF16)

        @pl.when(s == 0)
        def _():
            acc_w[...] = jnp.zeros_like(acc_w)

        acc_w[...] += _dot_tn(mg_ref[0].reshape(rows, d), dxb)

        @pl.when(s == nblk - 1)
        def _():
            gw_ref[...] = acc_w[...].astype(BF16)

        dma = _dot_nt(dxb, wa_ref[...]).reshape(N_RES, TI, 512)
        dmc = _dot_nt(dxb, wb_ref[...]).reshape(N_RES, TI, 512)
        attn_v, g_av = at_ref[0], ga_ref[...]
        _, ah, ra = _rms_fwd(attn_v, g_av)
        dattn = _rms_bwd(dma, ah, ra, g_av)
        da_ref[0] = dattn
        z = (dattn * attn_v).reshape(rows, 512)
        hs = hs_ref[...]
        z1 = z.astype(BF16)
        z2 = (z - z1.astype(F32)).astype(BF16)
        dsum_ref[0] = (_dot(z1, hs) + _dot(z2, hs)).reshape(N_RES, TI, 512)
        bg, _, _, _, conv = _conv_fwd(gt_ref[0], gb_ref[0], s == 0, cw_ref[...])
        g_cv = gc_ref[...]
        _, yh, rc = _rms_fwd(bg * conv, g_cv)
        dy_ref[0] = _rms_bwd(dmc, yh, rc, g_cv)
        pa, pc = _sum_tokens(dma * ah), _sum_tokens(dmc * yh)

        @pl.when(s == 0)
        def _():
            gga_ref[...] = pa
            ggc_ref[...] = pc

        @pl.when(s != 0)
        def _():
            gga_ref[...] += pa
            ggc_ref[...] += pc

    const = lambda r, c, i0=0: pl.BlockSpec((r, c), lambda s: (i0, 0))
    shape4 = _x4(attn).shape
    res, _ = _pcall(
        body, name="mixer_bwd", grid=(nblk,),
        in_specs=[main(d), main(d), main(512), main(1536), before(1536), const(3, 512), const(1, 512), const(1, 512),
                  const(512, d), const(512, d, 1), const(512, 512)],
        out_specs=[main(512)] * 3 + [const(1, 512), const(1, 512), const(d, d)],
        out_shape=[jax.ShapeDtypeStruct(shape4, F32)] * 3 + [jax.ShapeDtypeStruct((1, 512), F32)] * 2
        + [jax.ShapeDtypeStruct((d, d), BF16)],
        scratch_shapes=[pltpu.VMEM((d, d), F32)],
        semantics=("arbitrary",), vmem_mb=48,
    )(_x4(dx1), _x4(merged), _x4(attn), _x4(gates), _x4(gates), cw, g_a, g_c, w_out, w_out, head_sum)
    return [r.reshape(t, 512) for r in res[:3]] + res[3:]


def _conv_bwd(dy, gates, cw, after=()):
    t = dy.shape[0]
    nblk, main, before, ahead = _order_specs(t)
    n_i = SEG // TI

    def body(dy_ref, dya_ref, gt_ref, gb_ref, ga_ref, cw_ref, dp_ref, gcw_ref):
        s = pl.program_id(0)
        cw_v, gates_v = cw_ref[...], gt_ref[0]
        bg, u, m1, m2, conv = _conv_fwd(gates_v, gb_ref[0], s == 0, cw_v)
        dy_v = dy_ref[0]
        dconv = dy_v * bg
        dca = dya_ref[0] * ga_ref[0][..., 0:512].astype(F32)
        dca = jnp.where(s == nblk - 1, jnp.zeros_like(dca), dca)
        p1, p2 = _taps_ahead(dconv, dca)
        du = dconv * cw_v[2:3, :] + p1 * cw_v[1:2, :] + p2 * cw_v[0:1, :]
        dp_ref[0, 0] = (dy_v * conv).astype(BF16)
        dp_ref[1, 0] = (du * gates_v[..., 1024:1536].astype(F32)).astype(BF16)
        dp_ref[2, 0] = (du * gates_v[..., 512:1024].astype(F32)).astype(BF16)
        parts = [_sum_tokens(dconv * m2), _sum_tokens(dconv * m1), _sum_tokens(dconv * u)]

        @pl.when(s == 0)
        def _():
            gcw_ref[...] = jnp.zeros_like(gcw_ref)

        for tap in range(3):
            gcw_ref[tap:tap + 1, :] += parts[tap]

    (dproj, gcw), _ = _pcall(
        body, name="conv_bwd", grid=(nblk,),
        in_specs=[main(512), ahead(512), main(1536), before(1536), ahead(1536),
                  pl.BlockSpec((3, 512), lambda s: (0, 0))],
        out_specs=[pl.BlockSpec((3, 1, N_RES, TI, 512), lambda s: (1, s // n_i, 0, s % n_i, 0)),
                   pl.BlockSpec((8, 512), lambda s: (0, 0))],
        out_shape=[jax.ShapeDtypeStruct((6, t // HALF, N_RES, SEG, 512), BF16), jax.ShapeDtypeStruct((8, 512), F32)],
        semantics=("arbitrary",), vmem_mb=40, after=after,
    )(_x4(dy), _x4(dy), _x4(gates), _x4(gates), _x4(gates), cw)
    return dproj.reshape(6, t, 512), gcw


def _xattn_fwd(x1, g, w_q, kv, w_o, *, tb):
    t, d = x1.shape
    hd = d // N_MEM_HEADS
    m = kv.shape[0]

    def body(x_ref, g_ref, wq_ref, k_ref, v_ref, wo_ref, x2_ref, h_ref, q_ref, o_ref):
        xv = x_ref[...]
        h = _rms_fwd(xv, g_ref[...])[0].astype(BF16)
        h_ref[...] = h
        q = _dot(h, wq_ref[...]).astype(BF16)
        q_ref[...] = q
        for hh in range(N_MEM_HEADS):
            sl = slice(hh * hd, (hh + 1) * hd)
            s = _dot_nt(q[:, sl], k_ref[:, sl]) * (1.0 / 16.0)
            e = jnp.exp(s - jnp.max(s, axis=1, keepdims=True))
            p = e / jnp.sum(e, axis=1, keepdims=True)
            o_ref[:, sl] = _dot(p.astype(BF16), v_ref[:, sl]).astype(BF16)
        x2_ref[...] = xv + _dot(o_ref[...], wo_ref[...])

    tok = pl.BlockSpec((tb, d), lambda i: (i, 0))
    full = pl.BlockSpec((d, d), lambda i: (0, 0))
    return pl.pallas_call(
        body, name="xattn_fwd", grid=(t // tb,),
        in_specs=[tok, pl.BlockSpec((1, d), lambda i: (0, 0)), full,
                  pl.BlockSpec((m, d), lambda i: (0, 0)), pl.BlockSpec((m, d), lambda i: (0, 1)), full],
        out_specs=[tok] * 4,
        out_shape=[jax.ShapeDtypeStruct((t, d), F32)] + [jax.ShapeDtypeStruct((t, d), BF16)] * 3,
        compiler_params=_params(("parallel",), 48),
    )(x1, g, w_q, kv, kv, w_o)


def _xattn_bwd(dx2, x1, g, q, w_q, kv, w_o, *, tb):
    t, d = x1.shape
    hd = d // N_MEM_HEADS
    m = kv.shape[0]

    def body(dx2_ref, x_ref, g_ref, q_ref, wq_ref, k_ref, v_ref, wo_ref,
             dx1_ref, dq_ref, dk_ref, dv_ref, gg_ref):
        i = pl.program_id(0)

        @pl.when(i == 0)
        def _():
            dk_ref[...] = jnp.zeros_like(dk_ref)
            dv_ref[...] = jnp.zeros_like(dv_ref)

        dx2 = dx2_ref[...]
        do = _dot_nt(dx2.astype(BF16), wo_ref[...]).astype(BF16)
        for hh in range(N_MEM_HEADS):
            sl = slice(hh * hd, (hh + 1) * hd)
            qh, kh, vh, doh = q_ref[:, sl], k_ref[:, sl], v_ref[:, sl], do[:, sl]
            s = _dot_nt(qh, kh) * (1.0 / 16.0)
            e = jnp.exp(s - jnp.max(s, axis=1, keepdims=True))
            p = e / jnp.sum(e, axis=1, keepdims=True)
            dp = _dot_nt(doh, vh)
            ds = (p * (dp - jnp.sum(dp * p, axis=1, keepdims=True)) * (1.0 / 16.0)).astype(BF16)
            dq_ref[:, sl] = _dot(ds, kh).astype(BF16)
            dk_ref[:, sl] += _dot_tn(ds, qh)
            dv_ref[:, sl] += _dot_tn(p.astype(BF16), doh)
        dh = _dot_nt(dq_ref[...], wq_ref[...])
        g_v = g_ref[...]
        _, xh, r = _rms_fwd(x_ref[...], g_v)
        dx1 = dx2 + _rms_bwd(dh, xh, r, g_v)
        dx1_ref[...] = dx1
        part = jnp.sum(dh * xh, axis=0, keepdims=True)

        @pl.when(i == 0)
        def _():
            gg_ref[...] = part

        @pl.when(i != 0)
        def _():
            gg_ref[...] += part

    tok = pl.BlockSpec((tb, d), lambda i: (i, 0))
    full = pl.BlockSpec((d, d), lambda i: (0, 0))
    acc = pl.BlockSpec((m, d), lambda i: (0, 0))
    res, _ = _pcall(
        body, name="xattn_bwd", grid=(t // tb,),
        in_specs=[tok, tok, pl.BlockSpec((1, d), lambda i: (0, 0)), tok, full,
                  pl.BlockSpec((m, d), lambda i: (0, 0)), pl.BlockSpec((m, d), lambda i: (0, 1)), full],
        out_specs=[tok, tok, acc, acc, pl.BlockSpec((1, d), lambda i: (0, 0))],
        out_shape=[jax.ShapeDtypeStruct((t, d), F32), jax.ShapeDtypeStruct((t, d), BF16),
                   jax.ShapeDtypeStruct((m, d), F32), jax.ShapeDtypeStruct((m, d), F32),
                   jax.ShapeDtypeStruct((1, d), F32)],
        semantics=("arbitrary",), vmem_mb=48,
    )(dx2, x1, g, q, w_q, kv, kv, w_o)
    return res


def _mem_bwd(dk, dv, w_kv, mem, mem_n, g):
    m, d = mem.shape

    def body(dk_ref, dv_ref, w_ref, x_ref, h_ref, g_ref, gw_ref, gg_ref):
        h = h_ref[...]
        dh = jnp.zeros((m, d), F32)
        for i, dy_ref in enumerate((dk_ref, dv_ref)):
            cols = slice(i * d, (i + 1) * d)
            dy = dy_ref[...].astype(BF16)
            gw_ref[:, cols] = _dot_tn(h, dy).astype(BF16)
            dh = dh + _dot_nt(dy, w_ref[:, cols])
        xh = _rms_fwd(x_ref[...], g_ref[...])[1]
        gg_ref[...] = jnp.sum(dh * xh, axis=0, keepdims=True)

    return pl.pallas_call(
        body, name="mem_bwd",
        out_shape=[jax.ShapeDtypeStruct(w_kv.shape, BF16), jax.ShapeDtypeStruct((1, d), F32)],
        compiler_params=pltpu.CompilerParams(vmem_limit_bytes=32 << 20),
    )(dk, dv, w_kv, mem, mem_n, g)


def _mlp_down_loss(a, w_down, x2, tgt, g, *, tb):
    t, d = x2.shape
    f = a.shape[1]

    def body(a_ref, w_ref, x_ref, t_ref, g_ref, dx_ref, dxb_ref, loss_ref, gg_ref):
        i = pl.program_id(0)
        av = a_ref[...].astype(F32)
        x3 = x_ref[...] + _dot((av * av).astype(BF16), w_ref[...])
        g_v = g_ref[...]
        out, xh, r = _rms_fwd(x3, g_v)
        err = out - t_ref[...]
        dout = err * (1.0 / d)
        dx = _rms_bwd(dout, xh, r, g_v)
        dx_ref[...] = dx
        dxb_ref[...] = dx.astype(BF16)
        part = jnp.sum(dout * xh, axis=0, keepdims=True)
        lpart = 0.5 * jnp.sum(jnp.mean(err * err, axis=-1, keepdims=True), axis=0, keepdims=True)
        lpart = jnp.broadcast_to(lpart, loss_ref.shape)

        @pl.when(i == 0)
        def _():
            gg_ref[...] = part
            loss_ref[...] = lpart

        @pl.when(i != 0)
        def _():
            gg_ref[...] += part
            loss_ref[...] += lpart

    tok = pl.BlockSpec((tb, d), lambda i: (i, 0))
    return pl.pallas_call(
        body, name="mlp_down_loss", grid=(t // tb,),
        in_specs=[pl.BlockSpec((tb, f), lambda i: (i, 0)), pl.BlockSpec((f, d), lambda i: (0, 0)), tok, tok,
                  pl.BlockSpec((1, d), lambda i: (0, 0))],
        out_specs=[tok, tok, pl.BlockSpec((8, 128), lambda i: (0, 0)), pl.BlockSpec((1, d), lambda i: (0, 0))],
        out_shape=[jax.ShapeDtypeStruct((t, d), F32), jax.ShapeDtypeStruct((t, d), BF16),
                   jax.ShapeDtypeStruct((8, 128), F32), jax.ShapeDtypeStruct((1, d), F32)],
        compiler_params=_params(("arbitrary",), 56),
    )(a, w_down, x2, tgt, g)


def _mlp_dpre(dx3, w_down, a, *, tb, bn):
    t, d = dx3.shape
    f = a.shape[1]

    def body(dx_ref, w_ref, a_ref, o_ref):
        o_ref[...] = (2.0 * a_ref[...].astype(F32) * _dot_nt(dx_ref[...], w_ref[...])).astype(BF16)

    return pl.pallas_call(
        body, name="mlp_dpre", grid=(t // tb, f // bn),
        in_specs=[pl.BlockSpec((tb, d), lambda i, j: (i, 0)), pl.BlockSpec((bn, d), lambda i, j: (j, 0)),
                  pl.BlockSpec((tb, bn), lambda i, j: (i, j))],
        out_specs=pl.BlockSpec((tb, bn), lambda i, j: (i, j)),
        out_shape=jax.ShapeDtypeStruct((t, f), BF16),
        compiler_params=_params(("parallel", "arbitrary"), 48),
    )(dx3, w_down, a)


def _adamw(gsum, w, m, v):
    m_new = ADAM_B1 * m + (1.0 - ADAM_B1) * gsum
    v_new = ADAM_B2 * v + (1.0 - ADAM_B2) * (gsum * gsum)
    m_hat = m_new / (1.0 - ADAM_B1 ** ADAM_STEP)
    v_hat = v_new / (1.0 - ADAM_B2 ** ADAM_STEP)
    delta = -ADAM_LR * (m_hat / (jnp.sqrt(v_hat) + ADAM_EPS) + ADAM_WD * w)
    return delta, m_new, v_new


def _sum_adamw(shards, *, name, tr):
    r, c = shards[0][1].shape
    n = len(shards)

    def body(*refs):
        for s in range(n):
            p_ref, w_ref, m_ref, v_ref = refs[4 * s:4 * s + 4]
            g_ref, d_ref, mo_ref, vo_ref = refs[4 * (n + s):4 * (n + s) + 4]
            g = p_ref[0].astype(F32)
            for k in range(1, N_DEV):
                g = g + p_ref[k].astype(F32)
            g_ref[...] = g
            d_ref[...], mo_ref[...], vo_ref[...] = _adamw(g, w_ref[...], m_ref[...], v_ref[...])

    blk = pl.BlockSpec((tr, c), lambda i: (i, 0))
    res = pl.pallas_call(
        body, name=name, grid=(r // tr,),
        in_specs=[pl.BlockSpec((N_DEV, tr, c), lambda i: (0, i, 0)), blk, blk, blk] * n,
        out_specs=[blk] * (4 * n), out_shape=[jax.ShapeDtypeStruct((r, c), F32)] * (4 * n),
        compiler_params=_params(("parallel",), 40),
    )(*[pltpu.with_memory_space_constraint(a, pltpu.HBM) for shard in shards for a in shard])
    return [res[4 * s:4 * s + 4] for s in range(n)]


_GAIN_ROWS = ("g_mix", "g_xattn", "g_mem", "g_mlp", "g_final")
PAIR_ROW = 5
LOSS_ROW = 6
TAPS_ROW = 8
SMALL_ROWS = 16
_SMALL = _GAIN_ROWS + ("g_attn_out", "g_conv_out", "conv_w")
CONV_SHARD = 512 // N_DEV


def _pack_small(gains, gg_attn, gg_conv, gcw, loss_blk):
    def body(*refs):
        o_ref = refs[-1]
        ga_ref, gc_ref, cw_ref, l_ref = refs[len(gains):-1]
        o_ref[...] = jnp.zeros_like(o_ref)
        for i, g_ref in enumerate(refs[:len(gains)]):
            o_ref[i:i + 1, :] = g_ref[...]
        o_ref[PAIR_ROW:PAIR_ROW + 1, 0:512] = ga_ref[...]
        o_ref[PAIR_ROW:PAIR_ROW + 1, 512:1024] = gc_ref[...]
        o_ref[LOSS_ROW:LOSS_ROW + 1, 0:BLK] = l_ref[0:1, :]
        o_ref[TAPS_ROW:SMALL_ROWS, 0:512] = cw_ref[...]

    return pl.pallas_call(body, name="pack_small", out_shape=jax.ShapeDtypeStruct((SMALL_ROWS, 1024), F32))(
        *gains, gg_attn, gg_conv, gcw, loss_blk)


def _update_small(parts, me, w, m, v):
    n = len(_SMALL)

    def body(me_ref, p_ref, *refs):
        ins, loss_ref, outs = refs[:3 * n], refs[3 * n], refs[3 * n + 1:]

        def total(lo, hi):
            s = p_ref[0, lo:hi, :]
            for k in range(1, N_DEV):
                s = s + p_ref[k, lo:hi, :]
            return s

        grads = {k: total(i, i + 1) for i, k in enumerate(_GAIN_ROWS)}
        both = total(PAIR_ROW, PAIR_ROW + 1)
        grads["g_attn_out"], grads["g_conv_out"] = both[:, 0:512], both[:, 512:1024]
        taps = total(TAPS_ROW, SMALL_ROWS)
        mine = jnp.zeros((SMALL_ROWS - TAPS_ROW, BLK), F32)
        for j in range(N_DEV):
            lo = j * CONV_SHARD // BLK * BLK
            blk = taps[:, lo:lo + BLK]
            if j * CONV_SHARD != lo:
                blk = pltpu.roll(blk, BLK - (j * CONV_SHARD - lo), axis=1)
            mine = jnp.where(me_ref[0] == j, blk, mine)
        grads["conv_w"] = mine[0:3, 0:CONV_SHARD]
        loss_ref[...] = total(LOSS_ROW, LOSS_ROW + 1)[:, 0:1]
        for i, k in enumerate(_SMALL):
            g_ref, d_ref, mo_ref, vo_ref = outs[4 * i:4 * i + 4]
            g_ref[...] = grads[k]
            d_ref[...], mo_ref[...], vo_ref[...] = _adamw(grads[k], ins[i][...], ins[n + i][...], ins[2 * n + i][...])

    vmem = pl.BlockSpec(memory_space=pltpu.VMEM)
    args = [d[k] for d in (w, m, v) for k in _SMALL]
    res = pl.pallas_call(
        body, name="update_small",
        in_specs=[pl.BlockSpec(memory_space=pltpu.SMEM)] + [vmem] * (1 + 3 * n),
        out_shape=[jax.ShapeDtypeStruct((1, 1), F32)] + [jax.ShapeDtypeStruct(w[k].shape, F32) for k in _SMALL
                                                         for _ in range(4)],
    )(me, parts, *args)
    return res[0], {k: res[1 + 4 * i:5 + 4 * i] for i, k in enumerate(_SMALL)}


def _head_sum_matrix():
    r = lax.broadcasted_iota(jnp.int32, (512, 512), 0) // HEAD_DIM
    c = lax.broadcasted_iota(jnp.int32, (512, 512), 1) // HEAD_DIM
    return (r == c).astype(BF16)


_SHARD_AXIS = dict(w_in=1, w_out=0, w_q=0, w_kv=1, w_o=0, w_up=1, w_down=0, conv_w=None, small=None)


class _Weights:
    def __init__(self, full, shards=None):
        self.full = dict(full)
        self.shards = shards

    def rider(self, names, late=False):
        if self.shards is None:
            return None
        return _Gather([self.shards[n] for n in names], [_SHARD_AXIS[n] for n in names], late)

    def arrived(self, names, gathered):
        if gathered is not None:
            for n, g in zip(names, gathered):
                self.full[n] = g.transpose(1, 0, 2).reshape(g.shape[1], -1) if n == "conv_w" else g

    def __getitem__(self, name):
        return self.full[name]


class _Grads:
    def __init__(self, distributed):
        self.distributed = distributed
        self.local = {}
        self.pending = {}

    def add(self, name, g):
        self.local[name] = g

    def send(self, *names):
        if not self.distributed:
            return []
        rider = _Exchange([self.local[n] for n in names], [_SHARD_AXIS[n] for n in names])
        started = _exchange_start(rider, "send_" + "_".join(names))
        self.pending[names[0]] = (names, rider, started)
        return [started[3]]

    def wait(self, first_name, after):
        names, rider, started = self.pending.pop(first_name)
        return _exchange_wait(rider, started, after, "wait_" + "_".join(names))


def _ride(fn, *args, rider=None, **kw):
    if rider is None:
        return fn(*args, **kw), None
    return fn(*args, rider=rider, **kw)


def _local_step(x, mem, tgt, gains, weights, grads):
    names = ["w_in", "conv_w"]
    (x, tgt), got = _ride(_reorder, [x, tgt], "reorder_in", rider=weights.rider(names, late=True))
    weights.arrived(names, got)
    w_in, cw = weights["w_in"], weights["conv_w"]

    names = ["w_out", "w_kv"]
    (qkv, gates, h1), got = _ride(_proj, x, gains["g_mix"], w_in, tb=1024, rider=weights.rider(names))
    weights.arrived(names, got)
    names = ["w_q", "w_o", "w_up"]
    (attn, *lses), got = _ride(_attention_fwd, qkv, rider=weights.rider(names))
    weights.arrived(names, got)
    x1, merged = _mixer_fwd(x, attn, gates, cw, gains["g_attn_out"], gains["g_conv_out"], weights["w_out"])
    kv, mem_n = _norm_matmul(mem, gains["g_mem"], weights["w_kv"], name="mem_kv", out_dtype=BF16, tb=mem.shape[0],
                             bn=1024, save_h=True)
    x2, h2, qm, om = _xattn_fwd(x1, gains["g_xattn"], weights["w_q"], kv, weights["w_o"], tb=512)
    w_up = weights["w_up"]
    (a, h3), got = _ride(_norm_matmul, x2, gains["g_mlp"], w_up, name="mlp_up", out_dtype=BF16, tb=1024, bn=2048,
                         relu=True, save_h=True, rider=weights.rider(["w_down"], late=True))
    weights.arrived(["w_down"], got)
    w_down = weights["w_down"]
    dx3, dx3b, loss_blk, gg_final = _mlp_down_loss(a, w_down, x2, tgt, gains["g_final"], tb=512)

    dpre = _mlp_dpre(dx3b, w_down, a, tb=1024, bn=2048)
    grads.add("w_down", _matmul_tn(a, dx3b, name="grad_w_down", bm=512, bn=1024, square_a=True))
    sent = grads.send("w_down")
    grads.add("w_up", _matmul_tn(h3, dpre, name="grad_w_up", bm=1024, bn=1024, after=sent))
    sent = grads.send("w_up")
    dx2, dx2b, gg_mlp = _matmul_nt_normbwd(dpre, w_up, x2, gains["g_mlp"], dx3, name="mlp_dx", tb=512,
                                           also_bf16=True, after=sent)

    grads.add("w_o", _matmul_tn(om, dx2b, name="grad_w_o", bm=512, bn=512))
    dx1, dqm, dk, dv, gg_xattn = _xattn_bwd(dx2, x1, gains["g_xattn"], qm, weights["w_q"], kv, weights["w_o"], tb=512)
    grads.add("w_q", _matmul_tn(h2, dqm, name="grad_w_q", bm=1024, bn=512))
    gw_kv, gg_mem = _mem_bwd(dk, dv, weights["w_kv"], mem, mem_n, gains["g_mem"])
    grads.add("w_kv", gw_kv)

    dattn, dsum, dy, gg_attn, gg_conv, gw_out = _mixer_bwd(dx1, merged, attn, gates, cw, gains["g_attn_out"],
                                                           gains["g_conv_out"], weights["w_out"], _head_sum_matrix())
    grads.add("w_out", gw_out)
    sent = grads.send("w_o", "w_q", "w_kv", "w_out")
    dproj, gcw = _conv_bwd(dy, gates, cw, after=sent)
    dproj = _attention_bwd(qkv, dattn, dsum, lses, dproj)
    grads.add("w_in", _matmul_tn(h1, dproj, name="grad_w_in", bm=1024, bn=512))
    sent = grads.send("w_in")
    grad_x, gg_mix = _matmul_nt_normbwd(dproj, w_in, x, gains["g_mix"], dx1, name="mixer_dx", tb=512,
                                        to_natural=True, after=sent)

    grads.add("small", _pack_small([gg_mix, gg_xattn, gg_mem, gg_mlp, gg_final], gg_attn, gg_conv, gcw, loss_blk))
    return grad_x


_BIG = ("w_in", "w_out", "w_q", "w_kv", "w_o", "w_up", "w_down")


def kernel(x, mem, g_mix, w_in, conv_w, g_attn_out, g_conv_out, w_out, g_xattn, g_mem, w_q_mem, w_kv_mem, w_o_mem, g_mlp, w_up, w_down, g_final, loss_target, m_g_mix, m_w_in, m_conv_w, m_g_attn_out, m_g_conv_out, m_w_out, m_g_xattn, m_g_mem, m_w_q_mem, m_w_kv_mem, m_w_o_mem, m_g_mlp, m_w_up, m_w_down, m_g_final, v_g_mix, v_w_in, v_conv_w, v_g_attn_out, v_g_conv_out, v_w_out, v_g_xattn, v_g_mem, v_w_q_mem, v_w_kv_mem, v_w_o_mem, v_g_mlp, v_w_up, v_w_down, v_g_final):
    d = x.shape[-1]
    me = 4 * lax.axis_index("x") + 2 * lax.axis_index("y") + lax.axis_index("c")
    w_shards = dict(w_in=w_in, w_out=w_out, w_q=w_q_mem, w_kv=w_kv_mem, w_o=w_o_mem, w_up=w_up, w_down=w_down)
    m_shards = dict(w_in=m_w_in, w_out=m_w_out, w_q=m_w_q_mem, w_kv=m_w_kv_mem, w_o=m_w_o_mem, w_up=m_w_up,
                    w_down=m_w_down)
    v_shards = dict(w_in=v_w_in, w_out=v_w_out, w_q=v_w_q_mem, w_kv=v_w_kv_mem, w_o=v_w_o_mem, w_up=v_w_up,
                    w_down=v_w_down)
    gains = dict(g_mix=g_mix, g_attn_out=g_attn_out, g_conv_out=g_conv_out, g_xattn=g_xattn, g_mem=g_mem,
                 g_mlp=g_mlp, g_final=g_final)
    gains2 = {k: v.reshape(1, -1) for k, v in gains.items()}

    shards = {k: w_shards[k].astype(BF16) for k in _BIG}
    shards["conv_w"] = conv_w
    grads = _Grads(distributed=True)
    grad_x = _local_step(x[0], mem[0], loss_target[0], gains2, _Weights({}, shards), grads)

    after = grads.send("small")
    outs = {}
    tiles = dict(w_in=256, w_out=32, w_q=32, w_kv=256, w_o=32, w_up=256, w_down=256)
    for group in (("w_down",), ("w_up",), ("w_o", "w_q", "w_kv", "w_out"), ("w_in",)):
        received = dict(zip(group, grads.wait(group[0], after)))
        same_shape = {}
        for k in group:
            same_shape.setdefault(w_shards[k].shape, []).append(k)
        for names in same_shape.values():
            res = _sum_adamw([(received[k], w_shards[k], m_shards[k], v_shards[k]) for k in names],
                             name="adamw_" + "_".join(names), tr=tiles[names[0]])
            outs.update(zip(names, res))
            after = [res[-1][0]]
    small_received, = grads.wait("small", after)

    m_small = dict(g_mix=m_g_mix, g_attn_out=m_g_attn_out, g_conv_out=m_g_conv_out, g_xattn=m_g_xattn,
                   g_mem=m_g_mem, g_mlp=m_g_mlp, g_final=m_g_final)
    v_small = dict(g_mix=v_g_mix, g_attn_out=v_g_attn_out, g_conv_out=v_g_conv_out, g_xattn=v_g_xattn,
                   g_mem=v_g_mem, g_mlp=v_g_mlp, g_final=v_g_final)
    as_rows = lambda vals, conv: dict({k: a.reshape(1, -1) for k, a in vals.items()}, conv_w=conv)
    loss, small_out = _update_small(small_received, me.reshape(1), as_rows(gains, conv_w),
                                    as_rows(m_small, m_conv_w), as_rows(v_small, v_conv_w))
    small_out = {k: [a.reshape(dict(gains, conv_w=conv_w)[k].shape) for a in res] for k, res in small_out.items()}
    names = {"g_mix": "g_mix", "w_in": "w_in", "conv_w": "conv_w", "g_attn_out": "g_attn_out",
             "g_conv_out": "g_conv_out", "w_out": "w_out", "g_xattn": "g_xattn", "g_mem": "g_mem",
             "w_q_mem": "w_q", "w_kv_mem": "w_kv", "w_o_mem": "w_o", "g_mlp": "g_mlp", "w_up": "w_up",
             "w_down": "w_down", "g_final": "g_final"}
    result = [loss.reshape(()), grad_x[None]]
    for which in range(4):
        for key in names.values():
            result.append(outs[key][which] if key in outs else small_out[key][which])
    return tuple(result)
```

```python
import math

import jax
import jax.numpy as jnp
from jax import lax
from jax.experimental import pallas as pl
from jax.experimental.pallas import tpu as pltpu

F32 = jnp.float32
BF16 = jnp.bfloat16
NORM_EPS = 1e-6
NEG_INF = -1e30
N_DEV = 8
BLK = 128
HEAD_DIM = 64
N_MEM_HEADS = 4
ADAM_LR = 0.001
ADAM_B1 = 0.9
ADAM_B2 = 0.999
ADAM_EPS = 1e-08
ADAM_WD = 0.01
ADAM_STEP = 10
MESH = pl.DeviceIdType.MESH
ANY = pl.BlockSpec(memory_space=pl.ANY)


def _dot(a, b):
    return jnp.dot(a, b, preferred_element_type=F32)


def _dot_nt(a, b):
    return lax.dot_general(a, b, (((1,), (1,)), ((), ())), preferred_element_type=F32)


def _dot_tn(a, b):
    return lax.dot_general(a, b, (((0,), (0,)), ((), ())), preferred_element_type=F32)


def _params(semantics, vmem_mb):
    return pltpu.CompilerParams(dimension_semantics=semantics, vmem_limit_bytes=vmem_mb << 20)


def _rms_fwd(x, g):
    r = lax.rsqrt(jnp.mean(x * x, axis=-1, keepdims=True) + NORM_EPS)
    xh = x * r
    return xh * g, xh, r


def _rms_bwd(dy, xh, r, g):
    gy = dy * g
    return r * (gy - xh * jnp.mean(xh * gy, axis=-1, keepdims=True))


def _position():
    x, y, c = lax.axis_index("x"), lax.axis_index("y"), lax.axis_index("c")
    return x, y, c


def _block_of(ref, j, axis, shard_shape):
    r, c = shard_shape
    if axis is None:
        return ref.at[j]
    if axis == 0:
        return ref.at[pl.ds(j * r, r), :]
    return ref.at[:, pl.ds(j * c, c)]


class _Gather:
    has_mid = True
    alias_pairs = ()

    def __init__(self, shards, axes, late=False):
        self.arrays = list(shards)
        self.axes = list(axes)
        self.late = late
        self.n = len(self.arrays)

    def out_shape(self):
        res = []
        for s, axis in zip(self.arrays, self.axes):
            r, c = s.shape
            shape = (N_DEV, r, c) if axis is None else (N_DEV * r, c) if axis == 0 else (r, N_DEV * c)
            res.append(jax.ShapeDtypeStruct(shape, s.dtype))
        return res

    def scratch(self):
        return [pltpu.SemaphoreType.DMA((self.n, 7)), pltpu.SemaphoreType.DMA((self.n, 7)),
                pltpu.SemaphoreType.DMA((self.n,))]

    def _ctx(self, ins, outs, sems):
        send_sems, recv_sems, local_sems = sems
        x, y, c = _position()
        me, sibling = (x, y, c), (x, y, 1 - c)
        chips = [(1 - x, y), (x, 1 - y), (1 - x, 1 - y)]

        def lin(px, py, pc):
            return 4 * px + 2 * py + pc

        def place(a, block):
            return _block_of(outs[a], lin(*block), self.axes[a], self.arrays[a].shape)

        def copy(a, k, block, to, src=None):
            dst = place(a, block)
            return pltpu.make_async_remote_copy(
                src_ref=dst if src is None else src, dst_ref=dst,
                send_sem=send_sems.at[a, k], recv_sem=recv_sems.at[a, k],
                device_id=to, device_id_type=MESH)

        def mine():
            return [pltpu.make_async_copy(ins[a], place(a, me), local_sems.at[a]) for a in range(self.n)]

        def first():
            res = []
            for a in range(self.n):
                res.append(copy(a, 0, me, sibling, src=ins[a]))
                res += [copy(a, 1 + j, me, (*chip, c), src=ins[a]) for j, chip in enumerate(chips)]
            return res

        return c, me, sibling, chips, copy, mine, first

    def start(self, ins, outs, sems):
        _, _, _, _, _, mine, first = self._ctx(ins, outs, sems)
        for cp in mine() + first():
            cp.start()

    def mid(self, ins, outs, sems):
        c, me, sibling, chips, copy, _, _ = self._ctx(ins, outs, sems)
        for j, chip in enumerate(chips):
            for a in range(self.n):
                copy(a, 1 + j, (*chip, c), me).wait_recv()
                copy(a, 4 + j, (*chip, c), sibling).start()

    def finish(self, ins, outs, sems):
        c, me, sibling, chips, copy, mine, first = self._ctx(ins, outs, sems)
        for a in range(self.n):
            copy(a, 0, sibling, me).wait_recv()
            for j, chip in enumerate(chips):
                copy(a, 4 + j, (*chip, 1 - c), me).wait_recv()
        for cp in first():
            cp.wait_send()
        for j, chip in enumerate(chips):
            for a in range(self.n):
                copy(a, 4 + j, (*chip, c), sibling).wait_send()
        for cp in mine():
            cp.wait()


class _Exchange:
    def __init__(self, parts, axes):
        self.n = len(parts)
        self.axes = list(axes)
        self.arrays = list(parts)

    def _piece(self, a):
        r, c = self.arrays[a].shape
        axis = self.axes[a]
        return (r, c) if axis is None else (r // N_DEV, c) if axis == 0 else (r, c // N_DEV)

    def out_shape(self):
        return [jax.ShapeDtypeStruct((N_DEV,) + self._piece(a), self.arrays[a].dtype) for a in range(self.n)]

    def semaphores(self):
        return [pltpu.SemaphoreType.DMA((7 * self.n,)), pltpu.SemaphoreType.DMA((7 * self.n,)),
                pltpu.SemaphoreType.DMA((self.n,))]

    def _ctx(self, ins, outs, sems):
        send_sems, recv_sems, local_sems = sems
        x, y, c = _position()
        me = 4 * x + 2 * y + c

        def src(a, j):
            return ins[a] if self.axes[a] is None else _block_of(ins[a], j, self.axes[a], self._piece(a))

        def dst(a, j):
            return outs[a].at[j]

        def local():
            return [pltpu.make_async_copy(src(a, me), dst(a, me), local_sems.at[a]) for a in range(self.n)]

        def remote(inbound):
            res = []
            for a in range(self.n):
                for k in range(1, N_DEV):
                    peer = (1 - x if k & 4 else x, 1 - y if k & 2 else y, 1 - c if k & 1 else c)
                    plin = 4 * peer[0] + 2 * peer[1] + peer[2]
                    res.append(pltpu.make_async_remote_copy(
                        src_ref=src(a, plin), dst_ref=dst(a, plin if inbound else me),
                        send_sem=send_sems.at[7 * a + k - 1], recv_sem=recv_sems.at[7 * a + k - 1],
                        device_id=peer, device_id_type=MESH))
            return res

        return local, remote

    def start(self, ins, outs, sems):
        local, remote = self._ctx(ins, outs, sems)
        for cp in local() + remote(False):
            cp.start()

    def finish(self, ins, outs, sems):
        local, remote = self._ctx(ins, outs, sems)
        for cp in remote(True):
            cp.wait_recv()
        for cp in remote(False):
            cp.wait_send()
        for cp in local():
            cp.wait()


def _exchange_start(rider, name):
    n = rider.n
    parts = rider.arrays
    lands = [lax.empty(s.shape, s.dtype) for s in rider.out_shape()]
    hbm = pl.BlockSpec(memory_space=pltpu.HBM)
    sem = pl.BlockSpec(memory_space=pltpu.SEMAPHORE)

    def body(*refs):
        ins, sems = refs[:n], refs[2 * n:2 * n + 3]
        outs, token = refs[2 * n + 3 + n:2 * n + 3 + 2 * n], refs[-1]
        rider.start(ins, outs, sems)
        token[...] = jnp.zeros_like(token)

    res = pl.pallas_call(
        body, name=name,
        out_shape=rider.semaphores() + [pltpu.HBM(p.shape, p.dtype) for p in parts]
                  + [pltpu.HBM(z.shape, z.dtype) for z in lands] + [jax.ShapeDtypeStruct((8, 128), F32)],
        in_specs=[hbm] * (2 * n), out_specs=[sem] * 3 + [hbm] * (2 * n) + [pl.BlockSpec(memory_space=pltpu.VMEM)],
        input_output_aliases={i: 3 + i for i in range(2 * n)},
        compiler_params=pltpu.CompilerParams(has_side_effects=pltpu.SideEffectType.DATAFLOW_SIDE_EFFECTING),
    )(*[pltpu.with_memory_space_constraint(a, pltpu.HBM) for a in parts + lands])
    return res[:3], res[3:3 + n], res[3 + n:3 + 2 * n], res[-1]


def _exchange_wait(rider, started, after, name):
    n = rider.n
    sems, parts, lands, _ = started
    hbm = pl.BlockSpec(memory_space=pltpu.HBM)
    sem = pl.BlockSpec(memory_space=pltpu.SEMAPHORE)

    def body(*refs):
        rider.finish(refs[:n], refs[n:2 * n], refs[2 * n:2 * n + 3])

    res = pl.pallas_call(
        body, name=name, out_shape=[pltpu.HBM(a.shape, a.dtype) for a in list(parts) + list(lands)],
        in_specs=[hbm] * (2 * n) + [sem] * 3 + [ANY] * len(after), out_specs=[hbm] * (2 * n),
        input_output_aliases={i: i for i in range(2 * n)},
        compiler_params=pltpu.CompilerParams(has_side_effects=pltpu.SideEffectType.DATAFLOW_SIDE_EFFECTING),
    )(*parts, *lands, *sems, *after)
    return list(res[n:])


def _pcall(body, *, name, grid, in_specs, out_specs, out_shape, scratch_shapes=(), semantics, vmem_mb, rider=None,
           aliases=None, after=()):
    in_specs, out_specs, out_shape = list(in_specs), list(out_specs), list(out_shape)
    scratch_shapes = list(scratch_shapes)
    aliases = dict(aliases or {})
    if rider is None:
        n_in, after = len(in_specs), list(after)

        def plain(*refs):
            body(*refs[:n_in], *refs[n_in + len(after):])

        call = pl.pallas_call(plain if after else body, name=name, grid=grid, in_specs=in_specs + [ANY] * len(after),
                              out_specs=out_specs, out_shape=out_shape, scratch_shapes=scratch_shapes,
                              input_output_aliases=aliases, compiler_params=_params(semantics, vmem_mb))
        return lambda *args: (list(call(*args, *after)), None)
    n_in, n_out, n_scr = len(in_specs), len(out_specs), len(scratch_shapes)
    r_in, r_shapes = len(rider.arrays), rider.out_shape()
    r_out = len(r_shapes)
    aliases.update({n_in + i: n_out + o for i, o in rider.alias_pairs})
    total = math.prod(grid)
    mid_step = total - 1 if rider.has_mid and rider.late else (3 * total) // 4

    def wrapped(*refs):
        bounds = [0, n_in, r_in, n_out, r_out, n_scr]
        for i in range(1, len(bounds)):
            bounds[i] += bounds[i - 1]
        a, ra, o, ro, s = (refs[bounds[i]:bounds[i + 1]] for i in range(5))
        rs = refs[bounds[5]:]
        step = pl.program_id(0)
        for k in range(1, len(grid)):
            step = step * grid[k] + pl.program_id(k)
        pl.when(step == 0)(lambda: rider.start(ra, ro, rs))
        body(*a, *o, *s)
        if rider.has_mid:
            pl.when(step == mid_step)(lambda: rider.mid(ra, ro, rs))
        pl.when(step == total - 1)(lambda: rider.finish(ra, ro, rs))

    call = pl.pallas_call(
        wrapped, name=name, grid=grid, in_specs=in_specs + [ANY] * r_in, out_specs=out_specs + [ANY] * r_out,
        out_shape=out_shape + r_shapes, scratch_shapes=scratch_shapes + rider.scratch(),
        input_output_aliases=aliases, compiler_params=_params(("arbitrary",) * len(grid), vmem_mb))

    def run(*args):
        res = call(*args, *rider.arrays)
        return list(res[:n_out]), list(res[n_out:])

    return run


def _norm_matmul(x, g, w, *, name, out_dtype, tb, bn, relu=False, save_h=False, rider=None):
    t, d = x.shape
    n = w.shape[1]

    def body(x_ref, g_ref, w_ref, o_ref, *rest):
        h_scr = rest[-1]

        @pl.when(pl.program_id(1) == 0)
        def _():
            h = _rms_fwd(x_ref[...], g_ref[...])[0].astype(BF16)
            h_scr[...] = h
            if save_h:
                rest[0][...] = h

        acc = _dot(h_scr[...], w_ref[...])
        if relu:
            acc = jnp.maximum(acc, 0.0)
        o_ref[...] = acc.astype(out_dtype)

    out_shape = [jax.ShapeDtypeStruct((t, n), out_dtype)]
    out_specs = [pl.BlockSpec((tb, bn), lambda i, j: (i, j))]
    if save_h:
        out_shape.append(jax.ShapeDtypeStruct((t, d), BF16))
        out_specs.append(pl.BlockSpec((tb, d), lambda i, j: (i, 0)))
    res, extra = _pcall(
        body, name=name, grid=(t // tb, n // bn),
        in_specs=[pl.BlockSpec((tb, d), lambda i, j: (i, 0)),
                  pl.BlockSpec((1, d), lambda i, j: (0, 0)),
                  pl.BlockSpec((d, bn), lambda i, j: (0, j))],
        out_specs=out_specs, out_shape=out_shape,
        scratch_shapes=[pltpu.VMEM((tb, d), BF16)],
        semantics=("parallel", "arbitrary"), vmem_mb=48, rider=rider,
    )(x, g, w)
    res = res if save_h else res[0]
    return res if rider is None else (res, extra)


def _proj(x, g, w, *, tb, rider=None):
    t, d = x.shape
    half = w.shape[1] // 2

    def body(x_ref, g_ref, w_ref, qkv_ref, gates_ref, h_ref, h_scr):
        j = pl.program_id(1)

        @pl.when(j == 0)
        def _():
            h = _rms_fwd(x_ref[...], g_ref[...])[0].astype(BF16)
            h_scr[...] = h
            h_ref[...] = h

        acc = _dot(h_scr[...], w_ref[...])

        @pl.when(j == 0)
        def _():
            qkv_ref[...] = acc

        @pl.when(j == 1)
        def _():
            gates_ref[...] = acc.astype(BF16)

    tok = lambda c: pl.BlockSpec((tb, c), lambda i, j: (i, 0))
    res, extra = _pcall(
        body, name="proj", grid=(t // tb, 2),
        in_specs=[tok(d), pl.BlockSpec((1, d), lambda i, j: (0, 0)), pl.BlockSpec((d, half), lambda i, j: (0, j))],
        out_specs=[tok(half), tok(half), tok(d)],
        out_shape=[jax.ShapeDtypeStruct((t, half), F32), jax.ShapeDtypeStruct((t, half), BF16),
                   jax.ShapeDtypeStruct((t, d), BF16)],
        scratch_shapes=[pltpu.VMEM((tb, d), BF16)],
        semantics=("parallel", "arbitrary"), vmem_mb=48, rider=rider,
    )(x, g, w)
    return res if rider is None else (res, extra)


def _matmul_nt_normbwd(dy, w, x, g, dres, *, name, tb, also_bf16=False, to_natural=False, after=()):
    t, d = x.shape
    stacked = dy.ndim == 3
    n_i = SEG // TI
    if to_natural:
        tb = N_RES * TI

    def body(dy_ref, w_ref, x_ref, g_ref, dres_ref, *rest):
        rest = list(rest)
        dx_ref = rest.pop(0)
        dxb_ref = rest.pop(0) if also_bf16 else None
        gg_ref = rest.pop(0)
        i = pl.program_id(0)

        def rows(ref, *lead):
            v = ref[lead] if lead else ref[...]
            return v[0].reshape(tb, v.shape[-1]) if to_natural else v

        if stacked:
            kb = dy_ref.shape[-1]
            dh = _dot_nt(rows(dy_ref, 0), w_ref[:, 0:kb])
            for s in range(1, dy_ref.shape[0]):
                dh = dh + _dot_nt(rows(dy_ref, s), w_ref[:, s * kb:(s + 1) * kb])
        else:
            dh = _dot_nt(rows(dy_ref), w_ref[...])
        g_v = g_ref[...]
        _, xh, r = _rms_fwd(rows(x_ref), g_v)
        dx = _rms_bwd(dh, xh, r, g_v) + rows(dres_ref)
        if to_natural:
            scr = rest.pop(0)
            for cb in range(d // BLK):
                cols = slice(cb * BLK, (cb + 1) * BLK)
                slab = scr.at[cb]
                for res in range(N_RES):
                    slab[pl.ds(res, TI, stride=N_RES), :] = dx[res * TI:(res + 1) * TI, cols]
                dx_ref[:, cols] = slab[...]
        else:
            dx_ref[...] = dx
        if also_bf16:
            dxb_ref[...] = dx.astype(BF16)
        part = jnp.sum(dh * xh, axis=0, keepdims=True)

        @pl.when(i == 0)
        def _():
            gg_ref[...] = part

        @pl.when(i != 0)
        def _():
            gg_ref[...] += part

    tok = pl.BlockSpec((tb, d), lambda i: (i, 0))
    row = pl.BlockSpec((1, d), lambda i: (0, 0))
    if to_natural:
        act = pl.BlockSpec((1, N_RES, TI, d), lambda i: (i // n_i, 0, i % n_i, 0))
        dy_spec = pl.BlockSpec((dy.shape[0], 1, N_RES, TI, dy.shape[2]), lambda i: (0, i // n_i, 0, i % n_i, 0))
        dy, x, dres = dy.reshape(dy.shape[0], t // HALF, N_RES, SEG, dy.shape[2]), _x4(x), _x4(dres)
    elif stacked:
        act, dy_spec = tok, pl.BlockSpec((dy.shape[0], tb, dy.shape[2]), lambda i: (0, i, 0))
    else:
        act, dy_spec = tok, pl.BlockSpec((tb, dy.shape[1]), lambda i: (i, 0))
    in_specs = [dy_spec, pl.BlockSpec(w.shape, lambda i: (0, 0)), act, row, act]
    out_specs = [tok] + ([tok] if also_bf16 else []) + [row]
    out_shape = ([jax.ShapeDtypeStruct((t, d), F32)] + ([jax.ShapeDtypeStruct((t, d), BF16)] if also_bf16 else [])
                 + [jax.ShapeDtypeStruct((1, d), F32)])
    res, _ = _pcall(
        body, name=name, grid=(t // tb,), in_specs=in_specs, out_specs=out_specs, out_shape=out_shape,
        scratch_shapes=[pltpu.VMEM((d // BLK, tb, BLK), F32)] if to_natural else [],
        semantics=("arbitrary",), vmem_mb=56, after=after,
    )(dy, w, x, g, dres)
    return res


def _matmul_tn(a, b, *, name, bm, bn, square_a=False, after=()):
    t, m = a.shape
    stacked = b.ndim == 3
    n = b.shape[0] * bn if stacked else b.shape[1]

    def body(a_ref, b_ref, o_ref):
        av = a_ref[...]
        if square_a:
            av = av.astype(F32)
            av = (av * av).astype(BF16)
        o_ref[...] = _dot_tn(av, b_ref[...]).astype(BF16)

    res, _ = _pcall(
        body, name=name, grid=(m // bm, n // bn),
        in_specs=[pl.BlockSpec((t, bm), lambda i, j: (0, i)),
                  pl.BlockSpec((None, t, bn), lambda i, j: (j, 0, 0)) if stacked
                  else pl.BlockSpec((t, bn), lambda i, j: (0, j))],
        out_specs=[pl.BlockSpec((bm, bn), lambda i, j: (i, j))], out_shape=[jax.ShapeDtypeStruct((m, n), BF16)],
        semantics=("parallel", "parallel"), vmem_mb=56, after=after,
    )(a, b)
    return res[0]


N_RES = 16
SEG = 128
HALF = N_RES * SEG
TI = 32
HALO = 16


def _x4(a):
    return a.reshape(a.shape[0] // HALF, N_RES, SEG, a.shape[1])


def _reorder(arrays, name, rider=None):
    t, c = arrays[0].shape
    n = len(arrays)
    n_i = SEG // TI

    def body(*refs):
        scr = refs[-1]
        for i_ref, o_ref in zip(refs[:n], refs[n:2 * n]):
            for cb in range(c // BLK):
                cols = slice(cb * BLK, (cb + 1) * BLK)
                slab = scr.at[cb]
                slab[...] = i_ref[:, cols]
                for r in range(N_RES):
                    o_ref[0, r, :, cols] = slab[pl.ds(r, TI, stride=N_RES), :]

    res, extra = _pcall(
        body, name=name, grid=(t // (TI * N_RES),),
        in_specs=[pl.BlockSpec((TI * N_RES, c), lambda s: (s, 0))] * n,
        out_specs=[pl.BlockSpec((1, N_RES, TI, c), lambda s: (s // n_i, 0, s % n_i, 0))] * n,
        out_shape=[jax.ShapeDtypeStruct((t // HALF, N_RES, SEG, c), F32)] * n,
        scratch_shapes=[pltpu.VMEM((c // BLK, TI * N_RES, BLK), F32)],
        semantics=("parallel",), vmem_mb=32, rider=rider,
    )(*arrays)
    res = [r.reshape(t, c) for r in res]
    return res if rider is None else (res, extra)


_PATTERNS = ((1, 16, 8, SEG), (4, 4, 32, 4 * SEG), (16, 1, SEG, 0))
_FIRST = {1: 1, 4: 4, 16: 16}


def _group_rows(d, g):
    a = g >> 4
    if d == 16:
        base = a * HALF + (g & 15) * SEG
        prev = base - HALF
    elif d == 4:
        c = (g >> 2) & 3
        base = a * HALF + (g & 3) * SEG + c * 32
        prev = jnp.where(c > 0, base - 32, base - HALF + 96)
    else:
        c = g & 15
        base = a * HALF + c * 8
        prev = jnp.where(c > 0, base - 8, base - HALF + 120)
    return base, prev


def _load_rows(ref, base, n, rows, stride):
    parts = [ref[pl.ds(pl.multiple_of(base + j * stride, 8), rows), :] for j in range(n)]
    return parts[0] if n == 1 else jnp.concatenate(parts, axis=0)


def _store_rows(ref, base, val, n, rows, stride, add=False):
    for j in range(n):
        sl = pl.ds(pl.multiple_of(base + j * stride, 8), rows)
        piece = val[j * rows:(j + 1) * rows, :]
        if add:
            ref[sl, :] += piece
        else:
            ref[sl, :] = piece


def _band_bias(n, rows):
    shift = rows.bit_length() - 1
    lq = lax.broadcasted_iota(jnp.int32, (BLK, BLK), 0)
    lk = lax.broadcasted_iota(jnp.int32, (BLK, BLK), 1)
    iq = (lq & (rows - 1)) * n + (lq >> shift)
    ik = (lk & (rows - 1)) * n + (lk >> shift)
    zero = jnp.zeros((BLK, BLK), F32)
    return jnp.where(ik >= iq, zero, NEG_INF), jnp.where(ik <= iq, zero, NEG_INF)


def _set_bias(bias_scr, n, rows):
    prev_b, cur_b = _band_bias(n, rows)
    for half in range(2):
        bias_scr[half * BLK:(half + 1) * BLK, 0:BLK] = prev_b
        bias_scr[half * BLK:(half + 1) * BLK, BLK:2 * BLK] = cur_b


SCALE = 1.0 / math.sqrt(HEAD_DIM)


def _head_consts(value=1.0):
    lane_lo = lax.broadcasted_iota(jnp.int32, (BLK, BLK), 1) < HEAD_DIM
    return lane_lo, [jnp.where(lane_lo, value, 0.0).astype(BF16), jnp.where(lane_lo, 0.0, value).astype(BF16)]


def _stack_heads(v, head_mask):
    return jnp.concatenate([v * head_mask[0], v * head_mask[1]], axis=0)


def _unstack_heads(v2, lane_lo):
    return jnp.where(lane_lo, v2[:BLK], v2[BLK:])


def _rows_per_head(v, lane_lo):
    rolled = pltpu.roll(v, HEAD_DIM, axis=1)
    return jnp.concatenate([jnp.where(lane_lo, v, rolled), jnp.where(lane_lo, rolled, v)], axis=0)


WIDTH = 4


def _loop(lo, hi, fn, width=None):
    if width is None:
        def body(g, carry):
            fn(g)
            return carry

        if hi > lo:
            lax.fori_loop(lo, hi, body, 0)
        return
    while hi > lo:
        trips = (hi - lo) // width
        if trips:
            def body(i, carry, lo=lo, width=width):
                fn([lo + width * i + j for j in range(width)])
                return carry

            lax.fori_loop(0, trips, body, 0)
            lo += trips * width
        width = max(1, width // 2)


def _mix_weights(l1, l2, l3):
    mx = jnp.maximum(jnp.maximum(l1, l2), l3)
    e1, e2, e3 = jnp.exp(l1 - mx), jnp.exp(l2 - mx), jnp.exp(l3 - mx)
    inv = 1.0 / (e1 + e2 + e3)
    return e1 * inv, e2 * inv, e3 * inv


def _attention_fwd(qkv, rider=None):
    t = qkv.shape[0]
    groups = 16 * (t // HALF)

    def body(q_ref, k_ref, v_ref, attn_ref, l1_ref, l2_ref, l3_ref, o_scr, bias_scr):
        lane_lo, q_mask = _head_consts(SCALE)
        l_refs = (l1_ref, l2_ref, l3_ref)
        for p, (d, n, rows, stride) in enumerate(_PATTERNS):
            _set_bias(bias_scr, n, rows)
            o_p, l_p = o_scr.at[p], l_refs[p]

            def block(gs, has_prev):
                at = [_group_rows(d, g) for g in gs]

                def load(ref, b):
                    return _load_rows(ref, b, n, rows, stride).astype(BF16)

                q2 = [_stack_heads(load(q_ref, b), q_mask) for b, _ in at]
                k2 = [load(k_ref, b) for b, _ in at]
                v2 = [load(v_ref, b) for b, _ in at]
                if has_prev:
                    k2 = [jnp.concatenate([load(k_ref, pv), k], axis=0) for (_, pv), k in zip(at, k2)]
                    v2 = [jnp.concatenate([load(v_ref, pv), v], axis=0) for (_, pv), v in zip(at, v2)]
                s = [_dot_nt(q, k) for q, k in zip(q2, k2)]
                s = [x + (bias_scr[...] if has_prev else bias_scr[:, BLK:2 * BLK]) for x in s]
                mx = [jnp.max(x, axis=1, keepdims=True) for x in s]
                e = [jnp.exp(x - m) for x, m in zip(s, mx)]
                den = [jnp.sum(x, axis=1, keepdims=True) for x in e]
                o2 = [_dot(x.astype(BF16), v) * (1.0 / dn) for x, v, dn in zip(e, v2, den)]
                lse2 = [jnp.broadcast_to(m + jnp.log(dn), (2 * BLK, BLK)) for m, dn in zip(mx, den)]
                for (b, _), o, l in zip(at, o2, lse2):
                    _store_rows(o_p, b, _unstack_heads(o, lane_lo), n, rows, stride)
                    _store_rows(l_p, b, _unstack_heads(l, lane_lo), n, rows, stride)

            _loop(0, _FIRST[d], lambda gs: block(gs, False), width=2 * WIDTH)
            _loop(_FIRST[d], groups, lambda gs: block(gs, True), width=2 * WIDTH)

        def mix(i):
            sl = pl.ds(pl.multiple_of(i * 256, 256), 256)
            w = _mix_weights(l1_ref[sl, :], l2_ref[sl, :], l3_ref[sl, :])
            attn_ref[sl, :] = w[0] * o_scr[0, sl, :] + w[1] * o_scr[1, sl, :] + w[2] * o_scr[2, sl, :]

        _loop(0, t // 256, mix)

    def col(c0):
        return pl.BlockSpec((t, BLK), lambda hp: (0, c0 + hp))

    res, extra = _pcall(
        body, name="attention_fwd", grid=(4,), in_specs=[col(0), col(4), col(8)], out_specs=[col(0)] * 4,
        out_shape=[jax.ShapeDtypeStruct((t, 512), F32)] * 4,
        scratch_shapes=[pltpu.VMEM((3, t, BLK), F32), pltpu.VMEM((2 * BLK, 2 * BLK), F32)],
        semantics=("parallel",), vmem_mb=48, rider=rider,
    )(qkv, qkv, qkv)
    return res if rider is None else (res, extra)


def _attention_bwd(qkv, dattn, dsum, lses, dproj):
    t = qkv.shape[0]
    groups = 16 * (t // HALF)

    def body(q_ref, k_ref, v_ref, da_ref, ds_ref, l1_ref, l2_ref, l3_ref, kept_ref, out_ref, acc, bias_scr):
        del kept_ref
        lane_lo, head_mask = _head_consts()
        q_mask = _head_consts(SCALE)[1]
        l_refs = (l1_ref, l2_ref, l3_ref)

        def clear(i):
            sl = pl.ds(pl.multiple_of(i * 512, 512), 512)
            for s in range(3):
                acc[s, sl, :] = jnp.zeros((512, BLK), F32)

        _loop(0, t // 512, clear)
        dq_acc, dk_acc, dv_acc = acc.at[0], acc.at[1], acc.at[2]
        for p, (d, n, rows, stride) in enumerate(_PATTERNS):
            _set_bias(bias_scr, n, rows)

            def block(gs, has_prev):
                at = [_group_rows(d, g) for g in gs]

                def load(ref, b):
                    return _load_rows(ref, b, n, rows, stride)

                def put(ref, b, val):
                    _store_rows(ref, b, val, n, rows, stride, add=True)

                def wide(x):
                    return jnp.concatenate([x, x], axis=1) if has_prev else x

                lse = [[load(ref, b) for ref in l_refs] for b, _ in at]
                w = [_mix_weights(*ls)[p] for ls in lse]
                do2 = [_stack_heads((wg * load(da_ref, b)).astype(BF16), head_mask) for wg, (b, _) in zip(w, at)]
                dl2 = [wide(_rows_per_head(wg * load(ds_ref, b), lane_lo)) for wg, (b, _) in zip(w, at)]
                lse2 = [wide(_rows_per_head(ls[p], lane_lo)) for ls in lse]
                q2 = [_stack_heads(load(q_ref, b).astype(BF16), q_mask) for b, _ in at]
                k2 = [load(k_ref, b).astype(BF16) for b, _ in at]
                v2 = [load(v_ref, b).astype(BF16) for b, _ in at]
                if has_prev:
                    k2 = [jnp.concatenate([load(k_ref, pv).astype(BF16), k], axis=0) for (_, pv), k in zip(at, k2)]
                    v2 = [jnp.concatenate([load(v_ref, pv).astype(BF16), v], axis=0) for (_, pv), v in zip(at, v2)]
                s = [_dot_nt(q, k) for q, k in zip(q2, k2)]
                dp = [_dot_nt(do, v) for do, v in zip(do2, v2)]
                pr = [jnp.exp(x + (bias_scr[...] if has_prev else bias_scr[:, BLK:2 * BLK]) - l)
                      for x, l in zip(s, lse2)]
                ds = [(pg * (x - dl)).astype(BF16) for pg, x, dl in zip(pr, dp, dl2)]
                dq2 = [_dot(x, k) * SCALE for x, k in zip(ds, k2)]
                dk2 = [_dot_tn(x, q) for x, q in zip(ds, q2)]
                dv2 = [_dot_tn(pg.astype(BF16), do) for pg, do in zip(pr, do2)]
                for (b, pv), dq, dk, dv in zip(at, dq2, dk2, dv2):
                    put(dq_acc, b, _unstack_heads(dq, lane_lo))
                    if has_prev:
                        put(dk_acc, pv, dk[:BLK])
                        put(dv_acc, pv, dv[:BLK])
                        put(dk_acc, b, dk[BLK:])
                        put(dv_acc, b, dv[BLK:])
                    else:
                        put(dk_acc, b, dk)
                        put(dv_acc, b, dv)

            _loop(0, _FIRST[d], lambda gs: block(gs, False), width=WIDTH)
            _loop(_FIRST[d], groups, lambda gs: block(gs, True), width=WIDTH)

        def emit(i):
            sl = pl.ds(pl.multiple_of(i * 512, 512), 512)
            for s in range(3):
                out_ref[s, sl, :] = acc[s, sl, :].astype(BF16)

        _loop(0, t // 512, emit)

    def col(c0):
        return pl.BlockSpec((t, BLK), lambda hp: (0, c0 + hp))

    res, _ = _pcall(
        body, name="attention_bwd", grid=(4,),
        in_specs=[col(0), col(4), col(8)] + [col(0)] * 5 + [ANY],
        out_specs=[pl.BlockSpec((3, t, BLK), lambda hp: (0, 0, hp))],
        out_shape=[jax.ShapeDtypeStruct(dproj.shape, BF16)],
        scratch_shapes=[pltpu.VMEM((3, t, BLK), F32), pltpu.VMEM((2 * BLK, 2 * BLK), F32)],
        semantics=("parallel",), vmem_mb=56, aliases={8: 0},
    )(qkv, qkv, qkv, dattn, dsum, *lses, dproj)
    return res[0]


def _order_specs(t):
    n_i = SEG // TI
    nblk = (t // HALF) * n_i
    per = TI // HALO

    def main(c, col=0):
        return pl.BlockSpec((1, N_RES, TI, c), lambda s: (s // n_i, 0, s % n_i, col))

    def before(c, col=0):
        return pl.BlockSpec((1, 2, HALO, c), lambda s: (jnp.maximum(s - 1, 0) // n_i, N_RES // 2 - 1,
                                                        (jnp.maximum(s - 1, 0) % n_i) * per + per - 1, col))

    def after(c, col=0):
        return pl.BlockSpec((1, 2, HALO, c), lambda s: (jnp.minimum(s + 1, nblk - 1) // n_i, 0,
                                                        (jnp.minimum(s + 1, nblk - 1) % n_i) * per, col))

    return nblk, main, before, after


def _shift_in(v, row_in, up):
    rows = v.shape[0]
    idx = lax.broadcasted_iota(jnp.int32, v.shape, 0)
    fill = jnp.broadcast_to(row_in, v.shape)
    if up:
        return jnp.where(idx == rows - 1, fill, pltpu.roll(v, rows - 1, axis=0))
    return jnp.where(idx == 0, fill, pltpu.roll(v, 1, axis=0))


def _taps_behind(u, before):
    s15 = _shift_in(u[N_RES - 1], before[1, HALO - 1:HALO, :], up=False)
    s14 = _shift_in(u[N_RES - 2], before[0, HALO - 1:HALO, :], up=False)
    m1 = jnp.concatenate([s15[None], u[:N_RES - 1]], axis=0)
    m2 = jnp.concatenate([s14[None], s15[None], u[:N_RES - 2]], axis=0)
    return m1, m2


def _taps_ahead(u, after):
    t0 = _shift_in(u[0], after[0, 0:1, :], up=True)
    t1 = _shift_in(u[1], after[1, 0:1, :], up=True)
    p1 = jnp.concatenate([u[1:], t0[None]], axis=0)
    p2 = jnp.concatenate([u[2:], t0[None], t1[None]], axis=0)
    return p1, p2


def _conv_fwd(gates, before, first, cw):
    gates, before = gates.astype(F32), before.astype(F32)
    bg, cg, xc = gates[..., 0:512], gates[..., 512:1024], gates[..., 1024:1536]
    u = cg * xc
    ub = before[..., 512:1024] * before[..., 1024:1536]
    ub = jnp.where(first, jnp.zeros_like(ub), ub)
    m1, m2 = _taps_behind(u, ub)
    conv = m2 * cw[0:1, :] + m1 * cw[1:2, :] + u * cw[2:3, :]
    return bg, u, m1, m2, conv


def _sum_tokens(v):
    return jnp.sum(jnp.sum(v, axis=0), axis=0, keepdims=True)


def _mixer_fwd(x, attn, gates, cw, g_a, g_c, w_out):
    t, d = x.shape
    nblk, main, before, _ = _order_specs(t)
    rows = N_RES * TI

    def body(x_ref, at_ref, gt_ref, gb_ref, cw_ref, ga_ref, gc_ref, wa_ref, wb_ref, x1_ref, mg_ref):
        an = _rms_fwd(at_ref[0], ga_ref[...])[0].astype(BF16)
        bg, _, _, _, conv = _conv_fwd(gt_ref[0], gb_ref[0], pl.program_id(0) == 0, cw_ref[...])
        cn = _rms_fwd(bg * conv, gc_ref[...])[0].astype(BF16)
        mg_ref[0, :, :, 0:512] = an
        mg_ref[0, :, :, 512:1024] = cn
        y = _dot(an.reshape(rows, 512), wa_ref[...]) + _dot(cn.reshape(rows, 512), wb_ref[...])
        x1_ref[0] = x_ref[0] + y.reshape(N_RES, TI, d)

    const = lambda r, c, i0=0: pl.BlockSpec((r, c), lambda s: (i0, 0))
    x1, merged = pl.pallas_call(
        body, name="mixer_fwd", grid=(nblk,),
        in_specs=[main(d), main(512), main(1536), before(1536), const(3, 512), const(1, 512), const(1, 512),
                  const(512, d), const(512, d, 1)],
        out_specs=[main(d), main(d)],
        out_shape=[jax.ShapeDtypeStruct(_x4(x).shape, F32), jax.ShapeDtypeStruct(_x4(x).shape, BF16)],
        compiler_params=_params(("parallel",), 48),
    )(_x4(x), _x4(attn), _x4(gates), _x4(gates), cw, g_a, g_c, w_out, w_out)
    return x1.reshape(t, d), merged.reshape(t, d)


def _mixer_bwd(dx1, merged, attn, gates, cw, g_a, g_c, w_out, head_sum):
    t, d = dx1.shape
    nblk, main, before, _ = _order_specs(t)
    rows = N_RES * TI

    def body(dx_ref, mg_ref, at_ref, gt_ref, gb_ref, cw_ref, ga_ref, gc_ref, wa_ref, wb_ref, hs_ref,
             da_ref, dsum_ref, dy_ref, gga_ref, ggc_ref, gw_ref, acc_w):
        s = pl.program_id(0)
        dxb = dx_ref[0].reshape(rows, d).astype(BF16)

        @pl.when(s == 0)
        def _():
            acc_w[...] = jnp.zeros_like(acc_w)

        acc_w[...] += _dot_tn(mg_ref[0].reshape(rows, d), dxb)

        @pl.when(s == nblk - 1)
        def _():
            gw_ref[...] = acc_w[...].astype(BF16)

        dma = _dot_nt(dxb, wa_ref[...]).reshape(N_RES, TI, 512)
        dmc = _dot_nt(dxb, wb_ref[...]).reshape(N_RES, TI, 512)
        attn_v, g_av = at_ref[0], ga_ref[...]
        _, ah, ra = _rms_fwd(attn_v, g_av)
        dattn = _rms_bwd(dma, ah, ra, g_av)
        da_ref[0] = dattn
        z = (dattn * attn_v).reshape(rows, 512)
        hs = hs_ref[...]
        z1 = z.astype(BF16)
        z2 = (z - z1.astype(F32)).astype(BF16)
        dsum_ref[0] = (_dot(z1, hs) + _dot(z2, hs)).reshape(N_RES, TI, 512)
        bg, _, _, _, conv = _conv_fwd(gt_ref[0], gb_ref[0], s == 0, cw_ref[...])
        g_cv = gc_ref[...]
        _, yh, rc = _rms_fwd(bg * conv, g_cv)
        dy_ref[0] = _rms_bwd(dmc, yh, rc, g_cv)
        pa, pc = _sum_tokens(dma * ah), _sum_tokens(dmc * yh)

        @pl.when(s == 0)
        def _():
            gga_ref[...] = pa
            ggc_ref[...] = pc

        @pl.when(s != 0)
        def _():
            gga_ref[...] += pa
            ggc_ref[...] += pc

    const = lambda r, c, i0=0: pl.BlockSpec((r, c), lambda s: (i0, 0))
    shape4 = _x4(attn).shape
    res, _ = _pcall(
        body, name="mixer_bwd", grid=(nblk,),
        in_specs=[main(d), main(d), main(512), main(1536), before(1536), const(3, 512), const(1, 512), const(1, 512),
                  const(512, d), const(512, d, 1), const(512, 512)],
        out_specs=[main(512)] * 3 + [const(1, 512), const(1, 512), const(d, d)],
        out_shape=[jax.ShapeDtypeStruct(shape4, F32)] * 3 + [jax.ShapeDtypeStruct((1, 512), F32)] * 2
        + [jax.ShapeDtypeStruct((d, d), BF16)],
        scratch_shapes=[pltpu.VMEM((d, d), F32)],
        semantics=("arbitrary",), vmem_mb=48,
    )(_x4(dx1), _x4(merged), _x4(attn), _x4(gates), _x4(gates), cw, g_a, g_c, w_out, w_out, head_sum)
    return [r.reshape(t, 512) for r in res[:3]] + res[3:]


def _conv_bwd(dy, gates, cw, after=()):
    t = dy.shape[0]
    nblk, main, before, ahead = _order_specs(t)
    n_i = SEG // TI

    def body(dy_ref, dya_ref, gt_ref, gb_ref, ga_ref, cw_ref, dp_ref, gcw_ref):
        s = pl.program_id(0)
        cw_v, gates_v = cw_ref[...], gt_ref[0]
        bg, u, m1, m2, conv = _conv_fwd(gates_v, gb_ref[0], s == 0, cw_v)
        dy_v = dy_ref[0]
        dconv = dy_v * bg
        dca = dya_ref[0] * ga_ref[0][..., 0:512].astype(F32)
        dca = jnp.where(s == nblk - 1, jnp.zeros_like(dca), dca)
        p1, p2 = _taps_ahead(dconv, dca)
        du = dconv * cw_v[2:3, :] + p1 * cw_v[1:2, :] + p2 * cw_v[0:1, :]
        dp_ref[0, 0] = (dy_v * conv).astype(BF16)
        dp_ref[1, 0] = (du * gates_v[..., 1024:1536].astype(F32)).astype(BF16)
        dp_ref[2, 0] = (du * gates_v[..., 512:1024].astype(F32)).astype(BF16)
        parts = [_sum_tokens(dconv * m2), _sum_tokens(dconv * m1), _sum_tokens(dconv * u)]

        @pl.when(s == 0)
        def _():
            gcw_ref[...] = jnp.zeros_like(gcw_ref)

        for tap in range(3):
            gcw_ref[tap:tap + 1, :] += parts[tap]

    (dproj, gcw), _ = _pcall(
        body, name="conv_bwd", grid=(nblk,),
        in_specs=[main(512), ahead(512), main(1536), before(1536), ahead(1536),
                  pl.BlockSpec((3, 512), lambda s: (0, 0))],
        out_specs=[pl.BlockSpec((3, 1, N_RES, TI, 512), lambda s: (1, s // n_i, 0, s % n_i, 0)),
                   pl.BlockSpec((8, 512), lambda s: (0, 0))],
        out_shape=[jax.ShapeDtypeStruct((6, t // HALF, N_RES, SEG, 512), BF16), jax.ShapeDtypeStruct((8, 512), F32)],
        semantics=("arbitrary",), vmem_mb=40, after=after,
    )(_x4(dy), _x4(dy), _x4(gates), _x4(gates), _x4(gates), cw)
    return dproj.reshape(6, t, 512), gcw


def _xattn_fwd(x1, g, w_q, kv, w_o, *, tb):
    t, d = x1.shape
    hd = d // N_MEM_HEADS
    m = kv.shape[0]

    def body(x_ref, g_ref, wq_ref, k_ref, v_ref, wo_ref, x2_ref, h_ref, q_ref, o_ref):
        xv = x_ref[...]
        h = _rms_fwd(xv, g_ref[...])[0].astype(BF16)
        h_ref[...] = h
        q = _dot(h, wq_ref[...]).astype(BF16)
        q_ref[...] = q
        for hh in range(N_MEM_HEADS):
            sl = slice(hh * hd, (hh + 1) * hd)
            s = _dot_nt(q[:, sl], k_ref[:, sl]) * (1.0 / 16.0)
            e = jnp.exp(s - jnp.max(s, axis=1, keepdims=True))
            p = e / jnp.sum(e, axis=1, keepdims=True)
            o_ref[:, sl] = _dot(p.astype(BF16), v_ref[:, sl]).astype(BF16)
        x2_ref[...] = xv + _dot(o_ref[...], wo_ref[...])

    tok = pl.BlockSpec((tb, d), lambda i: (i, 0))
    full = pl.BlockSpec((d, d), lambda i: (0, 0))
    return pl.pallas_call(
        body, name="xattn_fwd", grid=(t // tb,),
        in_specs=[tok, pl.BlockSpec((1, d), lambda i: (0, 0)), full,
                  pl.BlockSpec((m, d), lambda i: (0, 0)), pl.BlockSpec((m, d), lambda i: (0, 1)), full],
        out_specs=[tok] * 4,
        out_shape=[jax.ShapeDtypeStruct((t, d), F32)] + [jax.ShapeDtypeStruct((t, d), BF16)] * 3,
        compiler_params=_params(("parallel",), 48),
    )(x1, g, w_q, kv, kv, w_o)


def _xattn_bwd(dx2, x1, g, q, w_q, kv, w_o, *, tb):
    t, d = x1.shape
    hd = d // N_MEM_HEADS
    m = kv.shape[0]

    def body(dx2_ref, x_ref, g_ref, q_ref, wq_ref, k_ref, v_ref, wo_ref,
             dx1_ref, dq_ref, dk_ref, dv_ref, gg_ref):
        i = pl.program_id(0)

        @pl.when(i == 0)
        def _():
            dk_ref[...] = jnp.zeros_like(dk_ref)
            dv_ref[...] = jnp.zeros_like(dv_ref)

        dx2 = dx2_ref[...]
        do = _dot_nt(dx2.astype(BF16), wo_ref[...]).astype(BF16)
        for hh in range(N_MEM_HEADS):
            sl = slice(hh * hd, (hh + 1) * hd)
            qh, kh, vh, doh = q_ref[:, sl], k_ref[:, sl], v_ref[:, sl], do[:, sl]
            s = _dot_nt(qh, kh) * (1.0 / 16.0)
            e = jnp.exp(s - jnp.max(s, axis=1, keepdims=True))
            p = e / jnp.sum(e, axis=1, keepdims=True)
            dp = _dot_nt(doh, vh)
            ds = (p * (dp - jnp.sum(dp * p, axis=1, keepdims=True)) * (1.0 / 16.0)).astype(BF16)
            dq_ref[:, sl] = _dot(ds, kh).astype(BF16)
            dk_ref[:, sl] += _dot_tn(ds, qh)
            dv_ref[:, sl] += _dot_tn(p.astype(BF16), doh)
        dh = _dot_nt(dq_ref[...], wq_ref[...])
        g_v = g_ref[...]
        _, xh, r = _rms_fwd(x_ref[...], g_v)
        dx1 = dx2 + _rms_bwd(dh, xh, r, g_v)
        dx1_ref[...] = dx1
        part = jnp.sum(dh * xh, axis=0, keepdims=True)

        @pl.when(i == 0)
        def _():
            gg_ref[...] = part

        @pl.when(i != 0)
        def _():
            gg_ref[...] += part

    tok = pl.BlockSpec((tb, d), lambda i: (i, 0))
    full = pl.BlockSpec((d, d), lambda i: (0, 0))
    acc = pl.BlockSpec((m, d), lambda i: (0, 0))
    res, _ = _pcall(
        body, name="xattn_bwd", grid=(t // tb,),
        in_specs=[tok, tok, pl.BlockSpec((1, d), lambda i: (0, 0)), tok, full,
                  pl.BlockSpec((m, d), lambda i: (0, 0)), pl.BlockSpec((m, d), lambda i: (0, 1)), full],
        out_specs=[tok, tok, acc, acc, pl.BlockSpec((1, d), lambda i: (0, 0))],
        out_shape=[jax.ShapeDtypeStruct((t, d), F32), jax.ShapeDtypeStruct((t, d), BF16),
                   jax.ShapeDtypeStruct((m, d), F32), jax.ShapeDtypeStruct((m, d), F32),
                   jax.ShapeDtypeStruct((1, d), F32)],
        semantics=("arbitrary",), vmem_mb=48,
    )(dx2, x1, g, q, w_q, kv, kv, w_o)
    return res


def _mem_bwd(dk, dv, w_kv, mem, mem_n, g):
    m, d = mem.shape

    def body(dk_ref, dv_ref, w_ref, x_ref, h_ref, g_ref, gw_ref, gg_ref):
        h = h_ref[...]
        dh = jnp.zeros((m, d), F32)
        for i, dy_ref in enumerate((dk_ref, dv_ref)):
            cols = slice(i * d, (i + 1) * d)
            dy = dy_ref[...].astype(BF16)
            gw_ref[:, cols] = _dot_tn(h, dy).astype(BF16)
            dh = dh + _dot_nt(dy, w_ref[:, cols])
        xh = _rms_fwd(x_ref[...], g_ref[...])[1]
        gg_ref[...] = jnp.sum(dh * xh, axis=0, keepdims=True)

    return pl.pallas_call(
        body, name="mem_bwd",
        out_shape=[jax.ShapeDtypeStruct(w_kv.shape, BF16), jax.ShapeDtypeStruct((1, d), F32)],
        compiler_params=pltpu.CompilerParams(vmem_limit_bytes=32 << 20),
    )(dk, dv, w_kv, mem, mem_n, g)


def _mlp_down_loss(a, w_down, x2, tgt, g, *, tb):
    t, d = x2.shape
    f = a.shape[1]

    def body(a_ref, w_ref, x_ref, t_ref, g_ref, dx_ref, dxb_ref, loss_ref, gg_ref):
        i = pl.program_id(0)
        av = a_ref[...].astype(F32)
        x3 = x_ref[...] + _dot((av * av).astype(BF16), w_ref[...])
        g_v = g_ref[...]
        out, xh, r = _rms_fwd(x3, g_v)
        err = out - t_ref[...]
        dout = err * (1.0 / d)
        dx = _rms_bwd(dout, xh, r, g_v)
        dx_ref[...] = dx
        dxb_ref[...] = dx.astype(BF16)
        part = jnp.sum(dout * xh, axis=0, keepdims=True)
        lpart = 0.5 * jnp.sum(jnp.mean(err * err, axis=-1, keepdims=True), axis=0, keepdims=True)
        lpart = jnp.broadcast_to(lpart, loss_ref.shape)

        @pl.when(i == 0)
        def _():
            gg_ref[...] = part
            loss_ref[...] = lpart

        @pl.when(i != 0)
        def _():
            gg_ref[...] += part
            loss_ref[...] += lpart

    tok = pl.BlockSpec((tb, d), lambda i: (i, 0))
    return pl.pallas_call(
        body, name="mlp_down_loss", grid=(t // tb,),
        in_specs=[pl.BlockSpec((tb, f), lambda i: (i, 0)), pl.BlockSpec((f, d), lambda i: (0, 0)), tok, tok,
                  pl.BlockSpec((1, d), lambda i: (0, 0))],
        out_specs=[tok, tok, pl.BlockSpec((8, 128), lambda i: (0, 0)), pl.BlockSpec((1, d), lambda i: (0, 0))],
        out_shape=[jax.ShapeDtypeStruct((t, d), F32), jax.ShapeDtypeStruct((t, d), BF16),
                   jax.ShapeDtypeStruct((8, 128), F32), jax.ShapeDtypeStruct((1, d), F32)],
        compiler_params=_params(("arbitrary",), 56),
    )(a, w_down, x2, tgt, g)


def _mlp_dpre(dx3, w_down, a, *, tb, bn):
    t, d = dx3.shape
    f = a.shape[1]

    def body(dx_ref, w_ref, a_ref, o_ref):
        o_ref[...] = (2.0 * a_ref[...].astype(F32) * _dot_nt(dx_ref[...], w_ref[...])).astype(BF16)

    return pl.pallas_call(
        body, name="mlp_dpre", grid=(t // tb, f // bn),
        in_specs=[pl.BlockSpec((tb, d), lambda i, j: (i, 0)), pl.BlockSpec((bn, d), lambda i, j: (j, 0)),
                  pl.BlockSpec((tb, bn), lambda i, j: (i, j))],
        out_specs=pl.BlockSpec((tb, bn), lambda i, j: (i, j)),
        out_shape=jax.ShapeDtypeStruct((t, f), BF16),
        compiler_params=_params(("parallel", "arbitrary"), 48),
    )(dx3, w_down, a)


def _adamw(gsum, w, m, v):
    m_new = ADAM_B1 * m + (1.0 - ADAM_B1) * gsum
    v_new = ADAM_B2 * v + (1.0 - ADAM_B2) * (gsum * gsum)
    m_hat = m_new / (1.0 - ADAM_B1 ** ADAM_STEP)
    v_hat = v_new / (1.0 - ADAM_B2 ** ADAM_STEP)
    delta = -ADAM_LR * (m_hat / (jnp.sqrt(v_hat) + ADAM_EPS) + ADAM_WD * w)
    return delta, m_new, v_new


def _sum_adamw(shards, *, name, tr):
    r, c = shards[0][1].shape
    n = len(shards)

    def body(*refs):
        for s in range(n):
            p_ref, w_ref, m_ref, v_ref = refs[4 * s:4 * s + 4]
            g_ref, d_ref, mo_ref, vo_ref = refs[4 * (n + s):4 * (n + s) + 4]
            g = p_ref[0].astype(F32)
            for k in range(1, N_DEV):
                g = g + p_ref[k].astype(F32)
            g_ref[...] = g
            d_ref[...], mo_ref[...], vo_ref[...] = _adamw(g, w_ref[...], m_ref[...], v_ref[...])

    blk = pl.BlockSpec((tr, c), lambda i: (i, 0))
    res = pl.pallas_call(
        body, name=name, grid=(r // tr,),
        in_specs=[pl.BlockSpec((N_DEV, tr, c), lambda i: (0, i, 0)), blk, blk, blk] * n,
        out_specs=[blk] * (4 * n), out_shape=[jax.ShapeDtypeStruct((r, c), F32)] * (4 * n),
        compiler_params=_params(("parallel",), 40),
    )(*[pltpu.with_memory_space_constraint(a, pltpu.HBM) for shard in shards for a in shard])
    return [res[4 * s:4 * s + 4] for s in range(n)]


_GAIN_ROWS = ("g_mix", "g_xattn", "g_mem", "g_mlp", "g_final")
PAIR_ROW = 5
LOSS_ROW = 6
TAPS_ROW = 8
SMALL_ROWS = 16
_SMALL = _GAIN_ROWS + ("g_attn_out", "g_conv_out", "conv_w")
CONV_SHARD = 512 // N_DEV


def _pack_small(gains, gg_attn, gg_conv, gcw, loss_blk):
    def body(*refs):
        o_ref = refs[-1]
        ga_ref, gc_ref, cw_ref, l_ref = refs[len(gains):-1]
        o_ref[...] = jnp.zeros_like(o_ref)
        for i, g_ref in enumerate(refs[:len(gains)]):
            o_ref[i:i + 1, :] = g_ref[...]
        o_ref[PAIR_ROW:PAIR_ROW + 1, 0:512] = ga_ref[...]
        o_ref[PAIR_ROW:PAIR_ROW + 1, 512:1024] = gc_ref[...]
        o_ref[LOSS_ROW:LOSS_ROW + 1, 0:BLK] = l_ref[0:1, :]
        o_ref[TAPS_ROW:SMALL_ROWS, 0:512] = cw_ref[...]

    return pl.pallas_call(body, name="pack_small", out_shape=jax.ShapeDtypeStruct((SMALL_ROWS, 1024), F32))(
        *gains, gg_attn, gg_conv, gcw, loss_blk)


def _update_small(parts, me, w, m, v):
    n = len(_SMALL)

    def body(me_ref, p_ref, *refs):
        ins, loss_ref, outs = refs[:3 * n], refs[3 * n], refs[3 * n + 1:]

        def total(lo, hi):
            s = p_ref[0, lo:hi, :]
            for k in range(1, N_DEV):
                s = s + p_ref[k, lo:hi, :]
            return s

        grads = {k: total(i, i + 1) for i, k in enumerate(_GAIN_ROWS)}
        both = total(PAIR_ROW, PAIR_ROW + 1)
        grads["g_attn_out"], grads["g_conv_out"] = both[:, 0:512], both[:, 512:1024]
        taps = total(TAPS_ROW, SMALL_ROWS)
        mine = jnp.zeros((SMALL_ROWS - TAPS_ROW, BLK), F32)
        for j in range(N_DEV):
            lo = j * CONV_SHARD // BLK * BLK
            blk = taps[:, lo:lo + BLK]
            if j * CONV_SHARD != lo:
                blk = pltpu.roll(blk, BLK - (j * CONV_SHARD - lo), axis=1)
            mine = jnp.where(me_ref[0] == j, blk, mine)
        grads["conv_w"] = mine[0:3, 0:CONV_SHARD]
        loss_ref[...] = total(LOSS_ROW, LOSS_ROW + 1)[:, 0:1]
        for i, k in enumerate(_SMALL):
            g_ref, d_ref, mo_ref, vo_ref = outs[4 * i:4 * i + 4]
            g_ref[...] = grads[k]
            d_ref[...], mo_ref[...], vo_ref[...] = _adamw(grads[k], ins[i][...], ins[n + i][...], ins[2 * n + i][...])

    vmem = pl.BlockSpec(memory_space=pltpu.VMEM)
    args = [d[k] for d in (w, m, v) for k in _SMALL]
    res = pl.pallas_call(
        body, name="update_small",
        in_specs=[pl.BlockSpec(memory_space=pltpu.SMEM)] + [vmem] * (1 + 3 * n),
        out_shape=[jax.ShapeDtypeStruct((1, 1), F32)] + [jax.ShapeDtypeStruct(w[k].shape, F32) for k in _SMALL
                                                         for _ in range(4)],
    )(me, parts, *args)
    return res[0], {k: res[1 + 4 * i:5 + 4 * i] for i, k in enumerate(_SMALL)}


def _head_sum_matrix():
    r = lax.broadcasted_iota(jnp.int32, (512, 512), 0) // HEAD_DIM
    c = lax.broadcasted_iota(jnp.int32, (512, 512), 1) // HEAD_DIM
    return (r == c).astype(BF16)


_SHARD_AXIS = dict(w_in=1, w_out=0, w_q=0, w_kv=1, w_o=0, w_up=1, w_down=0, conv_w=None, small=None)


class _Weights:
    def __init__(self, full, shards=None):
        self.full = dict(full)
        self.shards = shards

    def rider(self, names, late=False):
        if self.shards is None:
            return None
        return _Gather([self.shards[n] for n in names], [_SHARD_AXIS[n] for n in names], late)

    def arrived(self, names, gathered):
        if gathered is not None:
            for n, g in zip(names, gathered):
                self.full[n] = g.transpose(1, 0, 2).reshape(g.shape[1], -1) if n == "conv_w" else g

    def __getitem__(self, name):
        return self.full[name]


class _Grads:
    def __init__(self, distributed):
        self.distributed = distributed
        self.local = {}
        self.pending = {}

    def add(self, name, g):
        self.local[name] = g

    def send(self, *names):
        if not self.distributed:
            return []
        rider = _Exchange([self.local[n] for n in names], [_SHARD_AXIS[n] for n in names])
        started = _exchange_start(rider, "send_" + "_".join(names))
        self.pending[names[0]] = (names, rider, started)
        return [started[3]]

    def wait(self, first_name, after):
        names, rider, started = self.pending.pop(first_name)
        return _exchange_wait(rider, started, after, "wait_" + "_".join(names))


def _ride(fn, *args, rider=None, **kw):
    if rider is None:
        return fn(*args, **kw), None
    return fn(*args, rider=rider, **kw)


def _local_step(x, mem, tgt, gains, weights, grads):
    names = ["w_in", "conv_w"]
    (x, tgt), got = _ride(_reorder, [x, tgt], "reorder_in", rider=weights.rider(names, late=True))
    weights.arrived(names, got)
    w_in, cw = weights["w_in"], weights["conv_w"]

    names = ["w_out", "w_kv"]
    (qkv, gates, h1), got = _ride(_proj, x, gains["g_mix"], w_in, tb=1024, rider=weights.rider(names))
    weights.arrived(names, got)
    names = ["w_q", "w_o", "w_up"]
    (attn, *lses), got = _ride(_attention_fwd, qkv, rider=weights.rider(names))
    weights.arrived(names, got)
    x1, merged = _mixer_fwd(x, attn, gates, cw, gains["g_attn_out"], gains["g_conv_out"], weights["w_out"])
    kv, mem_n = _norm_matmul(mem, gains["g_mem"], weights["w_kv"], name="mem_kv", out_dtype=BF16, tb=mem.shape[0],
                             bn=1024, save_h=True)
    x2, h2, qm, om = _xattn_fwd(x1, gains["g_xattn"], weights["w_q"], kv, weights["w_o"], tb=512)
    w_up = weights["w_up"]
    (a, h3), got = _ride(_norm_matmul, x2, gains["g_mlp"], w_up, name="mlp_up", out_dtype=BF16, tb=1024, bn=2048,
                         relu=True, save_h=True, rider=weights.rider(["w_down"], late=True))
    weights.arrived(["w_down"], got)
    w_down = weights["w_down"]
    dx3, dx3b, loss_blk, gg_final = _mlp_down_loss(a, w_down, x2, tgt, gains["g_final"], tb=512)

    dpre = _mlp_dpre(dx3b, w_down, a, tb=1024, bn=2048)
    grads.add("w_down", _matmul_tn(a, dx3b, name="grad_w_down", bm=512, bn=1024, square_a=True))
    sent = grads.send("w_down")
    grads.add("w_up", _matmul_tn(h3, dpre, name="grad_w_up", bm=1024, bn=1024, after=sent))
    sent = grads.send("w_up")
    dx2, dx2b, gg_mlp = _matmul_nt_normbwd(dpre, w_up, x2, gains["g_mlp"], dx3, name="mlp_dx", tb=512,
                                           also_bf16=True, after=sent)

    grads.add("w_o", _matmul_tn(om, dx2b, name="grad_w_o", bm=512, bn=512))
    dx1, dqm, dk, dv, gg_xattn = _xattn_bwd(dx2, x1, gains["g_xattn"], qm, weights["w_q"], kv, weights["w_o"], tb=512)
    grads.add("w_q", _matmul_tn(h2, dqm, name="grad_w_q", bm=1024, bn=512))
    gw_kv, gg_mem = _mem_bwd(dk, dv, weights["w_kv"], mem, mem_n, gains["g_mem"])
    grads.add("w_kv", gw_kv)

    dattn, dsum, dy, gg_attn, gg_conv, gw_out = _mixer_bwd(dx1, merged, attn, gates, cw, gains["g_attn_out"],
                                                           gains["g_conv_out"], weights["w_out"], _head_sum_matrix())
    grads.add("w_out", gw_out)
    sent = grads.send("w_o", "w_q", "w_kv", "w_out")
    dproj, gcw = _conv_bwd(dy, gates, cw, after=sent)
    dproj = _attention_bwd(qkv, dattn, dsum, lses, dproj)
    grads.add("w_in", _matmul_tn(h1, dproj, name="grad_w_in", bm=1024, bn=512))
    sent = grads.send("w_in")
    grad_x, gg_mix = _matmul_nt_normbwd(dproj, w_in, x, gains["g_mix"], dx1, name="mixer_dx", tb=512,
                                        to_natural=True, after=sent)

    grads.add("small", _pack_small([gg_mix, gg_xattn, gg_mem, gg_mlp, gg_final], gg_attn, gg_conv, gcw, loss_blk))
    return grad_x


_BIG = ("w_in", "w_out", "w_q", "w_kv", "w_o", "w_up", "w_down")


def kernel(x, mem, g_mix, w_in, conv_w, g_attn_out, g_conv_out, w_out, g_xattn, g_mem, w_q_mem, w_kv_mem, w_o_mem, g_mlp, w_up, w_down, g_final, loss_target, m_g_mix, m_w_in, m_conv_w, m_g_attn_out, m_g_conv_out, m_w_out, m_g_xattn, m_g_mem, m_w_q_mem, m_w_kv_mem, m_w_o_mem, m_g_mlp, m_w_up, m_w_down, m_g_final, v_g_mix, v_w_in, v_conv_w, v_g_attn_out, v_g_conv_out, v_w_out, v_g_xattn, v_g_mem, v_w_q_mem, v_w_kv_mem, v_w_o_mem, v_g_mlp, v_w_up, v_w_down, v_g_final):
    d = x.shape[-1]
    me = 4 * lax.axis_index("x") + 2 * lax.axis_index("y") + lax.axis_index("c")
    w_shards = dict(w_in=w_in, w_out=w_out, w_q=w_q_mem, w_kv=w_kv_mem, w_o=w_o_mem, w_up=w_up, w_down=w_down)
    m_shards = dict(w_in=m_w_in, w_out=m_w_out, w_q=m_w_q_mem, w_kv=m_w_kv_mem, w_o=m_w_o_mem, w_up=m_w_up,
                    w_down=m_w_down)
    v_shards = dict(w_in=v_w_in, w_out=v_w_out, w_q=v_w_q_mem, w_kv=v_w_kv_mem, w_o=v_w_o_mem, w_up=v_w_up,
                    w_down=v_w_down)
    gains = dict(g_mix=g_mix, g_attn_out=g_attn_out, g_conv_out=g_conv_out, g_xattn=g_xattn, g_mem=g_mem,
                 g_mlp=g_mlp, g_final=g_final)
    gains2 = {k: v.reshape(1, -1) for k, v in gains.items()}

    shards = {k: w_shards[k].astype(BF16) for k in _BIG}
    shards["conv_w"] = conv_w
    grads = _Grads(distributed=True)
    grad_x = _local_step(x[0], mem[0], loss_target[0], gains2, _Weights({}, shards), grads)

    after = grads.send("small")
    outs = {}
    tiles = dict(w_in=128, w_out=32, w_q=32, w_kv=128, w_o=32, w_up=128, w_down=128)
    for group in (("w_down",), ("w_up",), ("w_o", "w_q", "w_kv", "w_out"), ("w_in",)):
        received = dict(zip(group, grads.wait(group[0], after)))
        same_shape = {}
        for k in group:
            same_shape.setdefault(w_shards[k].shape, []).append(k)
        for names in same_shape.values():
            res = _sum_adamw([(received[k], w_shards[k], m_shards[k], v_shards[k]) for k in names],
                             name="adamw_" + "_".join(names), tr=tiles[names[0]])
            outs.update(zip(names, res))
            after = [res[-1][0]]
    small_received, = grads.wait("small", after)

    m_small = dict(g_mix=m_g_mix, g_attn_out=m_g_attn_out, g_conv_out=m_g_conv_out, g_xattn=m_g_xattn,
                   g_mem=m_g_mem, g_mlp=m_g_mlp, g_final=m_g_final)
    v_small = dict(g_mix=v_g_mix, g_attn_out=v_g_attn_out, g_conv_out=v_g_conv_out, g_xattn=v_g_xattn,
                   g_mem=v_g_mem, g_mlp=v_g_mlp, g_final=v_g_final)
    as_rows = lambda vals, conv: dict({k: a.reshape(1, -1) for k, a in vals.items()}, conv_w=conv)
    loss, small_out = _update_small(small_received, me.reshape(1), as_rows(gains, conv_w),
                                    as_rows(m_small, m_conv_w), as_rows(v_small, v_conv_w))
    small_out = {k: [a.reshape(dict(gains, conv_w=conv_w)[k].shape) for a in res] for k, res in small_out.items()}
    names = {"g_mix": "g_mix", "w_in": "w_in", "conv_w": "conv_w", "g_attn_out": "g_attn_out",
             "g_conv_out": "g_conv_out", "w_out": "w_out", "g_xattn": "g_xattn", "g_mem": "g_mem",
             "w_q_mem": "w_q", "w_kv_mem": "w_kv", "w_o_mem": "w_o", "g_mlp": "g_mlp", "w_up": "w_up",
             "w_down": "w_down", "g_final": "g_final"}
    result = [loss.reshape(()), grad_x[None]]
    for which in range(4):
        for key in names.values():
            result.append(outs[key][which] if key in outs else small_out[key][which])
    return tuple(result)
```

```python
import math

import jax
import jax.numpy as jnp
from jax import lax
from jax.experimental import pallas as pl
from jax.experimental.pallas import tpu as pltpu

F32 = jnp.float32
BF16 = jnp.bfloat16
NORM_EPS = 1e-6
NEG_INF = -1e30
N_DEV = 8
BLK = 128
HEAD_DIM = 64
N_MEM_HEADS = 4
ADAM_LR = 0.001
ADAM_B1 = 0.9
ADAM_B2 = 0.999
ADAM_EPS = 1e-08
ADAM_WD = 0.01
ADAM_STEP = 10
MESH = pl.DeviceIdType.MESH
ANY = pl.BlockSpec(memory_space=pl.ANY)


def _dot(a, b):
    return jnp.dot(a, b, preferred_element_type=F32)


def _dot_nt(a, b):
    return lax.dot_general(a, b, (((1,), (1,)), ((), ())), preferred_element_type=F32)


def _dot_tn(a, b):
    return lax.dot_general(a, b, (((0,), (0,)), ((), ())), preferred_element_type=F32)


def _params(semantics, vmem_mb):
    return pltpu.CompilerParams(dimension_semantics=semantics, vmem_limit_bytes=vmem_mb << 20)


def _rms_fwd(x, g):
    r = lax.rsqrt(jnp.mean(x * x, axis=-1, keepdims=True) + NORM_EPS)
    xh = x * r
    return xh * g, xh, r


def _rms_bwd(dy, xh, r, g):
    gy = dy * g
    return r * (gy - xh * jnp.mean(xh * gy, axis=-1, keepdims=True))


def _position():
    x, y, c = lax.axis_index("x"), lax.axis_index("y"), lax.axis_index("c")
    return x, y, c


def _block_of(ref, j, axis, shard_shape):
    r, c = shard_shape
    if axis is None:
        return ref.at[j]
    if axis == 0:
        return ref.at[pl.ds(j * r, r), :]
    return ref.at[:, pl.ds(j * c, c)]


class _Gather:
    has_mid = True
    alias_pairs = ()

    def __init__(self, shards, axes, late=False):
        self.arrays = list(shards)
        self.axes = list(axes)
        self.late = late
        self.n = len(self.arrays)

    def out_shape(self):
        res = []
        for s, axis in zip(self.arrays, self.axes):
            r, c = s.shape
            shape = (N_DEV, r, c) if axis is None else (N_DEV * r, c) if axis == 0 else (r, N_DEV * c)
            res.append(jax.ShapeDtypeStruct(shape, s.dtype))
        return res

    def scratch(self):
        return [pltpu.SemaphoreType.DMA((self.n, 7)), pltpu.SemaphoreType.DMA((self.n, 7)),
                pltpu.SemaphoreType.DMA((self.n,))]

    def _ctx(self, ins, outs, sems):
        send_sems, recv_sems, local_sems = sems
        x, y, c = _position()
        me, sibling = (x, y, c), (x, y, 1 - c)
        chips = [(1 - x, y), (x, 1 - y), (1 - x, 1 - y)]

        def lin(px, py, pc):
            return 4 * px + 2 * py + pc

        def place(a, block):
            return _block_of(outs[a], lin(*block), self.axes[a], self.arrays[a].shape)

        def copy(a, k, block, to, src=None):
            dst = place(a, block)
            return pltpu.make_async_remote_copy(
                src_ref=dst if src is None else src, dst_ref=dst,
                send_sem=send_sems.at[a, k], recv_sem=recv_sems.at[a, k],
                device_id=to, device_id_type=MESH)

        def mine():
            return [pltpu.make_async_copy(ins[a], place(a, me), local_sems.at[a]) for a in range(self.n)]

        def first():
            res = []
            for a in range(self.n):
                res.append(copy(a, 0, me, sibling, src=ins[a]))
                res += [copy(a, 1 + j, me, (*chip, c), src=ins[a]) for j, chip in enumerate(chips)]
            return res

        return c, me, sibling, chips, copy, mine, first

    def start(self, ins, outs, sems):
        _, _, _, _, _, mine, first = self._ctx(ins, outs, sems)
        for cp in mine() + first():
            cp.start()

    def mid(self, ins, outs, sems):
        c, me, sibling, chips, copy, _, _ = self._ctx(ins, outs, sems)
        for j, chip in enumerate(chips):
            for a in range(self.n):
                copy(a, 1 + j, (*chip, c), me).wait_recv()
                copy(a, 4 + j, (*chip, c), sibling).start()

    def finish(self, ins, outs, sems):
        c, me, sibling, chips, copy, mine, first = self._ctx(ins, outs, sems)
        for a in range(self.n):
            copy(a, 0, sibling, me).wait_recv()
            for j, chip in enumerate(chips):
                copy(a, 4 + j, (*chip, 1 - c), me).wait_recv()
        for cp in first():
            cp.wait_send()
        for j, chip in enumerate(chips):
            for a in range(self.n):
                copy(a, 4 + j, (*chip, c), sibling).wait_send()
        for cp in mine():
            cp.wait()


class _Exchange:
    def __init__(self, parts, axes):
        self.n = len(parts)
        self.axes = list(axes)
        self.arrays = list(parts)

    def _piece(self, a):
        r, c = self.arrays[a].shape
        axis = self.axes[a]
        return (r, c) if axis is None else (r // N_DEV, c) if axis == 0 else (r, c // N_DEV)

    def out_shape(self):
        return [jax.ShapeDtypeStruct((N_DEV,) + self._piece(a), self.arrays[a].dtype) for a in range(self.n)]

    def semaphores(self):
        return [pltpu.SemaphoreType.DMA((7 * self.n,)), pltpu.SemaphoreType.DMA((7 * self.n,)),
                pltpu.SemaphoreType.DMA((self.n,))]

    def _ctx(self, ins, outs, sems):
        send_sems, recv_sems, local_sems = sems
        x, y, c = _position()
        me = 4 * x + 2 * y + c

        def src(a, j):
            return ins[a] if self.axes[a] is None else _block_of(ins[a], j, self.axes[a], self._piece(a))

        def dst(a, j):
            return outs[a].at[j]

        def local():
            return [pltpu.make_async_copy(src(a, me), dst(a, me), local_sems.at[a]) for a in range(self.n)]

        def remote(inbound):
            res = []
            for a in range(self.n):
                for k in range(1, N_DEV):
                    peer = (1 - x if k & 4 else x, 1 - y if k & 2 else y, 1 - c if k & 1 else c)
                    plin = 4 * peer[0] + 2 * peer[1] + peer[2]
                    res.append(pltpu.make_async_remote_copy(
                        src_ref=src(a, plin), dst_ref=dst(a, plin if inbound else me),
                        send_sem=send_sems.at[7 * a + k - 1], recv_sem=recv_sems.at[7 * a + k - 1],
                        device_id=peer, device_id_type=MESH))
            return res

        return local, remote

    def start(self, ins, outs, sems):
        local, remote = self._ctx(ins, outs, sems)
        for cp in local() + remote(False):
            cp.start()

    def finish(self, ins, outs, sems):
        local, remote = self._ctx(ins, outs, sems)
        for cp in remote(True):
            cp.wait_recv()
        for cp in remote(False):
            cp.wait_send()
        for cp in local():
            cp.wait()


def _exchange_start(rider, name):
    n = rider.n
    parts = rider.arrays
    lands = [lax.empty(s.shape, s.dtype) for s in rider.out_shape()]
    hbm = pl.BlockSpec(memory_space=pltpu.HBM)
    sem = pl.BlockSpec(memory_space=pltpu.SEMAPHORE)

    def body(*refs):
        ins, sems = refs[:n], refs[2 * n:2 * n + 3]
        outs, token = refs[2 * n + 3 + n:2 * n + 3 + 2 * n], refs[-1]
        rider.start(ins, outs, sems)
        token[...] = jnp.zeros_like(token)

    res = pl.pallas_call(
        body, name=name,
        out_shape=rider.semaphores() + [pltpu.HBM(p.shape, p.dtype) for p in parts]
                  + [pltpu.HBM(z.shape, z.dtype) for z in lands] + [jax.ShapeDtypeStruct((8, 128), F32)],
        in_specs=[hbm] * (2 * n), out_specs=[sem] * 3 + [hbm] * (2 * n) + [pl.BlockSpec(memory_space=pltpu.VMEM)],
        input_output_aliases={i: 3 + i for i in range(2 * n)},
        compiler_params=pltpu.CompilerParams(has_side_effects=pltpu.SideEffectType.DATAFLOW_SIDE_EFFECTING),
    )(*[pltpu.with_memory_space_constraint(a, pltpu.HBM) for a in parts + lands])
    return res[:3], res[3:3 + n], res[3 + n:3 + 2 * n], res[-1]


def _exchange_wait(rider, started, after, name):
    n = rider.n
    sems, parts, lands, _ = started
    hbm = pl.BlockSpec(memory_space=pltpu.HBM)
    sem = pl.BlockSpec(memory_space=pltpu.SEMAPHORE)

    def body(*refs):
        rider.finish(refs[:n], refs[n:2 * n], refs[2 * n:2 * n + 3])

    res = pl.pallas_call(
        body, name=name, out_shape=[pltpu.HBM(a.shape, a.dtype) for a in list(parts) + list(lands)],
        in_specs=[hbm] * (2 * n) + [sem] * 3 + [ANY] * len(after), out_specs=[hbm] * (2 * n),
        input_output_aliases={i: i for i in range(2 * n)},
        compiler_params=pltpu.CompilerParams(has_side_effects=pltpu.SideEffectType.DATAFLOW_SIDE_EFFECTING),
    )(*parts, *lands, *sems, *after)
    return list(res[n:])


def _pcall(body, *, name, grid, in_specs, out_specs, out_shape, scratch_shapes=(), semantics, vmem_mb, rider=None,
           aliases=None, after=()):
    in_specs, out_specs, out_shape = list(in_specs), list(out_specs), list(out_shape)
    scratch_shapes = list(scratch_shapes)
    aliases = dict(aliases or {})
    if rider is None:
        n_in, after = len(in_specs), list(after)

        def plain(*refs):
            body(*refs[:n_in], *refs[n_in + len(after):])

        call = pl.pallas_call(plain if after else body, name=name, grid=grid, in_specs=in_specs + [ANY] * len(after),
                              out_specs=out_specs, out_shape=out_shape, scratch_shapes=scratch_shapes,
                              input_output_aliases=aliases, compiler_params=_params(semantics, vmem_mb))
        return lambda *args: (list(call(*args, *after)), None)
    n_in, n_out, n_scr = len(in_specs), len(out_specs), len(scratch_shapes)
    r_in, r_shapes = len(rider.arrays), rider.out_shape()
    r_out = len(r_shapes)
    aliases.update({n_in + i: n_out + o for i, o in rider.alias_pairs})
    total = math.prod(grid)
    mid_step = total - 1 if rider.has_mid and rider.late else (3 * total) // 4

    def wrapped(*refs):
        bounds = [0, n_in, r_in, n_out, r_out, n_scr]
        for i in range(1, len(bounds)):
            bounds[i] += bounds[i - 1]
        a, ra, o, ro, s = (refs[bounds[i]:bounds[i + 1]] for i in range(5))
        rs = refs[bounds[5]:]
        step = pl.program_id(0)
        for k in range(1, len(grid)):
            step = step * grid[k] + pl.program_id(k)
        pl.when(step == 0)(lambda: rider.start(ra, ro, rs))
        body(*a, *o, *s)
        if rider.has_mid:
            pl.when(step == mid_step)(lambda: rider.mid(ra, ro, rs))
        pl.when(step == total - 1)(lambda: rider.finish(ra, ro, rs))

    call = pl.pallas_call(
        wrapped, name=name, grid=grid, in_specs=in_specs + [ANY] * r_in, out_specs=out_specs + [ANY] * r_out,
        out_shape=out_shape + r_shapes, scratch_shapes=scratch_shapes + rider.scratch(),
        input_output_aliases=aliases, compiler_params=_params(("arbitrary",) * len(grid), vmem_mb))

    def run(*args):
        res = call(*args, *rider.arrays)
        return list(res[:n_out]), list(res[n_out:])

    return run


def _norm_matmul(x, g, w, *, name, out_dtype, tb, bn, relu=False, save_h=False, rider=None):
    t, d = x.shape
    n = w.shape[1]

    def body(x_ref, g_ref, w_ref, o_ref, *rest):
        h_scr = rest[-1]

        @pl.when(pl.program_id(1) == 0)
        def _():
            h = _rms_fwd(x_ref[...], g_ref[...])[0].astype(BF16)
            h_scr[...] = h
            if save_h:
                rest[0][...] = h

        acc = _dot(h_scr[...], w_ref[...])
        if relu:
            acc = jnp.maximum(acc, 0.0)
        o_ref[...] = acc.astype(out_dtype)

    out_shape = [jax.ShapeDtypeStruct((t, n), out_dtype)]
    out_specs = [pl.BlockSpec((tb, bn), lambda i, j: (i, j))]
    if save_h:
        out_shape.append(jax.ShapeDtypeStruct((t, d), BF16))
        out_specs.append(pl.BlockSpec((tb, d), lambda i, j: (i, 0)))
    res, extra = _pcall(
        body, name=name, grid=(t // tb, n // bn),
        in_specs=[pl.BlockSpec((tb, d), lambda i, j: (i, 0)),
                  pl.BlockSpec((1, d), lambda i, j: (0, 0)),
                  pl.BlockSpec((d, bn), lambda i, j: (0, j))],
        out_specs=out_specs, out_shape=out_shape,
        scratch_shapes=[pltpu.VMEM((tb, d), BF16)],
        semantics=("parallel", "arbitrary"), vmem_mb=48, rider=rider,
    )(x, g, w)
    res = res if save_h else res[0]
    return res if rider is None else (res, extra)


def _proj(x, g, w, *, tb, rider=None):
    t, d = x.shape
    half = w.shape[1] // 2

    def body(x_ref, g_ref, w_ref, qkv_ref, gates_ref, h_ref, h_scr):
        j = pl.program_id(1)

        @pl.when(j == 0)
        def _():
            h = _rms_fwd(x_ref[...], g_ref[...])[0].astype(BF16)
            h_scr[...] = h
            h_ref[...] = h

        acc = _dot(h_scr[...], w_ref[...])

        @pl.when(j == 0)
        def _():
            qkv_ref[...] = acc

        @pl.when(j == 1)
        def _():
            gates_ref[...] = acc.astype(BF16)

    tok = lambda c: pl.BlockSpec((tb, c), lambda i, j: (i, 0))
    res, extra = _pcall(
        body, name="proj", grid=(t // tb, 2),
        in_specs=[tok(d), pl.BlockSpec((1, d), lambda i, j: (0, 0)), pl.BlockSpec((d, half), lambda i, j: (0, j))],
        out_specs=[tok(half), tok(half), tok(d)],
        out_shape=[jax.ShapeDtypeStruct((t, half), F32), jax.ShapeDtypeStruct((t, half), BF16),
                   jax.ShapeDtypeStruct((t, d), BF16)],
        scratch_shapes=[pltpu.VMEM((tb, d), BF16)],
        semantics=("parallel", "arbitrary"), vmem_mb=48, rider=rider,
    )(x, g, w)
    return res if rider is None else (res, extra)


def _matmul_nt_normbwd(dy, w, x, g, dres, *, name, tb, also_bf16=False, to_natural=False, after=()):
    t, d = x.shape
    stacked = dy.ndim == 3
    n_i = SEG // TI
    if to_natural:
        tb = N_RES * TI

    def body(dy_ref, w_ref, x_ref, g_ref, dres_ref, *rest):
        rest = list(rest)
        dx_ref = rest.pop(0)
        dxb_ref = rest.pop(0) if also_bf16 else None
        gg_ref = rest.pop(0)
        i = pl.program_id(0)

        def rows(ref, *lead):
            v = ref[lead] if lead else ref[...]
            return v[0].reshape(tb, v.shape[-1]) if to_natural else v

        if stacked:
            kb = dy_ref.shape[-1]
            dh = _dot_nt(rows(dy_ref, 0), w_ref[:, 0:kb])
            for s in range(1, dy_ref.shape[0]):
                dh = dh + _dot_nt(rows(dy_ref, s), w_ref[:, s * kb:(s + 1) * kb])
        else:
            dh = _dot_nt(rows(dy_ref), w_ref[...])
        g_v = g_ref[...]
        _, xh, r = _rms_fwd(rows(x_ref), g_v)
        dx = _rms_bwd(dh, xh, r, g_v) + rows(dres_ref)
        if to_natural:
            scr = rest.pop(0)
            for cb in range(d // BLK):
                cols = slice(cb * BLK, (cb + 1) * BLK)
                slab = scr.at[cb]
                for res in range(N_RES):
                    slab[pl.ds(res, TI, stride=N_RES), :] = dx[res * TI:(res + 1) * TI, cols]
                dx_ref[:, cols] = slab[...]
        else:
            dx_ref[...] = dx
        if also_bf16:
            dxb_ref[...] = dx.astype(BF16)
        part = jnp.sum(dh * xh, axis=0, keepdims=True)

        @pl.when(i == 0)
        def _():
            gg_ref[...] = part

        @pl.when(i != 0)
        def _():
            gg_ref[...] += part

    tok = pl.BlockSpec((tb, d), lambda i: (i, 0))
    row = pl.BlockSpec((1, d), lambda i: (0, 0))
    if to_natural:
        act = pl.BlockSpec((1, N_RES, TI, d), lambda i: (i // n_i, 0, i % n_i, 0))
        dy_spec = pl.BlockSpec((dy.shape[0], 1, N_RES, TI, dy.shape[2]), lambda i: (0, i // n_i, 0, i % n_i, 0))
        dy, x, dres = dy.reshape(dy.shape[0], t // HALF, N_RES, SEG, dy.shape[2]), _x4(x), _x4(dres)
    elif stacked:
        act, dy_spec = tok, pl.BlockSpec((dy.shape[0], tb, dy.shape[2]), lambda i: (0, i, 0))
    else:
        act, dy_spec = tok, pl.BlockSpec((tb, dy.shape[1]), lambda i: (i, 0))
    in_specs = [dy_spec, pl.BlockSpec(w.shape, lambda i: (0, 0)), act, row, act]
    out_specs = [tok] + ([tok] if also_bf16 else []) + [row]
    out_shape = ([jax.ShapeDtypeStruct((t, d), F32)] + ([jax.ShapeDtypeStruct((t, d), BF16)] if also_bf16 else [])
                 + [jax.ShapeDtypeStruct((1, d), F32)])
    res, _ = _pcall(
        body, name=name, grid=(t // tb,), in_specs=in_specs, out_specs=out_specs, out_shape=out_shape,
        scratch_shapes=[pltpu.VMEM((d // BLK, tb, BLK), F32)] if to_natural else [],
        semantics=("arbitrary",), vmem_mb=56, after=after,
    )(dy, w, x, g, dres)
    return res


def _matmul_tn(a, b, *, name, bm, bn, square_a=False, after=()):
    t, m = a.shape
    stacked = b.ndim == 3
    n = b.shape[0] * bn if stacked else b.shape[1]

    def body(a_ref, b_ref, o_ref):
        av = a_ref[...]
        if square_a:
            av = av.astype(F32)
            av = (av * av).astype(BF16)
        o_ref[...] = _dot_tn(av, b_ref[...]).astype(BF16)

    res, _ = _pcall(
        body, name=name, grid=(m // bm, n // bn),
        in_specs=[pl.BlockSpec((t, bm), lambda i, j: (0, i)),
                  pl.BlockSpec((None, t, bn), lambda i, j: (j, 0, 0)) if stacked
                  else pl.BlockSpec((t, bn), lambda i, j: (0, j))],
        out_specs=[pl.BlockSpec((bm, bn), lambda i, j: (i, j))], out_shape=[jax.ShapeDtypeStruct((m, n), BF16)],
        semantics=("parallel", "parallel"), vmem_mb=56, after=after,
    )(a, b)
    return res[0]


N_RES = 16
SEG = 128
HALF = N_RES * SEG
TI = 32
HALO = 16


def _x4(a):
    return a.reshape(a.shape[0] // HALF, N_RES, SEG, a.shape[1])


def _reorder(arrays, name, rider=None):
    t, c = arrays[0].shape
    n = len(arrays)
    n_i = SEG // TI

    def body(*refs):
        scr = refs[-1]
        for i_ref, o_ref in zip(refs[:n], refs[n:2 * n]):
            for cb in range(c // BLK):
                cols = slice(cb * BLK, (cb + 1) * BLK)
                slab = scr.at[cb]
                slab[...] = i_ref[:, cols]
                for r in range(N_RES):
                    o_ref[0, r, :, cols] = slab[pl.ds(r, TI, stride=N_RES), :]

    res, extra = _pcall(
        body, name=name, grid=(t // (TI * N_RES),),
        in_specs=[pl.BlockSpec((TI * N_RES, c), lambda s: (s, 0))] * n,
        out_specs=[pl.BlockSpec((1, N_RES, TI, c), lambda s: (s // n_i, 0, s % n_i, 0))] * n,
        out_shape=[jax.ShapeDtypeStruct((t // HALF, N_RES, SEG, c), F32)] * n,
        scratch_shapes=[pltpu.VMEM((c // BLK, TI * N_RES, BLK), F32)],
        semantics=("parallel",), vmem_mb=32, rider=rider,
    )(*arrays)
    res = [r.reshape(t, c) for r in res]
    return res if rider is None else (res, extra)


_PATTERNS = ((1, 16, 8, SEG), (4, 4, 32, 4 * SEG), (16, 1, SEG, 0))
_FIRST = {1: 1, 4: 4, 16: 16}


def _group_rows(d, g):
    a = g >> 4
    if d == 16:
        base = a * HALF + (g & 15) * SEG
        prev = base - HALF
    elif d == 4:
        c = (g >> 2) & 3
        base = a * HALF + (g & 3) * SEG + c * 32
        prev = jnp.where(c > 0, base - 32, base - HALF + 96)
    else:
        c = g & 15
        base = a * HALF + c * 8
        prev = jnp.where(c > 0, base - 8, base - HALF + 120)
    return base, prev


def _load_rows(ref, base, n, rows, stride):
    parts = [ref[pl.ds(pl.multiple_of(base + j * stride, 8), rows), :] for j in range(n)]
    return parts[0] if n == 1 else jnp.concatenate(parts, axis=0)


def _store_rows(ref, base, val, n, rows, stride, add=False):
    for j in range(n):
        sl = pl.ds(pl.multiple_of(base + j * stride, 8), rows)
        piece = val[j * rows:(j + 1) * rows, :]
        if add:
            ref[sl, :] += piece
        else:
            ref[sl, :] = piece


def _band_bias(n, rows):
    shift = rows.bit_length() - 1
    lq = lax.broadcasted_iota(jnp.int32, (BLK, BLK), 0)
    lk = lax.broadcasted_iota(jnp.int32, (BLK, BLK), 1)
    iq = (lq & (rows - 1)) * n + (lq >> shift)
    ik = (lk & (rows - 1)) * n + (lk >> shift)
    zero = jnp.zeros((BLK, BLK), F32)
    return jnp.where(ik >= iq, zero, NEG_INF), jnp.where(ik <= iq, zero, NEG_INF)


def _set_bias(bias_scr, n, rows):
    prev_b, cur_b = _band_bias(n, rows)
    for half in range(2):
        bias_scr[half * BLK:(half + 1) * BLK, 0:BLK] = prev_b
        bias_scr[half * BLK:(half + 1) * BLK, BLK:2 * BLK] = cur_b


SCALE = 1.0 / math.sqrt(HEAD_DIM)


def _head_consts(value=1.0):
    lane_lo = lax.broadcasted_iota(jnp.int32, (BLK, BLK), 1) < HEAD_DIM
    return lane_lo, [jnp.where(lane_lo, value, 0.0).astype(BF16), jnp.where(lane_lo, 0.0, value).astype(BF16)]


def _stack_heads(v, head_mask):
    return jnp.concatenate([v * head_mask[0], v * head_mask[1]], axis=0)


def _unstack_heads(v2, lane_lo):
    return jnp.where(lane_lo, v2[:BLK], v2[BLK:])


def _rows_per_head(v, lane_lo):
    rolled = pltpu.roll(v, HEAD_DIM, axis=1)
    return jnp.concatenate([jnp.where(lane_lo, v, rolled), jnp.where(lane_lo, rolled, v)], axis=0)


WIDTH = 4


def _loop(lo, hi, fn, width=None):
    if width is None:
        def body(g, carry):
            fn(g)
            return carry

        if hi > lo:
            lax.fori_loop(lo, hi, body, 0)
        return
    while hi > lo:
        trips = (hi - lo) // width
        if trips:
            def body(i, carry, lo=lo, width=width):
                fn([lo + width * i + j for j in range(width)])
                return carry

            lax.fori_loop(0, trips, body, 0)
            lo += trips * width
        width = max(1, width // 2)


def _mix_weights(l1, l2, l3):
    mx = jnp.maximum(jnp.maximum(l1, l2), l3)
    e1, e2, e3 = jnp.exp(l1 - mx), jnp.exp(l2 - mx), jnp.exp(l3 - mx)
    inv = 1.0 / (e1 + e2 + e3)
    return e1 * inv, e2 * inv, e3 * inv


def _attention_fwd(qkv, rider=None):
    t = qkv.shape[0]
    groups = 16 * (t // HALF)

    def body(q_ref, k_ref, v_ref, attn_ref, l1_ref, l2_ref, l3_ref, o_scr, bias_scr):
        lane_lo, q_mask = _head_consts(SCALE)
        l_refs = (l1_ref, l2_ref, l3_ref)
        for p, (d, n, rows, stride) in enumerate(_PATTERNS):
            _set_bias(bias_scr, n, rows)
            o_p, l_p = o_scr.at[p], l_refs[p]

            def block(gs, has_prev):
                at = [_group_rows(d, g) for g in gs]

                def load(ref, b):
                    return _load_rows(ref, b, n, rows, stride).astype(BF16)

                q2 = [_stack_heads(load(q_ref, b), q_mask) for b, _ in at]
                k2 = [load(k_ref, b) for b, _ in at]
                v2 = [load(v_ref, b) for b, _ in at]
                if has_prev:
                    k2 = [jnp.concatenate([load(k_ref, pv), k], axis=0) for (_, pv), k in zip(at, k2)]
                    v2 = [jnp.concatenate([load(v_ref, pv), v], axis=0) for (_, pv), v in zip(at, v2)]
                s = [_dot_nt(q, k) for q, k in zip(q2, k2)]
                s = [x + (bias_scr[...] if has_prev else bias_scr[:, BLK:2 * BLK]) for x in s]
                mx = [jnp.max(x, axis=1, keepdims=True) for x in s]
                e = [jnp.exp(x - m) for x, m in zip(s, mx)]
                den = [jnp.sum(x, axis=1, keepdims=True) for x in e]
                o2 = [_dot(x.astype(BF16), v) * (1.0 / dn) for x, v, dn in zip(e, v2, den)]
                lse2 = [jnp.broadcast_to(m + jnp.log(dn), (2 * BLK, BLK)) for m, dn in zip(mx, den)]
                for (b, _), o, l in zip(at, o2, lse2):
                    _store_rows(o_p, b, _unstack_heads(o, lane_lo), n, rows, stride)
                    _store_rows(l_p, b, _unstack_heads(l, lane_lo), n, rows, stride)

            _loop(0, _FIRST[d], lambda gs: block(gs, False), width=2 * WIDTH)
            _loop(_FIRST[d], groups, lambda gs: block(gs, True), width=2 * WIDTH)

        def mix(i):
            sl = pl.ds(pl.multiple_of(i * 256, 256), 256)
            w = _mix_weights(l1_ref[sl, :], l2_ref[sl, :], l3_ref[sl, :])
            attn_ref[sl, :] = w[0] * o_scr[0, sl, :] + w[1] * o_scr[1, sl, :] + w[2] * o_scr[2, sl, :]

        _loop(0, t // 256, mix)

    def col(c0):
        return pl.BlockSpec((t, BLK), lambda hp: (0, c0 + hp))

    res, extra = _pcall(
        body, name="attention_fwd", grid=(4,), in_specs=[col(0), col(4), col(8)], out_specs=[col(0)] * 4,
        out_shape=[jax.ShapeDtypeStruct((t, 512), F32)] * 4,
        scratch_shapes=[pltpu.VMEM((3, t, BLK), F32), pltpu.VMEM((2 * BLK, 2 * BLK), F32)],
        semantics=("parallel",), vmem_mb=48, rider=rider,
    )(qkv, qkv, qkv)
    return res if rider is None else (res, extra)


def _attention_bwd(qkv, dattn, dsum, lses, dproj):
    t = qkv.shape[0]
    groups = 16 * (t // HALF)

    def body(q_ref, k_ref, v_ref, da_ref, ds_ref, l1_ref, l2_ref, l3_ref, kept_ref, out_ref, acc, bias_scr):
        del kept_ref
        lane_lo, head_mask = _head_consts()
        q_mask = _head_consts(SCALE)[1]
        l_refs = (l1_ref, l2_ref, l3_ref)

        def clear(i):
            sl = pl.ds(pl.multiple_of(i * 512, 512), 512)
            for s in range(3):
                acc[s, sl, :] = jnp.zeros((512, BLK), F32)

        _loop(0, t // 512, clear)
        dq_acc, dk_acc, dv_acc = acc.at[0], acc.at[1], acc.at[2]
        for p, (d, n, rows, stride) in enumerate(_PATTERNS):
            _set_bias(bias_scr, n, rows)

            def block(gs, has_prev):
                at = [_group_rows(d, g) for g in gs]

                def load(ref, b):
                    return _load_rows(ref, b, n, rows, stride)

                def put(ref, b, val):
                    _store_rows(ref, b, val, n, rows, stride, add=True)

                def wide(x):
                    return jnp.concatenate([x, x], axis=1) if has_prev else x

                lse = [[load(ref, b) for ref in l_refs] for b, _ in at]
                w = [_mix_weights(*ls)[p] for ls in lse]
                do2 = [_stack_heads((wg * load(da_ref, b)).astype(BF16), head_mask) for wg, (b, _) in zip(w, at)]
                dl2 = [wide(_rows_per_head(wg * load(ds_ref, b), lane_lo)) for wg, (b, _) in zip(w, at)]
                lse2 = [wide(_rows_per_head(ls[p], lane_lo)) for ls in lse]
                q2 = [_stack_heads(load(q_ref, b).astype(BF16), q_mask) for b, _ in at]
                k2 = [load(k_ref, b).astype(BF16) for b, _ in at]
                v2 = [load(v_ref, b).astype(BF16) for b, _ in at]
                if has_prev:
                    k2 = [jnp.concatenate([load(k_ref, pv).astype(BF16), k], axis=0) for (_, pv), k in zip(at, k2)]
                    v2 = [jnp.concatenate([load(v_ref, pv).astype(BF16), v], axis=0) for (_, pv), v in zip(at, v2)]
                s = [_dot_nt(q, k) for q, k in zip(q2, k2)]
                dp = [_dot_nt(do, v) for do, v in zip(do2, v2)]
                pr = [jnp.exp(x + (bias_scr[...] if has_prev else bias_scr[:, BLK:2 * BLK]) - l)
                      for x, l in zip(s, lse2)]
                ds = [(pg * (x - dl)).astype(BF16) for pg, x, dl in zip(pr, dp, dl2)]
                dq2 = [_dot(x, k) * SCALE for x, k in zip(ds, k2)]
                dk2 = [_dot_tn(x, q) for x, q in zip(ds, q2)]
                dv2 = [_dot_tn(pg.astype(BF16), do) for pg, do in zip(pr, do2)]
                for (b, pv), dq, dk, dv in zip(at, dq2, dk2, dv2):
                    put(dq_acc, b, _unstack_heads(dq, lane_lo))
                    if has_prev:
                        put(dk_acc, pv, dk[:BLK])
                        put(dv_acc, pv, dv[:BLK])
                        put(dk_acc, b, dk[BLK:])
                        put(dv_acc, b, dv[BLK:])
                    else:
                        put(dk_acc, b, dk)
                        put(dv_acc, b, dv)

            _loop(0, _FIRST[d], lambda gs: block(gs, False), width=WIDTH)
            _loop(_FIRST[d], groups, lambda gs: block(gs, True), width=WIDTH)

        def emit(i):
            sl = pl.ds(pl.multiple_of(i * 512, 512), 512)
            for s in range(3):
                out_ref[s, sl, :] = acc[s, sl, :].astype(BF16)

        _loop(0, t // 512, emit)

    def col(c0):
        return pl.BlockSpec((t, BLK), lambda hp: (0, c0 + hp))

    res, _ = _pcall(
        body, name="attention_bwd", grid=(4,),
        in_specs=[col(0), col(4), col(8)] + [col(0)] * 5 + [ANY],
        out_specs=[pl.BlockSpec((3, t, BLK), lambda hp: (0, 0, hp))],
        out_shape=[jax.ShapeDtypeStruct(dproj.shape, BF16)],
        scratch_shapes=[pltpu.VMEM((3, t, BLK), F32), pltpu.VMEM((2 * BLK, 2 * BLK), F32)],
        semantics=("parallel",), vmem_mb=56, aliases={8: 0},
    )(qkv, qkv, qkv, dattn, dsum, *lses, dproj)
    return res[0]


def _order_specs(t):
    n_i = SEG // TI
    nblk = (t // HALF) * n_i
    per = TI // HALO

    def main(c, col=0):
        return pl.BlockSpec((1, N_RES, TI, c), lambda s: (s // n_i, 0, s % n_i, col))

    def before(c, col=0):
        return pl.BlockSpec((1, 2, HALO, c), lambda s: (jnp.maximum(s - 1, 0) // n_i, N_RES // 2 - 1,
                                                        (jnp.maximum(s - 1, 0) % n_i) * per + per - 1, col))

    def after(c, col=0):
        return pl.BlockSpec((1, 2, HALO, c), lambda s: (jnp.minimum(s + 1, nblk - 1) // n_i, 0,
                                                        (jnp.minimum(s + 1, nblk - 1) % n_i) * per, col))

    return nblk, main, before, after


def _shift_in(v, row_in, up):
    rows = v.shape[0]
    idx = lax.broadcasted_iota(jnp.int32, v.shape, 0)
    fill = jnp.broadcast_to(row_in, v.shape)
    if up:
        return jnp.where(idx == rows - 1, fill, pltpu.roll(v, rows - 1, axis=0))
    return jnp.where(idx == 0, fill, pltpu.roll(v, 1, axis=0))


def _taps_behind(u, before):
    s15 = _shift_in(u[N_RES - 1], before[1, HALO - 1:HALO, :], up=False)
    s14 = _shift_in(u[N_RES - 2], before[0, HALO - 1:HALO, :], up=False)
    m1 = jnp.concatenate([s15[None], u[:N_RES - 1]], axis=0)
    m2 = jnp.concatenate([s14[None], s15[None], u[:N_RES - 2]], axis=0)
    return m1, m2


def _taps_ahead(u, after):
    t0 = _shift_in(u[0], after[0, 0:1, :], up=True)
    t1 = _shift_in(u[1], after[1, 0:1, :], up=True)
    p1 = jnp.concatenate([u[1:], t0[None]], axis=0)
    p2 = jnp.concatenate([u[2:], t0[None], t1[None]], axis=0)
    return p1, p2


def _conv_fwd(gates, before, first, cw):
    gates, before = gates.astype(F32), before.astype(F32)
    bg, cg, xc = gates[..., 0:512], gates[..., 512:1024], gates[..., 1024:1536]
    u = cg * xc
    ub = before[..., 512:1024] * before[..., 1024:1536]
    ub = jnp.where(first, jnp.zeros_like(ub), ub)
    m1, m2 = _taps_behind(u, ub)
    conv = m2 * cw[0:1, :] + m1 * cw[1:2, :] + u * cw[2:3, :]
    return bg, u, m1, m2, conv


def _sum_tokens(v):
    return jnp.sum(jnp.sum(v, axis=0), axis=0, keepdims=True)


def _mixer_fwd(x, attn, gates, cw, g_a, g_c, w_out):
    t, d = x.shape
    nblk, main, before, _ = _order_specs(t)
    rows = N_RES * TI

    def body(x_ref, at_ref, gt_ref, gb_ref, cw_ref, ga_ref, gc_ref, wa_ref, wb_ref, x1_ref, mg_ref):
        an = _rms_fwd(at_ref[0], ga_ref[...])[0].astype(BF16)
        bg, _, _, _, conv = _conv_fwd(gt_ref[0], gb_ref[0], pl.program_id(0) == 0, cw_ref[...])
        cn = _rms_fwd(bg * conv, gc_ref[...])[0].astype(BF16)
        mg_ref[0, :, :, 0:512] = an
        mg_ref[0, :, :, 512:1024] = cn
        y = _dot(an.reshape(rows, 512), wa_ref[...]) + _dot(cn.reshape(rows, 512), wb_ref[...])
        x1_ref[0] = x_ref[0] + y.reshape(N_RES, TI, d)

    const = lambda r, c, i0=0: pl.BlockSpec((r, c), lambda s: (i0, 0))
    x1, merged = pl.pallas_call(
        body, name="mixer_fwd", grid=(nblk,),
        in_specs=[main(d), main(512), main(1536), before(1536), const(3, 512), const(1, 512), const(1, 512),
                  const(512, d), const(512, d, 1)],
        out_specs=[main(d), main(d)],
        out_shape=[jax.ShapeDtypeStruct(_x4(x).shape, F32), jax.ShapeDtypeStruct(_x4(x).shape, BF16)],
        compiler_params=_params(("parallel",), 48),
    )(_x4(x), _x4(attn), _x4(gates), _x4(gates), cw, g_a, g_c, w_out, w_out)
    return x1.reshape(t, d), merged.reshape(t, d)


def _mixer_bwd(dx1, merged, attn, gates, cw, g_a, g_c, w_out, head_sum):
    t, d = dx1.shape
    nblk, main, before, _ = _order_specs(t)
    rows = N_RES * TI

    def body(dx_ref, mg_ref, at_ref, gt_ref, gb_ref, cw_ref, ga_ref, gc_ref, wa_ref, wb_ref, hs_ref,
             da_ref, dsum_ref, dy_ref, gga_ref, ggc_ref, gw_ref, acc_w):
        s = pl.program_id(0)
        dxb = dx_ref[0].reshape(rows, d).astype(BF16)

        @pl.when(s == 0)
        def _():
            acc_w[...] = jnp.zeros_like(acc_w)

        acc_w[...] += _dot_tn(mg_ref[0].reshape(rows, d), dxb)

        @pl.when(s == nblk - 1)
        def _():
            gw_ref[...] = acc_w[...].astype(BF16)

        dma = _dot_nt(dxb, wa_ref[...]).reshape(N_RES, TI, 512)
        dmc = _dot_nt(dxb, wb_ref[...]).reshape(N_RES, TI, 512)
        attn_v, g_av = at_ref[0], ga_ref[...]
        _, ah, ra = _rms_fwd(attn_v, g_av)
        dattn = _rms_bwd(dma, ah, ra, g_av)
        da_ref[0] = dattn
        z = (dattn * attn_v).reshape(rows, 512)
        hs = hs_ref[...]
        z1 = z.astype(BF16)
        z2 = (z - z1.astype(F32)).astype(BF16)
        dsum_ref[0] = (_dot(z1, hs) + _dot(z2, hs)).reshape(N_RES, TI, 512)
        bg, _, _, _, conv = _conv_fwd(gt_ref[0], gb_ref[0], s == 0, cw_ref[...])
        g_cv = gc_ref[...]
        _, yh, rc = _rms_fwd(bg * conv, g_cv)
        dy_ref[0] = _rms_bwd(dmc, yh, rc, g_cv)
        pa, pc = _sum_tokens(dma * ah), _sum_tokens(dmc * yh)

        @pl.when(s == 0)
        def _():
            gga_ref[...] = pa
            ggc_ref[...] = pc

        @pl.when(s != 0)
        def _():
            gga_ref[...] += pa
            ggc_ref[...] += pc

    const = lambda r, c, i0=0: pl.BlockSpec((r, c), lambda s: (i0, 0))
    shape4 = _x4(attn).shape
    res, _ = _pcall(
        body, name="mixer_bwd", grid=(nblk,),
        in_specs=[main(d), main(d), main(512), main(1536), before(1536), const(3, 512), const(1, 512), const(1, 512),
                  const(512, d), const(512, d, 1), const(512, 512)],
        out_specs=[main(512)] * 3 + [const(1, 512), const(1, 512), const(d, d)],
        out_shape=[jax.ShapeDtypeStruct(shape4, F32)] * 3 + [jax.ShapeDtypeStruct((1, 512), F32)] * 2
        + [jax.ShapeDtypeStruct((d, d), BF16)],
        scratch_shapes=[pltpu.VMEM((d, d), F32)],
        semantics=("arbitrary",), vmem_mb=48,
    )(_x4(dx1), _x4(merged), _x4(attn), _x4(gates), _x4(gates), cw, g_a, g_c, w_out, w_out, head_sum)
    return [r.reshape(t, 512) for r in res[:3]] + res[3:]


def _conv_bwd(dy, gates, cw, after=()):
    t = dy.shape[0]
    nblk, main, before, ahead = _order_specs(t)
    n_i = SEG // TI

    def body(dy_ref, dya_ref, gt_ref, gb_ref, ga_ref, cw_ref, dp_ref, gcw_ref):
        s = pl.program_id(0)
        cw_v, gates_v = cw_ref[...], gt_ref[0]
        bg, u, m1, m2, conv = _conv_fwd(gates_v, gb_ref[0], s == 0, cw_v)
        dy_v = dy_ref[0]
        dconv = dy_v * bg
        dca = dya_ref[0] * ga_ref[0][..., 0:512].astype(F32)
        dca = jnp.where(s == nblk - 1, jnp.zeros_like(dca), dca)
        p1, p2 = _taps_ahead(dconv, dca)
        du = dconv * cw_v[2:3, :] + p1 * cw_v[1:2, :] + p2 * cw_v[0:1, :]
        dp_ref[0, 0] = (dy_v * conv).astype(BF16)
        dp_ref[1, 0] = (du * gates_v[..., 1024:1536].astype(F32)).astype(BF16)
        dp_ref[2, 0] = (du * gates_v[..., 512:1024].astype(F32)).astype(BF16)
        parts = [_sum_tokens(dconv * m2), _sum_tokens(dconv * m1), _sum_tokens(dconv * u)]

        @pl.when(s == 0)
        def _():
            gcw_ref[...] = jnp.zeros_like(gcw_ref)

        for tap in range(3):
            gcw_ref[tap:tap + 1, :] += parts[tap]

    (dproj, gcw), _ = _pcall(
        body, name="conv_bwd", grid=(nblk,),
        in_specs=[main(512), ahead(512), main(1536), before(1536), ahead(1536),
                  pl.BlockSpec((3, 512), lambda s: (0, 0))],
        out_specs=[pl.BlockSpec((3, 1, N_RES, TI, 512), lambda s: (1, s // n_i, 0, s % n_i, 0)),
                   pl.BlockSpec((8, 512), lambda s: (0, 0))],
        out_shape=[jax.ShapeDtypeStruct((6, t // HALF, N_RES, SEG, 512), BF16), jax.ShapeDtypeStruct((8, 512), F32)],
        semantics=("arbitrary",), vmem_mb=40, after=after,
    )(_x4(dy), _x4(dy), _x4(gates), _x4(gates), _x4(gates), cw)
    return dproj.reshape(6, t, 512), gcw


def _xattn_fwd(x1, g, w_q, kv, w_o, *, tb):
    t, d = x1.shape
    hd = d // N_MEM_HEADS
    m = kv.shape[0]

    def body(x_ref, g_ref, wq_ref, k_ref, v_ref, wo_ref, x2_ref, h_ref, q_ref, o_ref):
        xv = x_ref[...]
        h = _rms_fwd(xv, g_ref[...])[0].astype(BF16)
        h_ref[...] = h
        q = _dot(h, wq_ref[...]).astype(BF16)
        q_ref[...] = q
        for hh in range(N_MEM_HEADS):
            sl = slice(hh * hd, (hh + 1) * hd)
            s = _dot_nt(q[:, sl], k_ref[:, sl]) * (1.0 / 16.0)
            e = jnp.exp(s - jnp.max(s, axis=1, keepdims=True))
            p = e / jnp.sum(e, axis=1, keepdims=True)
            o_ref[:, sl] = _dot(p.astype(BF16), v_ref[:, sl]).astype(BF16)
        x2_ref[...] = xv + _dot(o_ref[...], wo_ref[...])

    tok = pl.BlockSpec((tb, d), lambda i: (i, 0))
    full = pl.BlockSpec((d, d), lambda i: (0, 0))
    return pl.pallas_call(
        body, name="xattn_fwd", grid=(t // tb,),
        in_specs=[tok, pl.BlockSpec((1, d), lambda i: (0, 0)), full,
                  pl.BlockSpec((m, d), lambda i: (0, 0)), pl.BlockSpec((m, d), lambda i: (0, 1)), full],
        out_specs=[tok] * 4,
        out_shape=[jax.ShapeDtypeStruct((t, d), F32)] + [jax.ShapeDtypeStruct((t, d), BF16)] * 3,
        compiler_params=_params(("parallel",), 48),
    )(x1, g, w_q, kv, kv, w_o)


def _xattn_bwd(dx2, x1, g, q, w_q, kv, w_o, *, tb):
    t, d = x1.shape
    hd = d // N_MEM_HEADS
    m = kv.shape[0]

    def body(dx2_ref, x_ref, g_ref, q_ref, wq_ref, k_ref, v_ref, wo_ref,
             dx1_ref, dq_ref, dk_ref, dv_ref, gg_ref):
        i = pl.program_id(0)

        @pl.when(i == 0)
        def _():
            dk_ref[...] = jnp.zeros_like(dk_ref)
            dv_ref[...] = jnp.zeros_like(dv_ref)

        dx2 = dx2_ref[...]
        do = _dot_nt(dx2.astype(BF16), wo_ref[...]).astype(BF16)
        for hh in range(N_MEM_HEADS):
            sl = slice(hh * hd, (hh + 1) * hd)
            qh, kh, vh, doh = q_ref[:, sl], k_ref[:, sl], v_ref[:, sl], do[:, sl]
            s = _dot_nt(qh, kh) * (1.0 / 16.0)
            e = jnp.exp(s - jnp.max(s, axis=1, keepdims=True))
            p = e / jnp.sum(e, axis=1, keepdims=True)
            dp = _dot_nt(doh, vh)
            ds = (p * (dp - jnp.sum(dp * p, axis=1, keepdims=True)) * (1.0 / 16.0)).astype(BF16)
            dq_ref[:, sl] = _dot(ds, kh).astype(BF16)
            dk_ref[:, sl] += _dot_tn(ds, qh)
            dv_ref[:, sl] += _dot_tn(p.astype(BF16), doh)
        dh = _dot_nt(dq_ref[...], wq_ref[...])
        g_v = g_ref[...]
        _, xh, r = _rms_fwd(x_ref[...], g_v)
        dx1 = dx2 + _rms_bwd(dh, xh, r, g_v)
        dx1_ref[...] = dx1
        part = jnp.sum(dh * xh, axis=0, keepdims=True)

        @pl.when(i == 0)
        def _():
            gg_ref[...] = part

        @pl.when(i != 0)
        def _():
            gg_ref[...] += part

    tok = pl.BlockSpec((tb, d), lambda i: (i, 0))
    full = pl.BlockSpec((d, d), lambda i: (0, 0))
    acc = pl.BlockSpec((m, d), lambda i: (0, 0))
    res, _ = _pcall(
        body, name="xattn_bwd", grid=(t // tb,),
        in_specs=[tok, tok, pl.BlockSpec((1, d), lambda i: (0, 0)), tok, full,
                  pl.BlockSpec((m, d), lambda i: (0, 0)), pl.BlockSpec((m, d), lambda i: (0, 1)), full],
        out_specs=[tok, tok, acc, acc, pl.BlockSpec((1, d), lambda i: (0, 0))],
        out_shape=[jax.ShapeDtypeStruct((t, d), F32), jax.ShapeDtypeStruct((t, d), BF16),
                   jax.ShapeDtypeStruct((m, d), F32), jax.ShapeDtypeStruct((m, d), F32),
                   jax.ShapeDtypeStruct((1, d), F32)],
        semantics=("arbitrary",), vmem_mb=48,
    )(dx2, x1, g, q, w_q, kv, kv, w_o)
    return res


def _mem_bwd(dk, dv, w_kv, mem, mem_n, g):
    m, d = mem.shape

    def body(dk_ref, dv_ref, w_ref, x_ref, h_ref, g_ref, gw_ref, gg_ref):
        h = h_ref[...]
        dh = jnp.zeros((m, d), F32)
        for i, dy_ref in enumerate((dk_ref, dv_ref)):
            cols = slice(i * d, (i + 1) * d)
            dy = dy_ref[...].astype(BF16)
            gw_ref[:, cols] = _dot_tn(h, dy).astype(BF16)
            dh = dh + _dot_nt(dy, w_ref[:, cols])
        xh = _rms_fwd(x_ref[...], g_ref[...])[1]
        gg_ref[...] = jnp.sum(dh * xh, axis=0, keepdims=True)

    return pl.pallas_call(
        body, name="mem_bwd",
        out_shape=[jax.ShapeDtypeStruct(w_kv.shape, BF16), jax.ShapeDtypeStruct((1, d), F32)],
        compiler_params=pltpu.CompilerParams(vmem_limit_bytes=32 << 20),
    )(dk, dv, w_kv, mem, mem_n, g)


def _mlp_down_loss(a, w_down, x2, tgt, g, *, tb):
    t, d = x2.shape
    f = a.shape[1]

    def body(a_ref, w_ref, x_ref, t_ref, g_ref, dx_ref, dxb_ref, loss_ref, gg_ref):
        i = pl.program_id(0)
        av = a_ref[...].astype(F32)
        x3 = x_ref[...] + _dot((av * av).astype(BF16), w_ref[...])
        g_v = g_ref[...]
        out, xh, r = _rms_fwd(x3, g_v)
        err = out - t_ref[...]
        dout = err * (1.0 / d)
        dx = _rms_bwd(dout, xh, r, g_v)
        dx_ref[...] = dx
        dxb_ref[...] = dx.astype(BF16)
        part = jnp.sum(dout * xh, axis=0, keepdims=True)
        lpart = 0.5 * jnp.sum(jnp.mean(err * err, axis=-1, keepdims=True), axis=0, keepdims=True)
        lpart = jnp.broadcast_to(lpart, loss_ref.shape)

        @pl.when(i == 0)
        def _():
            gg_ref[...] = part
            loss_ref[...] = lpart

        @pl.when(i != 0)
        def _():
            gg_ref[...] += part
            loss_ref[...] += lpart

    tok = pl.BlockSpec((tb, d), lambda i: (i, 0))
    return pl.pallas_call(
        body, name="mlp_down_loss", grid=(t // tb,),
        in_specs=[pl.BlockSpec((tb, f), lambda i: (i, 0)), pl.BlockSpec((f, d), lambda i: (0, 0)), tok, tok,
                  pl.BlockSpec((1, d), lambda i: (0, 0))],
        out_specs=[tok, tok, pl.BlockSpec((8, 128), lambda i: (0, 0)), pl.BlockSpec((1, d), lambda i: (0, 0))],
        out_shape=[jax.ShapeDtypeStruct((t, d), F32), jax.ShapeDtypeStruct((t, d), BF16),
                   jax.ShapeDtypeStruct((8, 128), F32), jax.ShapeDtypeStruct((1, d), F32)],
        compiler_params=_params(("arbitrary",), 56),
    )(a, w_down, x2, tgt, g)


def _mlp_dpre(dx3, w_down, a, *, tb, bn):
    t, d = dx3.shape
    f = a.shape[1]

    def body(dx_ref, w_ref, a_ref, o_ref):
        o_ref[...] = (2.0 * a_ref[...].astype(F32) * _dot_nt(dx_ref[...], w_ref[...])).astype(BF16)

    return pl.pallas_call(
        body, name="mlp_dpre", grid=(t // tb, f // bn),
        in_specs=[pl.BlockSpec((tb, d), lambda i, j: (i, 0)), pl.BlockSpec((bn, d), lambda i, j: (j, 0)),
                  pl.BlockSpec((tb, bn), lambda i, j: (i, j))],
        out_specs=pl.BlockSpec((tb, bn), lambda i, j: (i, j)),
        out_shape=jax.ShapeDtypeStruct((t, f), BF16),
        compiler_params=_params(("parallel", "arbitrary"), 48),
    )(dx3, w_down, a)


def _adamw(gsum, w, m, v):
    m_new = ADAM_B1 * m + (1.0 - ADAM_B1) * gsum
    v_new = ADAM_B2 * v + (1.0 - ADAM_B2) * (gsum * gsum)
    m_hat = m_new / (1.0 - ADAM_B1 ** ADAM_STEP)
    v_hat = v_new / (1.0 - ADAM_B2 ** ADAM_STEP)
    delta = -ADAM_LR * (m_hat / (jnp.sqrt(v_hat) + ADAM_EPS) + ADAM_WD * w)
    return delta, m_new, v_new


def _sum_adamw(shards, *, name, tr):
    r, c = shards[0][1].shape
    n = len(shards)

    def body(*refs):
        for s in range(n):
            p_ref, w_ref, m_ref, v_ref = refs[4 * s:4 * s + 4]
            g_ref, d_ref, mo_ref, vo_ref = refs[4 * (n + s):4 * (n + s) + 4]
            g = p_ref[0].astype(F32)
            for k in range(1, N_DEV):
                g = g + p_ref[k].astype(F32)
            g_ref[...] = g
            d_ref[...], mo_ref[...], vo_ref[...] = _adamw(g, w_ref[...], m_ref[...], v_ref[...])

    blk = pl.BlockSpec((tr, c), lambda i: (i, 0))
    res = pl.pallas_call(
        body, name=name, grid=(r // tr,),
        in_specs=[pl.BlockSpec((N_DEV, tr, c), lambda i: (0, i, 0)), blk, blk, blk] * n,
        out_specs=[blk] * (4 * n), out_shape=[jax.ShapeDtypeStruct((r, c), F32)] * (4 * n),
        compiler_params=_params(("parallel",), 40),
    )(*[pltpu.with_memory_space_constraint(a, pltpu.HBM) for shard in shards for a in shard])
    return [res[4 * s:4 * s + 4] for s in range(n)]


_GAIN_ROWS = ("g_mix", "g_xattn", "g_mem", "g_mlp", "g_final")
PAIR_ROW = 5
LOSS_ROW = 6
TAPS_ROW = 8
SMALL_ROWS = 16
_SMALL = _GAIN_ROWS + ("g_attn_out", "g_conv_out", "conv_w")
CONV_SHARD = 512 // N_DEV


def _pack_small(gains, gg_attn, gg_conv, gcw, loss_blk):
    def body(*refs):
        o_ref = refs[-1]
        ga_ref, gc_ref, cw_ref, l_ref = refs[len(gains):-1]
        o_ref[...] = jnp.zeros_like(o_ref)
        for i, g_ref in enumerate(refs[:len(gains)]):
            o_ref[i:i + 1, :] = g_ref[...]
        o_ref[PAIR_ROW:PAIR_ROW + 1, 0:512] = ga_ref[...]
        o_ref[PAIR_ROW:PAIR_ROW + 1, 512:1024] = gc_ref[...]
        o_ref[LOSS_ROW:LOSS_ROW + 1, 0:BLK] = l_ref[0:1, :]
        o_ref[TAPS_ROW:SMALL_ROWS, 0:512] = cw_ref[...]

    return pl.pallas_call(body, name="pack_small", out_shape=jax.ShapeDtypeStruct((SMALL_ROWS, 1024), F32))(
        *gains, gg_attn, gg_conv, gcw, loss_blk)


def _update_small(parts, me, w, m, v):
    n = len(_SMALL)

    def body(me_ref, p_ref, *refs):
        ins, loss_ref, outs = refs[:3 * n], refs[3 * n], refs[3 * n + 1:]

        def total(lo, hi):
            s = p_ref[0, lo:hi, :]
            for k in range(1, N_DEV):
                s = s + p_ref[k, lo:hi, :]
            return s

        grads = {k: total(i, i + 1) for i, k in enumerate(_GAIN_ROWS)}
        both = total(PAIR_ROW, PAIR_ROW + 1)
        grads["g_attn_out"], grads["g_conv_out"] = both[:, 0:512], both[:, 512:1024]
        taps = total(TAPS_ROW, SMALL_ROWS)
        mine = jnp.zeros((SMALL_ROWS - TAPS_ROW, BLK), F32)
        for j in range(N_DEV):
            lo = j * CONV_SHARD // BLK * BLK
            blk = taps[:, lo:lo + BLK]
            if j * CONV_SHARD != lo:
                blk = pltpu.roll(blk, BLK - (j * CONV_SHARD - lo), axis=1)
            mine = jnp.where(me_ref[0] == j, blk, mine)
        grads["conv_w"] = mine[0:3, 0:CONV_SHARD]
        loss_ref[...] = total(LOSS_ROW, LOSS_ROW + 1)[:, 0:1]
        for i, k in enumerate(_SMALL):
            g_ref, d_ref, mo_ref, vo_ref = outs[4 * i:4 * i + 4]
            g_ref[...] = grads[k]
            d_ref[...], mo_ref[...], vo_ref[...] = _adamw(grads[k], ins[i][...], ins[n + i][...], ins[2 * n + i][...])

    vmem = pl.BlockSpec(memory_space=pltpu.VMEM)
    args = [d[k] for d in (w, m, v) for k in _SMALL]
    res = pl.pallas_call(
        body, name="update_small",
        in_specs=[pl.BlockSpec(memory_space=pltpu.SMEM)] + [vmem] * (1 + 3 * n),
        out_shape=[jax.ShapeDtypeStruct((1, 1), F32)] + [jax.ShapeDtypeStruct(w[k].shape, F32) for k in _SMALL
                                                         for _ in range(4)],
    )(me, parts, *args)
    return res[0], {k: res[1 + 4 * i:5 + 4 * i] for i, k in enumerate(_SMALL)}


def _head_sum_matrix():
    r = lax.broadcasted_iota(jnp.int32, (512, 512), 0) // HEAD_DIM
    c = lax.broadcasted_iota(jnp.int32, (512, 512), 1) // HEAD_DIM
    return (r == c).astype(BF16)


_SHARD_AXIS = dict(w_in=1, w_out=0, w_q=0, w_kv=1, w_o=0, w_up=1, w_down=0, conv_w=None, small=None)


class _Weights:
    def __init__(self, full, shards=None):
        self.full = dict(full)
        self.shards = shards

    def rider(self, names, late=False):
        if self.shards is None:
            return None
        return _Gather([self.shards[n] for n in names], [_SHARD_AXIS[n] for n in names], late)

    def arrived(self, names, gathered):
        if gathered is not None:
            for n, g in zip(names, gathered):
                self.full[n] = g.transpose(1, 0, 2).reshape(g.shape[1], -1) if n == "conv_w" else g

    def __getitem__(self, name):
        return self.full[name]


class _Grads:
    def __init__(self, distributed):
        self.distributed = distributed
        self.local = {}
        self.pending = {}

    def add(self, name, g):
        self.local[name] = g

    def send(self, *names):
        if not self.distributed:
            return []
        rider = _Exchange([self.local[n] for n in names], [_SHARD_AXIS[n] for n in names])
        started = _exchange_start(rider, "send_" + "_".join(names))
        self.pending[names[0]] = (names, rider, started)
        return [started[3]]

    def wait(self, first_name, after):
        names, rider, started = self.pending.pop(first_name)
        return _exchange_wait(rider, started, after, "wait_" + "_".join(names))


def _ride(fn, *args, rider=None, **kw):
    if rider is None:
        return fn(*args, **kw), None
    return fn(*args, rider=rider, **kw)


def _local_step(x, mem, tgt, gains, weights, grads):
    names = ["w_in", "conv_w"]
    (x, tgt), got = _ride(_reorder, [x, tgt], "reorder_in", rider=weights.rider(names, late=True))
    weights.arrived(names, got)
    w_in, cw = weights["w_in"], weights["conv_w"]

    names = ["w_out", "w_kv"]
    (qkv, gates, h1), got = _ride(_proj, x, gains["g_mix"], w_in, tb=1024, rider=weights.rider(names))
    weights.arrived(names, got)
    names = ["w_q", "w_o", "w_up"]
    (attn, *lses), got = _ride(_attention_fwd, qkv, rider=weights.rider(names))
    weights.arrived(names, got)
    x1, merged = _mixer_fwd(x, attn, gates, cw, gains["g_attn_out"], gains["g_conv_out"], weights["w_out"])
    kv, mem_n = _norm_matmul(mem, gains["g_mem"], weights["w_kv"], name="mem_kv", out_dtype=BF16, tb=mem.shape[0],
                             bn=1024, save_h=True)
    x2, h2, qm, om = _xattn_fwd(x1, gains["g_xattn"], weights["w_q"], kv, weights["w_o"], tb=512)
    w_up = weights["w_up"]
    (a, h3), got = _ride(_norm_matmul, x2, gains["g_mlp"], w_up, name="mlp_up", out_dtype=BF16, tb=1024, bn=2048,
                         relu=True, save_h=True, rider=weights.rider(["w_down"], late=True))
    weights.arrived(["w_down"], got)
    w_down = weights["w_down"]
    dx3, dx3b, loss_blk, gg_final = _mlp_down_loss(a, w_down, x2, tgt, gains["g_final"], tb=512)

    dpre = _mlp_dpre(dx3b, w_down, a, tb=1024, bn=2048)
    grads.add("w_down", _matmul_tn(a, dx3b, name="grad_w_down", bm=512, bn=1024, square_a=True))
    sent = grads.send("w_down")
    grads.add("w_up", _matmul_tn(h3, dpre, name="grad_w_up", bm=1024, bn=1024, after=sent))
    sent = grads.send("w_up")
    dx2, dx2b, gg_mlp = _matmul_nt_normbwd(dpre, w_up, x2, gains["g_mlp"], dx3, name="mlp_dx", tb=512,
                                           also_bf16=True, after=sent)

    grads.add("w_o", _matmul_tn(om, dx2b, name="grad_w_o", bm=512, bn=512))
    dx1, dqm, dk, dv, gg_xattn = _xattn_bwd(dx2, x1, gains["g_xattn"], qm, weights["w_q"], kv, weights["w_o"], tb=512)
    grads.add("w_q", _matmul_tn(h2, dqm, name="grad_w_q", bm=1024, bn=512))
    gw_kv, gg_mem = _mem_bwd(dk, dv, weights["w_kv"], mem, mem_n, gains["g_mem"])
    grads.add("w_kv", gw_kv)

    dattn, dsum, dy, gg_attn, gg_conv, gw_out = _mixer_bwd(dx1, merged, attn, gates, cw, gains["g_attn_out"],
                                                           gains["g_conv_out"], weights["w_out"], _head_sum_matrix())
    grads.add("w_out", gw_out)
    sent = grads.send("w_o", "w_q", "w_kv", "w_out")
    dproj, gcw = _conv_bwd(dy, gates, cw, after=sent)
    dproj = _attention_bwd(qkv, dattn, dsum, lses, dproj)
    grads.add("w_in", _matmul_tn(h1, dproj, name="grad_w_in", bm=1024, bn=512))
    sent = grads.send("w_in")
    grad_x, gg_mix = _matmul_nt_normbwd(dproj, w_in, x, gains["g_mix"], dx1, name="mixer_dx", tb=512,
                                        to_natural=True, after=sent)

    grads.add("small", _pack_small([gg_mix, gg_xattn, gg_mem, gg_mlp, gg_final], gg_attn, gg_conv, gcw, loss_blk))
    return grad_x


_BIG = ("w_in", "w_out", "w_q", "w_kv", "w_o", "w_up", "w_down")


def kernel(x, mem, g_mix, w_in, conv_w, g_attn_out, g_conv_out, w_out, g_xattn, g_mem, w_q_mem, w_kv_mem, w_o_mem, g_mlp, w_up, w_down, g_final, loss_target, m_g_mix, m_w_in, m_conv_w, m_g_attn_out, m_g_conv_out, m_w_out, m_g_xattn, m_g_mem, m_w_q_mem, m_w_kv_mem, m_w_o_mem, m_g_mlp, m_w_up, m_w_down, m_g_final, v_g_mix, v_w_in, v_conv_w, v_g_attn_out, v_g_conv_out, v_w_out, v_g_xattn, v_g_mem, v_w_q_mem, v_w_kv_mem, v_w_o_mem, v_g_mlp, v_w_up, v_w_down, v_g_final):
    d = x.shape[-1]
    me = 4 * lax.axis_index("x") + 2 * lax.axis_index("y") + lax.axis_index("c")
    w_shards = dict(w_in=w_in, w_out=w_out, w_q=w_q_mem, w_kv=w_kv_mem, w_o=w_o_mem, w_up=w_up, w_down=w_down)
    m_shards = dict(w_in=m_w_in, w_out=m_w_out, w_q=m_w_q_mem, w_kv=m_w_kv_mem, w_o=m_w_o_mem, w_up=m_w_up,
                    w_down=m_w_down)
    v_shards = dict(w_in=v_w_in, w_out=v_w_out, w_q=v_w_q_mem, w_kv=v_w_kv_mem, w_o=v_w_o_mem, w_up=v_w_up,
                    w_down=v_w_down)
    gains = dict(g_mix=g_mix, g_attn_out=g_attn_out, g_conv_out=g_conv_out, g_xattn=g_xattn, g_mem=g_mem,
                 g_mlp=g_mlp, g_final=g_final)
    gains2 = {k: v.reshape(1, -1) for k, v in gains.items()}

    shards = {k: w_shards[k].astype(BF16) for k in _BIG}
    shards["conv_w"] = conv_w
    grads = _Grads(distributed=True)
    grad_x = _local_step(x[0], mem[0], loss_target[0], gains2, _Weights({}, shards), grads)

    after = grads.send("small")
    outs = {}
    tiles = dict(w_in=256, w_out=64, w_q=64, w_kv=256, w_o=64, w_up=256, w_down=256)
    for group in (("w_down",), ("w_up",), ("w_o", "w_q", "w_kv", "w_out"), ("w_in",)):
        received = dict(zip(group, grads.wait(group[0], after)))
        same_shape = {}
        for k in group:
            same_shape.setdefault(w_shards[k].shape, []).append(k)
        for names in same_shape.values():
            res = _sum_adamw([(received[k], w_shards[k], m_shards[k], v_shards[k]) for k in names],
                             name="adamw_" + "_".join(names), tr=tiles[names[0]])
            outs.update(zip(names, res))
            after = [res[-1][0]]
    small_received, = grads.wait("small", after)

    m_small = dict(g_mix=m_g_mix, g_attn_out=m_g_attn_out, g_conv_out=m_g_conv_out, g_xattn=m_g_xattn,
                   g_mem=m_g_mem, g_mlp=m_g_mlp, g_final=m_g_final)
    v_small = dict(g_mix=v_g_mix, g_attn_out=v_g_attn_out, g_conv_out=v_g_conv_out, g_xattn=v_g_xattn,
                   g_mem=v_g_mem, g_mlp=v_g_mlp, g_final=v_g_final)
    as_rows = lambda vals, conv: dict({k: a.reshape(1, -1) for k, a in vals.items()}, conv_w=conv)
    loss, small_out = _update_small(small_received, me.reshape(1), as_rows(gains, conv_w),
                                    as_rows(m_small, m_conv_w), as_rows(v_small, v_conv_w))
    small_out = {k: [a.reshape(dict(gains, conv_w=conv_w)[k].shape) for a in res] for k, res in small_out.items()}
    names = {"g_mix": "g_mix", "w_in": "w_in", "conv_w": "conv_w", "g_attn_out": "g_attn_out",
             "g_conv_out": "g_conv_out", "w_out": "w_out", "g_xattn": "g_xattn", "g_mem": "g_mem",
             "w_q_mem": "w_q", "w_kv_mem": "w_kv", "w_o_mem": "w_o", "g_mlp": "g_mlp", "w_up": "w_up",
             "w_down": "w_down", "g_final": "g_final"}
    result = [loss.reshape(()), grad_x[None]]
    for which in range(4):
        for key in names.values():
            result.append(outs[key][which] if key in outs else small_out[key][which])
    return tuple(result)
```

```python
import math

import jax
import jax.numpy as jnp
from jax import lax
from jax.experimental import pallas as pl
from jax.experimental.pallas import tpu as pltpu

F32 = jnp.float32
BF16 = jnp.bfloat16
NORM_EPS = 1e-6
NEG_INF = -1e30
N_DEV = 8
BLK = 128
HEAD_DIM = 64
N_MEM_HEADS = 4
ADAM_LR = 0.001
ADAM_B1 = 0.9
ADAM_B2 = 0.999
ADAM_EPS = 1e-08
ADAM_WD = 0.01
ADAM_STEP = 10
MESH = pl.DeviceIdType.MESH
ANY = pl.BlockSpec(memory_space=pl.ANY)


def _dot(a, b):
    return jnp.dot(a, b, preferred_element_type=F32)


def _dot_nt(a, b):
    return lax.dot_general(a, b, (((1,), (1,)), ((), ())), preferred_element_type=F32)


def _dot_tn(a, b):
    return lax.dot_general(a, b, (((0,), (0,)), ((), ())), preferred_element_type=F32)


def _params(semantics, vmem_mb):
    return pltpu.CompilerParams(dimension_semantics=semantics, vmem_limit_bytes=vmem_mb << 20)


def _rms_fwd(x, g):
    r = lax.rsqrt(jnp.mean(x * x, axis=-1, keepdims=True) + NORM_EPS)
    xh = x * r
    return xh * g, xh, r


def _rms_bwd(dy, xh, r, g):
    gy = dy * g
    return r * (gy - xh * jnp.mean(xh * gy, axis=-1, keepdims=True))


def _position():
    x, y, c = lax.axis_index("x"), lax.axis_index("y"), lax.axis_index("c")
    return x, y, c


def _block_of(ref, j, axis, shard_shape):
    r, c = shard_shape
    if axis is None:
        return ref.at[j]
    if axis == 0:
        return ref.at[pl.ds(j * r, r), :]
    return ref.at[:, pl.ds(j * c, c)]


class _Gather:
    has_mid = True
    alias_pairs = ()

    def __init__(self, shards, axes, late=False):
        self.arrays = list(shards)
        self.axes = list(axes)
        self.late = late
        self.n = len(self.arrays)

    def out_shape(self):
        res = []
        for s, axis in zip(self.arrays, self.axes):
            r, c = s.shape
            shape = (N_DEV, r, c) if axis is None else (N_DEV * r, c) if axis == 0 else (r, N_DEV * c)
            res.append(jax.ShapeDtypeStruct(shape, s.dtype))
        return res

    def scratch(self):
        return [pltpu.SemaphoreType.DMA((self.n, 7)), pltpu.SemaphoreType.DMA((self.n, 7)),
                pltpu.SemaphoreType.DMA((self.n,))]

    def _ctx(self, ins, outs, sems):
        send_sems, recv_sems, local_sems = sems
        x, y, c = _position()
        me, sibling = (x, y, c), (x, y, 1 - c)
        chips = [(1 - x, y), (x, 1 - y), (1 - x, 1 - y)]

        def lin(px, py, pc):
            return 4 * px + 2 * py + pc

        def place(a, block):
            return _block_of(outs[a], lin(*block), self.axes[a], self.arrays[a].shape)

        def copy(a, k, block, to, src=None):
            dst = place(a, block)
            return pltpu.make_async_remote_copy(
                src_ref=dst if src is None else src, dst_ref=dst,
                send_sem=send_sems.at[a, k], recv_sem=recv_sems.at[a, k],
                device_id=to, device_id_type=MESH)

        def mine():
            return [pltpu.make_async_copy(ins[a], place(a, me), local_sems.at[a]) for a in range(self.n)]

        def first():
            res = []
            for a in range(self.n):
                res.append(copy(a, 0, me, sibling, src=ins[a]))
                res += [copy(a, 1 + j, me, (*chip, c), src=ins[a]) for j, chip in enumerate(chips)]
            return res

        return c, me, sibling, chips, copy, mine, first

    def start(self, ins, outs, sems):
        _, _, _, _, _, mine, first = self._ctx(ins, outs, sems)
        for cp in mine() + first():
            cp.start()

    def mid(self, ins, outs, sems):
        c, me, sibling, chips, copy, _, _ = self._ctx(ins, outs, sems)
        for j, chip in enumerate(chips):
            for a in range(self.n):
                copy(a, 1 + j, (*chip, c), me).wait_recv()
                copy(a, 4 + j, (*chip, c), sibling).start()

    def finish(self, ins, outs, sems):
        c, me, sibling, chips, copy, mine, first = self._ctx(ins, outs, sems)
        for a in range(self.n):
            copy(a, 0, sibling, me).wait_recv()
            for j, chip in enumerate(chips):
                copy(a, 4 + j, (*chip, 1 - c), me).wait_recv()
        for cp in first():
            cp.wait_send()
        for j, chip in enumerate(chips):
            for a in range(self.n):
                copy(a, 4 + j, (*chip, c), sibling).wait_send()
        for cp in mine():
            cp.wait()


class _Exchange:
    def __init__(self, parts, axes):
        self.n = len(parts)
        self.axes = list(axes)
        self.arrays = list(parts)

    def _piece(self, a):
        r, c = self.arrays[a].shape
        axis = self.axes[a]
        return (r, c) if axis is None else (r // N_DEV, c) if axis == 0 else (r, c // N_DEV)

    def out_shape(self):
        return [jax.ShapeDtypeStruct((N_DEV,) + self._piece(a), self.arrays[a].dtype) for a in range(self.n)]

    def semaphores(self):
        return [pltpu.SemaphoreType.DMA((7 * self.n,)), pltpu.SemaphoreType.DMA((7 * self.n,)),
                pltpu.SemaphoreType.DMA((self.n,))]

    def _ctx(self, ins, outs, sems):
        send_sems, recv_sems, local_sems = sems
        x, y, c = _position()
        me = 4 * x + 2 * y + c

        def src(a, j):
            return ins[a] if self.axes[a] is None else _block_of(ins[a], j, self.axes[a], self._piece(a))

        def dst(a, j):
            return outs[a].at[j]

        def local():
            return [pltpu.make_async_copy(src(a, me), dst(a, me), local_sems.at[a]) for a in range(self.n)]

        def remote(inbound):
            res = []
            for a in range(self.n):
                for k in range(1, N_DEV):
                    peer = (1 - x if k & 4 else x, 1 - y if k & 2 else y, 1 - c if k & 1 else c)
                    plin = 4 * peer[0] + 2 * peer[1] + peer[2]
                    res.append(pltpu.make_async_remote_copy(
                        src_ref=src(a, plin), dst_ref=dst(a, plin if inbound else me),
                        send_sem=send_sems.at[7 * a + k - 1], recv_sem=recv_sems.at[7 * a + k - 1],
                        device_id=peer, device_id_type=MESH))
            return res

        return local, remote

    def start(self, ins, outs, sems):
        local, remote = self._ctx(ins, outs, sems)
        for cp in local() + remote(False):
            cp.start()

    def finish(self, ins, outs, sems):
        local, remote = self._ctx(ins, outs, sems)
        for cp in remote(True):
            cp.wait_recv()
        for cp in remote(False):
            cp.wait_send()
        for cp in local():
            cp.wait()


def _exchange_start(rider, name):
    n = rider.n
    parts = rider.arrays
    lands = [lax.empty(s.shape, s.dtype) for s in rider.out_shape()]
    hbm = pl.BlockSpec(memory_space=pltpu.HBM)
    sem = pl.BlockSpec(memory_space=pltpu.SEMAPHORE)

    def body(*refs):
        ins, sems = refs[:n], refs[2 * n:2 * n + 3]
        outs, token = refs[2 * n + 3 + n:2 * n + 3 + 2 * n], refs[-1]
        rider.start(ins, outs, sems)
        token[...] = jnp.zeros_like(token)

    res = pl.pallas_call(
        body, name=name,
        out_shape=rider.semaphores() + [pltpu.HBM(p.shape, p.dtype) for p in parts]
                  + [pltpu.HBM(z.shape, z.dtype) for z in lands] + [jax.ShapeDtypeStruct((8, 128), F32)],
        in_specs=[hbm] * (2 * n), out_specs=[sem] * 3 + [hbm] * (2 * n) + [pl.BlockSpec(memory_space=pltpu.VMEM)],
        input_output_aliases={i: 3 + i for i in range(2 * n)},
        compiler_params=pltpu.CompilerParams(has_side_effects=pltpu.SideEffectType.DATAFLOW_SIDE_EFFECTING),
    )(*[pltpu.with_memory_space_constraint(a, pltpu.HBM) for a in parts + lands])
    return res[:3], res[3:3 + n], res[3 + n:3 + 2 * n], res[-1]


def _exchange_wait(rider, started, after, name):
    n = rider.n
    sems, parts, lands, _ = started
    hbm = pl.BlockSpec(memory_space=pltpu.HBM)
    sem = pl.BlockSpec(memory_space=pltpu.SEMAPHORE)

    def body(*refs):
        rider.finish(refs[:n], refs[n:2 * n], refs[2 * n:2 * n + 3])

    res = pl.pallas_call(
        body, name=name, out_shape=[pltpu.HBM(a.shape, a.dtype) for a in list(parts) + list(lands)],
        in_specs=[hbm] * (2 * n) + [sem] * 3 + [ANY] * len(after), out_specs=[hbm] * (2 * n),
        input_output_aliases={i: i for i in range(2 * n)},
        compiler_params=pltpu.CompilerParams(has_side_effects=pltpu.SideEffectType.DATAFLOW_SIDE_EFFECTING),
    )(*parts, *lands, *sems, *after)
    return list(res[n:])


def _pcall(body, *, name, grid, in_specs, out_specs, out_shape, scratch_shapes=(), semantics, vmem_mb, rider=None,
           aliases=None, after=()):
    in_specs, out_specs, out_shape = list(in_specs), list(out_specs), list(out_shape)
    scratch_shapes = list(scratch_shapes)
    aliases = dict(aliases or {})
    if rider is None:
        n_in, after = len(in_specs), list(after)

        def plain(*refs):
            body(*refs[:n_in], *refs[n_in + len(after):])

        call = pl.pallas_call(plain if after else body, name=name, grid=grid, in_specs=in_specs + [ANY] * len(after),
                              out_specs=out_specs, out_shape=out_shape, scratch_shapes=scratch_shapes,
                              input_output_aliases=aliases, compiler_params=_params(semantics, vmem_mb))
        return lambda *args: (list(call(*args, *after)), None)
    n_in, n_out, n_scr = len(in_specs), len(out_specs), len(scratch_shapes)
    r_in, r_shapes = len(rider.arrays), rider.out_shape()
    r_out = len(r_shapes)
    aliases.update({n_in + i: n_out + o for i, o in rider.alias_pairs})
    total = math.prod(grid)
    mid_step = total - 1 if rider.has_mid and rider.late else (3 * total) // 4

    def wrapped(*refs):
        bounds = [0, n_in, r_in, n_out, r_out, n_scr]
        for i in range(1, len(bounds)):
            bounds[i] += bounds[i - 1]
        a, ra, o, ro, s = (refs[bounds[i]:bounds[i + 1]] for i in range(5))
        rs = refs[bounds[5]:]
        step = pl.program_id(0)
        for k in range(1, len(grid)):
            step = step * grid[k] + pl.program_id(k)
        pl.when(step == 0)(lambda: rider.start(ra, ro, rs))
        body(*a, *o, *s)
        if rider.has_mid:
            pl.when(step == mid_step)(lambda: rider.mid(ra, ro, rs))
        pl.when(step == total - 1)(lambda: rider.finish(ra, ro, rs))

    call = pl.pallas_call(
        wrapped, name=name, grid=grid, in_specs=in_specs + [ANY] * r_in, out_specs=out_specs + [ANY] * r_out,
        out_shape=out_shape + r_shapes, scratch_shapes=scratch_shapes + rider.scratch(),
        input_output_aliases=aliases, compiler_params=_params(("arbitrary",) * len(grid), vmem_mb))

    def run(*args):
        res = call(*args, *rider.arrays)
        return list(res[:n_out]), list(res[n_out:])

    return run


def _norm_matmul(x, g, w, *, name, out_dtype, tb, bn, relu=False, save_h=False, rider=None):
    t, d = x.shape
    n = w.shape[1]

    def body(x_ref, g_ref, w_ref, o_ref, *rest):
        h_scr = rest[-1]

        @pl.when(pl.program_id(1) == 0)
        def _():
            h = _rms_fwd(x_ref[...], g_ref[...])[0].astype(BF16)
            h_scr[...] = h
            if save_h:
                rest[0][...] = h

        acc = _dot(h_scr[...], w_ref[...])
        if relu:
            acc = jnp.maximum(acc, 0.0)
        o_ref[...] = acc.astype(out_dtype)

    out_shape = [jax.ShapeDtypeStruct((t, n), out_dtype)]
    out_specs = [pl.BlockSpec((tb, bn), lambda i, j: (i, j))]
    if save_h:
        out_shape.append(jax.ShapeDtypeStruct((t, d), BF16))
        out_specs.append(pl.BlockSpec((tb, d), lambda i, j: (i, 0)))
    res, extra = _pcall(
        body, name=name, grid=(t // tb, n // bn),
        in_specs=[pl.BlockSpec((tb, d), lambda i, j: (i, 0)),
                  pl.BlockSpec((1, d), lambda i, j: (0, 0)),
                  pl.BlockSpec((d, bn), lambda i, j: (0, j))],
        out_specs=out_specs, out_shape=out_shape,
        scratch_shapes=[pltpu.VMEM((tb, d), BF16)],
        semantics=("parallel", "arbitrary"), vmem_mb=48, rider=rider,
    )(x, g, w)
    res = res if save_h else res[0]
    return res if rider is None else (res, extra)


def _proj(x, g, w, *, tb, rider=None):
    t, d = x.shape
    half = w.shape[1] // 2

    def body(x_ref, g_ref, w_ref, qkv_ref, gates_ref, h_ref, h_scr):
        j = pl.program_id(1)

        @pl.when(j == 0)
        def _():
            h = _rms_fwd(x_ref[...], g_ref[...])[0].astype(BF16)
            h_scr[...] = h
            h_ref[...] = h

        acc = _dot(h_scr[...], w_ref[...])

        @pl.when(j == 0)
        def _():
            qkv_ref[...] = acc

        @pl.when(j == 1)
        def _():
            gates_ref[...] = acc.astype(BF16)

    tok = lambda c: pl.BlockSpec((tb, c), lambda i, j: (i, 0))
    res, extra = _pcall(
        body, name="proj", grid=(t // tb, 2),
        in_specs=[tok(d), pl.BlockSpec((1, d), lambda i, j: (0, 0)), pl.BlockSpec((d, half), lambda i, j: (0, j))],
        out_specs=[tok(half), tok(half), tok(d)],
        out_shape=[jax.ShapeDtypeStruct((t, half), F32), jax.ShapeDtypeStruct((t, half), BF16),
                   jax.ShapeDtypeStruct((t, d), BF16)],
        scratch_shapes=[pltpu.VMEM((tb, d), BF16)],
        semantics=("parallel", "arbitrary"), vmem_mb=48, rider=rider,
    )(x, g, w)
    return res if rider is None else (res, extra)


def _matmul_nt_normbwd(dy, w, x, g, dres, *, name, tb, also_bf16=False, to_natural=False, after=()):
    t, d = x.shape
    stacked = dy.ndim == 3
    n_i = SEG // TI
    if to_natural:
        tb = N_RES * TI

    def body(dy_ref, w_ref, x_ref, g_ref, dres_ref, *rest):
        rest = list(rest)
        dx_ref = rest.pop(0)
        dxb_ref = rest.pop(0) if also_bf16 else None
        gg_ref = rest.pop(0)
        i = pl.program_id(0)

        def rows(ref, *lead):
            v = ref[lead] if lead else ref[...]
            return v[0].reshape(tb, v.shape[-1]) if to_natural else v

        if stacked:
            kb = dy_ref.shape[-1]
            dh = _dot_nt(rows(dy_ref, 0), w_ref[:, 0:kb])
            for s in range(1, dy_ref.shape[0]):
                dh = dh + _dot_nt(rows(dy_ref, s), w_ref[:, s * kb:(s + 1) * kb])
        else:
            dh = _dot_nt(rows(dy_ref), w_ref[...])
        g_v = g_ref[...]
        _, xh, r = _rms_fwd(rows(x_ref), g_v)
        dx = _rms_bwd(dh, xh, r, g_v) + rows(dres_ref)
        if to_natural:
            scr = rest.pop(0)
            for cb in range(d // BLK):
                cols = slice(cb * BLK, (cb + 1) * BLK)
                slab = scr.at[cb]
                for res in range(N_RES):
                    slab[pl.ds(res, TI, stride=N_RES), :] = dx[res * TI:(res + 1) * TI, cols]
                dx_ref[:, cols] = slab[...]
        else:
            dx_ref[...] = dx
        if also_bf16:
            dxb_ref[...] = dx.astype(BF16)
        part = jnp.sum(dh * xh, axis=0, keepdims=True)

        @pl.when(i == 0)
        def _():
            gg_ref[...] = part

        @pl.when(i != 0)
        def _():
            gg_ref[...] += part

    tok = pl.BlockSpec((tb, d), lambda i: (i, 0))
    row = pl.BlockSpec((1, d), lambda i: (0, 0))
    if to_natural:
        act = pl.BlockSpec((1, N_RES, TI, d), lambda i: (i // n_i, 0, i % n_i, 0))
        dy_spec = pl.BlockSpec((dy.shape[0], 1, N_RES, TI, dy.shape[2]), lambda i: (0, i // n_i, 0, i % n_i, 0))
        dy, x, dres = dy.reshape(dy.shape[0], t // HALF, N_RES, SEG, dy.shape[2]), _x4(x), _x4(dres)
    elif stacked:
        act, dy_spec = tok, pl.BlockSpec((dy.shape[0], tb, dy.shape[2]), lambda i: (0, i, 0))
    else:
        act, dy_spec = tok, pl.BlockSpec((tb, dy.shape[1]), lambda i: (i, 0))
    in_specs = [dy_spec, pl.BlockSpec(w.shape, lambda i: (0, 0)), act, row, act]
    out_specs = [tok] + ([tok] if also_bf16 else []) + [row]
    out_shape = ([jax.ShapeDtypeStruct((t, d), F32)] + ([jax.ShapeDtypeStruct((t, d), BF16)] if also_bf16 else [])
                 + [jax.ShapeDtypeStruct((1, d), F32)])
    res, _ = _pcall(
        body, name=name, grid=(t // tb,), in_specs=in_specs, out_specs=out_specs, out_shape=out_shape,
        scratch_shapes=[pltpu.VMEM((d // BLK, tb, BLK), F32)] if to_natural else [],
        semantics=("arbitrary",), vmem_mb=56, after=after,
    )(dy, w, x, g, dres)
    return res


def _matmul_tn(a, b, *, name, bm, bn, square_a=False, after=()):
    t, m = a.shape
    stacked = b.ndim == 3
    n = b.shape[0] * bn if stacked else b.shape[1]

    def body(a_ref, b_ref, o_ref):
        av = a_ref[...]
        if square_a:
            av = av.astype(F32)
            av = (av * av).astype(BF16)
        o_ref[...] = _dot_tn(av, b_ref[...]).astype(BF16)

    res, _ = _pcall(
        body, name=name, grid=(m // bm, n // bn),
        in_specs=[pl.BlockSpec((t, bm), lambda i, j: (0, i)),
                  pl.BlockSpec((None, t, bn), lambda i, j: (j, 0, 0)) if stacked
                  else pl.BlockSpec((t, bn), lambda i, j: (0, j))],
        out_specs=[pl.BlockSpec((bm, bn), lambda i, j: (i, j))], out_shape=[jax.ShapeDtypeStruct((m, n), BF16)],
        semantics=("parallel", "parallel"), vmem_mb=56, after=after,
    )(a, b)
    return res[0]


N_RES = 16
SEG = 128
HALF = N_RES * SEG
TI = 32
HALO = 16


def _x4(a):
    return a.reshape(a.shape[0] // HALF, N_RES, SEG, a.shape[1])


def _reorder(arrays, name, rider=None):
    t, c = arrays[0].shape
    n = len(arrays)
    n_i = SEG // TI

    def body(*refs):
        scr = refs[-1]
        for i_ref, o_ref in zip(refs[:n], refs[n:2 * n]):
            for cb in range(c // BLK):
                cols = slice(cb * BLK, (cb + 1) * BLK)
                slab = scr.at[cb]
                slab[...] = i_ref[:, cols]
                for r in range(N_RES):
                    o_ref[0, r, :, cols] = slab[pl.ds(r, TI, stride=N_RES), :]

    res, extra = _pcall(
        body, name=name, grid=(t // (TI * N_RES),),
        in_specs=[pl.BlockSpec((TI * N_RES, c), lambda s: (s, 0))] * n,
        out_specs=[pl.BlockSpec((1, N_RES, TI, c), lambda s: (s // n_i, 0, s % n_i, 0))] * n,
        out_shape=[jax.ShapeDtypeStruct((t // HALF, N_RES, SEG, c), F32)] * n,
        scratch_shapes=[pltpu.VMEM((c // BLK, TI * N_RES, BLK), F32)],
        semantics=("parallel",), vmem_mb=32, rider=rider,
    )(*arrays)
    res = [r.reshape(t, c) for r in res]
    return res if rider is None else (res, extra)


_PATTERNS = ((1, 16, 8, SEG), (4, 4, 32, 4 * SEG), (16, 1, SEG, 0))
_FIRST = {1: 1, 4: 4, 16: 16}


def _group_rows(d, g):
    a = g >> 4
    if d == 16:
        base = a * HALF + (g & 15) * SEG
        prev = base - HALF
    elif d == 4:
        c = (g >> 2) & 3
        base = a * HALF + (g & 3) * SEG + c * 32
        prev = jnp.where(c > 0, base - 32, base - HALF + 96)
    else:
        c = g & 15
        base = a * HALF + c * 8
        prev = jnp.where(c > 0, base - 8, base - HALF + 120)
    return base, prev


def _load_rows(ref, base, n, rows, stride):
    parts = [ref[pl.ds(pl.multiple_of(base + j * stride, 8), rows), :] for j in range(n)]
    return parts[0] if n == 1 else jnp.concatenate(parts, axis=0)


def _store_rows(ref, base, val, n, rows, stride, add=False):
    for j in range(n):
        sl = pl.ds(pl.multiple_of(base + j * stride, 8), rows)
        piece = val[j * rows:(j + 1) * rows, :]
        if add:
            ref[sl, :] += piece
        else:
            ref[sl, :] = piece


def _band_bias(n, rows):
    shift = rows.bit_length() - 1
    lq = lax.broadcasted_iota(jnp.int32, (BLK, BLK), 0)
    lk = lax.broadcasted_iota(jnp.int32, (BLK, BLK), 1)
    iq = (lq & (rows - 1)) * n + (lq >> shift)
    ik = (lk & (rows - 1)) * n + (lk >> shift)
    zero = jnp.zeros((BLK, BLK), F32)
    return jnp.where(ik >= iq, zero, NEG_INF), jnp.where(ik <= iq, zero, NEG_INF)


def _set_bias(bias_scr, n, rows):
    prev_b, cur_b = _band_bias(n, rows)
    for half in range(2):
        bias_scr[half * BLK:(half + 1) * BLK, 0:BLK] = prev_b
        bias_scr[half * BLK:(half + 1) * BLK, BLK:2 * BLK] = cur_b


SCALE = 1.0 / math.sqrt(HEAD_DIM)


def _head_consts(value=1.0):
    lane_lo = lax.broadcasted_iota(jnp.int32, (BLK, BLK), 1) < HEAD_DIM
    return lane_lo, [jnp.where(lane_lo, value, 0.0).astype(BF16), jnp.where(lane_lo, 0.0, value).astype(BF16)]


def _stack_heads(v, head_mask):
    return jnp.concatenate([v * head_mask[0], v * head_mask[1]], axis=0)


def _unstack_heads(v2, lane_lo):
    return jnp.where(lane_lo, v2[:BLK], v2[BLK:])


def _rows_per_head(v, lane_lo):
    rolled = pltpu.roll(v, HEAD_DIM, axis=1)
    return jnp.concatenate([jnp.where(lane_lo, v, rolled), jnp.where(lane_lo, rolled, v)], axis=0)


WIDTH = 4


def _loop(lo, hi, fn, width=None):
    if width is None:
        def body(g, carry):
            fn(g)
            return carry

        if hi > lo:
            lax.fori_loop(lo, hi, body, 0)
        return
    while hi > lo:
        trips = (hi - lo) // width
        if trips:
            def body(i, carry, lo=lo, width=width):
                fn([lo + width * i + j for j in range(width)])
                return carry

            lax.fori_loop(0, trips, body, 0)
            lo += trips * width
        width = max(1, width // 2)


def _mix_weights(l1, l2, l3):
    mx = jnp.maximum(jnp.maximum(l1, l2), l3)
    e1, e2, e3 = jnp.exp(l1 - mx), jnp.exp(l2 - mx), jnp.exp(l3 - mx)
    inv = 1.0 / (e1 + e2 + e3)
    return e1 * inv, e2 * inv, e3 * inv


def _attention_fwd(qkv, rider=None):
    t = qkv.shape[0]
    groups = 16 * (t // HALF)

    def body(q_ref, k_ref, v_ref, attn_ref, l1_ref, l2_ref, l3_ref, o_scr, bias_scr):
        lane_lo, q_mask = _head_consts(SCALE)
        l_refs = (l1_ref, l2_ref, l3_ref)
        for p, (d, n, rows, stride) in enumerate(_PATTERNS):
            _set_bias(bias_scr, n, rows)
            o_p, l_p = o_scr.at[p], l_refs[p]

            def block(gs, has_prev):
                at = [_group_rows(d, g) for g in gs]

                def load(ref, b):
                    return _load_rows(ref, b, n, rows, stride).astype(BF16)

                q2 = [_stack_heads(load(q_ref, b), q_mask) for b, _ in at]
                k2 = [load(k_ref, b) for b, _ in at]
                v2 = [load(v_ref, b) for b, _ in at]
                if has_prev:
                    k2 = [jnp.concatenate([load(k_ref, pv), k], axis=0) for (_, pv), k in zip(at, k2)]
                    v2 = [jnp.concatenate([load(v_ref, pv), v], axis=0) for (_, pv), v in zip(at, v2)]
                s = [_dot_nt(q, k) for q, k in zip(q2, k2)]
                s = [x + (bias_scr[...] if has_prev else bias_scr[:, BLK:2 * BLK]) for x in s]
                mx = [jnp.max(x, axis=1, keepdims=True) for x in s]
                e = [jnp.exp(x - m) for x, m in zip(s, mx)]
                den = [jnp.sum(x, axis=1, keepdims=True) for x in e]
                o2 = [_dot(x.astype(BF16), v) * (1.0 / dn) for x, v, dn in zip(e, v2, den)]
                lse2 = [jnp.broadcast_to(m + jnp.log(dn), (2 * BLK, BLK)) for m, dn in zip(mx, den)]
                for (b, _), o, l in zip(at, o2, lse2):
                    _store_rows(o_p, b, _unstack_heads(o, lane_lo), n, rows, stride)
                    _store_rows(l_p, b, _unstack_heads(l, lane_lo), n, rows, stride)

            _loop(0, _FIRST[d], lambda gs: block(gs, False), width=2 * WIDTH)
            _loop(_FIRST[d], groups, lambda gs: block(gs, True), width=2 * WIDTH)

        def mix(i):
            sl = pl.ds(pl.multiple_of(i * 256, 256), 256)
            w = _mix_weights(l1_ref[sl, :], l2_ref[sl, :], l3_ref[sl, :])
            attn_ref[sl, :] = w[0] * o_scr[0, sl, :] + w[1] * o_scr[1, sl, :] + w[2] * o_scr[2, sl, :]

        _loop(0, t // 256, mix)

    def col(c0):
        return pl.BlockSpec((t, BLK), lambda hp: (0, c0 + hp))

    res, extra = _pcall(
        body, name="attention_fwd", grid=(4,), in_specs=[col(0), col(4), col(8)], out_specs=[col(0)] * 4,
        out_shape=[jax.ShapeDtypeStruct((t, 512), F32)] * 4,
        scratch_shapes=[pltpu.VMEM((3, t, BLK), F32), pltpu.VMEM((2 * BLK, 2 * BLK), F32)],
        semantics=("parallel",), vmem_mb=48, rider=rider,
    )(qkv, qkv, qkv)
    return res if rider is None else (res, extra)


def _attention_bwd(qkv, dattn, dsum, lses, dproj):
    t = qkv.shape[0]
    groups = 16 * (t // HALF)

    def body(q_ref, k_ref, v_ref, da_ref, ds_ref, l1_ref, l2_ref, l3_ref, kept_ref, out_ref, acc, bias_scr):
        del kept_ref
        lane_lo, head_mask = _head_consts()
        q_mask = _head_consts(SCALE)[1]
        l_refs = (l1_ref, l2_ref, l3_ref)

        def clear(i):
            sl = pl.ds(pl.multiple_of(i * 512, 512), 512)
            for s in range(3):
                acc[s, sl, :] = jnp.zeros((512, BLK), F32)

        _loop(0, t // 512, clear)
        dq_acc, dk_acc, dv_acc = acc.at[0], acc.at[1], acc.at[2]
        for p, (d, n, rows, stride) in enumerate(_PATTERNS):
            _set_bias(bias_scr, n, rows)

            def block(gs, has_prev):
                at = [_group_rows(d, g) for g in gs]

                def load(ref, b):
                    return _load_rows(ref, b, n, rows, stride)

                def put(ref, b, val):
                    _store_rows(ref, b, val, n, rows, stride, add=True)

                def wide(x):
                    return jnp.concatenate([x, x], axis=1) if has_prev else x

                lse = [[load(ref, b) for ref in l_refs] for b, _ in at]
                w = [_mix_weights(*ls)[p] for ls in lse]
                do2 = [_stack_heads((wg * load(da_ref, b)).astype(BF16), head_mask) for wg, (b, _) in zip(w, at)]
                dl2 = [wide(_rows_per_head(wg * load(ds_ref, b), lane_lo)) for wg, (b, _) in zip(w, at)]
                lse2 = [wide(_rows_per_head(ls[p], lane_lo)) for ls in lse]
                q2 = [_stack_heads(load(q_ref, b).astype(BF16), q_mask) for b, _ in at]
                k2 = [load(k_ref, b).astype(BF16) for b, _ in at]
                v2 = [load(v_ref, b).astype(BF16) for b, _ in at]
                if has_prev:
                    k2 = [jnp.concatenate([load(k_ref, pv).astype(BF16), k], axis=0) for (_, pv), k in zip(at, k2)]
                    v2 = [jnp.concatenate([load(v_ref, pv).astype(BF16), v], axis=0) for (_, pv), v in zip(at, v2)]
                s = [_dot_nt(q, k) for q, k in zip(q2, k2)]
                dp = [_dot_nt(do, v) for do, v in zip(do2, v2)]
                pr = [jnp.exp(x + (bias_scr[...] if has_prev else bias_scr[:, BLK:2 * BLK]) - l)
                      for x, l in zip(s, lse2)]
                ds = [(pg * (x - dl)).astype(BF16) for pg, x, dl in zip(pr, dp, dl2)]
                dq2 = [_dot(x, k) * SCALE for x, k in zip(ds, k2)]
                dk2 = [_dot_tn(x, q) for x, q in zip(ds, q2)]
                dv2 = [_dot_tn(pg.astype(BF16), do) for pg, do in zip(pr, do2)]
                for (b, pv), dq, dk, dv in zip(at, dq2, dk2, dv2):
                    put(dq_acc, b, _unstack_heads(dq, lane_lo))
                    if has_prev:
                        put(dk_acc, pv, dk[:BLK])
                        put(dv_acc, pv, dv[:BLK])
                        put(dk_acc, b, dk[BLK:])
                        put(dv_acc, b, dv[BLK:])
                    else:
                        put(dk_acc, b, dk)
                        put(dv_acc, b, dv)

            _loop(0, _FIRST[d], lambda gs: block(gs, False), width=WIDTH)
            _loop(_FIRST[d], groups, lambda gs: block(gs, True), width=WIDTH)

        def emit(i):
            sl = pl.ds(pl.multiple_of(i * 512, 512), 512)
            for s in range(3):
                out_ref[s, sl, :] = acc[s, sl, :].astype(BF16)

        _loop(0, t // 512, emit)

    def col(c0):
        return pl.BlockSpec((t, BLK), lambda hp: (0, c0 + hp))

    res, _ = _pcall(
        body, name="attention_bwd", grid=(4,),
        in_specs=[col(0), col(4), col(8)] + [col(0)] * 5 + [ANY],
        out_specs=[pl.BlockSpec((3, t, BLK), lambda hp: (0, 0, hp))],
        out_shape=[jax.ShapeDtypeStruct(dproj.shape, BF16)],
        scratch_shapes=[pltpu.VMEM((3, t, BLK), F32), pltpu.VMEM((2 * BLK, 2 * BLK), F32)],
        semantics=("parallel",), vmem_mb=56, aliases={8: 0},
    )(qkv, qkv, qkv, dattn, dsum, *lses, dproj)
    return res[0]


def _order_specs(t):
    n_i = SEG // TI
    nblk = (t // HALF) * n_i
    per = TI // HALO

    def main(c, col=0):
        return pl.BlockSpec((1, N_RES, TI, c), lambda s: (s // n_i, 0, s % n_i, col))

    def before(c, col=0):
        return pl.BlockSpec((1, 2, HALO, c), lambda s: (jnp.maximum(s - 1, 0) // n_i, N_RES // 2 - 1,
                                                        (jnp.maximum(s - 1, 0) % n_i) * per + per - 1, col))

    def after(c, col=0):
        return pl.BlockSpec((1, 2, HALO, c), lambda s: (jnp.minimum(s + 1, nblk - 1) // n_i, 0,
                                                        (jnp.minimum(s + 1, nblk - 1) % n_i) * per, col))

    return nblk, main, before, after


def _shift_in(v, row_in, up):
    rows = v.shape[0]
    idx = lax.broadcasted_iota(jnp.int32, v.shape, 0)
    fill = jnp.broadcast_to(row_in, v.shape)
    if up:
        return jnp.where(idx == rows - 1, fill, pltpu.roll(v, rows - 1, axis=0))
    return jnp.where(idx == 0, fill, pltpu.roll(v, 1, axis=0))


def _taps_behind(u, before):
    s15 = _shift_in(u[N_RES - 1], before[1, HALO - 1:HALO, :], up=False)
    s14 = _shift_in(u[N_RES - 2], before[0, HALO - 1:HALO, :], up=False)
    m1 = jnp.concatenate([s15[None], u[:N_RES - 1]], axis=0)
    m2 = jnp.concatenate([s14[None], s15[None], u[:N_RES - 2]], axis=0)
    return m1, m2


def _taps_ahead(u, after):
    t0 = _shift_in(u[0], after[0, 0:1, :], up=True)
    t1 = _shift_in(u[1], after[1, 0:1, :], up=True)
    p1 = jnp.concatenate([u[1:], t0[None]], axis=0)
    p2 = jnp.concatenate([u[2:], t0[None], t1[None]], axis=0)
    return p1, p2


def _conv_fwd(gates, before, first, cw):
    gates, before = gates.astype(F32), before.astype(F32)
    bg, cg, xc = gates[..., 0:512], gates[..., 512:1024], gates[..., 1024:1536]
    u = cg * xc
    ub = before[..., 512:1024] * before[..., 1024:1536]
    ub = jnp.where(first, jnp.zeros_like(ub), ub)
    m1, m2 = _taps_behind(u, ub)
    conv = m2 * cw[0:1, :] + m1 * cw[1:2, :] + u * cw[2:3, :]
    return bg, u, m1, m2, conv


def _sum_tokens(v):
    return jnp.sum(jnp.sum(v, axis=0), axis=0, keepdims=True)


def _mixer_fwd(x, attn, gates, cw, g_a, g_c, w_out):
    t, d = x.shape
    nblk, main, before, _ = _order_specs(t)
    rows = N_RES * TI

    def body(x_ref, at_ref, gt_ref, gb_ref, cw_ref, ga_ref, gc_ref, wa_ref, wb_ref, x1_ref, mg_ref):
        an = _rms_fwd(at_ref[0], ga_ref[...])[0].astype(BF16)
        bg, _, _, _, conv = _conv_fwd(gt_ref[0], gb_ref[0], pl.program_id(0) == 0, cw_ref[...])
        cn = _rms_fwd(bg * conv, gc_ref[...])[0].astype(BF16)
        mg_ref[0, :, :, 0:512] = an
        mg_ref[0, :, :, 512:1024] = cn
        y = _dot(an.reshape(rows, 512), wa_ref[...]) + _dot(cn.reshape(rows, 512), wb_ref[...])
        x1_ref[0] = x_ref[0] + y.reshape(N_RES, TI, d)

    const = lambda r, c, i0=0: pl.BlockSpec((r, c), lambda s: (i0, 0))
    x1, merged = pl.pallas_call(
        body, name="mixer_fwd", grid=(nblk,),
        in_specs=[main(d), main(512), main(1536), before(1536), const(3, 512), const(1, 512), const(1, 512),
                  const(512, d), const(512, d, 1)],
        out_specs=[main(d), main(d)],
        out_shape=[jax.ShapeDtypeStruct(_x4(x).shape, F32), jax.ShapeDtypeStruct(_x4(x).shape, BF16)],
        compiler_params=_params(("parallel",), 48),
    )(_x4(x), _x4(attn), _x4(gates), _x4(gates), cw, g_a, g_c, w_out, w_out)
    return x1.reshape(t, d), merged.reshape(t, d)


def _mixer_bwd(dx1, merged, attn, gates, cw, g_a, g_c, w_out, head_sum):
    t, d = dx1.shape
    nblk, main, before, _ = _order_specs(t)
    rows = N_RES * TI

    def body(dx_ref, mg_ref, at_ref, gt_ref, gb_ref, cw_ref, ga_ref, gc_ref, wa_ref, wb_ref, hs_ref,
             da_ref, dsum_ref, dy_ref, gga_ref, ggc_ref, gw_ref, acc_w):
        s = pl.program_id(0)
        dxb = dx_ref[0].reshape(rows, d).astype(BF16)

        @pl.when(s == 0)
        def _():
            acc_w[...] = jnp.zeros_like(acc_w)

        acc_w[...] += _dot_tn(mg_ref[0].reshape(rows, d), dxb)

        @pl.when(s == nblk - 1)
        def _():
            gw_ref[...] = acc_w[...].astype(BF16)

        dma = _dot_nt(dxb, wa_ref[...]).reshape(N_RES, TI, 512)
        dmc = _dot_nt(dxb, wb_ref[...]).reshape(N_RES, TI, 512)
        attn_v, g_av = at_ref[0], ga_ref[...]
        _, ah, ra = _rms_fwd(attn_v, g_av)
        dattn = _rms_bwd(dma, ah, ra, g_av)
        da_ref[0] = dattn
        z = (dattn * attn_v).reshape(rows, 512)
        hs = hs_ref[...]
        z1 = z.astype(BF16)
        z2 = (z - z1.astype(F32)).astype(BF16)
        dsum_ref[0] = (_dot(z1, hs) + _dot(z2, hs)).reshape(N_RES, TI, 512)
        bg, _, _, _, conv = _conv_fwd(gt_ref[0], gb_ref[0], s == 0, cw_ref[...])
        g_cv = gc_ref[...]
        _, yh, rc = _rms_fwd(bg * conv, g_cv)
        dy_ref[0] = _rms_bwd(dmc, yh, rc, g_cv)
        pa, pc = _sum_tokens(dma * ah), _sum_tokens(dmc * yh)

        @pl.when(s == 0)
        def _():
            gga_ref[...] = pa
            ggc_ref[...] = pc

        @pl.when(s != 0)
        def _():
            gga_ref[...] += pa
            ggc_ref[...] += pc

    const = lambda r, c, i0=0: pl.BlockSpec((r, c), lambda s: (i0, 0))
    shape4 = _x4(attn).shape
    res, _ = _pcall(
        body, name="mixer_bwd", grid=(nblk,),
        in_specs=[main(d), main(d), main(512), main(1536), before(1536), const(3, 512), const(1, 512), const(1, 512),
                  const(512, d), const(512, d, 1), const(512, 512)],
        out_specs=[main(512)] * 3 + [const(1, 512), const(1, 512), const(d, d)],
        out_shape=[jax.ShapeDtypeStruct(shape4, F32)] * 3 + [jax.ShapeDtypeStruct((1, 512), F32)] * 2
        + [jax.ShapeDtypeStruct((d, d), BF16)],
        scratch_shapes=[pltpu.VMEM((d, d), F32)],
        semantics=("arbitrary",), vmem_mb=48,
    )(_x4(dx1), _x4(merged), _x4(attn), _x4(gates), _x4(gates), cw, g_a, g_c, w_out, w_out, head_sum)
    return [r.reshape(t, 512) for r in res[:3]] + res[3:]


def _conv_bwd(dy, gates, cw, after=()):
    t = dy.shape[0]
    nblk, main, before, ahead = _order_specs(t)
    n_i = SEG // TI

    def body(dy_ref, dya_ref, gt_ref, gb_ref, ga_ref, cw_ref, dp_ref, gcw_ref):
        s = pl.program_id(0)
        cw_v, gates_v = cw_ref[...], gt_ref[0]
        bg, u, m1, m2, conv = _conv_fwd(gates_v, gb_ref[0], s == 0, cw_v)
        dy_v = dy_ref[0]
        dconv = dy_v * bg
        dca = dya_ref[0] * ga_ref[0][..., 0:512].astype(F32)
        dca = jnp.where(s == nblk - 1, jnp.zeros_like(dca), dca)
        p1, p2 = _taps_ahead(dconv, dca)
        du = dconv * cw_v[2:3, :] + p1 * cw_v[1:2, :] + p2 * cw_v[0:1, :]
        dp_ref[0, 0] = (dy_v * conv).astype(BF16)
        dp_ref[1, 0] = (du * gates_v[..., 1024:1536].astype(F32)).astype(BF16)
        dp_ref[2, 0] = (du * gates_v[..., 512:1024].astype(F32)).astype(BF16)
        parts = [_sum_tokens(dconv * m2), _sum_tokens(dconv * m1), _sum_tokens(dconv * u)]

        @pl.when(s == 0)
        def _():
            gcw_ref[...] = jnp.zeros_like(gcw_ref)

        for tap in range(3):
            gcw_ref[tap:tap + 1, :] += parts[tap]

    (dproj, gcw), _ = _pcall(
        body, name="conv_bwd", grid=(nblk,),
        in_specs=[main(512), ahead(512), main(1536), before(1536), ahead(1536),
                  pl.BlockSpec((3, 512), lambda s: (0, 0))],
        out_specs=[pl.BlockSpec((3, 1, N_RES, TI, 512), lambda s: (1, s // n_i, 0, s % n_i, 0)),
                   pl.BlockSpec((8, 512), lambda s: (0, 0))],
        out_shape=[jax.ShapeDtypeStruct((6, t // HALF, N_RES, SEG, 512), BF16), jax.ShapeDtypeStruct((8, 512), F32)],
        semantics=("arbitrary",), vmem_mb=40, after=after,
    )(_x4(dy), _x4(dy), _x4(gates), _x4(gates), _x4(gates), cw)
    return dproj.reshape(6, t, 512), gcw


def _xattn_fwd(x1, g, w_q, kv, w_o, *, tb):
    t, d = x1.shape
    hd = d // N_MEM_HEADS
    m = kv.shape[0]

    def body(x_ref, g_ref, wq_ref, k_ref, v_ref, wo_ref, x2_ref, h_ref, q_ref, o_ref):
        xv = x_ref[...]
        h = _rms_fwd(xv, g_ref[...])[0].astype(BF16)
        h_ref[...] = h
        q = _dot(h, wq_ref[...]).astype(BF16)
        q_ref[...] = q
        for hh in range(N_MEM_HEADS):
            sl = slice(hh * hd, (hh + 1) * hd)
            s = _dot_nt(q[:, sl], k_ref[:, sl]) * (1.0 / 16.0)
            e = jnp.exp(s - jnp.max(s, axis=1, keepdims=True))
            p = e / jnp.sum(e, axis=1, keepdims=True)
            o_ref[:, sl] = _dot(p.astype(BF16), v_ref[:, sl]).astype(BF16)
        x2_ref[...] = xv + _dot(o_ref[...], wo_ref[...])

    tok = pl.BlockSpec((tb, d), lambda i: (i, 0))
    full = pl.BlockSpec((d, d), lambda i: (0, 0))
    return pl.pallas_call(
        body, name="xattn_fwd", grid=(t // tb,),
        in_specs=[tok, pl.BlockSpec((1, d), lambda i: (0, 0)), full,
                  pl.BlockSpec((m, d), lambda i: (0, 0)), pl.BlockSpec((m, d), lambda i: (0, 1)), full],
        out_specs=[tok] * 4,
        out_shape=[jax.ShapeDtypeStruct((t, d), F32)] + [jax.ShapeDtypeStruct((t, d), BF16)] * 3,
        compiler_params=_params(("parallel",), 48),
    )(x1, g, w_q, kv, kv, w_o)


def _xattn_bwd(dx2, x1, g, q, w_q, kv, w_o, *, tb):
    t, d = x1.shape
    hd = d // N_MEM_HEADS
    m = kv.shape[0]

    def body(dx2_ref, x_ref, g_ref, q_ref, wq_ref, k_ref, v_ref, wo_ref,
             dx1_ref, dq_ref, dk_ref, dv_ref, gg_ref):
        i = pl.program_id(0)

        @pl.when(i == 0)
        def _():
            dk_ref[...] = jnp.zeros_like(dk_ref)
            dv_ref[...] = jnp.zeros_like(dv_ref)

        dx2 = dx2_ref[...]
        do = _dot_nt(dx2.astype(BF16), wo_ref[...]).astype(BF16)
        for hh in range(N_MEM_HEADS):
            sl = slice(hh * hd, (hh + 1) * hd)
            qh, kh, vh, doh = q_ref[:, sl], k_ref[:, sl], v_ref[:, sl], do[:, sl]
            s = _dot_nt(qh, kh) * (1.0 / 16.0)
            e = jnp.exp(s - jnp.max(s, axis=1, keepdims=True))
            p = e / jnp.sum(e, axis=1, keepdims=True)
            dp = _dot_nt(doh, vh)
            ds = (p * (dp - jnp.sum(dp * p, axis=1, keepdims=True)) * (1.0 / 16.0)).astype(BF16)
            dq_ref[:, sl] = _dot(ds, kh).astype(BF16)
            dk_ref[:, sl] += _dot_tn(ds, qh)
            dv_ref[:, sl] += _dot_tn(p.astype(BF16), doh)
        dh = _dot_nt(dq_ref[...], wq_ref[...])
        g_v = g_ref[...]
        _, xh, r = _rms_fwd(x_ref[...], g_v)
        dx1 = dx2 + _rms_bwd(dh, xh, r, g_v)
        dx1_ref[...] = dx1
        part = jnp.sum(dh * xh, axis=0, keepdims=True)

        @pl.when(i == 0)
        def _():
            gg_ref[...] = part

        @pl.when(i != 0)
        def _():
            gg_ref[...] += part

    tok = pl.BlockSpec((tb, d), lambda i: (i, 0))
    full = pl.BlockSpec((d, d), lambda i: (0, 0))
    acc = pl.BlockSpec((m, d), lambda i: (0, 0))
    res, _ = _pcall(
        body, name="xattn_bwd", grid=(t // tb,),
        in_specs=[tok, tok, pl.BlockSpec((1, d), lambda i: (0, 0)), tok, full,
                  pl.BlockSpec((m, d), lambda i: (0, 0)), pl.BlockSpec((m, d), lambda i: (0, 1)), full],
        out_specs=[tok, tok, acc, acc, pl.BlockSpec((1, d), lambda i: (0, 0))],
        out_shape=[jax.ShapeDtypeStruct((t, d), F32), jax.ShapeDtypeStruct((t, d), BF16),
                   jax.ShapeDtypeStruct((m, d), F32), jax.ShapeDtypeStruct((m, d), F32),
                   jax.ShapeDtypeStruct((1, d), F32)],
        semantics=("arbitrary",), vmem_mb=48,
    )(dx2, x1, g, q, w_q, kv, kv, w_o)
    return res


def _mem_bwd(dk, dv, w_kv, mem, mem_n, g):
    m, d = mem.shape

    def body(dk_ref, dv_ref, w_ref, x_ref, h_ref, g_ref, gw_ref, gg_ref):
        h = h_ref[...]
        dh = jnp.zeros((m, d), F32)
        for i, dy_ref in enumerate((dk_ref, dv_ref)):
            cols = slice(i * d, (i + 1) * d)
            dy = dy_ref[...].astype(BF16)
            gw_ref[:, cols] = _dot_tn(h, dy).astype(BF16)
            dh = dh + _dot_nt(dy, w_ref[:, cols])
        xh = _rms_fwd(x_ref[...], g_ref[...])[1]
        gg_ref[...] = jnp.sum(dh * xh, axis=0, keepdims=True)

    return pl.pallas_call(
        body, name="mem_bwd",
        out_shape=[jax.ShapeDtypeStruct(w_kv.shape, BF16), jax.ShapeDtypeStruct((1, d), F32)],
        compiler_params=pltpu.CompilerParams(vmem_limit_bytes=32 << 20),
    )(dk, dv, w_kv, mem, mem_n, g)


def _mlp_down_loss(a, w_down, x2, tgt, g, *, tb):
    t, d = x2.shape
    f = a.shape[1]

    def body(a_ref, w_ref, x_ref, t_ref, g_ref, dx_ref, dxb_ref, loss_ref, gg_ref):
        i = pl.program_id(0)
        av = a_ref[...].astype(F32)
        x3 = x_ref[...] + _dot((av * av).astype(BF16), w_ref[...])
        g_v = g_ref[...]
        out, xh, r = _rms_fwd(x3, g_v)
        err = out - t_ref[...]
        dout = err * (1.0 / d)
        dx = _rms_bwd(dout, xh, r, g_v)
        dx_ref[...] = dx
        dxb_ref[...] = dx.astype(BF16)
        part = jnp.sum(dout * xh, axis=0, keepdims=True)
        lpart = 0.5 * jnp.sum(jnp.mean(err * err, axis=-1, keepdims=True), axis=0, keepdims=True)
        lpart = jnp.broadcast_to(lpart, loss_ref.shape)

        @pl.when(i == 0)
        def _():
            gg_ref[...] = part
            loss_ref[...] = lpart

        @pl.when(i != 0)
        def _():
            gg_ref[...] += part
            loss_ref[...] += lpart

    tok = pl.BlockSpec((tb, d), lambda i: (i, 0))
    return pl.pallas_call(
        body, name="mlp_down_loss", grid=(t // tb,),
        in_specs=[pl.BlockSpec((tb, f), lambda i: (i, 0)), pl.BlockSpec((f, d), lambda i: (0, 0)), tok, tok,
                  pl.BlockSpec((1, d), lambda i: (0, 0))],
        out_specs=[tok, tok, pl.BlockSpec((8, 128), lambda i: (0, 0)), pl.BlockSpec((1, d), lambda i: (0, 0))],
        out_shape=[jax.ShapeDtypeStruct((t, d), F32), jax.ShapeDtypeStruct((t, d), BF16),
                   jax.ShapeDtypeStruct((8, 128), F32), jax.ShapeDtypeStruct((1, d), F32)],
        compiler_params=_params(("arbitrary",), 56),
    )(a, w_down, x2, tgt, g)


def _mlp_dpre(dx3, w_down, a, *, tb, bn):
    t, d = dx3.shape
    f = a.shape[1]

    def body(dx_ref, w_ref, a_ref, o_ref):
        o_ref[...] = (2.0 * a_ref[...].astype(F32) * _dot_nt(dx_ref[...], w_ref[...])).astype(BF16)

    return pl.pallas_call(
        body, name="mlp_dpre", grid=(t // tb, f // bn),
        in_specs=[pl.BlockSpec((tb, d), lambda i, j: (i, 0)), pl.BlockSpec((bn, d), lambda i, j: (j, 0)),
                  pl.BlockSpec((tb, bn), lambda i, j: (i, j))],
        out_specs=pl.BlockSpec((tb, bn), lambda i, j: (i, j)),
        out_shape=jax.ShapeDtypeStruct((t, f), BF16),
        compiler_params=_params(("parallel", "arbitrary"), 48),
    )(dx3, w_down, a)


def _adamw(gsum, w, m, v):
    m_new = ADAM_B1 * m + (1.0 - ADAM_B1) * gsum
    v_new = ADAM_B2 * v + (1.0 - ADAM_B2) * (gsum * gsum)
    m_hat = m_new / (1.0 - ADAM_B1 ** ADAM_STEP)
    v_hat = v_new / (1.0 - ADAM_B2 ** ADAM_STEP)
    delta = -ADAM_LR * (m_hat / (jnp.sqrt(v_hat) + ADAM_EPS) + ADAM_WD * w)
    return delta, m_new, v_new


def _sum_adamw(shards, *, name, tr):
    r, c = shards[0][1].shape
    n = len(shards)

    def body(*refs):
        for s in range(n):
            p_ref, w_ref, m_ref, v_ref = refs[4 * s:4 * s + 4]
            g_ref, d_ref, mo_ref, vo_ref = refs[4 * (n + s):4 * (n + s) + 4]
            g = p_ref[0].astype(F32)
            for k in range(1, N_DEV):
                g = g + p_ref[k].astype(F32)
            g_ref[...] = g
            d_ref[...], mo_ref[...], vo_ref[...] = _adamw(g, w_ref[...], m_ref[...], v_ref[...])

    blk = pl.BlockSpec((tr, c), lambda i: (i, 0))
    res = pl.pallas_call(
        body, name=name, grid=(r // tr,),
        in_specs=[pl.BlockSpec((N_DEV, tr, c), lambda i: (0, i, 0)), blk, blk, blk] * n,
        out_specs=[blk] * (4 * n), out_shape=[jax.ShapeDtypeStruct((r, c), F32)] * (4 * n),
        compiler_params=_params(("parallel",), 40),
    )(*[pltpu.with_memory_space_constraint(a, pltpu.HBM) for shard in shards for a in shard])
    return [res[4 * s:4 * s + 4] for s in range(n)]


_GAIN_ROWS = ("g_mix", "g_xattn", "g_mem", "g_mlp", "g_final")
PAIR_ROW = 5
LOSS_ROW = 6
TAPS_ROW = 8
SMALL_ROWS = 16
_SMALL = _GAIN_ROWS + ("g_attn_out", "g_conv_out", "conv_w")
CONV_SHARD = 512 // N_DEV


def _pack_small(gains, gg_attn, gg_conv, gcw, loss_blk):
    def body(*refs):
        o_ref = refs[-1]
        ga_ref, gc_ref, cw_ref, l_ref = refs[len(gains):-1]
        o_ref[...] = jnp.zeros_like(o_ref)
        for i, g_ref in enumerate(refs[:len(gains)]):
            o_ref[i:i + 1, :] = g_ref[...]
        o_ref[PAIR_ROW:PAIR_ROW + 1, 0:512] = ga_ref[...]
        o_ref[PAIR_ROW:PAIR_ROW + 1, 512:1024] = gc_ref[...]
        o_ref[LOSS_ROW:LOSS_ROW + 1, 0:BLK] = l_ref[0:1, :]
        o_ref[TAPS_ROW:SMALL_ROWS, 0:512] = cw_ref[...]

    return pl.pallas_call(body, name="pack_small", out_shape=jax.ShapeDtypeStruct((SMALL_ROWS, 1024), F32))(
        *gains, gg_attn, gg_conv, gcw, loss_blk)


def _update_small(parts, me, w, m, v):
    n = len(_SMALL)

    def body(me_ref, p_ref, *refs):
        ins, loss_ref, outs = refs[:3 * n], refs[3 * n], refs[3 * n + 1:]

        def total(lo, hi):
            s = p_ref[0, lo:hi, :]
            for k in range(1, N_DEV):
                s = s + p_ref[k, lo:hi, :]
            return s

        grads = {k: total(i, i + 1) for i, k in enumerate(_GAIN_ROWS)}
        both = total(PAIR_ROW, PAIR_ROW + 1)
        grads["g_attn_out"], grads["g_conv_out"] = both[:, 0:512], both[:, 512:1024]
        taps = total(TAPS_ROW, SMALL_ROWS)
        mine = jnp.zeros((SMALL_ROWS - TAPS_ROW, BLK), F32)
        for j in range(N_DEV):
            lo = j * CONV_SHARD // BLK * BLK
            blk = taps[:, lo:lo + BLK]
            if j * CONV_SHARD != lo:
                blk = pltpu.roll(blk, BLK - (j * CONV_SHARD - lo), axis=1)
            mine = jnp.where(me_ref[0] == j, blk, mine)
        grads["conv_w"] = mine[0:3, 0:CONV_SHARD]
        loss_ref[...] = total(LOSS_ROW, LOSS_ROW + 1)[:, 0:1]
        for i, k in enumerate(_SMALL):
            g_ref, d_ref, mo_ref, vo_ref = outs[4 * i:4 * i + 4]
            g_ref[...] = grads[k]
            d_ref[...], mo_ref[...], vo_ref[...] = _adamw(grads[k], ins[i][...], ins[n + i][...], ins[2 * n + i][...])

    vmem = pl.BlockSpec(memory_space=pltpu.VMEM)
    args = [d[k] for d in (w, m, v) for k in _SMALL]
    res = pl.pallas_call(
        body, name="update_small",
        in_specs=[pl.BlockSpec(memory_space=pltpu.SMEM)] + [vmem] * (1 + 3 * n),
        out_shape=[jax.ShapeDtypeStruct((1, 1), F32)] + [jax.ShapeDtypeStruct(w[k].shape, F32) for k in _SMALL
                                                         for _ in range(4)],
    )(me, parts, *args)
    return res[0], {k: res[1 + 4 * i:5 + 4 * i] for i, k in enumerate(_SMALL)}


def _head_sum_matrix():
    r = lax.broadcasted_iota(jnp.int32, (512, 512), 0) // HEAD_DIM
    c = lax.broadcasted_iota(jnp.int32, (512, 512), 1) // HEAD_DIM
    return (r == c).astype(BF16)


_SHARD_AXIS = dict(w_in=1, w_out=0, w_q=0, w_kv=1, w_o=0, w_up=1, w_down=0, conv_w=None, small=None)


class _Weights:
    def __init__(self, full, shards=None):
        self.full = dict(full)
        self.shards = shards

    def rider(self, names, late=False):
        if self.shards is None:
            return None
        return _Gather([self.shards[n] for n in names], [_SHARD_AXIS[n] for n in names], late)

    def arrived(self, names, gathered):
        if gathered is not None:
            for n, g in zip(names, gathered):
                self.full[n] = g.transpose(1, 0, 2).reshape(g.shape[1], -1) if n == "conv_w" else g

    def __getitem__(self, name):
        return self.full[name]


class _Grads:
    def __init__(self, distributed):
        self.distributed = distributed
        self.local = {}
        self.pending = {}

    def add(self, name, g):
        self.local[name] = g

    def send(self, *names):
        if not self.distributed:
            return []
        rider = _Exchange([self.local[n] for n in names], [_SHARD_AXIS[n] for n in names])
        started = _exchange_start(rider, "send_" + "_".join(names))
        self.pending[names[0]] = (names, rider, started)
        return [started[3]]

    def wait(self, first_name, after):
        names, rider, started = self.pending.pop(first_name)
        return _exchange_wait(rider, started, after, "wait_" + "_".join(names))


def _ride(fn, *args, rider=None, **kw):
    if rider is None:
        return fn(*args, **kw), None
    return fn(*args, rider=rider, **kw)


def _local_step(x, mem, tgt, gains, weights, grads):
    names = ["w_in", "conv_w"]
    (x, tgt), got = _ride(_reorder, [x, tgt], "reorder_in", rider=weights.rider(names, late=True))
    weights.arrived(names, got)
    w_in, cw = weights["w_in"], weights["conv_w"]

    names = ["w_out", "w_kv"]
    (qkv, gates, h1), got = _ride(_proj, x, gains["g_mix"], w_in, tb=1024, rider=weights.rider(names))
    weights.arrived(names, got)
    names = ["w_q", "w_o", "w_up"]
    (attn, *lses), got = _ride(_attention_fwd, qkv, rider=weights.rider(names))
    weights.arrived(names, got)
    x1, merged = _mixer_fwd(x, attn, gates, cw, gains["g_attn_out"], gains["g_conv_out"], weights["w_out"])
    kv, mem_n = _norm_matmul(mem, gains["g_mem"], weights["w_kv"], name="mem_kv", out_dtype=BF16, tb=mem.shape[0],
                             bn=1024, save_h=True)
    x2, h2, qm, om = _xattn_fwd(x1, gains["g_xattn"], weights["w_q"], kv, weights["w_o"], tb=512)
    w_up = weights["w_up"]
    (a, h3), got = _ride(_norm_matmul, x2, gains["g_mlp"], w_up, name="mlp_up", out_dtype=BF16, tb=1024, bn=2048,
                         relu=True, save_h=True, rider=weights.rider(["w_down"], late=True))
    weights.arrived(["w_down"], got)
    w_down = weights["w_down"]
    dx3, dx3b, loss_blk, gg_final = _mlp_down_loss(a, w_down, x2, tgt, gains["g_final"], tb=512)

    dpre = _mlp_dpre(dx3b, w_down, a, tb=1024, bn=2048)
    grads.add("w_down", _matmul_tn(a, dx3b, name="grad_w_down", bm=512, bn=1024, square_a=True))
    sent = grads.send("w_down")
    grads.add("w_up", _matmul_tn(h3, dpre, name="grad_w_up", bm=1024, bn=1024, after=sent))
    sent = grads.send("w_up")
    dx2, dx2b, gg_mlp = _matmul_nt_normbwd(dpre, w_up, x2, gains["g_mlp"], dx3, name="mlp_dx", tb=512,
                                           also_bf16=True, after=sent)

    grads.add("w_o", _matmul_tn(om, dx2b, name="grad_w_o", bm=512, bn=512))
    dx1, dqm, dk, dv, gg_xattn = _xattn_bwd(dx2, x1, gains["g_xattn"], qm, weights["w_q"], kv, weights["w_o"], tb=512)
    grads.add("w_q", _matmul_tn(h2, dqm, name="grad_w_q", bm=1024, bn=512))
    gw_kv, gg_mem = _mem_bwd(dk, dv, weights["w_kv"], mem, mem_n, gains["g_mem"])
    grads.add("w_kv", gw_kv)

    dattn, dsum, dy, gg_attn, gg_conv, gw_out = _mixer_bwd(dx1, merged, attn, gates, cw, gains["g_attn_out"],
                                                           gains["g_conv_out"], weights["w_out"], _head_sum_matrix())
    grads.add("w_out", gw_out)
    sent = grads.send("w_o", "w_q", "w_kv", "w_out")
    dproj, gcw = _conv_bwd(dy, gates, cw, after=sent)
    dproj = _attention_bwd(qkv, dattn, dsum, lses, dproj)
    grads.add("w_in", _matmul_tn(h1, dproj, name="grad_w_in", bm=1024, bn=512))
    sent = grads.send("w_in")
    grad_x, gg_mix = _matmul_nt_normbwd(dproj, w_in, x, gains["g_mix"], dx1, name="mixer_dx", tb=512,
                                        to_natural=True, after=sent)

    grads.add("small", _pack_small([gg_mix, gg_xattn, gg_mem, gg_mlp, gg_final], gg_attn, gg_conv, gcw, loss_blk))
    return grad_x


_BIG = ("w_in", "w_out", "w_q", "w_kv", "w_o", "w_up", "w_down")


def kernel(x, mem, g_mix, w_in, conv_w, g_attn_out, g_conv_out, w_out, g_xattn, g_mem, w_q_mem, w_kv_mem, w_o_mem, g_mlp, w_up, w_down, g_final, loss_target, m_g_mix, m_w_in, m_conv_w, m_g_attn_out, m_g_conv_out, m_w_out, m_g_xattn, m_g_mem, m_w_q_mem, m_w_kv_mem, m_w_o_mem, m_g_mlp, m_w_up, m_w_down, m_g_final, v_g_mix, v_w_in, v_conv_w, v_g_attn_out, v_g_conv_out, v_w_out, v_g_xattn, v_g_mem, v_w_q_mem, v_w_kv_mem, v_w_o_mem, v_g_mlp, v_w_up, v_w_down, v_g_final):
    d = x.shape[-1]
    me = 4 * lax.axis_index("x") + 2 * lax.axis_index("y") + lax.axis_index("c")
    w_shards = dict(w_in=w_in, w_out=w_out, w_q=w_q_mem, w_kv=w_kv_mem, w_o=w_o_mem, w_up=w_up, w_down=w_down)
    m_shards = dict(w_in=m_w_in, w_out=m_w_out, w_q=m_w_q_mem, w_kv=m_w_kv_mem, w_o=m_w_o_mem, w_up=m_w_up,
                    w_down=m_w_down)
    v_shards = dict(w_in=v_w_in, w_out=v_w_out, w_q=v_w_q_mem, w_kv=v_w_kv_mem, w_o=v_w_o_mem, w_up=v_w_up,
                    w_down=v_w_down)
    gains = dict(g_mix=g_mix, g_attn_out=g_attn_out, g_conv_out=g_conv_out, g_xattn=g_xattn, g_mem=g_mem,
                 g_mlp=g_mlp, g_final=g_final)
    gains2 = {k: v.reshape(1, -1) for k, v in gains.items()}

    shards = {k: w_shards[k].astype(BF16) for k in _BIG}
    shards["conv_w"] = conv_w
    grads = _Grads(distributed=True)
    grad_x = _local_step(x[0], mem[0], loss_target[0], gains2, _Weights({}, shards), grads)

    after = grads.send("small")
    outs = {}
    tiles = dict(w_in=512, w_out=64, w_q=64, w_kv=512, w_o=64, w_up=512, w_down=256)
    for group in (("w_down",), ("w_up",), ("w_o", "w_q", "w_kv", "w_out"), ("w_in",)):
        received = dict(zip(group, grads.wait(group[0], after)))
        same_shape = {}
        for k in group:
            same_shape.setdefault(w_shards[k].shape, []).append(k)
        for names in same_shape.values():
            res = _sum_adamw([(received[k], w_shards[k], m_shards[k], v_shards[k]) for k in names],
                             name="adamw_" + "_".join(names), tr=tiles[names[0]])
            outs.update(zip(names, res))
            after = [res[-1][0]]
    small_received, = grads.wait("small", after)

    m_small = dict(g_mix=m_g_mix, g_attn_out=m_g_attn_out, g_conv_out=m_g_conv_out, g_xattn=m_g_xattn,
                   g_mem=m_g_mem, g_mlp=m_g_mlp, g_final=m_g_final)
    v_small = dict(g_mix=v_g_mix, g_attn_out=v_g_attn_out, g_conv_out=v_g_conv_out, g_xattn=v_g_xattn,
                   g_mem=v_g_mem, g_mlp=v_g_mlp, g_final=v_g_final)
    as_rows = lambda vals, conv: dict({k: a.reshape(1, -1) for k, a in vals.items()}, conv_w=conv)
    loss, small_out = _update_small(small_received, me.reshape(1), as_rows(gains, conv_w),
                                    as_rows(m_small, m_conv_w), as_rows(v_small, v_conv_w))
    small_out = {k: [a.reshape(dict(gains, conv_w=conv_w)[k].shape) for a in res] for k, res in small_out.items()}
    names = {"g_mix": "g_mix", "w_in": "w_in", "conv_w": "conv_w", "g_attn_out": "g_attn_out",
             "g_conv_out": "g_conv_out", "w_out": "w_out", "g_xattn": "g_xattn", "g_mem": "g_mem",
             "w_q_mem": "w_q", "w_kv_mem": "w_kv", "w_o_mem": "w_o", "g_mlp": "g_mlp", "w_up": "w_up",
             "w_down": "w_down", "g_final": "g_final"}
    result = [loss.reshape(()), grad_x[None]]
    for which in range(4):
        for key in names.values():
            result.append(outs[key][which] if key in outs else small_out[key][which])
    return tuple(result)
```

```python
import math

import jax
import jax.numpy as jnp
from jax import lax
from jax.experimental import pallas as pl
from jax.experimental.pallas import tpu as pltpu

F32 = jnp.float32
BF16 = jnp.bfloat16
NORM_EPS = 1e-6
NEG_INF = -1e30
N_DEV = 8
BLK = 128
HEAD_DIM = 64
N_MEM_HEADS = 4
ADAM_LR = 0.001
ADAM_B1 = 0.9
ADAM_B2 = 0.999
ADAM_EPS = 1e-08
ADAM_WD = 0.01
ADAM_STEP = 10
MESH = pl.DeviceIdType.MESH
ANY = pl.BlockSpec(memory_space=pl.ANY)


def _dot(a, b):
    return jnp.dot(a, b, preferred_element_type=F32)


def _dot_nt(a, b):
    return lax.dot_general(a, b, (((1,), (1,)), ((), ())), preferred_element_type=F32)


def _dot_tn(a, b):
    return lax.dot_general(a, b, (((0,), (0,)), ((), ())), preferred_element_type=F32)


def _params(semantics, vmem_mb):
    return pltpu.CompilerParams(dimension_semantics=semantics, vmem_limit_bytes=vmem_mb << 20)


def _rms_fwd(x, g):
    r = lax.rsqrt(jnp.mean(x * x, axis=-1, keepdims=True) + NORM_EPS)
    xh = x * r
    return xh * g, xh, r


def _rms_bwd(dy, xh, r, g):
    gy = dy * g
    return r * (gy - xh * jnp.mean(xh * gy, axis=-1, keepdims=True))


def _position():
    x, y, c = lax.axis_index("x"), lax.axis_index("y"), lax.axis_index("c")
    return x, y, c


def _block_of(ref, j, axis, shard_shape):
    r, c = shard_shape
    if axis is None:
        return ref.at[j]
    if axis == 0:
        return ref.at[pl.ds(j * r, r), :]
    return ref.at[:, pl.ds(j * c, c)]


class _Gather:
    has_mid = True
    alias_pairs = ()

    def __init__(self, shards, axes, late=False):
        self.arrays = list(shards)
        self.axes = list(axes)
        self.late = late
        self.n = len(self.arrays)

    def out_shape(self):
        res = []
        for s, axis in zip(self.arrays, self.axes):
            r, c = s.shape
            shape = (N_DEV, r, c) if axis is None else (N_DEV * r, c) if axis == 0 else (r, N_DEV * c)
            res.append(jax.ShapeDtypeStruct(shape, s.dtype))
        return res

    def scratch(self):
        return [pltpu.SemaphoreType.DMA((self.n, 7)), pltpu.SemaphoreType.DMA((self.n, 7)),
                pltpu.SemaphoreType.DMA((self.n,))]

    def _ctx(self, ins, outs, sems):
        send_sems, recv_sems, local_sems = sems
        x, y, c = _position()
        me, sibling = (x, y, c), (x, y, 1 - c)
        chips = [(1 - x, y), (x, 1 - y), (1 - x, 1 - y)]

        def lin(px, py, pc):
            return 4 * px + 2 * py + pc

        def place(a, block):
            return _block_of(outs[a], lin(*block), self.axes[a], self.arrays[a].shape)

        def copy(a, k, block, to, src=None):
            dst = place(a, block)
            return pltpu.make_async_remote_copy(
                src_ref=dst if src is None else src, dst_ref=dst,
                send_sem=send_sems.at[a, k], recv_sem=recv_sems.at[a, k],
                device_id=to, device_id_type=MESH)

        def mine():
            return [pltpu.make_async_copy(ins[a], place(a, me), local_sems.at[a]) for a in range(self.n)]

        def first():
            res = []
            for a in range(self.n):
                res.append(copy(a, 0, me, sibling, src=ins[a]))
                res += [copy(a, 1 + j, me, (*chip, c), src=ins[a]) for j, chip in enumerate(chips)]
            return res

        return c, me, sibling, chips, copy, mine, first

    def start(self, ins, outs, sems):
        _, _, _, _, _, mine, first = self._ctx(ins, outs, sems)
        for cp in mine() + first():
            cp.start()

    def mid(self, ins, outs, sems):
        c, me, sibling, chips, copy, _, _ = self._ctx(ins, outs, sems)
        for j, chip in enumerate(chips):
            for a in range(self.n):
                copy(a, 1 + j, (*chip, c), me).wait_recv()
                copy(a, 4 + j, (*chip, c), sibling).start()

    def finish(self, ins, outs, sems):
        c, me, sibling, chips, copy, mine, first = self._ctx(ins, outs, sems)
        for a in range(self.n):
            copy(a, 0, sibling, me).wait_recv()
            for j, chip in enumerate(chips):
                copy(a, 4 + j, (*chip, 1 - c), me).wait_recv()
        for cp in first():
            cp.wait_send()
        for j, chip in enumerate(chips):
            for a in range(self.n):
                copy(a, 4 + j, (*chip, c), sibling).wait_send()
        for cp in mine():
            cp.wait()


class _Exchange:
    def __init__(self, parts, axes):
        self.n = len(parts)
        self.axes = list(axes)
        self.arrays = list(parts)

    def _piece(self, a):
        r, c = self.arrays[a].shape
        axis = self.axes[a]
        return (r, c) if axis is None else (r // N_DEV, c) if axis == 0 else (r, c // N_DEV)

    def out_shape(self):
        return [jax.ShapeDtypeStruct((N_DEV,) + self._piece(a), self.arrays[a].dtype) for a in range(self.n)]

    def semaphores(self):
        return [pltpu.SemaphoreType.DMA((7 * self.n,)), pltpu.SemaphoreType.DMA((7 * self.n,)),
                pltpu.SemaphoreType.DMA((self.n,))]

    def _ctx(self, ins, outs, sems):
        send_sems, recv_sems, local_sems = sems
        x, y, c = _position()
        me = 4 * x + 2 * y + c

        def src(a, j):
            return ins[a] if self.axes[a] is None else _block_of(ins[a], j, self.axes[a], self._piece(a))

        def dst(a, j):
            return outs[a].at[j]

        def local():
            return [pltpu.make_async_copy(src(a, me), dst(a, me), local_sems.at[a]) for a in range(self.n)]

        def remote(inbound):
            res = []
            for a in range(self.n):
                for k in range(1, N_DEV):
                    peer = (1 - x if k & 4 else x, 1 - y if k & 2 else y, 1 - c if k & 1 else c)
                    plin = 4 * peer[0] + 2 * peer[1] + peer[2]
                    res.append(pltpu.make_async_remote_copy(
                        src_ref=src(a, plin), dst_ref=dst(a, plin if inbound else me),
                        send_sem=send_sems.at[7 * a + k - 1], recv_sem=recv_sems.at[7 * a + k - 1],
                        device_id=peer, device_id_type=MESH))
            return res

        return local, remote

    def start(self, ins, outs, sems):
        local, remote = self._ctx(ins, outs, sems)
        for cp in local() + remote(False):
            cp.start()

    def finish(self, ins, outs, sems):
        local, remote = self._ctx(ins, outs, sems)
        for cp in remote(True):
            cp.wait_recv()
        for cp in remote(False):
            cp.wait_send()
        for cp in local():
            cp.wait()


def _exchange_start(rider, name):
    n = rider.n
    parts = rider.arrays
    lands = [lax.empty(s.shape, s.dtype) for s in rider.out_shape()]
    hbm = pl.BlockSpec(memory_space=pltpu.HBM)
    sem = pl.BlockSpec(memory_space=pltpu.SEMAPHORE)

    def body(*refs):
        ins, sems = refs[:n], refs[2 * n:2 * n + 3]
        outs, token = refs[2 * n + 3 + n:2 * n + 3 + 2 * n], refs[-1]
        rider.start(ins, outs, sems)
        token[...] = jnp.zeros_like(token)

    res = pl.pallas_call(
        body, name=name,
        out_shape=rider.semaphores() + [pltpu.HBM(p.shape, p.dtype) for p in parts]
                  + [pltpu.HBM(z.shape, z.dtype) for z in lands] + [jax.ShapeDtypeStruct((8, 128), F32)],
        in_specs=[hbm] * (2 * n), out_specs=[sem] * 3 + [hbm] * (2 * n) + [pl.BlockSpec(memory_space=pltpu.VMEM)],
        input_output_aliases={i: 3 + i for i in range(2 * n)},
        compiler_params=pltpu.CompilerParams(has_side_effects=pltpu.SideEffectType.DATAFLOW_SIDE_EFFECTING),
    )(*[pltpu.with_memory_space_constraint(a, pltpu.HBM) for a in parts + lands])
    return res[:3], res[3:3 + n], res[3 + n:3 + 2 * n], res[-1]


def _exchange_wait(rider, started, after, name):
    n = rider.n
    sems, parts, lands, _ = started
    hbm = pl.BlockSpec(memory_space=pltpu.HBM)
    sem = pl.BlockSpec(memory_space=pltpu.SEMAPHORE)

    def body(*refs):
        rider.finish(refs[:n], refs[n:2 * n], refs[2 * n:2 * n + 3])

    res = pl.pallas_call(
        body, name=name, out_shape=[pltpu.HBM(a.shape, a.dtype) for a in list(parts) + list(lands)],
        in_specs=[hbm] * (2 * n) + [sem] * 3 + [ANY] * len(after), out_specs=[hbm] * (2 * n),
        input_output_aliases={i: i for i in range(2 * n)},
        compiler_params=pltpu.CompilerParams(has_side_effects=pltpu.SideEffectType.DATAFLOW_SIDE_EFFECTING),
    )(*parts, *lands, *sems, *after)
    return list(res[n:])


def _pcall(body, *, name, grid, in_specs, out_specs, out_shape, scratch_shapes=(), semantics, vmem_mb, rider=None,
           aliases=None, after=()):
    in_specs, out_specs, out_shape = list(in_specs), list(out_specs), list(out_shape)
    scratch_shapes = list(scratch_shapes)
    aliases = dict(aliases or {})
    if rider is None:
        n_in, after = len(in_specs), list(after)

        def plain(*refs):
            body(*refs[:n_in], *refs[n_in + len(after):])

        call = pl.pallas_call(plain if after else body, name=name, grid=grid, in_specs=in_specs + [ANY] * len(after),
                              out_specs=out_specs, out_shape=out_shape, scratch_shapes=scratch_shapes,
                              input_output_aliases=aliases, compiler_params=_params(semantics, vmem_mb))
        return lambda *args: (list(call(*args, *after)), None)
    n_in, n_out, n_scr = len(in_specs), len(out_specs), len(scratch_shapes)
    r_in, r_shapes = len(rider.arrays), rider.out_shape()
    r_out = len(r_shapes)
    aliases.update({n_in + i: n_out + o for i, o in rider.alias_pairs})
    total = math.prod(grid)
    mid_step = total - 1 if rider.has_mid and rider.late else (3 * total) // 4

    def wrapped(*refs):
        bounds = [0, n_in, r_in, n_out, r_out, n_scr]
        for i in range(1, len(bounds)):
            bounds[i] += bounds[i - 1]
        a, ra, o, ro, s = (refs[bounds[i]:bounds[i + 1]] for i in range(5))
        rs = refs[bounds[5]:]
        step = pl.program_id(0)
        for k in range(1, len(grid)):
            step = step * grid[k] + pl.program_id(k)
        pl.when(step == 0)(lambda: rider.start(ra, ro, rs))
        body(*a, *o, *s)
        if rider.has_mid:
            pl.when(step == mid_step)(lambda: rider.mid(ra, ro, rs))
        pl.when(step == total - 1)(lambda: rider.finish(ra, ro, rs))

    call = pl.pallas_call(
        wrapped, name=name, grid=grid, in_specs=in_specs + [ANY] * r_in, out_specs=out_specs + [ANY] * r_out,
        out_shape=out_shape + r_shapes, scratch_shapes=scratch_shapes + rider.scratch(),
        input_output_aliases=aliases, compiler_params=_params(("arbitrary",) * len(grid), vmem_mb))

    def run(*args):
        res = call(*args, *rider.arrays)
        return list(res[:n_out]), list(res[n_out:])

    return run


def _norm_matmul(x, g, w, *, name, out_dtype, tb, bn, relu=False, save_h=False, rider=None):
    t, d = x.shape
    n = w.shape[1]

    def body(x_ref, g_ref, w_ref, o_ref, *rest):
        h_scr = rest[-1]

        @pl.when(pl.program_id(1) == 0)
        def _():
            h = _rms_fwd(x_ref[...], g_ref[...])[0].astype(BF16)
            h_scr[...] = h
            if save_h:
                rest[0][...] = h

        acc = _dot(h_scr[...], w_ref[...])
        if relu:
            acc = jnp.maximum(acc, 0.0)
        o_ref[...] = acc.astype(out_dtype)

    out_shape = [jax.ShapeDtypeStruct((t, n), out_dtype)]
    out_specs = [pl.BlockSpec((tb, bn), lambda i, j: (i, j))]
    if save_h:
        out_shape.append(jax.ShapeDtypeStruct((t, d), BF16))
        out_specs.append(pl.BlockSpec((tb, d), lambda i, j: (i, 0)))
    res, extra = _pcall(
        body, name=name, grid=(t // tb, n // bn),
        in_specs=[pl.BlockSpec((tb, d), lambda i, j: (i, 0)),
                  pl.BlockSpec((1, d), lambda i, j: (0, 0)),
                  pl.BlockSpec((d, bn), lambda i, j: (0, j))],
        out_specs=out_specs, out_shape=out_shape,
        scratch_shapes=[pltpu.VMEM((tb, d), BF16)],
        semantics=("parallel", "arbitrary"), vmem_mb=48, rider=rider,
    )(x, g, w)
    res = res if save_h else res[0]
    return res if rider is None else (res, extra)


def _proj(h, w, *, tb, rider=None):
    t, d = h.shape
    half = w.shape[1] // 2

    def body(h_ref, w_ref, qkv_ref, gates_ref):
        j = pl.program_id(1)
        acc = _dot(h_ref[...], w_ref[...])

        @pl.when(j == 0)
        def _():
            qkv_ref[...] = acc

        @pl.when(j == 1)
        def _():
            gates_ref[...] = acc.astype(BF16)

    tok = lambda c: pl.BlockSpec((tb, c), lambda i, j: (i, 0))
    res, extra = _pcall(
        body, name="proj", grid=(t // tb, 2),
        in_specs=[tok(d), pl.BlockSpec((d, half), lambda i, j: (0, j))],
        out_specs=[tok(half), tok(half)],
        out_shape=[jax.ShapeDtypeStruct((t, half), F32), jax.ShapeDtypeStruct((t, half), BF16)],
        semantics=("parallel", "arbitrary"), vmem_mb=48, rider=rider,
    )(h, w)
    return res if rider is None else (res, extra)


def _matmul_nt_normbwd(dy, w, x, g, dres, *, name, tb, also_bf16=False, to_natural=False, after=()):
    t, d = x.shape
    stacked = dy.ndim == 3
    n_i = SEG // TI
    if to_natural:
        tb = N_RES * TI

    def body(dy_ref, w_ref, x_ref, g_ref, dres_ref, *rest):
        rest = list(rest)
        dx_ref = rest.pop(0)
        dxb_ref = rest.pop(0) if also_bf16 else None
        gg_ref = rest.pop(0)
        i = pl.program_id(0)

        def rows(ref, *lead):
            v = ref[lead] if lead else ref[...]
            return v[0].reshape(tb, v.shape[-1]) if to_natural else v

        if stacked:
            kb = dy_ref.shape[-1]
            dh = _dot_nt(rows(dy_ref, 0), w_ref[:, 0:kb])
            for s in range(1, dy_ref.shape[0]):
                dh = dh + _dot_nt(rows(dy_ref, s), w_ref[:, s * kb:(s + 1) * kb])
        else:
            dh = _dot_nt(rows(dy_ref), w_ref[...])
        g_v = g_ref[...]
        _, xh, r = _rms_fwd(rows(x_ref), g_v)
        dx = _rms_bwd(dh, xh, r, g_v) + rows(dres_ref)
        if to_natural:
            scr = rest.pop(0)
            for cb in range(d // BLK):
                cols = slice(cb * BLK, (cb + 1) * BLK)
                slab = scr.at[cb]
                for res in range(N_RES):
                    slab[pl.ds(res, TI, stride=N_RES), :] = dx[res * TI:(res + 1) * TI, cols]
                dx_ref[:, cols] = slab[...]
        else:
            dx_ref[...] = dx
        if also_bf16:
            dxb_ref[...] = dx.astype(BF16)
        part = jnp.sum(dh * xh, axis=0, keepdims=True)

        @pl.when(i == 0)
        def _():
            gg_ref[...] = part

        @pl.when(i != 0)
        def _():
            gg_ref[...] += part

    tok = pl.BlockSpec((tb, d), lambda i: (i, 0))
    row = pl.BlockSpec((1, d), lambda i: (0, 0))
    if to_natural:
        act = pl.BlockSpec((1, N_RES, TI, d), lambda i: (i // n_i, 0, i % n_i, 0))
        dy_spec = pl.BlockSpec((dy.shape[0], 1, N_RES, TI, dy.shape[2]), lambda i: (0, i // n_i, 0, i % n_i, 0))
        dy, x, dres = dy.reshape(dy.shape[0], t // HALF, N_RES, SEG, dy.shape[2]), _x4(x), _x4(dres)
    elif stacked:
        act, dy_spec = tok, pl.BlockSpec((dy.shape[0], tb, dy.shape[2]), lambda i: (0, i, 0))
    else:
        act, dy_spec = tok, pl.BlockSpec((tb, dy.shape[1]), lambda i: (i, 0))
    in_specs = [dy_spec, pl.BlockSpec(w.shape, lambda i: (0, 0)), act, row, act]
    out_specs = [tok] + ([tok] if also_bf16 else []) + [row]
    out_shape = ([jax.ShapeDtypeStruct((t, d), F32)] + ([jax.ShapeDtypeStruct((t, d), BF16)] if also_bf16 else [])
                 + [jax.ShapeDtypeStruct((1, d), F32)])
    res, _ = _pcall(
        body, name=name, grid=(t // tb,), in_specs=in_specs, out_specs=out_specs, out_shape=out_shape,
        scratch_shapes=[pltpu.VMEM((d // BLK, tb, BLK), F32)] if to_natural else [],
        semantics=("arbitrary",), vmem_mb=56, after=after,
    )(dy, w, x, g, dres)
    return res


def _matmul_tn(a, b, *, name, bm, bn, square_a=False, after=()):
    t, m = a.shape
    stacked = b.ndim == 3
    n = b.shape[0] * bn if stacked else b.shape[1]

    def body(a_ref, b_ref, o_ref):
        av = a_ref[...]
        if square_a:
            av = av.astype(F32)
            av = (av * av).astype(BF16)
        o_ref[...] = _dot_tn(av, b_ref[...]).astype(BF16)

    res, _ = _pcall(
        body, name=name, grid=(m // bm, n // bn),
        in_specs=[pl.BlockSpec((t, bm), lambda i, j: (0, i)),
                  pl.BlockSpec((None, t, bn), lambda i, j: (j, 0, 0)) if stacked
                  else pl.BlockSpec((t, bn), lambda i, j: (0, j))],
        out_specs=[pl.BlockSpec((bm, bn), lambda i, j: (i, j))], out_shape=[jax.ShapeDtypeStruct((m, n), BF16)],
        semantics=("parallel", "parallel"), vmem_mb=56, after=after,
    )(a, b)
    return res[0]


N_RES = 16
SEG = 128
HALF = N_RES * SEG
TI = 32
HALO = 16


def _x4(a):
    return a.reshape(a.shape[0] // HALF, N_RES, SEG, a.shape[1])


def _reorder(arrays, gain, name, rider=None):
    t, c = arrays[0].shape
    n = len(arrays)
    n_i = SEG // TI

    def body(*refs):
        g_ref, h_ref, scr = refs[n], refs[2 * n + 1], refs[-1]
        for i_ref, o_ref in zip(refs[:n], refs[n + 1:2 * n + 1]):
            for cb in range(c // BLK):
                cols = slice(cb * BLK, (cb + 1) * BLK)
                slab = scr.at[cb]
                slab[...] = i_ref[:, cols]
                for r in range(N_RES):
                    o_ref[0, r, :, cols] = slab[pl.ds(r, TI, stride=N_RES), :]
        h_ref[0] = _rms_fwd(refs[n + 1][0], g_ref[...])[0].astype(BF16)

    out = pl.BlockSpec((1, N_RES, TI, c), lambda s: (s // n_i, 0, s % n_i, 0))
    res, extra = _pcall(
        body, name=name, grid=(t // (TI * N_RES),),
        in_specs=[pl.BlockSpec((TI * N_RES, c), lambda s: (s, 0))] * n + [pl.BlockSpec((1, c), lambda s: (0, 0))],
        out_specs=[out] * (n + 1),
        out_shape=[jax.ShapeDtypeStruct((t // HALF, N_RES, SEG, c), F32)] * n
        + [jax.ShapeDtypeStruct((t // HALF, N_RES, SEG, c), BF16)],
        scratch_shapes=[pltpu.VMEM((c // BLK, TI * N_RES, BLK), F32)],
        semantics=("parallel",), vmem_mb=32, rider=rider,
    )(*arrays, gain)
    res = [r.reshape(t, c) for r in res]
    return res if rider is None else (res, extra)


_PATTERNS = ((1, 16, 8, SEG), (4, 4, 32, 4 * SEG), (16, 1, SEG, 0))
_FIRST = {1: 1, 4: 4, 16: 16}


def _group_rows(d, g):
    a = g >> 4
    if d == 16:
        base = a * HALF + (g & 15) * SEG
        prev = base - HALF
    elif d == 4:
        c = (g >> 2) & 3
        base = a * HALF + (g & 3) * SEG + c * 32
        prev = jnp.where(c > 0, base - 32, base - HALF + 96)
    else:
        c = g & 15
        base = a * HALF + c * 8
        prev = jnp.where(c > 0, base - 8, base - HALF + 120)
    return base, prev


def _load_rows(ref, base, n, rows, stride):
    parts = [ref[pl.ds(pl.multiple_of(base + j * stride, 8), rows), :] for j in range(n)]
    return parts[0] if n == 1 else jnp.concatenate(parts, axis=0)


def _store_rows(ref, base, val, n, rows, stride, add=False):
    for j in range(n):
        sl = pl.ds(pl.multiple_of(base + j * stride, 8), rows)
        piece = val[j * rows:(j + 1) * rows, :]
        if add:
            ref[sl, :] += piece
        else:
            ref[sl, :] = piece


def _band_bias(n, rows):
    shift = rows.bit_length() - 1
    lq = lax.broadcasted_iota(jnp.int32, (BLK, BLK), 0)
    lk = lax.broadcasted_iota(jnp.int32, (BLK, BLK), 1)
    iq = (lq & (rows - 1)) * n + (lq >> shift)
    ik = (lk & (rows - 1)) * n + (lk >> shift)
    zero = jnp.zeros((BLK, BLK), F32)
    return jnp.where(ik >= iq, zero, NEG_INF), jnp.where(ik <= iq, zero, NEG_INF)


def _set_bias(bias_scr, n, rows):
    prev_b, cur_b = _band_bias(n, rows)
    for half in range(2):
        bias_scr[half * BLK:(half + 1) * BLK, 0:BLK] = prev_b
        bias_scr[half * BLK:(half + 1) * BLK, BLK:2 * BLK] = cur_b


SCALE = 1.0 / math.sqrt(HEAD_DIM)


def _head_consts(value=1.0):
    lane_lo = lax.broadcasted_iota(jnp.int32, (BLK, BLK), 1) < HEAD_DIM
    return lane_lo, [jnp.where(lane_lo, value, 0.0).astype(BF16), jnp.where(lane_lo, 0.0, value).astype(BF16)]


def _stack_heads(v, head_mask):
    return jnp.concatenate([v * head_mask[0], v * head_mask[1]], axis=0)


def _unstack_heads(v2, lane_lo):
    return jnp.where(lane_lo, v2[:BLK], v2[BLK:])


def _rows_per_head(v, lane_lo):
    rolled = pltpu.roll(v, HEAD_DIM, axis=1)
    return jnp.concatenate([jnp.where(lane_lo, v, rolled), jnp.where(lane_lo, rolled, v)], axis=0)


WIDTH = 4


def _loop(lo, hi, fn, width=None):
    if width is None:
        def body(g, carry):
            fn(g)
            return carry

        if hi > lo:
            lax.fori_loop(lo, hi, body, 0)
        return
    while hi > lo:
        trips = (hi - lo) // width
        if trips:
            def body(i, carry, lo=lo, width=width):
                fn([lo + width * i + j for j in range(width)])
                return carry

            lax.fori_loop(0, trips, body, 0)
            lo += trips * width
        width = max(1, width // 2)


def _mix_weights(l1, l2, l3):
    mx = jnp.maximum(jnp.maximum(l1, l2), l3)
    e1, e2, e3 = jnp.exp(l1 - mx), jnp.exp(l2 - mx), jnp.exp(l3 - mx)
    inv = 1.0 / (e1 + e2 + e3)
    return e1 * inv, e2 * inv, e3 * inv


def _attention_fwd(qkv, rider=None):
    t = qkv.shape[0]
    groups = 16 * (t // HALF)

    def body(q_ref, k_ref, v_ref, attn_ref, l1_ref, l2_ref, l3_ref, o_scr, bias_scr):
        lane_lo, q_mask = _head_consts(SCALE)
        l_refs = (l1_ref, l2_ref, l3_ref)
        for p, (d, n, rows, stride) in enumerate(_PATTERNS):
            _set_bias(bias_scr, n, rows)
            o_p, l_p = o_scr.at[p], l_refs[p]

            def block(gs, has_prev):
                at = [_group_rows(d, g) for g in gs]

                def load(ref, b):
                    return _load_rows(ref, b, n, rows, stride).astype(BF16)

                q2 = [_stack_heads(load(q_ref, b), q_mask) for b, _ in at]
                k2 = [load(k_ref, b) for b, _ in at]
                v2 = [load(v_ref, b) for b, _ in at]
                if has_prev:
                    k2 = [jnp.concatenate([load(k_ref, pv), k], axis=0) for (_, pv), k in zip(at, k2)]
                    v2 = [jnp.concatenate([load(v_ref, pv), v], axis=0) for (_, pv), v in zip(at, v2)]
                s = [_dot_nt(q, k) for q, k in zip(q2, k2)]
                s = [x + (bias_scr[...] if has_prev else bias_scr[:, BLK:2 * BLK]) for x in s]
                mx = [jnp.max(x, axis=1, keepdims=True) for x in s]
                e = [jnp.exp(x - m) for x, m in zip(s, mx)]
                den = [jnp.sum(x, axis=1, keepdims=True) for x in e]
                o2 = [_dot(x.astype(BF16), v) * (1.0 / dn) for x, v, dn in zip(e, v2, den)]
                lse2 = [jnp.broadcast_to(m + jnp.log(dn), (2 * BLK, BLK)) for m, dn in zip(mx, den)]
                for (b, _), o, l in zip(at, o2, lse2):
                    _store_rows(o_p, b, _unstack_heads(o, lane_lo), n, rows, stride)
                    _store_rows(l_p, b, _unstack_heads(l, lane_lo), n, rows, stride)

            _loop(0, _FIRST[d], lambda gs: block(gs, False), width=2 * WIDTH)
            _loop(_FIRST[d], groups, lambda gs: block(gs, True), width=2 * WIDTH)

        def mix(i):
            sl = pl.ds(pl.multiple_of(i * 256, 256), 256)
            w = _mix_weights(l1_ref[sl, :], l2_ref[sl, :], l3_ref[sl, :])
            attn_ref[sl, :] = w[0] * o_scr[0, sl, :] + w[1] * o_scr[1, sl, :] + w[2] * o_scr[2, sl, :]

        _loop(0, t // 256, mix)

    def col(c0):
        return pl.BlockSpec((t, BLK), lambda hp: (0, c0 + hp))

    res, extra = _pcall(
        body, name="attention_fwd", grid=(4,), in_specs=[col(0), col(4), col(8)], out_specs=[col(0)] * 4,
        out_shape=[jax.ShapeDtypeStruct((t, 512), F32)] * 4,
        scratch_shapes=[pltpu.VMEM((3, t, BLK), F32), pltpu.VMEM((2 * BLK, 2 * BLK), F32)],
        semantics=("parallel",), vmem_mb=48, rider=rider,
    )(qkv, qkv, qkv)
    return res if rider is None else (res, extra)


def _attention_bwd(qkv, dattn, dsum, lses, dproj):
    t = qkv.shape[0]
    groups = 16 * (t // HALF)

    def body(q_ref, k_ref, v_ref, da_ref, ds_ref, l1_ref, l2_ref, l3_ref, kept_ref, out_ref, acc, bias_scr):
        del kept_ref
        lane_lo, head_mask = _head_consts()
        q_mask = _head_consts(SCALE)[1]
        l_refs = (l1_ref, l2_ref, l3_ref)

        def clear(i):
            sl = pl.ds(pl.multiple_of(i * 512, 512), 512)
            for s in range(3):
                acc[s, sl, :] = jnp.zeros((512, BLK), F32)

        _loop(0, t // 512, clear)
        dq_acc, dk_acc, dv_acc = acc.at[0], acc.at[1], acc.at[2]
        for p, (d, n, rows, stride) in enumerate(_PATTERNS):
            _set_bias(bias_scr, n, rows)

            def block(gs, has_prev):
                at = [_group_rows(d, g) for g in gs]

                def load(ref, b):
                    return _load_rows(ref, b, n, rows, stride)

                def put(ref, b, val):
                    _store_rows(ref, b, val, n, rows, stride, add=True)

                def wide(x):
                    return jnp.concatenate([x, x], axis=1) if has_prev else x

                lse = [[load(ref, b) for ref in l_refs] for b, _ in at]
                w = [_mix_weights(*ls)[p] for ls in lse]
                do2 = [_stack_heads((wg * load(da_ref, b)).astype(BF16), head_mask) for wg, (b, _) in zip(w, at)]
                dl2 = [wide(_rows_per_head(wg * load(ds_ref, b), lane_lo)) for wg, (b, _) in zip(w, at)]
                lse2 = [wide(_rows_per_head(ls[p], lane_lo)) for ls in lse]
                q2 = [_stack_heads(load(q_ref, b).astype(BF16), q_mask) for b, _ in at]
                k2 = [load(k_ref, b).astype(BF16) for b, _ in at]
                v2 = [load(v_ref, b).astype(BF16) for b, _ in at]
                if has_prev:
                    k2 = [jnp.concatenate([load(k_ref, pv).astype(BF16), k], axis=0) for (_, pv), k in zip(at, k2)]
                    v2 = [jnp.concatenate([load(v_ref, pv).astype(BF16), v], axis=0) for (_, pv), v in zip(at, v2)]
                s = [_dot_nt(q, k) for q, k in zip(q2, k2)]
                dp = [_dot_nt(do, v) for do, v in zip(do2, v2)]
                pr = [jnp.exp(x + (bias_scr[...] if has_prev else bias_scr[:, BLK:2 * BLK]) - l)
                      for x, l in zip(s, lse2)]
                ds = [(pg * (x - dl)).astype(BF16) for pg, x, dl in zip(pr, dp, dl2)]
                dq2 = [_dot(x, k) * SCALE for x, k in zip(ds, k2)]
                dk2 = [_dot_tn(x, q) for x, q in zip(ds, q2)]
                dv2 = [_dot_tn(pg.astype(BF16), do) for pg, do in zip(pr, do2)]
                for (b, pv), dq, dk, dv in zip(at, dq2, dk2, dv2):
                    put(dq_acc, b, _unstack_heads(dq, lane_lo))
                    if has_prev:
                        put(dk_acc, pv, dk[:BLK])
                        put(dv_acc, pv, dv[:BLK])
                        put(dk_acc, b, dk[BLK:])
                        put(dv_acc, b, dv[BLK:])
                    else:
                        put(dk_acc, b, dk)
                        put(dv_acc, b, dv)

            _loop(0, _FIRST[d], lambda gs: block(gs, False), width=WIDTH)
            _loop(_FIRST[d], groups, lambda gs: block(gs, True), width=WIDTH)

        def emit(i):
            sl = pl.ds(pl.multiple_of(i * 512, 512), 512)
            for s in range(3):
                out_ref[s, sl, :] = acc[s, sl, :].astype(BF16)

        _loop(0, t // 512, emit)

    def col(c0):
        return pl.BlockSpec((t, BLK), lambda hp: (0, c0 + hp))

    res, _ = _pcall(
        body, name="attention_bwd", grid=(4,),
        in_specs=[col(0), col(4), col(8)] + [col(0)] * 5 + [ANY],
        out_specs=[pl.BlockSpec((3, t, BLK), lambda hp: (0, 0, hp))],
        out_shape=[jax.ShapeDtypeStruct(dproj.shape, BF16)],
        scratch_shapes=[pltpu.VMEM((3, t, BLK), F32), pltpu.VMEM((2 * BLK, 2 * BLK), F32)],
        semantics=("parallel",), vmem_mb=56, aliases={8: 0},
    )(qkv, qkv, qkv, dattn, dsum, *lses, dproj)
    return res[0]


def _order_specs(t):
    n_i = SEG // TI
    nblk = (t // HALF) * n_i
    per = TI // HALO

    def main(c, col=0):
        return pl.BlockSpec((1, N_RES, TI, c), lambda s: (s // n_i, 0, s % n_i, col))

    def before(c, col=0):
        return pl.BlockSpec((1, 2, HALO, c), lambda s: (jnp.maximum(s - 1, 0) // n_i, N_RES // 2 - 1,
                                                        (jnp.maximum(s - 1, 0) % n_i) * per + per - 1, col))

    def after(c, col=0):
        return pl.BlockSpec((1, 2, HALO, c), lambda s: (jnp.minimum(s + 1, nblk - 1) // n_i, 0,
                                                        (jnp.minimum(s + 1, nblk - 1) % n_i) * per, col))

    return nblk, main, before, after


def _shift_in(v, row_in, up):
    rows = v.shape[0]
    idx = lax.broadcasted_iota(jnp.int32, v.shape, 0)
    fill = jnp.broadcast_to(row_in, v.shape)
    if up:
        return jnp.where(idx == rows - 1, fill, pltpu.roll(v, rows - 1, axis=0))
    return jnp.where(idx == 0, fill, pltpu.roll(v, 1, axis=0))


def _taps_behind(u, before):
    s15 = _shift_in(u[N_RES - 1], before[1, HALO - 1:HALO, :], up=False)
    s14 = _shift_in(u[N_RES - 2], before[0, HALO - 1:HALO, :], up=False)
    m1 = jnp.concatenate([s15[None], u[:N_RES - 1]], axis=0)
    m2 = jnp.concatenate([s14[None], s15[None], u[:N_RES - 2]], axis=0)
    return m1, m2


def _taps_ahead(u, after):
    t0 = _shift_in(u[0], after[0, 0:1, :], up=True)
    t1 = _shift_in(u[1], after[1, 0:1, :], up=True)
    p1 = jnp.concatenate([u[1:], t0[None]], axis=0)
    p2 = jnp.concatenate([u[2:], t0[None], t1[None]], axis=0)
    return p1, p2


def _conv_fwd(gates, before, first, cw):
    gates, before = gates.astype(F32), before.astype(F32)
    bg, cg, xc = gates[..., 0:512], gates[..., 512:1024], gates[..., 1024:1536]
    u = cg * xc
    ub = before[..., 512:1024] * before[..., 1024:1536]
    ub = jnp.where(first, jnp.zeros_like(ub), ub)
    m1, m2 = _taps_behind(u, ub)
    conv = m2 * cw[0:1, :] + m1 * cw[1:2, :] + u * cw[2:3, :]
    return bg, u, m1, m2, conv


def _sum_tokens(v):
    return jnp.sum(jnp.sum(v, axis=0), axis=0, keepdims=True)


def _mixer_fwd(x, attn, gates, cw, g_a, g_c, w_out):
    t, d = x.shape
    nblk, main, before, _ = _order_specs(t)
    rows = N_RES * TI

    def body(x_ref, at_ref, gt_ref, gb_ref, cw_ref, ga_ref, gc_ref, wa_ref, wb_ref, x1_ref, mg_ref):
        an = _rms_fwd(at_ref[0], ga_ref[...])[0].astype(BF16)
        bg, _, _, _, conv = _conv_fwd(gt_ref[0], gb_ref[0], pl.program_id(0) == 0, cw_ref[...])
        cn = _rms_fwd(bg * conv, gc_ref[...])[0].astype(BF16)
        mg_ref[0, :, :, 0:512] = an
        mg_ref[0, :, :, 512:1024] = cn
        y = _dot(an.reshape(rows, 512), wa_ref[...]) + _dot(cn.reshape(rows, 512), wb_ref[...])
        x1_ref[0] = x_ref[0] + y.reshape(N_RES, TI, d)

    const = lambda r, c, i0=0: pl.BlockSpec((r, c), lambda s: (i0, 0))
    x1, merged = pl.pallas_call(
        body, name="mixer_fwd", grid=(nblk,),
        in_specs=[main(d), main(512), main(1536), before(1536), const(3, 512), const(1, 512), const(1, 512),
                  const(512, d), const(512, d, 1)],
        out_specs=[main(d), main(d)],
        out_shape=[jax.ShapeDtypeStruct(_x4(x).shape, F32), jax.ShapeDtypeStruct(_x4(x).shape, BF16)],
        compiler_params=_params(("parallel",), 48),
    )(_x4(x), _x4(attn), _x4(gates), _x4(gates), cw, g_a, g_c, w_out, w_out)
    return x1.reshape(t, d), merged.reshape(t, d)


def _mixer_bwd(dx1, merged, attn, gates, cw, g_a, g_c, w_out, head_sum):
    t, d = dx1.shape
    nblk, main, before, _ = _order_specs(t)
    rows = N_RES * TI

    def body(dx_ref, mg_ref, at_ref, gt_ref, gb_ref, cw_ref, ga_ref, gc_ref, wa_ref, wb_ref, hs_ref,
             da_ref, dsum_ref, dy_ref, gga_ref, ggc_ref, gw_ref, acc_w):
        s = pl.program_id(0)
        dxb = dx_ref[0].reshape(rows, d).astype(BF16)

        @pl.when(s == 0)
        def _():
            acc_w[...] = jnp.zeros_like(acc_w)

        acc_w[...] += _dot_tn(mg_ref[0].reshape(rows, d), dxb)

        @pl.when(s == nblk - 1)
        def _():
            gw_ref[...] = acc_w[...].astype(BF16)

        dma = _dot_nt(dxb, wa_ref[...]).reshape(N_RES, TI, 512)
        dmc = _dot_nt(dxb, wb_ref[...]).reshape(N_RES, TI, 512)
        attn_v, g_av = at_ref[0], ga_ref[...]
        _, ah, ra = _rms_fwd(attn_v, g_av)
        dattn = _rms_bwd(dma, ah, ra, g_av)
        da_ref[0] = dattn
        z = (dattn * attn_v).reshape(rows, 512)
        hs = hs_ref[...]
        z1 = z.astype(BF16)
        z2 = (z - z1.astype(F32)).astype(BF16)
        dsum_ref[0] = (_dot(z1, hs) + _dot(z2, hs)).reshape(N_RES, TI, 512)
        bg, _, _, _, conv = _conv_fwd(gt_ref[0], gb_ref[0], s == 0, cw_ref[...])
        g_cv = gc_ref[...]
        _, yh, rc = _rms_fwd(bg * conv, g_cv)
        dy_ref[0] = _rms_bwd(dmc, yh, rc, g_cv)
        pa, pc = _sum_tokens(dma * ah), _sum_tokens(dmc * yh)

        @pl.when(s == 0)
        def _():
            gga_ref[...] = pa
            ggc_ref[...] = pc

        @pl.when(s != 0)
        def _():
            gga_ref[...] += pa
            ggc_ref[...] += pc

    const = lambda r, c, i0=0: pl.BlockSpec((r, c), lambda s: (i0, 0))
    shape4 = _x4(attn).shape
    res, _ = _pcall(
        body, name="mixer_bwd", grid=(nblk,),
        in_specs=[main(d), main(d), main(512), main(1536), before(1536), const(3, 512), const(1, 512), const(1, 512),
                  const(512, d), const(512, d, 1), const(512, 512)],
        out_specs=[main(512)] * 3 + [const(1, 512), const(1, 512), const(d, d)],
        out_shape=[jax.ShapeDtypeStruct(shape4, F32)] * 3 + [jax.ShapeDtypeStruct((1, 512), F32)] * 2
        + [jax.ShapeDtypeStruct((d, d), BF16)],
        scratch_shapes=[pltpu.VMEM((d, d), F32)],
        semantics=("arbitrary",), vmem_mb=48,
    )(_x4(dx1), _x4(merged), _x4(attn), _x4(gates), _x4(gates), cw, g_a, g_c, w_out, w_out, head_sum)
    return [r.reshape(t, 512) for r in res[:3]] + res[3:]


def _conv_bwd(dy, gates, cw, after=()):
    t = dy.shape[0]
    nblk, main, before, ahead = _order_specs(t)
    n_i = SEG // TI

    def body(dy_ref, dya_ref, gt_ref, gb_ref, ga_ref, cw_ref, dp_ref, gcw_ref):
        s = pl.program_id(0)
        cw_v, gates_v = cw_ref[...], gt_ref[0]
        bg, u, m1, m2, conv = _conv_fwd(gates_v, gb_ref[0], s == 0, cw_v)
        dy_v = dy_ref[0]
        dconv = dy_v * bg
        dca = dya_ref[0] * ga_ref[0][..., 0:512].astype(F32)
        dca = jnp.where(s == nblk - 1, jnp.zeros_like(dca), dca)
        p1, p2 = _taps_ahead(dconv, dca)
        du = dconv * cw_v[2:3, :] + p1 * cw_v[1:2, :] + p2 * cw_v[0:1, :]
        dp_ref[0, 0] = (dy_v * conv).astype(BF16)
        dp_ref[1, 0] = (du * gates_v[..., 1024:1536].astype(F32)).astype(BF16)
        dp_ref[2, 0] = (du * gates_v[..., 512:1024].astype(F32)).astype(BF16)
        parts = [_sum_tokens(dconv * m2), _sum_tokens(dconv * m1), _sum_tokens(dconv * u)]

        @pl.when(s == 0)
        def _():
            gcw_ref[...] = jnp.zeros_like(gcw_ref)

        for tap in range(3):
            gcw_ref[tap:tap + 1, :] += parts[tap]

    (dproj, gcw), _ = _pcall(
        body, name="conv_bwd", grid=(nblk,),
        in_specs=[main(512), ahead(512), main(1536), before(1536), ahead(1536),
                  pl.BlockSpec((3, 512), lambda s: (0, 0))],
        out_specs=[pl.BlockSpec((3, 1, N_RES, TI, 512), lambda s: (1, s // n_i, 0, s % n_i, 0)),
                   pl.BlockSpec((8, 512), lambda s: (0, 0))],
        out_shape=[jax.ShapeDtypeStruct((6, t // HALF, N_RES, SEG, 512), BF16), jax.ShapeDtypeStruct((8, 512), F32)],
        semantics=("arbitrary",), vmem_mb=40, after=after,
    )(_x4(dy), _x4(dy), _x4(gates), _x4(gates), _x4(gates), cw)
    return dproj.reshape(6, t, 512), gcw


def _xattn_fwd(x1, g, w_q, kv, w_o, *, tb):
    t, d = x1.shape
    hd = d // N_MEM_HEADS
    m = kv.shape[0]

    def body(x_ref, g_ref, wq_ref, k_ref, v_ref, wo_ref, x2_ref, h_ref, q_ref, o_ref):
        xv = x_ref[...]
        h = _rms_fwd(xv, g_ref[...])[0].astype(BF16)
        h_ref[...] = h
        q = _dot(h, wq_ref[...]).astype(BF16)
        q_ref[...] = q
        for hh in range(N_MEM_HEADS):
            sl = slice(hh * hd, (hh + 1) * hd)
            s = _dot_nt(q[:, sl], k_ref[:, sl]) * (1.0 / 16.0)
            e = jnp.exp(s - jnp.max(s, axis=1, keepdims=True))
            p = e / jnp.sum(e, axis=1, keepdims=True)
            o_ref[:, sl] = _dot(p.astype(BF16), v_ref[:, sl]).astype(BF16)
        x2_ref[...] = xv + _dot(o_ref[...], wo_ref[...])

    tok = pl.BlockSpec((tb, d), lambda i: (i, 0))
    full = pl.BlockSpec((d, d), lambda i: (0, 0))
    return pl.pallas_call(
        body, name="xattn_fwd", grid=(t // tb,),
        in_specs=[tok, pl.BlockSpec((1, d), lambda i: (0, 0)), full,
                  pl.BlockSpec((m, d), lambda i: (0, 0)), pl.BlockSpec((m, d), lambda i: (0, 1)), full],
        out_specs=[tok] * 4,
        out_shape=[jax.ShapeDtypeStruct((t, d), F32)] + [jax.ShapeDtypeStruct((t, d), BF16)] * 3,
        compiler_params=_params(("parallel",), 48),
    )(x1, g, w_q, kv, kv, w_o)


def _xattn_bwd(dx2, x1, g, q, w_q, kv, w_o, *, tb):
    t, d = x1.shape
    hd = d // N_MEM_HEADS
    m = kv.shape[0]

    def body(dx2_ref, x_ref, g_ref, q_ref, wq_ref, k_ref, v_ref, wo_ref,
             dx1_ref, dq_ref, dk_ref, dv_ref, gg_ref):
        i = pl.program_id(0)

        @pl.when(i == 0)
        def _():
            dk_ref[...] = jnp.zeros_like(dk_ref)
            dv_ref[...] = jnp.zeros_like(dv_ref)

        dx2 = dx2_ref[...]
        do = _dot_nt(dx2.astype(BF16), wo_ref[...]).astype(BF16)
        for hh in range(N_MEM_HEADS):
            sl = slice(hh * hd, (hh + 1) * hd)
            qh, kh, vh, doh = q_ref[:, sl], k_ref[:, sl], v_ref[:, sl], do[:, sl]
            s = _dot_nt(qh, kh) * (1.0 / 16.0)
            e = jnp.exp(s - jnp.max(s, axis=1, keepdims=True))
            p = e / jnp.sum(e, axis=1, keepdims=True)
            dp = _dot_nt(doh, vh)
            ds = (p * (dp - jnp.sum(dp * p, axis=1, keepdims=True)) * (1.0 / 16.0)).astype(BF16)
            dq_ref[:, sl] = _dot(ds, kh).astype(BF16)
            dk_ref[:, sl] += _dot_tn(ds, qh)
            dv_ref[:, sl] += _dot_tn(p.astype(BF16), doh)
        dh = _dot_nt(dq_ref[...], wq_ref[...])
        g_v = g_ref[...]
        _, xh, r = _rms_fwd(x_ref[...], g_v)
        dx1 = dx2 + _rms_bwd(dh, xh, r, g_v)
        dx1_ref[...] = dx1
        part = jnp.sum(dh * xh, axis=0, keepdims=True)

        @pl.when(i == 0)
        def _():
            gg_ref[...] = part

        @pl.when(i != 0)
        def _():
            gg_ref[...] += part

    tok = pl.BlockSpec((tb, d), lambda i: (i, 0))
    full = pl.BlockSpec((d, d), lambda i: (0, 0))
    acc = pl.BlockSpec((m, d), lambda i: (0, 0))
    res, _ = _pcall(
        body, name="xattn_bwd", grid=(t // tb,),
        in_specs=[tok, tok, pl.BlockSpec((1, d), lambda i: (0, 0)), tok, full,
                  pl.BlockSpec((m, d), lambda i: (0, 0)), pl.BlockSpec((m, d), lambda i: (0, 1)), full],
        out_specs=[tok, tok, acc, acc, pl.BlockSpec((1, d), lambda i: (0, 0))],
        out_shape=[jax.ShapeDtypeStruct((t, d), F32), jax.ShapeDtypeStruct((t, d), BF16),
                   jax.ShapeDtypeStruct((m, d), F32), jax.ShapeDtypeStruct((m, d), F32),
                   jax.ShapeDtypeStruct((1, d), F32)],
        semantics=("arbitrary",), vmem_mb=48,
    )(dx2, x1, g, q, w_q, kv, kv, w_o)
    return res


def _mem_bwd(dk, dv, w_kv, mem, mem_n, g):
    m, d = mem.shape

    def body(dk_ref, dv_ref, w_ref, x_ref, h_ref, g_ref, gw_ref, gg_ref):
        h = h_ref[...]
        dh = jnp.zeros((m, d), F32)
        for i, dy_ref in enumerate((dk_ref, dv_ref)):
            cols = slice(i * d, (i + 1) * d)
            dy = dy_ref[...].astype(BF16)
            gw_ref[:, cols] = _dot_tn(h, dy).astype(BF16)
            dh = dh + _dot_nt(dy, w_ref[:, cols])
        xh = _rms_fwd(x_ref[...], g_ref[...])[1]
        gg_ref[...] = jnp.sum(dh * xh, axis=0, keepdims=True)

    return pl.pallas_call(
        body, name="mem_bwd",
        out_shape=[jax.ShapeDtypeStruct(w_kv.shape, BF16), jax.ShapeDtypeStruct((1, d), F32)],
        compiler_params=pltpu.CompilerParams(vmem_limit_bytes=32 << 20),
    )(dk, dv, w_kv, mem, mem_n, g)


def _mlp_down_loss(a, w_down, x2, tgt, g, *, tb):
    t, d = x2.shape
    f = a.shape[1]

    def body(a_ref, w_ref, x_ref, t_ref, g_ref, dx_ref, dxb_ref, loss_ref, gg_ref):
        i = pl.program_id(0)
        av = a_ref[...].astype(F32)
        x3 = x_ref[...] + _dot((av * av).astype(BF16), w_ref[...])
        g_v = g_ref[...]
        out, xh, r = _rms_fwd(x3, g_v)
        err = out - t_ref[...]
        dout = err * (1.0 / d)
        dx = _rms_bwd(dout, xh, r, g_v)
        dx_ref[...] = dx
        dxb_ref[...] = dx.astype(BF16)
        part = jnp.sum(dout * xh, axis=0, keepdims=True)
        lpart = 0.5 * jnp.sum(jnp.mean(err * err, axis=-1, keepdims=True), axis=0, keepdims=True)
        lpart = jnp.broadcast_to(lpart, loss_ref.shape)

        @pl.when(i == 0)
        def _():
            gg_ref[...] = part
            loss_ref[...] = lpart

        @pl.when(i != 0)
        def _():
            gg_ref[...] += part
            loss_ref[...] += lpart

    tok = pl.BlockSpec((tb, d), lambda i: (i, 0))
    return pl.pallas_call(
        body, name="mlp_down_loss", grid=(t // tb,),
        in_specs=[pl.BlockSpec((tb, f), lambda i: (i, 0)), pl.BlockSpec((f, d), lambda i: (0, 0)), tok, tok,
                  pl.BlockSpec((1, d), lambda i: (0, 0))],
        out_specs=[tok, tok, pl.BlockSpec((8, 128), lambda i: (0, 0)), pl.BlockSpec((1, d), lambda i: (0, 0))],
        out_shape=[jax.ShapeDtypeStruct((t, d), F32), jax.ShapeDtypeStruct((t, d), BF16),
                   jax.ShapeDtypeStruct((8, 128), F32), jax.ShapeDtypeStruct((1, d), F32)],
        compiler_params=_params(("arbitrary",), 56),
    )(a, w_down, x2, tgt, g)


def _mlp_dpre(dx3, w_down, a, *, tb, bn):
    t, d = dx3.shape
    f = a.shape[1]

    def body(dx_ref, w_ref, a_ref, o_ref):
        o_ref[...] = (2.0 * a_ref[...].astype(F32) * _dot_nt(dx_ref[...], w_ref[...])).astype(BF16)

    return pl.pallas_call(
        body, name="mlp_dpre", grid=(t // tb, f // bn),
        in_specs=[pl.BlockSpec((tb, d), lambda i, j: (i, 0)), pl.BlockSpec((bn, d), lambda i, j: (j, 0)),
                  pl.BlockSpec((tb, bn), lambda i, j: (i, j))],
        out_specs=pl.BlockSpec((tb, bn), lambda i, j: (i, j)),
        out_shape=jax.ShapeDtypeStruct((t, f), BF16),
        compiler_params=_params(("parallel", "arbitrary"), 48),
    )(dx3, w_down, a)


def _adamw(gsum, w, m, v):
    m_new = ADAM_B1 * m + (1.0 - ADAM_B1) * gsum
    v_new = ADAM_B2 * v + (1.0 - ADAM_B2) * (gsum * gsum)
    m_hat = m_new / (1.0 - ADAM_B1 ** ADAM_STEP)
    v_hat = v_new / (1.0 - ADAM_B2 ** ADAM_STEP)
    delta = -ADAM_LR * (m_hat / (jnp.sqrt(v_hat) + ADAM_EPS) + ADAM_WD * w)
    return delta, m_new, v_new


def _sum_adamw(shards, *, name, tr):
    r, c = shards[0][1].shape
    n = len(shards)

    def body(*refs):
        for s in range(n):
            p_ref, w_ref, m_ref, v_ref = refs[4 * s:4 * s + 4]
            g_ref, d_ref, mo_ref, vo_ref = refs[4 * (n + s):4 * (n + s) + 4]
            g = p_ref[0].astype(F32)
            for k in range(1, N_DEV):
                g = g + p_ref[k].astype(F32)
            g_ref[...] = g
            d_ref[...], mo_ref[...], vo_ref[...] = _adamw(g, w_ref[...], m_ref[...], v_ref[...])

    blk = pl.BlockSpec((tr, c), lambda i: (i, 0))
    res = pl.pallas_call(
        body, name=name, grid=(r // tr,),
        in_specs=[pl.BlockSpec((N_DEV, tr, c), lambda i: (0, i, 0)), blk, blk, blk] * n,
        out_specs=[blk] * (4 * n), out_shape=[jax.ShapeDtypeStruct((r, c), F32)] * (4 * n),
        compiler_params=_params(("parallel",), 40),
    )(*[pltpu.with_memory_space_constraint(a, pltpu.HBM) for shard in shards for a in shard])
    return [res[4 * s:4 * s + 4] for s in range(n)]


_GAIN_ROWS = ("g_mix", "g_xattn", "g_mem", "g_mlp", "g_final")
PAIR_ROW = 5
LOSS_ROW = 6
TAPS_ROW = 8
SMALL_ROWS = 16
_SMALL = _GAIN_ROWS + ("g_attn_out", "g_conv_out", "conv_w")
CONV_SHARD = 512 // N_DEV


def _pack_small(gains, gg_attn, gg_conv, gcw, loss_blk):
    def body(*refs):
        o_ref = refs[-1]
        ga_ref, gc_ref, cw_ref, l_ref = refs[len(gains):-1]
        o_ref[...] = jnp.zeros_like(o_ref)
        for i, g_ref in enumerate(refs[:len(gains)]):
            o_ref[i:i + 1, :] = g_ref[...]
        o_ref[PAIR_ROW:PAIR_ROW + 1, 0:512] = ga_ref[...]
        o_ref[PAIR_ROW:PAIR_ROW + 1, 512:1024] = gc_ref[...]
        o_ref[LOSS_ROW:LOSS_ROW + 1, 0:BLK] = l_ref[0:1, :]
        o_ref[TAPS_ROW:SMALL_ROWS, 0:512] = cw_ref[...]

    return pl.pallas_call(body, name="pack_small", out_shape=jax.ShapeDtypeStruct((SMALL_ROWS, 1024), F32))(
        *gains, gg_attn, gg_conv, gcw, loss_blk)


def _update_small(parts, me, w, m, v):
    n = len(_SMALL)

    def body(me_ref, p_ref, *refs):
        ins, loss_ref, outs = refs[:3 * n], refs[3 * n], refs[3 * n + 1:]

        def total(lo, hi):
            s = p_ref[0, lo:hi, :]
            for k in range(1, N_DEV):
                s = s + p_ref[k, lo:hi, :]
            return s

        grads = {k: total(i, i + 1) for i, k in enumerate(_GAIN_ROWS)}
        both = total(PAIR_ROW, PAIR_ROW + 1)
        grads["g_attn_out"], grads["g_conv_out"] = both[:, 0:512], both[:, 512:1024]
        taps = total(TAPS_ROW, SMALL_ROWS)
        mine = jnp.zeros((SMALL_ROWS - TAPS_ROW, BLK), F32)
        for j in range(N_DEV):
            lo = j * CONV_SHARD // BLK * BLK
            blk = taps[:, lo:lo + BLK]
            if j * CONV_SHARD != lo:
                blk = pltpu.roll(blk, BLK - (j * CONV_SHARD - lo), axis=1)
            mine = jnp.where(me_ref[0] == j, blk, mine)
        grads["conv_w"] = mine[0:3, 0:CONV_SHARD]
        loss_ref[...] = total(LOSS_ROW, LOSS_ROW + 1)[:, 0:1]
        for i, k in enumerate(_SMALL):
            g_ref, d_ref, mo_ref, vo_ref = outs[4 * i:4 * i + 4]
            g_ref[...] = grads[k]
            d_ref[...], mo_ref[...], vo_ref[...] = _adamw(grads[k], ins[i][...], ins[n + i][...], ins[2 * n + i][...])

    vmem = pl.BlockSpec(memory_space=pltpu.VMEM)
    args = [d[k] for d in (w, m, v) for k in _SMALL]
    res = pl.pallas_call(
        body, name="update_small",
        in_specs=[pl.BlockSpec(memory_space=pltpu.SMEM)] + [vmem] * (1 + 3 * n),
        out_shape=[jax.ShapeDtypeStruct((1, 1), F32)] + [jax.ShapeDtypeStruct(w[k].shape, F32) for k in _SMALL
                                                         for _ in range(4)],
    )(me, parts, *args)
    return res[0], {k: res[1 + 4 * i:5 + 4 * i] for i, k in enumerate(_SMALL)}


def _head_sum_matrix():
    r = lax.broadcasted_iota(jnp.int32, (512, 512), 0) // HEAD_DIM
    c = lax.broadcasted_iota(jnp.int32, (512, 512), 1) // HEAD_DIM
    return (r == c).astype(BF16)


_SHARD_AXIS = dict(w_in=1, w_out=0, w_q=0, w_kv=1, w_o=0, w_up=1, w_down=0, conv_w=None, small=None)


class _Weights:
    def __init__(self, full, shards=None):
        self.full = dict(full)
        self.shards = shards

    def rider(self, names, late=False):
        if self.shards is None:
            return None
        return _Gather([self.shards[n] for n in names], [_SHARD_AXIS[n] for n in names], late)

    def arrived(self, names, gathered):
        if gathered is not None:
            for n, g in zip(names, gathered):
                self.full[n] = g.transpose(1, 0, 2).reshape(g.shape[1], -1) if n == "conv_w" else g

    def __getitem__(self, name):
        return self.full[name]


class _Grads:
    def __init__(self, distributed):
        self.distributed = distributed
        self.local = {}
        self.pending = {}

    def add(self, name, g):
        self.local[name] = g

    def send(self, *names):
        if not self.distributed:
            return []
        rider = _Exchange([self.local[n] for n in names], [_SHARD_AXIS[n] for n in names])
        started = _exchange_start(rider, "send_" + "_".join(names))
        self.pending[names[0]] = (names, rider, started)
        return [started[3]]

    def wait(self, first_name, after):
        names, rider, started = self.pending.pop(first_name)
        return _exchange_wait(rider, started, after, "wait_" + "_".join(names))


def _ride(fn, *args, rider=None, **kw):
    if rider is None:
        return fn(*args, **kw), None
    return fn(*args, rider=rider, **kw)


def _local_step(x, mem, tgt, gains, weights, grads):
    names = ["w_in", "conv_w"]
    (x, tgt, h1), got = _ride(_reorder, [x, tgt], gains["g_mix"], "reorder_in", rider=weights.rider(names, late=True))
    weights.arrived(names, got)
    w_in, cw = weights["w_in"], weights["conv_w"]

    names = ["w_out", "w_kv"]
    (qkv, gates), got = _ride(_proj, h1, w_in, tb=1024, rider=weights.rider(names))
    weights.arrived(names, got)
    names = ["w_q", "w_o", "w_up"]
    (attn, *lses), got = _ride(_attention_fwd, qkv, rider=weights.rider(names))
    weights.arrived(names, got)
    x1, merged = _mixer_fwd(x, attn, gates, cw, gains["g_attn_out"], gains["g_conv_out"], weights["w_out"])
    kv, mem_n = _norm_matmul(mem, gains["g_mem"], weights["w_kv"], name="mem_kv", out_dtype=BF16, tb=mem.shape[0],
                             bn=1024, save_h=True)
    x2, h2, qm, om = _xattn_fwd(x1, gains["g_xattn"], weights["w_q"], kv, weights["w_o"], tb=512)
    w_up = weights["w_up"]
    (a, h3), got = _ride(_norm_matmul, x2, gains["g_mlp"], w_up, name="mlp_up", out_dtype=BF16, tb=1024, bn=2048,
                         relu=True, save_h=True, rider=weights.rider(["w_down"], late=True))
    weights.arrived(["w_down"], got)
    w_down = weights["w_down"]
    dx3, dx3b, loss_blk, gg_final = _mlp_down_loss(a, w_down, x2, tgt, gains["g_final"], tb=512)

    dpre = _mlp_dpre(dx3b, w_down, a, tb=1024, bn=2048)
    grads.add("w_down", _matmul_tn(a, dx3b, name="grad_w_down", bm=512, bn=1024, square_a=True))
    sent = grads.send("w_down")
    grads.add("w_up", _matmul_tn(h3, dpre, name="grad_w_up", bm=1024, bn=1024, after=sent))
    sent = grads.send("w_up")
    dx2, dx2b, gg_mlp = _matmul_nt_normbwd(dpre, w_up, x2, gains["g_mlp"], dx3, name="mlp_dx", tb=512,
                                           also_bf16=True, after=sent)

    grads.add("w_o", _matmul_tn(om, dx2b, name="grad_w_o", bm=512, bn=512))
    dx1, dqm, dk, dv, gg_xattn = _xattn_bwd(dx2, x1, gains["g_xattn"], qm, weights["w_q"], kv, weights["w_o"], tb=512)
    grads.add("w_q", _matmul_tn(h2, dqm, name="grad_w_q", bm=1024, bn=512))
    gw_kv, gg_mem = _mem_bwd(dk, dv, weights["w_kv"], mem, mem_n, gains["g_mem"])
    grads.add("w_kv", gw_kv)

    dattn, dsum, dy, gg_attn, gg_conv, gw_out = _mixer_bwd(dx1, merged, attn, gates, cw, gains["g_attn_out"],
                                                           gains["g_conv_out"], weights["w_out"], _head_sum_matrix())
    grads.add("w_out", gw_out)
    sent = grads.send("w_o", "w_q", "w_kv", "w_out")
    dproj, gcw = _conv_bwd(dy, gates, cw, after=sent)
    dproj = _attention_bwd(qkv, dattn, dsum, lses, dproj)
    grads.add("w_in", _matmul_tn(h1, dproj, name="grad_w_in", bm=1024, bn=512))
    sent = grads.send("w_in")
    grad_x, gg_mix = _matmul_nt_normbwd(dproj, w_in, x, gains["g_mix"], dx1, name="mixer_dx", tb=512,
                                        to_natural=True, after=sent)

    grads.add("small", _pack_small([gg_mix, gg_xattn, gg_mem, gg_mlp, gg_final], gg_attn, gg_conv, gcw, loss_blk))
    return grad_x


_BIG = ("w_in", "w_out", "w_q", "w_kv", "w_o", "w_up", "w_down")


def kernel(x, mem, g_mix, w_in, conv_w, g_attn_out, g_conv_out, w_out, g_xattn, g_mem, w_q_mem, w_kv_mem, w_o_mem, g_mlp, w_up, w_down, g_final, loss_target, m_g_mix, m_w_in, m_conv_w, m_g_attn_out, m_g_conv_out, m_w_out, m_g_xattn, m_g_mem, m_w_q_mem, m_w_kv_mem, m_w_o_mem, m_g_mlp, m_w_up, m_w_down, m_g_final, v_g_mix, v_w_in, v_conv_w, v_g_attn_out, v_g_conv_out, v_w_out, v_g_xattn, v_g_mem, v_w_q_mem, v_w_kv_mem, v_w_o_mem, v_g_mlp, v_w_up, v_w_down, v_g_final):
    d = x.shape[-1]
    me = 4 * lax.axis_index("x") + 2 * lax.axis_index("y") + lax.axis_index("c")
    w_shards = dict(w_in=w_in, w_out=w_out, w_q=w_q_mem, w_kv=w_kv_mem, w_o=w_o_mem, w_up=w_up, w_down=w_down)
    m_shards = dict(w_in=m_w_in, w_out=m_w_out, w_q=m_w_q_mem, w_kv=m_w_kv_mem, w_o=m_w_o_mem, w_up=m_w_up,
                    w_down=m_w_down)
    v_shards = dict(w_in=v_w_in, w_out=v_w_out, w_q=v_w_q_mem, w_kv=v_w_kv_mem, w_o=v_w_o_mem, w_up=v_w_up,
                    w_down=v_w_down)
    gains = dict(g_mix=g_mix, g_attn_out=g_attn_out, g_conv_out=g_conv_out, g_xattn=g_xattn, g_mem=g_mem,
                 g_mlp=g_mlp, g_final=g_final)
    gains2 = {k: v.reshape(1, -1) for k, v in gains.items()}

    shards = {k: w_shards[k].astype(BF16) for k in _BIG}
    shards["conv_w"] = conv_w
    grads = _Grads(distributed=True)
    grad_x = _local_step(x[0], mem[0], loss_target[0], gains2, _Weights({}, shards), grads)

    after = grads.send("small")
    outs = {}
    tiles = dict(w_in=512, w_out=64, w_q=64, w_kv=512, w_o=64, w_up=512, w_down=256)
    for group in (("w_down",), ("w_up",), ("w_o", "w_q", "w_kv", "w_out"), ("w_in",)):
        received = dict(zip(group, grads.wait(group[0], after)))
        same_shape = {}
        for k in group:
            same_shape.setdefault(w_shards[k].shape, []).append(k)
        for names in same_shape.values():
            res = _sum_adamw([(received[k], w_shards[k], m_shards[k], v_shards[k]) for k in names],
                             name="adamw_" + "_".join(names), tr=tiles[names[0]])
            outs.update(zip(names, res))
            after = [res[-1][0]]
    small_received, = grads.wait("small", after)

    m_small = dict(g_mix=m_g_mix, g_attn_out=m_g_attn_out, g_conv_out=m_g_conv_out, g_xattn=m_g_xattn,
                   g_mem=m_g_mem, g_mlp=m_g_mlp, g_final=m_g_final)
    v_small = dict(g_mix=v_g_mix, g_attn_out=v_g_attn_out, g_conv_out=v_g_conv_out, g_xattn=v_g_xattn,
                   g_mem=v_g_mem, g_mlp=v_g_mlp, g_final=v_g_final)
    as_rows = lambda vals, conv: dict({k: a.reshape(1, -1) for k, a in vals.items()}, conv_w=conv)
    loss, small_out = _update_small(small_received, me.reshape(1), as_rows(gains, conv_w),
                                    as_rows(m_small, m_conv_w), as_rows(v_small, v_conv_w))
    small_out = {k: [a.reshape(dict(gains, conv_w=conv_w)[k].shape) for a in res] for k, res in small_out.items()}
    names = {"g_mix": "g_mix", "w_in": "w_in", "conv_w": "conv_w", "g_attn_out": "g_attn_out",
             "g_conv_out": "g_conv_out", "w_out": "w_out", "g_xattn": "g_xattn", "g_mem": "g_mem",
             "w_q_mem": "w_q", "w_kv_mem": "w_kv", "w_o_mem": "w_o", "g_mlp": "g_mlp", "w_up": "w_up",
             "w_down": "w_down", "g_final": "g_final"}
    result = [loss.reshape(()), grad_x[None]]
    for which in range(4):
        for key in names.values():
            result.append(outs[key][which] if key in outs else small_out[key][which])
    return tuple(result)
```

```python
import math

import jax
import jax.numpy as jnp
from jax import lax
from jax.experimental import pallas as pl
from jax.experimental.pallas import tpu as pltpu

F32 = jnp.float32
BF16 = jnp.bfloat16
NORM_EPS = 1e-6
NEG_INF = -1e30
N_DEV = 8
BLK = 128
HEAD_DIM = 64
N_MEM_HEADS = 4
ADAM_LR = 0.001
ADAM_B1 = 0.9
ADAM_B2 = 0.999
ADAM_EPS = 1e-08
ADAM_WD = 0.01
ADAM_STEP = 10
MESH = pl.DeviceIdType.MESH
ANY = pl.BlockSpec(memory_space=pl.ANY)


def _dot(a, b):
    return jnp.dot(a, b, preferred_element_type=F32)


def _dot_nt(a, b):
    return lax.dot_general(a, b, (((1,), (1,)), ((), ())), preferred_element_type=F32)


def _dot_tn(a, b):
    return lax.dot_general(a, b, (((0,), (0,)), ((), ())), preferred_element_type=F32)


def _params(semantics, vmem_mb):
    return pltpu.CompilerParams(dimension_semantics=semantics, vmem_limit_bytes=vmem_mb << 20)


def _rms_fwd(x, g):
    r = lax.rsqrt(jnp.mean(x * x, axis=-1, keepdims=True) + NORM_EPS)
    xh = x * r
    return xh * g, xh, r


def _rms_bwd(dy, xh, r, g):
    gy = dy * g
    return r * (gy - xh * jnp.mean(xh * gy, axis=-1, keepdims=True))


def _position():
    x, y, c = lax.axis_index("x"), lax.axis_index("y"), lax.axis_index("c")
    return x, y, c


def _block_of(ref, j, axis, shard_shape):
    r, c = shard_shape
    if axis is None:
        return ref.at[j]
    if axis == 0:
        return ref.at[pl.ds(j * r, r), :]
    return ref.at[:, pl.ds(j * c, c)]


class _Gather:
    has_mid = True
    alias_pairs = ()

    def __init__(self, shards, axes, late=False):
        self.arrays = list(shards)
        self.axes = list(axes)
        self.late = late
        self.n = len(self.arrays)

    def out_shape(self):
        res = []
        for s, axis in zip(self.arrays, self.axes):
            r, c = s.shape
            shape = (N_DEV, r, c) if axis is None else (N_DEV * r, c) if axis == 0 else (r, N_DEV * c)
            res.append(jax.ShapeDtypeStruct(shape, s.dtype))
        return res

    def scratch(self):
        return [pltpu.SemaphoreType.DMA((self.n, 7)), pltpu.SemaphoreType.DMA((self.n, 7)),
                pltpu.SemaphoreType.DMA((self.n,))]

    def _ctx(self, ins, outs, sems):
        send_sems, recv_sems, local_sems = sems
        x, y, c = _position()
        me, sibling = (x, y, c), (x, y, 1 - c)
        chips = [(1 - x, y), (x, 1 - y), (1 - x, 1 - y)]

        def lin(px, py, pc):
            return 4 * px + 2 * py + pc

        def place(a, block):
            return _block_of(outs[a], lin(*block), self.axes[a], self.arrays[a].shape)

        def copy(a, k, block, to, src=None):
            dst = place(a, block)
            return pltpu.make_async_remote_copy(
                src_ref=dst if src is None else src, dst_ref=dst,
                send_sem=send_sems.at[a, k], recv_sem=recv_sems.at[a, k],
                device_id=to, device_id_type=MESH)

        def mine():
            return [pltpu.make_async_copy(ins[a], place(a, me), local_sems.at[a]) for a in range(self.n)]

        def first():
            res = []
            for a in range(self.n):
                res.append(copy(a, 0, me, sibling, src=ins[a]))
                res += [copy(a, 1 + j, me, (*chip, c), src=ins[a]) for j, chip in enumerate(chips)]
            return res

        return c, me, sibling, chips, copy, mine, first

    def start(self, ins, outs, sems):
        _, _, _, _, _, mine, first = self._ctx(ins, outs, sems)
        for cp in mine() + first():
            cp.start()

    def mid(self, ins, outs, sems):
        c, me, sibling, chips, copy, _, _ = self._ctx(ins, outs, sems)
        for j, chip in enumerate(chips):
            for a in range(self.n):
                copy(a, 1 + j, (*chip, c), me).wait_recv()
                copy(a, 4 + j, (*chip, c), sibling).start()

    def finish(self, ins, outs, sems):
        c, me, sibling, chips, copy, mine, first = self._ctx(ins, outs, sems)
        for a in range(self.n):
            copy(a, 0, sibling, me).wait_recv()
            for j, chip in enumerate(chips):
                copy(a, 4 + j, (*chip, 1 - c), me).wait_recv()
        for cp in first():
            cp.wait_send()
        for j, chip in enumerate(chips):
            for a in range(self.n):
                copy(a, 4 + j, (*chip, c), sibling).wait_send()
        for cp in mine():
            cp.wait()


class _Exchange:
    def __init__(self, parts, axes):
        self.n = len(parts)
        self.axes = list(axes)
        self.arrays = list(parts)

    def _piece(self, a):
        r, c = self.arrays[a].shape
        axis = self.axes[a]
        return (r, c) if axis is None else (r // N_DEV, c) if axis == 0 else (r, c // N_DEV)

    def out_shape(self):
        return [jax.ShapeDtypeStruct((N_DEV,) + self._piece(a), self.arrays[a].dtype) for a in range(self.n)]

    def semaphores(self):
        return [pltpu.SemaphoreType.DMA((7 * self.n,)), pltpu.SemaphoreType.DMA((7 * self.n,)),
                pltpu.SemaphoreType.DMA((self.n,))]

    def _ctx(self, ins, outs, sems):
        send_sems, recv_sems, local_sems = sems
        x, y, c = _position()
        me = 4 * x + 2 * y + c

        def src(a, j):
            return ins[a] if self.axes[a] is None else _block_of(ins[a], j, self.axes[a], self._piece(a))

        def dst(a, j):
            return outs[a].at[j]

        def local():
            return [pltpu.make_async_copy(src(a, me), dst(a, me), local_sems.at[a]) for a in range(self.n)]

        def remote(inbound):
            res = []
            for a in range(self.n):
                for k in range(1, N_DEV):
                    peer = (1 - x if k & 4 else x, 1 - y if k & 2 else y, 1 - c if k & 1 else c)
                    plin = 4 * peer[0] + 2 * peer[1] + peer[2]
                    res.append(pltpu.make_async_remote_copy(
                        src_ref=src(a, plin), dst_ref=dst(a, plin if inbound else me),
                        send_sem=send_sems.at[7 * a + k - 1], recv_sem=recv_sems.at[7 * a + k - 1],
                        device_id=peer, device_id_type=MESH))
            return res

        return local, remote

    def start(self, ins, outs, sems):
        local, remote = self._ctx(ins, outs, sems)
        for cp in local() + remote(False):
            cp.start()

    def finish(self, ins, outs, sems):
        local, remote = self._ctx(ins, outs, sems)
        for cp in remote(True):
            cp.wait_recv()
        for cp in remote(False):
            cp.wait_send()
        for cp in local():
            cp.wait()


def _exchange_start(rider, name):
    n = rider.n
    parts = rider.arrays
    lands = [lax.empty(s.shape, s.dtype) for s in rider.out_shape()]
    hbm = pl.BlockSpec(memory_space=pltpu.HBM)
    sem = pl.BlockSpec(memory_space=pltpu.SEMAPHORE)

    def body(*refs):
        ins, sems = refs[:n], refs[2 * n:2 * n + 3]
        outs, token = refs[2 * n + 3 + n:2 * n + 3 + 2 * n], refs[-1]
        rider.start(ins, outs, sems)
        token[...] = jnp.zeros_like(token)

    res = pl.pallas_call(
        body, name=name,
        out_shape=rider.semaphores() + [pltpu.HBM(p.shape, p.dtype) for p in parts]
                  + [pltpu.HBM(z.shape, z.dtype) for z in lands] + [jax.ShapeDtypeStruct((8, 128), F32)],
        in_specs=[hbm] * (2 * n), out_specs=[sem] * 3 + [hbm] * (2 * n) + [pl.BlockSpec(memory_space=pltpu.VMEM)],
        input_output_aliases={i: 3 + i for i in range(2 * n)},
        compiler_params=pltpu.CompilerParams(has_side_effects=pltpu.SideEffectType.DATAFLOW_SIDE_EFFECTING),
    )(*[pltpu.with_memory_space_constraint(a, pltpu.HBM) for a in parts + lands])
    return res[:3], res[3:3 + n], res[3 + n:3 + 2 * n], res[-1]


def _exchange_wait(rider, started, after, name):
    n = rider.n
    sems, parts, lands, _ = started
    hbm = pl.BlockSpec(memory_space=pltpu.HBM)
    sem = pl.BlockSpec(memory_space=pltpu.SEMAPHORE)

    def body(*refs):
        rider.finish(refs[:n], refs[n:2 * n], refs[2 * n:2 * n + 3])

    res = pl.pallas_call(
        body, name=name, out_shape=[pltpu.HBM(a.shape, a.dtype) for a in list(parts) + list(lands)],
        in_specs=[hbm] * (2 * n) + [sem] * 3 + [ANY] * len(after), out_specs=[hbm] * (2 * n),
        input_output_aliases={i: i for i in range(2 * n)},
        compiler_params=pltpu.CompilerParams(has_side_effects=pltpu.SideEffectType.DATAFLOW_SIDE_EFFECTING),
    )(*parts, *lands, *sems, *after)
    return list(res[n:])


def _pcall(body, *, name, grid, in_specs, out_specs, out_shape, scratch_shapes=(), semantics, vmem_mb, rider=None,
           aliases=None, after=()):
    in_specs, out_specs, out_shape = list(in_specs), list(out_specs), list(out_shape)
    scratch_shapes = list(scratch_shapes)
    aliases = dict(aliases or {})
    if rider is None:
        n_in, after = len(in_specs), list(after)

        def plain(*refs):
            body(*refs[:n_in], *refs[n_in + len(after):])

        call = pl.pallas_call(plain if after else body, name=name, grid=grid, in_specs=in_specs + [ANY] * len(after),
                              out_specs=out_specs, out_shape=out_shape, scratch_shapes=scratch_shapes,
                              input_output_aliases=aliases, compiler_params=_params(semantics, vmem_mb))
        return lambda *args: (list(call(*args, *after)), None)
    n_in, n_out, n_scr = len(in_specs), len(out_specs), len(scratch_shapes)
    r_in, r_shapes = len(rider.arrays), rider.out_shape()
    r_out = len(r_shapes)
    aliases.update({n_in + i: n_out + o for i, o in rider.alias_pairs})
    total = math.prod(grid)
    mid_step = total - 1 if rider.has_mid and rider.late else (3 * total) // 4

    def wrapped(*refs):
        bounds = [0, n_in, r_in, n_out, r_out, n_scr]
        for i in range(1, len(bounds)):
            bounds[i] += bounds[i - 1]
        a, ra, o, ro, s = (refs[bounds[i]:bounds[i + 1]] for i in range(5))
        rs = refs[bounds[5]:]
        step = pl.program_id(0)
        for k in range(1, len(grid)):
            step = step * grid[k] + pl.program_id(k)
        pl.when(step == 0)(lambda: rider.start(ra, ro, rs))
        body(*a, *o, *s)
        if rider.has_mid:
            pl.when(step == mid_step)(lambda: rider.mid(ra, ro, rs))
        pl.when(step == total - 1)(lambda: rider.finish(ra, ro, rs))

    call = pl.pallas_call(
        wrapped, name=name, grid=grid, in_specs=in_specs + [ANY] * r_in, out_specs=out_specs + [ANY] * r_out,
        out_shape=out_shape + r_shapes, scratch_shapes=scratch_shapes + rider.scratch(),
        input_output_aliases=aliases, compiler_params=_params(("arbitrary",) * len(grid), vmem_mb))

    def run(*args):
        res = call(*args, *rider.arrays)
        return list(res[:n_out]), list(res[n_out:])

    return run


def _norm_matmul(x, g, w, *, name, out_dtype, tb, bn, relu=False, save_h=False, rider=None):
    t, d = x.shape
    n = w.shape[1]

    def body(x_ref, g_ref, w_ref, o_ref, *rest):
        h_scr = rest[-1]

        @pl.when(pl.program_id(1) == 0)
        def _():
            h = _rms_fwd(x_ref[...], g_ref[...])[0].astype(BF16)
            h_scr[...] = h
            if save_h:
                rest[0][...] = h

        acc = _dot(h_scr[...], w_ref[...])
        if relu:
            acc = jnp.maximum(acc, 0.0)
        o_ref[...] = acc.astype(out_dtype)

    out_shape = [jax.ShapeDtypeStruct((t, n), out_dtype)]
    out_specs = [pl.BlockSpec((tb, bn), lambda i, j: (i, j))]
    if save_h:
        out_shape.append(jax.ShapeDtypeStruct((t, d), BF16))
        out_specs.append(pl.BlockSpec((tb, d), lambda i, j: (i, 0)))
    res, extra = _pcall(
        body, name=name, grid=(t // tb, n // bn),
        in_specs=[pl.BlockSpec((tb, d), lambda i, j: (i, 0)),
                  pl.BlockSpec((1, d), lambda i, j: (0, 0)),
                  pl.BlockSpec((d, bn), lambda i, j: (0, j))],
        out_specs=out_specs, out_shape=out_shape,
        scratch_shapes=[pltpu.VMEM((tb, d), BF16)],
        semantics=("parallel", "arbitrary"), vmem_mb=48, rider=rider,
    )(x, g, w)
    res = res if save_h else res[0]
    return res if rider is None else (res, extra)


def _proj(x, g, w, *, tb, rider=None):
    t, d = x.shape
    half = w.shape[1] // 2

    def body(x_ref, g_ref, w_ref, qkv_ref, gates_ref, h_ref, h_scr):
        j = pl.program_id(1)

        @pl.when(j == 0)
        def _():
            h = _rms_fwd(x_ref[...], g_ref[...])[0].astype(BF16)
            h_scr[...] = h
            h_ref[...] = h

        acc = _dot(h_scr[...], w_ref[...])

        @pl.when(j == 0)
        def _():
            qkv_ref[...] = acc

        @pl.when(j == 1)
        def _():
            gates_ref[...] = acc.astype(BF16)

    tok = lambda c: pl.BlockSpec((tb, c), lambda i, j: (i, 0))
    res, extra = _pcall(
        body, name="proj", grid=(t // tb, 2),
        in_specs=[tok(d), pl.BlockSpec((1, d), lambda i, j: (0, 0)), pl.BlockSpec((d, half), lambda i, j: (0, j))],
        out_specs=[tok(half), tok(half), tok(d)],
        out_shape=[jax.ShapeDtypeStruct((t, half), F32), jax.ShapeDtypeStruct((t, half), BF16),
                   jax.ShapeDtypeStruct((t, d), BF16)],
        scratch_shapes=[pltpu.VMEM((tb, d), BF16)],
        semantics=("parallel", "arbitrary"), vmem_mb=48, rider=rider,
    )(x, g, w)
    return res if rider is None else (res, extra)


def _matmul_nt_normbwd(dy, w, x, g, dres, *, name, tb, also_bf16=False, to_natural=False, after=()):
    t, d = x.shape
    stacked = dy.ndim == 3
    n_i = SEG // TI
    if to_natural:
        tb = N_RES * TI

    def body(dy_ref, w_ref, x_ref, g_ref, dres_ref, *rest):
        rest = list(rest)
        dx_ref = rest.pop(0)
        dxb_ref = rest.pop(0) if also_bf16 else None
        gg_ref = rest.pop(0)
        i = pl.program_id(0)

        def rows(ref, *lead):
            v = ref[lead] if lead else ref[...]
            return v[0].reshape(tb, v.shape[-1]) if to_natural else v

        if stacked:
            kb = dy_ref.shape[-1]
            dh = _dot_nt(rows(dy_ref, 0), w_ref[:, 0:kb])
            for s in range(1, dy_ref.shape[0]):
                dh = dh + _dot_nt(rows(dy_ref, s), w_ref[:, s * kb:(s + 1) * kb])
        else:
            dh = _dot_nt(rows(dy_ref), w_ref[...])
        g_v = g_ref[...]
        _, xh, r = _rms_fwd(rows(x_ref), g_v)
        dx = _rms_bwd(dh, xh, r, g_v) + rows(dres_ref)
        if to_natural:
            scr = rest.pop(0)
            for cb in range(d // BLK):
                cols = slice(cb * BLK, (cb + 1) * BLK)
                slab = scr.at[cb]
                for res in range(N_RES):
                    slab[pl.ds(res, TI, stride=N_RES), :] = dx[res * TI:(res + 1) * TI, cols]
                dx_ref[:, cols] = slab[...]
        else:
            dx_ref[...] = dx
        if also_bf16:
            dxb_ref[...] = dx.astype(BF16)
        part = jnp.sum(dh * xh, axis=0, keepdims=True)

        @pl.when(i == 0)
        def _():
            gg_ref[...] = part

        @pl.when(i != 0)
        def _():
            gg_ref[...] += part

    tok = pl.BlockSpec((tb, d), lambda i: (i, 0))
    row = pl.BlockSpec((1, d), lambda i: (0, 0))
    if to_natural:
        act = pl.BlockSpec((1, N_RES, TI, d), lambda i: (i // n_i, 0, i % n_i, 0))
        dy_spec = pl.BlockSpec((dy.shape[0], 1, N_RES, TI, dy.shape[2]), lambda i: (0, i // n_i, 0, i % n_i, 0))
        dy, x, dres = dy.reshape(dy.shape[0], t // HALF, N_RES, SEG, dy.shape[2]), _x4(x), _x4(dres)
    elif stacked:
        act, dy_spec = tok, pl.BlockSpec((dy.shape[0], tb, dy.shape[2]), lambda i: (0, i, 0))
    else:
        act, dy_spec = tok, pl.BlockSpec((tb, dy.shape[1]), lambda i: (i, 0))
    in_specs = [dy_spec, pl.BlockSpec(w.shape, lambda i: (0, 0)), act, row, act]
    out_specs = [tok] + ([tok] if also_bf16 else []) + [row]
    out_shape = ([jax.ShapeDtypeStruct((t, d), F32)] + ([jax.ShapeDtypeStruct((t, d), BF16)] if also_bf16 else [])
                 + [jax.ShapeDtypeStruct((1, d), F32)])
    res, _ = _pcall(
        body, name=name, grid=(t // tb,), in_specs=in_specs, out_specs=out_specs, out_shape=out_shape,
        scratch_shapes=[pltpu.VMEM((d // BLK, tb, BLK), F32)] if to_natural else [],
        semantics=("arbitrary",), vmem_mb=56, after=after,
    )(dy, w, x, g, dres)
    return res


def _matmul_tn(a, b, *, name, bm, bn, square_a=False, after=()):
    t, m = a.shape
    stacked = b.ndim == 3
    n = b.shape[0] * bn if stacked else b.shape[1]

    def body(a_ref, b_ref, o_ref):
        av = a_ref[...]
        if square_a:
            av = av.astype(F32)
            av = (av * av).astype(BF16)
        o_ref[...] = _dot_tn(av, b_ref[...]).astype(BF16)

    res, _ = _pcall(
        body, name=name, grid=(m // bm, n // bn),
        in_specs=[pl.BlockSpec((t, bm), lambda i, j: (0, i)),
                  pl.BlockSpec((None, t, bn), lambda i, j: (j, 0, 0)) if stacked
                  else pl.BlockSpec((t, bn), lambda i, j: (0, j))],
        out_specs=[pl.BlockSpec((bm, bn), lambda i, j: (i, j))], out_shape=[jax.ShapeDtypeStruct((m, n), BF16)],
        semantics=("parallel", "parallel"), vmem_mb=56, after=after,
    )(a, b)
    return res[0]


N_RES = 16
SEG = 128
HALF = N_RES * SEG
TI = 32
HALO = 16


def _x4(a):
    return a.reshape(a.shape[0] // HALF, N_RES, SEG, a.shape[1])


def _reorder(arrays, name, rider=None):
    t, c = arrays[0].shape
    n = len(arrays)
    n_i = SEG // TI

    def body(*refs):
        scr = refs[-1]
        for i_ref, o_ref in zip(refs[:n], refs[n:2 * n]):
            for cb in range(c // BLK):
                cols = slice(cb * BLK, (cb + 1) * BLK)
                slab = scr.at[cb]
                slab[...] = i_ref[:, cols]
                for r in range(N_RES):
                    o_ref[0, r, :, cols] = slab[pl.ds(r, TI, stride=N_RES), :]

    res, extra = _pcall(
        body, name=name, grid=(t // (TI * N_RES),),
        in_specs=[pl.BlockSpec((TI * N_RES, c), lambda s: (s, 0))] * n,
        out_specs=[pl.BlockSpec((1, N_RES, TI, c), lambda s: (s // n_i, 0, s % n_i, 0))] * n,
        out_shape=[jax.ShapeDtypeStruct((t // HALF, N_RES, SEG, c), F32)] * n,
        scratch_shapes=[pltpu.VMEM((c // BLK, TI * N_RES, BLK), F32)],
        semantics=("parallel",), vmem_mb=32, rider=rider,
    )(*arrays)
    res = [r.reshape(t, c) for r in res]
    return res if rider is None else (res, extra)


_PATTERNS = ((1, 16, 8, SEG), (4, 4, 32, 4 * SEG), (16, 1, SEG, 0))
_FIRST = {1: 1, 4: 4, 16: 16}


def _group_rows(d, g):
    a = g >> 4
    if d == 16:
        base = a * HALF + (g & 15) * SEG
        prev = base - HALF
    elif d == 4:
        c = (g >> 2) & 3
        base = a * HALF + (g & 3) * SEG + c * 32
        prev = jnp.where(c > 0, base - 32, base - HALF + 96)
    else:
        c = g & 15
        base = a * HALF + c * 8
        prev = jnp.where(c > 0, base - 8, base - HALF + 120)
    return base, prev


def _load_rows(ref, base, n, rows, stride):
    parts = [ref[pl.ds(pl.multiple_of(base + j * stride, 8), rows), :] for j in range(n)]
    return parts[0] if n == 1 else jnp.concatenate(parts, axis=0)


def _store_rows(ref, base, val, n, rows, stride, add=False):
    for j in range(n):
        sl = pl.ds(pl.multiple_of(base + j * stride, 8), rows)
        piece = val[j * rows:(j + 1) * rows, :]
        if add:
            ref[sl, :] += piece
        else:
            ref[sl, :] = piece


def _band_bias(n, rows):
    shift = rows.bit_length() - 1
    lq = lax.broadcasted_iota(jnp.int32, (BLK, BLK), 0)
    lk = lax.broadcasted_iota(jnp.int32, (BLK, BLK), 1)
    iq = (lq & (rows - 1)) * n + (lq >> shift)
    ik = (lk & (rows - 1)) * n + (lk >> shift)
    zero = jnp.zeros((BLK, BLK), F32)
    return jnp.where(ik >= iq, zero, NEG_INF), jnp.where(ik <= iq, zero, NEG_INF)


def _set_bias(bias_scr, n, rows):
    prev_b, cur_b = _band_bias(n, rows)
    for half in range(2):
        bias_scr[half * BLK:(half + 1) * BLK, 0:BLK] = prev_b
        bias_scr[half * BLK:(half + 1) * BLK, BLK:2 * BLK] = cur_b


SCALE = 1.0 / math.sqrt(HEAD_DIM)


def _head_consts(value=1.0):
    lane_lo = lax.broadcasted_iota(jnp.int32, (BLK, BLK), 1) < HEAD_DIM
    return lane_lo, [jnp.where(lane_lo, value, 0.0).astype(BF16), jnp.where(lane_lo, 0.0, value).astype(BF16)]


def _stack_heads(v, head_mask):
    return jnp.concatenate([v * head_mask[0], v * head_mask[1]], axis=0)


def _unstack_heads(v2, lane_lo):
    return jnp.where(lane_lo, v2[:BLK], v2[BLK:])


def _rows_per_head(v, lane_lo):
    rolled = pltpu.roll(v, HEAD_DIM, axis=1)
    return jnp.concatenate([jnp.where(lane_lo, v, rolled), jnp.where(lane_lo, rolled, v)], axis=0)


WIDTH = 4


def _loop(lo, hi, fn, width=None):
    if width is None:
        def body(g, carry):
            fn(g)
            return carry

        if hi > lo:
            lax.fori_loop(lo, hi, body, 0)
        return
    while hi > lo:
        trips = (hi - lo) // width
        if trips:
            def body(i, carry, lo=lo, width=width):
                fn([lo + width * i + j for j in range(width)])
                return carry

            lax.fori_loop(0, trips, body, 0)
            lo += trips * width
        width = max(1, width // 2)


def _mix_weights(l1, l2, l3):
    mx = jnp.maximum(jnp.maximum(l1, l2), l3)
    e1, e2, e3 = jnp.exp(l1 - mx), jnp.exp(l2 - mx), jnp.exp(l3 - mx)
    inv = 1.0 / (e1 + e2 + e3)
    return e1 * inv, e2 * inv, e3 * inv


def _attention_fwd(qkv, rider=None):
    t = qkv.shape[0]
    groups = 16 * (t // HALF)

    def body(q_ref, k_ref, v_ref, attn_ref, l1_ref, l2_ref, l3_ref, o_scr, bias_scr):
        lane_lo, q_mask = _head_consts(SCALE)
        l_refs = (l1_ref, l2_ref, l3_ref)
        for p, (d, n, rows, stride) in enumerate(_PATTERNS):
            _set_bias(bias_scr, n, rows)
            o_p, l_p = o_scr.at[p], l_refs[p]

            def block(gs, has_prev):
                at = [_group_rows(d, g) for g in gs]

                def load(ref, b):
                    return _load_rows(ref, b, n, rows, stride).astype(BF16)

                q2 = [_stack_heads(load(q_ref, b), q_mask) for b, _ in at]
                k2 = [load(k_ref, b) for b, _ in at]
                v2 = [load(v_ref, b) for b, _ in at]
                if has_prev:
                    k2 = [jnp.concatenate([load(k_ref, pv), k], axis=0) for (_, pv), k in zip(at, k2)]
                    v2 = [jnp.concatenate([load(v_ref, pv), v], axis=0) for (_, pv), v in zip(at, v2)]
                s = [_dot_nt(q, k) for q, k in zip(q2, k2)]
                s = [x + (bias_scr[...] if has_prev else bias_scr[:, BLK:2 * BLK]) for x in s]
                mx = [jnp.max(x, axis=1, keepdims=True) for x in s]
                e = [jnp.exp(x - m) for x, m in zip(s, mx)]
                den = [jnp.sum(x, axis=1, keepdims=True) for x in e]
                o2 = [_dot(x.astype(BF16), v) * (1.0 / dn) for x, v, dn in zip(e, v2, den)]
                lse2 = [jnp.broadcast_to(m + jnp.log(dn), (2 * BLK, BLK)) for m, dn in zip(mx, den)]
                for (b, _), o, l in zip(at, o2, lse2):
                    _store_rows(o_p, b, _unstack_heads(o, lane_lo), n, rows, stride)
                    _store_rows(l_p, b, _unstack_heads(l, lane_lo), n, rows, stride)

            _loop(0, _FIRST[d], lambda gs: block(gs, False), width=2 * WIDTH)
            _loop(_FIRST[d], groups, lambda gs: block(gs, True), width=2 * WIDTH)

        def mix(i):
            sl = pl.ds(pl.multiple_of(i * 256, 256), 256)
            w = _mix_weights(l1_ref[sl, :], l2_ref[sl, :], l3_ref[sl, :])
            attn_ref[sl, :] = w[0] * o_scr[0, sl, :] + w[1] * o_scr[1, sl, :] + w[2] * o_scr[2, sl, :]

        _loop(0, t // 256, mix)

    def col(c0):
        return pl.BlockSpec((t, BLK), lambda hp: (0, c0 + hp))

    res, extra = _pcall(
        body, name="attention_fwd", grid=(4,), in_specs=[col(0), col(4), col(8)], out_specs=[col(0)] * 4,
        out_shape=[jax.ShapeDtypeStruct((t, 512), F32)] * 4,
        scratch_shapes=[pltpu.VMEM((3, t, BLK), F32), pltpu.VMEM((2 * BLK, 2 * BLK), F32)],
        semantics=("parallel",), vmem_mb=48, rider=rider,
    )(qkv, qkv, qkv)
    return res if rider is None else (res, extra)


def _attention_bwd(qkv, dattn, dsum, lses, dproj):
    t = qkv.shape[0]
    groups = 16 * (t // HALF)

    def body(q_ref, k_ref, v_ref, da_ref, ds_ref, l1_ref, l2_ref, l3_ref, kept_ref, out_ref, acc, bias_scr):
        del kept_ref
        lane_lo, head_mask = _head_consts()
        q_mask = _head_consts(SCALE)[1]
        l_refs = (l1_ref, l2_ref, l3_ref)

        def clear(i):
            sl = pl.ds(pl.multiple_of(i * 512, 512), 512)
            for s in range(3):
                acc[s, sl, :] = jnp.zeros((512, BLK), F32)

        _loop(0, t // 512, clear)
        dq_acc, dk_acc, dv_acc = acc.at[0], acc.at[1], acc.at[2]
        for p, (d, n, rows, stride) in enumerate(_PATTERNS):
            _set_bias(bias_scr, n, rows)

            def block(gs, has_prev):
                at = [_group_rows(d, g) for g in gs]

                def load(ref, b):
                    return _load_rows(ref, b, n, rows, stride)

                def put(ref, b, val):
                    _store_rows(ref, b, val, n, rows, stride, add=True)

                def wide(x):
                    return jnp.concatenate([x, x], axis=1) if has_prev else x

                lse = [[load(ref, b) for ref in l_refs] for b, _ in at]
                w = [_mix_weights(*ls)[p] for ls in lse]
                do2 = [_stack_heads((wg * load(da_ref, b)).astype(BF16), head_mask) for wg, (b, _) in zip(w, at)]
                dl2 = [wide(_rows_per_head(wg * load(ds_ref, b), lane_lo)) for wg, (b, _) in zip(w, at)]
                lse2 = [wide(_rows_per_head(ls[p], lane_lo)) for ls in lse]
                q2 = [_stack_heads(load(q_ref, b).astype(BF16), q_mask) for b, _ in at]
                k2 = [load(k_ref, b).astype(BF16) for b, _ in at]
                v2 = [load(v_ref, b).astype(BF16) for b, _ in at]
                if has_prev:
                    k2 = [jnp.concatenate([load(k_ref, pv).astype(BF16), k], axis=0) for (_, pv), k in zip(at, k2)]
                    v2 = [jnp.concatenate([load(v_ref, pv).astype(BF16), v], axis=0) for (_, pv), v in zip(at, v2)]
                s = [_dot_nt(q, k) for q, k in zip(q2, k2)]
                dp = [_dot_nt(do, v) for do, v in zip(do2, v2)]
                pr = [jnp.exp(x + (bias_scr[...] if has_prev else bias_scr[:, BLK:2 * BLK]) - l)
                      for x, l in zip(s, lse2)]
                ds = [(pg * (x - dl)).astype(BF16) for pg, x, dl in zip(pr, dp, dl2)]
                dq2 = [_dot(x, k) * SCALE for x, k in zip(ds, k2)]
                dk2 = [_dot_tn(x, q) for x, q in zip(ds, q2)]
                dv2 = [_dot_tn(pg.astype(BF16), do) for pg, do in zip(pr, do2)]
                for (b, pv), dq, dk, dv in zip(at, dq2, dk2, dv2):
                    put(dq_acc, b, _unstack_heads(dq, lane_lo))
                    if has_prev:
                        put(dk_acc, pv, dk[:BLK])
                        put(dv_acc, pv, dv[:BLK])
                        put(dk_acc, b, dk[BLK:])
                        put(dv_acc, b, dv[BLK:])
                    else:
                        put(dk_acc, b, dk)
                        put(dv_acc, b, dv)

            _loop(0, _FIRST[d], lambda gs: block(gs, False), width=WIDTH)
            _loop(_FIRST[d], groups, lambda gs: block(gs, True), width=WIDTH)

        def emit(i):
            sl = pl.ds(pl.multiple_of(i * 512, 512), 512)
            for s in range(3):
                out_ref[s, sl, :] = acc[s, sl, :].astype(BF16)

        _loop(0, t // 512, emit)

    def col(c0):
        return pl.BlockSpec((t, BLK), lambda hp: (0, c0 + hp))

    res, _ = _pcall(
        body, name="attention_bwd", grid=(4,),
        in_specs=[col(0), col(4), col(8)] + [col(0)] * 5 + [ANY],
        out_specs=[pl.BlockSpec((3, t, BLK), lambda hp: (0, 0, hp))],
        out_shape=[jax.ShapeDtypeStruct(dproj.shape, BF16)],
        scratch_shapes=[pltpu.VMEM((3, t, BLK), F32), pltpu.VMEM((2 * BLK, 2 * BLK), F32)],
        semantics=("parallel",), vmem_mb=56, aliases={8: 0},
    )(qkv, qkv, qkv, dattn, dsum, *lses, dproj)
    return res[0]


def _order_specs(t):
    n_i = SEG // TI
    nblk = (t // HALF) * n_i
    per = TI // HALO

    def main(c, col=0):
        return pl.BlockSpec((1, N_RES, TI, c), lambda s: (s // n_i, 0, s % n_i, col))

    def before(c, col=0):
        return pl.BlockSpec((1, 2, HALO, c), lambda s: (jnp.maximum(s - 1, 0) // n_i, N_RES // 2 - 1,
                                                        (jnp.maximum(s - 1, 0) % n_i) * per + per - 1, col))

    def after(c, col=0):
        return pl.BlockSpec((1, 2, HALO, c), lambda s: (jnp.minimum(s + 1, nblk - 1) // n_i, 0,
                                                        (jnp.minimum(s + 1, nblk - 1) % n_i) * per, col))

    return nblk, main, before, after


def _shift_in(v, row_in, up):
    rows = v.shape[0]
    idx = lax.broadcasted_iota(jnp.int32, v.shape, 0)
    fill = jnp.broadcast_to(row_in, v.shape)
    if up:
        return jnp.where(idx == rows - 1, fill, pltpu.roll(v, rows - 1, axis=0))
    return jnp.where(idx == 0, fill, pltpu.roll(v, 1, axis=0))


def _taps_behind(u, before):
    s15 = _shift_in(u[N_RES - 1], before[1, HALO - 1:HALO, :], up=False)
    s14 = _shift_in(u[N_RES - 2], before[0, HALO - 1:HALO, :], up=False)
    m1 = jnp.concatenate([s15[None], u[:N_RES - 1]], axis=0)
    m2 = jnp.concatenate([s14[None], s15[None], u[:N_RES - 2]], axis=0)
    return m1, m2


def _taps_ahead(u, after):
    t0 = _shift_in(u[0], after[0, 0:1, :], up=True)
    t1 = _shift_in(u[1], after[1, 0:1, :], up=True)
    p1 = jnp.concatenate([u[1:], t0[None]], axis=0)
    p2 = jnp.concatenate([u[2:], t0[None], t1[None]], axis=0)
    return p1, p2


def _conv_fwd(gates, before, first, cw):
    gates, before = gates.astype(F32), before.astype(F32)
    bg, cg, xc = gates[..., 0:512], gates[..., 512:1024], gates[..., 1024:1536]
    u = cg * xc
    ub = before[..., 512:1024] * before[..., 1024:1536]
    ub = jnp.where(first, jnp.zeros_like(ub), ub)
    m1, m2 = _taps_behind(u, ub)
    conv = m2 * cw[0:1, :] + m1 * cw[1:2, :] + u * cw[2:3, :]
    return bg, u, m1, m2, conv


def _sum_tokens(v):
    return jnp.sum(jnp.sum(v, axis=0), axis=0, keepdims=True)


def _mixer_fwd(x, attn, gates, cw, g_a, g_c, w_out):
    t, d = x.shape
    nblk, main, before, _ = _order_specs(t)
    rows = N_RES * TI

    def body(x_ref, at_ref, gt_ref, gb_ref, cw_ref, ga_ref, gc_ref, wa_ref, wb_ref, x1_ref, mg_ref):
        an = _rms_fwd(at_ref[0], ga_ref[...])[0].astype(BF16)
        bg, _, _, _, conv = _conv_fwd(gt_ref[0], gb_ref[0], pl.program_id(0) == 0, cw_ref[...])
        cn = _rms_fwd(bg * conv, gc_ref[...])[0].astype(BF16)
        mg_ref[0, :, :, 0:512] = an
        mg_ref[0, :, :, 512:1024] = cn
        y = _dot(an.reshape(rows, 512), wa_ref[...]) + _dot(cn.reshape(rows, 512), wb_ref[...])
        x1_ref[0] = x_ref[0] + y.reshape(N_RES, TI, d)

    const = lambda r, c, i0=0: pl.BlockSpec((r, c), lambda s: (i0, 0))
    x1, merged = pl.pallas_call(
        body, name="mixer_fwd", grid=(nblk,),
        in_specs=[main(d), main(512), main(1536), before(1536), const(3, 512), const(1, 512), const(1, 512),
                  const(512, d), const(512, d, 1)],
        out_specs=[main(d), main(d)],
        out_shape=[jax.ShapeDtypeStruct(_x4(x).shape, F32), jax.ShapeDtypeStruct(_x4(x).shape, BF16)],
        compiler_params=_params(("parallel",), 48),
    )(_x4(x), _x4(attn), _x4(gates), _x4(gates), cw, g_a, g_c, w_out, w_out)
    return x1.reshape(t, d), merged.reshape(t, d)


def _mixer_bwd(dx1, merged, attn, gates, cw, g_a, g_c, w_out, head_sum):
    t, d = dx1.shape
    nblk, main, before, _ = _order_specs(t)
    rows = N_RES * TI

    def body(dx_ref, mg_ref, at_ref, gt_ref, gb_ref, cw_ref, ga_ref, gc_ref, wa_ref, wb_ref, hs_ref,
             da_ref, dsum_ref, dy_ref, gga_ref, ggc_ref, gw_ref, acc_w):
        s = pl.program_id(0)
        dxb = dx_ref[0].reshape(rows, d).astype(BF16)

        @pl.when(s == 0)
        def _():
            acc_w[...] = jnp.zeros_like(acc_w)

        acc_w[...] += _dot_tn(mg_ref[0].reshape(rows, d), dxb)

        @pl.when(s == nblk - 1)
        def _():
            gw_ref[...] = acc_w[...].astype(BF16)

        dma = _dot_nt(dxb, wa_ref[...]).reshape(N_RES, TI, 512)
        dmc = _dot_nt(dxb, wb_ref[...]).reshape(N_RES, TI, 512)
        attn_v, g_av = at_ref[0], ga_ref[...]
        _, ah, ra = _rms_fwd(attn_v, g_av)
        dattn = _rms_bwd(dma, ah, ra, g_av)
        da_ref[0] = dattn
        z = (dattn * attn_v).reshape(rows, 512)
        hs = hs_ref[...]
        z1 = z.astype(BF16)
        z2 = (z - z1.astype(F32)).astype(BF16)
        dsum_ref[0] = (_dot(z1, hs) + _dot(z2, hs)).reshape(N_RES, TI, 512)
        bg, _, _, _, conv = _conv_fwd(gt_ref[0], gb_ref[0], s == 0, cw_ref[...])
        g_cv = gc_ref[...]
        _, yh, rc = _rms_fwd(bg * conv, g_cv)
        dy_ref[0] = _rms_bwd(dmc, yh, rc, g_cv)
        pa, pc = _sum_tokens(dma * ah), _sum_tokens(dmc * yh)

        @pl.when(s == 0)
        def _():
            gga_ref[...] = pa
            ggc_ref[...] = pc

        @pl.when(s != 0)
        def _():
            gga_ref[...] += pa
            ggc_ref[...] += pc

    const = lambda r, c, i0=0: pl.BlockSpec((r, c), lambda s: (i0, 0))
    shape4 = _x4(attn).shape
    res, _ = _pcall(
        body, name="mixer_bwd", grid=(nblk,),
        in_specs=[main(d), main(d), main(512), main(1536), before(1536), const(3, 512), const(1, 512), const(1, 512),
                  const(512, d), const(512, d, 1), const(512, 512)],
        out_specs=[main(512)] * 3 + [const(1, 512), const(1, 512), const(d, d)],
        out_shape=[jax.ShapeDtypeStruct(shape4, F32)] * 3 + [jax.ShapeDtypeStruct((1, 512), F32)] * 2
        + [jax.ShapeDtypeStruct((d, d), BF16)],
        scratch_shapes=[pltpu.VMEM((d, d), F32)],
        semantics=("arbitrary",), vmem_mb=48,
    )(_x4(dx1), _x4(merged), _x4(attn), _x4(gates), _x4(gates), cw, g_a, g_c, w_out, w_out, head_sum)
    return [r.reshape(t, 512) for r in res[:3]] + res[3:]


def _conv_bwd(dy, gates, cw, after=()):
    t = dy.shape[0]
    nblk, main, before, ahead = _order_specs(t)
    n_i = SEG // TI

    def body(dy_ref, dya_ref, gt_ref, gb_ref, ga_ref, cw_ref, dp_ref, gcw_ref):
        s = pl.program_id(0)
        cw_v, gates_v = cw_ref[...], gt_ref[0]
        bg, u, m1, m2, conv = _conv_fwd(gates_v, gb_ref[0], s == 0, cw_v)
        dy_v = dy_ref[0]
        dconv = dy_v * bg
        dca = dya_ref[0] * ga_ref[0][..., 0:512].astype(F32)
        dca = jnp.where(s == nblk - 1, jnp.zeros_like(dca), dca)
        p1, p2 = _taps_ahead(dconv, dca)
        du = dconv * cw_v[2:3, :] + p1 * cw_v[1:2, :] + p2 * cw_v[0:1, :]
        dp_ref[0, 0] = (dy_v * conv).astype(BF16)
        dp_ref[1, 0] = (du * gates_v[..., 1024:1536].astype(F32)).astype(BF16)
        dp_ref[2, 0] = (du * gates_v[..., 512:1024].astype(F32)).astype(BF16)
        parts = [_sum_tokens(dconv * m2), _sum_tokens(dconv * m1), _sum_tokens(dconv * u)]

        @pl.when(s == 0)
        def _():
            gcw_ref[...] = jnp.zeros_like(gcw_ref)

        for tap in range(3):
            gcw_ref[tap:tap + 1, :] += parts[tap]

    (dproj, gcw), _ = _pcall(
        body, name="conv_bwd", grid=(nblk,),
        in_specs=[main(512), ahead(512), main(1536), before(1536), ahead(1536),
                  pl.BlockSpec((3, 512), lambda s: (0, 0))],
        out_specs=[pl.BlockSpec((3, 1, N_RES, TI, 512), lambda s: (1, s // n_i, 0, s % n_i, 0)),
                   pl.BlockSpec((8, 512), lambda s: (0, 0))],
        out_shape=[jax.ShapeDtypeStruct((6, t // HALF, N_RES, SEG, 512), BF16), jax.ShapeDtypeStruct((8, 512), F32)],
        semantics=("arbitrary",), vmem_mb=40, after=after,
    )(_x4(dy), _x4(dy), _x4(gates), _x4(gates), _x4(gates), cw)
    return dproj.reshape(6, t, 512), gcw


def _xattn_fwd(x1, g, w_q, kv, w_o, *, tb):
    t, d = x1.shape
    hd = d // N_MEM_HEADS
    m = kv.shape[0]

    def body(x_ref, g_ref, wq_ref, k_ref, v_ref, wo_ref, x2_ref, h_ref, q_ref, o_ref):
        xv = x_ref[...]
        h = _rms_fwd(xv, g_ref[...])[0].astype(BF16)
        h_ref[...] = h
        q = _dot(h, wq_ref[...]).astype(BF16)
        q_ref[...] = q
        for hh in range(N_MEM_HEADS):
            sl = slice(hh * hd, (hh + 1) * hd)
            s = _dot_nt(q[:, sl], k_ref[:, sl]) * (1.0 / 16.0)
            e = jnp.exp(s - jnp.max(s, axis=1, keepdims=True))
            p = e / jnp.sum(e, axis=1, keepdims=True)
            o_ref[:, sl] = _dot(p.astype(BF16), v_ref[:, sl]).astype(BF16)
        x2_ref[...] = xv + _dot(o_ref[...], wo_ref[...])

    tok = pl.BlockSpec((tb, d), lambda i: (i, 0))
    full = pl.BlockSpec((d, d), lambda i: (0, 0))
    return pl.pallas_call(
        body, name="xattn_fwd", grid=(t // tb,),
        in_specs=[tok, pl.BlockSpec((1, d), lambda i: (0, 0)), full,
                  pl.BlockSpec((m, d), lambda i: (0, 0)), pl.BlockSpec((m, d), lambda i: (0, 1)), full],
        out_specs=[tok] * 4,
        out_shape=[jax.ShapeDtypeStruct((t, d), F32)] + [jax.ShapeDtypeStruct((t, d), BF16)] * 3,
        compiler_params=_params(("parallel",), 48),
    )(x1, g, w_q, kv, kv, w_o)


def _xattn_bwd(dx2, x1, g, q, w_q, kv, w_o, *, tb):
    t, d = x1.shape
    hd = d // N_MEM_HEADS
    m = kv.shape[0]

    def body(dx2_ref, x_ref, g_ref, q_ref, wq_ref, k_ref, v_ref, wo_ref,
             dx1_ref, dq_ref, dk_ref, dv_ref, gg_ref):
        i = pl.program_id(0)

        @pl.when(i == 0)
        def _():
            dk_ref[...] = jnp.zeros_like(dk_ref)
            dv_ref[...] = jnp.zeros_like(dv_ref)

        dx2 = dx2_ref[...]
        do = _dot_nt(dx2.astype(BF16), wo_ref[...]).astype(BF16)
        for hh in range(N_MEM_HEADS):
            sl = slice(hh * hd, (hh + 1) * hd)
            qh, kh, vh, doh = q_ref[:, sl], k_ref[:, sl], v_ref[:, sl], do[:, sl]
            s = _dot_nt(qh, kh) * (1.0 / 16.0)
            e = jnp.exp(s - jnp.max(s, axis=1, keepdims=True))
            p = e / jnp.sum(e, axis=1, keepdims=True)
            dp = _dot_nt(doh, vh)
            ds = (p * (dp - jnp.sum(dp * p, axis=1, keepdims=True)) * (1.0 / 16.0)).astype(BF16)
            dq_ref[:, sl] = _dot(ds, kh).astype(BF16)
            dk_ref[:, sl] += _dot_tn(ds, qh)
            dv_ref[:, sl] += _dot_tn(p.astype(BF16), doh)
        dh = _dot_nt(dq_ref[...], wq_ref[...])
        g_v = g_ref[...]
        _, xh, r = _rms_fwd(x_ref[...], g_v)
        dx1 = dx2 + _rms_bwd(dh, xh, r, g_v)
        dx1_ref[...] = dx1
        part = jnp.sum(dh * xh, axis=0, keepdims=True)

        @pl.when(i == 0)
        def _():
            gg_ref[...] = part

        @pl.when(i != 0)
        def _():
            gg_ref[...] += part

    tok = pl.BlockSpec((tb, d), lambda i: (i, 0))
    full = pl.BlockSpec((d, d), lambda i: (0, 0))
    acc = pl.BlockSpec((m, d), lambda i: (0, 0))
    res, _ = _pcall(
        body, name="xattn_bwd", grid=(t // tb,),
        in_specs=[tok, tok, pl.BlockSpec((1, d), lambda i: (0, 0)), tok, full,
                  pl.BlockSpec((m, d), lambda i: (0, 0)), pl.BlockSpec((m, d), lambda i: (0, 1)), full],
        out_specs=[tok, tok, acc, acc, pl.BlockSpec((1, d), lambda i: (0, 0))],
        out_shape=[jax.ShapeDtypeStruct((t, d), F32), jax.ShapeDtypeStruct((t, d), BF16),
                   jax.ShapeDtypeStruct((m, d), F32), jax.ShapeDtypeStruct((m, d), F32),
                   jax.ShapeDtypeStruct((1, d), F32)],
        semantics=("arbitrary",), vmem_mb=48,
    )(dx2, x1, g, q, w_q, kv, kv, w_o)
    return res


def _mem_bwd(dk, dv, w_kv, mem, mem_n, g):
    m, d = mem.shape

    def body(dk_ref, dv_ref, w_ref, x_ref, h_ref, g_ref, gw_ref, gg_ref):
        h = h_ref[...]
        dh = jnp.zeros((m, d), F32)
        for i, dy_ref in enumerate((dk_ref, dv_ref)):
            cols = slice(i * d, (i + 1) * d)
            dy = dy_ref[...].astype(BF16)
            gw_ref[:, cols] = _dot_tn(h, dy).astype(BF16)
            dh = dh + _dot_nt(dy, w_ref[:, cols])
        xh = _rms_fwd(x_ref[...], g_ref[...])[1]
        gg_ref[...] = jnp.sum(dh * xh, axis=0, keepdims=True)

    return pl.pallas_call(
        body, name="mem_bwd",
        out_shape=[jax.ShapeDtypeStruct(w_kv.shape, BF16), jax.ShapeDtypeStruct((1, d), F32)],
        compiler_params=pltpu.CompilerParams(vmem_limit_bytes=32 << 20),
    )(dk, dv, w_kv, mem, mem_n, g)


def _mlp_down_loss(a, w_down, x2, tgt, g, *, tb):
    t, d = x2.shape
    f = a.shape[1]

    def body(a_ref, w_ref, x_ref, t_ref, g_ref, dx_ref, dxb_ref, loss_ref, gg_ref):
        i = pl.program_id(0)
        av = a_ref[...].astype(F32)
        x3 = x_ref[...] + _dot((av * av).astype(BF16), w_ref[...])
        g_v = g_ref[...]
        out, xh, r = _rms_fwd(x3, g_v)
        err = out - t_ref[...]
        dout = err * (1.0 / d)
        dx = _rms_bwd(dout, xh, r, g_v)
        dx_ref[...] = dx
        dxb_ref[...] = dx.astype(BF16)
        part = jnp.sum(dout * xh, axis=0, keepdims=True)
        lpart = 0.5 * jnp.sum(jnp.mean(err * err, axis=-1, keepdims=True), axis=0, keepdims=True)
        lpart = jnp.broadcast_to(lpart, loss_ref.shape)

        @pl.when(i == 0)
        def _():
            gg_ref[...] = part
            loss_ref[...] = lpart

        @pl.when(i != 0)
        def _():
            gg_ref[...] += part
            loss_ref[...] += lpart

    tok = pl.BlockSpec((tb, d), lambda i: (i, 0))
    return pl.pallas_call(
        body, name="mlp_down_loss", grid=(t // tb,),
        in_specs=[pl.BlockSpec((tb, f), lambda i: (i, 0)), pl.BlockSpec((f, d), lambda i: (0, 0)), tok, tok,
                  pl.BlockSpec((1, d), lambda i: (0, 0))],
        out_specs=[tok, tok, pl.BlockSpec((8, 128), lambda i: (0, 0)), pl.BlockSpec((1, d), lambda i: (0, 0))],
        out_shape=[jax.ShapeDtypeStruct((t, d), F32), jax.ShapeDtypeStruct((t, d), BF16),
                   jax.ShapeDtypeStruct((8, 128), F32), jax.ShapeDtypeStruct((1, d), F32)],
        compiler_params=_params(("arbitrary",), 56),
    )(a, w_down, x2, tgt, g)


def _mlp_dpre(dx3, w_down, a, *, tb, bn, after=()):
    t, d = dx3.shape
    f = a.shape[1]

    def body(dx_ref, w_ref, a_ref, o_ref):
        o_ref[...] = (2.0 * a_ref[...].astype(F32) * _dot_nt(dx_ref[...], w_ref[...])).astype(BF16)

    (dpre,), _ = _pcall(
        body, name="mlp_dpre", grid=(t // tb, f // bn),
        in_specs=[pl.BlockSpec((tb, d), lambda i, j: (i, 0)), pl.BlockSpec((bn, d), lambda i, j: (j, 0)),
                  pl.BlockSpec((tb, bn), lambda i, j: (i, j))],
        out_specs=[pl.BlockSpec((tb, bn), lambda i, j: (i, j))],
        out_shape=[jax.ShapeDtypeStruct((t, f), BF16)],
        semantics=("parallel", "arbitrary"), vmem_mb=48, after=after,
    )(dx3, w_down, a)
    return dpre


def _adamw(gsum, w, m, v):
    m_new = ADAM_B1 * m + (1.0 - ADAM_B1) * gsum
    v_new = ADAM_B2 * v + (1.0 - ADAM_B2) * (gsum * gsum)
    m_hat = m_new / (1.0 - ADAM_B1 ** ADAM_STEP)
    v_hat = v_new / (1.0 - ADAM_B2 ** ADAM_STEP)
    delta = -ADAM_LR * (m_hat / (jnp.sqrt(v_hat) + ADAM_EPS) + ADAM_WD * w)
    return delta, m_new, v_new


def _sum_adamw(shards, *, name, tr):
    r, c = shards[0][1].shape
    n = len(shards)

    def body(*refs):
        for s in range(n):
            p_ref, w_ref, m_ref, v_ref = refs[4 * s:4 * s + 4]
            g_ref, d_ref, mo_ref, vo_ref = refs[4 * (n + s):4 * (n + s) + 4]
            g = p_ref[0].astype(F32)
            for k in range(1, N_DEV):
                g = g + p_ref[k].astype(F32)
            g_ref[...] = g
            d_ref[...], mo_ref[...], vo_ref[...] = _adamw(g, w_ref[...], m_ref[...], v_ref[...])

    blk = pl.BlockSpec((tr, c), lambda i: (i, 0))
    res = pl.pallas_call(
        body, name=name, grid=(r // tr,),
        in_specs=[pl.BlockSpec((N_DEV, tr, c), lambda i: (0, i, 0)), blk, blk, blk] * n,
        out_specs=[blk] * (4 * n), out_shape=[jax.ShapeDtypeStruct((r, c), F32)] * (4 * n),
        compiler_params=_params(("parallel",), 40),
    )(*[pltpu.with_memory_space_constraint(a, pltpu.HBM) for shard in shards for a in shard])
    return [res[4 * s:4 * s + 4] for s in range(n)]


_GAIN_ROWS = ("g_mix", "g_xattn", "g_mem", "g_mlp", "g_final")
PAIR_ROW = 5
LOSS_ROW = 6
TAPS_ROW = 8
SMALL_ROWS = 16
_SMALL = _GAIN_ROWS + ("g_attn_out", "g_conv_out", "conv_w")
CONV_SHARD = 512 // N_DEV


def _pack_small(gains, gg_attn, gg_conv, gcw, loss_blk):
    def body(*refs):
        o_ref = refs[-1]
        ga_ref, gc_ref, cw_ref, l_ref = refs[len(gains):-1]
        o_ref[...] = jnp.zeros_like(o_ref)
        for i, g_ref in enumerate(refs[:len(gains)]):
            o_ref[i:i + 1, :] = g_ref[...]
        o_ref[PAIR_ROW:PAIR_ROW + 1, 0:512] = ga_ref[...]
        o_ref[PAIR_ROW:PAIR_ROW + 1, 512:1024] = gc_ref[...]
        o_ref[LOSS_ROW:LOSS_ROW + 1, 0:BLK] = l_ref[0:1, :]
        o_ref[TAPS_ROW:SMALL_ROWS, 0:512] = cw_ref[...]

    return pl.pallas_call(body, name="pack_small", out_shape=jax.ShapeDtypeStruct((SMALL_ROWS, 1024), F32))(
        *gains, gg_attn, gg_conv, gcw, loss_blk)


def _update_small(parts, me, w, m, v):
    n = len(_SMALL)

    def body(me_ref, p_ref, *refs):
        ins, loss_ref, outs = refs[:3 * n], refs[3 * n], refs[3 * n + 1:]

        def total(lo, hi):
            s = p_ref[0, lo:hi, :]
            for k in range(1, N_DEV):
                s = s + p_ref[k, lo:hi, :]
            return s

        grads = {k: total(i, i + 1) for i, k in enumerate(_GAIN_ROWS)}
        both = total(PAIR_ROW, PAIR_ROW + 1)
        grads["g_attn_out"], grads["g_conv_out"] = both[:, 0:512], both[:, 512:1024]
        taps = total(TAPS_ROW, SMALL_ROWS)
        mine = jnp.zeros((SMALL_ROWS - TAPS_ROW, BLK), F32)
        for j in range(N_DEV):
            lo = j * CONV_SHARD // BLK * BLK
            blk = taps[:, lo:lo + BLK]
            if j * CONV_SHARD != lo:
                blk = pltpu.roll(blk, BLK - (j * CONV_SHARD - lo), axis=1)
            mine = jnp.where(me_ref[0] == j, blk, mine)
        grads["conv_w"] = mine[0:3, 0:CONV_SHARD]
        loss_ref[...] = total(LOSS_ROW, LOSS_ROW + 1)[:, 0:1]
        for i, k in enumerate(_SMALL):
            g_ref, d_ref, mo_ref, vo_ref = outs[4 * i:4 * i + 4]
            g_ref[...] = grads[k]
            d_ref[...], mo_ref[...], vo_ref[...] = _adamw(grads[k], ins[i][...], ins[n + i][...], ins[2 * n + i][...])

    vmem = pl.BlockSpec(memory_space=pltpu.VMEM)
    args = [d[k] for d in (w, m, v) for k in _SMALL]
    res = pl.pallas_call(
        body, name="update_small",
        in_specs=[pl.BlockSpec(memory_space=pltpu.SMEM)] + [vmem] * (1 + 3 * n),
        out_shape=[jax.ShapeDtypeStruct((1, 1), F32)] + [jax.ShapeDtypeStruct(w[k].shape, F32) for k in _SMALL
                                                         for _ in range(4)],
    )(me, parts, *args)
    return res[0], {k: res[1 + 4 * i:5 + 4 * i] for i, k in enumerate(_SMALL)}


def _head_sum_matrix():
    r = lax.broadcasted_iota(jnp.int32, (512, 512), 0) // HEAD_DIM
    c = lax.broadcasted_iota(jnp.int32, (512, 512), 1) // HEAD_DIM
    return (r == c).astype(BF16)


_SHARD_AXIS = dict(w_in=1, w_out=0, w_q=0, w_kv=1, w_o=0, w_up=1, w_down=0, conv_w=None, small=None)


class _Weights:
    def __init__(self, full, shards=None):
        self.full = dict(full)
        self.shards = shards

    def rider(self, names, late=False):
        if self.shards is None:
            return None
        return _Gather([self.shards[n] for n in names], [_SHARD_AXIS[n] for n in names], late)

    def arrived(self, names, gathered):
        if gathered is not None:
            for n, g in zip(names, gathered):
                self.full[n] = g.transpose(1, 0, 2).reshape(g.shape[1], -1) if n == "conv_w" else g

    def __getitem__(self, name):
        return self.full[name]


class _Grads:
    def __init__(self, distributed):
        self.distributed = distributed
        self.local = {}
        self.pending = {}

    def add(self, name, g):
        self.local[name] = g

    def send(self, *names):
        if not self.distributed:
            return []
        rider = _Exchange([self.local[n] for n in names], [_SHARD_AXIS[n] for n in names])
        started = _exchange_start(rider, "send_" + "_".join(names))
        self.pending[names[0]] = (names, rider, started)
        return [started[3]]

    def wait(self, first_name, after):
        names, rider, started = self.pending.pop(first_name)
        return _exchange_wait(rider, started, after, "wait_" + "_".join(names))


def _ride(fn, *args, rider=None, **kw):
    if rider is None:
        return fn(*args, **kw), None
    return fn(*args, rider=rider, **kw)


def _local_step(x, mem, tgt, gains, weights, grads):
    names = ["w_in", "conv_w"]
    (x, tgt), got = _ride(_reorder, [x, tgt], "reorder_in", rider=weights.rider(names, late=True))
    weights.arrived(names, got)
    w_in, cw = weights["w_in"], weights["conv_w"]

    names = ["w_out", "w_kv"]
    (qkv, gates, h1), got = _ride(_proj, x, gains["g_mix"], w_in, tb=1024, rider=weights.rider(names))
    weights.arrived(names, got)
    names = ["w_q", "w_o", "w_up"]
    (attn, *lses), got = _ride(_attention_fwd, qkv, rider=weights.rider(names))
    weights.arrived(names, got)
    x1, merged = _mixer_fwd(x, attn, gates, cw, gains["g_attn_out"], gains["g_conv_out"], weights["w_out"])
    kv, mem_n = _norm_matmul(mem, gains["g_mem"], weights["w_kv"], name="mem_kv", out_dtype=BF16, tb=mem.shape[0],
                             bn=1024, save_h=True)
    x2, h2, qm, om = _xattn_fwd(x1, gains["g_xattn"], weights["w_q"], kv, weights["w_o"], tb=512)
    w_up = weights["w_up"]
    (a, h3), got = _ride(_norm_matmul, x2, gains["g_mlp"], w_up, name="mlp_up", out_dtype=BF16, tb=1024, bn=2048,
                         relu=True, save_h=True, rider=weights.rider(["w_down"], late=True))
    weights.arrived(["w_down"], got)
    w_down = weights["w_down"]
    dx3, dx3b, loss_blk, gg_final = _mlp_down_loss(a, w_down, x2, tgt, gains["g_final"], tb=512)

    grads.add("w_down", _matmul_tn(a, dx3b, name="grad_w_down", bm=512, bn=1024, square_a=True))
    sent = grads.send("w_down")
    dpre = _mlp_dpre(dx3b, w_down, a, tb=1024, bn=2048, after=sent)
    grads.add("w_up", _matmul_tn(h3, dpre, name="grad_w_up", bm=1024, bn=1024))
    sent = grads.send("w_up")
    dx2, dx2b, gg_mlp = _matmul_nt_normbwd(dpre, w_up, x2, gains["g_mlp"], dx3, name="mlp_dx", tb=512,
                                           also_bf16=True, after=sent)

    grads.add("w_o", _matmul_tn(om, dx2b, name="grad_w_o", bm=512, bn=512))
    dx1, dqm, dk, dv, gg_xattn = _xattn_bwd(dx2, x1, gains["g_xattn"], qm, weights["w_q"], kv, weights["w_o"], tb=512)
    grads.add("w_q", _matmul_tn(h2, dqm, name="grad_w_q", bm=1024, bn=512))
    gw_kv, gg_mem = _mem_bwd(dk, dv, weights["w_kv"], mem, mem_n, gains["g_mem"])
    grads.add("w_kv", gw_kv)

    dattn, dsum, dy, gg_attn, gg_conv, gw_out = _mixer_bwd(dx1, merged, attn, gates, cw, gains["g_attn_out"],
                                                           gains["g_conv_out"], weights["w_out"], _head_sum_matrix())
    grads.add("w_out", gw_out)
    sent = grads.send("w_o", "w_q", "w_kv", "w_out")
    dproj, gcw = _conv_bwd(dy, gates, cw, after=sent)
    dproj = _attention_bwd(qkv, dattn, dsum, lses, dproj)
    grads.add("w_in", _matmul_tn(h1, dproj, name="grad_w_in", bm=1024, bn=512))
    sent = grads.send("w_in")
    grad_x, gg_mix = _matmul_nt_normbwd(dproj, w_in, x, gains["g_mix"], dx1, name="mixer_dx", tb=512,
                                        to_natural=True, after=sent)

    grads.add("small", _pack_small([gg_mix, gg_xattn, gg_mem, gg_mlp, gg_final], gg_attn, gg_conv, gcw, loss_blk))
    return grad_x


_BIG = ("w_in", "w_out", "w_q", "w_kv", "w_o", "w_up", "w_down")


def kernel(x, mem, g_mix, w_in, conv_w, g_attn_out, g_conv_out, w_out, g_xattn, g_mem, w_q_mem, w_kv_mem, w_o_mem, g_mlp, w_up, w_down, g_final, loss_target, m_g_mix, m_w_in, m_conv_w, m_g_attn_out, m_g_conv_out, m_w_out, m_g_xattn, m_g_mem, m_w_q_mem, m_w_kv_mem, m_w_o_mem, m_g_mlp, m_w_up, m_w_down, m_g_final, v_g_mix, v_w_in, v_conv_w, v_g_attn_out, v_g_conv_out, v_w_out, v_g_xattn, v_g_mem, v_w_q_mem, v_w_kv_mem, v_w_o_mem, v_g_mlp, v_w_up, v_w_down, v_g_final):
    d = x.shape[-1]
    me = 4 * lax.axis_index("x") + 2 * lax.axis_index("y") + lax.axis_index("c")
    w_shards = dict(w_in=w_in, w_out=w_out, w_q=w_q_mem, w_kv=w_kv_mem, w_o=w_o_mem, w_up=w_up, w_down=w_down)
    m_shards = dict(w_in=m_w_in, w_out=m_w_out, w_q=m_w_q_mem, w_kv=m_w_kv_mem, w_o=m_w_o_mem, w_up=m_w_up,
                    w_down=m_w_down)
    v_shards = dict(w_in=v_w_in, w_out=v_w_out, w_q=v_w_q_mem, w_kv=v_w_kv_mem, w_o=v_w_o_mem, w_up=v_w_up,
                    w_down=v_w_down)
    gains = dict(g_mix=g_mix, g_attn_out=g_attn_out, g_conv_out=g_conv_out, g_xattn=g_xattn, g_mem=g_mem,
                 g_mlp=g_mlp, g_final=g_final)
    gains2 = {k: v.reshape(1, -1) for k, v in gains.items()}

    shards = {k: w_shards[k].astype(BF16) for k in _BIG}
    shards["conv_w"] = conv_w
    grads = _Grads(distributed=True)
    grad_x = _local_step(x[0], mem[0], loss_target[0], gains2, _Weights({}, shards), grads)

    after = grads.send("small")
    outs = {}
    tiles = dict(w_in=512, w_out=64, w_q=64, w_kv=512, w_o=64, w_up=512, w_down=256)
    for group in (("w_down",), ("w_up",), ("w_o", "w_q", "w_kv", "w_out"), ("w_in",)):
        received = dict(zip(group, grads.wait(group[0], after)))
        same_shape = {}
        for k in group:
            same_shape.setdefault(w_shards[k].shape, []).append(k)
        for names in same_shape.values():
            res = _sum_adamw([(received[k], w_shards[k], m_shards[k], v_shards[k]) for k in names],
                             name="adamw_" + "_".join(names), tr=tiles[names[0]])
            outs.update(zip(names, res))
            after = [res[-1][0]]
    small_received, = grads.wait("small", after)

    m_small = dict(g_mix=m_g_mix, g_attn_out=m_g_attn_out, g_conv_out=m_g_conv_out, g_xattn=m_g_xattn,
                   g_mem=m_g_mem, g_mlp=m_g_mlp, g_final=m_g_final)
    v_small = dict(g_mix=v_g_mix, g_attn_out=v_g_attn_out, g_conv_out=v_g_conv_out, g_xattn=v_g_xattn,
                   g_mem=v_g_mem, g_mlp=v_g_mlp, g_final=v_g_final)
    as_rows = lambda vals, conv: dict({k: a.reshape(1, -1) for k, a in vals.items()}, conv_w=conv)
    loss, small_out = _update_small(small_received, me.reshape(1), as_rows(gains, conv_w),
                                    as_rows(m_small, m_conv_w), as_rows(v_small, v_conv_w))
    small_out = {k: [a.reshape(dict(gains, conv_w=conv_w)[k].shape) for a in res] for k, res in small_out.items()}
    names = {"g_mix": "g_mix", "w_in": "w_in", "conv_w": "conv_w", "g_attn_out": "g_attn_out",
             "g_conv_out": "g_conv_out", "w_out": "w_out", "g_xattn": "g_xattn", "g_mem": "g_mem",
             "w_q_mem": "w_q", "w_kv_mem": "w_kv", "w_o_mem": "w_o", "g_mlp": "g_mlp", "w_up": "w_up",
             "w_down": "w_down", "g_final": "g_final"}
    result = [loss.reshape(()), grad_x[None]]
    for which in range(4):
        for key in names.values():
            result.append(outs[key][which] if key in outs else small_out[key][which])
    return tuple(result)
```
